```python
import jax, jax.numpy as jnp
from jax import lax
import numpy as np

D_MODEL = 1024
BATCH = 8
SEQ = 2048
DEPTH = 2

CHUNK = 64
Q_BLOCK = 128
A_HEADS = 8
A_HEAD_DIM = D_MODEL // 16
A_WIDTH = A_HEADS * A_HEAD_DIM
POOL_WINDOWS = (2, 4, 8, 16)
POOL_GROUPS = len(POOL_WINDOWS)
POOL_WIDTH = D_MODEL // 4
POOL_GROUP_DIM = POOL_WIDTH // POOL_GROUPS
CONV_WIDTH = D_MODEL // 4
CONV_K = 3
N_BRANCH = 3
MIX_WIDTH = A_WIDTH + POOL_WIDTH + CONV_WIDTH
D_FF = 4 * D_MODEL
RMS_EPS = 1e-6
NEG_INF = -1e30
IN_COLS = 3 * A_WIDTH + A_HEADS + POOL_WIDTH + 3 * CONV_WIDTH + N_BRANCH * D_MODEL

kernel_name = 'hybrid_fox_pool_conv_block'


def rmsnorm(x, g):
    xf = x.astype(jnp.float32)
    y = xf * lax.rsqrt(jnp.mean(xf * xf, axis=-1, keepdims=True) + RMS_EPS)
    return (y * g.astype(jnp.float32)).astype(x.dtype)


def forgetting_attention(q, k, v, f_logit):
    b, s, h, dh = q.shape
    q = q.transpose(0, 2, 1, 3)
    k = k.transpose(0, 2, 1, 3)
    v = v.transpose(0, 2, 1, 3)
    log_f = jax.nn.log_sigmoid(f_logit.astype(jnp.float32))
    cum_f = jnp.cumsum(log_f, axis=1).transpose(0, 2, 1)
    scale = dh ** -0.5
    outs = []
    for i in range(s // Q_BLOCK):
        qs, qe = i * Q_BLOCK, (i + 1) * Q_BLOCK
        qb = q[:, :, qs:qe]
        kb = k[:, :, :qe]
        vb = v[:, :, :qe]
        logits = jnp.einsum('bhqd,bhkd->bhqk', qb, kb).astype(jnp.float32) * scale
        logits = logits + cum_f[:, :, qs:qe, None] - cum_f[:, :, None, :qe]
        causal = jnp.arange(qs, qe)[:, None] >= jnp.arange(qe)[None, :]
        logits = jnp.where(causal[None, None], logits, NEG_INF)
        p = jax.nn.softmax(logits, axis=-1).astype(v.dtype)
        outs.append(jnp.einsum('bhqk,bhkd->bhqd', p, vb))
    o = jnp.concatenate(outs, axis=2)
    return o.transpose(0, 2, 1, 3).reshape(b, s, h * dh)


def pool_mixer(u, w_pool, pool_scale):
    b, s, _ = u.shape
    uf = u.astype(jnp.float32)
    cs = jnp.cumsum(uf, axis=1)
    groups = []
    for g, w in enumerate(POOL_WINDOWS):
        sl = slice(g * POOL_GROUP_DIM, (g + 1) * POOL_GROUP_DIM)
        cs_g = cs[..., sl]
        lagged = jnp.pad(cs_g, ((0, 0), (w, 0), (0, 0)))[:, :s]
        count = jnp.minimum(jnp.arange(1, s + 1, dtype=jnp.float32), float(w))
        groups.append((cs_g - lagged) / count[None, :, None] - uf[..., sl])
    p = jnp.stack(groups, axis=2).astype(u.dtype)
    y = jnp.einsum('bsgc,gcd->bsgd', p, w_pool).reshape(b, s, POOL_WIDTH)
    return y * pool_scale


def short_conv(h, b_gate, c_gate, conv_w):
    u = c_gate * h
    y = lax.conv_general_dilated(
        u, conv_w[:, None, :].astype(u.dtype), window_strides=(1,), padding=[(CONV_K - 1, 0)],
        dimension_numbers=('NWC', 'WIO', 'NWC'), feature_group_count=CONV_WIDTH)
    return b_gate * y


def hybrid_layer(x, c, w_ada, b_ada, g_mix_pre, g_mix_post, g_ff_pre, g_ff_post, w_in, b_f,
                 w_pool, pool_scale, conv_w, w_branch, w_out, w_ff1, w_ff2):
    b, s, d = x.shape
    mod = jax.nn.silu(c) @ w_ada + b_ada
    shift_m, scale_m, gate_m, shift_f, scale_f, gate_f = jnp.split(mod, 6, axis=-1)

    h = rmsnorm(x, g_mix_pre) * (1.0 + scale_m[:, None]) + shift_m[:, None]
    z = h @ w_in
    sizes = [A_WIDTH, A_WIDTH, A_WIDTH, A_HEADS, POOL_WIDTH, CONV_WIDTH, CONV_WIDTH, CONV_WIDTH]
    cuts = [int(v) for v in np.cumsum(sizes)]
    q, k, v, fl, pu, ch, cb, cc, gl = jnp.split(z, cuts, axis=-1)
    heads = (b, s, A_HEADS, A_HEAD_DIM)
    br_a = forgetting_attention(q.reshape(heads), k.reshape(heads), v.reshape(heads), fl + b_f)
    br_b = pool_mixer(pu, w_pool, pool_scale)
    br_c = short_conv(ch, cb, cc, conv_w)
    gates = jax.nn.sigmoid(gl).reshape(b, s, N_BRANCH, d)
    wa = w_branch[:A_WIDTH]
    wb = w_branch[A_WIDTH:A_WIDTH + POOL_WIDTH]
    wc = w_branch[A_WIDTH + POOL_WIDTH:]
    merged = gates[:, :, 0] * (br_a @ wa) + gates[:, :, 1] * (br_b @ wb) + gates[:, :, 2] * (br_c @ wc)
    y = merged @ w_out
    x = x + gate_m[:, None] * rmsnorm(y, g_mix_post)

    h2 = rmsnorm(x, g_ff_pre) * (1.0 + scale_f[:, None]) + shift_f[:, None]
    y2 = jnp.square(jax.nn.relu(h2 @ w_ff1)) @ w_ff2
    return x + gate_f[:, None] * rmsnorm(y2, g_ff_post)


def _fwd_setup_inputs(seed: int = 0) -> dict:
    key = jax.random.key(seed)
    ks = jax.random.split(key, 20)

    def nrm(k, shape, std):
        return jax.random.normal(k, shape, jnp.float32) * std

    d = D_MODEL
    branch_row_scale = jnp.concatenate([
        jnp.full((A_WIDTH,), A_WIDTH ** -0.5, jnp.float32),
        jnp.full((POOL_WIDTH,), POOL_WIDTH ** -0.5, jnp.float32),
        jnp.full((CONV_WIDTH,), CONV_WIDTH ** -0.5, jnp.float32)])
    return {
        'x': nrm(ks[0], (BATCH, SEQ, d), 1.0),
        'c': nrm(ks[1], (BATCH, d), 1.0),
        'w_ada': nrm(ks[2], (DEPTH, d, 6 * d), 0.5 * d ** -0.5),
        'b_ada': nrm(ks[3], (DEPTH, 6 * d), 0.02),
        'g_mix_pre': 1.0 + nrm(ks[4], (DEPTH, d), 0.05),
        'g_mix_post': 1.0 + nrm(ks[5], (DEPTH, d), 0.05),
        'g_ff_pre': 1.0 + nrm(ks[6], (DEPTH, d), 0.05),
        'g_ff_post': 1.0 + nrm(ks[7], (DEPTH, d), 0.05),
        'w_in': nrm(ks[8], (DEPTH, d, IN_COLS), d ** -0.5),
        'b_f': 3.0 + nrm(ks[9], (DEPTH, A_HEADS), 0.1),
        'w_pool': nrm(ks[10], (DEPTH, POOL_GROUPS, POOL_GROUP_DIM, POOL_GROUP_DIM), POOL_GROUP_DIM ** -0.5),
        'pool_scale': 1.0 + nrm(ks[11], (DEPTH, POOL_WIDTH), 0.1),
        'conv_w': nrm(ks[12], (DEPTH, CONV_K, CONV_WIDTH), CONV_K ** -0.5),
        'w_branch': nrm(ks[13], (DEPTH, MIX_WIDTH, d), 1.0) * branch_row_scale[None, :, None],
        'w_out': nrm(ks[14], (DEPTH, d, d), d ** -0.5),
        'w_ff1': nrm(ks[15], (DEPTH, d, D_FF), d ** -0.5),
        'w_ff2': nrm(ks[16], (DEPTH, D_FF, d), D_FF ** -0.5),
    }


def _fwd_reference(x, c, w_ada, b_ada, g_mix_pre, g_mix_post, g_ff_pre, g_ff_post, w_in, b_f,
              w_pool, pool_scale, conv_w, w_branch, w_out, w_ff1, w_ff2):
    for l in range(DEPTH):
        x = hybrid_layer(x, c, w_ada[l], b_ada[l], g_mix_pre[l], g_mix_post[l], g_ff_pre[l], g_ff_post[l],
                         w_in[l], b_f[l], w_pool[l], pool_scale[l], conv_w[l], w_branch[l], w_out[l],
                         w_ff1[l], w_ff2[l])
    return x


import jax as _jax
import jax.numpy as _jnp

TWIN_FORMAT = 'train_step'
FWD_PARAMS = ['x', 'c', 'w_ada', 'b_ada', 'g_mix_pre', 'g_mix_post', 'g_ff_pre', 'g_ff_post', 'w_in', 'b_f', 'w_pool', 'pool_scale', 'conv_w', 'w_branch', 'w_out', 'w_ff1', 'w_ff2']
TWIN_WEIGHTS = ['w_ada', 'b_ada', 'g_mix_pre', 'g_mix_post', 'g_ff_pre', 'g_ff_post', 'w_in', 'b_f', 'w_pool', 'pool_scale', 'conv_w', 'w_branch', 'w_out', 'w_ff1', 'w_ff2']
TWIN_DIFF_INPUT = 'x'
TWIN_INPUTS = ['x', 'c', 'w_ada', 'b_ada', 'g_mix_pre', 'g_mix_post', 'g_ff_pre', 'g_ff_post', 'w_in', 'b_f', 'w_pool', 'pool_scale', 'conv_w', 'w_branch', 'w_out', 'w_ff1', 'w_ff2', 'loss_target', 'm_w_ada', 'm_b_ada', 'm_g_mix_pre', 'm_g_mix_post', 'm_g_ff_pre', 'm_g_ff_post', 'm_w_in', 'm_b_f', 'm_w_pool', 'm_pool_scale', 'm_conv_w', 'm_w_branch', 'm_w_out', 'm_w_ff1', 'm_w_ff2', 'v_w_ada', 'v_b_ada', 'v_g_mix_pre', 'v_g_mix_post', 'v_g_ff_pre', 'v_g_ff_post', 'v_w_in', 'v_b_f', 'v_w_pool', 'v_pool_scale', 'v_conv_w', 'v_w_branch', 'v_w_out', 'v_w_ff1', 'v_w_ff2']
TWIN_OUTPUTS = ['loss', 'grad_x', 'grad_w_ada', 'grad_b_ada', 'grad_g_mix_pre', 'grad_g_mix_post', 'grad_g_ff_pre', 'grad_g_ff_post', 'grad_w_in', 'grad_b_f', 'grad_w_pool', 'grad_pool_scale', 'grad_conv_w', 'grad_w_branch', 'grad_w_out', 'grad_w_ff1', 'grad_w_ff2', 'delta_w_ada', 'delta_b_ada', 'delta_g_mix_pre', 'delta_g_mix_post', 'delta_g_ff_pre', 'delta_g_ff_post', 'delta_w_in', 'delta_b_f', 'delta_w_pool', 'delta_pool_scale', 'delta_conv_w', 'delta_w_branch', 'delta_w_out', 'delta_w_ff1', 'delta_w_ff2', 'new_m_w_ada', 'new_m_b_ada', 'new_m_g_mix_pre', 'new_m_g_mix_post', 'new_m_g_ff_pre', 'new_m_g_ff_post', 'new_m_w_in', 'new_m_b_f', 'new_m_w_pool', 'new_m_pool_scale', 'new_m_conv_w', 'new_m_w_branch', 'new_m_w_out', 'new_m_w_ff1', 'new_m_w_ff2', 'new_v_w_ada', 'new_v_b_ada', 'new_v_g_mix_pre', 'new_v_g_mix_post', 'new_v_g_ff_pre', 'new_v_g_ff_post', 'new_v_w_in', 'new_v_b_f', 'new_v_w_pool', 'new_v_pool_scale', 'new_v_conv_w', 'new_v_w_branch', 'new_v_w_out', 'new_v_w_ff1', 'new_v_w_ff2']
TWIN_LEAF_KINDS = {'loss': 'loss', 'grad_x': 'grad_x', 'grad_w_ada': 'grad_w', 'grad_b_ada': 'grad_w', 'grad_g_mix_pre': 'grad_w', 'grad_g_mix_post': 'grad_w', 'grad_g_ff_pre': 'grad_w', 'grad_g_ff_post': 'grad_w', 'grad_w_in': 'grad_w', 'grad_b_f': 'grad_w', 'grad_w_pool': 'grad_w', 'grad_pool_scale': 'grad_w', 'grad_conv_w': 'grad_w', 'grad_w_branch': 'grad_w', 'grad_w_out': 'grad_w', 'grad_w_ff1': 'grad_w', 'grad_w_ff2': 'grad_w', 'delta_w_ada': 'delta_w', 'delta_b_ada': 'delta_w', 'delta_g_mix_pre': 'delta_w', 'delta_g_mix_post': 'delta_w', 'delta_g_ff_pre': 'delta_w', 'delta_g_ff_post': 'delta_w', 'delta_w_in': 'delta_w', 'delta_b_f': 'delta_w', 'delta_w_pool': 'delta_w', 'delta_pool_scale': 'delta_w', 'delta_conv_w': 'delta_w', 'delta_w_branch': 'delta_w', 'delta_w_out': 'delta_w', 'delta_w_ff1': 'delta_w', 'delta_w_ff2': 'delta_w', 'new_m_w_ada': 'new_m', 'new_m_b_ada': 'new_m', 'new_m_g_mix_pre': 'new_m', 'new_m_g_mix_post': 'new_m', 'new_m_g_ff_pre': 'new_m', 'new_m_g_ff_post': 'new_m', 'new_m_w_in': 'new_m', 'new_m_b_f': 'new_m', 'new_m_w_pool': 'new_m', 'new_m_pool_scale': 'new_m', 'new_m_conv_w': 'new_m', 'new_m_w_branch': 'new_m', 'new_m_w_out': 'new_m', 'new_m_w_ff1': 'new_m', 'new_m_w_ff2': 'new_m', 'new_v_w_ada': 'new_v', 'new_v_b_ada': 'new_v', 'new_v_g_mix_pre': 'new_v', 'new_v_g_mix_post': 'new_v', 'new_v_g_ff_pre': 'new_v', 'new_v_g_ff_post': 'new_v', 'new_v_w_in': 'new_v', 'new_v_b_f': 'new_v', 'new_v_w_pool': 'new_v', 'new_v_pool_scale': 'new_v', 'new_v_conv_w': 'new_v', 'new_v_w_branch': 'new_v', 'new_v_w_out': 'new_v', 'new_v_w_ff1': 'new_v', 'new_v_w_ff2': 'new_v'}


def _forward(args):
    return _fwd_reference(*[args[k] for k in FWD_PARAMS])


def _output_shape():
    out = _jax.eval_shape(lambda: _forward(_fwd_setup_inputs(0)))
    return out.shape, out.dtype

N_MICROBATCH = 1
ADAM_LR = 0.001
ADAM_B1 = 0.9
ADAM_B2 = 0.999
ADAM_EPS = 1e-08
ADAM_WD = 0.01
ADAM_STEP = 10
PER_EXAMPLE_BATCH_AXIS = {'x': 0, 'c': 0, 'loss_target': 0}
SHARED_INPUTS = []
_WEIGHT_DTYPES = {'w_ada': _jnp.float32, 'b_ada': _jnp.float32, 'g_mix_pre': _jnp.float32, 'g_mix_post': _jnp.float32, 'g_ff_pre': _jnp.float32, 'g_ff_post': _jnp.float32, 'w_in': _jnp.float32, 'b_f': _jnp.float32, 'w_pool': _jnp.float32, 'pool_scale': _jnp.float32, 'conv_w': _jnp.float32, 'w_branch': _jnp.float32, 'w_out': _jnp.float32, 'w_ff1': _jnp.float32, 'w_ff2': _jnp.float32}
MOMENT_SCALE = {'w_ada': 9.199897e-01, 'b_ada': 1.708763e+00, 'g_mix_pre': 1.061313e-01, 'g_mix_post': 1.906031e+00, 'g_ff_pre': 9.341143e-02, 'g_ff_post': 1.890727e+00, 'w_in': 5.719582e-02, 'b_f': 8.847063e-02, 'w_pool': 1.077194e-01, 'pool_scale': 1.253718e-01, 'conv_w': 1.141726e-01, 'w_branch': 6.850270e-02, 'w_out': 1.152995e-01, 'w_ff1': 6.654309e-02, 'w_ff2': 2.512386e-01}


def _to_microbatches(a, axis):
    t = _jnp.moveaxis(a, axis, 0)
    t = t.reshape((N_MICROBATCH, t.shape[0] // N_MICROBATCH) + t.shape[1:])
    return _jnp.moveaxis(t, 1, axis + 1)


def setup_inputs(seed: int = 0) -> dict:
    inp = _fwd_setup_inputs(seed)
    key = _jax.random.fold_in(_jax.random.key(seed), 7919)
    shape, _ = _output_shape()
    out = dict(inp)
    out["loss_target"] = _jax.random.normal(_jax.random.fold_in(key, 0), shape, _jnp.float32)
    for i, name in enumerate(TWIN_WEIGHTS):
        w = inp[name].astype(_jnp.float32)
        if MOMENT_SCALE is None:
            s = _jnp.sqrt(_jnp.mean(_jnp.square(w)) + 1e-30)
        else:
            s = MOMENT_SCALE[name]
        km, kv = _jax.random.split(_jax.random.fold_in(key, i + 1))
        out[name] = w
        out["m_" + name] = s * _jax.random.normal(km, w.shape, _jnp.float32)
        out["v_" + name] = (s * s) * _jax.random.uniform(kv, w.shape, _jnp.float32, 0.5, 1.5)
    if N_MICROBATCH > 1:
        for name, axis in PER_EXAMPLE_BATCH_AXIS.items():
            out[name] = _to_microbatches(out[name], axis)
    return {'x': out['x'], 'c': out['c'], 'w_ada': out['w_ada'], 'b_ada': out['b_ada'], 'g_mix_pre': out['g_mix_pre'], 'g_mix_post': out['g_mix_post'], 'g_ff_pre': out['g_ff_pre'], 'g_ff_post': out['g_ff_post'], 'w_in': out['w_in'], 'b_f': out['b_f'], 'w_pool': out['w_pool'], 'pool_scale': out['pool_scale'], 'conv_w': out['conv_w'], 'w_branch': out['w_branch'], 'w_out': out['w_out'], 'w_ff1': out['w_ff1'], 'w_ff2': out['w_ff2'], 'loss_target': out['loss_target'], 'm_w_ada': out['m_w_ada'], 'm_b_ada': out['m_b_ada'], 'm_g_mix_pre': out['m_g_mix_pre'], 'm_g_mix_post': out['m_g_mix_post'], 'm_g_ff_pre': out['m_g_ff_pre'], 'm_g_ff_post': out['m_g_ff_post'], 'm_w_in': out['m_w_in'], 'm_b_f': out['m_b_f'], 'm_w_pool': out['m_w_pool'], 'm_pool_scale': out['m_pool_scale'], 'm_conv_w': out['m_conv_w'], 'm_w_branch': out['m_w_branch'], 'm_w_out': out['m_w_out'], 'm_w_ff1': out['m_w_ff1'], 'm_w_ff2': out['m_w_ff2'], 'v_w_ada': out['v_w_ada'], 'v_b_ada': out['v_b_ada'], 'v_g_mix_pre': out['v_g_mix_pre'], 'v_g_mix_post': out['v_g_mix_post'], 'v_g_ff_pre': out['v_g_ff_pre'], 'v_g_ff_post': out['v_g_ff_post'], 'v_w_in': out['v_w_in'], 'v_b_f': out['v_b_f'], 'v_w_pool': out['v_w_pool'], 'v_pool_scale': out['v_pool_scale'], 'v_conv_w': out['v_conv_w'], 'v_w_branch': out['v_w_branch'], 'v_w_out': out['v_w_out'], 'v_w_ff1': out['v_w_ff1'], 'v_w_ff2': out['v_w_ff2']}


def _loss(weights, diff, rest, loss_target):
    with _jax.named_scope("forward"):
        args = {**rest, TWIN_DIFF_INPUT: diff, **{k: w.astype(_WEIGHT_DTYPES[k]) for k, w in weights.items()}}
        y = _forward(args)
    with _jax.named_scope("loss_head"):
        err = _jnp.square(y.astype(_jnp.float32) - loss_target)
        return 0.5 * _jnp.sum(_jnp.mean(err, axis=-1)) if err.ndim else 0.5 * err


def _adamw(w, g, m, v):
    m = ADAM_B1 * m + (1.0 - ADAM_B1) * g
    v = ADAM_B2 * v + (1.0 - ADAM_B2) * _jnp.square(g)
    m_hat = m / (1.0 - ADAM_B1 ** ADAM_STEP)
    v_hat = v / (1.0 - ADAM_B2 ** ADAM_STEP)
    delta = -ADAM_LR * (m_hat / (_jnp.sqrt(v_hat) + ADAM_EPS) + ADAM_WD * w)
    return delta, m, v


def reference(x, c, w_ada, b_ada, g_mix_pre, g_mix_post, g_ff_pre, g_ff_post, w_in, b_f, w_pool, pool_scale, conv_w, w_branch, w_out, w_ff1, w_ff2, loss_target, m_w_ada, m_b_ada, m_g_mix_pre, m_g_mix_post, m_g_ff_pre, m_g_ff_post, m_w_in, m_b_f, m_w_pool, m_pool_scale, m_conv_w, m_w_branch, m_w_out, m_w_ff1, m_w_ff2, v_w_ada, v_b_ada, v_g_mix_pre, v_g_mix_post, v_g_ff_pre, v_g_ff_post, v_w_in, v_b_f, v_w_pool, v_pool_scale, v_conv_w, v_w_branch, v_w_out, v_w_ff1, v_w_ff2):
    given = dict(x=x, c=c, w_ada=w_ada, b_ada=b_ada, g_mix_pre=g_mix_pre, g_mix_post=g_mix_post, g_ff_pre=g_ff_pre, g_ff_post=g_ff_post, w_in=w_in, b_f=b_f, w_pool=w_pool, pool_scale=pool_scale, conv_w=conv_w, w_branch=w_branch, w_out=w_out, w_ff1=w_ff1, w_ff2=w_ff2, loss_target=loss_target, m_w_ada=m_w_ada, m_b_ada=m_b_ada, m_g_mix_pre=m_g_mix_pre, m_g_mix_post=m_g_mix_post, m_g_ff_pre=m_g_ff_pre, m_g_ff_post=m_g_ff_post, m_w_in=m_w_in, m_b_f=m_b_f, m_w_pool=m_w_pool, m_pool_scale=m_pool_scale, m_conv_w=m_conv_w, m_w_branch=m_w_branch, m_w_out=m_w_out, m_w_ff1=m_w_ff1, m_w_ff2=m_w_ff2, v_w_ada=v_w_ada, v_b_ada=v_b_ada, v_g_mix_pre=v_g_mix_pre, v_g_mix_post=v_g_mix_post, v_g_ff_pre=v_g_ff_pre, v_g_ff_post=v_g_ff_post, v_w_in=v_w_in, v_b_f=v_b_f, v_w_pool=v_w_pool, v_pool_scale=v_pool_scale, v_conv_w=v_conv_w, v_w_branch=v_w_branch, v_w_out=v_w_out, v_w_ff1=v_w_ff1, v_w_ff2=v_w_ff2)
    weights = {n: given[n] for n in TWIN_WEIGHTS}
    shared = {n: given[n] for n in SHARED_INPUTS}
    per_example = {n: given[n] for n in ['x', 'c']}
    grad_fn = _jax.value_and_grad(_loss, argnums=(0, 1))

    def one_microbatch(ex, loss_target):
        ex = dict(ex)
        diff = ex.pop(TWIN_DIFF_INPUT)
        return grad_fn(weights, diff, {**shared, **ex}, loss_target)

    if N_MICROBATCH == 1:
        loss, (grad_w, grad_x) = one_microbatch(per_example, given["loss_target"])
    else:
        def body(carry, xs):
            loss_sum, grad_sum = carry
            l_k, (gw_k, gx_k) = one_microbatch(xs[0], xs[1])
            with _jax.named_scope("update"):
                return (loss_sum + l_k, _jax.tree.map(_jnp.add, grad_sum, gw_k)), gx_k

        init = (_jnp.zeros((), _jnp.float32), _jax.tree.map(_jnp.zeros_like, weights))
        (loss, grad_w), grad_x = _jax.lax.scan(body, init, (per_example, given["loss_target"]))
    with _jax.named_scope("update"):
        delta_w, new_m, new_v = {}, {}, {}
        for n in TWIN_WEIGHTS:
            delta_w[n], new_m[n], new_v[n] = _adamw(weights[n], grad_w[n], given["m_" + n], given["v_" + n])
    return (loss, grad_x, *[grad_w[n] for n in TWIN_WEIGHTS], *[delta_w[n] for n in TWIN_WEIGHTS],
            *[new_m[n] for n in TWIN_WEIGHTS], *[new_v[n] for n in TWIN_WEIGHTS])
```

```python
import functools

import jax
import jax.numpy as jnp
from jax import lax
from jax.experimental import pallas as pl
from jax.experimental.pallas import tpu as pltpu

F32 = jnp.float32
BF16 = jnp.bfloat16
MESH = pl.DeviceIdType.MESH

D = 1024
DEPTH = 2
HEADS = 8
HEAD_DIM = 64
A_WIDTH = 512
POOL_WIDTH = 256
CONV_WIDTH = 256
D_FF = 4096
IN_COLS = 5640
Z_GL, Z_QKV, Z_PC, Z_FL, Z_COLS = 0, 3072, 4608, 5632, 5760
RMS_EPS = 1e-6
NEG_INF = -1e30
ATT_BLOCK = 128
ROW_TILE = 256
N_CHIPS = 4
N_DEV = 8
V7X_VMEM_LIMIT = 48 * 1024 * 1024

ADAM_LR = 0.001
ADAM_B1 = 0.9
ADAM_B2 = 0.999
ADAM_EPS = 1e-08
ADAM_WD = 0.01
ADAM_STEP = 10

_HBM = pl.BlockSpec(memory_space=pltpu.HBM)


def _params(*sem):
    return pltpu.CompilerParams(dimension_semantics=sem, vmem_limit_bytes=V7X_VMEM_LIMIT)


def _pick(dim, cands):
    for cand in cands:
        if dim % cand == 0:
            return cand
    return dim


def _mm(a, b, *, ta=False, tb=False, out_dtype=F32, name):
    (k, m) = a.shape if ta else a.shape[::-1]
    (n, k2) = b.shape if tb else b.shape[::-1]
    assert k == k2, (a.shape, b.shape, ta, tb)
    tm = _pick(m, (512, 256, 128))
    tn = _pick(n, (1024, 640, 512, 256, 128))
    tk = _pick(k, (512, 640, 256, 128))
    nk = k // tk
    dims = (((0 if ta else 1,), (1 if tb else 0,)), ((), ()))

    def body(a_ref, b_ref, o_ref, acc_ref):
        kk = pl.program_id(2)

        @pl.when(kk == 0)
        def _():
            acc_ref[...] = jnp.zeros_like(acc_ref)

        acc_ref[...] += lax.dot_general(a_ref[...].astype(BF16), b_ref[...].astype(BF16), dims,
                                        preferred_element_type=F32)

        @pl.when(kk == nk - 1)
        def _():
            o_ref[...] = acc_ref[...].astype(o_ref.dtype)

    a_spec = pl.BlockSpec((tk, tm), lambda i, j, kk: (kk, i)) if ta else pl.BlockSpec((tm, tk), lambda i, j, kk: (i, kk))
    b_spec = pl.BlockSpec((tn, tk), lambda i, j, kk: (j, kk)) if tb else pl.BlockSpec((tk, tn), lambda i, j, kk: (kk, j))
    return pl.pallas_call(
        body, name=name, grid=(m // tm, n // tn, nk),
        in_specs=[a_spec, b_spec], out_specs=pl.BlockSpec((tm, tn), lambda i, j, kk: (i, j)),
        out_shape=jax.ShapeDtypeStruct((m, n), out_dtype),
        scratch_shapes=[pltpu.VMEM((tm, tn), F32)],
        compiler_params=_params("parallel", "parallel", "arbitrary"),
    )(a, b)


def _ew(fn, ins, out_dtypes, name, tc=None):
    rows, cols = ins[0].shape
    tr = _pick(rows, (ROW_TILE, 128, 8))
    tc = cols if tc is None else tc
    n_in = len(ins)

    def body(*refs):
        res = fn(*[r[...] for r in refs[:n_in]])
        for o_ref, val in zip(refs[n_in:], res):
            o_ref[...] = val.astype(o_ref.dtype)

    spec = pl.BlockSpec((tr, tc), lambda i, j: (i, j))
    return pl.pallas_call(
        body, name=name, grid=(rows // tr, cols // tc),
        in_specs=[spec] * n_in, out_specs=[spec] * len(out_dtypes),
        out_shape=[jax.ShapeDtypeStruct((rows, cols), dt) for dt in out_dtypes],
        compiler_params=_params("parallel", "parallel"),
    )(*ins)


def _act_fwd(a):
    def fn(a):
        r = jnp.maximum(a, 0.0)
        return (r * r,)
    return _ew(fn, [a], [BF16], "act_fwd", tc=2048)[0]


def _act_bwd(dr, a):
    def fn(dr, a):
        return (dr * (2.0 * jnp.maximum(a, 0.0)),)
    return _ew(fn, [dr, a], [BF16], "act_bwd", tc=2048)[0]


def _adamw(w, g, m, v, name):
    bc1 = 1.0 - ADAM_B1 ** ADAM_STEP
    bc2 = 1.0 - ADAM_B2 ** ADAM_STEP

    def fn(w, g, m, v):
        m = ADAM_B1 * m + (1.0 - ADAM_B1) * g
        v = ADAM_B2 * v + (1.0 - ADAM_B2) * (g * g)
        m_hat = m / bc1
        v_hat = v / bc2
        delta = -ADAM_LR * (m_hat / (jnp.sqrt(v_hat) + ADAM_EPS) + ADAM_WD * w)
        return delta, m, v
    return _ew(fn, [w, g, m, v], [F32, F32, F32], name)


def _row_spec(cols, block=0):
    return pl.BlockSpec((ROW_TILE, cols), lambda i, block=block: (i, block))


def _vec_spec(cols):
    return pl.BlockSpec((1, cols), lambda i: (0, 0))


def _sum_spec(cols):
    return pl.BlockSpec((8, cols), lambda i: (0, 0))


def _rstd(x):
    return lax.rsqrt(jnp.mean(x * x, axis=-1, keepdims=True) + RMS_EPS)


def _modnorm_fwd(x, g, shift, scale, name):
    s = x.shape[0]

    def body(x_ref, g_ref, sh_ref, sc_ref, h_ref):
        xv = x_ref[...]
        n = xv * _rstd(xv)
        h_ref[...] = ((n * g_ref[...]) * (1.0 + sc_ref[...]) + sh_ref[...]).astype(BF16)

    return pl.pallas_call(
        body, name=name, grid=(s // ROW_TILE,),
        in_specs=[_row_spec(D), _vec_spec(D), _vec_spec(D), _vec_spec(D)], out_specs=_row_spec(D),
        out_shape=jax.ShapeDtypeStruct((s, D), BF16), compiler_params=_params("parallel"),
    )(x, g, shift, scale)


def _post_fwd(x, y, g, gate, name):
    s = x.shape[0]

    def body(x_ref, y_ref, g_ref, gate_ref, o_ref):
        yv = y_ref[...]
        o_ref[...] = x_ref[...] + gate_ref[...] * ((yv * _rstd(yv)) * g_ref[...])

    return pl.pallas_call(
        body, name=name, grid=(s // ROW_TILE,),
        in_specs=[_row_spec(D), _row_spec(D), _vec_spec(D), _vec_spec(D)], out_specs=_row_spec(D),
        out_shape=jax.ShapeDtypeStruct((s, D), F32), compiler_params=_params("parallel"),
    )(x, y, g, gate)


def _post_bwd(dxo, y, g, gate, name):
    s = dxo.shape[0]

    def body(d_ref, y_ref, g_ref, gate_ref, dy_ref, sum_ref):
        @pl.when(pl.program_id(0) == 0)
        def _():
            sum_ref[...] = jnp.zeros_like(sum_ref)

        dv, yv = d_ref[...], y_ref[...]
        r = _rstd(yv)
        n = yv * r
        sum_ref[0:1, :] += jnp.sum(dv * (n * g_ref[...]), axis=0, keepdims=True)
        sum_ref[1:2, :] += jnp.sum((dv * gate_ref[...]) * n, axis=0, keepdims=True)
        dn = (dv * gate_ref[...]) * g_ref[...]
        dy_ref[...] = (r * (dn - n * jnp.mean(dn * n, axis=-1, keepdims=True))).astype(BF16)

    return pl.pallas_call(
        body, name=name, grid=(s // ROW_TILE,),
        in_specs=[_row_spec(D), _row_spec(D), _vec_spec(D), _vec_spec(D)],
        out_specs=[_row_spec(D), _sum_spec(D)],
        out_shape=[jax.ShapeDtypeStruct((s, D), BF16), jax.ShapeDtypeStruct((8, D), F32)],
        compiler_params=_params("arbitrary"),
    )(dxo, y, g, gate)


def _modnorm_bwd(dh, x, dxo, g, scale, name):
    s = dh.shape[0]

    def body(dh_ref, x_ref, d_ref, g_ref, sc_ref, dx_ref, sum_ref):
        @pl.when(pl.program_id(0) == 0)
        def _():
            sum_ref[...] = jnp.zeros_like(sum_ref)

        dhv, xv = dh_ref[...], x_ref[...]
        r = _rstd(xv)
        n = xv * r
        one_sc = 1.0 + sc_ref[...]
        sum_ref[0:1, :] += jnp.sum(dhv, axis=0, keepdims=True)
        sum_ref[1:2, :] += jnp.sum(dhv * (n * g_ref[...]), axis=0, keepdims=True)
        sum_ref[2:3, :] += jnp.sum((dhv * one_sc) * n, axis=0, keepdims=True)
        dn = (dhv * one_sc) * g_ref[...]
        dx_ref[...] = d_ref[...] + r * (dn - n * jnp.mean(dn * n, axis=-1, keepdims=True))

    return pl.pallas_call(
        body, name=name, grid=(s // ROW_TILE,),
        in_specs=[_row_spec(D), _row_spec(D), _row_spec(D), _vec_spec(D), _vec_spec(D)],
        out_specs=[_row_spec(D), _sum_spec(D)],
        out_shape=[jax.ShapeDtypeStruct((s, D), F32), jax.ShapeDtypeStruct((8, D), F32)],
        compiler_params=_params("arbitrary"),
    )(dh, x, dxo, g, scale)


def _loss_head(y, target):
    s = y.shape[0]

    def body(y_ref, t_ref, dy_ref, sum_ref):
        @pl.when(pl.program_id(0) == 0)
        def _():
            sum_ref[...] = jnp.zeros_like(sum_ref)

        err = y_ref[...] - t_ref[...]
        dy_ref[...] = err * (1.0 / D)
        sum_ref[...] += jnp.sum(err * err)

    return pl.pallas_call(
        body, name="loss_head", grid=(s // ROW_TILE,),
        in_specs=[_row_spec(D), _row_spec(D)],
        out_specs=[_row_spec(D), pl.BlockSpec((8, 128), lambda i: (0, 0))],
        out_shape=[jax.ShapeDtypeStruct((s, D), F32), jax.ShapeDtypeStruct((8, 128), F32)],
        compiler_params=_params("arbitrary"),
    )(y, target)


def _merge_fwd(z, pa, pb, pc):
    s = z.shape[0]

    def body(g0_ref, g1_ref, g2_ref, pa_ref, pb_ref, pc_ref, o_ref):
        o_ref[...] = (jax.nn.sigmoid(g0_ref[...]) * pa_ref[...] + jax.nn.sigmoid(g1_ref[...]) * pb_ref[...]
                      + jax.nn.sigmoid(g2_ref[...]) * pc_ref[...]).astype(BF16)

    return pl.pallas_call(
        body, name="merge_fwd", grid=(s // ROW_TILE,),
        in_specs=[_row_spec(D, 0), _row_spec(D, 1), _row_spec(D, 2), _row_spec(D), _row_spec(D), _row_spec(D)],
        out_specs=_row_spec(D), out_shape=jax.ShapeDtypeStruct((s, D), BF16),
        compiler_params=_params("parallel"),
    )(z, z, z, pa, pb, pc)


def _merge_bwd(dm, z, pa, pb, pc):
    s = z.shape[0]

    def body(dm_ref, g0_ref, g1_ref, g2_ref, pa_ref, pb_ref, pc_ref, dgl_ref, da_ref, db_ref, dc_ref):
        dmv = dm_ref[...]
        for i, (g_ref, p_ref, d_ref) in enumerate(((g0_ref, pa_ref, da_ref), (g1_ref, pb_ref, db_ref), (g2_ref, pc_ref, dc_ref))):
            gate = jax.nn.sigmoid(g_ref[...])
            dgl_ref[:, i * D:(i + 1) * D] = ((dmv * p_ref[...]) * (gate * (1.0 - gate))).astype(BF16)
            d_ref[...] = (dmv * gate).astype(BF16)

    return pl.pallas_call(
        body, name="merge_bwd", grid=(s // ROW_TILE,),
        in_specs=[_row_spec(D), _row_spec(D, 0), _row_spec(D, 1), _row_spec(D, 2), _row_spec(D), _row_spec(D), _row_spec(D)],
        out_specs=[_row_spec(3 * D), _row_spec(D), _row_spec(D), _row_spec(D)],
        out_shape=[jax.ShapeDtypeStruct((s, 3 * D), BF16)] + [jax.ShapeDtypeStruct((s, D), BF16)] * 3,
        compiler_params=_params("parallel"),
    )(dm, z, z, z, pa, pb, pc)


def _shift_down(v, n):
    row = lax.broadcasted_iota(jnp.int32, v.shape, 0)
    return jnp.where(row >= n, pltpu.roll(v, n, axis=0), 0.0)


def _shift_up(v, n):
    s = v.shape[0]
    row = lax.broadcasted_iota(jnp.int32, v.shape, 0)
    return jnp.where(row < s - n, pltpu.roll(v, s - n, axis=0), 0.0)


def _log_sigmoid(v):
    return jnp.minimum(v, 0.0) - jnp.log1p(jnp.exp(-jnp.abs(v)))


def _cumf_fwd(fl, bias):
    s = fl.shape[0]

    def body(fl_ref, b_ref, o_ref):
        acc = _log_sigmoid(fl_ref[...] + b_ref[...])
        step = 1
        while step < s:
            acc = acc + _shift_down(acc, step)
            step *= 2
        o_ref[...] = acc

    return pl.pallas_call(body, name="cumf_fwd", out_shape=jax.ShapeDtypeStruct((s, 128), F32),
                          compiler_params=pltpu.CompilerParams(vmem_limit_bytes=V7X_VMEM_LIMIT))(fl, bias)


def _cumf_bwd(dcum, fl, bias):
    s = fl.shape[0]

    def body(d_ref, fl_ref, b_ref, dfl_ref, db_ref):
        acc = d_ref[...]
        step = 1
        while step < s:
            acc = acc + _shift_up(acc, step)
            step *= 2
        dfl = acc * jax.nn.sigmoid(-(fl_ref[...] + b_ref[...]))
        dfl_ref[...] = dfl.astype(BF16)
        db_ref[...] = jnp.broadcast_to(jnp.sum(dfl, axis=0, keepdims=True), (8, 128))

    return pl.pallas_call(
        body, name="cumf_bwd",
        out_shape=[jax.ShapeDtypeStruct((s, 128), BF16), jax.ShapeDtypeStruct((8, 128), F32)],
        compiler_params=pltpu.CompilerParams(vmem_limit_bytes=V7X_VMEM_LIMIT))(dcum, fl, bias)


def _pool_windows(v, shift):
    s2 = v + shift(v, 1)
    s4 = s2 + shift(s2, 2)
    s8 = s4 + shift(s4, 4)
    s16 = s8 + shift(s8, 8)
    group = lax.broadcasted_iota(jnp.int32, v.shape, 1) // 64
    return jnp.where(group == 0, s2, jnp.where(group == 1, s4, jnp.where(group == 2, s8, s16)))


def _pool_count(shape):
    group = lax.broadcasted_iota(jnp.int32, shape, 1) // 64
    window = jnp.where(group == 0, 2.0, jnp.where(group == 1, 4.0, jnp.where(group == 2, 8.0, 16.0)))
    t1 = (lax.broadcasted_iota(jnp.int32, shape, 0) + 1).astype(F32)
    return jnp.minimum(t1, window)


def _pc_specs(s):
    zcol = lambda blk: pl.BlockSpec((s, 256), lambda i, blk=blk: (0, blk))
    first = Z_PC // 256
    return [zcol(first), zcol(first + 1), zcol(first + 2), zcol(first + 3),
            pl.BlockSpec((256, 256), lambda i: (0, 0)), pl.BlockSpec((1, 256), lambda i: (0, 0)),
            pl.BlockSpec((3, 256), lambda i: (0, 0))]


def _poolconv_fwd(z, wbd, pscale, convw):
    s = z.shape[0]

    def body(pu_ref, ch_ref, cb_ref, cc_ref, w_ref, ps_ref, cw_ref, yb_ref, yc_ref):
        u = pu_ref[...]
        p = _pool_windows(u, _shift_down) / _pool_count(u.shape) - u
        yb = jnp.dot(p.astype(BF16), w_ref[...].astype(BF16), preferred_element_type=F32) * ps_ref[...]
        yb_ref[...] = yb.astype(BF16)
        uc = cc_ref[...] * ch_ref[...]
        cw = cw_ref[...]
        conv = cw[0:1, :] * _shift_down(uc, 2) + cw[1:2, :] * _shift_down(uc, 1) + cw[2:3, :] * uc
        yc_ref[...] = (cb_ref[...] * conv).astype(BF16)

    out = pl.BlockSpec((s, 256), lambda i: (0, 0))
    return pl.pallas_call(
        body, name="poolconv_fwd", grid=(1,), in_specs=_pc_specs(s), out_specs=[out, out],
        out_shape=[jax.ShapeDtypeStruct((s, 256), BF16)] * 2, compiler_params=_params("arbitrary"),
    )(z, z, z, z, wbd, pscale, convw)


def _poolconv_bwd(dyb, dyc, z, wbd, pscale, convw):
    s = z.shape[0]

    def body(dyb_ref, dyc_ref, pu_ref, ch_ref, cb_ref, cc_ref, w_ref, ps_ref, cw_ref, dz_ref, dw_ref, dps_ref, dcw_ref):
        u = pu_ref[...]
        count = _pool_count(u.shape)
        p = (_pool_windows(u, _shift_down) / count - u).astype(BF16)
        wb = w_ref[...].astype(BF16)
        dyb_v = dyb_ref[...]
        pw = jnp.dot(p, wb, preferred_element_type=F32)
        dps_ref[...] = jnp.broadcast_to(jnp.sum(dyb_v * pw, axis=0, keepdims=True), (8, 256))
        dys = (dyb_v * ps_ref[...]).astype(BF16)
        dp = lax.dot_general(dys, wb, (((1,), (1,)), ((), ())), preferred_element_type=F32)
        dw_ref[...] = lax.dot_general(p, dys, (((0,), (0,)), ((), ())), preferred_element_type=F32)
        dz_ref[:, 0:256] = (_pool_windows(dp / count, _shift_up) - dp).astype(BF16)

        ch, cb, cc = ch_ref[...], cb_ref[...], cc_ref[...]
        uc = cc * ch
        cw = cw_ref[...]
        u2, u1 = _shift_down(uc, 2), _shift_down(uc, 1)
        conv = cw[0:1, :] * u2 + cw[1:2, :] * u1 + cw[2:3, :] * uc
        dyc_v = dyc_ref[...]
        dconv = dyc_v * cb
        du = cw[0:1, :] * _shift_up(dconv, 2) + cw[1:2, :] * _shift_up(dconv, 1) + cw[2:3, :] * dconv
        dz_ref[:, 256:512] = (du * cc).astype(BF16)
        dz_ref[:, 512:768] = (dyc_v * conv).astype(BF16)
        dz_ref[:, 768:1024] = (du * ch).astype(BF16)
        dcw_ref[...] = jnp.zeros_like(dcw_ref)
        dcw_ref[0:1, :] = jnp.sum(dconv * u2, axis=0, keepdims=True)
        dcw_ref[1:2, :] = jnp.sum(dconv * u1, axis=0, keepdims=True)
        dcw_ref[2:3, :] = jnp.sum(dconv * uc, axis=0, keepdims=True)

    blk = lambda r, c: pl.BlockSpec((r, c), lambda i: (0, 0))
    return pl.pallas_call(
        body, name="poolconv_bwd", grid=(1,),
        in_specs=[blk(s, 256), blk(s, 256)] + _pc_specs(s),
        out_specs=[blk(s, 1024), blk(256, 256), blk(8, 256), blk(8, 256)],
        out_shape=[jax.ShapeDtypeStruct((s, 1024), BF16), jax.ShapeDtypeStruct((256, 256), F32),
                   jax.ShapeDtypeStruct((8, 256), F32), jax.ShapeDtypeStruct((8, 256), F32)],
        compiler_params=_params("arbitrary"),
    )(dyb, dyc, z, z, z, z, wbd, pscale, convw)


_NT = (((1,), (1,)), ((), ()))
_TN = (((0,), (0,)), ((), ()))


def _att_logits(q, k, fc, fr, q0, k0):
    logits = lax.dot_general(q, k, _NT, preferred_element_type=F32) * (HEAD_DIM ** -0.5) + fc - fr
    row = q0 + lax.broadcasted_iota(jnp.int32, logits.shape, 0)
    col = k0 + lax.broadcasted_iota(jnp.int32, logits.shape, 1)
    return jnp.where(row >= col, logits, NEG_INF)


def _attn_fwd(q, k, v, fc, fr):
    h, s, dh = q.shape
    blk = ATT_BLOCK
    nb = s // blk

    def body(q_ref, k_ref, v_ref, fc_ref, fr_ref, o_ref, lse_ref):
        qi = pl.program_id(1)
        qv, fcv = q_ref[0], fc_ref[0]

        def step(j, carry):
            m, l, acc = carry
            k0 = pl.multiple_of(j * blk, blk)
            kv, vv = k_ref[0, pl.ds(k0, blk), :], v_ref[0, pl.ds(k0, blk), :]
            logits = _att_logits(qv, kv, fcv, fr_ref[0, pl.ds(j, 1), :], qi * blk, k0)
            m_new = jnp.maximum(m, jnp.max(logits, axis=-1, keepdims=True))
            p = jnp.exp(logits - m_new)
            alpha = jnp.exp(m - m_new)
            l = alpha * l + jnp.sum(p, axis=-1, keepdims=True)
            acc = alpha * acc + jnp.dot(p.astype(BF16), vv, preferred_element_type=F32)
            return m_new, l, acc

        init = (jnp.full((blk, 1), NEG_INF, F32), jnp.zeros((blk, 1), F32), jnp.zeros((blk, dh), F32))
        m, l, acc = lax.fori_loop(0, qi + 1, step, init)
        o_ref[0] = acc / l
        lse_ref[0] = m + jnp.log(l)

    qspec = pl.BlockSpec((1, blk, dh), lambda hh, i: (hh, i, 0))
    full = pl.BlockSpec((1, s, dh), lambda hh, i: (hh, 0, 0))
    cspec = pl.BlockSpec((1, blk, 1), lambda hh, i: (hh, i, 0))
    return pl.pallas_call(
        body, name="attn_fwd", grid=(h, nb),
        in_specs=[qspec, full, full, cspec, pl.BlockSpec((1, nb, blk), lambda hh, i: (hh, 0, 0))],
        out_specs=[qspec, cspec],
        out_shape=[jax.ShapeDtypeStruct((h, s, dh), F32), jax.ShapeDtypeStruct((h, s, 1), F32)],
        compiler_params=_params("parallel", "parallel"),
    )(q, k, v, fc, fr)


def _attn_bwd(q, k, v, do, o, lse, fc, fr):
    h, s, dh = q.shape
    blk = ATT_BLOCK
    nb = s // blk
    scale = HEAD_DIM ** -0.5

    def body(q_ref, k_ref, v_ref, do_ref, o_ref, lse_ref, fc_ref, fr_ref, dq_ref, dk_ref, dv_ref, dfc_ref, dfr_ref):
        dk_ref[...] = jnp.zeros_like(dk_ref)
        dv_ref[...] = jnp.zeros_like(dv_ref)
        dfr_ref[...] = jnp.zeros_like(dfr_ref)

        def outer(i, carry):
            q0 = pl.multiple_of(i * blk, blk)
            rows = pl.ds(q0, blk)
            qv, dov = q_ref[0, rows, :], do_ref[0, rows, :]
            delta = jnp.sum(dov * o_ref[0, rows, :], axis=-1, keepdims=True)
            dob = dov.astype(BF16)
            lsev, fcv = lse_ref[0, rows, :], fc_ref[0, rows, :]

            def inner(j, carry):
                dq, dfc = carry
                k0 = pl.multiple_of(j * blk, blk)
                cols = pl.ds(k0, blk)
                kv, vv = k_ref[0, cols, :], v_ref[0, cols, :]
                p = jnp.exp(_att_logits(qv, kv, fcv, fr_ref[0, pl.ds(j, 1), :], q0, k0) - lsev)
                dp = lax.dot_general(dob, vv, _NT, preferred_element_type=F32)
                ds = p * (dp - delta)
                dsb = ds.astype(BF16)
                dk_ref[0, cols, :] += lax.dot_general(dsb, qv, _TN, preferred_element_type=F32)
                dv_ref[0, cols, :] += lax.dot_general(p.astype(BF16), dob, _TN, preferred_element_type=F32)
                dfr_ref[0, pl.ds(j, 1), :] -= jnp.sum(ds, axis=0, keepdims=True)
                return dq + jnp.dot(dsb, kv, preferred_element_type=F32), dfc + jnp.sum(ds, axis=-1, keepdims=True)

            dq, dfc = lax.fori_loop(0, i + 1, inner, (jnp.zeros((blk, dh), F32), jnp.zeros((blk, 1), F32)))
            dq_ref[0, rows, :] = dq * scale
            dfc_ref[0, rows, :] = dfc
            return carry

        lax.fori_loop(0, nb, outer, 0)
        dk_ref[...] = dk_ref[...] * scale

    full = pl.BlockSpec((1, s, dh), lambda hh: (hh, 0, 0))
    col = pl.BlockSpec((1, s, 1), lambda hh: (hh, 0, 0))
    rowv = pl.BlockSpec((1, nb, blk), lambda hh: (hh, 0, 0))
    return pl.pallas_call(
        body, name="attn_bwd", grid=(h,),
        in_specs=[full, full, full, full, full, col, col, rowv],
        out_specs=[full, full, full, col, rowv],
        out_shape=[jax.ShapeDtypeStruct((h, s, dh), F32)] * 3 + [jax.ShapeDtypeStruct((h, s, 1), F32), jax.ShapeDtypeStruct((h, nb, blk), F32)],
        compiler_params=_params("parallel"),
    )(q, k, v, do, o, lse, fc, fr)


def _ada_fwd(c_all, w_ada, b_loc):
    depth, _, n = w_ada.shape
    tn = 512

    def body(c_ref, w_ref, b_ref, o_ref, sc_ref):
        cv = c_ref[...]
        sc = cv * jax.nn.sigmoid(cv)
        sc_ref[...] = sc
        o_ref[0] = jnp.dot(sc.astype(BF16), w_ref[0].astype(BF16), preferred_element_type=F32) + b_ref[0]

    return pl.pallas_call(
        body, name="ada_fwd", grid=(depth, n // tn),
        in_specs=[pl.BlockSpec((N_DEV, D), lambda l, j: (0, 0)), pl.BlockSpec((1, D, tn), lambda l, j: (l, 0, j)),
                  pl.BlockSpec((1, 1, tn), lambda l, j: (l, 0, j))],
        out_specs=[pl.BlockSpec((1, N_DEV, tn), lambda l, j: (l, 0, j)), pl.BlockSpec((N_DEV, D), lambda l, j: (0, 0))],
        out_shape=[jax.ShapeDtypeStruct((depth, N_DEV, n), F32), jax.ShapeDtypeStruct((N_DEV, D), F32)],
        compiler_params=_params("arbitrary", "arbitrary"),
    )(c_all, w_ada, b_loc)


def _sum_devices(gathered):
    n = gathered.shape[1]
    tn = _pick(n, (1408, 1024, 640, 512, 128))

    def body(g_ref, o_ref):
        acc = g_ref[0:8, :]
        for dev in range(1, N_DEV):
            acc = acc + g_ref[8 * dev:8 * dev + 8, :]
        o_ref[...] = acc

    return pl.pallas_call(
        body, name="sum_devices", grid=(n // tn,),
        in_specs=[pl.BlockSpec((8 * N_DEV, tn), lambda j: (0, j))], out_specs=pl.BlockSpec((8, tn), lambda j: (0, j)),
        out_shape=jax.ShapeDtypeStruct((8, n), F32), compiler_params=_params("parallel"),
    )(gathered)


def _place():
    x, y, c = lax.axis_index("x"), lax.axis_index("y"), lax.axis_index("c")
    chips = [(1 - x, y), (x, 1 - y), (1 - x, 1 - y)]
    return x, y, c, chips


def _allgather8(block, name):
    m_per, n = block.shape

    def body(x_ref, out_ref, send_sems, recv_sems, local_sem):
        x, y, c, chips = _place()
        me, sibling = (x, y, c), (x, y, 1 - c)

        def rows(px, py, pc):
            return out_ref.at[pl.ds((4 * px + 2 * py + pc) * m_per, m_per), :]

        def copy(k, blk, to, src=None):
            return pltpu.make_async_remote_copy(
                src_ref=rows(*blk) if src is None else src, dst_ref=rows(*blk),
                send_sem=send_sems.at[k], recv_sem=recv_sems.at[k], device_id=to, device_id_type=MESH)

        mine = pltpu.make_async_copy(x_ref, rows(*me), local_sem)
        mine.start()
        first = [copy(0, me, sibling, src=x_ref)]
        first += [copy(1 + j, me, (*chip, c), src=x_ref) for j, chip in enumerate(chips)]
        for cp in first:
            cp.start()
        passed = [copy(4 + j, (*chip, c), sibling) for j, chip in enumerate(chips)]
        for j, chip in enumerate(chips):
            copy(1 + j, (*chip, c), me).wait_recv()
            passed[j].start()
        copy(0, sibling, me).wait_recv()
        for j, chip in enumerate(chips):
            copy(4 + j, (*chip, 1 - c), me).wait_recv()
        for cp in first + passed:
            cp.wait_send()
        mine.wait()

    return pl.pallas_call(
        body, name=name, out_shape=jax.ShapeDtypeStruct((N_DEV * m_per, n), block.dtype),
        in_specs=[pl.BlockSpec(memory_space=pltpu.VMEM)], out_specs=pl.BlockSpec(memory_space=pltpu.VMEM),
        scratch_shapes=[pltpu.SemaphoreType.DMA((7,)), pltpu.SemaphoreType.DMA((7,)), pltpu.SemaphoreType.DMA],
        compiler_params=pltpu.CompilerParams(vmem_limit_bytes=V7X_VMEM_LIMIT),
    )(block)


def _gather_weights(shards):
    n = len(shards)

    def body(*refs):
        ins, outs = refs[:n], refs[n:2 * n]
        send_sems, recv_sems, local_sems = refs[2 * n:]
        x, y, c, chips = _place()
        j = 2 * x + y
        sibling = (x, y, 1 - c)
        chip_ids = [2 * px + py for px, py in chips]

        def remote(a, slot, src, dst, to):
            return pltpu.make_async_remote_copy(src_ref=src, dst_ref=dst, send_sem=send_sems.at[6 * a + slot],
                                                recv_sem=recv_sems.at[6 * a + slot], device_id=to, device_id_type=MESH)

        local = [pltpu.make_async_copy(ins[a], outs[a].at[j], local_sems.at[a]) for a in range(n)]
        for cp in local:
            cp.start()
        sends = [remote(a, k, ins[a].at[c], outs[a].at[j, c], (*chips[k], c)) for a in range(n) for k in range(3)]
        for cp in sends:
            cp.start()
        passed = []
        for a in range(n):
            for k in range(3):
                landed = outs[a].at[chip_ids[k], c]
                remote(a, k, landed, landed, sibling).wait_recv()
                fwd = remote(a, 3 + k, landed, landed, sibling)
                fwd.start()
                passed.append(fwd)
        for a in range(n):
            for k in range(3):
                from_sibling = outs[a].at[chip_ids[k], 1 - c]
                remote(a, 3 + k, from_sibling, from_sibling, sibling).wait_recv()
        for cp in sends + passed:
            cp.wait_send()
        for cp in local:
            cp.wait()

    return pl.pallas_call(
        body, name="gather_weights",
        out_shape=[jax.ShapeDtypeStruct((N_CHIPS,) + s.shape, s.dtype) for s in shards],
        in_specs=[_HBM] * n, out_specs=[_HBM] * n,
        scratch_shapes=[pltpu.SemaphoreType.DMA((6 * n,)), pltpu.SemaphoreType.DMA((6 * n,)), pltpu.SemaphoreType.DMA((n,))],
    )(*shards)


def _send_sibling_layer(grads):
    n = len(grads)
    flat = [g for pair in grads for g in pair]

    def body(*refs):
        ins, outs = refs[:2 * n], refs[2 * n:3 * n]
        send_sems, recv_sems = refs[3 * n:]
        x, y, c, _ = _place()
        sibling = (x, y, 1 - c)
        for mine in (0, 1):
            @pl.when(c == mine)
            def _():
                cps = [pltpu.make_async_remote_copy(src_ref=ins[2 * a + (1 - mine)], dst_ref=outs[a], send_sem=send_sems.at[a],
                                                    recv_sem=recv_sems.at[a], device_id=sibling, device_id_type=MESH)
                       for a in range(n)]
                for cp in cps:
                    cp.start()
                for cp in cps:
                    cp.wait()

    return pl.pallas_call(
        body, name="rs_sibling", out_shape=[jax.ShapeDtypeStruct(pair[0].shape, pair[0].dtype) for pair in grads],
        in_specs=[_HBM] * (2 * n), out_specs=[_HBM] * n,
        scratch_shapes=[pltpu.SemaphoreType.DMA((n,)), pltpu.SemaphoreType.DMA((n,))],
    )(*flat)


def _chip_sum(g0, g1, other, sel, name):
    _, r, cdim = g0.shape
    tr = _pick(r, (256, 128))

    def body(sel_ref, g0_ref, g1_ref, t_ref, wire_ref, own_ref):
        p = pl.program_id(2)
        total = jnp.where(sel_ref[0] == 0, g0_ref[0], g1_ref[0]) + t_ref[0]
        wire_ref[0] = total.astype(BF16)

        @pl.when(p == sel_ref[1])
        def _():
            own_ref[...] = total

    def pick(layer):
        def index(i, jj, p, sel_ref):
            use = jnp.where(sel_ref[0] == layer, 1, 0)
            return (p * use, i * use, 0)
        return pl.BlockSpec((1, tr, cdim), index)

    blk = pl.BlockSpec((1, tr, cdim), lambda i, jj, p, sel_ref: (p, i, 0))
    return pl.pallas_call(
        body, name=name,
        grid_spec=pltpu.PrefetchScalarGridSpec(
            num_scalar_prefetch=1, grid=(r // tr, 1, N_CHIPS),
            in_specs=[pick(0), pick(1), blk], out_specs=[blk, pl.BlockSpec((tr, cdim), lambda i, jj, p, sel_ref: (i, 0))]),
        out_shape=[jax.ShapeDtypeStruct(g0.shape, BF16), jax.ShapeDtypeStruct((r, cdim), F32)],
        compiler_params=_params("parallel", "arbitrary", "arbitrary"),
    )(sel, g0, g1, other)


def _exchange_chips(wires):
    n = len(wires)

    def body(*refs):
        ins, outs = refs[:n], refs[n:2 * n]
        send_sems, recv_sems = refs[2 * n:]
        x, y, c, chips = _place()
        cps = [pltpu.make_async_remote_copy(src_ref=ins[a].at[2 * chips[k][0] + chips[k][1]], dst_ref=outs[a].at[k],
                                            send_sem=send_sems.at[3 * a + k], recv_sem=recv_sems.at[3 * a + k],
                                            device_id=(*chips[k], c), device_id_type=MESH)
               for a in range(n) for k in range(3)]
        for cp in cps:
            cp.start()
        for cp in cps:
            cp.wait()

    return pl.pallas_call(
        body, name="rs_chips", out_shape=[jax.ShapeDtypeStruct((3,) + w.shape[1:], w.dtype) for w in wires],
        in_specs=[_HBM] * n, out_specs=[_HBM] * n,
        scratch_shapes=[pltpu.SemaphoreType.DMA((3 * n,)), pltpu.SemaphoreType.DMA((3 * n,))],
    )(*wires)


def _final_sum(own, recv, name):
    r, cdim = own.shape
    tr = _pick(r, (256, 128))

    def body(own_ref, r0_ref, r1_ref, r2_ref, o_ref):
        o_ref[...] = ((own_ref[...] + r0_ref[0].astype(F32)) + r1_ref[0].astype(F32)) + r2_ref[0].astype(F32)

    part = lambda k: pl.BlockSpec((1, tr, cdim), lambda i, k=k: (k, i, 0))
    spec = pl.BlockSpec((tr, cdim), lambda i: (i, 0))
    return pl.pallas_call(
        body, name=name, grid=(r // tr,), in_specs=[spec, part(0), part(1), part(2)], out_specs=spec,
        out_shape=jax.ShapeDtypeStruct((r, cdim), F32), compiler_params=_params("parallel"),
    )(own, recv, recv, recv)


def _share_layers(sums):
    n = len(sums)

    def body(*refs):
        ins, outs = refs[:n], refs[n:2 * n]
        send_sems, recv_sems, local_sems = refs[2 * n:]
        x, y, c, _ = _place()
        local = [pltpu.make_async_copy(ins[a], outs[a].at[c], local_sems.at[a]) for a in range(n)]
        cps = [pltpu.make_async_remote_copy(src_ref=ins[a], dst_ref=outs[a].at[c], send_sem=send_sems.at[a], recv_sem=recv_sems.at[a],
                                            device_id=(x, y, 1 - c), device_id_type=MESH) for a in range(n)]
        for cp in local + cps:
            cp.start()
        for a in range(n):
            arrived = outs[a].at[1 - c]
            pltpu.make_async_remote_copy(src_ref=arrived, dst_ref=arrived, send_sem=send_sems.at[a], recv_sem=recv_sems.at[a],
                                         device_id=(x, y, 1 - c), device_id_type=MESH).wait_recv()
        for cp in cps:
            cp.wait_send()
        for cp in local:
            cp.wait()

    return pl.pallas_call(
        body, name="rs_share", out_shape=[jax.ShapeDtypeStruct((DEPTH,) + s.shape, s.dtype) for s in sums],
        in_specs=[_HBM] * n, out_specs=[_HBM] * n,
        scratch_shapes=[pltpu.SemaphoreType.DMA((n,)), pltpu.SemaphoreType.DMA((n,)), pltpu.SemaphoreType.DMA((n,))],
    )(*sums)


def _row(v):
    return v.reshape(1, -1)


def _layer_fwd(x, w, mod):
    s = x.shape[0]
    nb = s // ATT_BLOCK
    h = _modnorm_fwd(x, _row(w["g_mix_pre"]), mod[0:1], mod[1:2], "mix_pre_fwd")
    z = _mm(h, w["w_all"], name="mm_in")
    qkv = z[:, Z_QKV:Z_PC].astype(BF16).reshape(s, 3, HEADS, HEAD_DIM).transpose(1, 2, 0, 3)
    fl = z[:, Z_FL:Z_COLS]
    cum = _cumf_fwd(fl, w["b_f_pad"])
    f_heads = cum[:, :HEADS].T
    fc, fr = f_heads[:, :, None], f_heads.reshape(HEADS, nb, ATT_BLOCK)
    o, lse = _attn_fwd(qkv[0], qkv[1], qkv[2], fc, fr)
    br_a = o.transpose(1, 0, 2).reshape(s, A_WIDTH).astype(BF16)
    br_b, br_c = _poolconv_fwd(z, w["w_pool_bd"], _row(w["pool_scale"]), w["conv_w"])
    wbr = w["w_branch"]
    pa = _mm(br_a, wbr[:A_WIDTH], name="mm_br_a")
    pb = _mm(br_b, wbr[A_WIDTH:A_WIDTH + POOL_WIDTH], name="mm_br_b")
    pc = _mm(br_c, wbr[A_WIDTH + POOL_WIDTH:], name="mm_br_c")
    merged = _merge_fwd(z, pa, pb, pc)
    y = _mm(merged, w["w_out"], name="mm_out")
    x1 = _post_fwd(x, y, _row(w["g_mix_post"]), mod[2:3], "mix_post_fwd")
    h2 = _modnorm_fwd(x1, _row(w["g_ff_pre"]), mod[3:4], mod[4:5], "ff_pre_fwd")
    a = _mm(h2, w["w_ff1"], name="mm_ff1")
    r = _act_fwd(a)
    y2 = _mm(r, w["w_ff2"], name="mm_ff2")
    x2 = _post_fwd(x1, y2, _row(w["g_ff_post"]), mod[5:6], "ff_post_fwd")
    saved = dict(x=x, h=h, z=z, qkv=qkv, fl=fl, fc=fc, fr=fr, o=o, lse=lse, br_a=br_a, br_b=br_b, br_c=br_c, pa=pa, pb=pb, pc=pc,
                 merged=merged, y=y, x1=x1, h2=h2, a=a, r=r, y2=y2)
    return x2, saved


def _layer_bwd(dx2, sv, w, mod):
    s = dx2.shape[0]
    dy2, sum_ff_post = _post_bwd(dx2, sv["y2"], _row(w["g_ff_post"]), mod[5:6], "ff_post_bwd")
    dr = _mm(dy2, w["w_ff2"], tb=True, name="mm_ff2_dx")
    d_w_ff2 = _mm(sv["r"], dy2, ta=True, name="mm_ff2_dw")
    da = _act_bwd(dr, sv["a"])
    dh2 = _mm(da, w["w_ff1"], tb=True, name="mm_ff1_dx")
    d_w_ff1 = _mm(sv["h2"], da, ta=True, name="mm_ff1_dw")
    dx1, sum_ff_pre = _modnorm_bwd(dh2, sv["x1"], dx2, _row(w["g_ff_pre"]), mod[4:5], "ff_pre_bwd")

    dy, sum_mix_post = _post_bwd(dx1, sv["y"], _row(w["g_mix_post"]), mod[2:3], "mix_post_bwd")
    dmerged = _mm(dy, w["w_out"], tb=True, name="mm_out_dx")
    d_w_out = _mm(sv["merged"], dy, ta=True, name="mm_out_dw")
    dgl, dpa, dpb, dpc = _merge_bwd(dmerged, sv["z"], sv["pa"], sv["pb"], sv["pc"])
    wbr = w["w_branch"]
    dbr_a = _mm(dpa, wbr[:A_WIDTH], tb=True, name="mm_br_a_dx")
    dbr_b = _mm(dpb, wbr[A_WIDTH:A_WIDTH + POOL_WIDTH], tb=True, name="mm_br_b_dx")
    dbr_c = _mm(dpc, wbr[A_WIDTH + POOL_WIDTH:], tb=True, name="mm_br_c_dx")
    d_w_branch = jnp.concatenate([_mm(sv["br_a"], dpa, ta=True, name="mm_br_a_dw"), _mm(sv["br_b"], dpb, ta=True, name="mm_br_b_dw"),
                                  _mm(sv["br_c"], dpc, ta=True, name="mm_br_c_dw")], axis=0)

    do = dbr_a.reshape(s, HEADS, HEAD_DIM).transpose(1, 0, 2)
    qkv = sv["qkv"]
    dq, dk, dv, dfc, dfr = _attn_bwd(qkv[0], qkv[1], qkv[2], do, sv["o"], sv["lse"], sv["fc"], sv["fr"])
    dcum = jnp.pad((dfc[:, :, 0] + dfr.reshape(HEADS, s)).T, ((0, 0), (0, 128 - HEADS)))
    dfl, sum_bf = _cumf_bwd(dcum, sv["fl"], w["b_f_pad"])
    dpc_z, d_wbd, sum_ps, sum_cw = _poolconv_bwd(dbr_b, dbr_c, sv["z"], w["w_pool_bd"], _row(w["pool_scale"]), w["conv_w"])
    dqkv = jnp.stack([dq, dk, dv]).transpose(2, 0, 1, 3).reshape(s, 3 * A_WIDTH).astype(BF16)
    dz = jnp.concatenate([dgl, dqkv, dpc_z, dfl], axis=1)
    dh = _mm(dz, w["w_all"], tb=True, name="mm_in_dx")
    d_w_all = _mm(sv["h"], dz, ta=True, name="mm_in_dw")
    dx, sum_mix_pre = _modnorm_bwd(dh, sv["x"], dx1, _row(w["g_mix_pre"]), mod[1:2], "mix_pre_bwd")

    dmod = jnp.stack([sum_mix_pre[0], sum_mix_pre[1], sum_mix_post[0], sum_ff_pre[0], sum_ff_pre[1], sum_ff_post[0]])
    d_w_in = jnp.concatenate([d_w_all[:, Z_QKV:Z_PC], d_w_all[:, Z_FL:Z_FL + HEADS], d_w_all[:, Z_PC:Z_FL], d_w_all[:, Z_GL:Z_QKV]], axis=1)
    d_w_pool = jnp.stack([d_wbd[64 * g:64 * g + 64, 64 * g:64 * g + 64] for g in range(4)])
    big = dict(w_in=d_w_in, w_branch=d_w_branch, w_out=d_w_out, w_ff1=d_w_ff1, w_ff2=d_w_ff2)
    small = dict(g_mix_pre=sum_mix_pre[2], g_mix_post=sum_mix_post[1], g_ff_pre=sum_ff_pre[2], g_ff_post=sum_ff_post[1],
                 b_f=sum_bf[0, :HEADS], w_pool=d_w_pool, pool_scale=sum_ps[0], conv_w=sum_cw[0:3])
    return dx, dmod, big, small


def _full_layer_weights(w_in_nat, w_branch, w_out, w_ff1, w_ff2, g_mix_pre, g_mix_post, g_ff_pre, g_ff_post, b_f, w_pool, pool_scale, conv_w):
    q_end, f_end, pc_end = 3 * A_WIDTH, 3 * A_WIDTH + HEADS, 3 * A_WIDTH + HEADS + POOL_WIDTH + 3 * CONV_WIDTH
    w_all = jnp.concatenate([w_in_nat[:, pc_end:], w_in_nat[:, :q_end], w_in_nat[:, f_end:pc_end], w_in_nat[:, q_end:f_end],
                             jnp.zeros((D, Z_COLS - Z_FL - HEADS), w_in_nat.dtype)], axis=1)
    wbd = jnp.zeros((POOL_WIDTH, POOL_WIDTH), F32)
    for g in range(4):
        wbd = wbd.at[64 * g:64 * g + 64, 64 * g:64 * g + 64].set(w_pool[g])
    return dict(w_all=w_all, w_branch=w_branch, w_out=w_out, w_ff1=w_ff1, w_ff2=w_ff2, g_mix_pre=g_mix_pre, g_mix_post=g_mix_post,
                g_ff_pre=g_ff_pre, g_ff_post=g_ff_post, b_f_pad=jnp.pad(b_f, (0, 128 - HEADS)).reshape(1, 128), w_pool_bd=wbd,
                pool_scale=pool_scale, conv_w=conv_w)


def _local_step(x, target, mods, layers):
    saved = []
    act = x
    for l in range(DEPTH):
        act, sv = _layer_fwd(act, layers[l], mods[l])
        saved.append(sv)
    dact, sq = _loss_head(act, target)
    loss = sq[0, 0] * (0.5 / D)
    dmods, bigs, smalls = [None] * DEPTH, [None] * DEPTH, [None] * DEPTH
    for l in reversed(range(DEPTH)):
        dact, dmods[l], bigs[l], smalls[l] = _layer_bwd(dact, saved[l], layers[l], mods[l])
    return loss, dact, jnp.stack(dmods), bigs, smalls


_SMALL = ("g_mix_pre", "g_mix_post", "g_ff_pre", "g_ff_post", "b_f", "w_pool", "pool_scale", "conv_w")
_BIG = ("w_in", "w_branch", "w_out", "w_ff1", "w_ff2")


def _pack(parts, rows=8):
    flat = jnp.concatenate([p.reshape(-1) for p in parts])
    width = -(-flat.shape[0] // (rows * 128)) * 128
    return jnp.pad(flat, (0, rows * width - flat.shape[0])).reshape(rows, width)


def _unpack(packed, like):
    flat = packed.reshape(-1)
    out, at = [], 0
    for ref in like:
        out.append(flat[at:at + ref.size].reshape(ref.shape))
        at += ref.size
    return out


def kernel(x, c, w_ada, b_ada, g_mix_pre, g_mix_post, g_ff_pre, g_ff_post, w_in, b_f, w_pool, pool_scale, conv_w, w_branch, w_out, w_ff1, w_ff2, loss_target, m_w_ada, m_b_ada, m_g_mix_pre, m_g_mix_post, m_g_ff_pre, m_g_ff_post, m_w_in, m_b_f, m_w_pool, m_pool_scale, m_conv_w, m_w_branch, m_w_out, m_w_ff1, m_w_ff2, v_w_ada, v_b_ada, v_g_mix_pre, v_g_mix_post, v_g_ff_pre, v_g_ff_post, v_w_in, v_b_f, v_w_pool, v_pool_scale, v_conv_w, v_w_branch, v_w_out, v_w_ff1, v_w_ff2):
    xi, yi, ci = lax.axis_index("x"), lax.axis_index("y"), lax.axis_index("c")
    chip = 2 * xi + yi
    dev = 2 * chip + ci
    n_ada = w_ada.shape[2]

    g_in, g_br, g_out, g_f1, g_f2 = _gather_weights([w.astype(BF16) for w in (w_in, w_branch, w_out, w_ff1, w_ff2)])
    full_in = g_in.transpose(1, 2, 0, 3).reshape(DEPTH, D, IN_COLS)
    full_br = g_br.transpose(1, 0, 2, 3).reshape(DEPTH, D, D)
    full_out = g_out.transpose(1, 0, 2, 3).reshape(DEPTH, D, D)
    full_f1 = g_f1.transpose(1, 2, 0, 3).reshape(DEPTH, D, D_FF)
    full_f2 = g_f2.transpose(1, 0, 2, 3).reshape(DEPTH, D_FF, D)

    first = jnp.zeros((8, D + 384), F32).at[0, :D].set(c[0]).at[0, D:].set(conv_w.reshape(-1))
    got = _allgather8(first, "gather_cond").reshape(N_DEV, 8, D + 384)[:, 0]
    c_all = got[:, :D]
    conv_full = got[0::2, D:].reshape(N_CHIPS, DEPTH, 3, CONV_WIDTH // N_CHIPS).transpose(1, 2, 0, 3).reshape(DEPTH, 3, CONV_WIDTH)

    b_loc = lax.dynamic_slice_in_dim(b_ada, chip * n_ada, n_ada, axis=1).reshape(DEPTH, 1, n_ada)
    mod_cols, silu_c = _ada_fwd(c_all, w_ada, b_loc)
    got = _allgather8(mod_cols.reshape(DEPTH * N_DEV, n_ada), "gather_mod").reshape(N_DEV, DEPTH, N_DEV, n_ada)[0::2]
    mod_all = got.transpose(1, 2, 0, 3).reshape(DEPTH, N_DEV, 6, D)
    mods = lax.dynamic_index_in_dim(mod_all, dev, axis=1, keepdims=False)

    layers = [_full_layer_weights(full_in[l], full_br[l], full_out[l], full_f1[l], full_f2[l], g_mix_pre[l], g_mix_post[l], g_ff_pre[l],
                                  g_ff_post[l], b_f[l], w_pool[l], pool_scale[l], conv_full[l]) for l in range(DEPTH)]
    loss_part, grad_x, dmods, bigs, smalls = _local_step(x[0], loss_target[0], mods, layers)
    loss = lax.psum(loss_part, ("x", "y", "c"))

    small_parts = [smalls[l][name] for name in _SMALL for l in range(DEPTH)]
    packed = _pack([dmods] + small_parts)
    gathered = _allgather8(packed, "gather_small")
    dmod_all = gathered.reshape(N_DEV, -1)[:, :dmods.size].reshape(N_DEV, DEPTH, 6 * D)
    summed = _unpack(_sum_devices(gathered), [dmods] + small_parts)
    grad_b_ada = summed[0].reshape(DEPTH, 6 * D)
    small_grads = {name: jnp.stack(summed[1 + 2 * i:3 + 2 * i]) for i, name in enumerate(_SMALL)}
    small_grads["conv_w"] = lax.dynamic_slice_in_dim(small_grads["conv_w"], chip * (CONV_WIDTH // N_CHIPS), CONV_WIDTH // N_CHIPS, axis=2)

    dmod_loc = lax.dynamic_slice_in_dim(dmod_all.transpose(1, 0, 2), chip * n_ada, n_ada, axis=2)
    silu_pad = jnp.pad(silu_c, ((0, 128 - N_DEV), (0, 0)))
    grad_w_ada = jnp.stack([_mm(silu_pad, jnp.pad(dmod_loc[l], ((0, 128 - N_DEV), (0, 0))), ta=True, name="mm_ada_dw") for l in range(DEPTH)])

    def shard_blocks(name, g):
        if name == "w_in":
            return g.reshape(D, N_CHIPS, IN_COLS // N_CHIPS).transpose(1, 0, 2)
        if name == "w_ff1":
            return g.reshape(D, N_CHIPS, D_FF // N_CHIPS).transpose(1, 0, 2)
        return g.reshape(N_CHIPS, g.shape[0] // N_CHIPS, g.shape[1])

    pairs = [tuple(shard_blocks(name, bigs[l][name]) for l in range(DEPTH)) for name in _BIG]
    others = _send_sibling_layer(pairs)
    sel = jnp.stack([ci, chip]).astype(jnp.int32)
    wires, owns = zip(*[_chip_sum(pairs[i][0], pairs[i][1], others[i], sel, "rs_chip_sum_" + name) for i, name in enumerate(_BIG)])
    recvs = _exchange_chips(list(wires))
    sums = [_final_sum(owns[i], recvs[i], "rs_final_" + name) for i, name in enumerate(_BIG)]
    big_grads = dict(zip(_BIG, _share_layers(sums)))

    grads = dict(w_ada=grad_w_ada, b_ada=grad_b_ada, **small_grads, **big_grads)
    weights = dict(w_ada=w_ada, b_ada=b_ada, g_mix_pre=g_mix_pre, g_mix_post=g_mix_post, g_ff_pre=g_ff_pre, g_ff_post=g_ff_post, w_in=w_in,
                   b_f=b_f, w_pool=w_pool, pool_scale=pool_scale, conv_w=conv_w, w_branch=w_branch, w_out=w_out, w_ff1=w_ff1, w_ff2=w_ff2)
    m_in = dict(w_ada=m_w_ada, b_ada=m_b_ada, g_mix_pre=m_g_mix_pre, g_mix_post=m_g_mix_post, g_ff_pre=m_g_ff_pre, g_ff_post=m_g_ff_post,
                w_in=m_w_in, b_f=m_b_f, w_pool=m_w_pool, pool_scale=m_pool_scale, conv_w=m_conv_w, w_branch=m_w_branch, w_out=m_w_out,
                w_ff1=m_w_ff1, w_ff2=m_w_ff2)
    v_in = dict(w_ada=v_w_ada, b_ada=v_b_ada, g_mix_pre=v_g_mix_pre, g_mix_post=v_g_mix_post, g_ff_pre=v_g_ff_pre, g_ff_post=v_g_ff_post,
                w_in=v_w_in, b_f=v_b_f, w_pool=v_w_pool, pool_scale=v_pool_scale, conv_w=v_conv_w, w_branch=v_w_branch, w_out=v_w_out,
                w_ff1=v_w_ff1, w_ff2=v_w_ff2)
    order = ("w_ada", "b_ada", "g_mix_pre", "g_mix_post", "g_ff_pre", "g_ff_post", "w_in", "b_f", "w_pool", "pool_scale", "conv_w",
             "w_branch", "w_out", "w_ff1", "w_ff2")
    delta, new_m, new_v = {}, {}, {}
    for name in ("w_ada",) + _BIG:
        shape = weights[name].shape
        flat = lambda t: t.reshape(shape[0] * shape[1], shape[2])
        res = _adamw(flat(weights[name]), flat(grads[name]), flat(m_in[name]), flat(v_in[name]), "adamw_" + name)
        delta[name], new_m[name], new_v[name] = [t.reshape(shape) for t in res]
    tiny = ("b_ada",) + _SMALL
    res = _adamw(*[_pack([src[name] for name in tiny]) for src in (weights, grads, m_in, v_in)], "adamw_small")
    for out, packed_res in zip((delta, new_m, new_v), res):
        for name, val in zip(tiny, _unpack(packed_res, [weights[name] for name in tiny])):
            out[name] = val

    return (loss, grad_x[None], *[grads[n] for n in order], *[delta[n] for n in order], *[new_m[n] for n in order],
            *[new_v[n] for n in order])
```

```python
import functools

import jax
import jax.numpy as jnp
from jax import lax
from jax.experimental import pallas as pl
from jax.experimental.pallas import tpu as pltpu

F32 = jnp.float32
BF16 = jnp.bfloat16
MESH = pl.DeviceIdType.MESH

D = 1024
DEPTH = 2
HEADS = 8
HEAD_DIM = 64
A_WIDTH = 512
POOL_WIDTH = 256
CONV_WIDTH = 256
D_FF = 4096
IN_COLS = 5640
Z_GL, Z_QKV, Z_PC, Z_FL, Z_COLS = 0, 3072, 4608, 5632, 5760
RMS_EPS = 1e-6
NEG_INF = -1e30
ATT_BLOCK = 256
ROW_TILE = 256
N_CHIPS = 4
N_DEV = 8
V7X_VMEM_LIMIT = 48 * 1024 * 1024

ADAM_LR = 0.001
ADAM_B1 = 0.9
ADAM_B2 = 0.999
ADAM_EPS = 1e-08
ADAM_WD = 0.01
ADAM_STEP = 10

_HBM = pl.BlockSpec(memory_space=pltpu.HBM)


def _params(*sem):
    return pltpu.CompilerParams(dimension_semantics=sem, vmem_limit_bytes=V7X_VMEM_LIMIT)


def _pick(dim, cands):
    for cand in cands:
        if dim % cand == 0:
            return cand
    return dim


def _mm(a, b, *, ta=False, tb=False, out_dtype=F32, name):
    (k, m) = a.shape if ta else a.shape[::-1]
    (n, k2) = b.shape if tb else b.shape[::-1]
    assert k == k2, (a.shape, b.shape, ta, tb)
    tm = _pick(m, (512, 256, 128))
    tn = _pick(n, (1024, 640, 512, 256, 128))
    tk = _pick(k, (512, 640, 256, 128))
    nk = k // tk
    dims = (((0 if ta else 1,), (1 if tb else 0,)), ((), ()))

    def body(a_ref, b_ref, o_ref, acc_ref):
        kk = pl.program_id(2)

        @pl.when(kk == 0)
        def _():
            acc_ref[...] = jnp.zeros_like(acc_ref)

        acc_ref[...] += lax.dot_general(a_ref[...].astype(BF16), b_ref[...].astype(BF16), dims,
                                        preferred_element_type=F32)

        @pl.when(kk == nk - 1)
        def _():
            o_ref[...] = acc_ref[...].astype(o_ref.dtype)

    a_spec = pl.BlockSpec((tk, tm), lambda i, j, kk: (kk, i)) if ta else pl.BlockSpec((tm, tk), lambda i, j, kk: (i, kk))
    b_spec = pl.BlockSpec((tn, tk), lambda i, j, kk: (j, kk)) if tb else pl.BlockSpec((tk, tn), lambda i, j, kk: (kk, j))
    return pl.pallas_call(
        body, name=name, grid=(m // tm, n // tn, nk),
        in_specs=[a_spec, b_spec], out_specs=pl.BlockSpec((tm, tn), lambda i, j, kk: (i, j)),
        out_shape=jax.ShapeDtypeStruct((m, n), out_dtype),
        scratch_shapes=[pltpu.VMEM((tm, tn), F32)],
        compiler_params=_params("parallel", "parallel", "arbitrary"),
    )(a, b)


def _ew(fn, ins, out_dtypes, name, tc=None):
    rows, cols = ins[0].shape
    tr = _pick(rows, (ROW_TILE, 128, 8))
    tc = cols if tc is None else tc
    n_in = len(ins)

    def body(*refs):
        res = fn(*[r[...] for r in refs[:n_in]])
        for o_ref, val in zip(refs[n_in:], res):
            o_ref[...] = val.astype(o_ref.dtype)

    spec = pl.BlockSpec((tr, tc), lambda i, j: (i, j))
    return pl.pallas_call(
        body, name=name, grid=(rows // tr, cols // tc),
        in_specs=[spec] * n_in, out_specs=[spec] * len(out_dtypes),
        out_shape=[jax.ShapeDtypeStruct((rows, cols), dt) for dt in out_dtypes],
        compiler_params=_params("parallel", "parallel"),
    )(*ins)


def _act_fwd(a):
    def fn(a):
        r = jnp.maximum(a, 0.0)
        return (r * r,)
    return _ew(fn, [a], [BF16], "act_fwd", tc=2048)[0]


def _act_bwd(dr, a):
    def fn(dr, a):
        return (dr * (2.0 * jnp.maximum(a, 0.0)),)
    return _ew(fn, [dr, a], [BF16], "act_bwd", tc=2048)[0]


def _adamw(w, g, m, v, name):
    bc1 = 1.0 - ADAM_B1 ** ADAM_STEP
    bc2 = 1.0 - ADAM_B2 ** ADAM_STEP

    def fn(w, g, m, v):
        m = ADAM_B1 * m + (1.0 - ADAM_B1) * g
        v = ADAM_B2 * v + (1.0 - ADAM_B2) * (g * g)
        m_hat = m / bc1
        v_hat = v / bc2
        delta = -ADAM_LR * (m_hat / (jnp.sqrt(v_hat) + ADAM_EPS) + ADAM_WD * w)
        return delta, m, v
    return _ew(fn, [w, g, m, v], [F32, F32, F32], name)


def _row_spec(cols, block=0):
    return pl.BlockSpec((ROW_TILE, cols), lambda i, block=block: (i, block))


def _vec_spec(cols):
    return pl.BlockSpec((1, cols), lambda i: (0, 0))


def _sum_spec(cols):
    return pl.BlockSpec((8, cols), lambda i: (0, 0))


def _rstd(x):
    return lax.rsqrt(jnp.mean(x * x, axis=-1, keepdims=True) + RMS_EPS)


def _modnorm_fwd(x, g, shift, scale, name):
    s = x.shape[0]

    def body(x_ref, g_ref, sh_ref, sc_ref, h_ref):
        xv = x_ref[...]
        n = xv * _rstd(xv)
        h_ref[...] = ((n * g_ref[...]) * (1.0 + sc_ref[...]) + sh_ref[...]).astype(BF16)

    return pl.pallas_call(
        body, name=name, grid=(s // ROW_TILE,),
        in_specs=[_row_spec(D), _vec_spec(D), _vec_spec(D), _vec_spec(D)], out_specs=_row_spec(D),
        out_shape=jax.ShapeDtypeStruct((s, D), BF16), compiler_params=_params("parallel"),
    )(x, g, shift, scale)


def _post_fwd(x, y, g, gate, name):
    s = x.shape[0]

    def body(x_ref, y_ref, g_ref, gate_ref, o_ref):
        yv = y_ref[...]
        o_ref[...] = x_ref[...] + gate_ref[...] * ((yv * _rstd(yv)) * g_ref[...])

    return pl.pallas_call(
        body, name=name, grid=(s // ROW_TILE,),
        in_specs=[_row_spec(D), _row_spec(D), _vec_spec(D), _vec_spec(D)], out_specs=_row_spec(D),
        out_shape=jax.ShapeDtypeStruct((s, D), F32), compiler_params=_params("parallel"),
    )(x, y, g, gate)


def _post_bwd(dxo, y, g, gate, name):
    s = dxo.shape[0]

    def body(d_ref, y_ref, g_ref, gate_ref, dy_ref, sum_ref):
        @pl.when(pl.program_id(0) == 0)
        def _():
            sum_ref[...] = jnp.zeros_like(sum_ref)

        dv, yv = d_ref[...], y_ref[...]
        r = _rstd(yv)
        n = yv * r
        sum_ref[0:1, :] += jnp.sum(dv * (n * g_ref[...]), axis=0, keepdims=True)
        sum_ref[1:2, :] += jnp.sum((dv * gate_ref[...]) * n, axis=0, keepdims=True)
        dn = (dv * gate_ref[...]) * g_ref[...]
        dy_ref[...] = (r * (dn - n * jnp.mean(dn * n, axis=-1, keepdims=True))).astype(BF16)

    return pl.pallas_call(
        body, name=name, grid=(s // ROW_TILE,),
        in_specs=[_row_spec(D), _row_spec(D), _vec_spec(D), _vec_spec(D)],
        out_specs=[_row_spec(D), _sum_spec(D)],
        out_shape=[jax.ShapeDtypeStruct((s, D), BF16), jax.ShapeDtypeStruct((8, D), F32)],
        compiler_params=_params("arbitrary"),
    )(dxo, y, g, gate)


def _modnorm_bwd(dh, x, dxo, g, scale, name):
    s = dh.shape[0]

    def body(dh_ref, x_ref, d_ref, g_ref, sc_ref, dx_ref, sum_ref):
        @pl.when(pl.program_id(0) == 0)
        def _():
            sum_ref[...] = jnp.zeros_like(sum_ref)

        dhv, xv = dh_ref[...], x_ref[...]
        r = _rstd(xv)
        n = xv * r
        one_sc = 1.0 + sc_ref[...]
        sum_ref[0:1, :] += jnp.sum(dhv, axis=0, keepdims=True)
        sum_ref[1:2, :] += jnp.sum(dhv * (n * g_ref[...]), axis=0, keepdims=True)
        sum_ref[2:3, :] += jnp.sum((dhv * one_sc) * n, axis=0, keepdims=True)
        dn = (dhv * one_sc) * g_ref[...]
        dx_ref[...] = d_ref[...] + r * (dn - n * jnp.mean(dn * n, axis=-1, keepdims=True))

    return pl.pallas_call(
        body, name=name, grid=(s // ROW_TILE,),
        in_specs=[_row_spec(D), _row_spec(D), _row_spec(D), _vec_spec(D), _vec_spec(D)],
        out_specs=[_row_spec(D), _sum_spec(D)],
        out_shape=[jax.ShapeDtypeStruct((s, D), F32), jax.ShapeDtypeStruct((8, D), F32)],
        compiler_params=_params("arbitrary"),
    )(dh, x, dxo, g, scale)


def _loss_head(y, target):
    s = y.shape[0]

    def body(y_ref, t_ref, dy_ref, sum_ref):
        @pl.when(pl.program_id(0) == 0)
        def _():
            sum_ref[...] = jnp.zeros_like(sum_ref)

        err = y_ref[...] - t_ref[...]
        dy_ref[...] = err * (1.0 / D)
        sum_ref[...] += jnp.sum(err * err)

    return pl.pallas_call(
        body, name="loss_head", grid=(s // ROW_TILE,),
        in_specs=[_row_spec(D), _row_spec(D)],
        out_specs=[_row_spec(D), pl.BlockSpec((8, 128), lambda i: (0, 0))],
        out_shape=[jax.ShapeDtypeStruct((s, D), F32), jax.ShapeDtypeStruct((8, 128), F32)],
        compiler_params=_params("arbitrary"),
    )(y, target)


def _merge_fwd(z, pa, pb, pc):
    s = z.shape[0]

    def body(g0_ref, g1_ref, g2_ref, pa_ref, pb_ref, pc_ref, o_ref):
        o_ref[...] = (jax.nn.sigmoid(g0_ref[...]) * pa_ref[...] + jax.nn.sigmoid(g1_ref[...]) * pb_ref[...]
                      + jax.nn.sigmoid(g2_ref[...]) * pc_ref[...]).astype(BF16)

    return pl.pallas_call(
        body, name="merge_fwd", grid=(s // ROW_TILE,),
        in_specs=[_row_spec(D, 0), _row_spec(D, 1), _row_spec(D, 2), _row_spec(D), _row_spec(D), _row_spec(D)],
        out_specs=_row_spec(D), out_shape=jax.ShapeDtypeStruct((s, D), BF16),
        compiler_params=_params("parallel"),
    )(z, z, z, pa, pb, pc)


def _merge_bwd(dm, z, pa, pb, pc):
    s = z.shape[0]

    def body(dm_ref, g0_ref, g1_ref, g2_ref, pa_ref, pb_ref, pc_ref, dgl_ref, da_ref, db_ref, dc_ref):
        dmv = dm_ref[...]
        for i, (g_ref, p_ref, d_ref) in enumerate(((g0_ref, pa_ref, da_ref), (g1_ref, pb_ref, db_ref), (g2_ref, pc_ref, dc_ref))):
            gate = jax.nn.sigmoid(g_ref[...])
            dgl_ref[:, i * D:(i + 1) * D] = ((dmv * p_ref[...]) * (gate * (1.0 - gate))).astype(BF16)
            d_ref[...] = (dmv * gate).astype(BF16)

    return pl.pallas_call(
        body, name="merge_bwd", grid=(s // ROW_TILE,),
        in_specs=[_row_spec(D), _row_spec(D, 0), _row_spec(D, 1), _row_spec(D, 2), _row_spec(D), _row_spec(D), _row_spec(D)],
        out_specs=[_row_spec(3 * D), _row_spec(D), _row_spec(D), _row_spec(D)],
        out_shape=[jax.ShapeDtypeStruct((s, 3 * D), BF16)] + [jax.ShapeDtypeStruct((s, D), BF16)] * 3,
        compiler_params=_params("parallel"),
    )(dm, z, z, z, pa, pb, pc)


def _shift_down(v, n):
    row = lax.broadcasted_iota(jnp.int32, v.shape, 0)
    return jnp.where(row >= n, pltpu.roll(v, n, axis=0), 0.0)


def _shift_up(v, n):
    s = v.shape[0]
    row = lax.broadcasted_iota(jnp.int32, v.shape, 0)
    return jnp.where(row < s - n, pltpu.roll(v, s - n, axis=0), 0.0)


def _log_sigmoid(v):
    return jnp.minimum(v, 0.0) - jnp.log1p(jnp.exp(-jnp.abs(v)))


def _cumf_fwd(fl, bias):
    s = fl.shape[0]

    def body(fl_ref, b_ref, o_ref):
        acc = _log_sigmoid(fl_ref[...] + b_ref[...])
        step = 1
        while step < s:
            acc = acc + _shift_down(acc, step)
            step *= 2
        o_ref[...] = acc

    return pl.pallas_call(body, name="cumf_fwd", out_shape=jax.ShapeDtypeStruct((s, 128), F32),
                          compiler_params=pltpu.CompilerParams(vmem_limit_bytes=V7X_VMEM_LIMIT))(fl, bias)


def _cumf_bwd(dcum, fl, bias):
    s = fl.shape[0]

    def body(d_ref, fl_ref, b_ref, dfl_ref, db_ref):
        acc = d_ref[...]
        step = 1
        while step < s:
            acc = acc + _shift_up(acc, step)
            step *= 2
        dfl = acc * jax.nn.sigmoid(-(fl_ref[...] + b_ref[...]))
        dfl_ref[...] = dfl.astype(BF16)
        db_ref[...] = jnp.broadcast_to(jnp.sum(dfl, axis=0, keepdims=True), (8, 128))

    return pl.pallas_call(
        body, name="cumf_bwd",
        out_shape=[jax.ShapeDtypeStruct((s, 128), BF16), jax.ShapeDtypeStruct((8, 128), F32)],
        compiler_params=pltpu.CompilerParams(vmem_limit_bytes=V7X_VMEM_LIMIT))(dcum, fl, bias)


def _pool_windows(v, shift):
    s2 = v + shift(v, 1)
    s4 = s2 + shift(s2, 2)
    s8 = s4 + shift(s4, 4)
    s16 = s8 + shift(s8, 8)
    group = lax.broadcasted_iota(jnp.int32, v.shape, 1) // 64
    return jnp.where(group == 0, s2, jnp.where(group == 1, s4, jnp.where(group == 2, s8, s16)))


def _pool_count(shape):
    group = lax.broadcasted_iota(jnp.int32, shape, 1) // 64
    window = jnp.where(group == 0, 2.0, jnp.where(group == 1, 4.0, jnp.where(group == 2, 8.0, 16.0)))
    t1 = (lax.broadcasted_iota(jnp.int32, shape, 0) + 1).astype(F32)
    return jnp.minimum(t1, window)


def _pc_specs(s):
    zcol = lambda blk: pl.BlockSpec((s, 256), lambda i, blk=blk: (0, blk))
    first = Z_PC // 256
    return [zcol(first), zcol(first + 1), zcol(first + 2), zcol(first + 3),
            pl.BlockSpec((256, 256), lambda i: (0, 0)), pl.BlockSpec((1, 256), lambda i: (0, 0)),
            pl.BlockSpec((3, 256), lambda i: (0, 0))]


def _poolconv_fwd(z, wbd, pscale, convw):
    s = z.shape[0]

    def body(pu_ref, ch_ref, cb_ref, cc_ref, w_ref, ps_ref, cw_ref, yb_ref, yc_ref):
        u = pu_ref[...]
        p = _pool_windows(u, _shift_down) / _pool_count(u.shape) - u
        yb = jnp.dot(p.astype(BF16), w_ref[...].astype(BF16), preferred_element_type=F32) * ps_ref[...]
        yb_ref[...] = yb.astype(BF16)
        uc = cc_ref[...] * ch_ref[...]
        cw = cw_ref[...]
        conv = cw[0:1, :] * _shift_down(uc, 2) + cw[1:2, :] * _shift_down(uc, 1) + cw[2:3, :] * uc
        yc_ref[...] = (cb_ref[...] * conv).astype(BF16)

    out = pl.BlockSpec((s, 256), lambda i: (0, 0))
    return pl.pallas_call(
        body, name="poolconv_fwd", grid=(1,), in_specs=_pc_specs(s), out_specs=[out, out],
        out_shape=[jax.ShapeDtypeStruct((s, 256), BF16)] * 2, compiler_params=_params("arbitrary"),
    )(z, z, z, z, wbd, pscale, convw)


def _poolconv_bwd(dyb, dyc, z, wbd, pscale, convw):
    s = z.shape[0]

    def body(dyb_ref, dyc_ref, pu_ref, ch_ref, cb_ref, cc_ref, w_ref, ps_ref, cw_ref, dz_ref, dw_ref, dps_ref, dcw_ref):
        u = pu_ref[...]
        count = _pool_count(u.shape)
        p = (_pool_windows(u, _shift_down) / count - u).astype(BF16)
        wb = w_ref[...].astype(BF16)
        dyb_v = dyb_ref[...]
        pw = jnp.dot(p, wb, preferred_element_type=F32)
        dps_ref[...] = jnp.broadcast_to(jnp.sum(dyb_v * pw, axis=0, keepdims=True), (8, 256))
        dys = (dyb_v * ps_ref[...]).astype(BF16)
        dp = lax.dot_general(dys, wb, (((1,), (1,)), ((), ())), preferred_element_type=F32)
        dw_ref[...] = lax.dot_general(p, dys, (((0,), (0,)), ((), ())), preferred_element_type=F32)
        dz_ref[:, 0:256] = (_pool_windows(dp / count, _shift_up) - dp).astype(BF16)

        ch, cb, cc = ch_ref[...], cb_ref[...], cc_ref[...]
        uc = cc * ch
        cw = cw_ref[...]
        u2, u1 = _shift_down(uc, 2), _shift_down(uc, 1)
        conv = cw[0:1, :] * u2 + cw[1:2, :] * u1 + cw[2:3, :] * uc
        dyc_v = dyc_ref[...]
        dconv = dyc_v * cb
        du = cw[0:1, :] * _shift_up(dconv, 2) + cw[1:2, :] * _shift_up(dconv, 1) + cw[2:3, :] * dconv
        dz_ref[:, 256:512] = (du * cc).astype(BF16)
        dz_ref[:, 512:768] = (dyc_v * conv).astype(BF16)
        dz_ref[:, 768:1024] = (du * ch).astype(BF16)
        dcw_ref[...] = jnp.zeros_like(dcw_ref)
        dcw_ref[0:1, :] = jnp.sum(dconv * u2, axis=0, keepdims=True)
        dcw_ref[1:2, :] = jnp.sum(dconv * u1, axis=0, keepdims=True)
        dcw_ref[2:3, :] = jnp.sum(dconv * uc, axis=0, keepdims=True)

    blk = lambda r, c: pl.BlockSpec((r, c), lambda i: (0, 0))
    return pl.pallas_call(
        body, name="poolconv_bwd", grid=(1,),
        in_specs=[blk(s, 256), blk(s, 256)] + _pc_specs(s),
        out_specs=[blk(s, 1024), blk(256, 256), blk(8, 256), blk(8, 256)],
        out_shape=[jax.ShapeDtypeStruct((s, 1024), BF16), jax.ShapeDtypeStruct((256, 256), F32),
                   jax.ShapeDtypeStruct((8, 256), F32), jax.ShapeDtypeStruct((8, 256), F32)],
        compiler_params=_params("arbitrary"),
    )(dyb, dyc, z, z, z, z, wbd, pscale, convw)


_NT = (((1,), (1,)), ((), ()))
_TN = (((0,), (0,)), ((), ()))


def _att_logits(q, k, fc, fr, q0, k0):
    logits = lax.dot_general(q, k, _NT, preferred_element_type=F32) * (HEAD_DIM ** -0.5) + fc - fr
    row = q0 + lax.broadcasted_iota(jnp.int32, logits.shape, 0)
    col = k0 + lax.broadcasted_iota(jnp.int32, logits.shape, 1)
    return jnp.where(row >= col, logits, NEG_INF)


def _attn_fwd(q, k, v, fc, fr):
    h, s, dh = q.shape
    blk = ATT_BLOCK
    nb = s // blk

    def body(q_ref, k_ref, v_ref, fc_ref, fr_ref, o_ref, lse_ref):
        qi = pl.program_id(1)
        qv, fcv = q_ref[0], fc_ref[0]

        def step(j, carry):
            m, l, acc = carry
            k0 = pl.multiple_of(j * blk, blk)
            kv, vv = k_ref[0, pl.ds(k0, blk), :], v_ref[0, pl.ds(k0, blk), :]
            logits = _att_logits(qv, kv, fcv, fr_ref[0, pl.ds(j, 1), :], qi * blk, k0)
            m_new = jnp.maximum(m, jnp.max(logits, axis=-1, keepdims=True))
            p = jnp.exp(logits - m_new)
            alpha = jnp.exp(m - m_new)
            l = alpha * l + jnp.sum(p, axis=-1, keepdims=True)
            acc = alpha * acc + jnp.dot(p.astype(BF16), vv, preferred_element_type=F32)
            return m_new, l, acc

        init = (jnp.full((blk, 1), NEG_INF, F32), jnp.zeros((blk, 1), F32), jnp.zeros((blk, dh), F32))
        m, l, acc = lax.fori_loop(0, qi + 1, step, init)
        o_ref[0] = acc / l
        lse_ref[0] = m + jnp.log(l)

    qspec = pl.BlockSpec((1, blk, dh), lambda hh, i: (hh, i, 0))
    full = pl.BlockSpec((1, s, dh), lambda hh, i: (hh, 0, 0))
    cspec = pl.BlockSpec((1, blk, 1), lambda hh, i: (hh, i, 0))
    return pl.pallas_call(
        body, name="attn_fwd", grid=(h, nb),
        in_specs=[qspec, full, full, cspec, pl.BlockSpec((1, nb, blk), lambda hh, i: (hh, 0, 0))],
        out_specs=[qspec, cspec],
        out_shape=[jax.ShapeDtypeStruct((h, s, dh), F32), jax.ShapeDtypeStruct((h, s, 1), F32)],
        compiler_params=_params("parallel", "parallel"),
    )(q, k, v, fc, fr)


def _attn_bwd(q, k, v, do, o, lse, fc, fr):
    h, s, dh = q.shape
    blk = ATT_BLOCK
    nb = s // blk
    scale = HEAD_DIM ** -0.5

    def body(q_ref, k_ref, v_ref, do_ref, o_ref, lse_ref, fc_ref, fr_ref, dq_ref, dk_ref, dv_ref, dfc_ref, dfr_ref):
        dk_ref[...] = jnp.zeros_like(dk_ref)
        dv_ref[...] = jnp.zeros_like(dv_ref)
        dfr_ref[...] = jnp.zeros_like(dfr_ref)

        def outer(i, carry):
            q0 = pl.multiple_of(i * blk, blk)
            rows = pl.ds(q0, blk)
            qv, dov = q_ref[0, rows, :], do_ref[0, rows, :]
            delta = jnp.sum(dov * o_ref[0, rows, :], axis=-1, keepdims=True)
            dob = dov.astype(BF16)
            lsev, fcv = lse_ref[0, rows, :], fc_ref[0, rows, :]

            def inner(j, carry):
                dq, dfc = carry
                k0 = pl.multiple_of(j * blk, blk)
                cols = pl.ds(k0, blk)
                kv, vv = k_ref[0, cols, :], v_ref[0, cols, :]
                p = jnp.exp(_att_logits(qv, kv, fcv, fr_ref[0, pl.ds(j, 1), :], q0, k0) - lsev)
                dp = lax.dot_general(dob, vv, _NT, preferred_element_type=F32)
                ds = p * (dp - delta)
                dsb = ds.astype(BF16)
                dk_ref[0, cols, :] += lax.dot_general(dsb, qv, _TN, preferred_element_type=F32)
                dv_ref[0, cols, :] += lax.dot_general(p.astype(BF16), dob, _TN, preferred_element_type=F32)
                dfr_ref[0, pl.ds(j, 1), :] -= jnp.sum(ds, axis=0, keepdims=True)
                return dq + jnp.dot(dsb, kv, preferred_element_type=F32), dfc + jnp.sum(ds, axis=-1, keepdims=True)

            dq, dfc = lax.fori_loop(0, i + 1, inner, (jnp.zeros((blk, dh), F32), jnp.zeros((blk, 1), F32)))
            dq_ref[0, rows, :] = dq * scale
            dfc_ref[0, rows, :] = dfc
            return carry

        lax.fori_loop(0, nb, outer, 0)
        dk_ref[...] = dk_ref[...] * scale

    full = pl.BlockSpec((1, s, dh), lambda hh: (hh, 0, 0))
    col = pl.BlockSpec((1, s, 1), lambda hh: (hh, 0, 0))
    rowv = pl.BlockSpec((1, nb, blk), lambda hh: (hh, 0, 0))
    return pl.pallas_call(
        body, name="attn_bwd", grid=(h,),
        in_specs=[full, full, full, full, full, col, col, rowv],
        out_specs=[full, full, full, col, rowv],
        out_shape=[jax.ShapeDtypeStruct((h, s, dh), F32)] * 3 + [jax.ShapeDtypeStruct((h, s, 1), F32), jax.ShapeDtypeStruct((h, nb, blk), F32)],
        compiler_params=_params("parallel"),
    )(q, k, v, do, o, lse, fc, fr)


def _ada_fwd(c_all, w_ada, b_loc):
    depth, _, n = w_ada.shape
    tn = 512

    def body(c_ref, w_ref, b_ref, o_ref, sc_ref):
        cv = c_ref[...]
        sc = cv * jax.nn.sigmoid(cv)
        sc_ref[...] = sc
        o_ref[0] = jnp.dot(sc.astype(BF16), w_ref[0].astype(BF16), preferred_element_type=F32) + b_ref[0]

    return pl.pallas_call(
        body, name="ada_fwd", grid=(depth, n // tn),
        in_specs=[pl.BlockSpec((N_DEV, D), lambda l, j: (0, 0)), pl.BlockSpec((1, D, tn), lambda l, j: (l, 0, j)),
                  pl.BlockSpec((1, 1, tn), lambda l, j: (l, 0, j))],
        out_specs=[pl.BlockSpec((1, N_DEV, tn), lambda l, j: (l, 0, j)), pl.BlockSpec((N_DEV, D), lambda l, j: (0, 0))],
        out_shape=[jax.ShapeDtypeStruct((depth, N_DEV, n), F32), jax.ShapeDtypeStruct((N_DEV, D), F32)],
        compiler_params=_params("arbitrary", "arbitrary"),
    )(c_all, w_ada, b_loc)


def _sum_devices(gathered):
    n = gathered.shape[1]
    tn = _pick(n, (1408, 1024, 640, 512, 128))

    def body(g_ref, o_ref):
        acc = g_ref[0:8, :]
        for dev in range(1, N_DEV):
            acc = acc + g_ref[8 * dev:8 * dev + 8, :]
        o_ref[...] = acc

    return pl.pallas_call(
        body, name="sum_devices", grid=(n // tn,),
        in_specs=[pl.BlockSpec((8 * N_DEV, tn), lambda j: (0, j))], out_specs=pl.BlockSpec((8, tn), lambda j: (0, j)),
        out_shape=jax.ShapeDtypeStruct((8, n), F32), compiler_params=_params("parallel"),
    )(gathered)


def _place():
    x, y, c = lax.axis_index("x"), lax.axis_index("y"), lax.axis_index("c")
    chips = [(1 - x, y), (x, 1 - y), (1 - x, 1 - y)]
    return x, y, c, chips


def _allgather8(block, name):
    m_per, n = block.shape

    def body(x_ref, out_ref, send_sems, recv_sems, local_sem):
        x, y, c, chips = _place()
        me, sibling = (x, y, c), (x, y, 1 - c)

        def rows(px, py, pc):
            return out_ref.at[pl.ds((4 * px + 2 * py + pc) * m_per, m_per), :]

        def copy(k, blk, to, src=None):
            return pltpu.make_async_remote_copy(
                src_ref=rows(*blk) if src is None else src, dst_ref=rows(*blk),
                send_sem=send_sems.at[k], recv_sem=recv_sems.at[k], device_id=to, device_id_type=MESH)

        mine = pltpu.make_async_copy(x_ref, rows(*me), local_sem)
        mine.start()
        first = [copy(0, me, sibling, src=x_ref)]
        first += [copy(1 + j, me, (*chip, c), src=x_ref) for j, chip in enumerate(chips)]
        for cp in first:
            cp.start()
        passed = [copy(4 + j, (*chip, c), sibling) for j, chip in enumerate(chips)]
        for j, chip in enumerate(chips):
            copy(1 + j, (*chip, c), me).wait_recv()
            passed[j].start()
        copy(0, sibling, me).wait_recv()
        for j, chip in enumerate(chips):
            copy(4 + j, (*chip, 1 - c), me).wait_recv()
        for cp in first + passed:
            cp.wait_send()
        mine.wait()

    return pl.pallas_call(
        body, name=name, out_shape=jax.ShapeDtypeStruct((N_DEV * m_per, n), block.dtype),
        in_specs=[pl.BlockSpec(memory_space=pltpu.VMEM)], out_specs=pl.BlockSpec(memory_space=pltpu.VMEM),
        scratch_shapes=[pltpu.SemaphoreType.DMA((7,)), pltpu.SemaphoreType.DMA((7,)), pltpu.SemaphoreType.DMA],
        compiler_params=pltpu.CompilerParams(vmem_limit_bytes=V7X_VMEM_LIMIT),
    )(block)


def _gather_weights(shards):
    n = len(shards)

    def body(*refs):
        ins, outs = refs[:n], refs[n:2 * n]
        send_sems, recv_sems = refs[2 * n:]
        x, y, c, chips = _place()
        j = 2 * x + y
        sibling = (x, y, 1 - c)
        chip_ids = [2 * px + py for px, py in chips]

        def remote(a, slot, src, dst, to):
            return pltpu.make_async_remote_copy(src_ref=src, dst_ref=dst, send_sem=send_sems.at[6 * a + slot],
                                                recv_sem=recv_sems.at[6 * a + slot], device_id=to, device_id_type=MESH)

        sends = [remote(a, k, ins[a].at[c], outs[a].at[j, c], (*chips[k], c)) for a in range(n) for k in range(3)]
        for cp in sends:
            cp.start()
        passed = []
        for a in range(n):
            for k in range(3):
                landed = outs[a].at[chip_ids[k], c]
                remote(a, k, landed, landed, sibling).wait_recv()
                fwd = remote(a, 3 + k, landed, landed, sibling)
                fwd.start()
                passed.append(fwd)
        for a in range(n):
            for k in range(3):
                from_sibling = outs[a].at[chip_ids[k], 1 - c]
                remote(a, 3 + k, from_sibling, from_sibling, sibling).wait_recv()
        for cp in sends + passed:
            cp.wait_send()

    return pl.pallas_call(
        body, name="gather_weights",
        out_shape=[jax.ShapeDtypeStruct((N_CHIPS,) + s.shape, s.dtype) for s in shards],
        in_specs=[_HBM] * n, out_specs=[_HBM] * n,
        scratch_shapes=[pltpu.SemaphoreType.DMA((6 * n,)), pltpu.SemaphoreType.DMA((6 * n,))],
    )(*shards)


def _send_sibling_layer(grads):
    n = len(grads)
    flat = [g for pair in grads for g in pair]

    def body(*refs):
        ins, outs = refs[:2 * n], refs[2 * n:3 * n]
        send_sems, recv_sems = refs[3 * n:]
        x, y, c, _ = _place()
        sibling = (x, y, 1 - c)
        for mine in (0, 1):
            @pl.when(c == mine)
            def _():
                cps = [pltpu.make_async_remote_copy(src_ref=ins[2 * a + (1 - mine)], dst_ref=outs[a], send_sem=send_sems.at[a],
                                                    recv_sem=recv_sems.at[a], device_id=sibling, device_id_type=MESH)
                       for a in range(n)]
                for cp in cps:
                    cp.start()
                for cp in cps:
                    cp.wait()

    return pl.pallas_call(
        body, name="rs_sibling", out_shape=[jax.ShapeDtypeStruct(pair[0].shape, pair[0].dtype) for pair in grads],
        in_specs=[_HBM] * (2 * n), out_specs=[_HBM] * n,
        scratch_shapes=[pltpu.SemaphoreType.DMA((n,)), pltpu.SemaphoreType.DMA((n,))],
    )(*flat)


def _chip_sum(g0, g1, other, sel, name):
    _, r, cdim = g0.shape
    tr = _pick(r, (256, 128))

    def body(sel_ref, g0_ref, g1_ref, t_ref, wire_ref, own_ref):
        p = pl.program_id(2)
        total = jnp.where(sel_ref[0] == 0, g0_ref[0], g1_ref[0]) + t_ref[0]
        wire_ref[0] = total.astype(BF16)

        @pl.when(p == sel_ref[1])
        def _():
            own_ref[...] = total

    def pick(layer):
        def index(i, jj, p, sel_ref):
            use = jnp.where(sel_ref[0] == layer, 1, 0)
            return (p * use, i * use, 0)
        return pl.BlockSpec((1, tr, cdim), index)

    blk = pl.BlockSpec((1, tr, cdim), lambda i, jj, p, sel_ref: (p, i, 0))
    return pl.pallas_call(
        body, name=name,
        grid_spec=pltpu.PrefetchScalarGridSpec(
            num_scalar_prefetch=1, grid=(r // tr, 1, N_CHIPS),
            in_specs=[pick(0), pick(1), blk], out_specs=[blk, pl.BlockSpec((tr, cdim), lambda i, jj, p, sel_ref: (i, 0))]),
        out_shape=[jax.ShapeDtypeStruct(g0.shape, BF16), jax.ShapeDtypeStruct((r, cdim), F32)],
        compiler_params=_params("parallel", "arbitrary", "arbitrary"),
    )(sel, g0, g1, other)


def _exchange_chips(wires):
    n = len(wires)

    def body(*refs):
        ins, outs = refs[:n], refs[n:2 * n]
        send_sems, recv_sems = refs[2 * n:]
        x, y, c, chips = _place()
        cps = [pltpu.make_async_remote_copy(src_ref=ins[a].at[2 * chips[k][0] + chips[k][1]], dst_ref=outs[a].at[k],
                                            send_sem=send_sems.at[3 * a + k], recv_sem=recv_sems.at[3 * a + k],
                                            device_id=(*chips[k], c), device_id_type=MESH)
               for a in range(n) for k in range(3)]
        for cp in cps:
            cp.start()
        for cp in cps:
            cp.wait()

    return pl.pallas_call(
        body, name="rs_chips", out_shape=[jax.ShapeDtypeStruct((3,) + w.shape[1:], w.dtype) for w in wires],
        in_specs=[_HBM] * n, out_specs=[_HBM] * n,
        scratch_shapes=[pltpu.SemaphoreType.DMA((3 * n,)), pltpu.SemaphoreType.DMA((3 * n,))],
    )(*wires)


def _final_sum(own, recv, name):
    r, cdim = own.shape
    tr = _pick(r, (256, 128))

    def body(own_ref, r0_ref, r1_ref, r2_ref, o_ref):
        o_ref[...] = ((own_ref[...] + r0_ref[0].astype(F32)) + r1_ref[0].astype(F32)) + r2_ref[0].astype(F32)

    part = lambda k: pl.BlockSpec((1, tr, cdim), lambda i, k=k: (k, i, 0))
    spec = pl.BlockSpec((tr, cdim), lambda i: (i, 0))
    return pl.pallas_call(
        body, name=name, grid=(r // tr,), in_specs=[spec, part(0), part(1), part(2)], out_specs=spec,
        out_shape=jax.ShapeDtypeStruct((r, cdim), F32), compiler_params=_params("parallel"),
    )(own, recv, recv, recv)


def _share_layers(sums):
    n = len(sums)

    def body(*refs):
        ins, outs = refs[:n], refs[n:2 * n]
        send_sems, recv_sems = refs[2 * n:]
        x, y, c, _ = _place()
        cps = [pltpu.make_async_remote_copy(src_ref=ins[a], dst_ref=outs[a], send_sem=send_sems.at[a], recv_sem=recv_sems.at[a],
                                            device_id=(x, y, 1 - c), device_id_type=MESH) for a in range(n)]
        for cp in cps:
            cp.start()
        for cp in cps:
            cp.wait()

    return pl.pallas_call(
        body, name="rs_share", out_shape=[jax.ShapeDtypeStruct(s.shape, s.dtype) for s in sums],
        in_specs=[_HBM] * n, out_specs=[_HBM] * n,
        scratch_shapes=[pltpu.SemaphoreType.DMA((n,)), pltpu.SemaphoreType.DMA((n,))],
    )(*sums)


def _row(v):
    return v.reshape(1, -1)


def _layer_fwd(x, w, mod):
    s = x.shape[0]
    nb = s // ATT_BLOCK
    h = _modnorm_fwd(x, _row(w["g_mix_pre"]), mod[0:1], mod[1:2], "mix_pre_fwd")
    z = _mm(h, w["w_all"], name="mm_in")
    qkv = z[:, Z_QKV:Z_PC].astype(BF16).reshape(s, 3, HEADS, HEAD_DIM).transpose(1, 2, 0, 3)
    fl = z[:, Z_FL:Z_COLS]
    cum = _cumf_fwd(fl, w["b_f_pad"])
    f_heads = cum[:, :HEADS].T
    fc, fr = f_heads[:, :, None], f_heads.reshape(HEADS, nb, ATT_BLOCK)
    o, lse = _attn_fwd(qkv[0], qkv[1], qkv[2], fc, fr)
    br_a = o.transpose(1, 0, 2).reshape(s, A_WIDTH).astype(BF16)
    br_b, br_c = _poolconv_fwd(z, w["w_pool_bd"], _row(w["pool_scale"]), w["conv_w"])
    wbr = w["w_branch"]
    pa = _mm(br_a, wbr[:A_WIDTH], name="mm_br_a")
    pb = _mm(br_b, wbr[A_WIDTH:A_WIDTH + POOL_WIDTH], name="mm_br_b")
    pc = _mm(br_c, wbr[A_WIDTH + POOL_WIDTH:], name="mm_br_c")
    merged = _merge_fwd(z, pa, pb, pc)
    y = _mm(merged, w["w_out"], name="mm_out")
    x1 = _post_fwd(x, y, _row(w["g_mix_post"]), mod[2:3], "mix_post_fwd")
    h2 = _modnorm_fwd(x1, _row(w["g_ff_pre"]), mod[3:4], mod[4:5], "ff_pre_fwd")
    a = _mm(h2, w["w_ff1"], name="mm_ff1")
    r = _act_fwd(a)
    y2 = _mm(r, w["w_ff2"], name="mm_ff2")
    x2 = _post_fwd(x1, y2, _row(w["g_ff_post"]), mod[5:6], "ff_post_fwd")
    saved = dict(x=x, h=h, z=z, qkv=qkv, fl=fl, fc=fc, fr=fr, o=o, lse=lse, br_a=br_a, br_b=br_b, br_c=br_c, pa=pa, pb=pb, pc=pc,
                 merged=merged, y=y, x1=x1, h2=h2, a=a, r=r, y2=y2)
    return x2, saved


def _layer_bwd(dx2, sv, w, mod):
    s = dx2.shape[0]
    dy2, sum_ff_post = _post_bwd(dx2, sv["y2"], _row(w["g_ff_post"]), mod[5:6], "ff_post_bwd")
    dr = _mm(dy2, w["w_ff2"], tb=True, name="mm_ff2_dx")
    d_w_ff2 = _mm(sv["r"], dy2, ta=True, name="mm_ff2_dw")
    da = _act_bwd(dr, sv["a"])
    dh2 = _mm(da, w["w_ff1"], tb=True, name="mm_ff1_dx")
    d_w_ff1 = _mm(sv["h2"], da, ta=True, name="mm_ff1_dw")
    dx1, sum_ff_pre = _modnorm_bwd(dh2, sv["x1"], dx2, _row(w["g_ff_pre"]), mod[4:5], "ff_pre_bwd")

    dy, sum_mix_post = _post_bwd(dx1, sv["y"], _row(w["g_mix_post"]), mod[2:3], "mix_post_bwd")
    dmerged = _mm(dy, w["w_out"], tb=True, name="mm_out_dx")
    d_w_out = _mm(sv["merged"], dy, ta=True, name="mm_out_dw")
    dgl, dpa, dpb, dpc = _merge_bwd(dmerged, sv["z"], sv["pa"], sv["pb"], sv["pc"])
    wbr = w["w_branch"]
    dbr_a = _mm(dpa, wbr[:A_WIDTH], tb=True, name="mm_br_a_dx")
    dbr_b = _mm(dpb, wbr[A_WIDTH:A_WIDTH + POOL_WIDTH], tb=True, name="mm_br_b_dx")
    dbr_c = _mm(dpc, wbr[A_WIDTH + POOL_WIDTH:], tb=True, name="mm_br_c_dx")
    d_w_branch = jnp.concatenate([_mm(sv["br_a"], dpa, ta=True, name="mm_br_a_dw"), _mm(sv["br_b"], dpb, ta=True, name="mm_br_b_dw"),
                                  _mm(sv["br_c"], dpc, ta=True, name="mm_br_c_dw")], axis=0)

    do = dbr_a.reshape(s, HEADS, HEAD_DIM).transpose(1, 0, 2)
    qkv = sv["qkv"]
    dq, dk, dv, dfc, dfr = _attn_bwd(qkv[0], qkv[1], qkv[2], do, sv["o"], sv["lse"], sv["fc"], sv["fr"])
    dcum = jnp.pad((dfc[:, :, 0] + dfr.reshape(HEADS, s)).T, ((0, 0), (0, 128 - HEADS)))
    dfl, sum_bf = _cumf_bwd(dcum, sv["fl"], w["b_f_pad"])
    dpc_z, d_wbd, sum_ps, sum_cw = _poolconv_bwd(dbr_b, dbr_c, sv["z"], w["w_pool_bd"], _row(w["pool_scale"]), w["conv_w"])
    dqkv = jnp.stack([dq, dk, dv]).transpose(2, 0, 1, 3).reshape(s, 3 * A_WIDTH).astype(BF16)
    dz = jnp.concatenate([dgl, dqkv, dpc_z, dfl], axis=1)
    dh = _mm(dz, w["w_all"], tb=True, name="mm_in_dx")
    d_w_all = _mm(sv["h"], dz, ta=True, name="mm_in_dw")
    dx, sum_mix_pre = _modnorm_bwd(dh, sv["x"], dx1, _row(w["g_mix_pre"]), mod[1:2], "mix_pre_bwd")

    dmod = jnp.stack([sum_mix_pre[0], sum_mix_pre[1], sum_mix_post[0], sum_ff_pre[0], sum_ff_pre[1], sum_ff_post[0]])
    d_w_in = jnp.concatenate([d_w_all[:, Z_QKV:Z_PC], d_w_all[:, Z_FL:Z_FL + HEADS], d_w_all[:, Z_PC:Z_FL], d_w_all[:, Z_GL:Z_QKV]], axis=1)
    d_w_pool = jnp.stack([d_wbd[64 * g:64 * g + 64, 64 * g:64 * g + 64] for g in range(4)])
    big = dict(w_in=d_w_in, w_branch=d_w_branch, w_out=d_w_out, w_ff1=d_w_ff1, w_ff2=d_w_ff2)
    small = dict(g_mix_pre=sum_mix_pre[2], g_mix_post=sum_mix_post[1], g_ff_pre=sum_ff_pre[2], g_ff_post=sum_ff_post[1],
                 b_f=sum_bf[0, :HEADS], w_pool=d_w_pool, pool_scale=sum_ps[0], conv_w=sum_cw[0:3])
    return dx, dmod, big, small


def _full_layer_weights(w_in_nat, w_branch, w_out, w_ff1, w_ff2, g_mix_pre, g_mix_post, g_ff_pre, g_ff_post, b_f, w_pool, pool_scale, conv_w):
    q_end, f_end, pc_end = 3 * A_WIDTH, 3 * A_WIDTH + HEADS, 3 * A_WIDTH + HEADS + POOL_WIDTH + 3 * CONV_WIDTH
    w_all = jnp.concatenate([w_in_nat[:, pc_end:], w_in_nat[:, :q_end], w_in_nat[:, f_end:pc_end], w_in_nat[:, q_end:f_end],
                             jnp.zeros((D, Z_COLS - Z_FL - HEADS), w_in_nat.dtype)], axis=1)
    wbd = jnp.zeros((POOL_WIDTH, POOL_WIDTH), F32)
    for g in range(4):
        wbd = wbd.at[64 * g:64 * g + 64, 64 * g:64 * g + 64].set(w_pool[g])
    return dict(w_all=w_all, w_branch=w_branch, w_out=w_out, w_ff1=w_ff1, w_ff2=w_ff2, g_mix_pre=g_mix_pre, g_mix_post=g_mix_post,
                g_ff_pre=g_ff_pre, g_ff_post=g_ff_post, b_f_pad=jnp.pad(b_f, (0, 128 - HEADS)).reshape(1, 128), w_pool_bd=wbd,
                pool_scale=pool_scale, conv_w=conv_w)


def _local_step(x, target, mods, layers):
    saved = []
    act = x
    for l in range(DEPTH):
        act, sv = _layer_fwd(act, layers[l], mods[l])
        saved.append(sv)
    dact, sq = _loss_head(act, target)
    loss = sq[0, 0] * (0.5 / D)
    dmods, bigs, smalls = [None] * DEPTH, [None] * DEPTH, [None] * DEPTH
    for l in reversed(range(DEPTH)):
        dact, dmods[l], bigs[l], smalls[l] = _layer_bwd(dact, saved[l], layers[l], mods[l])
    return loss, dact, jnp.stack(dmods), bigs, smalls


_SMALL = ("g_mix_pre", "g_mix_post", "g_ff_pre", "g_ff_post", "b_f", "w_pool", "pool_scale", "conv_w")
_BIG = ("w_in", "w_branch", "w_out", "w_ff1", "w_ff2")


def _pack(parts, rows=8):
    flat = jnp.concatenate([p.reshape(-1) for p in parts])
    width = -(-flat.shape[0] // (rows * 128)) * 128
    return jnp.pad(flat, (0, rows * width - flat.shape[0])).reshape(rows, width)


def _unpack(packed, like):
    flat = packed.reshape(-1)
    out, at = [], 0
    for ref in like:
        out.append(flat[at:at + ref.size].reshape(ref.shape))
        at += ref.size
    return out


def kernel(x, c, w_ada, b_ada, g_mix_pre, g_mix_post, g_ff_pre, g_ff_post, w_in, b_f, w_pool, pool_scale, conv_w, w_branch, w_out, w_ff1, w_ff2, loss_target, m_w_ada, m_b_ada, m_g_mix_pre, m_g_mix_post, m_g_ff_pre, m_g_ff_post, m_w_in, m_b_f, m_w_pool, m_pool_scale, m_conv_w, m_w_branch, m_w_out, m_w_ff1, m_w_ff2, v_w_ada, v_b_ada, v_g_mix_pre, v_g_mix_post, v_g_ff_pre, v_g_ff_post, v_w_in, v_b_f, v_w_pool, v_pool_scale, v_conv_w, v_w_branch, v_w_out, v_w_ff1, v_w_ff2):
    xi, yi, ci = lax.axis_index("x"), lax.axis_index("y"), lax.axis_index("c")
    chip = 2 * xi + yi
    dev = 2 * chip + ci
    n_ada = w_ada.shape[2]

    own = [w.astype(BF16) for w in (w_in, w_branch, w_out, w_ff1, w_ff2)]
    g_in, g_br, g_out, g_f1, g_f2 = [lax.dynamic_update_index_in_dim(g, o, chip, 0) for g, o in zip(_gather_weights(own), own)]
    full_in = g_in.transpose(1, 2, 0, 3).reshape(DEPTH, D, IN_COLS)
    full_br = g_br.transpose(1, 0, 2, 3).reshape(DEPTH, D, D)
    full_out = g_out.transpose(1, 0, 2, 3).reshape(DEPTH, D, D)
    full_f1 = g_f1.transpose(1, 2, 0, 3).reshape(DEPTH, D, D_FF)
    full_f2 = g_f2.transpose(1, 0, 2, 3).reshape(DEPTH, D_FF, D)

    first = jnp.zeros((8, D + 384), F32).at[0, :D].set(c[0]).at[0, D:].set(conv_w.reshape(-1))
    got = _allgather8(first, "gather_cond").reshape(N_DEV, 8, D + 384)[:, 0]
    c_all = got[:, :D]
    conv_full = got[0::2, D:].reshape(N_CHIPS, DEPTH, 3, CONV_WIDTH // N_CHIPS).transpose(1, 2, 0, 3).reshape(DEPTH, 3, CONV_WIDTH)

    b_loc = lax.dynamic_slice_in_dim(b_ada, chip * n_ada, n_ada, axis=1).reshape(DEPTH, 1, n_ada)
    mod_cols, silu_c = _ada_fwd(c_all, w_ada, b_loc)
    got = _allgather8(mod_cols.reshape(DEPTH * N_DEV, n_ada), "gather_mod").reshape(N_DEV, DEPTH, N_DEV, n_ada)[0::2]
    mod_all = got.transpose(1, 2, 0, 3).reshape(DEPTH, N_DEV, 6, D)
    mods = lax.dynamic_index_in_dim(mod_all, dev, axis=1, keepdims=False)

    layers = [_full_layer_weights(full_in[l], full_br[l], full_out[l], full_f1[l], full_f2[l], g_mix_pre[l], g_mix_post[l], g_ff_pre[l],
                                  g_ff_post[l], b_f[l], w_pool[l], pool_scale[l], conv_full[l]) for l in range(DEPTH)]
    loss_part, grad_x, dmods, bigs, smalls = _local_step(x[0], loss_target[0], mods, layers)
    loss = lax.psum(loss_part, ("x", "y", "c"))

    small_parts = [smalls[l][name] for name in _SMALL for l in range(DEPTH)]
    packed = _pack([dmods] + small_parts)
    gathered = _allgather8(packed, "gather_small")
    dmod_all = gathered.reshape(N_DEV, -1)[:, :dmods.size].reshape(N_DEV, DEPTH, 6 * D)
    summed = _unpack(_sum_devices(gathered), [dmods] + small_parts)
    grad_b_ada = summed[0].reshape(DEPTH, 6 * D)
    small_grads = {name: jnp.stack(summed[1 + 2 * i:3 + 2 * i]) for i, name in enumerate(_SMALL)}
    small_grads["conv_w"] = lax.dynamic_slice_in_dim(small_grads["conv_w"], chip * (CONV_WIDTH // N_CHIPS), CONV_WIDTH // N_CHIPS, axis=2)

    dmod_loc = lax.dynamic_slice_in_dim(dmod_all.transpose(1, 0, 2), chip * n_ada, n_ada, axis=2)
    silu_pad = jnp.pad(silu_c, ((0, 128 - N_DEV), (0, 0)))
    grad_w_ada = jnp.stack([_mm(silu_pad, jnp.pad(dmod_loc[l], ((0, 128 - N_DEV), (0, 0))), ta=True, name="mm_ada_dw") for l in range(DEPTH)])

    def shard_blocks(name, g):
        if name == "w_in":
            return g.reshape(D, N_CHIPS, IN_COLS // N_CHIPS).transpose(1, 0, 2)
        if name == "w_ff1":
            return g.reshape(D, N_CHIPS, D_FF // N_CHIPS).transpose(1, 0, 2)
        return g.reshape(N_CHIPS, g.shape[0] // N_CHIPS, g.shape[1])

    pairs = [tuple(shard_blocks(name, bigs[l][name]) for l in range(DEPTH)) for name in _BIG]
    others = _send_sibling_layer(pairs)
    sel = jnp.stack([ci, chip]).astype(jnp.int32)
    wires, owns = zip(*[_chip_sum(pairs[i][0], pairs[i][1], others[i], sel, "rs_chip_sum_" + name) for i, name in enumerate(_BIG)])
    recvs = _exchange_chips(list(wires))
    sums = [_final_sum(owns[i], recvs[i], "rs_final_" + name) for i, name in enumerate(_BIG)]
    big_grads = {name: jnp.where(ci == 0, jnp.stack([mine, theirs]), jnp.stack([theirs, mine]))
                 for name, mine, theirs in zip(_BIG, sums, _share_layers(sums))}

    grads = dict(w_ada=grad_w_ada, b_ada=grad_b_ada, **small_grads, **big_grads)
    weights = dict(w_ada=w_ada, b_ada=b_ada, g_mix_pre=g_mix_pre, g_mix_post=g_mix_post, g_ff_pre=g_ff_pre, g_ff_post=g_ff_post, w_in=w_in,
                   b_f=b_f, w_pool=w_pool, pool_scale=pool_scale, conv_w=conv_w, w_branch=w_branch, w_out=w_out, w_ff1=w_ff1, w_ff2=w_ff2)
    m_in = dict(w_ada=m_w_ada, b_ada=m_b_ada, g_mix_pre=m_g_mix_pre, g_mix_post=m_g_mix_post, g_ff_pre=m_g_ff_pre, g_ff_post=m_g_ff_post,
                w_in=m_w_in, b_f=m_b_f, w_pool=m_w_pool, pool_scale=m_pool_scale, conv_w=m_conv_w, w_branch=m_w_branch, w_out=m_w_out,
                w_ff1=m_w_ff1, w_ff2=m_w_ff2)
    v_in = dict(w_ada=v_w_ada, b_ada=v_b_ada, g_mix_pre=v_g_mix_pre, g_mix_post=v_g_mix_post, g_ff_pre=v_g_ff_pre, g_ff_post=v_g_ff_post,
                w_in=v_w_in, b_f=v_b_f, w_pool=v_w_pool, pool_scale=v_pool_scale, conv_w=v_conv_w, w_branch=v_w_branch, w_out=v_w_out,
                w_ff1=v_w_ff1, w_ff2=v_w_ff2)
    order = ("w_ada", "b_ada", "g_mix_pre", "g_mix_post", "g_ff_pre", "g_ff_post", "w_in", "b_f", "w_pool", "pool_scale", "conv_w",
             "w_branch", "w_out", "w_ff1", "w_ff2")
    delta, new_m, new_v = {}, {}, {}
    for name in ("w_ada",) + _BIG:
        shape = weights[name].shape
        flat = lambda t: t.reshape(shape[0] * shape[1], shape[2])
        res = _adamw(flat(weights[name]), flat(grads[name]), flat(m_in[name]), flat(v_in[name]), "adamw_" + name)
        delta[name], new_m[name], new_v[name] = [t.reshape(shape) for t in res]
    tiny = ("b_ada",) + _SMALL
    res = _adamw(*[_pack([src[name] for name in tiny]) for src in (weights, grads, m_in, v_in)], "adamw_small")
    for out, packed_res in zip((delta, new_m, new_v), res):
        for name, val in zip(tiny, _unpack(packed_res, [weights[name] for name in tiny])):
            out[name] = val

    return (loss, grad_x[None], *[grads[n] for n in order], *[delta[n] for n in order], *[new_m[n] for n in order],
            *[new_v[n] for n in order])
```

```python
import functools

import jax
import jax.numpy as jnp
from jax import lax
from jax.experimental import pallas as pl
from jax.experimental.pallas import tpu as pltpu

F32 = jnp.float32
BF16 = jnp.bfloat16
MESH = pl.DeviceIdType.MESH

D = 1024
DEPTH = 2
HEADS = 8
HEAD_DIM = 64
A_WIDTH = 512
POOL_WIDTH = 256
CONV_WIDTH = 256
D_FF = 4096
IN_COLS = 5640
Z_GL, Z_QKV, Z_PC, Z_FL, Z_COLS = 0, 3072, 4608, 5632, 5760
RMS_EPS = 1e-6
NEG_INF = -1e30
ATT_BLOCK = 256
ROW_TILE = 256
N_CHIPS = 4
N_DEV = 8
V7X_VMEM_LIMIT = 48 * 1024 * 1024

ADAM_LR = 0.001
ADAM_B1 = 0.9
ADAM_B2 = 0.999
ADAM_EPS = 1e-08
ADAM_WD = 0.01
ADAM_STEP = 10

_HBM = pl.BlockSpec(memory_space=pltpu.HBM)


def _params(*sem):
    return pltpu.CompilerParams(dimension_semantics=sem, vmem_limit_bytes=V7X_VMEM_LIMIT)


def _pick(dim, cands):
    for cand in cands:
        if dim % cand == 0:
            return cand
    return dim


def _mm(a, b, *, ta=False, tb=False, b_split=1, out_split=1, out_dtype=F32, name):
    (k, m) = a.shape if ta else a.shape[::-1]
    b_rows, b_cols = b.shape[-2], b.shape[-1] * b_split
    (n, k2) = (b_rows, b_cols) if tb else (b_cols, b_rows)
    assert k == k2, (a.shape, b.shape, ta, tb)
    n_unit = n // (out_split * (1 if tb else b_split))
    k_unit = k // (b_split if tb else 1)
    tm = _pick(m, (1024, 512, 256, 128))
    tn = _pick(n_unit, (1024, 1152, 768, 640, 512, 256, 128))
    tk = _pick(k_unit, (1024, 1152, 512, 640, 256, 128))
    nk = k // tk
    dims = (((0 if ta else 1,), (1 if tb else 0,)), ((), ()))

    def dot(a_ref, b_ref):
        b_val = b_ref[0] if b_split > 1 else b_ref[...]
        return lax.dot_general(a_ref[...].astype(BF16), b_val.astype(BF16), dims, preferred_element_type=F32)

    def put(o_ref, val):
        if out_split > 1:
            o_ref[0] = val.astype(o_ref.dtype)
        else:
            o_ref[...] = val.astype(o_ref.dtype)

    def body_single(a_ref, b_ref, o_ref):
        put(o_ref, dot(a_ref, b_ref))

    def body_acc(a_ref, b_ref, o_ref, acc_ref):
        kk = pl.program_id(2)

        @pl.when(kk == 0)
        def _():
            acc_ref[...] = jnp.zeros_like(acc_ref)

        acc_ref[...] += dot(a_ref, b_ref)

        @pl.when(kk == nk - 1)
        def _():
            put(o_ref, acc_ref[...])

    a_spec = pl.BlockSpec((tk, tm), lambda i, j, kk: (kk, i)) if ta else pl.BlockSpec((tm, tk), lambda i, j, kk: (i, kk))
    if b_split == 1:
        b_spec = pl.BlockSpec((tn, tk), lambda i, j, kk: (j, kk)) if tb else pl.BlockSpec((tk, tn), lambda i, j, kk: (kk, j))
    elif tb:
        per = k_unit // tk
        b_spec = pl.BlockSpec((1, tn, tk), lambda i, j, kk: (kk // per, j, kk % per))
    else:
        per = n // b_split // tn
        b_spec = pl.BlockSpec((1, tk, tn), lambda i, j, kk: (j // per, kk, j % per))
    if out_split == 1:
        o_spec = pl.BlockSpec((tm, tn), lambda i, j, kk: (i, j))
        o_shape = jax.ShapeDtypeStruct((m, n), out_dtype)
    else:
        per_o = n // out_split // tn
        o_spec = pl.BlockSpec((1, tm, tn), lambda i, j, kk: (j // per_o, i, j % per_o))
        o_shape = jax.ShapeDtypeStruct((out_split, m, n // out_split), out_dtype)
    return pl.pallas_call(
        body_single if nk == 1 else body_acc, name=name, grid=(m // tm, n // tn, nk),
        in_specs=[a_spec, b_spec], out_specs=o_spec, out_shape=o_shape,
        scratch_shapes=[] if nk == 1 else [pltpu.VMEM((tm, tn), F32)],
        compiler_params=_params("parallel", "parallel", "arbitrary"),
    )(a, b)


def _ew(fn, ins, out_dtypes, name, tc=None):
    shape = ins[0].shape
    lead, (rows, cols) = shape[:-2], shape[-2:]
    tr = _pick(rows, (ROW_TILE, 128, 8))
    tc = cols if tc is None else tc
    n_in = len(ins)

    def body(*refs):
        res = fn(*[r[...] for r in refs[:n_in]])
        for o_ref, val in zip(refs[n_in:], res):
            o_ref[...] = val.astype(o_ref.dtype)

    if lead:
        spec = pl.BlockSpec((None, tr, tc), lambda l, i, j: (l, i, j))
    else:
        spec = pl.BlockSpec((tr, tc), lambda i, j: (i, j))
    return pl.pallas_call(
        body, name=name, grid=lead + (rows // tr, cols // tc),
        in_specs=[spec] * n_in, out_specs=[spec] * len(out_dtypes),
        out_shape=[jax.ShapeDtypeStruct(shape, dt) for dt in out_dtypes],
        compiler_params=_params(*(["parallel"] * (len(lead) + 2))),
    )(*ins)


def _act_fwd(a):
    def fn(a):
        r = jnp.maximum(a, 0.0)
        return (r * r,)
    return _ew(fn, [a], [BF16], "act_fwd", tc=2048)[0]


def _act_bwd(dr, a):
    def fn(dr, a):
        return (dr * (2.0 * jnp.maximum(a, 0.0)),)
    return _ew(fn, [dr, a], [BF16], "act_bwd", tc=2048)[0]


def _adamw(w, g, m, v, name):
    bc1 = 1.0 - ADAM_B1 ** ADAM_STEP
    bc2 = 1.0 - ADAM_B2 ** ADAM_STEP

    def fn(w, g, m, v):
        m = ADAM_B1 * m + (1.0 - ADAM_B1) * g
        v = ADAM_B2 * v + (1.0 - ADAM_B2) * (g * g)
        m_hat = m / bc1
        v_hat = v / bc2
        delta = -ADAM_LR * (m_hat / (jnp.sqrt(v_hat) + ADAM_EPS) + ADAM_WD * w)
        return delta, m, v
    return _ew(fn, [w, g, m, v], [F32, F32, F32], name)


def _row_spec(cols, block=0):
    return pl.BlockSpec((ROW_TILE, cols), lambda i, block=block: (i, block))


def _vec_spec(cols):
    return pl.BlockSpec((1, cols), lambda i: (0, 0))


def _sum_spec(cols):
    return pl.BlockSpec((8, cols), lambda i: (0, 0))


def _rstd(x):
    return lax.rsqrt(jnp.mean(x * x, axis=-1, keepdims=True) + RMS_EPS)


def _modnorm_fwd(x, g, shift, scale, name):
    s = x.shape[0]

    def body(x_ref, g_ref, sh_ref, sc_ref, h_ref):
        xv = x_ref[...]
        n = xv * _rstd(xv)
        h_ref[...] = ((n * g_ref[...]) * (1.0 + sc_ref[...]) + sh_ref[...]).astype(BF16)

    return pl.pallas_call(
        body, name=name, grid=(s // ROW_TILE,),
        in_specs=[_row_spec(D), _vec_spec(D), _vec_spec(D), _vec_spec(D)], out_specs=_row_spec(D),
        out_shape=jax.ShapeDtypeStruct((s, D), BF16), compiler_params=_params("parallel"),
    )(x, g, shift, scale)


def _post_fwd(x, y, g, gate, name):
    s = x.shape[0]

    def body(x_ref, y_ref, g_ref, gate_ref, o_ref):
        yv = y_ref[...]
        o_ref[...] = x_ref[...] + gate_ref[...] * ((yv * _rstd(yv)) * g_ref[...])

    return pl.pallas_call(
        body, name=name, grid=(s // ROW_TILE,),
        in_specs=[_row_spec(D), _row_spec(D), _vec_spec(D), _vec_spec(D)], out_specs=_row_spec(D),
        out_shape=jax.ShapeDtypeStruct((s, D), F32), compiler_params=_params("parallel"),
    )(x, y, g, gate)


def _post_bwd(dxo, y, g, gate, name):
    s = dxo.shape[0]

    def body(d_ref, y_ref, g_ref, gate_ref, dy_ref, sum_ref):
        @pl.when(pl.program_id(0) == 0)
        def _():
            sum_ref[...] = jnp.zeros_like(sum_ref)

        dv, yv = d_ref[...], y_ref[...]
        r = _rstd(yv)
        n = yv * r
        sum_ref[0:1, :] += jnp.sum(dv * (n * g_ref[...]), axis=0, keepdims=True)
        sum_ref[1:2, :] += jnp.sum((dv * gate_ref[...]) * n, axis=0, keepdims=True)
        dn = (dv * gate_ref[...]) * g_ref[...]
        dy_ref[...] = (r * (dn - n * jnp.mean(dn * n, axis=-1, keepdims=True))).astype(BF16)

    return pl.pallas_call(
        body, name=name, grid=(s // ROW_TILE,),
        in_specs=[_row_spec(D), _row_spec(D), _vec_spec(D), _vec_spec(D)],
        out_specs=[_row_spec(D), _sum_spec(D)],
        out_shape=[jax.ShapeDtypeStruct((s, D), BF16), jax.ShapeDtypeStruct((8, D), F32)],
        compiler_params=_params("arbitrary"),
    )(dxo, y, g, gate)


def _modnorm_bwd(dh, x, dxo, g, scale, name):
    s = dh.shape[0]

    def body(dh_ref, x_ref, d_ref, g_ref, sc_ref, dx_ref, sum_ref):
        @pl.when(pl.program_id(0) == 0)
        def _():
            sum_ref[...] = jnp.zeros_like(sum_ref)

        dhv, xv = dh_ref[...], x_ref[...]
        r = _rstd(xv)
        n = xv * r
        one_sc = 1.0 + sc_ref[...]
        sum_ref[0:1, :] += jnp.sum(dhv, axis=0, keepdims=True)
        sum_ref[1:2, :] += jnp.sum(dhv * (n * g_ref[...]), axis=0, keepdims=True)
        sum_ref[2:3, :] += jnp.sum((dhv * one_sc) * n, axis=0, keepdims=True)
        dn = (dhv * one_sc) * g_ref[...]
        dx_ref[...] = d_ref[...] + r * (dn - n * jnp.mean(dn * n, axis=-1, keepdims=True))

    return pl.pallas_call(
        body, name=name, grid=(s // ROW_TILE,),
        in_specs=[_row_spec(D), _row_spec(D), _row_spec(D), _vec_spec(D), _vec_spec(D)],
        out_specs=[_row_spec(D), _sum_spec(D)],
        out_shape=[jax.ShapeDtypeStruct((s, D), F32), jax.ShapeDtypeStruct((8, D), F32)],
        compiler_params=_params("arbitrary"),
    )(dh, x, dxo, g, scale)


def _loss_head(y, target):
    s = y.shape[0]

    def body(y_ref, t_ref, dy_ref, sum_ref):
        @pl.when(pl.program_id(0) == 0)
        def _():
            sum_ref[...] = jnp.zeros_like(sum_ref)

        err = y_ref[...] - t_ref[...]
        dy_ref[...] = err * (1.0 / D)
        sum_ref[...] += jnp.sum(err * err)

    return pl.pallas_call(
        body, name="loss_head", grid=(s // ROW_TILE,),
        in_specs=[_row_spec(D), _row_spec(D)],
        out_specs=[_row_spec(D), pl.BlockSpec((8, 128), lambda i: (0, 0))],
        out_shape=[jax.ShapeDtypeStruct((s, D), F32), jax.ShapeDtypeStruct((8, 128), F32)],
        compiler_params=_params("arbitrary"),
    )(y, target)


def _merge_fwd(z, pa, pb, pc):
    s = z.shape[0]

    def body(g0_ref, g1_ref, g2_ref, pa_ref, pb_ref, pc_ref, o_ref):
        o_ref[...] = (jax.nn.sigmoid(g0_ref[...]) * pa_ref[...] + jax.nn.sigmoid(g1_ref[...]) * pb_ref[...]
                      + jax.nn.sigmoid(g2_ref[...]) * pc_ref[...]).astype(BF16)

    return pl.pallas_call(
        body, name="merge_fwd", grid=(s // ROW_TILE,),
        in_specs=[_row_spec(D, 0), _row_spec(D, 1), _row_spec(D, 2), _row_spec(D), _row_spec(D), _row_spec(D)],
        out_specs=_row_spec(D), out_shape=jax.ShapeDtypeStruct((s, D), BF16),
        compiler_params=_params("parallel"),
    )(z, z, z, pa, pb, pc)


def _merge_bwd(dm, z, pa, pb, pc):
    s = z.shape[0]

    def body(dm_ref, g0_ref, g1_ref, g2_ref, pa_ref, pb_ref, pc_ref, dgl_ref, da_ref, db_ref, dc_ref):
        dmv = dm_ref[...]
        for i, (g_ref, p_ref, d_ref) in enumerate(((g0_ref, pa_ref, da_ref), (g1_ref, pb_ref, db_ref), (g2_ref, pc_ref, dc_ref))):
            gate = jax.nn.sigmoid(g_ref[...])
            dgl_ref[:, i * D:(i + 1) * D] = ((dmv * p_ref[...]) * (gate * (1.0 - gate))).astype(BF16)
            d_ref[...] = (dmv * gate).astype(BF16)

    return pl.pallas_call(
        body, name="merge_bwd", grid=(s // ROW_TILE,),
        in_specs=[_row_spec(D), _row_spec(D, 0), _row_spec(D, 1), _row_spec(D, 2), _row_spec(D), _row_spec(D), _row_spec(D)],
        out_specs=[_row_spec(3 * D), _row_spec(D), _row_spec(D), _row_spec(D)],
        out_shape=[jax.ShapeDtypeStruct((s, 3 * D), BF16)] + [jax.ShapeDtypeStruct((s, D), BF16)] * 3,
        compiler_params=_params("parallel"),
    )(dm, z, z, z, pa, pb, pc)


def _shift_down(v, n):
    row = lax.broadcasted_iota(jnp.int32, v.shape, 0)
    return jnp.where(row >= n, pltpu.roll(v, n, axis=0), 0.0)


def _shift_up(v, n):
    s = v.shape[0]
    row = lax.broadcasted_iota(jnp.int32, v.shape, 0)
    return jnp.where(row < s - n, pltpu.roll(v, s - n, axis=0), 0.0)


def _log_sigmoid(v):
    return jnp.minimum(v, 0.0) - jnp.log1p(jnp.exp(-jnp.abs(v)))


def _cumf_fwd(fl, bias):
    s = fl.shape[0]

    def body(fl_ref, b_ref, o_ref):
        acc = _log_sigmoid(fl_ref[...] + b_ref[...])
        step = 1
        while step < s:
            acc = acc + _shift_down(acc, step)
            step *= 2
        o_ref[...] = acc

    return pl.pallas_call(body, name="cumf_fwd", out_shape=jax.ShapeDtypeStruct((s, 128), F32),
                          compiler_params=pltpu.CompilerParams(vmem_limit_bytes=V7X_VMEM_LIMIT))(fl, bias)


def _cumf_bwd(dcum, fl, bias):
    s = fl.shape[0]

    def body(d_ref, fl_ref, b_ref, dfl_ref, db_ref):
        acc = d_ref[...]
        step = 1
        while step < s:
            acc = acc + _shift_up(acc, step)
            step *= 2
        dfl = acc * jax.nn.sigmoid(-(fl_ref[...] + b_ref[...]))
        dfl_ref[...] = dfl.astype(BF16)
        db_ref[...] = jnp.broadcast_to(jnp.sum(dfl, axis=0, keepdims=True), (8, 128))

    return pl.pallas_call(
        body, name="cumf_bwd",
        out_shape=[jax.ShapeDtypeStruct((s, 128), BF16), jax.ShapeDtypeStruct((8, 128), F32)],
        compiler_params=pltpu.CompilerParams(vmem_limit_bytes=V7X_VMEM_LIMIT))(dcum, fl, bias)


def _pool_windows(v, shift):
    s2 = v + shift(v, 1)
    s4 = s2 + shift(s2, 2)
    s8 = s4 + shift(s4, 4)
    s16 = s8 + shift(s8, 8)
    group = lax.broadcasted_iota(jnp.int32, v.shape, 1) // 64
    return jnp.where(group == 0, s2, jnp.where(group == 1, s4, jnp.where(group == 2, s8, s16)))


def _pool_count(shape):
    group = lax.broadcasted_iota(jnp.int32, shape, 1) // 64
    window = jnp.where(group == 0, 2.0, jnp.where(group == 1, 4.0, jnp.where(group == 2, 8.0, 16.0)))
    t1 = (lax.broadcasted_iota(jnp.int32, shape, 0) + 1).astype(F32)
    return jnp.minimum(t1, window)


def _pc_specs(s):
    zcol = lambda blk: pl.BlockSpec((s, 256), lambda i, blk=blk: (0, blk))
    first = Z_PC // 256
    return [zcol(first), zcol(first + 1), zcol(first + 2), zcol(first + 3),
            pl.BlockSpec((256, 256), lambda i: (0, 0)), pl.BlockSpec((1, 256), lambda i: (0, 0)),
            pl.BlockSpec((3, 256), lambda i: (0, 0))]


def _poolconv_fwd(z, wbd, pscale, convw):
    s = z.shape[0]

    def body(pu_ref, ch_ref, cb_ref, cc_ref, w_ref, ps_ref, cw_ref, yb_ref, yc_ref):
        u = pu_ref[...]
        p = _pool_windows(u, _shift_down) / _pool_count(u.shape) - u
        yb = jnp.dot(p.astype(BF16), w_ref[...].astype(BF16), preferred_element_type=F32) * ps_ref[...]
        yb_ref[...] = yb.astype(BF16)
        uc = cc_ref[...] * ch_ref[...]
        cw = cw_ref[...]
        conv = cw[0:1, :] * _shift_down(uc, 2) + cw[1:2, :] * _shift_down(uc, 1) + cw[2:3, :] * uc
        yc_ref[...] = (cb_ref[...] * conv).astype(BF16)

    out = pl.BlockSpec((s, 256), lambda i: (0, 0))
    return pl.pallas_call(
        body, name="poolconv_fwd", grid=(1,), in_specs=_pc_specs(s), out_specs=[out, out],
        out_shape=[jax.ShapeDtypeStruct((s, 256), BF16)] * 2, compiler_params=_params("arbitrary"),
    )(z, z, z, z, wbd, pscale, convw)


def _poolconv_bwd(dyb, dyc, z, wbd, pscale, convw):
    s = z.shape[0]

    def body(dyb_ref, dyc_ref, pu_ref, ch_ref, cb_ref, cc_ref, w_ref, ps_ref, cw_ref, dz_ref, dw_ref, dps_ref, dcw_ref):
        u = pu_ref[...]
        count = _pool_count(u.shape)
        p = (_pool_windows(u, _shift_down) / count - u).astype(BF16)
        wb = w_ref[...].astype(BF16)
        dyb_v = dyb_ref[...]
        pw = jnp.dot(p, wb, preferred_element_type=F32)
        dps_ref[...] = jnp.broadcast_to(jnp.sum(dyb_v * pw, axis=0, keepdims=True), (8, 256))
        dys = (dyb_v * ps_ref[...]).astype(BF16)
        dp = lax.dot_general(dys, wb, (((1,), (1,)), ((), ())), preferred_element_type=F32)
        dw_ref[...] = lax.dot_general(p, dys, (((0,), (0,)), ((), ())), preferred_element_type=F32)
        dz_ref[:, 0:256] = (_pool_windows(dp / count, _shift_up) - dp).astype(BF16)

        ch, cb, cc = ch_ref[...], cb_ref[...], cc_ref[...]
        uc = cc * ch
        cw = cw_ref[...]
        u2, u1 = _shift_down(uc, 2), _shift_down(uc, 1)
        conv = cw[0:1, :] * u2 + cw[1:2, :] * u1 + cw[2:3, :] * uc
        dyc_v = dyc_ref[...]
        dconv = dyc_v * cb
        du = cw[0:1, :] * _shift_up(dconv, 2) + cw[1:2, :] * _shift_up(dconv, 1) + cw[2:3, :] * dconv
        dz_ref[:, 256:512] = (du * cc).astype(BF16)
        dz_ref[:, 512:768] = (dyc_v * conv).astype(BF16)
        dz_ref[:, 768:1024] = (du * ch).astype(BF16)
        dcw_ref[...] = jnp.zeros_like(dcw_ref)
        dcw_ref[0:1, :] = jnp.sum(dconv * u2, axis=0, keepdims=True)
        dcw_ref[1:2, :] = jnp.sum(dconv * u1, axis=0, keepdims=True)
        dcw_ref[2:3, :] = jnp.sum(dconv * uc, axis=0, keepdims=True)

    blk = lambda r, c: pl.BlockSpec((r, c), lambda i: (0, 0))
    return pl.pallas_call(
        body, name="poolconv_bwd", grid=(1,),
        in_specs=[blk(s, 256), blk(s, 256)] + _pc_specs(s),
        out_specs=[blk(s, 1024), blk(256, 256), blk(8, 256), blk(8, 256)],
        out_shape=[jax.ShapeDtypeStruct((s, 1024), BF16), jax.ShapeDtypeStruct((256, 256), F32),
                   jax.ShapeDtypeStruct((8, 256), F32), jax.ShapeDtypeStruct((8, 256), F32)],
        compiler_params=_params("arbitrary"),
    )(dyb, dyc, z, z, z, z, wbd, pscale, convw)


_NT = (((1,), (1,)), ((), ()))
_TN = (((0,), (0,)), ((), ()))


def _att_logits(q, k, fc, fr, q0, k0):
    logits = lax.dot_general(q, k, _NT, preferred_element_type=F32) * (HEAD_DIM ** -0.5) + fc - fr
    row = q0 + lax.broadcasted_iota(jnp.int32, logits.shape, 0)
    col = k0 + lax.broadcasted_iota(jnp.int32, logits.shape, 1)
    return jnp.where(row >= col, logits, NEG_INF)


HEAD_PAIRS = HEADS // 2


def _lane_pick(v, lane, idx):
    return jnp.sum(jnp.where(lane == idx, v, 0.0), axis=-1, keepdims=True)


def _lane_put(lane, idx, col):
    return jnp.where(lane == idx, col, 0.0)


def _split_heads(v, low):
    zero = jnp.zeros_like(v)
    return jnp.where(low, v, zero), jnp.where(low, zero, v)


def _attn_fwd(qkv, cum, fr):
    s = qkv.shape[0]
    blk = ATT_BLOCK
    nb = s // blk

    def body(q_ref, k_ref, v_ref, cum_ref, fr_ref, o_ref, lse_ref):
        qi, hp = pl.program_id(0), pl.program_id(1)
        lane = lax.broadcasted_iota(jnp.int32, (blk, 128), 1)
        low = lane < HEAD_DIM
        qs = _split_heads(q_ref[...], low)
        cumv = cum_ref[...]
        fcs = (_lane_pick(cumv, lane, 2 * hp), _lane_pick(cumv, lane, 2 * hp + 1))

        def step(j, carry):
            k0 = pl.multiple_of(j * blk, blk)
            k2, v2 = k_ref[pl.ds(k0, blk), :], v_ref[pl.ds(k0, blk), :]
            out = []
            for sub in (0, 1):
                m, l, acc = carry[sub]
                logits = _att_logits(qs[sub], k2, fcs[sub], fr_ref[sub, pl.ds(j, 1), :], qi * blk, k0)
                m_new = jnp.maximum(m, jnp.max(logits, axis=-1, keepdims=True))
                p = jnp.exp(logits - m_new)
                alpha = jnp.exp(m - m_new)
                l = alpha * l + jnp.sum(p, axis=-1, keepdims=True)
                acc = alpha * acc + jnp.dot(p.astype(BF16), v2, preferred_element_type=F32)
                out.append((m_new, l, acc))
            return tuple(out)

        one = (jnp.full((blk, 1), NEG_INF, F32), jnp.zeros((blk, 1), F32), jnp.zeros((blk, 128), F32))
        (m0, l0, acc0), (m1, l1, acc1) = lax.fori_loop(0, qi + 1, step, (one, one))
        o_ref[...] = jnp.where(low, acc0 / l0, acc1 / l1)

        @pl.when(hp == 0)
        def _():
            lse_ref[...] = jnp.zeros_like(lse_ref)

        lse_ref[...] += _lane_put(lane, 2 * hp, m0 + jnp.log(l0)) + _lane_put(lane, 2 * hp + 1, m1 + jnp.log(l1))

    return pl.pallas_call(
        body, name="attn_fwd", grid=(nb, HEAD_PAIRS),
        in_specs=[pl.BlockSpec((blk, 128), lambda i, hp: (i, hp)),
                  pl.BlockSpec((s, 128), lambda i, hp: (0, HEAD_PAIRS + hp)),
                  pl.BlockSpec((s, 128), lambda i, hp: (0, 2 * HEAD_PAIRS + hp)),
                  pl.BlockSpec((blk, 128), lambda i, hp: (i, 0)),
                  pl.BlockSpec((2, nb, blk), lambda i, hp: (hp, 0, 0))],
        out_specs=[pl.BlockSpec((blk, 128), lambda i, hp: (i, hp)), pl.BlockSpec((blk, 128), lambda i, hp: (i, 0))],
        out_shape=[jax.ShapeDtypeStruct((s, A_WIDTH), F32), jax.ShapeDtypeStruct((s, 128), F32)],
        compiler_params=_params("parallel", "arbitrary"),
    )(qkv, qkv, qkv, cum, fr)


def _attn_bwd(qkv, do, o, lse, cum, fr):
    s = qkv.shape[0]
    blk = ATT_BLOCK
    nb = s // blk
    scale = HEAD_DIM ** -0.5

    def body(q_ref, k_ref, v_ref, do_ref, o_ref, lse_ref, cum_ref, fr_ref, dq_ref, dk_ref, dv_ref, dfc_ref, dfr_ref, dk_acc, dv_acc):
        hp = pl.program_id(0)
        lane = lax.broadcasted_iota(jnp.int32, (blk, 128), 1)
        low = lane < HEAD_DIM
        dk_acc[...] = jnp.zeros_like(dk_acc)
        dv_acc[...] = jnp.zeros_like(dv_acc)
        dfr_ref[...] = jnp.zeros_like(dfr_ref)

        @pl.when(hp == 0)
        def _():
            dfc_ref[...] = jnp.zeros_like(dfc_ref)

        def outer(i, carry):
            q0 = pl.multiple_of(i * blk, blk)
            rows = pl.ds(q0, blk)
            q2, do2 = q_ref[rows, :], do_ref[rows, :]
            prod = do2 * o_ref[rows, :]
            deltas = (jnp.sum(jnp.where(low, prod, 0.0), axis=-1, keepdims=True),
                      jnp.sum(jnp.where(low, 0.0, prod), axis=-1, keepdims=True))
            dob2 = do2.astype(BF16)
            qs, dos = _split_heads(q2, low), _split_heads(dob2, low)
            lsev, cumv = lse_ref[rows, :], cum_ref[rows, :]
            lses = (_lane_pick(lsev, lane, 2 * hp), _lane_pick(lsev, lane, 2 * hp + 1))
            fcs = (_lane_pick(cumv, lane, 2 * hp), _lane_pick(cumv, lane, 2 * hp + 1))

            def inner(j, carry):
                k0 = pl.multiple_of(j * blk, blk)
                cols = pl.ds(k0, blk)
                k2, v2 = k_ref[cols, :], v_ref[cols, :]
                out, dk_parts, dv_parts = [], [], []
                for sub in (0, 1):
                    dq, dfc = carry[sub]
                    p = jnp.exp(_att_logits(qs[sub], k2, fcs[sub], fr_ref[sub, pl.ds(j, 1), :], q0, k0) - lses[sub])
                    dp = lax.dot_general(dos[sub], v2, _NT, preferred_element_type=F32)
                    ds = p * (dp - deltas[sub])
                    dsb = ds.astype(BF16)
                    dk_parts.append(lax.dot_general(dsb, q2, _TN, preferred_element_type=F32))
                    dv_parts.append(lax.dot_general(p.astype(BF16), dob2, _TN, preferred_element_type=F32))
                    dfr_ref[sub, pl.ds(j, 1), :] -= jnp.sum(ds, axis=0, keepdims=True)
                    out.append((dq + jnp.dot(dsb, k2, preferred_element_type=F32), dfc + jnp.sum(ds, axis=-1, keepdims=True)))
                dk_acc[cols, :] += jnp.where(low, dk_parts[0], dk_parts[1])
                dv_acc[cols, :] += jnp.where(low, dv_parts[0], dv_parts[1])
                return tuple(out)

            one = (jnp.zeros((blk, 128), F32), jnp.zeros((blk, 1), F32))
            (dq0, dfc0), (dq1, dfc1) = lax.fori_loop(0, i + 1, inner, (one, one))
            dq_ref[rows, :] = (jnp.where(low, dq0, dq1) * scale).astype(BF16)
            dfc_ref[rows, :] += _lane_put(lane, 2 * hp, dfc0) + _lane_put(lane, 2 * hp + 1, dfc1)
            return carry

        lax.fori_loop(0, nb, outer, 0)
        dk_ref[...] = (dk_acc[...] * scale).astype(BF16)
        dv_ref[...] = dv_acc[...].astype(BF16)

    pair = lambda first: pl.BlockSpec((s, 128), lambda hp, first=first: (0, first + hp))
    whole = pl.BlockSpec((s, 128), lambda hp: (0, 0))
    rowv = pl.BlockSpec((2, nb, blk), lambda hp: (hp, 0, 0))
    return pl.pallas_call(
        body, name="attn_bwd", grid=(HEAD_PAIRS,),
        in_specs=[pair(0), pair(HEAD_PAIRS), pair(2 * HEAD_PAIRS), pair(0), pair(0), whole, whole, rowv],
        out_specs=[pair(0), pair(0), pair(0), whole, rowv],
        out_shape=[jax.ShapeDtypeStruct((s, A_WIDTH), BF16)] * 3 + [jax.ShapeDtypeStruct((s, 128), F32), jax.ShapeDtypeStruct((HEADS, nb, blk), F32)],
        scratch_shapes=[pltpu.VMEM((s, 128), F32), pltpu.VMEM((s, 128), F32)],
        compiler_params=_params("arbitrary"),
    )(qkv, qkv, qkv, do, o, lse, cum, fr)


def _ada_fwd(c_all, w_ada, b_loc):
    depth, _, n = w_ada.shape
    tn = 512

    def body(c_ref, w_ref, b_ref, o_ref, sc_ref):
        cv = c_ref[...]
        sc = cv * jax.nn.sigmoid(cv)
        sc_ref[...] = sc
        o_ref[0] = jnp.dot(sc.astype(BF16), w_ref[0].astype(BF16), preferred_element_type=F32) + b_ref[0]

    return pl.pallas_call(
        body, name="ada_fwd", grid=(depth, n // tn),
        in_specs=[pl.BlockSpec((N_DEV, D), lambda l, j: (0, 0)), pl.BlockSpec((1, D, tn), lambda l, j: (l, 0, j)),
                  pl.BlockSpec((1, 1, tn), lambda l, j: (l, 0, j))],
        out_specs=[pl.BlockSpec((1, N_DEV, tn), lambda l, j: (l, 0, j)), pl.BlockSpec((N_DEV, D), lambda l, j: (0, 0))],
        out_shape=[jax.ShapeDtypeStruct((depth, N_DEV, n), F32), jax.ShapeDtypeStruct((N_DEV, D), F32)],
        compiler_params=_params("arbitrary", "arbitrary"),
    )(c_all, w_ada, b_loc)


def _sum_devices(gathered):
    n = gathered.shape[1]
    tn = _pick(n, (1408, 1024, 640, 512, 128))

    def body(g_ref, o_ref):
        acc = g_ref[0:8, :]
        for dev in range(1, N_DEV):
            acc = acc + g_ref[8 * dev:8 * dev + 8, :]
        o_ref[...] = acc

    return pl.pallas_call(
        body, name="sum_devices", grid=(n // tn,),
        in_specs=[pl.BlockSpec((8 * N_DEV, tn), lambda j: (0, j))], out_specs=pl.BlockSpec((8, tn), lambda j: (0, j)),
        out_shape=jax.ShapeDtypeStruct((8, n), F32), compiler_params=_params("parallel"),
    )(gathered)


def _place():
    x, y, c = lax.axis_index("x"), lax.axis_index("y"), lax.axis_index("c")
    chips = [(1 - x, y), (x, 1 - y), (1 - x, 1 - y)]
    return x, y, c, chips


def _allgather8(block, name):
    m_per, n = block.shape

    def body(x_ref, out_ref, send_sems, recv_sems, local_sem):
        x, y, c, chips = _place()
        me, sibling = (x, y, c), (x, y, 1 - c)

        def rows(px, py, pc):
            return out_ref.at[pl.ds((4 * px + 2 * py + pc) * m_per, m_per), :]

        def copy(k, blk, to, src=None):
            return pltpu.make_async_remote_copy(
                src_ref=rows(*blk) if src is None else src, dst_ref=rows(*blk),
                send_sem=send_sems.at[k], recv_sem=recv_sems.at[k], device_id=to, device_id_type=MESH)

        mine = pltpu.make_async_copy(x_ref, rows(*me), local_sem)
        mine.start()
        first = [copy(0, me, sibling, src=x_ref)]
        first += [copy(1 + j, me, (*chip, c), src=x_ref) for j, chip in enumerate(chips)]
        for cp in first:
            cp.start()
        passed = [copy(4 + j, (*chip, c), sibling) for j, chip in enumerate(chips)]
        for j, chip in enumerate(chips):
            copy(1 + j, (*chip, c), me).wait_recv()
            passed[j].start()
        copy(0, sibling, me).wait_recv()
        for j, chip in enumerate(chips):
            copy(4 + j, (*chip, 1 - c), me).wait_recv()
        for cp in first + passed:
            cp.wait_send()
        mine.wait()

    return pl.pallas_call(
        body, name=name, out_shape=jax.ShapeDtypeStruct((N_DEV * m_per, n), block.dtype),
        in_specs=[pl.BlockSpec(memory_space=pltpu.VMEM)], out_specs=pl.BlockSpec(memory_space=pltpu.VMEM),
        scratch_shapes=[pltpu.SemaphoreType.DMA((7,)), pltpu.SemaphoreType.DMA((7,)), pltpu.SemaphoreType.DMA],
        compiler_params=pltpu.CompilerParams(vmem_limit_bytes=V7X_VMEM_LIMIT),
    )(block)


def _gather_weights(shards):
    n = len(shards)

    def body(*refs):
        ins, outs = refs[:n], refs[n:2 * n]
        send_sems, recv_sems = refs[2 * n:]
        x, y, c, chips = _place()
        j = 2 * x + y
        sibling = (x, y, 1 - c)
        chip_ids = [2 * px + py for px, py in chips]

        def remote(a, slot, src, dst, to):
            return pltpu.make_async_remote_copy(src_ref=src, dst_ref=dst, send_sem=send_sems.at[6 * a + slot],
                                                recv_sem=recv_sems.at[6 * a + slot], device_id=to, device_id_type=MESH)

        sends = [remote(a, k, ins[a].at[c], outs[a].at[j, c], (*chips[k], c)) for a in range(n) for k in range(3)]
        for cp in sends:
            cp.start()
        passed = []
        for a in range(n):
            for k in range(3):
                landed = outs[a].at[chip_ids[k], c]
                remote(a, k, landed, landed, sibling).wait_recv()
                fwd = remote(a, 3 + k, landed, landed, sibling)
                fwd.start()
                passed.append(fwd)
        for a in range(n):
            for k in range(3):
                from_sibling = outs[a].at[chip_ids[k], 1 - c]
                remote(a, 3 + k, from_sibling, from_sibling, sibling).wait_recv()
        for cp in sends + passed:
            cp.wait_send()

    return pl.pallas_call(
        body, name="gather_weights",
        out_shape=[jax.ShapeDtypeStruct((N_CHIPS,) + s.shape, s.dtype) for s in shards],
        in_specs=[_HBM] * n, out_specs=[_HBM] * n,
        scratch_shapes=[pltpu.SemaphoreType.DMA((6 * n,)), pltpu.SemaphoreType.DMA((6 * n,))],
    )(*shards)


def _send_sibling_layer(grads):
    n = len(grads)
    flat = [g for pair in grads for g in pair]

    def body(*refs):
        ins, outs = refs[:2 * n], refs[2 * n:3 * n]
        send_sems, recv_sems = refs[3 * n:]
        x, y, c, _ = _place()
        sibling = (x, y, 1 - c)
        for mine in (0, 1):
            @pl.when(c == mine)
            def _():
                cps = [pltpu.make_async_remote_copy(src_ref=ins[2 * a + (1 - mine)], dst_ref=outs[a], send_sem=send_sems.at[a],
                                                    recv_sem=recv_sems.at[a], device_id=sibling, device_id_type=MESH)
                       for a in range(n)]
                for cp in cps:
                    cp.start()
                for cp in cps:
                    cp.wait()

    return pl.pallas_call(
        body, name="rs_sibling", out_shape=[jax.ShapeDtypeStruct(pair[0].shape, pair[0].dtype) for pair in grads],
        in_specs=[_HBM] * (2 * n), out_specs=[_HBM] * n,
        scratch_shapes=[pltpu.SemaphoreType.DMA((n,)), pltpu.SemaphoreType.DMA((n,))],
    )(*flat)


def _chip_sum(g0, g1, other, sel, name):
    _, r, cdim = g0.shape
    tr = _pick(r, (256, 128))

    def body(sel_ref, g0_ref, g1_ref, t_ref, wire_ref, own_ref):
        p = pl.program_id(2)
        total = jnp.where(sel_ref[0] == 0, g0_ref[0], g1_ref[0]) + t_ref[0]
        wire_ref[0] = total.astype(BF16)

        @pl.when(p == sel_ref[1])
        def _():
            own_ref[...] = total

    def pick(layer):
        def index(i, jj, p, sel_ref):
            use = jnp.where(sel_ref[0] == layer, 1, 0)
            return (p * use, i * use, 0)
        return pl.BlockSpec((1, tr, cdim), index)

    blk = pl.BlockSpec((1, tr, cdim), lambda i, jj, p, sel_ref: (p, i, 0))
    return pl.pallas_call(
        body, name=name,
        grid_spec=pltpu.PrefetchScalarGridSpec(
            num_scalar_prefetch=1, grid=(r // tr, 1, N_CHIPS),
            in_specs=[pick(0), pick(1), blk], out_specs=[blk, pl.BlockSpec((tr, cdim), lambda i, jj, p, sel_ref: (i, 0))]),
        out_shape=[jax.ShapeDtypeStruct(g0.shape, BF16), jax.ShapeDtypeStruct((r, cdim), F32)],
        compiler_params=_params("parallel", "arbitrary", "arbitrary"),
    )(sel, g0, g1, other)


def _exchange_chips(wires):
    n = len(wires)

    def body(*refs):
        ins, outs = refs[:n], refs[n:2 * n]
        send_sems, recv_sems = refs[2 * n:]
        x, y, c, chips = _place()
        cps = [pltpu.make_async_remote_copy(src_ref=ins[a].at[2 * chips[k][0] + chips[k][1]], dst_ref=outs[a].at[k],
                                            send_sem=send_sems.at[3 * a + k], recv_sem=recv_sems.at[3 * a + k],
                                            device_id=(*chips[k], c), device_id_type=MESH)
               for a in range(n) for k in range(3)]
        for cp in cps:
            cp.start()
        for cp in cps:
            cp.wait()

    return pl.pallas_call(
        body, name="rs_chips", out_shape=[jax.ShapeDtypeStruct((3,) + w.shape[1:], w.dtype) for w in wires],
        in_specs=[_HBM] * n, out_specs=[_HBM] * n,
        scratch_shapes=[pltpu.SemaphoreType.DMA((3 * n,)), pltpu.SemaphoreType.DMA((3 * n,))],
    )(*wires)


def _final_sum(own, recv, name):
    r, cdim = own.shape
    tr = _pick(r, (256, 128))

    def body(own_ref, r0_ref, r1_ref, r2_ref, o_ref):
        o_ref[...] = ((own_ref[...] + r0_ref[0].astype(F32)) + r1_ref[0].astype(F32)) + r2_ref[0].astype(F32)

    part = lambda k: pl.BlockSpec((1, tr, cdim), lambda i, k=k: (k, i, 0))
    spec = pl.BlockSpec((tr, cdim), lambda i: (i, 0))
    return pl.pallas_call(
        body, name=name, grid=(r // tr,), in_specs=[spec, part(0), part(1), part(2)], out_specs=spec,
        out_shape=jax.ShapeDtypeStruct((r, cdim), F32), compiler_params=_params("parallel"),
    )(own, recv, recv, recv)


def _share_layers(sums):
    n = len(sums)

    def body(*refs):
        ins, outs = refs[:n], refs[n:2 * n]
        send_sems, recv_sems = refs[2 * n:]
        x, y, c, _ = _place()
        cps = [pltpu.make_async_remote_copy(src_ref=ins[a], dst_ref=outs[a], send_sem=send_sems.at[a], recv_sem=recv_sems.at[a],
                                            device_id=(x, y, 1 - c), device_id_type=MESH) for a in range(n)]
        for cp in cps:
            cp.start()
        for cp in cps:
            cp.wait()

    return pl.pallas_call(
        body, name="rs_share", out_shape=[jax.ShapeDtypeStruct(s.shape, s.dtype) for s in sums],
        in_specs=[_HBM] * n, out_specs=[_HBM] * n,
        scratch_shapes=[pltpu.SemaphoreType.DMA((n,)), pltpu.SemaphoreType.DMA((n,))],
    )(*sums)


def _row(v):
    return v.reshape(1, -1)


def _layer_fwd(x, w, mod):
    s = x.shape[0]
    nb = s // ATT_BLOCK
    h = _modnorm_fwd(x, _row(w["g_mix_pre"]), mod[0:1], mod[1:2], "mix_pre_fwd")
    z = _mm(h, w["w_all"], name="mm_in")
    qkv = z[:, Z_QKV:Z_PC].astype(BF16)
    fl = z[:, Z_FL:Z_COLS]
    cum = _cumf_fwd(fl, w["b_f_pad"])
    fr = cum[:, :HEADS].T.reshape(HEADS, nb, ATT_BLOCK)
    br_a, lse = _attn_fwd(qkv, cum, fr)
    br_b, br_c = _poolconv_fwd(z, w["w_pool_bd"], _row(w["pool_scale"]), w["conv_w"])
    wbr = w["w_branch"]
    pa = _mm(br_a, wbr[:A_WIDTH], name="mm_br_a")
    pb = _mm(br_b, wbr[A_WIDTH:A_WIDTH + POOL_WIDTH], name="mm_br_b")
    pc = _mm(br_c, wbr[A_WIDTH + POOL_WIDTH:], name="mm_br_c")
    merged = _merge_fwd(z, pa, pb, pc)
    y = _mm(merged, w["w_out"], name="mm_out")
    x1 = _post_fwd(x, y, _row(w["g_mix_post"]), mod[2:3], "mix_post_fwd")
    h2 = _modnorm_fwd(x1, _row(w["g_ff_pre"]), mod[3:4], mod[4:5], "ff_pre_fwd")
    a = _mm(h2, w["w_ff1"], b_split=N_CHIPS, name="mm_ff1")
    r = _act_fwd(a)
    y2 = _mm(r, w["w_ff2"], name="mm_ff2")
    x2 = _post_fwd(x1, y2, _row(w["g_ff_post"]), mod[5:6], "ff_post_fwd")
    saved = dict(x=x, h=h, z=z, qkv=qkv, fl=fl, cum=cum, fr=fr, lse=lse, br_a=br_a, br_b=br_b, br_c=br_c, pa=pa, pb=pb, pc=pc,
                 merged=merged, y=y, x1=x1, h2=h2, a=a, r=r, y2=y2)
    return x2, saved


def _layer_bwd(dx2, sv, w, mod):
    s = dx2.shape[0]
    dy2, sum_ff_post = _post_bwd(dx2, sv["y2"], _row(w["g_ff_post"]), mod[5:6], "ff_post_bwd")
    dr = _mm(dy2, w["w_ff2"], tb=True, name="mm_ff2_dx")
    d_w_ff2 = _mm(sv["r"], dy2, ta=True, name="mm_ff2_dw")
    da = _act_bwd(dr, sv["a"])
    dh2 = _mm(da, w["w_ff1"], tb=True, b_split=N_CHIPS, name="mm_ff1_dx")
    d_w_ff1 = _mm(sv["h2"], da, ta=True, out_split=N_CHIPS, name="mm_ff1_dw")
    dx1, sum_ff_pre = _modnorm_bwd(dh2, sv["x1"], dx2, _row(w["g_ff_pre"]), mod[4:5], "ff_pre_bwd")

    dy, sum_mix_post = _post_bwd(dx1, sv["y"], _row(w["g_mix_post"]), mod[2:3], "mix_post_bwd")
    dmerged = _mm(dy, w["w_out"], tb=True, name="mm_out_dx")
    d_w_out = _mm(sv["merged"], dy, ta=True, name="mm_out_dw")
    dgl, dpa, dpb, dpc = _merge_bwd(dmerged, sv["z"], sv["pa"], sv["pb"], sv["pc"])
    wbr = w["w_branch"]
    dbr_a = _mm(dpa, wbr[:A_WIDTH], tb=True, name="mm_br_a_dx")
    dbr_b = _mm(dpb, wbr[A_WIDTH:A_WIDTH + POOL_WIDTH], tb=True, name="mm_br_b_dx")
    dbr_c = _mm(dpc, wbr[A_WIDTH + POOL_WIDTH:], tb=True, name="mm_br_c_dx")
    d_w_branch = jnp.concatenate([_mm(sv["br_a"], dpa, ta=True, name="mm_br_a_dw"), _mm(sv["br_b"], dpb, ta=True, name="mm_br_b_dw"),
                                  _mm(sv["br_c"], dpc, ta=True, name="mm_br_c_dw")], axis=0)

    dq, dk, dv, dfc, dfr = _attn_bwd(sv["qkv"], dbr_a, sv["br_a"], sv["lse"], sv["cum"], sv["fr"])
    dcum = dfc + jnp.pad(dfr.reshape(HEADS, s).T, ((0, 0), (0, 128 - HEADS)))
    dfl, sum_bf = _cumf_bwd(dcum, sv["fl"], w["b_f_pad"])
    dpc_z, d_wbd, sum_ps, sum_cw = _poolconv_bwd(dbr_b, dbr_c, sv["z"], w["w_pool_bd"], _row(w["pool_scale"]), w["conv_w"])
    dz = jnp.concatenate([dgl, dq, dk, dv, dpc_z, dfl], axis=1)
    dh = _mm(dz, w["w_all"], tb=True, name="mm_in_dx")
    d_w_all = _mm(sv["h"], dz, ta=True, name="mm_in_dw")
    dx, sum_mix_pre = _modnorm_bwd(dh, sv["x"], dx1, _row(w["g_mix_pre"]), mod[1:2], "mix_pre_bwd")

    dmod = jnp.stack([sum_mix_pre[0], sum_mix_pre[1], sum_mix_post[0], sum_ff_pre[0], sum_ff_pre[1], sum_ff_post[0]])
    d_w_in = _w_in_shards(d_w_all)
    d_w_pool = jnp.stack([d_wbd[64 * g:64 * g + 64, 64 * g:64 * g + 64] for g in range(4)])
    big = dict(w_in=d_w_in, w_branch=d_w_branch, w_out=d_w_out, w_ff1=d_w_ff1, w_ff2=d_w_ff2)
    small = dict(g_mix_pre=sum_mix_pre[2], g_mix_post=sum_mix_post[1], g_ff_pre=sum_ff_pre[2], g_ff_post=sum_ff_post[1],
                 b_f=sum_bf[0, :HEADS], w_pool=d_w_pool, pool_scale=sum_ps[0], conv_w=sum_cw[0:3])
    return dx, dmod, big, small


_QKV_END, _FL_END, _PC_END = 3 * A_WIDTH, 3 * A_WIDTH + HEADS, 3 * A_WIDTH + HEADS + POOL_WIDTH + 3 * CONV_WIDTH
_W_IN_GROUPS = ((_PC_END, IN_COLS, Z_GL), (0, _QKV_END, Z_QKV), (_FL_END, _PC_END, Z_PC), (_QKV_END, _FL_END, Z_FL))
_SHARD_COLS = IN_COLS // N_CHIPS


def _w_all_from_shards(blocks):
    pieces = []
    for lo, hi, _ in _W_IN_GROUPS:
        for p in range(N_CHIPS):
            a, b = max(lo, p * _SHARD_COLS), min(hi, (p + 1) * _SHARD_COLS)
            if a < b:
                pieces.append(blocks[p][:, a - p * _SHARD_COLS:b - p * _SHARD_COLS])
    pieces.append(jnp.zeros((D, Z_COLS - IN_COLS), blocks.dtype))
    return jnp.concatenate(pieces, axis=1)


def _w_in_shards(d_w_all):
    blocks = []
    for p in range(N_CHIPS):
        pieces = []
        for lo, hi, at in sorted(_W_IN_GROUPS):
            a, b = max(lo, p * _SHARD_COLS), min(hi, (p + 1) * _SHARD_COLS)
            if a < b:
                pieces.append(d_w_all[:, at + a - lo:at + b - lo])
        blocks.append(jnp.concatenate(pieces, axis=1))
    return jnp.stack(blocks)


def _full_layer_weights(w_in_blocks, w_branch, w_out, w_ff1, w_ff2, g_mix_pre, g_mix_post, g_ff_pre, g_ff_post, b_f, w_pool, pool_scale, conv_w):
    w_all = _w_all_from_shards(w_in_blocks)
    wbd = jnp.zeros((POOL_WIDTH, POOL_WIDTH), F32)
    for g in range(4):
        wbd = wbd.at[64 * g:64 * g + 64, 64 * g:64 * g + 64].set(w_pool[g])
    return dict(w_all=w_all, w_branch=w_branch, w_out=w_out, w_ff1=w_ff1, w_ff2=w_ff2, g_mix_pre=g_mix_pre, g_mix_post=g_mix_post,
                g_ff_pre=g_ff_pre, g_ff_post=g_ff_post, b_f_pad=jnp.pad(b_f, (0, 128 - HEADS)).reshape(1, 128), w_pool_bd=wbd,
                pool_scale=pool_scale, conv_w=conv_w)


def _local_step(x, target, mods, layers):
    saved = []
    act = x
    for l in range(DEPTH):
        act, sv = _layer_fwd(act, layers[l], mods[l])
        saved.append(sv)
    dact, sq = _loss_head(act, target)
    loss = sq[0, 0] * (0.5 / D)
    dmods, bigs, smalls = [None] * DEPTH, [None] * DEPTH, [None] * DEPTH
    for l in reversed(range(DEPTH)):
        dact, dmods[l], bigs[l], smalls[l] = _layer_bwd(dact, saved[l], layers[l], mods[l])
    return loss, dact, jnp.stack(dmods), bigs, smalls


_SMALL = ("g_mix_pre", "g_mix_post", "g_ff_pre", "g_ff_post", "b_f", "w_pool", "pool_scale", "conv_w")
_BIG = ("w_in", "w_branch", "w_out", "w_ff1", "w_ff2")


def _pack(parts, rows=8):
    flat = jnp.concatenate([p.reshape(-1) for p in parts])
    width = -(-flat.shape[0] // (rows * 128)) * 128
    return jnp.pad(flat, (0, rows * width - flat.shape[0])).reshape(rows, width)


def _unpack(packed, like):
    flat = packed.reshape(-1)
    out, at = [], 0
    for ref in like:
        out.append(flat[at:at + ref.size].reshape(ref.shape))
        at += ref.size
    return out


def kernel(x, c, w_ada, b_ada, g_mix_pre, g_mix_post, g_ff_pre, g_ff_post, w_in, b_f, w_pool, pool_scale, conv_w, w_branch, w_out, w_ff1, w_ff2, loss_target, m_w_ada, m_b_ada, m_g_mix_pre, m_g_mix_post, m_g_ff_pre, m_g_ff_post, m_w_in, m_b_f, m_w_pool, m_pool_scale, m_conv_w, m_w_branch, m_w_out, m_w_ff1, m_w_ff2, v_w_ada, v_b_ada, v_g_mix_pre, v_g_mix_post, v_g_ff_pre, v_g_ff_post, v_w_in, v_b_f, v_w_pool, v_pool_scale, v_conv_w, v_w_branch, v_w_out, v_w_ff1, v_w_ff2):
    xi, yi, ci = lax.axis_index("x"), lax.axis_index("y"), lax.axis_index("c")
    chip = 2 * xi + yi
    dev = 2 * chip + ci
    n_ada = w_ada.shape[2]

    own = [w.astype(BF16) for w in (w_in, w_branch, w_out, w_ff1, w_ff2)]
    g_in, g_br, g_out, g_f1, g_f2 = [lax.dynamic_update_index_in_dim(g, o, chip, 0) for g, o in zip(_gather_weights(own), own)]
    full_br = g_br.transpose(1, 0, 2, 3).reshape(DEPTH, D, D)
    full_out = g_out.transpose(1, 0, 2, 3).reshape(DEPTH, D, D)
    full_f2 = g_f2.transpose(1, 0, 2, 3).reshape(DEPTH, D_FF, D)

    first = jnp.zeros((8, D + 384), F32).at[0, :D].set(c[0]).at[0, D:].set(conv_w.reshape(-1))
    got = _allgather8(first, "gather_cond").reshape(N_DEV, 8, D + 384)[:, 0]
    c_all = got[:, :D]
    conv_full = got[0::2, D:].reshape(N_CHIPS, DEPTH, 3, CONV_WIDTH // N_CHIPS).transpose(1, 2, 0, 3).reshape(DEPTH, 3, CONV_WIDTH)

    b_loc = lax.dynamic_slice_in_dim(b_ada, chip * n_ada, n_ada, axis=1).reshape(DEPTH, 1, n_ada)
    mod_cols, silu_c = _ada_fwd(c_all, w_ada, b_loc)
    got = _allgather8(mod_cols.reshape(DEPTH * N_DEV, n_ada), "gather_mod").reshape(N_DEV, DEPTH, N_DEV, n_ada)[0::2]
    mod_all = got.transpose(1, 2, 0, 3).reshape(DEPTH, N_DEV, 6, D)
    mods = lax.dynamic_index_in_dim(mod_all, dev, axis=1, keepdims=False)

    layers = [_full_layer_weights(g_in[:, l], full_br[l], full_out[l], g_f1[:, l], full_f2[l], g_mix_pre[l], g_mix_post[l], g_ff_pre[l],
                                  g_ff_post[l], b_f[l], w_pool[l], pool_scale[l], conv_full[l]) for l in range(DEPTH)]
    loss_part, grad_x, dmods, bigs, smalls = _local_step(x[0], loss_target[0], mods, layers)
    loss = lax.psum(loss_part, ("x", "y", "c"))

    small_parts = [smalls[l][name] for name in _SMALL for l in range(DEPTH)]
    packed = _pack([dmods] + small_parts)
    gathered = _allgather8(packed, "gather_small")
    dmod_all = gathered.reshape(N_DEV, -1)[:, :dmods.size].reshape(N_DEV, DEPTH, 6 * D)
    summed = _unpack(_sum_devices(gathered), [dmods] + small_parts)
    grad_b_ada = summed[0].reshape(DEPTH, 6 * D)
    small_grads = {name: jnp.stack(summed[1 + 2 * i:3 + 2 * i]) for i, name in enumerate(_SMALL)}
    small_grads["conv_w"] = lax.dynamic_slice_in_dim(small_grads["conv_w"], chip * (CONV_WIDTH // N_CHIPS), CONV_WIDTH // N_CHIPS, axis=2)

    dmod_loc = lax.dynamic_slice_in_dim(dmod_all.transpose(1, 0, 2), chip * n_ada, n_ada, axis=2)
    silu_pad = jnp.pad(silu_c, ((0, 128 - N_DEV), (0, 0)))
    grad_w_ada = jnp.stack([_mm(silu_pad, jnp.pad(dmod_loc[l], ((0, 128 - N_DEV), (0, 0))), ta=True, name="mm_ada_dw") for l in range(DEPTH)])

    def shard_blocks(name, g):
        if name in ("w_in", "w_ff1"):
            return g
        return g.reshape(N_CHIPS, g.shape[0] // N_CHIPS, g.shape[1])

    pairs = [tuple(shard_blocks(name, bigs[l][name]) for l in range(DEPTH)) for name in _BIG]
    others = _send_sibling_layer(pairs)
    sel = jnp.stack([ci, chip]).astype(jnp.int32)
    wires, owns = zip(*[_chip_sum(pairs[i][0], pairs[i][1], others[i], sel, "rs_chip_sum_" + name) for i, name in enumerate(_BIG)])
    recvs = _exchange_chips(list(wires))
    sums = [_final_sum(owns[i], recvs[i], "rs_final_" + name) for i, name in enumerate(_BIG)]
    big_grads = {name: jnp.where(ci == 0, jnp.stack([mine, theirs]), jnp.stack([theirs, mine]))
                 for name, mine, theirs in zip(_BIG, sums, _share_layers(sums))}

    grads = dict(w_ada=grad_w_ada, b_ada=grad_b_ada, **small_grads, **big_grads)
    weights = dict(w_ada=w_ada, b_ada=b_ada, g_mix_pre=g_mix_pre, g_mix_post=g_mix_post, g_ff_pre=g_ff_pre, g_ff_post=g_ff_post, w_in=w_in,
                   b_f=b_f, w_pool=w_pool, pool_scale=pool_scale, conv_w=conv_w, w_branch=w_branch, w_out=w_out, w_ff1=w_ff1, w_ff2=w_ff2)
    m_in = dict(w_ada=m_w_ada, b_ada=m_b_ada, g_mix_pre=m_g_mix_pre, g_mix_post=m_g_mix_post, g_ff_pre=m_g_ff_pre, g_ff_post=m_g_ff_post,
                w_in=m_w_in, b_f=m_b_f, w_pool=m_w_pool, pool_scale=m_pool_scale, conv_w=m_conv_w, w_branch=m_w_branch, w_out=m_w_out,
                w_ff1=m_w_ff1, w_ff2=m_w_ff2)
    v_in = dict(w_ada=v_w_ada, b_ada=v_b_ada, g_mix_pre=v_g_mix_pre, g_mix_post=v_g_mix_post, g_ff_pre=v_g_ff_pre, g_ff_post=v_g_ff_post,
                w_in=v_w_in, b_f=v_b_f, w_pool=v_w_pool, pool_scale=v_pool_scale, conv_w=v_conv_w, w_branch=v_w_branch, w_out=v_w_out,
                w_ff1=v_w_ff1, w_ff2=v_w_ff2)
    order = ("w_ada", "b_ada", "g_mix_pre", "g_mix_post", "g_ff_pre", "g_ff_post", "w_in", "b_f", "w_pool", "pool_scale", "conv_w",
             "w_branch", "w_out", "w_ff1", "w_ff2")
    delta, new_m, new_v = {}, {}, {}
    for name in ("w_ada",) + _BIG:
        delta[name], new_m[name], new_v[name] = _adamw(weights[name], grads[name], m_in[name], v_in[name], "adamw_" + name)
    tiny = ("b_ada",) + _SMALL
    res = _adamw(*[_pack([src[name] for name in tiny]) for src in (weights, grads, m_in, v_in)], "adamw_small")
    for out, packed_res in zip((delta, new_m, new_v), res):
        for name, val in zip(tiny, _unpack(packed_res, [weights[name] for name in tiny])):
            out[name] = val

    return (loss, grad_x[None], *[grads[n] for n in order], *[delta[n] for n in order], *[new_m[n] for n in order],
            *[new_v[n] for n in order])
```

```python
import functools

import jax
import jax.numpy as jnp
from jax import lax
from jax.experimental import pallas as pl
from jax.experimental.pallas import tpu as pltpu

F32 = jnp.float32
BF16 = jnp.bfloat16
MESH = pl.DeviceIdType.MESH

D = 1024
DEPTH = 2
HEADS = 8
HEAD_DIM = 64
A_WIDTH = 512
POOL_WIDTH = 256
CONV_WIDTH = 256
D_FF = 4096
IN_COLS = 5640
Z_GL, Z_QKV, Z_PC, Z_FL, Z_COLS = 0, 3072, 4608, 5632, 5760
RMS_EPS = 1e-6
NEG_INF = -1e30
ATT_BLOCK = 256
ROW_TILE = 256
N_CHIPS = 4
N_DEV = 8
V7X_VMEM_LIMIT = 48 * 1024 * 1024

ADAM_LR = 0.001
ADAM_B1 = 0.9
ADAM_B2 = 0.999
ADAM_EPS = 1e-08
ADAM_WD = 0.01
ADAM_STEP = 10

_HBM = pl.BlockSpec(memory_space=pltpu.HBM)


def _params(*sem):
    return pltpu.CompilerParams(dimension_semantics=sem, vmem_limit_bytes=V7X_VMEM_LIMIT)


def _pick(dim, cands):
    for cand in cands:
        if dim % cand == 0:
            return cand
    return dim


def _mm(a, b, *, ta=False, tb=False, b_split=1, out_split=1, out_dtype=F32, name):
    (k, m) = a.shape if ta else a.shape[::-1]
    b_rows, b_cols = b.shape[-2], b.shape[-1] * b_split
    (n, k2) = (b_rows, b_cols) if tb else (b_cols, b_rows)
    assert k == k2, (a.shape, b.shape, ta, tb)
    n_unit = n // (out_split * (1 if tb else b_split))
    k_unit = k // (b_split if tb else 1)
    tm = _pick(m, (1024, 512, 256, 128))
    tn = _pick(n_unit, (1024, 1152, 768, 640, 512, 256, 128))
    tk = _pick(k_unit, (1024, 1152, 512, 640, 256, 128))
    nk = k // tk
    dims = (((0 if ta else 1,), (1 if tb else 0,)), ((), ()))

    def dot(a_ref, b_ref):
        b_val = b_ref[0] if b_split > 1 else b_ref[...]
        return lax.dot_general(a_ref[...].astype(BF16), b_val.astype(BF16), dims, preferred_element_type=F32)

    def put(o_ref, val):
        if out_split > 1:
            o_ref[0] = val.astype(o_ref.dtype)
        else:
            o_ref[...] = val.astype(o_ref.dtype)

    def body_single(a_ref, b_ref, o_ref):
        put(o_ref, dot(a_ref, b_ref))

    def body_acc(a_ref, b_ref, o_ref, acc_ref):
        kk = pl.program_id(2)

        @pl.when(kk == 0)
        def _():
            acc_ref[...] = jnp.zeros_like(acc_ref)

        acc_ref[...] += dot(a_ref, b_ref)

        @pl.when(kk == nk - 1)
        def _():
            put(o_ref, acc_ref[...])

    a_spec = pl.BlockSpec((tk, tm), lambda i, j, kk: (kk, i)) if ta else pl.BlockSpec((tm, tk), lambda i, j, kk: (i, kk))
    if b_split == 1:
        b_spec = pl.BlockSpec((tn, tk), lambda i, j, kk: (j, kk)) if tb else pl.BlockSpec((tk, tn), lambda i, j, kk: (kk, j))
    elif tb:
        per = k_unit // tk
        b_spec = pl.BlockSpec((1, tn, tk), lambda i, j, kk: (kk // per, j, kk % per))
    else:
        per = n // b_split // tn
        b_spec = pl.BlockSpec((1, tk, tn), lambda i, j, kk: (j // per, kk, j % per))
    if out_split == 1:
        o_spec = pl.BlockSpec((tm, tn), lambda i, j, kk: (i, j))
        o_shape = jax.ShapeDtypeStruct((m, n), out_dtype)
    else:
        per_o = n // out_split // tn
        o_spec = pl.BlockSpec((1, tm, tn), lambda i, j, kk: (j // per_o, i, j % per_o))
        o_shape = jax.ShapeDtypeStruct((out_split, m, n // out_split), out_dtype)
    return pl.pallas_call(
        body_single if nk == 1 else body_acc, name=name, grid=(m // tm, n // tn, nk),
        in_specs=[a_spec, b_spec], out_specs=o_spec, out_shape=o_shape,
        scratch_shapes=[] if nk == 1 else [pltpu.VMEM((tm, tn), F32)],
        compiler_params=_params("parallel", "parallel", "arbitrary"),
    )(a, b)


def _ew(fn, ins, out_dtypes, name, tc=None):
    shape = ins[0].shape
    lead, (rows, cols) = shape[:-2], shape[-2:]
    tr = _pick(rows, (ROW_TILE, 128, 8))
    tc = cols if tc is None else tc
    n_in = len(ins)

    def body(*refs):
        res = fn(*[r[...] for r in refs[:n_in]])
        for o_ref, val in zip(refs[n_in:], res):
            o_ref[...] = val.astype(o_ref.dtype)

    if lead:
        spec = pl.BlockSpec((None, tr, tc), lambda l, i, j: (l, i, j))
    else:
        spec = pl.BlockSpec((tr, tc), lambda i, j: (i, j))
    return pl.pallas_call(
        body, name=name, grid=lead + (rows // tr, cols // tc),
        in_specs=[spec] * n_in, out_specs=[spec] * len(out_dtypes),
        out_shape=[jax.ShapeDtypeStruct(shape, dt) for dt in out_dtypes],
        compiler_params=_params(*(["parallel"] * (len(lead) + 2))),
    )(*ins)


def _act_fwd(a):
    def fn(a):
        r = jnp.maximum(a, 0.0)
        return (r * r,)
    return _ew(fn, [a], [BF16], "act_fwd", tc=2048)[0]


def _act_bwd(dr, a):
    def fn(dr, a):
        return (dr * (2.0 * jnp.maximum(a, 0.0)),)
    return _ew(fn, [dr, a], [BF16], "act_bwd", tc=2048)[0]


def _adamw(w, g, m, v, name):
    bc1 = 1.0 - ADAM_B1 ** ADAM_STEP
    bc2 = 1.0 - ADAM_B2 ** ADAM_STEP

    def fn(w, g, m, v):
        m = ADAM_B1 * m + (1.0 - ADAM_B1) * g
        v = ADAM_B2 * v + (1.0 - ADAM_B2) * (g * g)
        m_hat = m / bc1
        v_hat = v / bc2
        delta = -ADAM_LR * (m_hat / (jnp.sqrt(v_hat) + ADAM_EPS) + ADAM_WD * w)
        return delta, m, v
    return _ew(fn, [w, g, m, v], [F32, F32, F32], name)


def _row_spec(cols, block=0):
    return pl.BlockSpec((ROW_TILE, cols), lambda i, block=block: (i, block))


def _vec_spec(cols):
    return pl.BlockSpec((1, cols), lambda i: (0, 0))


def _sum_spec(cols):
    return pl.BlockSpec((8, cols), lambda i: (0, 0))


def _rstd(x):
    return lax.rsqrt(jnp.mean(x * x, axis=-1, keepdims=True) + RMS_EPS)


def _modnorm_fwd(x, g, shift, scale, name):
    s = x.shape[0]

    def body(x_ref, g_ref, sh_ref, sc_ref, h_ref):
        xv = x_ref[...]
        n = xv * _rstd(xv)
        h_ref[...] = ((n * g_ref[...]) * (1.0 + sc_ref[...]) + sh_ref[...]).astype(BF16)

    return pl.pallas_call(
        body, name=name, grid=(s // ROW_TILE,),
        in_specs=[_row_spec(D), _vec_spec(D), _vec_spec(D), _vec_spec(D)], out_specs=_row_spec(D),
        out_shape=jax.ShapeDtypeStruct((s, D), BF16), compiler_params=_params("parallel"),
    )(x, g, shift, scale)


def _post_fwd(x, y, g, gate, name):
    s = x.shape[0]

    def body(x_ref, y_ref, g_ref, gate_ref, o_ref):
        yv = y_ref[...]
        o_ref[...] = x_ref[...] + gate_ref[...] * ((yv * _rstd(yv)) * g_ref[...])

    return pl.pallas_call(
        body, name=name, grid=(s // ROW_TILE,),
        in_specs=[_row_spec(D), _row_spec(D), _vec_spec(D), _vec_spec(D)], out_specs=_row_spec(D),
        out_shape=jax.ShapeDtypeStruct((s, D), F32), compiler_params=_params("parallel"),
    )(x, y, g, gate)


def _post_bwd(dxo, y, g, gate, name):
    s = dxo.shape[0]

    def body(d_ref, y_ref, g_ref, gate_ref, dy_ref, sum_ref):
        @pl.when(pl.program_id(0) == 0)
        def _():
            sum_ref[...] = jnp.zeros_like(sum_ref)

        dv, yv = d_ref[...], y_ref[...]
        r = _rstd(yv)
        n = yv * r
        sum_ref[0:1, :] += jnp.sum(dv * (n * g_ref[...]), axis=0, keepdims=True)
        sum_ref[1:2, :] += jnp.sum((dv * gate_ref[...]) * n, axis=0, keepdims=True)
        dn = (dv * gate_ref[...]) * g_ref[...]
        dy_ref[...] = (r * (dn - n * jnp.mean(dn * n, axis=-1, keepdims=True))).astype(BF16)

    return pl.pallas_call(
        body, name=name, grid=(s // ROW_TILE,),
        in_specs=[_row_spec(D), _row_spec(D), _vec_spec(D), _vec_spec(D)],
        out_specs=[_row_spec(D), _sum_spec(D)],
        out_shape=[jax.ShapeDtypeStruct((s, D), BF16), jax.ShapeDtypeStruct((8, D), F32)],
        compiler_params=_params("arbitrary"),
    )(dxo, y, g, gate)


def _modnorm_bwd(dh, x, dxo, g, scale, name):
    s = dh.shape[0]

    def body(dh_ref, x_ref, d_ref, g_ref, sc_ref, dx_ref, sum_ref):
        @pl.when(pl.program_id(0) == 0)
        def _():
            sum_ref[...] = jnp.zeros_like(sum_ref)

        dhv, xv = dh_ref[...], x_ref[...]
        r = _rstd(xv)
        n = xv * r
        one_sc = 1.0 + sc_ref[...]
        sum_ref[0:1, :] += jnp.sum(dhv, axis=0, keepdims=True)
        sum_ref[1:2, :] += jnp.sum(dhv * (n * g_ref[...]), axis=0, keepdims=True)
        sum_ref[2:3, :] += jnp.sum((dhv * one_sc) * n, axis=0, keepdims=True)
        dn = (dhv * one_sc) * g_ref[...]
        dx_ref[...] = d_ref[...] + r * (dn - n * jnp.mean(dn * n, axis=-1, keepdims=True))

    return pl.pallas_call(
        body, name=name, grid=(s // ROW_TILE,),
        in_specs=[_row_spec(D), _row_spec(D), _row_spec(D), _vec_spec(D), _vec_spec(D)],
        out_specs=[_row_spec(D), _sum_spec(D)],
        out_shape=[jax.ShapeDtypeStruct((s, D), F32), jax.ShapeDtypeStruct((8, D), F32)],
        compiler_params=_params("arbitrary"),
    )(dh, x, dxo, g, scale)


def _loss_head(y, target):
    s = y.shape[0]

    def body(y_ref, t_ref, dy_ref, sum_ref):
        @pl.when(pl.program_id(0) == 0)
        def _():
            sum_ref[...] = jnp.zeros_like(sum_ref)

        err = y_ref[...] - t_ref[...]
        dy_ref[...] = err * (1.0 / D)
        sum_ref[...] += jnp.sum(err * err)

    return pl.pallas_call(
        body, name="loss_head", grid=(s // ROW_TILE,),
        in_specs=[_row_spec(D), _row_spec(D)],
        out_specs=[_row_spec(D), pl.BlockSpec((8, 128), lambda i: (0, 0))],
        out_shape=[jax.ShapeDtypeStruct((s, D), F32), jax.ShapeDtypeStruct((8, 128), F32)],
        compiler_params=_params("arbitrary"),
    )(y, target)


def _merge_fwd(z, pa, pb, pc):
    s = z.shape[0]

    def body(g0_ref, g1_ref, g2_ref, pa_ref, pb_ref, pc_ref, o_ref):
        o_ref[...] = (jax.nn.sigmoid(g0_ref[...]) * pa_ref[...] + jax.nn.sigmoid(g1_ref[...]) * pb_ref[...]
                      + jax.nn.sigmoid(g2_ref[...]) * pc_ref[...]).astype(BF16)

    return pl.pallas_call(
        body, name="merge_fwd", grid=(s // ROW_TILE,),
        in_specs=[_row_spec(D, 0), _row_spec(D, 1), _row_spec(D, 2), _row_spec(D), _row_spec(D), _row_spec(D)],
        out_specs=_row_spec(D), out_shape=jax.ShapeDtypeStruct((s, D), BF16),
        compiler_params=_params("parallel"),
    )(z, z, z, pa, pb, pc)


def _merge_bwd(dm, z, pa, pb, pc):
    s = z.shape[0]

    def body(dm_ref, g0_ref, g1_ref, g2_ref, pa_ref, pb_ref, pc_ref, dgl_ref, da_ref, db_ref, dc_ref):
        dmv = dm_ref[...]
        for i, (g_ref, p_ref, d_ref) in enumerate(((g0_ref, pa_ref, da_ref), (g1_ref, pb_ref, db_ref), (g2_ref, pc_ref, dc_ref))):
            gate = jax.nn.sigmoid(g_ref[...])
            dgl_ref[:, i * D:(i + 1) * D] = ((dmv * p_ref[...]) * (gate * (1.0 - gate))).astype(BF16)
            d_ref[...] = (dmv * gate).astype(BF16)

    return pl.pallas_call(
        body, name="merge_bwd", grid=(s // ROW_TILE,),
        in_specs=[_row_spec(D), _row_spec(D, 0), _row_spec(D, 1), _row_spec(D, 2), _row_spec(D), _row_spec(D), _row_spec(D)],
        out_specs=[_row_spec(3 * D), _row_spec(D), _row_spec(D), _row_spec(D)],
        out_shape=[jax.ShapeDtypeStruct((s, 3 * D), BF16)] + [jax.ShapeDtypeStruct((s, D), BF16)] * 3,
        compiler_params=_params("parallel"),
    )(dm, z, z, z, pa, pb, pc)


def _shift_down(v, n):
    row = lax.broadcasted_iota(jnp.int32, v.shape, 0)
    return jnp.where(row >= n, pltpu.roll(v, n, axis=0), 0.0)


def _shift_up(v, n):
    s = v.shape[0]
    row = lax.broadcasted_iota(jnp.int32, v.shape, 0)
    return jnp.where(row < s - n, pltpu.roll(v, s - n, axis=0), 0.0)


def _log_sigmoid(v):
    return jnp.minimum(v, 0.0) - jnp.log1p(jnp.exp(-jnp.abs(v)))


def _cumf_fwd(fl, bias):
    s = fl.shape[0]

    def body(fl_ref, b_ref, o_ref):
        acc = _log_sigmoid(fl_ref[...] + b_ref[...])
        step = 1
        while step < s:
            acc = acc + _shift_down(acc, step)
            step *= 2
        o_ref[...] = acc

    return pl.pallas_call(body, name="cumf_fwd", out_shape=jax.ShapeDtypeStruct((s, 128), F32),
                          compiler_params=pltpu.CompilerParams(vmem_limit_bytes=V7X_VMEM_LIMIT))(fl, bias)


def _cumf_bwd(dcum, fl, bias):
    s = fl.shape[0]

    def body(d_ref, fl_ref, b_ref, dfl_ref, db_ref):
        acc = d_ref[...]
        step = 1
        while step < s:
            acc = acc + _shift_up(acc, step)
            step *= 2
        dfl = acc * jax.nn.sigmoid(-(fl_ref[...] + b_ref[...]))
        dfl_ref[...] = dfl.astype(BF16)
        db_ref[...] = jnp.broadcast_to(jnp.sum(dfl, axis=0, keepdims=True), (8, 128))

    return pl.pallas_call(
        body, name="cumf_bwd",
        out_shape=[jax.ShapeDtypeStruct((s, 128), BF16), jax.ShapeDtypeStruct((8, 128), F32)],
        compiler_params=pltpu.CompilerParams(vmem_limit_bytes=V7X_VMEM_LIMIT))(dcum, fl, bias)


def _pool_windows(v, shift):
    s2 = v + shift(v, 1)
    s4 = s2 + shift(s2, 2)
    s8 = s4 + shift(s4, 4)
    s16 = s8 + shift(s8, 8)
    group = lax.broadcasted_iota(jnp.int32, v.shape, 1) // 64
    return jnp.where(group == 0, s2, jnp.where(group == 1, s4, jnp.where(group == 2, s8, s16)))


def _pool_count(shape):
    group = lax.broadcasted_iota(jnp.int32, shape, 1) // 64
    window = jnp.where(group == 0, 2.0, jnp.where(group == 1, 4.0, jnp.where(group == 2, 8.0, 16.0)))
    t1 = (lax.broadcasted_iota(jnp.int32, shape, 0) + 1).astype(F32)
    return jnp.minimum(t1, window)


def _pc_specs(s):
    zcol = lambda blk: pl.BlockSpec((s, 256), lambda i, blk=blk: (0, blk))
    first = Z_PC // 256
    return [zcol(first), zcol(first + 1), zcol(first + 2), zcol(first + 3),
            pl.BlockSpec((256, 256), lambda i: (0, 0)), pl.BlockSpec((1, 256), lambda i: (0, 0)),
            pl.BlockSpec((3, 256), lambda i: (0, 0))]


def _poolconv_fwd(z, wbd, pscale, convw):
    s = z.shape[0]

    def body(pu_ref, ch_ref, cb_ref, cc_ref, w_ref, ps_ref, cw_ref, yb_ref, yc_ref):
        u = pu_ref[...]
        p = _pool_windows(u, _shift_down) / _pool_count(u.shape) - u
        yb = jnp.dot(p.astype(BF16), w_ref[...].astype(BF16), preferred_element_type=F32) * ps_ref[...]
        yb_ref[...] = yb.astype(BF16)
        uc = cc_ref[...] * ch_ref[...]
        cw = cw_ref[...]
        conv = cw[0:1, :] * _shift_down(uc, 2) + cw[1:2, :] * _shift_down(uc, 1) + cw[2:3, :] * uc
        yc_ref[...] = (cb_ref[...] * conv).astype(BF16)

    out = pl.BlockSpec((s, 256), lambda i: (0, 0))
    return pl.pallas_call(
        body, name="poolconv_fwd", grid=(1,), in_specs=_pc_specs(s), out_specs=[out, out],
        out_shape=[jax.ShapeDtypeStruct((s, 256), BF16)] * 2, compiler_params=_params("arbitrary"),
    )(z, z, z, z, wbd, pscale, convw)


def _poolconv_bwd(dyb, dyc, z, wbd, pscale, convw):
    s = z.shape[0]

    def body(dyb_ref, dyc_ref, pu_ref, ch_ref, cb_ref, cc_ref, w_ref, ps_ref, cw_ref, dz_ref, dw_ref, dps_ref, dcw_ref):
        u = pu_ref[...]
        count = _pool_count(u.shape)
        p = (_pool_windows(u, _shift_down) / count - u).astype(BF16)
        wb = w_ref[...].astype(BF16)
        dyb_v = dyb_ref[...]
        pw = jnp.dot(p, wb, preferred_element_type=F32)
        dps_ref[...] = jnp.broadcast_to(jnp.sum(dyb_v * pw, axis=0, keepdims=True), (8, 256))
        dys = (dyb_v * ps_ref[...]).astype(BF16)
        dp = lax.dot_general(dys, wb, (((1,), (1,)), ((), ())), preferred_element_type=F32)
        dw_ref[...] = lax.dot_general(p, dys, (((0,), (0,)), ((), ())), preferred_element_type=F32)
        dz_ref[:, 0:256] = (_pool_windows(dp / count, _shift_up) - dp).astype(BF16)

        ch, cb, cc = ch_ref[...], cb_ref[...], cc_ref[...]
        uc = cc * ch
        cw = cw_ref[...]
        u2, u1 = _shift_down(uc, 2), _shift_down(uc, 1)
        conv = cw[0:1, :] * u2 + cw[1:2, :] * u1 + cw[2:3, :] * uc
        dyc_v = dyc_ref[...]
        dconv = dyc_v * cb
        du = cw[0:1, :] * _shift_up(dconv, 2) + cw[1:2, :] * _shift_up(dconv, 1) + cw[2:3, :] * dconv
        dz_ref[:, 256:512] = (du * cc).astype(BF16)
        dz_ref[:, 512:768] = (dyc_v * conv).astype(BF16)
        dz_ref[:, 768:1024] = (du * ch).astype(BF16)
        dcw_ref[...] = jnp.zeros_like(dcw_ref)
        dcw_ref[0:1, :] = jnp.sum(dconv * u2, axis=0, keepdims=True)
        dcw_ref[1:2, :] = jnp.sum(dconv * u1, axis=0, keepdims=True)
        dcw_ref[2:3, :] = jnp.sum(dconv * uc, axis=0, keepdims=True)

    blk = lambda r, c: pl.BlockSpec((r, c), lambda i: (0, 0))
    return pl.pallas_call(
        body, name="poolconv_bwd", grid=(1,),
        in_specs=[blk(s, 256), blk(s, 256)] + _pc_specs(s),
        out_specs=[blk(s, 1024), blk(256, 256), blk(8, 256), blk(8, 256)],
        out_shape=[jax.ShapeDtypeStruct((s, 1024), BF16), jax.ShapeDtypeStruct((256, 256), F32),
                   jax.ShapeDtypeStruct((8, 256), F32), jax.ShapeDtypeStruct((8, 256), F32)],
        compiler_params=_params("arbitrary"),
    )(dyb, dyc, z, z, z, z, wbd, pscale, convw)


_NT = (((1,), (1,)), ((), ()))
_TN = (((0,), (0,)), ((), ()))


def _att_logits(q, k, fc, fr, q0, k0):
    logits = lax.dot_general(q, k, _NT, preferred_element_type=F32) * (HEAD_DIM ** -0.5) + fc - fr
    row = q0 + lax.broadcasted_iota(jnp.int32, logits.shape, 0)
    col = k0 + lax.broadcasted_iota(jnp.int32, logits.shape, 1)
    return jnp.where(row >= col, logits, NEG_INF)


HEAD_PAIRS = HEADS // 2


def _lane_pick(v, lane, idx):
    return jnp.sum(jnp.where(lane == idx, v, 0.0), axis=-1, keepdims=True)


def _lane_put(lane, idx, col):
    return jnp.where(lane == idx, col, 0.0)


def _split_heads(v, low):
    zero = jnp.zeros_like(v)
    return jnp.where(low, v, zero), jnp.where(low, zero, v)


def _attn_fwd(qkv, cum, fr):
    s = qkv.shape[0]
    blk = ATT_BLOCK
    nb = s // blk

    def body(q_ref, k_ref, v_ref, cum_ref, fr_ref, o_ref, lse_ref):
        qi, hp = pl.program_id(0), pl.program_id(1)
        lane = lax.broadcasted_iota(jnp.int32, (blk, 128), 1)
        low = lane < HEAD_DIM
        qs = _split_heads(q_ref[...], low)
        cumv = cum_ref[...]
        fcs = (_lane_pick(cumv, lane, 2 * hp), _lane_pick(cumv, lane, 2 * hp + 1))

        def step(j, carry):
            k0 = pl.multiple_of(j * blk, blk)
            k2, v2 = k_ref[pl.ds(k0, blk), :], v_ref[pl.ds(k0, blk), :]
            out = []
            for sub in (0, 1):
                m, l, acc = carry[sub]
                logits = _att_logits(qs[sub], k2, fcs[sub], fr_ref[sub, pl.ds(j, 1), :], qi * blk, k0)
                m_new = jnp.maximum(m, jnp.max(logits, axis=-1, keepdims=True))
                p = jnp.exp(logits - m_new)
                alpha = jnp.exp(m - m_new)
                l = alpha * l + jnp.sum(p, axis=-1, keepdims=True)
                acc = alpha * acc + jnp.dot(p.astype(BF16), v2, preferred_element_type=F32)
                out.append((m_new, l, acc))
            return tuple(out)

        one = (jnp.full((blk, 1), NEG_INF, F32), jnp.zeros((blk, 1), F32), jnp.zeros((blk, 128), F32))
        (m0, l0, acc0), (m1, l1, acc1) = lax.fori_loop(0, qi + 1, step, (one, one))
        o_ref[...] = jnp.where(low, acc0 / l0, acc1 / l1)

        @pl.when(hp == 0)
        def _():
            lse_ref[...] = jnp.zeros_like(lse_ref)

        lse_ref[...] += _lane_put(lane, 2 * hp, m0 + jnp.log(l0)) + _lane_put(lane, 2 * hp + 1, m1 + jnp.log(l1))

    return pl.pallas_call(
        body, name="attn_fwd", grid=(nb, HEAD_PAIRS),
        in_specs=[pl.BlockSpec((blk, 128), lambda i, hp: (i, hp)),
                  pl.BlockSpec((s, 128), lambda i, hp: (0, HEAD_PAIRS + hp)),
                  pl.BlockSpec((s, 128), lambda i, hp: (0, 2 * HEAD_PAIRS + hp)),
                  pl.BlockSpec((blk, 128), lambda i, hp: (i, 0)),
                  pl.BlockSpec((2, nb, blk), lambda i, hp: (hp, 0, 0))],
        out_specs=[pl.BlockSpec((blk, 128), lambda i, hp: (i, hp)), pl.BlockSpec((blk, 128), lambda i, hp: (i, 0))],
        out_shape=[jax.ShapeDtypeStruct((s, A_WIDTH), F32), jax.ShapeDtypeStruct((s, 128), F32)],
        compiler_params=_params("parallel", "arbitrary"),
    )(qkv, qkv, qkv, cum, fr)


def _attn_bwd(qkv, do, o, lse, cum, fr):
    s = qkv.shape[0]
    blk = ATT_BLOCK
    nb = s // blk
    scale = HEAD_DIM ** -0.5

    def body(q_ref, k_ref, v_ref, do_ref, o_ref, lse_ref, cum_ref, fr_ref, dq_ref, dk_ref, dv_ref, dfc_ref, dfr_ref, dk_acc, dv_acc):
        hp = pl.program_id(0)
        lane = lax.broadcasted_iota(jnp.int32, (blk, 128), 1)
        low = lane < HEAD_DIM
        dk_acc[...] = jnp.zeros_like(dk_acc)
        dv_acc[...] = jnp.zeros_like(dv_acc)
        dfr_ref[...] = jnp.zeros_like(dfr_ref)

        @pl.when(hp == 0)
        def _():
            dfc_ref[...] = jnp.zeros_like(dfc_ref)

        def outer(i, carry):
            q0 = pl.multiple_of(i * blk, blk)
            rows = pl.ds(q0, blk)
            q2, do2 = q_ref[rows, :], do_ref[rows, :]
            prod = do2 * o_ref[rows, :]
            deltas = (jnp.sum(jnp.where(low, prod, 0.0), axis=-1, keepdims=True),
                      jnp.sum(jnp.where(low, 0.0, prod), axis=-1, keepdims=True))
            dob2 = do2.astype(BF16)
            qs, dos = _split_heads(q2, low), _split_heads(dob2, low)
            lsev, cumv = lse_ref[rows, :], cum_ref[rows, :]
            lses = (_lane_pick(lsev, lane, 2 * hp), _lane_pick(lsev, lane, 2 * hp + 1))
            fcs = (_lane_pick(cumv, lane, 2 * hp), _lane_pick(cumv, lane, 2 * hp + 1))

            def inner(j, carry):
                k0 = pl.multiple_of(j * blk, blk)
                cols = pl.ds(k0, blk)
                k2, v2 = k_ref[cols, :], v_ref[cols, :]
                out, dk_parts, dv_parts = [], [], []
                for sub in (0, 1):
                    dq, dfc = carry[sub]
                    p = jnp.exp(_att_logits(qs[sub], k2, fcs[sub], fr_ref[sub, pl.ds(j, 1), :], q0, k0) - lses[sub])
                    dp = lax.dot_general(dos[sub], v2, _NT, preferred_element_type=F32)
                    ds = p * (dp - deltas[sub])
                    dsb = ds.astype(BF16)
                    dk_parts.append(lax.dot_general(dsb, q2, _TN, preferred_element_type=F32))
                    dv_parts.append(lax.dot_general(p.astype(BF16), dob2, _TN, preferred_element_type=F32))
                    dfr_ref[sub, pl.ds(j, 1), :] -= jnp.sum(ds, axis=0, keepdims=True)
                    out.append((dq + jnp.dot(dsb, k2, preferred_element_type=F32), dfc + jnp.sum(ds, axis=-1, keepdims=True)))
                dk_acc[cols, :] += jnp.where(low, dk_parts[0], dk_parts[1])
                dv_acc[cols, :] += jnp.where(low, dv_parts[0], dv_parts[1])
                return tuple(out)

            one = (jnp.zeros((blk, 128), F32), jnp.zeros((blk, 1), F32))
            (dq0, dfc0), (dq1, dfc1) = lax.fori_loop(0, i + 1, inner, (one, one))
            dq_ref[rows, :] = (jnp.where(low, dq0, dq1) * scale).astype(BF16)
            dfc_ref[rows, :] += _lane_put(lane, 2 * hp, dfc0) + _lane_put(lane, 2 * hp + 1, dfc1)
            return carry

        lax.fori_loop(0, nb, outer, 0)
        dk_ref[...] = (dk_acc[...] * scale).astype(BF16)
        dv_ref[...] = dv_acc[...].astype(BF16)

    pair = lambda first: pl.BlockSpec((s, 128), lambda hp, first=first: (0, first + hp))
    whole = pl.BlockSpec((s, 128), lambda hp: (0, 0))
    rowv = pl.BlockSpec((2, nb, blk), lambda hp: (hp, 0, 0))
    return pl.pallas_call(
        body, name="attn_bwd", grid=(HEAD_PAIRS,),
        in_specs=[pair(0), pair(HEAD_PAIRS), pair(2 * HEAD_PAIRS), pair(0), pair(0), whole, whole, rowv],
        out_specs=[pair(0), pair(0), pair(0), whole, rowv],
        out_shape=[jax.ShapeDtypeStruct((s, A_WIDTH), BF16)] * 3 + [jax.ShapeDtypeStruct((s, 128), F32), jax.ShapeDtypeStruct((HEADS, nb, blk), F32)],
        scratch_shapes=[pltpu.VMEM((s, 128), F32), pltpu.VMEM((s, 128), F32)],
        compiler_params=_params("arbitrary"),
    )(qkv, qkv, qkv, do, o, lse, cum, fr)


def _ada_fwd(c_all, w_ada, b_loc):
    depth, _, n = w_ada.shape
    tn = 512

    def body(c_ref, w_ref, b_ref, o_ref, sc_ref):
        cv = c_ref[...]
        sc = cv * jax.nn.sigmoid(cv)
        sc_ref[...] = sc
        o_ref[0] = jnp.dot(sc.astype(BF16), w_ref[0].astype(BF16), preferred_element_type=F32) + b_ref[0]

    return pl.pallas_call(
        body, name="ada_fwd", grid=(depth, n // tn),
        in_specs=[pl.BlockSpec((N_DEV, D), lambda l, j: (0, 0)), pl.BlockSpec((1, D, tn), lambda l, j: (l, 0, j)),
                  pl.BlockSpec((1, 1, tn), lambda l, j: (l, 0, j))],
        out_specs=[pl.BlockSpec((1, N_DEV, tn), lambda l, j: (l, 0, j)), pl.BlockSpec((N_DEV, D), lambda l, j: (0, 0))],
        out_shape=[jax.ShapeDtypeStruct((depth, N_DEV, n), F32), jax.ShapeDtypeStruct((N_DEV, D), F32)],
        compiler_params=_params("arbitrary", "arbitrary"),
    )(c_all, w_ada, b_loc)


def _sum_devices(gathered):
    n = gathered.shape[1]
    tn = _pick(n, (1408, 1024, 640, 512, 128))

    def body(g_ref, o_ref):
        acc = g_ref[0:8, :]
        for dev in range(1, N_DEV):
            acc = acc + g_ref[8 * dev:8 * dev + 8, :]
        o_ref[...] = acc

    return pl.pallas_call(
        body, name="sum_devices", grid=(n // tn,),
        in_specs=[pl.BlockSpec((8 * N_DEV, tn), lambda j: (0, j))], out_specs=pl.BlockSpec((8, tn), lambda j: (0, j)),
        out_shape=jax.ShapeDtypeStruct((8, n), F32), compiler_params=_params("parallel"),
    )(gathered)


def _place():
    x, y, c = lax.axis_index("x"), lax.axis_index("y"), lax.axis_index("c")
    chips = [(1 - x, y), (x, 1 - y), (1 - x, 1 - y)]
    return x, y, c, chips


def _allgather8(block, name):
    m_per, n = block.shape

    def body(x_ref, out_ref, send_sems, recv_sems, local_sem):
        x, y, c, chips = _place()
        me, sibling = (x, y, c), (x, y, 1 - c)

        def rows(px, py, pc):
            return out_ref.at[pl.ds((4 * px + 2 * py + pc) * m_per, m_per), :]

        def copy(k, blk, to, src=None):
            return pltpu.make_async_remote_copy(
                src_ref=rows(*blk) if src is None else src, dst_ref=rows(*blk),
                send_sem=send_sems.at[k], recv_sem=recv_sems.at[k], device_id=to, device_id_type=MESH)

        mine = pltpu.make_async_copy(x_ref, rows(*me), local_sem)
        mine.start()
        first = [copy(0, me, sibling, src=x_ref)]
        first += [copy(1 + j, me, (*chip, c), src=x_ref) for j, chip in enumerate(chips)]
        for cp in first:
            cp.start()
        passed = [copy(4 + j, (*chip, c), sibling) for j, chip in enumerate(chips)]
        for j, chip in enumerate(chips):
            copy(1 + j, (*chip, c), me).wait_recv()
            passed[j].start()
        copy(0, sibling, me).wait_recv()
        for j, chip in enumerate(chips):
            copy(4 + j, (*chip, 1 - c), me).wait_recv()
        for cp in first + passed:
            cp.wait_send()
        mine.wait()

    return pl.pallas_call(
        body, name=name, out_shape=jax.ShapeDtypeStruct((N_DEV * m_per, n), block.dtype),
        in_specs=[pl.BlockSpec(memory_space=pltpu.VMEM)], out_specs=pl.BlockSpec(memory_space=pltpu.VMEM),
        scratch_shapes=[pltpu.SemaphoreType.DMA((7,)), pltpu.SemaphoreType.DMA((7,)), pltpu.SemaphoreType.DMA],
        compiler_params=pltpu.CompilerParams(vmem_limit_bytes=V7X_VMEM_LIMIT),
    )(block)


_SEM = pl.BlockSpec(memory_space=pltpu.SEMAPHORE)
_DATAFLOW = pltpu.SideEffectType.DATAFLOW_SIDE_EFFECTING


def _plan_copies(plan, refs, send_sems, recv_sems):
    return [pltpu.make_async_remote_copy(src_ref=src, dst_ref=dst, send_sem=send_sems.at[i], recv_sem=recv_sems.at[i],
                                         device_id=to, device_id_type=MESH) for i, (src, dst, to) in enumerate(plan(refs))]


def _copies_start(bufs, plan, n_copies, after, name):
    nb = len(bufs)

    def body(*refs):
        for cp in _plan_copies(plan, refs[:nb], refs[nb + 1], refs[nb + 2]):
            cp.start()
        token = refs[-1]
        token[...] = jnp.zeros_like(token)

    sem = pltpu.SemaphoreType.DMA((n_copies,))
    outs = pl.pallas_call(
        body, name=name,
        out_shape=(sem, sem, *[pltpu.HBM(b.shape, b.dtype) for b in bufs], jax.ShapeDtypeStruct((8, 128), F32)),
        in_specs=[_HBM] * nb + [pl.BlockSpec(memory_space=pl.ANY)],
        out_specs=(_SEM, _SEM, *[_HBM] * nb, pl.BlockSpec(memory_space=pltpu.VMEM)),
        input_output_aliases={i: 2 + i for i in range(nb)},
        compiler_params=pltpu.CompilerParams(has_side_effects=_DATAFLOW),
    )(*[pltpu.with_memory_space_constraint(b, pltpu.HBM) for b in bufs], after)
    return outs[0], outs[1], list(outs[2:2 + nb]), outs[-1]


def _copies_wait(started, plan, after, name):
    send_sems, recv_sems, bufs, _ = started
    nb = len(bufs)

    def body(*refs):
        for cp in _plan_copies(plan, refs[:nb], refs[nb], refs[nb + 1]):
            cp.wait_send()
            cp.wait_recv()

    return list(pl.pallas_call(
        body, name=name, out_shape=tuple(pltpu.HBM(b.shape, b.dtype) for b in bufs),
        in_specs=[_HBM] * nb + [_SEM, _SEM, pl.BlockSpec(memory_space=pl.ANY)], out_specs=tuple([_HBM] * nb),
        input_output_aliases={i: i for i in range(nb)},
        compiler_params=pltpu.CompilerParams(has_side_effects=_DATAFLOW),
    )(*bufs, send_sems, recv_sems, after))


def _half_rows(ref, axis, c):
    half = ref.shape[axis] // 2
    return pl.ds(c * half, half)


def _plan_gather_ici(refs):
    n = len(refs) // 2
    x, y, c, chips = _place()
    out = []
    for a in range(n):
        rows = _half_rows(refs[a], 0, c)
        out += [(refs[a].at[rows], refs[n + a].at[2 * x + y, rows], (*chip, c)) for chip in chips]
    return out


def _plan_gather_d2d(refs):
    x, y, c, chips = _place()
    out = []
    for ref in refs:
        rows = _half_rows(ref, 1, c)
        for px, py in chips:
            landed = ref.at[2 * px + py, rows]
            out.append((landed, landed, (x, y, 1 - c)))
    return out


def _plan_rs_sibling(refs):
    n = len(refs) // 2
    x, y, c, _ = _place()
    return [(refs[a].at[pl.ds(0, N_CHIPS), _half_rows(refs[a], 1, 1 - c)], refs[n + a], (x, y, 1 - c)) for a in range(n)]


def _plan_rs_chips(refs):
    n = len(refs) // 2
    x, y, c, chips = _place()
    return [(refs[a].at[2 * px + py], refs[n + a].at[k], (px, py, c)) for a in range(n) for k, (px, py) in enumerate(chips)]


def _plan_rs_share(refs):
    n = len(refs) // 2
    x, y, c, _ = _place()
    return [(refs[a], refs[n + a], (x, y, 1 - c)) for a in range(n)]


def _chip_sum(g, other, sel, name):
    _, half, cdim = other.shape
    tr = _pick(half, (256, 128, 64))
    per = half // tr

    def body(sel_ref, g_ref, t_ref, wire_ref, own_ref):
        total = g_ref[0] + t_ref[0]
        wire_ref[0] = total.astype(BF16)

        @pl.when(pl.program_id(1) == sel_ref[1])
        def _():
            own_ref[...] = total

    blk = pl.BlockSpec((1, tr, cdim), lambda i, p, sel_ref: (p, i, 0))
    return pl.pallas_call(
        body, name=name,
        grid_spec=pltpu.PrefetchScalarGridSpec(
            num_scalar_prefetch=1, grid=(per, N_CHIPS),
            in_specs=[pl.BlockSpec((1, tr, cdim), lambda i, p, sel_ref: (p, sel_ref[0] * per + i, 0)), blk],
            out_specs=[blk, pl.BlockSpec((tr, cdim), lambda i, p, sel_ref: (i, 0))]),
        out_shape=[jax.ShapeDtypeStruct(other.shape, BF16), jax.ShapeDtypeStruct((half, cdim), F32)],
        compiler_params=_params("parallel", "arbitrary"),
    )(sel, g, other)


def _final_sum(own, recv, name):
    r, cdim = own.shape
    tr = _pick(r, (256, 128))

    def body(own_ref, r0_ref, r1_ref, r2_ref, o_ref):
        o_ref[...] = ((own_ref[...] + r0_ref[0].astype(F32)) + r1_ref[0].astype(F32)) + r2_ref[0].astype(F32)

    part = lambda k: pl.BlockSpec((1, tr, cdim), lambda i, k=k: (k, i, 0))
    spec = pl.BlockSpec((tr, cdim), lambda i: (i, 0))
    return pl.pallas_call(
        body, name=name, grid=(r // tr,), in_specs=[spec, part(0), part(1), part(2)], out_specs=spec,
        out_shape=jax.ShapeDtypeStruct((r, cdim), F32), compiler_params=_params("parallel"),
    )(own, recv, recv, recv)


def _row(v):
    return v.reshape(1, -1)


def _tie(v, token):
    return v if token is None else v + token[0:1, 0:1]


def _no_hook(point, after):
    return None


def _layer_fwd(x, w, mod, hook=_no_hook):
    s = x.shape[0]
    nb = s // ATT_BLOCK
    h = _modnorm_fwd(x, _row(w["g_mix_pre"]), mod[0:1], mod[1:2], "mix_pre_fwd")
    z = _mm(h, w["w_all"], name="mm_in")
    qkv = z[:, Z_QKV:Z_PC].astype(BF16)
    fl = z[:, Z_FL:Z_COLS]
    cum = _cumf_fwd(fl, w["b_f_pad"])
    fr = cum[:, :HEADS].T.reshape(HEADS, nb, ATT_BLOCK)
    br_a, lse = _attn_fwd(qkv, cum, fr)
    br_b, br_c = _poolconv_fwd(z, w["w_pool_bd"], _tie(_row(w["pool_scale"]), hook("attn", lse)), w["conv_w"])
    wbr = w["w_branch"]
    pa = _mm(br_a, wbr[:A_WIDTH], name="mm_br_a")
    pb = _mm(br_b, wbr[A_WIDTH:A_WIDTH + POOL_WIDTH], name="mm_br_b")
    pc = _mm(br_c, wbr[A_WIDTH + POOL_WIDTH:], name="mm_br_c")
    merged = _merge_fwd(z, pa, pb, pc)
    y = _mm(merged, w["w_out"], name="mm_out")
    x1 = _post_fwd(x, y, _row(w["g_mix_post"]), mod[2:3], "mix_post_fwd")
    h2 = _modnorm_fwd(x1, _row(w["g_ff_pre"]), mod[3:4], mod[4:5], "ff_pre_fwd")
    a = _mm(h2, w["w_ff1"], b_split=N_CHIPS, name="mm_ff1")
    r = _act_fwd(a)
    y2 = _mm(r, w["w_ff2"], name="mm_ff2")
    x2 = _post_fwd(x1, y2, _row(w["g_ff_post"]), mod[5:6], "ff_post_fwd")
    hook("end", x2)
    saved = dict(x=x, h=h, z=z, qkv=qkv, fl=fl, cum=cum, fr=fr, lse=lse, br_a=br_a, br_b=br_b, br_c=br_c, pa=pa, pb=pb, pc=pc,
                 merged=merged, y=y, x1=x1, h2=h2, a=a, r=r, y2=y2)
    return x2, saved


def _layer_bwd(dx2, sv, w, mod, hook=_no_hook):
    s = dx2.shape[0]
    dy2, sum_ff_post = _post_bwd(dx2, sv["y2"], _row(w["g_ff_post"]), mod[5:6], "ff_post_bwd")
    dr = _mm(dy2, w["w_ff2"], tb=True, name="mm_ff2_dx")
    d_w_ff2 = _mm(sv["r"], dy2, ta=True, name="mm_ff2_dw")
    da = _act_bwd(dr, sv["a"])
    dh2 = _mm(da, w["w_ff1"], tb=True, b_split=N_CHIPS, name="mm_ff1_dx")
    d_w_ff1 = _mm(sv["h2"], da, ta=True, out_split=N_CHIPS, name="mm_ff1_dw")
    dx1, sum_ff_pre = _modnorm_bwd(dh2, sv["x1"], dx2, _tie(_row(w["g_ff_pre"]), hook("ff_pre", dh2)), mod[4:5], "ff_pre_bwd")

    dy, sum_mix_post = _post_bwd(dx1, sv["y"], _row(w["g_mix_post"]), mod[2:3], "mix_post_bwd")
    dmerged = _mm(dy, w["w_out"], tb=True, name="mm_out_dx")
    d_w_out = _mm(sv["merged"], dy, ta=True, name="mm_out_dw")
    dgl, dpa, dpb, dpc = _merge_bwd(dmerged, sv["z"], sv["pa"], sv["pb"], sv["pc"])
    wbr = w["w_branch"]
    dbr_a = _mm(dpa, wbr[:A_WIDTH], tb=True, name="mm_br_a_dx")
    dbr_b = _mm(dpb, wbr[A_WIDTH:A_WIDTH + POOL_WIDTH], tb=True, name="mm_br_b_dx")
    dbr_c = _mm(dpc, wbr[A_WIDTH + POOL_WIDTH:], tb=True, name="mm_br_c_dx")
    d_w_branch = jnp.concatenate([_mm(sv["br_a"], dpa, ta=True, name="mm_br_a_dw"), _mm(sv["br_b"], dpb, ta=True, name="mm_br_b_dw"),
                                  _mm(sv["br_c"], dpc, ta=True, name="mm_br_c_dw")], axis=0)

    dq, dk, dv, dfc, dfr = _attn_bwd(sv["qkv"], dbr_a, sv["br_a"], sv["lse"], sv["cum"], sv["fr"])
    dcum = dfc + jnp.pad(dfr.reshape(HEADS, s).T, ((0, 0), (0, 128 - HEADS)))
    dfl, sum_bf = _cumf_bwd(dcum, sv["fl"], _tie(w["b_f_pad"], hook("cumf", dfc)))
    dpc_z, d_wbd, sum_ps, sum_cw = _poolconv_bwd(dbr_b, dbr_c, sv["z"], w["w_pool_bd"], _row(w["pool_scale"]), w["conv_w"])
    dz = jnp.concatenate([dgl, dq, dk, dv, dpc_z, dfl], axis=1)
    dh = _mm(dz, w["w_all"], tb=True, name="mm_in_dx")
    d_w_all = _mm(sv["h"], dz, ta=True, name="mm_in_dw")
    hook("mix_pre", dh)
    dx, sum_mix_pre = _modnorm_bwd(dh, sv["x"], dx1, _row(w["g_mix_pre"]), mod[1:2], "mix_pre_bwd")

    dmod = jnp.stack([sum_mix_pre[0], sum_mix_pre[1], sum_mix_post[0], sum_ff_pre[0], sum_ff_pre[1], sum_ff_post[0]])
    d_w_in = _w_in_shards(d_w_all)
    d_w_pool = jnp.stack([d_wbd[64 * g:64 * g + 64, 64 * g:64 * g + 64] for g in range(4)])
    big = dict(w_in=d_w_in, w_branch=d_w_branch, w_out=d_w_out, w_ff1=d_w_ff1, w_ff2=d_w_ff2)
    small = dict(g_mix_pre=sum_mix_pre[2], g_mix_post=sum_mix_post[1], g_ff_pre=sum_ff_pre[2], g_ff_post=sum_ff_post[1],
                 b_f=sum_bf[0, :HEADS], w_pool=d_w_pool, pool_scale=sum_ps[0], conv_w=sum_cw[0:3])
    return dx, dmod, big, small


_QKV_END, _FL_END, _PC_END = 3 * A_WIDTH, 3 * A_WIDTH + HEADS, 3 * A_WIDTH + HEADS + POOL_WIDTH + 3 * CONV_WIDTH
_W_IN_GROUPS = ((_PC_END, IN_COLS, Z_GL), (0, _QKV_END, Z_QKV), (_FL_END, _PC_END, Z_PC), (_QKV_END, _FL_END, Z_FL))
_SHARD_COLS = IN_COLS // N_CHIPS


def _w_all_from_shards(blocks):
    pieces = []
    for lo, hi, _ in _W_IN_GROUPS:
        for p in range(N_CHIPS):
            a, b = max(lo, p * _SHARD_COLS), min(hi, (p + 1) * _SHARD_COLS)
            if a < b:
                pieces.append(blocks[p][:, a - p * _SHARD_COLS:b - p * _SHARD_COLS])
    pieces.append(jnp.zeros((D, Z_COLS - IN_COLS), blocks.dtype))
    return jnp.concatenate(pieces, axis=1)


def _w_in_shards(d_w_all):
    blocks = []
    for p in range(N_CHIPS):
        pieces = []
        for lo, hi, at in sorted(_W_IN_GROUPS):
            a, b = max(lo, p * _SHARD_COLS), min(hi, (p + 1) * _SHARD_COLS)
            if a < b:
                pieces.append(d_w_all[:, at + a - lo:at + b - lo])
        blocks.append(jnp.concatenate(pieces, axis=1))
    return jnp.stack(blocks)


def _full_layer_weights(w_in_blocks, w_branch, w_out, w_ff1, w_ff2, g_mix_pre, g_mix_post, g_ff_pre, g_ff_post, b_f, w_pool, pool_scale, conv_w):
    w_all = _w_all_from_shards(w_in_blocks)
    wbd = jnp.zeros((POOL_WIDTH, POOL_WIDTH), F32)
    for g in range(4):
        wbd = wbd.at[64 * g:64 * g + 64, 64 * g:64 * g + 64].set(w_pool[g])
    return dict(w_all=w_all, w_branch=w_branch, w_out=w_out, w_ff1=w_ff1, w_ff2=w_ff2, g_mix_pre=g_mix_pre, g_mix_post=g_mix_post,
                g_ff_pre=g_ff_pre, g_ff_post=g_ff_post, b_f_pad=jnp.pad(b_f, (0, 128 - HEADS)).reshape(1, 128), w_pool_bd=wbd,
                pool_scale=pool_scale, conv_w=conv_w)


class _NoComm:
    def layer_weights(self, l):
        raise NotImplementedError

    def fwd_hook(self, l):
        return _no_hook

    def bwd_hook(self, l):
        return _no_hook

    def grads_ready(self, l, big):
        return None


class _Layers(_NoComm):
    def __init__(self, layers):
        self.layers = layers

    def layer_weights(self, l):
        return self.layers[l]


def _local_step(x, target, mods, comm):
    saved, weights = [], []
    act = x
    for l in range(DEPTH):
        weights.append(comm.layer_weights(l))
        act, sv = _layer_fwd(act, weights[l], mods[l], comm.fwd_hook(l))
        saved.append(sv)
    dact, sq = _loss_head(act, target)
    loss = sq[0, 0] * (0.5 / D)
    dmods, bigs, smalls = [None] * DEPTH, [None] * DEPTH, [None] * DEPTH
    token = None
    for l in reversed(range(DEPTH)):
        dact, dmods[l], bigs[l], smalls[l] = _layer_bwd(dact, saved[l], weights[l], _tie(mods[l], token), comm.bwd_hook(l))
        token = comm.grads_ready(l, bigs[l])
    return loss, dact, jnp.stack(dmods), bigs, smalls


_BIG = ("w_in", "w_branch", "w_out", "w_ff1", "w_ff2")
N_BIG = len(_BIG)


class _StepComm(_NoComm):
    def __init__(self, shards, chip, ci):
        self.chip, self.ci = chip, ci
        self.sel = jnp.stack([ci, chip]).astype(jnp.int32)
        self.small, self.first_after = None, None
        self.grads = [None] * DEPTH
        self.rs = [dict() for _ in range(DEPTH)]
        self.gather, self.landed = [], [None] * DEPTH
        token = self.sel
        for l in range(DEPTH):
            lands = [lax.dynamic_update_index_in_dim(lax.empty((N_CHIPS,) + s.shape, s.dtype), s, chip, 0) for s in shards[l]]
            self.gather.append(_copies_start(list(shards[l]) + lands, _plan_gather_ici, 3 * N_BIG, token, "gather_ici_start_%d" % l))
            token = self.gather[l][3]
        self.start_token = token

    def _gather_pass_on(self, l, after):
        bufs = _copies_wait(self.gather[l], _plan_gather_ici, after, "gather_ici_wait_%d" % l)
        self.gather[l] = _copies_start(bufs[N_BIG:], _plan_gather_d2d, 3 * N_BIG, bufs[0], "gather_d2d_start_%d" % l)
        return self.gather[l][3]

    def _gather_done(self, l, after):
        self.landed[l] = _copies_wait(self.gather[l], _plan_gather_d2d, after, "gather_d2d_wait_%d" % l)

    def layer_weights(self, l):
        if l == 0:
            self._gather_done(0, self._gather_pass_on(0, self.first_after))
        g_in, g_br, g_out, g_f1, g_f2 = self.landed[l]
        return _full_layer_weights(g_in, g_br.reshape(D, D), g_out.reshape(D, D), g_f1, g_f2.reshape(D_FF, D), *self.small[l])

    def fwd_hook(self, l):
        if l != 0:
            return _no_hook

        def hook(point, after):
            if point == "attn":
                return self._gather_pass_on(1, after)
            self._gather_done(1, after)
            return None
        return hook

    def _rs_start(self, l, big):
        grads = [big[name] if big[name].ndim == 3 else big[name].reshape(N_CHIPS, -1, big[name].shape[1]) for name in _BIG]
        lands = [lax.empty((N_CHIPS, g.shape[1] // 2, g.shape[2]), F32) for g in grads]
        self.rs[l]["sibling"] = _copies_start(grads + lands, _plan_rs_sibling, N_BIG, self.sel, "rs_sibling_start_%d" % l)
        return self.rs[l]["sibling"][3]

    def _rs_chip_sums(self, l, after):
        bufs = _copies_wait(self.rs[l]["sibling"], _plan_rs_sibling, after, "rs_sibling_wait_%d" % l)
        wires, owns = zip(*[_chip_sum(bufs[i], bufs[N_BIG + i], self.sel, "rs_chip_sum_" + name) for i, name in enumerate(_BIG)])
        lands = [lax.empty((3,) + w.shape[1:], BF16) for w in wires]
        self.rs[l]["owns"] = owns
        self.rs[l]["chips"] = _copies_start(list(wires) + lands, _plan_rs_chips, 3 * N_BIG, owns[0], "rs_chips_start_%d" % l)
        return self.rs[l]["chips"][3]

    def _rs_final_sums(self, l, after):
        bufs = _copies_wait(self.rs[l]["chips"], _plan_rs_chips, after, "rs_chips_wait_%d" % l)
        sums = [_final_sum(self.rs[l]["owns"][i], bufs[N_BIG + i], "rs_final_" + name) for i, name in enumerate(_BIG)]
        lands = [lax.empty(s.shape, F32) for s in sums]
        self.rs[l]["share"] = _copies_start(sums + lands, _plan_rs_share, N_BIG, sums[0], "rs_share_start_%d" % l)
        return self.rs[l]["share"][3]

    def _rs_done(self, l, after):
        bufs = _copies_wait(self.rs[l]["share"], _plan_rs_share, after, "rs_share_wait_%d" % l)
        self.grads[l] = [jnp.where(self.ci == 0, jnp.concatenate([mine, theirs]), jnp.concatenate([theirs, mine]))
                         for mine, theirs in zip(bufs[:N_BIG], bufs[N_BIG:])]

    def bwd_hook(self, l):
        if l != 0:
            return _no_hook

        def hook(point, after):
            if point == "ff_pre":
                return self._rs_chip_sums(1, after)
            if point == "cumf":
                return self._rs_final_sums(1, after)
            self._rs_done(1, after)
            return None
        return hook

    def grads_ready(self, l, big):
        token = self._rs_start(l, big)
        if l == 0:
            self.tail_token = self._rs_chip_sums(0, token)
        return token

    def finish(self, after):
        self._rs_done(0, self._rs_final_sums(0, after))


_SMALL = ("g_mix_pre", "g_mix_post", "g_ff_pre", "g_ff_post", "b_f", "w_pool", "pool_scale", "conv_w")


def _pack(parts, rows=8):
    flat = jnp.concatenate([p.reshape(-1) for p in parts])
    width = -(-flat.shape[0] // (rows * 128)) * 128
    return jnp.pad(flat, (0, rows * width - flat.shape[0])).reshape(rows, width)


def _unpack(packed, like):
    flat = packed.reshape(-1)
    out, at = [], 0
    for ref in like:
        out.append(flat[at:at + ref.size].reshape(ref.shape))
        at += ref.size
    return out


def kernel(x, c, w_ada, b_ada, g_mix_pre, g_mix_post, g_ff_pre, g_ff_post, w_in, b_f, w_pool, pool_scale, conv_w, w_branch, w_out, w_ff1, w_ff2, loss_target, m_w_ada, m_b_ada, m_g_mix_pre, m_g_mix_post, m_g_ff_pre, m_g_ff_post, m_w_in, m_b_f, m_w_pool, m_pool_scale, m_conv_w, m_w_branch, m_w_out, m_w_ff1, m_w_ff2, v_w_ada, v_b_ada, v_g_mix_pre, v_g_mix_post, v_g_ff_pre, v_g_ff_post, v_w_in, v_b_f, v_w_pool, v_pool_scale, v_conv_w, v_w_branch, v_w_out, v_w_ff1, v_w_ff2):
    xi, yi, ci = lax.axis_index("x"), lax.axis_index("y"), lax.axis_index("c")
    chip = 2 * xi + yi
    dev = 2 * chip + ci
    n_ada = w_ada.shape[2]

    comm = _StepComm([[w[l].astype(BF16) for w in (w_in, w_branch, w_out, w_ff1, w_ff2)] for l in range(DEPTH)], chip, ci)

    first = jnp.zeros((8, D + 384), F32).at[0, :D].set(c[0]).at[0, D:].set(conv_w.reshape(-1))
    got = _allgather8(_tie(first, comm.start_token), "gather_cond").reshape(N_DEV, 8, D + 384)[:, 0]
    c_all = got[:, :D]
    conv_full = got[0::2, D:].reshape(N_CHIPS, DEPTH, 3, CONV_WIDTH // N_CHIPS).transpose(1, 2, 0, 3).reshape(DEPTH, 3, CONV_WIDTH)

    b_loc = lax.dynamic_slice_in_dim(b_ada, chip * n_ada, n_ada, axis=1).reshape(DEPTH, 1, n_ada)
    mod_cols, silu_c = _ada_fwd(c_all, w_ada, b_loc)
    got = _allgather8(mod_cols.reshape(DEPTH * N_DEV, n_ada), "gather_mod").reshape(N_DEV, DEPTH, N_DEV, n_ada)[0::2]
    mod_all = got.transpose(1, 2, 0, 3).reshape(DEPTH, N_DEV, 6, D)
    mods = lax.dynamic_index_in_dim(mod_all, dev, axis=1, keepdims=False)

    comm.small = [(g_mix_pre[l], g_mix_post[l], g_ff_pre[l], g_ff_post[l], b_f[l], w_pool[l], pool_scale[l], conv_full[l]) for l in range(DEPTH)]
    comm.first_after = mods
    loss_part, grad_x, dmods, bigs, smalls = _local_step(x[0], loss_target[0], mods, comm)
    loss = lax.psum(loss_part, ("x", "y", "c"))

    small_parts = [smalls[l][name] for name in _SMALL for l in range(DEPTH)]
    packed = _tie(_pack([dmods] + small_parts), comm.tail_token)
    gathered = _allgather8(packed, "gather_small")
    dmod_all = gathered.reshape(N_DEV, -1)[:, :dmods.size].reshape(N_DEV, DEPTH, 6 * D)
    summed = _unpack(_sum_devices(gathered), [dmods] + small_parts)
    grad_b_ada = summed[0].reshape(DEPTH, 6 * D)
    small_grads = {name: jnp.stack(summed[1 + 2 * i:3 + 2 * i]) for i, name in enumerate(_SMALL)}
    small_grads["conv_w"] = lax.dynamic_slice_in_dim(small_grads["conv_w"], chip * (CONV_WIDTH // N_CHIPS), CONV_WIDTH // N_CHIPS, axis=2)

    dmod_loc = lax.dynamic_slice_in_dim(dmod_all.transpose(1, 0, 2), chip * n_ada, n_ada, axis=2)
    silu_pad = jnp.pad(silu_c, ((0, 128 - N_DEV), (0, 0)))
    grad_w_ada = jnp.stack([_mm(silu_pad, jnp.pad(dmod_loc[l], ((0, 128 - N_DEV), (0, 0))), ta=True, name="mm_ada_dw") for l in range(DEPTH)])

    grads = dict(w_ada=grad_w_ada, b_ada=grad_b_ada, **small_grads)
    weights = dict(w_ada=w_ada, b_ada=b_ada, g_mix_pre=g_mix_pre, g_mix_post=g_mix_post, g_ff_pre=g_ff_pre, g_ff_post=g_ff_post, w_in=w_in,
                   b_f=b_f, w_pool=w_pool, pool_scale=pool_scale, conv_w=conv_w, w_branch=w_branch, w_out=w_out, w_ff1=w_ff1, w_ff2=w_ff2)
    m_in = dict(w_ada=m_w_ada, b_ada=m_b_ada, g_mix_pre=m_g_mix_pre, g_mix_post=m_g_mix_post, g_ff_pre=m_g_ff_pre, g_ff_post=m_g_ff_post,
                w_in=m_w_in, b_f=m_b_f, w_pool=m_w_pool, pool_scale=m_pool_scale, conv_w=m_conv_w, w_branch=m_w_branch, w_out=m_w_out,
                w_ff1=m_w_ff1, w_ff2=m_w_ff2)
    v_in = dict(w_ada=v_w_ada, b_ada=v_b_ada, g_mix_pre=v_g_mix_pre, g_mix_post=v_g_mix_post, g_ff_pre=v_g_ff_pre, g_ff_post=v_g_ff_post,
                w_in=v_w_in, b_f=v_b_f, w_pool=v_w_pool, pool_scale=v_pool_scale, conv_w=v_conv_w, w_branch=v_w_branch, w_out=v_w_out,
                w_ff1=v_w_ff1, w_ff2=v_w_ff2)
    order = ("w_ada", "b_ada", "g_mix_pre", "g_mix_post", "g_ff_pre", "g_ff_post", "w_in", "b_f", "w_pool", "pool_scale", "conv_w",
             "w_branch", "w_out", "w_ff1", "w_ff2")
    delta, new_m, new_v = {}, {}, {}
    tiny = ("b_ada",) + _SMALL
    res = _adamw(*[_pack([src[name] for name in tiny]) for src in (weights, grads, m_in, v_in)], "adamw_small")
    for out, packed_res in zip((delta, new_m, new_v), res):
        for name, val in zip(tiny, _unpack(packed_res, [weights[name] for name in tiny])):
            out[name] = val
    delta["w_ada"], new_m["w_ada"], new_v["w_ada"] = _adamw(w_ada, grad_w_ada, m_w_ada, v_w_ada, "adamw_w_ada")
    comm.finish(delta["w_ada"])
    for i, name in enumerate(_BIG):
        grads[name] = jnp.stack([comm.grads[l][i] for l in range(DEPTH)])
        delta[name], new_m[name], new_v[name] = _adamw(weights[name], grads[name], m_in[name], v_in[name], "adamw_" + name)

    return (loss, grad_x[None], *[grads[n] for n in order], *[delta[n] for n in order], *[new_m[n] for n in order],
            *[new_v[n] for n in order])
```

```python
import functools

import jax
import jax.numpy as jnp
from jax import lax
from jax.experimental import pallas as pl
from jax.experimental.pallas import tpu as pltpu

F32 = jnp.float32
BF16 = jnp.bfloat16
MESH = pl.DeviceIdType.MESH

D = 1024
DEPTH = 2
HEADS = 8
HEAD_DIM = 64
A_WIDTH = 512
POOL_WIDTH = 256
CONV_WIDTH = 256
D_FF = 4096
IN_COLS = 5640
Z_GL, Z_QKV, Z_PC, Z_FL, Z_COLS = 0, 3072, 4608, 5632, 5760
RMS_EPS = 1e-6
NEG_INF = -1e30
ATT_BLOCK = 256
ROW_TILE = 256
N_CHIPS = 4
N_DEV = 8
V7X_VMEM_LIMIT = 48 * 1024 * 1024

ADAM_LR = 0.001
ADAM_B1 = 0.9
ADAM_B2 = 0.999
ADAM_EPS = 1e-08
ADAM_WD = 0.01
ADAM_STEP = 10

_HBM = pl.BlockSpec(memory_space=pltpu.HBM)


def _params(*sem):
    return pltpu.CompilerParams(dimension_semantics=sem, vmem_limit_bytes=V7X_VMEM_LIMIT)


def _pick(dim, cands):
    for cand in cands:
        if dim % cand == 0:
            return cand
    return dim


def _mm(a, b, *, ta=False, tb=False, b_split=1, out_split=1, out_dtype=F32, name):
    (k, m) = a.shape if ta else a.shape[::-1]
    b_rows, b_cols = b.shape[-2], b.shape[-1] * b_split
    (n, k2) = (b_rows, b_cols) if tb else (b_cols, b_rows)
    assert k == k2, (a.shape, b.shape, ta, tb)
    n_unit = n // (out_split * (1 if tb else b_split))
    k_unit = k // (b_split if tb else 1)
    tm = _pick(m, (1024, 512, 256, 128))
    tn = _pick(n_unit, (1024, 1152, 768, 640, 512, 256, 128))
    tk = _pick(k_unit, (1024, 1152, 512, 640, 256, 128))
    nk = k // tk
    dims = (((0 if ta else 1,), (1 if tb else 0,)), ((), ()))

    def dot(a_ref, b_ref):
        b_val = b_ref[0] if b_split > 1 else b_ref[...]
        return lax.dot_general(a_ref[...].astype(BF16), b_val.astype(BF16), dims, preferred_element_type=F32)

    def put(o_ref, val):
        if out_split > 1:
            o_ref[0] = val.astype(o_ref.dtype)
        else:
            o_ref[...] = val.astype(o_ref.dtype)

    def body_single(a_ref, b_ref, o_ref):
        put(o_ref, dot(a_ref, b_ref))

    def body_acc(a_ref, b_ref, o_ref, acc_ref):
        kk = pl.program_id(2)

        @pl.when(kk == 0)
        def _():
            acc_ref[...] = jnp.zeros_like(acc_ref)

        acc_ref[...] += dot(a_ref, b_ref)

        @pl.when(kk == nk - 1)
        def _():
            put(o_ref, acc_ref[...])

    a_spec = pl.BlockSpec((tk, tm), lambda i, j, kk: (kk, i)) if ta else pl.BlockSpec((tm, tk), lambda i, j, kk: (i, kk))
    if b_split == 1:
        b_spec = pl.BlockSpec((tn, tk), lambda i, j, kk: (j, kk)) if tb else pl.BlockSpec((tk, tn), lambda i, j, kk: (kk, j))
    elif tb:
        per = k_unit // tk
        b_spec = pl.BlockSpec((1, tn, tk), lambda i, j, kk: (kk // per, j, kk % per))
    else:
        per = n // b_split // tn
        b_spec = pl.BlockSpec((1, tk, tn), lambda i, j, kk: (j // per, kk, j % per))
    if out_split == 1:
        o_spec = pl.BlockSpec((tm, tn), lambda i, j, kk: (i, j))
        o_shape = jax.ShapeDtypeStruct((m, n), out_dtype)
    else:
        per_o = n // out_split // tn
        o_spec = pl.BlockSpec((1, tm, tn), lambda i, j, kk: (j // per_o, i, j % per_o))
        o_shape = jax.ShapeDtypeStruct((out_split, m, n // out_split), out_dtype)
    return pl.pallas_call(
        body_single if nk == 1 else body_acc, name=name, grid=(m // tm, n // tn, nk),
        in_specs=[a_spec, b_spec], out_specs=o_spec, out_shape=o_shape,
        scratch_shapes=[] if nk == 1 else [pltpu.VMEM((tm, tn), F32)],
        compiler_params=_params("parallel", "parallel", "arbitrary"),
    )(a, b)


def _ew(fn, ins, out_dtypes, name, tc=None):
    shape = ins[0].shape
    lead, (rows, cols) = shape[:-2], shape[-2:]
    tr = _pick(rows, (ROW_TILE, 128, 8))
    tc = cols if tc is None else tc
    n_in = len(ins)

    def body(*refs):
        res = fn(*[r[...] for r in refs[:n_in]])
        for o_ref, val in zip(refs[n_in:], res):
            o_ref[...] = val.astype(o_ref.dtype)

    if lead:
        spec = pl.BlockSpec((None, tr, tc), lambda l, i, j: (l, i, j))
    else:
        spec = pl.BlockSpec((tr, tc), lambda i, j: (i, j))
    return pl.pallas_call(
        body, name=name, grid=lead + (rows // tr, cols // tc),
        in_specs=[spec] * n_in, out_specs=[spec] * len(out_dtypes),
        out_shape=[jax.ShapeDtypeStruct(shape, dt) for dt in out_dtypes],
        compiler_params=_params(*(["parallel"] * (len(lead) + 2))),
    )(*ins)


def _act_fwd(a):
    def fn(a):
        r = jnp.maximum(a, 0.0)
        return (r * r,)
    return _ew(fn, [a], [BF16], "act_fwd", tc=2048)[0]


def _act_bwd(dr, a):
    def fn(dr, a):
        return (dr * (2.0 * jnp.maximum(a, 0.0)),)
    return _ew(fn, [dr, a], [BF16], "act_bwd", tc=2048)[0]


def _adamw(w, g, m, v, name):
    bc1 = 1.0 - ADAM_B1 ** ADAM_STEP
    bc2 = 1.0 - ADAM_B2 ** ADAM_STEP

    def fn(w, g, m, v):
        m = ADAM_B1 * m + (1.0 - ADAM_B1) * g
        v = ADAM_B2 * v + (1.0 - ADAM_B2) * (g * g)
        m_hat = m / bc1
        v_hat = v / bc2
        delta = -ADAM_LR * (m_hat / (jnp.sqrt(v_hat) + ADAM_EPS) + ADAM_WD * w)
        return delta, m, v
    return _ew(fn, [w, g, m, v], [F32, F32, F32], name)


def _row_spec(cols, block=0):
    return pl.BlockSpec((ROW_TILE, cols), lambda i, block=block: (i, block))


def _vec_spec(cols):
    return pl.BlockSpec((1, cols), lambda i: (0, 0))


def _sum_spec(cols):
    return pl.BlockSpec((8, cols), lambda i: (0, 0))


def _rstd(x):
    return lax.rsqrt(jnp.mean(x * x, axis=-1, keepdims=True) + RMS_EPS)


def _modnorm_fwd(x, g, shift, scale, name):
    s = x.shape[0]

    def body(x_ref, g_ref, sh_ref, sc_ref, h_ref):
        xv = x_ref[...]
        n = xv * _rstd(xv)
        h_ref[...] = ((n * g_ref[...]) * (1.0 + sc_ref[...]) + sh_ref[...]).astype(BF16)

    return pl.pallas_call(
        body, name=name, grid=(s // ROW_TILE,),
        in_specs=[_row_spec(D), _vec_spec(D), _vec_spec(D), _vec_spec(D)], out_specs=_row_spec(D),
        out_shape=jax.ShapeDtypeStruct((s, D), BF16), compiler_params=_params("parallel"),
    )(x, g, shift, scale)


def _post_fwd(x, y, g, gate, name):
    s = x.shape[0]

    def body(x_ref, y_ref, g_ref, gate_ref, o_ref):
        yv = y_ref[...]
        o_ref[...] = x_ref[...] + gate_ref[...] * ((yv * _rstd(yv)) * g_ref[...])

    return pl.pallas_call(
        body, name=name, grid=(s // ROW_TILE,),
        in_specs=[_row_spec(D), _row_spec(D), _vec_spec(D), _vec_spec(D)], out_specs=_row_spec(D),
        out_shape=jax.ShapeDtypeStruct((s, D), F32), compiler_params=_params("parallel"),
    )(x, y, g, gate)


def _post_bwd(dxo, y, g, gate, name):
    s = dxo.shape[0]

    def body(d_ref, y_ref, g_ref, gate_ref, dy_ref, sum_ref):
        @pl.when(pl.program_id(0) == 0)
        def _():
            sum_ref[...] = jnp.zeros_like(sum_ref)

        dv, yv = d_ref[...], y_ref[...]
        r = _rstd(yv)
        n = yv * r
        sum_ref[0:1, :] += jnp.sum(dv * (n * g_ref[...]), axis=0, keepdims=True)
        sum_ref[1:2, :] += jnp.sum((dv * gate_ref[...]) * n, axis=0, keepdims=True)
        dn = (dv * gate_ref[...]) * g_ref[...]
        dy_ref[...] = (r * (dn - n * jnp.mean(dn * n, axis=-1, keepdims=True))).astype(BF16)

    return pl.pallas_call(
        body, name=name, grid=(s // ROW_TILE,),
        in_specs=[_row_spec(D), _row_spec(D), _vec_spec(D), _vec_spec(D)],
        out_specs=[_row_spec(D), _sum_spec(D)],
        out_shape=[jax.ShapeDtypeStruct((s, D), BF16), jax.ShapeDtypeStruct((8, D), F32)],
        compiler_params=_params("arbitrary"),
    )(dxo, y, g, gate)


def _modnorm_bwd(dh, x, dxo, g, scale, name):
    s = dh.shape[0]

    def body(dh_ref, x_ref, d_ref, g_ref, sc_ref, dx_ref, sum_ref):
        @pl.when(pl.program_id(0) == 0)
        def _():
            sum_ref[...] = jnp.zeros_like(sum_ref)

        dhv, xv = dh_ref[...], x_ref[...]
        r = _rstd(xv)
        n = xv * r
        one_sc = 1.0 + sc_ref[...]
        sum_ref[0:1, :] += jnp.sum(dhv, axis=0, keepdims=True)
        sum_ref[1:2, :] += jnp.sum(dhv * (n * g_ref[...]), axis=0, keepdims=True)
        sum_ref[2:3, :] += jnp.sum((dhv * one_sc) * n, axis=0, keepdims=True)
        dn = (dhv * one_sc) * g_ref[...]
        dx_ref[...] = d_ref[...] + r * (dn - n * jnp.mean(dn * n, axis=-1, keepdims=True))

    return pl.pallas_call(
        body, name=name, grid=(s // ROW_TILE,),
        in_specs=[_row_spec(D), _row_spec(D), _row_spec(D), _vec_spec(D), _vec_spec(D)],
        out_specs=[_row_spec(D), _sum_spec(D)],
        out_shape=[jax.ShapeDtypeStruct((s, D), F32), jax.ShapeDtypeStruct((8, D), F32)],
        compiler_params=_params("arbitrary"),
    )(dh, x, dxo, g, scale)


def _loss_head(y, target):
    s = y.shape[0]

    def body(y_ref, t_ref, dy_ref, sum_ref):
        @pl.when(pl.program_id(0) == 0)
        def _():
            sum_ref[...] = jnp.zeros_like(sum_ref)

        err = y_ref[...] - t_ref[...]
        dy_ref[...] = err * (1.0 / D)
        sum_ref[...] += jnp.sum(err * err)

    return pl.pallas_call(
        body, name="loss_head", grid=(s // ROW_TILE,),
        in_specs=[_row_spec(D), _row_spec(D)],
        out_specs=[_row_spec(D), pl.BlockSpec((8, 128), lambda i: (0, 0))],
        out_shape=[jax.ShapeDtypeStruct((s, D), F32), jax.ShapeDtypeStruct((8, 128), F32)],
        compiler_params=_params("arbitrary"),
    )(y, target)


def _merge_fwd(z, pa, pb, pc):
    s = z.shape[0]

    def body(g0_ref, g1_ref, g2_ref, pa_ref, pb_ref, pc_ref, o_ref):
        o_ref[...] = (jax.nn.sigmoid(g0_ref[...]) * pa_ref[...] + jax.nn.sigmoid(g1_ref[...]) * pb_ref[...]
                      + jax.nn.sigmoid(g2_ref[...]) * pc_ref[...]).astype(BF16)

    return pl.pallas_call(
        body, name="merge_fwd", grid=(s // ROW_TILE,),
        in_specs=[_row_spec(D, 0), _row_spec(D, 1), _row_spec(D, 2), _row_spec(D), _row_spec(D), _row_spec(D)],
        out_specs=_row_spec(D), out_shape=jax.ShapeDtypeStruct((s, D), BF16),
        compiler_params=_params("parallel"),
    )(z, z, z, pa, pb, pc)


def _merge_bwd(dm, z, pa, pb, pc):
    s = z.shape[0]

    def body(dm_ref, g0_ref, g1_ref, g2_ref, pa_ref, pb_ref, pc_ref, dgl_ref, da_ref, db_ref, dc_ref):
        dmv = dm_ref[...]
        for i, (g_ref, p_ref, d_ref) in enumerate(((g0_ref, pa_ref, da_ref), (g1_ref, pb_ref, db_ref), (g2_ref, pc_ref, dc_ref))):
            gate = jax.nn.sigmoid(g_ref[...])
            dgl_ref[:, i * D:(i + 1) * D] = ((dmv * p_ref[...]) * (gate * (1.0 - gate))).astype(BF16)
            d_ref[...] = (dmv * gate).astype(BF16)

    return pl.pallas_call(
        body, name="merge_bwd", grid=(s // ROW_TILE,),
        in_specs=[_row_spec(D), _row_spec(D, 0), _row_spec(D, 1), _row_spec(D, 2), _row_spec(D), _row_spec(D), _row_spec(D)],
        out_specs=[_row_spec(3 * D), _row_spec(D), _row_spec(D), _row_spec(D)],
        out_shape=[jax.ShapeDtypeStruct((s, 3 * D), BF16)] + [jax.ShapeDtypeStruct((s, D), BF16)] * 3,
        compiler_params=_params("parallel"),
    )(dm, z, z, z, pa, pb, pc)


def _shift_down(v, n):
    row = lax.broadcasted_iota(jnp.int32, v.shape, 0)
    return jnp.where(row >= n, pltpu.roll(v, n, axis=0), 0.0)


def _shift_up(v, n):
    s = v.shape[0]
    row = lax.broadcasted_iota(jnp.int32, v.shape, 0)
    return jnp.where(row < s - n, pltpu.roll(v, s - n, axis=0), 0.0)


def _log_sigmoid(v):
    return jnp.minimum(v, 0.0) - jnp.log1p(jnp.exp(-jnp.abs(v)))


def _cumf_fwd(fl, bias):
    s = fl.shape[0]

    def body(fl_ref, b_ref, o_ref):
        acc = _log_sigmoid(fl_ref[...] + b_ref[...])
        step = 1
        while step < s:
            acc = acc + _shift_down(acc, step)
            step *= 2
        o_ref[...] = acc

    return pl.pallas_call(body, name="cumf_fwd", out_shape=jax.ShapeDtypeStruct((s, 128), F32),
                          compiler_params=pltpu.CompilerParams(vmem_limit_bytes=V7X_VMEM_LIMIT))(fl, bias)


def _cumf_bwd(dcum, fl, bias):
    s = fl.shape[0]

    def body(d_ref, fl_ref, b_ref, dfl_ref, db_ref):
        acc = d_ref[...]
        step = 1
        while step < s:
            acc = acc + _shift_up(acc, step)
            step *= 2
        dfl = acc * jax.nn.sigmoid(-(fl_ref[...] + b_ref[...]))
        dfl_ref[...] = dfl.astype(BF16)
        db_ref[...] = jnp.broadcast_to(jnp.sum(dfl, axis=0, keepdims=True), (8, 128))

    return pl.pallas_call(
        body, name="cumf_bwd",
        out_shape=[jax.ShapeDtypeStruct((s, 128), BF16), jax.ShapeDtypeStruct((8, 128), F32)],
        compiler_params=pltpu.CompilerParams(vmem_limit_bytes=V7X_VMEM_LIMIT))(dcum, fl, bias)


def _pool_windows(v, shift):
    s2 = v + shift(v, 1)
    s4 = s2 + shift(s2, 2)
    s8 = s4 + shift(s4, 4)
    s16 = s8 + shift(s8, 8)
    group = lax.broadcasted_iota(jnp.int32, v.shape, 1) // 64
    return jnp.where(group == 0, s2, jnp.where(group == 1, s4, jnp.where(group == 2, s8, s16)))


def _pool_count(shape):
    group = lax.broadcasted_iota(jnp.int32, shape, 1) // 64
    window = jnp.where(group == 0, 2.0, jnp.where(group == 1, 4.0, jnp.where(group == 2, 8.0, 16.0)))
    t1 = (lax.broadcasted_iota(jnp.int32, shape, 0) + 1).astype(F32)
    return jnp.minimum(t1, window)


def _pc_specs(s):
    zcol = lambda blk: pl.BlockSpec((s, 256), lambda i, blk=blk: (0, blk))
    first = Z_PC // 256
    return [zcol(first), zcol(first + 1), zcol(first + 2), zcol(first + 3),
            pl.BlockSpec((256, 256), lambda i: (0, 0)), pl.BlockSpec((1, 256), lambda i: (0, 0)),
            pl.BlockSpec((3, 256), lambda i: (0, 0))]


def _poolconv_fwd(z, wbd, pscale, convw):
    s = z.shape[0]

    def body(pu_ref, ch_ref, cb_ref, cc_ref, w_ref, ps_ref, cw_ref, yb_ref, yc_ref):
        u = pu_ref[...]
        p = _pool_windows(u, _shift_down) / _pool_count(u.shape) - u
        yb = jnp.dot(p.astype(BF16), w_ref[...].astype(BF16), preferred_element_type=F32) * ps_ref[...]
        yb_ref[...] = yb.astype(BF16)
        uc = cc_ref[...] * ch_ref[...]
        cw = cw_ref[...]
        conv = cw[0:1, :] * _shift_down(uc, 2) + cw[1:2, :] * _shift_down(uc, 1) + cw[2:3, :] * uc
        yc_ref[...] = (cb_ref[...] * conv).astype(BF16)

    out = pl.BlockSpec((s, 256), lambda i: (0, 0))
    return pl.pallas_call(
        body, name="poolconv_fwd", grid=(1,), in_specs=_pc_specs(s), out_specs=[out, out],
        out_shape=[jax.ShapeDtypeStruct((s, 256), BF16)] * 2, compiler_params=_params("arbitrary"),
    )(z, z, z, z, wbd, pscale, convw)


def _poolconv_bwd(dyb, dyc, z, wbd, pscale, convw):
    s = z.shape[0]

    def body(dyb_ref, dyc_ref, pu_ref, ch_ref, cb_ref, cc_ref, w_ref, ps_ref, cw_ref, dz_ref, dw_ref, dps_ref, dcw_ref):
        u = pu_ref[...]
        count = _pool_count(u.shape)
        p = (_pool_windows(u, _shift_down) / count - u).astype(BF16)
        wb = w_ref[...].astype(BF16)
        dyb_v = dyb_ref[...]
        pw = jnp.dot(p, wb, preferred_element_type=F32)
        dps_ref[...] = jnp.broadcast_to(jnp.sum(dyb_v * pw, axis=0, keepdims=True), (8, 256))
        dys = (dyb_v * ps_ref[...]).astype(BF16)
        dp = lax.dot_general(dys, wb, (((1,), (1,)), ((), ())), preferred_element_type=F32)
        dw_ref[...] = lax.dot_general(p, dys, (((0,), (0,)), ((), ())), preferred_element_type=F32)
        dz_ref[:, 0:256] = (_pool_windows(dp / count, _shift_up) - dp).astype(BF16)

        ch, cb, cc = ch_ref[...], cb_ref[...], cc_ref[...]
        uc = cc * ch
        cw = cw_ref[...]
        u2, u1 = _shift_down(uc, 2), _shift_down(uc, 1)
        conv = cw[0:1, :] * u2 + cw[1:2, :] * u1 + cw[2:3, :] * uc
        dyc_v = dyc_ref[...]
        dconv = dyc_v * cb
        du = cw[0:1, :] * _shift_up(dconv, 2) + cw[1:2, :] * _shift_up(dconv, 1) + cw[2:3, :] * dconv
        dz_ref[:, 256:512] = (du * cc).astype(BF16)
        dz_ref[:, 512:768] = (dyc_v * conv).astype(BF16)
        dz_ref[:, 768:1024] = (du * ch).astype(BF16)
        dcw_ref[...] = jnp.zeros_like(dcw_ref)
        dcw_ref[0:1, :] = jnp.sum(dconv * u2, axis=0, keepdims=True)
        dcw_ref[1:2, :] = jnp.sum(dconv * u1, axis=0, keepdims=True)
        dcw_ref[2:3, :] = jnp.sum(dconv * uc, axis=0, keepdims=True)

    blk = lambda r, c: pl.BlockSpec((r, c), lambda i: (0, 0))
    return pl.pallas_call(
        body, name="poolconv_bwd", grid=(1,),
        in_specs=[blk(s, 256), blk(s, 256)] + _pc_specs(s),
        out_specs=[blk(s, 1024), blk(256, 256), blk(8, 256), blk(8, 256)],
        out_shape=[jax.ShapeDtypeStruct((s, 1024), BF16), jax.ShapeDtypeStruct((256, 256), F32),
                   jax.ShapeDtypeStruct((8, 256), F32), jax.ShapeDtypeStruct((8, 256), F32)],
        compiler_params=_params("arbitrary"),
    )(dyb, dyc, z, z, z, z, wbd, pscale, convw)


_NT = (((1,), (1,)), ((), ()))
_TN = (((0,), (0,)), ((), ()))


def _att_logits(q, k, fc, fr, q0, k0):
    logits = lax.dot_general(q, k, _NT, preferred_element_type=F32) * (HEAD_DIM ** -0.5) + fc - fr
    row = q0 + lax.broadcasted_iota(jnp.int32, logits.shape, 0)
    col = k0 + lax.broadcasted_iota(jnp.int32, logits.shape, 1)
    return jnp.where(row >= col, logits, NEG_INF)


HEAD_PAIRS = HEADS // 2


def _lane_pick(v, lane, idx):
    return jnp.sum(jnp.where(lane == idx, v, 0.0), axis=-1, keepdims=True)


def _lane_put(lane, idx, col):
    return jnp.where(lane == idx, col, 0.0)


def _split_heads(v, low):
    zero = jnp.zeros_like(v)
    return jnp.where(low, v, zero), jnp.where(low, zero, v)


def _attn_fwd(qkv, cum, fr):
    s = qkv.shape[0]
    blk = ATT_BLOCK
    nb = s // blk

    def body(q_ref, k_ref, v_ref, cum_ref, fr_ref, o_ref, lse_ref):
        qi, hp = pl.program_id(0), pl.program_id(1)
        lane = lax.broadcasted_iota(jnp.int32, (blk, 128), 1)
        low = lane < HEAD_DIM
        qs = _split_heads(q_ref[...], low)
        cumv = cum_ref[...]
        fcs = (_lane_pick(cumv, lane, 2 * hp), _lane_pick(cumv, lane, 2 * hp + 1))

        def step(j, carry):
            k0 = pl.multiple_of(j * blk, blk)
            k2, v2 = k_ref[pl.ds(k0, blk), :], v_ref[pl.ds(k0, blk), :]
            out = []
            for sub in (0, 1):
                m, l, acc = carry[sub]
                logits = _att_logits(qs[sub], k2, fcs[sub], fr_ref[sub, pl.ds(j, 1), :], qi * blk, k0)
                m_new = jnp.maximum(m, jnp.max(logits, axis=-1, keepdims=True))
                p = jnp.exp(logits - m_new)
                alpha = jnp.exp(m - m_new)
                l = alpha * l + jnp.sum(p, axis=-1, keepdims=True)
                acc = alpha * acc + jnp.dot(p.astype(BF16), v2, preferred_element_type=F32)
                out.append((m_new, l, acc))
            return tuple(out)

        one = (jnp.full((blk, 1), NEG_INF, F32), jnp.zeros((blk, 1), F32), jnp.zeros((blk, 128), F32))
        (m0, l0, acc0), (m1, l1, acc1) = lax.fori_loop(0, qi + 1, step, (one, one))
        o_ref[...] = jnp.where(low, acc0 / l0, acc1 / l1)

        @pl.when(hp == 0)
        def _():
            lse_ref[...] = jnp.zeros_like(lse_ref)

        lse_ref[...] += _lane_put(lane, 2 * hp, m0 + jnp.log(l0)) + _lane_put(lane, 2 * hp + 1, m1 + jnp.log(l1))

    return pl.pallas_call(
        body, name="attn_fwd", grid=(nb, HEAD_PAIRS),
        in_specs=[pl.BlockSpec((blk, 128), lambda i, hp: (i, hp)),
                  pl.BlockSpec((s, 128), lambda i, hp: (0, HEAD_PAIRS + hp)),
                  pl.BlockSpec((s, 128), lambda i, hp: (0, 2 * HEAD_PAIRS + hp)),
                  pl.BlockSpec((blk, 128), lambda i, hp: (i, 0)),
                  pl.BlockSpec((2, nb, blk), lambda i, hp: (hp, 0, 0))],
        out_specs=[pl.BlockSpec((blk, 128), lambda i, hp: (i, hp)), pl.BlockSpec((blk, 128), lambda i, hp: (i, 0))],
        out_shape=[jax.ShapeDtypeStruct((s, A_WIDTH), F32), jax.ShapeDtypeStruct((s, 128), F32)],
        compiler_params=_params("parallel", "arbitrary"),
    )(qkv, qkv, qkv, cum, fr)


def _attn_bwd(qkv, do, o, lse, cum, fr):
    s = qkv.shape[0]
    blk = ATT_BLOCK
    nb = s // blk
    scale = HEAD_DIM ** -0.5

    def body(q_ref, k_ref, v_ref, do_ref, o_ref, lse_ref, cum_ref, fr_ref, dq_ref, dk_ref, dv_ref, dfc_ref, dfr_ref, dk_acc, dv_acc):
        hp = pl.program_id(0)
        lane = lax.broadcasted_iota(jnp.int32, (blk, 128), 1)
        low = lane < HEAD_DIM
        dk_acc[...] = jnp.zeros_like(dk_acc)
        dv_acc[...] = jnp.zeros_like(dv_acc)
        dfr_ref[...] = jnp.zeros_like(dfr_ref)

        @pl.when(hp == 0)
        def _():
            dfc_ref[...] = jnp.zeros_like(dfc_ref)

        def outer(i, carry):
            q0 = pl.multiple_of(i * blk, blk)
            rows = pl.ds(q0, blk)
            q2, do2 = q_ref[rows, :], do_ref[rows, :]
            prod = do2 * o_ref[rows, :]
            deltas = (jnp.sum(jnp.where(low, prod, 0.0), axis=-1, keepdims=True),
                      jnp.sum(jnp.where(low, 0.0, prod), axis=-1, keepdims=True))
            dob2 = do2.astype(BF16)
            qs, dos = _split_heads(q2, low), _split_heads(dob2, low)
            lsev, cumv = lse_ref[rows, :], cum_ref[rows, :]
            lses = (_lane_pick(lsev, lane, 2 * hp), _lane_pick(lsev, lane, 2 * hp + 1))
            fcs = (_lane_pick(cumv, lane, 2 * hp), _lane_pick(cumv, lane, 2 * hp + 1))

            def inner(j, carry):
                k0 = pl.multiple_of(j * blk, blk)
                cols = pl.ds(k0, blk)
                k2, v2 = k_ref[cols, :], v_ref[cols, :]
                out, dk_parts, dv_parts = [], [], []
                for sub in (0, 1):
                    dq, dfc = carry[sub]
                    p = jnp.exp(_att_logits(qs[sub], k2, fcs[sub], fr_ref[sub, pl.ds(j, 1), :], q0, k0) - lses[sub])
                    dp = lax.dot_general(dos[sub], v2, _NT, preferred_element_type=F32)
                    ds = p * (dp - deltas[sub])
                    dsb = ds.astype(BF16)
                    dk_parts.append(lax.dot_general(dsb, q2, _TN, preferred_element_type=F32))
                    dv_parts.append(lax.dot_general(p.astype(BF16), dob2, _TN, preferred_element_type=F32))
                    dfr_ref[sub, pl.ds(j, 1), :] -= jnp.sum(ds, axis=0, keepdims=True)
                    out.append((dq + jnp.dot(dsb, k2, preferred_element_type=F32), dfc + jnp.sum(ds, axis=-1, keepdims=True)))
                dk_acc[cols, :] += jnp.where(low, dk_parts[0], dk_parts[1])
                dv_acc[cols, :] += jnp.where(low, dv_parts[0], dv_parts[1])
                return tuple(out)

            one = (jnp.zeros((blk, 128), F32), jnp.zeros((blk, 1), F32))
            (dq0, dfc0), (dq1, dfc1) = lax.fori_loop(0, i + 1, inner, (one, one))
            dq_ref[rows, :] = (jnp.where(low, dq0, dq1) * scale).astype(BF16)
            dfc_ref[rows, :] += _lane_put(lane, 2 * hp, dfc0) + _lane_put(lane, 2 * hp + 1, dfc1)
            return carry

        lax.fori_loop(0, nb, outer, 0)
        dk_ref[...] = (dk_acc[...] * scale).astype(BF16)
        dv_ref[...] = dv_acc[...].astype(BF16)

    pair = lambda first: pl.BlockSpec((s, 128), lambda hp, first=first: (0, first + hp))
    whole = pl.BlockSpec((s, 128), lambda hp: (0, 0))
    rowv = pl.BlockSpec((2, nb, blk), lambda hp: (hp, 0, 0))
    return pl.pallas_call(
        body, name="attn_bwd", grid=(HEAD_PAIRS,),
        in_specs=[pair(0), pair(HEAD_PAIRS), pair(2 * HEAD_PAIRS), pair(0), pair(0), whole, whole, rowv],
        out_specs=[pair(0), pair(0), pair(0), whole, rowv],
        out_shape=[jax.ShapeDtypeStruct((s, A_WIDTH), BF16)] * 3 + [jax.ShapeDtypeStruct((s, 128), F32), jax.ShapeDtypeStruct((HEADS, nb, blk), F32)],
        scratch_shapes=[pltpu.VMEM((s, 128), F32), pltpu.VMEM((s, 128), F32)],
        compiler_params=_params("arbitrary"),
    )(qkv, qkv, qkv, do, o, lse, cum, fr)


def _ada_fwd(c_all, w_ada, b_loc):
    depth, _, n = w_ada.shape
    tn = 512

    def body(c_ref, w_ref, b_ref, o_ref, sc_ref):
        cv = c_ref[...]
        sc = cv * jax.nn.sigmoid(cv)
        sc_ref[...] = sc
        o_ref[0] = jnp.dot(sc.astype(BF16), w_ref[0].astype(BF16), preferred_element_type=F32) + b_ref[0]

    return pl.pallas_call(
        body, name="ada_fwd", grid=(depth, n // tn),
        in_specs=[pl.BlockSpec((N_DEV, D), lambda l, j: (0, 0)), pl.BlockSpec((1, D, tn), lambda l, j: (l, 0, j)),
                  pl.BlockSpec((1, 1, tn), lambda l, j: (l, 0, j))],
        out_specs=[pl.BlockSpec((1, N_DEV, tn), lambda l, j: (l, 0, j)), pl.BlockSpec((N_DEV, D), lambda l, j: (0, 0))],
        out_shape=[jax.ShapeDtypeStruct((depth, N_DEV, n), F32), jax.ShapeDtypeStruct((N_DEV, D), F32)],
        compiler_params=_params("arbitrary", "arbitrary"),
    )(c_all, w_ada, b_loc)


def _sum_devices(gathered):
    n = gathered.shape[1]
    tn = _pick(n, (1408, 1024, 640, 512, 128))

    def body(g_ref, o_ref):
        acc = g_ref[0:8, :]
        for dev in range(1, N_DEV):
            acc = acc + g_ref[8 * dev:8 * dev + 8, :]
        o_ref[...] = acc

    return pl.pallas_call(
        body, name="sum_devices", grid=(n // tn,),
        in_specs=[pl.BlockSpec((8 * N_DEV, tn), lambda j: (0, j))], out_specs=pl.BlockSpec((8, tn), lambda j: (0, j)),
        out_shape=jax.ShapeDtypeStruct((8, n), F32), compiler_params=_params("parallel"),
    )(gathered)


def _place():
    x, y, c = lax.axis_index("x"), lax.axis_index("y"), lax.axis_index("c")
    chips = [(1 - x, y), (x, 1 - y), (1 - x, 1 - y)]
    return x, y, c, chips


def _allgather8(block, name):
    m_per, n = block.shape

    def body(x_ref, out_ref, send_sems, recv_sems, local_sem):
        x, y, c, chips = _place()
        me, sibling = (x, y, c), (x, y, 1 - c)

        def rows(px, py, pc):
            return out_ref.at[pl.ds((4 * px + 2 * py + pc) * m_per, m_per), :]

        def copy(k, blk, to, src=None):
            return pltpu.make_async_remote_copy(
                src_ref=rows(*blk) if src is None else src, dst_ref=rows(*blk),
                send_sem=send_sems.at[k], recv_sem=recv_sems.at[k], device_id=to, device_id_type=MESH)

        mine = pltpu.make_async_copy(x_ref, rows(*me), local_sem)
        mine.start()
        first = [copy(0, me, sibling, src=x_ref)]
        first += [copy(1 + j, me, (*chip, c), src=x_ref) for j, chip in enumerate(chips)]
        for cp in first:
            cp.start()
        passed = [copy(4 + j, (*chip, c), sibling) for j, chip in enumerate(chips)]
        for j, chip in enumerate(chips):
            copy(1 + j, (*chip, c), me).wait_recv()
            passed[j].start()
        copy(0, sibling, me).wait_recv()
        for j, chip in enumerate(chips):
            copy(4 + j, (*chip, 1 - c), me).wait_recv()
        for cp in first + passed:
            cp.wait_send()
        mine.wait()

    return pl.pallas_call(
        body, name=name, out_shape=jax.ShapeDtypeStruct((N_DEV * m_per, n), block.dtype),
        in_specs=[pl.BlockSpec(memory_space=pltpu.VMEM)], out_specs=pl.BlockSpec(memory_space=pltpu.VMEM),
        scratch_shapes=[pltpu.SemaphoreType.DMA((7,)), pltpu.SemaphoreType.DMA((7,)), pltpu.SemaphoreType.DMA],
        compiler_params=pltpu.CompilerParams(vmem_limit_bytes=V7X_VMEM_LIMIT),
    )(block)


_SEM = pl.BlockSpec(memory_space=pltpu.SEMAPHORE)
_DATAFLOW = pltpu.SideEffectType.DATAFLOW_SIDE_EFFECTING


def _plan_copies(plan, refs, send_sems, recv_sems):
    return [pltpu.make_async_remote_copy(src_ref=src, dst_ref=dst, send_sem=send_sems.at[i], recv_sem=recv_sems.at[i],
                                         device_id=to, device_id_type=MESH) for i, (src, dst, to) in enumerate(plan(refs))]


def _copies_start(bufs, plan, n_copies, after, name):
    nb = len(bufs)

    def body(*refs):
        for cp in _plan_copies(plan, refs[:nb], refs[nb + 1], refs[nb + 2]):
            cp.start()
        token = refs[-1]
        token[...] = jnp.zeros_like(token)

    sem = pltpu.SemaphoreType.DMA((n_copies,))
    outs = pl.pallas_call(
        body, name=name,
        out_shape=(sem, sem, *[pltpu.HBM(b.shape, b.dtype) for b in bufs], jax.ShapeDtypeStruct((8, 128), F32)),
        in_specs=[_HBM] * nb + [pl.BlockSpec(memory_space=pl.ANY)],
        out_specs=(_SEM, _SEM, *[_HBM] * nb, pl.BlockSpec(memory_space=pltpu.VMEM)),
        input_output_aliases={i: 2 + i for i in range(nb)},
        compiler_params=pltpu.CompilerParams(has_side_effects=_DATAFLOW),
    )(*[pltpu.with_memory_space_constraint(b, pltpu.HBM) for b in bufs], after)
    return outs[0], outs[1], list(outs[2:2 + nb]), outs[-1]


def _copies_wait(started, plan, after, name):
    send_sems, recv_sems, bufs, _ = started
    nb = len(bufs)

    def body(*refs):
        for cp in _plan_copies(plan, refs[:nb], refs[nb], refs[nb + 1]):
            cp.wait_send()
            cp.wait_recv()

    return list(pl.pallas_call(
        body, name=name, out_shape=tuple(pltpu.HBM(b.shape, b.dtype) for b in bufs),
        in_specs=[_HBM] * nb + [_SEM, _SEM, pl.BlockSpec(memory_space=pl.ANY)], out_specs=tuple([_HBM] * nb),
        input_output_aliases={i: i for i in range(nb)},
        compiler_params=pltpu.CompilerParams(has_side_effects=_DATAFLOW),
    )(*bufs, send_sems, recv_sems, after))


def _half_rows(ref, axis, c):
    half = ref.shape[axis] // 2
    return pl.ds(c * half, half)


def _plan_gather_ici(refs):
    n = len(refs) // 2
    x, y, c, chips = _place()
    out = []
    for a in range(n):
        rows = _half_rows(refs[a], 0, c)
        out += [(refs[a].at[rows], refs[n + a].at[2 * x + y, rows], (*chip, c)) for chip in chips]
    return out


def _plan_gather_d2d(refs):
    x, y, c, chips = _place()
    out = []
    for ref in refs:
        rows = _half_rows(ref, 1, c)
        for px, py in chips:
            landed = ref.at[2 * px + py, rows]
            out.append((landed, landed, (x, y, 1 - c)))
    return out


def _plan_rs_sibling(refs):
    n = len(refs) // 2
    x, y, c, _ = _place()
    return [(refs[a].at[pl.ds(0, N_CHIPS), _half_rows(refs[a], 1, 1 - c)], refs[n + a], (x, y, 1 - c)) for a in range(n)]


def _plan_rs_chips(refs):
    n = len(refs) // 2
    x, y, c, chips = _place()
    return [(refs[a].at[2 * px + py], refs[n + a].at[k], (px, py, c)) for a in range(n) for k, (px, py) in enumerate(chips)]


def _plan_rs_share(refs):
    n = len(refs) // 2
    x, y, c, _ = _place()
    return [(refs[a], refs[n + a], (x, y, 1 - c)) for a in range(n)]


def _chip_sum(g, other, sel, name):
    _, half, cdim = other.shape
    tr = _pick(half, (256, 128, 64))
    per = half // tr

    def body(sel_ref, g_ref, t_ref, wire_ref, own_ref):
        total = g_ref[0] + t_ref[0]
        wire_ref[0] = total.astype(BF16)

        @pl.when(pl.program_id(1) == sel_ref[1])
        def _():
            own_ref[...] = total

    blk = pl.BlockSpec((1, tr, cdim), lambda i, p, sel_ref: (p, i, 0))
    return pl.pallas_call(
        body, name=name,
        grid_spec=pltpu.PrefetchScalarGridSpec(
            num_scalar_prefetch=1, grid=(per, N_CHIPS),
            in_specs=[pl.BlockSpec((1, tr, cdim), lambda i, p, sel_ref: (p, sel_ref[0] * per + i, 0)), blk],
            out_specs=[blk, pl.BlockSpec((tr, cdim), lambda i, p, sel_ref: (i, 0))]),
        out_shape=[jax.ShapeDtypeStruct(other.shape, BF16), jax.ShapeDtypeStruct((half, cdim), F32)],
        compiler_params=_params("parallel", "arbitrary"),
    )(sel, g, other)


def _final_sum(own, recv, name):
    r, cdim = own.shape
    tr = _pick(r, (256, 128))

    def body(own_ref, r0_ref, r1_ref, r2_ref, o_ref):
        o_ref[...] = ((own_ref[...] + r0_ref[0].astype(F32)) + r1_ref[0].astype(F32)) + r2_ref[0].astype(F32)

    part = lambda k: pl.BlockSpec((1, tr, cdim), lambda i, k=k: (k, i, 0))
    spec = pl.BlockSpec((tr, cdim), lambda i: (i, 0))
    return pl.pallas_call(
        body, name=name, grid=(r // tr,), in_specs=[spec, part(0), part(1), part(2)], out_specs=spec,
        out_shape=jax.ShapeDtypeStruct((r, cdim), F32), compiler_params=_params("parallel"),
    )(own, recv, recv, recv)


def _row(v):
    return v.reshape(1, -1)


def _tie(v, token):
    return v if token is None else v + token[0:1, 0:1]


def _no_hook(point, after, ready=None):
    return None


def _layer_fwd(x, w, mod, hook=_no_hook):
    s = x.shape[0]
    nb = s // ATT_BLOCK
    h = _modnorm_fwd(x, _row(w["g_mix_pre"]), mod[0:1], mod[1:2], "mix_pre_fwd")
    z = _mm(h, w["w_all"], name="mm_in")
    qkv = z[:, Z_QKV:Z_PC].astype(BF16)
    fl = z[:, Z_FL:Z_COLS]
    cum = _cumf_fwd(fl, w["b_f_pad"])
    fr = cum[:, :HEADS].T.reshape(HEADS, nb, ATT_BLOCK)
    br_a, lse = _attn_fwd(qkv, cum, fr)
    br_b, br_c = _poolconv_fwd(z, w["w_pool_bd"], _tie(_row(w["pool_scale"]), hook("attn", lse)), w["conv_w"])
    hook("pool", br_b)
    wbr = w["w_branch"]
    pa = _mm(br_a, wbr[:A_WIDTH], name="mm_br_a")
    pb = _mm(br_b, wbr[A_WIDTH:A_WIDTH + POOL_WIDTH], name="mm_br_b")
    pc = _mm(br_c, wbr[A_WIDTH + POOL_WIDTH:], name="mm_br_c")
    merged = _merge_fwd(z, pa, pb, pc)
    y = _mm(merged, w["w_out"], name="mm_out")
    x1 = _post_fwd(x, y, _row(w["g_mix_post"]), mod[2:3], "mix_post_fwd")
    h2 = _modnorm_fwd(x1, _row(w["g_ff_pre"]), mod[3:4], mod[4:5], "ff_pre_fwd")
    a = _mm(h2, w["w_ff1"], b_split=N_CHIPS, name="mm_ff1")
    r = _act_fwd(a)
    y2 = _mm(r, w["w_ff2"], name="mm_ff2")
    x2 = _post_fwd(x1, y2, _tie(_row(w["g_ff_post"]), hook("ff_post", y2)), mod[5:6], "ff_post_fwd")
    hook("end", x2)
    saved = dict(x=x, h=h, z=z, qkv=qkv, fl=fl, cum=cum, fr=fr, lse=lse, br_a=br_a, br_b=br_b, br_c=br_c, pa=pa, pb=pb, pc=pc,
                 merged=merged, y=y, x1=x1, h2=h2, a=a, r=r, y2=y2)
    return x2, saved


def _layer_bwd(dx2, sv, w, mod, hook=_no_hook):
    s = dx2.shape[0]
    dy2, sum_ff_post = _post_bwd(dx2, sv["y2"], _row(w["g_ff_post"]), mod[5:6], "ff_post_bwd")
    dr = _mm(dy2, w["w_ff2"], tb=True, name="mm_ff2_dx")
    d_w_ff2 = _mm(sv["r"], dy2, ta=True, name="mm_ff2_dw")
    da = _act_bwd(dr, sv["a"])
    dh2 = _mm(da, w["w_ff1"], tb=True, b_split=N_CHIPS, name="mm_ff1_dx")
    d_w_ff1 = _mm(sv["h2"], da, ta=True, out_split=N_CHIPS, name="mm_ff1_dw")
    dx1, sum_ff_pre = _modnorm_bwd(dh2, sv["x1"], dx2, _tie(_row(w["g_ff_pre"]), hook("ff_pre", dh2, dict(w_ff1=d_w_ff1, w_ff2=d_w_ff2))), mod[4:5], "ff_pre_bwd")

    dy, sum_mix_post = _post_bwd(dx1, sv["y"], _row(w["g_mix_post"]), mod[2:3], "mix_post_bwd")
    dmerged = _mm(dy, w["w_out"], tb=True, name="mm_out_dx")
    d_w_out = _mm(sv["merged"], dy, ta=True, name="mm_out_dw")
    dgl, dpa, dpb, dpc = _merge_bwd(dmerged, sv["z"], sv["pa"], sv["pb"], sv["pc"])
    wbr = w["w_branch"]
    dbr_a = _mm(dpa, wbr[:A_WIDTH], tb=True, name="mm_br_a_dx")
    dbr_b = _mm(dpb, wbr[A_WIDTH:A_WIDTH + POOL_WIDTH], tb=True, name="mm_br_b_dx")
    dbr_c = _mm(dpc, wbr[A_WIDTH + POOL_WIDTH:], tb=True, name="mm_br_c_dx")
    d_w_branch = jnp.concatenate([_mm(sv["br_a"], dpa, ta=True, name="mm_br_a_dw"), _mm(sv["br_b"], dpb, ta=True, name="mm_br_b_dw"),
                                  _mm(sv["br_c"], dpc, ta=True, name="mm_br_c_dw")], axis=0)

    dq, dk, dv, dfc, dfr = _attn_bwd(sv["qkv"], dbr_a, sv["br_a"], sv["lse"], sv["cum"], sv["fr"])
    dcum = dfc + jnp.pad(dfr.reshape(HEADS, s).T, ((0, 0), (0, 128 - HEADS)))
    dfl, sum_bf = _cumf_bwd(dcum, sv["fl"], _tie(w["b_f_pad"], hook("cumf", dfc)))
    dpc_z, d_wbd, sum_ps, sum_cw = _poolconv_bwd(dbr_b, dbr_c, sv["z"], w["w_pool_bd"], _row(w["pool_scale"]), w["conv_w"])
    dz = jnp.concatenate([dgl, dq, dk, dv, dpc_z, dfl], axis=1)
    dh = _mm(dz, w["w_all"], tb=True, name="mm_in_dx")
    d_w_all = _mm(sv["h"], dz, ta=True, name="mm_in_dw")
    hook("mix_pre", dh)
    dx, sum_mix_pre = _modnorm_bwd(dh, sv["x"], dx1, _row(w["g_mix_pre"]), mod[1:2], "mix_pre_bwd")

    dmod = jnp.stack([sum_mix_pre[0], sum_mix_pre[1], sum_mix_post[0], sum_ff_pre[0], sum_ff_pre[1], sum_ff_post[0]])
    d_w_in = _w_in_shards(d_w_all)
    d_w_pool = jnp.stack([d_wbd[64 * g:64 * g + 64, 64 * g:64 * g + 64] for g in range(4)])
    big = dict(w_in=d_w_in, w_branch=d_w_branch, w_out=d_w_out, w_ff1=d_w_ff1, w_ff2=d_w_ff2)
    small = dict(g_mix_pre=sum_mix_pre[2], g_mix_post=sum_mix_post[1], g_ff_pre=sum_ff_pre[2], g_ff_post=sum_ff_post[1],
                 b_f=sum_bf[0, :HEADS], w_pool=d_w_pool, pool_scale=sum_ps[0], conv_w=sum_cw[0:3])
    return dx, dmod, big, small


_QKV_END, _FL_END, _PC_END = 3 * A_WIDTH, 3 * A_WIDTH + HEADS, 3 * A_WIDTH + HEADS + POOL_WIDTH + 3 * CONV_WIDTH
_W_IN_GROUPS = ((_PC_END, IN_COLS, Z_GL), (0, _QKV_END, Z_QKV), (_FL_END, _PC_END, Z_PC), (_QKV_END, _FL_END, Z_FL))
_SHARD_COLS = IN_COLS // N_CHIPS


def _w_all_from_shards(blocks):
    pieces = []
    for lo, hi, _ in _W_IN_GROUPS:
        for p in range(N_CHIPS):
            a, b = max(lo, p * _SHARD_COLS), min(hi, (p + 1) * _SHARD_COLS)
            if a < b:
                pieces.append(blocks[p][:, a - p * _SHARD_COLS:b - p * _SHARD_COLS])
    pieces.append(jnp.zeros((D, Z_COLS - IN_COLS), blocks.dtype))
    return jnp.concatenate(pieces, axis=1)


def _w_in_shards(d_w_all):
    blocks = []
    for p in range(N_CHIPS):
        pieces = []
        for lo, hi, at in sorted(_W_IN_GROUPS):
            a, b = max(lo, p * _SHARD_COLS), min(hi, (p + 1) * _SHARD_COLS)
            if a < b:
                pieces.append(d_w_all[:, at + a - lo:at + b - lo])
        blocks.append(jnp.concatenate(pieces, axis=1))
    return jnp.stack(blocks)


def _full_layer_weights(w_in_blocks, w_branch, w_out, w_ff1, w_ff2, g_mix_pre, g_mix_post, g_ff_pre, g_ff_post, b_f, w_pool, pool_scale, conv_w):
    w_all = _w_all_from_shards(w_in_blocks)
    wbd = jnp.zeros((POOL_WIDTH, POOL_WIDTH), F32)
    for g in range(4):
        wbd = wbd.at[64 * g:64 * g + 64, 64 * g:64 * g + 64].set(w_pool[g])
    return dict(w_all=w_all, w_branch=w_branch, w_out=w_out, w_ff1=w_ff1, w_ff2=w_ff2, g_mix_pre=g_mix_pre, g_mix_post=g_mix_post,
                g_ff_pre=g_ff_pre, g_ff_post=g_ff_post, b_f_pad=jnp.pad(b_f, (0, 128 - HEADS)).reshape(1, 128), w_pool_bd=wbd,
                pool_scale=pool_scale, conv_w=conv_w)


class _NoComm:
    def layer_weights(self, l):
        raise NotImplementedError

    def fwd_hook(self, l):
        return _no_hook

    def bwd_hook(self, l):
        return _no_hook

    def grads_ready(self, l, big):
        return None


class _Layers(_NoComm):
    def __init__(self, layers):
        self.layers = layers

    def layer_weights(self, l):
        return self.layers[l]


def _local_step(x, target, mods, comm):
    saved, weights = [], []
    act = x
    for l in range(DEPTH):
        weights.append(comm.layer_weights(l))
        act, sv = _layer_fwd(act, weights[l], mods[l], comm.fwd_hook(l))
        saved.append(sv)
    dact, sq = _loss_head(act, target)
    loss = sq[0, 0] * (0.5 / D)
    dmods, bigs, smalls = [None] * DEPTH, [None] * DEPTH, [None] * DEPTH
    token = None
    for l in reversed(range(DEPTH)):
        dact, dmods[l], bigs[l], smalls[l] = _layer_bwd(dact, saved[l], weights[l], _tie(mods[l], token), comm.bwd_hook(l))
        token = comm.grads_ready(l, bigs[l])
    return loss, dact, jnp.stack(dmods), bigs, smalls


_BIG = ("w_in", "w_branch", "w_out", "w_ff1", "w_ff2")
N_BIG = len(_BIG)


class _GatherJob:
    def __init__(self, tag, shards, chip, after):
        self.tag, self.n = tag, len(shards)
        lands = [lax.dynamic_update_index_in_dim(lax.empty((N_CHIPS,) + s.shape, s.dtype), s, chip, 0) for s in shards]
        self.state = _copies_start(list(shards) + lands, _plan_gather_ici, 3 * self.n, after, "gather_ici_start_" + tag)
        self.token = self.state[3]

    def pass_on(self, after):
        bufs = _copies_wait(self.state, _plan_gather_ici, after, "gather_ici_wait_" + self.tag)
        self.state = _copies_start(bufs[self.n:], _plan_gather_d2d, 3 * self.n, bufs[0], "gather_d2d_start_" + self.tag)
        self.token = self.state[3]
        return self.token

    def done(self, after):
        return _copies_wait(self.state, _plan_gather_d2d, after, "gather_d2d_wait_" + self.tag)


class _ReduceJob:
    def __init__(self, tag, names, grads, sel, after):
        self.tag, self.names, self.n, self.sel = tag, names, len(names), sel
        lands = [lax.empty((N_CHIPS, g.shape[1] // 2, g.shape[2]), F32) for g in grads]
        self.state = _copies_start(list(grads) + lands, _plan_rs_sibling, self.n, after, "rs_sibling_start_" + tag)
        self.token = self.state[3]

    def chip_sums(self, after):
        bufs = _copies_wait(self.state, _plan_rs_sibling, after, "rs_sibling_wait_" + self.tag)
        wires, self.owns = zip(*[_chip_sum(bufs[i], bufs[self.n + i], self.sel, "rs_chip_sum_" + name) for i, name in enumerate(self.names)])
        lands = [lax.empty((3,) + w.shape[1:], BF16) for w in wires]
        self.state = _copies_start(list(wires) + lands, _plan_rs_chips, 3 * self.n, self.owns[0], "rs_chips_start_" + self.tag)
        self.token = self.state[3]
        return self.token

    def final_sums(self, after):
        bufs = _copies_wait(self.state, _plan_rs_chips, after, "rs_chips_wait_" + self.tag)
        sums = [_final_sum(self.owns[i], bufs[self.n + i], "rs_final_" + name) for i, name in enumerate(self.names)]
        self.state = _copies_start(sums + [lax.empty(s.shape, F32) for s in sums], _plan_rs_share, self.n, sums[0], "rs_share_start_" + self.tag)
        self.token = self.state[3]
        return self.token

    def done(self, after):
        bufs = _copies_wait(self.state, _plan_rs_share, after, "rs_share_wait_" + self.tag)
        low = self.sel[0] == 0
        return {name: jnp.where(low, jnp.concatenate([mine, theirs]), jnp.concatenate([theirs, mine]))
                for name, mine, theirs in zip(self.names, bufs[:self.n], bufs[self.n:])}


def _chip_blocks(g):
    return g if g.ndim == 3 else g.reshape(N_CHIPS, -1, g.shape[1])


class _StepComm(_NoComm):
    def __init__(self, shards, chip, sel, after):
        self.sel = sel
        self.small, self.grads, self.jobs = None, [dict() for _ in range(DEPTH)], {}
        self.jobs["in0"] = _GatherJob("in0", shards[0][:1], chip, after)
        self.jobs["rest0"] = _GatherJob("rest0", shards[0][1:], chip, self.jobs["in0"].token)
        self.jobs["all1"] = _GatherJob("all1", shards[1], chip, self.jobs["rest0"].token)

    def layer_weights(self, l):
        if l == 0:
            job = self.jobs["in0"]
            (g_in,) = job.done(job.pass_on(self.jobs["all1"].token))
            self.weights0 = _full_layer_weights(g_in, None, None, None, None, *self.small[0])
            return self.weights0
        g_in, g_br, g_out, g_f1, g_f2 = self.landed1
        return _full_layer_weights(g_in, g_br.reshape(D, D), g_out.reshape(D, D), g_f1, g_f2.reshape(D_FF, D), *self.small[1])

    def fwd_hook(self, l):
        if l != 0:
            return _no_hook

        def hook(point, after, ready=None):
            if point == "attn":
                return self.jobs["rest0"].pass_on(after)
            if point == "ff_post":
                return self.jobs["all1"].pass_on(after)
            if point == "pool":
                g_br, g_out, g_f1, g_f2 = self.jobs["rest0"].done(after)
                self.weights0.update(w_branch=g_br.reshape(D, D), w_out=g_out.reshape(D, D), w_ff1=g_f1, w_ff2=g_f2.reshape(D_FF, D))
            if point == "end":
                self.landed1 = self.jobs["all1"].done(after)
            return None
        return hook

    def bwd_hook(self, l):
        if l != 0:
            return _no_hook

        def hook(point, after, ready=None):
            jobs = self.jobs
            if point == "ff_pre":
                token = jobs["rs1"].chip_sums(after)
                jobs["rs0_ff"] = _ReduceJob("0_ff", ("w_ff1", "w_ff2"), [_chip_blocks(ready[n]) for n in ("w_ff1", "w_ff2")], self.sel, token)
                return jobs["rs0_ff"].token
            if point == "cumf":
                return jobs["rs0_ff"].chip_sums(jobs["rs1"].final_sums(after))
            self.grads[1] = jobs["rs1"].done(after)
            return None
        return hook

    def grads_ready(self, l, big):
        if l == 1:
            self.jobs["rs1"] = _ReduceJob("1", _BIG, [_chip_blocks(big[n]) for n in _BIG], self.sel, self.sel)
            return self.jobs["rs1"].token
        names = ("w_in", "w_branch", "w_out")
        self.jobs["rs0_mix"] = _ReduceJob("0_mix", names, [_chip_blocks(big[n]) for n in names], self.sel, self.sel)
        return self.jobs["rs0_mix"].token

    def finish_sums(self, after):
        jobs = self.jobs
        token = jobs["rs0_mix"].chip_sums(after)
        return jobs["rs0_ff"].final_sums(token)

    def finish(self, after):
        jobs = self.jobs
        self.grads[0] = jobs["rs0_ff"].done(after)
        self.grads[0].update(jobs["rs0_mix"].done(jobs["rs0_mix"].final_sums(after)))


_SMALL = ("g_mix_pre", "g_mix_post", "g_ff_pre", "g_ff_post", "b_f", "w_pool", "pool_scale", "conv_w")


def _pack(parts, rows=8):
    flat = jnp.concatenate([p.reshape(-1) for p in parts])
    width = -(-flat.shape[0] // (rows * 128)) * 128
    return jnp.pad(flat, (0, rows * width - flat.shape[0])).reshape(rows, width)


def _unpack(packed, like):
    flat = packed.reshape(-1)
    out, at = [], 0
    for ref in like:
        out.append(flat[at:at + ref.size].reshape(ref.shape))
        at += ref.size
    return out


def kernel(x, c, w_ada, b_ada, g_mix_pre, g_mix_post, g_ff_pre, g_ff_post, w_in, b_f, w_pool, pool_scale, conv_w, w_branch, w_out, w_ff1, w_ff2, loss_target, m_w_ada, m_b_ada, m_g_mix_pre, m_g_mix_post, m_g_ff_pre, m_g_ff_post, m_w_in, m_b_f, m_w_pool, m_pool_scale, m_conv_w, m_w_branch, m_w_out, m_w_ff1, m_w_ff2, v_w_ada, v_b_ada, v_g_mix_pre, v_g_mix_post, v_g_ff_pre, v_g_ff_post, v_w_in, v_b_f, v_w_pool, v_pool_scale, v_conv_w, v_w_branch, v_w_out, v_w_ff1, v_w_ff2):
    xi, yi, ci = lax.axis_index("x"), lax.axis_index("y"), lax.axis_index("c")
    chip = 2 * xi + yi
    dev = 2 * chip + ci
    n_ada = w_ada.shape[2]

    first = jnp.zeros((8, D + 384), F32).at[0, :D].set(c[0]).at[0, D:].set(conv_w.reshape(-1))
    got = _allgather8(first, "gather_cond").reshape(N_DEV, 8, D + 384)[:, 0]
    c_all = got[:, :D]
    conv_full = got[0::2, D:].reshape(N_CHIPS, DEPTH, 3, CONV_WIDTH // N_CHIPS).transpose(1, 2, 0, 3).reshape(DEPTH, 3, CONV_WIDTH)

    b_loc = lax.dynamic_slice_in_dim(b_ada, chip * n_ada, n_ada, axis=1).reshape(DEPTH, 1, n_ada)
    mod_cols, silu_c = _ada_fwd(c_all, w_ada, b_loc)
    got = _allgather8(mod_cols.reshape(DEPTH * N_DEV, n_ada), "gather_mod").reshape(N_DEV, DEPTH, N_DEV, n_ada)[0::2]
    mod_all = got.transpose(1, 2, 0, 3).reshape(DEPTH, N_DEV, 6, D)
    mods = lax.dynamic_index_in_dim(mod_all, dev, axis=1, keepdims=False)

    comm = _StepComm([[w[l].astype(BF16) for w in (w_in, w_branch, w_out, w_ff1, w_ff2)] for l in range(DEPTH)], chip,
                     jnp.stack([ci, chip]).astype(jnp.int32), mods)
    comm.small = [(g_mix_pre[l], g_mix_post[l], g_ff_pre[l], g_ff_post[l], b_f[l], w_pool[l], pool_scale[l], conv_full[l]) for l in range(DEPTH)]
    loss_part, grad_x, dmods, bigs, smalls = _local_step(x[0], loss_target[0], mods, comm)
    loss = lax.psum(loss_part, ("x", "y", "c"))

    small_parts = [smalls[l][name] for name in _SMALL for l in range(DEPTH)]
    packed = _tie(_pack([dmods] + small_parts), comm.jobs["rs0_mix"].token)
    gathered = _allgather8(packed, "gather_small")
    dmod_all = gathered.reshape(N_DEV, -1)[:, :dmods.size].reshape(N_DEV, DEPTH, 6 * D)
    summed = _unpack(_sum_devices(gathered), [dmods] + small_parts)
    grad_b_ada = summed[0].reshape(DEPTH, 6 * D)
    small_grads = {name: jnp.stack(summed[1 + 2 * i:3 + 2 * i]) for i, name in enumerate(_SMALL)}
    small_grads["conv_w"] = lax.dynamic_slice_in_dim(small_grads["conv_w"], chip * (CONV_WIDTH // N_CHIPS), CONV_WIDTH // N_CHIPS, axis=2)

    dmod_loc = lax.dynamic_slice_in_dim(dmod_all.transpose(1, 0, 2), chip * n_ada, n_ada, axis=2)
    tail_token = comm.finish_sums(grad_b_ada)
    silu_pad = _tie(jnp.pad(silu_c, ((0, 128 - N_DEV), (0, 0))), tail_token)
    grad_w_ada = jnp.stack([_mm(silu_pad, jnp.pad(dmod_loc[l], ((0, 128 - N_DEV), (0, 0))), ta=True, name="mm_ada_dw") for l in range(DEPTH)])

    grads = dict(w_ada=grad_w_ada, b_ada=grad_b_ada, **small_grads)
    weights = dict(w_ada=w_ada, b_ada=b_ada, g_mix_pre=g_mix_pre, g_mix_post=g_mix_post, g_ff_pre=g_ff_pre, g_ff_post=g_ff_post, w_in=w_in,
                   b_f=b_f, w_pool=w_pool, pool_scale=pool_scale, conv_w=conv_w, w_branch=w_branch, w_out=w_out, w_ff1=w_ff1, w_ff2=w_ff2)
    m_in = dict(w_ada=m_w_ada, b_ada=m_b_ada, g_mix_pre=m_g_mix_pre, g_mix_post=m_g_mix_post, g_ff_pre=m_g_ff_pre, g_ff_post=m_g_ff_post,
                w_in=m_w_in, b_f=m_b_f, w_pool=m_w_pool, pool_scale=m_pool_scale, conv_w=m_conv_w, w_branch=m_w_branch, w_out=m_w_out,
                w_ff1=m_w_ff1, w_ff2=m_w_ff2)
    v_in = dict(w_ada=v_w_ada, b_ada=v_b_ada, g_mix_pre=v_g_mix_pre, g_mix_post=v_g_mix_post, g_ff_pre=v_g_ff_pre, g_ff_post=v_g_ff_post,
                w_in=v_w_in, b_f=v_b_f, w_pool=v_w_pool, pool_scale=v_pool_scale, conv_w=v_conv_w, w_branch=v_w_branch, w_out=v_w_out,
                w_ff1=v_w_ff1, w_ff2=v_w_ff2)
    order = ("w_ada", "b_ada", "g_mix_pre", "g_mix_post", "g_ff_pre", "g_ff_post", "w_in", "b_f", "w_pool", "pool_scale", "conv_w",
             "w_branch", "w_out", "w_ff1", "w_ff2")
    delta, new_m, new_v = {}, {}, {}
    tiny = ("b_ada",) + _SMALL
    packed_w, packed_g, packed_m, packed_v = [_pack([src[name] for name in tiny]) for src in (weights, grads, m_in, v_in)]
    res = _adamw(packed_w, _tie(packed_g, tail_token), packed_m, packed_v, "adamw_small")
    for out, packed_res in zip((delta, new_m, new_v), res):
        for name, val in zip(tiny, _unpack(packed_res, [weights[name] for name in tiny])):
            out[name] = val
    delta["w_ada"], new_m["w_ada"], new_v["w_ada"] = _adamw(w_ada, grad_w_ada, m_w_ada, v_w_ada, "adamw_w_ada")
    comm.finish(delta["w_ada"][0, :8, :128] + delta["b_ada"][0, :128])
    for name in _BIG:
        grads[name] = jnp.stack([comm.grads[l][name] for l in range(DEPTH)])
        delta[name], new_m[name], new_v[name] = _adamw(weights[name], grads[name], m_in[name], v_in[name], "adamw_" + name)

    return (loss, grad_x[None], *[grads[n] for n in order], *[delta[n] for n in order], *[new_m[n] for n in order],
            *[new_v[n] for n in order])
```

```python
import functools

import jax
import jax.numpy as jnp
from jax import lax
from jax.experimental import pallas as pl
from jax.experimental.pallas import tpu as pltpu

F32 = jnp.float32
BF16 = jnp.bfloat16
MESH = pl.DeviceIdType.MESH

D = 1024
DEPTH = 2
HEADS = 8
HEAD_DIM = 64
A_WIDTH = 512
POOL_WIDTH = 256
CONV_WIDTH = 256
D_FF = 4096
IN_COLS = 5640
Z_GL, Z_QKV, Z_PC, Z_FL, Z_COLS = 0, 3072, 4608, 5632, 5760
RMS_EPS = 1e-6
NEG_INF = -1e30
ROW_TILE = 256
N_CHIPS = 4
N_DEV = 8
V7X_VMEM_LIMIT = 48 * 1024 * 1024

ADAM_LR = 0.001
ADAM_B1 = 0.9
ADAM_B2 = 0.999
ADAM_EPS = 1e-08
ADAM_WD = 0.01
ADAM_STEP = 10

_HBM = pl.BlockSpec(memory_space=pltpu.HBM)


def _params(*sem):
    return pltpu.CompilerParams(dimension_semantics=sem, vmem_limit_bytes=V7X_VMEM_LIMIT)


def _pick(dim, cands):
    for cand in cands:
        if dim % cand == 0:
            return cand
    return dim


def _mm(a, b, *, ta=False, tb=False, b_split=1, out_split=1, out_dtype=F32, epilogue=None, extras=(), name):
    (k, m) = a.shape if ta else a.shape[::-1]
    b_rows, b_cols = b.shape[-2], b.shape[-1] * b_split
    (n, k2) = (b_rows, b_cols) if tb else (b_cols, b_rows)
    assert k == k2, (a.shape, b.shape, ta, tb)
    n_unit = n // (out_split * (1 if tb else b_split))
    k_unit = k // (b_split if tb else 1)
    tm = _pick(m, (1024, 512, 256, 128))
    tn = _pick(n_unit, (1024, 1152, 768, 640, 512, 256, 128))
    tk = _pick(k_unit, (1024, 1152, 512, 640, 256, 128))
    nk = k // tk
    dims = (((0 if ta else 1,), (1 if tb else 0,)), ((), ()))

    def dot(a_ref, b_ref):
        b_val = b_ref[0] if b_split > 1 else b_ref[...]
        return lax.dot_general(a_ref[...].astype(BF16), b_val.astype(BF16), dims, preferred_element_type=F32)

    n_extra = len(extras)
    assert epilogue is None or out_split == 1

    def put(refs, val):
        if epilogue is not None:
            for o_ref, res in zip(refs[n_extra:], epilogue(val, *[r[...] for r in refs[:n_extra]])):
                o_ref[...] = res.astype(o_ref.dtype)
        elif out_split > 1:
            refs[0][0] = val.astype(refs[0].dtype)
        else:
            refs[0][...] = val.astype(refs[0].dtype)

    def body_single(a_ref, b_ref, *refs):
        put(refs, dot(a_ref, b_ref))

    def body_acc(a_ref, b_ref, *refs):
        kk = pl.program_id(2)
        acc_ref = refs[-1]

        @pl.when(kk == 0)
        def _():
            acc_ref[...] = jnp.zeros_like(acc_ref)

        acc_ref[...] += dot(a_ref, b_ref)

        @pl.when(kk == nk - 1)
        def _():
            put(refs[:-1], acc_ref[...])

    a_spec = pl.BlockSpec((tk, tm), lambda i, j, kk: (kk, i)) if ta else pl.BlockSpec((tm, tk), lambda i, j, kk: (i, kk))
    if b_split == 1:
        b_spec = pl.BlockSpec((tn, tk), lambda i, j, kk: (j, kk)) if tb else pl.BlockSpec((tk, tn), lambda i, j, kk: (kk, j))
    elif tb:
        per = k_unit // tk
        b_spec = pl.BlockSpec((1, tn, tk), lambda i, j, kk: (kk // per, j, kk % per))
    else:
        per = n // b_split // tn
        b_spec = pl.BlockSpec((1, tk, tn), lambda i, j, kk: (j // per, kk, j % per))
    if out_split == 1:
        o_spec = pl.BlockSpec((tm, tn), lambda i, j, kk: (i, j))
        o_shape = None if epilogue is not None else jax.ShapeDtypeStruct((m, n), out_dtype)
    else:
        per_o = n // out_split // tn
        o_spec = pl.BlockSpec((1, tm, tn), lambda i, j, kk: (j // per_o, i, j % per_o))
        o_shape = jax.ShapeDtypeStruct((out_split, m, n // out_split), out_dtype)
    if epilogue is not None:
        o_shape = [jax.ShapeDtypeStruct((m, n), dt) for dt in out_dtype]
        o_spec = [o_spec] * len(out_dtype)
    return pl.pallas_call(
        body_single if nk == 1 else body_acc, name=name, grid=(m // tm, n // tn, nk),
        in_specs=[a_spec, b_spec] + [pl.BlockSpec((tm, tn), lambda i, j, kk: (i, j))] * n_extra, out_specs=o_spec, out_shape=o_shape,
        scratch_shapes=[] if nk == 1 else [pltpu.VMEM((tm, tn), F32)],
        compiler_params=_params("parallel", "parallel", "arbitrary"),
    )(a, b, *extras)


def _ew(fn, ins, out_dtypes, name, tc=None):
    shape = ins[0].shape
    lead, (rows, cols) = shape[:-2], shape[-2:]
    tc = cols if tc is None else tc
    tr = _pick(rows, (ROW_TILE, 128, 8)) if tc > 128 else _pick(rows, (4096, 2256, 2048, 1024, ROW_TILE, 8))
    n_in = len(ins)

    def body(*refs):
        res = fn(*[r[...] for r in refs[:n_in]])
        for o_ref, val in zip(refs[n_in:], res):
            o_ref[...] = val.astype(o_ref.dtype)

    if lead:
        spec = pl.BlockSpec((None, tr, tc), lambda l, i, j: (l, i, j))
    else:
        spec = pl.BlockSpec((tr, tc), lambda i, j: (i, j))
    return pl.pallas_call(
        body, name=name, grid=lead + (rows // tr, cols // tc),
        in_specs=[spec] * n_in, out_specs=[spec] * len(out_dtypes),
        out_shape=[jax.ShapeDtypeStruct(shape, dt) for dt in out_dtypes],
        compiler_params=_params(*(["parallel"] * (len(lead) + 2))),
    )(*ins)


def _relu2_fwd(a):
    r = jnp.maximum(a, 0.0)
    return a, r * r


def _relu2_bwd(dr, a):
    return (dr * (2.0 * jnp.maximum(a, 0.0)),)


def _adamw(w, g, m, v, name):
    bc1 = 1.0 - ADAM_B1 ** ADAM_STEP
    bc2 = 1.0 - ADAM_B2 ** ADAM_STEP

    def fn(w, g, m, v):
        m = ADAM_B1 * m + (1.0 - ADAM_B1) * g
        v = ADAM_B2 * v + (1.0 - ADAM_B2) * (g * g)
        m_hat = m / bc1
        v_hat = v / bc2
        delta = -ADAM_LR * (m_hat / (jnp.sqrt(v_hat) + ADAM_EPS) + ADAM_WD * w)
        return delta, m, v
    return _ew(fn, [w, g, m, v], [F32, F32, F32], name)


def _row_spec(cols, block=0):
    return pl.BlockSpec((ROW_TILE, cols), lambda i, block=block: (i, block))


def _vec_spec(cols):
    return pl.BlockSpec((1, cols), lambda i: (0, 0))


def _sum_spec(cols):
    return pl.BlockSpec((8, cols), lambda i: (0, 0))


def _rstd(x):
    return lax.rsqrt(jnp.mean(x * x, axis=-1, keepdims=True) + RMS_EPS)


def _modnorm_fwd(x, g, shift, scale, name):
    s = x.shape[0]

    def body(x_ref, g_ref, sh_ref, sc_ref, h_ref):
        xv = x_ref[...]
        n = xv * _rstd(xv)
        h_ref[...] = ((n * g_ref[...]) * (1.0 + sc_ref[...]) + sh_ref[...]).astype(BF16)

    return pl.pallas_call(
        body, name=name, grid=(s // ROW_TILE,),
        in_specs=[_row_spec(D), _vec_spec(D), _vec_spec(D), _vec_spec(D)], out_specs=_row_spec(D),
        out_shape=jax.ShapeDtypeStruct((s, D), BF16), compiler_params=_params("parallel"),
    )(x, g, shift, scale)


def _post_fwd(x, y, g, gate, name):
    s = x.shape[0]

    def body(x_ref, y_ref, g_ref, gate_ref, o_ref):
        yv = y_ref[...]
        o_ref[...] = x_ref[...] + gate_ref[...] * ((yv * _rstd(yv)) * g_ref[...])

    return pl.pallas_call(
        body, name=name, grid=(s // ROW_TILE,),
        in_specs=[_row_spec(D), _row_spec(D), _vec_spec(D), _vec_spec(D)], out_specs=_row_spec(D),
        out_shape=jax.ShapeDtypeStruct((s, D), F32), compiler_params=_params("parallel"),
    )(x, y, g, gate)


def _post_bwd(dxo, y, g, gate, name):
    s = dxo.shape[0]

    def body(d_ref, y_ref, g_ref, gate_ref, dy_ref, sum_ref):
        @pl.when(pl.program_id(0) == 0)
        def _():
            sum_ref[...] = jnp.zeros_like(sum_ref)

        dv, yv = d_ref[...], y_ref[...]
        r = _rstd(yv)
        n = yv * r
        sum_ref[0:1, :] += jnp.sum(dv * (n * g_ref[...]), axis=0, keepdims=True)
        sum_ref[1:2, :] += jnp.sum((dv * gate_ref[...]) * n, axis=0, keepdims=True)
        dn = (dv * gate_ref[...]) * g_ref[...]
        dy_ref[...] = (r * (dn - n * jnp.mean(dn * n, axis=-1, keepdims=True))).astype(BF16)

    return pl.pallas_call(
        body, name=name, grid=(s // ROW_TILE,),
        in_specs=[_row_spec(D), _row_spec(D), _vec_spec(D), _vec_spec(D)],
        out_specs=[_row_spec(D), _sum_spec(D)],
        out_shape=[jax.ShapeDtypeStruct((s, D), BF16), jax.ShapeDtypeStruct((8, D), F32)],
        compiler_params=_params("arbitrary"),
    )(dxo, y, g, gate)


def _modnorm_bwd(dh, x, dxo, g, scale, name):
    s = dh.shape[0]

    def body(dh_ref, x_ref, d_ref, g_ref, sc_ref, dx_ref, sum_ref):
        @pl.when(pl.program_id(0) == 0)
        def _():
            sum_ref[...] = jnp.zeros_like(sum_ref)

        dhv, xv = dh_ref[...], x_ref[...]
        r = _rstd(xv)
        n = xv * r
        one_sc = 1.0 + sc_ref[...]
        sum_ref[0:1, :] += jnp.sum(dhv, axis=0, keepdims=True)
        sum_ref[1:2, :] += jnp.sum(dhv * (n * g_ref[...]), axis=0, keepdims=True)
        sum_ref[2:3, :] += jnp.sum((dhv * one_sc) * n, axis=0, keepdims=True)
        dn = (dhv * one_sc) * g_ref[...]
        dx_ref[...] = d_ref[...] + r * (dn - n * jnp.mean(dn * n, axis=-1, keepdims=True))

    return pl.pallas_call(
        body, name=name, grid=(s // ROW_TILE,),
        in_specs=[_row_spec(D), _row_spec(D), _row_spec(D), _vec_spec(D), _vec_spec(D)],
        out_specs=[_row_spec(D), _sum_spec(D)],
        out_shape=[jax.ShapeDtypeStruct((s, D), F32), jax.ShapeDtypeStruct((8, D), F32)],
        compiler_params=_params("arbitrary"),
    )(dh, x, dxo, g, scale)


def _loss_head(y, target):
    s = y.shape[0]

    def body(y_ref, t_ref, dy_ref, sum_ref):
        @pl.when(pl.program_id(0) == 0)
        def _():
            sum_ref[...] = jnp.zeros_like(sum_ref)

        err = y_ref[...] - t_ref[...]
        dy_ref[...] = err * (1.0 / D)
        sum_ref[...] += jnp.sum(err * err)

    return pl.pallas_call(
        body, name="loss_head", grid=(s // ROW_TILE,),
        in_specs=[_row_spec(D), _row_spec(D)],
        out_specs=[_row_spec(D), pl.BlockSpec((8, 128), lambda i: (0, 0))],
        out_shape=[jax.ShapeDtypeStruct((s, D), F32), jax.ShapeDtypeStruct((8, 128), F32)],
        compiler_params=_params("arbitrary"),
    )(y, target)


def _merge_fwd(z, pa, pb, pc):
    s = z.shape[0]

    def body(g0_ref, g1_ref, g2_ref, pa_ref, pb_ref, pc_ref, o_ref):
        o_ref[...] = (jax.nn.sigmoid(g0_ref[...]) * pa_ref[...] + jax.nn.sigmoid(g1_ref[...]) * pb_ref[...]
                      + jax.nn.sigmoid(g2_ref[...]) * pc_ref[...]).astype(BF16)

    return pl.pallas_call(
        body, name="merge_fwd", grid=(s // ROW_TILE,),
        in_specs=[_row_spec(D, 0), _row_spec(D, 1), _row_spec(D, 2), _row_spec(D), _row_spec(D), _row_spec(D)],
        out_specs=_row_spec(D), out_shape=jax.ShapeDtypeStruct((s, D), BF16),
        compiler_params=_params("parallel"),
    )(z, z, z, pa, pb, pc)


def _merge_bwd(dm, z, pa, pb, pc):
    s = z.shape[0]

    def body(dm_ref, g0_ref, g1_ref, g2_ref, pa_ref, pb_ref, pc_ref, dgl_ref, da_ref, db_ref, dc_ref):
        dmv = dm_ref[...]
        for i, (g_ref, p_ref, d_ref) in enumerate(((g0_ref, pa_ref, da_ref), (g1_ref, pb_ref, db_ref), (g2_ref, pc_ref, dc_ref))):
            gate = jax.nn.sigmoid(g_ref[...])
            dgl_ref[:, i * D:(i + 1) * D] = ((dmv * p_ref[...]) * (gate * (1.0 - gate))).astype(BF16)
            d_ref[...] = (dmv * gate).astype(BF16)

    return pl.pallas_call(
        body, name="merge_bwd", grid=(s // ROW_TILE,),
        in_specs=[_row_spec(D), _row_spec(D, 0), _row_spec(D, 1), _row_spec(D, 2), _row_spec(D), _row_spec(D), _row_spec(D)],
        out_specs=[_row_spec(3 * D), _row_spec(D), _row_spec(D), _row_spec(D)],
        out_shape=[jax.ShapeDtypeStruct((s, 3 * D), BF16)] + [jax.ShapeDtypeStruct((s, D), BF16)] * 3,
        compiler_params=_params("parallel"),
    )(dm, z, z, z, pa, pb, pc)


def _shift_down(v, n):
    row = lax.broadcasted_iota(jnp.int32, v.shape, 0)
    return jnp.where(row >= n, pltpu.roll(v, n, axis=0), 0.0)


def _shift_up(v, n):
    s = v.shape[0]
    row = lax.broadcasted_iota(jnp.int32, v.shape, 0)
    return jnp.where(row < s - n, pltpu.roll(v, s - n, axis=0), 0.0)


def _log_sigmoid(v):
    return jnp.minimum(v, 0.0) - jnp.log1p(jnp.exp(-jnp.abs(v)))


def _cumf_fwd(fl, bias):
    s = fl.shape[0]

    def body(fl_ref, b_ref, o_ref):
        acc = _log_sigmoid(fl_ref[...] + b_ref[...])
        step = 1
        while step < s:
            acc = acc + _shift_down(acc, step)
            step *= 2
        o_ref[...] = acc

    return pl.pallas_call(body, name="cumf_fwd", out_shape=jax.ShapeDtypeStruct((s, 128), F32),
                          compiler_params=pltpu.CompilerParams(vmem_limit_bytes=V7X_VMEM_LIMIT))(fl, bias)


def _cumf_bwd(dcum, fl, bias):
    s = fl.shape[0]

    def body(d_ref, fl_ref, b_ref, dfl_ref, db_ref):
        acc = d_ref[...]
        step = 1
        while step < s:
            acc = acc + _shift_up(acc, step)
            step *= 2
        dfl = acc * jax.nn.sigmoid(-(fl_ref[...] + b_ref[...]))
        dfl_ref[...] = dfl.astype(BF16)
        db_ref[...] = jnp.broadcast_to(jnp.sum(dfl, axis=0, keepdims=True), (8, 128))

    return pl.pallas_call(
        body, name="cumf_bwd",
        out_shape=[jax.ShapeDtypeStruct((s, 128), BF16), jax.ShapeDtypeStruct((8, 128), F32)],
        compiler_params=pltpu.CompilerParams(vmem_limit_bytes=V7X_VMEM_LIMIT))(dcum, fl, bias)


def _pool_windows(v, shift):
    s2 = v + shift(v, 1)
    s4 = s2 + shift(s2, 2)
    s8 = s4 + shift(s4, 4)
    s16 = s8 + shift(s8, 8)
    group = lax.broadcasted_iota(jnp.int32, v.shape, 1) // 64
    return jnp.where(group == 0, s2, jnp.where(group == 1, s4, jnp.where(group == 2, s8, s16)))


def _pool_count(shape):
    group = lax.broadcasted_iota(jnp.int32, shape, 1) // 64
    window = jnp.where(group == 0, 2.0, jnp.where(group == 1, 4.0, jnp.where(group == 2, 8.0, 16.0)))
    t1 = (lax.broadcasted_iota(jnp.int32, shape, 0) + 1).astype(F32)
    return jnp.minimum(t1, window)


def _pc_specs(s):
    zcol = lambda blk: pl.BlockSpec((s, 256), lambda i, blk=blk: (0, blk))
    first = Z_PC // 256
    return [zcol(first), zcol(first + 1), zcol(first + 2), zcol(first + 3),
            pl.BlockSpec((256, 256), lambda i: (0, 0)), pl.BlockSpec((1, 256), lambda i: (0, 0)),
            pl.BlockSpec((3, 256), lambda i: (0, 0))]


def _poolconv_fwd(z, wbd, pscale, convw):
    s = z.shape[0]

    def body(pu_ref, ch_ref, cb_ref, cc_ref, w_ref, ps_ref, cw_ref, yb_ref, yc_ref):
        u = pu_ref[...]
        p = _pool_windows(u, _shift_down) / _pool_count(u.shape) - u
        yb = jnp.dot(p.astype(BF16), w_ref[...].astype(BF16), preferred_element_type=F32) * ps_ref[...]
        yb_ref[...] = yb.astype(BF16)
        uc = cc_ref[...] * ch_ref[...]
        cw = cw_ref[...]
        conv = cw[0:1, :] * _shift_down(uc, 2) + cw[1:2, :] * _shift_down(uc, 1) + cw[2:3, :] * uc
        yc_ref[...] = (cb_ref[...] * conv).astype(BF16)

    out = pl.BlockSpec((s, 256), lambda i: (0, 0))
    return pl.pallas_call(
        body, name="poolconv_fwd", grid=(1,), in_specs=_pc_specs(s), out_specs=[out, out],
        out_shape=[jax.ShapeDtypeStruct((s, 256), BF16)] * 2, compiler_params=_params("arbitrary"),
    )(z, z, z, z, wbd, pscale, convw)


def _poolconv_bwd(dyb, dyc, z, wbd, pscale, convw):
    s = z.shape[0]

    def body(dyb_ref, dyc_ref, pu_ref, ch_ref, cb_ref, cc_ref, w_ref, ps_ref, cw_ref, dz_ref, dw_ref, dps_ref, dcw_ref):
        u = pu_ref[...]
        count = _pool_count(u.shape)
        p = (_pool_windows(u, _shift_down) / count - u).astype(BF16)
        wb = w_ref[...].astype(BF16)
        dyb_v = dyb_ref[...]
        pw = jnp.dot(p, wb, preferred_element_type=F32)
        dps_ref[...] = jnp.broadcast_to(jnp.sum(dyb_v * pw, axis=0, keepdims=True), (8, 256))
        dys = (dyb_v * ps_ref[...]).astype(BF16)
        dp = lax.dot_general(dys, wb, (((1,), (1,)), ((), ())), preferred_element_type=F32)
        dw_ref[...] = lax.dot_general(p, dys, (((0,), (0,)), ((), ())), preferred_element_type=F32)
        dz_ref[:, 0:256] = (_pool_windows(dp / count, _shift_up) - dp).astype(BF16)

        ch, cb, cc = ch_ref[...], cb_ref[...], cc_ref[...]
        uc = cc * ch
        cw = cw_ref[...]
        u2, u1 = _shift_down(uc, 2), _shift_down(uc, 1)
        conv = cw[0:1, :] * u2 + cw[1:2, :] * u1 + cw[2:3, :] * uc
        dyc_v = dyc_ref[...]
        dconv = dyc_v * cb
        du = cw[0:1, :] * _shift_up(dconv, 2) + cw[1:2, :] * _shift_up(dconv, 1) + cw[2:3, :] * dconv
        dz_ref[:, 256:512] = (du * cc).astype(BF16)
        dz_ref[:, 512:768] = (dyc_v * conv).astype(BF16)
        dz_ref[:, 768:1024] = (du * ch).astype(BF16)
        dcw_ref[...] = jnp.zeros_like(dcw_ref)
        dcw_ref[0:1, :] = jnp.sum(dconv * u2, axis=0, keepdims=True)
        dcw_ref[1:2, :] = jnp.sum(dconv * u1, axis=0, keepdims=True)
        dcw_ref[2:3, :] = jnp.sum(dconv * uc, axis=0, keepdims=True)

    blk = lambda r, c: pl.BlockSpec((r, c), lambda i: (0, 0))
    return pl.pallas_call(
        body, name="poolconv_bwd", grid=(1,),
        in_specs=[blk(s, 256), blk(s, 256)] + _pc_specs(s),
        out_specs=[blk(s, 1024), blk(256, 256), blk(8, 256), blk(8, 256)],
        out_shape=[jax.ShapeDtypeStruct((s, 1024), BF16), jax.ShapeDtypeStruct((256, 256), F32),
                   jax.ShapeDtypeStruct((8, 256), F32), jax.ShapeDtypeStruct((8, 256), F32)],
        compiler_params=_params("arbitrary"),
    )(dyb, dyc, z, z, z, z, wbd, pscale, convw)


_NT = (((1,), (1,)), ((), ()))
_TN = (((0,), (0,)), ((), ()))


ATT_Q, ATT_K = 256, 256


def _att_logits(q, k, fr, q0, k0, masked):
    logits = lax.dot_general(q, k, _NT, preferred_element_type=F32) - fr
    if not masked:
        return logits
    row = q0 + lax.broadcasted_iota(jnp.int32, logits.shape, 0)
    col = k0 + lax.broadcasted_iota(jnp.int32, logits.shape, 1)
    return jnp.where(row >= col, logits, NEG_INF)


def _causal_sweep(step, qi, init):
    n_full = (qi * ATT_Q) // ATT_K
    carry = lax.fori_loop(0, n_full, lambda j, carry: step(j, carry, False), init)
    return step(n_full, carry, True)


HEAD_PAIRS = HEADS // 2


def _lane_pick(v, lane, idx):
    return jnp.sum(jnp.where(lane == idx, v, 0.0), axis=-1, keepdims=True)


def _lane_put(lane, idx, col):
    return jnp.where(lane == idx, col, 0.0)


def _split_heads(v, low):
    zero = jnp.zeros_like(v)
    return jnp.where(low, v, zero), jnp.where(low, zero, v)


def _attn_fwd(qkv, fr):
    s = qkv.shape[0]
    nk = s // ATT_K

    def body(q_ref, k_ref, v_ref, fr_ref, o_ref, lse_ref):
        qi, hp = pl.program_id(0), pl.program_id(1)
        lane = lax.broadcasted_iota(jnp.int32, (ATT_Q, 128), 1)
        low = lane < HEAD_DIM
        qs = _split_heads(q_ref[...] * (HEAD_DIM ** -0.5), low)
        def step(j, carry, masked):
            k0 = pl.multiple_of(j * ATT_K, ATT_K)
            k2, v2 = k_ref[pl.ds(k0, ATT_K), :], v_ref[pl.ds(k0, ATT_K), :]
            out = []
            for sub in (0, 1):
                m, l, acc = carry[sub]
                logits = _att_logits(qs[sub], k2, fr_ref[sub, pl.ds(j, 1), :], qi * ATT_Q, k0, masked)
                m_new = jnp.maximum(m, jnp.max(logits, axis=-1, keepdims=True))
                p = jnp.exp(logits - m_new)
                alpha = jnp.exp(m - m_new)
                l = alpha * l + jnp.sum(p, axis=-1, keepdims=True)
                acc = alpha * acc + jnp.dot(p.astype(BF16), v2, preferred_element_type=F32)
                out.append((m_new, l, acc))
            return tuple(out)

        one = (jnp.full((ATT_Q, 1), NEG_INF, F32), jnp.zeros((ATT_Q, 1), F32), jnp.zeros((ATT_Q, 128), F32))
        (m0, l0, acc0), (m1, l1, acc1) = _causal_sweep(step, qi, (one, one))
        o_ref[...] = jnp.where(low, acc0 / l0, acc1 / l1)

        @pl.when(hp == 0)
        def _():
            lse_ref[...] = jnp.zeros_like(lse_ref)

        lse_ref[...] += _lane_put(lane, 2 * hp, m0 + jnp.log(l0)) + _lane_put(lane, 2 * hp + 1, m1 + jnp.log(l1))

    return pl.pallas_call(
        body, name="attn_fwd", grid=(s // ATT_Q, HEAD_PAIRS),
        in_specs=[pl.BlockSpec((ATT_Q, 128), lambda i, hp: (i, hp)),
                  pl.BlockSpec((s, 128), lambda i, hp: (0, HEAD_PAIRS + hp)),
                  pl.BlockSpec((s, 128), lambda i, hp: (0, 2 * HEAD_PAIRS + hp)),
                  pl.BlockSpec((2, nk, ATT_K), lambda i, hp: (hp, 0, 0))],
        out_specs=[pl.BlockSpec((ATT_Q, 128), lambda i, hp: (i, hp)), pl.BlockSpec((ATT_Q, 128), lambda i, hp: (i, 0))],
        out_shape=[jax.ShapeDtypeStruct((s, A_WIDTH), F32), jax.ShapeDtypeStruct((s, 128), F32)],
        compiler_params=_params("parallel", "arbitrary"),
    )(qkv, qkv, qkv, fr)


def _attn_bwd(qkv, do, o, lse, fr):
    s = qkv.shape[0]
    nk = s // ATT_K
    scale = HEAD_DIM ** -0.5

    def body(q_ref, k_ref, v_ref, do_ref, o_ref, lse_ref, fr_ref, dq_ref, dk_ref, dv_ref, dfc_ref, dfr_ref, dk_acc, dv_acc):
        hp = pl.program_id(0)
        lane = lax.broadcasted_iota(jnp.int32, (ATT_Q, 128), 1)
        low = lane < HEAD_DIM
        low_k = lax.broadcasted_iota(jnp.int32, (ATT_K, 128), 1) < HEAD_DIM
        dk_acc[...] = jnp.zeros_like(dk_acc)
        dv_acc[...] = jnp.zeros_like(dv_acc)
        dfr_ref[...] = jnp.zeros_like(dfr_ref)

        @pl.when(hp == 0)
        def _():
            dfc_ref[...] = jnp.zeros_like(dfc_ref)

        def outer(i, carry):
            q0 = pl.multiple_of(i * ATT_Q, ATT_Q)
            rows = pl.ds(q0, ATT_Q)
            q2, do2 = q_ref[rows, :] * scale, do_ref[rows, :]
            prod = do2 * o_ref[rows, :]
            deltas = (jnp.sum(jnp.where(low, prod, 0.0), axis=-1, keepdims=True),
                      jnp.sum(jnp.where(low, 0.0, prod), axis=-1, keepdims=True))
            dob2 = do2.astype(BF16)
            qs, dos = _split_heads(q2, low), _split_heads(dob2, low)
            lsev = lse_ref[rows, :]
            lses = (_lane_pick(lsev, lane, 2 * hp), _lane_pick(lsev, lane, 2 * hp + 1))

            def inner(j, carry, masked):
                k0 = pl.multiple_of(j * ATT_K, ATT_K)
                cols = pl.ds(k0, ATT_K)
                k2, v2 = k_ref[cols, :], v_ref[cols, :]
                out, dk_parts, dv_parts = [], [], []
                for sub in (0, 1):
                    dq, dfc = carry[sub]
                    p = jnp.exp(_att_logits(qs[sub], k2, fr_ref[sub, pl.ds(j, 1), :], q0, k0, masked) - lses[sub])
                    dp = lax.dot_general(dos[sub], v2, _NT, preferred_element_type=F32)
                    ds = p * (dp - deltas[sub])
                    dsb = ds.astype(BF16)
                    dk_parts.append(lax.dot_general(dsb, q2, _TN, preferred_element_type=F32))
                    dv_parts.append(lax.dot_general(p.astype(BF16), dob2, _TN, preferred_element_type=F32))
                    dfr_ref[sub, pl.ds(j, 1), :] -= jnp.sum(ds, axis=0, keepdims=True)
                    out.append((dq + jnp.dot(dsb, k2, preferred_element_type=F32), dfc + jnp.sum(ds, axis=-1, keepdims=True)))
                dk_acc[cols, :] += jnp.where(low_k, dk_parts[0], dk_parts[1])
                dv_acc[cols, :] += jnp.where(low_k, dv_parts[0], dv_parts[1])
                return tuple(out)

            one = (jnp.zeros((ATT_Q, 128), F32), jnp.zeros((ATT_Q, 1), F32))
            (dq0, dfc0), (dq1, dfc1) = _causal_sweep(inner, i, (one, one))
            dq_ref[rows, :] = (jnp.where(low, dq0, dq1) * scale).astype(BF16)
            dfc_ref[rows, :] += _lane_put(lane, 2 * hp, dfc0) + _lane_put(lane, 2 * hp + 1, dfc1)
            return carry

        lax.fori_loop(0, s // ATT_Q, outer, 0)
        dk_ref[...] = dk_acc[...].astype(BF16)
        dv_ref[...] = dv_acc[...].astype(BF16)

    pair = lambda first: pl.BlockSpec((s, 128), lambda hp, first=first: (0, first + hp))
    whole = pl.BlockSpec((s, 128), lambda hp: (0, 0))
    rowv = pl.BlockSpec((2, nk, ATT_K), lambda hp: (hp, 0, 0))
    return pl.pallas_call(
        body, name="attn_bwd", grid=(HEAD_PAIRS,),
        in_specs=[pair(0), pair(HEAD_PAIRS), pair(2 * HEAD_PAIRS), pair(0), pair(0), whole, rowv],
        out_specs=[pair(0), pair(0), pair(0), whole, rowv],
        out_shape=[jax.ShapeDtypeStruct((s, A_WIDTH), BF16)] * 3 + [jax.ShapeDtypeStruct((s, 128), F32), jax.ShapeDtypeStruct((HEADS, nk, ATT_K), F32)],
        scratch_shapes=[pltpu.VMEM((s, 128), F32), pltpu.VMEM((s, 128), F32)],
        compiler_params=_params("arbitrary"),
    )(qkv, qkv, qkv, do, o, lse, fr)


def _ada_fwd(c_all, w_ada, b_loc):
    depth, _, n = w_ada.shape
    tn = 512

    def body(c_ref, w_ref, b_ref, o_ref, sc_ref):
        cv = c_ref[...]
        sc = cv * jax.nn.sigmoid(cv)
        sc_ref[...] = sc
        o_ref[0] = jnp.dot(sc.astype(BF16), w_ref[0].astype(BF16), preferred_element_type=F32) + b_ref[0]

    return pl.pallas_call(
        body, name="ada_fwd", grid=(depth, n // tn),
        in_specs=[pl.BlockSpec((N_DEV, D), lambda l, j: (0, 0)), pl.BlockSpec((1, D, tn), lambda l, j: (l, 0, j)),
                  pl.BlockSpec((1, 1, tn), lambda l, j: (l, 0, j))],
        out_specs=[pl.BlockSpec((1, N_DEV, tn), lambda l, j: (l, 0, j)), pl.BlockSpec((N_DEV, D), lambda l, j: (0, 0))],
        out_shape=[jax.ShapeDtypeStruct((depth, N_DEV, n), F32), jax.ShapeDtypeStruct((N_DEV, D), F32)],
        compiler_params=_params("arbitrary", "arbitrary"),
    )(c_all, w_ada, b_loc)


def _sum_devices(gathered):
    n = gathered.shape[1]
    tn = _pick(n, (1408, 1024, 640, 512, 128))

    def body(g_ref, o_ref):
        acc = g_ref[0:8, :]
        for dev in range(1, N_DEV):
            acc = acc + g_ref[8 * dev:8 * dev + 8, :]
        o_ref[...] = acc

    return pl.pallas_call(
        body, name="sum_devices", grid=(n // tn,),
        in_specs=[pl.BlockSpec((8 * N_DEV, tn), lambda j: (0, j))], out_specs=pl.BlockSpec((8, tn), lambda j: (0, j)),
        out_shape=jax.ShapeDtypeStruct((8, n), F32), compiler_params=_params("parallel"),
    )(gathered)


def _place():
    x, y, c = lax.axis_index("x"), lax.axis_index("y"), lax.axis_index("c")
    chips = [(1 - x, y), (x, 1 - y), (1 - x, 1 - y)]
    return x, y, c, chips


def _allgather8(block, name):
    m_per, n = block.shape

    def body(x_ref, out_ref, send_sems, recv_sems, local_sem):
        x, y, c, chips = _place()
        me, sibling = (x, y, c), (x, y, 1 - c)

        def rows(px, py, pc):
            return out_ref.at[pl.ds((4 * px + 2 * py + pc) * m_per, m_per), :]

        def copy(k, blk, to, src=None):
            return pltpu.make_async_remote_copy(
                src_ref=rows(*blk) if src is None else src, dst_ref=rows(*blk),
                send_sem=send_sems.at[k], recv_sem=recv_sems.at[k], device_id=to, device_id_type=MESH)

        mine = pltpu.make_async_copy(x_ref, rows(*me), local_sem)
        mine.start()
        first = [copy(0, me, sibling, src=x_ref)]
        first += [copy(1 + j, me, (*chip, c), src=x_ref) for j, chip in enumerate(chips)]
        for cp in first:
            cp.start()
        passed = [copy(4 + j, (*chip, c), sibling) for j, chip in enumerate(chips)]
        for j, chip in enumerate(chips):
            copy(1 + j, (*chip, c), me).wait_recv()
            passed[j].start()
        copy(0, sibling, me).wait_recv()
        for j, chip in enumerate(chips):
            copy(4 + j, (*chip, 1 - c), me).wait_recv()
        for cp in first + passed:
            cp.wait_send()
        mine.wait()

    return pl.pallas_call(
        body, name=name, out_shape=jax.ShapeDtypeStruct((N_DEV * m_per, n), block.dtype),
        in_specs=[pl.BlockSpec(memory_space=pltpu.VMEM)], out_specs=pl.BlockSpec(memory_space=pltpu.VMEM),
        scratch_shapes=[pltpu.SemaphoreType.DMA((7,)), pltpu.SemaphoreType.DMA((7,)), pltpu.SemaphoreType.DMA],
        compiler_params=pltpu.CompilerParams(vmem_limit_bytes=V7X_VMEM_LIMIT),
    )(block)


_SEM = pl.BlockSpec(memory_space=pltpu.SEMAPHORE)
_DATAFLOW = pltpu.SideEffectType.DATAFLOW_SIDE_EFFECTING


def _plan_copies(plan, refs, send_sems, recv_sems):
    return [pltpu.make_async_remote_copy(src_ref=src, dst_ref=dst, send_sem=send_sems.at[i], recv_sem=recv_sems.at[i],
                                         device_id=to, device_id_type=MESH) for i, (src, dst, to) in enumerate(plan(refs))]


def _copies_start(bufs, plan, n_copies, after, name):
    nb = len(bufs)

    def body(*refs):
        for cp in _plan_copies(plan, refs[:nb], refs[nb + 1], refs[nb + 2]):
            cp.start()
        token = refs[-1]
        token[...] = jnp.zeros_like(token)

    sem = pltpu.SemaphoreType.DMA((n_copies,))
    outs = pl.pallas_call(
        body, name=name,
        out_shape=(sem, sem, *[pltpu.HBM(b.shape, b.dtype) for b in bufs], jax.ShapeDtypeStruct((8, 128), F32)),
        in_specs=[_HBM] * nb + [pl.BlockSpec(memory_space=pl.ANY)],
        out_specs=(_SEM, _SEM, *[_HBM] * nb, pl.BlockSpec(memory_space=pltpu.VMEM)),
        input_output_aliases={i: 2 + i for i in range(nb)},
        compiler_params=pltpu.CompilerParams(has_side_effects=_DATAFLOW),
    )(*[pltpu.with_memory_space_constraint(b, pltpu.HBM) for b in bufs], after)
    return outs[0], outs[1], list(outs[2:2 + nb]), outs[-1]


def _copies_wait(started, plan, after, name):
    send_sems, recv_sems, bufs, _ = started
    nb = len(bufs)

    def body(*refs):
        for cp in _plan_copies(plan, refs[:nb], refs[nb], refs[nb + 1]):
            cp.wait_send()
            cp.wait_recv()

    return list(pl.pallas_call(
        body, name=name, out_shape=tuple(pltpu.HBM(b.shape, b.dtype) for b in bufs),
        in_specs=[_HBM] * nb + [_SEM, _SEM, pl.BlockSpec(memory_space=pl.ANY)], out_specs=tuple([_HBM] * nb),
        input_output_aliases={i: i for i in range(nb)},
        compiler_params=pltpu.CompilerParams(has_side_effects=_DATAFLOW),
    )(*bufs, send_sems, recv_sems, after))


def _half_rows(ref, axis, c):
    half = ref.shape[axis] // 2
    return pl.ds(c * half, half)


def _plan_gather_ici(refs):
    n = len(refs) // 2
    x, y, c, chips = _place()
    out = []
    for a in range(n):
        rows = _half_rows(refs[a], 0, c)
        out += [(refs[a].at[rows], refs[n + a].at[2 * x + y, rows], (*chip, c)) for chip in chips]
        out.append((refs[a], refs[n + a].at[2 * x + y], (x, y, 1 - c)))
    return out


def _plan_gather_d2d(refs):
    x, y, c, chips = _place()
    out = []
    for ref in refs:
        rows = _half_rows(ref, 1, c)
        for px, py in chips:
            landed = ref.at[2 * px + py, rows]
            out.append((landed, landed, (x, y, 1 - c)))
    return out


def _plan_rs_sibling(refs):
    n = len(refs) // 2
    x, y, c, _ = _place()
    return [(refs[a].at[pl.ds(0, N_CHIPS), _half_rows(refs[a], 1, 1 - c)], refs[n + a], (x, y, 1 - c)) for a in range(n)]


def _plan_rs_chips(refs):
    n = len(refs) // 2
    x, y, c, chips = _place()
    return [(refs[a].at[2 * px + py], refs[n + a].at[k], (px, py, c)) for a in range(n) for k, (px, py) in enumerate(chips)]


def _plan_rs_share(refs):
    n = len(refs) // 2
    x, y, c, _ = _place()
    return [(refs[a], refs[n + a], (x, y, 1 - c)) for a in range(n)]


def _chip_sum(g, other, sel, name):
    _, half, cdim = other.shape
    tr = _pick(half, (256, 128, 64))
    per = half // tr

    def body(sel_ref, g_ref, t_ref, wire_ref, own_ref):
        total = g_ref[0] + t_ref[0]
        wire_ref[0] = total.astype(BF16)

        @pl.when(pl.program_id(1) == sel_ref[1])
        def _():
            own_ref[...] = total

    blk = pl.BlockSpec((1, tr, cdim), lambda i, p, sel_ref: (p, i, 0))
    return pl.pallas_call(
        body, name=name,
        grid_spec=pltpu.PrefetchScalarGridSpec(
            num_scalar_prefetch=1, grid=(per, N_CHIPS),
            in_specs=[pl.BlockSpec((1, tr, cdim), lambda i, p, sel_ref: (p, sel_ref[0] * per + i, 0)), blk],
            out_specs=[blk, pl.BlockSpec((tr, cdim), lambda i, p, sel_ref: (i, 0))]),
        out_shape=[jax.ShapeDtypeStruct(other.shape, BF16), jax.ShapeDtypeStruct((half, cdim), F32)],
        compiler_params=_params("parallel", "arbitrary"),
    )(sel, g, other)


def _final_sum(own, recv, name):
    r, cdim = own.shape
    tr = _pick(r, (256, 128))

    def body(own_ref, r0_ref, r1_ref, r2_ref, o_ref):
        o_ref[...] = ((own_ref[...] + r0_ref[0].astype(F32)) + r1_ref[0].astype(F32)) + r2_ref[0].astype(F32)

    part = lambda k: pl.BlockSpec((1, tr, cdim), lambda i, k=k: (k, i, 0))
    spec = pl.BlockSpec((tr, cdim), lambda i: (i, 0))
    return pl.pallas_call(
        body, name=name, grid=(r // tr,), in_specs=[spec, part(0), part(1), part(2)], out_specs=spec,
        out_shape=jax.ShapeDtypeStruct((r, cdim), F32), compiler_params=_params("parallel"),
    )(own, recv, recv, recv)


def _row(v):
    return v.reshape(1, -1)


def _tie(v, token):
    return v if token is None else v + token[0:1, 0:1]


def _no_hook(point, after, ready=None):
    return None


def _layer_fwd(x, w, mod, hook=_no_hook):
    s = x.shape[0]
    h = _modnorm_fwd(x, _row(w["g_mix_pre"]), mod[0:1], mod[1:2], "mix_pre_fwd")
    z = _mm(h, w["w_all"], name="mm_in")
    qkv = z[:, Z_QKV:Z_PC].astype(BF16)
    fl = z[:, Z_FL:Z_COLS]
    cum = _cumf_fwd(fl, w["b_f_pad"])
    fr = cum[:, :HEADS].T.reshape(HEADS, s // ATT_K, ATT_K)
    br_a, lse = _attn_fwd(qkv, fr)
    br_b, br_c = _poolconv_fwd(z, w["w_pool_bd"], _tie(_row(w["pool_scale"]), hook("attn", lse)), w["conv_w"])
    hook("pool", br_b)
    wbr = w["w_branch"]
    pa = _mm(br_a, wbr[:A_WIDTH], name="mm_br_a")
    pb = _mm(br_b, wbr[A_WIDTH:A_WIDTH + POOL_WIDTH], name="mm_br_b")
    pc = _mm(br_c, wbr[A_WIDTH + POOL_WIDTH:], name="mm_br_c")
    merged = _merge_fwd(z, pa, pb, pc)
    y = _mm(merged, w["w_out"], name="mm_out")
    x1 = _post_fwd(x, y, _row(w["g_mix_post"]), mod[2:3], "mix_post_fwd")
    h2 = _modnorm_fwd(x1, _row(w["g_ff_pre"]), mod[3:4], mod[4:5], "ff_pre_fwd")
    a, r = _mm(h2, w["w_ff1"], b_split=N_CHIPS, epilogue=_relu2_fwd, out_dtype=(F32, BF16), name="mm_ff1")
    y2 = _mm(r, w["w_ff2"], name="mm_ff2")
    x2 = _post_fwd(x1, y2, _tie(_row(w["g_ff_post"]), hook("ff_post", y2)), mod[5:6], "ff_post_fwd")
    hook("end", x2)
    saved = dict(x=x, h=h, z=z, qkv=qkv, fl=fl, fr=fr, lse=lse, br_a=br_a, br_b=br_b, br_c=br_c, pa=pa, pb=pb, pc=pc,
                 merged=merged, y=y, x1=x1, h2=h2, a=a, r=r, y2=y2)
    return x2, saved


def _layer_bwd(dx2, sv, w, mod, hook=_no_hook):
    s = dx2.shape[0]
    dy2, sum_ff_post = _post_bwd(dx2, sv["y2"], _row(w["g_ff_post"]), mod[5:6], "ff_post_bwd")
    (da,) = _mm(dy2, w["w_ff2"], tb=True, epilogue=_relu2_bwd, extras=(sv["a"],), out_dtype=(BF16,), name="mm_ff2_dx")
    d_w_ff2 = _mm(sv["r"], dy2, ta=True, name="mm_ff2_dw")
    dh2 = _mm(da, w["w_ff1"], tb=True, b_split=N_CHIPS, name="mm_ff1_dx")
    d_w_ff1 = _mm(sv["h2"], da, ta=True, out_split=N_CHIPS, name="mm_ff1_dw")
    dx1, sum_ff_pre = _modnorm_bwd(dh2, sv["x1"], dx2, _tie(_row(w["g_ff_pre"]), hook("ff_pre", dh2, dict(w_ff1=d_w_ff1, w_ff2=d_w_ff2))), mod[4:5], "ff_pre_bwd")

    dy, sum_mix_post = _post_bwd(dx1, sv["y"], _row(w["g_mix_post"]), mod[2:3], "mix_post_bwd")
    dmerged = _mm(dy, w["w_out"], tb=True, name="mm_out_dx")
    d_w_out = _mm(sv["merged"], dy, ta=True, name="mm_out_dw")
    dgl, dpa, dpb, dpc = _merge_bwd(dmerged, sv["z"], sv["pa"], sv["pb"], sv["pc"])
    wbr = w["w_branch"]
    dbr_a = _mm(dpa, wbr[:A_WIDTH], tb=True, name="mm_br_a_dx")
    dbr_b = _mm(dpb, wbr[A_WIDTH:A_WIDTH + POOL_WIDTH], tb=True, name="mm_br_b_dx")
    dbr_c = _mm(dpc, wbr[A_WIDTH + POOL_WIDTH:], tb=True, name="mm_br_c_dx")
    d_w_branch = jnp.concatenate([_mm(sv["br_a"], dpa, ta=True, name="mm_br_a_dw"), _mm(sv["br_b"], dpb, ta=True, name="mm_br_b_dw"),
                                  _mm(sv["br_c"], dpc, ta=True, name="mm_br_c_dw")], axis=0)

    dq, dk, dv, dfc, dfr = _attn_bwd(sv["qkv"], dbr_a, sv["br_a"], sv["lse"], sv["fr"])
    dcum = dfc + jnp.pad(dfr.reshape(HEADS, s).T, ((0, 0), (0, 128 - HEADS)))
    dfl, sum_bf = _cumf_bwd(dcum, sv["fl"], _tie(w["b_f_pad"], hook("cumf", dfc)))
    dpc_z, d_wbd, sum_ps, sum_cw = _poolconv_bwd(dbr_b, dbr_c, sv["z"], w["w_pool_bd"], _row(w["pool_scale"]), w["conv_w"])
    dz = jnp.concatenate([dgl, dq, dk, dv, dpc_z, dfl], axis=1)
    dh = _mm(dz, w["w_all"], tb=True, name="mm_in_dx")
    d_w_all = _mm(sv["h"], dz, ta=True, name="mm_in_dw")
    hook("mix_pre", dh)
    dx, sum_mix_pre = _modnorm_bwd(dh, sv["x"], dx1, _row(w["g_mix_pre"]), mod[1:2], "mix_pre_bwd")

    dmod = jnp.stack([sum_mix_pre[0], sum_mix_pre[1], sum_mix_post[0], sum_ff_pre[0], sum_ff_pre[1], sum_ff_post[0]])
    d_w_in = _w_in_shards(d_w_all)
    d_w_pool = jnp.stack([d_wbd[64 * g:64 * g + 64, 64 * g:64 * g + 64] for g in range(4)])
    big = dict(w_in=d_w_in, w_branch=d_w_branch, w_out=d_w_out, w_ff1=d_w_ff1, w_ff2=d_w_ff2)
    small = dict(g_mix_pre=sum_mix_pre[2], g_mix_post=sum_mix_post[1], g_ff_pre=sum_ff_pre[2], g_ff_post=sum_ff_post[1],
                 b_f=sum_bf[0, :HEADS], w_pool=d_w_pool, pool_scale=sum_ps[0], conv_w=sum_cw[0:3])
    return dx, dmod, big, small


_QKV_END, _FL_END, _PC_END = 3 * A_WIDTH, 3 * A_WIDTH + HEADS, 3 * A_WIDTH + HEADS + POOL_WIDTH + 3 * CONV_WIDTH
_W_IN_GROUPS = ((_PC_END, IN_COLS, Z_GL), (0, _QKV_END, Z_QKV), (_FL_END, _PC_END, Z_PC), (_QKV_END, _FL_END, Z_FL))
_SHARD_COLS = IN_COLS // N_CHIPS


def _w_all_from_shards(blocks):
    pieces = []
    for lo, hi, _ in _W_IN_GROUPS:
        for p in range(N_CHIPS):
            a, b = max(lo, p * _SHARD_COLS), min(hi, (p + 1) * _SHARD_COLS)
            if a < b:
                pieces.append(blocks[p][:, a - p * _SHARD_COLS:b - p * _SHARD_COLS])
    pieces.append(jnp.zeros((D, Z_COLS - IN_COLS), blocks.dtype))
    return jnp.concatenate(pieces, axis=1)


def _w_in_shards(d_w_all):
    blocks = []
    for p in range(N_CHIPS):
        pieces = []
        for lo, hi, at in sorted(_W_IN_GROUPS):
            a, b = max(lo, p * _SHARD_COLS), min(hi, (p + 1) * _SHARD_COLS)
            if a < b:
                pieces.append(d_w_all[:, at + a - lo:at + b - lo])
        blocks.append(jnp.concatenate(pieces, axis=1))
    return jnp.stack(blocks)


def _full_layer_weights(w_in_blocks, w_branch, w_out, w_ff1, w_ff2, g_mix_pre, g_mix_post, g_ff_pre, g_ff_post, b_f, w_pool, pool_scale, conv_w):
    w_all = _w_all_from_shards(w_in_blocks)
    wbd = jnp.zeros((POOL_WIDTH, POOL_WIDTH), F32)
    for g in range(4):
        wbd = wbd.at[64 * g:64 * g + 64, 64 * g:64 * g + 64].set(w_pool[g])
    return dict(w_all=w_all, w_branch=w_branch, w_out=w_out, w_ff1=w_ff1, w_ff2=w_ff2, g_mix_pre=g_mix_pre, g_mix_post=g_mix_post,
                g_ff_pre=g_ff_pre, g_ff_post=g_ff_post, b_f_pad=jnp.pad(b_f, (0, 128 - HEADS)).reshape(1, 128), w_pool_bd=wbd,
                pool_scale=pool_scale, conv_w=conv_w)


class _NoComm:
    def layer_weights(self, l):
        raise NotImplementedError

    def fwd_hook(self, l):
        return _no_hook

    def bwd_hook(self, l):
        return _no_hook

    def grads_ready(self, l, big):
        return None


class _Layers(_NoComm):
    def __init__(self, layers):
        self.layers = layers

    def layer_weights(self, l):
        return self.layers[l]


def _local_step(x, target, mods, comm):
    saved, weights = [], []
    act = x
    for l in range(DEPTH):
        weights.append(comm.layer_weights(l))
        act, sv = _layer_fwd(act, weights[l], mods[l], comm.fwd_hook(l))
        saved.append(sv)
    dact, sq = _loss_head(act, target)
    loss = sq[0, 0] * (0.5 / D)
    dmods, bigs, smalls = [None] * DEPTH, [None] * DEPTH, [None] * DEPTH
    token = None
    for l in reversed(range(DEPTH)):
        dact, dmods[l], bigs[l], smalls[l] = _layer_bwd(dact, saved[l], weights[l], _tie(mods[l], token), comm.bwd_hook(l))
        token = comm.grads_ready(l, bigs[l])
    return loss, dact, jnp.stack(dmods), bigs, smalls


_BIG = ("w_in", "w_branch", "w_out", "w_ff1", "w_ff2")
N_BIG = len(_BIG)


class _GatherJob:
    def __init__(self, tag, shards, after):
        self.tag, self.n = tag, len(shards)
        lands = [lax.empty((N_CHIPS,) + s.shape, s.dtype) for s in shards]
        self.state = _copies_start(list(shards) + lands, _plan_gather_ici, 4 * self.n, after, "gather_ici_start_" + tag)
        self.token = self.state[3]

    def pass_on(self, after):
        bufs = _copies_wait(self.state, _plan_gather_ici, after, "gather_ici_wait_" + self.tag)
        self.state = _copies_start(bufs[self.n:], _plan_gather_d2d, 3 * self.n, bufs[0], "gather_d2d_start_" + self.tag)
        self.token = self.state[3]
        return self.token

    def done(self, after):
        return _copies_wait(self.state, _plan_gather_d2d, after, "gather_d2d_wait_" + self.tag)


class _ReduceJob:
    def __init__(self, tag, names, grads, sel, after):
        self.tag, self.names, self.n, self.sel = tag, names, len(names), sel
        lands = [lax.empty((N_CHIPS, g.shape[1] // 2, g.shape[2]), F32) for g in grads]
        self.state = _copies_start(list(grads) + lands, _plan_rs_sibling, self.n, after, "rs_sibling_start_" + tag)
        self.token = self.state[3]

    def chip_sums(self, after):
        bufs = _copies_wait(self.state, _plan_rs_sibling, after, "rs_sibling_wait_" + self.tag)
        wires, self.owns = zip(*[_chip_sum(bufs[i], bufs[self.n + i], self.sel, "rs_chip_sum_" + name) for i, name in enumerate(self.names)])
        lands = [lax.empty((3,) + w.shape[1:], BF16) for w in wires]
        self.state = _copies_start(list(wires) + lands, _plan_rs_chips, 3 * self.n, self.owns[0], "rs_chips_start_" + self.tag)
        self.token = self.state[3]
        return self.token

    def final_sums(self, after):
        bufs = _copies_wait(self.state, _plan_rs_chips, after, "rs_chips_wait_" + self.tag)
        sums = [_final_sum(self.owns[i], bufs[self.n + i], "rs_final_" + name) for i, name in enumerate(self.names)]
        self.state = _copies_start(sums + [lax.empty(s.shape, F32) for s in sums], _plan_rs_share, self.n, sums[0], "rs_share_start_" + self.tag)
        self.token = self.state[3]
        return self.token

    def done(self, after):
        bufs = _copies_wait(self.state, _plan_rs_share, after, "rs_share_wait_" + self.tag)
        low = self.sel[0] == 0
        return {name: jnp.where(low, jnp.concatenate([mine, theirs]), jnp.concatenate([theirs, mine]))
                for name, mine, theirs in zip(self.names, bufs[:self.n], bufs[self.n:])}


def _chip_blocks(g):
    return g if g.ndim == 3 else g.reshape(N_CHIPS, -1, g.shape[1])


class _StepComm(_NoComm):
    def __init__(self, shards, sel, after):
        self.sel = sel
        self.small, self.grads, self.jobs = None, [dict() for _ in range(DEPTH)], {}
        self.jobs["in0"] = _GatherJob("in0", shards[0][:1], after)
        self.jobs["rest0"] = _GatherJob("rest0", shards[0][1:], self.jobs["in0"].token)
        self.jobs["all1"] = _GatherJob("all1", shards[1], self.jobs["rest0"].token)

    def layer_weights(self, l):
        if l == 0:
            job = self.jobs["in0"]
            (g_in,) = job.done(job.pass_on(self.jobs["all1"].token))
            self.weights0 = _full_layer_weights(g_in, None, None, None, None, *self.small[0])
            return self.weights0
        g_in, g_br, g_out, g_f1, g_f2 = self.landed1
        return _full_layer_weights(g_in, g_br.reshape(D, D), g_out.reshape(D, D), g_f1, g_f2.reshape(D_FF, D), *self.small[1])

    def fwd_hook(self, l):
        if l != 0:
            return _no_hook

        def hook(point, after, ready=None):
            if point == "attn":
                return self.jobs["rest0"].pass_on(after)
            if point == "ff_post":
                return self.jobs["all1"].pass_on(after)
            if point == "pool":
                g_br, g_out, g_f1, g_f2 = self.jobs["rest0"].done(after)
                self.weights0.update(w_branch=g_br.reshape(D, D), w_out=g_out.reshape(D, D), w_ff1=g_f1, w_ff2=g_f2.reshape(D_FF, D))
            if point == "end":
                self.landed1 = self.jobs["all1"].done(after)
            return None
        return hook

    def bwd_hook(self, l):
        if l != 0:
            return _no_hook

        def hook(point, after, ready=None):
            jobs = self.jobs
            if point == "ff_pre":
                token = jobs["rs1"].chip_sums(after)
                jobs["rs0_ff"] = _ReduceJob("0_ff", ("w_ff1", "w_ff2"), [_chip_blocks(ready[n]) for n in ("w_ff1", "w_ff2")], self.sel, token)
                return jobs["rs0_ff"].token
            if point == "cumf":
                return jobs["rs0_ff"].chip_sums(jobs["rs1"].final_sums(after))
            self.grads[1] = jobs["rs1"].done(after)
            return None
        return hook

    def grads_ready(self, l, big):
        if l == 1:
            self.jobs["rs1"] = _ReduceJob("1", _BIG, [_chip_blocks(big[n]) for n in _BIG], self.sel, self.sel)
            return self.jobs["rs1"].token
        names = ("w_in", "w_branch", "w_out")
        self.jobs["rs0_mix"] = _ReduceJob("0_mix", names, [_chip_blocks(big[n]) for n in names], self.sel, self.sel)
        return self.jobs["rs0_mix"].token

    def finish_sums(self, after):
        jobs = self.jobs
        token = jobs["rs0_mix"].chip_sums(after)
        return jobs["rs0_ff"].final_sums(token)

    def finish(self, after):
        jobs = self.jobs
        self.grads[0] = jobs["rs0_ff"].done(after)
        self.grads[0].update(jobs["rs0_mix"].done(jobs["rs0_mix"].final_sums(after)))


_SMALL = ("g_mix_pre", "g_mix_post", "g_ff_pre", "g_ff_post", "b_f", "w_pool", "pool_scale", "conv_w")


def _w_in_view(t):
    return t.reshape(DEPTH, D // 128, 128, _SHARD_COLS).transpose(3, 1, 0, 2).reshape(_SHARD_COLS * (D // 128) * DEPTH, 128)


def _w_in_unview(t):
    return t.reshape(_SHARD_COLS, D // 128, DEPTH, 128).transpose(2, 1, 3, 0).reshape(DEPTH, D, _SHARD_COLS)


def _pack(parts, rows=8):
    flat = jnp.concatenate([p.reshape(-1) for p in parts])
    width = -(-flat.shape[0] // (rows * 128)) * 128
    return jnp.pad(flat, (0, rows * width - flat.shape[0])).reshape(rows, width)


def _unpack(packed, like):
    flat = packed.reshape(-1)
    out, at = [], 0
    for ref in like:
        out.append(flat[at:at + ref.size].reshape(ref.shape))
        at += ref.size
    return out


def kernel(x, c, w_ada, b_ada, g_mix_pre, g_mix_post, g_ff_pre, g_ff_post, w_in, b_f, w_pool, pool_scale, conv_w, w_branch, w_out, w_ff1, w_ff2, loss_target, m_w_ada, m_b_ada, m_g_mix_pre, m_g_mix_post, m_g_ff_pre, m_g_ff_post, m_w_in, m_b_f, m_w_pool, m_pool_scale, m_conv_w, m_w_branch, m_w_out, m_w_ff1, m_w_ff2, v_w_ada, v_b_ada, v_g_mix_pre, v_g_mix_post, v_g_ff_pre, v_g_ff_post, v_w_in, v_b_f, v_w_pool, v_pool_scale, v_conv_w, v_w_branch, v_w_out, v_w_ff1, v_w_ff2):
    xi, yi, ci = lax.axis_index("x"), lax.axis_index("y"), lax.axis_index("c")
    chip = 2 * xi + yi
    dev = 2 * chip + ci
    n_ada = w_ada.shape[2]

    first = jnp.zeros((8, D + 384), F32).at[0, :D].set(c[0]).at[0, D:].set(conv_w.reshape(-1))
    got = _allgather8(first, "gather_cond").reshape(N_DEV, 8, D + 384)[:, 0]
    c_all = got[:, :D]
    conv_full = got[0::2, D:].reshape(N_CHIPS, DEPTH, 3, CONV_WIDTH // N_CHIPS).transpose(1, 2, 0, 3).reshape(DEPTH, 3, CONV_WIDTH)

    b_loc = lax.dynamic_slice_in_dim(b_ada, chip * n_ada, n_ada, axis=1).reshape(DEPTH, 1, n_ada)
    mod_cols, silu_c = _ada_fwd(c_all, w_ada, b_loc)
    got = _allgather8(mod_cols.reshape(DEPTH * N_DEV, n_ada), "gather_mod").reshape(N_DEV, DEPTH, N_DEV, n_ada)[0::2]
    mod_all = got.transpose(1, 2, 0, 3).reshape(DEPTH, N_DEV, 6, D)
    mods = lax.dynamic_index_in_dim(mod_all, dev, axis=1, keepdims=False)

    comm = _StepComm([[w[l].astype(BF16) for w in (w_in, w_branch, w_out, w_ff1, w_ff2)] for l in range(DEPTH)],
                     jnp.stack([ci, chip]).astype(jnp.int32), mods)
    comm.small = [(g_mix_pre[l], g_mix_post[l], g_ff_pre[l], g_ff_post[l], b_f[l], w_pool[l], pool_scale[l], conv_full[l]) for l in range(DEPTH)]
    loss_part, grad_x, dmods, bigs, smalls = _local_step(x[0], loss_target[0], mods, comm)
    loss = lax.psum(loss_part, ("x", "y", "c"))

    small_parts = [smalls[l][name] for name in _SMALL for l in range(DEPTH)]
    packed = _tie(_pack([dmods] + small_parts), comm.jobs["rs0_mix"].token)
    gathered = _allgather8(packed, "gather_small")
    dmod_all = gathered.reshape(N_DEV, -1)[:, :dmods.size].reshape(N_DEV, DEPTH, 6 * D)
    summed = _unpack(_sum_devices(gathered), [dmods] + small_parts)
    grad_b_ada = summed[0].reshape(DEPTH, 6 * D)
    small_grads = {name: jnp.stack(summed[1 + 2 * i:3 + 2 * i]) for i, name in enumerate(_SMALL)}
    small_grads["conv_w"] = lax.dynamic_slice_in_dim(small_grads["conv_w"], chip * (CONV_WIDTH // N_CHIPS), CONV_WIDTH // N_CHIPS, axis=2)

    dmod_loc = lax.dynamic_slice_in_dim(dmod_all.transpose(1, 0, 2), chip * n_ada, n_ada, axis=2)
    tail_token = comm.finish_sums(grad_b_ada)
    silu_pad = _tie(jnp.pad(silu_c, ((0, 128 - N_DEV), (0, 0))), tail_token)
    grad_w_ada = jnp.stack([_mm(silu_pad, jnp.pad(dmod_loc[l], ((0, 128 - N_DEV), (0, 0))), ta=True, name="mm_ada_dw") for l in range(DEPTH)])

    grads = dict(w_ada=grad_w_ada, b_ada=grad_b_ada, **small_grads)
    weights = dict(w_ada=w_ada, b_ada=b_ada, g_mix_pre=g_mix_pre, g_mix_post=g_mix_post, g_ff_pre=g_ff_pre, g_ff_post=g_ff_post, w_in=w_in,
                   b_f=b_f, w_pool=w_pool, pool_scale=pool_scale, conv_w=conv_w, w_branch=w_branch, w_out=w_out, w_ff1=w_ff1, w_ff2=w_ff2)
    m_in = dict(w_ada=m_w_ada, b_ada=m_b_ada, g_mix_pre=m_g_mix_pre, g_mix_post=m_g_mix_post, g_ff_pre=m_g_ff_pre, g_ff_post=m_g_ff_post,
                w_in=m_w_in, b_f=m_b_f, w_pool=m_w_pool, pool_scale=m_pool_scale, conv_w=m_conv_w, w_branch=m_w_branch, w_out=m_w_out,
                w_ff1=m_w_ff1, w_ff2=m_w_ff2)
    v_in = dict(w_ada=v_w_ada, b_ada=v_b_ada, g_mix_pre=v_g_mix_pre, g_mix_post=v_g_mix_post, g_ff_pre=v_g_ff_pre, g_ff_post=v_g_ff_post,
                w_in=v_w_in, b_f=v_b_f, w_pool=v_w_pool, pool_scale=v_pool_scale, conv_w=v_conv_w, w_branch=v_w_branch, w_out=v_w_out,
                w_ff1=v_w_ff1, w_ff2=v_w_ff2)
    order = ("w_ada", "b_ada", "g_mix_pre", "g_mix_post", "g_ff_pre", "g_ff_post", "w_in", "b_f", "w_pool", "pool_scale", "conv_w",
             "w_branch", "w_out", "w_ff1", "w_ff2")
    delta, new_m, new_v = {}, {}, {}
    tiny = ("b_ada",) + _SMALL
    packed_w, packed_g, packed_m, packed_v = [_pack([src[name] for name in tiny]) for src in (weights, grads, m_in, v_in)]
    res = _adamw(packed_w, _tie(packed_g, tail_token), packed_m, packed_v, "adamw_small")
    for out, packed_res in zip((delta, new_m, new_v), res):
        for name, val in zip(tiny, _unpack(packed_res, [weights[name] for name in tiny])):
            out[name] = val
    delta["w_ada"], new_m["w_ada"], new_v["w_ada"] = _adamw(w_ada, grad_w_ada, m_w_ada, v_w_ada, "adamw_w_ada")
    comm.finish(delta["w_ada"][0, :8, :128] + delta["b_ada"][0, :128])
    for name in _BIG:
        grads[name] = jnp.stack([comm.grads[l][name] for l in range(DEPTH)])
        if name == "w_in":
            g_view = lax.optimization_barrier(_w_in_view(grads[name]))
            res = _adamw(_w_in_view(w_in), g_view, _w_in_view(m_w_in), _w_in_view(v_w_in), "adamw_w_in")
            grads[name], delta[name], new_m[name], new_v[name] = [_w_in_unview(t) for t in (g_view, *res)]
        else:
            delta[name], new_m[name], new_v[name] = _adamw(weights[name], grads[name], m_in[name], v_in[name], "adamw_" + name)

    return (loss, grad_x[None], *[grads[n] for n in order], *[delta[n] for n in order], *[new_m[n] for n in order],
            *[new_v[n] for n in order])
```

```python
import functools

import jax
import jax.numpy as jnp
from jax import lax
from jax.experimental import pallas as pl
from jax.experimental.pallas import tpu as pltpu

F32 = jnp.float32
BF16 = jnp.bfloat16
MESH = pl.DeviceIdType.MESH

D = 1024
DEPTH = 2
HEADS = 8
HEAD_DIM = 64
A_WIDTH = 512
POOL_WIDTH = 256
CONV_WIDTH = 256
D_FF = 4096
IN_COLS = 5640
Z_GL, Z_QKV, Z_PC, Z_FL, Z_COLS = 0, 3072, 4608, 5632, 5760
RMS_EPS = 1e-6
NEG_INF = -1e30
ROW_TILE = 256
N_CHIPS = 4
N_DEV = 8
V7X_VMEM_LIMIT = 48 * 1024 * 1024

ADAM_LR = 0.001
ADAM_B1 = 0.9
ADAM_B2 = 0.999
ADAM_EPS = 1e-08
ADAM_WD = 0.01
ADAM_STEP = 10

_HBM = pl.BlockSpec(memory_space=pltpu.HBM)


def _params(*sem):
    return pltpu.CompilerParams(dimension_semantics=sem, vmem_limit_bytes=V7X_VMEM_LIMIT)


def _pick(dim, cands):
    for cand in cands:
        if dim % cand == 0:
            return cand
    return dim


def _mm(a, b, *, ta=False, tb=False, b_split=1, out_split=1, out_dtype=F32, epilogue=None, extras=(), name):
    (k, m) = a.shape if ta else a.shape[::-1]
    b_rows, b_cols = b.shape[-2], b.shape[-1] * b_split
    (n, k2) = (b_rows, b_cols) if tb else (b_cols, b_rows)
    assert k == k2, (a.shape, b.shape, ta, tb)
    n_unit = n // (out_split * (1 if tb else b_split))
    k_unit = k // (b_split if tb else 1)
    tm = _pick(m, (1024, 512, 256, 128))
    tn = _pick(n_unit, (1024, 1152, 768, 640, 512, 256, 128))
    tk = _pick(k_unit, (1024, 1152, 512, 640, 256, 128))
    nk = k // tk
    dims = (((0 if ta else 1,), (1 if tb else 0,)), ((), ()))

    def dot(a_ref, b_ref):
        b_val = b_ref[0] if b_split > 1 else b_ref[...]
        return lax.dot_general(a_ref[...].astype(BF16), b_val.astype(BF16), dims, preferred_element_type=F32)

    n_extra = len(extras)
    assert epilogue is None or out_split == 1

    def put(refs, val):
        if epilogue is not None:
            for o_ref, res in zip(refs[n_extra:], epilogue(val, *[r[...] for r in refs[:n_extra]])):
                o_ref[...] = res.astype(o_ref.dtype)
        elif out_split > 1:
            refs[0][0] = val.astype(refs[0].dtype)
        else:
            refs[0][...] = val.astype(refs[0].dtype)

    def body_single(a_ref, b_ref, *refs):
        put(refs, dot(a_ref, b_ref))

    def body_acc(a_ref, b_ref, *refs):
        kk = pl.program_id(2)
        acc_ref = refs[-1]

        @pl.when(kk == 0)
        def _():
            acc_ref[...] = jnp.zeros_like(acc_ref)

        acc_ref[...] += dot(a_ref, b_ref)

        @pl.when(kk == nk - 1)
        def _():
            put(refs[:-1], acc_ref[...])

    a_spec = pl.BlockSpec((tk, tm), lambda i, j, kk: (kk, i)) if ta else pl.BlockSpec((tm, tk), lambda i, j, kk: (i, kk))
    if b_split == 1:
        b_spec = pl.BlockSpec((tn, tk), lambda i, j, kk: (j, kk)) if tb else pl.BlockSpec((tk, tn), lambda i, j, kk: (kk, j))
    elif tb:
        per = k_unit // tk
        b_spec = pl.BlockSpec((1, tn, tk), lambda i, j, kk: (kk // per, j, kk % per))
    else:
        per = n // b_split // tn
        b_spec = pl.BlockSpec((1, tk, tn), lambda i, j, kk: (j // per, kk, j % per))
    if out_split == 1:
        o_spec = pl.BlockSpec((tm, tn), lambda i, j, kk: (i, j))
        o_shape = None if epilogue is not None else jax.ShapeDtypeStruct((m, n), out_dtype)
    else:
        per_o = n // out_split // tn
        o_spec = pl.BlockSpec((1, tm, tn), lambda i, j, kk: (j // per_o, i, j % per_o))
        o_shape = jax.ShapeDtypeStruct((out_split, m, n // out_split), out_dtype)
    if epilogue is not None:
        o_shape = [jax.ShapeDtypeStruct((m, n), dt) for dt in out_dtype]
        o_spec = [o_spec] * len(out_dtype)
    return pl.pallas_call(
        body_single if nk == 1 else body_acc, name=name, grid=(m // tm, n // tn, nk),
        in_specs=[a_spec, b_spec] + [pl.BlockSpec((tm, tn), lambda i, j, kk: (i, j))] * n_extra, out_specs=o_spec, out_shape=o_shape,
        scratch_shapes=[] if nk == 1 else [pltpu.VMEM((tm, tn), F32)],
        compiler_params=_params("parallel", "parallel", "arbitrary"),
    )(a, b, *extras)


def _ew(fn, ins, out_dtypes, name, tc=None):
    shape = ins[0].shape
    lead, (rows, cols) = shape[:-2], shape[-2:]
    tc = cols if tc is None else tc
    tr = _pick(rows, (ROW_TILE, 128, 8)) if tc > 128 else _pick(rows, (4096, 2256, 2048, 1024, ROW_TILE, 8))
    n_in = len(ins)

    def body(*refs):
        res = fn(*[r[...] for r in refs[:n_in]])
        for o_ref, val in zip(refs[n_in:], res):
            o_ref[...] = val.astype(o_ref.dtype)

    if lead:
        spec = pl.BlockSpec((None, tr, tc), lambda l, i, j: (l, i, j))
    else:
        spec = pl.BlockSpec((tr, tc), lambda i, j: (i, j))
    return pl.pallas_call(
        body, name=name, grid=lead + (rows // tr, cols // tc),
        in_specs=[spec] * n_in, out_specs=[spec] * len(out_dtypes),
        out_shape=[jax.ShapeDtypeStruct(shape, dt) for dt in out_dtypes],
        compiler_params=_params(*(["parallel"] * (len(lead) + 2))),
    )(*ins)


def _relu2_fwd(a):
    r = jnp.maximum(a, 0.0)
    return a, r * r


def _relu2_bwd(dr, a):
    return (dr * (2.0 * jnp.maximum(a, 0.0)),)


def _adamw(w, g, m, v, name):
    bc1 = 1.0 - ADAM_B1 ** ADAM_STEP
    bc2 = 1.0 - ADAM_B2 ** ADAM_STEP

    def fn(w, g, m, v):
        m = ADAM_B1 * m + (1.0 - ADAM_B1) * g
        v = ADAM_B2 * v + (1.0 - ADAM_B2) * (g * g)
        m_hat = m / bc1
        v_hat = v / bc2
        delta = -ADAM_LR * (m_hat / (jnp.sqrt(v_hat) + ADAM_EPS) + ADAM_WD * w)
        return delta, m, v
    return _ew(fn, [w, g, m, v], [F32, F32, F32], name)


def _row_spec(cols, block=0):
    return pl.BlockSpec((ROW_TILE, cols), lambda i, block=block: (i, block))


def _vec_spec(cols):
    return pl.BlockSpec((1, cols), lambda i: (0, 0))


def _sum_spec(cols):
    return pl.BlockSpec((8, cols), lambda i: (0, 0))


def _rstd(x):
    return lax.rsqrt(jnp.mean(x * x, axis=-1, keepdims=True) + RMS_EPS)


def _modnorm_fwd(x, g, shift, scale, name):
    s = x.shape[0]

    def body(x_ref, g_ref, sh_ref, sc_ref, h_ref):
        xv = x_ref[...]
        n = xv * _rstd(xv)
        h_ref[...] = ((n * g_ref[...]) * (1.0 + sc_ref[...]) + sh_ref[...]).astype(BF16)

    return pl.pallas_call(
        body, name=name, grid=(s // ROW_TILE,),
        in_specs=[_row_spec(D), _vec_spec(D), _vec_spec(D), _vec_spec(D)], out_specs=_row_spec(D),
        out_shape=jax.ShapeDtypeStruct((s, D), BF16), compiler_params=_params("parallel"),
    )(x, g, shift, scale)


def _post_fwd(x, y, g, gate, name):
    s = x.shape[0]

    def body(x_ref, y_ref, g_ref, gate_ref, o_ref):
        yv = y_ref[...]
        o_ref[...] = x_ref[...] + gate_ref[...] * ((yv * _rstd(yv)) * g_ref[...])

    return pl.pallas_call(
        body, name=name, grid=(s // ROW_TILE,),
        in_specs=[_row_spec(D), _row_spec(D), _vec_spec(D), _vec_spec(D)], out_specs=_row_spec(D),
        out_shape=jax.ShapeDtypeStruct((s, D), F32), compiler_params=_params("parallel"),
    )(x, y, g, gate)


def _post_bwd(dxo, y, g, gate, name):
    s = dxo.shape[0]

    def body(d_ref, y_ref, g_ref, gate_ref, dy_ref, sum_ref):
        @pl.when(pl.program_id(0) == 0)
        def _():
            sum_ref[...] = jnp.zeros_like(sum_ref)

        dv, yv = d_ref[...], y_ref[...]
        r = _rstd(yv)
        n = yv * r
        sum_ref[0:1, :] += jnp.sum(dv * (n * g_ref[...]), axis=0, keepdims=True)
        sum_ref[1:2, :] += jnp.sum((dv * gate_ref[...]) * n, axis=0, keepdims=True)
        dn = (dv * gate_ref[...]) * g_ref[...]
        dy_ref[...] = (r * (dn - n * jnp.mean(dn * n, axis=-1, keepdims=True))).astype(BF16)

    return pl.pallas_call(
        body, name=name, grid=(s // ROW_TILE,),
        in_specs=[_row_spec(D), _row_spec(D), _vec_spec(D), _vec_spec(D)],
        out_specs=[_row_spec(D), _sum_spec(D)],
        out_shape=[jax.ShapeDtypeStruct((s, D), BF16), jax.ShapeDtypeStruct((8, D), F32)],
        compiler_params=_params("arbitrary"),
    )(dxo, y, g, gate)


def _modnorm_bwd(dh, x, dxo, g, scale, name):
    s = dh.shape[0]

    def body(dh_ref, x_ref, d_ref, g_ref, sc_ref, dx_ref, sum_ref):
        @pl.when(pl.program_id(0) == 0)
        def _():
            sum_ref[...] = jnp.zeros_like(sum_ref)

        dhv, xv = dh_ref[...], x_ref[...]
        r = _rstd(xv)
        n = xv * r
        one_sc = 1.0 + sc_ref[...]
        sum_ref[0:1, :] += jnp.sum(dhv, axis=0, keepdims=True)
        sum_ref[1:2, :] += jnp.sum(dhv * (n * g_ref[...]), axis=0, keepdims=True)
        sum_ref[2:3, :] += jnp.sum((dhv * one_sc) * n, axis=0, keepdims=True)
        dn = (dhv * one_sc) * g_ref[...]
        dx_ref[...] = d_ref[...] + r * (dn - n * jnp.mean(dn * n, axis=-1, keepdims=True))

    return pl.pallas_call(
        body, name=name, grid=(s // ROW_TILE,),
        in_specs=[_row_spec(D), _row_spec(D), _row_spec(D), _vec_spec(D), _vec_spec(D)],
        out_specs=[_row_spec(D), _sum_spec(D)],
        out_shape=[jax.ShapeDtypeStruct((s, D), F32), jax.ShapeDtypeStruct((8, D), F32)],
        compiler_params=_params("arbitrary"),
    )(dh, x, dxo, g, scale)


def _loss_head(y, target):
    s = y.shape[0]

    def body(y_ref, t_ref, dy_ref, sum_ref):
        @pl.when(pl.program_id(0) == 0)
        def _():
            sum_ref[...] = jnp.zeros_like(sum_ref)

        err = y_ref[...] - t_ref[...]
        dy_ref[...] = err * (1.0 / D)
        sum_ref[...] += jnp.sum(err * err)

    return pl.pallas_call(
        body, name="loss_head", grid=(s // ROW_TILE,),
        in_specs=[_row_spec(D), _row_spec(D)],
        out_specs=[_row_spec(D), pl.BlockSpec((8, 128), lambda i: (0, 0))],
        out_shape=[jax.ShapeDtypeStruct((s, D), F32), jax.ShapeDtypeStruct((8, 128), F32)],
        compiler_params=_params("arbitrary"),
    )(y, target)


def _merge_fwd(z, pa, pb, pc):
    s = z.shape[0]

    def body(g0_ref, g1_ref, g2_ref, pa_ref, pb_ref, pc_ref, o_ref):
        o_ref[...] = (jax.nn.sigmoid(g0_ref[...]) * pa_ref[...] + jax.nn.sigmoid(g1_ref[...]) * pb_ref[...]
                      + jax.nn.sigmoid(g2_ref[...]) * pc_ref[...]).astype(BF16)

    return pl.pallas_call(
        body, name="merge_fwd", grid=(s // ROW_TILE,),
        in_specs=[_row_spec(D, 0), _row_spec(D, 1), _row_spec(D, 2), _row_spec(D), _row_spec(D), _row_spec(D)],
        out_specs=_row_spec(D), out_shape=jax.ShapeDtypeStruct((s, D), BF16),
        compiler_params=_params("parallel"),
    )(z, z, z, pa, pb, pc)


def _merge_bwd(dm, z, pa, pb, pc):
    s = z.shape[0]

    def body(dm_ref, g0_ref, g1_ref, g2_ref, pa_ref, pb_ref, pc_ref, dgl_ref, da_ref, db_ref, dc_ref):
        dmv = dm_ref[...]
        for i, (g_ref, p_ref, d_ref) in enumerate(((g0_ref, pa_ref, da_ref), (g1_ref, pb_ref, db_ref), (g2_ref, pc_ref, dc_ref))):
            gate = jax.nn.sigmoid(g_ref[...])
            dgl_ref[:, i * D:(i + 1) * D] = ((dmv * p_ref[...]) * (gate * (1.0 - gate))).astype(BF16)
            d_ref[...] = (dmv * gate).astype(BF16)

    return pl.pallas_call(
        body, name="merge_bwd", grid=(s // ROW_TILE,),
        in_specs=[_row_spec(D), _row_spec(D, 0), _row_spec(D, 1), _row_spec(D, 2), _row_spec(D), _row_spec(D), _row_spec(D)],
        out_specs=[_row_spec(3 * D), _row_spec(D), _row_spec(D), _row_spec(D)],
        out_shape=[jax.ShapeDtypeStruct((s, 3 * D), BF16)] + [jax.ShapeDtypeStruct((s, D), BF16)] * 3,
        compiler_params=_params("parallel"),
    )(dm, z, z, z, pa, pb, pc)


def _shift_down(v, n):
    row = lax.broadcasted_iota(jnp.int32, v.shape, 0)
    return jnp.where(row >= n, pltpu.roll(v, n, axis=0), 0.0)


def _shift_up(v, n):
    s = v.shape[0]
    row = lax.broadcasted_iota(jnp.int32, v.shape, 0)
    return jnp.where(row < s - n, pltpu.roll(v, s - n, axis=0), 0.0)


def _log_sigmoid(v):
    return jnp.minimum(v, 0.0) - jnp.log1p(jnp.exp(-jnp.abs(v)))


def _cumf_fwd(fl, bias):
    s = fl.shape[0]

    def body(fl_ref, b_ref, o_ref):
        acc = _log_sigmoid(fl_ref[...] + b_ref[...])
        step = 1
        while step < s:
            acc = acc + _shift_down(acc, step)
            step *= 2
        o_ref[...] = acc

    return pl.pallas_call(body, name="cumf_fwd", out_shape=jax.ShapeDtypeStruct((s, 128), F32),
                          compiler_params=pltpu.CompilerParams(vmem_limit_bytes=V7X_VMEM_LIMIT))(fl, bias)


def _cumf_bwd(dcum, fl, bias):
    s = fl.shape[0]

    def body(d_ref, fl_ref, b_ref, dfl_ref, db_ref):
        acc = d_ref[...]
        step = 1
        while step < s:
            acc = acc + _shift_up(acc, step)
            step *= 2
        dfl = acc * jax.nn.sigmoid(-(fl_ref[...] + b_ref[...]))
        dfl_ref[...] = dfl.astype(BF16)
        db_ref[...] = jnp.broadcast_to(jnp.sum(dfl, axis=0, keepdims=True), (8, 128))

    return pl.pallas_call(
        body, name="cumf_bwd",
        out_shape=[jax.ShapeDtypeStruct((s, 128), BF16), jax.ShapeDtypeStruct((8, 128), F32)],
        compiler_params=pltpu.CompilerParams(vmem_limit_bytes=V7X_VMEM_LIMIT))(dcum, fl, bias)


def _pool_windows(v, shift):
    s2 = v + shift(v, 1)
    s4 = s2 + shift(s2, 2)
    s8 = s4 + shift(s4, 4)
    s16 = s8 + shift(s8, 8)
    group = lax.broadcasted_iota(jnp.int32, v.shape, 1) // 64
    return jnp.where(group == 0, s2, jnp.where(group == 1, s4, jnp.where(group == 2, s8, s16)))


def _pool_count(shape):
    group = lax.broadcasted_iota(jnp.int32, shape, 1) // 64
    window = jnp.where(group == 0, 2.0, jnp.where(group == 1, 4.0, jnp.where(group == 2, 8.0, 16.0)))
    t1 = (lax.broadcasted_iota(jnp.int32, shape, 0) + 1).astype(F32)
    return jnp.minimum(t1, window)


def _pc_specs(s):
    zcol = lambda blk: pl.BlockSpec((s, 256), lambda i, blk=blk: (0, blk))
    first = Z_PC // 256
    return [zcol(first), zcol(first + 1), zcol(first + 2), zcol(first + 3),
            pl.BlockSpec((256, 256), lambda i: (0, 0)), pl.BlockSpec((1, 256), lambda i: (0, 0)),
            pl.BlockSpec((3, 256), lambda i: (0, 0))]


def _poolconv_fwd(z, wbd, pscale, convw):
    s = z.shape[0]

    def body(pu_ref, ch_ref, cb_ref, cc_ref, w_ref, ps_ref, cw_ref, yb_ref, yc_ref):
        u = pu_ref[...]
        p = _pool_windows(u, _shift_down) / _pool_count(u.shape) - u
        yb = jnp.dot(p.astype(BF16), w_ref[...].astype(BF16), preferred_element_type=F32) * ps_ref[...]
        yb_ref[...] = yb.astype(BF16)
        uc = cc_ref[...] * ch_ref[...]
        cw = cw_ref[...]
        conv = cw[0:1, :] * _shift_down(uc, 2) + cw[1:2, :] * _shift_down(uc, 1) + cw[2:3, :] * uc
        yc_ref[...] = (cb_ref[...] * conv).astype(BF16)

    out = pl.BlockSpec((s, 256), lambda i: (0, 0))
    return pl.pallas_call(
        body, name="poolconv_fwd", grid=(1,), in_specs=_pc_specs(s), out_specs=[out, out],
        out_shape=[jax.ShapeDtypeStruct((s, 256), BF16)] * 2, compiler_params=_params("arbitrary"),
    )(z, z, z, z, wbd, pscale, convw)


def _poolconv_bwd(dyb, dyc, z, wbd, pscale, convw):
    s = z.shape[0]

    def body(dyb_ref, dyc_ref, pu_ref, ch_ref, cb_ref, cc_ref, w_ref, ps_ref, cw_ref, dz_ref, dw_ref, dps_ref, dcw_ref):
        u = pu_ref[...]
        count = _pool_count(u.shape)
        p = (_pool_windows(u, _shift_down) / count - u).astype(BF16)
        wb = w_ref[...].astype(BF16)
        dyb_v = dyb_ref[...]
        pw = jnp.dot(p, wb, preferred_element_type=F32)
        dps_ref[...] = jnp.broadcast_to(jnp.sum(dyb_v * pw, axis=0, keepdims=True), (8, 256))
        dys = (dyb_v * ps_ref[...]).astype(BF16)
        dp = lax.dot_general(dys, wb, (((1,), (1,)), ((), ())), preferred_element_type=F32)
        dw_ref[...] = lax.dot_general(p, dys, (((0,), (0,)), ((), ())), preferred_element_type=F32)
        dz_ref[:, 0:256] = (_pool_windows(dp / count, _shift_up) - dp).astype(BF16)

        ch, cb, cc = ch_ref[...], cb_ref[...], cc_ref[...]
        uc = cc * ch
        cw = cw_ref[...]
        u2, u1 = _shift_down(uc, 2), _shift_down(uc, 1)
        conv = cw[0:1, :] * u2 + cw[1:2, :] * u1 + cw[2:3, :] * uc
        dyc_v = dyc_ref[...]
        dconv = dyc_v * cb
        du = cw[0:1, :] * _shift_up(dconv, 2) + cw[1:2, :] * _shift_up(dconv, 1) + cw[2:3, :] * dconv
        dz_ref[:, 256:512] = (du * cc).astype(BF16)
        dz_ref[:, 512:768] = (dyc_v * conv).astype(BF16)
        dz_ref[:, 768:1024] = (du * ch).astype(BF16)
        dcw_ref[...] = jnp.zeros_like(dcw_ref)
        dcw_ref[0:1, :] = jnp.sum(dconv * u2, axis=0, keepdims=True)
        dcw_ref[1:2, :] = jnp.sum(dconv * u1, axis=0, keepdims=True)
        dcw_ref[2:3, :] = jnp.sum(dconv * uc, axis=0, keepdims=True)

    blk = lambda r, c: pl.BlockSpec((r, c), lambda i: (0, 0))
    return pl.pallas_call(
        body, name="poolconv_bwd", grid=(1,),
        in_specs=[blk(s, 256), blk(s, 256)] + _pc_specs(s),
        out_specs=[blk(s, 1024), blk(256, 256), blk(8, 256), blk(8, 256)],
        out_shape=[jax.ShapeDtypeStruct((s, 1024), BF16), jax.ShapeDtypeStruct((256, 256), F32),
                   jax.ShapeDtypeStruct((8, 256), F32), jax.ShapeDtypeStruct((8, 256), F32)],
        compiler_params=_params("arbitrary"),
    )(dyb, dyc, z, z, z, z, wbd, pscale, convw)


_NT = (((1,), (1,)), ((), ()))
_TN = (((0,), (0,)), ((), ()))


ATT_Q, ATT_K = 256, 256
ATT_HEADS_BWD = 4
ATT_HEADS = 8


def _att_logits(q, k, fr, q0, k0, masked):
    logits = lax.dot_general(q, k, _NT, preferred_element_type=F32) - fr
    if not masked:
        return logits
    row = q0 + lax.broadcasted_iota(jnp.int32, logits.shape, 0)
    col = k0 + lax.broadcasted_iota(jnp.int32, logits.shape, 1)
    return jnp.where(row >= col, logits, NEG_INF)


def _causal_sweep(step, qi, init):
    n_full = (qi * ATT_Q) // ATT_K
    carry = lax.fori_loop(0, n_full, lambda j, carry: step(j, carry, False), init)
    return step(n_full, carry, True)


HEAD_PAIRS = HEADS // 2


def _lane_pick(v, lane, idx):
    return jnp.sum(jnp.where(lane == idx, v, 0.0), axis=-1, keepdims=True)


def _lane_put(lane, idx, col):
    return jnp.where(lane == idx, col, 0.0)


def _split_heads(v, low):
    zero = jnp.zeros_like(v)
    return jnp.where(low, v, zero), jnp.where(low, zero, v)


def _attn_fwd(qkv, fr):
    s = qkv.shape[0]
    nk = s // ATT_K
    width = ATT_HEADS * HEAD_DIM
    groups = HEADS // ATT_HEADS

    def body(q_ref, k_ref, v_ref, fr_ref, o_ref, lse_ref):
        qi, grp = pl.program_id(0), pl.program_id(1)
        lane = lax.broadcasted_iota(jnp.int32, (ATT_Q, 128), 1)
        low = lane < HEAD_DIM
        qs = []
        for pr in range(ATT_HEADS // 2):
            qs += _split_heads(q_ref[:, 128 * pr:128 * (pr + 1)] * (HEAD_DIM ** -0.5), low)

        def step(j, carry, masked):
            k0 = pl.multiple_of(j * ATT_K, ATT_K)
            out = []
            for h in range(ATT_HEADS):
                cols = slice(128 * (h // 2), 128 * (h // 2 + 1))
                m, l, acc = carry[h]
                logits = _att_logits(qs[h], k_ref[pl.ds(k0, ATT_K), cols], fr_ref[h, pl.ds(j, 1), :], qi * ATT_Q, k0, masked)
                m_new = jnp.maximum(m, jnp.max(logits, axis=-1, keepdims=True))
                p = jnp.exp(logits - m_new)
                alpha = jnp.exp(m - m_new)
                l = alpha * l + jnp.sum(p, axis=-1, keepdims=True)
                acc = alpha * acc + jnp.dot(p.astype(BF16), v_ref[pl.ds(k0, ATT_K), cols], preferred_element_type=F32)
                out.append((m_new, l, acc))
            return tuple(out)

        one = (jnp.full((ATT_Q, 1), NEG_INF, F32), jnp.zeros((ATT_Q, 1), F32), jnp.zeros((ATT_Q, 128), F32))
        done = _causal_sweep(step, qi, (one,) * ATT_HEADS)

        @pl.when(grp == 0)
        def _():
            lse_ref[...] = jnp.zeros_like(lse_ref)

        lse = jnp.zeros((ATT_Q, 128), F32)
        for pr in range(ATT_HEADS // 2):
            (m0, l0, acc0), (m1, l1, acc1) = done[2 * pr], done[2 * pr + 1]
            o_ref[:, 128 * pr:128 * (pr + 1)] = jnp.where(low, acc0 / l0, acc1 / l1)
            head = ATT_HEADS * grp + 2 * pr
            lse = lse + _lane_put(lane, head, m0 + jnp.log(l0)) + _lane_put(lane, head + 1, m1 + jnp.log(l1))
        lse_ref[...] += lse

    return pl.pallas_call(
        body, name="attn_fwd", grid=(s // ATT_Q, groups),
        in_specs=[pl.BlockSpec((ATT_Q, width), lambda i, g: (i, g)),
                  pl.BlockSpec((s, width), lambda i, g: (0, groups + g)),
                  pl.BlockSpec((s, width), lambda i, g: (0, 2 * groups + g)),
                  pl.BlockSpec((ATT_HEADS, nk, ATT_K), lambda i, g: (g, 0, 0))],
        out_specs=[pl.BlockSpec((ATT_Q, width), lambda i, g: (i, g)), pl.BlockSpec((ATT_Q, 128), lambda i, g: (i, 0))],
        out_shape=[jax.ShapeDtypeStruct((s, A_WIDTH), F32), jax.ShapeDtypeStruct((s, 128), F32)],
        compiler_params=_params("parallel", "arbitrary"),
    )(qkv, qkv, qkv, fr)


def _attn_bwd(qkv, do, o, lse, fr):
    s = qkv.shape[0]
    nk = s // ATT_K
    scale = HEAD_DIM ** -0.5
    heads = ATT_HEADS_BWD
    width = heads * HEAD_DIM
    groups = HEADS // heads

    def body(q_ref, k_ref, v_ref, do_ref, o_ref, lse_ref, fr_ref, dq_ref, dk_ref, dv_ref, dfc_ref, dfr_ref, dk_acc, dv_acc):
        grp = pl.program_id(0)
        lane = lax.broadcasted_iota(jnp.int32, (ATT_Q, 128), 1)
        low = lane < HEAD_DIM
        low_k = lax.broadcasted_iota(jnp.int32, (ATT_K, 128), 1) < HEAD_DIM
        dk_acc[...] = jnp.zeros_like(dk_acc)
        dv_acc[...] = jnp.zeros_like(dv_acc)
        dfr_ref[...] = jnp.zeros_like(dfr_ref)

        @pl.when(grp == 0)
        def _():
            dfc_ref[...] = jnp.zeros_like(dfc_ref)

        def outer(i, carry):
            q0 = pl.multiple_of(i * ATT_Q, ATT_Q)
            rows = pl.ds(q0, ATT_Q)
            lsev = lse_ref[rows, :]
            q2s, dob2s, qs, dos, deltas, lses = [], [], [], [], [], []
            for pr in range(heads // 2):
                pcols = slice(128 * pr, 128 * (pr + 1))
                q2, do2 = q_ref[rows, pcols] * scale, do_ref[rows, pcols]
                prod = do2 * o_ref[rows, pcols]
                deltas += [jnp.sum(jnp.where(low, prod, 0.0), axis=-1, keepdims=True),
                           jnp.sum(jnp.where(low, 0.0, prod), axis=-1, keepdims=True)]
                dob2 = do2.astype(BF16)
                q2s.append(q2)
                dob2s.append(dob2)
                qs += _split_heads(q2, low)
                dos += _split_heads(dob2, low)
                lses += [_lane_pick(lsev, lane, heads * grp + 2 * pr), _lane_pick(lsev, lane, heads * grp + 2 * pr + 1)]

            def inner(j, carry, masked):
                k0 = pl.multiple_of(j * ATT_K, ATT_K)
                krows = pl.ds(k0, ATT_K)
                out, dk_parts, dv_parts = [], [], []
                for h in range(heads):
                    pcols = slice(128 * (h // 2), 128 * (h // 2 + 1))
                    dq, dfc = carry[h]
                    k2 = k_ref[krows, pcols]
                    p = jnp.exp(_att_logits(qs[h], k2, fr_ref[h, pl.ds(j, 1), :], q0, k0, masked) - lses[h])
                    dp = lax.dot_general(dos[h], v_ref[krows, pcols], _NT, preferred_element_type=F32)
                    ds = p * (dp - deltas[h])
                    dsb = ds.astype(BF16)
                    dk_parts.append(lax.dot_general(dsb, q2s[h // 2], _TN, preferred_element_type=F32))
                    dv_parts.append(lax.dot_general(p.astype(BF16), dob2s[h // 2], _TN, preferred_element_type=F32))
                    dfr_ref[h, pl.ds(j, 1), :] -= jnp.sum(ds, axis=0, keepdims=True)
                    out.append((dq + jnp.dot(dsb, k2, preferred_element_type=F32), dfc + jnp.sum(ds, axis=-1, keepdims=True)))
                for pr in range(heads // 2):
                    pcols = slice(128 * pr, 128 * (pr + 1))
                    dk_acc[krows, pcols] += jnp.where(low_k, dk_parts[2 * pr], dk_parts[2 * pr + 1])
                    dv_acc[krows, pcols] += jnp.where(low_k, dv_parts[2 * pr], dv_parts[2 * pr + 1])
                return tuple(out)

            one = (jnp.zeros((ATT_Q, 128), F32), jnp.zeros((ATT_Q, 1), F32))
            done = _causal_sweep(inner, i, (one,) * heads)
            dfc = jnp.zeros((ATT_Q, 128), F32)
            for pr in range(heads // 2):
                (dq0, dfc0), (dq1, dfc1) = done[2 * pr], done[2 * pr + 1]
                dq_ref[rows, 128 * pr:128 * (pr + 1)] = (jnp.where(low, dq0, dq1) * scale).astype(BF16)
                head = heads * grp + 2 * pr
                dfc = dfc + _lane_put(lane, head, dfc0) + _lane_put(lane, head + 1, dfc1)
            dfc_ref[rows, :] += dfc
            return carry

        lax.fori_loop(0, s // ATT_Q, outer, 0)
        dk_ref[...] = dk_acc[...].astype(BF16)
        dv_ref[...] = dv_acc[...].astype(BF16)

    part = lambda first: pl.BlockSpec((s, width), lambda g, first=first: (0, first + g))
    whole = pl.BlockSpec((s, 128), lambda g: (0, 0))
    rowv = pl.BlockSpec((heads, nk, ATT_K), lambda g: (g, 0, 0))
    return pl.pallas_call(
        body, name="attn_bwd", grid=(groups,),
        in_specs=[part(0), part(groups), part(2 * groups), part(0), part(0), whole, rowv],
        out_specs=[part(0), part(0), part(0), whole, rowv],
        out_shape=[jax.ShapeDtypeStruct((s, A_WIDTH), BF16)] * 3 + [jax.ShapeDtypeStruct((s, 128), F32), jax.ShapeDtypeStruct((HEADS, nk, ATT_K), F32)],
        scratch_shapes=[pltpu.VMEM((s, width), F32), pltpu.VMEM((s, width), F32)],
        compiler_params=_params("arbitrary"),
    )(qkv, qkv, qkv, do, o, lse, fr)


def _ada_fwd(c_all, w_ada, b_loc):
    depth, _, n = w_ada.shape
    tn = 512

    def body(c_ref, w_ref, b_ref, o_ref, sc_ref):
        cv = c_ref[...]
        sc = cv * jax.nn.sigmoid(cv)
        sc_ref[...] = sc
        o_ref[0] = jnp.dot(sc.astype(BF16), w_ref[0].astype(BF16), preferred_element_type=F32) + b_ref[0]

    return pl.pallas_call(
        body, name="ada_fwd", grid=(depth, n // tn),
        in_specs=[pl.BlockSpec((N_DEV, D), lambda l, j: (0, 0)), pl.BlockSpec((1, D, tn), lambda l, j: (l, 0, j)),
                  pl.BlockSpec((1, 1, tn), lambda l, j: (l, 0, j))],
        out_specs=[pl.BlockSpec((1, N_DEV, tn), lambda l, j: (l, 0, j)), pl.BlockSpec((N_DEV, D), lambda l, j: (0, 0))],
        out_shape=[jax.ShapeDtypeStruct((depth, N_DEV, n), F32), jax.ShapeDtypeStruct((N_DEV, D), F32)],
        compiler_params=_params("arbitrary", "arbitrary"),
    )(c_all, w_ada, b_loc)


def _sum_devices(gathered):
    n = gathered.shape[1]
    tn = _pick(n, (1408, 1024, 640, 512, 128))

    def body(g_ref, o_ref):
        acc = g_ref[0:8, :]
        for dev in range(1, N_DEV):
            acc = acc + g_ref[8 * dev:8 * dev + 8, :]
        o_ref[...] = acc

    return pl.pallas_call(
        body, name="sum_devices", grid=(n // tn,),
        in_specs=[pl.BlockSpec((8 * N_DEV, tn), lambda j: (0, j))], out_specs=pl.BlockSpec((8, tn), lambda j: (0, j)),
        out_shape=jax.ShapeDtypeStruct((8, n), F32), compiler_params=_params("parallel"),
    )(gathered)


def _place():
    x, y, c = lax.axis_index("x"), lax.axis_index("y"), lax.axis_index("c")
    chips = [(1 - x, y), (x, 1 - y), (1 - x, 1 - y)]
    return x, y, c, chips


def _allgather8(block, name):
    m_per, n = block.shape

    def body(x_ref, out_ref, send_sems, recv_sems, local_sem):
        x, y, c, chips = _place()
        me, sibling = (x, y, c), (x, y, 1 - c)

        def rows(px, py, pc):
            return out_ref.at[pl.ds((4 * px + 2 * py + pc) * m_per, m_per), :]

        def copy(k, blk, to, src=None):
            return pltpu.make_async_remote_copy(
                src_ref=rows(*blk) if src is None else src, dst_ref=rows(*blk),
                send_sem=send_sems.at[k], recv_sem=recv_sems.at[k], device_id=to, device_id_type=MESH)

        mine = pltpu.make_async_copy(x_ref, rows(*me), local_sem)
        mine.start()
        first = [copy(0, me, sibling, src=x_ref)]
        first += [copy(1 + j, me, (*chip, c), src=x_ref) for j, chip in enumerate(chips)]
        for cp in first:
            cp.start()
        passed = [copy(4 + j, (*chip, c), sibling) for j, chip in enumerate(chips)]
        for j, chip in enumerate(chips):
            copy(1 + j, (*chip, c), me).wait_recv()
            passed[j].start()
        copy(0, sibling, me).wait_recv()
        for j, chip in enumerate(chips):
            copy(4 + j, (*chip, 1 - c), me).wait_recv()
        for cp in first + passed:
            cp.wait_send()
        mine.wait()

    return pl.pallas_call(
        body, name=name, out_shape=jax.ShapeDtypeStruct((N_DEV * m_per, n), block.dtype),
        in_specs=[pl.BlockSpec(memory_space=pltpu.VMEM)], out_specs=pl.BlockSpec(memory_space=pltpu.VMEM),
        scratch_shapes=[pltpu.SemaphoreType.DMA((7,)), pltpu.SemaphoreType.DMA((7,)), pltpu.SemaphoreType.DMA],
        compiler_params=pltpu.CompilerParams(vmem_limit_bytes=V7X_VMEM_LIMIT),
    )(block)


_SEM = pl.BlockSpec(memory_space=pltpu.SEMAPHORE)
_DATAFLOW = pltpu.SideEffectType.DATAFLOW_SIDE_EFFECTING


def _plan_copies(plan, refs, send_sems, recv_sems):
    return [pltpu.make_async_remote_copy(src_ref=src, dst_ref=dst, send_sem=send_sems.at[i], recv_sem=recv_sems.at[i],
                                         device_id=to, device_id_type=MESH) for i, (src, dst, to) in enumerate(plan(refs))]


def _copies_start(bufs, plan, n_copies, after, name):
    nb = len(bufs)

    def body(*refs):
        for cp in _plan_copies(plan, refs[:nb], refs[nb + 1], refs[nb + 2]):
            cp.start()
        token = refs[-1]
        token[...] = jnp.zeros_like(token)

    sem = pltpu.SemaphoreType.DMA((n_copies,))
    outs = pl.pallas_call(
        body, name=name,
        out_shape=(sem, sem, *[pltpu.HBM(b.shape, b.dtype) for b in bufs], jax.ShapeDtypeStruct((8, 128), F32)),
        in_specs=[_HBM] * nb + [pl.BlockSpec(memory_space=pl.ANY)],
        out_specs=(_SEM, _SEM, *[_HBM] * nb, pl.BlockSpec(memory_space=pltpu.VMEM)),
        input_output_aliases={i: 2 + i for i in range(nb)},
        compiler_params=pltpu.CompilerParams(has_side_effects=_DATAFLOW),
    )(*[pltpu.with_memory_space_constraint(b, pltpu.HBM) for b in bufs], after)
    return outs[0], outs[1], list(outs[2:2 + nb]), outs[-1]


def _copies_wait(started, plan, after, name):
    send_sems, recv_sems, bufs, _ = started
    nb = len(bufs)

    def body(*refs):
        for cp in _plan_copies(plan, refs[:nb], refs[nb], refs[nb + 1]):
            cp.wait_send()
            cp.wait_recv()

    return list(pl.pallas_call(
        body, name=name, out_shape=tuple(pltpu.HBM(b.shape, b.dtype) for b in bufs),
        in_specs=[_HBM] * nb + [_SEM, _SEM, pl.BlockSpec(memory_space=pl.ANY)], out_specs=tuple([_HBM] * nb),
        input_output_aliases={i: i for i in range(nb)},
        compiler_params=pltpu.CompilerParams(has_side_effects=_DATAFLOW),
    )(*bufs, send_sems, recv_sems, after))


def _half_rows(ref, axis, c):
    half = ref.shape[axis] // 2
    return pl.ds(c * half, half)


def _plan_gather_ici(refs):
    n = len(refs) // 2
    x, y, c, chips = _place()
    out = []
    for a in range(n):
        rows = _half_rows(refs[a], 0, c)
        out += [(refs[a].at[rows], refs[n + a].at[2 * x + y, rows], (*chip, c)) for chip in chips]
        out.append((refs[a], refs[n + a].at[2 * x + y], (x, y, 1 - c)))
    return out


def _plan_gather_d2d(refs):
    x, y, c, chips = _place()
    out = []
    for ref in refs:
        rows = _half_rows(ref, 1, c)
        for px, py in chips:
            landed = ref.at[2 * px + py, rows]
            out.append((landed, landed, (x, y, 1 - c)))
    return out


def _plan_rs_sibling(refs):
    n = len(refs) // 2
    x, y, c, _ = _place()
    return [(refs[a].at[pl.ds(0, N_CHIPS), _half_rows(refs[a], 1, 1 - c)], refs[n + a], (x, y, 1 - c)) for a in range(n)]


def _plan_rs_chips(refs):
    n = len(refs) // 2
    x, y, c, chips = _place()
    return [(refs[a].at[2 * px + py], refs[n + a].at[k], (px, py, c)) for a in range(n) for k, (px, py) in enumerate(chips)]


def _plan_rs_share(refs):
    n = len(refs) // 2
    x, y, c, _ = _place()
    return [(refs[a], refs[n + a], (x, y, 1 - c)) for a in range(n)]


def _chip_sum(g, other, sel, name):
    _, half, cdim = other.shape
    tr = _pick(half, (256, 128, 64))
    per = half // tr

    def body(sel_ref, g_ref, t_ref, wire_ref, own_ref):
        total = g_ref[0] + t_ref[0]
        wire_ref[0] = total.astype(BF16)

        @pl.when(pl.program_id(1) == sel_ref[1])
        def _():
            own_ref[...] = total

    blk = pl.BlockSpec((1, tr, cdim), lambda i, p, sel_ref: (p, i, 0))
    return pl.pallas_call(
        body, name=name,
        grid_spec=pltpu.PrefetchScalarGridSpec(
            num_scalar_prefetch=1, grid=(per, N_CHIPS),
            in_specs=[pl.BlockSpec((1, tr, cdim), lambda i, p, sel_ref: (p, sel_ref[0] * per + i, 0)), blk],
            out_specs=[blk, pl.BlockSpec((tr, cdim), lambda i, p, sel_ref: (i, 0))]),
        out_shape=[jax.ShapeDtypeStruct(other.shape, BF16), jax.ShapeDtypeStruct((half, cdim), F32)],
        compiler_params=_params("parallel", "arbitrary"),
    )(sel, g, other)


def _final_sum(own, recv, name):
    r, cdim = own.shape
    tr = _pick(r, (256, 128))

    def body(own_ref, r0_ref, r1_ref, r2_ref, o_ref):
        o_ref[...] = ((own_ref[...] + r0_ref[0].astype(F32)) + r1_ref[0].astype(F32)) + r2_ref[0].astype(F32)

    part = lambda k: pl.BlockSpec((1, tr, cdim), lambda i, k=k: (k, i, 0))
    spec = pl.BlockSpec((tr, cdim), lambda i: (i, 0))
    return pl.pallas_call(
        body, name=name, grid=(r // tr,), in_specs=[spec, part(0), part(1), part(2)], out_specs=spec,
        out_shape=jax.ShapeDtypeStruct((r, cdim), F32), compiler_params=_params("parallel"),
    )(own, recv, recv, recv)


def _row(v):
    return v.reshape(1, -1)


def _tie(v, token):
    return v if token is None else v + token[0:1, 0:1]


def _no_hook(point, after, ready=None):
    return None


def _layer_fwd(x, w, mod, hook=_no_hook):
    s = x.shape[0]
    h = _modnorm_fwd(x, _row(w["g_mix_pre"]), mod[0:1], mod[1:2], "mix_pre_fwd")
    z = _mm(h, w["w_all"], name="mm_in")
    qkv = z[:, Z_QKV:Z_PC].astype(BF16)
    fl = z[:, Z_FL:Z_COLS]
    cum = _cumf_fwd(fl, w["b_f_pad"])
    fr = cum[:, :HEADS].T.reshape(HEADS, s // ATT_K, ATT_K)
    br_a, lse = _attn_fwd(qkv, fr)
    br_b, br_c = _poolconv_fwd(z, w["w_pool_bd"], _tie(_row(w["pool_scale"]), hook("attn", lse)), w["conv_w"])
    hook("pool", br_b)
    wbr = w["w_branch"]
    pa = _mm(br_a, wbr[:A_WIDTH], name="mm_br_a")
    pb = _mm(br_b, wbr[A_WIDTH:A_WIDTH + POOL_WIDTH], name="mm_br_b")
    pc = _mm(br_c, wbr[A_WIDTH + POOL_WIDTH:], name="mm_br_c")
    merged = _merge_fwd(z, pa, pb, pc)
    y = _mm(merged, w["w_out"], name="mm_out")
    x1 = _post_fwd(x, y, _row(w["g_mix_post"]), mod[2:3], "mix_post_fwd")
    h2 = _modnorm_fwd(x1, _row(w["g_ff_pre"]), mod[3:4], mod[4:5], "ff_pre_fwd")
    a, r = _mm(h2, w["w_ff1"], b_split=N_CHIPS, epilogue=_relu2_fwd, out_dtype=(F32, BF16), name="mm_ff1")
    y2 = _mm(r, w["w_ff2"], name="mm_ff2")
    x2 = _post_fwd(x1, y2, _tie(_row(w["g_ff_post"]), hook("ff_post", y2)), mod[5:6], "ff_post_fwd")
    hook("end", x2)
    saved = dict(x=x, h=h, z=z, qkv=qkv, fl=fl, fr=fr, lse=lse, br_a=br_a, br_b=br_b, br_c=br_c, pa=pa, pb=pb, pc=pc,
                 merged=merged, y=y, x1=x1, h2=h2, a=a, r=r, y2=y2)
    return x2, saved


def _layer_bwd(dx2, sv, w, mod, hook=_no_hook):
    s = dx2.shape[0]
    dy2, sum_ff_post = _post_bwd(dx2, sv["y2"], _row(w["g_ff_post"]), mod[5:6], "ff_post_bwd")
    (da,) = _mm(dy2, w["w_ff2"], tb=True, epilogue=_relu2_bwd, extras=(sv["a"],), out_dtype=(BF16,), name="mm_ff2_dx")
    d_w_ff2 = _mm(sv["r"], dy2, ta=True, name="mm_ff2_dw")
    dh2 = _mm(da, w["w_ff1"], tb=True, b_split=N_CHIPS, name="mm_ff1_dx")
    d_w_ff1 = _mm(sv["h2"], da, ta=True, out_split=N_CHIPS, name="mm_ff1_dw")
    dx1, sum_ff_pre = _modnorm_bwd(dh2, sv["x1"], dx2, _tie(_row(w["g_ff_pre"]), hook("ff_pre", dh2, dict(w_ff1=d_w_ff1, w_ff2=d_w_ff2))), mod[4:5], "ff_pre_bwd")

    dy, sum_mix_post = _post_bwd(dx1, sv["y"], _row(w["g_mix_post"]), mod[2:3], "mix_post_bwd")
    dmerged = _mm(dy, w["w_out"], tb=True, name="mm_out_dx")
    d_w_out = _mm(sv["merged"], dy, ta=True, name="mm_out_dw")
    dgl, dpa, dpb, dpc = _merge_bwd(dmerged, sv["z"], sv["pa"], sv["pb"], sv["pc"])
    wbr = w["w_branch"]
    dbr_a = _mm(dpa, wbr[:A_WIDTH], tb=True, name="mm_br_a_dx")
    dbr_b = _mm(dpb, wbr[A_WIDTH:A_WIDTH + POOL_WIDTH], tb=True, name="mm_br_b_dx")
    dbr_c = _mm(dpc, wbr[A_WIDTH + POOL_WIDTH:], tb=True, name="mm_br_c_dx")
    d_w_branch = jnp.concatenate([_mm(sv["br_a"], dpa, ta=True, name="mm_br_a_dw"), _mm(sv["br_b"], dpb, ta=True, name="mm_br_b_dw"),
                                  _mm(sv["br_c"], dpc, ta=True, name="mm_br_c_dw")], axis=0)

    dq, dk, dv, dfc, dfr = _attn_bwd(sv["qkv"], dbr_a, sv["br_a"], sv["lse"], sv["fr"])
    dcum = dfc + jnp.pad(dfr.reshape(HEADS, s).T, ((0, 0), (0, 128 - HEADS)))
    dfl, sum_bf = _cumf_bwd(dcum, sv["fl"], _tie(w["b_f_pad"], hook("cumf", dfc)))
    dpc_z, d_wbd, sum_ps, sum_cw = _poolconv_bwd(dbr_b, dbr_c, sv["z"], w["w_pool_bd"], _row(w["pool_scale"]), w["conv_w"])
    dz = jnp.concatenate([dgl, dq, dk, dv, dpc_z, dfl], axis=1)
    dh = _mm(dz, w["w_all"], tb=True, name="mm_in_dx")
    d_w_all = _mm(sv["h"], dz, ta=True, name="mm_in_dw")
    hook("mix_pre", dh)
    dx, sum_mix_pre = _modnorm_bwd(dh, sv["x"], dx1, _row(w["g_mix_pre"]), mod[1:2], "mix_pre_bwd")

    dmod = jnp.stack([sum_mix_pre[0], sum_mix_pre[1], sum_mix_post[0], sum_ff_pre[0], sum_ff_pre[1], sum_ff_post[0]])
    d_w_in = _w_in_shards(d_w_all)
    d_w_pool = jnp.stack([d_wbd[64 * g:64 * g + 64, 64 * g:64 * g + 64] for g in range(4)])
    big = dict(w_in=d_w_in, w_branch=d_w_branch, w_out=d_w_out, w_ff1=d_w_ff1, w_ff2=d_w_ff2)
    small = dict(g_mix_pre=sum_mix_pre[2], g_mix_post=sum_mix_post[1], g_ff_pre=sum_ff_pre[2], g_ff_post=sum_ff_post[1],
                 b_f=sum_bf[0, :HEADS], w_pool=d_w_pool, pool_scale=sum_ps[0], conv_w=sum_cw[0:3])
    return dx, dmod, big, small


_QKV_END, _FL_END, _PC_END = 3 * A_WIDTH, 3 * A_WIDTH + HEADS, 3 * A_WIDTH + HEADS + POOL_WIDTH + 3 * CONV_WIDTH
_W_IN_GROUPS = ((_PC_END, IN_COLS, Z_GL), (0, _QKV_END, Z_QKV), (_FL_END, _PC_END, Z_PC), (_QKV_END, _FL_END, Z_FL))
_SHARD_COLS = IN_COLS // N_CHIPS


def _w_all_from_shards(blocks):
    pieces = []
    for lo, hi, _ in _W_IN_GROUPS:
        for p in range(N_CHIPS):
            a, b = max(lo, p * _SHARD_COLS), min(hi, (p + 1) * _SHARD_COLS)
            if a < b:
                pieces.append(blocks[p][:, a - p * _SHARD_COLS:b - p * _SHARD_COLS])
    pieces.append(jnp.zeros((D, Z_COLS - IN_COLS), blocks.dtype))
    return jnp.concatenate(pieces, axis=1)


def _w_in_shards(d_w_all):
    blocks = []
    for p in range(N_CHIPS):
        pieces = []
        for lo, hi, at in sorted(_W_IN_GROUPS):
            a, b = max(lo, p * _SHARD_COLS), min(hi, (p + 1) * _SHARD_COLS)
            if a < b:
                pieces.append(d_w_all[:, at + a - lo:at + b - lo])
        blocks.append(jnp.concatenate(pieces, axis=1))
    return jnp.stack(blocks)


def _full_layer_weights(w_in_blocks, w_branch, w_out, w_ff1, w_ff2, g_mix_pre, g_mix_post, g_ff_pre, g_ff_post, b_f, w_pool, pool_scale, conv_w):
    w_all = _w_all_from_shards(w_in_blocks)
    wbd = jnp.zeros((POOL_WIDTH, POOL_WIDTH), F32)
    for g in range(4):
        wbd = wbd.at[64 * g:64 * g + 64, 64 * g:64 * g + 64].set(w_pool[g])
    return dict(w_all=w_all, w_branch=w_branch, w_out=w_out, w_ff1=w_ff1, w_ff2=w_ff2, g_mix_pre=g_mix_pre, g_mix_post=g_mix_post,
                g_ff_pre=g_ff_pre, g_ff_post=g_ff_post, b_f_pad=jnp.pad(b_f, (0, 128 - HEADS)).reshape(1, 128), w_pool_bd=wbd,
                pool_scale=pool_scale, conv_w=conv_w)


class _NoComm:
    def layer_weights(self, l):
        raise NotImplementedError

    def fwd_hook(self, l):
        return _no_hook

    def bwd_hook(self, l):
        return _no_hook

    def grads_ready(self, l, big):
        return None


class _Layers(_NoComm):
    def __init__(self, layers):
        self.layers = layers

    def layer_weights(self, l):
        return self.layers[l]


def _local_step(x, target, mods, comm):
    saved, weights = [], []
    act = x
    for l in range(DEPTH):
        weights.append(comm.layer_weights(l))
        act, sv = _layer_fwd(act, weights[l], mods[l], comm.fwd_hook(l))
        saved.append(sv)
    dact, sq = _loss_head(act, target)
    loss = sq[0, 0] * (0.5 / D)
    dmods, bigs, smalls = [None] * DEPTH, [None] * DEPTH, [None] * DEPTH
    token = None
    for l in reversed(range(DEPTH)):
        dact, dmods[l], bigs[l], smalls[l] = _layer_bwd(dact, saved[l], weights[l], _tie(mods[l], token), comm.bwd_hook(l))
        token = comm.grads_ready(l, bigs[l])
    return loss, dact, jnp.stack(dmods), bigs, smalls


_BIG = ("w_in", "w_branch", "w_out", "w_ff1", "w_ff2")
N_BIG = len(_BIG)


class _GatherJob:
    def __init__(self, tag, shards, after):
        self.tag, self.n = tag, len(shards)
        lands = [lax.empty((N_CHIPS,) + s.shape, s.dtype) for s in shards]
        self.state = _copies_start(list(shards) + lands, _plan_gather_ici, 4 * self.n, after, "gather_ici_start_" + tag)
        self.token = self.state[3]

    def pass_on(self, after):
        bufs = _copies_wait(self.state, _plan_gather_ici, after, "gather_ici_wait_" + self.tag)
        self.state = _copies_start(bufs[self.n:], _plan_gather_d2d, 3 * self.n, bufs[0], "gather_d2d_start_" + self.tag)
        self.token = self.state[3]
        return self.token

    def done(self, after):
        return _copies_wait(self.state, _plan_gather_d2d, after, "gather_d2d_wait_" + self.tag)


class _ReduceJob:
    def __init__(self, tag, names, grads, sel, after):
        self.tag, self.names, self.n, self.sel = tag, names, len(names), sel
        lands = [lax.empty((N_CHIPS, g.shape[1] // 2, g.shape[2]), F32) for g in grads]
        self.state = _copies_start(list(grads) + lands, _plan_rs_sibling, self.n, after, "rs_sibling_start_" + tag)
        self.token = self.state[3]

    def chip_sums(self, after):
        bufs = _copies_wait(self.state, _plan_rs_sibling, after, "rs_sibling_wait_" + self.tag)
        wires, self.owns = zip(*[_chip_sum(bufs[i], bufs[self.n + i], self.sel, "rs_chip_sum_" + name) for i, name in enumerate(self.names)])
        lands = [lax.empty((3,) + w.shape[1:], BF16) for w in wires]
        self.state = _copies_start(list(wires) + lands, _plan_rs_chips, 3 * self.n, self.owns[0], "rs_chips_start_" + self.tag)
        self.token = self.state[3]
        return self.token

    def final_sums(self, after):
        bufs = _copies_wait(self.state, _plan_rs_chips, after, "rs_chips_wait_" + self.tag)
        sums = [_final_sum(self.owns[i], bufs[self.n + i], "rs_final_" + name) for i, name in enumerate(self.names)]
        self.state = _copies_start(sums + [lax.empty(s.shape, F32) for s in sums], _plan_rs_share, self.n, sums[0], "rs_share_start_" + self.tag)
        self.token = self.state[3]
        return self.token

    def done(self, after):
        bufs = _copies_wait(self.state, _plan_rs_share, after, "rs_share_wait_" + self.tag)
        low = self.sel[0] == 0
        return {name: jnp.where(low, jnp.concatenate([mine, theirs]), jnp.concatenate([theirs, mine]))
                for name, mine, theirs in zip(self.names, bufs[:self.n], bufs[self.n:])}


def _chip_blocks(g):
    return g if g.ndim == 3 else g.reshape(N_CHIPS, -1, g.shape[1])


class _StepComm(_NoComm):
    def __init__(self, shards, sel, after):
        self.sel = sel
        self.small, self.grads, self.jobs = None, [dict() for _ in range(DEPTH)], {}
        self.jobs["in0"] = _GatherJob("in0", shards[0][:1], after)
        self.jobs["rest0"] = _GatherJob("rest0", shards[0][1:], self.jobs["in0"].token)
        self.jobs["all1"] = _GatherJob("all1", shards[1], self.jobs["rest0"].token)

    def layer_weights(self, l):
        if l == 0:
            job = self.jobs["in0"]
            (g_in,) = job.done(job.pass_on(self.jobs["all1"].token))
            self.weights0 = _full_layer_weights(g_in, None, None, None, None, *self.small[0])
            return self.weights0
        g_in, g_br, g_out, g_f1, g_f2 = self.landed1
        return _full_layer_weights(g_in, g_br.reshape(D, D), g_out.reshape(D, D), g_f1, g_f2.reshape(D_FF, D), *self.small[1])

    def fwd_hook(self, l):
        if l != 0:
            return _no_hook

        def hook(point, after, ready=None):
            if point == "attn":
                return self.jobs["rest0"].pass_on(after)
            if point == "ff_post":
                return self.jobs["all1"].pass_on(after)
            if point == "pool":
                g_br, g_out, g_f1, g_f2 = self.jobs["rest0"].done(after)
                self.weights0.update(w_branch=g_br.reshape(D, D), w_out=g_out.reshape(D, D), w_ff1=g_f1, w_ff2=g_f2.reshape(D_FF, D))
            if point == "end":
                self.landed1 = self.jobs["all1"].done(after)
            return None
        return hook

    def bwd_hook(self, l):
        if l != 0:
            return _no_hook

        def hook(point, after, ready=None):
            jobs = self.jobs
            if point == "ff_pre":
                token = jobs["rs1"].chip_sums(after)
                jobs["rs0_ff"] = _ReduceJob("0_ff", ("w_ff1", "w_ff2"), [_chip_blocks(ready[n]) for n in ("w_ff1", "w_ff2")], self.sel, token)
                return jobs["rs0_ff"].token
            if point == "cumf":
                return jobs["rs0_ff"].chip_sums(jobs["rs1"].final_sums(after))
            self.grads[1] = jobs["rs1"].done(after)
            return None
        return hook

    def grads_ready(self, l, big):
        if l == 1:
            self.jobs["rs1"] = _ReduceJob("1", _BIG, [_chip_blocks(big[n]) for n in _BIG], self.sel, self.sel)
            return self.jobs["rs1"].token
        names = ("w_in", "w_branch", "w_out")
        self.jobs["rs0_mix"] = _ReduceJob("0_mix", names, [_chip_blocks(big[n]) for n in names], self.sel, self.sel)
        return self.jobs["rs0_mix"].token

    def finish_sums(self, after):
        jobs = self.jobs
        token = jobs["rs0_mix"].chip_sums(after)
        return jobs["rs0_ff"].final_sums(token)

    def finish(self, after):
        jobs = self.jobs
        self.grads[0] = jobs["rs0_ff"].done(after)
        self.grads[0].update(jobs["rs0_mix"].done(jobs["rs0_mix"].final_sums(after)))


_SMALL = ("g_mix_pre", "g_mix_post", "g_ff_pre", "g_ff_post", "b_f", "w_pool", "pool_scale", "conv_w")


def _w_in_view(t):
    return t.reshape(DEPTH, D // 128, 128, _SHARD_COLS).transpose(3, 1, 0, 2).reshape(_SHARD_COLS * (D // 128) * DEPTH, 128)


def _w_in_unview(t):
    return t.reshape(_SHARD_COLS, D // 128, DEPTH, 128).transpose(2, 1, 3, 0).reshape(DEPTH, D, _SHARD_COLS)


def _pack(parts, rows=8):
    flat = jnp.concatenate([p.reshape(-1) for p in parts])
    width = -(-flat.shape[0] // (rows * 128)) * 128
    return jnp.pad(flat, (0, rows * width - flat.shape[0])).reshape(rows, width)


def _unpack(packed, like):
    flat = packed.reshape(-1)
    out, at = [], 0
    for ref in like:
        out.append(flat[at:at + ref.size].reshape(ref.shape))
        at += ref.size
    return out


def kernel(x, c, w_ada, b_ada, g_mix_pre, g_mix_post, g_ff_pre, g_ff_post, w_in, b_f, w_pool, pool_scale, conv_w, w_branch, w_out, w_ff1, w_ff2, loss_target, m_w_ada, m_b_ada, m_g_mix_pre, m_g_mix_post, m_g_ff_pre, m_g_ff_post, m_w_in, m_b_f, m_w_pool, m_pool_scale, m_conv_w, m_w_branch, m_w_out, m_w_ff1, m_w_ff2, v_w_ada, v_b_ada, v_g_mix_pre, v_g_mix_post, v_g_ff_pre, v_g_ff_post, v_w_in, v_b_f, v_w_pool, v_pool_scale, v_conv_w, v_w_branch, v_w_out, v_w_ff1, v_w_ff2):
    xi, yi, ci = lax.axis_index("x"), lax.axis_index("y"), lax.axis_index("c")
    chip = 2 * xi + yi
    dev = 2 * chip + ci
    n_ada = w_ada.shape[2]

    first = jnp.zeros((8, D + 384), F32).at[0, :D].set(c[0]).at[0, D:].set(conv_w.reshape(-1))
    got = _allgather8(first, "gather_cond").reshape(N_DEV, 8, D + 384)[:, 0]
    c_all = got[:, :D]
    conv_full = got[0::2, D:].reshape(N_CHIPS, DEPTH, 3, CONV_WIDTH // N_CHIPS).transpose(1, 2, 0, 3).reshape(DEPTH, 3, CONV_WIDTH)

    b_loc = lax.dynamic_slice_in_dim(b_ada, chip * n_ada, n_ada, axis=1).reshape(DEPTH, 1, n_ada)
    mod_cols, silu_c = _ada_fwd(c_all, w_ada, b_loc)
    got = _allgather8(mod_cols.reshape(DEPTH * N_DEV, n_ada), "gather_mod").reshape(N_DEV, DEPTH, N_DEV, n_ada)[0::2]
    mod_all = got.transpose(1, 2, 0, 3).reshape(DEPTH, N_DEV, 6, D)
    mods = lax.dynamic_index_in_dim(mod_all, dev, axis=1, keepdims=False)

    comm = _StepComm([[w[l].astype(BF16) for w in (w_in, w_branch, w_out, w_ff1, w_ff2)] for l in range(DEPTH)],
                     jnp.stack([ci, chip]).astype(jnp.int32), mods)
    comm.small = [(g_mix_pre[l], g_mix_post[l], g_ff_pre[l], g_ff_post[l], b_f[l], w_pool[l], pool_scale[l], conv_full[l]) for l in range(DEPTH)]
    loss_part, grad_x, dmods, bigs, smalls = _local_step(x[0], loss_target[0], mods, comm)
    loss = lax.psum(loss_part, ("x", "y", "c"))

    small_parts = [smalls[l][name] for name in _SMALL for l in range(DEPTH)]
    packed = _tie(_pack([dmods] + small_parts), comm.jobs["rs0_mix"].token)
    gathered = _allgather8(packed, "gather_small")
    dmod_all = gathered.reshape(N_DEV, -1)[:, :dmods.size].reshape(N_DEV, DEPTH, 6 * D)
    summed = _unpack(_sum_devices(gathered), [dmods] + small_parts)
    grad_b_ada = summed[0].reshape(DEPTH, 6 * D)
    small_grads = {name: jnp.stack(summed[1 + 2 * i:3 + 2 * i]) for i, name in enumerate(_SMALL)}
    small_grads["conv_w"] = lax.dynamic_slice_in_dim(small_grads["conv_w"], chip * (CONV_WIDTH // N_CHIPS), CONV_WIDTH // N_CHIPS, axis=2)

    dmod_loc = lax.dynamic_slice_in_dim(dmod_all.transpose(1, 0, 2), chip * n_ada, n_ada, axis=2)
    tail_token = comm.finish_sums(grad_b_ada)
    silu_pad = _tie(jnp.pad(silu_c, ((0, 128 - N_DEV), (0, 0))), tail_token)
    grad_w_ada = jnp.stack([_mm(silu_pad, jnp.pad(dmod_loc[l], ((0, 128 - N_DEV), (0, 0))), ta=True, name="mm_ada_dw") for l in range(DEPTH)])

    grads = dict(w_ada=grad_w_ada, b_ada=grad_b_ada, **small_grads)
    weights = dict(w_ada=w_ada, b_ada=b_ada, g_mix_pre=g_mix_pre, g_mix_post=g_mix_post, g_ff_pre=g_ff_pre, g_ff_post=g_ff_post, w_in=w_in,
                   b_f=b_f, w_pool=w_pool, pool_scale=pool_scale, conv_w=conv_w, w_branch=w_branch, w_out=w_out, w_ff1=w_ff1, w_ff2=w_ff2)
    m_in = dict(w_ada=m_w_ada, b_ada=m_b_ada, g_mix_pre=m_g_mix_pre, g_mix_post=m_g_mix_post, g_ff_pre=m_g_ff_pre, g_ff_post=m_g_ff_post,
                w_in=m_w_in, b_f=m_b_f, w_pool=m_w_pool, pool_scale=m_pool_scale, conv_w=m_conv_w, w_branch=m_w_branch, w_out=m_w_out,
                w_ff1=m_w_ff1, w_ff2=m_w_ff2)
    v_in = dict(w_ada=v_w_ada, b_ada=v_b_ada, g_mix_pre=v_g_mix_pre, g_mix_post=v_g_mix_post, g_ff_pre=v_g_ff_pre, g_ff_post=v_g_ff_post,
                w_in=v_w_in, b_f=v_b_f, w_pool=v_w_pool, pool_scale=v_pool_scale, conv_w=v_conv_w, w_branch=v_w_branch, w_out=v_w_out,
                w_ff1=v_w_ff1, w_ff2=v_w_ff2)
    order = ("w_ada", "b_ada", "g_mix_pre", "g_mix_post", "g_ff_pre", "g_ff_post", "w_in", "b_f", "w_pool", "pool_scale", "conv_w",
             "w_branch", "w_out", "w_ff1", "w_ff2")
    delta, new_m, new_v = {}, {}, {}
    tiny = ("b_ada",) + _SMALL
    packed_w, packed_g, packed_m, packed_v = [_pack([src[name] for name in tiny]) for src in (weights, grads, m_in, v_in)]
    res = _adamw(packed_w, _tie(packed_g, tail_token), packed_m, packed_v, "adamw_small")
    for out, packed_res in zip((delta, new_m, new_v), res):
        for name, val in zip(tiny, _unpack(packed_res, [weights[name] for name in tiny])):
            out[name] = val
    delta["w_ada"], new_m["w_ada"], new_v["w_ada"] = _adamw(w_ada, grad_w_ada, m_w_ada, v_w_ada, "adamw_w_ada")
    comm.finish(delta["w_ada"][0, :8, :128] + delta["b_ada"][0, :128])
    for name in _BIG:
        grads[name] = jnp.stack([comm.grads[l][name] for l in range(DEPTH)])
        if name == "w_in":
            g_view = lax.optimization_barrier(_w_in_view(grads[name]))
            res = _adamw(_w_in_view(w_in), g_view, _w_in_view(m_w_in), _w_in_view(v_w_in), "adamw_w_in")
            grads[name], delta[name], new_m[name], new_v[name] = [_w_in_unview(t) for t in (g_view, *res)]
        else:
            delta[name], new_m[name], new_v[name] = _adamw(weights[name], grads[name], m_in[name], v_in[name], "adamw_" + name)

    return (loss, grad_x[None], *[grads[n] for n in order], *[delta[n] for n in order], *[new_m[n] for n in order],
            *[new_v[n] for n in order])
```

```python
import functools

import jax
import jax.numpy as jnp
from jax import lax
from jax.experimental import pallas as pl
from jax.experimental.pallas import tpu as pltpu

F32 = jnp.float32
BF16 = jnp.bfloat16
MESH = pl.DeviceIdType.MESH

D = 1024
DEPTH = 2
HEADS = 8
HEAD_DIM = 64
A_WIDTH = 512
POOL_WIDTH = 256
CONV_WIDTH = 256
D_FF = 4096
IN_COLS = 5640
Z_GL, Z_QKV, Z_PC, Z_FL, Z_COLS = 0, 3072, 4608, 5632, 5760
RMS_EPS = 1e-6
NEG_INF = -1e30
ROW_TILE = 256
N_CHIPS = 4
N_DEV = 8
V7X_VMEM_LIMIT = 48 * 1024 * 1024

ADAM_LR = 0.001
ADAM_B1 = 0.9
ADAM_B2 = 0.999
ADAM_EPS = 1e-08
ADAM_WD = 0.01
ADAM_STEP = 10

_HBM = pl.BlockSpec(memory_space=pltpu.HBM)


def _params(*sem):
    return pltpu.CompilerParams(dimension_semantics=sem, vmem_limit_bytes=V7X_VMEM_LIMIT)


def _pick(dim, cands):
    for cand in cands:
        if dim % cand == 0:
            return cand
    return dim


def _mm(a, b, *, ta=False, tb=False, b_split=1, out_split=1, out_dtype=F32, epilogue=None, extras=(), name):
    (k, m) = a.shape if ta else a.shape[::-1]
    b_rows, b_cols = b.shape[-2], b.shape[-1] * b_split
    (n, k2) = (b_rows, b_cols) if tb else (b_cols, b_rows)
    assert k == k2, (a.shape, b.shape, ta, tb)
    n_unit = n // (out_split * (1 if tb else b_split))
    k_unit = k // (b_split if tb else 1)
    tm = _pick(m, (1024, 512, 256, 128))
    tn = _pick(n_unit, (1024, 1152, 768, 640, 512, 256, 128))
    tk = _pick(k_unit, (1024, 1152, 512, 640, 256, 128))
    nk = k // tk
    dims = (((0 if ta else 1,), (1 if tb else 0,)), ((), ()))

    def dot(a_ref, b_ref):
        b_val = b_ref[0] if b_split > 1 else b_ref[...]
        return lax.dot_general(a_ref[...].astype(BF16), b_val.astype(BF16), dims, preferred_element_type=F32)

    n_extra = len(extras)
    assert epilogue is None or out_split == 1

    def put(refs, val):
        if epilogue is not None:
            for o_ref, res in zip(refs[n_extra:], epilogue(val, *[r[...] for r in refs[:n_extra]])):
                o_ref[...] = res.astype(o_ref.dtype)
        elif out_split > 1:
            refs[0][0] = val.astype(refs[0].dtype)
        else:
            refs[0][...] = val.astype(refs[0].dtype)

    def body_single(a_ref, b_ref, *refs):
        put(refs, dot(a_ref, b_ref))

    def body_acc(a_ref, b_ref, *refs):
        kk = pl.program_id(2)
        acc_ref = refs[-1]

        @pl.when(kk == 0)
        def _():
            acc_ref[...] = jnp.zeros_like(acc_ref)

        acc_ref[...] += dot(a_ref, b_ref)

        @pl.when(kk == nk - 1)
        def _():
            put(refs[:-1], acc_ref[...])

    a_spec = pl.BlockSpec((tk, tm), lambda i, j, kk: (kk, i)) if ta else pl.BlockSpec((tm, tk), lambda i, j, kk: (i, kk))
    if b_split == 1:
        b_spec = pl.BlockSpec((tn, tk), lambda i, j, kk: (j, kk)) if tb else pl.BlockSpec((tk, tn), lambda i, j, kk: (kk, j))
    elif tb:
        per = k_unit // tk
        b_spec = pl.BlockSpec((1, tn, tk), lambda i, j, kk: (kk // per, j, kk % per))
    else:
        per = n // b_split // tn
        b_spec = pl.BlockSpec((1, tk, tn), lambda i, j, kk: (j // per, kk, j % per))
    if out_split == 1:
        o_spec = pl.BlockSpec((tm, tn), lambda i, j, kk: (i, j))
        o_shape = None if epilogue is not None else jax.ShapeDtypeStruct((m, n), out_dtype)
    else:
        per_o = n // out_split // tn
        o_spec = pl.BlockSpec((1, tm, tn), lambda i, j, kk: (j // per_o, i, j % per_o))
        o_shape = jax.ShapeDtypeStruct((out_split, m, n // out_split), out_dtype)
    if epilogue is not None:
        o_shape = [jax.ShapeDtypeStruct((m, n), dt) for dt in out_dtype]
        o_spec = [o_spec] * len(out_dtype)
    return pl.pallas_call(
        body_single if nk == 1 else body_acc, name=name, grid=(m // tm, n // tn, nk),
        in_specs=[a_spec, b_spec] + [pl.BlockSpec((tm, tn), lambda i, j, kk: (i, j))] * n_extra, out_specs=o_spec, out_shape=o_shape,
        scratch_shapes=[] if nk == 1 else [pltpu.VMEM((tm, tn), F32)],
        compiler_params=_params("parallel", "parallel", "arbitrary"),
    )(a, b, *extras)


def _ew(fn, ins, out_dtypes, name, tc=None):
    shape = ins[0].shape
    lead, (rows, cols) = shape[:-2], shape[-2:]
    tc = cols if tc is None else tc
    tr = _pick(rows, (ROW_TILE, 128, 8)) if tc > 128 else _pick(rows, (4096, 2256, 2048, 1024, ROW_TILE, 8))
    n_in = len(ins)

    def body(*refs):
        res = fn(*[r[...] for r in refs[:n_in]])
        for o_ref, val in zip(refs[n_in:], res):
            o_ref[...] = val.astype(o_ref.dtype)

    if lead:
        spec = pl.BlockSpec((None, tr, tc), lambda l, i, j: (l, i, j))
    else:
        spec = pl.BlockSpec((tr, tc), lambda i, j: (i, j))
    return pl.pallas_call(
        body, name=name, grid=lead + (rows // tr, cols // tc),
        in_specs=[spec] * n_in, out_specs=[spec] * len(out_dtypes),
        out_shape=[jax.ShapeDtypeStruct(shape, dt) for dt in out_dtypes],
        compiler_params=_params(*(["parallel"] * (len(lead) + 2))),
    )(*ins)


def _relu2_fwd(a):
    r = jnp.maximum(a, 0.0)
    return a, r * r


def _relu2_bwd(dr, a):
    return (dr * (2.0 * jnp.maximum(a, 0.0)),)


def _adamw(w, g, m, v, name):
    bc1 = 1.0 - ADAM_B1 ** ADAM_STEP
    bc2 = 1.0 - ADAM_B2 ** ADAM_STEP

    def fn(w, g, m, v):
        m = ADAM_B1 * m + (1.0 - ADAM_B1) * g
        v = ADAM_B2 * v + (1.0 - ADAM_B2) * (g * g)
        m_hat = m / bc1
        v_hat = v / bc2
        delta = -ADAM_LR * (m_hat / (jnp.sqrt(v_hat) + ADAM_EPS) + ADAM_WD * w)
        return delta, m, v
    return _ew(fn, [w, g, m, v], [F32, F32, F32], name)


def _row_spec(cols, block=0):
    return pl.BlockSpec((ROW_TILE, cols), lambda i, block=block: (i, block))


def _vec_spec(cols):
    return pl.BlockSpec((1, cols), lambda i: (0, 0))


def _sum_spec(cols):
    return pl.BlockSpec((8, cols), lambda i: (0, 0))


def _rstd(x):
    return lax.rsqrt(jnp.mean(x * x, axis=-1, keepdims=True) + RMS_EPS)


def _modnorm_fwd(x, g, shift, scale, name):
    s = x.shape[0]

    def body(x_ref, g_ref, sh_ref, sc_ref, h_ref):
        xv = x_ref[...]
        n = xv * _rstd(xv)
        h_ref[...] = ((n * g_ref[...]) * (1.0 + sc_ref[...]) + sh_ref[...]).astype(BF16)

    return pl.pallas_call(
        body, name=name, grid=(s // ROW_TILE,),
        in_specs=[_row_spec(D), _vec_spec(D), _vec_spec(D), _vec_spec(D)], out_specs=_row_spec(D),
        out_shape=jax.ShapeDtypeStruct((s, D), BF16), compiler_params=_params("parallel"),
    )(x, g, shift, scale)


def _post_fwd(x, y, g, gate, name):
    s = x.shape[0]

    def body(x_ref, y_ref, g_ref, gate_ref, o_ref):
        yv = y_ref[...]
        o_ref[...] = x_ref[...] + gate_ref[...] * ((yv * _rstd(yv)) * g_ref[...])

    return pl.pallas_call(
        body, name=name, grid=(s // ROW_TILE,),
        in_specs=[_row_spec(D), _row_spec(D), _vec_spec(D), _vec_spec(D)], out_specs=_row_spec(D),
        out_shape=jax.ShapeDtypeStruct((s, D), F32), compiler_params=_params("parallel"),
    )(x, y, g, gate)


def _post_bwd(dxo, y, g, gate, name):
    s = dxo.shape[0]

    def body(d_ref, y_ref, g_ref, gate_ref, dy_ref, sum_ref):
        @pl.when(pl.program_id(0) == 0)
        def _():
            sum_ref[...] = jnp.zeros_like(sum_ref)

        dv, yv = d_ref[...], y_ref[...]
        r = _rstd(yv)
        n = yv * r
        sum_ref[0:1, :] += jnp.sum(dv * (n * g_ref[...]), axis=0, keepdims=True)
        sum_ref[1:2, :] += jnp.sum((dv * gate_ref[...]) * n, axis=0, keepdims=True)
        dn = (dv * gate_ref[...]) * g_ref[...]
        dy_ref[...] = (r * (dn - n * jnp.mean(dn * n, axis=-1, keepdims=True))).astype(BF16)

    return pl.pallas_call(
        body, name=name, grid=(s // ROW_TILE,),
        in_specs=[_row_spec(D), _row_spec(D), _vec_spec(D), _vec_spec(D)],
        out_specs=[_row_spec(D), _sum_spec(D)],
        out_shape=[jax.ShapeDtypeStruct((s, D), BF16), jax.ShapeDtypeStruct((8, D), F32)],
        compiler_params=_params("arbitrary"),
    )(dxo, y, g, gate)


def _modnorm_bwd(dh, x, dxo, g, scale, name):
    s = dh.shape[0]

    def body(dh_ref, x_ref, d_ref, g_ref, sc_ref, dx_ref, sum_ref):
        @pl.when(pl.program_id(0) == 0)
        def _():
            sum_ref[...] = jnp.zeros_like(sum_ref)

        dhv, xv = dh_ref[...], x_ref[...]
        r = _rstd(xv)
        n = xv * r
        one_sc = 1.0 + sc_ref[...]
        sum_ref[0:1, :] += jnp.sum(dhv, axis=0, keepdims=True)
        sum_ref[1:2, :] += jnp.sum(dhv * (n * g_ref[...]), axis=0, keepdims=True)
        sum_ref[2:3, :] += jnp.sum((dhv * one_sc) * n, axis=0, keepdims=True)
        dn = (dhv * one_sc) * g_ref[...]
        dx_ref[...] = d_ref[...] + r * (dn - n * jnp.mean(dn * n, axis=-1, keepdims=True))

    return pl.pallas_call(
        body, name=name, grid=(s // ROW_TILE,),
        in_specs=[_row_spec(D), _row_spec(D), _row_spec(D), _vec_spec(D), _vec_spec(D)],
        out_specs=[_row_spec(D), _sum_spec(D)],
        out_shape=[jax.ShapeDtypeStruct((s, D), F32), jax.ShapeDtypeStruct((8, D), F32)],
        compiler_params=_params("arbitrary"),
    )(dh, x, dxo, g, scale)


def _loss_head(y, target):
    s = y.shape[0]

    def body(y_ref, t_ref, dy_ref, sum_ref):
        @pl.when(pl.program_id(0) == 0)
        def _():
            sum_ref[...] = jnp.zeros_like(sum_ref)

        err = y_ref[...] - t_ref[...]
        dy_ref[...] = err * (1.0 / D)
        sum_ref[...] += jnp.sum(err * err)

    return pl.pallas_call(
        body, name="loss_head", grid=(s // ROW_TILE,),
        in_specs=[_row_spec(D), _row_spec(D)],
        out_specs=[_row_spec(D), pl.BlockSpec((8, 128), lambda i: (0, 0))],
        out_shape=[jax.ShapeDtypeStruct((s, D), F32), jax.ShapeDtypeStruct((8, 128), F32)],
        compiler_params=_params("arbitrary"),
    )(y, target)


def _merge_fwd(z, pa, pb, pc):
    s = z.shape[0]

    def body(g0_ref, g1_ref, g2_ref, pa_ref, pb_ref, pc_ref, o_ref):
        o_ref[...] = (jax.nn.sigmoid(g0_ref[...]) * pa_ref[...] + jax.nn.sigmoid(g1_ref[...]) * pb_ref[...]
                      + jax.nn.sigmoid(g2_ref[...]) * pc_ref[...]).astype(BF16)

    return pl.pallas_call(
        body, name="merge_fwd", grid=(s // ROW_TILE,),
        in_specs=[_row_spec(D, 0), _row_spec(D, 1), _row_spec(D, 2), _row_spec(D), _row_spec(D), _row_spec(D)],
        out_specs=_row_spec(D), out_shape=jax.ShapeDtypeStruct((s, D), BF16),
        compiler_params=_params("parallel"),
    )(z, z, z, pa, pb, pc)


def _merge_bwd(dm, z, pa, pb, pc):
    s = z.shape[0]

    def body(dm_ref, g0_ref, g1_ref, g2_ref, pa_ref, pb_ref, pc_ref, dgl_ref, da_ref, db_ref, dc_ref):
        dmv = dm_ref[...]
        for i, (g_ref, p_ref, d_ref) in enumerate(((g0_ref, pa_ref, da_ref), (g1_ref, pb_ref, db_ref), (g2_ref, pc_ref, dc_ref))):
            gate = jax.nn.sigmoid(g_ref[...])
            dgl_ref[:, i * D:(i + 1) * D] = ((dmv * p_ref[...]) * (gate * (1.0 - gate))).astype(BF16)
            d_ref[...] = (dmv * gate).astype(BF16)

    return pl.pallas_call(
        body, name="merge_bwd", grid=(s // ROW_TILE,),
        in_specs=[_row_spec(D), _row_spec(D, 0), _row_spec(D, 1), _row_spec(D, 2), _row_spec(D), _row_spec(D), _row_spec(D)],
        out_specs=[_row_spec(3 * D), _row_spec(D), _row_spec(D), _row_spec(D)],
        out_shape=[jax.ShapeDtypeStruct((s, Z_COLS), BF16)] + [jax.ShapeDtypeStruct((s, D), BF16)] * 3,
        compiler_params=_params("parallel"),
    )(dm, z, z, z, pa, pb, pc)


def _shift_down(v, n):
    row = lax.broadcasted_iota(jnp.int32, v.shape, 0)
    return jnp.where(row >= n, pltpu.roll(v, n, axis=0), 0.0)


def _shift_up(v, n):
    s = v.shape[0]
    row = lax.broadcasted_iota(jnp.int32, v.shape, 0)
    return jnp.where(row < s - n, pltpu.roll(v, s - n, axis=0), 0.0)


def _log_sigmoid(v):
    return jnp.minimum(v, 0.0) - jnp.log1p(jnp.exp(-jnp.abs(v)))


def _cumf_fwd(fl, bias):
    s = fl.shape[0]

    def body(fl_ref, b_ref, o_ref):
        acc = _log_sigmoid(fl_ref[...] + b_ref[...])
        step = 1
        while step < s:
            acc = acc + _shift_down(acc, step)
            step *= 2
        o_ref[...] = acc

    return pl.pallas_call(body, name="cumf_fwd", out_shape=jax.ShapeDtypeStruct((s, 128), F32),
                          compiler_params=pltpu.CompilerParams(vmem_limit_bytes=V7X_VMEM_LIMIT))(fl, bias)


def _cumf_bwd(dcum, fl, bias):
    s = fl.shape[0]

    def body(d_ref, fl_ref, b_ref, dfl_ref, db_ref):
        acc = d_ref[...]
        step = 1
        while step < s:
            acc = acc + _shift_up(acc, step)
            step *= 2
        dfl = acc * jax.nn.sigmoid(-(fl_ref[...] + b_ref[...]))
        dfl_ref[...] = dfl.astype(BF16)
        db_ref[...] = jnp.broadcast_to(jnp.sum(dfl, axis=0, keepdims=True), (8, 128))

    return pl.pallas_call(
        body, name="cumf_bwd",
        out_shape=[jax.ShapeDtypeStruct((s, 128), BF16), jax.ShapeDtypeStruct((8, 128), F32)],
        compiler_params=pltpu.CompilerParams(vmem_limit_bytes=V7X_VMEM_LIMIT))(dcum, fl, bias)


def _pool_windows(v, shift):
    s2 = v + shift(v, 1)
    s4 = s2 + shift(s2, 2)
    s8 = s4 + shift(s4, 4)
    s16 = s8 + shift(s8, 8)
    group = lax.broadcasted_iota(jnp.int32, v.shape, 1) // 64
    return jnp.where(group == 0, s2, jnp.where(group == 1, s4, jnp.where(group == 2, s8, s16)))


def _pool_count(shape):
    group = lax.broadcasted_iota(jnp.int32, shape, 1) // 64
    window = jnp.where(group == 0, 2.0, jnp.where(group == 1, 4.0, jnp.where(group == 2, 8.0, 16.0)))
    t1 = (lax.broadcasted_iota(jnp.int32, shape, 0) + 1).astype(F32)
    return jnp.minimum(t1, window)


def _pc_specs(s):
    zcol = lambda blk: pl.BlockSpec((s, 256), lambda i, blk=blk: (0, blk))
    first = Z_PC // 256
    return [zcol(first), zcol(first + 1), zcol(first + 2), zcol(first + 3),
            pl.BlockSpec((256, 256), lambda i: (0, 0)), pl.BlockSpec((1, 256), lambda i: (0, 0)),
            pl.BlockSpec((3, 256), lambda i: (0, 0))]


def _poolconv_fwd(z, wbd, pscale, convw):
    s = z.shape[0]

    def body(pu_ref, ch_ref, cb_ref, cc_ref, w_ref, ps_ref, cw_ref, yb_ref, yc_ref):
        u = pu_ref[...]
        p = _pool_windows(u, _shift_down) / _pool_count(u.shape) - u
        yb = jnp.dot(p.astype(BF16), w_ref[...].astype(BF16), preferred_element_type=F32) * ps_ref[...]
        yb_ref[...] = yb.astype(BF16)
        uc = cc_ref[...] * ch_ref[...]
        cw = cw_ref[...]
        conv = cw[0:1, :] * _shift_down(uc, 2) + cw[1:2, :] * _shift_down(uc, 1) + cw[2:3, :] * uc
        yc_ref[...] = (cb_ref[...] * conv).astype(BF16)

    out = pl.BlockSpec((s, 256), lambda i: (0, 0))
    return pl.pallas_call(
        body, name="poolconv_fwd", grid=(1,), in_specs=_pc_specs(s), out_specs=[out, out],
        out_shape=[jax.ShapeDtypeStruct((s, 256), BF16)] * 2, compiler_params=_params("arbitrary"),
    )(z, z, z, z, wbd, pscale, convw)


def _poolconv_bwd(dyb, dyc, z, wbd, pscale, convw):
    s = z.shape[0]

    def body(dyb_ref, dyc_ref, pu_ref, ch_ref, cb_ref, cc_ref, w_ref, ps_ref, cw_ref, dz_ref, dw_ref, dps_ref, dcw_ref):
        u = pu_ref[...]
        count = _pool_count(u.shape)
        p = (_pool_windows(u, _shift_down) / count - u).astype(BF16)
        wb = w_ref[...].astype(BF16)
        dyb_v = dyb_ref[...]
        pw = jnp.dot(p, wb, preferred_element_type=F32)
        dps_ref[...] = jnp.broadcast_to(jnp.sum(dyb_v * pw, axis=0, keepdims=True), (8, 256))
        dys = (dyb_v * ps_ref[...]).astype(BF16)
        dp = lax.dot_general(dys, wb, (((1,), (1,)), ((), ())), preferred_element_type=F32)
        dw_ref[...] = lax.dot_general(p, dys, (((0,), (0,)), ((), ())), preferred_element_type=F32)
        dz_ref[:, 0:256] = (_pool_windows(dp / count, _shift_up) - dp).astype(BF16)

        ch, cb, cc = ch_ref[...], cb_ref[...], cc_ref[...]
        uc = cc * ch
        cw = cw_ref[...]
        u2, u1 = _shift_down(uc, 2), _shift_down(uc, 1)
        conv = cw[0:1, :] * u2 + cw[1:2, :] * u1 + cw[2:3, :] * uc
        dyc_v = dyc_ref[...]
        dconv = dyc_v * cb
        du = cw[0:1, :] * _shift_up(dconv, 2) + cw[1:2, :] * _shift_up(dconv, 1) + cw[2:3, :] * dconv
        dz_ref[:, 256:512] = (du * cc).astype(BF16)
        dz_ref[:, 512:768] = (dyc_v * conv).astype(BF16)
        dz_ref[:, 768:1024] = (du * ch).astype(BF16)
        dcw_ref[...] = jnp.zeros_like(dcw_ref)
        dcw_ref[0:1, :] = jnp.sum(dconv * u2, axis=0, keepdims=True)
        dcw_ref[1:2, :] = jnp.sum(dconv * u1, axis=0, keepdims=True)
        dcw_ref[2:3, :] = jnp.sum(dconv * uc, axis=0, keepdims=True)

    blk = lambda r, c: pl.BlockSpec((r, c), lambda i: (0, 0))
    return pl.pallas_call(
        body, name="poolconv_bwd", grid=(1,),
        in_specs=[blk(s, 256), blk(s, 256)] + _pc_specs(s),
        out_specs=[blk(s, 1024), blk(256, 256), blk(8, 256), blk(8, 256)],
        out_shape=[jax.ShapeDtypeStruct((s, 1024), BF16), jax.ShapeDtypeStruct((256, 256), F32),
                   jax.ShapeDtypeStruct((8, 256), F32), jax.ShapeDtypeStruct((8, 256), F32)],
        compiler_params=_params("arbitrary"),
    )(dyb, dyc, z, z, z, z, wbd, pscale, convw)


_NT = (((1,), (1,)), ((), ()))
_TN = (((0,), (0,)), ((), ()))


ATT_Q, ATT_K = 256, 256
ATT_HEADS_BWD = 4
ATT_HEADS = 8


def _att_logits(q, k, fr, q0, k0, masked):
    logits = lax.dot_general(q, k, _NT, preferred_element_type=F32) - fr
    if not masked:
        return logits
    row = q0 + lax.broadcasted_iota(jnp.int32, logits.shape, 0)
    col = k0 + lax.broadcasted_iota(jnp.int32, logits.shape, 1)
    return jnp.where(row >= col, logits, NEG_INF)


def _causal_sweep(step, qi, init):
    n_full = (qi * ATT_Q) // ATT_K
    carry = lax.fori_loop(0, n_full, lambda j, carry: step(j, carry, False), init)
    return step(n_full, carry, True)


HEAD_PAIRS = HEADS // 2


def _lane_pick(v, lane, idx):
    return jnp.sum(jnp.where(lane == idx, v, 0.0), axis=-1, keepdims=True)


def _lane_put(lane, idx, col):
    return jnp.where(lane == idx, col, 0.0)


def _split_heads(v, low):
    zero = jnp.zeros_like(v)
    return jnp.where(low, v, zero), jnp.where(low, zero, v)


def _attn_fwd(qkv, fr):
    s = qkv.shape[0]
    nk = s // ATT_K
    width = ATT_HEADS * HEAD_DIM
    groups = HEADS // ATT_HEADS

    def body(q_ref, k_ref, v_ref, fr_ref, o_ref, lse_ref):
        qi, grp = pl.program_id(0), pl.program_id(1)
        lane = lax.broadcasted_iota(jnp.int32, (ATT_Q, 128), 1)
        low = lane < HEAD_DIM
        qs = []
        for pr in range(ATT_HEADS // 2):
            qs += _split_heads(q_ref[:, 128 * pr:128 * (pr + 1)] * (HEAD_DIM ** -0.5), low)

        def step(j, carry, masked):
            k0 = pl.multiple_of(j * ATT_K, ATT_K)
            out = []
            for h in range(ATT_HEADS):
                cols = slice(128 * (h // 2), 128 * (h // 2 + 1))
                m, l, acc = carry[h]
                logits = _att_logits(qs[h], k_ref[pl.ds(k0, ATT_K), cols], fr_ref[h, pl.ds(j, 1), :], qi * ATT_Q, k0, masked)
                m_new = jnp.maximum(m, jnp.max(logits, axis=-1, keepdims=True))
                p = jnp.exp(logits - m_new)
                alpha = jnp.exp(m - m_new)
                l = alpha * l + jnp.sum(p, axis=-1, keepdims=True)
                acc = alpha * acc + jnp.dot(p.astype(BF16), v_ref[pl.ds(k0, ATT_K), cols], preferred_element_type=F32)
                out.append((m_new, l, acc))
            return tuple(out)

        one = (jnp.full((ATT_Q, 1), NEG_INF, F32), jnp.zeros((ATT_Q, 1), F32), jnp.zeros((ATT_Q, 128), F32))
        done = _causal_sweep(step, qi, (one,) * ATT_HEADS)

        @pl.when(grp == 0)
        def _():
            lse_ref[...] = jnp.zeros_like(lse_ref)

        lse = jnp.zeros((ATT_Q, 128), F32)
        for pr in range(ATT_HEADS // 2):
            (m0, l0, acc0), (m1, l1, acc1) = done[2 * pr], done[2 * pr + 1]
            o_ref[:, 128 * pr:128 * (pr + 1)] = jnp.where(low, acc0 / l0, acc1 / l1)
            head = ATT_HEADS * grp + 2 * pr
            lse = lse + _lane_put(lane, head, m0 + jnp.log(l0)) + _lane_put(lane, head + 1, m1 + jnp.log(l1))
        lse_ref[...] += lse

    return pl.pallas_call(
        body, name="attn_fwd", grid=(s // ATT_Q, groups),
        in_specs=[pl.BlockSpec((ATT_Q, width), lambda i, g: (i, g)),
                  pl.BlockSpec((s, width), lambda i, g: (0, groups + g)),
                  pl.BlockSpec((s, width), lambda i, g: (0, 2 * groups + g)),
                  pl.BlockSpec((ATT_HEADS, nk, ATT_K), lambda i, g: (g, 0, 0))],
        out_specs=[pl.BlockSpec((ATT_Q, width), lambda i, g: (i, g)), pl.BlockSpec((ATT_Q, 128), lambda i, g: (i, 0))],
        out_shape=[jax.ShapeDtypeStruct((s, A_WIDTH), F32), jax.ShapeDtypeStruct((s, 128), F32)],
        compiler_params=_params("parallel", "arbitrary"),
    )(qkv, qkv, qkv, fr)


def _attn_bwd(qkv, do, o, lse, fr):
    s = qkv.shape[0]
    nk = s // ATT_K
    scale = HEAD_DIM ** -0.5
    heads = ATT_HEADS_BWD
    width = heads * HEAD_DIM
    groups = HEADS // heads

    def body(q_ref, k_ref, v_ref, do_ref, o_ref, lse_ref, fr_ref, dq_ref, dk_ref, dv_ref, dfc_ref, dfr_ref, dk_acc, dv_acc):
        grp = pl.program_id(0)
        lane = lax.broadcasted_iota(jnp.int32, (ATT_Q, 128), 1)
        low = lane < HEAD_DIM
        low_k = lax.broadcasted_iota(jnp.int32, (ATT_K, 128), 1) < HEAD_DIM
        dk_acc[...] = jnp.zeros_like(dk_acc)
        dv_acc[...] = jnp.zeros_like(dv_acc)
        dfr_ref[...] = jnp.zeros_like(dfr_ref)

        @pl.when(grp == 0)
        def _():
            dfc_ref[...] = jnp.zeros_like(dfc_ref)

        def outer(i, carry):
            q0 = pl.multiple_of(i * ATT_Q, ATT_Q)
            rows = pl.ds(q0, ATT_Q)
            lsev = lse_ref[rows, :]
            q2s, dob2s, qs, dos, deltas, lses = [], [], [], [], [], []
            for pr in range(heads // 2):
                pcols = slice(128 * pr, 128 * (pr + 1))
                q2, do2 = q_ref[rows, pcols] * scale, do_ref[rows, pcols]
                prod = do2 * o_ref[rows, pcols]
                deltas += [jnp.sum(jnp.where(low, prod, 0.0), axis=-1, keepdims=True),
                           jnp.sum(jnp.where(low, 0.0, prod), axis=-1, keepdims=True)]
                dob2 = do2.astype(BF16)
                q2s.append(q2)
                dob2s.append(dob2)
                qs += _split_heads(q2, low)
                dos += _split_heads(dob2, low)
                lses += [_lane_pick(lsev, lane, heads * grp + 2 * pr), _lane_pick(lsev, lane, heads * grp + 2 * pr + 1)]

            def inner(j, carry, masked):
                k0 = pl.multiple_of(j * ATT_K, ATT_K)
                krows = pl.ds(k0, ATT_K)
                out, dk_parts, dv_parts = [], [], []
                for h in range(heads):
                    pcols = slice(128 * (h // 2), 128 * (h // 2 + 1))
                    dq, dfc = carry[h]
                    k2 = k_ref[krows, pcols]
                    p = jnp.exp(_att_logits(qs[h], k2, fr_ref[h, pl.ds(j, 1), :], q0, k0, masked) - lses[h])
                    dp = lax.dot_general(dos[h], v_ref[krows, pcols], _NT, preferred_element_type=F32)
                    ds = p * (dp - deltas[h])
                    dsb = ds.astype(BF16)
                    dk_parts.append(lax.dot_general(dsb, q2s[h // 2], _TN, preferred_element_type=F32))
                    dv_parts.append(lax.dot_general(p.astype(BF16), dob2s[h // 2], _TN, preferred_element_type=F32))
                    dfr_ref[h, pl.ds(j, 1), :] -= jnp.sum(ds, axis=0, keepdims=True)
                    out.append((dq + jnp.dot(dsb, k2, preferred_element_type=F32), dfc + jnp.sum(ds, axis=-1, keepdims=True)))
                for pr in range(heads // 2):
                    pcols = slice(128 * pr, 128 * (pr + 1))
                    dk_acc[krows, pcols] += jnp.where(low_k, dk_parts[2 * pr], dk_parts[2 * pr + 1])
                    dv_acc[krows, pcols] += jnp.where(low_k, dv_parts[2 * pr], dv_parts[2 * pr + 1])
                return tuple(out)

            one = (jnp.zeros((ATT_Q, 128), F32), jnp.zeros((ATT_Q, 1), F32))
            done = _causal_sweep(inner, i, (one,) * heads)
            dfc = jnp.zeros((ATT_Q, 128), F32)
            for pr in range(heads // 2):
                (dq0, dfc0), (dq1, dfc1) = done[2 * pr], done[2 * pr + 1]
                dq_ref[rows, 128 * pr:128 * (pr + 1)] = (jnp.where(low, dq0, dq1) * scale).astype(BF16)
                head = heads * grp + 2 * pr
                dfc = dfc + _lane_put(lane, head, dfc0) + _lane_put(lane, head + 1, dfc1)
            dfc_ref[rows, :] += dfc
            return carry

        lax.fori_loop(0, s // ATT_Q, outer, 0)
        dk_ref[...] = dk_acc[...].astype(BF16)
        dv_ref[...] = dv_acc[...].astype(BF16)

    part = lambda first: pl.BlockSpec((s, width), lambda g, first=first: (0, first + g))
    whole = pl.BlockSpec((s, 128), lambda g: (0, 0))
    rowv = pl.BlockSpec((heads, nk, ATT_K), lambda g: (g, 0, 0))
    return pl.pallas_call(
        body, name="attn_bwd", grid=(groups,),
        in_specs=[part(0), part(groups), part(2 * groups), part(0), part(0), whole, rowv],
        out_specs=[part(0), part(0), part(0), whole, rowv],
        out_shape=[jax.ShapeDtypeStruct((s, A_WIDTH), BF16)] * 3 + [jax.ShapeDtypeStruct((s, 128), F32), jax.ShapeDtypeStruct((HEADS, nk, ATT_K), F32)],
        scratch_shapes=[pltpu.VMEM((s, width), F32), pltpu.VMEM((s, width), F32)],
        compiler_params=_params("arbitrary"),
    )(qkv, qkv, qkv, do, o, lse, fr)


def _ada_fwd(c_all, w_ada, b_loc):
    depth, _, n = w_ada.shape
    tn = 512

    def body(c_ref, w_ref, b_ref, o_ref, sc_ref):
        cv = c_ref[...]
        sc = cv * jax.nn.sigmoid(cv)
        sc_ref[...] = sc
        o_ref[0] = jnp.dot(sc.astype(BF16), w_ref[0].astype(BF16), preferred_element_type=F32) + b_ref[0]

    return pl.pallas_call(
        body, name="ada_fwd", grid=(depth, n // tn),
        in_specs=[pl.BlockSpec((N_DEV, D), lambda l, j: (0, 0)), pl.BlockSpec((1, D, tn), lambda l, j: (l, 0, j)),
                  pl.BlockSpec((1, 1, tn), lambda l, j: (l, 0, j))],
        out_specs=[pl.BlockSpec((1, N_DEV, tn), lambda l, j: (l, 0, j)), pl.BlockSpec((N_DEV, D), lambda l, j: (0, 0))],
        out_shape=[jax.ShapeDtypeStruct((depth, N_DEV, n), F32), jax.ShapeDtypeStruct((N_DEV, D), F32)],
        compiler_params=_params("arbitrary", "arbitrary"),
    )(c_all, w_ada, b_loc)


def _sum_devices(gathered):
    n = gathered.shape[1]
    tn = _pick(n, (1408, 1024, 640, 512, 128))

    def body(g_ref, o_ref):
        acc = g_ref[0:8, :]
        for dev in range(1, N_DEV):
            acc = acc + g_ref[8 * dev:8 * dev + 8, :]
        o_ref[...] = acc

    return pl.pallas_call(
        body, name="sum_devices", grid=(n // tn,),
        in_specs=[pl.BlockSpec((8 * N_DEV, tn), lambda j: (0, j))], out_specs=pl.BlockSpec((8, tn), lambda j: (0, j)),
        out_shape=jax.ShapeDtypeStruct((8, n), F32), compiler_params=_params("parallel"),
    )(gathered)


def _place():
    x, y, c = lax.axis_index("x"), lax.axis_index("y"), lax.axis_index("c")
    chips = [(1 - x, y), (x, 1 - y), (1 - x, 1 - y)]
    return x, y, c, chips


def _allgather8(block, name):
    m_per, n = block.shape

    def body(x_ref, out_ref, send_sems, recv_sems, local_sem):
        x, y, c, chips = _place()
        me, sibling = (x, y, c), (x, y, 1 - c)

        def rows(px, py, pc):
            return out_ref.at[pl.ds((4 * px + 2 * py + pc) * m_per, m_per), :]

        def copy(k, blk, to, src=None):
            return pltpu.make_async_remote_copy(
                src_ref=rows(*blk) if src is None else src, dst_ref=rows(*blk),
                send_sem=send_sems.at[k], recv_sem=recv_sems.at[k], device_id=to, device_id_type=MESH)

        mine = pltpu.make_async_copy(x_ref, rows(*me), local_sem)
        mine.start()
        first = [copy(0, me, sibling, src=x_ref)]
        first += [copy(1 + j, me, (*chip, c), src=x_ref) for j, chip in enumerate(chips)]
        for cp in first:
            cp.start()
        passed = [copy(4 + j, (*chip, c), sibling) for j, chip in enumerate(chips)]
        for j, chip in enumerate(chips):
            copy(1 + j, (*chip, c), me).wait_recv()
            passed[j].start()
        copy(0, sibling, me).wait_recv()
        for j, chip in enumerate(chips):
            copy(4 + j, (*chip, 1 - c), me).wait_recv()
        for cp in first + passed:
            cp.wait_send()
        mine.wait()

    return pl.pallas_call(
        body, name=name, out_shape=jax.ShapeDtypeStruct((N_DEV * m_per, n), block.dtype),
        in_specs=[pl.BlockSpec(memory_space=pltpu.VMEM)], out_specs=pl.BlockSpec(memory_space=pltpu.VMEM),
        scratch_shapes=[pltpu.SemaphoreType.DMA((7,)), pltpu.SemaphoreType.DMA((7,)), pltpu.SemaphoreType.DMA],
        compiler_params=pltpu.CompilerParams(vmem_limit_bytes=V7X_VMEM_LIMIT),
    )(block)


_SEM = pl.BlockSpec(memory_space=pltpu.SEMAPHORE)
_DATAFLOW = pltpu.SideEffectType.DATAFLOW_SIDE_EFFECTING


def _plan_copies(plan, refs, send_sems, recv_sems):
    return [pltpu.make_async_remote_copy(src_ref=src, dst_ref=dst, send_sem=send_sems.at[i], recv_sem=recv_sems.at[i],
                                         device_id=to, device_id_type=MESH) for i, (src, dst, to) in enumerate(plan(refs))]


def _copies_start(bufs, plan, n_copies, after, name):
    nb = len(bufs)

    def body(*refs):
        for cp in _plan_copies(plan, refs[:nb], refs[nb + 1], refs[nb + 2]):
            cp.start()
        token = refs[-1]
        token[...] = jnp.zeros_like(token)

    sem = pltpu.SemaphoreType.DMA((n_copies,))
    outs = pl.pallas_call(
        body, name=name,
        out_shape=(sem, sem, *[pltpu.HBM(b.shape, b.dtype) for b in bufs], jax.ShapeDtypeStruct((8, 128), F32)),
        in_specs=[_HBM] * nb + [pl.BlockSpec(memory_space=pl.ANY)],
        out_specs=(_SEM, _SEM, *[_HBM] * nb, pl.BlockSpec(memory_space=pltpu.VMEM)),
        input_output_aliases={i: 2 + i for i in range(nb)},
        compiler_params=pltpu.CompilerParams(has_side_effects=_DATAFLOW),
    )(*[pltpu.with_memory_space_constraint(b, pltpu.HBM) for b in bufs], after)
    return outs[0], outs[1], list(outs[2:2 + nb]), outs[-1]


def _copies_wait(started, plan, after, name):
    send_sems, recv_sems, bufs, _ = started
    nb = len(bufs)

    def body(*refs):
        for cp in _plan_copies(plan, refs[:nb], refs[nb], refs[nb + 1]):
            cp.wait_send()
            cp.wait_recv()

    return list(pl.pallas_call(
        body, name=name, out_shape=tuple(pltpu.HBM(b.shape, b.dtype) for b in bufs),
        in_specs=[_HBM] * nb + [_SEM, _SEM, pl.BlockSpec(memory_space=pl.ANY)], out_specs=tuple([_HBM] * nb),
        input_output_aliases={i: i for i in range(nb)},
        compiler_params=pltpu.CompilerParams(has_side_effects=_DATAFLOW),
    )(*bufs, send_sems, recv_sems, after))


def _half_rows(ref, axis, c):
    half = ref.shape[axis] // 2
    return pl.ds(c * half, half)


def _plan_gather_ici(refs):
    n = len(refs) // 2
    x, y, c, chips = _place()
    out = []
    for a in range(n):
        rows = _half_rows(refs[a], 0, c)
        out += [(refs[a].at[rows], refs[n + a].at[2 * x + y, rows], (*chip, c)) for chip in chips]
        out.append((refs[a], refs[n + a].at[2 * x + y], (x, y, 1 - c)))
    return out


def _plan_gather_d2d(refs):
    x, y, c, chips = _place()
    out = []
    for ref in refs:
        rows = _half_rows(ref, 1, c)
        for px, py in chips:
            landed = ref.at[2 * px + py, rows]
            out.append((landed, landed, (x, y, 1 - c)))
    return out


def _plan_rs_sibling(refs):
    n = len(refs) // 2
    x, y, c, _ = _place()
    return [(refs[a].at[pl.ds(0, N_CHIPS), _half_rows(refs[a], 1, 1 - c)], refs[n + a], (x, y, 1 - c)) for a in range(n)]


def _plan_rs_chips(refs):
    n = len(refs) // 2
    x, y, c, chips = _place()
    return [(refs[a].at[2 * px + py], refs[n + a].at[k], (px, py, c)) for a in range(n) for k, (px, py) in enumerate(chips)]


def _plan_rs_share(refs):
    x, y, c, _ = _place()
    return [(ref.at[_half_rows(ref, 0, c)], ref.at[_half_rows(ref, 0, c)], (x, y, 1 - c)) for ref in refs]


def _chip_sum(g, other, sel, name):
    _, half, cdim = other.shape
    tr = _pick(half, (256, 128, 64))
    per = half // tr

    def body(sel_ref, g_ref, t_ref, wire_ref, own_ref):
        total = g_ref[0] + t_ref[0]
        wire_ref[0] = total.astype(BF16)

        @pl.when(pl.program_id(1) == sel_ref[1])
        def _():
            own_ref[...] = total

    blk = pl.BlockSpec((1, tr, cdim), lambda i, p, sel_ref: (p, i, 0))
    return pl.pallas_call(
        body, name=name,
        grid_spec=pltpu.PrefetchScalarGridSpec(
            num_scalar_prefetch=1, grid=(per, N_CHIPS),
            in_specs=[pl.BlockSpec((1, tr, cdim), lambda i, p, sel_ref: (p, sel_ref[0] * per + i, 0)), blk],
            out_specs=[blk, pl.BlockSpec((tr, cdim), lambda i, p, sel_ref: (i, 0))]),
        out_shape=[jax.ShapeDtypeStruct(other.shape, BF16), jax.ShapeDtypeStruct((half, cdim), F32)],
        compiler_params=_params("parallel", "arbitrary"),
    )(sel, g, other)


def _final_sum(own, recv, sel, name):
    half, cdim = own.shape
    tr = _pick(half, (256, 128, 64))
    per = half // tr

    def body(sel_ref, own_ref, r0_ref, r1_ref, r2_ref, o_ref):
        o_ref[...] = ((own_ref[...] + r0_ref[0].astype(F32)) + r1_ref[0].astype(F32)) + r2_ref[0].astype(F32)

    part = lambda k: pl.BlockSpec((1, tr, cdim), lambda i, sel_ref, k=k: (k, i, 0))
    return pl.pallas_call(
        body, name=name,
        grid_spec=pltpu.PrefetchScalarGridSpec(
            num_scalar_prefetch=1, grid=(per,),
            in_specs=[pl.BlockSpec((tr, cdim), lambda i, sel_ref: (i, 0)), part(0), part(1), part(2)],
            out_specs=pl.BlockSpec((tr, cdim), lambda i, sel_ref: (sel_ref[0] * per + i, 0))),
        out_shape=jax.ShapeDtypeStruct((2 * half, cdim), F32), compiler_params=_params("parallel"),
    )(sel, own, recv, recv, recv)


def _row(v):
    return v.reshape(1, -1)


def _tie(v, token):
    return v if token is None else v + token[0:1, 0:1]


def _no_hook(point, after, ready=None):
    return None


def _layer_fwd(x, w, mod, hook=_no_hook):
    s = x.shape[0]
    h = _modnorm_fwd(x, _row(w["g_mix_pre"]), mod[0:1], mod[1:2], "mix_pre_fwd")
    z = _mm(h, w["w_all"], name="mm_in")
    qkv = z[:, Z_QKV:Z_PC].astype(BF16)
    fl = z[:, Z_FL:Z_COLS]
    cum = _cumf_fwd(fl, w["b_f_pad"])
    fr = cum[:, :HEADS].T.reshape(HEADS, s // ATT_K, ATT_K)
    br_a, lse = _attn_fwd(qkv, fr)
    br_b, br_c = _poolconv_fwd(z, w["w_pool_bd"], _tie(_row(w["pool_scale"]), hook("attn", lse)), w["conv_w"])
    hook("pool", br_b)
    wbr = w["w_branch"]
    pa = _mm(br_a, wbr[:A_WIDTH], name="mm_br_a")
    pb = _mm(br_b, wbr[A_WIDTH:A_WIDTH + POOL_WIDTH], name="mm_br_b")
    pc = _mm(br_c, wbr[A_WIDTH + POOL_WIDTH:], name="mm_br_c")
    merged = _merge_fwd(z, pa, pb, pc)
    y = _mm(merged, w["w_out"], name="mm_out")
    x1 = _post_fwd(x, y, _row(w["g_mix_post"]), mod[2:3], "mix_post_fwd")
    h2 = _modnorm_fwd(x1, _row(w["g_ff_pre"]), mod[3:4], mod[4:5], "ff_pre_fwd")
    a, r = _mm(h2, w["w_ff1"], b_split=N_CHIPS, epilogue=_relu2_fwd, out_dtype=(F32, BF16), name="mm_ff1")
    y2 = _mm(r, w["w_ff2"], name="mm_ff2")
    x2 = _post_fwd(x1, y2, _tie(_row(w["g_ff_post"]), hook("ff_post", y2)), mod[5:6], "ff_post_fwd")
    hook("end", x2)
    saved = dict(x=x, h=h, z=z, qkv=qkv, fl=fl, fr=fr, lse=lse, br_a=br_a, br_b=br_b, br_c=br_c, pa=pa, pb=pb, pc=pc,
                 merged=merged, y=y, x1=x1, h2=h2, a=a, r=r, y2=y2)
    return x2, saved


def _layer_bwd(dx2, sv, w, mod, hook=_no_hook):
    s = dx2.shape[0]
    dy2, sum_ff_post = _post_bwd(dx2, sv["y2"], _row(w["g_ff_post"]), mod[5:6], "ff_post_bwd")
    (da,) = _mm(dy2, w["w_ff2"], tb=True, epilogue=_relu2_bwd, extras=(sv["a"],), out_dtype=(BF16,), name="mm_ff2_dx")
    d_w_ff2 = _mm(sv["r"], dy2, ta=True, name="mm_ff2_dw")
    dh2 = _mm(da, w["w_ff1"], tb=True, b_split=N_CHIPS, name="mm_ff1_dx")
    d_w_ff1 = _mm(sv["h2"], da, ta=True, out_split=N_CHIPS, name="mm_ff1_dw")
    dx1, sum_ff_pre = _modnorm_bwd(dh2, sv["x1"], dx2, _tie(_row(w["g_ff_pre"]), hook("ff_pre", dh2, dict(w_ff1=d_w_ff1, w_ff2=d_w_ff2))), mod[4:5], "ff_pre_bwd")

    dy, sum_mix_post = _post_bwd(dx1, sv["y"], _row(w["g_mix_post"]), mod[2:3], "mix_post_bwd")
    dmerged = _mm(dy, w["w_out"], tb=True, name="mm_out_dx")
    d_w_out = _mm(sv["merged"], dy, ta=True, name="mm_out_dw")
    dz, dpa, dpb, dpc = _merge_bwd(dmerged, sv["z"], sv["pa"], sv["pb"], sv["pc"])
    wbr = w["w_branch"]
    dbr_a = _mm(dpa, wbr[:A_WIDTH], tb=True, name="mm_br_a_dx")
    dbr_b = _mm(dpb, wbr[A_WIDTH:A_WIDTH + POOL_WIDTH], tb=True, name="mm_br_b_dx")
    dbr_c = _mm(dpc, wbr[A_WIDTH + POOL_WIDTH:], tb=True, name="mm_br_c_dx")
    d_w_branch = jnp.concatenate([_mm(sv["br_a"], dpa, ta=True, name="mm_br_a_dw"), _mm(sv["br_b"], dpb, ta=True, name="mm_br_b_dw"),
                                  _mm(sv["br_c"], dpc, ta=True, name="mm_br_c_dw")], axis=0)

    dq, dk, dv, dfc, dfr = _attn_bwd(sv["qkv"], dbr_a, sv["br_a"], sv["lse"], sv["fr"])
    dcum = dfc + jnp.pad(dfr.reshape(HEADS, s).T, ((0, 0), (0, 128 - HEADS)))
    dfl, sum_bf = _cumf_bwd(dcum, sv["fl"], _tie(w["b_f_pad"], hook("cumf", dfc)))
    dpc_z, d_wbd, sum_ps, sum_cw = _poolconv_bwd(dbr_b, dbr_c, sv["z"], w["w_pool_bd"], _row(w["pool_scale"]), w["conv_w"])
    for at, part in ((Z_QKV, dq), (Z_QKV + A_WIDTH, dk), (Z_QKV + 2 * A_WIDTH, dv), (Z_PC, dpc_z), (Z_FL, dfl)):
        dz = lax.dynamic_update_slice(dz, part, (0, at))
    dh = _mm(dz, w["w_all"], tb=True, name="mm_in_dx")
    d_w_all = _mm(sv["h"], dz, ta=True, name="mm_in_dw")
    hook("mix_pre", dh)
    dx, sum_mix_pre = _modnorm_bwd(dh, sv["x"], dx1, _row(w["g_mix_pre"]), mod[1:2], "mix_pre_bwd")

    dmod = jnp.stack([sum_mix_pre[0], sum_mix_pre[1], sum_mix_post[0], sum_ff_pre[0], sum_ff_pre[1], sum_ff_post[0]])
    d_w_in = _w_in_shards(d_w_all)
    d_w_pool = jnp.stack([d_wbd[64 * g:64 * g + 64, 64 * g:64 * g + 64] for g in range(4)])
    big = dict(w_in=d_w_in, w_branch=d_w_branch, w_out=d_w_out, w_ff1=d_w_ff1, w_ff2=d_w_ff2)
    small = dict(g_mix_pre=sum_mix_pre[2], g_mix_post=sum_mix_post[1], g_ff_pre=sum_ff_pre[2], g_ff_post=sum_ff_post[1],
                 b_f=sum_bf[0, :HEADS], w_pool=d_w_pool, pool_scale=sum_ps[0], conv_w=sum_cw[0:3])
    return dx, dmod, big, small


_QKV_END, _FL_END, _PC_END = 3 * A_WIDTH, 3 * A_WIDTH + HEADS, 3 * A_WIDTH + HEADS + POOL_WIDTH + 3 * CONV_WIDTH
_W_IN_GROUPS = ((_PC_END, IN_COLS, Z_GL), (0, _QKV_END, Z_QKV), (_FL_END, _PC_END, Z_PC), (_QKV_END, _FL_END, Z_FL))
_SHARD_COLS = IN_COLS // N_CHIPS


def _w_all_from_shards(blocks):
    pieces = []
    for lo, hi, _ in _W_IN_GROUPS:
        for p in range(N_CHIPS):
            a, b = max(lo, p * _SHARD_COLS), min(hi, (p + 1) * _SHARD_COLS)
            if a < b:
                pieces.append(blocks[p][:, a - p * _SHARD_COLS:b - p * _SHARD_COLS])
    pieces.append(jnp.zeros((D, Z_COLS - IN_COLS), blocks.dtype))
    return jnp.concatenate(pieces, axis=1)


def _w_in_shards(d_w_all):
    blocks = []
    for p in range(N_CHIPS):
        pieces = []
        for lo, hi, at in sorted(_W_IN_GROUPS):
            a, b = max(lo, p * _SHARD_COLS), min(hi, (p + 1) * _SHARD_COLS)
            if a < b:
                pieces.append(d_w_all[:, at + a - lo:at + b - lo])
        blocks.append(jnp.concatenate(pieces, axis=1))
    return jnp.stack(blocks)


def _full_layer_weights(w_in_blocks, w_branch, w_out, w_ff1, w_ff2, g_mix_pre, g_mix_post, g_ff_pre, g_ff_post, b_f, w_pool, pool_scale, conv_w):
    w_all = _w_all_from_shards(w_in_blocks)
    wbd = jnp.zeros((POOL_WIDTH, POOL_WIDTH), F32)
    for g in range(4):
        wbd = wbd.at[64 * g:64 * g + 64, 64 * g:64 * g + 64].set(w_pool[g])
    return dict(w_all=w_all, w_branch=w_branch, w_out=w_out, w_ff1=w_ff1, w_ff2=w_ff2, g_mix_pre=g_mix_pre, g_mix_post=g_mix_post,
                g_ff_pre=g_ff_pre, g_ff_post=g_ff_post, b_f_pad=jnp.pad(b_f, (0, 128 - HEADS)).reshape(1, 128), w_pool_bd=wbd,
                pool_scale=pool_scale, conv_w=conv_w)


class _NoComm:
    def layer_weights(self, l):
        raise NotImplementedError

    def fwd_hook(self, l):
        return _no_hook

    def bwd_hook(self, l):
        return _no_hook

    def grads_ready(self, l, big):
        return None


class _Layers(_NoComm):
    def __init__(self, layers):
        self.layers = layers

    def layer_weights(self, l):
        return self.layers[l]


def _local_step(x, target, mods, comm):
    saved, weights = [], []
    act = x
    for l in range(DEPTH):
        weights.append(comm.layer_weights(l))
        act, sv = _layer_fwd(act, weights[l], mods[l], comm.fwd_hook(l))
        saved.append(sv)
    dact, sq = _loss_head(act, target)
    loss = sq[0, 0] * (0.5 / D)
    dmods, bigs, smalls = [None] * DEPTH, [None] * DEPTH, [None] * DEPTH
    token = None
    for l in reversed(range(DEPTH)):
        dact, dmods[l], bigs[l], smalls[l] = _layer_bwd(dact, saved[l], weights[l], _tie(mods[l], token), comm.bwd_hook(l))
        token = comm.grads_ready(l, bigs[l])
    return loss, dact, jnp.stack(dmods), bigs, smalls


_BIG = ("w_in", "w_branch", "w_out", "w_ff1", "w_ff2")
N_BIG = len(_BIG)


class _GatherJob:
    def __init__(self, tag, shards, after):
        self.tag, self.n = tag, len(shards)
        lands = [lax.empty((N_CHIPS,) + s.shape, s.dtype) for s in shards]
        self.state = _copies_start(list(shards) + lands, _plan_gather_ici, 4 * self.n, after, "gather_ici_start_" + tag)
        self.token = self.state[3]

    def pass_on(self, after):
        bufs = _copies_wait(self.state, _plan_gather_ici, after, "gather_ici_wait_" + self.tag)
        self.state = _copies_start(bufs[self.n:], _plan_gather_d2d, 3 * self.n, bufs[0], "gather_d2d_start_" + self.tag)
        self.token = self.state[3]
        return self.token

    def done(self, after):
        return _copies_wait(self.state, _plan_gather_d2d, after, "gather_d2d_wait_" + self.tag)


class _ReduceJob:
    def __init__(self, tag, names, grads, sel, after):
        self.tag, self.names, self.n, self.sel = tag, names, len(names), sel
        lands = [lax.empty((N_CHIPS, g.shape[1] // 2, g.shape[2]), F32) for g in grads]
        self.state = _copies_start(list(grads) + lands, _plan_rs_sibling, self.n, after, "rs_sibling_start_" + tag)
        self.token = self.state[3]

    def chip_sums(self, after):
        bufs = _copies_wait(self.state, _plan_rs_sibling, after, "rs_sibling_wait_" + self.tag)
        wires, self.owns = zip(*[_chip_sum(bufs[i], bufs[self.n + i], self.sel, "rs_chip_sum_" + name) for i, name in enumerate(self.names)])
        lands = [lax.empty((3,) + w.shape[1:], BF16) for w in wires]
        self.state = _copies_start(list(wires) + lands, _plan_rs_chips, 3 * self.n, self.owns[0], "rs_chips_start_" + self.tag)
        self.token = self.state[3]
        return self.token

    def final_sums(self, after):
        bufs = _copies_wait(self.state, _plan_rs_chips, after, "rs_chips_wait_" + self.tag)
        sums = [_final_sum(self.owns[i], bufs[self.n + i], self.sel, "rs_final_" + name) for i, name in enumerate(self.names)]
        self.state = _copies_start(sums, _plan_rs_share, self.n, sums[0], "rs_share_start_" + self.tag)
        self.token = self.state[3]
        return self.token

    def done(self, after):
        return dict(zip(self.names, _copies_wait(self.state, _plan_rs_share, after, "rs_share_wait_" + self.tag)))


def _chip_blocks(g):
    return g if g.ndim == 3 else g.reshape(N_CHIPS, -1, g.shape[1])


class _StepComm(_NoComm):
    def __init__(self, shards, sel, after):
        self.sel = sel
        self.small, self.grads, self.jobs = None, [dict() for _ in range(DEPTH)], {}
        self.jobs["in0"] = _GatherJob("in0", shards[0][:1], after)
        self.jobs["rest0"] = _GatherJob("rest0", shards[0][1:], self.jobs["in0"].token)
        self.jobs["all1"] = _GatherJob("all1", shards[1], self.jobs["rest0"].token)

    def layer_weights(self, l):
        if l == 0:
            job = self.jobs["in0"]
            (g_in,) = job.done(job.pass_on(self.jobs["all1"].token))
            self.weights0 = _full_layer_weights(g_in, None, None, None, None, *self.small[0])
            return self.weights0
        g_in, g_br, g_out, g_f1, g_f2 = self.landed1
        return _full_layer_weights(g_in, g_br.reshape(D, D), g_out.reshape(D, D), g_f1, g_f2.reshape(D_FF, D), *self.small[1])

    def fwd_hook(self, l):
        if l != 0:
            return _no_hook

        def hook(point, after, ready=None):
            if point == "attn":
                return self.jobs["rest0"].pass_on(after)
            if point == "ff_post":
                return self.jobs["all1"].pass_on(after)
            if point == "pool":
                g_br, g_out, g_f1, g_f2 = self.jobs["rest0"].done(after)
                self.weights0.update(w_branch=g_br.reshape(D, D), w_out=g_out.reshape(D, D), w_ff1=g_f1, w_ff2=g_f2.reshape(D_FF, D))
            if point == "end":
                self.landed1 = self.jobs["all1"].done(after)
            return None
        return hook

    def bwd_hook(self, l):
        if l != 0:
            return _no_hook

        def hook(point, after, ready=None):
            jobs = self.jobs
            if point == "ff_pre":
                token = jobs["rs1"].chip_sums(after)
                jobs["rs0_ff"] = _ReduceJob("0_ff", ("w_ff1", "w_ff2"), [_chip_blocks(ready[n]) for n in ("w_ff1", "w_ff2")], self.sel, token)
                return jobs["rs0_ff"].token
            if point == "cumf":
                return jobs["rs0_ff"].chip_sums(jobs["rs1"].final_sums(after))
            self.grads[1] = jobs["rs1"].done(after)
            return None
        return hook

    def grads_ready(self, l, big):
        if l == 1:
            self.jobs["rs1"] = _ReduceJob("1", _BIG, [_chip_blocks(big[n]) for n in _BIG], self.sel, self.sel)
            return self.jobs["rs1"].token
        names = ("w_in", "w_branch", "w_out")
        self.jobs["rs0_mix"] = _ReduceJob("0_mix", names, [_chip_blocks(big[n]) for n in names], self.sel, self.sel)
        return self.jobs["rs0_mix"].token

    def finish_sums(self, after):
        jobs = self.jobs
        token = jobs["rs0_mix"].chip_sums(after)
        return jobs["rs0_ff"].final_sums(token)

    def finish(self, after):
        jobs = self.jobs
        self.grads[0] = jobs["rs0_ff"].done(after)
        self.grads[0].update(jobs["rs0_mix"].done(jobs["rs0_mix"].final_sums(after)))


_SMALL = ("g_mix_pre", "g_mix_post", "g_ff_pre", "g_ff_post", "b_f", "w_pool", "pool_scale", "conv_w")


def _w_in_view(t):
    return t.reshape(DEPTH, D // 128, 128, _SHARD_COLS).transpose(3, 1, 0, 2).reshape(_SHARD_COLS * (D // 128) * DEPTH, 128)


def _w_in_unview(t):
    return t.reshape(_SHARD_COLS, D // 128, DEPTH, 128).transpose(2, 1, 3, 0).reshape(DEPTH, D, _SHARD_COLS)


def _pack(parts, rows=8):
    flat = jnp.concatenate([p.reshape(-1) for p in parts])
    width = -(-flat.shape[0] // (rows * 128)) * 128
    return jnp.pad(flat, (0, rows * width - flat.shape[0])).reshape(rows, width)


def _unpack(packed, like):
    flat = packed.reshape(-1)
    out, at = [], 0
    for ref in like:
        out.append(flat[at:at + ref.size].reshape(ref.shape))
        at += ref.size
    return out


def kernel(x, c, w_ada, b_ada, g_mix_pre, g_mix_post, g_ff_pre, g_ff_post, w_in, b_f, w_pool, pool_scale, conv_w, w_branch, w_out, w_ff1, w_ff2, loss_target, m_w_ada, m_b_ada, m_g_mix_pre, m_g_mix_post, m_g_ff_pre, m_g_ff_post, m_w_in, m_b_f, m_w_pool, m_pool_scale, m_conv_w, m_w_branch, m_w_out, m_w_ff1, m_w_ff2, v_w_ada, v_b_ada, v_g_mix_pre, v_g_mix_post, v_g_ff_pre, v_g_ff_post, v_w_in, v_b_f, v_w_pool, v_pool_scale, v_conv_w, v_w_branch, v_w_out, v_w_ff1, v_w_ff2):
    xi, yi, ci = lax.axis_index("x"), lax.axis_index("y"), lax.axis_index("c")
    chip = 2 * xi + yi
    dev = 2 * chip + ci
    n_ada = w_ada.shape[2]

    first = jnp.zeros((8, D + 384), F32).at[0, :D].set(c[0]).at[0, D:].set(conv_w.reshape(-1))
    got = _allgather8(first, "gather_cond").reshape(N_DEV, 8, D + 384)[:, 0]
    c_all = got[:, :D]
    conv_full = got[0::2, D:].reshape(N_CHIPS, DEPTH, 3, CONV_WIDTH // N_CHIPS).transpose(1, 2, 0, 3).reshape(DEPTH, 3, CONV_WIDTH)

    b_loc = lax.dynamic_slice_in_dim(b_ada, chip * n_ada, n_ada, axis=1).reshape(DEPTH, 1, n_ada)
    mod_cols, silu_c = _ada_fwd(c_all, w_ada, b_loc)
    got = _allgather8(mod_cols.reshape(DEPTH * N_DEV, n_ada), "gather_mod").reshape(N_DEV, DEPTH, N_DEV, n_ada)[0::2]
    mod_all = got.transpose(1, 2, 0, 3).reshape(DEPTH, N_DEV, 6, D)
    mods = lax.dynamic_index_in_dim(mod_all, dev, axis=1, keepdims=False)

    comm = _StepComm([[w[l].astype(BF16) for w in (w_in, w_branch, w_out, w_ff1, w_ff2)] for l in range(DEPTH)],
                     jnp.stack([ci, chip]).astype(jnp.int32), mods)
    comm.small = [(g_mix_pre[l], g_mix_post[l], g_ff_pre[l], g_ff_post[l], b_f[l], w_pool[l], pool_scale[l], conv_full[l]) for l in range(DEPTH)]
    loss_part, grad_x, dmods, bigs, smalls = _local_step(x[0], loss_target[0], mods, comm)
    loss = lax.psum(loss_part, ("x", "y", "c"))

    small_parts = [smalls[l][name] for name in _SMALL for l in range(DEPTH)]
    packed = _tie(_pack([dmods] + small_parts), comm.jobs["rs0_mix"].token)
    gathered = _allgather8(packed, "gather_small")
    dmod_all = gathered.reshape(N_DEV, -1)[:, :dmods.size].reshape(N_DEV, DEPTH, 6 * D)
    summed = _unpack(_sum_devices(gathered), [dmods] + small_parts)
    grad_b_ada = summed[0].reshape(DEPTH, 6 * D)
    small_grads = {name: jnp.stack(summed[1 + 2 * i:3 + 2 * i]) for i, name in enumerate(_SMALL)}
    small_grads["conv_w"] = lax.dynamic_slice_in_dim(small_grads["conv_w"], chip * (CONV_WIDTH // N_CHIPS), CONV_WIDTH // N_CHIPS, axis=2)

    dmod_loc = lax.dynamic_slice_in_dim(dmod_all.transpose(1, 0, 2), chip * n_ada, n_ada, axis=2)
    tail_token = comm.finish_sums(grad_b_ada)
    silu_pad = _tie(jnp.pad(silu_c, ((0, 128 - N_DEV), (0, 0))), tail_token)
    grad_w_ada = jnp.stack([_mm(silu_pad, jnp.pad(dmod_loc[l], ((0, 128 - N_DEV), (0, 0))), ta=True, name="mm_ada_dw") for l in range(DEPTH)])

    grads = dict(w_ada=grad_w_ada, b_ada=grad_b_ada, **small_grads)
    weights = dict(w_ada=w_ada, b_ada=b_ada, g_mix_pre=g_mix_pre, g_mix_post=g_mix_post, g_ff_pre=g_ff_pre, g_ff_post=g_ff_post, w_in=w_in,
                   b_f=b_f, w_pool=w_pool, pool_scale=pool_scale, conv_w=conv_w, w_branch=w_branch, w_out=w_out, w_ff1=w_ff1, w_ff2=w_ff2)
    m_in = dict(w_ada=m_w_ada, b_ada=m_b_ada, g_mix_pre=m_g_mix_pre, g_mix_post=m_g_mix_post, g_ff_pre=m_g_ff_pre, g_ff_post=m_g_ff_post,
                w_in=m_w_in, b_f=m_b_f, w_pool=m_w_pool, pool_scale=m_pool_scale, conv_w=m_conv_w, w_branch=m_w_branch, w_out=m_w_out,
                w_ff1=m_w_ff1, w_ff2=m_w_ff2)
    v_in = dict(w_ada=v_w_ada, b_ada=v_b_ada, g_mix_pre=v_g_mix_pre, g_mix_post=v_g_mix_post, g_ff_pre=v_g_ff_pre, g_ff_post=v_g_ff_post,
                w_in=v_w_in, b_f=v_b_f, w_pool=v_w_pool, pool_scale=v_pool_scale, conv_w=v_conv_w, w_branch=v_w_branch, w_out=v_w_out,
                w_ff1=v_w_ff1, w_ff2=v_w_ff2)
    order = ("w_ada", "b_ada", "g_mix_pre", "g_mix_post", "g_ff_pre", "g_ff_post", "w_in", "b_f", "w_pool", "pool_scale", "conv_w",
             "w_branch", "w_out", "w_ff1", "w_ff2")
    delta, new_m, new_v = {}, {}, {}
    tiny = ("b_ada",) + _SMALL
    packed_w, packed_g, packed_m, packed_v = [_pack([src[name] for name in tiny]) for src in (weights, grads, m_in, v_in)]
    res = _adamw(packed_w, _tie(packed_g, tail_token), packed_m, packed_v, "adamw_small")
    for out, packed_res in zip((delta, new_m, new_v), res):
        for name, val in zip(tiny, _unpack(packed_res, [weights[name] for name in tiny])):
            out[name] = val
    delta["w_ada"], new_m["w_ada"], new_v["w_ada"] = _adamw(w_ada, grad_w_ada, m_w_ada, v_w_ada, "adamw_w_ada")
    comm.finish(delta["w_ada"][0, :8, :128] + delta["b_ada"][0, :128])
    for name in _BIG:
        grads[name] = jnp.stack([comm.grads[l][name] for l in range(DEPTH)])
        if name == "w_in":
            g_view = lax.optimization_barrier(_w_in_view(grads[name]))
            res = _adamw(_w_in_view(w_in), g_view, _w_in_view(m_w_in), _w_in_view(v_w_in), "adamw_w_in")
            grads[name], delta[name], new_m[name], new_v[name] = [_w_in_unview(t) for t in (g_view, *res)]
        else:
            delta[name], new_m[name], new_v[name] = _adamw(weights[name], grads[name], m_in[name], v_in[name], "adamw_" + name)

    return (loss, grad_x[None], *[grads[n] for n in order], *[delta[n] for n in order], *[new_m[n] for n in order],
            *[new_v[n] for n in order])
```

```python
import functools

import jax
import jax.numpy as jnp
from jax import lax
from jax.experimental import pallas as pl
from jax.experimental.pallas import tpu as pltpu

F32 = jnp.float32
BF16 = jnp.bfloat16
MESH = pl.DeviceIdType.MESH

D = 1024
DEPTH = 2
HEADS = 8
HEAD_DIM = 64
A_WIDTH = 512
POOL_WIDTH = 256
CONV_WIDTH = 256
D_FF = 4096
IN_COLS = 5640
Z_GL, Z_QKV, Z_PC, Z_FL, Z_COLS = 0, 3072, 4608, 5632, 5760
RMS_EPS = 1e-6
NEG_INF = -1e30
ROW_TILE = 256
N_CHIPS = 4
N_DEV = 8
V7X_VMEM_LIMIT = 48 * 1024 * 1024

ADAM_LR = 0.001
ADAM_B1 = 0.9
ADAM_B2 = 0.999
ADAM_EPS = 1e-08
ADAM_WD = 0.01
ADAM_STEP = 10

_HBM = pl.BlockSpec(memory_space=pltpu.HBM)


def _params(*sem):
    return pltpu.CompilerParams(dimension_semantics=sem, vmem_limit_bytes=V7X_VMEM_LIMIT)


def _pick(dim, cands):
    for cand in cands:
        if dim % cand == 0:
            return cand
    return dim


def _mm(a, b, *, ta=False, tb=False, b_split=1, out_split=1, out_dtype=F32, epilogue=None, extras=(), name):
    (k, m) = a.shape if ta else a.shape[::-1]
    b_rows, b_cols = b.shape[-2], b.shape[-1] * b_split
    (n, k2) = (b_rows, b_cols) if tb else (b_cols, b_rows)
    assert k == k2, (a.shape, b.shape, ta, tb)
    n_unit = n // (out_split * (1 if tb else b_split))
    k_unit = k // (b_split if tb else 1)
    tm = _pick(m, (1024, 512, 256, 128))
    tn = _pick(n_unit, (1024, 1152, 768, 640, 512, 256, 128))
    tk = _pick(k_unit, (1024, 1152, 512, 640, 256, 128))
    nk = k // tk
    dims = (((0 if ta else 1,), (1 if tb else 0,)), ((), ()))

    def dot(a_ref, b_ref):
        b_val = b_ref[0] if b_split > 1 else b_ref[...]
        return lax.dot_general(a_ref[...].astype(BF16), b_val.astype(BF16), dims, preferred_element_type=F32)

    n_extra = len(extras)
    assert epilogue is None or out_split == 1

    def put(refs, val):
        if epilogue is not None:
            for o_ref, res in zip(refs[n_extra:], epilogue(val, *[r[...] for r in refs[:n_extra]])):
                o_ref[...] = res.astype(o_ref.dtype)
        elif out_split > 1:
            refs[0][0] = val.astype(refs[0].dtype)
        else:
            refs[0][...] = val.astype(refs[0].dtype)

    def body_single(a_ref, b_ref, *refs):
        put(refs, dot(a_ref, b_ref))

    def body_acc(a_ref, b_ref, *refs):
        kk = pl.program_id(2)
        acc_ref = refs[-1]

        @pl.when(kk == 0)
        def _():
            acc_ref[...] = jnp.zeros_like(acc_ref)

        acc_ref[...] += dot(a_ref, b_ref)

        @pl.when(kk == nk - 1)
        def _():
            put(refs[:-1], acc_ref[...])

    a_spec = pl.BlockSpec((tk, tm), lambda i, j, kk: (kk, i)) if ta else pl.BlockSpec((tm, tk), lambda i, j, kk: (i, kk))
    if b_split == 1:
        b_spec = pl.BlockSpec((tn, tk), lambda i, j, kk: (j, kk)) if tb else pl.BlockSpec((tk, tn), lambda i, j, kk: (kk, j))
    elif tb:
        per = k_unit // tk
        b_spec = pl.BlockSpec((1, tn, tk), lambda i, j, kk: (kk // per, j, kk % per))
    else:
        per = n // b_split // tn
        b_spec = pl.BlockSpec((1, tk, tn), lambda i, j, kk: (j // per, kk, j % per))
    if out_split == 1:
        o_spec = pl.BlockSpec((tm, tn), lambda i, j, kk: (i, j))
        o_shape = None if epilogue is not None else jax.ShapeDtypeStruct((m, n), out_dtype)
    else:
        per_o = n // out_split // tn
        o_spec = pl.BlockSpec((1, tm, tn), lambda i, j, kk: (j // per_o, i, j % per_o))
        o_shape = jax.ShapeDtypeStruct((out_split, m, n // out_split), out_dtype)
    if epilogue is not None:
        o_shape = [jax.ShapeDtypeStruct((m, n), dt) for dt in out_dtype]
        o_spec = [o_spec] * len(out_dtype)
    return pl.pallas_call(
        body_single if nk == 1 else body_acc, name=name, grid=(m // tm, n // tn, nk),
        in_specs=[a_spec, b_spec] + [pl.BlockSpec((tm, tn), lambda i, j, kk: (i, j))] * n_extra, out_specs=o_spec, out_shape=o_shape,
        scratch_shapes=[] if nk == 1 else [pltpu.VMEM((tm, tn), F32)],
        compiler_params=_params("parallel", "parallel", "arbitrary"),
    )(a, b, *extras)


def _ew(fn, ins, out_dtypes, name, tc=None):
    shape = ins[0].shape
    lead, (rows, cols) = shape[:-2], shape[-2:]
    tc = cols if tc is None else tc
    tr = _pick(rows, (ROW_TILE, 128, 8)) if tc > 128 else _pick(rows, (4096, 2256, 2048, 1024, ROW_TILE, 8))
    n_in = len(ins)

    def body(*refs):
        res = fn(*[r[...] for r in refs[:n_in]])
        for o_ref, val in zip(refs[n_in:], res):
            o_ref[...] = val.astype(o_ref.dtype)

    if lead:
        spec = pl.BlockSpec((None, tr, tc), lambda l, i, j: (l, i, j))
    else:
        spec = pl.BlockSpec((tr, tc), lambda i, j: (i, j))
    return pl.pallas_call(
        body, name=name, grid=lead + (rows // tr, cols // tc),
        in_specs=[spec] * n_in, out_specs=[spec] * len(out_dtypes),
        out_shape=[jax.ShapeDtypeStruct(shape, dt) for dt in out_dtypes],
        compiler_params=_params(*(["parallel"] * (len(lead) + 2))),
    )(*ins)


def _relu2_fwd(a):
    r = jnp.maximum(a, 0.0)
    return a, r * r


def _relu2_bwd(dr, a):
    return (dr * (2.0 * jnp.maximum(a, 0.0)),)


def _adamw(w, g, m, v, name):
    bc1 = 1.0 - ADAM_B1 ** ADAM_STEP
    bc2 = 1.0 - ADAM_B2 ** ADAM_STEP

    def fn(w, g, m, v):
        m = ADAM_B1 * m + (1.0 - ADAM_B1) * g
        v = ADAM_B2 * v + (1.0 - ADAM_B2) * (g * g)
        m_hat = m / bc1
        v_hat = v / bc2
        delta = -ADAM_LR * (m_hat / (jnp.sqrt(v_hat) + ADAM_EPS) + ADAM_WD * w)
        return delta, m, v
    return _ew(fn, [w, g, m, v], [F32, F32, F32], name)


def _row_spec(cols, block=0):
    return pl.BlockSpec((ROW_TILE, cols), lambda i, block=block: (i, block))


def _vec_spec(cols):
    return pl.BlockSpec((1, cols), lambda i: (0, 0))


def _sum_spec(cols):
    return pl.BlockSpec((8, cols), lambda i: (0, 0))


def _rstd(x):
    return lax.rsqrt(jnp.mean(x * x, axis=-1, keepdims=True) + RMS_EPS)


def _modnorm_fwd(x, g, shift, scale, name):
    s = x.shape[0]

    def body(x_ref, g_ref, sh_ref, sc_ref, h_ref):
        xv = x_ref[...]
        n = xv * _rstd(xv)
        h_ref[...] = ((n * g_ref[...]) * (1.0 + sc_ref[...]) + sh_ref[...]).astype(BF16)

    return pl.pallas_call(
        body, name=name, grid=(s // ROW_TILE,),
        in_specs=[_row_spec(D), _vec_spec(D), _vec_spec(D), _vec_spec(D)], out_specs=_row_spec(D),
        out_shape=jax.ShapeDtypeStruct((s, D), BF16), compiler_params=_params("parallel"),
    )(x, g, shift, scale)


def _post_fwd(x, y, g, gate, name):
    s = x.shape[0]

    def body(x_ref, y_ref, g_ref, gate_ref, o_ref):
        yv = y_ref[...]
        o_ref[...] = x_ref[...] + gate_ref[...] * ((yv * _rstd(yv)) * g_ref[...])

    return pl.pallas_call(
        body, name=name, grid=(s // ROW_TILE,),
        in_specs=[_row_spec(D), _row_spec(D), _vec_spec(D), _vec_spec(D)], out_specs=_row_spec(D),
        out_shape=jax.ShapeDtypeStruct((s, D), F32), compiler_params=_params("parallel"),
    )(x, y, g, gate)


def _post_bwd(dxo, y, g, gate, name):
    s = dxo.shape[0]

    def body(d_ref, y_ref, g_ref, gate_ref, dy_ref, sum_ref):
        @pl.when(pl.program_id(0) == 0)
        def _():
            sum_ref[...] = jnp.zeros_like(sum_ref)

        dv, yv = d_ref[...], y_ref[...]
        r = _rstd(yv)
        n = yv * r
        sum_ref[0:1, :] += jnp.sum(dv * (n * g_ref[...]), axis=0, keepdims=True)
        sum_ref[1:2, :] += jnp.sum((dv * gate_ref[...]) * n, axis=0, keepdims=True)
        dn = (dv * gate_ref[...]) * g_ref[...]
        dy_ref[...] = (r * (dn - n * jnp.mean(dn * n, axis=-1, keepdims=True))).astype(BF16)

    return pl.pallas_call(
        body, name=name, grid=(s // ROW_TILE,),
        in_specs=[_row_spec(D), _row_spec(D), _vec_spec(D), _vec_spec(D)],
        out_specs=[_row_spec(D), _sum_spec(D)],
        out_shape=[jax.ShapeDtypeStruct((s, D), BF16), jax.ShapeDtypeStruct((8, D), F32)],
        compiler_params=_params("arbitrary"),
    )(dxo, y, g, gate)


def _modnorm_bwd(dh, x, dxo, g, scale, name):
    s = dh.shape[0]

    def body(dh_ref, x_ref, d_ref, g_ref, sc_ref, dx_ref, sum_ref):
        @pl.when(pl.program_id(0) == 0)
        def _():
            sum_ref[...] = jnp.zeros_like(sum_ref)

        dhv, xv = dh_ref[...], x_ref[...]
        r = _rstd(xv)
        n = xv * r
        one_sc = 1.0 + sc_ref[...]
        sum_ref[0:1, :] += jnp.sum(dhv, axis=0, keepdims=True)
        sum_ref[1:2, :] += jnp.sum(dhv * (n * g_ref[...]), axis=0, keepdims=True)
        sum_ref[2:3, :] += jnp.sum((dhv * one_sc) * n, axis=0, keepdims=True)
        dn = (dhv * one_sc) * g_ref[...]
        dx_ref[...] = d_ref[...] + r * (dn - n * jnp.mean(dn * n, axis=-1, keepdims=True))

    return pl.pallas_call(
        body, name=name, grid=(s // ROW_TILE,),
        in_specs=[_row_spec(D), _row_spec(D), _row_spec(D), _vec_spec(D), _vec_spec(D)],
        out_specs=[_row_spec(D), _sum_spec(D)],
        out_shape=[jax.ShapeDtypeStruct((s, D), F32), jax.ShapeDtypeStruct((8, D), F32)],
        compiler_params=_params("arbitrary"),
    )(dh, x, dxo, g, scale)


def _loss_head(y, target):
    s = y.shape[0]

    def body(y_ref, t_ref, dy_ref, sum_ref):
        @pl.when(pl.program_id(0) == 0)
        def _():
            sum_ref[...] = jnp.zeros_like(sum_ref)

        err = y_ref[...] - t_ref[...]
        dy_ref[...] = err * (1.0 / D)
        sum_ref[...] += jnp.sum(err * err)

    return pl.pallas_call(
        body, name="loss_head", grid=(s // ROW_TILE,),
        in_specs=[_row_spec(D), _row_spec(D)],
        out_specs=[_row_spec(D), pl.BlockSpec((8, 128), lambda i: (0, 0))],
        out_shape=[jax.ShapeDtypeStruct((s, D), F32), jax.ShapeDtypeStruct((8, 128), F32)],
        compiler_params=_params("arbitrary"),
    )(y, target)


def _merge_fwd(z, pa, pb, pc):
    s = z.shape[0]

    def body(g0_ref, g1_ref, g2_ref, pa_ref, pb_ref, pc_ref, o_ref):
        o_ref[...] = (jax.nn.sigmoid(g0_ref[...]) * pa_ref[...] + jax.nn.sigmoid(g1_ref[...]) * pb_ref[...]
                      + jax.nn.sigmoid(g2_ref[...]) * pc_ref[...]).astype(BF16)

    return pl.pallas_call(
        body, name="merge_fwd", grid=(s // ROW_TILE,),
        in_specs=[_row_spec(D, 0), _row_spec(D, 1), _row_spec(D, 2), _row_spec(D), _row_spec(D), _row_spec(D)],
        out_specs=_row_spec(D), out_shape=jax.ShapeDtypeStruct((s, D), BF16),
        compiler_params=_params("parallel"),
    )(z, z, z, pa, pb, pc)


def _merge_bwd(dm, z, pa, pb, pc):
    s = z.shape[0]

    def body(dm_ref, g0_ref, g1_ref, g2_ref, pa_ref, pb_ref, pc_ref, dgl_ref, da_ref, db_ref, dc_ref):
        dmv = dm_ref[...]
        for i, (g_ref, p_ref, d_ref) in enumerate(((g0_ref, pa_ref, da_ref), (g1_ref, pb_ref, db_ref), (g2_ref, pc_ref, dc_ref))):
            gate = jax.nn.sigmoid(g_ref[...])
            dgl_ref[:, i * D:(i + 1) * D] = ((dmv * p_ref[...]) * (gate * (1.0 - gate))).astype(BF16)
            d_ref[...] = (dmv * gate).astype(BF16)

    return pl.pallas_call(
        body, name="merge_bwd", grid=(s // ROW_TILE,),
        in_specs=[_row_spec(D), _row_spec(D, 0), _row_spec(D, 1), _row_spec(D, 2), _row_spec(D), _row_spec(D), _row_spec(D)],
        out_specs=[_row_spec(3 * D), _row_spec(D), _row_spec(D), _row_spec(D)],
        out_shape=[jax.ShapeDtypeStruct((s, Z_COLS), BF16)] + [jax.ShapeDtypeStruct((s, D), BF16)] * 3,
        compiler_params=_params("parallel"),
    )(dm, z, z, z, pa, pb, pc)


def _shift_down(v, n):
    row = lax.broadcasted_iota(jnp.int32, v.shape, 0)
    return jnp.where(row >= n, pltpu.roll(v, n, axis=0), 0.0)


def _shift_up(v, n):
    s = v.shape[0]
    row = lax.broadcasted_iota(jnp.int32, v.shape, 0)
    return jnp.where(row < s - n, pltpu.roll(v, s - n, axis=0), 0.0)


def _log_sigmoid(v):
    return jnp.minimum(v, 0.0) - jnp.log1p(jnp.exp(-jnp.abs(v)))


def _cumf_fwd(fl, bias):
    s = fl.shape[0]

    def body(fl_ref, b_ref, o_ref):
        acc = _log_sigmoid(fl_ref[...] + b_ref[...])
        step = 1
        while step < s:
            acc = acc + _shift_down(acc, step)
            step *= 2
        o_ref[...] = acc

    return pl.pallas_call(body, name="cumf_fwd", out_shape=jax.ShapeDtypeStruct((s, 128), F32),
                          compiler_params=pltpu.CompilerParams(vmem_limit_bytes=V7X_VMEM_LIMIT))(fl, bias)


def _cumf_bwd(dcum, fl, bias):
    s = fl.shape[0]

    def body(d_ref, fl_ref, b_ref, dfl_ref, db_ref):
        acc = d_ref[...]
        step = 1
        while step < s:
            acc = acc + _shift_up(acc, step)
            step *= 2
        dfl = acc * jax.nn.sigmoid(-(fl_ref[...] + b_ref[...]))
        dfl_ref[...] = dfl.astype(BF16)
        db_ref[...] = jnp.broadcast_to(jnp.sum(dfl, axis=0, keepdims=True), (8, 128))

    return pl.pallas_call(
        body, name="cumf_bwd",
        out_shape=[jax.ShapeDtypeStruct((s, 128), BF16), jax.ShapeDtypeStruct((8, 128), F32)],
        compiler_params=pltpu.CompilerParams(vmem_limit_bytes=V7X_VMEM_LIMIT))(dcum, fl, bias)


def _pool_windows(v, shift):
    s2 = v + shift(v, 1)
    s4 = s2 + shift(s2, 2)
    s8 = s4 + shift(s4, 4)
    s16 = s8 + shift(s8, 8)
    group = lax.broadcasted_iota(jnp.int32, v.shape, 1) // 64
    return jnp.where(group == 0, s2, jnp.where(group == 1, s4, jnp.where(group == 2, s8, s16)))


def _pool_count(shape):
    group = lax.broadcasted_iota(jnp.int32, shape, 1) // 64
    window = jnp.where(group == 0, 2.0, jnp.where(group == 1, 4.0, jnp.where(group == 2, 8.0, 16.0)))
    t1 = (lax.broadcasted_iota(jnp.int32, shape, 0) + 1).astype(F32)
    return jnp.minimum(t1, window)


def _pc_specs(s):
    zcol = lambda blk: pl.BlockSpec((s, 256), lambda i, blk=blk: (0, blk))
    first = Z_PC // 256
    return [zcol(first), zcol(first + 1), zcol(first + 2), zcol(first + 3),
            pl.BlockSpec((256, 256), lambda i: (0, 0)), pl.BlockSpec((1, 256), lambda i: (0, 0)),
            pl.BlockSpec((3, 256), lambda i: (0, 0))]


def _poolconv_fwd(z, wbd, pscale, convw):
    s = z.shape[0]

    def body(pu_ref, ch_ref, cb_ref, cc_ref, w_ref, ps_ref, cw_ref, yb_ref, yc_ref):
        u = pu_ref[...]
        p = _pool_windows(u, _shift_down) / _pool_count(u.shape) - u
        yb = jnp.dot(p.astype(BF16), w_ref[...].astype(BF16), preferred_element_type=F32) * ps_ref[...]
        yb_ref[...] = yb.astype(BF16)
        uc = cc_ref[...] * ch_ref[...]
        cw = cw_ref[...]
        conv = cw[0:1, :] * _shift_down(uc, 2) + cw[1:2, :] * _shift_down(uc, 1) + cw[2:3, :] * uc
        yc_ref[...] = (cb_ref[...] * conv).astype(BF16)

    out = pl.BlockSpec((s, 256), lambda i: (0, 0))
    return pl.pallas_call(
        body, name="poolconv_fwd", grid=(1,), in_specs=_pc_specs(s), out_specs=[out, out],
        out_shape=[jax.ShapeDtypeStruct((s, 256), BF16)] * 2, compiler_params=_params("arbitrary"),
    )(z, z, z, z, wbd, pscale, convw)


def _poolconv_bwd(dyb, dyc, z, wbd, pscale, convw):
    s = z.shape[0]

    def body(dyb_ref, dyc_ref, pu_ref, ch_ref, cb_ref, cc_ref, w_ref, ps_ref, cw_ref, dz_ref, dw_ref, dps_ref, dcw_ref):
        u = pu_ref[...]
        count = _pool_count(u.shape)
        p = (_pool_windows(u, _shift_down) / count - u).astype(BF16)
        wb = w_ref[...].astype(BF16)
        dyb_v = dyb_ref[...]
        pw = jnp.dot(p, wb, preferred_element_type=F32)
        dps_ref[...] = jnp.broadcast_to(jnp.sum(dyb_v * pw, axis=0, keepdims=True), (8, 256))
        dys = (dyb_v * ps_ref[...]).astype(BF16)
        dp = lax.dot_general(dys, wb, (((1,), (1,)), ((), ())), preferred_element_type=F32)
        dw_ref[...] = lax.dot_general(p, dys, (((0,), (0,)), ((), ())), preferred_element_type=F32)
        dz_ref[:, 0:256] = (_pool_windows(dp / count, _shift_up) - dp).astype(BF16)

        ch, cb, cc = ch_ref[...], cb_ref[...], cc_ref[...]
        uc = cc * ch
        cw = cw_ref[...]
        u2, u1 = _shift_down(uc, 2), _shift_down(uc, 1)
        conv = cw[0:1, :] * u2 + cw[1:2, :] * u1 + cw[2:3, :] * uc
        dyc_v = dyc_ref[...]
        dconv = dyc_v * cb
        du = cw[0:1, :] * _shift_up(dconv, 2) + cw[1:2, :] * _shift_up(dconv, 1) + cw[2:3, :] * dconv
        dz_ref[:, 256:512] = (du * cc).astype(BF16)
        dz_ref[:, 512:768] = (dyc_v * conv).astype(BF16)
        dz_ref[:, 768:1024] = (du * ch).astype(BF16)
        dcw_ref[...] = jnp.zeros_like(dcw_ref)
        dcw_ref[0:1, :] = jnp.sum(dconv * u2, axis=0, keepdims=True)
        dcw_ref[1:2, :] = jnp.sum(dconv * u1, axis=0, keepdims=True)
        dcw_ref[2:3, :] = jnp.sum(dconv * uc, axis=0, keepdims=True)

    blk = lambda r, c: pl.BlockSpec((r, c), lambda i: (0, 0))
    return pl.pallas_call(
        body, name="poolconv_bwd", grid=(1,),
        in_specs=[blk(s, 256), blk(s, 256)] + _pc_specs(s),
        out_specs=[blk(s, 1024), blk(256, 256), blk(8, 256), blk(8, 256)],
        out_shape=[jax.ShapeDtypeStruct((s, 1024), BF16), jax.ShapeDtypeStruct((256, 256), F32),
                   jax.ShapeDtypeStruct((8, 256), F32), jax.ShapeDtypeStruct((8, 256), F32)],
        compiler_params=_params("arbitrary"),
    )(dyb, dyc, z, z, z, z, wbd, pscale, convw)


_NT = (((1,), (1,)), ((), ()))
_TN = (((0,), (0,)), ((), ()))


ATT_Q, ATT_K = 256, 256
ATT_HEADS_BWD = 4
ATT_HEADS = 8


def _att_logits(q, k, fr, q0, k0, masked):
    logits = lax.dot_general(q, k, _NT, preferred_element_type=F32) - fr
    if not masked:
        return logits
    row = q0 + lax.broadcasted_iota(jnp.int32, logits.shape, 0)
    col = k0 + lax.broadcasted_iota(jnp.int32, logits.shape, 1)
    return jnp.where(row >= col, logits, NEG_INF)


def _causal_sweep(step, qi, init):
    n_full = (qi * ATT_Q) // ATT_K
    carry = lax.fori_loop(0, n_full, lambda j, carry: step(j, carry, False), init)
    return step(n_full, carry, True)


HEAD_PAIRS = HEADS // 2


def _lane_pick(v, lane, idx):
    return jnp.sum(jnp.where(lane == idx, v, 0.0), axis=-1, keepdims=True)


def _lane_put(lane, idx, col):
    return jnp.where(lane == idx, col, 0.0)


def _split_heads(v, low):
    zero = jnp.zeros_like(v)
    return jnp.where(low, v, zero), jnp.where(low, zero, v)


def _attn_fwd(qkv, fr):
    s = qkv.shape[0]
    nk = s // ATT_K
    width = ATT_HEADS * HEAD_DIM
    groups = HEADS // ATT_HEADS

    def body(q_ref, k_ref, v_ref, fr_ref, o_ref, lse_ref):
        qi, grp = pl.program_id(0), pl.program_id(1)
        lane = lax.broadcasted_iota(jnp.int32, (ATT_Q, 128), 1)
        low = lane < HEAD_DIM
        qs = []
        for pr in range(ATT_HEADS // 2):
            qs += _split_heads(q_ref[:, 128 * pr:128 * (pr + 1)] * (HEAD_DIM ** -0.5), low)

        def step(j, carry, masked):
            k0 = pl.multiple_of(j * ATT_K, ATT_K)
            out = []
            for h in range(ATT_HEADS):
                cols = slice(128 * (h // 2), 128 * (h // 2 + 1))
                m, l, acc = carry[h]
                logits = _att_logits(qs[h], k_ref[pl.ds(k0, ATT_K), cols], fr_ref[h, pl.ds(j, 1), :], qi * ATT_Q, k0, masked)
                m_new = jnp.maximum(m, jnp.max(logits, axis=-1, keepdims=True))
                p = jnp.exp(logits - m_new)
                alpha = jnp.exp(m - m_new)
                l = alpha * l + jnp.sum(p, axis=-1, keepdims=True)
                acc = alpha * acc + jnp.dot(p.astype(BF16), v_ref[pl.ds(k0, ATT_K), cols], preferred_element_type=F32)
                out.append((m_new, l, acc))
            return tuple(out)

        one = (jnp.full((ATT_Q, 1), NEG_INF, F32), jnp.zeros((ATT_Q, 1), F32), jnp.zeros((ATT_Q, 128), F32))
        done = _causal_sweep(step, qi, (one,) * ATT_HEADS)

        @pl.when(grp == 0)
        def _():
            lse_ref[...] = jnp.zeros_like(lse_ref)

        lse = jnp.zeros((ATT_Q, 128), F32)
        for pr in range(ATT_HEADS // 2):
            (m0, l0, acc0), (m1, l1, acc1) = done[2 * pr], done[2 * pr + 1]
            o_ref[:, 128 * pr:128 * (pr + 1)] = jnp.where(low, acc0 / l0, acc1 / l1)
            head = ATT_HEADS * grp + 2 * pr
            lse = lse + _lane_put(lane, head, m0 + jnp.log(l0)) + _lane_put(lane, head + 1, m1 + jnp.log(l1))
        lse_ref[...] += lse

    return pl.pallas_call(
        body, name="attn_fwd", grid=(s // ATT_Q, groups),
        in_specs=[pl.BlockSpec((ATT_Q, width), lambda i, g: (i, g)),
                  pl.BlockSpec((s, width), lambda i, g: (0, groups + g)),
                  pl.BlockSpec((s, width), lambda i, g: (0, 2 * groups + g)),
                  pl.BlockSpec((ATT_HEADS, nk, ATT_K), lambda i, g: (g, 0, 0))],
        out_specs=[pl.BlockSpec((ATT_Q, width), lambda i, g: (i, g)), pl.BlockSpec((ATT_Q, 128), lambda i, g: (i, 0))],
        out_shape=[jax.ShapeDtypeStruct((s, A_WIDTH), F32), jax.ShapeDtypeStruct((s, 128), F32)],
        compiler_params=_params("parallel", "arbitrary"),
    )(qkv, qkv, qkv, fr)


def _attn_bwd(qkv, do, o, lse, fr):
    s = qkv.shape[0]
    nk = s // ATT_K
    scale = HEAD_DIM ** -0.5
    heads = ATT_HEADS_BWD
    width = heads * HEAD_DIM
    groups = HEADS // heads

    def body(q_ref, k_ref, v_ref, do_ref, o_ref, lse_ref, fr_ref, dq_ref, dk_ref, dv_ref, dfc_ref, dfr_ref, dk_acc, dv_acc):
        grp = pl.program_id(0)
        lane = lax.broadcasted_iota(jnp.int32, (ATT_Q, 128), 1)
        low = lane < HEAD_DIM
        low_k = lax.broadcasted_iota(jnp.int32, (ATT_K, 128), 1) < HEAD_DIM
        dk_acc[...] = jnp.zeros_like(dk_acc)
        dv_acc[...] = jnp.zeros_like(dv_acc)
        dfr_ref[...] = jnp.zeros_like(dfr_ref)

        @pl.when(grp == 0)
        def _():
            dfc_ref[...] = jnp.zeros_like(dfc_ref)

        def outer(i, carry):
            q0 = pl.multiple_of(i * ATT_Q, ATT_Q)
            rows = pl.ds(q0, ATT_Q)
            lsev = lse_ref[rows, :]
            q2s, dob2s, qs, dos, deltas, lses = [], [], [], [], [], []
            for pr in range(heads // 2):
                pcols = slice(128 * pr, 128 * (pr + 1))
                q2, do2 = q_ref[rows, pcols] * scale, do_ref[rows, pcols]
                prod = do2 * o_ref[rows, pcols]
                deltas += [jnp.sum(jnp.where(low, prod, 0.0), axis=-1, keepdims=True),
                           jnp.sum(jnp.where(low, 0.0, prod), axis=-1, keepdims=True)]
                dob2 = do2.astype(BF16)
                q2s.append(q2)
                dob2s.append(dob2)
                qs += _split_heads(q2, low)
                dos += _split_heads(dob2, low)
                lses += [_lane_pick(lsev, lane, heads * grp + 2 * pr), _lane_pick(lsev, lane, heads * grp + 2 * pr + 1)]

            def inner(j, carry, masked):
                k0 = pl.multiple_of(j * ATT_K, ATT_K)
                krows = pl.ds(k0, ATT_K)
                out, dk_parts, dv_parts = [], [], []
                for h in range(heads):
                    pcols = slice(128 * (h // 2), 128 * (h // 2 + 1))
                    dq, dfc = carry[h]
                    k2 = k_ref[krows, pcols]
                    p = jnp.exp(_att_logits(qs[h], k2, fr_ref[h, pl.ds(j, 1), :], q0, k0, masked) - lses[h])
                    dp = lax.dot_general(dos[h], v_ref[krows, pcols], _NT, preferred_element_type=F32)
                    ds = p * (dp - deltas[h])
                    dsb = ds.astype(BF16)
                    dk_parts.append(lax.dot_general(dsb, q2s[h // 2], _TN, preferred_element_type=F32))
                    dv_parts.append(lax.dot_general(p.astype(BF16), dob2s[h // 2], _TN, preferred_element_type=F32))
                    dfr_ref[h, pl.ds(j, 1), :] -= jnp.sum(ds, axis=0, keepdims=True)
                    out.append((dq + jnp.dot(dsb, k2, preferred_element_type=F32), dfc + jnp.sum(ds, axis=-1, keepdims=True)))
                for pr in range(heads // 2):
                    pcols = slice(128 * pr, 128 * (pr + 1))
                    dk_acc[krows, pcols] += jnp.where(low_k, dk_parts[2 * pr], dk_parts[2 * pr + 1])
                    dv_acc[krows, pcols] += jnp.where(low_k, dv_parts[2 * pr], dv_parts[2 * pr + 1])
                return tuple(out)

            one = (jnp.zeros((ATT_Q, 128), F32), jnp.zeros((ATT_Q, 1), F32))
            done = _causal_sweep(inner, i, (one,) * heads)
            dfc = jnp.zeros((ATT_Q, 128), F32)
            for pr in range(heads // 2):
                (dq0, dfc0), (dq1, dfc1) = done[2 * pr], done[2 * pr + 1]
                dq_ref[rows, 128 * pr:128 * (pr + 1)] = (jnp.where(low, dq0, dq1) * scale).astype(BF16)
                head = heads * grp + 2 * pr
                dfc = dfc + _lane_put(lane, head, dfc0) + _lane_put(lane, head + 1, dfc1)
            dfc_ref[rows, :] += dfc
            return carry

        lax.fori_loop(0, s // ATT_Q, outer, 0)
        dk_ref[...] = dk_acc[...].astype(BF16)
        dv_ref[...] = dv_acc[...].astype(BF16)

    part = lambda first: pl.BlockSpec((s, width), lambda g, first=first: (0, first + g))
    whole = pl.BlockSpec((s, 128), lambda g: (0, 0))
    rowv = pl.BlockSpec((heads, nk, ATT_K), lambda g: (g, 0, 0))
    return pl.pallas_call(
        body, name="attn_bwd", grid=(groups,),
        in_specs=[part(0), part(groups), part(2 * groups), part(0), part(0), whole, rowv],
        out_specs=[part(0), part(0), part(0), whole, rowv],
        out_shape=[jax.ShapeDtypeStruct((s, A_WIDTH), BF16)] * 3 + [jax.ShapeDtypeStruct((s, 128), F32), jax.ShapeDtypeStruct((HEADS, nk, ATT_K), F32)],
        scratch_shapes=[pltpu.VMEM((s, width), F32), pltpu.VMEM((s, width), F32)],
        compiler_params=_params("arbitrary"),
    )(qkv, qkv, qkv, do, o, lse, fr)


def _ada_fwd(c_all, w_ada, b_loc):
    depth, _, n = w_ada.shape
    tn = 512

    def body(c_ref, w_ref, b_ref, o_ref, sc_ref):
        cv = c_ref[...]
        sc = cv * jax.nn.sigmoid(cv)
        sc_ref[...] = sc
        o_ref[0] = jnp.dot(sc.astype(BF16), w_ref[0].astype(BF16), preferred_element_type=F32) + b_ref[0]

    return pl.pallas_call(
        body, name="ada_fwd", grid=(depth, n // tn),
        in_specs=[pl.BlockSpec((N_DEV, D), lambda l, j: (0, 0)), pl.BlockSpec((1, D, tn), lambda l, j: (l, 0, j)),
                  pl.BlockSpec((1, 1, tn), lambda l, j: (l, 0, j))],
        out_specs=[pl.BlockSpec((1, N_DEV, tn), lambda l, j: (l, 0, j)), pl.BlockSpec((N_DEV, D), lambda l, j: (0, 0))],
        out_shape=[jax.ShapeDtypeStruct((depth, N_DEV, n), F32), jax.ShapeDtypeStruct((N_DEV, D), F32)],
        compiler_params=_params("arbitrary", "arbitrary"),
    )(c_all, w_ada, b_loc)


def _sum_devices(gathered):
    n = gathered.shape[1]
    tn = _pick(n, (1408, 1024, 640, 512, 128))

    def body(g_ref, o_ref):
        acc = g_ref[0:8, :]
        for dev in range(1, N_DEV):
            acc = acc + g_ref[8 * dev:8 * dev + 8, :]
        o_ref[...] = acc

    return pl.pallas_call(
        body, name="sum_devices", grid=(n // tn,),
        in_specs=[pl.BlockSpec((8 * N_DEV, tn), lambda j: (0, j))], out_specs=pl.BlockSpec((8, tn), lambda j: (0, j)),
        out_shape=jax.ShapeDtypeStruct((8, n), F32), compiler_params=_params("parallel"),
    )(gathered)


def _place():
    x, y, c = lax.axis_index("x"), lax.axis_index("y"), lax.axis_index("c")
    chips = [(1 - x, y), (x, 1 - y), (1 - x, 1 - y)]
    return x, y, c, chips


def _allgather8(block, name):
    m_per, n = block.shape

    def body(x_ref, out_ref, send_sems, recv_sems, local_sem):
        x, y, c, chips = _place()
        me, sibling = (x, y, c), (x, y, 1 - c)

        def rows(px, py, pc):
            return out_ref.at[pl.ds((4 * px + 2 * py + pc) * m_per, m_per), :]

        def copy(k, blk, to, src=None):
            return pltpu.make_async_remote_copy(
                src_ref=rows(*blk) if src is None else src, dst_ref=rows(*blk),
                send_sem=send_sems.at[k], recv_sem=recv_sems.at[k], device_id=to, device_id_type=MESH)

        mine = pltpu.make_async_copy(x_ref, rows(*me), local_sem)
        mine.start()
        first = [copy(0, me, sibling, src=x_ref)]
        first += [copy(1 + j, me, (*chip, c), src=x_ref) for j, chip in enumerate(chips)]
        for cp in first:
            cp.start()
        passed = [copy(4 + j, (*chip, c), sibling) for j, chip in enumerate(chips)]
        for j, chip in enumerate(chips):
            copy(1 + j, (*chip, c), me).wait_recv()
            passed[j].start()
        copy(0, sibling, me).wait_recv()
        for j, chip in enumerate(chips):
            copy(4 + j, (*chip, 1 - c), me).wait_recv()
        for cp in first + passed:
            cp.wait_send()
        mine.wait()

    return pl.pallas_call(
        body, name=name, out_shape=jax.ShapeDtypeStruct((N_DEV * m_per, n), block.dtype),
        in_specs=[pl.BlockSpec(memory_space=pltpu.VMEM)], out_specs=pl.BlockSpec(memory_space=pltpu.VMEM),
        scratch_shapes=[pltpu.SemaphoreType.DMA((7,)), pltpu.SemaphoreType.DMA((7,)), pltpu.SemaphoreType.DMA],
        compiler_params=pltpu.CompilerParams(vmem_limit_bytes=V7X_VMEM_LIMIT),
    )(block)


_SEM = pl.BlockSpec(memory_space=pltpu.SEMAPHORE)
_DATAFLOW = pltpu.SideEffectType.DATAFLOW_SIDE_EFFECTING


def _plan_copies(plan, refs, send_sems, recv_sems):
    return [pltpu.make_async_remote_copy(src_ref=src, dst_ref=dst, send_sem=send_sems.at[i], recv_sem=recv_sems.at[i],
                                         device_id=to, device_id_type=MESH) for i, (src, dst, to) in enumerate(plan(refs))]


def _copies_start(bufs, plan, n_copies, after, name):
    nb = len(bufs)

    def body(*refs):
        for cp in _plan_copies(plan, refs[:nb], refs[nb + 1], refs[nb + 2]):
            cp.start()
        token = refs[-1]
        token[...] = jnp.zeros_like(token)

    sem = pltpu.SemaphoreType.DMA((n_copies,))
    outs = pl.pallas_call(
        body, name=name,
        out_shape=(sem, sem, *[pltpu.HBM(b.shape, b.dtype) for b in bufs], jax.ShapeDtypeStruct((8, 128), F32)),
        in_specs=[_HBM] * nb + [pl.BlockSpec(memory_space=pl.ANY)],
        out_specs=(_SEM, _SEM, *[_HBM] * nb, pl.BlockSpec(memory_space=pltpu.VMEM)),
        input_output_aliases={i: 2 + i for i in range(nb)},
        compiler_params=pltpu.CompilerParams(has_side_effects=_DATAFLOW),
    )(*[pltpu.with_memory_space_constraint(b, pltpu.HBM) for b in bufs], after)
    return outs[0], outs[1], list(outs[2:2 + nb]), outs[-1]


def _copies_wait(started, plan, after, name):
    send_sems, recv_sems, bufs, _ = started
    nb = len(bufs)

    def body(*refs):
        for cp in _plan_copies(plan, refs[:nb], refs[nb], refs[nb + 1]):
            cp.wait_send()
            cp.wait_recv()

    return list(pl.pallas_call(
        body, name=name, out_shape=tuple(pltpu.HBM(b.shape, b.dtype) for b in bufs),
        in_specs=[_HBM] * nb + [_SEM, _SEM, pl.BlockSpec(memory_space=pl.ANY)], out_specs=tuple([_HBM] * nb),
        input_output_aliases={i: i for i in range(nb)},
        compiler_params=pltpu.CompilerParams(has_side_effects=_DATAFLOW),
    )(*bufs, send_sems, recv_sems, after))


def _half_rows(ref, axis, c):
    half = ref.shape[axis] // 2
    return pl.ds(c * half, half)


def _plan_gather_ici(refs):
    n = len(refs) // 2
    x, y, c, chips = _place()
    out = []
    for a in range(n):
        rows = _half_rows(refs[a], 0, c)
        out += [(refs[a].at[rows], refs[n + a].at[2 * x + y, rows], (*chip, c)) for chip in chips]
        out.append((refs[a], refs[n + a].at[2 * x + y], (x, y, 1 - c)))
    return out


def _plan_gather_d2d(refs):
    x, y, c, chips = _place()
    out = []
    for ref in refs:
        rows = _half_rows(ref, 1, c)
        for px, py in chips:
            landed = ref.at[2 * px + py, rows]
            out.append((landed, landed, (x, y, 1 - c)))
    return out


def _plan_rs_sibling(refs):
    n = len(refs) // 2
    x, y, c, _ = _place()
    return [(refs[a].at[pl.ds(0, N_CHIPS), _half_rows(refs[a], 1, 1 - c)], refs[n + a], (x, y, 1 - c)) for a in range(n)]


def _plan_rs_chips(refs):
    n = len(refs) // 2
    x, y, c, chips = _place()
    return [(refs[a].at[2 * px + py], refs[n + a].at[k], (px, py, c)) for a in range(n) for k, (px, py) in enumerate(chips)]


def _plan_rs_share(refs):
    x, y, c, _ = _place()
    return [(ref.at[_half_rows(ref, 0, c)], ref.at[_half_rows(ref, 0, c)], (x, y, 1 - c)) for ref in refs]


def _chip_sum(g, other, sel, name):
    _, half, cdim = other.shape
    tr = _pick(half, (512, 256, 128, 64))
    per = half // tr

    def body(sel_ref, g_ref, t_ref, wire_ref, own_ref):
        total = g_ref[0] + t_ref[0]
        wire_ref[0] = total.astype(BF16)

        @pl.when(pl.program_id(1) == sel_ref[1])
        def _():
            own_ref[...] = total

    blk = pl.BlockSpec((1, tr, cdim), lambda i, p, sel_ref: (p, i, 0))
    return pl.pallas_call(
        body, name=name,
        grid_spec=pltpu.PrefetchScalarGridSpec(
            num_scalar_prefetch=1, grid=(per, N_CHIPS),
            in_specs=[pl.BlockSpec((1, tr, cdim), lambda i, p, sel_ref: (p, sel_ref[0] * per + i, 0)), blk],
            out_specs=[blk, pl.BlockSpec((tr, cdim), lambda i, p, sel_ref: (i, 0))]),
        out_shape=[jax.ShapeDtypeStruct(other.shape, BF16), jax.ShapeDtypeStruct((half, cdim), F32)],
        compiler_params=_params("parallel", "arbitrary"),
    )(sel, g, other)


def _final_sum(own, recv, sel, name):
    half, cdim = own.shape
    tr = _pick(half, (512, 256, 128, 64))
    per = half // tr

    def body(sel_ref, own_ref, r0_ref, r1_ref, r2_ref, o_ref):
        o_ref[...] = ((own_ref[...] + r0_ref[0].astype(F32)) + r1_ref[0].astype(F32)) + r2_ref[0].astype(F32)

    part = lambda k: pl.BlockSpec((1, tr, cdim), lambda i, sel_ref, k=k: (k, i, 0))
    return pl.pallas_call(
        body, name=name,
        grid_spec=pltpu.PrefetchScalarGridSpec(
            num_scalar_prefetch=1, grid=(per,),
            in_specs=[pl.BlockSpec((tr, cdim), lambda i, sel_ref: (i, 0)), part(0), part(1), part(2)],
            out_specs=pl.BlockSpec((tr, cdim), lambda i, sel_ref: (sel_ref[0] * per + i, 0))),
        out_shape=jax.ShapeDtypeStruct((2 * half, cdim), F32), compiler_params=_params("parallel"),
    )(sel, own, recv, recv, recv)


def _row(v):
    return v.reshape(1, -1)


def _tie(v, token):
    return v if token is None else v + token[0:1, 0:1]


def _no_hook(point, after, ready=None):
    return None


def _layer_fwd(x, w, mod, hook=_no_hook):
    s = x.shape[0]
    h = _modnorm_fwd(x, _row(w["g_mix_pre"]), mod[0:1], mod[1:2], "mix_pre_fwd")
    z = _mm(h, w["w_all"], name="mm_in")
    qkv = z[:, Z_QKV:Z_PC].astype(BF16)
    fl = z[:, Z_FL:Z_COLS]
    cum = _cumf_fwd(fl, w["b_f_pad"])
    fr = cum[:, :HEADS].T.reshape(HEADS, s // ATT_K, ATT_K)
    br_a, lse = _attn_fwd(qkv, fr)
    br_b, br_c = _poolconv_fwd(z, w["w_pool_bd"], _tie(_row(w["pool_scale"]), hook("attn", lse)), w["conv_w"])
    hook("pool", br_b)
    wbr = w["w_branch"]
    pa = _mm(br_a, wbr[:A_WIDTH], name="mm_br_a")
    pb = _mm(br_b, wbr[A_WIDTH:A_WIDTH + POOL_WIDTH], name="mm_br_b")
    pc = _mm(br_c, wbr[A_WIDTH + POOL_WIDTH:], name="mm_br_c")
    merged = _merge_fwd(z, pa, pb, pc)
    y = _mm(merged, w["w_out"], name="mm_out")
    x1 = _post_fwd(x, y, _row(w["g_mix_post"]), mod[2:3], "mix_post_fwd")
    h2 = _modnorm_fwd(x1, _row(w["g_ff_pre"]), mod[3:4], mod[4:5], "ff_pre_fwd")
    a, r = _mm(h2, w["w_ff1"], b_split=N_CHIPS, epilogue=_relu2_fwd, out_dtype=(F32, BF16), name="mm_ff1")
    y2 = _mm(r, w["w_ff2"], name="mm_ff2")
    x2 = _post_fwd(x1, y2, _tie(_row(w["g_ff_post"]), hook("ff_post", y2)), mod[5:6], "ff_post_fwd")
    hook("end", x2)
    saved = dict(x=x, h=h, z=z, qkv=qkv, fl=fl, fr=fr, lse=lse, br_a=br_a, br_b=br_b, br_c=br_c, pa=pa, pb=pb, pc=pc,
                 merged=merged, y=y, x1=x1, h2=h2, a=a, r=r, y2=y2)
    return x2, saved


def _layer_bwd(dx2, sv, w, mod, hook=_no_hook):
    s = dx2.shape[0]
    dy2, sum_ff_post = _post_bwd(dx2, sv["y2"], _row(w["g_ff_post"]), mod[5:6], "ff_post_bwd")
    (da,) = _mm(dy2, w["w_ff2"], tb=True, epilogue=_relu2_bwd, extras=(sv["a"],), out_dtype=(BF16,), name="mm_ff2_dx")
    d_w_ff2 = _mm(sv["r"], dy2, ta=True, name="mm_ff2_dw")
    dh2 = _mm(da, w["w_ff1"], tb=True, b_split=N_CHIPS, name="mm_ff1_dx")
    d_w_ff1 = _mm(sv["h2"], da, ta=True, out_split=N_CHIPS, name="mm_ff1_dw")
    dx1, sum_ff_pre = _modnorm_bwd(dh2, sv["x1"], dx2, _tie(_row(w["g_ff_pre"]), hook("ff_pre", dh2, dict(w_ff1=d_w_ff1, w_ff2=d_w_ff2))), mod[4:5], "ff_pre_bwd")

    dy, sum_mix_post = _post_bwd(dx1, sv["y"], _row(w["g_mix_post"]), mod[2:3], "mix_post_bwd")
    dmerged = _mm(dy, w["w_out"], tb=True, name="mm_out_dx")
    d_w_out = _mm(sv["merged"], dy, ta=True, name="mm_out_dw")
    dz, dpa, dpb, dpc = _merge_bwd(dmerged, sv["z"], sv["pa"], sv["pb"], sv["pc"])
    wbr = w["w_branch"]
    dbr_a = _mm(dpa, wbr[:A_WIDTH], tb=True, name="mm_br_a_dx")
    dbr_b = _mm(dpb, wbr[A_WIDTH:A_WIDTH + POOL_WIDTH], tb=True, name="mm_br_b_dx")
    dbr_c = _mm(dpc, wbr[A_WIDTH + POOL_WIDTH:], tb=True, name="mm_br_c_dx")
    d_w_branch = jnp.concatenate([_mm(sv["br_a"], dpa, ta=True, name="mm_br_a_dw"), _mm(sv["br_b"], dpb, ta=True, name="mm_br_b_dw"),
                                  _mm(sv["br_c"], dpc, ta=True, name="mm_br_c_dw")], axis=0)

    dq, dk, dv, dfc, dfr = _attn_bwd(sv["qkv"], dbr_a, sv["br_a"], sv["lse"], sv["fr"])
    dcum = dfc + jnp.pad(dfr.reshape(HEADS, s).T, ((0, 0), (0, 128 - HEADS)))
    dfl, sum_bf = _cumf_bwd(dcum, sv["fl"], _tie(w["b_f_pad"], hook("cumf", dfc)))
    dpc_z, d_wbd, sum_ps, sum_cw = _poolconv_bwd(dbr_b, dbr_c, sv["z"], w["w_pool_bd"], _row(w["pool_scale"]), w["conv_w"])
    for at, part in ((Z_QKV, dq), (Z_QKV + A_WIDTH, dk), (Z_QKV + 2 * A_WIDTH, dv), (Z_PC, dpc_z), (Z_FL, dfl)):
        dz = lax.dynamic_update_slice(dz, part, (0, at))
    dh = _mm(dz, w["w_all"], tb=True, name="mm_in_dx")
    d_w_all = _mm(sv["h"], dz, ta=True, name="mm_in_dw")
    hook("mix_pre", dh)
    dx, sum_mix_pre = _modnorm_bwd(dh, sv["x"], dx1, _row(w["g_mix_pre"]), mod[1:2], "mix_pre_bwd")

    dmod = jnp.stack([sum_mix_pre[0], sum_mix_pre[1], sum_mix_post[0], sum_ff_pre[0], sum_ff_pre[1], sum_ff_post[0]])
    d_w_in = _w_in_shards(d_w_all)
    d_w_pool = jnp.stack([d_wbd[64 * g:64 * g + 64, 64 * g:64 * g + 64] for g in range(4)])
    big = dict(w_in=d_w_in, w_branch=d_w_branch, w_out=d_w_out, w_ff1=d_w_ff1, w_ff2=d_w_ff2)
    small = dict(g_mix_pre=sum_mix_pre[2], g_mix_post=sum_mix_post[1], g_ff_pre=sum_ff_pre[2], g_ff_post=sum_ff_post[1],
                 b_f=sum_bf[0, :HEADS], w_pool=d_w_pool, pool_scale=sum_ps[0], conv_w=sum_cw[0:3])
    return dx, dmod, big, small


_QKV_END, _FL_END, _PC_END = 3 * A_WIDTH, 3 * A_WIDTH + HEADS, 3 * A_WIDTH + HEADS + POOL_WIDTH + 3 * CONV_WIDTH
_W_IN_GROUPS = ((_PC_END, IN_COLS, Z_GL), (0, _QKV_END, Z_QKV), (_FL_END, _PC_END, Z_PC), (_QKV_END, _FL_END, Z_FL))
_SHARD_COLS = IN_COLS // N_CHIPS


def _w_all_from_shards(blocks):
    pieces = []
    for lo, hi, _ in _W_IN_GROUPS:
        for p in range(N_CHIPS):
            a, b = max(lo, p * _SHARD_COLS), min(hi, (p + 1) * _SHARD_COLS)
            if a < b:
                pieces.append(blocks[p][:, a - p * _SHARD_COLS:b - p * _SHARD_COLS])
    pieces.append(jnp.zeros((D, Z_COLS - IN_COLS), blocks.dtype))
    return jnp.concatenate(pieces, axis=1)


def _w_in_shards(d_w_all):
    blocks = []
    for p in range(N_CHIPS):
        pieces = []
        for lo, hi, at in sorted(_W_IN_GROUPS):
            a, b = max(lo, p * _SHARD_COLS), min(hi, (p + 1) * _SHARD_COLS)
            if a < b:
                pieces.append(d_w_all[:, at + a - lo:at + b - lo])
        blocks.append(jnp.concatenate(pieces, axis=1))
    return jnp.stack(blocks)


def _full_layer_weights(w_in_blocks, w_branch, w_out, w_ff1, w_ff2, g_mix_pre, g_mix_post, g_ff_pre, g_ff_post, b_f, w_pool, pool_scale, conv_w):
    w_all = _w_all_from_shards(w_in_blocks)
    wbd = jnp.zeros((POOL_WIDTH, POOL_WIDTH), F32)
    for g in range(4):
        wbd = wbd.at[64 * g:64 * g + 64, 64 * g:64 * g + 64].set(w_pool[g])
    return dict(w_all=w_all, w_branch=w_branch, w_out=w_out, w_ff1=w_ff1, w_ff2=w_ff2, g_mix_pre=g_mix_pre, g_mix_post=g_mix_post,
                g_ff_pre=g_ff_pre, g_ff_post=g_ff_post, b_f_pad=jnp.pad(b_f, (0, 128 - HEADS)).reshape(1, 128), w_pool_bd=wbd,
                pool_scale=pool_scale, conv_w=conv_w)


class _NoComm:
    def layer_weights(self, l):
        raise NotImplementedError

    def fwd_hook(self, l):
        return _no_hook

    def bwd_hook(self, l):
        return _no_hook

    def grads_ready(self, l, big):
        return None


class _Layers(_NoComm):
    def __init__(self, layers):
        self.layers = layers

    def layer_weights(self, l):
        return self.layers[l]


def _local_step(x, target, mods, comm):
    saved, weights = [], []
    act = x
    for l in range(DEPTH):
        weights.append(comm.layer_weights(l))
        act, sv = _layer_fwd(act, weights[l], mods[l], comm.fwd_hook(l))
        saved.append(sv)
    dact, sq = _loss_head(act, target)
    loss = sq[0, 0] * (0.5 / D)
    dmods, bigs, smalls = [None] * DEPTH, [None] * DEPTH, [None] * DEPTH
    token = None
    for l in reversed(range(DEPTH)):
        dact, dmods[l], bigs[l], smalls[l] = _layer_bwd(dact, saved[l], weights[l], _tie(mods[l], token), comm.bwd_hook(l))
        token = comm.grads_ready(l, bigs[l])
    return loss, dact, jnp.stack(dmods), bigs, smalls


_BIG = ("w_in", "w_branch", "w_out", "w_ff1", "w_ff2")
N_BIG = len(_BIG)


class _GatherJob:
    def __init__(self, tag, shards, after):
        self.tag, self.n = tag, len(shards)
        lands = [lax.empty((N_CHIPS,) + s.shape, s.dtype) for s in shards]
        self.state = _copies_start(list(shards) + lands, _plan_gather_ici, 4 * self.n, after, "gather_ici_start_" + tag)
        self.token = self.state[3]

    def pass_on(self, after):
        bufs = _copies_wait(self.state, _plan_gather_ici, after, "gather_ici_wait_" + self.tag)
        self.state = _copies_start(bufs[self.n:], _plan_gather_d2d, 3 * self.n, bufs[0], "gather_d2d_start_" + self.tag)
        self.token = self.state[3]
        return self.token

    def done(self, after):
        return _copies_wait(self.state, _plan_gather_d2d, after, "gather_d2d_wait_" + self.tag)


class _ReduceJob:
    def __init__(self, tag, names, grads, sel, after):
        self.tag, self.names, self.n, self.sel = tag, names, len(names), sel
        lands = [lax.empty((N_CHIPS, g.shape[1] // 2, g.shape[2]), F32) for g in grads]
        self.state = _copies_start(list(grads) + lands, _plan_rs_sibling, self.n, after, "rs_sibling_start_" + tag)
        self.token = self.state[3]

    def chip_sums(self, after):
        bufs = _copies_wait(self.state, _plan_rs_sibling, after, "rs_sibling_wait_" + self.tag)
        wires, self.owns = zip(*[_chip_sum(bufs[i], bufs[self.n + i], self.sel, "rs_chip_sum_" + name) for i, name in enumerate(self.names)])
        lands = [lax.empty((3,) + w.shape[1:], BF16) for w in wires]
        self.state = _copies_start(list(wires) + lands, _plan_rs_chips, 3 * self.n, self.owns[0], "rs_chips_start_" + self.tag)
        self.token = self.state[3]
        return self.token

    def final_sums(self, after):
        bufs = _copies_wait(self.state, _plan_rs_chips, after, "rs_chips_wait_" + self.tag)
        sums = [_final_sum(self.owns[i], bufs[self.n + i], self.sel, "rs_final_" + name) for i, name in enumerate(self.names)]
        self.state = _copies_start(sums, _plan_rs_share, self.n, sums[0], "rs_share_start_" + self.tag)
        self.token = self.state[3]
        return self.token

    def done(self, after):
        return dict(zip(self.names, _copies_wait(self.state, _plan_rs_share, after, "rs_share_wait_" + self.tag)))


def _chip_blocks(g):
    return g if g.ndim == 3 else g.reshape(N_CHIPS, -1, g.shape[1])


class _StepComm(_NoComm):
    def __init__(self, shards, sel, after):
        self.sel = sel
        self.small, self.grads, self.jobs = None, [dict() for _ in range(DEPTH)], {}
        self.jobs["in0"] = _GatherJob("in0", shards[0][:1], after)
        self.jobs["rest0"] = _GatherJob("rest0", shards[0][1:], self.jobs["in0"].token)
        self.jobs["all1"] = _GatherJob("all1", shards[1], self.jobs["rest0"].token)

    def layer_weights(self, l):
        if l == 0:
            job = self.jobs["in0"]
            (g_in,) = job.done(job.pass_on(self.jobs["all1"].token))
            self.weights0 = _full_layer_weights(g_in, None, None, None, None, *self.small[0])
            return self.weights0
        g_in, g_br, g_out, g_f1, g_f2 = self.landed1
        return _full_layer_weights(g_in, g_br.reshape(D, D), g_out.reshape(D, D), g_f1, g_f2.reshape(D_FF, D), *self.small[1])

    def fwd_hook(self, l):
        if l != 0:
            return _no_hook

        def hook(point, after, ready=None):
            if point == "attn":
                return self.jobs["rest0"].pass_on(after)
            if point == "ff_post":
                return self.jobs["all1"].pass_on(after)
            if point == "pool":
                g_br, g_out, g_f1, g_f2 = self.jobs["rest0"].done(after)
                self.weights0.update(w_branch=g_br.reshape(D, D), w_out=g_out.reshape(D, D), w_ff1=g_f1, w_ff2=g_f2.reshape(D_FF, D))
            if point == "end":
                self.landed1 = self.jobs["all1"].done(after)
            return None
        return hook

    def bwd_hook(self, l):
        if l != 0:
            return _no_hook

        def hook(point, after, ready=None):
            jobs = self.jobs
            if point == "ff_pre":
                token = jobs["rs1"].chip_sums(after)
                jobs["rs0_ff"] = _ReduceJob("0_ff", ("w_ff1", "w_ff2"), [_chip_blocks(ready[n]) for n in ("w_ff1", "w_ff2")], self.sel, token)
                return jobs["rs0_ff"].token
            if point == "cumf":
                return jobs["rs0_ff"].chip_sums(jobs["rs1"].final_sums(after))
            self.grads[1] = jobs["rs1"].done(after)
            return None
        return hook

    def grads_ready(self, l, big):
        if l == 1:
            self.jobs["rs1"] = _ReduceJob("1", _BIG, [_chip_blocks(big[n]) for n in _BIG], self.sel, self.sel)
            return self.jobs["rs1"].token
        names = ("w_in", "w_branch", "w_out")
        self.jobs["rs0_mix"] = _ReduceJob("0_mix", names, [_chip_blocks(big[n]) for n in names], self.sel, self.sel)
        return self.jobs["rs0_mix"].token

    def finish_sums(self, after):
        jobs = self.jobs
        token = jobs["rs0_mix"].chip_sums(after)
        return jobs["rs0_ff"].final_sums(token)

    def finish(self, after):
        jobs = self.jobs
        self.grads[0] = jobs["rs0_ff"].done(after)
        self.grads[0].update(jobs["rs0_mix"].done(jobs["rs0_mix"].final_sums(after)))


_SMALL = ("g_mix_pre", "g_mix_post", "g_ff_pre", "g_ff_post", "b_f", "w_pool", "pool_scale", "conv_w")


def _w_in_view(t):
    return t.reshape(DEPTH, D // 128, 128, _SHARD_COLS).transpose(3, 1, 0, 2).reshape(_SHARD_COLS * (D // 128) * DEPTH, 128)


def _w_in_unview(t):
    return t.reshape(_SHARD_COLS, D // 128, DEPTH, 128).transpose(2, 1, 3, 0).reshape(DEPTH, D, _SHARD_COLS)


def _pack(parts, rows=8):
    flat = jnp.concatenate([p.reshape(-1) for p in parts])
    width = -(-flat.shape[0] // (rows * 128)) * 128
    return jnp.pad(flat, (0, rows * width - flat.shape[0])).reshape(rows, width)


def _unpack(packed, like):
    flat = packed.reshape(-1)
    out, at = [], 0
    for ref in like:
        out.append(flat[at:at + ref.size].reshape(ref.shape))
        at += ref.size
    return out


def kernel(x, c, w_ada, b_ada, g_mix_pre, g_mix_post, g_ff_pre, g_ff_post, w_in, b_f, w_pool, pool_scale, conv_w, w_branch, w_out, w_ff1, w_ff2, loss_target, m_w_ada, m_b_ada, m_g_mix_pre, m_g_mix_post, m_g_ff_pre, m_g_ff_post, m_w_in, m_b_f, m_w_pool, m_pool_scale, m_conv_w, m_w_branch, m_w_out, m_w_ff1, m_w_ff2, v_w_ada, v_b_ada, v_g_mix_pre, v_g_mix_post, v_g_ff_pre, v_g_ff_post, v_w_in, v_b_f, v_w_pool, v_pool_scale, v_conv_w, v_w_branch, v_w_out, v_w_ff1, v_w_ff2):
    xi, yi, ci = lax.axis_index("x"), lax.axis_index("y"), lax.axis_index("c")
    chip = 2 * xi + yi
    dev = 2 * chip + ci
    n_ada = w_ada.shape[2]

    first = jnp.zeros((8, D + 384), F32).at[0, :D].set(c[0]).at[0, D:].set(conv_w.reshape(-1))
    got = _allgather8(first, "gather_cond").reshape(N_DEV, 8, D + 384)[:, 0]
    c_all = got[:, :D]
    conv_full = got[0::2, D:].reshape(N_CHIPS, DEPTH, 3, CONV_WIDTH // N_CHIPS).transpose(1, 2, 0, 3).reshape(DEPTH, 3, CONV_WIDTH)

    b_loc = lax.dynamic_slice_in_dim(b_ada, chip * n_ada, n_ada, axis=1).reshape(DEPTH, 1, n_ada)
    mod_cols, silu_c = _ada_fwd(c_all, w_ada, b_loc)
    got = _allgather8(mod_cols.reshape(DEPTH * N_DEV, n_ada), "gather_mod").reshape(N_DEV, DEPTH, N_DEV, n_ada)[0::2]
    mod_all = got.transpose(1, 2, 0, 3).reshape(DEPTH, N_DEV, 6, D)
    mods = lax.dynamic_index_in_dim(mod_all, dev, axis=1, keepdims=False)

    comm = _StepComm([[w[l].astype(BF16) for w in (w_in, w_branch, w_out, w_ff1, w_ff2)] for l in range(DEPTH)],
                     jnp.stack([ci, chip]).astype(jnp.int32), mods)
    comm.small = [(g_mix_pre[l], g_mix_post[l], g_ff_pre[l], g_ff_post[l], b_f[l], w_pool[l], pool_scale[l], conv_full[l]) for l in range(DEPTH)]
    loss_part, grad_x, dmods, bigs, smalls = _local_step(x[0], loss_target[0], mods, comm)

    small_parts = [smalls[l][name] for name in _SMALL for l in range(DEPTH)] + [loss_part.reshape(1)]
    packed = _tie(_pack([dmods] + small_parts), comm.jobs["rs0_mix"].token)
    gathered = _allgather8(packed, "gather_small")
    dmod_all = gathered.reshape(N_DEV, -1)[:, :dmods.size].reshape(N_DEV, DEPTH, 6 * D)
    summed = _unpack(_sum_devices(gathered), [dmods] + small_parts)
    grad_b_ada = summed[0].reshape(DEPTH, 6 * D)
    loss = summed[-1][0]
    small_grads = {name: jnp.stack(summed[1 + 2 * i:3 + 2 * i]) for i, name in enumerate(_SMALL)}
    small_grads["conv_w"] = lax.dynamic_slice_in_dim(small_grads["conv_w"], chip * (CONV_WIDTH // N_CHIPS), CONV_WIDTH // N_CHIPS, axis=2)

    dmod_loc = lax.dynamic_slice_in_dim(dmod_all.transpose(1, 0, 2), chip * n_ada, n_ada, axis=2)
    tail_token = comm.finish_sums(grad_b_ada)
    silu_pad = _tie(jnp.pad(silu_c, ((0, 128 - N_DEV), (0, 0))), tail_token)
    dmod_pad = jnp.pad(dmod_loc.transpose(1, 0, 2).reshape(N_DEV, DEPTH * n_ada), ((0, 128 - N_DEV), (0, 0)))
    grad_w_ada = _mm(silu_pad, dmod_pad, ta=True, out_split=DEPTH, name="mm_ada_dw")

    grads = dict(w_ada=grad_w_ada, b_ada=grad_b_ada, **small_grads)
    weights = dict(w_ada=w_ada, b_ada=b_ada, g_mix_pre=g_mix_pre, g_mix_post=g_mix_post, g_ff_pre=g_ff_pre, g_ff_post=g_ff_post, w_in=w_in,
                   b_f=b_f, w_pool=w_pool, pool_scale=pool_scale, conv_w=conv_w, w_branch=w_branch, w_out=w_out, w_ff1=w_ff1, w_ff2=w_ff2)
    m_in = dict(w_ada=m_w_ada, b_ada=m_b_ada, g_mix_pre=m_g_mix_pre, g_mix_post=m_g_mix_post, g_ff_pre=m_g_ff_pre, g_ff_post=m_g_ff_post,
                w_in=m_w_in, b_f=m_b_f, w_pool=m_w_pool, pool_scale=m_pool_scale, conv_w=m_conv_w, w_branch=m_w_branch, w_out=m_w_out,
                w_ff1=m_w_ff1, w_ff2=m_w_ff2)
    v_in = dict(w_ada=v_w_ada, b_ada=v_b_ada, g_mix_pre=v_g_mix_pre, g_mix_post=v_g_mix_post, g_ff_pre=v_g_ff_pre, g_ff_post=v_g_ff_post,
                w_in=v_w_in, b_f=v_b_f, w_pool=v_w_pool, pool_scale=v_pool_scale, conv_w=v_conv_w, w_branch=v_w_branch, w_out=v_w_out,
                w_ff1=v_w_ff1, w_ff2=v_w_ff2)
    order = ("w_ada", "b_ada", "g_mix_pre", "g_mix_post", "g_ff_pre", "g_ff_post", "w_in", "b_f", "w_pool", "pool_scale", "conv_w",
             "w_branch", "w_out", "w_ff1", "w_ff2")
    delta, new_m, new_v = {}, {}, {}
    tiny = ("b_ada",) + _SMALL
    packed_w, packed_g, packed_m, packed_v = [_pack([src[name] for name in tiny]) for src in (weights, grads, m_in, v_in)]
    res = _adamw(packed_w, _tie(packed_g, tail_token), packed_m, packed_v, "adamw_small")
    for out, packed_res in zip((delta, new_m, new_v), res):
        for name, val in zip(tiny, _unpack(packed_res, [weights[name] for name in tiny])):
            out[name] = val
    delta["w_ada"], new_m["w_ada"], new_v["w_ada"] = _adamw(w_ada, grad_w_ada, m_w_ada, v_w_ada, "adamw_w_ada")
    comm.finish(delta["w_ada"][0, :8, :128] + delta["b_ada"][0, :128])
    for name in _BIG:
        grads[name] = jnp.stack([comm.grads[l][name] for l in range(DEPTH)])
        if name == "w_in":
            g_view = lax.optimization_barrier(_w_in_view(grads[name]))
            res = _adamw(_w_in_view(w_in), g_view, _w_in_view(m_w_in), _w_in_view(v_w_in), "adamw_w_in")
            grads[name], delta[name], new_m[name], new_v[name] = [_w_in_unview(t) for t in (g_view, *res)]
        else:
            delta[name], new_m[name], new_v[name] = _adamw(weights[name], grads[name], m_in[name], v_in[name], "adamw_" + name)

    return (loss, grad_x[None], *[grads[n] for n in order], *[delta[n] for n in order], *[new_m[n] for n in order],
            *[new_v[n] for n in order])
```

```python
import functools

import jax
import jax.numpy as jnp
from jax import lax
from jax.experimental import pallas as pl
from jax.experimental.pallas import tpu as pltpu

F32 = jnp.float32
BF16 = jnp.bfloat16
MESH = pl.DeviceIdType.MESH

D = 1024
DEPTH = 2
HEADS = 8
HEAD_DIM = 64
A_WIDTH = 512
POOL_WIDTH = 256
CONV_WIDTH = 256
D_FF = 4096
IN_COLS = 5640
Z_GL, Z_QKV, Z_PC, Z_FL, Z_COLS = 0, 3072, 4608, 5632, 5760
RMS_EPS = 1e-6
NEG_INF = -1e30
ROW_TILE = 256
N_CHIPS = 4
N_DEV = 8
V7X_VMEM_LIMIT = 48 * 1024 * 1024

ADAM_LR = 0.001
ADAM_B1 = 0.9
ADAM_B2 = 0.999
ADAM_EPS = 1e-08
ADAM_WD = 0.01
ADAM_STEP = 10

_HBM = pl.BlockSpec(memory_space=pltpu.HBM)


def _params(*sem):
    return pltpu.CompilerParams(dimension_semantics=sem, vmem_limit_bytes=V7X_VMEM_LIMIT)


def _pick(dim, cands):
    for cand in cands:
        if dim % cand == 0:
            return cand
    return dim


def _mm(a, b, *, ta=False, tb=False, b_rows=None, b_split=1, out_split=1, out_dtype=F32, epilogue=None, extras=(), name):
    (k, m) = a.shape if ta else a.shape[::-1]
    b_row0, b_rows = (0, b.shape[-2]) if b_rows is None else b_rows
    b_cols = b.shape[-1] * b_split
    (n, k2) = (b_rows, b_cols) if tb else (b_cols, b_rows)
    assert k == k2, (a.shape, b.shape, ta, tb)
    n_unit = n // (out_split * (1 if tb else b_split))
    k_unit = k // (b_split if tb else 1)
    tm = _pick(m, (1024, 512, 256, 128))
    tn = _pick(n_unit, (1024, 1152, 768, 640, 512, 256, 128))
    tk = _pick(k_unit, (1024, 1152, 512, 640, 256, 128))
    nk = k // tk
    dims = (((0 if ta else 1,), (1 if tb else 0,)), ((), ()))

    def dot(a_ref, b_ref):
        b_val = b_ref[0] if b_split > 1 else b_ref[...]
        return lax.dot_general(a_ref[...].astype(BF16), b_val.astype(BF16), dims, preferred_element_type=F32)

    n_extra = len(extras)
    assert epilogue is None or out_split == 1

    def put(refs, val):
        if epilogue is not None:
            for o_ref, res in zip(refs[n_extra:], epilogue(val, *[r[...] for r in refs[:n_extra]])):
                o_ref[...] = res.astype(o_ref.dtype)
        elif out_split > 1:
            refs[0][0] = val.astype(refs[0].dtype)
        else:
            refs[0][...] = val.astype(refs[0].dtype)

    def body_single(a_ref, b_ref, *refs):
        put(refs, dot(a_ref, b_ref))

    def body_acc(a_ref, b_ref, *refs):
        kk = pl.program_id(2)
        acc_ref = refs[-1]

        @pl.when(kk == 0)
        def _():
            acc_ref[...] = jnp.zeros_like(acc_ref)

        acc_ref[...] += dot(a_ref, b_ref)

        @pl.when(kk == nk - 1)
        def _():
            put(refs[:-1], acc_ref[...])

    a_spec = pl.BlockSpec((tk, tm), lambda i, j, kk: (kk, i)) if ta else pl.BlockSpec((tm, tk), lambda i, j, kk: (i, kk))
    if b_split == 1:
        off = b_row0 // (tn if tb else tk)
        assert off * (tn if tb else tk) == b_row0
        b_spec = pl.BlockSpec((tn, tk), lambda i, j, kk: (j + off, kk)) if tb else pl.BlockSpec((tk, tn), lambda i, j, kk: (kk + off, j))
    elif tb:
        per = k_unit // tk
        b_spec = pl.BlockSpec((1, tn, tk), lambda i, j, kk: (kk // per, j, kk % per))
    else:
        per = n // b_split // tn
        b_spec = pl.BlockSpec((1, tk, tn), lambda i, j, kk: (j // per, kk, j % per))
    if out_split == 1:
        o_spec = pl.BlockSpec((tm, tn), lambda i, j, kk: (i, j))
        o_shape = None if epilogue is not None else jax.ShapeDtypeStruct((m, n), out_dtype)
    else:
        per_o = n // out_split // tn
        o_spec = pl.BlockSpec((1, tm, tn), lambda i, j, kk: (j // per_o, i, j % per_o))
        o_shape = jax.ShapeDtypeStruct((out_split, m, n // out_split), out_dtype)
    if epilogue is not None:
        o_shape = [jax.ShapeDtypeStruct((m, n), dt) for dt in out_dtype]
        o_spec = [o_spec] * len(out_dtype)
    return pl.pallas_call(
        body_single if nk == 1 else body_acc, name=name, grid=(m // tm, n // tn, nk),
        in_specs=[a_spec, b_spec] + [pl.BlockSpec((tm, tn), lambda i, j, kk: (i, j))] * n_extra, out_specs=o_spec, out_shape=o_shape,
        scratch_shapes=[] if nk == 1 else [pltpu.VMEM((tm, tn), F32)],
        compiler_params=_params("parallel", "parallel", "arbitrary"),
    )(a, b, *extras)


def _ew(fn, ins, out_dtypes, name, tc=None):
    shape = ins[0].shape
    lead, (rows, cols) = shape[:-2], shape[-2:]
    tc = cols if tc is None else tc
    tr = _pick(rows, (ROW_TILE, 128, 8)) if tc > 128 else _pick(rows, (4096, 2256, 2048, 1024, ROW_TILE, 8))
    n_in = len(ins)

    def body(*refs):
        res = fn(*[r[...] for r in refs[:n_in]])
        for o_ref, val in zip(refs[n_in:], res):
            o_ref[...] = val.astype(o_ref.dtype)

    if lead:
        spec = pl.BlockSpec((None, tr, tc), lambda l, i, j: (l, i, j))
    else:
        spec = pl.BlockSpec((tr, tc), lambda i, j: (i, j))
    return pl.pallas_call(
        body, name=name, grid=lead + (rows // tr, cols // tc),
        in_specs=[spec] * n_in, out_specs=[spec] * len(out_dtypes),
        out_shape=[jax.ShapeDtypeStruct(shape, dt) for dt in out_dtypes],
        compiler_params=_params(*(["parallel"] * (len(lead) + 2))),
    )(*ins)


def _relu2_fwd(a):
    r = jnp.maximum(a, 0.0)
    return a, r * r


def _relu2_bwd(dr, a):
    return (dr * (2.0 * jnp.maximum(a, 0.0)),)


def _adamw(w, g, m, v, name):
    bc1 = 1.0 - ADAM_B1 ** ADAM_STEP
    bc2 = 1.0 - ADAM_B2 ** ADAM_STEP

    def fn(w, g, m, v):
        m = ADAM_B1 * m + (1.0 - ADAM_B1) * g
        v = ADAM_B2 * v + (1.0 - ADAM_B2) * (g * g)
        m_hat = m / bc1
        v_hat = v / bc2
        delta = -ADAM_LR * (m_hat / (jnp.sqrt(v_hat) + ADAM_EPS) + ADAM_WD * w)
        return delta, m, v
    return _ew(fn, [w, g, m, v], [F32, F32, F32], name)


def _row_spec(cols, block=0):
    return pl.BlockSpec((ROW_TILE, cols), lambda i, block=block: (i, block))


def _vec_spec(cols):
    return pl.BlockSpec((1, cols), lambda i: (0, 0))


def _vec_args(*vecs):
    arrays = [v[0] if isinstance(v, tuple) else v for v in vecs]
    specs = [pl.BlockSpec((None, 1, D), lambda i, row=v[1]: (row, 0, 0)) if isinstance(v, tuple) else _vec_spec(D) for v in vecs]
    return arrays, specs


def _sum_spec(cols):
    return pl.BlockSpec((8, cols), lambda i: (0, 0))


def _rstd(x):
    return lax.rsqrt(jnp.mean(x * x, axis=-1, keepdims=True) + RMS_EPS)


def _modnorm_fwd(x, g, shift, scale, name):
    s = x.shape[0]

    def body(x_ref, g_ref, sh_ref, sc_ref, h_ref):
        xv = x_ref[...]
        n = xv * _rstd(xv)
        h_ref[...] = ((n * g_ref[...]) * (1.0 + sc_ref[...]) + sh_ref[...]).astype(BF16)

    vecs, vec_specs = _vec_args(g, shift, scale)
    return pl.pallas_call(
        body, name=name, grid=(s // ROW_TILE,),
        in_specs=[_row_spec(D)] + vec_specs, out_specs=_row_spec(D),
        out_shape=jax.ShapeDtypeStruct((s, D), BF16), compiler_params=_params("parallel"),
    )(x, *vecs)


def _post_fwd(x, y, g, gate, name):
    s = x.shape[0]

    def body(x_ref, y_ref, g_ref, gate_ref, o_ref):
        yv = y_ref[...]
        o_ref[...] = x_ref[...] + gate_ref[...] * ((yv * _rstd(yv)) * g_ref[...])

    vecs, vec_specs = _vec_args(g, gate)
    return pl.pallas_call(
        body, name=name, grid=(s // ROW_TILE,),
        in_specs=[_row_spec(D), _row_spec(D)] + vec_specs, out_specs=_row_spec(D),
        out_shape=jax.ShapeDtypeStruct((s, D), F32), compiler_params=_params("parallel"),
    )(x, y, *vecs)


def _post_bwd(dxo, y, g, gate, name):
    s = dxo.shape[0]

    def body(d_ref, y_ref, g_ref, gate_ref, dy_ref, sum_ref):
        @pl.when(pl.program_id(0) == 0)
        def _():
            sum_ref[...] = jnp.zeros_like(sum_ref)

        dv, yv = d_ref[...], y_ref[...]
        r = _rstd(yv)
        n = yv * r
        sum_ref[0:1, :] += jnp.sum(dv * (n * g_ref[...]), axis=0, keepdims=True)
        sum_ref[1:2, :] += jnp.sum((dv * gate_ref[...]) * n, axis=0, keepdims=True)
        dn = (dv * gate_ref[...]) * g_ref[...]
        dy_ref[...] = (r * (dn - n * jnp.mean(dn * n, axis=-1, keepdims=True))).astype(BF16)

    vecs, vec_specs = _vec_args(g, gate)
    return pl.pallas_call(
        body, name=name, grid=(s // ROW_TILE,),
        in_specs=[_row_spec(D), _row_spec(D)] + vec_specs,
        out_specs=[_row_spec(D), _sum_spec(D)],
        out_shape=[jax.ShapeDtypeStruct((s, D), BF16), jax.ShapeDtypeStruct((8, D), F32)],
        compiler_params=_params("arbitrary"),
    )(dxo, y, *vecs)


def _modnorm_bwd(dh, x, dxo, g, scale, name):
    s = dh.shape[0]

    def body(dh_ref, x_ref, d_ref, g_ref, sc_ref, dx_ref, sum_ref):
        @pl.when(pl.program_id(0) == 0)
        def _():
            sum_ref[...] = jnp.zeros_like(sum_ref)

        dhv, xv = dh_ref[...], x_ref[...]
        r = _rstd(xv)
        n = xv * r
        one_sc = 1.0 + sc_ref[...]
        sum_ref[0:1, :] += jnp.sum(dhv, axis=0, keepdims=True)
        sum_ref[1:2, :] += jnp.sum(dhv * (n * g_ref[...]), axis=0, keepdims=True)
        sum_ref[2:3, :] += jnp.sum((dhv * one_sc) * n, axis=0, keepdims=True)
        dn = (dhv * one_sc) * g_ref[...]
        dx_ref[...] = d_ref[...] + r * (dn - n * jnp.mean(dn * n, axis=-1, keepdims=True))

    vecs, vec_specs = _vec_args(g, scale)
    return pl.pallas_call(
        body, name=name, grid=(s // ROW_TILE,),
        in_specs=[_row_spec(D), _row_spec(D), _row_spec(D)] + vec_specs,
        out_specs=[_row_spec(D), _sum_spec(D)],
        out_shape=[jax.ShapeDtypeStruct((s, D), F32), jax.ShapeDtypeStruct((8, D), F32)],
        compiler_params=_params("arbitrary"),
    )(dh, x, dxo, *vecs)


def _loss_head(y, target):
    s = y.shape[0]

    def body(y_ref, t_ref, dy_ref, sum_ref):
        @pl.when(pl.program_id(0) == 0)
        def _():
            sum_ref[...] = jnp.zeros_like(sum_ref)

        err = y_ref[...] - t_ref[...]
        dy_ref[...] = err * (1.0 / D)
        sum_ref[...] += jnp.sum(err * err)

    return pl.pallas_call(
        body, name="loss_head", grid=(s // ROW_TILE,),
        in_specs=[_row_spec(D), _row_spec(D)],
        out_specs=[_row_spec(D), pl.BlockSpec((8, 128), lambda i: (0, 0))],
        out_shape=[jax.ShapeDtypeStruct((s, D), F32), jax.ShapeDtypeStruct((8, 128), F32)],
        compiler_params=_params("arbitrary"),
    )(y, target)


def _merge_fwd(z, pa, pb, pc):
    s = z.shape[0]

    def body(g0_ref, g1_ref, g2_ref, pa_ref, pb_ref, pc_ref, o_ref):
        o_ref[...] = (jax.nn.sigmoid(g0_ref[...]) * pa_ref[...] + jax.nn.sigmoid(g1_ref[...]) * pb_ref[...]
                      + jax.nn.sigmoid(g2_ref[...]) * pc_ref[...]).astype(BF16)

    return pl.pallas_call(
        body, name="merge_fwd", grid=(s // ROW_TILE,),
        in_specs=[_row_spec(D, 0), _row_spec(D, 1), _row_spec(D, 2), _row_spec(D), _row_spec(D), _row_spec(D)],
        out_specs=_row_spec(D), out_shape=jax.ShapeDtypeStruct((s, D), BF16),
        compiler_params=_params("parallel"),
    )(z, z, z, pa, pb, pc)


def _merge_bwd(dm, z, pa, pb, pc):
    s = z.shape[0]

    def body(dm_ref, g0_ref, g1_ref, g2_ref, pa_ref, pb_ref, pc_ref, dgl_ref, da_ref, db_ref, dc_ref):
        dmv = dm_ref[...]
        for i, (g_ref, p_ref, d_ref) in enumerate(((g0_ref, pa_ref, da_ref), (g1_ref, pb_ref, db_ref), (g2_ref, pc_ref, dc_ref))):
            gate = jax.nn.sigmoid(g_ref[...])
            dgl_ref[:, i * D:(i + 1) * D] = ((dmv * p_ref[...]) * (gate * (1.0 - gate))).astype(BF16)
            d_ref[...] = (dmv * gate).astype(BF16)

    return pl.pallas_call(
        body, name="merge_bwd", grid=(s // ROW_TILE,),
        in_specs=[_row_spec(D), _row_spec(D, 0), _row_spec(D, 1), _row_spec(D, 2), _row_spec(D), _row_spec(D), _row_spec(D)],
        out_specs=[_row_spec(3 * D), _row_spec(D), _row_spec(D), _row_spec(D)],
        out_shape=[jax.ShapeDtypeStruct((s, Z_COLS), BF16)] + [jax.ShapeDtypeStruct((s, D), BF16)] * 3,
        compiler_params=_params("parallel"),
    )(dm, z, z, z, pa, pb, pc)


def _shift_down(v, n):
    row = lax.broadcasted_iota(jnp.int32, v.shape, 0)
    return jnp.where(row >= n, pltpu.roll(v, n, axis=0), 0.0)


def _shift_up(v, n):
    s = v.shape[0]
    row = lax.broadcasted_iota(jnp.int32, v.shape, 0)
    return jnp.where(row < s - n, pltpu.roll(v, s - n, axis=0), 0.0)


def _log_sigmoid(v):
    return jnp.minimum(v, 0.0) - jnp.log1p(jnp.exp(-jnp.abs(v)))


def _cumf_fwd(fl, bias):
    s = fl.shape[0]

    def body(fl_ref, b_ref, o_ref):
        acc = _log_sigmoid(fl_ref[...] + b_ref[...])
        step = 1
        while step < s:
            acc = acc + _shift_down(acc, step)
            step *= 2
        o_ref[...] = acc

    return pl.pallas_call(body, name="cumf_fwd", out_shape=jax.ShapeDtypeStruct((s, 128), F32),
                          compiler_params=pltpu.CompilerParams(vmem_limit_bytes=V7X_VMEM_LIMIT))(fl, bias)


def _cumf_bwd(dcum, fl, bias):
    s = fl.shape[0]

    def body(d_ref, fl_ref, b_ref, dfl_ref, db_ref):
        acc = d_ref[...]
        step = 1
        while step < s:
            acc = acc + _shift_up(acc, step)
            step *= 2
        dfl = acc * jax.nn.sigmoid(-(fl_ref[...] + b_ref[...]))
        dfl_ref[...] = dfl.astype(BF16)
        db_ref[...] = jnp.broadcast_to(jnp.sum(dfl, axis=0, keepdims=True), (8, 128))

    return pl.pallas_call(
        body, name="cumf_bwd",
        out_shape=[jax.ShapeDtypeStruct((s, 128), BF16), jax.ShapeDtypeStruct((8, 128), F32)],
        compiler_params=pltpu.CompilerParams(vmem_limit_bytes=V7X_VMEM_LIMIT))(dcum, fl, bias)


def _pool_windows(v, shift):
    s2 = v + shift(v, 1)
    s4 = s2 + shift(s2, 2)
    s8 = s4 + shift(s4, 4)
    s16 = s8 + shift(s8, 8)
    group = lax.broadcasted_iota(jnp.int32, v.shape, 1) // 64
    return jnp.where(group == 0, s2, jnp.where(group == 1, s4, jnp.where(group == 2, s8, s16)))


def _pool_count(shape):
    group = lax.broadcasted_iota(jnp.int32, shape, 1) // 64
    window = jnp.where(group == 0, 2.0, jnp.where(group == 1, 4.0, jnp.where(group == 2, 8.0, 16.0)))
    t1 = (lax.broadcasted_iota(jnp.int32, shape, 0) + 1).astype(F32)
    return jnp.minimum(t1, window)


def _pc_specs(s):
    zcol = lambda blk: pl.BlockSpec((s, 256), lambda i, blk=blk: (0, blk))
    first = Z_PC // 256
    return [zcol(first), zcol(first + 1), zcol(first + 2), zcol(first + 3),
            pl.BlockSpec((256, 256), lambda i: (0, 0)), pl.BlockSpec((1, 256), lambda i: (0, 0)),
            pl.BlockSpec((3, 256), lambda i: (0, 0))]


def _poolconv_fwd(z, wbd, pscale, convw):
    s = z.shape[0]

    def body(pu_ref, ch_ref, cb_ref, cc_ref, w_ref, ps_ref, cw_ref, yb_ref, yc_ref):
        u = pu_ref[...]
        p = _pool_windows(u, _shift_down) / _pool_count(u.shape) - u
        yb = jnp.dot(p.astype(BF16), w_ref[...].astype(BF16), preferred_element_type=F32) * ps_ref[...]
        yb_ref[...] = yb.astype(BF16)
        uc = cc_ref[...] * ch_ref[...]
        cw = cw_ref[...]
        conv = cw[0:1, :] * _shift_down(uc, 2) + cw[1:2, :] * _shift_down(uc, 1) + cw[2:3, :] * uc
        yc_ref[...] = (cb_ref[...] * conv).astype(BF16)

    out = pl.BlockSpec((s, 256), lambda i: (0, 0))
    return pl.pallas_call(
        body, name="poolconv_fwd", grid=(1,), in_specs=_pc_specs(s), out_specs=[out, out],
        out_shape=[jax.ShapeDtypeStruct((s, 256), BF16)] * 2, compiler_params=_params("arbitrary"),
    )(z, z, z, z, wbd, pscale, convw)


def _poolconv_bwd(dyb, dyc, z, wbd, pscale, convw):
    s = z.shape[0]

    def body(dyb_ref, dyc_ref, pu_ref, ch_ref, cb_ref, cc_ref, w_ref, ps_ref, cw_ref, dz_ref, dw_ref, dps_ref, dcw_ref):
        u = pu_ref[...]
        count = _pool_count(u.shape)
        p = (_pool_windows(u, _shift_down) / count - u).astype(BF16)
        wb = w_ref[...].astype(BF16)
        dyb_v = dyb_ref[...]
        pw = jnp.dot(p, wb, preferred_element_type=F32)
        dps_ref[...] = jnp.broadcast_to(jnp.sum(dyb_v * pw, axis=0, keepdims=True), (8, 256))
        dys = (dyb_v * ps_ref[...]).astype(BF16)
        dp = lax.dot_general(dys, wb, (((1,), (1,)), ((), ())), preferred_element_type=F32)
        dw_ref[...] = lax.dot_general(p, dys, (((0,), (0,)), ((), ())), preferred_element_type=F32)
        dz_ref[:, 0:256] = (_pool_windows(dp / count, _shift_up) - dp).astype(BF16)

        ch, cb, cc = ch_ref[...], cb_ref[...], cc_ref[...]
        uc = cc * ch
        cw = cw_ref[...]
        u2, u1 = _shift_down(uc, 2), _shift_down(uc, 1)
        conv = cw[0:1, :] * u2 + cw[1:2, :] * u1 + cw[2:3, :] * uc
        dyc_v = dyc_ref[...]
        dconv = dyc_v * cb
        du = cw[0:1, :] * _shift_up(dconv, 2) + cw[1:2, :] * _shift_up(dconv, 1) + cw[2:3, :] * dconv
        dz_ref[:, 256:512] = (du * cc).astype(BF16)
        dz_ref[:, 512:768] = (dyc_v * conv).astype(BF16)
        dz_ref[:, 768:1024] = (du * ch).astype(BF16)
        dcw_ref[...] = jnp.zeros_like(dcw_ref)
        dcw_ref[0:1, :] = jnp.sum(dconv * u2, axis=0, keepdims=True)
        dcw_ref[1:2, :] = jnp.sum(dconv * u1, axis=0, keepdims=True)
        dcw_ref[2:3, :] = jnp.sum(dconv * uc, axis=0, keepdims=True)

    blk = lambda r, c: pl.BlockSpec((r, c), lambda i: (0, 0))
    return pl.pallas_call(
        body, name="poolconv_bwd", grid=(1,),
        in_specs=[blk(s, 256), blk(s, 256)] + _pc_specs(s),
        out_specs=[blk(s, 1024), blk(256, 256), blk(8, 256), blk(8, 256)],
        out_shape=[jax.ShapeDtypeStruct((s, 1024), BF16), jax.ShapeDtypeStruct((256, 256), F32),
                   jax.ShapeDtypeStruct((8, 256), F32), jax.ShapeDtypeStruct((8, 256), F32)],
        compiler_params=_params("arbitrary"),
    )(dyb, dyc, z, z, z, z, wbd, pscale, convw)


_NT = (((1,), (1,)), ((), ()))
_TN = (((0,), (0,)), ((), ()))


ATT_Q, ATT_K = 256, 256
ATT_HEADS_BWD = 4
ATT_HEADS = 8


def _att_logits(q, k, fr, q0, k0, masked):
    logits = lax.dot_general(q, k, _NT, preferred_element_type=F32) - fr
    if not masked:
        return logits
    row = q0 + lax.broadcasted_iota(jnp.int32, logits.shape, 0)
    col = k0 + lax.broadcasted_iota(jnp.int32, logits.shape, 1)
    return jnp.where(row >= col, logits, NEG_INF)


def _causal_sweep(step, qi, init):
    n_full = (qi * ATT_Q) // ATT_K
    carry = lax.fori_loop(0, n_full, lambda j, carry: step(j, carry, False), init)
    return step(n_full, carry, True)


HEAD_PAIRS = HEADS // 2


def _lane_pick(v, lane, idx):
    return jnp.sum(jnp.where(lane == idx, v, 0.0), axis=-1, keepdims=True)


def _lane_put(lane, idx, col):
    return jnp.where(lane == idx, col, 0.0)


def _split_heads(v, low):
    zero = jnp.zeros_like(v)
    return jnp.where(low, v, zero), jnp.where(low, zero, v)


def _attn_fwd(qkv, fr):
    s = qkv.shape[0]
    nk = s // ATT_K
    width = ATT_HEADS * HEAD_DIM
    groups = HEADS // ATT_HEADS

    def body(q_ref, k_ref, v_ref, fr_ref, o_ref, lse_ref):
        qi, grp = pl.program_id(0), pl.program_id(1)
        lane = lax.broadcasted_iota(jnp.int32, (ATT_Q, 128), 1)
        low = lane < HEAD_DIM
        qs = []
        for pr in range(ATT_HEADS // 2):
            qs += _split_heads(q_ref[:, 128 * pr:128 * (pr + 1)] * (HEAD_DIM ** -0.5), low)

        def step(j, carry, masked):
            k0 = pl.multiple_of(j * ATT_K, ATT_K)
            out = []
            for h in range(ATT_HEADS):
                cols = slice(128 * (h // 2), 128 * (h // 2 + 1))
                m, l, acc = carry[h]
                logits = _att_logits(qs[h], k_ref[pl.ds(k0, ATT_K), cols], fr_ref[h, pl.ds(j, 1), :], qi * ATT_Q, k0, masked)
                m_new = jnp.maximum(m, jnp.max(logits, axis=-1, keepdims=True))
                p = jnp.exp(logits - m_new)
                alpha = jnp.exp(m - m_new)
                l = alpha * l + jnp.sum(p, axis=-1, keepdims=True)
                acc = alpha * acc + jnp.dot(p.astype(BF16), v_ref[pl.ds(k0, ATT_K), cols], preferred_element_type=F32)
                out.append((m_new, l, acc))
            return tuple(out)

        one = (jnp.full((ATT_Q, 1), NEG_INF, F32), jnp.zeros((ATT_Q, 1), F32), jnp.zeros((ATT_Q, 128), F32))
        done = _causal_sweep(step, qi, (one,) * ATT_HEADS)

        @pl.when(grp == 0)
        def _():
            lse_ref[...] = jnp.zeros_like(lse_ref)

        lse = jnp.zeros((ATT_Q, 128), F32)
        for pr in range(ATT_HEADS // 2):
            (m0, l0, acc0), (m1, l1, acc1) = done[2 * pr], done[2 * pr + 1]
            o_ref[:, 128 * pr:128 * (pr + 1)] = jnp.where(low, acc0 / l0, acc1 / l1)
            head = ATT_HEADS * grp + 2 * pr
            lse = lse + _lane_put(lane, head, m0 + jnp.log(l0)) + _lane_put(lane, head + 1, m1 + jnp.log(l1))
        lse_ref[...] += lse

    return pl.pallas_call(
        body, name="attn_fwd", grid=(s // ATT_Q, groups),
        in_specs=[pl.BlockSpec((ATT_Q, width), lambda i, g: (i, g)),
                  pl.BlockSpec((s, width), lambda i, g: (0, groups + g)),
                  pl.BlockSpec((s, width), lambda i, g: (0, 2 * groups + g)),
                  pl.BlockSpec((ATT_HEADS, nk, ATT_K), lambda i, g: (g, 0, 0))],
        out_specs=[pl.BlockSpec((ATT_Q, width), lambda i, g: (i, g)), pl.BlockSpec((ATT_Q, 128), lambda i, g: (i, 0))],
        out_shape=[jax.ShapeDtypeStruct((s, A_WIDTH), F32), jax.ShapeDtypeStruct((s, 128), F32)],
        compiler_params=_params("parallel", "arbitrary"),
    )(qkv, qkv, qkv, fr)


def _attn_bwd(qkv, do, o, lse, fr):
    s = qkv.shape[0]
    nk = s // ATT_K
    scale = HEAD_DIM ** -0.5
    heads = ATT_HEADS_BWD
    width = heads * HEAD_DIM
    groups = HEADS // heads

    def body(q_ref, k_ref, v_ref, do_ref, o_ref, lse_ref, fr_ref, dq_ref, dk_ref, dv_ref, dfc_ref, dfr_ref, dk_acc, dv_acc):
        grp = pl.program_id(0)
        lane = lax.broadcasted_iota(jnp.int32, (ATT_Q, 128), 1)
        low = lane < HEAD_DIM
        low_k = lax.broadcasted_iota(jnp.int32, (ATT_K, 128), 1) < HEAD_DIM
        dk_acc[...] = jnp.zeros_like(dk_acc)
        dv_acc[...] = jnp.zeros_like(dv_acc)
        dfr_ref[...] = jnp.zeros_like(dfr_ref)

        @pl.when(grp == 0)
        def _():
            dfc_ref[...] = jnp.zeros_like(dfc_ref)

        def outer(i, carry):
            q0 = pl.multiple_of(i * ATT_Q, ATT_Q)
            rows = pl.ds(q0, ATT_Q)
            lsev = lse_ref[rows, :]
            q2s, dob2s, qs, dos, deltas, lses = [], [], [], [], [], []
            for pr in range(heads // 2):
                pcols = slice(128 * pr, 128 * (pr + 1))
                q2, do2 = q_ref[rows, pcols] * scale, do_ref[rows, pcols]
                prod = do2 * o_ref[rows, pcols]
                deltas += [jnp.sum(jnp.where(low, prod, 0.0), axis=-1, keepdims=True),
                           jnp.sum(jnp.where(low, 0.0, prod), axis=-1, keepdims=True)]
                dob2 = do2.astype(BF16)
                q2s.append(q2)
                dob2s.append(dob2)
                qs += _split_heads(q2, low)
                dos += _split_heads(dob2, low)
                lses += [_lane_pick(lsev, lane, heads * grp + 2 * pr), _lane_pick(lsev, lane, heads * grp + 2 * pr + 1)]

            def inner(j, carry, masked):
                k0 = pl.multiple_of(j * ATT_K, ATT_K)
                krows = pl.ds(k0, ATT_K)
                out, dk_parts, dv_parts = [], [], []
                for h in range(heads):
                    pcols = slice(128 * (h // 2), 128 * (h // 2 + 1))
                    dq, dfc = carry[h]
                    k2 = k_ref[krows, pcols]
                    p = jnp.exp(_att_logits(qs[h], k2, fr_ref[h, pl.ds(j, 1), :], q0, k0, masked) - lses[h])
                    dp = lax.dot_general(dos[h], v_ref[krows, pcols], _NT, preferred_element_type=F32)
                    ds = p * (dp - deltas[h])
                    dsb = ds.astype(BF16)
                    dk_parts.append(lax.dot_general(dsb, q2s[h // 2], _TN, preferred_element_type=F32))
                    dv_parts.append(lax.dot_general(p.astype(BF16), dob2s[h // 2], _TN, preferred_element_type=F32))
                    dfr_ref[h, pl.ds(j, 1), :] -= jnp.sum(ds, axis=0, keepdims=True)
                    out.append((dq + jnp.dot(dsb, k2, preferred_element_type=F32), dfc + jnp.sum(ds, axis=-1, keepdims=True)))
                for pr in range(heads // 2):
                    pcols = slice(128 * pr, 128 * (pr + 1))
                    dk_acc[krows, pcols] += jnp.where(low_k, dk_parts[2 * pr], dk_parts[2 * pr + 1])
                    dv_acc[krows, pcols] += jnp.where(low_k, dv_parts[2 * pr], dv_parts[2 * pr + 1])
                return tuple(out)

            one = (jnp.zeros((ATT_Q, 128), F32), jnp.zeros((ATT_Q, 1), F32))
            done = _causal_sweep(inner, i, (one,) * heads)
            dfc = jnp.zeros((ATT_Q, 128), F32)
            for pr in range(heads // 2):
                (dq0, dfc0), (dq1, dfc1) = done[2 * pr], done[2 * pr + 1]
                dq_ref[rows, 128 * pr:128 * (pr + 1)] = (jnp.where(low, dq0, dq1) * scale).astype(BF16)
                head = heads * grp + 2 * pr
                dfc = dfc + _lane_put(lane, head, dfc0) + _lane_put(lane, head + 1, dfc1)
            dfc_ref[rows, :] += dfc
            return carry

        lax.fori_loop(0, s // ATT_Q, outer, 0)
        dk_ref[...] = dk_acc[...].astype(BF16)
        dv_ref[...] = dv_acc[...].astype(BF16)

    part = lambda first: pl.BlockSpec((s, width), lambda g, first=first: (0, first + g))
    whole = pl.BlockSpec((s, 128), lambda g: (0, 0))
    rowv = pl.BlockSpec((heads, nk, ATT_K), lambda g: (g, 0, 0))
    return pl.pallas_call(
        body, name="attn_bwd", grid=(groups,),
        in_specs=[part(0), part(groups), part(2 * groups), part(0), part(0), whole, rowv],
        out_specs=[part(0), part(0), part(0), whole, rowv],
        out_shape=[jax.ShapeDtypeStruct((s, A_WIDTH), BF16)] * 3 + [jax.ShapeDtypeStruct((s, 128), F32), jax.ShapeDtypeStruct((HEADS, nk, ATT_K), F32)],
        scratch_shapes=[pltpu.VMEM((s, width), F32), pltpu.VMEM((s, width), F32)],
        compiler_params=_params("arbitrary"),
    )(qkv, qkv, qkv, do, o, lse, fr)


def _ada_fwd(c_all, w_ada, b_loc):
    depth, _, n = w_ada.shape
    tn = 512

    def body(c_ref, w_ref, b_ref, o_ref, sc_ref):
        cv = c_ref[...]
        sc = cv * jax.nn.sigmoid(cv)
        sc_ref[...] = sc
        o_ref[0] = jnp.dot(sc.astype(BF16), w_ref[0].astype(BF16), preferred_element_type=F32) + b_ref[0]

    return pl.pallas_call(
        body, name="ada_fwd", grid=(depth, n // tn),
        in_specs=[pl.BlockSpec((N_DEV, D), lambda l, j: (0, 0)), pl.BlockSpec((1, D, tn), lambda l, j: (l, 0, j)),
                  pl.BlockSpec((1, 1, tn), lambda l, j: (l, 0, j))],
        out_specs=[pl.BlockSpec((1, N_DEV, tn), lambda l, j: (l, 0, j)), pl.BlockSpec((N_DEV, D), lambda l, j: (0, 0))],
        out_shape=[jax.ShapeDtypeStruct((depth, N_DEV, n), F32), jax.ShapeDtypeStruct((N_DEV, D), F32)],
        compiler_params=_params("arbitrary", "arbitrary"),
    )(c_all, w_ada, b_loc)


def _sum_devices(gathered):
    n = gathered.shape[1]
    tn = _pick(n, (1408, 1024, 640, 512, 128))

    def body(g_ref, o_ref):
        acc = g_ref[0:8, :]
        for dev in range(1, N_DEV):
            acc = acc + g_ref[8 * dev:8 * dev + 8, :]
        o_ref[...] = acc

    return pl.pallas_call(
        body, name="sum_devices", grid=(n // tn,),
        in_specs=[pl.BlockSpec((8 * N_DEV, tn), lambda j: (0, j))], out_specs=pl.BlockSpec((8, tn), lambda j: (0, j)),
        out_shape=jax.ShapeDtypeStruct((8, n), F32), compiler_params=_params("parallel"),
    )(gathered)


def _place():
    x, y, c = lax.axis_index("x"), lax.axis_index("y"), lax.axis_index("c")
    chips = [(1 - x, y), (x, 1 - y), (1 - x, 1 - y)]
    return x, y, c, chips


def _allgather8(block, name):
    m_per, n = block.shape

    def body(x_ref, out_ref, send_sems, recv_sems, local_sem):
        x, y, c, chips = _place()
        me, sibling = (x, y, c), (x, y, 1 - c)

        def rows(px, py, pc):
            return out_ref.at[pl.ds((4 * px + 2 * py + pc) * m_per, m_per), :]

        def copy(k, blk, to, src=None):
            return pltpu.make_async_remote_copy(
                src_ref=rows(*blk) if src is None else src, dst_ref=rows(*blk),
                send_sem=send_sems.at[k], recv_sem=recv_sems.at[k], device_id=to, device_id_type=MESH)

        mine = pltpu.make_async_copy(x_ref, rows(*me), local_sem)
        mine.start()
        first = [copy(0, me, sibling, src=x_ref)]
        first += [copy(1 + j, me, (*chip, c), src=x_ref) for j, chip in enumerate(chips)]
        for cp in first:
            cp.start()
        passed = [copy(4 + j, (*chip, c), sibling) for j, chip in enumerate(chips)]
        for j, chip in enumerate(chips):
            copy(1 + j, (*chip, c), me).wait_recv()
            passed[j].start()
        copy(0, sibling, me).wait_recv()
        for j, chip in enumerate(chips):
            copy(4 + j, (*chip, 1 - c), me).wait_recv()
        for cp in first + passed:
            cp.wait_send()
        mine.wait()

    return pl.pallas_call(
        body, name=name, out_shape=jax.ShapeDtypeStruct((N_DEV * m_per, n), block.dtype),
        in_specs=[pl.BlockSpec(memory_space=pltpu.VMEM)], out_specs=pl.BlockSpec(memory_space=pltpu.VMEM),
        scratch_shapes=[pltpu.SemaphoreType.DMA((7,)), pltpu.SemaphoreType.DMA((7,)), pltpu.SemaphoreType.DMA],
        compiler_params=pltpu.CompilerParams(vmem_limit_bytes=V7X_VMEM_LIMIT),
    )(block)


_SEM = pl.BlockSpec(memory_space=pltpu.SEMAPHORE)
_DATAFLOW = pltpu.SideEffectType.DATAFLOW_SIDE_EFFECTING


def _plan_copies(plan, refs, send_sems, recv_sems):
    return [pltpu.make_async_remote_copy(src_ref=src, dst_ref=dst, send_sem=send_sems.at[i], recv_sem=recv_sems.at[i],
                                         device_id=to, device_id_type=MESH) for i, (src, dst, to) in enumerate(plan(refs))]


def _copies_start(bufs, plan, n_copies, after, name):
    nb = len(bufs)

    def body(*refs):
        for cp in _plan_copies(plan, refs[:nb], refs[nb + 1], refs[nb + 2]):
            cp.start()
        token = refs[-1]
        token[...] = jnp.zeros_like(token)

    sem = pltpu.SemaphoreType.DMA((n_copies,))
    outs = pl.pallas_call(
        body, name=name,
        out_shape=(sem, sem, *[pltpu.HBM(b.shape, b.dtype) for b in bufs], jax.ShapeDtypeStruct((8, 128), F32)),
        in_specs=[_HBM] * nb + [pl.BlockSpec(memory_space=pl.ANY)],
        out_specs=(_SEM, _SEM, *[_HBM] * nb, pl.BlockSpec(memory_space=pltpu.VMEM)),
        input_output_aliases={i: 2 + i for i in range(nb)},
        compiler_params=pltpu.CompilerParams(has_side_effects=_DATAFLOW),
    )(*[pltpu.with_memory_space_constraint(b, pltpu.HBM) for b in bufs], after)
    return outs[0], outs[1], list(outs[2:2 + nb]), outs[-1]


def _copies_wait(started, plan, after, name):
    send_sems, recv_sems, bufs, _ = started
    nb = len(bufs)

    def body(*refs):
        for cp in _plan_copies(plan, refs[:nb], refs[nb], refs[nb + 1]):
            cp.wait_send()
            cp.wait_recv()

    return list(pl.pallas_call(
        body, name=name, out_shape=tuple(pltpu.HBM(b.shape, b.dtype) for b in bufs),
        in_specs=[_HBM] * nb + [_SEM, _SEM, pl.BlockSpec(memory_space=pl.ANY)], out_specs=tuple([_HBM] * nb),
        input_output_aliases={i: i for i in range(nb)},
        compiler_params=pltpu.CompilerParams(has_side_effects=_DATAFLOW),
    )(*bufs, send_sems, recv_sems, after))


def _half_rows(ref, axis, c):
    half = ref.shape[axis] // 2
    return pl.ds(c * half, half)


def _plan_gather_ici(refs):
    n = len(refs) // 2
    x, y, c, chips = _place()
    out = []
    for a in range(n):
        rows = _half_rows(refs[a], 0, c)
        out += [(refs[a].at[rows], refs[n + a].at[2 * x + y, rows], (*chip, c)) for chip in chips]
        out.append((refs[a], refs[n + a].at[2 * x + y], (x, y, 1 - c)))
    return out


def _plan_gather_d2d(refs):
    x, y, c, chips = _place()
    out = []
    for ref in refs:
        rows = _half_rows(ref, 1, c)
        for px, py in chips:
            landed = ref.at[2 * px + py, rows]
            out.append((landed, landed, (x, y, 1 - c)))
    return out


def _plan_rs_sibling(refs):
    n = len(refs) // 2
    x, y, c, _ = _place()
    return [(refs[a].at[pl.ds(0, N_CHIPS), _half_rows(refs[a], 1, 1 - c)], refs[n + a], (x, y, 1 - c)) for a in range(n)]


def _plan_rs_chips(refs):
    n = len(refs) // 2
    x, y, c, chips = _place()
    return [(refs[a].at[2 * px + py], refs[n + a].at[k], (px, py, c)) for a in range(n) for k, (px, py) in enumerate(chips)]


def _plan_rs_share(refs):
    x, y, c, _ = _place()
    return [(ref.at[_half_rows(ref, 0, c)], ref.at[_half_rows(ref, 0, c)], (x, y, 1 - c)) for ref in refs]


def _chip_sum(g, other, sel, name):
    _, half, cdim = other.shape
    tr = _pick(half, (512, 256, 128, 64))
    per = half // tr

    def body(sel_ref, g_ref, t_ref, wire_ref, own_ref):
        total = g_ref[0] + t_ref[0]
        wire_ref[0] = total.astype(BF16)

        @pl.when(pl.program_id(1) == sel_ref[1])
        def _():
            own_ref[...] = total

    blk = pl.BlockSpec((1, tr, cdim), lambda i, p, sel_ref: (p, i, 0))
    return pl.pallas_call(
        body, name=name,
        grid_spec=pltpu.PrefetchScalarGridSpec(
            num_scalar_prefetch=1, grid=(per, N_CHIPS),
            in_specs=[pl.BlockSpec((1, tr, cdim), lambda i, p, sel_ref: (p, sel_ref[0] * per + i, 0)), blk],
            out_specs=[blk, pl.BlockSpec((tr, cdim), lambda i, p, sel_ref: (i, 0))]),
        out_shape=[jax.ShapeDtypeStruct(other.shape, BF16), jax.ShapeDtypeStruct((half, cdim), F32)],
        compiler_params=_params("parallel", "arbitrary"),
    )(sel, g, other)


def _final_sum(own, recv, sel, name):
    half, cdim = own.shape
    tr = _pick(half, (512, 256, 128, 64))
    per = half // tr

    def body(sel_ref, own_ref, r0_ref, r1_ref, r2_ref, o_ref):
        o_ref[...] = ((own_ref[...] + r0_ref[0].astype(F32)) + r1_ref[0].astype(F32)) + r2_ref[0].astype(F32)

    part = lambda k: pl.BlockSpec((1, tr, cdim), lambda i, sel_ref, k=k: (k, i, 0))
    return pl.pallas_call(
        body, name=name,
        grid_spec=pltpu.PrefetchScalarGridSpec(
            num_scalar_prefetch=1, grid=(per,),
            in_specs=[pl.BlockSpec((tr, cdim), lambda i, sel_ref: (i, 0)), part(0), part(1), part(2)],
            out_specs=pl.BlockSpec((tr, cdim), lambda i, sel_ref: (sel_ref[0] * per + i, 0))),
        out_shape=jax.ShapeDtypeStruct((2 * half, cdim), F32), compiler_params=_params("parallel"),
    )(sel, own, recv, recv, recv)


def _row(v):
    return v.reshape(1, -1)


_BR_A, _BR_B, _BR_C = (0, A_WIDTH), (A_WIDTH, POOL_WIDTH), (A_WIDTH + POOL_WIDTH, CONV_WIDTH)


def _tie(v, token):
    return v if token is None else v + token[0:1, 0:1]


def _no_hook(point, after, ready=None):
    return None


def _layer_fwd(x, w, mod, hook=_no_hook):
    s = x.shape[0]
    mod3 = mod.reshape(6, 1, D)
    h = _modnorm_fwd(x, _row(w["g_mix_pre"]), (mod3, 0), (mod3, 1), "mix_pre_fwd")
    z = _mm(h, w["w_all"], name="mm_in")
    qkv = z[:, Z_QKV:Z_PC].astype(BF16)
    fl = z[:, Z_FL:Z_COLS]
    cum = _cumf_fwd(fl, w["b_f_pad"])
    fr = cum[:, :HEADS].T.reshape(HEADS, s // ATT_K, ATT_K)
    br_a, lse = _attn_fwd(qkv, fr)
    br_b, br_c = _poolconv_fwd(z, w["w_pool_bd"], _tie(_row(w["pool_scale"]), hook("attn", lse)), w["conv_w"])
    hook("pool", br_b)
    wbr = w["w_branch"]
    pa = _mm(br_a, wbr, b_rows=_BR_A, name="mm_br_a")
    pb = _mm(br_b, wbr, b_rows=_BR_B, name="mm_br_b")
    pc = _mm(br_c, wbr, b_rows=_BR_C, name="mm_br_c")
    merged = _merge_fwd(z, pa, pb, pc)
    y = _mm(merged, w["w_out"], name="mm_out")
    x1 = _post_fwd(x, y, _row(w["g_mix_post"]), (mod3, 2), "mix_post_fwd")
    h2 = _modnorm_fwd(x1, _row(w["g_ff_pre"]), (mod3, 3), (mod3, 4), "ff_pre_fwd")
    a, r = _mm(h2, w["w_ff1"], b_split=N_CHIPS, epilogue=_relu2_fwd, out_dtype=(F32, BF16), name="mm_ff1")
    y2 = _mm(r, w["w_ff2"], name="mm_ff2")
    x2 = _post_fwd(x1, y2, _tie(_row(w["g_ff_post"]), hook("ff_post", y2)), (mod3, 5), "ff_post_fwd")
    hook("end", x2)
    saved = dict(x=x, h=h, z=z, qkv=qkv, fl=fl, fr=fr, lse=lse, br_a=br_a, br_b=br_b, br_c=br_c, pa=pa, pb=pb, pc=pc,
                 merged=merged, y=y, x1=x1, h2=h2, a=a, r=r, y2=y2)
    return x2, saved


def _layer_bwd(dx2, sv, w, mod, hook=_no_hook):
    s = dx2.shape[0]
    mod3 = mod.reshape(6, 1, D)
    dy2, sum_ff_post = _post_bwd(dx2, sv["y2"], _row(w["g_ff_post"]), (mod3, 5), "ff_post_bwd")
    (da,) = _mm(dy2, w["w_ff2"], tb=True, epilogue=_relu2_bwd, extras=(sv["a"],), out_dtype=(BF16,), name="mm_ff2_dx")
    d_w_ff2 = _mm(sv["r"], dy2, ta=True, name="mm_ff2_dw")
    dh2 = _mm(da, w["w_ff1"], tb=True, b_split=N_CHIPS, name="mm_ff1_dx")
    d_w_ff1 = _mm(sv["h2"], da, ta=True, out_split=N_CHIPS, name="mm_ff1_dw")
    dx1, sum_ff_pre = _modnorm_bwd(dh2, sv["x1"], dx2, _tie(_row(w["g_ff_pre"]), hook("ff_pre", dh2, dict(w_ff1=d_w_ff1, w_ff2=d_w_ff2))), (mod3, 4), "ff_pre_bwd")

    dy, sum_mix_post = _post_bwd(dx1, sv["y"], _row(w["g_mix_post"]), (mod3, 2), "mix_post_bwd")
    dmerged = _mm(dy, w["w_out"], tb=True, name="mm_out_dx")
    d_w_out = _mm(sv["merged"], dy, ta=True, name="mm_out_dw")
    dz, dpa, dpb, dpc = _merge_bwd(dmerged, sv["z"], sv["pa"], sv["pb"], sv["pc"])
    wbr = w["w_branch"]
    dbr_a = _mm(dpa, wbr, tb=True, b_rows=_BR_A, name="mm_br_a_dx")
    dbr_b = _mm(dpb, wbr, tb=True, b_rows=_BR_B, name="mm_br_b_dx")
    dbr_c = _mm(dpc, wbr, tb=True, b_rows=_BR_C, name="mm_br_c_dx")
    d_w_branch = jnp.concatenate([_mm(sv["br_a"], dpa, ta=True, name="mm_br_a_dw"), _mm(sv["br_b"], dpb, ta=True, name="mm_br_b_dw"),
                                  _mm(sv["br_c"], dpc, ta=True, name="mm_br_c_dw")], axis=0)

    dq, dk, dv, dfc, dfr = _attn_bwd(sv["qkv"], dbr_a, sv["br_a"], sv["lse"], sv["fr"])
    dcum = dfc + jnp.pad(dfr.reshape(HEADS, s).T, ((0, 0), (0, 128 - HEADS)))
    dfl, sum_bf = _cumf_bwd(dcum, sv["fl"], _tie(w["b_f_pad"], hook("cumf", dfc)))
    dpc_z, d_wbd, sum_ps, sum_cw = _poolconv_bwd(dbr_b, dbr_c, sv["z"], w["w_pool_bd"], _row(w["pool_scale"]), w["conv_w"])
    for at, part in ((Z_QKV, dq), (Z_QKV + A_WIDTH, dk), (Z_QKV + 2 * A_WIDTH, dv), (Z_PC, dpc_z), (Z_FL, dfl)):
        dz = lax.dynamic_update_slice(dz, part, (0, at))
    dh = _mm(dz, w["w_all"], tb=True, name="mm_in_dx")
    d_w_all = _mm(sv["h"], dz, ta=True, name="mm_in_dw")
    hook("mix_pre", dh)
    dx, sum_mix_pre = _modnorm_bwd(dh, sv["x"], dx1, _row(w["g_mix_pre"]), (mod3, 1), "mix_pre_bwd")

    dmod = jnp.stack([sum_mix_pre[0], sum_mix_pre[1], sum_mix_post[0], sum_ff_pre[0], sum_ff_pre[1], sum_ff_post[0]])
    d_w_in = _w_in_shards(d_w_all)
    d_w_pool = jnp.stack([d_wbd[64 * g:64 * g + 64, 64 * g:64 * g + 64] for g in range(4)])
    big = dict(w_in=d_w_in, w_branch=d_w_branch, w_out=d_w_out, w_ff1=d_w_ff1, w_ff2=d_w_ff2)
    small = dict(g_mix_pre=sum_mix_pre[2], g_mix_post=sum_mix_post[1], g_ff_pre=sum_ff_pre[2], g_ff_post=sum_ff_post[1],
                 b_f=sum_bf[0, :HEADS], w_pool=d_w_pool, pool_scale=sum_ps[0], conv_w=sum_cw[0:3])
    return dx, dmod, big, small


_QKV_END, _FL_END, _PC_END = 3 * A_WIDTH, 3 * A_WIDTH + HEADS, 3 * A_WIDTH + HEADS + POOL_WIDTH + 3 * CONV_WIDTH
_W_IN_GROUPS = ((_PC_END, IN_COLS, Z_GL), (0, _QKV_END, Z_QKV), (_FL_END, _PC_END, Z_PC), (_QKV_END, _FL_END, Z_FL))
_SHARD_COLS = IN_COLS // N_CHIPS


def _w_all_from_shards(blocks):
    pieces = []
    for lo, hi, _ in _W_IN_GROUPS:
        for p in range(N_CHIPS):
            a, b = max(lo, p * _SHARD_COLS), min(hi, (p + 1) * _SHARD_COLS)
            if a < b:
                pieces.append(blocks[p][:, a - p * _SHARD_COLS:b - p * _SHARD_COLS])
    pieces.append(jnp.zeros((D, Z_COLS - IN_COLS), blocks.dtype))
    return jnp.concatenate(pieces, axis=1)


def _w_in_shards(d_w_all):
    blocks = []
    for p in range(N_CHIPS):
        pieces = []
        for lo, hi, at in sorted(_W_IN_GROUPS):
            a, b = max(lo, p * _SHARD_COLS), min(hi, (p + 1) * _SHARD_COLS)
            if a < b:
                pieces.append(d_w_all[:, at + a - lo:at + b - lo])
        blocks.append(jnp.concatenate(pieces, axis=1))
    return jnp.stack(blocks)


def _full_layer_weights(w_in_blocks, w_branch, w_out, w_ff1, w_ff2, g_mix_pre, g_mix_post, g_ff_pre, g_ff_post, b_f, w_pool, pool_scale, conv_w):
    w_all = _w_all_from_shards(w_in_blocks)
    wbd = (w_pool[:, :, None, :] * jnp.eye(4, dtype=F32)[:, None, :, None]).reshape(POOL_WIDTH, POOL_WIDTH)
    return dict(w_all=w_all, w_branch=w_branch, w_out=w_out, w_ff1=w_ff1, w_ff2=w_ff2, g_mix_pre=g_mix_pre, g_mix_post=g_mix_post,
                g_ff_pre=g_ff_pre, g_ff_post=g_ff_post, b_f_pad=jnp.pad(b_f, (0, 128 - HEADS)).reshape(1, 128), w_pool_bd=wbd,
                pool_scale=pool_scale, conv_w=conv_w)


class _NoComm:
    def layer_weights(self, l):
        raise NotImplementedError

    def fwd_hook(self, l):
        return _no_hook

    def bwd_hook(self, l):
        return _no_hook

    def grads_ready(self, l, big):
        return None


class _Layers(_NoComm):
    def __init__(self, layers):
        self.layers = layers

    def layer_weights(self, l):
        return self.layers[l]


def _local_step(x, target, mods, comm):
    saved, weights = [], []
    act = x
    for l in range(DEPTH):
        weights.append(comm.layer_weights(l))
        act, sv = _layer_fwd(act, weights[l], mods[l], comm.fwd_hook(l))
        saved.append(sv)
    dact, sq = _loss_head(act, target)
    loss = sq[0, 0] * (0.5 / D)
    dmods, bigs, smalls = [None] * DEPTH, [None] * DEPTH, [None] * DEPTH
    token = None
    for l in reversed(range(DEPTH)):
        dact, dmods[l], bigs[l], smalls[l] = _layer_bwd(dact, saved[l], weights[l], _tie(mods[l], token), comm.bwd_hook(l))
        token = comm.grads_ready(l, bigs[l])
    return loss, dact, jnp.stack(dmods), bigs, smalls


_BIG = ("w_in", "w_branch", "w_out", "w_ff1", "w_ff2")
N_BIG = len(_BIG)


class _GatherJob:
    def __init__(self, tag, shards, after):
        self.tag, self.n = tag, len(shards)
        lands = [lax.empty((N_CHIPS,) + s.shape, s.dtype) for s in shards]
        self.state = _copies_start(list(shards) + lands, _plan_gather_ici, 4 * self.n, after, "gather_ici_start_" + tag)
        self.token = self.state[3]

    def pass_on(self, after):
        bufs = _copies_wait(self.state, _plan_gather_ici, after, "gather_ici_wait_" + self.tag)
        self.state = _copies_start(bufs[self.n:], _plan_gather_d2d, 3 * self.n, bufs[0], "gather_d2d_start_" + self.tag)
        self.token = self.state[3]
        return self.token

    def done(self, after):
        return _copies_wait(self.state, _plan_gather_d2d, after, "gather_d2d_wait_" + self.tag)


class _ReduceJob:
    def __init__(self, tag, names, grads, sel, after):
        self.tag, self.names, self.n, self.sel = tag, names, len(names), sel
        lands = [lax.empty((N_CHIPS, g.shape[1] // 2, g.shape[2]), F32) for g in grads]
        self.state = _copies_start(list(grads) + lands, _plan_rs_sibling, self.n, after, "rs_sibling_start_" + tag)
        self.token = self.state[3]

    def chip_sums(self, after):
        bufs = _copies_wait(self.state, _plan_rs_sibling, after, "rs_sibling_wait_" + self.tag)
        wires, self.owns = zip(*[_chip_sum(bufs[i], bufs[self.n + i], self.sel, "rs_chip_sum_" + name) for i, name in enumerate(self.names)])
        lands = [lax.empty((3,) + w.shape[1:], BF16) for w in wires]
        self.state = _copies_start(list(wires) + lands, _plan_rs_chips, 3 * self.n, self.owns[0], "rs_chips_start_" + self.tag)
        self.token = self.state[3]
        return self.token

    def final_sums(self, after):
        bufs = _copies_wait(self.state, _plan_rs_chips, after, "rs_chips_wait_" + self.tag)
        sums = [_final_sum(self.owns[i], bufs[self.n + i], self.sel, "rs_final_" + name) for i, name in enumerate(self.names)]
        self.state = _copies_start(sums, _plan_rs_share, self.n, sums[0], "rs_share_start_" + self.tag)
        self.token = self.state[3]
        return self.token

    def done(self, after):
        return dict(zip(self.names, _copies_wait(self.state, _plan_rs_share, after, "rs_share_wait_" + self.tag)))


def _chip_blocks(g):
    return g if g.ndim == 3 else g.reshape(N_CHIPS, -1, g.shape[1])


class _StepComm(_NoComm):
    def __init__(self, shards, sel, after):
        self.sel = sel
        self.small, self.grads, self.jobs = None, [dict() for _ in range(DEPTH)], {}
        self.jobs["in0"] = _GatherJob("in0", shards[0][:1], after)
        self.jobs["rest0"] = _GatherJob("rest0", shards[0][1:], self.jobs["in0"].token)
        self.jobs["all1"] = _GatherJob("all1", shards[1], self.jobs["rest0"].token)

    def layer_weights(self, l):
        if l == 0:
            job = self.jobs["in0"]
            (g_in,) = job.done(job.pass_on(self.jobs["all1"].token))
            self.weights0 = _full_layer_weights(g_in, None, None, None, None, *self.small[0])
            return self.weights0
        g_in, g_br, g_out, g_f1, g_f2 = self.landed1
        return _full_layer_weights(g_in, g_br.reshape(D, D), g_out.reshape(D, D), g_f1, g_f2.reshape(D_FF, D), *self.small[1])

    def fwd_hook(self, l):
        if l != 0:
            return _no_hook

        def hook(point, after, ready=None):
            if point == "attn":
                return self.jobs["rest0"].pass_on(after)
            if point == "ff_post":
                return self.jobs["all1"].pass_on(after)
            if point == "pool":
                g_br, g_out, g_f1, g_f2 = self.jobs["rest0"].done(after)
                self.weights0.update(w_branch=g_br.reshape(D, D), w_out=g_out.reshape(D, D), w_ff1=g_f1, w_ff2=g_f2.reshape(D_FF, D))
            if point == "end":
                self.landed1 = self.jobs["all1"].done(after)
            return None
        return hook

    def bwd_hook(self, l):
        if l != 0:
            return _no_hook

        def hook(point, after, ready=None):
            jobs = self.jobs
            if point == "ff_pre":
                token = jobs["rs1"].chip_sums(after)
                jobs["rs0_ff"] = _ReduceJob("0_ff", ("w_ff1", "w_ff2"), [_chip_blocks(ready[n]) for n in ("w_ff1", "w_ff2")], self.sel, token)
                return jobs["rs0_ff"].token
            if point == "cumf":
                return jobs["rs0_ff"].chip_sums(jobs["rs1"].final_sums(after))
            self.grads[1] = jobs["rs1"].done(after)
            return None
        return hook

    def grads_ready(self, l, big):
        if l == 1:
            self.jobs["rs1"] = _ReduceJob("1", _BIG, [_chip_blocks(big[n]) for n in _BIG], self.sel, self.sel)
            return self.jobs["rs1"].token
        names = ("w_in", "w_branch", "w_out")
        self.jobs["rs0_mix"] = _ReduceJob("0_mix", names, [_chip_blocks(big[n]) for n in names], self.sel, self.sel)
        return self.jobs["rs0_mix"].token

    def finish_sums(self, after):
        jobs = self.jobs
        token = jobs["rs0_mix"].chip_sums(after)
        return jobs["rs0_ff"].final_sums(token)

    def finish(self, after):
        jobs = self.jobs
        self.grads[0] = jobs["rs0_ff"].done(after)
        self.grads[0].update(jobs["rs0_mix"].done(jobs["rs0_mix"].final_sums(after)))


_SMALL = ("g_mix_pre", "g_mix_post", "g_ff_pre", "g_ff_post", "b_f", "w_pool", "pool_scale", "conv_w")


def _w_in_view(t):
    return t.reshape(DEPTH, D // 128, 128, _SHARD_COLS).transpose(3, 1, 0, 2).reshape(_SHARD_COLS * (D // 128) * DEPTH, 128)


def _w_in_unview(t):
    return t.reshape(_SHARD_COLS, D // 128, DEPTH, 128).transpose(2, 1, 3, 0).reshape(DEPTH, D, _SHARD_COLS)


def _pack(parts, rows=8):
    flat = jnp.concatenate([p.reshape(-1) for p in parts])
    width = -(-flat.shape[0] // (rows * 128)) * 128
    return jnp.pad(flat, (0, rows * width - flat.shape[0])).reshape(rows, width)


def _unpack(packed, like):
    flat = packed.reshape(-1)
    out, at = [], 0
    for ref in like:
        out.append(flat[at:at + ref.size].reshape(ref.shape))
        at += ref.size
    return out


def kernel(x, c, w_ada, b_ada, g_mix_pre, g_mix_post, g_ff_pre, g_ff_post, w_in, b_f, w_pool, pool_scale, conv_w, w_branch, w_out, w_ff1, w_ff2, loss_target, m_w_ada, m_b_ada, m_g_mix_pre, m_g_mix_post, m_g_ff_pre, m_g_ff_post, m_w_in, m_b_f, m_w_pool, m_pool_scale, m_conv_w, m_w_branch, m_w_out, m_w_ff1, m_w_ff2, v_w_ada, v_b_ada, v_g_mix_pre, v_g_mix_post, v_g_ff_pre, v_g_ff_post, v_w_in, v_b_f, v_w_pool, v_pool_scale, v_conv_w, v_w_branch, v_w_out, v_w_ff1, v_w_ff2):
    xi, yi, ci = lax.axis_index("x"), lax.axis_index("y"), lax.axis_index("c")
    chip = 2 * xi + yi
    dev = 2 * chip + ci
    n_ada = w_ada.shape[2]

    first = jnp.zeros((8, D + 384), F32).at[0, :D].set(c[0]).at[0, D:].set(conv_w.reshape(-1))
    got = _allgather8(first, "gather_cond").reshape(N_DEV, 8, D + 384)[:, 0]
    c_all = got[:, :D]
    conv_full = got[0::2, D:].reshape(N_CHIPS, DEPTH, 3, CONV_WIDTH // N_CHIPS).transpose(1, 2, 0, 3).reshape(DEPTH, 3, CONV_WIDTH)

    b_loc = lax.dynamic_slice_in_dim(b_ada, chip * n_ada, n_ada, axis=1).reshape(DEPTH, 1, n_ada)
    mod_cols, silu_c = _ada_fwd(c_all, w_ada, b_loc)
    got = _allgather8(mod_cols.reshape(DEPTH * N_DEV, n_ada), "gather_mod").reshape(N_DEV, DEPTH, N_DEV, n_ada)[0::2]
    mod_all = got.transpose(1, 2, 0, 3).reshape(DEPTH, N_DEV, 6, D)
    mods = lax.dynamic_index_in_dim(mod_all, dev, axis=1, keepdims=False)

    comm = _StepComm([[w[l].astype(BF16) for w in (w_in, w_branch, w_out, w_ff1, w_ff2)] for l in range(DEPTH)],
                     jnp.stack([ci, chip]).astype(jnp.int32), mods)
    comm.small = [(g_mix_pre[l], g_mix_post[l], g_ff_pre[l], g_ff_post[l], b_f[l], w_pool[l], pool_scale[l], conv_full[l]) for l in range(DEPTH)]
    loss_part, grad_x, dmods, bigs, smalls = _local_step(x[0], loss_target[0], mods, comm)

    small_parts = [smalls[l][name] for name in _SMALL for l in range(DEPTH)] + [loss_part.reshape(1)]
    packed = _tie(_pack([dmods] + small_parts), comm.jobs["rs0_mix"].token)
    gathered = _allgather8(packed, "gather_small")
    dmod_all = gathered.reshape(N_DEV, -1)[:, :dmods.size].reshape(N_DEV, DEPTH, 6 * D)
    summed = _unpack(_sum_devices(gathered), [dmods] + small_parts)
    grad_b_ada = summed[0].reshape(DEPTH, 6 * D)
    loss = summed[-1][0]
    small_grads = {name: jnp.stack(summed[1 + 2 * i:3 + 2 * i]) for i, name in enumerate(_SMALL)}
    small_grads["conv_w"] = lax.dynamic_slice_in_dim(small_grads["conv_w"], chip * (CONV_WIDTH // N_CHIPS), CONV_WIDTH // N_CHIPS, axis=2)

    dmod_loc = lax.dynamic_slice_in_dim(dmod_all.transpose(1, 0, 2), chip * n_ada, n_ada, axis=2)
    tail_token = comm.finish_sums(grad_b_ada)
    silu_pad = _tie(jnp.pad(silu_c, ((0, 128 - N_DEV), (0, 0))), tail_token)
    dmod_pad = jnp.pad(dmod_loc.transpose(1, 0, 2).reshape(N_DEV, DEPTH * n_ada), ((0, 128 - N_DEV), (0, 0)))
    grad_w_ada = _mm(silu_pad, dmod_pad, ta=True, out_split=DEPTH, name="mm_ada_dw")

    grads = dict(w_ada=grad_w_ada, b_ada=grad_b_ada, **small_grads)
    weights = dict(w_ada=w_ada, b_ada=b_ada, g_mix_pre=g_mix_pre, g_mix_post=g_mix_post, g_ff_pre=g_ff_pre, g_ff_post=g_ff_post, w_in=w_in,
                   b_f=b_f, w_pool=w_pool, pool_scale=pool_scale, conv_w=conv_w, w_branch=w_branch, w_out=w_out, w_ff1=w_ff1, w_ff2=w_ff2)
    m_in = dict(w_ada=m_w_ada, b_ada=m_b_ada, g_mix_pre=m_g_mix_pre, g_mix_post=m_g_mix_post, g_ff_pre=m_g_ff_pre, g_ff_post=m_g_ff_post,
                w_in=m_w_in, b_f=m_b_f, w_pool=m_w_pool, pool_scale=m_pool_scale, conv_w=m_conv_w, w_branch=m_w_branch, w_out=m_w_out,
                w_ff1=m_w_ff1, w_ff2=m_w_ff2)
    v_in = dict(w_ada=v_w_ada, b_ada=v_b_ada, g_mix_pre=v_g_mix_pre, g_mix_post=v_g_mix_post, g_ff_pre=v_g_ff_pre, g_ff_post=v_g_ff_post,
                w_in=v_w_in, b_f=v_b_f, w_pool=v_w_pool, pool_scale=v_pool_scale, conv_w=v_conv_w, w_branch=v_w_branch, w_out=v_w_out,
                w_ff1=v_w_ff1, w_ff2=v_w_ff2)
    order = ("w_ada", "b_ada", "g_mix_pre", "g_mix_post", "g_ff_pre", "g_ff_post", "w_in", "b_f", "w_pool", "pool_scale", "conv_w",
             "w_branch", "w_out", "w_ff1", "w_ff2")
    delta, new_m, new_v = {}, {}, {}
    tiny = ("b_ada",) + _SMALL
    packed_w, packed_g, packed_m, packed_v = [_pack([src[name] for name in tiny]) for src in (weights, grads, m_in, v_in)]
    res = _adamw(packed_w, _tie(packed_g, tail_token), packed_m, packed_v, "adamw_small")
    for out, packed_res in zip((delta, new_m, new_v), res):
        for name, val in zip(tiny, _unpack(packed_res, [weights[name] for name in tiny])):
            out[name] = val
    delta["w_ada"], new_m["w_ada"], new_v["w_ada"] = _adamw(w_ada, grad_w_ada, m_w_ada, v_w_ada, "adamw_w_ada")
    comm.finish(delta["w_ada"][0, :8, :128] + delta["b_ada"][0, :128])
    for name in _BIG:
        grads[name] = jnp.stack([comm.grads[l][name] for l in range(DEPTH)])
        if name == "w_in":
            g_view = lax.optimization_barrier(_w_in_view(grads[name]))
            res = _adamw(_w_in_view(w_in), g_view, _w_in_view(m_w_in), _w_in_view(v_w_in), "adamw_w_in")
            grads[name], delta[name], new_m[name], new_v[name] = [_w_in_unview(t) for t in (g_view, *res)]
        else:
            delta[name], new_m[name], new_v[name] = _adamw(weights[name], grads[name], m_in[name], v_in[name], "adamw_" + name)

    return (loss, grad_x[None], *[grads[n] for n in order], *[delta[n] for n in order], *[new_m[n] for n in order],
            *[new_v[n] for n in order])
```

```python
import functools

import jax
import jax.numpy as jnp
from jax import lax
from jax.experimental import pallas as pl
from jax.experimental.pallas import tpu as pltpu

F32 = jnp.float32
BF16 = jnp.bfloat16
MESH = pl.DeviceIdType.MESH

D = 1024
DEPTH = 2
HEADS = 8
HEAD_DIM = 64
A_WIDTH = 512
POOL_WIDTH = 256
CONV_WIDTH = 256
D_FF = 4096
IN_COLS = 5640
Z_GL, Z_QKV, Z_PC, Z_FL, Z_COLS = 0, 3072, 4608, 5632, 5760
RMS_EPS = 1e-6
NEG_INF = -1e30
ROW_TILE = 256
N_CHIPS = 4
N_DEV = 8
V7X_VMEM_LIMIT = 48 * 1024 * 1024

ADAM_LR = 0.001
ADAM_B1 = 0.9
ADAM_B2 = 0.999
ADAM_EPS = 1e-08
ADAM_WD = 0.01
ADAM_STEP = 10

_HBM = pl.BlockSpec(memory_space=pltpu.HBM)


def _params(*sem):
    return pltpu.CompilerParams(dimension_semantics=sem, vmem_limit_bytes=V7X_VMEM_LIMIT)


def _pick(dim, cands):
    for cand in cands:
        if dim % cand == 0:
            return cand
    return dim


def _mm(a, b, *, ta=False, tb=False, b_rows=None, b_split=1, out_split=1, out_dtype=F32, epilogue=None, extras=(), name):
    (k, m) = a.shape if ta else a.shape[::-1]
    b_row0, b_rows = (0, b.shape[-2]) if b_rows is None else b_rows
    b_cols = b.shape[-1] * b_split
    (n, k2) = (b_rows, b_cols) if tb else (b_cols, b_rows)
    assert k == k2, (a.shape, b.shape, ta, tb)
    n_unit = n // (out_split * (1 if tb else b_split))
    k_unit = k // (b_split if tb else 1)
    tm = _pick(m, (1024, 512, 256, 128))
    tn = _pick(n_unit, (1024, 1152, 768, 640, 512, 256, 128))
    tk = _pick(k_unit, (1024, 1152, 512, 640, 256, 128))
    nk = k // tk
    dims = (((0 if ta else 1,), (1 if tb else 0,)), ((), ()))

    def dot(a_ref, b_ref):
        b_val = b_ref[0] if b_split > 1 else b_ref[...]
        return lax.dot_general(a_ref[...].astype(BF16), b_val.astype(BF16), dims, preferred_element_type=F32)

    n_extra = len(extras)
    assert epilogue is None or out_split == 1

    def put(refs, val):
        if epilogue is not None:
            for o_ref, res in zip(refs[n_extra:], epilogue(val, *[r[...] for r in refs[:n_extra]])):
                o_ref[...] = res.astype(o_ref.dtype)
        elif out_split > 1:
            refs[0][0] = val.astype(refs[0].dtype)
        else:
            refs[0][...] = val.astype(refs[0].dtype)

    def body_single(a_ref, b_ref, *refs):
        put(refs, dot(a_ref, b_ref))

    def body_acc(a_ref, b_ref, *refs):
        kk = pl.program_id(2)
        acc_ref = refs[-1]

        @pl.when(kk == 0)
        def _():
            acc_ref[...] = jnp.zeros_like(acc_ref)

        acc_ref[...] += dot(a_ref, b_ref)

        @pl.when(kk == nk - 1)
        def _():
            put(refs[:-1], acc_ref[...])

    a_spec = pl.BlockSpec((tk, tm), lambda i, j, kk: (kk, i)) if ta else pl.BlockSpec((tm, tk), lambda i, j, kk: (i, kk))
    if b_split == 1:
        off = b_row0 // (tn if tb else tk)
        assert off * (tn if tb else tk) == b_row0
        b_spec = pl.BlockSpec((tn, tk), lambda i, j, kk: (j + off, kk)) if tb else pl.BlockSpec((tk, tn), lambda i, j, kk: (kk + off, j))
    elif tb:
        per = k_unit // tk
        b_spec = pl.BlockSpec((1, tn, tk), lambda i, j, kk: (kk // per, j, kk % per))
    else:
        per = n // b_split // tn
        b_spec = pl.BlockSpec((1, tk, tn), lambda i, j, kk: (j // per, kk, j % per))
    if out_split == 1:
        o_spec = pl.BlockSpec((tm, tn), lambda i, j, kk: (i, j))
        o_shape = None if epilogue is not None else jax.ShapeDtypeStruct((m, n), out_dtype)
    else:
        per_o = n // out_split // tn
        o_spec = pl.BlockSpec((1, tm, tn), lambda i, j, kk: (j // per_o, i, j % per_o))
        o_shape = jax.ShapeDtypeStruct((out_split, m, n // out_split), out_dtype)
    if epilogue is not None:
        o_shape = [jax.ShapeDtypeStruct((m, n), dt) for dt in out_dtype]
        o_spec = [o_spec] * len(out_dtype)
    return pl.pallas_call(
        body_single if nk == 1 else body_acc, name=name, grid=(m // tm, n // tn, nk),
        in_specs=[a_spec, b_spec] + [pl.BlockSpec((tm, tn), lambda i, j, kk: (i, j))] * n_extra, out_specs=o_spec, out_shape=o_shape,
        scratch_shapes=[] if nk == 1 else [pltpu.VMEM((tm, tn), F32)],
        compiler_params=_params("parallel", "parallel", "arbitrary"),
    )(a, b, *extras)


def _ew(fn, ins, out_dtypes, name, tc=None):
    shape = ins[0].shape
    lead, (rows, cols) = shape[:-2], shape[-2:]
    tc = cols if tc is None else tc
    tr = _pick(rows, (ROW_TILE, 128, 8)) if tc > 128 else _pick(rows, (4096, 2256, 2048, 1024, ROW_TILE, 8))
    n_in = len(ins)

    def body(*refs):
        res = fn(*[r[...] for r in refs[:n_in]])
        for o_ref, val in zip(refs[n_in:], res):
            o_ref[...] = val.astype(o_ref.dtype)

    if lead:
        spec = pl.BlockSpec((None, tr, tc), lambda l, i, j: (l, i, j))
    else:
        spec = pl.BlockSpec((tr, tc), lambda i, j: (i, j))
    return pl.pallas_call(
        body, name=name, grid=lead + (rows // tr, cols // tc),
        in_specs=[spec] * n_in, out_specs=[spec] * len(out_dtypes),
        out_shape=[jax.ShapeDtypeStruct(shape, dt) for dt in out_dtypes],
        compiler_params=_params(*(["parallel"] * (len(lead) + 2))),
    )(*ins)


def _relu2_fwd(a):
    r = jnp.maximum(a, 0.0)
    return a, r * r


def _relu2_bwd(dr, a):
    return (dr * (2.0 * jnp.maximum(a, 0.0)),)


def _adamw(w, g, m, v, name):
    bc1 = 1.0 - ADAM_B1 ** ADAM_STEP
    bc2 = 1.0 - ADAM_B2 ** ADAM_STEP

    def fn(w, g, m, v):
        m = ADAM_B1 * m + (1.0 - ADAM_B1) * g
        v = ADAM_B2 * v + (1.0 - ADAM_B2) * (g * g)
        m_hat = m / bc1
        v_hat = v / bc2
        delta = -ADAM_LR * (m_hat / (jnp.sqrt(v_hat) + ADAM_EPS) + ADAM_WD * w)
        return delta, m, v
    return _ew(fn, [w, g, m, v], [F32, F32, F32], name)


def _row_spec(cols, block=0):
    return pl.BlockSpec((ROW_TILE, cols), lambda i, block=block: (i, block))


def _vec_spec(cols):
    return pl.BlockSpec((1, cols), lambda i: (0, 0))


def _vec_args(*vecs):
    arrays = [v[0] if isinstance(v, tuple) else v for v in vecs]
    specs = [pl.BlockSpec((None, 1, D), lambda i, row=v[1]: (row, 0, 0)) if isinstance(v, tuple) else _vec_spec(D) for v in vecs]
    return arrays, specs


def _sum_spec(cols):
    return pl.BlockSpec((8, cols), lambda i: (0, 0))


def _rstd(x):
    return lax.rsqrt(jnp.mean(x * x, axis=-1, keepdims=True) + RMS_EPS)


def _modnorm_fwd(x, g, shift, scale, name):
    s = x.shape[0]

    def body(x_ref, g_ref, sh_ref, sc_ref, h_ref):
        xv = x_ref[...]
        n = xv * _rstd(xv)
        h_ref[...] = ((n * g_ref[...]) * (1.0 + sc_ref[...]) + sh_ref[...]).astype(BF16)

    vecs, vec_specs = _vec_args(g, shift, scale)
    return pl.pallas_call(
        body, name=name, grid=(s // ROW_TILE,),
        in_specs=[_row_spec(D)] + vec_specs, out_specs=_row_spec(D),
        out_shape=jax.ShapeDtypeStruct((s, D), BF16), compiler_params=_params("parallel"),
    )(x, *vecs)


def _post_fwd(x, y, g, gate, name):
    s = x.shape[0]

    def body(x_ref, y_ref, g_ref, gate_ref, o_ref):
        yv = y_ref[...]
        o_ref[...] = x_ref[...] + gate_ref[...] * ((yv * _rstd(yv)) * g_ref[...])

    vecs, vec_specs = _vec_args(g, gate)
    return pl.pallas_call(
        body, name=name, grid=(s // ROW_TILE,),
        in_specs=[_row_spec(D), _row_spec(D)] + vec_specs, out_specs=_row_spec(D),
        out_shape=jax.ShapeDtypeStruct((s, D), F32), compiler_params=_params("parallel"),
    )(x, y, *vecs)


def _post_bwd(dxo, y, g, gate, name):
    s = dxo.shape[0]

    def body(d_ref, y_ref, g_ref, gate_ref, dy_ref, sum_ref):
        @pl.when(pl.program_id(0) == 0)
        def _():
            sum_ref[...] = jnp.zeros_like(sum_ref)

        dv, yv = d_ref[...], y_ref[...]
        r = _rstd(yv)
        n = yv * r
        sum_ref[0:1, :] += jnp.sum(dv * (n * g_ref[...]), axis=0, keepdims=True)
        sum_ref[1:2, :] += jnp.sum((dv * gate_ref[...]) * n, axis=0, keepdims=True)
        dn = (dv * gate_ref[...]) * g_ref[...]
        dy_ref[...] = (r * (dn - n * jnp.mean(dn * n, axis=-1, keepdims=True))).astype(BF16)

    vecs, vec_specs = _vec_args(g, gate)
    return pl.pallas_call(
        body, name=name, grid=(s // ROW_TILE,),
        in_specs=[_row_spec(D), _row_spec(D)] + vec_specs,
        out_specs=[_row_spec(D), _sum_spec(D)],
        out_shape=[jax.ShapeDtypeStruct((s, D), BF16), jax.ShapeDtypeStruct((8, D), F32)],
        compiler_params=_params("arbitrary"),
    )(dxo, y, *vecs)


def _modnorm_bwd(dh, x, dxo, g, scale, name):
    s = dh.shape[0]

    def body(dh_ref, x_ref, d_ref, g_ref, sc_ref, dx_ref, sum_ref):
        @pl.when(pl.program_id(0) == 0)
        def _():
            sum_ref[...] = jnp.zeros_like(sum_ref)

        dhv, xv = dh_ref[...], x_ref[...]
        r = _rstd(xv)
        n = xv * r
        one_sc = 1.0 + sc_ref[...]
        sum_ref[0:1, :] += jnp.sum(dhv, axis=0, keepdims=True)
        sum_ref[1:2, :] += jnp.sum(dhv * (n * g_ref[...]), axis=0, keepdims=True)
        sum_ref[2:3, :] += jnp.sum((dhv * one_sc) * n, axis=0, keepdims=True)
        dn = (dhv * one_sc) * g_ref[...]
        dx_ref[...] = d_ref[...] + r * (dn - n * jnp.mean(dn * n, axis=-1, keepdims=True))

    vecs, vec_specs = _vec_args(g, scale)
    return pl.pallas_call(
        body, name=name, grid=(s // ROW_TILE,),
        in_specs=[_row_spec(D), _row_spec(D), _row_spec(D)] + vec_specs,
        out_specs=[_row_spec(D), _sum_spec(D)],
        out_shape=[jax.ShapeDtypeStruct((s, D), F32), jax.ShapeDtypeStruct((8, D), F32)],
        compiler_params=_params("arbitrary"),
    )(dh, x, dxo, *vecs)


def _loss_head(y, target):
    s = y.shape[0]

    def body(y_ref, t_ref, dy_ref, sum_ref):
        @pl.when(pl.program_id(0) == 0)
        def _():
            sum_ref[...] = jnp.zeros_like(sum_ref)

        err = y_ref[...] - t_ref[...]
        dy_ref[...] = err * (1.0 / D)
        sum_ref[...] += jnp.sum(err * err)

    return pl.pallas_call(
        body, name="loss_head", grid=(s // ROW_TILE,),
        in_specs=[_row_spec(D), _row_spec(D)],
        out_specs=[_row_spec(D), pl.BlockSpec((8, 128), lambda i: (0, 0))],
        out_shape=[jax.ShapeDtypeStruct((s, D), F32), jax.ShapeDtypeStruct((8, 128), F32)],
        compiler_params=_params("arbitrary"),
    )(y, target)


def _merge_fwd(z, pa, pb, pc):
    s = z.shape[0]

    def body(g0_ref, g1_ref, g2_ref, pa_ref, pb_ref, pc_ref, o_ref):
        o_ref[...] = (jax.nn.sigmoid(g0_ref[...]) * pa_ref[...] + jax.nn.sigmoid(g1_ref[...]) * pb_ref[...]
                      + jax.nn.sigmoid(g2_ref[...]) * pc_ref[...]).astype(BF16)

    return pl.pallas_call(
        body, name="merge_fwd", grid=(s // ROW_TILE,),
        in_specs=[_row_spec(D, 0), _row_spec(D, 1), _row_spec(D, 2), _row_spec(D), _row_spec(D), _row_spec(D)],
        out_specs=_row_spec(D), out_shape=jax.ShapeDtypeStruct((s, D), BF16),
        compiler_params=_params("parallel"),
    )(z, z, z, pa, pb, pc)


def _merge_bwd(dm, z, pa, pb, pc):
    s = z.shape[0]

    def body(dm_ref, g0_ref, g1_ref, g2_ref, pa_ref, pb_ref, pc_ref, dgl_ref, da_ref, db_ref, dc_ref):
        dmv = dm_ref[...]
        for i, (g_ref, p_ref, d_ref) in enumerate(((g0_ref, pa_ref, da_ref), (g1_ref, pb_ref, db_ref), (g2_ref, pc_ref, dc_ref))):
            gate = jax.nn.sigmoid(g_ref[...])
            dgl_ref[:, i * D:(i + 1) * D] = ((dmv * p_ref[...]) * (gate * (1.0 - gate))).astype(BF16)
            d_ref[...] = (dmv * gate).astype(BF16)

    return pl.pallas_call(
        body, name="merge_bwd", grid=(s // ROW_TILE,),
        in_specs=[_row_spec(D), _row_spec(D, 0), _row_spec(D, 1), _row_spec(D, 2), _row_spec(D), _row_spec(D), _row_spec(D)],
        out_specs=[_row_spec(3 * D), _row_spec(D), _row_spec(D), _row_spec(D)],
        out_shape=[jax.ShapeDtypeStruct((s, Z_COLS), BF16)] + [jax.ShapeDtypeStruct((s, D), BF16)] * 3,
        compiler_params=_params("parallel"),
    )(dm, z, z, z, pa, pb, pc)


def _shift_down(v, n):
    row = lax.broadcasted_iota(jnp.int32, v.shape, 0)
    return jnp.where(row >= n, pltpu.roll(v, n, axis=0), 0.0)


def _shift_up(v, n):
    s = v.shape[0]
    row = lax.broadcasted_iota(jnp.int32, v.shape, 0)
    return jnp.where(row < s - n, pltpu.roll(v, s - n, axis=0), 0.0)


def _log_sigmoid(v):
    return jnp.minimum(v, 0.0) - jnp.log1p(jnp.exp(-jnp.abs(v)))


def _cumf_fwd(fl, bias):
    s = fl.shape[0]

    def body(fl_ref, b_ref, o_ref):
        acc = _log_sigmoid(fl_ref[...] + b_ref[...])
        step = 1
        while step < s:
            acc = acc + _shift_down(acc, step)
            step *= 2
        o_ref[...] = acc

    return pl.pallas_call(body, name="cumf_fwd", out_shape=jax.ShapeDtypeStruct((s, 128), F32),
                          compiler_params=pltpu.CompilerParams(vmem_limit_bytes=V7X_VMEM_LIMIT))(fl, bias)


def _cumf_bwd(dcum, fl, bias):
    s = fl.shape[0]

    def body(d_ref, fl_ref, b_ref, dfl_ref, db_ref):
        acc = d_ref[...]
        step = 1
        while step < s:
            acc = acc + _shift_up(acc, step)
            step *= 2
        dfl = acc * jax.nn.sigmoid(-(fl_ref[...] + b_ref[...]))
        dfl_ref[...] = dfl.astype(BF16)
        db_ref[...] = jnp.broadcast_to(jnp.sum(dfl, axis=0, keepdims=True), (8, 128))

    return pl.pallas_call(
        body, name="cumf_bwd",
        out_shape=[jax.ShapeDtypeStruct((s, 128), BF16), jax.ShapeDtypeStruct((8, 128), F32)],
        compiler_params=pltpu.CompilerParams(vmem_limit_bytes=V7X_VMEM_LIMIT))(dcum, fl, bias)


def _pool_windows(v, shift):
    s2 = v + shift(v, 1)
    s4 = s2 + shift(s2, 2)
    s8 = s4 + shift(s4, 4)
    s16 = s8 + shift(s8, 8)
    group = lax.broadcasted_iota(jnp.int32, v.shape, 1) // 64
    return jnp.where(group == 0, s2, jnp.where(group == 1, s4, jnp.where(group == 2, s8, s16)))


def _pool_count(shape):
    group = lax.broadcasted_iota(jnp.int32, shape, 1) // 64
    window = jnp.where(group == 0, 2.0, jnp.where(group == 1, 4.0, jnp.where(group == 2, 8.0, 16.0)))
    t1 = (lax.broadcasted_iota(jnp.int32, shape, 0) + 1).astype(F32)
    return jnp.minimum(t1, window)


def _pc_specs(s):
    zcol = lambda blk: pl.BlockSpec((s, 256), lambda i, blk=blk: (0, blk))
    first = Z_PC // 256
    return [zcol(first), zcol(first + 1), zcol(first + 2), zcol(first + 3),
            pl.BlockSpec((256, 256), lambda i: (0, 0)), pl.BlockSpec((1, 256), lambda i: (0, 0)),
            pl.BlockSpec((3, 256), lambda i: (0, 0))]


def _poolconv_fwd(z, wbd, pscale, convw):
    s = z.shape[0]

    def body(pu_ref, ch_ref, cb_ref, cc_ref, w_ref, ps_ref, cw_ref, yb_ref, yc_ref):
        u = pu_ref[...]
        p = _pool_windows(u, _shift_down) / _pool_count(u.shape) - u
        yb = jnp.dot(p.astype(BF16), w_ref[...].astype(BF16), preferred_element_type=F32) * ps_ref[...]
        yb_ref[...] = yb.astype(BF16)
        uc = cc_ref[...] * ch_ref[...]
        cw = cw_ref[...]
        conv = cw[0:1, :] * _shift_down(uc, 2) + cw[1:2, :] * _shift_down(uc, 1) + cw[2:3, :] * uc
        yc_ref[...] = (cb_ref[...] * conv).astype(BF16)

    out = pl.BlockSpec((s, 256), lambda i: (0, 0))
    return pl.pallas_call(
        body, name="poolconv_fwd", grid=(1,), in_specs=_pc_specs(s), out_specs=[out, out],
        out_shape=[jax.ShapeDtypeStruct((s, 256), BF16)] * 2, compiler_params=_params("arbitrary"),
    )(z, z, z, z, wbd, pscale, convw)


def _poolconv_bwd(dyb, dyc, z, wbd, pscale, convw):
    s = z.shape[0]

    def body(dyb_ref, dyc_ref, pu_ref, ch_ref, cb_ref, cc_ref, w_ref, ps_ref, cw_ref, dz_ref, dw_ref, dps_ref, dcw_ref):
        u = pu_ref[...]
        count = _pool_count(u.shape)
        p = (_pool_windows(u, _shift_down) / count - u).astype(BF16)
        wb = w_ref[...].astype(BF16)
        dyb_v = dyb_ref[...]
        pw = jnp.dot(p, wb, preferred_element_type=F32)
        dps_ref[...] = jnp.broadcast_to(jnp.sum(dyb_v * pw, axis=0, keepdims=True), (8, 256))
        dys = (dyb_v * ps_ref[...]).astype(BF16)
        dp = lax.dot_general(dys, wb, (((1,), (1,)), ((), ())), preferred_element_type=F32)
        dw_ref[...] = lax.dot_general(p, dys, (((0,), (0,)), ((), ())), preferred_element_type=F32)
        dz_ref[:, 0:256] = (_pool_windows(dp / count, _shift_up) - dp).astype(BF16)

        ch, cb, cc = ch_ref[...], cb_ref[...], cc_ref[...]
        uc = cc * ch
        cw = cw_ref[...]
        u2, u1 = _shift_down(uc, 2), _shift_down(uc, 1)
        conv = cw[0:1, :] * u2 + cw[1:2, :] * u1 + cw[2:3, :] * uc
        dyc_v = dyc_ref[...]
        dconv = dyc_v * cb
        du = cw[0:1, :] * _shift_up(dconv, 2) + cw[1:2, :] * _shift_up(dconv, 1) + cw[2:3, :] * dconv
        dz_ref[:, 256:512] = (du * cc).astype(BF16)
        dz_ref[:, 512:768] = (dyc_v * conv).astype(BF16)
        dz_ref[:, 768:1024] = (du * ch).astype(BF16)
        dcw_ref[...] = jnp.zeros_like(dcw_ref)
        dcw_ref[0:1, :] = jnp.sum(dconv * u2, axis=0, keepdims=True)
        dcw_ref[1:2, :] = jnp.sum(dconv * u1, axis=0, keepdims=True)
        dcw_ref[2:3, :] = jnp.sum(dconv * uc, axis=0, keepdims=True)

    blk = lambda r, c: pl.BlockSpec((r, c), lambda i: (0, 0))
    return pl.pallas_call(
        body, name="poolconv_bwd", grid=(1,),
        in_specs=[blk(s, 256), blk(s, 256)] + _pc_specs(s),
        out_specs=[blk(s, 1024), blk(256, 256), blk(8, 256), blk(8, 256)],
        out_shape=[jax.ShapeDtypeStruct((s, 1024), BF16), jax.ShapeDtypeStruct((256, 256), F32),
                   jax.ShapeDtypeStruct((8, 256), F32), jax.ShapeDtypeStruct((8, 256), F32)],
        compiler_params=_params("arbitrary"),
    )(dyb, dyc, z, z, z, z, wbd, pscale, convw)


_NT = (((1,), (1,)), ((), ()))
_TN = (((0,), (0,)), ((), ()))


ATT_Q, ATT_K = 256, 256
ATT_HEADS_BWD = 8
ATT_HEADS = 8


def _att_logits(q, k, fr, q0, k0, masked):
    logits = lax.dot_general(q, k, _NT, preferred_element_type=F32) - fr
    if not masked:
        return logits
    row = q0 + lax.broadcasted_iota(jnp.int32, logits.shape, 0)
    col = k0 + lax.broadcasted_iota(jnp.int32, logits.shape, 1)
    return jnp.where(row >= col, logits, NEG_INF)


def _causal_sweep(step, qi, init):
    n_full = (qi * ATT_Q) // ATT_K
    carry = lax.fori_loop(0, n_full, lambda j, carry: step(j, carry, False), init)
    return step(n_full, carry, True)


HEAD_PAIRS = HEADS // 2


def _lane_pick(v, lane, idx):
    return jnp.sum(jnp.where(lane == idx, v, 0.0), axis=-1, keepdims=True)


def _lane_put(lane, idx, col):
    return jnp.where(lane == idx, col, 0.0)


def _split_heads(v, low):
    zero = jnp.zeros_like(v)
    return jnp.where(low, v, zero), jnp.where(low, zero, v)


def _attn_fwd(qkv, fr):
    s = qkv.shape[0]
    nk = s // ATT_K
    width = ATT_HEADS * HEAD_DIM
    groups = HEADS // ATT_HEADS

    def body(q_ref, k_ref, v_ref, fr_ref, o_ref, lse_ref):
        qi, grp = pl.program_id(0), pl.program_id(1)
        lane = lax.broadcasted_iota(jnp.int32, (ATT_Q, 128), 1)
        low = lane < HEAD_DIM
        qs = []
        for pr in range(ATT_HEADS // 2):
            qs += _split_heads(q_ref[:, 128 * pr:128 * (pr + 1)] * (HEAD_DIM ** -0.5), low)

        def step(j, carry, masked):
            k0 = pl.multiple_of(j * ATT_K, ATT_K)
            out = []
            for h in range(ATT_HEADS):
                cols = slice(128 * (h // 2), 128 * (h // 2 + 1))
                m, l, acc = carry[h]
                logits = _att_logits(qs[h], k_ref[pl.ds(k0, ATT_K), cols], fr_ref[h, pl.ds(j, 1), :], qi * ATT_Q, k0, masked)
                m_new = jnp.maximum(m, jnp.max(logits, axis=-1, keepdims=True))
                p = jnp.exp(logits - m_new)
                alpha = jnp.exp(m - m_new)
                l = alpha * l + jnp.sum(p, axis=-1, keepdims=True)
                acc = alpha * acc + jnp.dot(p.astype(BF16), v_ref[pl.ds(k0, ATT_K), cols], preferred_element_type=F32)
                out.append((m_new, l, acc))
            return tuple(out)

        one = (jnp.full((ATT_Q, 1), NEG_INF, F32), jnp.zeros((ATT_Q, 1), F32), jnp.zeros((ATT_Q, 128), F32))
        done = _causal_sweep(step, qi, (one,) * ATT_HEADS)

        @pl.when(grp == 0)
        def _():
            lse_ref[...] = jnp.zeros_like(lse_ref)

        lse = jnp.zeros((ATT_Q, 128), F32)
        for pr in range(ATT_HEADS // 2):
            (m0, l0, acc0), (m1, l1, acc1) = done[2 * pr], done[2 * pr + 1]
            o_ref[:, 128 * pr:128 * (pr + 1)] = jnp.where(low, acc0 / l0, acc1 / l1)
            head = ATT_HEADS * grp + 2 * pr
            lse = lse + _lane_put(lane, head, m0 + jnp.log(l0)) + _lane_put(lane, head + 1, m1 + jnp.log(l1))
        lse_ref[...] += lse

    return pl.pallas_call(
        body, name="attn_fwd", grid=(s // ATT_Q, groups),
        in_specs=[pl.BlockSpec((ATT_Q, width), lambda i, g: (i, g)),
                  pl.BlockSpec((s, width), lambda i, g: (0, groups + g)),
                  pl.BlockSpec((s, width), lambda i, g: (0, 2 * groups + g)),
                  pl.BlockSpec((ATT_HEADS, nk, ATT_K), lambda i, g: (g, 0, 0))],
        out_specs=[pl.BlockSpec((ATT_Q, width), lambda i, g: (i, g)), pl.BlockSpec((ATT_Q, 128), lambda i, g: (i, 0))],
        out_shape=[jax.ShapeDtypeStruct((s, A_WIDTH), F32), jax.ShapeDtypeStruct((s, 128), F32)],
        compiler_params=_params("parallel", "arbitrary"),
    )(qkv, qkv, qkv, fr)


def _attn_bwd(qkv, do, o, lse, fr):
    s = qkv.shape[0]
    nk = s // ATT_K
    scale = HEAD_DIM ** -0.5
    heads = ATT_HEADS_BWD
    width = heads * HEAD_DIM
    groups = HEADS // heads

    def body(q_ref, k_ref, v_ref, do_ref, o_ref, lse_ref, fr_ref, dq_ref, dk_ref, dv_ref, dfc_ref, dfr_ref, dk_acc, dv_acc):
        grp = pl.program_id(0)
        lane = lax.broadcasted_iota(jnp.int32, (ATT_Q, 128), 1)
        low = lane < HEAD_DIM
        low_t = lax.broadcasted_iota(jnp.int32, (128, ATT_Q), 0) < HEAD_DIM
        dk_acc[...] = jnp.zeros_like(dk_acc)
        dv_acc[...] = jnp.zeros_like(dv_acc)
        dfr_ref[...] = jnp.zeros_like(dfr_ref)

        @pl.when(grp == 0)
        def _():
            dfc_ref[...] = jnp.zeros_like(dfc_ref)

        def outer(i, carry):
            q0 = pl.multiple_of(i * ATT_Q, ATT_Q)
            rows = pl.ds(q0, ATT_Q)
            lsev = lse_ref[rows, :]
            qts, dots, qs, dos, deltas, lses = [], [], [], [], [], []
            for pr in range(heads // 2):
                pcols = slice(128 * pr, 128 * (pr + 1))
                q2, do2 = q_ref[rows, pcols] * scale, do_ref[rows, pcols]
                prod = do2 * o_ref[rows, pcols]
                deltas += [jnp.sum(jnp.where(low, prod, 0.0), axis=-1, keepdims=True),
                           jnp.sum(jnp.where(low, 0.0, prod), axis=-1, keepdims=True)]
                dob2 = do2.astype(BF16)
                qts += _split_heads(q2.astype(F32).T.astype(BF16), low_t)
                dots += _split_heads(do2.T.astype(BF16), low_t)
                qs += _split_heads(q2, low)
                dos += _split_heads(dob2, low)
                lses += [_lane_pick(lsev, lane, heads * grp + 2 * pr), _lane_pick(lsev, lane, heads * grp + 2 * pr + 1)]

            def inner(j, carry, masked):
                k0 = pl.multiple_of(j * ATT_K, ATT_K)
                krows = pl.ds(k0, ATT_K)
                out, dkt, dvt = [], [], []
                for h in range(heads):
                    pcols = slice(128 * (h // 2), 128 * (h // 2 + 1))
                    dq, dfc = carry[h]
                    k2 = k_ref[krows, pcols]
                    p = jnp.exp(_att_logits(qs[h], k2, fr_ref[h, pl.ds(j, 1), :], q0, k0, masked) - lses[h])
                    dp = lax.dot_general(dos[h], v_ref[krows, pcols], _NT, preferred_element_type=F32)
                    ds = p * (dp - deltas[h])
                    dsb = ds.astype(BF16)
                    dkt.append(jnp.dot(qts[h], dsb, preferred_element_type=F32))
                    dvt.append(jnp.dot(dots[h], p.astype(BF16), preferred_element_type=F32))
                    dfr_ref[h, pl.ds(j, 1), :] -= jnp.sum(ds, axis=0, keepdims=True)
                    out.append((dq + jnp.dot(dsb, k2, preferred_element_type=F32), dfc + (ds[:, :128] + ds[:, 128:])))
                for pr in range(heads // 2):
                    prows = slice(128 * pr, 128 * (pr + 1))
                    dk_acc[j, prows, :] += dkt[2 * pr] + dkt[2 * pr + 1]
                    dv_acc[j, prows, :] += dvt[2 * pr] + dvt[2 * pr + 1]
                return tuple(out)

            one = (jnp.zeros((ATT_Q, 128), F32), jnp.zeros((ATT_Q, 128), F32))
            done = _causal_sweep(inner, i, (one,) * heads)
            dfc = jnp.zeros((ATT_Q, 128), F32)
            for pr in range(heads // 2):
                (dq0, dfc0), (dq1, dfc1) = done[2 * pr], done[2 * pr + 1]
                dq_ref[rows, 128 * pr:128 * (pr + 1)] = (jnp.where(low, dq0, dq1) * scale).astype(BF16)
                head = heads * grp + 2 * pr
                dfc = (dfc + _lane_put(lane, head, jnp.sum(dfc0, axis=-1, keepdims=True))
                       + _lane_put(lane, head + 1, jnp.sum(dfc1, axis=-1, keepdims=True)))
            dfc_ref[rows, :] += dfc
            return carry

        lax.fori_loop(0, s // ATT_Q, outer, 0)
        for j in range(nk):
            for pr in range(heads // 2):
                prows, pcols = slice(128 * pr, 128 * (pr + 1)), slice(128 * pr, 128 * (pr + 1))
                dk_ref[ATT_K * j:ATT_K * (j + 1), pcols] = dk_acc[j, prows, :].T.astype(BF16)
                dv_ref[ATT_K * j:ATT_K * (j + 1), pcols] = dv_acc[j, prows, :].T.astype(BF16)

    part = lambda first: pl.BlockSpec((s, width), lambda g, first=first: (0, first + g))
    whole = pl.BlockSpec((s, 128), lambda g: (0, 0))
    rowv = pl.BlockSpec((heads, nk, ATT_K), lambda g: (g, 0, 0))
    return pl.pallas_call(
        body, name="attn_bwd", grid=(groups,),
        in_specs=[part(0), part(groups), part(2 * groups), part(0), part(0), whole, rowv],
        out_specs=[part(0), part(0), part(0), whole, rowv],
        out_shape=[jax.ShapeDtypeStruct((s, A_WIDTH), BF16)] * 3 + [jax.ShapeDtypeStruct((s, 128), F32), jax.ShapeDtypeStruct((HEADS, nk, ATT_K), F32)],
        scratch_shapes=[pltpu.VMEM((nk, width, ATT_K), F32), pltpu.VMEM((nk, width, ATT_K), F32)],
        compiler_params=_params("arbitrary"),
    )(qkv, qkv, qkv, do, o, lse, fr)


def _ada_fwd(c_all, w_ada, b_loc):
    depth, _, n = w_ada.shape
    tn = 512

    def body(c_ref, w_ref, b_ref, o_ref, sc_ref):
        cv = c_ref[...]
        sc = cv * jax.nn.sigmoid(cv)
        sc_ref[...] = sc
        o_ref[0] = jnp.dot(sc.astype(BF16), w_ref[0].astype(BF16), preferred_element_type=F32) + b_ref[0]

    return pl.pallas_call(
        body, name="ada_fwd", grid=(depth, n // tn),
        in_specs=[pl.BlockSpec((N_DEV, D), lambda l, j: (0, 0)), pl.BlockSpec((1, D, tn), lambda l, j: (l, 0, j)),
                  pl.BlockSpec((1, 1, tn), lambda l, j: (l, 0, j))],
        out_specs=[pl.BlockSpec((1, N_DEV, tn), lambda l, j: (l, 0, j)), pl.BlockSpec((N_DEV, D), lambda l, j: (0, 0))],
        out_shape=[jax.ShapeDtypeStruct((depth, N_DEV, n), F32), jax.ShapeDtypeStruct((N_DEV, D), F32)],
        compiler_params=_params("arbitrary", "arbitrary"),
    )(c_all, w_ada, b_loc)


def _sum_devices(gathered):
    n = gathered.shape[1]
    tn = _pick(n, (1408, 1024, 640, 512, 128))

    def body(g_ref, o_ref):
        acc = g_ref[0:8, :]
        for dev in range(1, N_DEV):
            acc = acc + g_ref[8 * dev:8 * dev + 8, :]
        o_ref[...] = acc

    return pl.pallas_call(
        body, name="sum_devices", grid=(n // tn,),
        in_specs=[pl.BlockSpec((8 * N_DEV, tn), lambda j: (0, j))], out_specs=pl.BlockSpec((8, tn), lambda j: (0, j)),
        out_shape=jax.ShapeDtypeStruct((8, n), F32), compiler_params=_params("parallel"),
    )(gathered)


def _place():
    x, y, c = lax.axis_index("x"), lax.axis_index("y"), lax.axis_index("c")
    chips = [(1 - x, y), (x, 1 - y), (1 - x, 1 - y)]
    return x, y, c, chips


def _allgather8(block, name):
    m_per, n = block.shape

    def body(x_ref, out_ref, send_sems, recv_sems, local_sem):
        x, y, c, chips = _place()
        me, sibling = (x, y, c), (x, y, 1 - c)

        def rows(px, py, pc):
            return out_ref.at[pl.ds((4 * px + 2 * py + pc) * m_per, m_per), :]

        def copy(k, blk, to, src=None):
            return pltpu.make_async_remote_copy(
                src_ref=rows(*blk) if src is None else src, dst_ref=rows(*blk),
                send_sem=send_sems.at[k], recv_sem=recv_sems.at[k], device_id=to, device_id_type=MESH)

        mine = pltpu.make_async_copy(x_ref, rows(*me), local_sem)
        mine.start()
        first = [copy(0, me, sibling, src=x_ref)]
        first += [copy(1 + j, me, (*chip, c), src=x_ref) for j, chip in enumerate(chips)]
        for cp in first:
            cp.start()
        passed = [copy(4 + j, (*chip, c), sibling) for j, chip in enumerate(chips)]
        for j, chip in enumerate(chips):
            copy(1 + j, (*chip, c), me).wait_recv()
            passed[j].start()
        copy(0, sibling, me).wait_recv()
        for j, chip in enumerate(chips):
            copy(4 + j, (*chip, 1 - c), me).wait_recv()
        for cp in first + passed:
            cp.wait_send()
        mine.wait()

    return pl.pallas_call(
        body, name=name, out_shape=jax.ShapeDtypeStruct((N_DEV * m_per, n), block.dtype),
        in_specs=[pl.BlockSpec(memory_space=pltpu.VMEM)], out_specs=pl.BlockSpec(memory_space=pltpu.VMEM),
        scratch_shapes=[pltpu.SemaphoreType.DMA((7,)), pltpu.SemaphoreType.DMA((7,)), pltpu.SemaphoreType.DMA],
        compiler_params=pltpu.CompilerParams(vmem_limit_bytes=V7X_VMEM_LIMIT),
    )(block)


_SEM = pl.BlockSpec(memory_space=pltpu.SEMAPHORE)
_DATAFLOW = pltpu.SideEffectType.DATAFLOW_SIDE_EFFECTING


def _plan_copies(plan, refs, send_sems, recv_sems):
    return [pltpu.make_async_remote_copy(src_ref=src, dst_ref=dst, send_sem=send_sems.at[i], recv_sem=recv_sems.at[i],
                                         device_id=to, device_id_type=MESH) for i, (src, dst, to) in enumerate(plan(refs))]


def _copies_start(bufs, plan, n_copies, after, name):
    nb = len(bufs)

    def body(*refs):
        for cp in _plan_copies(plan, refs[:nb], refs[nb + 1], refs[nb + 2]):
            cp.start()
        token = refs[-1]
        token[...] = jnp.zeros_like(token)

    sem = pltpu.SemaphoreType.DMA((n_copies,))
    outs = pl.pallas_call(
        body, name=name,
        out_shape=(sem, sem, *[pltpu.HBM(b.shape, b.dtype) for b in bufs], jax.ShapeDtypeStruct((8, 128), F32)),
        in_specs=[_HBM] * nb + [pl.BlockSpec(memory_space=pl.ANY)],
        out_specs=(_SEM, _SEM, *[_HBM] * nb, pl.BlockSpec(memory_space=pltpu.VMEM)),
        input_output_aliases={i: 2 + i for i in range(nb)},
        compiler_params=pltpu.CompilerParams(has_side_effects=_DATAFLOW),
    )(*[pltpu.with_memory_space_constraint(b, pltpu.HBM) for b in bufs], after)
    return outs[0], outs[1], list(outs[2:2 + nb]), outs[-1]


def _copies_wait(started, plan, after, name):
    send_sems, recv_sems, bufs, _ = started
    nb = len(bufs)

    def body(*refs):
        for cp in _plan_copies(plan, refs[:nb], refs[nb], refs[nb + 1]):
            cp.wait_send()
            cp.wait_recv()

    return list(pl.pallas_call(
        body, name=name, out_shape=tuple(pltpu.HBM(b.shape, b.dtype) for b in bufs),
        in_specs=[_HBM] * nb + [_SEM, _SEM, pl.BlockSpec(memory_space=pl.ANY)], out_specs=tuple([_HBM] * nb),
        input_output_aliases={i: i for i in range(nb)},
        compiler_params=pltpu.CompilerParams(has_side_effects=_DATAFLOW),
    )(*bufs, send_sems, recv_sems, after))


def _half_rows(ref, axis, c):
    half = ref.shape[axis] // 2
    return pl.ds(c * half, half)


def _plan_gather_ici(refs):
    n = len(refs) // 2
    x, y, c, chips = _place()
    out = []
    for a in range(n):
        rows = _half_rows(refs[a], 0, c)
        out += [(refs[a].at[rows], refs[n + a].at[2 * x + y, rows], (*chip, c)) for chip in chips]
        out.append((refs[a], refs[n + a].at[2 * x + y], (x, y, 1 - c)))
    return out


def _plan_gather_d2d(refs):
    x, y, c, chips = _place()
    out = []
    for ref in refs:
        rows = _half_rows(ref, 1, c)
        for px, py in chips:
            landed = ref.at[2 * px + py, rows]
            out.append((landed, landed, (x, y, 1 - c)))
    return out


def _plan_rs_sibling(refs):
    n = len(refs) // 2
    x, y, c, _ = _place()
    return [(refs[a].at[pl.ds(0, N_CHIPS), _half_rows(refs[a], 1, 1 - c)], refs[n + a], (x, y, 1 - c)) for a in range(n)]


def _plan_rs_chips(refs):
    n = len(refs) // 2
    x, y, c, chips = _place()
    return [(refs[a].at[2 * px + py], refs[n + a].at[k], (px, py, c)) for a in range(n) for k, (px, py) in enumerate(chips)]


def _plan_rs_share(refs):
    x, y, c, _ = _place()
    return [(ref.at[_half_rows(ref, 0, c)], ref.at[_half_rows(ref, 0, c)], (x, y, 1 - c)) for ref in refs]


def _chip_sum(g, other, sel, name):
    _, half, cdim = other.shape
    tr = _pick(half, (512, 256, 128, 64))
    per = half // tr

    def body(sel_ref, g_ref, t_ref, wire_ref, own_ref):
        total = g_ref[0] + t_ref[0]
        wire_ref[0] = total.astype(BF16)

        @pl.when(pl.program_id(1) == sel_ref[1])
        def _():
            own_ref[...] = total

    blk = pl.BlockSpec((1, tr, cdim), lambda i, p, sel_ref: (p, i, 0))
    return pl.pallas_call(
        body, name=name,
        grid_spec=pltpu.PrefetchScalarGridSpec(
            num_scalar_prefetch=1, grid=(per, N_CHIPS),
            in_specs=[pl.BlockSpec((1, tr, cdim), lambda i, p, sel_ref: (p, sel_ref[0] * per + i, 0)), blk],
            out_specs=[blk, pl.BlockSpec((tr, cdim), lambda i, p, sel_ref: (i, 0))]),
        out_shape=[jax.ShapeDtypeStruct(other.shape, BF16), jax.ShapeDtypeStruct((half, cdim), F32)],
        compiler_params=_params("parallel", "arbitrary"),
    )(sel, g, other)


def _final_sum(own, recv, sel, name):
    half, cdim = own.shape
    tr = _pick(half, (512, 256, 128, 64))
    per = half // tr

    def body(sel_ref, own_ref, r0_ref, r1_ref, r2_ref, o_ref):
        o_ref[...] = ((own_ref[...] + r0_ref[0].astype(F32)) + r1_ref[0].astype(F32)) + r2_ref[0].astype(F32)

    part = lambda k: pl.BlockSpec((1, tr, cdim), lambda i, sel_ref, k=k: (k, i, 0))
    return pl.pallas_call(
        body, name=name,
        grid_spec=pltpu.PrefetchScalarGridSpec(
            num_scalar_prefetch=1, grid=(per,),
            in_specs=[pl.BlockSpec((tr, cdim), lambda i, sel_ref: (i, 0)), part(0), part(1), part(2)],
            out_specs=pl.BlockSpec((tr, cdim), lambda i, sel_ref: (sel_ref[0] * per + i, 0))),
        out_shape=jax.ShapeDtypeStruct((2 * half, cdim), F32), compiler_params=_params("parallel"),
    )(sel, own, recv, recv, recv)


def _row(v):
    return v.reshape(1, -1)


_BR_A, _BR_B, _BR_C = (0, A_WIDTH), (A_WIDTH, POOL_WIDTH), (A_WIDTH + POOL_WIDTH, CONV_WIDTH)


def _tie(v, token):
    return v if token is None else v + token[0:1, 0:1]


def _no_hook(point, after, ready=None):
    return None


def _layer_fwd(x, w, mod, hook=_no_hook):
    s = x.shape[0]
    mod3 = mod.reshape(6, 1, D)
    h = _modnorm_fwd(x, _row(w["g_mix_pre"]), (mod3, 0), (mod3, 1), "mix_pre_fwd")
    z = _mm(h, w["w_all"], name="mm_in")
    qkv = z[:, Z_QKV:Z_PC].astype(BF16)
    fl = z[:, Z_FL:Z_COLS]
    cum = _cumf_fwd(fl, w["b_f_pad"])
    fr = cum[:, :HEADS].T.reshape(HEADS, s // ATT_K, ATT_K)
    br_a, lse = _attn_fwd(qkv, fr)
    br_b, br_c = _poolconv_fwd(z, w["w_pool_bd"], _tie(_row(w["pool_scale"]), hook("attn", lse)), w["conv_w"])
    hook("pool", br_b)
    wbr = w["w_branch"]
    pa = _mm(br_a, wbr, b_rows=_BR_A, name="mm_br_a")
    pb = _mm(br_b, wbr, b_rows=_BR_B, name="mm_br_b")
    pc = _mm(br_c, wbr, b_rows=_BR_C, name="mm_br_c")
    merged = _merge_fwd(z, pa, pb, pc)
    y = _mm(merged, w["w_out"], name="mm_out")
    x1 = _post_fwd(x, y, _row(w["g_mix_post"]), (mod3, 2), "mix_post_fwd")
    h2 = _modnorm_fwd(x1, _row(w["g_ff_pre"]), (mod3, 3), (mod3, 4), "ff_pre_fwd")
    a, r = _mm(h2, w["w_ff1"], b_split=N_CHIPS, epilogue=_relu2_fwd, out_dtype=(F32, BF16), name="mm_ff1")
    y2 = _mm(r, w["w_ff2"], name="mm_ff2")
    x2 = _post_fwd(x1, y2, _tie(_row(w["g_ff_post"]), hook("ff_post", y2)), (mod3, 5), "ff_post_fwd")
    hook("end", x2)
    saved = dict(x=x, h=h, z=z, qkv=qkv, fl=fl, fr=fr, lse=lse, br_a=br_a, br_b=br_b, br_c=br_c, pa=pa, pb=pb, pc=pc,
                 merged=merged, y=y, x1=x1, h2=h2, a=a, r=r, y2=y2)
    return x2, saved


def _layer_bwd(dx2, sv, w, mod, hook=_no_hook):
    s = dx2.shape[0]
    mod3 = mod.reshape(6, 1, D)
    dy2, sum_ff_post = _post_bwd(dx2, sv["y2"], _row(w["g_ff_post"]), (mod3, 5), "ff_post_bwd")
    (da,) = _mm(dy2, w["w_ff2"], tb=True, epilogue=_relu2_bwd, extras=(sv["a"],), out_dtype=(BF16,), name="mm_ff2_dx")
    d_w_ff2 = _mm(sv["r"], dy2, ta=True, name="mm_ff2_dw")
    dh2 = _mm(da, w["w_ff1"], tb=True, b_split=N_CHIPS, name="mm_ff1_dx")
    d_w_ff1 = _mm(sv["h2"], da, ta=True, out_split=N_CHIPS, name="mm_ff1_dw")
    dx1, sum_ff_pre = _modnorm_bwd(dh2, sv["x1"], dx2, _tie(_row(w["g_ff_pre"]), hook("ff_pre", dh2, dict(w_ff1=d_w_ff1, w_ff2=d_w_ff2))), (mod3, 4), "ff_pre_bwd")

    dy, sum_mix_post = _post_bwd(dx1, sv["y"], _row(w["g_mix_post"]), (mod3, 2), "mix_post_bwd")
    dmerged = _mm(dy, w["w_out"], tb=True, name="mm_out_dx")
    d_w_out = _mm(sv["merged"], dy, ta=True, name="mm_out_dw")
    dz, dpa, dpb, dpc = _merge_bwd(dmerged, sv["z"], sv["pa"], sv["pb"], sv["pc"])
    wbr = w["w_branch"]
    dbr_a = _mm(dpa, wbr, tb=True, b_rows=_BR_A, name="mm_br_a_dx")
    dbr_b = _mm(dpb, wbr, tb=True, b_rows=_BR_B, name="mm_br_b_dx")
    dbr_c = _mm(dpc, wbr, tb=True, b_rows=_BR_C, name="mm_br_c_dx")
    d_w_branch = jnp.concatenate([_mm(sv["br_a"], dpa, ta=True, name="mm_br_a_dw"), _mm(sv["br_b"], dpb, ta=True, name="mm_br_b_dw"),
                                  _mm(sv["br_c"], dpc, ta=True, name="mm_br_c_dw")], axis=0)

    dq, dk, dv, dfc, dfr = _attn_bwd(sv["qkv"], dbr_a, sv["br_a"], sv["lse"], sv["fr"])
    dcum = dfc + jnp.pad(dfr.reshape(HEADS, s).T, ((0, 0), (0, 128 - HEADS)))
    dfl, sum_bf = _cumf_bwd(dcum, sv["fl"], _tie(w["b_f_pad"], hook("cumf", dfc)))
    dpc_z, d_wbd, sum_ps, sum_cw = _poolconv_bwd(dbr_b, dbr_c, sv["z"], w["w_pool_bd"], _row(w["pool_scale"]), w["conv_w"])
    for at, part in ((Z_QKV, dq), (Z_QKV + A_WIDTH, dk), (Z_QKV + 2 * A_WIDTH, dv), (Z_PC, dpc_z), (Z_FL, dfl)):
        dz = lax.dynamic_update_slice(dz, part, (0, at))
    dh = _mm(dz, w["w_all"], tb=True, name="mm_in_dx")
    d_w_all = _mm(sv["h"], dz, ta=True, name="mm_in_dw")
    hook("mix_pre", dh)
    dx, sum_mix_pre = _modnorm_bwd(dh, sv["x"], dx1, _row(w["g_mix_pre"]), (mod3, 1), "mix_pre_bwd")

    dmod = jnp.stack([sum_mix_pre[0], sum_mix_pre[1], sum_mix_post[0], sum_ff_pre[0], sum_ff_pre[1], sum_ff_post[0]])
    d_w_in = _w_in_shards(d_w_all)
    d_w_pool = jnp.stack([d_wbd[64 * g:64 * g + 64, 64 * g:64 * g + 64] for g in range(4)])
    big = dict(w_in=d_w_in, w_branch=d_w_branch, w_out=d_w_out, w_ff1=d_w_ff1, w_ff2=d_w_ff2)
    small = dict(g_mix_pre=sum_mix_pre[2], g_mix_post=sum_mix_post[1], g_ff_pre=sum_ff_pre[2], g_ff_post=sum_ff_post[1],
                 b_f=sum_bf[0, :HEADS], w_pool=d_w_pool, pool_scale=sum_ps[0], conv_w=sum_cw[0:3])
    return dx, dmod, big, small


_QKV_END, _FL_END, _PC_END = 3 * A_WIDTH, 3 * A_WIDTH + HEADS, 3 * A_WIDTH + HEADS + POOL_WIDTH + 3 * CONV_WIDTH
_W_IN_GROUPS = ((_PC_END, IN_COLS, Z_GL), (0, _QKV_END, Z_QKV), (_FL_END, _PC_END, Z_PC), (_QKV_END, _FL_END, Z_FL))
_SHARD_COLS = IN_COLS // N_CHIPS


def _w_all_from_shards(blocks):
    pieces = []
    for lo, hi, _ in _W_IN_GROUPS:
        for p in range(N_CHIPS):
            a, b = max(lo, p * _SHARD_COLS), min(hi, (p + 1) * _SHARD_COLS)
            if a < b:
                pieces.append(blocks[p][:, a - p * _SHARD_COLS:b - p * _SHARD_COLS])
    pieces.append(jnp.zeros((D, Z_COLS - IN_COLS), blocks.dtype))
    return jnp.concatenate(pieces, axis=1)


def _w_in_shards(d_w_all):
    blocks = []
    for p in range(N_CHIPS):
        pieces = []
        for lo, hi, at in sorted(_W_IN_GROUPS):
            a, b = max(lo, p * _SHARD_COLS), min(hi, (p + 1) * _SHARD_COLS)
            if a < b:
                pieces.append(d_w_all[:, at + a - lo:at + b - lo])
        blocks.append(jnp.concatenate(pieces, axis=1))
    return jnp.stack(blocks)


def _full_layer_weights(w_in_blocks, w_branch, w_out, w_ff1, w_ff2, g_mix_pre, g_mix_post, g_ff_pre, g_ff_post, b_f, w_pool, pool_scale, conv_w):
    w_all = _w_all_from_shards(w_in_blocks)
    wbd = (w_pool[:, :, None, :] * jnp.eye(4, dtype=F32)[:, None, :, None]).reshape(POOL_WIDTH, POOL_WIDTH)
    return dict(w_all=w_all, w_branch=w_branch, w_out=w_out, w_ff1=w_ff1, w_ff2=w_ff2, g_mix_pre=g_mix_pre, g_mix_post=g_mix_post,
                g_ff_pre=g_ff_pre, g_ff_post=g_ff_post, b_f_pad=jnp.pad(b_f, (0, 128 - HEADS)).reshape(1, 128), w_pool_bd=wbd,
                pool_scale=pool_scale, conv_w=conv_w)


class _NoComm:
    def layer_weights(self, l):
        raise NotImplementedError

    def fwd_hook(self, l):
        return _no_hook

    def bwd_hook(self, l):
        return _no_hook

    def grads_ready(self, l, big):
        return None


class _Layers(_NoComm):
    def __init__(self, layers):
        self.layers = layers

    def layer_weights(self, l):
        return self.layers[l]


def _local_step(x, target, mods, comm):
    saved, weights = [], []
    act = x
    for l in range(DEPTH):
        weights.append(comm.layer_weights(l))
        act, sv = _layer_fwd(act, weights[l], mods[l], comm.fwd_hook(l))
        saved.append(sv)
    dact, sq = _loss_head(act, target)
    loss = sq[0, 0] * (0.5 / D)
    dmods, bigs, smalls = [None] * DEPTH, [None] * DEPTH, [None] * DEPTH
    token = None
    for l in reversed(range(DEPTH)):
        dact, dmods[l], bigs[l], smalls[l] = _layer_bwd(dact, saved[l], weights[l], _tie(mods[l], token), comm.bwd_hook(l))
        token = comm.grads_ready(l, bigs[l])
    return loss, dact, jnp.stack(dmods), bigs, smalls


_BIG = ("w_in", "w_branch", "w_out", "w_ff1", "w_ff2")
N_BIG = len(_BIG)


class _GatherJob:
    def __init__(self, tag, shards, after):
        self.tag, self.n = tag, len(shards)
        lands = [lax.empty((N_CHIPS,) + s.shape, s.dtype) for s in shards]
        self.state = _copies_start(list(shards) + lands, _plan_gather_ici, 4 * self.n, after, "gather_ici_start_" + tag)
        self.token = self.state[3]

    def pass_on(self, after):
        bufs = _copies_wait(self.state, _plan_gather_ici, after, "gather_ici_wait_" + self.tag)
        self.state = _copies_start(bufs[self.n:], _plan_gather_d2d, 3 * self.n, bufs[0], "gather_d2d_start_" + self.tag)
        self.token = self.state[3]
        return self.token

    def done(self, after):
        return _copies_wait(self.state, _plan_gather_d2d, after, "gather_d2d_wait_" + self.tag)


class _ReduceJob:
    def __init__(self, tag, names, grads, sel, after):
        self.tag, self.names, self.n, self.sel = tag, names, len(names), sel
        lands = [lax.empty((N_CHIPS, g.shape[1] // 2, g.shape[2]), F32) for g in grads]
        self.state = _copies_start(list(grads) + lands, _plan_rs_sibling, self.n, after, "rs_sibling_start_" + tag)
        self.token = self.state[3]

    def chip_sums(self, after):
        bufs = _copies_wait(self.state, _plan_rs_sibling, after, "rs_sibling_wait_" + self.tag)
        wires, self.owns = zip(*[_chip_sum(bufs[i], bufs[self.n + i], self.sel, "rs_chip_sum_" + name) for i, name in enumerate(self.names)])
        lands = [lax.empty((3,) + w.shape[1:], BF16) for w in wires]
        self.state = _copies_start(list(wires) + lands, _plan_rs_chips, 3 * self.n, self.owns[0], "rs_chips_start_" + self.tag)
        self.token = self.state[3]
        return self.token

    def final_sums(self, after):
        bufs = _copies_wait(self.state, _plan_rs_chips, after, "rs_chips_wait_" + self.tag)
        sums = [_final_sum(self.owns[i], bufs[self.n + i], self.sel, "rs_final_" + name) for i, name in enumerate(self.names)]
        self.state = _copies_start(sums, _plan_rs_share, self.n, sums[0], "rs_share_start_" + self.tag)
        self.token = self.state[3]
        return self.token

    def done(self, after):
        return dict(zip(self.names, _copies_wait(self.state, _plan_rs_share, after, "rs_share_wait_" + self.tag)))


def _chip_blocks(g):
    return g if g.ndim == 3 else g.reshape(N_CHIPS, -1, g.shape[1])


class _StepComm(_NoComm):
    def __init__(self, shards, sel, after):
        self.sel = sel
        self.small, self.grads, self.jobs = None, [dict() for _ in range(DEPTH)], {}
        self.jobs["in0"] = _GatherJob("in0", shards[0][:1], after)
        self.jobs["rest0"] = _GatherJob("rest0", shards[0][1:], self.jobs["in0"].token)
        self.jobs["all1"] = _GatherJob("all1", shards[1], self.jobs["rest0"].token)

    def layer_weights(self, l):
        if l == 0:
            job = self.jobs["in0"]
            (g_in,) = job.done(job.pass_on(self.jobs["all1"].token))
            self.weights0 = _full_layer_weights(g_in, None, None, None, None, *self.small[0])
            return self.weights0
        g_in, g_br, g_out, g_f1, g_f2 = self.landed1
        return _full_layer_weights(g_in, g_br.reshape(D, D), g_out.reshape(D, D), g_f1, g_f2.reshape(D_FF, D), *self.small[1])

    def fwd_hook(self, l):
        if l != 0:
            return _no_hook

        def hook(point, after, ready=None):
            if point == "attn":
                return self.jobs["rest0"].pass_on(after)
            if point == "ff_post":
                return self.jobs["all1"].pass_on(after)
            if point == "pool":
                g_br, g_out, g_f1, g_f2 = self.jobs["rest0"].done(after)
                self.weights0.update(w_branch=g_br.reshape(D, D), w_out=g_out.reshape(D, D), w_ff1=g_f1, w_ff2=g_f2.reshape(D_FF, D))
            if point == "end":
                self.landed1 = self.jobs["all1"].done(after)
            return None
        return hook

    def bwd_hook(self, l):
        if l != 0:
            return _no_hook

        def hook(point, after, ready=None):
            jobs = self.jobs
            if point == "ff_pre":
                token = jobs["rs1"].chip_sums(after)
                jobs["rs0_ff"] = _ReduceJob("0_ff", ("w_ff1", "w_ff2"), [_chip_blocks(ready[n]) for n in ("w_ff1", "w_ff2")], self.sel, token)
                return jobs["rs0_ff"].token
            if point == "cumf":
                return jobs["rs0_ff"].chip_sums(jobs["rs1"].final_sums(after))
            self.grads[1] = jobs["rs1"].done(after)
            return None
        return hook

    def grads_ready(self, l, big):
        if l == 1:
            self.jobs["rs1"] = _ReduceJob("1", _BIG, [_chip_blocks(big[n]) for n in _BIG], self.sel, self.sel)
            return self.jobs["rs1"].token
        names = ("w_in", "w_branch", "w_out")
        self.jobs["rs0_mix"] = _ReduceJob("0_mix", names, [_chip_blocks(big[n]) for n in names], self.sel, self.sel)
        return self.jobs["rs0_mix"].token

    def finish_sums(self, after):
        jobs = self.jobs
        token = jobs["rs0_mix"].chip_sums(after)
        return jobs["rs0_ff"].final_sums(token)

    def finish_ff(self, after):
        self.grads[0].update(self.jobs["rs0_ff"].done(after))

    def finish_mix(self, after):
        job = self.jobs["rs0_mix"]
        self.grads[0].update(job.done(job.final_sums(after)))


_SMALL = ("g_mix_pre", "g_mix_post", "g_ff_pre", "g_ff_post", "b_f", "w_pool", "pool_scale", "conv_w")


def _w_in_view(t):
    return t.reshape(DEPTH, D // 128, 128, _SHARD_COLS).transpose(3, 1, 0, 2).reshape(_SHARD_COLS * (D // 128) * DEPTH, 128)


def _w_in_unview(t):
    return t.reshape(_SHARD_COLS, D // 128, DEPTH, 128).transpose(2, 1, 3, 0).reshape(DEPTH, D, _SHARD_COLS)


def _pack(parts, rows=8):
    flat = jnp.concatenate([p.reshape(-1) for p in parts])
    width = -(-flat.shape[0] // (rows * 128)) * 128
    return jnp.pad(flat, (0, rows * width - flat.shape[0])).reshape(rows, width)


def _unpack(packed, like):
    flat = packed.reshape(-1)
    out, at = [], 0
    for ref in like:
        out.append(flat[at:at + ref.size].reshape(ref.shape))
        at += ref.size
    return out


def kernel(x, c, w_ada, b_ada, g_mix_pre, g_mix_post, g_ff_pre, g_ff_post, w_in, b_f, w_pool, pool_scale, conv_w, w_branch, w_out, w_ff1, w_ff2, loss_target, m_w_ada, m_b_ada, m_g_mix_pre, m_g_mix_post, m_g_ff_pre, m_g_ff_post, m_w_in, m_b_f, m_w_pool, m_pool_scale, m_conv_w, m_w_branch, m_w_out, m_w_ff1, m_w_ff2, v_w_ada, v_b_ada, v_g_mix_pre, v_g_mix_post, v_g_ff_pre, v_g_ff_post, v_w_in, v_b_f, v_w_pool, v_pool_scale, v_conv_w, v_w_branch, v_w_out, v_w_ff1, v_w_ff2):
    xi, yi, ci = lax.axis_index("x"), lax.axis_index("y"), lax.axis_index("c")
    chip = 2 * xi + yi
    dev = 2 * chip + ci
    n_ada = w_ada.shape[2]

    first = jnp.zeros((8, D + 384), F32).at[0, :D].set(c[0]).at[0, D:].set(conv_w.reshape(-1))
    got = _allgather8(first, "gather_cond").reshape(N_DEV, 8, D + 384)[:, 0]
    c_all = got[:, :D]
    conv_full = got[0::2, D:].reshape(N_CHIPS, DEPTH, 3, CONV_WIDTH // N_CHIPS).transpose(1, 2, 0, 3).reshape(DEPTH, 3, CONV_WIDTH)

    b_loc = lax.dynamic_slice_in_dim(b_ada, chip * n_ada, n_ada, axis=1).reshape(DEPTH, 1, n_ada)
    mod_cols, silu_c = _ada_fwd(c_all, w_ada, b_loc)
    got = _allgather8(mod_cols.reshape(DEPTH * N_DEV, n_ada), "gather_mod").reshape(N_DEV, DEPTH, N_DEV, n_ada)[0::2]
    mod_all = got.transpose(1, 2, 0, 3).reshape(DEPTH, N_DEV, 6, D)
    mods = lax.dynamic_index_in_dim(mod_all, dev, axis=1, keepdims=False)

    comm = _StepComm([[w[l].astype(BF16) for w in (w_in, w_branch, w_out, w_ff1, w_ff2)] for l in range(DEPTH)],
                     jnp.stack([ci, chip]).astype(jnp.int32), mods)
    comm.small = [(g_mix_pre[l], g_mix_post[l], g_ff_pre[l], g_ff_post[l], b_f[l], w_pool[l], pool_scale[l], conv_full[l]) for l in range(DEPTH)]
    loss_part, grad_x, dmods, bigs, smalls = _local_step(x[0], loss_target[0], mods, comm)

    small_parts = [smalls[l][name] for name in _SMALL for l in range(DEPTH)] + [loss_part.reshape(1)]
    packed = _tie(_pack([dmods] + small_parts), comm.jobs["rs0_mix"].token)
    gathered = _allgather8(packed, "gather_small")
    dmod_all = gathered.reshape(N_DEV, -1)[:, :dmods.size].reshape(N_DEV, DEPTH, 6 * D)
    summed = _unpack(_sum_devices(gathered), [dmods] + small_parts)
    grad_b_ada = summed[0].reshape(DEPTH, 6 * D)
    loss = summed[-1][0]
    small_grads = {name: jnp.stack(summed[1 + 2 * i:3 + 2 * i]) for i, name in enumerate(_SMALL)}
    small_grads["conv_w"] = lax.dynamic_slice_in_dim(small_grads["conv_w"], chip * (CONV_WIDTH // N_CHIPS), CONV_WIDTH // N_CHIPS, axis=2)

    dmod_loc = lax.dynamic_slice_in_dim(dmod_all.transpose(1, 0, 2), chip * n_ada, n_ada, axis=2)
    tail_token = comm.finish_sums(grad_b_ada)
    silu_pad = _tie(jnp.pad(silu_c, ((0, 128 - N_DEV), (0, 0))), tail_token)
    dmod_pad = jnp.pad(dmod_loc.transpose(1, 0, 2).reshape(N_DEV, DEPTH * n_ada), ((0, 128 - N_DEV), (0, 0)))
    grad_w_ada = _mm(silu_pad, dmod_pad, ta=True, out_split=DEPTH, name="mm_ada_dw")

    grads = dict(w_ada=grad_w_ada, b_ada=grad_b_ada, **small_grads)
    weights = dict(w_ada=w_ada, b_ada=b_ada, g_mix_pre=g_mix_pre, g_mix_post=g_mix_post, g_ff_pre=g_ff_pre, g_ff_post=g_ff_post, w_in=w_in,
                   b_f=b_f, w_pool=w_pool, pool_scale=pool_scale, conv_w=conv_w, w_branch=w_branch, w_out=w_out, w_ff1=w_ff1, w_ff2=w_ff2)
    m_in = dict(w_ada=m_w_ada, b_ada=m_b_ada, g_mix_pre=m_g_mix_pre, g_mix_post=m_g_mix_post, g_ff_pre=m_g_ff_pre, g_ff_post=m_g_ff_post,
                w_in=m_w_in, b_f=m_b_f, w_pool=m_w_pool, pool_scale=m_pool_scale, conv_w=m_conv_w, w_branch=m_w_branch, w_out=m_w_out,
                w_ff1=m_w_ff1, w_ff2=m_w_ff2)
    v_in = dict(w_ada=v_w_ada, b_ada=v_b_ada, g_mix_pre=v_g_mix_pre, g_mix_post=v_g_mix_post, g_ff_pre=v_g_ff_pre, g_ff_post=v_g_ff_post,
                w_in=v_w_in, b_f=v_b_f, w_pool=v_w_pool, pool_scale=v_pool_scale, conv_w=v_conv_w, w_branch=v_w_branch, w_out=v_w_out,
                w_ff1=v_w_ff1, w_ff2=v_w_ff2)
    order = ("w_ada", "b_ada", "g_mix_pre", "g_mix_post", "g_ff_pre", "g_ff_post", "w_in", "b_f", "w_pool", "pool_scale", "conv_w",
             "w_branch", "w_out", "w_ff1", "w_ff2")
    delta, new_m, new_v = {}, {}, {}
    tiny = ("b_ada",) + _SMALL
    packed_w, packed_g, packed_m, packed_v = [_pack([src[name] for name in tiny]) for src in (weights, grads, m_in, v_in)]
    res = _adamw(packed_w, _tie(packed_g, tail_token), packed_m, packed_v, "adamw_small")
    for out, packed_res in zip((delta, new_m, new_v), res):
        for name, val in zip(tiny, _unpack(packed_res, [weights[name] for name in tiny])):
            out[name] = val
    delta["w_ada"], new_m["w_ada"], new_v["w_ada"] = _adamw(w_ada, grad_w_ada, m_w_ada, v_w_ada, "adamw_w_ada")
    comm.finish_ff(delta["w_ada"][0, :8, :128] + delta["b_ada"][0, :128])
    for name in ("w_ff1", "w_ff2", "w_in", "w_branch", "w_out"):
        if name == "w_in":
            comm.finish_mix(delta["w_ff2"][0, :8, :128])
        grads[name] = jnp.stack([comm.grads[l][name] for l in range(DEPTH)])
        if name == "w_in":
            g_view = lax.optimization_barrier(_w_in_view(grads[name]))
            res = _adamw(_w_in_view(w_in), g_view, _w_in_view(m_w_in), _w_in_view(v_w_in), "adamw_w_in")
            grads[name], delta[name], new_m[name], new_v[name] = [_w_in_unview(t) for t in (g_view, *res)]
        else:
            delta[name], new_m[name], new_v[name] = _adamw(weights[name], grads[name], m_in[name], v_in[name], "adamw_" + name)

    return (loss, grad_x[None], *[grads[n] for n in order], *[delta[n] for n in order], *[new_m[n] for n in order],
            *[new_v[n] for n in order])
```

```python
import functools
from typing import NamedTuple

import jax
import jax.numpy as jnp
from jax import lax
from jax.experimental import pallas as pl
from jax.experimental.pallas import tpu as pltpu

F32 = jnp.float32
BF16 = jnp.bfloat16
MESH = pl.DeviceIdType.MESH

D = 1024
DEPTH = 2
HEADS = 8
HEAD_DIM = 64
A_WIDTH = 512
POOL_WIDTH = 256
CONV_WIDTH = 256
D_FF = 4096
IN_COLS = 5640
Z_GL, Z_QKV, Z_PC, Z_FL, Z_COLS = 0, 3072, 4608, 5632, 5760
RMS_EPS = 1e-6
NEG_INF = -1e30
ROW_TILE = 256
N_CHIPS = 4
N_DEV = 8
V7X_VMEM_LIMIT = 48 * 1024 * 1024

ADAM_LR = 0.001
ADAM_B1 = 0.9
ADAM_B2 = 0.999
ADAM_EPS = 1e-08
ADAM_WD = 0.01
ADAM_STEP = 10

_HBM = pl.BlockSpec(memory_space=pltpu.HBM)


def _params(*sem):
    return pltpu.CompilerParams(dimension_semantics=sem, vmem_limit_bytes=V7X_VMEM_LIMIT)


def _pick(dim, cands):
    for cand in cands:
        if dim % cand == 0:
            return cand
    return dim


def _mm(a, b, *, ta=False, tb=False, b_rows=None, b_split=1, out_split=1, out_dtype=F32, epilogue=None, extras=(), name):
    (k, m) = a.shape if ta else a.shape[::-1]
    b_row0, b_rows = (0, b.shape[-2]) if b_rows is None else b_rows
    b_cols = b.shape[-1] * b_split
    (n, k2) = (b_rows, b_cols) if tb else (b_cols, b_rows)
    assert k == k2, (a.shape, b.shape, ta, tb)
    n_unit = n // (out_split * (1 if tb else b_split))
    k_unit = k // (b_split if tb else 1)
    tm = _pick(m, (1024, 512, 256, 128))
    tn = _pick(n_unit, (1024, 1152, 768, 640, 512, 256, 128))
    tk = _pick(k_unit, (1024, 1152, 512, 640, 256, 128))
    nk = k // tk
    dims = (((0 if ta else 1,), (1 if tb else 0,)), ((), ()))

    def dot(a_ref, b_ref):
        b_val = b_ref[0] if b_split > 1 else b_ref[...]
        return lax.dot_general(a_ref[...].astype(BF16), b_val.astype(BF16), dims, preferred_element_type=F32)

    n_extra = len(extras)
    assert epilogue is None or out_split == 1

    def put(refs, val):
        if epilogue is not None:
            for o_ref, res in zip(refs[n_extra:], epilogue(val, *[r[...] for r in refs[:n_extra]])):
                o_ref[...] = res.astype(o_ref.dtype)
        elif out_split > 1:
            refs[0][0] = val.astype(refs[0].dtype)
        else:
            refs[0][...] = val.astype(refs[0].dtype)

    def body_single(a_ref, b_ref, *refs):
        put(refs, dot(a_ref, b_ref))

    def body_acc(a_ref, b_ref, *refs):
        kk = pl.program_id(2)
        acc_ref = refs[-1]

        @pl.when(kk == 0)
        def _():
            acc_ref[...] = jnp.zeros_like(acc_ref)

        acc_ref[...] += dot(a_ref, b_ref)

        @pl.when(kk == nk - 1)
        def _():
            put(refs[:-1], acc_ref[...])

    a_spec = pl.BlockSpec((tk, tm), lambda i, j, kk: (kk, i)) if ta else pl.BlockSpec((tm, tk), lambda i, j, kk: (i, kk))
    if b_split == 1:
        off = b_row0 // (tn if tb else tk)
        assert off * (tn if tb else tk) == b_row0
        b_spec = pl.BlockSpec((tn, tk), lambda i, j, kk: (j + off, kk)) if tb else pl.BlockSpec((tk, tn), lambda i, j, kk: (kk + off, j))
    elif tb:
        per = k_unit // tk
        b_spec = pl.BlockSpec((1, tn, tk), lambda i, j, kk: (kk // per, j, kk % per))
    else:
        per = n // b_split // tn
        b_spec = pl.BlockSpec((1, tk, tn), lambda i, j, kk: (j // per, kk, j % per))
    if out_split == 1:
        o_spec = pl.BlockSpec((tm, tn), lambda i, j, kk: (i, j))
        o_shape = None if epilogue is not None else jax.ShapeDtypeStruct((m, n), out_dtype)
    else:
        per_o = n // out_split // tn
        o_spec = pl.BlockSpec((1, tm, tn), lambda i, j, kk: (j // per_o, i, j % per_o))
        o_shape = jax.ShapeDtypeStruct((out_split, m, n // out_split), out_dtype)
    if epilogue is not None:
        o_shape = [jax.ShapeDtypeStruct((m, n), dt) for dt in out_dtype]
        o_spec = [o_spec] * len(out_dtype)
    return pl.pallas_call(
        body_single if nk == 1 else body_acc, name=name, grid=(m // tm, n // tn, nk),
        in_specs=[a_spec, b_spec] + [pl.BlockSpec((tm, tn), lambda i, j, kk: (i, j))] * n_extra, out_specs=o_spec, out_shape=o_shape,
        scratch_shapes=[] if nk == 1 else [pltpu.VMEM((tm, tn), F32)],
        compiler_params=_params("parallel", "parallel", "arbitrary"),
    )(a, b, *extras)


def _ew(fn, ins, out_dtypes, name, tc=None):
    shape = ins[0].shape
    lead, (rows, cols) = shape[:-2], shape[-2:]
    tc = cols if tc is None else tc
    tr = _pick(rows, (ROW_TILE, 128, 8)) if tc > 128 else _pick(rows, (4096, 2256, 2048, 1024, ROW_TILE, 8))
    n_in = len(ins)

    def body(*refs):
        res = fn(*[r[...] for r in refs[:n_in]])
        for o_ref, val in zip(refs[n_in:], res):
            o_ref[...] = val.astype(o_ref.dtype)

    if lead:
        spec = pl.BlockSpec((None, tr, tc), lambda l, i, j: (l, i, j))
    else:
        spec = pl.BlockSpec((tr, tc), lambda i, j: (i, j))
    return pl.pallas_call(
        body, name=name, grid=lead + (rows // tr, cols // tc),
        in_specs=[spec] * n_in, out_specs=[spec] * len(out_dtypes),
        out_shape=[jax.ShapeDtypeStruct(shape, dt) for dt in out_dtypes],
        compiler_params=_params(*(["parallel"] * (len(lead) + 2))),
    )(*ins)


def _relu2_fwd(a):
    r = jnp.maximum(a, 0.0)
    return a, r * r


def _relu2_bwd(dr, a):
    return (dr * (2.0 * jnp.maximum(a, 0.0)),)


def _adamw_math(w, g, m, v):
    m = ADAM_B1 * m + (1.0 - ADAM_B1) * g
    v = ADAM_B2 * v + (1.0 - ADAM_B2) * (g * g)
    m_hat = m / (1.0 - ADAM_B1 ** ADAM_STEP)
    v_hat = v / (1.0 - ADAM_B2 ** ADAM_STEP)
    delta = -ADAM_LR * (m_hat / (jnp.sqrt(v_hat) + ADAM_EPS) + ADAM_WD * w)
    return delta, m, v


def _adamw(w, g, m, v, name):
    return _ew(_adamw_math, [w, g, m, v], [F32, F32, F32], name)


def _adamw_many(ws, gs, ms, vs, name):
    n = len(ws)

    def body(*refs):
        for i in range(n):
            res = _adamw_math(*[refs[k * n + i][...] for k in range(4)])
            for k in range(3):
                refs[(4 + k) * n + i][...] = res[k]

    outs = pl.pallas_call(
        body, name=name, out_shape=[jax.ShapeDtypeStruct(w.shape, F32) for w in ws] * 3,
        compiler_params=pltpu.CompilerParams(vmem_limit_bytes=V7X_VMEM_LIMIT),
    )(*ws, *gs, *ms, *vs)
    return outs[:n], outs[n:2 * n], outs[2 * n:]


def _row_spec(cols, block=0):
    return pl.BlockSpec((ROW_TILE, cols), lambda i, block=block: (i, block))


def _vec_spec(cols):
    return pl.BlockSpec((1, cols), lambda i: (0, 0))


def _vec_args(*vecs):
    arrays = [v[0] if isinstance(v, tuple) else v for v in vecs]
    specs = [pl.BlockSpec((None, 1, D), lambda i, row=v[1]: (row, 0, 0)) if isinstance(v, tuple) else _vec_spec(D) for v in vecs]
    return arrays, specs


def _sum_spec(cols):
    return pl.BlockSpec((8, cols), lambda i: (0, 0))


def _rstd(x):
    return lax.rsqrt(jnp.mean(x * x, axis=-1, keepdims=True) + RMS_EPS)


def _modnorm_fwd(x, g, shift, scale, name):
    s = x.shape[0]

    def body(x_ref, g_ref, sh_ref, sc_ref, h_ref):
        xv = x_ref[...]
        n = xv * _rstd(xv)
        h_ref[...] = ((n * g_ref[...]) * (1.0 + sc_ref[...]) + sh_ref[...]).astype(BF16)

    vecs, vec_specs = _vec_args(g, shift, scale)
    return pl.pallas_call(
        body, name=name, grid=(s // ROW_TILE,),
        in_specs=[_row_spec(D)] + vec_specs, out_specs=_row_spec(D),
        out_shape=jax.ShapeDtypeStruct((s, D), BF16), compiler_params=_params("parallel"),
    )(x, *vecs)


def _post_fwd(x, y, g, gate, name):
    s = x.shape[0]

    def body(x_ref, y_ref, g_ref, gate_ref, o_ref):
        yv = y_ref[...]
        o_ref[...] = x_ref[...] + gate_ref[...] * ((yv * _rstd(yv)) * g_ref[...])

    vecs, vec_specs = _vec_args(g, gate)
    return pl.pallas_call(
        body, name=name, grid=(s // ROW_TILE,),
        in_specs=[_row_spec(D), _row_spec(D)] + vec_specs, out_specs=_row_spec(D),
        out_shape=jax.ShapeDtypeStruct((s, D), F32), compiler_params=_params("parallel"),
    )(x, y, *vecs)


def _post_bwd(dxo, y, g, gate, name):
    s = dxo.shape[0]

    def body(d_ref, y_ref, g_ref, gate_ref, dy_ref, sum_ref):
        @pl.when(pl.program_id(0) == 0)
        def _():
            sum_ref[...] = jnp.zeros_like(sum_ref)

        dv, yv = d_ref[...], y_ref[...]
        r = _rstd(yv)
        n = yv * r
        sum_ref[0:1, :] += jnp.sum(dv * (n * g_ref[...]), axis=0, keepdims=True)
        sum_ref[1:2, :] += jnp.sum((dv * gate_ref[...]) * n, axis=0, keepdims=True)
        dn = (dv * gate_ref[...]) * g_ref[...]
        dy_ref[...] = (r * (dn - n * jnp.mean(dn * n, axis=-1, keepdims=True))).astype(BF16)

    vecs, vec_specs = _vec_args(g, gate)
    return pl.pallas_call(
        body, name=name, grid=(s // ROW_TILE,),
        in_specs=[_row_spec(D), _row_spec(D)] + vec_specs,
        out_specs=[_row_spec(D), _sum_spec(D)],
        out_shape=[jax.ShapeDtypeStruct((s, D), BF16), jax.ShapeDtypeStruct((8, D), F32)],
        compiler_params=_params("arbitrary"),
    )(dxo, y, *vecs)


def _modnorm_bwd(dh, x, dxo, g, scale, name):
    s = dh.shape[0]

    def body(dh_ref, x_ref, d_ref, g_ref, sc_ref, dx_ref, sum_ref):
        @pl.when(pl.program_id(0) == 0)
        def _():
            sum_ref[...] = jnp.zeros_like(sum_ref)

        dhv, xv = dh_ref[...], x_ref[...]
        r = _rstd(xv)
        n = xv * r
        one_sc = 1.0 + sc_ref[...]
        sum_ref[0:1, :] += jnp.sum(dhv, axis=0, keepdims=True)
        sum_ref[1:2, :] += jnp.sum(dhv * (n * g_ref[...]), axis=0, keepdims=True)
        sum_ref[2:3, :] += jnp.sum((dhv * one_sc) * n, axis=0, keepdims=True)
        dn = (dhv * one_sc) * g_ref[...]
        dx_ref[...] = d_ref[...] + r * (dn - n * jnp.mean(dn * n, axis=-1, keepdims=True))

    vecs, vec_specs = _vec_args(g, scale)
    return pl.pallas_call(
        body, name=name, grid=(s // ROW_TILE,),
        in_specs=[_row_spec(D), _row_spec(D), _row_spec(D)] + vec_specs,
        out_specs=[_row_spec(D), _sum_spec(D)],
        out_shape=[jax.ShapeDtypeStruct((s, D), F32), jax.ShapeDtypeStruct((8, D), F32)],
        compiler_params=_params("arbitrary"),
    )(dh, x, dxo, *vecs)


def _loss_head(y, target):
    s = y.shape[0]

    def body(y_ref, t_ref, dy_ref, sum_ref):
        @pl.when(pl.program_id(0) == 0)
        def _():
            sum_ref[...] = jnp.zeros_like(sum_ref)

        err = y_ref[...] - t_ref[...]
        dy_ref[...] = err * (1.0 / D)
        sum_ref[...] += jnp.sum(err * err)

    return pl.pallas_call(
        body, name="loss_head", grid=(s // ROW_TILE,),
        in_specs=[_row_spec(D), _row_spec(D)],
        out_specs=[_row_spec(D), pl.BlockSpec((8, 128), lambda i: (0, 0))],
        out_shape=[jax.ShapeDtypeStruct((s, D), F32), jax.ShapeDtypeStruct((8, 128), F32)],
        compiler_params=_params("arbitrary"),
    )(y, target)


def _merge_fwd(z, pa, pb, pc):
    s = z.shape[0]

    def body(g0_ref, g1_ref, g2_ref, pa_ref, pb_ref, pc_ref, o_ref):
        o_ref[...] = (jax.nn.sigmoid(g0_ref[...]) * pa_ref[...] + jax.nn.sigmoid(g1_ref[...]) * pb_ref[...]
                      + jax.nn.sigmoid(g2_ref[...]) * pc_ref[...]).astype(BF16)

    return pl.pallas_call(
        body, name="merge_fwd", grid=(s // ROW_TILE,),
        in_specs=[_row_spec(D, 0), _row_spec(D, 1), _row_spec(D, 2), _row_spec(D), _row_spec(D), _row_spec(D)],
        out_specs=_row_spec(D), out_shape=jax.ShapeDtypeStruct((s, D), BF16),
        compiler_params=_params("parallel"),
    )(z, z, z, pa, pb, pc)


def _merge_bwd(dm, z, pa, pb, pc):
    s = z.shape[0]

    def body(dm_ref, g0_ref, g1_ref, g2_ref, pa_ref, pb_ref, pc_ref, dgl_ref, da_ref, db_ref, dc_ref):
        dmv = dm_ref[...]
        for i, (g_ref, p_ref, d_ref) in enumerate(((g0_ref, pa_ref, da_ref), (g1_ref, pb_ref, db_ref), (g2_ref, pc_ref, dc_ref))):
            gate = jax.nn.sigmoid(g_ref[...])
            dgl_ref[:, i * D:(i + 1) * D] = ((dmv * p_ref[...]) * (gate * (1.0 - gate))).astype(BF16)
            d_ref[...] = (dmv * gate).astype(BF16)

    return pl.pallas_call(
        body, name="merge_bwd", grid=(s // ROW_TILE,),
        in_specs=[_row_spec(D), _row_spec(D, 0), _row_spec(D, 1), _row_spec(D, 2), _row_spec(D), _row_spec(D), _row_spec(D)],
        out_specs=[_row_spec(3 * D), _row_spec(D), _row_spec(D), _row_spec(D)],
        out_shape=[jax.ShapeDtypeStruct((s, Z_COLS), BF16)] + [jax.ShapeDtypeStruct((s, D), BF16)] * 3,
        compiler_params=_params("parallel"),
    )(dm, z, z, z, pa, pb, pc)


def _shift_down(v, n):
    row = lax.broadcasted_iota(jnp.int32, v.shape, 0)
    return jnp.where(row >= n, pltpu.roll(v, n, axis=0), 0.0)


def _shift_up(v, n):
    s = v.shape[0]
    row = lax.broadcasted_iota(jnp.int32, v.shape, 0)
    return jnp.where(row < s - n, pltpu.roll(v, s - n, axis=0), 0.0)


def _log_sigmoid(v):
    return jnp.minimum(v, 0.0) - jnp.log1p(jnp.exp(-jnp.abs(v)))


def _cumf_fwd(fl, bias):
    s = fl.shape[0]

    def body(fl_ref, b_ref, o_ref):
        acc = _log_sigmoid(fl_ref[...] + b_ref[...])
        step = 1
        while step < s:
            acc = acc + _shift_down(acc, step)
            step *= 2
        o_ref[...] = acc

    return pl.pallas_call(body, name="cumf_fwd", out_shape=jax.ShapeDtypeStruct((s, 128), F32),
                          compiler_params=pltpu.CompilerParams(vmem_limit_bytes=V7X_VMEM_LIMIT))(fl, bias)


def _cumf_bwd(dcum, fl, bias):
    s = fl.shape[0]

    def body(d_ref, fl_ref, b_ref, dfl_ref, db_ref):
        acc = d_ref[...]
        step = 1
        while step < s:
            acc = acc + _shift_up(acc, step)
            step *= 2
        dfl = acc * jax.nn.sigmoid(-(fl_ref[...] + b_ref[...]))
        dfl_ref[...] = dfl.astype(BF16)
        db_ref[...] = jnp.broadcast_to(jnp.sum(dfl, axis=0, keepdims=True), (8, 128))

    return pl.pallas_call(
        body, name="cumf_bwd",
        out_shape=[jax.ShapeDtypeStruct((s, 128), BF16), jax.ShapeDtypeStruct((8, 128), F32)],
        compiler_params=pltpu.CompilerParams(vmem_limit_bytes=V7X_VMEM_LIMIT))(dcum, fl, bias)


def _pool_windows(v, shift):
    s2 = v + shift(v, 1)
    s4 = s2 + shift(s2, 2)
    s8 = s4 + shift(s4, 4)
    s16 = s8 + shift(s8, 8)
    group = lax.broadcasted_iota(jnp.int32, v.shape, 1) // 64
    return jnp.where(group == 0, s2, jnp.where(group == 1, s4, jnp.where(group == 2, s8, s16)))


def _pool_count(shape):
    group = lax.broadcasted_iota(jnp.int32, shape, 1) // 64
    window = jnp.where(group == 0, 2.0, jnp.where(group == 1, 4.0, jnp.where(group == 2, 8.0, 16.0)))
    t1 = (lax.broadcasted_iota(jnp.int32, shape, 0) + 1).astype(F32)
    return jnp.minimum(t1, window)


def _pc_specs(s):
    zcol = lambda blk: pl.BlockSpec((s, 256), lambda i, blk=blk: (0, blk))
    first = Z_PC // 256
    return [zcol(first), zcol(first + 1), zcol(first + 2), zcol(first + 3),
            pl.BlockSpec((256, 256), lambda i: (0, 0)), pl.BlockSpec((1, 256), lambda i: (0, 0)),
            pl.BlockSpec((3, 256), lambda i: (0, 0))]


def _poolconv_fwd(z, wbd, pscale, convw):
    s = z.shape[0]

    def body(pu_ref, ch_ref, cb_ref, cc_ref, w_ref, ps_ref, cw_ref, yb_ref, yc_ref):
        u = pu_ref[...]
        p = _pool_windows(u, _shift_down) / _pool_count(u.shape) - u
        yb = jnp.dot(p.astype(BF16), w_ref[...].astype(BF16), preferred_element_type=F32) * ps_ref[...]
        yb_ref[...] = yb.astype(BF16)
        uc = cc_ref[...] * ch_ref[...]
        cw = cw_ref[...]
        conv = cw[0:1, :] * _shift_down(uc, 2) + cw[1:2, :] * _shift_down(uc, 1) + cw[2:3, :] * uc
        yc_ref[...] = (cb_ref[...] * conv).astype(BF16)

    out = pl.BlockSpec((s, 256), lambda i: (0, 0))
    return pl.pallas_call(
        body, name="poolconv_fwd", grid=(1,), in_specs=_pc_specs(s), out_specs=[out, out],
        out_shape=[jax.ShapeDtypeStruct((s, 256), BF16)] * 2, compiler_params=_params("arbitrary"),
    )(z, z, z, z, wbd, pscale, convw)


def _poolconv_bwd(dyb, dyc, z, wbd, pscale, convw):
    s = z.shape[0]

    def body(dyb_ref, dyc_ref, pu_ref, ch_ref, cb_ref, cc_ref, w_ref, ps_ref, cw_ref, dz_ref, dw_ref, dps_ref, dcw_ref):
        u = pu_ref[...]
        count = _pool_count(u.shape)
        p = (_pool_windows(u, _shift_down) / count - u).astype(BF16)
        wb = w_ref[...].astype(BF16)
        dyb_v = dyb_ref[...]
        pw = jnp.dot(p, wb, preferred_element_type=F32)
        dps_ref[...] = jnp.broadcast_to(jnp.sum(dyb_v * pw, axis=0, keepdims=True), (8, 256))
        dys = (dyb_v * ps_ref[...]).astype(BF16)
        dp = lax.dot_general(dys, wb, (((1,), (1,)), ((), ())), preferred_element_type=F32)
        dw_ref[...] = lax.dot_general(p, dys, (((0,), (0,)), ((), ())), preferred_element_type=F32)
        dz_ref[:, 0:256] = (_pool_windows(dp / count, _shift_up) - dp).astype(BF16)

        ch, cb, cc = ch_ref[...], cb_ref[...], cc_ref[...]
        uc = cc * ch
        cw = cw_ref[...]
        u2, u1 = _shift_down(uc, 2), _shift_down(uc, 1)
        conv = cw[0:1, :] * u2 + cw[1:2, :] * u1 + cw[2:3, :] * uc
        dyc_v = dyc_ref[...]
        dconv = dyc_v * cb
        du = cw[0:1, :] * _shift_up(dconv, 2) + cw[1:2, :] * _shift_up(dconv, 1) + cw[2:3, :] * dconv
        dz_ref[:, 256:512] = (du * cc).astype(BF16)
        dz_ref[:, 512:768] = (dyc_v * conv).astype(BF16)
        dz_ref[:, 768:1024] = (du * ch).astype(BF16)
        dcw_ref[...] = jnp.zeros_like(dcw_ref)
        dcw_ref[0:1, :] = jnp.sum(dconv * u2, axis=0, keepdims=True)
        dcw_ref[1:2, :] = jnp.sum(dconv * u1, axis=0, keepdims=True)
        dcw_ref[2:3, :] = jnp.sum(dconv * uc, axis=0, keepdims=True)

    blk = lambda r, c: pl.BlockSpec((r, c), lambda i: (0, 0))
    return pl.pallas_call(
        body, name="poolconv_bwd", grid=(1,),
        in_specs=[blk(s, 256), blk(s, 256)] + _pc_specs(s),
        out_specs=[blk(s, 1024), blk(256, 256), blk(8, 256), blk(8, 256)],
        out_shape=[jax.ShapeDtypeStruct((s, 1024), BF16), jax.ShapeDtypeStruct((256, 256), F32),
                   jax.ShapeDtypeStruct((8, 256), F32), jax.ShapeDtypeStruct((8, 256), F32)],
        compiler_params=_params("arbitrary"),
    )(dyb, dyc, z, z, z, z, wbd, pscale, convw)


_NT = (((1,), (1,)), ((), ()))
_TN = (((0,), (0,)), ((), ()))


ATT_Q, ATT_K = 256, 256
ATT_HEADS_BWD = 8
ATT_HEADS = 8


def _att_logits(q, k, fr, q0, k0, masked):
    logits = lax.dot_general(q, k, _NT, preferred_element_type=F32) - fr
    if not masked:
        return logits
    row = q0 + lax.broadcasted_iota(jnp.int32, logits.shape, 0)
    col = k0 + lax.broadcasted_iota(jnp.int32, logits.shape, 1)
    return jnp.where(row >= col, logits, NEG_INF)


def _causal_sweep(step, qi, init):
    n_full = (qi * ATT_Q) // ATT_K
    carry = lax.fori_loop(0, n_full, lambda j, carry: step(j, carry, False), init)
    return step(n_full, carry, True)


HEAD_PAIRS = HEADS // 2


def _lane_pick(v, lane, idx):
    return jnp.sum(jnp.where(lane == idx, v, 0.0), axis=-1, keepdims=True)


def _lane_put(lane, idx, col):
    return jnp.where(lane == idx, col, 0.0)


def _split_heads(v, low):
    zero = jnp.zeros_like(v)
    return jnp.where(low, v, zero), jnp.where(low, zero, v)


def _attn_fwd(qkv, fr):
    s = qkv.shape[0]
    nk = s // ATT_K
    width = ATT_HEADS * HEAD_DIM
    groups = HEADS // ATT_HEADS

    def body(q_ref, k_ref, v_ref, fr_ref, o_ref, lse_ref):
        qi, grp = pl.program_id(0), pl.program_id(1)
        lane = lax.broadcasted_iota(jnp.int32, (ATT_Q, 128), 1)
        low = lane < HEAD_DIM
        qs = []
        for pr in range(ATT_HEADS // 2):
            qs += _split_heads(q_ref[:, 128 * pr:128 * (pr + 1)] * (HEAD_DIM ** -0.5), low)

        def step(j, carry, masked):
            k0 = pl.multiple_of(j * ATT_K, ATT_K)
            out = []
            for h in range(ATT_HEADS):
                cols = slice(128 * (h // 2), 128 * (h // 2 + 1))
                m, l, acc = carry[h]
                logits = _att_logits(qs[h], k_ref[pl.ds(k0, ATT_K), cols], fr_ref[h, pl.ds(j, 1), :], qi * ATT_Q, k0, masked)
                m_new = jnp.maximum(m, jnp.max(logits, axis=-1, keepdims=True))
                p = jnp.exp(logits - m_new)
                alpha = jnp.exp(m - m_new)
                l = alpha * l + jnp.sum(p, axis=-1, keepdims=True)
                acc = alpha * acc + jnp.dot(p.astype(BF16), v_ref[pl.ds(k0, ATT_K), cols], preferred_element_type=F32)
                out.append((m_new, l, acc))
            return tuple(out)

        one = (jnp.full((ATT_Q, 1), NEG_INF, F32), jnp.zeros((ATT_Q, 1), F32), jnp.zeros((ATT_Q, 128), F32))
        done = _causal_sweep(step, qi, (one,) * ATT_HEADS)

        @pl.when(grp == 0)
        def _():
            lse_ref[...] = jnp.zeros_like(lse_ref)

        lse = jnp.zeros((ATT_Q, 128), F32)
        for pr in range(ATT_HEADS // 2):
            (m0, l0, acc0), (m1, l1, acc1) = done[2 * pr], done[2 * pr + 1]
            o_ref[:, 128 * pr:128 * (pr + 1)] = jnp.where(low, acc0 / l0, acc1 / l1)
            head = ATT_HEADS * grp + 2 * pr
            lse = lse + _lane_put(lane, head, m0 + jnp.log(l0)) + _lane_put(lane, head + 1, m1 + jnp.log(l1))
        lse_ref[...] += lse

    return pl.pallas_call(
        body, name="attn_fwd", grid=(s // ATT_Q, groups),
        in_specs=[pl.BlockSpec((ATT_Q, width), lambda i, g: (i, g)),
                  pl.BlockSpec((s, width), lambda i, g: (0, groups + g)),
                  pl.BlockSpec((s, width), lambda i, g: (0, 2 * groups + g)),
                  pl.BlockSpec((ATT_HEADS, nk, ATT_K), lambda i, g: (g, 0, 0))],
        out_specs=[pl.BlockSpec((ATT_Q, width), lambda i, g: (i, g)), pl.BlockSpec((ATT_Q, 128), lambda i, g: (i, 0))],
        out_shape=[jax.ShapeDtypeStruct((s, A_WIDTH), F32), jax.ShapeDtypeStruct((s, 128), F32)],
        compiler_params=_params("parallel", "arbitrary"),
    )(qkv, qkv, qkv, fr)


def _attn_bwd(qkv, do, o, lse, fr):
    s = qkv.shape[0]
    nk = s // ATT_K
    scale = HEAD_DIM ** -0.5
    heads = ATT_HEADS_BWD
    width = heads * HEAD_DIM
    groups = HEADS // heads

    def body(q_ref, k_ref, v_ref, do_ref, o_ref, lse_ref, fr_ref, dq_ref, dk_ref, dv_ref, dfc_ref, dfr_ref, dk_acc, dv_acc):
        grp = pl.program_id(0)
        lane = lax.broadcasted_iota(jnp.int32, (ATT_Q, 128), 1)
        low = lane < HEAD_DIM
        low_t = lax.broadcasted_iota(jnp.int32, (128, ATT_Q), 0) < HEAD_DIM
        dk_acc[...] = jnp.zeros_like(dk_acc)
        dv_acc[...] = jnp.zeros_like(dv_acc)
        dfr_ref[...] = jnp.zeros_like(dfr_ref)

        @pl.when(grp == 0)
        def _():
            dfc_ref[...] = jnp.zeros_like(dfc_ref)

        def outer(i, carry):
            q0 = pl.multiple_of(i * ATT_Q, ATT_Q)
            rows = pl.ds(q0, ATT_Q)
            lsev = lse_ref[rows, :]
            qts, dots, qs, dos, deltas, lses = [], [], [], [], [], []
            for pr in range(heads // 2):
                pcols = slice(128 * pr, 128 * (pr + 1))
                q2, do2 = q_ref[rows, pcols] * scale, do_ref[rows, pcols]
                prod = do2 * o_ref[rows, pcols]
                deltas += [jnp.sum(jnp.where(low, prod, 0.0), axis=-1, keepdims=True),
                           jnp.sum(jnp.where(low, 0.0, prod), axis=-1, keepdims=True)]
                dob2 = do2.astype(BF16)
                qts += _split_heads(q2.astype(F32).T.astype(BF16), low_t)
                dots += _split_heads(do2.T.astype(BF16), low_t)
                qs += _split_heads(q2, low)
                dos += _split_heads(dob2, low)
                lses += [_lane_pick(lsev, lane, heads * grp + 2 * pr), _lane_pick(lsev, lane, heads * grp + 2 * pr + 1)]

            def inner(j, carry, masked):
                k0 = pl.multiple_of(j * ATT_K, ATT_K)
                krows = pl.ds(k0, ATT_K)
                out, dkt, dvt = [], [], []
                for h in range(heads):
                    pcols = slice(128 * (h // 2), 128 * (h // 2 + 1))
                    dq, dfc = carry[h]
                    k2 = k_ref[krows, pcols]
                    p = jnp.exp(_att_logits(qs[h], k2, fr_ref[h, pl.ds(j, 1), :], q0, k0, masked) - lses[h])
                    dp = lax.dot_general(dos[h], v_ref[krows, pcols], _NT, preferred_element_type=F32)
                    ds = p * (dp - deltas[h])
                    dsb = ds.astype(BF16)
                    dkt.append(jnp.dot(qts[h], dsb, preferred_element_type=F32))
                    dvt.append(jnp.dot(dots[h], p.astype(BF16), preferred_element_type=F32))
                    dfr_ref[h, pl.ds(j, 1), :] -= jnp.sum(ds, axis=0, keepdims=True)
                    out.append((dq + jnp.dot(dsb, k2, preferred_element_type=F32), dfc + (ds[:, :128] + ds[:, 128:])))
                for pr in range(heads // 2):
                    prows = slice(128 * pr, 128 * (pr + 1))
                    dk_acc[j, prows, :] += dkt[2 * pr] + dkt[2 * pr + 1]
                    dv_acc[j, prows, :] += dvt[2 * pr] + dvt[2 * pr + 1]
                return tuple(out)

            one = (jnp.zeros((ATT_Q, 128), F32), jnp.zeros((ATT_Q, 128), F32))
            done = _causal_sweep(inner, i, (one,) * heads)
            dfc = jnp.zeros((ATT_Q, 128), F32)
            for pr in range(heads // 2):
                (dq0, dfc0), (dq1, dfc1) = done[2 * pr], done[2 * pr + 1]
                dq_ref[rows, 128 * pr:128 * (pr + 1)] = (jnp.where(low, dq0, dq1) * scale).astype(BF16)
                head = heads * grp + 2 * pr
                dfc = (dfc + _lane_put(lane, head, jnp.sum(dfc0, axis=-1, keepdims=True))
                       + _lane_put(lane, head + 1, jnp.sum(dfc1, axis=-1, keepdims=True)))
            dfc_ref[rows, :] += dfc
            return carry

        lax.fori_loop(0, s // ATT_Q, outer, 0)
        for j in range(nk):
            for pr in range(heads // 2):
                prows, pcols = slice(128 * pr, 128 * (pr + 1)), slice(128 * pr, 128 * (pr + 1))
                dk_ref[ATT_K * j:ATT_K * (j + 1), pcols] = dk_acc[j, prows, :].T.astype(BF16)
                dv_ref[ATT_K * j:ATT_K * (j + 1), pcols] = dv_acc[j, prows, :].T.astype(BF16)

    part = lambda first: pl.BlockSpec((s, width), lambda g, first=first: (0, first + g))
    whole = pl.BlockSpec((s, 128), lambda g: (0, 0))
    rowv = pl.BlockSpec((heads, nk, ATT_K), lambda g: (g, 0, 0))
    return pl.pallas_call(
        body, name="attn_bwd", grid=(groups,),
        in_specs=[part(0), part(groups), part(2 * groups), part(0), part(0), whole, rowv],
        out_specs=[part(0), part(0), part(0), whole, rowv],
        out_shape=[jax.ShapeDtypeStruct((s, A_WIDTH), BF16)] * 3 + [jax.ShapeDtypeStruct((s, 128), F32), jax.ShapeDtypeStruct((HEADS, nk, ATT_K), F32)],
        scratch_shapes=[pltpu.VMEM((nk, width, ATT_K), F32), pltpu.VMEM((nk, width, ATT_K), F32)],
        compiler_params=_params("arbitrary"),
    )(qkv, qkv, qkv, do, o, lse, fr)


def _ada_fwd(c_all, w_ada, b_loc):
    depth, _, n = w_ada.shape
    tn = 512

    def body(c_ref, w_ref, b_ref, o_ref, sc_ref):
        cv = c_ref[...]
        sc = cv * jax.nn.sigmoid(cv)
        sc_ref[...] = sc
        o_ref[0] = jnp.dot(sc.astype(BF16), w_ref[0].astype(BF16), preferred_element_type=F32) + b_ref[0]

    return pl.pallas_call(
        body, name="ada_fwd", grid=(depth, n // tn),
        in_specs=[pl.BlockSpec((N_DEV, D), lambda l, j: (0, 0)), pl.BlockSpec((1, D, tn), lambda l, j: (l, 0, j)),
                  pl.BlockSpec((1, 1, tn), lambda l, j: (l, 0, j))],
        out_specs=[pl.BlockSpec((1, N_DEV, tn), lambda l, j: (l, 0, j)), pl.BlockSpec((N_DEV, D), lambda l, j: (0, 0))],
        out_shape=[jax.ShapeDtypeStruct((depth, N_DEV, n), F32), jax.ShapeDtypeStruct((N_DEV, D), F32)],
        compiler_params=_params("arbitrary", "arbitrary"),
    )(c_all, w_ada, b_loc)


def _sum_devices(gathered):
    n = gathered.shape[1]
    tn = _pick(n, (1408, 1024, 640, 512, 128))

    def body(g_ref, o_ref):
        acc = g_ref[0:8, :]
        for dev in range(1, N_DEV):
            acc = acc + g_ref[8 * dev:8 * dev + 8, :]
        o_ref[...] = acc

    return pl.pallas_call(
        body, name="sum_devices", grid=(n // tn,),
        in_specs=[pl.BlockSpec((8 * N_DEV, tn), lambda j: (0, j))], out_specs=pl.BlockSpec((8, tn), lambda j: (0, j)),
        out_shape=jax.ShapeDtypeStruct((8, n), F32), compiler_params=_params("parallel"),
    )(gathered)


def _place():
    x, y, c = lax.axis_index("x"), lax.axis_index("y"), lax.axis_index("c")
    chips = [(1 - x, y), (x, 1 - y), (1 - x, 1 - y)]
    return x, y, c, chips


def _allgather8(block, name):
    m_per, n = block.shape

    def body(x_ref, out_ref, send_sems, recv_sems, local_sem):
        x, y, c, chips = _place()
        me, sibling = (x, y, c), (x, y, 1 - c)

        def rows(px, py, pc):
            return out_ref.at[pl.ds((4 * px + 2 * py + pc) * m_per, m_per), :]

        def copy(k, blk, to, src=None):
            return pltpu.make_async_remote_copy(
                src_ref=rows(*blk) if src is None else src, dst_ref=rows(*blk),
                send_sem=send_sems.at[k], recv_sem=recv_sems.at[k], device_id=to, device_id_type=MESH)

        mine = pltpu.make_async_copy(x_ref, rows(*me), local_sem)
        mine.start()
        first = [copy(0, me, sibling, src=x_ref)]
        first += [copy(1 + j, me, (*chip, c), src=x_ref) for j, chip in enumerate(chips)]
        for cp in first:
            cp.start()
        passed = [copy(4 + j, (*chip, c), sibling) for j, chip in enumerate(chips)]
        for j, chip in enumerate(chips):
            copy(1 + j, (*chip, c), me).wait_recv()
            passed[j].start()
        copy(0, sibling, me).wait_recv()
        for j, chip in enumerate(chips):
            copy(4 + j, (*chip, 1 - c), me).wait_recv()
        for cp in first + passed:
            cp.wait_send()
        mine.wait()

    return pl.pallas_call(
        body, name=name, out_shape=jax.ShapeDtypeStruct((N_DEV * m_per, n), block.dtype),
        in_specs=[pl.BlockSpec(memory_space=pltpu.VMEM)], out_specs=pl.BlockSpec(memory_space=pltpu.VMEM),
        scratch_shapes=[pltpu.SemaphoreType.DMA((7,)), pltpu.SemaphoreType.DMA((7,)), pltpu.SemaphoreType.DMA],
        compiler_params=pltpu.CompilerParams(vmem_limit_bytes=V7X_VMEM_LIMIT),
    )(block)


_SEM = pl.BlockSpec(memory_space=pltpu.SEMAPHORE)
_DATAFLOW = pltpu.SideEffectType.DATAFLOW_SIDE_EFFECTING


def _plan_copies(plan, refs, send_sems, recv_sems):
    return [pltpu.make_async_remote_copy(src_ref=src, dst_ref=dst, send_sem=send_sems.at[i], recv_sem=recv_sems.at[i],
                                         device_id=to, device_id_type=MESH) for i, (src, dst, to) in enumerate(plan(refs))]


class _Token(NamedTuple):
    after: jax.Array
    tie: jax.Array


def _after_operand(after):
    return after.after if isinstance(after, _Token) else after


def _copies_start(bufs, plan, n_copies, after, name):
    nb = len(bufs)

    def body(*refs):
        for cp in _plan_copies(plan, refs[:nb], refs[nb + 1], refs[nb + 2]):
            cp.start()
        for token in refs[-2:]:
            token[...] = jnp.zeros_like(token)

    sem = pltpu.SemaphoreType.DMA((n_copies,))
    vmem = pl.BlockSpec(memory_space=pltpu.VMEM)
    outs = pl.pallas_call(
        body, name=name,
        out_shape=(sem, sem, *[pltpu.HBM(b.shape, b.dtype) for b in bufs], jax.ShapeDtypeStruct((8, 128), F32),
                   jax.ShapeDtypeStruct((1, 1), F32)),
        in_specs=[_HBM] * nb + [pl.BlockSpec(memory_space=pl.ANY)],
        out_specs=(_SEM, _SEM, *[_HBM] * nb, vmem, vmem),
        input_output_aliases={i: 2 + i for i in range(nb)},
        compiler_params=pltpu.CompilerParams(has_side_effects=_DATAFLOW),
    )(*[pltpu.with_memory_space_constraint(b, pltpu.HBM) for b in bufs], _after_operand(after))
    return outs[0], outs[1], list(outs[2:2 + nb]), _Token(outs[-2], outs[-1])


def _copies_wait(started, plan, after, name):
    send_sems, recv_sems, bufs, _ = started
    nb = len(bufs)

    def body(*refs):
        for cp in _plan_copies(plan, refs[:nb], refs[nb], refs[nb + 1]):
            cp.wait_send()
            cp.wait_recv()

    return list(pl.pallas_call(
        body, name=name, out_shape=tuple(pltpu.HBM(b.shape, b.dtype) for b in bufs),
        in_specs=[_HBM] * nb + [_SEM, _SEM, pl.BlockSpec(memory_space=pl.ANY)], out_specs=tuple([_HBM] * nb),
        input_output_aliases={i: i for i in range(nb)},
        compiler_params=pltpu.CompilerParams(has_side_effects=_DATAFLOW),
    )(*bufs, send_sems, recv_sems, _after_operand(after)))


def _half_rows(ref, axis, c):
    half = ref.shape[axis] // 2
    return pl.ds(c * half, half)


def _plan_gather_ici(refs):
    n = len(refs) // 2
    x, y, c, chips = _place()
    out = []
    for a in range(n):
        rows = _half_rows(refs[a], 0, c)
        out += [(refs[a].at[rows], refs[n + a].at[2 * x + y, rows], (*chip, c)) for chip in chips]
        out.append((refs[a], refs[n + a].at[2 * x + y], (x, y, 1 - c)))
    return out


def _plan_gather_d2d(refs):
    x, y, c, chips = _place()
    out = []
    for ref in refs:
        rows = _half_rows(ref, 1, c)
        for px, py in chips:
            landed = ref.at[2 * px + py, rows]
            out.append((landed, landed, (x, y, 1 - c)))
    return out


def _plan_rs_sibling(refs):
    n = len(refs) // 2
    x, y, c, _ = _place()
    return [(refs[a].at[pl.ds(0, N_CHIPS), _half_rows(refs[a], 1, 1 - c)], refs[n + a], (x, y, 1 - c)) for a in range(n)]


def _plan_rs_chips(refs):
    n = len(refs) // 2
    x, y, c, chips = _place()
    return [(refs[a].at[2 * px + py], refs[n + a].at[k], (px, py, c)) for a in range(n) for k, (px, py) in enumerate(chips)]


def _plan_rs_share(refs):
    x, y, c, _ = _place()
    return [(ref.at[_half_rows(ref, 0, c)], ref.at[_half_rows(ref, 0, c)], (x, y, 1 - c)) for ref in refs]


def _chip_sum(g, other, sel, name):
    _, half, cdim = other.shape
    tr = _pick(half, (512, 256, 128, 64))
    per = half // tr

    def body(sel_ref, g_ref, t_ref, wire_ref, own_ref):
        total = g_ref[0] + t_ref[0]
        wire_ref[0] = total.astype(BF16)

        @pl.when(pl.program_id(1) == sel_ref[1])
        def _():
            own_ref[...] = total

    blk = pl.BlockSpec((1, tr, cdim), lambda i, p, sel_ref: (p, i, 0))
    return pl.pallas_call(
        body, name=name,
        grid_spec=pltpu.PrefetchScalarGridSpec(
            num_scalar_prefetch=1, grid=(per, N_CHIPS),
            in_specs=[pl.BlockSpec((1, tr, cdim), lambda i, p, sel_ref: (p, sel_ref[0] * per + i, 0)), blk],
            out_specs=[blk, pl.BlockSpec((tr, cdim), lambda i, p, sel_ref: (i, 0))]),
        out_shape=[jax.ShapeDtypeStruct(other.shape, BF16), jax.ShapeDtypeStruct((half, cdim), F32)],
        compiler_params=_params("parallel", "arbitrary"),
    )(sel, g, other)


def _final_sum(own, recv, sel, name):
    half, cdim = own.shape
    tr = _pick(half, (512, 256, 128, 64))
    per = half // tr

    def body(sel_ref, own_ref, r0_ref, r1_ref, r2_ref, o_ref):
        o_ref[...] = ((own_ref[...] + r0_ref[0].astype(F32)) + r1_ref[0].astype(F32)) + r2_ref[0].astype(F32)

    part = lambda k: pl.BlockSpec((1, tr, cdim), lambda i, sel_ref, k=k: (k, i, 0))
    return pl.pallas_call(
        body, name=name,
        grid_spec=pltpu.PrefetchScalarGridSpec(
            num_scalar_prefetch=1, grid=(per,),
            in_specs=[pl.BlockSpec((tr, cdim), lambda i, sel_ref: (i, 0)), part(0), part(1), part(2)],
            out_specs=pl.BlockSpec((tr, cdim), lambda i, sel_ref: (sel_ref[0] * per + i, 0))),
        out_shape=jax.ShapeDtypeStruct((2 * half, cdim), F32), compiler_params=_params("parallel"),
    )(sel, own, recv, recv, recv)


def _row(v):
    return v.reshape(1, -1)


_BR_A, _BR_B, _BR_C = (0, A_WIDTH), (A_WIDTH, POOL_WIDTH), (A_WIDTH + POOL_WIDTH, CONV_WIDTH)


def _tie(v, token):
    return v if token is None else v + token.tie


def _no_hook(point, after, ready=None):
    return None


def _layer_fwd(x, w, mod, hook=_no_hook):
    s = x.shape[0]
    mod3 = mod.reshape(6, 1, D)
    h = _modnorm_fwd(x, _row(w["g_mix_pre"]), (mod3, 0), (mod3, 1), "mix_pre_fwd")
    z = _mm(h, w["w_all"], name="mm_in")
    qkv = z[:, Z_QKV:Z_PC].astype(BF16)
    fl = z[:, Z_FL:Z_COLS]
    cum = _cumf_fwd(fl, w["b_f_pad"])
    fr = cum[:, :HEADS].T.reshape(HEADS, s // ATT_K, ATT_K)
    br_a, lse = _attn_fwd(qkv, fr)
    br_b, br_c = _poolconv_fwd(z, w["w_pool_bd"], _tie(_row(w["pool_scale"]), hook("attn", lse)), w["conv_w"])
    hook("pool", br_b)
    wbr = w["w_branch"]
    pa = _mm(br_a, wbr, b_rows=_BR_A, name="mm_br_a")
    pb = _mm(br_b, wbr, b_rows=_BR_B, name="mm_br_b")
    pc = _mm(br_c, wbr, b_rows=_BR_C, name="mm_br_c")
    merged = _merge_fwd(z, pa, pb, pc)
    y = _mm(merged, w["w_out"], name="mm_out")
    x1 = _post_fwd(x, y, _row(w["g_mix_post"]), (mod3, 2), "mix_post_fwd")
    h2 = _modnorm_fwd(x1, _row(w["g_ff_pre"]), (mod3, 3), (mod3, 4), "ff_pre_fwd")
    a, r = _mm(h2, w["w_ff1"], b_split=N_CHIPS, epilogue=_relu2_fwd, out_dtype=(F32, BF16), name="mm_ff1")
    y2 = _mm(r, w["w_ff2"], name="mm_ff2")
    x2 = _post_fwd(x1, y2, _tie(_row(w["g_ff_post"]), hook("ff_post", y2)), (mod3, 5), "ff_post_fwd")
    hook("end", x2)
    saved = dict(x=x, h=h, z=z, qkv=qkv, fl=fl, fr=fr, lse=lse, br_a=br_a, br_b=br_b, br_c=br_c, pa=pa, pb=pb, pc=pc,
                 merged=merged, y=y, x1=x1, h2=h2, a=a, r=r, y2=y2)
    return x2, saved


def _layer_bwd(dx2, sv, w, mod, hook=_no_hook):
    s = dx2.shape[0]
    mod3 = mod.reshape(6, 1, D)
    dy2, sum_ff_post = _post_bwd(dx2, sv["y2"], _row(w["g_ff_post"]), (mod3, 5), "ff_post_bwd")
    (da,) = _mm(dy2, w["w_ff2"], tb=True, epilogue=_relu2_bwd, extras=(sv["a"],), out_dtype=(BF16,), name="mm_ff2_dx")
    d_w_ff2 = _mm(sv["r"], dy2, ta=True, name="mm_ff2_dw")
    dh2 = _mm(da, w["w_ff1"], tb=True, b_split=N_CHIPS, name="mm_ff1_dx")
    d_w_ff1 = _mm(sv["h2"], da, ta=True, out_split=N_CHIPS, name="mm_ff1_dw")
    dx1, sum_ff_pre = _modnorm_bwd(dh2, sv["x1"], dx2, _tie(_row(w["g_ff_pre"]), hook("ff_pre", dh2, dict(w_ff1=d_w_ff1, w_ff2=d_w_ff2))), (mod3, 4), "ff_pre_bwd")

    dy, sum_mix_post = _post_bwd(dx1, sv["y"], _row(w["g_mix_post"]), (mod3, 2), "mix_post_bwd")
    dmerged = _mm(dy, w["w_out"], tb=True, name="mm_out_dx")
    d_w_out = _mm(sv["merged"], dy, ta=True, name="mm_out_dw")
    dz, dpa, dpb, dpc = _merge_bwd(dmerged, sv["z"], sv["pa"], sv["pb"], sv["pc"])
    wbr = w["w_branch"]
    dbr_a = _mm(dpa, wbr, tb=True, b_rows=_BR_A, name="mm_br_a_dx")
    dbr_b = _mm(dpb, wbr, tb=True, b_rows=_BR_B, name="mm_br_b_dx")
    dbr_c = _mm(dpc, wbr, tb=True, b_rows=_BR_C, name="mm_br_c_dx")
    d_w_branch = jnp.concatenate([_mm(sv["br_a"], dpa, ta=True, name="mm_br_a_dw"), _mm(sv["br_b"], dpb, ta=True, name="mm_br_b_dw"),
                                  _mm(sv["br_c"], dpc, ta=True, name="mm_br_c_dw")], axis=0)

    dq, dk, dv, dfc, dfr = _attn_bwd(sv["qkv"], dbr_a, sv["br_a"], sv["lse"], sv["fr"])
    dcum = dfc + jnp.pad(dfr.reshape(HEADS, s).T, ((0, 0), (0, 128 - HEADS)))
    dfl, sum_bf = _cumf_bwd(dcum, sv["fl"], _tie(w["b_f_pad"], hook("cumf", dfc)))
    dpc_z, d_wbd, sum_ps, sum_cw = _poolconv_bwd(dbr_b, dbr_c, sv["z"], w["w_pool_bd"], _row(w["pool_scale"]), w["conv_w"])
    for at, part in ((Z_QKV, dq), (Z_QKV + A_WIDTH, dk), (Z_QKV + 2 * A_WIDTH, dv), (Z_PC, dpc_z), (Z_FL, dfl)):
        dz = lax.dynamic_update_slice(dz, part, (0, at))
    dh = _mm(dz, w["w_all"], tb=True, name="mm_in_dx")
    d_w_all = _mm(sv["h"], dz, ta=True, name="mm_in_dw")
    hook("mix_pre", dh)
    dx, sum_mix_pre = _modnorm_bwd(dh, sv["x"], dx1, _row(w["g_mix_pre"]), (mod3, 1), "mix_pre_bwd")

    dmod = jnp.stack([sum_mix_pre[0], sum_mix_pre[1], sum_mix_post[0], sum_ff_pre[0], sum_ff_pre[1], sum_ff_post[0]])
    d_w_in = _w_in_shards(d_w_all)
    d_w_pool = jnp.stack([d_wbd[64 * g:64 * g + 64, 64 * g:64 * g + 64] for g in range(4)])
    big = dict(w_in=d_w_in, w_branch=d_w_branch, w_out=d_w_out, w_ff1=d_w_ff1, w_ff2=d_w_ff2)
    small = dict(g_mix_pre=sum_mix_pre[2], g_mix_post=sum_mix_post[1], g_ff_pre=sum_ff_pre[2], g_ff_post=sum_ff_post[1],
                 b_f=sum_bf[0, :HEADS], w_pool=d_w_pool, pool_scale=sum_ps[0], conv_w=sum_cw[0:3])
    return dx, dmod, big, small


_QKV_END, _FL_END, _PC_END = 3 * A_WIDTH, 3 * A_WIDTH + HEADS, 3 * A_WIDTH + HEADS + POOL_WIDTH + 3 * CONV_WIDTH
_W_IN_GROUPS = ((_PC_END, IN_COLS, Z_GL), (0, _QKV_END, Z_QKV), (_FL_END, _PC_END, Z_PC), (_QKV_END, _FL_END, Z_FL))
_SHARD_COLS = IN_COLS // N_CHIPS


def _w_all_from_shards(blocks):
    pieces = []
    for lo, hi, _ in _W_IN_GROUPS:
        for p in range(N_CHIPS):
            a, b = max(lo, p * _SHARD_COLS), min(hi, (p + 1) * _SHARD_COLS)
            if a < b:
                pieces.append(blocks[p][:, a - p * _SHARD_COLS:b - p * _SHARD_COLS])
    pieces.append(jnp.zeros((D, Z_COLS - IN_COLS), blocks.dtype))
    return jnp.concatenate(pieces, axis=1)


def _w_in_shards(d_w_all):
    blocks = []
    for p in range(N_CHIPS):
        pieces = []
        for lo, hi, at in sorted(_W_IN_GROUPS):
            a, b = max(lo, p * _SHARD_COLS), min(hi, (p + 1) * _SHARD_COLS)
            if a < b:
                pieces.append(d_w_all[:, at + a - lo:at + b - lo])
        blocks.append(jnp.concatenate(pieces, axis=1))
    return jnp.stack(blocks)


def _full_layer_weights(w_in_blocks, w_branch, w_out, w_ff1, w_ff2, g_mix_pre, g_mix_post, g_ff_pre, g_ff_post, b_f, w_pool, pool_scale, conv_w):
    w_all = _w_all_from_shards(w_in_blocks)
    wbd = (w_pool[:, :, None, :] * jnp.eye(4, dtype=F32)[:, None, :, None]).reshape(POOL_WIDTH, POOL_WIDTH)
    return dict(w_all=w_all, w_branch=w_branch, w_out=w_out, w_ff1=w_ff1, w_ff2=w_ff2, g_mix_pre=g_mix_pre, g_mix_post=g_mix_post,
                g_ff_pre=g_ff_pre, g_ff_post=g_ff_post, b_f_pad=jnp.pad(b_f, (0, 128 - HEADS)).reshape(1, 128), w_pool_bd=wbd,
                pool_scale=pool_scale, conv_w=conv_w)


class _NoComm:
    def layer_weights(self, l):
        raise NotImplementedError

    def fwd_hook(self, l):
        return _no_hook

    def bwd_hook(self, l):
        return _no_hook

    def grads_ready(self, l, big):
        return None


class _Layers(_NoComm):
    def __init__(self, layers):
        self.layers = layers

    def layer_weights(self, l):
        return self.layers[l]


def _local_step(x, target, mods, comm):
    saved, weights = [], []
    act = x
    for l in range(DEPTH):
        weights.append(comm.layer_weights(l))
        act, sv = _layer_fwd(act, weights[l], mods[l], comm.fwd_hook(l))
        saved.append(sv)
    dact, sq = _loss_head(act, target)
    loss = sq[0, 0] * (0.5 / D)
    dmods, bigs, smalls = [None] * DEPTH, [None] * DEPTH, [None] * DEPTH
    token = None
    for l in reversed(range(DEPTH)):
        dact, dmods[l], bigs[l], smalls[l] = _layer_bwd(dact, saved[l], weights[l], _tie(mods[l], token), comm.bwd_hook(l))
        token = comm.grads_ready(l, bigs[l])
    return loss, dact, jnp.stack(dmods), bigs, smalls


_BIG = ("w_in", "w_branch", "w_out", "w_ff1", "w_ff2")
N_BIG = len(_BIG)


class _GatherJob:
    def __init__(self, tag, shards, after):
        self.tag, self.n = tag, len(shards)
        lands = [lax.empty((N_CHIPS,) + s.shape, s.dtype) for s in shards]
        self.state = _copies_start(list(shards) + lands, _plan_gather_ici, 4 * self.n, after, "gather_ici_start_" + tag)
        self.token = self.state[3]

    def pass_on(self, after):
        bufs = _copies_wait(self.state, _plan_gather_ici, after, "gather_ici_wait_" + self.tag)
        self.state = _copies_start(bufs[self.n:], _plan_gather_d2d, 3 * self.n, bufs[0], "gather_d2d_start_" + self.tag)
        self.token = self.state[3]
        return self.token

    def done(self, after):
        return _copies_wait(self.state, _plan_gather_d2d, after, "gather_d2d_wait_" + self.tag)


class _ReduceJob:
    def __init__(self, tag, names, grads, sel, after):
        self.tag, self.names, self.n, self.sel = tag, names, len(names), sel
        lands = [lax.empty((N_CHIPS, g.shape[1] // 2, g.shape[2]), F32) for g in grads]
        self.state = _copies_start(list(grads) + lands, _plan_rs_sibling, self.n, after, "rs_sibling_start_" + tag)
        self.token = self.state[3]

    def chip_sums(self, after):
        bufs = _copies_wait(self.state, _plan_rs_sibling, after, "rs_sibling_wait_" + self.tag)
        wires, self.owns = zip(*[_chip_sum(bufs[i], bufs[self.n + i], self.sel, "rs_chip_sum_" + name) for i, name in enumerate(self.names)])
        lands = [lax.empty((3,) + w.shape[1:], BF16) for w in wires]
        self.state = _copies_start(list(wires) + lands, _plan_rs_chips, 3 * self.n, self.owns[0], "rs_chips_start_" + self.tag)
        self.token = self.state[3]
        return self.token

    def final_sums(self, after):
        bufs = _copies_wait(self.state, _plan_rs_chips, after, "rs_chips_wait_" + self.tag)
        sums = [_final_sum(self.owns[i], bufs[self.n + i], self.sel, "rs_final_" + name) for i, name in enumerate(self.names)]
        self.state = _copies_start(sums, _plan_rs_share, self.n, sums[0], "rs_share_start_" + self.tag)
        self.token = self.state[3]
        return self.token

    def done(self, after):
        return dict(zip(self.names, _copies_wait(self.state, _plan_rs_share, after, "rs_share_wait_" + self.tag)))


def _chip_blocks(g):
    return g if g.ndim == 3 else g.reshape(N_CHIPS, -1, g.shape[1])


class _StepComm(_NoComm):
    def __init__(self, shards, sel, after):
        self.sel = sel
        self.small, self.grads, self.jobs = None, [dict() for _ in range(DEPTH)], {}
        self.jobs["in0"] = _GatherJob("in0", shards[0][:1], after)
        self.jobs["rest0"] = _GatherJob("rest0", shards[0][1:], self.jobs["in0"].token)
        self.jobs["all1"] = _GatherJob("all1", shards[1], self.jobs["rest0"].token)

    def layer_weights(self, l):
        if l == 0:
            job = self.jobs["in0"]
            (g_in,) = job.done(job.pass_on(self.jobs["all1"].token))
            self.weights0 = _full_layer_weights(g_in, None, None, None, None, *self.small[0])
            return self.weights0
        g_in, g_br, g_out, g_f1, g_f2 = self.landed1
        return _full_layer_weights(g_in, g_br.reshape(D, D), g_out.reshape(D, D), g_f1, g_f2.reshape(D_FF, D), *self.small[1])

    def fwd_hook(self, l):
        if l != 0:
            return _no_hook

        def hook(point, after, ready=None):
            if point == "attn":
                return self.jobs["rest0"].pass_on(after)
            if point == "ff_post":
                return self.jobs["all1"].pass_on(after)
            if point == "pool":
                g_br, g_out, g_f1, g_f2 = self.jobs["rest0"].done(after)
                self.weights0.update(w_branch=g_br.reshape(D, D), w_out=g_out.reshape(D, D), w_ff1=g_f1, w_ff2=g_f2.reshape(D_FF, D))
            if point == "end":
                self.landed1 = self.jobs["all1"].done(after)
            return None
        return hook

    def bwd_hook(self, l):
        if l != 0:
            return _no_hook

        def hook(point, after, ready=None):
            jobs = self.jobs
            if point == "ff_pre":
                token = jobs["rs1"].chip_sums(after)
                jobs["rs0_ff"] = _ReduceJob("0_ff", ("w_ff1", "w_ff2"), [_chip_blocks(ready[n]) for n in ("w_ff1", "w_ff2")], self.sel, token)
                return jobs["rs0_ff"].token
            if point == "cumf":
                return jobs["rs0_ff"].chip_sums(jobs["rs1"].final_sums(after))
            self.grads[1] = jobs["rs1"].done(after)
            return None
        return hook

    def grads_ready(self, l, big):
        if l == 1:
            self.jobs["rs1"] = _ReduceJob("1", _BIG, [_chip_blocks(big[n]) for n in _BIG], self.sel, self.sel)
            return self.jobs["rs1"].token
        names = ("w_in", "w_branch", "w_out")
        self.jobs["rs0_mix"] = _ReduceJob("0_mix", names, [_chip_blocks(big[n]) for n in names], self.sel, self.sel)
        return self.jobs["rs0_mix"].token

    def finish_sums(self, after):
        jobs = self.jobs
        token = jobs["rs0_mix"].chip_sums(after)
        return jobs["rs0_ff"].final_sums(token)

    def finish_ff(self, after):
        self.grads[0].update(self.jobs["rs0_ff"].done(after))

    def finish_mix(self, after):
        job = self.jobs["rs0_mix"]
        self.grads[0].update(job.done(job.final_sums(after)))


_SMALL = ("g_mix_pre", "g_mix_post", "g_ff_pre", "g_ff_post", "b_f", "w_pool", "pool_scale", "conv_w")


def _w_in_view(t):
    return t.reshape(DEPTH, D // 128, 128, _SHARD_COLS).transpose(3, 1, 0, 2).reshape(_SHARD_COLS * (D // 128) * DEPTH, 128)


def _w_in_unview(t):
    return t.reshape(_SHARD_COLS, D // 128, DEPTH, 128).transpose(2, 1, 3, 0).reshape(DEPTH, D, _SHARD_COLS)


def _pack(parts, rows=8):
    flat = jnp.concatenate([p.reshape(-1) for p in parts])
    width = -(-flat.shape[0] // (rows * 128)) * 128
    return jnp.pad(flat, (0, rows * width - flat.shape[0])).reshape(rows, width)


def _unpack(packed, like):
    flat = packed.reshape(-1)
    out, at = [], 0
    for ref in like:
        out.append(flat[at:at + ref.size].reshape(ref.shape))
        at += ref.size
    return out


def kernel(x, c, w_ada, b_ada, g_mix_pre, g_mix_post, g_ff_pre, g_ff_post, w_in, b_f, w_pool, pool_scale, conv_w, w_branch, w_out, w_ff1, w_ff2, loss_target, m_w_ada, m_b_ada, m_g_mix_pre, m_g_mix_post, m_g_ff_pre, m_g_ff_post, m_w_in, m_b_f, m_w_pool, m_pool_scale, m_conv_w, m_w_branch, m_w_out, m_w_ff1, m_w_ff2, v_w_ada, v_b_ada, v_g_mix_pre, v_g_mix_post, v_g_ff_pre, v_g_ff_post, v_w_in, v_b_f, v_w_pool, v_pool_scale, v_conv_w, v_w_branch, v_w_out, v_w_ff1, v_w_ff2):
    xi, yi, ci = lax.axis_index("x"), lax.axis_index("y"), lax.axis_index("c")
    chip = 2 * xi + yi
    dev = 2 * chip + ci
    n_ada = w_ada.shape[2]

    first = jnp.zeros((8, D + 384), F32).at[0, :D].set(c[0]).at[0, D:].set(conv_w.reshape(-1))
    got = _allgather8(first, "gather_cond").reshape(N_DEV, 8, D + 384)[:, 0]
    c_all = got[:, :D]
    conv_full = got[0::2, D:].reshape(N_CHIPS, DEPTH, 3, CONV_WIDTH // N_CHIPS).transpose(1, 2, 0, 3).reshape(DEPTH, 3, CONV_WIDTH)

    b_loc = lax.dynamic_slice_in_dim(b_ada, chip * n_ada, n_ada, axis=1).reshape(DEPTH, 1, n_ada)
    mod_cols, silu_c = _ada_fwd(c_all, w_ada, b_loc)
    got = _allgather8(mod_cols.reshape(DEPTH * N_DEV, n_ada), "gather_mod").reshape(N_DEV, DEPTH, N_DEV, n_ada)[0::2]
    mod_all = got.transpose(1, 2, 0, 3).reshape(DEPTH, N_DEV, 6, D)
    mods = lax.dynamic_index_in_dim(mod_all, dev, axis=1, keepdims=False)

    comm = _StepComm([[w[l].astype(BF16) for w in (w_in, w_branch, w_out, w_ff1, w_ff2)] for l in range(DEPTH)],
                     jnp.stack([ci, chip]).astype(jnp.int32), mods)
    comm.small = [(g_mix_pre[l], g_mix_post[l], g_ff_pre[l], g_ff_post[l], b_f[l], w_pool[l], pool_scale[l], conv_full[l]) for l in range(DEPTH)]
    loss_part, grad_x, dmods, bigs, smalls = _local_step(x[0], loss_target[0], mods, comm)

    small_parts = [smalls[l][name] for name in _SMALL for l in range(DEPTH)] + [loss_part.reshape(1)]
    packed = _tie(_pack([dmods] + small_parts), comm.jobs["rs0_mix"].token)
    gathered = _allgather8(packed, "gather_small")
    dmod_all = gathered.reshape(N_DEV, -1)[:, :dmods.size].reshape(N_DEV, DEPTH, 6 * D)
    summed = _unpack(_sum_devices(gathered), [dmods] + small_parts)
    grad_b_ada = summed[0].reshape(DEPTH, 6 * D)
    loss = summed[-1][0]
    small_grads = {name: jnp.stack(summed[1 + 2 * i:3 + 2 * i]) for i, name in enumerate(_SMALL)}
    small_grads["conv_w"] = lax.dynamic_slice_in_dim(small_grads["conv_w"], chip * (CONV_WIDTH // N_CHIPS), CONV_WIDTH // N_CHIPS, axis=2)

    dmod_loc = lax.dynamic_slice_in_dim(dmod_all.transpose(1, 0, 2), chip * n_ada, n_ada, axis=2)
    tail_token = comm.finish_sums(grad_b_ada)
    silu_pad = _tie(jnp.pad(silu_c, ((0, 128 - N_DEV), (0, 0))), tail_token)
    dmod_pad = jnp.pad(dmod_loc.transpose(1, 0, 2).reshape(N_DEV, DEPTH * n_ada), ((0, 128 - N_DEV), (0, 0)))
    grad_w_ada = _mm(silu_pad, dmod_pad, ta=True, out_split=DEPTH, name="mm_ada_dw")

    grads = dict(w_ada=grad_w_ada, b_ada=grad_b_ada, **small_grads)
    weights = dict(w_ada=w_ada, b_ada=b_ada, g_mix_pre=g_mix_pre, g_mix_post=g_mix_post, g_ff_pre=g_ff_pre, g_ff_post=g_ff_post, w_in=w_in,
                   b_f=b_f, w_pool=w_pool, pool_scale=pool_scale, conv_w=conv_w, w_branch=w_branch, w_out=w_out, w_ff1=w_ff1, w_ff2=w_ff2)
    m_in = dict(w_ada=m_w_ada, b_ada=m_b_ada, g_mix_pre=m_g_mix_pre, g_mix_post=m_g_mix_post, g_ff_pre=m_g_ff_pre, g_ff_post=m_g_ff_post,
                w_in=m_w_in, b_f=m_b_f, w_pool=m_w_pool, pool_scale=m_pool_scale, conv_w=m_conv_w, w_branch=m_w_branch, w_out=m_w_out,
                w_ff1=m_w_ff1, w_ff2=m_w_ff2)
    v_in = dict(w_ada=v_w_ada, b_ada=v_b_ada, g_mix_pre=v_g_mix_pre, g_mix_post=v_g_mix_post, g_ff_pre=v_g_ff_pre, g_ff_post=v_g_ff_post,
                w_in=v_w_in, b_f=v_b_f, w_pool=v_w_pool, pool_scale=v_pool_scale, conv_w=v_conv_w, w_branch=v_w_branch, w_out=v_w_out,
                w_ff1=v_w_ff1, w_ff2=v_w_ff2)
    order = ("w_ada", "b_ada", "g_mix_pre", "g_mix_post", "g_ff_pre", "g_ff_post", "w_in", "b_f", "w_pool", "pool_scale", "conv_w",
             "w_branch", "w_out", "w_ff1", "w_ff2")
    delta, new_m, new_v = {}, {}, {}
    tiny = ("b_ada",) + _SMALL
    tiny_g = [_tie(grads[tiny[0]], tail_token)] + [grads[name] for name in tiny[1:]]
    res = _adamw_many([weights[name] for name in tiny], tiny_g, [m_in[name] for name in tiny], [v_in[name] for name in tiny], "adamw_small")
    for out, vals in zip((delta, new_m, new_v), res):
        out.update(zip(tiny, vals))
    delta["w_ada"], new_m["w_ada"], new_v["w_ada"] = _adamw(w_ada, grad_w_ada, m_w_ada, v_w_ada, "adamw_w_ada")
    comm.finish_ff(delta["w_ada"][0, :8, :128] + delta["b_ada"][0, :128])
    for name in ("w_ff1", "w_ff2", "w_in", "w_branch", "w_out"):
        if name == "w_in":
            comm.finish_mix(delta["w_ff2"][0, :8, :128])
        grads[name] = jnp.stack([comm.grads[l][name] for l in range(DEPTH)])
        if name == "w_in":
            g_view = lax.optimization_barrier(_w_in_view(grads[name]))
            res = _adamw(_w_in_view(w_in), g_view, _w_in_view(m_w_in), _w_in_view(v_w_in), "adamw_w_in")
            grads[name], delta[name], new_m[name], new_v[name] = [_w_in_unview(t) for t in (g_view, *res)]
        else:
            delta[name], new_m[name], new_v[name] = _adamw(weights[name], grads[name], m_in[name], v_in[name], "adamw_" + name)

    return (loss, grad_x[None], *[grads[n] for n in order], *[delta[n] for n in order], *[new_m[n] for n in order],
            *[new_v[n] for n in order])
```

```python
import functools
from typing import NamedTuple

import jax
import jax.numpy as jnp
from jax import lax
from jax.experimental import pallas as pl
from jax.experimental.pallas import tpu as pltpu

F32 = jnp.float32
BF16 = jnp.bfloat16
MESH = pl.DeviceIdType.MESH

D = 1024
DEPTH = 2
HEADS = 8
HEAD_DIM = 64
A_WIDTH = 512
POOL_WIDTH = 256
CONV_WIDTH = 256
D_FF = 4096
IN_COLS = 5640
Z_GL, Z_QKV, Z_PC, Z_FL, Z_COLS = 0, 3072, 4608, 5632, 5760
RMS_EPS = 1e-6
NEG_INF = -1e30
ROW_TILE = 512
EW_ROWS = 256
N_CHIPS = 4
N_DEV = 8
V7X_VMEM_LIMIT = 48 * 1024 * 1024

ADAM_LR = 0.001
ADAM_B1 = 0.9
ADAM_B2 = 0.999
ADAM_EPS = 1e-08
ADAM_WD = 0.01
ADAM_STEP = 10

_HBM = pl.BlockSpec(memory_space=pltpu.HBM)


def _params(*sem):
    return pltpu.CompilerParams(dimension_semantics=sem, vmem_limit_bytes=V7X_VMEM_LIMIT)


def _pick(dim, cands):
    for cand in cands:
        if dim % cand == 0:
            return cand
    return dim


def _mm(a, b, *, ta=False, tb=False, b_rows=None, b_split=1, out_split=1, out_dtype=F32, epilogue=None, extras=(), name):
    (k, m) = a.shape if ta else a.shape[::-1]
    b_row0, b_rows = (0, b.shape[-2]) if b_rows is None else b_rows
    b_cols = b.shape[-1] * b_split
    (n, k2) = (b_rows, b_cols) if tb else (b_cols, b_rows)
    assert k == k2, (a.shape, b.shape, ta, tb)
    n_unit = n // (out_split * (1 if tb else b_split))
    k_unit = k // (b_split if tb else 1)
    tm = _pick(m, (1024, 512, 256, 128))
    tn = _pick(n_unit, (1024, 1152, 768, 640, 512, 256, 128))
    tk = _pick(k_unit, (1024, 1152, 512, 640, 256, 128))
    nk = k // tk
    dims = (((0 if ta else 1,), (1 if tb else 0,)), ((), ()))

    def dot(a_ref, b_ref):
        b_val = b_ref[0] if b_split > 1 else b_ref[...]
        return lax.dot_general(a_ref[...].astype(BF16), b_val.astype(BF16), dims, preferred_element_type=F32)

    n_extra = len(extras)
    assert epilogue is None or out_split == 1

    def put(refs, val):
        if epilogue is not None:
            for o_ref, res in zip(refs[n_extra:], epilogue(val, *[r[...] for r in refs[:n_extra]])):
                o_ref[...] = res.astype(o_ref.dtype)
        elif out_split > 1:
            refs[0][0] = val.astype(refs[0].dtype)
        else:
            refs[0][...] = val.astype(refs[0].dtype)

    def body_single(a_ref, b_ref, *refs):
        put(refs, dot(a_ref, b_ref))

    def body_acc(a_ref, b_ref, *refs):
        kk = pl.program_id(2)
        acc_ref = refs[-1]

        @pl.when(kk == 0)
        def _():
            acc_ref[...] = jnp.zeros_like(acc_ref)

        acc_ref[...] += dot(a_ref, b_ref)

        @pl.when(kk == nk - 1)
        def _():
            put(refs[:-1], acc_ref[...])

    a_spec = pl.BlockSpec((tk, tm), lambda i, j, kk: (kk, i)) if ta else pl.BlockSpec((tm, tk), lambda i, j, kk: (i, kk))
    if b_split == 1:
        off = b_row0 // (tn if tb else tk)
        assert off * (tn if tb else tk) == b_row0
        b_spec = pl.BlockSpec((tn, tk), lambda i, j, kk: (j + off, kk)) if tb else pl.BlockSpec((tk, tn), lambda i, j, kk: (kk + off, j))
    elif tb:
        per = k_unit // tk
        b_spec = pl.BlockSpec((1, tn, tk), lambda i, j, kk: (kk // per, j, kk % per))
    else:
        per = n // b_split // tn
        b_spec = pl.BlockSpec((1, tk, tn), lambda i, j, kk: (j // per, kk, j % per))
    if out_split == 1:
        o_spec = pl.BlockSpec((tm, tn), lambda i, j, kk: (i, j))
        o_shape = None if epilogue is not None else jax.ShapeDtypeStruct((m, n), out_dtype)
    else:
        per_o = n // out_split // tn
        o_spec = pl.BlockSpec((1, tm, tn), lambda i, j, kk: (j // per_o, i, j % per_o))
        o_shape = jax.ShapeDtypeStruct((out_split, m, n // out_split), out_dtype)
    if epilogue is not None:
        o_shape = [jax.ShapeDtypeStruct((m, n), dt) for dt in out_dtype]
        o_spec = [o_spec] * len(out_dtype)
    return pl.pallas_call(
        body_single if nk == 1 else body_acc, name=name, grid=(m // tm, n // tn, nk),
        in_specs=[a_spec, b_spec] + [pl.BlockSpec((tm, tn), lambda i, j, kk: (i, j))] * n_extra, out_specs=o_spec, out_shape=o_shape,
        scratch_shapes=[] if nk == 1 else [pltpu.VMEM((tm, tn), F32)],
        compiler_params=_params("parallel", "parallel", "arbitrary"),
    )(a, b, *extras)


def _ew(fn, ins, out_dtypes, name, tc=None):
    shape = ins[0].shape
    lead, (rows, cols) = shape[:-2], shape[-2:]
    tc = cols if tc is None else tc
    tr = _pick(rows, (EW_ROWS, 128, 8)) if tc > 128 else _pick(rows, (4096, 2256, 2048, 1024, EW_ROWS, 8))
    n_in = len(ins)

    def body(*refs):
        res = fn(*[r[...] for r in refs[:n_in]])
        for o_ref, val in zip(refs[n_in:], res):
            o_ref[...] = val.astype(o_ref.dtype)

    if lead:
        spec = pl.BlockSpec((None, tr, tc), lambda l, i, j: (l, i, j))
    else:
        spec = pl.BlockSpec((tr, tc), lambda i, j: (i, j))
    return pl.pallas_call(
        body, name=name, grid=lead + (rows // tr, cols // tc),
        in_specs=[spec] * n_in, out_specs=[spec] * len(out_dtypes),
        out_shape=[jax.ShapeDtypeStruct(shape, dt) for dt in out_dtypes],
        compiler_params=_params(*(["parallel"] * (len(lead) + 2))),
    )(*ins)


def _relu2_fwd(a):
    r = jnp.maximum(a, 0.0)
    return a, r * r


def _relu2_bwd(dr, a):
    return (dr * (2.0 * jnp.maximum(a, 0.0)),)


def _adamw_math(w, g, m, v):
    m = ADAM_B1 * m + (1.0 - ADAM_B1) * g
    v = ADAM_B2 * v + (1.0 - ADAM_B2) * (g * g)
    m_hat = m / (1.0 - ADAM_B1 ** ADAM_STEP)
    v_hat = v / (1.0 - ADAM_B2 ** ADAM_STEP)
    delta = -ADAM_LR * (m_hat / (jnp.sqrt(v_hat) + ADAM_EPS) + ADAM_WD * w)
    return delta, m, v


def _adamw(w, g, m, v, name):
    return _ew(_adamw_math, [w, g, m, v], [F32, F32, F32], name)


def _adamw_many(ws, gs, ms, vs, name):
    n = len(ws)

    def body(*refs):
        for i in range(n):
            res = _adamw_math(*[refs[k * n + i][...] for k in range(4)])
            for k in range(3):
                refs[(4 + k) * n + i][...] = res[k]

    outs = pl.pallas_call(
        body, name=name, out_shape=[jax.ShapeDtypeStruct(w.shape, F32) for w in ws] * 3,
        compiler_params=pltpu.CompilerParams(vmem_limit_bytes=V7X_VMEM_LIMIT),
    )(*ws, *gs, *ms, *vs)
    return outs[:n], outs[n:2 * n], outs[2 * n:]


def _row_spec(cols, block=0):
    return pl.BlockSpec((ROW_TILE, cols), lambda i, block=block: (i, block))


def _vec_spec(cols):
    return pl.BlockSpec((1, cols), lambda i: (0, 0))


def _vec_args(*vecs):
    arrays = [v[0] if isinstance(v, tuple) else v for v in vecs]
    specs = [pl.BlockSpec((None, 1, D), lambda i, row=v[1]: (row, 0, 0)) if isinstance(v, tuple) else _vec_spec(D) for v in vecs]
    return arrays, specs


def _sum_spec(cols):
    return pl.BlockSpec((8, cols), lambda i: (0, 0))


def _rstd(x):
    return lax.rsqrt(jnp.mean(x * x, axis=-1, keepdims=True) + RMS_EPS)


def _modnorm_fwd(x, g, shift, scale, name):
    s = x.shape[0]

    def body(x_ref, g_ref, sh_ref, sc_ref, h_ref):
        xv = x_ref[...]
        n = xv * _rstd(xv)
        h_ref[...] = ((n * g_ref[...]) * (1.0 + sc_ref[...]) + sh_ref[...]).astype(BF16)

    vecs, vec_specs = _vec_args(g, shift, scale)
    return pl.pallas_call(
        body, name=name, grid=(s // ROW_TILE,),
        in_specs=[_row_spec(D)] + vec_specs, out_specs=_row_spec(D),
        out_shape=jax.ShapeDtypeStruct((s, D), BF16), compiler_params=_params("parallel"),
    )(x, *vecs)


def _post_fwd(x, y, g, gate, name):
    s = x.shape[0]

    def body(x_ref, y_ref, g_ref, gate_ref, o_ref):
        yv = y_ref[...]
        o_ref[...] = x_ref[...] + gate_ref[...] * ((yv * _rstd(yv)) * g_ref[...])

    vecs, vec_specs = _vec_args(g, gate)
    return pl.pallas_call(
        body, name=name, grid=(s // ROW_TILE,),
        in_specs=[_row_spec(D), _row_spec(D)] + vec_specs, out_specs=_row_spec(D),
        out_shape=jax.ShapeDtypeStruct((s, D), F32), compiler_params=_params("parallel"),
    )(x, y, *vecs)


def _post_bwd(dxo, y, g, gate, name):
    s = dxo.shape[0]

    def body(d_ref, y_ref, g_ref, gate_ref, dy_ref, sum_ref):
        @pl.when(pl.program_id(0) == 0)
        def _():
            sum_ref[...] = jnp.zeros_like(sum_ref)

        dv, yv = d_ref[...], y_ref[...]
        r = _rstd(yv)
        n = yv * r
        sum_ref[0:1, :] += jnp.sum(dv * (n * g_ref[...]), axis=0, keepdims=True)
        sum_ref[1:2, :] += jnp.sum((dv * gate_ref[...]) * n, axis=0, keepdims=True)
        dn = (dv * gate_ref[...]) * g_ref[...]
        dy_ref[...] = (r * (dn - n * jnp.mean(dn * n, axis=-1, keepdims=True))).astype(BF16)

    vecs, vec_specs = _vec_args(g, gate)
    return pl.pallas_call(
        body, name=name, grid=(s // ROW_TILE,),
        in_specs=[_row_spec(D), _row_spec(D)] + vec_specs,
        out_specs=[_row_spec(D), _sum_spec(D)],
        out_shape=[jax.ShapeDtypeStruct((s, D), BF16), jax.ShapeDtypeStruct((8, D), F32)],
        compiler_params=_params("arbitrary"),
    )(dxo, y, *vecs)


def _modnorm_bwd(dh, x, dxo, g, scale, name):
    s = dh.shape[0]

    def body(dh_ref, x_ref, d_ref, g_ref, sc_ref, dx_ref, sum_ref):
        @pl.when(pl.program_id(0) == 0)
        def _():
            sum_ref[...] = jnp.zeros_like(sum_ref)

        dhv, xv = dh_ref[...], x_ref[...]
        r = _rstd(xv)
        n = xv * r
        one_sc = 1.0 + sc_ref[...]
        sum_ref[0:1, :] += jnp.sum(dhv, axis=0, keepdims=True)
        sum_ref[1:2, :] += jnp.sum(dhv * (n * g_ref[...]), axis=0, keepdims=True)
        sum_ref[2:3, :] += jnp.sum((dhv * one_sc) * n, axis=0, keepdims=True)
        dn = (dhv * one_sc) * g_ref[...]
        dx_ref[...] = d_ref[...] + r * (dn - n * jnp.mean(dn * n, axis=-1, keepdims=True))

    vecs, vec_specs = _vec_args(g, scale)
    return pl.pallas_call(
        body, name=name, grid=(s // ROW_TILE,),
        in_specs=[_row_spec(D), _row_spec(D), _row_spec(D)] + vec_specs,
        out_specs=[_row_spec(D), _sum_spec(D)],
        out_shape=[jax.ShapeDtypeStruct((s, D), F32), jax.ShapeDtypeStruct((8, D), F32)],
        compiler_params=_params("arbitrary"),
    )(dh, x, dxo, *vecs)


def _loss_head(y, target):
    s = y.shape[0]

    def body(y_ref, t_ref, dy_ref, sum_ref):
        @pl.when(pl.program_id(0) == 0)
        def _():
            sum_ref[...] = jnp.zeros_like(sum_ref)

        err = y_ref[...] - t_ref[...]
        dy_ref[...] = err * (1.0 / D)
        sum_ref[...] += jnp.sum(err * err)

    return pl.pallas_call(
        body, name="loss_head", grid=(s // ROW_TILE,),
        in_specs=[_row_spec(D), _row_spec(D)],
        out_specs=[_row_spec(D), pl.BlockSpec((8, 128), lambda i: (0, 0))],
        out_shape=[jax.ShapeDtypeStruct((s, D), F32), jax.ShapeDtypeStruct((8, 128), F32)],
        compiler_params=_params("arbitrary"),
    )(y, target)


def _merge_fwd(z, pa, pb, pc):
    s = z.shape[0]

    def body(g0_ref, g1_ref, g2_ref, pa_ref, pb_ref, pc_ref, o_ref):
        o_ref[...] = (jax.nn.sigmoid(g0_ref[...]) * pa_ref[...] + jax.nn.sigmoid(g1_ref[...]) * pb_ref[...]
                      + jax.nn.sigmoid(g2_ref[...]) * pc_ref[...]).astype(BF16)

    return pl.pallas_call(
        body, name="merge_fwd", grid=(s // ROW_TILE,),
        in_specs=[_row_spec(D, 0), _row_spec(D, 1), _row_spec(D, 2), _row_spec(D), _row_spec(D), _row_spec(D)],
        out_specs=_row_spec(D), out_shape=jax.ShapeDtypeStruct((s, D), BF16),
        compiler_params=_params("parallel"),
    )(z, z, z, pa, pb, pc)


def _merge_bwd(dm, z, pa, pb, pc):
    s = z.shape[0]

    def body(dm_ref, g0_ref, g1_ref, g2_ref, pa_ref, pb_ref, pc_ref, dgl_ref, da_ref, db_ref, dc_ref):
        dmv = dm_ref[...]
        for i, (g_ref, p_ref, d_ref) in enumerate(((g0_ref, pa_ref, da_ref), (g1_ref, pb_ref, db_ref), (g2_ref, pc_ref, dc_ref))):
            gate = jax.nn.sigmoid(g_ref[...])
            dgl_ref[:, i * D:(i + 1) * D] = ((dmv * p_ref[...]) * (gate * (1.0 - gate))).astype(BF16)
            d_ref[...] = (dmv * gate).astype(BF16)

    return pl.pallas_call(
        body, name="merge_bwd", grid=(s // ROW_TILE,),
        in_specs=[_row_spec(D), _row_spec(D, 0), _row_spec(D, 1), _row_spec(D, 2), _row_spec(D), _row_spec(D), _row_spec(D)],
        out_specs=[_row_spec(3 * D), _row_spec(D), _row_spec(D), _row_spec(D)],
        out_shape=[jax.ShapeDtypeStruct((s, Z_COLS), BF16)] + [jax.ShapeDtypeStruct((s, D), BF16)] * 3,
        compiler_params=_params("parallel"),
    )(dm, z, z, z, pa, pb, pc)


def _shift_down(v, n):
    row = lax.broadcasted_iota(jnp.int32, v.shape, 0)
    return jnp.where(row >= n, pltpu.roll(v, n, axis=0), 0.0)


def _shift_up(v, n):
    s = v.shape[0]
    row = lax.broadcasted_iota(jnp.int32, v.shape, 0)
    return jnp.where(row < s - n, pltpu.roll(v, s - n, axis=0), 0.0)


def _log_sigmoid(v):
    return jnp.minimum(v, 0.0) - jnp.log1p(jnp.exp(-jnp.abs(v)))


def _cumf_fwd(fl, bias):
    s = fl.shape[0]

    def body(fl_ref, b_ref, o_ref):
        acc = _log_sigmoid(fl_ref[...] + b_ref[...])
        step = 1
        while step < s:
            acc = acc + _shift_down(acc, step)
            step *= 2
        o_ref[...] = acc

    return pl.pallas_call(body, name="cumf_fwd", out_shape=jax.ShapeDtypeStruct((s, 128), F32),
                          compiler_params=pltpu.CompilerParams(vmem_limit_bytes=V7X_VMEM_LIMIT))(fl, bias)


def _cumf_bwd(dcum, fl, bias):
    s = fl.shape[0]

    def body(d_ref, fl_ref, b_ref, dfl_ref, db_ref):
        acc = d_ref[...]
        step = 1
        while step < s:
            acc = acc + _shift_up(acc, step)
            step *= 2
        dfl = acc * jax.nn.sigmoid(-(fl_ref[...] + b_ref[...]))
        dfl_ref[...] = dfl.astype(BF16)
        db_ref[...] = jnp.broadcast_to(jnp.sum(dfl, axis=0, keepdims=True), (8, 128))

    return pl.pallas_call(
        body, name="cumf_bwd",
        out_shape=[jax.ShapeDtypeStruct((s, 128), BF16), jax.ShapeDtypeStruct((8, 128), F32)],
        compiler_params=pltpu.CompilerParams(vmem_limit_bytes=V7X_VMEM_LIMIT))(dcum, fl, bias)


def _pool_windows(v, shift):
    s2 = v + shift(v, 1)
    s4 = s2 + shift(s2, 2)
    s8 = s4 + shift(s4, 4)
    s16 = s8 + shift(s8, 8)
    group = lax.broadcasted_iota(jnp.int32, v.shape, 1) // 64
    return jnp.where(group == 0, s2, jnp.where(group == 1, s4, jnp.where(group == 2, s8, s16)))


def _pool_count(shape):
    group = lax.broadcasted_iota(jnp.int32, shape, 1) // 64
    window = jnp.where(group == 0, 2.0, jnp.where(group == 1, 4.0, jnp.where(group == 2, 8.0, 16.0)))
    t1 = (lax.broadcasted_iota(jnp.int32, shape, 0) + 1).astype(F32)
    return jnp.minimum(t1, window)


def _pc_specs(s):
    zcol = lambda blk: pl.BlockSpec((s, 256), lambda i, blk=blk: (0, blk))
    first = Z_PC // 256
    return [zcol(first), zcol(first + 1), zcol(first + 2), zcol(first + 3),
            pl.BlockSpec((256, 256), lambda i: (0, 0)), pl.BlockSpec((1, 256), lambda i: (0, 0)),
            pl.BlockSpec((3, 256), lambda i: (0, 0))]


def _poolconv_fwd(z, wbd, pscale, convw):
    s = z.shape[0]

    def body(pu_ref, ch_ref, cb_ref, cc_ref, w_ref, ps_ref, cw_ref, yb_ref, yc_ref):
        u = pu_ref[...]
        p = _pool_windows(u, _shift_down) / _pool_count(u.shape) - u
        yb = jnp.dot(p.astype(BF16), w_ref[...].astype(BF16), preferred_element_type=F32) * ps_ref[...]
        yb_ref[...] = yb.astype(BF16)
        uc = cc_ref[...] * ch_ref[...]
        cw = cw_ref[...]
        conv = cw[0:1, :] * _shift_down(uc, 2) + cw[1:2, :] * _shift_down(uc, 1) + cw[2:3, :] * uc
        yc_ref[...] = (cb_ref[...] * conv).astype(BF16)

    out = pl.BlockSpec((s, 256), lambda i: (0, 0))
    return pl.pallas_call(
        body, name="poolconv_fwd", grid=(1,), in_specs=_pc_specs(s), out_specs=[out, out],
        out_shape=[jax.ShapeDtypeStruct((s, 256), BF16)] * 2, compiler_params=_params("arbitrary"),
    )(z, z, z, z, wbd, pscale, convw)


def _poolconv_bwd(dyb, dyc, z, wbd, pscale, convw):
    s = z.shape[0]

    def body(dyb_ref, dyc_ref, pu_ref, ch_ref, cb_ref, cc_ref, w_ref, ps_ref, cw_ref, dz_ref, dw_ref, dps_ref, dcw_ref):
        u = pu_ref[...]
        count = _pool_count(u.shape)
        p = (_pool_windows(u, _shift_down) / count - u).astype(BF16)
        wb = w_ref[...].astype(BF16)
        dyb_v = dyb_ref[...]
        pw = jnp.dot(p, wb, preferred_element_type=F32)
        dps_ref[...] = jnp.broadcast_to(jnp.sum(dyb_v * pw, axis=0, keepdims=True), (8, 256))
        dys = (dyb_v * ps_ref[...]).astype(BF16)
        dp = lax.dot_general(dys, wb, (((1,), (1,)), ((), ())), preferred_element_type=F32)
        dw_ref[...] = lax.dot_general(p, dys, (((0,), (0,)), ((), ())), preferred_element_type=F32)
        dz_ref[:, 0:256] = (_pool_windows(dp / count, _shift_up) - dp).astype(BF16)

        ch, cb, cc = ch_ref[...], cb_ref[...], cc_ref[...]
        uc = cc * ch
        cw = cw_ref[...]
        u2, u1 = _shift_down(uc, 2), _shift_down(uc, 1)
        conv = cw[0:1, :] * u2 + cw[1:2, :] * u1 + cw[2:3, :] * uc
        dyc_v = dyc_ref[...]
        dconv = dyc_v * cb
        du = cw[0:1, :] * _shift_up(dconv, 2) + cw[1:2, :] * _shift_up(dconv, 1) + cw[2:3, :] * dconv
        dz_ref[:, 256:512] = (du * cc).astype(BF16)
        dz_ref[:, 512:768] = (dyc_v * conv).astype(BF16)
        dz_ref[:, 768:1024] = (du * ch).astype(BF16)
        dcw_ref[...] = jnp.zeros_like(dcw_ref)
        dcw_ref[0:1, :] = jnp.sum(dconv * u2, axis=0, keepdims=True)
        dcw_ref[1:2, :] = jnp.sum(dconv * u1, axis=0, keepdims=True)
        dcw_ref[2:3, :] = jnp.sum(dconv * uc, axis=0, keepdims=True)

    blk = lambda r, c: pl.BlockSpec((r, c), lambda i: (0, 0))
    return pl.pallas_call(
        body, name="poolconv_bwd", grid=(1,),
        in_specs=[blk(s, 256), blk(s, 256)] + _pc_specs(s),
        out_specs=[blk(s, 1024), blk(256, 256), blk(8, 256), blk(8, 256)],
        out_shape=[jax.ShapeDtypeStruct((s, 1024), BF16), jax.ShapeDtypeStruct((256, 256), F32),
                   jax.ShapeDtypeStruct((8, 256), F32), jax.ShapeDtypeStruct((8, 256), F32)],
        compiler_params=_params("arbitrary"),
    )(dyb, dyc, z, z, z, z, wbd, pscale, convw)


_NT = (((1,), (1,)), ((), ()))
_TN = (((0,), (0,)), ((), ()))


ATT_Q, ATT_K = 256, 256
ATT_HEADS_BWD = 8
ATT_HEADS = 8


def _att_logits(q, k, fr, q0, k0, masked):
    logits = lax.dot_general(q, k, _NT, preferred_element_type=F32) - fr
    if not masked:
        return logits
    row = q0 + lax.broadcasted_iota(jnp.int32, logits.shape, 0)
    col = k0 + lax.broadcasted_iota(jnp.int32, logits.shape, 1)
    return jnp.where(row >= col, logits, NEG_INF)


def _causal_sweep(step, qi, init):
    n_full = (qi * ATT_Q) // ATT_K
    carry = lax.fori_loop(0, n_full, lambda j, carry: step(j, carry, False), init)
    return step(n_full, carry, True)


HEAD_PAIRS = HEADS // 2


def _lane_pick(v, lane, idx):
    return jnp.sum(jnp.where(lane == idx, v, 0.0), axis=-1, keepdims=True)


def _lane_put(lane, idx, col):
    return jnp.where(lane == idx, col, 0.0)


def _split_heads(v, low):
    zero = jnp.zeros_like(v)
    return jnp.where(low, v, zero), jnp.where(low, zero, v)


def _attn_fwd(qkv, fr):
    s = qkv.shape[0]
    nk = s // ATT_K
    width = ATT_HEADS * HEAD_DIM
    groups = HEADS // ATT_HEADS

    def body(q_ref, k_ref, v_ref, fr_ref, o_ref, lse_ref):
        qi, grp = pl.program_id(0), pl.program_id(1)
        lane = lax.broadcasted_iota(jnp.int32, (ATT_Q, 128), 1)
        low = lane < HEAD_DIM
        qs = []
        for pr in range(ATT_HEADS // 2):
            qs += _split_heads(q_ref[:, 128 * pr:128 * (pr + 1)] * (HEAD_DIM ** -0.5), low)

        def step(j, carry, masked):
            k0 = pl.multiple_of(j * ATT_K, ATT_K)
            out = []
            for h in range(ATT_HEADS):
                cols = slice(128 * (h // 2), 128 * (h // 2 + 1))
                m, l, acc = carry[h]
                logits = _att_logits(qs[h], k_ref[pl.ds(k0, ATT_K), cols], fr_ref[h, pl.ds(j, 1), :], qi * ATT_Q, k0, masked)
                m_new = jnp.maximum(m, jnp.max(logits, axis=-1, keepdims=True))
                p = jnp.exp(logits - m_new)
                alpha = jnp.exp(m - m_new)
                l = alpha * l + jnp.sum(p, axis=-1, keepdims=True)
                acc = alpha * acc + jnp.dot(p.astype(BF16), v_ref[pl.ds(k0, ATT_K), cols], preferred_element_type=F32)
                out.append((m_new, l, acc))
            return tuple(out)

        one = (jnp.full((ATT_Q, 1), NEG_INF, F32), jnp.zeros((ATT_Q, 1), F32), jnp.zeros((ATT_Q, 128), F32))
        done = _causal_sweep(step, qi, (one,) * ATT_HEADS)

        @pl.when(grp == 0)
        def _():
            lse_ref[...] = jnp.zeros_like(lse_ref)

        lse = jnp.zeros((ATT_Q, 128), F32)
        for pr in range(ATT_HEADS // 2):
            (m0, l0, acc0), (m1, l1, acc1) = done[2 * pr], done[2 * pr + 1]
            o_ref[:, 128 * pr:128 * (pr + 1)] = jnp.where(low, acc0 / l0, acc1 / l1)
            head = ATT_HEADS * grp + 2 * pr
            lse = lse + _lane_put(lane, head, m0 + jnp.log(l0)) + _lane_put(lane, head + 1, m1 + jnp.log(l1))
        lse_ref[...] += lse

    return pl.pallas_call(
        body, name="attn_fwd", grid=(s // ATT_Q, groups),
        in_specs=[pl.BlockSpec((ATT_Q, width), lambda i, g: (i, g)),
                  pl.BlockSpec((s, width), lambda i, g: (0, groups + g)),
                  pl.BlockSpec((s, width), lambda i, g: (0, 2 * groups + g)),
                  pl.BlockSpec((ATT_HEADS, nk, ATT_K), lambda i, g: (g, 0, 0))],
        out_specs=[pl.BlockSpec((ATT_Q, width), lambda i, g: (i, g)), pl.BlockSpec((ATT_Q, 128), lambda i, g: (i, 0))],
        out_shape=[jax.ShapeDtypeStruct((s, A_WIDTH), F32), jax.ShapeDtypeStruct((s, 128), F32)],
        compiler_params=_params("parallel", "arbitrary"),
    )(qkv, qkv, qkv, fr)


def _attn_bwd(qkv, do, o, lse, fr):
    s = qkv.shape[0]
    nk = s // ATT_K
    scale = HEAD_DIM ** -0.5
    heads = ATT_HEADS_BWD
    width = heads * HEAD_DIM
    groups = HEADS // heads

    def body(q_ref, k_ref, v_ref, do_ref, o_ref, lse_ref, fr_ref, dq_ref, dk_ref, dv_ref, dfc_ref, dfr_ref, dk_acc, dv_acc):
        grp = pl.program_id(0)
        lane = lax.broadcasted_iota(jnp.int32, (ATT_Q, 128), 1)
        low = lane < HEAD_DIM
        low_t = lax.broadcasted_iota(jnp.int32, (128, ATT_Q), 0) < HEAD_DIM
        dk_acc[...] = jnp.zeros_like(dk_acc)
        dv_acc[...] = jnp.zeros_like(dv_acc)
        dfr_ref[...] = jnp.zeros_like(dfr_ref)

        @pl.when(grp == 0)
        def _():
            dfc_ref[...] = jnp.zeros_like(dfc_ref)

        def outer(i, carry):
            q0 = pl.multiple_of(i * ATT_Q, ATT_Q)
            rows = pl.ds(q0, ATT_Q)
            lsev = lse_ref[rows, :]
            qts, dots, qs, dos, deltas, lses = [], [], [], [], [], []
            for pr in range(heads // 2):
                pcols = slice(128 * pr, 128 * (pr + 1))
                q2, do2 = q_ref[rows, pcols] * scale, do_ref[rows, pcols]
                prod = do2 * o_ref[rows, pcols]
                deltas += [jnp.sum(jnp.where(low, prod, 0.0), axis=-1, keepdims=True),
                           jnp.sum(jnp.where(low, 0.0, prod), axis=-1, keepdims=True)]
                dob2 = do2.astype(BF16)
                qts += _split_heads(q2.astype(F32).T.astype(BF16), low_t)
                dots += _split_heads(do2.T.astype(BF16), low_t)
                qs += _split_heads(q2, low)
                dos += _split_heads(dob2, low)
                lses += [_lane_pick(lsev, lane, heads * grp + 2 * pr), _lane_pick(lsev, lane, heads * grp + 2 * pr + 1)]

            def inner(j, carry, masked):
                k0 = pl.multiple_of(j * ATT_K, ATT_K)
                krows = pl.ds(k0, ATT_K)
                out, dkt, dvt = [], [], []
                for h in range(heads):
                    pcols = slice(128 * (h // 2), 128 * (h // 2 + 1))
                    dq, dfc = carry[h]
                    k2 = k_ref[krows, pcols]
                    p = jnp.exp(_att_logits(qs[h], k2, fr_ref[h, pl.ds(j, 1), :], q0, k0, masked) - lses[h])
                    dp = lax.dot_general(dos[h], v_ref[krows, pcols], _NT, preferred_element_type=F32)
                    ds = p * (dp - deltas[h])
                    dsb = ds.astype(BF16)
                    dkt.append(jnp.dot(qts[h], dsb, preferred_element_type=F32))
                    dvt.append(jnp.dot(dots[h], p.astype(BF16), preferred_element_type=F32))
                    dfr_ref[h, pl.ds(j, 1), :] -= jnp.sum(ds, axis=0, keepdims=True)
                    out.append((dq + jnp.dot(dsb, k2, preferred_element_type=F32), dfc + (ds[:, :128] + ds[:, 128:])))
                for pr in range(heads // 2):
                    prows = slice(128 * pr, 128 * (pr + 1))
                    dk_acc[j, prows, :] += dkt[2 * pr] + dkt[2 * pr + 1]
                    dv_acc[j, prows, :] += dvt[2 * pr] + dvt[2 * pr + 1]
                return tuple(out)

            one = (jnp.zeros((ATT_Q, 128), F32), jnp.zeros((ATT_Q, 128), F32))
            done = _causal_sweep(inner, i, (one,) * heads)
            dfc = jnp.zeros((ATT_Q, 128), F32)
            for pr in range(heads // 2):
                (dq0, dfc0), (dq1, dfc1) = done[2 * pr], done[2 * pr + 1]
                dq_ref[rows, 128 * pr:128 * (pr + 1)] = (jnp.where(low, dq0, dq1) * scale).astype(BF16)
                head = heads * grp + 2 * pr
                dfc = (dfc + _lane_put(lane, head, jnp.sum(dfc0, axis=-1, keepdims=True))
                       + _lane_put(lane, head + 1, jnp.sum(dfc1, axis=-1, keepdims=True)))
            dfc_ref[rows, :] += dfc
            return carry

        lax.fori_loop(0, s // ATT_Q, outer, 0)
        for j in range(nk):
            for pr in range(heads // 2):
                prows, pcols = slice(128 * pr, 128 * (pr + 1)), slice(128 * pr, 128 * (pr + 1))
                dk_ref[ATT_K * j:ATT_K * (j + 1), pcols] = dk_acc[j, prows, :].T.astype(BF16)
                dv_ref[ATT_K * j:ATT_K * (j + 1), pcols] = dv_acc[j, prows, :].T.astype(BF16)

    part = lambda first: pl.BlockSpec((s, width), lambda g, first=first: (0, first + g))
    whole = pl.BlockSpec((s, 128), lambda g: (0, 0))
    rowv = pl.BlockSpec((heads, nk, ATT_K), lambda g: (g, 0, 0))
    return pl.pallas_call(
        body, name="attn_bwd", grid=(groups,),
        in_specs=[part(0), part(groups), part(2 * groups), part(0), part(0), whole, rowv],
        out_specs=[part(0), part(0), part(0), whole, rowv],
        out_shape=[jax.ShapeDtypeStruct((s, A_WIDTH), BF16)] * 3 + [jax.ShapeDtypeStruct((s, 128), F32), jax.ShapeDtypeStruct((HEADS, nk, ATT_K), F32)],
        scratch_shapes=[pltpu.VMEM((nk, width, ATT_K), F32), pltpu.VMEM((nk, width, ATT_K), F32)],
        compiler_params=_params("arbitrary"),
    )(qkv, qkv, qkv, do, o, lse, fr)


def _ada_fwd(c_all, w_ada, b_loc):
    depth, _, n = w_ada.shape
    tn = 512

    def body(c_ref, w_ref, b_ref, o_ref, sc_ref):
        cv = c_ref[...]
        sc = cv * jax.nn.sigmoid(cv)
        sc_ref[...] = sc
        o_ref[0] = jnp.dot(sc.astype(BF16), w_ref[0].astype(BF16), preferred_element_type=F32) + b_ref[0]

    return pl.pallas_call(
        body, name="ada_fwd", grid=(depth, n // tn),
        in_specs=[pl.BlockSpec((N_DEV, D), lambda l, j: (0, 0)), pl.BlockSpec((1, D, tn), lambda l, j: (l, 0, j)),
                  pl.BlockSpec((1, 1, tn), lambda l, j: (l, 0, j))],
        out_specs=[pl.BlockSpec((1, N_DEV, tn), lambda l, j: (l, 0, j)), pl.BlockSpec((N_DEV, D), lambda l, j: (0, 0))],
        out_shape=[jax.ShapeDtypeStruct((depth, N_DEV, n), F32), jax.ShapeDtypeStruct((N_DEV, D), F32)],
        compiler_params=_params("arbitrary", "arbitrary"),
    )(c_all, w_ada, b_loc)


def _sum_devices(gathered):
    n = gathered.shape[1]
    tn = _pick(n, (1408, 1024, 640, 512, 128))

    def body(g_ref, o_ref):
        acc = g_ref[0:8, :]
        for dev in range(1, N_DEV):
            acc = acc + g_ref[8 * dev:8 * dev + 8, :]
        o_ref[...] = acc

    return pl.pallas_call(
        body, name="sum_devices", grid=(n // tn,),
        in_specs=[pl.BlockSpec((8 * N_DEV, tn), lambda j: (0, j))], out_specs=pl.BlockSpec((8, tn), lambda j: (0, j)),
        out_shape=jax.ShapeDtypeStruct((8, n), F32), compiler_params=_params("parallel"),
    )(gathered)


def _place():
    x, y, c = lax.axis_index("x"), lax.axis_index("y"), lax.axis_index("c")
    chips = [(1 - x, y), (x, 1 - y), (1 - x, 1 - y)]
    return x, y, c, chips


def _allgather8(block, name):
    m_per, n = block.shape

    def body(x_ref, out_ref, send_sems, recv_sems, local_sem):
        x, y, c, chips = _place()
        me, sibling = (x, y, c), (x, y, 1 - c)

        def rows(px, py, pc):
            return out_ref.at[pl.ds((4 * px + 2 * py + pc) * m_per, m_per), :]

        def copy(k, blk, to, src=None):
            return pltpu.make_async_remote_copy(
                src_ref=rows(*blk) if src is None else src, dst_ref=rows(*blk),
                send_sem=send_sems.at[k], recv_sem=recv_sems.at[k], device_id=to, device_id_type=MESH)

        mine = pltpu.make_async_copy(x_ref, rows(*me), local_sem)
        mine.start()
        first = [copy(0, me, sibling, src=x_ref)]
        first += [copy(1 + j, me, (*chip, c), src=x_ref) for j, chip in enumerate(chips)]
        for cp in first:
            cp.start()
        passed = [copy(4 + j, (*chip, c), sibling) for j, chip in enumerate(chips)]
        for j, chip in enumerate(chips):
            copy(1 + j, (*chip, c), me).wait_recv()
            passed[j].start()
        copy(0, sibling, me).wait_recv()
        for j, chip in enumerate(chips):
            copy(4 + j, (*chip, 1 - c), me).wait_recv()
        for cp in first + passed:
            cp.wait_send()
        mine.wait()

    return pl.pallas_call(
        body, name=name, out_shape=jax.ShapeDtypeStruct((N_DEV * m_per, n), block.dtype),
        in_specs=[pl.BlockSpec(memory_space=pltpu.VMEM)], out_specs=pl.BlockSpec(memory_space=pltpu.VMEM),
        scratch_shapes=[pltpu.SemaphoreType.DMA((7,)), pltpu.SemaphoreType.DMA((7,)), pltpu.SemaphoreType.DMA],
        compiler_params=pltpu.CompilerParams(vmem_limit_bytes=V7X_VMEM_LIMIT),
    )(block)


_SEM = pl.BlockSpec(memory_space=pltpu.SEMAPHORE)
_DATAFLOW = pltpu.SideEffectType.DATAFLOW_SIDE_EFFECTING


def _plan_copies(plan, refs, send_sems, recv_sems):
    return [pltpu.make_async_remote_copy(src_ref=src, dst_ref=dst, send_sem=send_sems.at[i], recv_sem=recv_sems.at[i],
                                         device_id=to, device_id_type=MESH) for i, (src, dst, to) in enumerate(plan(refs))]


class _Token(NamedTuple):
    after: jax.Array
    tie: jax.Array


def _after_operand(after):
    return after.after if isinstance(after, _Token) else after


def _copies_start(bufs, plan, n_copies, after, name):
    nb = len(bufs)

    def body(*refs):
        for cp in _plan_copies(plan, refs[:nb], refs[nb + 1], refs[nb + 2]):
            cp.start()
        for token in refs[-2:]:
            token[...] = jnp.zeros_like(token)

    sem = pltpu.SemaphoreType.DMA((n_copies,))
    vmem = pl.BlockSpec(memory_space=pltpu.VMEM)
    outs = pl.pallas_call(
        body, name=name,
        out_shape=(sem, sem, *[pltpu.HBM(b.shape, b.dtype) for b in bufs], jax.ShapeDtypeStruct((8, 128), F32),
                   jax.ShapeDtypeStruct((1, 1), F32)),
        in_specs=[_HBM] * nb + [pl.BlockSpec(memory_space=pl.ANY)],
        out_specs=(_SEM, _SEM, *[_HBM] * nb, vmem, vmem),
        input_output_aliases={i: 2 + i for i in range(nb)},
        compiler_params=pltpu.CompilerParams(has_side_effects=_DATAFLOW),
    )(*[pltpu.with_memory_space_constraint(b, pltpu.HBM) for b in bufs], _after_operand(after))
    return outs[0], outs[1], list(outs[2:2 + nb]), _Token(outs[-2], outs[-1])


def _copies_wait(started, plan, after, name):
    send_sems, recv_sems, bufs, _ = started
    nb = len(bufs)

    def body(*refs):
        for cp in _plan_copies(plan, refs[:nb], refs[nb], refs[nb + 1]):
            cp.wait_send()
            cp.wait_recv()

    return list(pl.pallas_call(
        body, name=name, out_shape=tuple(pltpu.HBM(b.shape, b.dtype) for b in bufs),
        in_specs=[_HBM] * nb + [_SEM, _SEM, pl.BlockSpec(memory_space=pl.ANY)], out_specs=tuple([_HBM] * nb),
        input_output_aliases={i: i for i in range(nb)},
        compiler_params=pltpu.CompilerParams(has_side_effects=_DATAFLOW),
    )(*bufs, send_sems, recv_sems, _after_operand(after)))


def _half_rows(ref, axis, c):
    half = ref.shape[axis] // 2
    return pl.ds(c * half, half)


def _plan_gather_ici(refs):
    n = len(refs) // 2
    x, y, c, chips = _place()
    out = []
    for a in range(n):
        rows = _half_rows(refs[a], 0, c)
        out += [(refs[a].at[rows], refs[n + a].at[2 * x + y, rows], (*chip, c)) for chip in chips]
        out.append((refs[a], refs[n + a].at[2 * x + y], (x, y, 1 - c)))
    return out


def _plan_gather_d2d(refs):
    x, y, c, chips = _place()
    out = []
    for ref in refs:
        rows = _half_rows(ref, 1, c)
        for px, py in chips:
            landed = ref.at[2 * px + py, rows]
            out.append((landed, landed, (x, y, 1 - c)))
    return out


def _plan_rs_sibling(refs):
    n = len(refs) // 2
    x, y, c, _ = _place()
    return [(refs[a].at[pl.ds(0, N_CHIPS), _half_rows(refs[a], 1, 1 - c)], refs[n + a], (x, y, 1 - c)) for a in range(n)]


def _plan_rs_chips(refs):
    n = len(refs) // 2
    x, y, c, chips = _place()
    return [(refs[a].at[2 * px + py], refs[n + a].at[k], (px, py, c)) for a in range(n) for k, (px, py) in enumerate(chips)]


def _plan_rs_share(layer):
    def plan(refs):
        x, y, c, _ = _place()
        return [(ref.at[layer, _half_rows(ref, 1, c)], ref.at[layer, _half_rows(ref, 1, c)], (x, y, 1 - c)) for ref in refs]
    return plan


def _chip_sum(g, other, sel, name):
    _, half, cdim = other.shape
    tr = _pick(half, (512, 256, 128, 64))
    per = half // tr

    def body(sel_ref, g_ref, t_ref, wire_ref, own_ref):
        total = g_ref[0] + t_ref[0]
        wire_ref[0] = total.astype(BF16)

        @pl.when(pl.program_id(1) == sel_ref[1])
        def _():
            own_ref[...] = total

    blk = pl.BlockSpec((1, tr, cdim), lambda i, p, sel_ref: (p, i, 0))
    return pl.pallas_call(
        body, name=name,
        grid_spec=pltpu.PrefetchScalarGridSpec(
            num_scalar_prefetch=1, grid=(per, N_CHIPS),
            in_specs=[pl.BlockSpec((1, tr, cdim), lambda i, p, sel_ref: (p, sel_ref[0] * per + i, 0)), blk],
            out_specs=[blk, pl.BlockSpec((tr, cdim), lambda i, p, sel_ref: (i, 0))]),
        out_shape=[jax.ShapeDtypeStruct(other.shape, BF16), jax.ShapeDtypeStruct((half, cdim), F32)],
        compiler_params=_params("parallel", "arbitrary"),
    )(sel, g, other)


def _final_sum(own, recv, sel, layer, into, name):
    half, cdim = own.shape
    tr = _pick(half, (512, 256, 128, 64))
    per = half // tr

    def body(sel_ref, own_ref, r0_ref, r1_ref, r2_ref, *rest):
        rest[-1][...] = ((own_ref[...] + r0_ref[0].astype(F32)) + r1_ref[0].astype(F32)) + r2_ref[0].astype(F32)

    part = lambda k: pl.BlockSpec((1, tr, cdim), lambda i, sel_ref, k=k: (k, i, 0))
    prior = [] if into is None else [into]
    return pl.pallas_call(
        body, name=name,
        grid_spec=pltpu.PrefetchScalarGridSpec(
            num_scalar_prefetch=1, grid=(per,),
            in_specs=[pl.BlockSpec((tr, cdim), lambda i, sel_ref: (i, 0)), part(0), part(1), part(2)]
            + [pl.BlockSpec(memory_space=pl.ANY)] * len(prior),
            out_specs=pl.BlockSpec((None, tr, cdim), lambda i, sel_ref: (layer, sel_ref[0] * per + i, 0))),
        out_shape=jax.ShapeDtypeStruct((DEPTH, 2 * half, cdim), F32),
        input_output_aliases={5: 0} if prior else {}, compiler_params=_params("parallel"),
    )(sel, own, recv, recv, recv, *prior)


def _row(v):
    return v.reshape(1, -1)


_BR_A, _BR_B, _BR_C = (0, A_WIDTH), (A_WIDTH, POOL_WIDTH), (A_WIDTH + POOL_WIDTH, CONV_WIDTH)


def _tie(v, token):
    return v if token is None else v + token.tie


def _no_hook(point, after, ready=None):
    return None


def _layer_fwd(x, w, mod, hook=_no_hook):
    s = x.shape[0]
    mod3 = mod.reshape(6, 1, D)
    h = _modnorm_fwd(x, _row(w["g_mix_pre"]), (mod3, 0), (mod3, 1), "mix_pre_fwd")
    z = _mm(h, w["w_all"], name="mm_in")
    qkv = z[:, Z_QKV:Z_PC].astype(BF16)
    fl = z[:, Z_FL:Z_COLS]
    cum = _cumf_fwd(fl, w["b_f_pad"])
    fr = cum[:, :HEADS].T.reshape(HEADS, s // ATT_K, ATT_K)
    br_a, lse = _attn_fwd(qkv, fr)
    br_b, br_c = _poolconv_fwd(z, w["w_pool_bd"], _tie(_row(w["pool_scale"]), hook("attn", lse)), w["conv_w"])
    hook("pool", br_b)
    wbr = w["w_branch"]
    pa = _mm(br_a, wbr, b_rows=_BR_A, name="mm_br_a")
    pb = _mm(br_b, wbr, b_rows=_BR_B, name="mm_br_b")
    pc = _mm(br_c, wbr, b_rows=_BR_C, name="mm_br_c")
    merged = _merge_fwd(z, pa, pb, pc)
    y = _mm(merged, w["w_out"], name="mm_out")
    x1 = _post_fwd(x, y, _row(w["g_mix_post"]), (mod3, 2), "mix_post_fwd")
    h2 = _modnorm_fwd(x1, _row(w["g_ff_pre"]), (mod3, 3), (mod3, 4), "ff_pre_fwd")
    a, r = _mm(h2, w["w_ff1"], b_split=N_CHIPS, epilogue=_relu2_fwd, out_dtype=(F32, BF16), name="mm_ff1")
    y2 = _mm(r, w["w_ff2"], name="mm_ff2")
    x2 = _post_fwd(x1, y2, _tie(_row(w["g_ff_post"]), hook("ff_post", y2)), (mod3, 5), "ff_post_fwd")
    hook("end", x2)
    saved = dict(x=x, h=h, z=z, qkv=qkv, fl=fl, fr=fr, lse=lse, br_a=br_a, br_b=br_b, br_c=br_c, pa=pa, pb=pb, pc=pc,
                 merged=merged, y=y, x1=x1, h2=h2, a=a, r=r, y2=y2)
    return x2, saved


def _layer_bwd(dx2, sv, w, mod, hook=_no_hook):
    s = dx2.shape[0]
    mod3 = mod.reshape(6, 1, D)
    dy2, sum_ff_post = _post_bwd(dx2, sv["y2"], _row(w["g_ff_post"]), (mod3, 5), "ff_post_bwd")
    (da,) = _mm(dy2, w["w_ff2"], tb=True, epilogue=_relu2_bwd, extras=(sv["a"],), out_dtype=(BF16,), name="mm_ff2_dx")
    d_w_ff2 = _mm(sv["r"], dy2, ta=True, name="mm_ff2_dw")
    dh2 = _mm(da, w["w_ff1"], tb=True, b_split=N_CHIPS, name="mm_ff1_dx")
    d_w_ff1 = _mm(sv["h2"], da, ta=True, out_split=N_CHIPS, name="mm_ff1_dw")
    dx1, sum_ff_pre = _modnorm_bwd(dh2, sv["x1"], dx2, _tie(_row(w["g_ff_pre"]), hook("ff_pre", dh2, dict(w_ff1=d_w_ff1, w_ff2=d_w_ff2))), (mod3, 4), "ff_pre_bwd")

    dy, sum_mix_post = _post_bwd(dx1, sv["y"], _row(w["g_mix_post"]), (mod3, 2), "mix_post_bwd")
    dmerged = _mm(dy, w["w_out"], tb=True, name="mm_out_dx")
    d_w_out = _mm(sv["merged"], dy, ta=True, name="mm_out_dw")
    dz, dpa, dpb, dpc = _merge_bwd(dmerged, sv["z"], sv["pa"], sv["pb"], sv["pc"])
    wbr = w["w_branch"]
    dbr_a = _mm(dpa, wbr, tb=True, b_rows=_BR_A, name="mm_br_a_dx")
    dbr_b = _mm(dpb, wbr, tb=True, b_rows=_BR_B, name="mm_br_b_dx")
    dbr_c = _mm(dpc, wbr, tb=True, b_rows=_BR_C, name="mm_br_c_dx")
    d_w_branch = jnp.concatenate([_mm(sv["br_a"], dpa, ta=True, name="mm_br_a_dw"), _mm(sv["br_b"], dpb, ta=True, name="mm_br_b_dw"),
                                  _mm(sv["br_c"], dpc, ta=True, name="mm_br_c_dw")], axis=0)

    dq, dk, dv, dfc, dfr = _attn_bwd(sv["qkv"], dbr_a, sv["br_a"], sv["lse"], sv["fr"])
    dcum = dfc + jnp.pad(dfr.reshape(HEADS, s).T, ((0, 0), (0, 128 - HEADS)))
    dfl, sum_bf = _cumf_bwd(dcum, sv["fl"], _tie(w["b_f_pad"], hook("cumf", dfc)))
    dpc_z, d_wbd, sum_ps, sum_cw = _poolconv_bwd(dbr_b, dbr_c, sv["z"], w["w_pool_bd"], _row(w["pool_scale"]), w["conv_w"])
    for at, part in ((Z_QKV, dq), (Z_QKV + A_WIDTH, dk), (Z_QKV + 2 * A_WIDTH, dv), (Z_PC, dpc_z), (Z_FL, dfl)):
        dz = lax.dynamic_update_slice(dz, part, (0, at))
    dh = _mm(dz, w["w_all"], tb=True, name="mm_in_dx")
    d_w_all = _mm(sv["h"], dz, ta=True, name="mm_in_dw")
    hook("mix_pre", dh)
    dx, sum_mix_pre = _modnorm_bwd(dh, sv["x"], dx1, _row(w["g_mix_pre"]), (mod3, 1), "mix_pre_bwd")

    dmod = jnp.stack([sum_mix_pre[0], sum_mix_pre[1], sum_mix_post[0], sum_ff_pre[0], sum_ff_pre[1], sum_ff_post[0]])
    d_w_in = _w_in_shards(d_w_all)
    d_w_pool = jnp.stack([d_wbd[64 * g:64 * g + 64, 64 * g:64 * g + 64] for g in range(4)])
    big = dict(w_in=d_w_in, w_branch=d_w_branch, w_out=d_w_out, w_ff1=d_w_ff1, w_ff2=d_w_ff2)
    small = dict(g_mix_pre=sum_mix_pre[2], g_mix_post=sum_mix_post[1], g_ff_pre=sum_ff_pre[2], g_ff_post=sum_ff_post[1],
                 b_f=sum_bf[0, :HEADS], w_pool=d_w_pool, pool_scale=sum_ps[0], conv_w=sum_cw[0:3])
    return dx, dmod, big, small


_QKV_END, _FL_END, _PC_END = 3 * A_WIDTH, 3 * A_WIDTH + HEADS, 3 * A_WIDTH + HEADS + POOL_WIDTH + 3 * CONV_WIDTH
_W_IN_GROUPS = ((_PC_END, IN_COLS, Z_GL), (0, _QKV_END, Z_QKV), (_FL_END, _PC_END, Z_PC), (_QKV_END, _FL_END, Z_FL))
_SHARD_COLS = IN_COLS // N_CHIPS


def _w_all_from_shards(blocks):
    pieces = []
    for lo, hi, _ in _W_IN_GROUPS:
        for p in range(N_CHIPS):
            a, b = max(lo, p * _SHARD_COLS), min(hi, (p + 1) * _SHARD_COLS)
            if a < b:
                pieces.append(blocks[p][:, a - p * _SHARD_COLS:b - p * _SHARD_COLS])
    pieces.append(jnp.zeros((D, Z_COLS - IN_COLS), blocks.dtype))
    return jnp.concatenate(pieces, axis=1)


def _w_in_shards(d_w_all):
    blocks = []
    for p in range(N_CHIPS):
        pieces = []
        for lo, hi, at in sorted(_W_IN_GROUPS):
            a, b = max(lo, p * _SHARD_COLS), min(hi, (p + 1) * _SHARD_COLS)
            if a < b:
                pieces.append(d_w_all[:, at + a - lo:at + b - lo])
        blocks.append(jnp.concatenate(pieces, axis=1))
    return jnp.stack(blocks)


def _full_layer_weights(w_in_blocks, w_branch, w_out, w_ff1, w_ff2, g_mix_pre, g_mix_post, g_ff_pre, g_ff_post, b_f, w_pool, pool_scale, conv_w):
    w_all = _w_all_from_shards(w_in_blocks)
    wbd = (w_pool[:, :, None, :] * jnp.eye(4, dtype=F32)[:, None, :, None]).reshape(POOL_WIDTH, POOL_WIDTH)
    return dict(w_all=w_all, w_branch=w_branch, w_out=w_out, w_ff1=w_ff1, w_ff2=w_ff2, g_mix_pre=g_mix_pre, g_mix_post=g_mix_post,
                g_ff_pre=g_ff_pre, g_ff_post=g_ff_post, b_f_pad=jnp.pad(b_f, (0, 128 - HEADS)).reshape(1, 128), w_pool_bd=wbd,
                pool_scale=pool_scale, conv_w=conv_w)


class _NoComm:
    def layer_weights(self, l):
        raise NotImplementedError

    def fwd_hook(self, l):
        return _no_hook

    def bwd_hook(self, l):
        return _no_hook

    def grads_ready(self, l, big):
        return None


class _Layers(_NoComm):
    def __init__(self, layers):
        self.layers = layers

    def layer_weights(self, l):
        return self.layers[l]


def _local_step(x, target, mods, comm):
    saved, weights = [], []
    act = x
    for l in range(DEPTH):
        weights.append(comm.layer_weights(l))
        act, sv = _layer_fwd(act, weights[l], mods[l], comm.fwd_hook(l))
        saved.append(sv)
    dact, sq = _loss_head(act, target)
    loss = sq[0, 0] * (0.5 / D)
    dmods, bigs, smalls = [None] * DEPTH, [None] * DEPTH, [None] * DEPTH
    token = None
    for l in reversed(range(DEPTH)):
        dact, dmods[l], bigs[l], smalls[l] = _layer_bwd(dact, saved[l], weights[l], _tie(mods[l], token), comm.bwd_hook(l))
        token = comm.grads_ready(l, bigs[l])
    return loss, dact, jnp.stack(dmods), bigs, smalls


_BIG = ("w_in", "w_branch", "w_out", "w_ff1", "w_ff2")
N_BIG = len(_BIG)


class _GatherJob:
    def __init__(self, tag, shards, after):
        self.tag, self.n = tag, len(shards)
        lands = [lax.empty((N_CHIPS,) + s.shape, s.dtype) for s in shards]
        self.state = _copies_start(list(shards) + lands, _plan_gather_ici, 4 * self.n, after, "gather_ici_start_" + tag)
        self.token = self.state[3]

    def pass_on(self, after):
        bufs = _copies_wait(self.state, _plan_gather_ici, after, "gather_ici_wait_" + self.tag)
        self.state = _copies_start(bufs[self.n:], _plan_gather_d2d, 3 * self.n, bufs[0], "gather_d2d_start_" + self.tag)
        self.token = self.state[3]
        return self.token

    def done(self, after):
        return _copies_wait(self.state, _plan_gather_d2d, after, "gather_d2d_wait_" + self.tag)


class _ReduceJob:
    def __init__(self, tag, names, grads, sel, after, layer, into=None):
        self.tag, self.names, self.n, self.sel, self.layer, self.into = tag, names, len(names), sel, layer, into or {}
        lands = [lax.empty((N_CHIPS, g.shape[1] // 2, g.shape[2]), F32) for g in grads]
        self.state = _copies_start(list(grads) + lands, _plan_rs_sibling, self.n, after, "rs_sibling_start_" + tag)
        self.token = self.state[3]

    def chip_sums(self, after):
        bufs = _copies_wait(self.state, _plan_rs_sibling, after, "rs_sibling_wait_" + self.tag)
        wires, self.owns = zip(*[_chip_sum(bufs[i], bufs[self.n + i], self.sel, "rs_chip_sum_" + name) for i, name in enumerate(self.names)])
        lands = [lax.empty((3,) + w.shape[1:], BF16) for w in wires]
        self.state = _copies_start(list(wires) + lands, _plan_rs_chips, 3 * self.n, self.owns[0], "rs_chips_start_" + self.tag)
        self.token = self.state[3]
        return self.token

    def final_sums(self, after):
        bufs = _copies_wait(self.state, _plan_rs_chips, after, "rs_chips_wait_" + self.tag)
        sums = [_final_sum(self.owns[i], bufs[self.n + i], self.sel, self.layer, self.into.get(name), "rs_final_" + name)
                for i, name in enumerate(self.names)]
        self.state = _copies_start(sums, _plan_rs_share(self.layer), self.n, sums[0], "rs_share_start_" + self.tag)
        self.token = self.state[3]
        return self.token

    def done(self, after):
        return dict(zip(self.names, _copies_wait(self.state, _plan_rs_share(self.layer), after, "rs_share_wait_" + self.tag)))


def _chip_blocks(g):
    return g if g.ndim == 3 else g.reshape(N_CHIPS, -1, g.shape[1])


class _StepComm(_NoComm):
    def __init__(self, shards, sel, after):
        self.sel = sel
        self.small, self.grads, self.jobs = None, {}, {}
        self.jobs["in0"] = _GatherJob("in0", shards[0][:1], after)
        self.jobs["rest0"] = _GatherJob("rest0", shards[0][1:], self.jobs["in0"].token)
        self.jobs["all1"] = _GatherJob("all1", shards[1], self.jobs["rest0"].token)

    def layer_weights(self, l):
        if l == 0:
            job = self.jobs["in0"]
            (g_in,) = job.done(job.pass_on(self.jobs["all1"].token))
            self.weights0 = _full_layer_weights(g_in, None, None, None, None, *self.small[0])
            return self.weights0
        g_in, g_br, g_out, g_f1, g_f2 = self.landed1
        return _full_layer_weights(g_in, g_br.reshape(D, D), g_out.reshape(D, D), g_f1, g_f2.reshape(D_FF, D), *self.small[1])

    def fwd_hook(self, l):
        if l != 0:
            return _no_hook

        def hook(point, after, ready=None):
            if point == "attn":
                return self.jobs["rest0"].pass_on(after)
            if point == "ff_post":
                return self.jobs["all1"].pass_on(after)
            if point == "pool":
                g_br, g_out, g_f1, g_f2 = self.jobs["rest0"].done(after)
                self.weights0.update(w_branch=g_br.reshape(D, D), w_out=g_out.reshape(D, D), w_ff1=g_f1, w_ff2=g_f2.reshape(D_FF, D))
            if point == "end":
                self.landed1 = self.jobs["all1"].done(after)
            return None
        return hook

    def bwd_hook(self, l):
        if l != 0:
            return _no_hook

        def hook(point, after, ready=None):
            jobs = self.jobs
            if point == "ff_pre":
                token = jobs["rs1"].chip_sums(after)
                jobs["rs0_ff"] = _ReduceJob("0_ff", ("w_ff1", "w_ff2"), [_chip_blocks(ready[n]) for n in ("w_ff1", "w_ff2")], self.sel, token, 0)
                return jobs["rs0_ff"].token
            if point == "cumf":
                return jobs["rs0_ff"].chip_sums(jobs["rs1"].final_sums(after))
            self.layer1 = jobs["rs1"].done(after)
            jobs["rs0_ff"].into = self.layer1
            return None
        return hook

    def grads_ready(self, l, big):
        if l == 1:
            self.jobs["rs1"] = _ReduceJob("1", _BIG, [_chip_blocks(big[n]) for n in _BIG], self.sel, self.sel, 1)
            return self.jobs["rs1"].token
        names = ("w_in", "w_branch", "w_out")
        self.jobs["rs0_mix"] = _ReduceJob("0_mix", names, [_chip_blocks(big[n]) for n in names], self.sel, self.sel, 0, self.layer1)
        return self.jobs["rs0_mix"].token

    def finish_sums(self, after):
        jobs = self.jobs
        token = jobs["rs0_mix"].chip_sums(after)
        return jobs["rs0_ff"].final_sums(token)

    def finish_ff(self, after):
        self.grads.update(self.jobs["rs0_ff"].done(after))

    def finish_mix(self, after):
        job = self.jobs["rs0_mix"]
        self.grads.update(job.done(job.final_sums(after)))


_SMALL = ("g_mix_pre", "g_mix_post", "g_ff_pre", "g_ff_post", "b_f", "w_pool", "pool_scale", "conv_w")


def _w_in_view(t):
    return t.reshape(DEPTH, D // 128, 128, _SHARD_COLS).transpose(3, 1, 0, 2).reshape(_SHARD_COLS * (D // 128) * DEPTH, 128)


def _w_in_unview(t):
    return t.reshape(_SHARD_COLS, D // 128, DEPTH, 128).transpose(2, 1, 3, 0).reshape(DEPTH, D, _SHARD_COLS)


def _pack(parts, rows=8):
    flat = jnp.concatenate([p.reshape(-1) for p in parts])
    width = -(-flat.shape[0] // (rows * 128)) * 128
    return jnp.pad(flat, (0, rows * width - flat.shape[0])).reshape(rows, width)


def _unpack(packed, like):
    flat = packed.reshape(-1)
    out, at = [], 0
    for ref in like:
        out.append(flat[at:at + ref.size].reshape(ref.shape))
        at += ref.size
    return out


def kernel(x, c, w_ada, b_ada, g_mix_pre, g_mix_post, g_ff_pre, g_ff_post, w_in, b_f, w_pool, pool_scale, conv_w, w_branch, w_out, w_ff1, w_ff2, loss_target, m_w_ada, m_b_ada, m_g_mix_pre, m_g_mix_post, m_g_ff_pre, m_g_ff_post, m_w_in, m_b_f, m_w_pool, m_pool_scale, m_conv_w, m_w_branch, m_w_out, m_w_ff1, m_w_ff2, v_w_ada, v_b_ada, v_g_mix_pre, v_g_mix_post, v_g_ff_pre, v_g_ff_post, v_w_in, v_b_f, v_w_pool, v_pool_scale, v_conv_w, v_w_branch, v_w_out, v_w_ff1, v_w_ff2):
    xi, yi, ci = lax.axis_index("x"), lax.axis_index("y"), lax.axis_index("c")
    chip = 2 * xi + yi
    dev = 2 * chip + ci
    n_ada = w_ada.shape[2]

    first = jnp.zeros((8, D + 384), F32).at[0, :D].set(c[0]).at[0, D:].set(conv_w.reshape(-1))
    got = _allgather8(first, "gather_cond").reshape(N_DEV, 8, D + 384)[:, 0]
    c_all = got[:, :D]
    conv_full = got[0::2, D:].reshape(N_CHIPS, DEPTH, 3, CONV_WIDTH // N_CHIPS).transpose(1, 2, 0, 3).reshape(DEPTH, 3, CONV_WIDTH)

    b_loc = lax.dynamic_slice_in_dim(b_ada, chip * n_ada, n_ada, axis=1).reshape(DEPTH, 1, n_ada)
    mod_cols, silu_c = _ada_fwd(c_all, w_ada, b_loc)
    got = _allgather8(mod_cols.reshape(DEPTH * N_DEV, n_ada), "gather_mod").reshape(N_DEV, DEPTH, N_DEV, n_ada)[0::2]
    mod_all = got.transpose(1, 2, 0, 3).reshape(DEPTH, N_DEV, 6, D)
    mods = lax.dynamic_index_in_dim(mod_all, dev, axis=1, keepdims=False)

    comm = _StepComm([[w[l].astype(BF16) for w in (w_in, w_branch, w_out, w_ff1, w_ff2)] for l in range(DEPTH)],
                     jnp.stack([ci, chip]).astype(jnp.int32), mods)
    comm.small = [(g_mix_pre[l], g_mix_post[l], g_ff_pre[l], g_ff_post[l], b_f[l], w_pool[l], pool_scale[l], conv_full[l]) for l in range(DEPTH)]
    loss_part, grad_x, dmods, bigs, smalls = _local_step(x[0], loss_target[0], mods, comm)

    small_parts = [smalls[l][name] for name in _SMALL for l in range(DEPTH)] + [loss_part.reshape(1)]
    packed = _tie(_pack([dmods] + small_parts), comm.jobs["rs0_mix"].token)
    gathered = _allgather8(packed, "gather_small")
    dmod_all = gathered.reshape(N_DEV, -1)[:, :dmods.size].reshape(N_DEV, DEPTH, 6 * D)
    summed = _unpack(_sum_devices(gathered), [dmods] + small_parts)
    grad_b_ada = summed[0].reshape(DEPTH, 6 * D)
    loss = summed[-1][0]
    small_grads = {name: jnp.stack(summed[1 + 2 * i:3 + 2 * i]) for i, name in enumerate(_SMALL)}
    small_grads["conv_w"] = lax.dynamic_slice_in_dim(small_grads["conv_w"], chip * (CONV_WIDTH // N_CHIPS), CONV_WIDTH // N_CHIPS, axis=2)

    dmod_loc = lax.dynamic_slice_in_dim(dmod_all.transpose(1, 0, 2), chip * n_ada, n_ada, axis=2)
    tail_token = comm.finish_sums(grad_b_ada)
    silu_pad = _tie(jnp.pad(silu_c, ((0, 128 - N_DEV), (0, 0))), tail_token)
    dmod_pad = jnp.pad(dmod_loc.transpose(1, 0, 2).reshape(N_DEV, DEPTH * n_ada), ((0, 128 - N_DEV), (0, 0)))
    grad_w_ada = _mm(silu_pad, dmod_pad, ta=True, out_split=DEPTH, name="mm_ada_dw")

    grads = dict(w_ada=grad_w_ada, b_ada=grad_b_ada, **small_grads)
    weights = dict(w_ada=w_ada, b_ada=b_ada, g_mix_pre=g_mix_pre, g_mix_post=g_mix_post, g_ff_pre=g_ff_pre, g_ff_post=g_ff_post, w_in=w_in,
                   b_f=b_f, w_pool=w_pool, pool_scale=pool_scale, conv_w=conv_w, w_branch=w_branch, w_out=w_out, w_ff1=w_ff1, w_ff2=w_ff2)
    m_in = dict(w_ada=m_w_ada, b_ada=m_b_ada, g_mix_pre=m_g_mix_pre, g_mix_post=m_g_mix_post, g_ff_pre=m_g_ff_pre, g_ff_post=m_g_ff_post,
                w_in=m_w_in, b_f=m_b_f, w_pool=m_w_pool, pool_scale=m_pool_scale, conv_w=m_conv_w, w_branch=m_w_branch, w_out=m_w_out,
                w_ff1=m_w_ff1, w_ff2=m_w_ff2)
    v_in = dict(w_ada=v_w_ada, b_ada=v_b_ada, g_mix_pre=v_g_mix_pre, g_mix_post=v_g_mix_post, g_ff_pre=v_g_ff_pre, g_ff_post=v_g_ff_post,
                w_in=v_w_in, b_f=v_b_f, w_pool=v_w_pool, pool_scale=v_pool_scale, conv_w=v_conv_w, w_branch=v_w_branch, w_out=v_w_out,
                w_ff1=v_w_ff1, w_ff2=v_w_ff2)
    order = ("w_ada", "b_ada", "g_mix_pre", "g_mix_post", "g_ff_pre", "g_ff_post", "w_in", "b_f", "w_pool", "pool_scale", "conv_w",
             "w_branch", "w_out", "w_ff1", "w_ff2")
    delta, new_m, new_v = {}, {}, {}
    tiny = ("b_ada",) + _SMALL
    tiny_g = [_tie(grads[tiny[0]], tail_token)] + [grads[name] for name in tiny[1:]]
    res = _adamw_many([weights[name] for name in tiny], tiny_g, [m_in[name] for name in tiny], [v_in[name] for name in tiny], "adamw_small")
    for out, vals in zip((delta, new_m, new_v), res):
        out.update(zip(tiny, vals))
    delta["w_ada"], new_m["w_ada"], new_v["w_ada"] = _adamw(w_ada, grad_w_ada, m_w_ada, v_w_ada, "adamw_w_ada")
    comm.finish_ff(delta["w_ada"][0, :8, :128] + delta["b_ada"][0, :128])
    for name in ("w_ff1", "w_ff2", "w_in", "w_branch", "w_out"):
        if name == "w_in":
            comm.finish_mix(delta["w_ff2"][0, :8, :128])
        grads[name] = comm.grads[name]
        if name == "w_in":
            g_view = lax.optimization_barrier(_w_in_view(grads[name]))
            res = _adamw(_w_in_view(w_in), g_view, _w_in_view(m_w_in), _w_in_view(v_w_in), "adamw_w_in")
            grads[name], delta[name], new_m[name], new_v[name] = [_w_in_unview(t) for t in (g_view, *res)]
        else:
            delta[name], new_m[name], new_v[name] = _adamw(weights[name], grads[name], m_in[name], v_in[name], "adamw_" + name)

    return (loss, grad_x[None], *[grads[n] for n in order], *[delta[n] for n in order], *[new_m[n] for n in order],
            *[new_v[n] for n in order])
```

```python
from typing import NamedTuple

import jax
import jax.numpy as jnp
from jax import lax
from jax.experimental import pallas as pl
from jax.experimental.pallas import tpu as pltpu

F32 = jnp.float32
BF16 = jnp.bfloat16
MESH = pl.DeviceIdType.MESH

D = 1024
DEPTH = 2
HEADS = 8
HEAD_DIM = 64
A_WIDTH = 512
POOL_WIDTH = 256
CONV_WIDTH = 256
D_FF = 4096
IN_COLS = 5640
Z_GL, Z_QKV, Z_PC, Z_FL, Z_COLS = 0, 3072, 4608, 5632, 5760
RMS_EPS = 1e-6
NEG_INF = -1e30
ROW_TILE = 512
EW_ROWS = 256
N_CHIPS = 4
N_DEV = 8
V7X_VMEM_LIMIT = 48 * 1024 * 1024

ADAM_LR = 0.001
ADAM_B1 = 0.9
ADAM_B2 = 0.999
ADAM_EPS = 1e-08
ADAM_WD = 0.01
ADAM_STEP = 10

_HBM = pl.BlockSpec(memory_space=pltpu.HBM)


def _params(*sem):
    return pltpu.CompilerParams(dimension_semantics=sem, vmem_limit_bytes=V7X_VMEM_LIMIT)


def _pick(dim, cands):
    for cand in cands:
        if dim % cand == 0:
            return cand
    return dim


def _mm(a, b, *, ta=False, tb=False, b_rows=None, b_split=1, out_split=1, out_dtype=F32, epilogue=None, extras=(), name):
    (k, m) = a.shape if ta else a.shape[::-1]
    b_row0, b_rows = (0, b.shape[-2]) if b_rows is None else b_rows
    b_cols = b.shape[-1] * b_split
    (n, k2) = (b_rows, b_cols) if tb else (b_cols, b_rows)
    assert k == k2, (a.shape, b.shape, ta, tb)
    n_unit = n // (out_split * (1 if tb else b_split))
    k_unit = k // (b_split if tb else 1)
    tm = _pick(m, (1024, 512, 256, 128))
    tn = _pick(n_unit, (1024, 1152, 768, 640, 512, 256, 128))
    tk = _pick(k_unit, (1024, 1152, 512, 640, 256, 128))
    nk = k // tk
    dims = (((0 if ta else 1,), (1 if tb else 0,)), ((), ()))

    def dot(a_ref, b_ref):
        b_val = b_ref[0] if b_split > 1 else b_ref[...]
        return lax.dot_general(a_ref[...].astype(BF16), b_val.astype(BF16), dims, preferred_element_type=F32)

    n_extra = len(extras)
    assert epilogue is None or out_split == 1

    def put(refs, val):
        if epilogue is not None:
            for o_ref, res in zip(refs[n_extra:], epilogue(val, *[r[...] for r in refs[:n_extra]])):
                o_ref[...] = res.astype(o_ref.dtype)
        elif out_split > 1:
            refs[0][0] = val.astype(refs[0].dtype)
        else:
            refs[0][...] = val.astype(refs[0].dtype)

    def body_single(a_ref, b_ref, *refs):
        put(refs, dot(a_ref, b_ref))

    def body_acc(a_ref, b_ref, *refs):
        kk = pl.program_id(2)
        acc_ref = refs[-1]

        @pl.when(kk == 0)
        def _():
            acc_ref[...] = jnp.zeros_like(acc_ref)

        acc_ref[...] += dot(a_ref, b_ref)

        @pl.when(kk == nk - 1)
        def _():
            put(refs[:-1], acc_ref[...])

    a_spec = pl.BlockSpec((tk, tm), lambda i, j, kk: (kk, i)) if ta else pl.BlockSpec((tm, tk), lambda i, j, kk: (i, kk))
    if b_split == 1:
        off = b_row0 // (tn if tb else tk)
        assert off * (tn if tb else tk) == b_row0
        b_spec = pl.BlockSpec((tn, tk), lambda i, j, kk: (j + off, kk)) if tb else pl.BlockSpec((tk, tn), lambda i, j, kk: (kk + off, j))
    elif tb:
        per = k_unit // tk
        b_spec = pl.BlockSpec((1, tn, tk), lambda i, j, kk: (kk // per, j, kk % per))
    else:
        per = n // b_split // tn
        b_spec = pl.BlockSpec((1, tk, tn), lambda i, j, kk: (j // per, kk, j % per))
    if out_split == 1:
        o_spec = pl.BlockSpec((tm, tn), lambda i, j, kk: (i, j))
        o_shape = None if epilogue is not None else jax.ShapeDtypeStruct((m, n), out_dtype)
    else:
        per_o = n // out_split // tn
        o_spec = pl.BlockSpec((1, tm, tn), lambda i, j, kk: (j // per_o, i, j % per_o))
        o_shape = jax.ShapeDtypeStruct((out_split, m, n // out_split), out_dtype)
    if epilogue is not None:
        o_shape = [jax.ShapeDtypeStruct((m, n), dt) for dt in out_dtype]
        o_spec = [o_spec] * len(out_dtype)
    return pl.pallas_call(
        body_single if nk == 1 else body_acc, name=name, grid=(m // tm, n // tn, nk),
        in_specs=[a_spec, b_spec] + [pl.BlockSpec((tm, tn), lambda i, j, kk: (i, j))] * n_extra, out_specs=o_spec, out_shape=o_shape,
        scratch_shapes=[] if nk == 1 else [pltpu.VMEM((tm, tn), F32)],
        compiler_params=_params("parallel", "parallel", "arbitrary"),
    )(a, b, *extras)


def _ew(fn, ins, out_dtypes, name, tc=None):
    shape = ins[0].shape
    lead, (rows, cols) = shape[:-2], shape[-2:]
    tc = cols if tc is None else tc
    tr = _pick(rows, (EW_ROWS, 128, 8)) if tc > 128 else _pick(rows, (4096, 2256, 2048, 1024, EW_ROWS, 8))
    n_in = len(ins)

    def body(*refs):
        res = fn(*[r[...] for r in refs[:n_in]])
        for o_ref, val in zip(refs[n_in:], res):
            o_ref[...] = val.astype(o_ref.dtype)

    if lead:
        spec = pl.BlockSpec((None, tr, tc), lambda l, i, j: (l, i, j))
    else:
        spec = pl.BlockSpec((tr, tc), lambda i, j: (i, j))
    return pl.pallas_call(
        body, name=name, grid=lead + (rows // tr, cols // tc),
        in_specs=[spec] * n_in, out_specs=[spec] * len(out_dtypes),
        out_shape=[jax.ShapeDtypeStruct(shape, dt) for dt in out_dtypes],
        compiler_params=_params(*(["parallel"] * (len(lead) + 2))),
    )(*ins)


def _relu2_fwd(a):
    r = jnp.maximum(a, 0.0)
    return a, r * r


def _relu2_bwd(dr, a):
    return (dr * (2.0 * jnp.maximum(a, 0.0)),)


def _adamw_math(w, g, m, v):
    m = ADAM_B1 * m + (1.0 - ADAM_B1) * g
    v = ADAM_B2 * v + (1.0 - ADAM_B2) * (g * g)
    m_hat = m / (1.0 - ADAM_B1 ** ADAM_STEP)
    v_hat = v / (1.0 - ADAM_B2 ** ADAM_STEP)
    delta = -ADAM_LR * (m_hat / (jnp.sqrt(v_hat) + ADAM_EPS) + ADAM_WD * w)
    return delta, m, v


def _adamw(w, g, m, v, name):
    return _ew(_adamw_math, [w, g, m, v], [F32, F32, F32], name)


def _adamw_many(ws, gs, ms, vs, name):
    n = len(ws)

    def body(*refs):
        for i in range(n):
            res = _adamw_math(*[refs[k * n + i][...] for k in range(4)])
            for k in range(3):
                refs[(4 + k) * n + i][...] = res[k]

    outs = pl.pallas_call(
        body, name=name, out_shape=[jax.ShapeDtypeStruct(w.shape, F32) for w in ws] * 3,
        compiler_params=pltpu.CompilerParams(vmem_limit_bytes=V7X_VMEM_LIMIT),
    )(*ws, *gs, *ms, *vs)
    return outs[:n], outs[n:2 * n], outs[2 * n:]


def _row_spec(cols, block=0):
    return pl.BlockSpec((ROW_TILE, cols), lambda i, block=block: (i, block))


def _vec_spec(cols):
    return pl.BlockSpec((1, cols), lambda i: (0, 0))


def _vec_args(*vecs):
    arrays = [v[0] if isinstance(v, tuple) else v for v in vecs]
    specs = [pl.BlockSpec((None, 1, D), lambda i, row=v[1]: (row, 0, 0)) if isinstance(v, tuple) else _vec_spec(D) for v in vecs]
    return arrays, specs


def _sum_spec(cols):
    return pl.BlockSpec((8, cols), lambda i: (0, 0))


def _rstd(x):
    return lax.rsqrt(jnp.mean(x * x, axis=-1, keepdims=True) + RMS_EPS)


def _modnorm_fwd(x, g, shift, scale, name):
    s = x.shape[0]

    def body(x_ref, g_ref, sh_ref, sc_ref, h_ref):
        xv = x_ref[...]
        n = xv * _rstd(xv)
        h_ref[...] = ((n * g_ref[...]) * (1.0 + sc_ref[...]) + sh_ref[...]).astype(BF16)

    vecs, vec_specs = _vec_args(g, shift, scale)
    return pl.pallas_call(
        body, name=name, grid=(s // ROW_TILE,),
        in_specs=[_row_spec(D)] + vec_specs, out_specs=_row_spec(D),
        out_shape=jax.ShapeDtypeStruct((s, D), BF16), compiler_params=_params("parallel"),
    )(x, *vecs)


def _post_fwd(x, y, g, gate, name):
    s = x.shape[0]

    def body(x_ref, y_ref, g_ref, gate_ref, o_ref):
        yv = y_ref[...]
        o_ref[...] = x_ref[...] + gate_ref[...] * ((yv * _rstd(yv)) * g_ref[...])

    vecs, vec_specs = _vec_args(g, gate)
    return pl.pallas_call(
        body, name=name, grid=(s // ROW_TILE,),
        in_specs=[_row_spec(D), _row_spec(D)] + vec_specs, out_specs=_row_spec(D),
        out_shape=jax.ShapeDtypeStruct((s, D), F32), compiler_params=_params("parallel"),
    )(x, y, *vecs)


def _post_bwd(dxo, y, g, gate, name):
    s = dxo.shape[0]

    def body(d_ref, y_ref, g_ref, gate_ref, dy_ref, sum_ref):
        @pl.when(pl.program_id(0) == 0)
        def _():
            sum_ref[...] = jnp.zeros_like(sum_ref)

        dv, yv = d_ref[...], y_ref[...]
        r = _rstd(yv)
        n = yv * r
        sum_ref[0:1, :] += jnp.sum(dv * (n * g_ref[...]), axis=0, keepdims=True)
        sum_ref[1:2, :] += jnp.sum((dv * gate_ref[...]) * n, axis=0, keepdims=True)
        dn = (dv * gate_ref[...]) * g_ref[...]
        dy_ref[...] = (r * (dn - n * jnp.mean(dn * n, axis=-1, keepdims=True))).astype(BF16)

    vecs, vec_specs = _vec_args(g, gate)
    return pl.pallas_call(
        body, name=name, grid=(s // ROW_TILE,),
        in_specs=[_row_spec(D), _row_spec(D)] + vec_specs,
        out_specs=[_row_spec(D), _sum_spec(D)],
        out_shape=[jax.ShapeDtypeStruct((s, D), BF16), jax.ShapeDtypeStruct((8, D), F32)],
        compiler_params=_params("arbitrary"),
    )(dxo, y, *vecs)


def _modnorm_bwd(dh, x, dxo, g, scale, name):
    s = dh.shape[0]

    def body(dh_ref, x_ref, d_ref, g_ref, sc_ref, dx_ref, sum_ref):
        @pl.when(pl.program_id(0) == 0)
        def _():
            sum_ref[...] = jnp.zeros_like(sum_ref)

        dhv, xv = dh_ref[...], x_ref[...]
        r = _rstd(xv)
        n = xv * r
        one_sc = 1.0 + sc_ref[...]
        sum_ref[0:1, :] += jnp.sum(dhv, axis=0, keepdims=True)
        sum_ref[1:2, :] += jnp.sum(dhv * (n * g_ref[...]), axis=0, keepdims=True)
        sum_ref[2:3, :] += jnp.sum((dhv * one_sc) * n, axis=0, keepdims=True)
        dn = (dhv * one_sc) * g_ref[...]
        dx_ref[...] = d_ref[...] + r * (dn - n * jnp.mean(dn * n, axis=-1, keepdims=True))

    vecs, vec_specs = _vec_args(g, scale)
    return pl.pallas_call(
        body, name=name, grid=(s // ROW_TILE,),
        in_specs=[_row_spec(D), _row_spec(D), _row_spec(D)] + vec_specs,
        out_specs=[_row_spec(D), _sum_spec(D)],
        out_shape=[jax.ShapeDtypeStruct((s, D), F32), jax.ShapeDtypeStruct((8, D), F32)],
        compiler_params=_params("arbitrary"),
    )(dh, x, dxo, *vecs)


def _post_pre_fwd(x, y, g_post, gate, g_pre, shift, scale, name):
    s = x.shape[0]

    def body(x_ref, y_ref, gp_ref, gate_ref, g_ref, sh_ref, sc_ref, o_ref, h_ref):
        yv = y_ref[...]
        xo = x_ref[...] + gate_ref[...] * ((yv * _rstd(yv)) * gp_ref[...])
        o_ref[...] = xo
        h_ref[...] = (((xo * _rstd(xo)) * g_ref[...]) * (1.0 + sc_ref[...]) + sh_ref[...]).astype(BF16)

    vecs, vec_specs = _vec_args(g_post, gate, g_pre, shift, scale)
    return pl.pallas_call(
        body, name=name, grid=(s // ROW_TILE,),
        in_specs=[_row_spec(D), _row_spec(D)] + vec_specs, out_specs=[_row_spec(D), _row_spec(D)],
        out_shape=[jax.ShapeDtypeStruct((s, D), F32), jax.ShapeDtypeStruct((s, D), BF16)], compiler_params=_params("parallel"),
    )(x, y, *vecs)


def _pre_post_bwd(dh, x, dxo, g_pre, scale, y, g_post, gate, name):
    s = dh.shape[0]

    def body(dh_ref, x_ref, d_ref, y_ref, g_ref, sc_ref, gp_ref, gate_ref, dx_ref, dy_ref, sum_ref):
        @pl.when(pl.program_id(0) == 0)
        def _():
            sum_ref[...] = jnp.zeros_like(sum_ref)

        dhv, xv = dh_ref[...], x_ref[...]
        r = _rstd(xv)
        n = xv * r
        one_sc = 1.0 + sc_ref[...]
        sum_ref[0:1, :] += jnp.sum(dhv, axis=0, keepdims=True)
        sum_ref[1:2, :] += jnp.sum(dhv * (n * g_ref[...]), axis=0, keepdims=True)
        sum_ref[2:3, :] += jnp.sum((dhv * one_sc) * n, axis=0, keepdims=True)
        dn = (dhv * one_sc) * g_ref[...]
        dv = d_ref[...] + r * (dn - n * jnp.mean(dn * n, axis=-1, keepdims=True))
        dx_ref[...] = dv

        yv = y_ref[...]
        ry = _rstd(yv)
        ny = yv * ry
        sum_ref[3:4, :] += jnp.sum(dv * (ny * gp_ref[...]), axis=0, keepdims=True)
        sum_ref[4:5, :] += jnp.sum((dv * gate_ref[...]) * ny, axis=0, keepdims=True)
        dny = (dv * gate_ref[...]) * gp_ref[...]
        dy_ref[...] = (ry * (dny - ny * jnp.mean(dny * ny, axis=-1, keepdims=True))).astype(BF16)

    vecs, vec_specs = _vec_args(g_pre, scale, g_post, gate)
    return pl.pallas_call(
        body, name=name, grid=(s // ROW_TILE,),
        in_specs=[_row_spec(D)] * 4 + vec_specs,
        out_specs=[_row_spec(D), _row_spec(D), _sum_spec(D)],
        out_shape=[jax.ShapeDtypeStruct((s, D), F32), jax.ShapeDtypeStruct((s, D), BF16), jax.ShapeDtypeStruct((8, D), F32)],
        compiler_params=_params("arbitrary"),
    )(dh, x, dxo, y, *vecs)


def _loss_head(y, target):
    s = y.shape[0]

    def body(y_ref, t_ref, dy_ref, sum_ref):
        @pl.when(pl.program_id(0) == 0)
        def _():
            sum_ref[...] = jnp.zeros_like(sum_ref)

        err = y_ref[...] - t_ref[...]
        dy_ref[...] = err * (1.0 / D)
        sum_ref[...] += jnp.sum(err * err)

    return pl.pallas_call(
        body, name="loss_head", grid=(s // ROW_TILE,),
        in_specs=[_row_spec(D), _row_spec(D)],
        out_specs=[_row_spec(D), pl.BlockSpec((8, 128), lambda i: (0, 0))],
        out_shape=[jax.ShapeDtypeStruct((s, D), F32), jax.ShapeDtypeStruct((8, 128), F32)],
        compiler_params=_params("arbitrary"),
    )(y, target)


def _merge_fwd(z, pa, pb, pc):
    s = z.shape[0]

    def body(g0_ref, g1_ref, g2_ref, pa_ref, pb_ref, pc_ref, o_ref):
        o_ref[...] = (jax.nn.sigmoid(g0_ref[...]) * pa_ref[...] + jax.nn.sigmoid(g1_ref[...]) * pb_ref[...]
                      + jax.nn.sigmoid(g2_ref[...]) * pc_ref[...]).astype(BF16)

    return pl.pallas_call(
        body, name="merge_fwd", grid=(s // ROW_TILE,),
        in_specs=[_row_spec(D, 0), _row_spec(D, 1), _row_spec(D, 2), _row_spec(D), _row_spec(D), _row_spec(D)],
        out_specs=_row_spec(D), out_shape=jax.ShapeDtypeStruct((s, D), BF16),
        compiler_params=_params("parallel"),
    )(z, z, z, pa, pb, pc)


def _merge_bwd(dm, z, pa, pb, pc):
    s = z.shape[0]

    def body(dm_ref, g0_ref, g1_ref, g2_ref, pa_ref, pb_ref, pc_ref, dgl_ref, da_ref, db_ref, dc_ref):
        dmv = dm_ref[...]
        for i, (g_ref, p_ref, d_ref) in enumerate(((g0_ref, pa_ref, da_ref), (g1_ref, pb_ref, db_ref), (g2_ref, pc_ref, dc_ref))):
            gate = jax.nn.sigmoid(g_ref[...])
            dgl_ref[:, i * D:(i + 1) * D] = ((dmv * p_ref[...]) * (gate * (1.0 - gate))).astype(BF16)
            d_ref[...] = (dmv * gate).astype(BF16)

    return pl.pallas_call(
        body, name="merge_bwd", grid=(s // ROW_TILE,),
        in_specs=[_row_spec(D), _row_spec(D, 0), _row_spec(D, 1), _row_spec(D, 2), _row_spec(D), _row_spec(D), _row_spec(D)],
        out_specs=[_row_spec(3 * D), _row_spec(D), _row_spec(D), _row_spec(D)],
        out_shape=[jax.ShapeDtypeStruct((s, Z_COLS), BF16)] + [jax.ShapeDtypeStruct((s, D), BF16)] * 3,
        compiler_params=_params("parallel"),
    )(dm, z, z, z, pa, pb, pc)


def _shift_down(v, n):
    row = lax.broadcasted_iota(jnp.int32, v.shape, 0)
    return jnp.where(row >= n, pltpu.roll(v, n, axis=0), 0.0)


def _shift_up(v, n):
    s = v.shape[0]
    row = lax.broadcasted_iota(jnp.int32, v.shape, 0)
    return jnp.where(row < s - n, pltpu.roll(v, s - n, axis=0), 0.0)


def _log_sigmoid(v):
    return jnp.minimum(v, 0.0) - jnp.log1p(jnp.exp(-jnp.abs(v)))


def _cumf_fwd(fl, bias):
    s = fl.shape[0]

    def body(fl_ref, b_ref, o_ref):
        acc = _log_sigmoid(fl_ref[...] + b_ref[...])
        step = 1
        while step < s:
            acc = acc + _shift_down(acc, step)
            step *= 2
        o_ref[...] = acc

    return pl.pallas_call(body, name="cumf_fwd", out_shape=jax.ShapeDtypeStruct((s, 128), F32),
                          compiler_params=pltpu.CompilerParams(vmem_limit_bytes=V7X_VMEM_LIMIT))(fl, bias)


def _cumf_bwd(dcum, fl, bias):
    s = fl.shape[0]

    def body(d_ref, fl_ref, b_ref, dfl_ref, db_ref):
        acc = d_ref[...]
        step = 1
        while step < s:
            acc = acc + _shift_up(acc, step)
            step *= 2
        dfl = acc * jax.nn.sigmoid(-(fl_ref[...] + b_ref[...]))
        dfl_ref[...] = dfl.astype(BF16)
        db_ref[...] = jnp.broadcast_to(jnp.sum(dfl, axis=0, keepdims=True), (8, 128))

    return pl.pallas_call(
        body, name="cumf_bwd",
        out_shape=[jax.ShapeDtypeStruct((s, 128), BF16), jax.ShapeDtypeStruct((8, 128), F32)],
        compiler_params=pltpu.CompilerParams(vmem_limit_bytes=V7X_VMEM_LIMIT))(dcum, fl, bias)


def _pool_windows(v, shift):
    s2 = v + shift(v, 1)
    s4 = s2 + shift(s2, 2)
    s8 = s4 + shift(s4, 4)
    s16 = s8 + shift(s8, 8)
    group = lax.broadcasted_iota(jnp.int32, v.shape, 1) // 64
    return jnp.where(group == 0, s2, jnp.where(group == 1, s4, jnp.where(group == 2, s8, s16)))


def _pool_count(shape):
    group = lax.broadcasted_iota(jnp.int32, shape, 1) // 64
    window = jnp.where(group == 0, 2.0, jnp.where(group == 1, 4.0, jnp.where(group == 2, 8.0, 16.0)))
    t1 = (lax.broadcasted_iota(jnp.int32, shape, 0) + 1).astype(F32)
    return jnp.minimum(t1, window)


def _pc_specs(s):
    zcol = lambda blk: pl.BlockSpec((s, 256), lambda i, blk=blk: (0, blk))
    first = Z_PC // 256
    return [zcol(first), zcol(first + 1), zcol(first + 2), zcol(first + 3),
            pl.BlockSpec((256, 256), lambda i: (0, 0)), pl.BlockSpec((1, 256), lambda i: (0, 0)),
            pl.BlockSpec((3, 256), lambda i: (0, 0))]


def _poolconv_fwd(z, wbd, pscale, convw):
    s = z.shape[0]

    def body(pu_ref, ch_ref, cb_ref, cc_ref, w_ref, ps_ref, cw_ref, yb_ref, yc_ref):
        u = pu_ref[...]
        p = _pool_windows(u, _shift_down) / _pool_count(u.shape) - u
        yb = jnp.dot(p.astype(BF16), w_ref[...].astype(BF16), preferred_element_type=F32) * ps_ref[...]
        yb_ref[...] = yb.astype(BF16)
        uc = cc_ref[...] * ch_ref[...]
        cw = cw_ref[...]
        conv = cw[0:1, :] * _shift_down(uc, 2) + cw[1:2, :] * _shift_down(uc, 1) + cw[2:3, :] * uc
        yc_ref[...] = (cb_ref[...] * conv).astype(BF16)

    out = pl.BlockSpec((s, 256), lambda i: (0, 0))
    return pl.pallas_call(
        body, name="poolconv_fwd", grid=(1,), in_specs=_pc_specs(s), out_specs=[out, out],
        out_shape=[jax.ShapeDtypeStruct((s, 256), BF16)] * 2, compiler_params=_params("arbitrary"),
    )(z, z, z, z, wbd, pscale, convw)


def _poolconv_bwd(dyb, dyc, z, wbd, pscale, convw):
    s = z.shape[0]

    def body(dyb_ref, dyc_ref, pu_ref, ch_ref, cb_ref, cc_ref, w_ref, ps_ref, cw_ref, dz_ref, dw_ref, dps_ref, dcw_ref):
        u = pu_ref[...]
        count = _pool_count(u.shape)
        p = (_pool_windows(u, _shift_down) / count - u).astype(BF16)
        wb = w_ref[...].astype(BF16)
        dyb_v = dyb_ref[...]
        pw = jnp.dot(p, wb, preferred_element_type=F32)
        dps_ref[...] = jnp.broadcast_to(jnp.sum(dyb_v * pw, axis=0, keepdims=True), (8, 256))
        dys = (dyb_v * ps_ref[...]).astype(BF16)
        dp = lax.dot_general(dys, wb, (((1,), (1,)), ((), ())), preferred_element_type=F32)
        dw_ref[...] = lax.dot_general(p, dys, (((0,), (0,)), ((), ())), preferred_element_type=F32)
        dz_ref[:, 0:256] = (_pool_windows(dp / count, _shift_up) - dp).astype(BF16)

        ch, cb, cc = ch_ref[...], cb_ref[...], cc_ref[...]
        uc = cc * ch
        cw = cw_ref[...]
        u2, u1 = _shift_down(uc, 2), _shift_down(uc, 1)
        conv = cw[0:1, :] * u2 + cw[1:2, :] * u1 + cw[2:3, :] * uc
        dyc_v = dyc_ref[...]
        dconv = dyc_v * cb
        du = cw[0:1, :] * _shift_up(dconv, 2) + cw[1:2, :] * _shift_up(dconv, 1) + cw[2:3, :] * dconv
        dz_ref[:, 256:512] = (du * cc).astype(BF16)
        dz_ref[:, 512:768] = (dyc_v * conv).astype(BF16)
        dz_ref[:, 768:1024] = (du * ch).astype(BF16)
        dcw_ref[...] = jnp.zeros_like(dcw_ref)
        dcw_ref[0:1, :] = jnp.sum(dconv * u2, axis=0, keepdims=True)
        dcw_ref[1:2, :] = jnp.sum(dconv * u1, axis=0, keepdims=True)
        dcw_ref[2:3, :] = jnp.sum(dconv * uc, axis=0, keepdims=True)

    blk = lambda r, c: pl.BlockSpec((r, c), lambda i: (0, 0))
    return pl.pallas_call(
        body, name="poolconv_bwd", grid=(1,),
        in_specs=[blk(s, 256), blk(s, 256)] + _pc_specs(s),
        out_specs=[blk(s, 1024), blk(256, 256), blk(8, 256), blk(8, 256)],
        out_shape=[jax.ShapeDtypeStruct((s, 1024), BF16), jax.ShapeDtypeStruct((256, 256), F32),
                   jax.ShapeDtypeStruct((8, 256), F32), jax.ShapeDtypeStruct((8, 256), F32)],
        compiler_params=_params("arbitrary"),
    )(dyb, dyc, z, z, z, z, wbd, pscale, convw)


_NT = (((1,), (1,)), ((), ()))
_TN = (((0,), (0,)), ((), ()))


ATT_Q, ATT_K = 256, 256
ATT_HEADS_BWD = 8
ATT_HEADS = 8


def _att_logits(q, k, fr, q0, k0, masked):
    logits = lax.dot_general(q, k, _NT, preferred_element_type=F32) - fr
    if not masked:
        return logits
    row = q0 + lax.broadcasted_iota(jnp.int32, logits.shape, 0)
    col = k0 + lax.broadcasted_iota(jnp.int32, logits.shape, 1)
    return jnp.where(row >= col, logits, NEG_INF)


def _causal_sweep(step, qi, init):
    n_full = (qi * ATT_Q) // ATT_K
    carry = lax.fori_loop(0, n_full, lambda j, carry: step(j, carry, False), init)
    return step(n_full, carry, True)


HEAD_PAIRS = HEADS // 2


def _lane_pick(v, lane, idx):
    return jnp.sum(jnp.where(lane == idx, v, 0.0), axis=-1, keepdims=True)


def _lane_put(lane, idx, col):
    return jnp.where(lane == idx, col, 0.0)


def _split_heads(v, low):
    zero = jnp.zeros_like(v)
    return jnp.where(low, v, zero), jnp.where(low, zero, v)


def _attn_fwd(qkv, fr):
    s = qkv.shape[0]
    nk = s // ATT_K
    width = ATT_HEADS * HEAD_DIM
    groups = HEADS // ATT_HEADS

    def body(q_ref, k_ref, v_ref, fr_ref, o_ref, lse_ref):
        qi, grp = pl.program_id(0), pl.program_id(1)
        lane = lax.broadcasted_iota(jnp.int32, (ATT_Q, 128), 1)
        low = lane < HEAD_DIM
        qs = []
        for pr in range(ATT_HEADS // 2):
            qs += _split_heads(q_ref[:, 128 * pr:128 * (pr + 1)] * (HEAD_DIM ** -0.5), low)

        def step(j, carry, masked):
            k0 = pl.multiple_of(j * ATT_K, ATT_K)
            out = []
            for h in range(ATT_HEADS):
                cols = slice(128 * (h // 2), 128 * (h // 2 + 1))
                m, l, acc = carry[h]
                logits = _att_logits(qs[h], k_ref[pl.ds(k0, ATT_K), cols], fr_ref[h, pl.ds(j, 1), :], qi * ATT_Q, k0, masked)
                m_new = jnp.maximum(m, jnp.max(logits, axis=-1, keepdims=True))
                p = jnp.exp(logits - m_new)
                alpha = jnp.exp(m - m_new)
                l = alpha * l + jnp.sum(p, axis=-1, keepdims=True)
                acc = alpha * acc + jnp.dot(p.astype(BF16), v_ref[pl.ds(k0, ATT_K), cols], preferred_element_type=F32)
                out.append((m_new, l, acc))
            return tuple(out)

        one = (jnp.full((ATT_Q, 1), NEG_INF, F32), jnp.zeros((ATT_Q, 1), F32), jnp.zeros((ATT_Q, 128), F32))
        done = _causal_sweep(step, qi, (one,) * ATT_HEADS)

        @pl.when(grp == 0)
        def _():
            lse_ref[...] = jnp.zeros_like(lse_ref)

        lse = jnp.zeros((ATT_Q, 128), F32)
        for pr in range(ATT_HEADS // 2):
            (m0, l0, acc0), (m1, l1, acc1) = done[2 * pr], done[2 * pr + 1]
            o_ref[:, 128 * pr:128 * (pr + 1)] = jnp.where(low, acc0 / l0, acc1 / l1)
            head = ATT_HEADS * grp + 2 * pr
            lse = lse + _lane_put(lane, head, m0 + jnp.log(l0)) + _lane_put(lane, head + 1, m1 + jnp.log(l1))
        lse_ref[...] += lse

    return pl.pallas_call(
        body, name="attn_fwd", grid=(s // ATT_Q, groups),
        in_specs=[pl.BlockSpec((ATT_Q, width), lambda i, g: (i, g)),
                  pl.BlockSpec((s, width), lambda i, g: (0, groups + g)),
                  pl.BlockSpec((s, width), lambda i, g: (0, 2 * groups + g)),
                  pl.BlockSpec((ATT_HEADS, nk, ATT_K), lambda i, g: (g, 0, 0))],
        out_specs=[pl.BlockSpec((ATT_Q, width), lambda i, g: (i, g)), pl.BlockSpec((ATT_Q, 128), lambda i, g: (i, 0))],
        out_shape=[jax.ShapeDtypeStruct((s, A_WIDTH), F32), jax.ShapeDtypeStruct((s, 128), F32)],
        compiler_params=_params("parallel", "arbitrary"),
    )(qkv, qkv, qkv, fr)


def _attn_bwd(qkv, do, o, lse, fr):
    s = qkv.shape[0]
    nk = s // ATT_K
    scale = HEAD_DIM ** -0.5
    heads = ATT_HEADS_BWD
    width = heads * HEAD_DIM
    groups = HEADS // heads

    def body(q_ref, k_ref, v_ref, do_ref, o_ref, lse_ref, fr_ref, dq_ref, dk_ref, dv_ref, dfc_ref, dfr_ref, dk_acc, dv_acc):
        grp = pl.program_id(0)
        lane = lax.broadcasted_iota(jnp.int32, (ATT_Q, 128), 1)
        low = lane < HEAD_DIM
        low_t = lax.broadcasted_iota(jnp.int32, (128, ATT_Q), 0) < HEAD_DIM
        dk_acc[...] = jnp.zeros_like(dk_acc)
        dv_acc[...] = jnp.zeros_like(dv_acc)
        dfr_ref[...] = jnp.zeros_like(dfr_ref)

        @pl.when(grp == 0)
        def _():
            dfc_ref[...] = jnp.zeros_like(dfc_ref)

        def outer(i, carry):
            q0 = pl.multiple_of(i * ATT_Q, ATT_Q)
            rows = pl.ds(q0, ATT_Q)
            lsev = lse_ref[rows, :]
            qts, dots, qs, dos, deltas, lses = [], [], [], [], [], []
            for pr in range(heads // 2):
                pcols = slice(128 * pr, 128 * (pr + 1))
                q2, do2 = q_ref[rows, pcols] * scale, do_ref[rows, pcols]
                prod = do2 * o_ref[rows, pcols]
                deltas += [jnp.sum(jnp.where(low, prod, 0.0), axis=-1, keepdims=True),
                           jnp.sum(jnp.where(low, 0.0, prod), axis=-1, keepdims=True)]
                dob2 = do2.astype(BF16)
                qts += _split_heads(q2.astype(F32).T.astype(BF16), low_t)
                dots += _split_heads(do2.T.astype(BF16), low_t)
                qs += _split_heads(q2, low)
                dos += _split_heads(dob2, low)
                lses += [_lane_pick(lsev, lane, heads * grp + 2 * pr), _lane_pick(lsev, lane, heads * grp + 2 * pr + 1)]

            def inner(j, carry, masked):
                k0 = pl.multiple_of(j * ATT_K, ATT_K)
                krows = pl.ds(k0, ATT_K)
                out, dkt, dvt = [], [], []
                for h in range(heads):
                    pcols = slice(128 * (h // 2), 128 * (h // 2 + 1))
                    dq, dfc = carry[h]
                    k2 = k_ref[krows, pcols]
                    p = jnp.exp(_att_logits(qs[h], k2, fr_ref[h, pl.ds(j, 1), :], q0, k0, masked) - lses[h])
                    dp = lax.dot_general(dos[h], v_ref[krows, pcols], _NT, preferred_element_type=F32)
                    ds = p * (dp - deltas[h])
                    dsb = ds.astype(BF16)
                    dkt.append(jnp.dot(qts[h], dsb, preferred_element_type=F32))
                    dvt.append(jnp.dot(dots[h], p.astype(BF16), preferred_element_type=F32))
                    dfr_ref[h, pl.ds(j, 1), :] -= jnp.sum(ds, axis=0, keepdims=True)
                    out.append((dq + jnp.dot(dsb, k2, preferred_element_type=F32), dfc + (ds[:, :128] + ds[:, 128:])))
                for pr in range(heads // 2):
                    prows = slice(128 * pr, 128 * (pr + 1))
                    dk_acc[j, prows, :] += dkt[2 * pr] + dkt[2 * pr + 1]
                    dv_acc[j, prows, :] += dvt[2 * pr] + dvt[2 * pr + 1]
                return tuple(out)

            one = (jnp.zeros((ATT_Q, 128), F32), jnp.zeros((ATT_Q, 128), F32))
            done = _causal_sweep(inner, i, (one,) * heads)
            dfc = jnp.zeros((ATT_Q, 128), F32)
            for pr in range(heads // 2):
                (dq0, dfc0), (dq1, dfc1) = done[2 * pr], done[2 * pr + 1]
                dq_ref[rows, 128 * pr:128 * (pr + 1)] = (jnp.where(low, dq0, dq1) * scale).astype(BF16)
                head = heads * grp + 2 * pr
                dfc = (dfc + _lane_put(lane, head, jnp.sum(dfc0, axis=-1, keepdims=True))
                       + _lane_put(lane, head + 1, jnp.sum(dfc1, axis=-1, keepdims=True)))
            dfc_ref[rows, :] += dfc
            return carry

        lax.fori_loop(0, s // ATT_Q, outer, 0)
        for j in range(nk):
            for pr in range(heads // 2):
                prows, pcols = slice(128 * pr, 128 * (pr + 1)), slice(128 * pr, 128 * (pr + 1))
                dk_ref[ATT_K * j:ATT_K * (j + 1), pcols] = dk_acc[j, prows, :].T.astype(BF16)
                dv_ref[ATT_K * j:ATT_K * (j + 1), pcols] = dv_acc[j, prows, :].T.astype(BF16)

    part = lambda first: pl.BlockSpec((s, width), lambda g, first=first: (0, first + g))
    whole = pl.BlockSpec((s, 128), lambda g: (0, 0))
    rowv = pl.BlockSpec((heads, nk, ATT_K), lambda g: (g, 0, 0))
    return pl.pallas_call(
        body, name="attn_bwd", grid=(groups,),
        in_specs=[part(0), part(groups), part(2 * groups), part(0), part(0), whole, rowv],
        out_specs=[part(0), part(0), part(0), whole, rowv],
        out_shape=[jax.ShapeDtypeStruct((s, A_WIDTH), BF16)] * 3 + [jax.ShapeDtypeStruct((s, 128), F32), jax.ShapeDtypeStruct((HEADS, nk, ATT_K), F32)],
        scratch_shapes=[pltpu.VMEM((nk, width, ATT_K), F32), pltpu.VMEM((nk, width, ATT_K), F32)],
        compiler_params=_params("arbitrary"),
    )(qkv, qkv, qkv, do, o, lse, fr)


def _ada_fwd(c_all, w_ada, b_loc):
    depth, _, n = w_ada.shape
    tn = 512

    def body(c_ref, w_ref, b_ref, o_ref, sc_ref):
        cv = c_ref[...]
        sc = cv * jax.nn.sigmoid(cv)
        sc_ref[...] = sc
        o_ref[0] = jnp.dot(sc.astype(BF16), w_ref[0].astype(BF16), preferred_element_type=F32) + b_ref[0]

    return pl.pallas_call(
        body, name="ada_fwd", grid=(depth, n // tn),
        in_specs=[pl.BlockSpec((N_DEV, D), lambda l, j: (0, 0)), pl.BlockSpec((1, D, tn), lambda l, j: (l, 0, j)),
                  pl.BlockSpec((1, 1, tn), lambda l, j: (l, 0, j))],
        out_specs=[pl.BlockSpec((1, N_DEV, tn), lambda l, j: (l, 0, j)), pl.BlockSpec((N_DEV, D), lambda l, j: (0, 0))],
        out_shape=[jax.ShapeDtypeStruct((depth, N_DEV, n), F32), jax.ShapeDtypeStruct((N_DEV, D), F32)],
        compiler_params=_params("arbitrary", "arbitrary"),
    )(c_all, w_ada, b_loc)


def _sum_devices(gathered):
    n = gathered.shape[1]
    tn = _pick(n, (1408, 1024, 640, 512, 128))

    def body(g_ref, o_ref):
        acc = g_ref[0:8, :]
        for dev in range(1, N_DEV):
            acc = acc + g_ref[8 * dev:8 * dev + 8, :]
        o_ref[...] = acc

    return pl.pallas_call(
        body, name="sum_devices", grid=(n // tn,),
        in_specs=[pl.BlockSpec((8 * N_DEV, tn), lambda j: (0, j))], out_specs=pl.BlockSpec((8, tn), lambda j: (0, j)),
        out_shape=jax.ShapeDtypeStruct((8, n), F32), compiler_params=_params("parallel"),
    )(gathered)


def _place():
    x, y, c = lax.axis_index("x"), lax.axis_index("y"), lax.axis_index("c")
    chips = [(1 - x, y), (x, 1 - y), (1 - x, 1 - y)]
    return x, y, c, chips


def _allgather8(block, name):
    m_per, n = block.shape

    def body(x_ref, out_ref, send_sems, recv_sems, local_sem):
        x, y, c, chips = _place()
        me, sibling = (x, y, c), (x, y, 1 - c)

        def rows(px, py, pc):
            return out_ref.at[pl.ds((4 * px + 2 * py + pc) * m_per, m_per), :]

        def copy(k, blk, to, src=None):
            return pltpu.make_async_remote_copy(
                src_ref=rows(*blk) if src is None else src, dst_ref=rows(*blk),
                send_sem=send_sems.at[k], recv_sem=recv_sems.at[k], device_id=to, device_id_type=MESH)

        mine = pltpu.make_async_copy(x_ref, rows(*me), local_sem)
        mine.start()
        first = [copy(0, me, sibling, src=x_ref)]
        first += [copy(1 + j, me, (*chip, c), src=x_ref) for j, chip in enumerate(chips)]
        for cp in first:
            cp.start()
        passed = [copy(4 + j, (*chip, c), sibling) for j, chip in enumerate(chips)]
        for j, chip in enumerate(chips):
            copy(1 + j, (*chip, c), me).wait_recv()
            passed[j].start()
        copy(0, sibling, me).wait_recv()
        for j, chip in enumerate(chips):
            copy(4 + j, (*chip, 1 - c), me).wait_recv()
        for cp in first + passed:
            cp.wait_send()
        mine.wait()

    return pl.pallas_call(
        body, name=name, out_shape=jax.ShapeDtypeStruct((N_DEV * m_per, n), block.dtype),
        in_specs=[pl.BlockSpec(memory_space=pltpu.VMEM)], out_specs=pl.BlockSpec(memory_space=pltpu.VMEM),
        scratch_shapes=[pltpu.SemaphoreType.DMA((7,)), pltpu.SemaphoreType.DMA((7,)), pltpu.SemaphoreType.DMA],
        compiler_params=pltpu.CompilerParams(vmem_limit_bytes=V7X_VMEM_LIMIT),
    )(block)


_SEM = pl.BlockSpec(memory_space=pltpu.SEMAPHORE)
_DATAFLOW = pltpu.SideEffectType.DATAFLOW_SIDE_EFFECTING


def _plan_copies(plan, refs, send_sems, recv_sems):
    return [pltpu.make_async_remote_copy(src_ref=src, dst_ref=dst, send_sem=send_sems.at[i], recv_sem=recv_sems.at[i],
                                         device_id=to, device_id_type=MESH) for i, (src, dst, to) in enumerate(plan(refs))]


class _Token(NamedTuple):
    after: jax.Array
    tie: jax.Array


def _after_operand(after):
    return after.after if isinstance(after, _Token) else after


def _copies_start(bufs, plan, n_copies, after, name):
    nb = len(bufs)

    def body(*refs):
        for cp in _plan_copies(plan, refs[:nb], refs[nb + 1], refs[nb + 2]):
            cp.start()
        for token in refs[-2:]:
            token[...] = jnp.zeros_like(token)

    sem = pltpu.SemaphoreType.DMA((n_copies,))
    vmem = pl.BlockSpec(memory_space=pltpu.VMEM)
    outs = pl.pallas_call(
        body, name=name,
        out_shape=(sem, sem, *[pltpu.HBM(b.shape, b.dtype) for b in bufs], jax.ShapeDtypeStruct((8, 128), F32),
                   jax.ShapeDtypeStruct((1, 1), F32)),
        in_specs=[_HBM] * nb + [pl.BlockSpec(memory_space=pl.ANY)],
        out_specs=(_SEM, _SEM, *[_HBM] * nb, vmem, vmem),
        input_output_aliases={i: 2 + i for i in range(nb)},
        compiler_params=pltpu.CompilerParams(has_side_effects=_DATAFLOW),
    )(*[pltpu.with_memory_space_constraint(b, pltpu.HBM) for b in bufs], _after_operand(after))
    return outs[0], outs[1], list(outs[2:2 + nb]), _Token(outs[-2], outs[-1])


def _copies_wait(started, plan, after, name):
    send_sems, recv_sems, bufs, _ = started
    nb = len(bufs)

    def body(*refs):
        for cp in _plan_copies(plan, refs[:nb], refs[nb], refs[nb + 1]):
            cp.wait_send()
            cp.wait_recv()

    return list(pl.pallas_call(
        body, name=name, out_shape=tuple(pltpu.HBM(b.shape, b.dtype) for b in bufs),
        in_specs=[_HBM] * nb + [_SEM, _SEM, pl.BlockSpec(memory_space=pl.ANY)], out_specs=tuple([_HBM] * nb),
        input_output_aliases={i: i for i in range(nb)},
        compiler_params=pltpu.CompilerParams(has_side_effects=_DATAFLOW),
    )(*bufs, send_sems, recv_sems, _after_operand(after)))


def _half_rows(ref, axis, c):
    half = ref.shape[axis] // 2
    return pl.ds(c * half, half)


def _plan_gather_ici(refs):
    n = len(refs) // 2
    x, y, c, chips = _place()
    out = []
    for a in range(n):
        rows = _half_rows(refs[a], 0, c)
        out += [(refs[a].at[rows], refs[n + a].at[2 * x + y, rows], (*chip, c)) for chip in chips]
        out.append((refs[a], refs[n + a].at[2 * x + y], (x, y, 1 - c)))
    return out


def _plan_gather_d2d(refs):
    x, y, c, chips = _place()
    out = []
    for ref in refs:
        rows = _half_rows(ref, 1, c)
        for px, py in chips:
            landed = ref.at[2 * px + py, rows]
            out.append((landed, landed, (x, y, 1 - c)))
    return out


def _plan_rs_sibling(refs):
    n = len(refs) // 2
    x, y, c, _ = _place()
    return [(refs[a].at[pl.ds(0, N_CHIPS), _half_rows(refs[a], 1, 1 - c)], refs[n + a], (x, y, 1 - c)) for a in range(n)]


def _plan_rs_chips(refs):
    n = len(refs) // 2
    x, y, c, chips = _place()
    return [(refs[a].at[2 * px + py], refs[n + a].at[k], (px, py, c)) for a in range(n) for k, (px, py) in enumerate(chips)]


def _plan_rs_share(layer):
    def plan(refs):
        x, y, c, _ = _place()
        return [(ref.at[layer, _half_rows(ref, 1, c)], ref.at[layer, _half_rows(ref, 1, c)], (x, y, 1 - c)) for ref in refs]
    return plan


def _chip_sum(g, other, sel, name):
    _, half, cdim = other.shape
    tr = _pick(half, (512, 256, 128, 64))
    per = half // tr

    def body(sel_ref, g_ref, t_ref, wire_ref, own_ref):
        total = g_ref[0] + t_ref[0]
        wire_ref[0] = total.astype(BF16)

        @pl.when(pl.program_id(1) == sel_ref[1])
        def _():
            own_ref[...] = total

    blk = pl.BlockSpec((1, tr, cdim), lambda i, p, sel_ref: (p, i, 0))
    return pl.pallas_call(
        body, name=name,
        grid_spec=pltpu.PrefetchScalarGridSpec(
            num_scalar_prefetch=1, grid=(per, N_CHIPS),
            in_specs=[pl.BlockSpec((1, tr, cdim), lambda i, p, sel_ref: (p, sel_ref[0] * per + i, 0)), blk],
            out_specs=[blk, pl.BlockSpec((tr, cdim), lambda i, p, sel_ref: (i, 0))]),
        out_shape=[jax.ShapeDtypeStruct(other.shape, BF16), jax.ShapeDtypeStruct((half, cdim), F32)],
        compiler_params=_params("parallel", "arbitrary"),
    )(sel, g, other)


def _final_sum(own, recv, sel, layer, into, name):
    half, cdim = own.shape
    tr = _pick(half, (512, 256, 128, 64))
    per = half // tr

    def body(sel_ref, own_ref, r0_ref, r1_ref, r2_ref, *rest):
        rest[-1][...] = ((own_ref[...] + r0_ref[0].astype(F32)) + r1_ref[0].astype(F32)) + r2_ref[0].astype(F32)

    part = lambda k: pl.BlockSpec((1, tr, cdim), lambda i, sel_ref, k=k: (k, i, 0))
    prior = [] if into is None else [into]
    return pl.pallas_call(
        body, name=name,
        grid_spec=pltpu.PrefetchScalarGridSpec(
            num_scalar_prefetch=1, grid=(per,),
            in_specs=[pl.BlockSpec((tr, cdim), lambda i, sel_ref: (i, 0)), part(0), part(1), part(2)]
            + [pl.BlockSpec(memory_space=pl.ANY)] * len(prior),
            out_specs=pl.BlockSpec((None, tr, cdim), lambda i, sel_ref: (layer, sel_ref[0] * per + i, 0))),
        out_shape=jax.ShapeDtypeStruct((DEPTH, 2 * half, cdim), F32),
        input_output_aliases={5: 0} if prior else {}, compiler_params=_params("parallel"),
    )(sel, own, recv, recv, recv, *prior)


def _row(v):
    return v.reshape(1, -1)


_BR_A, _BR_B, _BR_C = (0, A_WIDTH), (A_WIDTH, POOL_WIDTH), (A_WIDTH + POOL_WIDTH, CONV_WIDTH)


def _tie(v, token):
    return v if token is None else v + token.tie


def _no_hook(point, after, ready=None):
    return None


def _layer_fwd(x, w, mod, hook=_no_hook):
    s = x.shape[0]
    mod3 = mod.reshape(6, 1, D)
    h = _modnorm_fwd(x, _row(w["g_mix_pre"]), (mod3, 0), (mod3, 1), "mix_pre_fwd")
    z = _mm(h, w["w_all"], name="mm_in")
    qkv = z[:, Z_QKV:Z_PC].astype(BF16)
    fl = z[:, Z_FL:Z_COLS]
    cum = _cumf_fwd(fl, w["b_f_pad"])
    fr = cum[:, :HEADS].T.reshape(HEADS, s // ATT_K, ATT_K)
    br_a, lse = _attn_fwd(qkv, fr)
    br_b, br_c = _poolconv_fwd(z, w["w_pool_bd"], _tie(_row(w["pool_scale"]), hook("attn", lse)), w["conv_w"])
    hook("pool", br_b)
    wbr = w["w_branch"]
    pa = _mm(br_a, wbr, b_rows=_BR_A, name="mm_br_a")
    pb = _mm(br_b, wbr, b_rows=_BR_B, name="mm_br_b")
    pc = _mm(br_c, wbr, b_rows=_BR_C, name="mm_br_c")
    merged = _merge_fwd(z, pa, pb, pc)
    y = _mm(merged, w["w_out"], name="mm_out")
    x1, h2 = _post_pre_fwd(x, y, _row(w["g_mix_post"]), (mod3, 2), _row(w["g_ff_pre"]), (mod3, 3), (mod3, 4), "mix_post_ff_pre_fwd")
    a, r = _mm(h2, w["w_ff1"], b_split=N_CHIPS, epilogue=_relu2_fwd, out_dtype=(F32, BF16), name="mm_ff1")
    y2 = _mm(r, w["w_ff2"], name="mm_ff2")
    x2 = _post_fwd(x1, y2, _tie(_row(w["g_ff_post"]), hook("ff_post", y2)), (mod3, 5), "ff_post_fwd")
    hook("end", x2)
    saved = dict(x=x, h=h, z=z, qkv=qkv, fl=fl, fr=fr, lse=lse, br_a=br_a, br_b=br_b, br_c=br_c, pa=pa, pb=pb, pc=pc,
                 merged=merged, y=y, x1=x1, h2=h2, a=a, r=r, y2=y2)
    return x2, saved


def _layer_bwd(dx2, sv, w, mod, hook=_no_hook):
    s = dx2.shape[0]
    mod3 = mod.reshape(6, 1, D)
    dy2, sum_ff_post = _post_bwd(dx2, sv["y2"], _row(w["g_ff_post"]), (mod3, 5), "ff_post_bwd")
    (da,) = _mm(dy2, w["w_ff2"], tb=True, epilogue=_relu2_bwd, extras=(sv["a"],), out_dtype=(BF16,), name="mm_ff2_dx")
    d_w_ff2 = _mm(sv["r"], dy2, ta=True, name="mm_ff2_dw")
    dh2 = _mm(da, w["w_ff1"], tb=True, b_split=N_CHIPS, name="mm_ff1_dx")
    d_w_ff1 = _mm(sv["h2"], da, ta=True, out_split=N_CHIPS, name="mm_ff1_dw")
    g_ff_pre = _tie(_row(w["g_ff_pre"]), hook("ff_pre", dh2, dict(w_ff1=d_w_ff1, w_ff2=d_w_ff2)))
    dx1, dy, sum_mid = _pre_post_bwd(dh2, sv["x1"], dx2, g_ff_pre, (mod3, 4), sv["y"], _row(w["g_mix_post"]), (mod3, 2), "ff_pre_mix_post_bwd")
    sum_ff_pre, sum_mix_post = sum_mid, sum_mid[3:]
    dmerged = _mm(dy, w["w_out"], tb=True, name="mm_out_dx")
    d_w_out = _mm(sv["merged"], dy, ta=True, name="mm_out_dw")
    dz, dpa, dpb, dpc = _merge_bwd(dmerged, sv["z"], sv["pa"], sv["pb"], sv["pc"])
    wbr = w["w_branch"]
    dbr_a = _mm(dpa, wbr, tb=True, b_rows=_BR_A, name="mm_br_a_dx")
    dbr_b = _mm(dpb, wbr, tb=True, b_rows=_BR_B, name="mm_br_b_dx")
    dbr_c = _mm(dpc, wbr, tb=True, b_rows=_BR_C, name="mm_br_c_dx")
    d_w_branch = jnp.concatenate([_mm(sv["br_a"], dpa, ta=True, name="mm_br_a_dw"), _mm(sv["br_b"], dpb, ta=True, name="mm_br_b_dw"),
                                  _mm(sv["br_c"], dpc, ta=True, name="mm_br_c_dw")], axis=0)

    dq, dk, dv, dfc, dfr = _attn_bwd(sv["qkv"], dbr_a, sv["br_a"], sv["lse"], sv["fr"])
    dcum = dfc + jnp.pad(dfr.reshape(HEADS, s).T, ((0, 0), (0, 128 - HEADS)))
    dfl, sum_bf = _cumf_bwd(dcum, sv["fl"], _tie(w["b_f_pad"], hook("cumf", dfc)))
    dpc_z, d_wbd, sum_ps, sum_cw = _poolconv_bwd(dbr_b, dbr_c, sv["z"], w["w_pool_bd"], _row(w["pool_scale"]), w["conv_w"])
    for at, part in ((Z_QKV, dq), (Z_QKV + A_WIDTH, dk), (Z_QKV + 2 * A_WIDTH, dv), (Z_PC, dpc_z), (Z_FL, dfl)):
        dz = lax.dynamic_update_slice(dz, part, (0, at))
    dh = _mm(dz, w["w_all"], tb=True, name="mm_in_dx")
    d_w_all = _mm(sv["h"], dz, ta=True, name="mm_in_dw")
    hook("mix_pre", dh)
    dx, sum_mix_pre = _modnorm_bwd(dh, sv["x"], dx1, _row(w["g_mix_pre"]), (mod3, 1), "mix_pre_bwd")

    dmod = jnp.stack([sum_mix_pre[0], sum_mix_pre[1], sum_mix_post[0], sum_ff_pre[0], sum_ff_pre[1], sum_ff_post[0]])
    d_w_in = _w_in_shards(d_w_all)
    d_w_pool = jnp.stack([d_wbd[64 * g:64 * g + 64, 64 * g:64 * g + 64] for g in range(4)])
    big = dict(w_in=d_w_in, w_branch=d_w_branch, w_out=d_w_out, w_ff1=d_w_ff1, w_ff2=d_w_ff2)
    small = dict(g_mix_pre=sum_mix_pre[2], g_mix_post=sum_mix_post[1], g_ff_pre=sum_ff_pre[2], g_ff_post=sum_ff_post[1],
                 b_f=sum_bf[0, :HEADS], w_pool=d_w_pool, pool_scale=sum_ps[0], conv_w=sum_cw[0:3])
    return dx, dmod, big, small


_QKV_END, _FL_END, _PC_END = 3 * A_WIDTH, 3 * A_WIDTH + HEADS, 3 * A_WIDTH + HEADS + POOL_WIDTH + 3 * CONV_WIDTH
_W_IN_GROUPS = ((_PC_END, IN_COLS, Z_GL), (0, _QKV_END, Z_QKV), (_FL_END, _PC_END, Z_PC), (_QKV_END, _FL_END, Z_FL))
_SHARD_COLS = IN_COLS // N_CHIPS


def _w_all_from_shards(blocks):
    pieces = []
    for lo, hi, _ in _W_IN_GROUPS:
        for p in range(N_CHIPS):
            a, b = max(lo, p * _SHARD_COLS), min(hi, (p + 1) * _SHARD_COLS)
            if a < b:
                pieces.append(blocks[p][:, a - p * _SHARD_COLS:b - p * _SHARD_COLS])
    pieces.append(jnp.zeros((D, Z_COLS - IN_COLS), blocks.dtype))
    return jnp.concatenate(pieces, axis=1)


def _w_in_shards(d_w_all):
    blocks = []
    for p in range(N_CHIPS):
        pieces = []
        for lo, hi, at in sorted(_W_IN_GROUPS):
            a, b = max(lo, p * _SHARD_COLS), min(hi, (p + 1) * _SHARD_COLS)
            if a < b:
                pieces.append(d_w_all[:, at + a - lo:at + b - lo])
        blocks.append(jnp.concatenate(pieces, axis=1))
    return jnp.stack(blocks)


def _full_layer_weights(w_in_blocks, w_branch, w_out, w_ff1, w_ff2, g_mix_pre, g_mix_post, g_ff_pre, g_ff_post, b_f, w_pool, pool_scale, conv_w):
    w_all = _w_all_from_shards(w_in_blocks)
    wbd = (w_pool[:, :, None, :] * jnp.eye(4, dtype=F32)[:, None, :, None]).reshape(POOL_WIDTH, POOL_WIDTH)
    return dict(w_all=w_all, w_branch=w_branch, w_out=w_out, w_ff1=w_ff1, w_ff2=w_ff2, g_mix_pre=g_mix_pre, g_mix_post=g_mix_post,
                g_ff_pre=g_ff_pre, g_ff_post=g_ff_post, b_f_pad=jnp.pad(b_f, (0, 128 - HEADS)).reshape(1, 128), w_pool_bd=wbd,
                pool_scale=pool_scale, conv_w=conv_w)


class _NoComm:
    def layer_weights(self, l):
        raise NotImplementedError

    def fwd_hook(self, l):
        return _no_hook

    def bwd_hook(self, l):
        return _no_hook

    def grads_ready(self, l, big):
        return None


class _Layers(_NoComm):
    def __init__(self, layers):
        self.layers = layers

    def layer_weights(self, l):
        return self.layers[l]


def _local_step(x, target, mods, comm):
    saved, weights = [], []
    act = x
    for l in range(DEPTH):
        weights.append(comm.layer_weights(l))
        act, sv = _layer_fwd(act, weights[l], mods[l], comm.fwd_hook(l))
        saved.append(sv)
    dact, sq = _loss_head(act, target)
    loss = sq[0, 0] * (0.5 / D)
    dmods, bigs, smalls = [None] * DEPTH, [None] * DEPTH, [None] * DEPTH
    token = None
    for l in reversed(range(DEPTH)):
        dact, dmods[l], bigs[l], smalls[l] = _layer_bwd(dact, saved[l], weights[l], _tie(mods[l], token), comm.bwd_hook(l))
        token = comm.grads_ready(l, bigs[l])
    return loss, dact, jnp.stack(dmods), bigs, smalls


_BIG = ("w_in", "w_branch", "w_out", "w_ff1", "w_ff2")


class _GatherJob:
    def __init__(self, tag, shards, after):
        self.tag, self.n = tag, len(shards)
        lands = [lax.empty((N_CHIPS,) + s.shape, s.dtype) for s in shards]
        self.state = _copies_start(list(shards) + lands, _plan_gather_ici, 4 * self.n, after, "gather_ici_start_" + tag)
        self.token = self.state[3]

    def pass_on(self, after):
        bufs = _copies_wait(self.state, _plan_gather_ici, after, "gather_ici_wait_" + self.tag)
        self.state = _copies_start(bufs[self.n:], _plan_gather_d2d, 3 * self.n, bufs[0], "gather_d2d_start_" + self.tag)
        self.token = self.state[3]
        return self.token

    def done(self, after):
        return _copies_wait(self.state, _plan_gather_d2d, after, "gather_d2d_wait_" + self.tag)


class _ReduceJob:
    def __init__(self, tag, names, grads, sel, after, layer, into=None):
        self.tag, self.names, self.n, self.sel, self.layer, self.into = tag, names, len(names), sel, layer, into or {}
        lands = [lax.empty((N_CHIPS, g.shape[1] // 2, g.shape[2]), F32) for g in grads]
        self.state = _copies_start(list(grads) + lands, _plan_rs_sibling, self.n, after, "rs_sibling_start_" + tag)
        self.token = self.state[3]

    def chip_sums(self, after):
        bufs = _copies_wait(self.state, _plan_rs_sibling, after, "rs_sibling_wait_" + self.tag)
        wires, self.owns = zip(*[_chip_sum(bufs[i], bufs[self.n + i], self.sel, "rs_chip_sum_" + name) for i, name in enumerate(self.names)])
        lands = [lax.empty((3,) + w.shape[1:], BF16) for w in wires]
        self.state = _copies_start(list(wires) + lands, _plan_rs_chips, 3 * self.n, self.owns[0], "rs_chips_start_" + self.tag)
        self.token = self.state[3]
        return self.token

    def final_sums(self, after):
        bufs = _copies_wait(self.state, _plan_rs_chips, after, "rs_chips_wait_" + self.tag)
        sums = [_final_sum(self.owns[i], bufs[self.n + i], self.sel, self.layer, self.into.get(name), "rs_final_" + name)
                for i, name in enumerate(self.names)]
        self.state = _copies_start(sums, _plan_rs_share(self.layer), self.n, sums[0], "rs_share_start_" + self.tag)
        self.token = self.state[3]
        return self.token

    def done(self, after):
        return dict(zip(self.names, _copies_wait(self.state, _plan_rs_share(self.layer), after, "rs_share_wait_" + self.tag)))


def _chip_blocks(g):
    return g if g.ndim == 3 else g.reshape(N_CHIPS, -1, g.shape[1])


class _StepComm(_NoComm):
    def __init__(self, shards, sel, after):
        self.sel = sel
        self.small, self.grads, self.jobs = None, {}, {}
        self.jobs["in0"] = _GatherJob("in0", shards[0][:1], after)
        self.jobs["rest0"] = _GatherJob("rest0", shards[0][1:], self.jobs["in0"].token)
        self.jobs["all1"] = _GatherJob("all1", shards[1], self.jobs["rest0"].token)

    def layer_weights(self, l):
        if l == 0:
            job = self.jobs["in0"]
            (g_in,) = job.done(job.pass_on(self.jobs["all1"].token))
            self.weights0 = _full_layer_weights(g_in, None, None, None, None, *self.small[0])
            return self.weights0
        g_in, g_br, g_out, g_f1, g_f2 = self.landed1
        return _full_layer_weights(g_in, g_br.reshape(D, D), g_out.reshape(D, D), g_f1, g_f2.reshape(D_FF, D), *self.small[1])

    def fwd_hook(self, l):
        if l != 0:
            return _no_hook

        def hook(point, after, ready=None):
            if point == "attn":
                return self.jobs["rest0"].pass_on(after)
            if point == "ff_post":
                return self.jobs["all1"].pass_on(after)
            if point == "pool":
                g_br, g_out, g_f1, g_f2 = self.jobs["rest0"].done(after)
                self.weights0.update(w_branch=g_br.reshape(D, D), w_out=g_out.reshape(D, D), w_ff1=g_f1, w_ff2=g_f2.reshape(D_FF, D))
            if point == "end":
                self.landed1 = self.jobs["all1"].done(after)
            return None
        return hook

    def bwd_hook(self, l):
        if l != 0:
            return _no_hook

        def hook(point, after, ready=None):
            jobs = self.jobs
            if point == "ff_pre":
                token = jobs["rs1"].chip_sums(after)
                jobs["rs0_ff"] = _ReduceJob("0_ff", ("w_ff1", "w_ff2"), [_chip_blocks(ready[n]) for n in ("w_ff1", "w_ff2")], self.sel, token, 0)
                return jobs["rs0_ff"].token
            if point == "cumf":
                return jobs["rs0_ff"].chip_sums(jobs["rs1"].final_sums(after))
            self.layer1 = jobs["rs1"].done(after)
            jobs["rs0_ff"].into = self.layer1
            return None
        return hook

    def grads_ready(self, l, big):
        if l == 1:
            self.jobs["rs1"] = _ReduceJob("1", _BIG, [_chip_blocks(big[n]) for n in _BIG], self.sel, self.sel, 1)
            return self.jobs["rs1"].token
        names = ("w_in", "w_branch", "w_out")
        self.jobs["rs0_mix"] = _ReduceJob("0_mix", names, [_chip_blocks(big[n]) for n in names], self.sel, self.sel, 0, self.layer1)
        return self.jobs["rs0_mix"].token

    def finish_sums(self, after):
        jobs = self.jobs
        token = jobs["rs0_mix"].chip_sums(after)
        return jobs["rs0_ff"].final_sums(token)

    def finish_ff(self, after):
        self.grads.update(self.jobs["rs0_ff"].done(after))

    def finish_mix(self, after):
        job = self.jobs["rs0_mix"]
        self.grads.update(job.done(job.final_sums(after)))


_SMALL = ("g_mix_pre", "g_mix_post", "g_ff_pre", "g_ff_post", "b_f", "w_pool", "pool_scale", "conv_w")


def _w_in_view(t):
    return t.reshape(DEPTH, D // 128, 128, _SHARD_COLS).transpose(3, 1, 0, 2).reshape(_SHARD_COLS * (D // 128) * DEPTH, 128)


def _w_in_unview(t):
    return t.reshape(_SHARD_COLS, D // 128, DEPTH, 128).transpose(2, 1, 3, 0).reshape(DEPTH, D, _SHARD_COLS)


def _pack(parts, rows=8):
    flat = jnp.concatenate([p.reshape(-1) for p in parts])
    width = -(-flat.shape[0] // (rows * 128)) * 128
    return jnp.pad(flat, (0, rows * width - flat.shape[0])).reshape(rows, width)


def _unpack(packed, like):
    flat = packed.reshape(-1)
    out, at = [], 0
    for ref in like:
        out.append(flat[at:at + ref.size].reshape(ref.shape))
        at += ref.size
    return out


def kernel(x, c, w_ada, b_ada, g_mix_pre, g_mix_post, g_ff_pre, g_ff_post, w_in, b_f, w_pool, pool_scale, conv_w, w_branch, w_out, w_ff1, w_ff2, loss_target, m_w_ada, m_b_ada, m_g_mix_pre, m_g_mix_post, m_g_ff_pre, m_g_ff_post, m_w_in, m_b_f, m_w_pool, m_pool_scale, m_conv_w, m_w_branch, m_w_out, m_w_ff1, m_w_ff2, v_w_ada, v_b_ada, v_g_mix_pre, v_g_mix_post, v_g_ff_pre, v_g_ff_post, v_w_in, v_b_f, v_w_pool, v_pool_scale, v_conv_w, v_w_branch, v_w_out, v_w_ff1, v_w_ff2):
    xi, yi, ci = lax.axis_index("x"), lax.axis_index("y"), lax.axis_index("c")
    chip = 2 * xi + yi
    dev = 2 * chip + ci
    n_ada = w_ada.shape[2]

    first = jnp.zeros((8, D + 384), F32).at[0, :D].set(c[0]).at[0, D:].set(conv_w.reshape(-1))
    got = _allgather8(first, "gather_cond").reshape(N_DEV, 8, D + 384)[:, 0]
    c_all = got[:, :D]
    conv_full = got[0::2, D:].reshape(N_CHIPS, DEPTH, 3, CONV_WIDTH // N_CHIPS).transpose(1, 2, 0, 3).reshape(DEPTH, 3, CONV_WIDTH)

    b_loc = lax.dynamic_slice_in_dim(b_ada, chip * n_ada, n_ada, axis=1).reshape(DEPTH, 1, n_ada)
    mod_cols, silu_c = _ada_fwd(c_all, w_ada, b_loc)
    got = _allgather8(mod_cols.reshape(DEPTH * N_DEV, n_ada), "gather_mod").reshape(N_DEV, DEPTH, N_DEV, n_ada)[0::2]
    mod_all = got.transpose(1, 2, 0, 3).reshape(DEPTH, N_DEV, 6, D)
    mods = lax.dynamic_index_in_dim(mod_all, dev, axis=1, keepdims=False)

    comm = _StepComm([[w[l].astype(BF16) for w in (w_in, w_branch, w_out, w_ff1, w_ff2)] for l in range(DEPTH)],
                     jnp.stack([ci, chip]).astype(jnp.int32), mods)
    comm.small = [(g_mix_pre[l], g_mix_post[l], g_ff_pre[l], g_ff_post[l], b_f[l], w_pool[l], pool_scale[l], conv_full[l]) for l in range(DEPTH)]
    loss_part, grad_x, dmods, bigs, smalls = _local_step(x[0], loss_target[0], mods, comm)

    small_parts = [smalls[l][name] for name in _SMALL for l in range(DEPTH)] + [loss_part.reshape(1)]
    packed = _tie(_pack([dmods] + small_parts), comm.jobs["rs0_mix"].token)
    gathered = _allgather8(packed, "gather_small")
    dmod_all = gathered.reshape(N_DEV, -1)[:, :dmods.size].reshape(N_DEV, DEPTH, 6 * D)
    summed = _unpack(_sum_devices(gathered), [dmods] + small_parts)
    grad_b_ada = summed[0].reshape(DEPTH, 6 * D)
    loss = summed[-1][0]
    small_grads = {name: jnp.stack(summed[1 + 2 * i:3 + 2 * i]) for i, name in enumerate(_SMALL)}
    small_grads["conv_w"] = lax.dynamic_slice_in_dim(small_grads["conv_w"], chip * (CONV_WIDTH // N_CHIPS), CONV_WIDTH // N_CHIPS, axis=2)

    dmod_loc = lax.dynamic_slice_in_dim(dmod_all.transpose(1, 0, 2), chip * n_ada, n_ada, axis=2)
    tail_token = comm.finish_sums(grad_b_ada)
    silu_pad = _tie(jnp.pad(silu_c, ((0, 128 - N_DEV), (0, 0))), tail_token)
    dmod_pad = jnp.pad(dmod_loc.transpose(1, 0, 2).reshape(N_DEV, DEPTH * n_ada), ((0, 128 - N_DEV), (0, 0)))
    grad_w_ada = _mm(silu_pad, dmod_pad, ta=True, out_split=DEPTH, name="mm_ada_dw")

    grads = dict(w_ada=grad_w_ada, b_ada=grad_b_ada, **small_grads)
    weights = dict(w_ada=w_ada, b_ada=b_ada, g_mix_pre=g_mix_pre, g_mix_post=g_mix_post, g_ff_pre=g_ff_pre, g_ff_post=g_ff_post, w_in=w_in,
                   b_f=b_f, w_pool=w_pool, pool_scale=pool_scale, conv_w=conv_w, w_branch=w_branch, w_out=w_out, w_ff1=w_ff1, w_ff2=w_ff2)
    m_in = dict(w_ada=m_w_ada, b_ada=m_b_ada, g_mix_pre=m_g_mix_pre, g_mix_post=m_g_mix_post, g_ff_pre=m_g_ff_pre, g_ff_post=m_g_ff_post,
                w_in=m_w_in, b_f=m_b_f, w_pool=m_w_pool, pool_scale=m_pool_scale, conv_w=m_conv_w, w_branch=m_w_branch, w_out=m_w_out,
                w_ff1=m_w_ff1, w_ff2=m_w_ff2)
    v_in = dict(w_ada=v_w_ada, b_ada=v_b_ada, g_mix_pre=v_g_mix_pre, g_mix_post=v_g_mix_post, g_ff_pre=v_g_ff_pre, g_ff_post=v_g_ff_post,
                w_in=v_w_in, b_f=v_b_f, w_pool=v_w_pool, pool_scale=v_pool_scale, conv_w=v_conv_w, w_branch=v_w_branch, w_out=v_w_out,
                w_ff1=v_w_ff1, w_ff2=v_w_ff2)
    order = ("w_ada", "b_ada", "g_mix_pre", "g_mix_post", "g_ff_pre", "g_ff_post", "w_in", "b_f", "w_pool", "pool_scale", "conv_w",
             "w_branch", "w_out", "w_ff1", "w_ff2")
    delta, new_m, new_v = {}, {}, {}
    tiny = ("b_ada",) + _SMALL
    tiny_g = [_tie(grads[tiny[0]], tail_token)] + [grads[name] for name in tiny[1:]]
    res = _adamw_many([weights[name] for name in tiny], tiny_g, [m_in[name] for name in tiny], [v_in[name] for name in tiny], "adamw_small")
    for out, vals in zip((delta, new_m, new_v), res):
        out.update(zip(tiny, vals))
    delta["w_ada"], new_m["w_ada"], new_v["w_ada"] = _adamw(w_ada, grad_w_ada, m_w_ada, v_w_ada, "adamw_w_ada")
    comm.finish_ff(delta["w_ada"][0, :8, :128] + delta["b_ada"][0, :128])
    for name in ("w_ff1", "w_ff2", "w_in", "w_branch", "w_out"):
        if name == "w_in":
            comm.finish_mix(delta["w_ff2"][0, :8, :128])
        grads[name] = comm.grads[name]
        if name == "w_in":
            g_view = lax.optimization_barrier(_w_in_view(grads[name]))
            res = _adamw(_w_in_view(w_in), g_view, _w_in_view(m_w_in), _w_in_view(v_w_in), "adamw_w_in")
            grads[name], delta[name], new_m[name], new_v[name] = [_w_in_unview(t) for t in (g_view, *res)]
        else:
            delta[name], new_m[name], new_v[name] = _adamw(weights[name], grads[name], m_in[name], v_in[name], "adamw_" + name)

    return (loss, grad_x[None], *[grads[n] for n in order], *[delta[n] for n in order], *[new_m[n] for n in order],
            *[new_v[n] for n in order])
```

```python
from typing import NamedTuple

import jax
import jax.numpy as jnp
from jax import lax
from jax.experimental import pallas as pl
from jax.experimental.pallas import tpu as pltpu

F32 = jnp.float32
BF16 = jnp.bfloat16
MESH = pl.DeviceIdType.MESH

D = 1024
DEPTH = 2
HEADS = 8
HEAD_DIM = 64
A_WIDTH = 512
POOL_WIDTH = 256
CONV_WIDTH = 256
D_FF = 4096
IN_COLS = 5640
Z_GL, Z_QKV, Z_PC, Z_FL, Z_COLS = 0, 3072, 4608, 5632, 5760
RMS_EPS = 1e-6
NEG_INF = -1e30
ROW_TILE = 512
EW_ROWS = 256
N_CHIPS = 4
N_DEV = 8
V7X_VMEM_LIMIT = 48 * 1024 * 1024

ADAM_LR = 0.001
ADAM_B1 = 0.9
ADAM_B2 = 0.999
ADAM_EPS = 1e-08
ADAM_WD = 0.01
ADAM_STEP = 10

_HBM = pl.BlockSpec(memory_space=pltpu.HBM)


def _params(*sem):
    return pltpu.CompilerParams(dimension_semantics=sem, vmem_limit_bytes=V7X_VMEM_LIMIT)


def _pick(dim, cands):
    for cand in cands:
        if dim % cand == 0:
            return cand
    return dim


def _mm(a, b, *, ta=False, tb=False, b_rows=None, b_split=1, out_split=1, out_dtype=F32, epilogue=None, extras=(), name):
    (k, m) = a.shape if ta else a.shape[::-1]
    b_row0, b_rows = (0, b.shape[-2]) if b_rows is None else b_rows
    b_cols = b.shape[-1] * b_split
    (n, k2) = (b_rows, b_cols) if tb else (b_cols, b_rows)
    assert k == k2, (a.shape, b.shape, ta, tb)
    n_unit = n // (out_split * (1 if tb else b_split))
    k_unit = k // (b_split if tb else 1)
    tm = _pick(m, (2048, 1024, 512, 256, 128) if k <= 1024 and not ta else (1024, 512, 256, 128))
    tn = _pick(n_unit, (1024, 1152, 768, 640, 512, 256, 128))
    tk = _pick(k_unit, (1024, 1152, 512, 640, 256, 128))
    nk = k // tk
    dims = (((0 if ta else 1,), (1 if tb else 0,)), ((), ()))

    def dot(a_ref, b_ref):
        b_val = b_ref[0] if b_split > 1 else b_ref[...]
        return lax.dot_general(a_ref[...].astype(BF16), b_val.astype(BF16), dims, preferred_element_type=F32)

    n_extra = len(extras)
    assert epilogue is None or out_split == 1

    def put(refs, val):
        if epilogue is not None:
            for o_ref, res in zip(refs[n_extra:], epilogue(val, *[r[...] for r in refs[:n_extra]])):
                o_ref[...] = res.astype(o_ref.dtype)
        elif out_split > 1:
            refs[0][0] = val.astype(refs[0].dtype)
        else:
            refs[0][...] = val.astype(refs[0].dtype)

    def body_single(a_ref, b_ref, *refs):
        put(refs, dot(a_ref, b_ref))

    def body_acc(a_ref, b_ref, *refs):
        kk = pl.program_id(2)
        acc_ref = refs[-1]

        @pl.when(kk == 0)
        def _():
            acc_ref[...] = jnp.zeros_like(acc_ref)

        acc_ref[...] += dot(a_ref, b_ref)

        @pl.when(kk == nk - 1)
        def _():
            put(refs[:-1], acc_ref[...])

    a_spec = pl.BlockSpec((tk, tm), lambda i, j, kk: (kk, i)) if ta else pl.BlockSpec((tm, tk), lambda i, j, kk: (i, kk))
    if b_split == 1:
        off = b_row0 // (tn if tb else tk)
        assert off * (tn if tb else tk) == b_row0
        b_spec = pl.BlockSpec((tn, tk), lambda i, j, kk: (j + off, kk)) if tb else pl.BlockSpec((tk, tn), lambda i, j, kk: (kk + off, j))
    elif tb:
        per = k_unit // tk
        b_spec = pl.BlockSpec((1, tn, tk), lambda i, j, kk: (kk // per, j, kk % per))
    else:
        per = n // b_split // tn
        b_spec = pl.BlockSpec((1, tk, tn), lambda i, j, kk: (j // per, kk, j % per))
    if out_split == 1:
        o_spec = pl.BlockSpec((tm, tn), lambda i, j, kk: (i, j))
        o_shape = None if epilogue is not None else jax.ShapeDtypeStruct((m, n), out_dtype)
    else:
        per_o = n // out_split // tn
        o_spec = pl.BlockSpec((1, tm, tn), lambda i, j, kk: (j // per_o, i, j % per_o))
        o_shape = jax.ShapeDtypeStruct((out_split, m, n // out_split), out_dtype)
    if epilogue is not None:
        o_shape = [jax.ShapeDtypeStruct((m, n), dt) for dt in out_dtype]
        o_spec = [o_spec] * len(out_dtype)
    return pl.pallas_call(
        body_single if nk == 1 else body_acc, name=name, grid=(m // tm, n // tn, nk),
        in_specs=[a_spec, b_spec] + [pl.BlockSpec((tm, tn), lambda i, j, kk: (i, j))] * n_extra, out_specs=o_spec, out_shape=o_shape,
        scratch_shapes=[] if nk == 1 else [pltpu.VMEM((tm, tn), F32)],
        compiler_params=_params("parallel", "parallel", "arbitrary"),
    )(a, b, *extras)


def _ew(fn, ins, out_dtypes, name, tc=None):
    shape = ins[0].shape
    lead, (rows, cols) = shape[:-2], shape[-2:]
    tc = cols if tc is None else tc
    if tc > 1024:
        tr = _pick(rows, (EW_ROWS, 128, 8))
    elif tc > 128:
        tr = _pick(rows, (2 * EW_ROWS, EW_ROWS, 128, 8))
    else:
        tr = _pick(rows, (4096, 2256, 2048, 1024, EW_ROWS, 8))
    n_in = len(ins)

    def body(*refs):
        res = fn(*[r[...] for r in refs[:n_in]])
        for o_ref, val in zip(refs[n_in:], res):
            o_ref[...] = val.astype(o_ref.dtype)

    if lead:
        spec = pl.BlockSpec((None, tr, tc), lambda l, i, j: (l, i, j))
    else:
        spec = pl.BlockSpec((tr, tc), lambda i, j: (i, j))
    return pl.pallas_call(
        body, name=name, grid=lead + (rows // tr, cols // tc),
        in_specs=[spec] * n_in, out_specs=[spec] * len(out_dtypes),
        out_shape=[jax.ShapeDtypeStruct(shape, dt) for dt in out_dtypes],
        compiler_params=_params(*(["parallel"] * (len(lead) + 2))),
    )(*ins)


def _relu2_fwd(a):
    r = jnp.maximum(a, 0.0)
    return a, r * r


def _relu2_bwd(dr, a):
    return (dr * (2.0 * jnp.maximum(a, 0.0)),)


def _adamw_math(w, g, m, v):
    m = ADAM_B1 * m + (1.0 - ADAM_B1) * g
    v = ADAM_B2 * v + (1.0 - ADAM_B2) * (g * g)
    m_hat = m / (1.0 - ADAM_B1 ** ADAM_STEP)
    v_hat = v / (1.0 - ADAM_B2 ** ADAM_STEP)
    delta = -ADAM_LR * (m_hat / (jnp.sqrt(v_hat) + ADAM_EPS) + ADAM_WD * w)
    return delta, m, v


def _adamw(w, g, m, v, name):
    return _ew(_adamw_math, [w, g, m, v], [F32, F32, F32], name)


def _adamw_many(ws, gs, ms, vs, name):
    n = len(ws)

    def body(*refs):
        for i in range(n):
            res = _adamw_math(*[refs[k * n + i][...] for k in range(4)])
            for k in range(3):
                refs[(4 + k) * n + i][...] = res[k]

    outs = pl.pallas_call(
        body, name=name, out_shape=[jax.ShapeDtypeStruct(w.shape, F32) for w in ws] * 3,
        compiler_params=pltpu.CompilerParams(vmem_limit_bytes=V7X_VMEM_LIMIT),
    )(*ws, *gs, *ms, *vs)
    return outs[:n], outs[n:2 * n], outs[2 * n:]


def _row_spec(cols, block=0):
    return pl.BlockSpec((ROW_TILE, cols), lambda i, block=block: (i, block))


def _vec_spec(cols):
    return pl.BlockSpec((1, cols), lambda i: (0, 0))


def _vec_args(*vecs):
    arrays = [v[0] if isinstance(v, tuple) else v for v in vecs]
    specs = [pl.BlockSpec((None, 1, D), lambda i, row=v[1]: (row, 0, 0)) if isinstance(v, tuple) else _vec_spec(D) for v in vecs]
    return arrays, specs


def _sum_spec(cols):
    return pl.BlockSpec((8, cols), lambda i: (0, 0))


def _rstd(x):
    return lax.rsqrt(jnp.mean(x * x, axis=-1, keepdims=True) + RMS_EPS)


def _modnorm_fwd(x, g, shift, scale, name):
    s = x.shape[0]

    def body(x_ref, g_ref, sh_ref, sc_ref, h_ref):
        xv = x_ref[...]
        n = xv * _rstd(xv)
        h_ref[...] = ((n * g_ref[...]) * (1.0 + sc_ref[...]) + sh_ref[...]).astype(BF16)

    vecs, vec_specs = _vec_args(g, shift, scale)
    return pl.pallas_call(
        body, name=name, grid=(s // ROW_TILE,),
        in_specs=[_row_spec(D)] + vec_specs, out_specs=_row_spec(D),
        out_shape=jax.ShapeDtypeStruct((s, D), BF16), compiler_params=_params("parallel"),
    )(x, *vecs)


def _post_fwd(x, y, g, gate, name):
    s = x.shape[0]

    def body(x_ref, y_ref, g_ref, gate_ref, o_ref):
        yv = y_ref[...]
        o_ref[...] = x_ref[...] + gate_ref[...] * ((yv * _rstd(yv)) * g_ref[...])

    vecs, vec_specs = _vec_args(g, gate)
    return pl.pallas_call(
        body, name=name, grid=(s // ROW_TILE,),
        in_specs=[_row_spec(D), _row_spec(D)] + vec_specs, out_specs=_row_spec(D),
        out_shape=jax.ShapeDtypeStruct((s, D), F32), compiler_params=_params("parallel"),
    )(x, y, *vecs)


def _post_bwd(dxo, y, g, gate, name):
    s = dxo.shape[0]

    def body(d_ref, y_ref, g_ref, gate_ref, dy_ref, sum_ref):
        @pl.when(pl.program_id(0) == 0)
        def _():
            sum_ref[...] = jnp.zeros_like(sum_ref)

        dv, yv = d_ref[...], y_ref[...]
        r = _rstd(yv)
        n = yv * r
        sum_ref[0:1, :] += jnp.sum(dv * (n * g_ref[...]), axis=0, keepdims=True)
        sum_ref[1:2, :] += jnp.sum((dv * gate_ref[...]) * n, axis=0, keepdims=True)
        dn = (dv * gate_ref[...]) * g_ref[...]
        dy_ref[...] = (r * (dn - n * jnp.mean(dn * n, axis=-1, keepdims=True))).astype(BF16)

    vecs, vec_specs = _vec_args(g, gate)
    return pl.pallas_call(
        body, name=name, grid=(s // ROW_TILE,),
        in_specs=[_row_spec(D), _row_spec(D)] + vec_specs,
        out_specs=[_row_spec(D), _sum_spec(D)],
        out_shape=[jax.ShapeDtypeStruct((s, D), BF16), jax.ShapeDtypeStruct((8, D), F32)],
        compiler_params=_params("arbitrary"),
    )(dxo, y, *vecs)


def _modnorm_bwd(dh, x, dxo, g, scale, name):
    s = dh.shape[0]

    def body(dh_ref, x_ref, d_ref, g_ref, sc_ref, dx_ref, sum_ref):
        @pl.when(pl.program_id(0) == 0)
        def _():
            sum_ref[...] = jnp.zeros_like(sum_ref)

        dhv, xv = dh_ref[...], x_ref[...]
        r = _rstd(xv)
        n = xv * r
        one_sc = 1.0 + sc_ref[...]
        sum_ref[0:1, :] += jnp.sum(dhv, axis=0, keepdims=True)
        sum_ref[1:2, :] += jnp.sum(dhv * (n * g_ref[...]), axis=0, keepdims=True)
        sum_ref[2:3, :] += jnp.sum((dhv * one_sc) * n, axis=0, keepdims=True)
        dn = (dhv * one_sc) * g_ref[...]
        dx_ref[...] = d_ref[...] + r * (dn - n * jnp.mean(dn * n, axis=-1, keepdims=True))

    vecs, vec_specs = _vec_args(g, scale)
    return pl.pallas_call(
        body, name=name, grid=(s // ROW_TILE,),
        in_specs=[_row_spec(D), _row_spec(D), _row_spec(D)] + vec_specs,
        out_specs=[_row_spec(D), _sum_spec(D)],
        out_shape=[jax.ShapeDtypeStruct((s, D), F32), jax.ShapeDtypeStruct((8, D), F32)],
        compiler_params=_params("arbitrary"),
    )(dh, x, dxo, *vecs)


def _post_pre_fwd(x, y, g_post, gate, g_pre, shift, scale, name):
    s = x.shape[0]

    def body(x_ref, y_ref, gp_ref, gate_ref, g_ref, sh_ref, sc_ref, o_ref, h_ref):
        yv = y_ref[...]
        xo = x_ref[...] + gate_ref[...] * ((yv * _rstd(yv)) * gp_ref[...])
        o_ref[...] = xo
        h_ref[...] = (((xo * _rstd(xo)) * g_ref[...]) * (1.0 + sc_ref[...]) + sh_ref[...]).astype(BF16)

    vecs, vec_specs = _vec_args(g_post, gate, g_pre, shift, scale)
    return pl.pallas_call(
        body, name=name, grid=(s // ROW_TILE,),
        in_specs=[_row_spec(D), _row_spec(D)] + vec_specs, out_specs=[_row_spec(D), _row_spec(D)],
        out_shape=[jax.ShapeDtypeStruct((s, D), F32), jax.ShapeDtypeStruct((s, D), BF16)], compiler_params=_params("parallel"),
    )(x, y, *vecs)


def _pre_post_bwd(dh, x, dxo, g_pre, scale, y, g_post, gate, name):
    s = dh.shape[0]

    def body(dh_ref, x_ref, d_ref, y_ref, g_ref, sc_ref, gp_ref, gate_ref, dx_ref, dy_ref, sum_ref):
        @pl.when(pl.program_id(0) == 0)
        def _():
            sum_ref[...] = jnp.zeros_like(sum_ref)

        dhv, xv = dh_ref[...], x_ref[...]
        r = _rstd(xv)
        n = xv * r
        one_sc = 1.0 + sc_ref[...]
        sum_ref[0:1, :] += jnp.sum(dhv, axis=0, keepdims=True)
        sum_ref[1:2, :] += jnp.sum(dhv * (n * g_ref[...]), axis=0, keepdims=True)
        sum_ref[2:3, :] += jnp.sum((dhv * one_sc) * n, axis=0, keepdims=True)
        dn = (dhv * one_sc) * g_ref[...]
        dv = d_ref[...] + r * (dn - n * jnp.mean(dn * n, axis=-1, keepdims=True))
        dx_ref[...] = dv

        yv = y_ref[...]
        ry = _rstd(yv)
        ny = yv * ry
        sum_ref[3:4, :] += jnp.sum(dv * (ny * gp_ref[...]), axis=0, keepdims=True)
        sum_ref[4:5, :] += jnp.sum((dv * gate_ref[...]) * ny, axis=0, keepdims=True)
        dny = (dv * gate_ref[...]) * gp_ref[...]
        dy_ref[...] = (ry * (dny - ny * jnp.mean(dny * ny, axis=-1, keepdims=True))).astype(BF16)

    vecs, vec_specs = _vec_args(g_pre, scale, g_post, gate)
    return pl.pallas_call(
        body, name=name, grid=(s // ROW_TILE,),
        in_specs=[_row_spec(D)] * 4 + vec_specs,
        out_specs=[_row_spec(D), _row_spec(D), _sum_spec(D)],
        out_shape=[jax.ShapeDtypeStruct((s, D), F32), jax.ShapeDtypeStruct((s, D), BF16), jax.ShapeDtypeStruct((8, D), F32)],
        compiler_params=_params("arbitrary"),
    )(dh, x, dxo, y, *vecs)


def _loss_head(y, target):
    s = y.shape[0]

    def body(y_ref, t_ref, dy_ref, sum_ref):
        @pl.when(pl.program_id(0) == 0)
        def _():
            sum_ref[...] = jnp.zeros_like(sum_ref)

        err = y_ref[...] - t_ref[...]
        dy_ref[...] = err * (1.0 / D)
        sum_ref[...] += jnp.sum(err * err)

    return pl.pallas_call(
        body, name="loss_head", grid=(s // ROW_TILE,),
        in_specs=[_row_spec(D), _row_spec(D)],
        out_specs=[_row_spec(D), pl.BlockSpec((8, 128), lambda i: (0, 0))],
        out_shape=[jax.ShapeDtypeStruct((s, D), F32), jax.ShapeDtypeStruct((8, 128), F32)],
        compiler_params=_params("arbitrary"),
    )(y, target)


def _merge_fwd(z, pa, pb, pc):
    s = z.shape[0]

    def body(g0_ref, g1_ref, g2_ref, pa_ref, pb_ref, pc_ref, o_ref):
        o_ref[...] = (jax.nn.sigmoid(g0_ref[...]) * pa_ref[...] + jax.nn.sigmoid(g1_ref[...]) * pb_ref[...]
                      + jax.nn.sigmoid(g2_ref[...]) * pc_ref[...]).astype(BF16)

    return pl.pallas_call(
        body, name="merge_fwd", grid=(s // ROW_TILE,),
        in_specs=[_row_spec(D, 0), _row_spec(D, 1), _row_spec(D, 2), _row_spec(D), _row_spec(D), _row_spec(D)],
        out_specs=_row_spec(D), out_shape=jax.ShapeDtypeStruct((s, D), BF16),
        compiler_params=_params("parallel"),
    )(z, z, z, pa, pb, pc)


def _merge_bwd(dm, z, pa, pb, pc):
    s = z.shape[0]

    def body(dm_ref, g0_ref, g1_ref, g2_ref, pa_ref, pb_ref, pc_ref, dgl_ref, da_ref, db_ref, dc_ref):
        dmv = dm_ref[...]
        for i, (g_ref, p_ref, d_ref) in enumerate(((g0_ref, pa_ref, da_ref), (g1_ref, pb_ref, db_ref), (g2_ref, pc_ref, dc_ref))):
            gate = jax.nn.sigmoid(g_ref[...])
            dgl_ref[:, i * D:(i + 1) * D] = ((dmv * p_ref[...]) * (gate * (1.0 - gate))).astype(BF16)
            d_ref[...] = (dmv * gate).astype(BF16)

    return pl.pallas_call(
        body, name="merge_bwd", grid=(s // ROW_TILE,),
        in_specs=[_row_spec(D), _row_spec(D, 0), _row_spec(D, 1), _row_spec(D, 2), _row_spec(D), _row_spec(D), _row_spec(D)],
        out_specs=[_row_spec(3 * D), _row_spec(D), _row_spec(D), _row_spec(D)],
        out_shape=[jax.ShapeDtypeStruct((s, Z_COLS), BF16)] + [jax.ShapeDtypeStruct((s, D), BF16)] * 3,
        compiler_params=_params("parallel"),
    )(dm, z, z, z, pa, pb, pc)


def _shift_down(v, n):
    row = lax.broadcasted_iota(jnp.int32, v.shape, 0)
    return jnp.where(row >= n, pltpu.roll(v, n, axis=0), 0.0)


def _shift_up(v, n):
    s = v.shape[0]
    row = lax.broadcasted_iota(jnp.int32, v.shape, 0)
    return jnp.where(row < s - n, pltpu.roll(v, s - n, axis=0), 0.0)


def _log_sigmoid(v):
    return jnp.minimum(v, 0.0) - jnp.log1p(jnp.exp(-jnp.abs(v)))


def _cumf_fwd(fl, bias):
    s = fl.shape[0]

    def body(fl_ref, b_ref, o_ref):
        acc = _log_sigmoid(fl_ref[...] + b_ref[...])
        step = 1
        while step < s:
            acc = acc + _shift_down(acc, step)
            step *= 2
        o_ref[...] = acc

    return pl.pallas_call(body, name="cumf_fwd", out_shape=jax.ShapeDtypeStruct((s, 128), F32),
                          compiler_params=pltpu.CompilerParams(vmem_limit_bytes=V7X_VMEM_LIMIT))(fl, bias)


def _cumf_bwd(dcum, fl, bias):
    s = fl.shape[0]

    def body(d_ref, fl_ref, b_ref, dfl_ref, db_ref):
        acc = d_ref[...]
        step = 1
        while step < s:
            acc = acc + _shift_up(acc, step)
            step *= 2
        dfl = acc * jax.nn.sigmoid(-(fl_ref[...] + b_ref[...]))
        dfl_ref[...] = dfl.astype(BF16)
        db_ref[...] = jnp.broadcast_to(jnp.sum(dfl, axis=0, keepdims=True), (8, 128))

    return pl.pallas_call(
        body, name="cumf_bwd",
        out_shape=[jax.ShapeDtypeStruct((s, 128), BF16), jax.ShapeDtypeStruct((8, 128), F32)],
        compiler_params=pltpu.CompilerParams(vmem_limit_bytes=V7X_VMEM_LIMIT))(dcum, fl, bias)


def _pool_windows(v, shift):
    s2 = v + shift(v, 1)
    s4 = s2 + shift(s2, 2)
    s8 = s4 + shift(s4, 4)
    s16 = s8 + shift(s8, 8)
    group = lax.broadcasted_iota(jnp.int32, v.shape, 1) // 64
    return jnp.where(group == 0, s2, jnp.where(group == 1, s4, jnp.where(group == 2, s8, s16)))


def _pool_count(shape):
    group = lax.broadcasted_iota(jnp.int32, shape, 1) // 64
    window = jnp.where(group == 0, 2.0, jnp.where(group == 1, 4.0, jnp.where(group == 2, 8.0, 16.0)))
    t1 = (lax.broadcasted_iota(jnp.int32, shape, 0) + 1).astype(F32)
    return jnp.minimum(t1, window)


def _pc_specs(s):
    zcol = lambda blk: pl.BlockSpec((s, 256), lambda i, blk=blk: (0, blk))
    first = Z_PC // 256
    return [zcol(first), zcol(first + 1), zcol(first + 2), zcol(first + 3),
            pl.BlockSpec((256, 256), lambda i: (0, 0)), pl.BlockSpec((1, 256), lambda i: (0, 0)),
            pl.BlockSpec((3, 256), lambda i: (0, 0))]


def _poolconv_fwd(z, wbd, pscale, convw):
    s = z.shape[0]

    def body(pu_ref, ch_ref, cb_ref, cc_ref, w_ref, ps_ref, cw_ref, yb_ref, yc_ref):
        u = pu_ref[...]
        p = _pool_windows(u, _shift_down) / _pool_count(u.shape) - u
        yb = jnp.dot(p.astype(BF16), w_ref[...].astype(BF16), preferred_element_type=F32) * ps_ref[...]
        yb_ref[...] = yb.astype(BF16)
        uc = cc_ref[...] * ch_ref[...]
        cw = cw_ref[...]
        conv = cw[0:1, :] * _shift_down(uc, 2) + cw[1:2, :] * _shift_down(uc, 1) + cw[2:3, :] * uc
        yc_ref[...] = (cb_ref[...] * conv).astype(BF16)

    out = pl.BlockSpec((s, 256), lambda i: (0, 0))
    return pl.pallas_call(
        body, name="poolconv_fwd", grid=(1,), in_specs=_pc_specs(s), out_specs=[out, out],
        out_shape=[jax.ShapeDtypeStruct((s, 256), BF16)] * 2, compiler_params=_params("arbitrary"),
    )(z, z, z, z, wbd, pscale, convw)


def _poolconv_bwd(dyb, dyc, z, wbd, pscale, convw):
    s = z.shape[0]

    def body(dyb_ref, dyc_ref, pu_ref, ch_ref, cb_ref, cc_ref, w_ref, ps_ref, cw_ref, dz_ref, dw_ref, dps_ref, dcw_ref):
        u = pu_ref[...]
        count = _pool_count(u.shape)
        p = (_pool_windows(u, _shift_down) / count - u).astype(BF16)
        wb = w_ref[...].astype(BF16)
        dyb_v = dyb_ref[...]
        pw = jnp.dot(p, wb, preferred_element_type=F32)
        dps_ref[...] = jnp.broadcast_to(jnp.sum(dyb_v * pw, axis=0, keepdims=True), (8, 256))
        dys = (dyb_v * ps_ref[...]).astype(BF16)
        dp = lax.dot_general(dys, wb, (((1,), (1,)), ((), ())), preferred_element_type=F32)
        dw_ref[...] = lax.dot_general(p, dys, (((0,), (0,)), ((), ())), preferred_element_type=F32)
        dz_ref[:, 0:256] = (_pool_windows(dp / count, _shift_up) - dp).astype(BF16)

        ch, cb, cc = ch_ref[...], cb_ref[...], cc_ref[...]
        uc = cc * ch
        cw = cw_ref[...]
        u2, u1 = _shift_down(uc, 2), _shift_down(uc, 1)
        conv = cw[0:1, :] * u2 + cw[1:2, :] * u1 + cw[2:3, :] * uc
        dyc_v = dyc_ref[...]
        dconv = dyc_v * cb
        du = cw[0:1, :] * _shift_up(dconv, 2) + cw[1:2, :] * _shift_up(dconv, 1) + cw[2:3, :] * dconv
        dz_ref[:, 256:512] = (du * cc).astype(BF16)
        dz_ref[:, 512:768] = (dyc_v * conv).astype(BF16)
        dz_ref[:, 768:1024] = (du * ch).astype(BF16)
        dcw_ref[...] = jnp.zeros_like(dcw_ref)
        dcw_ref[0:1, :] = jnp.sum(dconv * u2, axis=0, keepdims=True)
        dcw_ref[1:2, :] = jnp.sum(dconv * u1, axis=0, keepdims=True)
        dcw_ref[2:3, :] = jnp.sum(dconv * uc, axis=0, keepdims=True)

    blk = lambda r, c: pl.BlockSpec((r, c), lambda i: (0, 0))
    return pl.pallas_call(
        body, name="poolconv_bwd", grid=(1,),
        in_specs=[blk(s, 256), blk(s, 256)] + _pc_specs(s),
        out_specs=[blk(s, 1024), blk(256, 256), blk(8, 256), blk(8, 256)],
        out_shape=[jax.ShapeDtypeStruct((s, 1024), BF16), jax.ShapeDtypeStruct((256, 256), F32),
                   jax.ShapeDtypeStruct((8, 256), F32), jax.ShapeDtypeStruct((8, 256), F32)],
        compiler_params=_params("arbitrary"),
    )(dyb, dyc, z, z, z, z, wbd, pscale, convw)


_NT = (((1,), (1,)), ((), ()))
_TN = (((0,), (0,)), ((), ()))


ATT_Q, ATT_K = 256, 256
ATT_HEADS_BWD = 8
ATT_HEADS = 8


def _att_logits(q, k, fr, q0, k0, masked):
    logits = lax.dot_general(q, k, _NT, preferred_element_type=F32) - fr
    if not masked:
        return logits
    row = q0 + lax.broadcasted_iota(jnp.int32, logits.shape, 0)
    col = k0 + lax.broadcasted_iota(jnp.int32, logits.shape, 1)
    return jnp.where(row >= col, logits, NEG_INF)


def _causal_sweep(step, qi, init):
    n_full = (qi * ATT_Q) // ATT_K
    carry = lax.fori_loop(0, n_full, lambda j, carry: step(j, carry, False), init)
    return step(n_full, carry, True)


HEAD_PAIRS = HEADS // 2


def _lane_pick(v, lane, idx):
    return jnp.sum(jnp.where(lane == idx, v, 0.0), axis=-1, keepdims=True)


def _lane_put(lane, idx, col):
    return jnp.where(lane == idx, col, 0.0)


def _split_heads(v, low):
    zero = jnp.zeros_like(v)
    return jnp.where(low, v, zero), jnp.where(low, zero, v)


def _attn_fwd(qkv, fr):
    s = qkv.shape[0]
    nk = s // ATT_K
    width = ATT_HEADS * HEAD_DIM
    groups = HEADS // ATT_HEADS

    def body(q_ref, k_ref, v_ref, fr_ref, o_ref, lse_ref):
        qi, grp = pl.program_id(0), pl.program_id(1)
        lane = lax.broadcasted_iota(jnp.int32, (ATT_Q, 128), 1)
        low = lane < HEAD_DIM
        qs = []
        for pr in range(ATT_HEADS // 2):
            qs += _split_heads(q_ref[:, 128 * pr:128 * (pr + 1)] * (HEAD_DIM ** -0.5), low)

        def step(j, carry, masked):
            k0 = pl.multiple_of(j * ATT_K, ATT_K)
            out = []
            for h in range(ATT_HEADS):
                cols = slice(128 * (h // 2), 128 * (h // 2 + 1))
                m, l, acc = carry[h]
                logits = _att_logits(qs[h], k_ref[pl.ds(k0, ATT_K), cols], fr_ref[h, pl.ds(j, 1), :], qi * ATT_Q, k0, masked)
                m_new = jnp.maximum(m, jnp.max(logits, axis=-1, keepdims=True))
                p = jnp.exp(logits - m_new)
                alpha = jnp.exp(m - m_new)
                l = alpha * l + jnp.sum(p, axis=-1, keepdims=True)
                acc = alpha * acc + jnp.dot(p.astype(BF16), v_ref[pl.ds(k0, ATT_K), cols], preferred_element_type=F32)
                out.append((m_new, l, acc))
            return tuple(out)

        one = (jnp.full((ATT_Q, 1), NEG_INF, F32), jnp.zeros((ATT_Q, 1), F32), jnp.zeros((ATT_Q, 128), F32))
        done = _causal_sweep(step, qi, (one,) * ATT_HEADS)

        @pl.when(grp == 0)
        def _():
            lse_ref[...] = jnp.zeros_like(lse_ref)

        lse = jnp.zeros((ATT_Q, 128), F32)
        for pr in range(ATT_HEADS // 2):
            (m0, l0, acc0), (m1, l1, acc1) = done[2 * pr], done[2 * pr + 1]
            o_ref[:, 128 * pr:128 * (pr + 1)] = jnp.where(low, acc0 / l0, acc1 / l1)
            head = ATT_HEADS * grp + 2 * pr
            lse = lse + _lane_put(lane, head, m0 + jnp.log(l0)) + _lane_put(lane, head + 1, m1 + jnp.log(l1))
        lse_ref[...] += lse

    return pl.pallas_call(
        body, name="attn_fwd", grid=(s // ATT_Q, groups),
        in_specs=[pl.BlockSpec((ATT_Q, width), lambda i, g: (i, g)),
                  pl.BlockSpec((s, width), lambda i, g: (0, groups + g)),
                  pl.BlockSpec((s, width), lambda i, g: (0, 2 * groups + g)),
                  pl.BlockSpec((ATT_HEADS, nk, ATT_K), lambda i, g: (g, 0, 0))],
        out_specs=[pl.BlockSpec((ATT_Q, width), lambda i, g: (i, g)), pl.BlockSpec((ATT_Q, 128), lambda i, g: (i, 0))],
        out_shape=[jax.ShapeDtypeStruct((s, A_WIDTH), F32), jax.ShapeDtypeStruct((s, 128), F32)],
        compiler_params=_params("parallel", "arbitrary"),
    )(qkv, qkv, qkv, fr)


def _attn_bwd(qkv, do, o, lse, fr):
    s = qkv.shape[0]
    nk = s // ATT_K
    scale = HEAD_DIM ** -0.5
    heads = ATT_HEADS_BWD
    width = heads * HEAD_DIM
    groups = HEADS // heads

    def body(q_ref, k_ref, v_ref, do_ref, o_ref, lse_ref, fr_ref, dq_ref, dk_ref, dv_ref, dfc_ref, dfr_ref, dk_acc, dv_acc):
        grp = pl.program_id(0)
        lane = lax.broadcasted_iota(jnp.int32, (ATT_Q, 128), 1)
        low = lane < HEAD_DIM
        low_t = lax.broadcasted_iota(jnp.int32, (128, ATT_Q), 0) < HEAD_DIM
        dk_acc[...] = jnp.zeros_like(dk_acc)
        dv_acc[...] = jnp.zeros_like(dv_acc)
        dfr_ref[...] = jnp.zeros_like(dfr_ref)

        @pl.when(grp == 0)
        def _():
            dfc_ref[...] = jnp.zeros_like(dfc_ref)

        def outer(i, carry):
            q0 = pl.multiple_of(i * ATT_Q, ATT_Q)
            rows = pl.ds(q0, ATT_Q)
            lsev = lse_ref[rows, :]
            qts, dots, qs, dos, deltas, lses = [], [], [], [], [], []
            for pr in range(heads // 2):
                pcols = slice(128 * pr, 128 * (pr + 1))
                q2, do2 = q_ref[rows, pcols] * scale, do_ref[rows, pcols]
                prod = do2 * o_ref[rows, pcols]
                deltas += [jnp.sum(jnp.where(low, prod, 0.0), axis=-1, keepdims=True),
                           jnp.sum(jnp.where(low, 0.0, prod), axis=-1, keepdims=True)]
                dob2 = do2.astype(BF16)
                qts += _split_heads(q2.astype(F32).T.astype(BF16), low_t)
                dots += _split_heads(do2.T.astype(BF16), low_t)
                qs += _split_heads(q2, low)
                dos += _split_heads(dob2, low)
                lses += [_lane_pick(lsev, lane, heads * grp + 2 * pr), _lane_pick(lsev, lane, heads * grp + 2 * pr + 1)]

            def inner(j, carry, masked):
                k0 = pl.multiple_of(j * ATT_K, ATT_K)
                krows = pl.ds(k0, ATT_K)
                out, dkt, dvt = [], [], []
                for h in range(heads):
                    pcols = slice(128 * (h // 2), 128 * (h // 2 + 1))
                    dq, dfc = carry[h]
                    k2 = k_ref[krows, pcols]
                    p = jnp.exp(_att_logits(qs[h], k2, fr_ref[h, pl.ds(j, 1), :], q0, k0, masked) - lses[h])
                    dp = lax.dot_general(dos[h], v_ref[krows, pcols], _NT, preferred_element_type=F32)
                    ds = p * (dp - deltas[h])
                    dsb = ds.astype(BF16)
                    dkt.append(jnp.dot(qts[h], dsb, preferred_element_type=F32))
                    dvt.append(jnp.dot(dots[h], p.astype(BF16), preferred_element_type=F32))
                    dfr_ref[h, pl.ds(j, 1), :] -= jnp.sum(ds, axis=0, keepdims=True)
                    out.append((dq + jnp.dot(dsb, k2, preferred_element_type=F32), dfc + (ds[:, :128] + ds[:, 128:])))
                for pr in range(heads // 2):
                    prows = slice(128 * pr, 128 * (pr + 1))
                    dk_acc[j, prows, :] += dkt[2 * pr] + dkt[2 * pr + 1]
                    dv_acc[j, prows, :] += dvt[2 * pr] + dvt[2 * pr + 1]
                return tuple(out)

            one = (jnp.zeros((ATT_Q, 128), F32), jnp.zeros((ATT_Q, 128), F32))
            done = _causal_sweep(inner, i, (one,) * heads)
            dfc = jnp.zeros((ATT_Q, 128), F32)
            for pr in range(heads // 2):
                (dq0, dfc0), (dq1, dfc1) = done[2 * pr], done[2 * pr + 1]
                dq_ref[rows, 128 * pr:128 * (pr + 1)] = (jnp.where(low, dq0, dq1) * scale).astype(BF16)
                head = heads * grp + 2 * pr
                dfc = (dfc + _lane_put(lane, head, jnp.sum(dfc0, axis=-1, keepdims=True))
                       + _lane_put(lane, head + 1, jnp.sum(dfc1, axis=-1, keepdims=True)))
            dfc_ref[rows, :] += dfc
            return carry

        lax.fori_loop(0, s // ATT_Q, outer, 0)
        for j in range(nk):
            for pr in range(heads // 2):
                prows, pcols = slice(128 * pr, 128 * (pr + 1)), slice(128 * pr, 128 * (pr + 1))
                dk_ref[ATT_K * j:ATT_K * (j + 1), pcols] = dk_acc[j, prows, :].T.astype(BF16)
                dv_ref[ATT_K * j:ATT_K * (j + 1), pcols] = dv_acc[j, prows, :].T.astype(BF16)

    part = lambda first: pl.BlockSpec((s, width), lambda g, first=first: (0, first + g))
    whole = pl.BlockSpec((s, 128), lambda g: (0, 0))
    rowv = pl.BlockSpec((heads, nk, ATT_K), lambda g: (g, 0, 0))
    return pl.pallas_call(
        body, name="attn_bwd", grid=(groups,),
        in_specs=[part(0), part(groups), part(2 * groups), part(0), part(0), whole, rowv],
        out_specs=[part(0), part(0), part(0), whole, rowv],
        out_shape=[jax.ShapeDtypeStruct((s, A_WIDTH), BF16)] * 3 + [jax.ShapeDtypeStruct((s, 128), F32), jax.ShapeDtypeStruct((HEADS, nk, ATT_K), F32)],
        scratch_shapes=[pltpu.VMEM((nk, width, ATT_K), F32), pltpu.VMEM((nk, width, ATT_K), F32)],
        compiler_params=_params("arbitrary"),
    )(qkv, qkv, qkv, do, o, lse, fr)


def _ada_fwd(c_all, w_ada, b_loc):
    depth, _, n = w_ada.shape
    tn = 512

    def body(c_ref, w_ref, b_ref, o_ref, sc_ref):
        cv = c_ref[...]
        sc = cv * jax.nn.sigmoid(cv)
        sc_ref[...] = sc
        o_ref[0] = jnp.dot(sc.astype(BF16), w_ref[0].astype(BF16), preferred_element_type=F32) + b_ref[0]

    return pl.pallas_call(
        body, name="ada_fwd", grid=(depth, n // tn),
        in_specs=[pl.BlockSpec((N_DEV, D), lambda l, j: (0, 0)), pl.BlockSpec((1, D, tn), lambda l, j: (l, 0, j)),
                  pl.BlockSpec((1, 1, tn), lambda l, j: (l, 0, j))],
        out_specs=[pl.BlockSpec((1, N_DEV, tn), lambda l, j: (l, 0, j)), pl.BlockSpec((N_DEV, D), lambda l, j: (0, 0))],
        out_shape=[jax.ShapeDtypeStruct((depth, N_DEV, n), F32), jax.ShapeDtypeStruct((N_DEV, D), F32)],
        compiler_params=_params("arbitrary", "arbitrary"),
    )(c_all, w_ada, b_loc)


def _sum_devices(gathered):
    n = gathered.shape[1]
    tn = _pick(n, (1408, 1024, 640, 512, 128))

    def body(g_ref, o_ref):
        acc = g_ref[0:8, :]
        for dev in range(1, N_DEV):
            acc = acc + g_ref[8 * dev:8 * dev + 8, :]
        o_ref[...] = acc

    return pl.pallas_call(
        body, name="sum_devices", grid=(n // tn,),
        in_specs=[pl.BlockSpec((8 * N_DEV, tn), lambda j: (0, j))], out_specs=pl.BlockSpec((8, tn), lambda j: (0, j)),
        out_shape=jax.ShapeDtypeStruct((8, n), F32), compiler_params=_params("parallel"),
    )(gathered)


def _place():
    x, y, c = lax.axis_index("x"), lax.axis_index("y"), lax.axis_index("c")
    chips = [(1 - x, y), (x, 1 - y), (1 - x, 1 - y)]
    return x, y, c, chips


def _allgather8(block, name):
    m_per, n = block.shape

    def body(x_ref, out_ref, send_sems, recv_sems, local_sem):
        x, y, c, chips = _place()
        me, sibling = (x, y, c), (x, y, 1 - c)

        def rows(px, py, pc):
            return out_ref.at[pl.ds((4 * px + 2 * py + pc) * m_per, m_per), :]

        def copy(k, blk, to, src=None):
            return pltpu.make_async_remote_copy(
                src_ref=rows(*blk) if src is None else src, dst_ref=rows(*blk),
                send_sem=send_sems.at[k], recv_sem=recv_sems.at[k], device_id=to, device_id_type=MESH)

        mine = pltpu.make_async_copy(x_ref, rows(*me), local_sem)
        mine.start()
        first = [copy(0, me, sibling, src=x_ref)]
        first += [copy(1 + j, me, (*chip, c), src=x_ref) for j, chip in enumerate(chips)]
        for cp in first:
            cp.start()
        passed = [copy(4 + j, (*chip, c), sibling) for j, chip in enumerate(chips)]
        for j, chip in enumerate(chips):
            copy(1 + j, (*chip, c), me).wait_recv()
            passed[j].start()
        copy(0, sibling, me).wait_recv()
        for j, chip in enumerate(chips):
            copy(4 + j, (*chip, 1 - c), me).wait_recv()
        for cp in first + passed:
            cp.wait_send()
        mine.wait()

    return pl.pallas_call(
        body, name=name, out_shape=jax.ShapeDtypeStruct((N_DEV * m_per, n), block.dtype),
        in_specs=[pl.BlockSpec(memory_space=pltpu.VMEM)], out_specs=pl.BlockSpec(memory_space=pltpu.VMEM),
        scratch_shapes=[pltpu.SemaphoreType.DMA((7,)), pltpu.SemaphoreType.DMA((7,)), pltpu.SemaphoreType.DMA],
        compiler_params=pltpu.CompilerParams(vmem_limit_bytes=V7X_VMEM_LIMIT),
    )(block)


_SEM = pl.BlockSpec(memory_space=pltpu.SEMAPHORE)
_DATAFLOW = pltpu.SideEffectType.DATAFLOW_SIDE_EFFECTING


def _plan_copies(plan, refs, send_sems, recv_sems):
    return [pltpu.make_async_remote_copy(src_ref=src, dst_ref=dst, send_sem=send_sems.at[i], recv_sem=recv_sems.at[i],
                                         device_id=to, device_id_type=MESH) for i, (src, dst, to) in enumerate(plan(refs))]


class _Token(NamedTuple):
    after: jax.Array
    tie: jax.Array


def _after_operand(after):
    return after.after if isinstance(after, _Token) else after


def _copies_start(bufs, plan, n_copies, after, name):
    nb = len(bufs)

    def body(*refs):
        for cp in _plan_copies(plan, refs[:nb], refs[nb + 1], refs[nb + 2]):
            cp.start()
        for token in refs[-2:]:
            token[...] = jnp.zeros_like(token)

    sem = pltpu.SemaphoreType.DMA((n_copies,))
    vmem = pl.BlockSpec(memory_space=pltpu.VMEM)
    outs = pl.pallas_call(
        body, name=name,
        out_shape=(sem, sem, *[pltpu.HBM(b.shape, b.dtype) for b in bufs], jax.ShapeDtypeStruct((8, 128), F32),
                   jax.ShapeDtypeStruct((1, 1), F32)),
        in_specs=[_HBM] * nb + [pl.BlockSpec(memory_space=pl.ANY)],
        out_specs=(_SEM, _SEM, *[_HBM] * nb, vmem, vmem),
        input_output_aliases={i: 2 + i for i in range(nb)},
        compiler_params=pltpu.CompilerParams(has_side_effects=_DATAFLOW),
    )(*[pltpu.with_memory_space_constraint(b, pltpu.HBM) for b in bufs], _after_operand(after))
    return outs[0], outs[1], list(outs[2:2 + nb]), _Token(outs[-2], outs[-1])


def _copies_wait(started, plan, after, name):
    send_sems, recv_sems, bufs, _ = started
    nb = len(bufs)

    def body(*refs):
        for cp in _plan_copies(plan, refs[:nb], refs[nb], refs[nb + 1]):
            cp.wait_send()
            cp.wait_recv()

    return list(pl.pallas_call(
        body, name=name, out_shape=tuple(pltpu.HBM(b.shape, b.dtype) for b in bufs),
        in_specs=[_HBM] * nb + [_SEM, _SEM, pl.BlockSpec(memory_space=pl.ANY)], out_specs=tuple([_HBM] * nb),
        input_output_aliases={i: i for i in range(nb)},
        compiler_params=pltpu.CompilerParams(has_side_effects=_DATAFLOW),
    )(*bufs, send_sems, recv_sems, _after_operand(after)))


def _half_rows(ref, axis, c):
    half = ref.shape[axis] // 2
    return pl.ds(c * half, half)


def _plan_gather_ici(refs):
    n = len(refs) // 2
    x, y, c, chips = _place()
    out = []
    for a in range(n):
        rows = _half_rows(refs[a], 0, c)
        out += [(refs[a].at[rows], refs[n + a].at[2 * x + y, rows], (*chip, c)) for chip in chips]
        out.append((refs[a], refs[n + a].at[2 * x + y], (x, y, 1 - c)))
    return out


def _plan_gather_d2d(refs):
    x, y, c, chips = _place()
    out = []
    for ref in refs:
        rows = _half_rows(ref, 1, c)
        for px, py in chips:
            landed = ref.at[2 * px + py, rows]
            out.append((landed, landed, (x, y, 1 - c)))
    return out


def _plan_rs_sibling(refs):
    n = len(refs) // 2
    x, y, c, _ = _place()
    return [(refs[a].at[pl.ds(0, N_CHIPS), _half_rows(refs[a], 1, 1 - c)], refs[n + a], (x, y, 1 - c)) for a in range(n)]


def _plan_rs_chips(refs):
    n = len(refs) // 2
    x, y, c, chips = _place()
    return [(refs[a].at[2 * px + py], refs[n + a].at[k], (px, py, c)) for a in range(n) for k, (px, py) in enumerate(chips)]


def _plan_rs_share(layer):
    def plan(refs):
        x, y, c, _ = _place()
        return [(ref.at[layer, _half_rows(ref, 1, c)], ref.at[layer, _half_rows(ref, 1, c)], (x, y, 1 - c)) for ref in refs]
    return plan


def _chip_sum(g, other, sel, name):
    _, half, cdim = other.shape
    tr = _pick(half, (512, 256, 128, 64))
    per = half // tr

    def body(sel_ref, g_ref, t_ref, wire_ref, own_ref):
        total = g_ref[0] + t_ref[0]
        wire_ref[0] = total.astype(BF16)

        @pl.when(pl.program_id(1) == sel_ref[1])
        def _():
            own_ref[...] = total

    blk = pl.BlockSpec((1, tr, cdim), lambda i, p, sel_ref: (p, i, 0))
    return pl.pallas_call(
        body, name=name,
        grid_spec=pltpu.PrefetchScalarGridSpec(
            num_scalar_prefetch=1, grid=(per, N_CHIPS),
            in_specs=[pl.BlockSpec((1, tr, cdim), lambda i, p, sel_ref: (p, sel_ref[0] * per + i, 0)), blk],
            out_specs=[blk, pl.BlockSpec((tr, cdim), lambda i, p, sel_ref: (i, 0))]),
        out_shape=[jax.ShapeDtypeStruct(other.shape, BF16), jax.ShapeDtypeStruct((half, cdim), F32)],
        compiler_params=_params("parallel", "arbitrary"),
    )(sel, g, other)


def _final_sum(own, recv, sel, layer, into, name):
    half, cdim = own.shape
    tr = _pick(half, (512, 256, 128, 64))
    per = half // tr

    def body(sel_ref, own_ref, r0_ref, r1_ref, r2_ref, *rest):
        rest[-1][...] = ((own_ref[...] + r0_ref[0].astype(F32)) + r1_ref[0].astype(F32)) + r2_ref[0].astype(F32)

    part = lambda k: pl.BlockSpec((1, tr, cdim), lambda i, sel_ref, k=k: (k, i, 0))
    prior = [] if into is None else [into]
    return pl.pallas_call(
        body, name=name,
        grid_spec=pltpu.PrefetchScalarGridSpec(
            num_scalar_prefetch=1, grid=(per,),
            in_specs=[pl.BlockSpec((tr, cdim), lambda i, sel_ref: (i, 0)), part(0), part(1), part(2)]
            + [pl.BlockSpec(memory_space=pl.ANY)] * len(prior),
            out_specs=pl.BlockSpec((None, tr, cdim), lambda i, sel_ref: (layer, sel_ref[0] * per + i, 0))),
        out_shape=jax.ShapeDtypeStruct((DEPTH, 2 * half, cdim), F32),
        input_output_aliases={5: 0} if prior else {}, compiler_params=_params("parallel"),
    )(sel, own, recv, recv, recv, *prior)


def _row(v):
    return v.reshape(1, -1)


_BR_A, _BR_B, _BR_C = (0, A_WIDTH), (A_WIDTH, POOL_WIDTH), (A_WIDTH + POOL_WIDTH, CONV_WIDTH)


def _tie(v, token):
    return v if token is None else v + token.tie


def _no_hook(point, after, ready=None):
    return None


def _layer_fwd(x, w, mod, hook=_no_hook):
    s = x.shape[0]
    mod3 = mod.reshape(6, 1, D)
    h = _modnorm_fwd(x, _row(w["g_mix_pre"]), (mod3, 0), (mod3, 1), "mix_pre_fwd")
    z = _mm(h, w["w_all"], name="mm_in")
    qkv = z[:, Z_QKV:Z_PC].astype(BF16)
    fl = z[:, Z_FL:Z_COLS]
    cum = _cumf_fwd(fl, w["b_f_pad"])
    fr = cum[:, :HEADS].T.reshape(HEADS, s // ATT_K, ATT_K)
    br_a, lse = _attn_fwd(qkv, fr)
    br_b, br_c = _poolconv_fwd(z, w["w_pool_bd"], _tie(_row(w["pool_scale"]), hook("attn", lse)), w["conv_w"])
    hook("pool", br_b)
    wbr = w["w_branch"]
    pa = _mm(br_a, wbr, b_rows=_BR_A, name="mm_br_a")
    pb = _mm(br_b, wbr, b_rows=_BR_B, name="mm_br_b")
    pc = _mm(br_c, wbr, b_rows=_BR_C, name="mm_br_c")
    merged = _merge_fwd(z, pa, pb, pc)
    y = _mm(merged, w["w_out"], name="mm_out")
    x1, h2 = _post_pre_fwd(x, y, _row(w["g_mix_post"]), (mod3, 2), _row(w["g_ff_pre"]), (mod3, 3), (mod3, 4), "mix_post_ff_pre_fwd")
    a, r = _mm(h2, w["w_ff1"], b_split=N_CHIPS, epilogue=_relu2_fwd, out_dtype=(F32, BF16), name="mm_ff1")
    y2 = _mm(r, w["w_ff2"], name="mm_ff2")
    x2 = _post_fwd(x1, y2, _tie(_row(w["g_ff_post"]), hook("ff_post", y2)), (mod3, 5), "ff_post_fwd")
    hook("end", x2)
    saved = dict(x=x, h=h, z=z, qkv=qkv, fl=fl, fr=fr, lse=lse, br_a=br_a, br_b=br_b, br_c=br_c, pa=pa, pb=pb, pc=pc,
                 merged=merged, y=y, x1=x1, h2=h2, a=a, r=r, y2=y2)
    return x2, saved


def _layer_bwd(dx2, sv, w, mod, hook=_no_hook):
    s = dx2.shape[0]
    mod3 = mod.reshape(6, 1, D)
    dy2, sum_ff_post = _post_bwd(dx2, sv["y2"], _row(w["g_ff_post"]), (mod3, 5), "ff_post_bwd")
    (da,) = _mm(dy2, w["w_ff2"], tb=True, epilogue=_relu2_bwd, extras=(sv["a"],), out_dtype=(BF16,), name="mm_ff2_dx")
    d_w_ff2 = _mm(sv["r"], dy2, ta=True, name="mm_ff2_dw")
    dh2 = _mm(da, w["w_ff1"], tb=True, b_split=N_CHIPS, name="mm_ff1_dx")
    d_w_ff1 = _mm(sv["h2"], da, ta=True, out_split=N_CHIPS, name="mm_ff1_dw")
    g_ff_pre = _tie(_row(w["g_ff_pre"]), hook("ff_pre", dh2, dict(w_ff1=d_w_ff1, w_ff2=d_w_ff2)))
    dx1, dy, sum_mid = _pre_post_bwd(dh2, sv["x1"], dx2, g_ff_pre, (mod3, 4), sv["y"], _row(w["g_mix_post"]), (mod3, 2), "ff_pre_mix_post_bwd")
    sum_ff_pre, sum_mix_post = sum_mid, sum_mid[3:]
    dmerged = _mm(dy, w["w_out"], tb=True, name="mm_out_dx")
    d_w_out = _mm(sv["merged"], dy, ta=True, name="mm_out_dw")
    dz, dpa, dpb, dpc = _merge_bwd(dmerged, sv["z"], sv["pa"], sv["pb"], sv["pc"])
    wbr = w["w_branch"]
    dbr_a = _mm(dpa, wbr, tb=True, b_rows=_BR_A, name="mm_br_a_dx")
    dbr_b = _mm(dpb, wbr, tb=True, b_rows=_BR_B, name="mm_br_b_dx")
    dbr_c = _mm(dpc, wbr, tb=True, b_rows=_BR_C, name="mm_br_c_dx")
    d_w_branch = jnp.concatenate([_mm(sv["br_a"], dpa, ta=True, name="mm_br_a_dw"), _mm(sv["br_b"], dpb, ta=True, name="mm_br_b_dw"),
                                  _mm(sv["br_c"], dpc, ta=True, name="mm_br_c_dw")], axis=0)

    dq, dk, dv, dfc, dfr = _attn_bwd(sv["qkv"], dbr_a, sv["br_a"], sv["lse"], sv["fr"])
    dcum = dfc + jnp.pad(dfr.reshape(HEADS, s).T, ((0, 0), (0, 128 - HEADS)))
    dfl, sum_bf = _cumf_bwd(dcum, sv["fl"], _tie(w["b_f_pad"], hook("cumf", dfc)))
    dpc_z, d_wbd, sum_ps, sum_cw = _poolconv_bwd(dbr_b, dbr_c, sv["z"], w["w_pool_bd"], _row(w["pool_scale"]), w["conv_w"])
    for at, part in ((Z_QKV, dq), (Z_QKV + A_WIDTH, dk), (Z_QKV + 2 * A_WIDTH, dv), (Z_PC, dpc_z), (Z_FL, dfl)):
        dz = lax.dynamic_update_slice(dz, part, (0, at))
    dh = _mm(dz, w["w_all"], tb=True, name="mm_in_dx")
    d_w_all = _mm(sv["h"], dz, ta=True, name="mm_in_dw")
    hook("mix_pre", dh)
    dx, sum_mix_pre = _modnorm_bwd(dh, sv["x"], dx1, _row(w["g_mix_pre"]), (mod3, 1), "mix_pre_bwd")

    dmod = jnp.stack([sum_mix_pre[0], sum_mix_pre[1], sum_mix_post[0], sum_ff_pre[0], sum_ff_pre[1], sum_ff_post[0]])
    d_w_in = _w_in_shards(d_w_all)
    d_w_pool = jnp.stack([d_wbd[64 * g:64 * g + 64, 64 * g:64 * g + 64] for g in range(4)])
    big = dict(w_in=d_w_in, w_branch=d_w_branch, w_out=d_w_out, w_ff1=d_w_ff1, w_ff2=d_w_ff2)
    small = dict(g_mix_pre=sum_mix_pre[2], g_mix_post=sum_mix_post[1], g_ff_pre=sum_ff_pre[2], g_ff_post=sum_ff_post[1],
                 b_f=sum_bf[0, :HEADS], w_pool=d_w_pool, pool_scale=sum_ps[0], conv_w=sum_cw[0:3])
    return dx, dmod, big, small


_QKV_END, _FL_END, _PC_END = 3 * A_WIDTH, 3 * A_WIDTH + HEADS, 3 * A_WIDTH + HEADS + POOL_WIDTH + 3 * CONV_WIDTH
_W_IN_GROUPS = ((_PC_END, IN_COLS, Z_GL), (0, _QKV_END, Z_QKV), (_FL_END, _PC_END, Z_PC), (_QKV_END, _FL_END, Z_FL))
_SHARD_COLS = IN_COLS // N_CHIPS


def _w_all_from_shards(blocks):
    pieces = []
    for lo, hi, _ in _W_IN_GROUPS:
        for p in range(N_CHIPS):
            a, b = max(lo, p * _SHARD_COLS), min(hi, (p + 1) * _SHARD_COLS)
            if a < b:
                pieces.append(blocks[p][:, a - p * _SHARD_COLS:b - p * _SHARD_COLS])
    pieces.append(jnp.zeros((D, Z_COLS - IN_COLS), blocks.dtype))
    return jnp.concatenate(pieces, axis=1)


def _w_in_shards(d_w_all):
    blocks = []
    for p in range(N_CHIPS):
        pieces = []
        for lo, hi, at in sorted(_W_IN_GROUPS):
            a, b = max(lo, p * _SHARD_COLS), min(hi, (p + 1) * _SHARD_COLS)
            if a < b:
                pieces.append(d_w_all[:, at + a - lo:at + b - lo])
        blocks.append(jnp.concatenate(pieces, axis=1))
    return jnp.stack(blocks)


def _full_layer_weights(w_in_blocks, w_branch, w_out, w_ff1, w_ff2, g_mix_pre, g_mix_post, g_ff_pre, g_ff_post, b_f, w_pool, pool_scale, conv_w):
    w_all = _w_all_from_shards(w_in_blocks)
    wbd = (w_pool[:, :, None, :] * jnp.eye(4, dtype=F32)[:, None, :, None]).reshape(POOL_WIDTH, POOL_WIDTH)
    return dict(w_all=w_all, w_branch=w_branch, w_out=w_out, w_ff1=w_ff1, w_ff2=w_ff2, g_mix_pre=g_mix_pre, g_mix_post=g_mix_post,
                g_ff_pre=g_ff_pre, g_ff_post=g_ff_post, b_f_pad=jnp.pad(b_f, (0, 128 - HEADS)).reshape(1, 128), w_pool_bd=wbd,
                pool_scale=pool_scale, conv_w=conv_w)


class _NoComm:
    def layer_weights(self, l):
        raise NotImplementedError

    def fwd_hook(self, l):
        return _no_hook

    def bwd_hook(self, l):
        return _no_hook

    def grads_ready(self, l, big):
        return None


class _Layers(_NoComm):
    def __init__(self, layers):
        self.layers = layers

    def layer_weights(self, l):
        return self.layers[l]


def _local_step(x, target, mods, comm):
    saved, weights = [], []
    act = x
    for l in range(DEPTH):
        weights.append(comm.layer_weights(l))
        act, sv = _layer_fwd(act, weights[l], mods[l], comm.fwd_hook(l))
        saved.append(sv)
    dact, sq = _loss_head(act, target)
    loss = sq[0, 0] * (0.5 / D)
    dmods, bigs, smalls = [None] * DEPTH, [None] * DEPTH, [None] * DEPTH
    token = None
    for l in reversed(range(DEPTH)):
        dact, dmods[l], bigs[l], smalls[l] = _layer_bwd(dact, saved[l], weights[l], _tie(mods[l], token), comm.bwd_hook(l))
        token = comm.grads_ready(l, bigs[l])
    return loss, dact, jnp.stack(dmods), bigs, smalls


_BIG = ("w_in", "w_branch", "w_out", "w_ff1", "w_ff2")


class _GatherJob:
    def __init__(self, tag, shards, after):
        self.tag, self.n = tag, len(shards)
        lands = [lax.empty((N_CHIPS,) + s.shape, s.dtype) for s in shards]
        self.state = _copies_start(list(shards) + lands, _plan_gather_ici, 4 * self.n, after, "gather_ici_start_" + tag)
        self.token = self.state[3]

    def pass_on(self, after):
        bufs = _copies_wait(self.state, _plan_gather_ici, after, "gather_ici_wait_" + self.tag)
        self.state = _copies_start(bufs[self.n:], _plan_gather_d2d, 3 * self.n, bufs[0], "gather_d2d_start_" + self.tag)
        self.token = self.state[3]
        return self.token

    def done(self, after):
        return _copies_wait(self.state, _plan_gather_d2d, after, "gather_d2d_wait_" + self.tag)


class _ReduceJob:
    def __init__(self, tag, names, grads, sel, after, layer, into=None):
        self.tag, self.names, self.n, self.sel, self.layer, self.into = tag, names, len(names), sel, layer, into or {}
        lands = [lax.empty((N_CHIPS, g.shape[1] // 2, g.shape[2]), F32) for g in grads]
        self.state = _copies_start(list(grads) + lands, _plan_rs_sibling, self.n, after, "rs_sibling_start_" + tag)
        self.token = self.state[3]

    def chip_sums(self, after):
        bufs = _copies_wait(self.state, _plan_rs_sibling, after, "rs_sibling_wait_" + self.tag)
        wires, self.owns = zip(*[_chip_sum(bufs[i], bufs[self.n + i], self.sel, "rs_chip_sum_" + name) for i, name in enumerate(self.names)])
        lands = [lax.empty((3,) + w.shape[1:], BF16) for w in wires]
        self.state = _copies_start(list(wires) + lands, _plan_rs_chips, 3 * self.n, self.owns[0], "rs_chips_start_" + self.tag)
        self.token = self.state[3]
        return self.token

    def final_sums(self, after):
        bufs = _copies_wait(self.state, _plan_rs_chips, after, "rs_chips_wait_" + self.tag)
        sums = [_final_sum(self.owns[i], bufs[self.n + i], self.sel, self.layer, self.into.get(name), "rs_final_" + name)
                for i, name in enumerate(self.names)]
        self.state = _copies_start(sums, _plan_rs_share(self.layer), self.n, sums[0], "rs_share_start_" + self.tag)
        self.token = self.state[3]
        return self.token

    def done(self, after):
        return dict(zip(self.names, _copies_wait(self.state, _plan_rs_share(self.layer), after, "rs_share_wait_" + self.tag)))


def _chip_blocks(g):
    return g if g.ndim == 3 else g.reshape(N_CHIPS, -1, g.shape[1])


class _StepComm(_NoComm):
    def __init__(self, shards, sel, after):
        self.sel = sel
        self.small, self.grads, self.jobs = None, {}, {}
        self.jobs["in0"] = _GatherJob("in0", shards[0][:1], after)
        self.jobs["rest0"] = _GatherJob("rest0", shards[0][1:], self.jobs["in0"].token)
        self.jobs["all1"] = _GatherJob("all1", shards[1], self.jobs["rest0"].token)

    def layer_weights(self, l):
        if l == 0:
            job = self.jobs["in0"]
            (g_in,) = job.done(job.pass_on(self.jobs["all1"].token))
            self.weights0 = _full_layer_weights(g_in, None, None, None, None, *self.small[0])
            return self.weights0
        g_in, g_br, g_out, g_f1, g_f2 = self.landed1
        return _full_layer_weights(g_in, g_br.reshape(D, D), g_out.reshape(D, D), g_f1, g_f2.reshape(D_FF, D), *self.small[1])

    def fwd_hook(self, l):
        if l != 0:
            return _no_hook

        def hook(point, after, ready=None):
            if point == "attn":
                return self.jobs["rest0"].pass_on(after)
            if point == "ff_post":
                return self.jobs["all1"].pass_on(after)
            if point == "pool":
                g_br, g_out, g_f1, g_f2 = self.jobs["rest0"].done(after)
                self.weights0.update(w_branch=g_br.reshape(D, D), w_out=g_out.reshape(D, D), w_ff1=g_f1, w_ff2=g_f2.reshape(D_FF, D))
            if point == "end":
                self.landed1 = self.jobs["all1"].done(after)
            return None
        return hook

    def bwd_hook(self, l):
        if l != 0:
            return _no_hook

        def hook(point, after, ready=None):
            jobs = self.jobs
            if point == "ff_pre":
                token = jobs["rs1"].chip_sums(after)
                jobs["rs0_ff"] = _ReduceJob("0_ff", ("w_ff1", "w_ff2"), [_chip_blocks(ready[n]) for n in ("w_ff1", "w_ff2")], self.sel, token, 0)
                return jobs["rs0_ff"].token
            if point == "cumf":
                return jobs["rs0_ff"].chip_sums(jobs["rs1"].final_sums(after))
            self.layer1 = jobs["rs1"].done(after)
            jobs["rs0_ff"].into = self.layer1
            return None
        return hook

    def grads_ready(self, l, big):
        if l == 1:
            self.jobs["rs1"] = _ReduceJob("1", _BIG, [_chip_blocks(big[n]) for n in _BIG], self.sel, self.sel, 1)
            return self.jobs["rs1"].token
        names = ("w_in", "w_branch", "w_out")
        self.jobs["rs0_mix"] = _ReduceJob("0_mix", names, [_chip_blocks(big[n]) for n in names], self.sel, self.sel, 0, self.layer1)
        return self.jobs["rs0_mix"].token

    def finish_sums(self, after):
        jobs = self.jobs
        token = jobs["rs0_mix"].chip_sums(after)
        return jobs["rs0_ff"].final_sums(token)

    def finish_ff(self, after):
        self.grads.update(self.jobs["rs0_ff"].done(after))

    def finish_mix(self, after):
        job = self.jobs["rs0_mix"]
        self.grads.update(job.done(job.final_sums(after)))


_SMALL = ("g_mix_pre", "g_mix_post", "g_ff_pre", "g_ff_post", "b_f", "w_pool", "pool_scale", "conv_w")


def _w_in_view(t):
    return t.reshape(DEPTH, D // 128, 128, _SHARD_COLS).transpose(3, 1, 0, 2).reshape(_SHARD_COLS * (D // 128) * DEPTH, 128)


def _w_in_unview(t):
    return t.reshape(_SHARD_COLS, D // 128, DEPTH, 128).transpose(2, 1, 3, 0).reshape(DEPTH, D, _SHARD_COLS)


def _pack(parts, rows=8):
    flat = jnp.concatenate([p.reshape(-1) for p in parts])
    width = -(-flat.shape[0] // (rows * 128)) * 128
    return jnp.pad(flat, (0, rows * width - flat.shape[0])).reshape(rows, width)


def _unpack(packed, like):
    flat = packed.reshape(-1)
    out, at = [], 0
    for ref in like:
        out.append(flat[at:at + ref.size].reshape(ref.shape))
        at += ref.size
    return out


def kernel(x, c, w_ada, b_ada, g_mix_pre, g_mix_post, g_ff_pre, g_ff_post, w_in, b_f, w_pool, pool_scale, conv_w, w_branch, w_out, w_ff1, w_ff2, loss_target, m_w_ada, m_b_ada, m_g_mix_pre, m_g_mix_post, m_g_ff_pre, m_g_ff_post, m_w_in, m_b_f, m_w_pool, m_pool_scale, m_conv_w, m_w_branch, m_w_out, m_w_ff1, m_w_ff2, v_w_ada, v_b_ada, v_g_mix_pre, v_g_mix_post, v_g_ff_pre, v_g_ff_post, v_w_in, v_b_f, v_w_pool, v_pool_scale, v_conv_w, v_w_branch, v_w_out, v_w_ff1, v_w_ff2):
    xi, yi, ci = lax.axis_index("x"), lax.axis_index("y"), lax.axis_index("c")
    chip = 2 * xi + yi
    dev = 2 * chip + ci
    n_ada = w_ada.shape[2]

    first = jnp.zeros((8, D + 384), F32).at[0, :D].set(c[0]).at[0, D:].set(conv_w.reshape(-1))
    got = _allgather8(first, "gather_cond").reshape(N_DEV, 8, D + 384)[:, 0]
    c_all = got[:, :D]
    conv_full = got[0::2, D:].reshape(N_CHIPS, DEPTH, 3, CONV_WIDTH // N_CHIPS).transpose(1, 2, 0, 3).reshape(DEPTH, 3, CONV_WIDTH)

    b_loc = lax.dynamic_slice_in_dim(b_ada, chip * n_ada, n_ada, axis=1).reshape(DEPTH, 1, n_ada)
    mod_cols, silu_c = _ada_fwd(c_all, w_ada, b_loc)
    got = _allgather8(mod_cols.reshape(DEPTH * N_DEV, n_ada), "gather_mod").reshape(N_DEV, DEPTH, N_DEV, n_ada)[0::2]
    mod_all = got.transpose(1, 2, 0, 3).reshape(DEPTH, N_DEV, 6, D)
    mods = lax.dynamic_index_in_dim(mod_all, dev, axis=1, keepdims=False)

    comm = _StepComm([[w[l].astype(BF16) for w in (w_in, w_branch, w_out, w_ff1, w_ff2)] for l in range(DEPTH)],
                     jnp.stack([ci, chip]).astype(jnp.int32), mods)
    comm.small = [(g_mix_pre[l], g_mix_post[l], g_ff_pre[l], g_ff_post[l], b_f[l], w_pool[l], pool_scale[l], conv_full[l]) for l in range(DEPTH)]
    loss_part, grad_x, dmods, bigs, smalls = _local_step(x[0], loss_target[0], mods, comm)

    small_parts = [smalls[l][name] for name in _SMALL for l in range(DEPTH)] + [loss_part.reshape(1)]
    packed = _tie(_pack([dmods] + small_parts), comm.jobs["rs0_mix"].token)
    gathered = _allgather8(packed, "gather_small")
    dmod_all = gathered.reshape(N_DEV, -1)[:, :dmods.size].reshape(N_DEV, DEPTH, 6 * D)
    summed = _unpack(_sum_devices(gathered), [dmods] + small_parts)
    grad_b_ada = summed[0].reshape(DEPTH, 6 * D)
    loss = summed[-1][0]
    small_grads = {name: jnp.stack(summed[1 + 2 * i:3 + 2 * i]) for i, name in enumerate(_SMALL)}
    small_grads["conv_w"] = lax.dynamic_slice_in_dim(small_grads["conv_w"], chip * (CONV_WIDTH // N_CHIPS), CONV_WIDTH // N_CHIPS, axis=2)

    dmod_loc = lax.dynamic_slice_in_dim(dmod_all.transpose(1, 0, 2), chip * n_ada, n_ada, axis=2)
    tail_token = comm.finish_sums(grad_b_ada)
    silu_pad = _tie(jnp.pad(silu_c, ((0, 128 - N_DEV), (0, 0))), tail_token)
    dmod_pad = jnp.pad(dmod_loc.transpose(1, 0, 2).reshape(N_DEV, DEPTH * n_ada), ((0, 128 - N_DEV), (0, 0)))
    grad_w_ada = _mm(silu_pad, dmod_pad, ta=True, out_split=DEPTH, name="mm_ada_dw")

    grads = dict(w_ada=grad_w_ada, b_ada=grad_b_ada, **small_grads)
    weights = dict(w_ada=w_ada, b_ada=b_ada, g_mix_pre=g_mix_pre, g_mix_post=g_mix_post, g_ff_pre=g_ff_pre, g_ff_post=g_ff_post, w_in=w_in,
                   b_f=b_f, w_pool=w_pool, pool_scale=pool_scale, conv_w=conv_w, w_branch=w_branch, w_out=w_out, w_ff1=w_ff1, w_ff2=w_ff2)
    m_in = dict(w_ada=m_w_ada, b_ada=m_b_ada, g_mix_pre=m_g_mix_pre, g_mix_post=m_g_mix_post, g_ff_pre=m_g_ff_pre, g_ff_post=m_g_ff_post,
                w_in=m_w_in, b_f=m_b_f, w_pool=m_w_pool, pool_scale=m_pool_scale, conv_w=m_conv_w, w_branch=m_w_branch, w_out=m_w_out,
                w_ff1=m_w_ff1, w_ff2=m_w_ff2)
    v_in = dict(w_ada=v_w_ada, b_ada=v_b_ada, g_mix_pre=v_g_mix_pre, g_mix_post=v_g_mix_post, g_ff_pre=v_g_ff_pre, g_ff_post=v_g_ff_post,
                w_in=v_w_in, b_f=v_b_f, w_pool=v_w_pool, pool_scale=v_pool_scale, conv_w=v_conv_w, w_branch=v_w_branch, w_out=v_w_out,
                w_ff1=v_w_ff1, w_ff2=v_w_ff2)
    order = ("w_ada", "b_ada", "g_mix_pre", "g_mix_post", "g_ff_pre", "g_ff_post", "w_in", "b_f", "w_pool", "pool_scale", "conv_w",
             "w_branch", "w_out", "w_ff1", "w_ff2")
    delta, new_m, new_v = {}, {}, {}
    tiny = ("b_ada",) + _SMALL
    tiny_g = [_tie(grads[tiny[0]], tail_token)] + [grads[name] for name in tiny[1:]]
    res = _adamw_many([weights[name] for name in tiny], tiny_g, [m_in[name] for name in tiny], [v_in[name] for name in tiny], "adamw_small")
    for out, vals in zip((delta, new_m, new_v), res):
        out.update(zip(tiny, vals))
    delta["w_ada"], new_m["w_ada"], new_v["w_ada"] = _adamw(w_ada, grad_w_ada, m_w_ada, v_w_ada, "adamw_w_ada")
    comm.finish_ff(delta["w_ada"][0, :8, :128] + delta["b_ada"][0, :128])
    for name in ("w_ff1", "w_ff2", "w_in", "w_branch", "w_out"):
        if name == "w_in":
            comm.finish_mix(delta["w_ff2"][0, :8, :128])
        grads[name] = comm.grads[name]
        if name == "w_in":
            g_view = lax.optimization_barrier(_w_in_view(grads[name]))
            res = _adamw(_w_in_view(w_in), g_view, _w_in_view(m_w_in), _w_in_view(v_w_in), "adamw_w_in")
            grads[name], delta[name], new_m[name], new_v[name] = [_w_in_unview(t) for t in (g_view, *res)]
        else:
            delta[name], new_m[name], new_v[name] = _adamw(weights[name], grads[name], m_in[name], v_in[name], "adamw_" + name)

    return (loss, grad_x[None], *[grads[n] for n in order], *[delta[n] for n in order], *[new_m[n] for n in order],
            *[new_v[n] for n in order])
```

```python
from typing import NamedTuple

import jax
import jax.numpy as jnp
from jax import lax
from jax.experimental import pallas as pl
from jax.experimental.pallas import tpu as pltpu

F32 = jnp.float32
BF16 = jnp.bfloat16
MESH = pl.DeviceIdType.MESH

D = 1024
DEPTH = 2
HEADS = 8
HEAD_DIM = 64
A_WIDTH = 512
POOL_WIDTH = 256
CONV_WIDTH = 256
D_FF = 4096
IN_COLS = 5640
Z_GL, Z_QKV, Z_PC, Z_FL, Z_COLS = 0, 3072, 4608, 5632, 5760
RMS_EPS = 1e-6
NEG_INF = -1e30
ROW_TILE = 512
EW_ROWS = 256
N_CHIPS = 4
N_DEV = 8
V7X_VMEM_LIMIT = 48 * 1024 * 1024

ADAM_LR = 0.001
ADAM_B1 = 0.9
ADAM_B2 = 0.999
ADAM_EPS = 1e-08
ADAM_WD = 0.01
ADAM_STEP = 10

_HBM = pl.BlockSpec(memory_space=pltpu.HBM)


def _params(*sem):
    return pltpu.CompilerParams(dimension_semantics=sem, vmem_limit_bytes=V7X_VMEM_LIMIT)


def _pick(dim, cands):
    for cand in cands:
        if dim % cand == 0:
            return cand
    return dim


def _mm(a, b, *, ta=False, tb=False, b_rows=None, b_split=1, out_split=1, out_dtype=F32, epilogue=None, extras=(), name):
    (k, m) = a.shape if ta else a.shape[::-1]
    b_row0, b_rows = (0, b.shape[-2]) if b_rows is None else b_rows
    b_cols = b.shape[-1] * b_split
    (n, k2) = (b_rows, b_cols) if tb else (b_cols, b_rows)
    assert k == k2, (a.shape, b.shape, ta, tb)
    n_unit = n // (out_split * (1 if tb else b_split))
    k_unit = k // (b_split if tb else 1)
    tall = k <= 1024 and not ta and n >= 2048
    tm = _pick(m, (2048, 1024, 512, 256, 128) if tall else (1024, 512, 256, 128))
    tn = _pick(n_unit, (1024, 1152, 768, 640, 512, 256, 128))
    tk = _pick(k_unit, ((2048,) if ta else ()) + (1024, 1152, 512, 640, 256, 128))
    nk = k // tk
    dims = (((0 if ta else 1,), (1 if tb else 0,)), ((), ()))

    def dot(a_ref, b_ref):
        b_val = b_ref[0] if b_split > 1 else b_ref[...]
        return lax.dot_general(a_ref[...].astype(BF16), b_val.astype(BF16), dims, preferred_element_type=F32)

    n_extra = len(extras)
    assert epilogue is None or out_split == 1

    def put(refs, val):
        if epilogue is not None:
            for o_ref, res in zip(refs[n_extra:], epilogue(val, *[r[...] for r in refs[:n_extra]])):
                o_ref[...] = res.astype(o_ref.dtype)
        elif out_split > 1:
            refs[0][0] = val.astype(refs[0].dtype)
        else:
            refs[0][...] = val.astype(refs[0].dtype)

    def body_single(a_ref, b_ref, *refs):
        put(refs, dot(a_ref, b_ref))

    def body_acc(a_ref, b_ref, *refs):
        kk = pl.program_id(2)
        acc_ref = refs[-1]

        @pl.when(kk == 0)
        def _():
            acc_ref[...] = jnp.zeros_like(acc_ref)

        acc_ref[...] += dot(a_ref, b_ref)

        @pl.when(kk == nk - 1)
        def _():
            put(refs[:-1], acc_ref[...])

    a_spec = pl.BlockSpec((tk, tm), lambda i, j, kk: (kk, i)) if ta else pl.BlockSpec((tm, tk), lambda i, j, kk: (i, kk))
    if b_split == 1:
        off = b_row0 // (tn if tb else tk)
        assert off * (tn if tb else tk) == b_row0
        b_spec = pl.BlockSpec((tn, tk), lambda i, j, kk: (j + off, kk)) if tb else pl.BlockSpec((tk, tn), lambda i, j, kk: (kk + off, j))
    elif tb:
        per = k_unit // tk
        b_spec = pl.BlockSpec((1, tn, tk), lambda i, j, kk: (kk // per, j, kk % per))
    else:
        per = n // b_split // tn
        b_spec = pl.BlockSpec((1, tk, tn), lambda i, j, kk: (j // per, kk, j % per))
    if out_split == 1:
        o_spec = pl.BlockSpec((tm, tn), lambda i, j, kk: (i, j))
        o_shape = None if epilogue is not None else jax.ShapeDtypeStruct((m, n), out_dtype)
    else:
        per_o = n // out_split // tn
        o_spec = pl.BlockSpec((1, tm, tn), lambda i, j, kk: (j // per_o, i, j % per_o))
        o_shape = jax.ShapeDtypeStruct((out_split, m, n // out_split), out_dtype)
    if epilogue is not None:
        o_shape = [jax.ShapeDtypeStruct((m, n), dt) for dt in out_dtype]
        o_spec = [o_spec] * len(out_dtype)
    return pl.pallas_call(
        body_single if nk == 1 else body_acc, name=name, grid=(m // tm, n // tn, nk),
        in_specs=[a_spec, b_spec] + [pl.BlockSpec((tm, tn), lambda i, j, kk: (i, j))] * n_extra, out_specs=o_spec, out_shape=o_shape,
        scratch_shapes=[] if nk == 1 else [pltpu.VMEM((tm, tn), F32)],
        compiler_params=_params("parallel", "parallel", "arbitrary"),
    )(a, b, *extras)


def _ew(fn, ins, out_dtypes, name, tc=None):
    shape = ins[0].shape
    lead, (rows, cols) = shape[:-2], shape[-2:]
    tc = cols if tc is None else tc
    if tc > 1024:
        tr = _pick(rows, (EW_ROWS, 128, 8))
    elif tc > 128:
        tr = _pick(rows, (2 * EW_ROWS, EW_ROWS, 128, 8))
    else:
        tr = _pick(rows, (4096, 2256, 2048, 1024, EW_ROWS, 8))
    n_in = len(ins)

    def body(*refs):
        res = fn(*[r[...] for r in refs[:n_in]])
        for o_ref, val in zip(refs[n_in:], res):
            o_ref[...] = val.astype(o_ref.dtype)

    if lead:
        spec = pl.BlockSpec((None, tr, tc), lambda l, i, j: (l, i, j))
    else:
        spec = pl.BlockSpec((tr, tc), lambda i, j: (i, j))
    return pl.pallas_call(
        body, name=name, grid=lead + (rows // tr, cols // tc),
        in_specs=[spec] * n_in, out_specs=[spec] * len(out_dtypes),
        out_shape=[jax.ShapeDtypeStruct(shape, dt) for dt in out_dtypes],
        compiler_params=_params(*(["parallel"] * (len(lead) + 2))),
    )(*ins)


def _relu2_fwd(a):
    r = jnp.maximum(a, 0.0)
    return a, r * r


def _relu2_bwd(dr, a):
    return (dr * (2.0 * jnp.maximum(a, 0.0)),)


def _adamw_math(w, g, m, v):
    m = ADAM_B1 * m + (1.0 - ADAM_B1) * g
    v = ADAM_B2 * v + (1.0 - ADAM_B2) * (g * g)
    m_hat = m / (1.0 - ADAM_B1 ** ADAM_STEP)
    v_hat = v / (1.0 - ADAM_B2 ** ADAM_STEP)
    delta = -ADAM_LR * (m_hat / (jnp.sqrt(v_hat) + ADAM_EPS) + ADAM_WD * w)
    return delta, m, v


def _adamw(w, g, m, v, name):
    return _ew(_adamw_math, [w, g, m, v], [F32, F32, F32], name)


def _adamw_many(ws, gs, ms, vs, name):
    n = len(ws)

    def body(*refs):
        for i in range(n):
            res = _adamw_math(*[refs[k * n + i][...] for k in range(4)])
            for k in range(3):
                refs[(4 + k) * n + i][...] = res[k]

    outs = pl.pallas_call(
        body, name=name, out_shape=[jax.ShapeDtypeStruct(w.shape, F32) for w in ws] * 3,
        compiler_params=pltpu.CompilerParams(vmem_limit_bytes=V7X_VMEM_LIMIT),
    )(*ws, *gs, *ms, *vs)
    return outs[:n], outs[n:2 * n], outs[2 * n:]


def _row_spec(cols, block=0):
    return pl.BlockSpec((ROW_TILE, cols), lambda i, block=block: (i, block))


def _vec_spec(cols):
    return pl.BlockSpec((1, cols), lambda i: (0, 0))


def _vec_args(*vecs):
    arrays = [v[0] if isinstance(v, tuple) else v for v in vecs]
    specs = [pl.BlockSpec((None, 1, D), lambda i, row=v[1]: (row, 0, 0)) if isinstance(v, tuple) else _vec_spec(D) for v in vecs]
    return arrays, specs


def _sum_spec(cols):
    return pl.BlockSpec((8, cols), lambda i: (0, 0))


def _rstd(x):
    return lax.rsqrt(jnp.mean(x * x, axis=-1, keepdims=True) + RMS_EPS)


def _modnorm_fwd(x, g, shift, scale, name):
    s = x.shape[0]

    def body(x_ref, g_ref, sh_ref, sc_ref, h_ref):
        xv = x_ref[...]
        n = xv * _rstd(xv)
        h_ref[...] = ((n * g_ref[...]) * (1.0 + sc_ref[...]) + sh_ref[...]).astype(BF16)

    vecs, vec_specs = _vec_args(g, shift, scale)
    return pl.pallas_call(
        body, name=name, grid=(s // ROW_TILE,),
        in_specs=[_row_spec(D)] + vec_specs, out_specs=_row_spec(D),
        out_shape=jax.ShapeDtypeStruct((s, D), BF16), compiler_params=_params("parallel"),
    )(x, *vecs)


def _post_fwd(x, y, g, gate, name):
    s = x.shape[0]

    def body(x_ref, y_ref, g_ref, gate_ref, o_ref):
        yv = y_ref[...]
        o_ref[...] = x_ref[...] + gate_ref[...] * ((yv * _rstd(yv)) * g_ref[...])

    vecs, vec_specs = _vec_args(g, gate)
    return pl.pallas_call(
        body, name=name, grid=(s // ROW_TILE,),
        in_specs=[_row_spec(D), _row_spec(D)] + vec_specs, out_specs=_row_spec(D),
        out_shape=jax.ShapeDtypeStruct((s, D), F32), compiler_params=_params("parallel"),
    )(x, y, *vecs)


def _post_bwd(dxo, y, g, gate, name):
    s = dxo.shape[0]

    def body(d_ref, y_ref, g_ref, gate_ref, dy_ref, sum_ref):
        @pl.when(pl.program_id(0) == 0)
        def _():
            sum_ref[...] = jnp.zeros_like(sum_ref)

        dv, yv = d_ref[...], y_ref[...]
        r = _rstd(yv)
        n = yv * r
        sum_ref[0:1, :] += jnp.sum(dv * (n * g_ref[...]), axis=0, keepdims=True)
        sum_ref[1:2, :] += jnp.sum((dv * gate_ref[...]) * n, axis=0, keepdims=True)
        dn = (dv * gate_ref[...]) * g_ref[...]
        dy_ref[...] = (r * (dn - n * jnp.mean(dn * n, axis=-1, keepdims=True))).astype(BF16)

    vecs, vec_specs = _vec_args(g, gate)
    return pl.pallas_call(
        body, name=name, grid=(s // ROW_TILE,),
        in_specs=[_row_spec(D), _row_spec(D)] + vec_specs,
        out_specs=[_row_spec(D), _sum_spec(D)],
        out_shape=[jax.ShapeDtypeStruct((s, D), BF16), jax.ShapeDtypeStruct((8, D), F32)],
        compiler_params=_params("arbitrary"),
    )(dxo, y, *vecs)


def _modnorm_bwd(dh, x, dxo, g, scale, name):
    s = dh.shape[0]

    def body(dh_ref, x_ref, d_ref, g_ref, sc_ref, dx_ref, sum_ref):
        @pl.when(pl.program_id(0) == 0)
        def _():
            sum_ref[...] = jnp.zeros_like(sum_ref)

        dhv, xv = dh_ref[...], x_ref[...]
        r = _rstd(xv)
        n = xv * r
        one_sc = 1.0 + sc_ref[...]
        sum_ref[0:1, :] += jnp.sum(dhv, axis=0, keepdims=True)
        sum_ref[1:2, :] += jnp.sum(dhv * (n * g_ref[...]), axis=0, keepdims=True)
        sum_ref[2:3, :] += jnp.sum((dhv * one_sc) * n, axis=0, keepdims=True)
        dn = (dhv * one_sc) * g_ref[...]
        dx_ref[...] = d_ref[...] + r * (dn - n * jnp.mean(dn * n, axis=-1, keepdims=True))

    vecs, vec_specs = _vec_args(g, scale)
    return pl.pallas_call(
        body, name=name, grid=(s // ROW_TILE,),
        in_specs=[_row_spec(D), _row_spec(D), _row_spec(D)] + vec_specs,
        out_specs=[_row_spec(D), _sum_spec(D)],
        out_shape=[jax.ShapeDtypeStruct((s, D), F32), jax.ShapeDtypeStruct((8, D), F32)],
        compiler_params=_params("arbitrary"),
    )(dh, x, dxo, *vecs)


def _post_pre_fwd(x, y, g_post, gate, g_pre, shift, scale, name):
    s = x.shape[0]

    def body(x_ref, y_ref, gp_ref, gate_ref, g_ref, sh_ref, sc_ref, o_ref, h_ref):
        yv = y_ref[...]
        xo = x_ref[...] + gate_ref[...] * ((yv * _rstd(yv)) * gp_ref[...])
        o_ref[...] = xo
        h_ref[...] = (((xo * _rstd(xo)) * g_ref[...]) * (1.0 + sc_ref[...]) + sh_ref[...]).astype(BF16)

    vecs, vec_specs = _vec_args(g_post, gate, g_pre, shift, scale)
    return pl.pallas_call(
        body, name=name, grid=(s // ROW_TILE,),
        in_specs=[_row_spec(D), _row_spec(D)] + vec_specs, out_specs=[_row_spec(D), _row_spec(D)],
        out_shape=[jax.ShapeDtypeStruct((s, D), F32), jax.ShapeDtypeStruct((s, D), BF16)], compiler_params=_params("parallel"),
    )(x, y, *vecs)


def _pre_post_bwd(dh, x, dxo, g_pre, scale, y, g_post, gate, name):
    s = dh.shape[0]

    def body(dh_ref, x_ref, d_ref, y_ref, g_ref, sc_ref, gp_ref, gate_ref, dx_ref, dy_ref, sum_ref):
        @pl.when(pl.program_id(0) == 0)
        def _():
            sum_ref[...] = jnp.zeros_like(sum_ref)

        dhv, xv = dh_ref[...], x_ref[...]
        r = _rstd(xv)
        n = xv * r
        one_sc = 1.0 + sc_ref[...]
        sum_ref[0:1, :] += jnp.sum(dhv, axis=0, keepdims=True)
        sum_ref[1:2, :] += jnp.sum(dhv * (n * g_ref[...]), axis=0, keepdims=True)
        sum_ref[2:3, :] += jnp.sum((dhv * one_sc) * n, axis=0, keepdims=True)
        dn = (dhv * one_sc) * g_ref[...]
        dv = d_ref[...] + r * (dn - n * jnp.mean(dn * n, axis=-1, keepdims=True))
        dx_ref[...] = dv

        yv = y_ref[...]
        ry = _rstd(yv)
        ny = yv * ry
        sum_ref[3:4, :] += jnp.sum(dv * (ny * gp_ref[...]), axis=0, keepdims=True)
        sum_ref[4:5, :] += jnp.sum((dv * gate_ref[...]) * ny, axis=0, keepdims=True)
        dny = (dv * gate_ref[...]) * gp_ref[...]
        dy_ref[...] = (ry * (dny - ny * jnp.mean(dny * ny, axis=-1, keepdims=True))).astype(BF16)

    vecs, vec_specs = _vec_args(g_pre, scale, g_post, gate)
    return pl.pallas_call(
        body, name=name, grid=(s // ROW_TILE,),
        in_specs=[_row_spec(D)] * 4 + vec_specs,
        out_specs=[_row_spec(D), _row_spec(D), _sum_spec(D)],
        out_shape=[jax.ShapeDtypeStruct((s, D), F32), jax.ShapeDtypeStruct((s, D), BF16), jax.ShapeDtypeStruct((8, D), F32)],
        compiler_params=_params("arbitrary"),
    )(dh, x, dxo, y, *vecs)


def _loss_head(y, target):
    s = y.shape[0]

    def body(y_ref, t_ref, dy_ref, sum_ref):
        @pl.when(pl.program_id(0) == 0)
        def _():
            sum_ref[...] = jnp.zeros_like(sum_ref)

        err = y_ref[...] - t_ref[...]
        dy_ref[...] = err * (1.0 / D)
        sum_ref[...] += jnp.sum(err * err)

    return pl.pallas_call(
        body, name="loss_head", grid=(s // ROW_TILE,),
        in_specs=[_row_spec(D), _row_spec(D)],
        out_specs=[_row_spec(D), pl.BlockSpec((8, 128), lambda i: (0, 0))],
        out_shape=[jax.ShapeDtypeStruct((s, D), F32), jax.ShapeDtypeStruct((8, 128), F32)],
        compiler_params=_params("arbitrary"),
    )(y, target)


def _merge_fwd(z, pa, pb, pc):
    s = z.shape[0]

    def body(g0_ref, g1_ref, g2_ref, pa_ref, pb_ref, pc_ref, o_ref):
        o_ref[...] = (jax.nn.sigmoid(g0_ref[...]) * pa_ref[...] + jax.nn.sigmoid(g1_ref[...]) * pb_ref[...]
                      + jax.nn.sigmoid(g2_ref[...]) * pc_ref[...]).astype(BF16)

    return pl.pallas_call(
        body, name="merge_fwd", grid=(s // ROW_TILE,),
        in_specs=[_row_spec(D, 0), _row_spec(D, 1), _row_spec(D, 2), _row_spec(D), _row_spec(D), _row_spec(D)],
        out_specs=_row_spec(D), out_shape=jax.ShapeDtypeStruct((s, D), BF16),
        compiler_params=_params("parallel"),
    )(z, z, z, pa, pb, pc)


def _merge_bwd(dm, z, pa, pb, pc):
    s = z.shape[0]

    def body(dm_ref, g0_ref, g1_ref, g2_ref, pa_ref, pb_ref, pc_ref, dgl_ref, da_ref, db_ref, dc_ref):
        dmv = dm_ref[...]
        for i, (g_ref, p_ref, d_ref) in enumerate(((g0_ref, pa_ref, da_ref), (g1_ref, pb_ref, db_ref), (g2_ref, pc_ref, dc_ref))):
            gate = jax.nn.sigmoid(g_ref[...])
            dgl_ref[:, i * D:(i + 1) * D] = ((dmv * p_ref[...]) * (gate * (1.0 - gate))).astype(BF16)
            d_ref[...] = (dmv * gate).astype(BF16)

    return pl.pallas_call(
        body, name="merge_bwd", grid=(s // ROW_TILE,),
        in_specs=[_row_spec(D), _row_spec(D, 0), _row_spec(D, 1), _row_spec(D, 2), _row_spec(D), _row_spec(D), _row_spec(D)],
        out_specs=[_row_spec(3 * D), _row_spec(D), _row_spec(D), _row_spec(D)],
        out_shape=[jax.ShapeDtypeStruct((s, Z_COLS), BF16)] + [jax.ShapeDtypeStruct((s, D), BF16)] * 3,
        compiler_params=_params("parallel"),
    )(dm, z, z, z, pa, pb, pc)


def _shift_down(v, n):
    row = lax.broadcasted_iota(jnp.int32, v.shape, 0)
    return jnp.where(row >= n, pltpu.roll(v, n, axis=0), 0.0)


def _shift_up(v, n):
    s = v.shape[0]
    row = lax.broadcasted_iota(jnp.int32, v.shape, 0)
    return jnp.where(row < s - n, pltpu.roll(v, s - n, axis=0), 0.0)


def _log_sigmoid(v):
    return jnp.minimum(v, 0.0) - jnp.log1p(jnp.exp(-jnp.abs(v)))


def _cumf_fwd(fl, bias):
    s = fl.shape[0]

    def body(fl_ref, b_ref, o_ref):
        acc = _log_sigmoid(fl_ref[...] + b_ref[...])
        step = 1
        while step < s:
            acc = acc + _shift_down(acc, step)
            step *= 2
        o_ref[...] = acc

    return pl.pallas_call(body, name="cumf_fwd", out_shape=jax.ShapeDtypeStruct((s, 128), F32),
                          compiler_params=pltpu.CompilerParams(vmem_limit_bytes=V7X_VMEM_LIMIT))(fl, bias)


def _cumf_bwd(dcum, fl, bias):
    s = fl.shape[0]

    def body(d_ref, fl_ref, b_ref, dfl_ref, db_ref):
        acc = d_ref[...]
        step = 1
        while step < s:
            acc = acc + _shift_up(acc, step)
            step *= 2
        dfl = acc * jax.nn.sigmoid(-(fl_ref[...] + b_ref[...]))
        dfl_ref[...] = dfl.astype(BF16)
        db_ref[...] = jnp.broadcast_to(jnp.sum(dfl, axis=0, keepdims=True), (8, 128))

    return pl.pallas_call(
        body, name="cumf_bwd",
        out_shape=[jax.ShapeDtypeStruct((s, 128), BF16), jax.ShapeDtypeStruct((8, 128), F32)],
        compiler_params=pltpu.CompilerParams(vmem_limit_bytes=V7X_VMEM_LIMIT))(dcum, fl, bias)


def _pool_windows(v, shift):
    s2 = v + shift(v, 1)
    s4 = s2 + shift(s2, 2)
    s8 = s4 + shift(s4, 4)
    s16 = s8 + shift(s8, 8)
    group = lax.broadcasted_iota(jnp.int32, v.shape, 1) // 64
    return jnp.where(group == 0, s2, jnp.where(group == 1, s4, jnp.where(group == 2, s8, s16)))


def _pool_count(shape):
    group = lax.broadcasted_iota(jnp.int32, shape, 1) // 64
    window = jnp.where(group == 0, 2.0, jnp.where(group == 1, 4.0, jnp.where(group == 2, 8.0, 16.0)))
    t1 = (lax.broadcasted_iota(jnp.int32, shape, 0) + 1).astype(F32)
    return jnp.minimum(t1, window)


def _pc_specs(s):
    zcol = lambda blk: pl.BlockSpec((s, 256), lambda i, blk=blk: (0, blk))
    first = Z_PC // 256
    return [zcol(first), zcol(first + 1), zcol(first + 2), zcol(first + 3),
            pl.BlockSpec((256, 256), lambda i: (0, 0)), pl.BlockSpec((1, 256), lambda i: (0, 0)),
            pl.BlockSpec((3, 256), lambda i: (0, 0))]


def _poolconv_fwd(z, wbd, pscale, convw):
    s = z.shape[0]

    def body(pu_ref, ch_ref, cb_ref, cc_ref, w_ref, ps_ref, cw_ref, yb_ref, yc_ref):
        u = pu_ref[...]
        p = _pool_windows(u, _shift_down) / _pool_count(u.shape) - u
        yb = jnp.dot(p.astype(BF16), w_ref[...].astype(BF16), preferred_element_type=F32) * ps_ref[...]
        yb_ref[...] = yb.astype(BF16)
        uc = cc_ref[...] * ch_ref[...]
        cw = cw_ref[...]
        conv = cw[0:1, :] * _shift_down(uc, 2) + cw[1:2, :] * _shift_down(uc, 1) + cw[2:3, :] * uc
        yc_ref[...] = (cb_ref[...] * conv).astype(BF16)

    out = pl.BlockSpec((s, 256), lambda i: (0, 0))
    return pl.pallas_call(
        body, name="poolconv_fwd", grid=(1,), in_specs=_pc_specs(s), out_specs=[out, out],
        out_shape=[jax.ShapeDtypeStruct((s, 256), BF16)] * 2, compiler_params=_params("arbitrary"),
    )(z, z, z, z, wbd, pscale, convw)


def _poolconv_bwd(dyb, dyc, z, wbd, pscale, convw):
    s = z.shape[0]

    def body(dyb_ref, dyc_ref, pu_ref, ch_ref, cb_ref, cc_ref, w_ref, ps_ref, cw_ref, dz_ref, dw_ref, dps_ref, dcw_ref):
        u = pu_ref[...]
        count = _pool_count(u.shape)
        p = (_pool_windows(u, _shift_down) / count - u).astype(BF16)
        wb = w_ref[...].astype(BF16)
        dyb_v = dyb_ref[...]
        pw = jnp.dot(p, wb, preferred_element_type=F32)
        dps_ref[...] = jnp.broadcast_to(jnp.sum(dyb_v * pw, axis=0, keepdims=True), (8, 256))
        dys = (dyb_v * ps_ref[...]).astype(BF16)
        dp = lax.dot_general(dys, wb, (((1,), (1,)), ((), ())), preferred_element_type=F32)
        dw_ref[...] = lax.dot_general(p, dys, (((0,), (0,)), ((), ())), preferred_element_type=F32)
        dz_ref[:, 0:256] = (_pool_windows(dp / count, _shift_up) - dp).astype(BF16)

        ch, cb, cc = ch_ref[...], cb_ref[...], cc_ref[...]
        uc = cc * ch
        cw = cw_ref[...]
        u2, u1 = _shift_down(uc, 2), _shift_down(uc, 1)
        conv = cw[0:1, :] * u2 + cw[1:2, :] * u1 + cw[2:3, :] * uc
        dyc_v = dyc_ref[...]
        dconv = dyc_v * cb
        du = cw[0:1, :] * _shift_up(dconv, 2) + cw[1:2, :] * _shift_up(dconv, 1) + cw[2:3, :] * dconv
        dz_ref[:, 256:512] = (du * cc).astype(BF16)
        dz_ref[:, 512:768] = (dyc_v * conv).astype(BF16)
        dz_ref[:, 768:1024] = (du * ch).astype(BF16)
        dcw_ref[...] = jnp.zeros_like(dcw_ref)
        dcw_ref[0:1, :] = jnp.sum(dconv * u2, axis=0, keepdims=True)
        dcw_ref[1:2, :] = jnp.sum(dconv * u1, axis=0, keepdims=True)
        dcw_ref[2:3, :] = jnp.sum(dconv * uc, axis=0, keepdims=True)

    blk = lambda r, c: pl.BlockSpec((r, c), lambda i: (0, 0))
    return pl.pallas_call(
        body, name="poolconv_bwd", grid=(1,),
        in_specs=[blk(s, 256), blk(s, 256)] + _pc_specs(s),
        out_specs=[blk(s, 1024), blk(256, 256), blk(8, 256), blk(8, 256)],
        out_shape=[jax.ShapeDtypeStruct((s, 1024), BF16), jax.ShapeDtypeStruct((256, 256), F32),
                   jax.ShapeDtypeStruct((8, 256), F32), jax.ShapeDtypeStruct((8, 256), F32)],
        compiler_params=_params("arbitrary"),
    )(dyb, dyc, z, z, z, z, wbd, pscale, convw)


_NT = (((1,), (1,)), ((), ()))
_TN = (((0,), (0,)), ((), ()))


ATT_Q, ATT_K = 256, 256
ATT_HEADS_BWD = 8
ATT_HEADS = 8


def _att_logits(q, k, fr, q0, k0, masked):
    logits = lax.dot_general(q, k, _NT, preferred_element_type=F32) - fr
    if not masked:
        return logits
    row = q0 + lax.broadcasted_iota(jnp.int32, logits.shape, 0)
    col = k0 + lax.broadcasted_iota(jnp.int32, logits.shape, 1)
    return jnp.where(row >= col, logits, NEG_INF)


def _causal_sweep(step, qi, init):
    n_full = (qi * ATT_Q) // ATT_K
    carry = lax.fori_loop(0, n_full, lambda j, carry: step(j, carry, False), init)
    return step(n_full, carry, True)


HEAD_PAIRS = HEADS // 2


def _lane_pick(v, lane, idx):
    return jnp.sum(jnp.where(lane == idx, v, 0.0), axis=-1, keepdims=True)


def _lane_put(lane, idx, col):
    return jnp.where(lane == idx, col, 0.0)


def _split_heads(v, low):
    zero = jnp.zeros_like(v)
    return jnp.where(low, v, zero), jnp.where(low, zero, v)


def _attn_fwd(qkv, fr):
    s = qkv.shape[0]
    nk = s // ATT_K
    width = ATT_HEADS * HEAD_DIM
    groups = HEADS // ATT_HEADS

    def body(q_ref, k_ref, v_ref, fr_ref, o_ref, lse_ref):
        qi, grp = pl.program_id(0), pl.program_id(1)
        lane = lax.broadcasted_iota(jnp.int32, (ATT_Q, 128), 1)
        low = lane < HEAD_DIM
        qs = []
        for pr in range(ATT_HEADS // 2):
            qs += _split_heads(q_ref[:, 128 * pr:128 * (pr + 1)] * (HEAD_DIM ** -0.5), low)

        def step(j, carry, masked):
            k0 = pl.multiple_of(j * ATT_K, ATT_K)
            out = []
            for h in range(ATT_HEADS):
                cols = slice(128 * (h // 2), 128 * (h // 2 + 1))
                m, l, acc = carry[h]
                logits = _att_logits(qs[h], k_ref[pl.ds(k0, ATT_K), cols], fr_ref[h, pl.ds(j, 1), :], qi * ATT_Q, k0, masked)
                m_new = jnp.maximum(m, jnp.max(logits, axis=-1, keepdims=True))
                p = jnp.exp(logits - m_new)
                alpha = jnp.exp(m - m_new)
                l = alpha * l + jnp.sum(p, axis=-1, keepdims=True)
                acc = alpha * acc + jnp.dot(p.astype(BF16), v_ref[pl.ds(k0, ATT_K), cols], preferred_element_type=F32)
                out.append((m_new, l, acc))
            return tuple(out)

        one = (jnp.full((ATT_Q, 1), NEG_INF, F32), jnp.zeros((ATT_Q, 1), F32), jnp.zeros((ATT_Q, 128), F32))
        done = _causal_sweep(step, qi, (one,) * ATT_HEADS)

        @pl.when(grp == 0)
        def _():
            lse_ref[...] = jnp.zeros_like(lse_ref)

        lse = jnp.zeros((ATT_Q, 128), F32)
        for pr in range(ATT_HEADS // 2):
            (m0, l0, acc0), (m1, l1, acc1) = done[2 * pr], done[2 * pr + 1]
            o_ref[:, 128 * pr:128 * (pr + 1)] = jnp.where(low, acc0 / l0, acc1 / l1)
            head = ATT_HEADS * grp + 2 * pr
            lse = lse + _lane_put(lane, head, m0 + jnp.log(l0)) + _lane_put(lane, head + 1, m1 + jnp.log(l1))
        lse_ref[...] += lse

    return pl.pallas_call(
        body, name="attn_fwd", grid=(s // ATT_Q, groups),
        in_specs=[pl.BlockSpec((ATT_Q, width), lambda i, g: (i, g)),
                  pl.BlockSpec((s, width), lambda i, g: (0, groups + g)),
                  pl.BlockSpec((s, width), lambda i, g: (0, 2 * groups + g)),
                  pl.BlockSpec((ATT_HEADS, nk, ATT_K), lambda i, g: (g, 0, 0))],
        out_specs=[pl.BlockSpec((ATT_Q, width), lambda i, g: (i, g)), pl.BlockSpec((ATT_Q, 128), lambda i, g: (i, 0))],
        out_shape=[jax.ShapeDtypeStruct((s, A_WIDTH), F32), jax.ShapeDtypeStruct((s, 128), F32)],
        compiler_params=_params("parallel", "arbitrary"),
    )(qkv, qkv, qkv, fr)


def _attn_bwd(qkv, do, o, lse, fr):
    s = qkv.shape[0]
    nk = s // ATT_K
    scale = HEAD_DIM ** -0.5
    heads = ATT_HEADS_BWD
    width = heads * HEAD_DIM
    groups = HEADS // heads

    def body(q_ref, k_ref, v_ref, do_ref, o_ref, lse_ref, fr_ref, dq_ref, dk_ref, dv_ref, dfc_ref, dfr_ref, dk_acc, dv_acc):
        grp = pl.program_id(0)
        lane = lax.broadcasted_iota(jnp.int32, (ATT_Q, 128), 1)
        low = lane < HEAD_DIM
        low_t = lax.broadcasted_iota(jnp.int32, (128, ATT_Q), 0) < HEAD_DIM
        dk_acc[...] = jnp.zeros_like(dk_acc)
        dv_acc[...] = jnp.zeros_like(dv_acc)
        dfr_ref[...] = jnp.zeros_like(dfr_ref)

        @pl.when(grp == 0)
        def _():
            dfc_ref[...] = jnp.zeros_like(dfc_ref)

        def outer(i, carry):
            q0 = pl.multiple_of(i * ATT_Q, ATT_Q)
            rows = pl.ds(q0, ATT_Q)
            lsev = lse_ref[rows, :]
            qts, dots, qs, dos, deltas, lses = [], [], [], [], [], []
            for pr in range(heads // 2):
                pcols = slice(128 * pr, 128 * (pr + 1))
                q2, do2 = q_ref[rows, pcols] * scale, do_ref[rows, pcols]
                prod = do2 * o_ref[rows, pcols]
                deltas += [jnp.sum(jnp.where(low, prod, 0.0), axis=-1, keepdims=True),
                           jnp.sum(jnp.where(low, 0.0, prod), axis=-1, keepdims=True)]
                dob2 = do2.astype(BF16)
                qts += _split_heads(q2.astype(F32).T.astype(BF16), low_t)
                dots += _split_heads(do2.T.astype(BF16), low_t)
                qs += _split_heads(q2, low)
                dos += _split_heads(dob2, low)
                lses += [_lane_pick(lsev, lane, heads * grp + 2 * pr), _lane_pick(lsev, lane, heads * grp + 2 * pr + 1)]

            def inner(j, carry, masked):
                k0 = pl.multiple_of(j * ATT_K, ATT_K)
                krows = pl.ds(k0, ATT_K)
                out, dkt, dvt = [], [], []
                for h in range(heads):
                    pcols = slice(128 * (h // 2), 128 * (h // 2 + 1))
                    dq, dfc = carry[h]
                    k2 = k_ref[krows, pcols]
                    p = jnp.exp(_att_logits(qs[h], k2, fr_ref[h, pl.ds(j, 1), :], q0, k0, masked) - lses[h])
                    dp = lax.dot_general(dos[h], v_ref[krows, pcols], _NT, preferred_element_type=F32)
                    ds = p * (dp - deltas[h])
                    dsb = ds.astype(BF16)
                    dkt.append(jnp.dot(qts[h], dsb, preferred_element_type=F32))
                    dvt.append(jnp.dot(dots[h], p.astype(BF16), preferred_element_type=F32))
                    dfr_ref[h, pl.ds(j, 1), :] -= jnp.sum(ds, axis=0, keepdims=True)
                    out.append((dq + jnp.dot(dsb, k2, preferred_element_type=F32), dfc + (ds[:, :128] + ds[:, 128:])))
                for pr in range(heads // 2):
                    prows = slice(128 * pr, 128 * (pr + 1))
                    dk_acc[j, prows, :] += dkt[2 * pr] + dkt[2 * pr + 1]
                    dv_acc[j, prows, :] += dvt[2 * pr] + dvt[2 * pr + 1]
                return tuple(out)

            one = (jnp.zeros((ATT_Q, 128), F32), jnp.zeros((ATT_Q, 128), F32))
            done = _causal_sweep(inner, i, (one,) * heads)
            dfc = jnp.zeros((ATT_Q, 128), F32)
            for pr in range(heads // 2):
                (dq0, dfc0), (dq1, dfc1) = done[2 * pr], done[2 * pr + 1]
                dq_ref[rows, 128 * pr:128 * (pr + 1)] = (jnp.where(low, dq0, dq1) * scale).astype(BF16)
                head = heads * grp + 2 * pr
                dfc = (dfc + _lane_put(lane, head, jnp.sum(dfc0, axis=-1, keepdims=True))
                       + _lane_put(lane, head + 1, jnp.sum(dfc1, axis=-1, keepdims=True)))
            dfc_ref[rows, :] += dfc
            return carry

        lax.fori_loop(0, s // ATT_Q, outer, 0)
        for j in range(nk):
            for pr in range(heads // 2):
                prows, pcols = slice(128 * pr, 128 * (pr + 1)), slice(128 * pr, 128 * (pr + 1))
                dk_ref[ATT_K * j:ATT_K * (j + 1), pcols] = dk_acc[j, prows, :].T.astype(BF16)
                dv_ref[ATT_K * j:ATT_K * (j + 1), pcols] = dv_acc[j, prows, :].T.astype(BF16)

    part = lambda first: pl.BlockSpec((s, width), lambda g, first=first: (0, first + g))
    whole = pl.BlockSpec((s, 128), lambda g: (0, 0))
    rowv = pl.BlockSpec((heads, nk, ATT_K), lambda g: (g, 0, 0))
    return pl.pallas_call(
        body, name="attn_bwd", grid=(groups,),
        in_specs=[part(0), part(groups), part(2 * groups), part(0), part(0), whole, rowv],
        out_specs=[part(0), part(0), part(0), whole, rowv],
        out_shape=[jax.ShapeDtypeStruct((s, A_WIDTH), BF16)] * 3 + [jax.ShapeDtypeStruct((s, 128), F32), jax.ShapeDtypeStruct((HEADS, nk, ATT_K), F32)],
        scratch_shapes=[pltpu.VMEM((nk, width, ATT_K), F32), pltpu.VMEM((nk, width, ATT_K), F32)],
        compiler_params=_params("arbitrary"),
    )(qkv, qkv, qkv, do, o, lse, fr)


def _ada_fwd(c_all, w_ada, b_loc):
    depth, _, n = w_ada.shape
    tn = 512

    def body(c_ref, w_ref, b_ref, o_ref, sc_ref):
        cv = c_ref[...]
        sc = cv * jax.nn.sigmoid(cv)
        sc_ref[...] = sc
        o_ref[0] = jnp.dot(sc.astype(BF16), w_ref[0].astype(BF16), preferred_element_type=F32) + b_ref[0]

    return pl.pallas_call(
        body, name="ada_fwd", grid=(depth, n // tn),
        in_specs=[pl.BlockSpec((N_DEV, D), lambda l, j: (0, 0)), pl.BlockSpec((1, D, tn), lambda l, j: (l, 0, j)),
                  pl.BlockSpec((1, 1, tn), lambda l, j: (l, 0, j))],
        out_specs=[pl.BlockSpec((1, N_DEV, tn), lambda l, j: (l, 0, j)), pl.BlockSpec((N_DEV, D), lambda l, j: (0, 0))],
        out_shape=[jax.ShapeDtypeStruct((depth, N_DEV, n), F32), jax.ShapeDtypeStruct((N_DEV, D), F32)],
        compiler_params=_params("arbitrary", "arbitrary"),
    )(c_all, w_ada, b_loc)


def _sum_devices(gathered):
    n = gathered.shape[1]
    tn = _pick(n, (1408, 1024, 640, 512, 128))

    def body(g_ref, o_ref):
        acc = g_ref[0:8, :]
        for dev in range(1, N_DEV):
            acc = acc + g_ref[8 * dev:8 * dev + 8, :]
        o_ref[...] = acc

    return pl.pallas_call(
        body, name="sum_devices", grid=(n // tn,),
        in_specs=[pl.BlockSpec((8 * N_DEV, tn), lambda j: (0, j))], out_specs=pl.BlockSpec((8, tn), lambda j: (0, j)),
        out_shape=jax.ShapeDtypeStruct((8, n), F32), compiler_params=_params("parallel"),
    )(gathered)


def _place():
    x, y, c = lax.axis_index("x"), lax.axis_index("y"), lax.axis_index("c")
    chips = [(1 - x, y), (x, 1 - y), (1 - x, 1 - y)]
    return x, y, c, chips


def _allgather8(block, name):
    m_per, n = block.shape

    def body(x_ref, out_ref, send_sems, recv_sems, local_sem):
        x, y, c, chips = _place()
        me, sibling = (x, y, c), (x, y, 1 - c)

        def rows(px, py, pc):
            return out_ref.at[pl.ds((4 * px + 2 * py + pc) * m_per, m_per), :]

        def copy(k, blk, to, src=None):
            return pltpu.make_async_remote_copy(
                src_ref=rows(*blk) if src is None else src, dst_ref=rows(*blk),
                send_sem=send_sems.at[k], recv_sem=recv_sems.at[k], device_id=to, device_id_type=MESH)

        mine = pltpu.make_async_copy(x_ref, rows(*me), local_sem)
        mine.start()
        first = [copy(0, me, sibling, src=x_ref)]
        first += [copy(1 + j, me, (*chip, c), src=x_ref) for j, chip in enumerate(chips)]
        for cp in first:
            cp.start()
        passed = [copy(4 + j, (*chip, c), sibling) for j, chip in enumerate(chips)]
        for j, chip in enumerate(chips):
            copy(1 + j, (*chip, c), me).wait_recv()
            passed[j].start()
        copy(0, sibling, me).wait_recv()
        for j, chip in enumerate(chips):
            copy(4 + j, (*chip, 1 - c), me).wait_recv()
        for cp in first + passed:
            cp.wait_send()
        mine.wait()

    return pl.pallas_call(
        body, name=name, out_shape=jax.ShapeDtypeStruct((N_DEV * m_per, n), block.dtype),
        in_specs=[pl.BlockSpec(memory_space=pltpu.VMEM)], out_specs=pl.BlockSpec(memory_space=pltpu.VMEM),
        scratch_shapes=[pltpu.SemaphoreType.DMA((7,)), pltpu.SemaphoreType.DMA((7,)), pltpu.SemaphoreType.DMA],
        compiler_params=pltpu.CompilerParams(vmem_limit_bytes=V7X_VMEM_LIMIT),
    )(block)


_SEM = pl.BlockSpec(memory_space=pltpu.SEMAPHORE)
_DATAFLOW = pltpu.SideEffectType.DATAFLOW_SIDE_EFFECTING


def _plan_copies(plan, refs, send_sems, recv_sems):
    return [pltpu.make_async_remote_copy(src_ref=src, dst_ref=dst, send_sem=send_sems.at[i], recv_sem=recv_sems.at[i],
                                         device_id=to, device_id_type=MESH) for i, (src, dst, to) in enumerate(plan(refs))]


class _Token(NamedTuple):
    after: jax.Array
    tie: jax.Array


def _after_operand(after):
    return after.after if isinstance(after, _Token) else after


def _copies_start(bufs, plan, n_copies, after, name):
    nb = len(bufs)

    def body(*refs):
        for cp in _plan_copies(plan, refs[:nb], refs[nb + 1], refs[nb + 2]):
            cp.start()
        for token in refs[-2:]:
            token[...] = jnp.zeros_like(token)

    sem = pltpu.SemaphoreType.DMA((n_copies,))
    vmem = pl.BlockSpec(memory_space=pltpu.VMEM)
    outs = pl.pallas_call(
        body, name=name,
        out_shape=(sem, sem, *[pltpu.HBM(b.shape, b.dtype) for b in bufs], jax.ShapeDtypeStruct((8, 128), F32),
                   jax.ShapeDtypeStruct((1, 1), F32)),
        in_specs=[_HBM] * nb + [pl.BlockSpec(memory_space=pl.ANY)],
        out_specs=(_SEM, _SEM, *[_HBM] * nb, vmem, vmem),
        input_output_aliases={i: 2 + i for i in range(nb)},
        compiler_params=pltpu.CompilerParams(has_side_effects=_DATAFLOW),
    )(*[pltpu.with_memory_space_constraint(b, pltpu.HBM) for b in bufs], _after_operand(after))
    return outs[0], outs[1], list(outs[2:2 + nb]), _Token(outs[-2], outs[-1])


def _copies_wait(started, plan, after, name):
    send_sems, recv_sems, bufs, _ = started
    nb = len(bufs)

    def body(*refs):
        for cp in _plan_copies(plan, refs[:nb], refs[nb], refs[nb + 1]):
            cp.wait_send()
            cp.wait_recv()

    return list(pl.pallas_call(
        body, name=name, out_shape=tuple(pltpu.HBM(b.shape, b.dtype) for b in bufs),
        in_specs=[_HBM] * nb + [_SEM, _SEM, pl.BlockSpec(memory_space=pl.ANY)], out_specs=tuple([_HBM] * nb),
        input_output_aliases={i: i for i in range(nb)},
        compiler_params=pltpu.CompilerParams(has_side_effects=_DATAFLOW),
    )(*bufs, send_sems, recv_sems, _after_operand(after)))


def _half_rows(ref, axis, c):
    half = ref.shape[axis] // 2
    return pl.ds(c * half, half)


def _plan_gather_ici(refs):
    n = len(refs) // 2
    x, y, c, chips = _place()
    out = []
    for a in range(n):
        rows = _half_rows(refs[a], 0, c)
        out += [(refs[a].at[rows], refs[n + a].at[2 * x + y, rows], (*chip, c)) for chip in chips]
        out.append((refs[a], refs[n + a].at[2 * x + y], (x, y, 1 - c)))
    return out


def _plan_gather_d2d(refs):
    x, y, c, chips = _place()
    out = []
    for ref in refs:
        rows = _half_rows(ref, 1, c)
        for px, py in chips:
            landed = ref.at[2 * px + py, rows]
            out.append((landed, landed, (x, y, 1 - c)))
    return out


def _plan_rs_sibling(refs):
    n = len(refs) // 2
    x, y, c, _ = _place()
    return [(refs[a].at[pl.ds(0, N_CHIPS), _half_rows(refs[a], 1, 1 - c)], refs[n + a], (x, y, 1 - c)) for a in range(n)]


def _plan_rs_chips(refs):
    n = len(refs) // 2
    x, y, c, chips = _place()
    return [(refs[a].at[2 * px + py], refs[n + a].at[k], (px, py, c)) for a in range(n) for k, (px, py) in enumerate(chips)]


def _plan_rs_share(layer):
    def plan(refs):
        x, y, c, _ = _place()
        return [(ref.at[layer, _half_rows(ref, 1, c)], ref.at[layer, _half_rows(ref, 1, c)], (x, y, 1 - c)) for ref in refs]
    return plan


def _chip_sum(g, other, sel, name):
    _, half, cdim = other.shape
    tr = _pick(half, (512, 256, 128, 64))
    per = half // tr

    def body(sel_ref, g_ref, t_ref, wire_ref, own_ref):
        total = g_ref[0] + t_ref[0]
        wire_ref[0] = total.astype(BF16)

        @pl.when(pl.program_id(1) == sel_ref[1])
        def _():
            own_ref[...] = total

    blk = pl.BlockSpec((1, tr, cdim), lambda i, p, sel_ref: (p, i, 0))
    return pl.pallas_call(
        body, name=name,
        grid_spec=pltpu.PrefetchScalarGridSpec(
            num_scalar_prefetch=1, grid=(per, N_CHIPS),
            in_specs=[pl.BlockSpec((1, tr, cdim), lambda i, p, sel_ref: (p, sel_ref[0] * per + i, 0)), blk],
            out_specs=[blk, pl.BlockSpec((tr, cdim), lambda i, p, sel_ref: (i, 0))]),
        out_shape=[jax.ShapeDtypeStruct(other.shape, BF16), jax.ShapeDtypeStruct((half, cdim), F32)],
        compiler_params=_params("parallel", "arbitrary"),
    )(sel, g, other)


def _final_sum(own, recv, sel, layer, into, name):
    half, cdim = own.shape
    tr = _pick(half, (512, 256, 128, 64))
    per = half // tr

    def body(sel_ref, own_ref, r0_ref, r1_ref, r2_ref, *rest):
        rest[-1][...] = ((own_ref[...] + r0_ref[0].astype(F32)) + r1_ref[0].astype(F32)) + r2_ref[0].astype(F32)

    part = lambda k: pl.BlockSpec((1, tr, cdim), lambda i, sel_ref, k=k: (k, i, 0))
    prior = [] if into is None else [into]
    return pl.pallas_call(
        body, name=name,
        grid_spec=pltpu.PrefetchScalarGridSpec(
            num_scalar_prefetch=1, grid=(per,),
            in_specs=[pl.BlockSpec((tr, cdim), lambda i, sel_ref: (i, 0)), part(0), part(1), part(2)]
            + [pl.BlockSpec(memory_space=pl.ANY)] * len(prior),
            out_specs=pl.BlockSpec((None, tr, cdim), lambda i, sel_ref: (layer, sel_ref[0] * per + i, 0))),
        out_shape=jax.ShapeDtypeStruct((DEPTH, 2 * half, cdim), F32),
        input_output_aliases={5: 0} if prior else {}, compiler_params=_params("parallel"),
    )(sel, own, recv, recv, recv, *prior)


def _row(v):
    return v.reshape(1, -1)


_BR_A, _BR_B, _BR_C = (0, A_WIDTH), (A_WIDTH, POOL_WIDTH), (A_WIDTH + POOL_WIDTH, CONV_WIDTH)


def _tie(v, token):
    return v if token is None else v + token.tie


def _no_hook(point, after, ready=None):
    return None


def _layer_fwd(x, w, mod, hook=_no_hook):
    s = x.shape[0]
    mod3 = mod.reshape(6, 1, D)
    h = _modnorm_fwd(x, _row(w["g_mix_pre"]), (mod3, 0), (mod3, 1), "mix_pre_fwd")
    z = _mm(h, w["w_all"], name="mm_in")
    qkv = z[:, Z_QKV:Z_PC].astype(BF16)
    fl = z[:, Z_FL:Z_COLS]
    cum = _cumf_fwd(fl, w["b_f_pad"])
    fr = cum[:, :HEADS].T.reshape(HEADS, s // ATT_K, ATT_K)
    br_a, lse = _attn_fwd(qkv, fr)
    br_b, br_c = _poolconv_fwd(z, w["w_pool_bd"], _tie(_row(w["pool_scale"]), hook("attn", lse)), w["conv_w"])
    hook("pool", br_b)
    wbr = w["w_branch"]
    pa = _mm(br_a, wbr, b_rows=_BR_A, name="mm_br_a")
    pb = _mm(br_b, wbr, b_rows=_BR_B, name="mm_br_b")
    pc = _mm(br_c, wbr, b_rows=_BR_C, name="mm_br_c")
    merged = _merge_fwd(z, pa, pb, pc)
    y = _mm(merged, w["w_out"], name="mm_out")
    x1, h2 = _post_pre_fwd(x, y, _row(w["g_mix_post"]), (mod3, 2), _row(w["g_ff_pre"]), (mod3, 3), (mod3, 4), "mix_post_ff_pre_fwd")
    a, r = _mm(h2, w["w_ff1"], b_split=N_CHIPS, epilogue=_relu2_fwd, out_dtype=(F32, BF16), name="mm_ff1")
    y2 = _mm(r, w["w_ff2"], name="mm_ff2")
    x2 = _post_fwd(x1, y2, _tie(_row(w["g_ff_post"]), hook("ff_post", y2)), (mod3, 5), "ff_post_fwd")
    hook("end", x2)
    saved = dict(x=x, h=h, z=z, qkv=qkv, fl=fl, fr=fr, lse=lse, br_a=br_a, br_b=br_b, br_c=br_c, pa=pa, pb=pb, pc=pc,
                 merged=merged, y=y, x1=x1, h2=h2, a=a, r=r, y2=y2)
    return x2, saved


def _layer_bwd(dx2, sv, w, mod, hook=_no_hook):
    s = dx2.shape[0]
    mod3 = mod.reshape(6, 1, D)
    dy2, sum_ff_post = _post_bwd(dx2, sv["y2"], _row(w["g_ff_post"]), (mod3, 5), "ff_post_bwd")
    (da,) = _mm(dy2, w["w_ff2"], tb=True, epilogue=_relu2_bwd, extras=(sv["a"],), out_dtype=(BF16,), name="mm_ff2_dx")
    d_w_ff2 = _mm(sv["r"], dy2, ta=True, name="mm_ff2_dw")
    dh2 = _mm(da, w["w_ff1"], tb=True, b_split=N_CHIPS, name="mm_ff1_dx")
    d_w_ff1 = _mm(sv["h2"], da, ta=True, out_split=N_CHIPS, name="mm_ff1_dw")
    g_ff_pre = _tie(_row(w["g_ff_pre"]), hook("ff_pre", dh2, dict(w_ff1=d_w_ff1, w_ff2=d_w_ff2)))
    dx1, dy, sum_mid = _pre_post_bwd(dh2, sv["x1"], dx2, g_ff_pre, (mod3, 4), sv["y"], _row(w["g_mix_post"]), (mod3, 2), "ff_pre_mix_post_bwd")
    sum_ff_pre, sum_mix_post = sum_mid, sum_mid[3:]
    dmerged = _mm(dy, w["w_out"], tb=True, name="mm_out_dx")
    d_w_out = _mm(sv["merged"], dy, ta=True, name="mm_out_dw")
    dz, dpa, dpb, dpc = _merge_bwd(dmerged, sv["z"], sv["pa"], sv["pb"], sv["pc"])
    wbr = w["w_branch"]
    dbr_a = _mm(dpa, wbr, tb=True, b_rows=_BR_A, name="mm_br_a_dx")
    dbr_b = _mm(dpb, wbr, tb=True, b_rows=_BR_B, name="mm_br_b_dx")
    dbr_c = _mm(dpc, wbr, tb=True, b_rows=_BR_C, name="mm_br_c_dx")
    d_w_branch = jnp.concatenate([_mm(sv["br_a"], dpa, ta=True, name="mm_br_a_dw"), _mm(sv["br_b"], dpb, ta=True, name="mm_br_b_dw"),
                                  _mm(sv["br_c"], dpc, ta=True, name="mm_br_c_dw")], axis=0)

    dq, dk, dv, dfc, dfr = _attn_bwd(sv["qkv"], dbr_a, sv["br_a"], sv["lse"], sv["fr"])
    dcum = dfc + jnp.pad(dfr.reshape(HEADS, s).T, ((0, 0), (0, 128 - HEADS)))
    dfl, sum_bf = _cumf_bwd(dcum, sv["fl"], _tie(w["b_f_pad"], hook("cumf", dfc)))
    dpc_z, d_wbd, sum_ps, sum_cw = _poolconv_bwd(dbr_b, dbr_c, sv["z"], w["w_pool_bd"], _row(w["pool_scale"]), w["conv_w"])
    for at, part in ((Z_QKV, dq), (Z_QKV + A_WIDTH, dk), (Z_QKV + 2 * A_WIDTH, dv), (Z_PC, dpc_z), (Z_FL, dfl)):
        dz = lax.dynamic_update_slice(dz, part, (0, at))
    dh = _mm(dz, w["w_all"], tb=True, name="mm_in_dx")
    d_w_all = _mm(sv["h"], dz, ta=True, name="mm_in_dw")
    hook("mix_pre", dh)
    dx, sum_mix_pre = _modnorm_bwd(dh, sv["x"], dx1, _row(w["g_mix_pre"]), (mod3, 1), "mix_pre_bwd")

    dmod = jnp.stack([sum_mix_pre[0], sum_mix_pre[1], sum_mix_post[0], sum_ff_pre[0], sum_ff_pre[1], sum_ff_post[0]])
    d_w_in = _w_in_shards(d_w_all)
    d_w_pool = jnp.stack([d_wbd[64 * g:64 * g + 64, 64 * g:64 * g + 64] for g in range(4)])
    big = dict(w_in=d_w_in, w_branch=d_w_branch, w_out=d_w_out, w_ff1=d_w_ff1, w_ff2=d_w_ff2)
    small = dict(g_mix_pre=sum_mix_pre[2], g_mix_post=sum_mix_post[1], g_ff_pre=sum_ff_pre[2], g_ff_post=sum_ff_post[1],
                 b_f=sum_bf[0, :HEADS], w_pool=d_w_pool, pool_scale=sum_ps[0], conv_w=sum_cw[0:3])
    return dx, dmod, big, small


_QKV_END, _FL_END, _PC_END = 3 * A_WIDTH, 3 * A_WIDTH + HEADS, 3 * A_WIDTH + HEADS + POOL_WIDTH + 3 * CONV_WIDTH
_W_IN_GROUPS = ((_PC_END, IN_COLS, Z_GL), (0, _QKV_END, Z_QKV), (_FL_END, _PC_END, Z_PC), (_QKV_END, _FL_END, Z_FL))
_SHARD_COLS = IN_COLS // N_CHIPS


def _w_all_from_shards(blocks):
    pieces = []
    for lo, hi, _ in _W_IN_GROUPS:
        for p in range(N_CHIPS):
            a, b = max(lo, p * _SHARD_COLS), min(hi, (p + 1) * _SHARD_COLS)
            if a < b:
                pieces.append(blocks[p][:, a - p * _SHARD_COLS:b - p * _SHARD_COLS])
    pieces.append(jnp.zeros((D, Z_COLS - IN_COLS), blocks.dtype))
    return jnp.concatenate(pieces, axis=1)


def _w_in_shards(d_w_all):
    blocks = []
    for p in range(N_CHIPS):
        pieces = []
        for lo, hi, at in sorted(_W_IN_GROUPS):
            a, b = max(lo, p * _SHARD_COLS), min(hi, (p + 1) * _SHARD_COLS)
            if a < b:
                pieces.append(d_w_all[:, at + a - lo:at + b - lo])
        blocks.append(jnp.concatenate(pieces, axis=1))
    return jnp.stack(blocks)


def _full_layer_weights(w_in_blocks, w_branch, w_out, w_ff1, w_ff2, g_mix_pre, g_mix_post, g_ff_pre, g_ff_post, b_f, w_pool, pool_scale, conv_w):
    w_all = _w_all_from_shards(w_in_blocks)
    wbd = (w_pool[:, :, None, :] * jnp.eye(4, dtype=F32)[:, None, :, None]).reshape(POOL_WIDTH, POOL_WIDTH)
    return dict(w_all=w_all, w_branch=w_branch, w_out=w_out, w_ff1=w_ff1, w_ff2=w_ff2, g_mix_pre=g_mix_pre, g_mix_post=g_mix_post,
                g_ff_pre=g_ff_pre, g_ff_post=g_ff_post, b_f_pad=jnp.pad(b_f, (0, 128 - HEADS)).reshape(1, 128), w_pool_bd=wbd,
                pool_scale=pool_scale, conv_w=conv_w)


class _NoComm:
    def layer_weights(self, l):
        raise NotImplementedError

    def fwd_hook(self, l):
        return _no_hook

    def bwd_hook(self, l):
        return _no_hook

    def grads_ready(self, l, big):
        return None


class _Layers(_NoComm):
    def __init__(self, layers):
        self.layers = layers

    def layer_weights(self, l):
        return self.layers[l]


def _local_step(x, target, mods, comm):
    saved, weights = [], []
    act = x
    for l in range(DEPTH):
        weights.append(comm.layer_weights(l))
        act, sv = _layer_fwd(act, weights[l], mods[l], comm.fwd_hook(l))
        saved.append(sv)
    dact, sq = _loss_head(act, target)
    loss = sq[0, 0] * (0.5 / D)
    dmods, bigs, smalls = [None] * DEPTH, [None] * DEPTH, [None] * DEPTH
    token = None
    for l in reversed(range(DEPTH)):
        dact, dmods[l], bigs[l], smalls[l] = _layer_bwd(dact, saved[l], weights[l], _tie(mods[l], token), comm.bwd_hook(l))
        token = comm.grads_ready(l, bigs[l])
    return loss, dact, jnp.stack(dmods), bigs, smalls


_BIG = ("w_in", "w_branch", "w_out", "w_ff1", "w_ff2")


class _GatherJob:
    def __init__(self, tag, shards, after):
        self.tag, self.n = tag, len(shards)
        lands = [lax.empty((N_CHIPS,) + s.shape, s.dtype) for s in shards]
        self.state = _copies_start(list(shards) + lands, _plan_gather_ici, 4 * self.n, after, "gather_ici_start_" + tag)
        self.token = self.state[3]

    def pass_on(self, after):
        bufs = _copies_wait(self.state, _plan_gather_ici, after, "gather_ici_wait_" + self.tag)
        self.state = _copies_start(bufs[self.n:], _plan_gather_d2d, 3 * self.n, bufs[0], "gather_d2d_start_" + self.tag)
        self.token = self.state[3]
        return self.token

    def done(self, after):
        return _copies_wait(self.state, _plan_gather_d2d, after, "gather_d2d_wait_" + self.tag)


class _ReduceJob:
    def __init__(self, tag, names, grads, sel, after, layer, into=None):
        self.tag, self.names, self.n, self.sel, self.layer, self.into = tag, names, len(names), sel, layer, into or {}
        lands = [lax.empty((N_CHIPS, g.shape[1] // 2, g.shape[2]), F32) for g in grads]
        self.state = _copies_start(list(grads) + lands, _plan_rs_sibling, self.n, after, "rs_sibling_start_" + tag)
        self.token = self.state[3]

    def chip_sums(self, after):
        bufs = _copies_wait(self.state, _plan_rs_sibling, after, "rs_sibling_wait_" + self.tag)
        wires, self.owns = zip(*[_chip_sum(bufs[i], bufs[self.n + i], self.sel, "rs_chip_sum_" + name) for i, name in enumerate(self.names)])
        lands = [lax.empty((3,) + w.shape[1:], BF16) for w in wires]
        self.state = _copies_start(list(wires) + lands, _plan_rs_chips, 3 * self.n, self.owns[0], "rs_chips_start_" + self.tag)
        self.token = self.state[3]
        return self.token

    def final_sums(self, after):
        bufs = _copies_wait(self.state, _plan_rs_chips, after, "rs_chips_wait_" + self.tag)
        sums = [_final_sum(self.owns[i], bufs[self.n + i], self.sel, self.layer, self.into.get(name), "rs_final_" + name)
                for i, name in enumerate(self.names)]
        self.state = _copies_start(sums, _plan_rs_share(self.layer), self.n, sums[0], "rs_share_start_" + self.tag)
        self.token = self.state[3]
        return self.token

    def done(self, after):
        return dict(zip(self.names, _copies_wait(self.state, _plan_rs_share(self.layer), after, "rs_share_wait_" + self.tag)))


def _chip_blocks(g):
    return g if g.ndim == 3 else g.reshape(N_CHIPS, -1, g.shape[1])


class _StepComm(_NoComm):
    def __init__(self, shards, sel, after):
        self.sel = sel
        self.small, self.grads, self.jobs = None, {}, {}
        self.jobs["in0"] = _GatherJob("in0", shards[0][:1], after)
        self.jobs["rest0"] = _GatherJob("rest0", shards[0][1:], self.jobs["in0"].token)
        self.jobs["all1"] = _GatherJob("all1", shards[1], self.jobs["rest0"].token)

    def layer_weights(self, l):
        if l == 0:
            job = self.jobs["in0"]
            (g_in,) = job.done(job.pass_on(self.jobs["all1"].token))
            self.weights0 = _full_layer_weights(g_in, None, None, None, None, *self.small[0])
            return self.weights0
        g_in, g_br, g_out, g_f1, g_f2 = self.landed1
        return _full_layer_weights(g_in, g_br.reshape(D, D), g_out.reshape(D, D), g_f1, g_f2.reshape(D_FF, D), *self.small[1])

    def fwd_hook(self, l):
        if l != 0:
            return _no_hook

        def hook(point, after, ready=None):
            if point == "attn":
                return self.jobs["rest0"].pass_on(after)
            if point == "ff_post":
                return self.jobs["all1"].pass_on(after)
            if point == "pool":
                g_br, g_out, g_f1, g_f2 = self.jobs["rest0"].done(after)
                self.weights0.update(w_branch=g_br.reshape(D, D), w_out=g_out.reshape(D, D), w_ff1=g_f1, w_ff2=g_f2.reshape(D_FF, D))
            if point == "end":
                self.landed1 = self.jobs["all1"].done(after)
            return None
        return hook

    def bwd_hook(self, l):
        if l != 0:
            return _no_hook

        def hook(point, after, ready=None):
            jobs = self.jobs
            if point == "ff_pre":
                token = jobs["rs1"].chip_sums(after)
                jobs["rs0_ff"] = _ReduceJob("0_ff", ("w_ff1", "w_ff2"), [_chip_blocks(ready[n]) for n in ("w_ff1", "w_ff2")], self.sel, token, 0)
                return jobs["rs0_ff"].token
            if point == "cumf":
                return jobs["rs0_ff"].chip_sums(jobs["rs1"].final_sums(after))
            self.layer1 = jobs["rs1"].done(after)
            jobs["rs0_ff"].into = self.layer1
            return None
        return hook

    def grads_ready(self, l, big):
        if l == 1:
            self.jobs["rs1"] = _ReduceJob("1", _BIG, [_chip_blocks(big[n]) for n in _BIG], self.sel, self.sel, 1)
            return self.jobs["rs1"].token
        names = ("w_in", "w_branch", "w_out")
        self.jobs["rs0_mix"] = _ReduceJob("0_mix", names, [_chip_blocks(big[n]) for n in names], self.sel, self.sel, 0, self.layer1)
        return self.jobs["rs0_mix"].token

    def finish_sums(self, after):
        jobs = self.jobs
        token = jobs["rs0_mix"].chip_sums(after)
        return jobs["rs0_ff"].final_sums(token)

    def finish_ff(self, after):
        self.grads.update(self.jobs["rs0_ff"].done(after))

    def finish_mix(self, after):
        job = self.jobs["rs0_mix"]
        self.grads.update(job.done(job.final_sums(after)))


_SMALL = ("g_mix_pre", "g_mix_post", "g_ff_pre", "g_ff_post", "b_f", "w_pool", "pool_scale", "conv_w")


def _w_in_view(t):
    return t.reshape(DEPTH, D // 128, 128, _SHARD_COLS).transpose(3, 1, 0, 2).reshape(_SHARD_COLS * (D // 128) * DEPTH, 128)


def _w_in_unview(t):
    return t.reshape(_SHARD_COLS, D // 128, DEPTH, 128).transpose(2, 1, 3, 0).reshape(DEPTH, D, _SHARD_COLS)


def _pack(parts, rows=8):
    flat = jnp.concatenate([p.reshape(-1) for p in parts])
    width = -(-flat.shape[0] // (rows * 128)) * 128
    return jnp.pad(flat, (0, rows * width - flat.shape[0])).reshape(rows, width)


def _unpack(packed, like):
    flat = packed.reshape(-1)
    out, at = [], 0
    for ref in like:
        out.append(flat[at:at + ref.size].reshape(ref.shape))
        at += ref.size
    return out


def kernel(x, c, w_ada, b_ada, g_mix_pre, g_mix_post, g_ff_pre, g_ff_post, w_in, b_f, w_pool, pool_scale, conv_w, w_branch, w_out, w_ff1, w_ff2, loss_target, m_w_ada, m_b_ada, m_g_mix_pre, m_g_mix_post, m_g_ff_pre, m_g_ff_post, m_w_in, m_b_f, m_w_pool, m_pool_scale, m_conv_w, m_w_branch, m_w_out, m_w_ff1, m_w_ff2, v_w_ada, v_b_ada, v_g_mix_pre, v_g_mix_post, v_g_ff_pre, v_g_ff_post, v_w_in, v_b_f, v_w_pool, v_pool_scale, v_conv_w, v_w_branch, v_w_out, v_w_ff1, v_w_ff2):
    xi, yi, ci = lax.axis_index("x"), lax.axis_index("y"), lax.axis_index("c")
    chip = 2 * xi + yi
    dev = 2 * chip + ci
    n_ada = w_ada.shape[2]

    first = jnp.zeros((8, D + 384), F32).at[0, :D].set(c[0]).at[0, D:].set(conv_w.reshape(-1))
    got = _allgather8(first, "gather_cond").reshape(N_DEV, 8, D + 384)[:, 0]
    c_all = got[:, :D]
    conv_full = got[0::2, D:].reshape(N_CHIPS, DEPTH, 3, CONV_WIDTH // N_CHIPS).transpose(1, 2, 0, 3).reshape(DEPTH, 3, CONV_WIDTH)

    b_loc = lax.dynamic_slice_in_dim(b_ada, chip * n_ada, n_ada, axis=1).reshape(DEPTH, 1, n_ada)
    mod_cols, silu_c = _ada_fwd(c_all, w_ada, b_loc)
    got = _allgather8(mod_cols.reshape(DEPTH * N_DEV, n_ada), "gather_mod").reshape(N_DEV, DEPTH, N_DEV, n_ada)[0::2]
    mod_all = got.transpose(1, 2, 0, 3).reshape(DEPTH, N_DEV, 6, D)
    mods = lax.dynamic_index_in_dim(mod_all, dev, axis=1, keepdims=False)

    comm = _StepComm([[w[l].astype(BF16) for w in (w_in, w_branch, w_out, w_ff1, w_ff2)] for l in range(DEPTH)],
                     jnp.stack([ci, chip]).astype(jnp.int32), mods)
    comm.small = [(g_mix_pre[l], g_mix_post[l], g_ff_pre[l], g_ff_post[l], b_f[l], w_pool[l], pool_scale[l], conv_full[l]) for l in range(DEPTH)]
    loss_part, grad_x, dmods, bigs, smalls = _local_step(x[0], loss_target[0], mods, comm)

    small_parts = [smalls[l][name] for name in _SMALL for l in range(DEPTH)] + [loss_part.reshape(1)]
    packed = _tie(_pack([dmods] + small_parts), comm.jobs["rs0_mix"].token)
    gathered = _allgather8(packed, "gather_small")
    dmod_all = gathered.reshape(N_DEV, -1)[:, :dmods.size].reshape(N_DEV, DEPTH, 6 * D)
    summed = _unpack(_sum_devices(gathered), [dmods] + small_parts)
    grad_b_ada = summed[0].reshape(DEPTH, 6 * D)
    loss = summed[-1][0]
    small_grads = {name: jnp.stack(summed[1 + 2 * i:3 + 2 * i]) for i, name in enumerate(_SMALL)}
    small_grads["conv_w"] = lax.dynamic_slice_in_dim(small_grads["conv_w"], chip * (CONV_WIDTH // N_CHIPS), CONV_WIDTH // N_CHIPS, axis=2)

    dmod_loc = lax.dynamic_slice_in_dim(dmod_all.transpose(1, 0, 2), chip * n_ada, n_ada, axis=2)
    tail_token = comm.finish_sums(grad_b_ada)
    silu_pad = _tie(jnp.pad(silu_c, ((0, 128 - N_DEV), (0, 0))), tail_token)
    dmod_pad = jnp.pad(dmod_loc.transpose(1, 0, 2).reshape(N_DEV, DEPTH * n_ada), ((0, 128 - N_DEV), (0, 0)))
    grad_w_ada = _mm(silu_pad, dmod_pad, ta=True, out_split=DEPTH, name="mm_ada_dw")

    grads = dict(w_ada=grad_w_ada, b_ada=grad_b_ada, **small_grads)
    weights = dict(w_ada=w_ada, b_ada=b_ada, g_mix_pre=g_mix_pre, g_mix_post=g_mix_post, g_ff_pre=g_ff_pre, g_ff_post=g_ff_post, w_in=w_in,
                   b_f=b_f, w_pool=w_pool, pool_scale=pool_scale, conv_w=conv_w, w_branch=w_branch, w_out=w_out, w_ff1=w_ff1, w_ff2=w_ff2)
    m_in = dict(w_ada=m_w_ada, b_ada=m_b_ada, g_mix_pre=m_g_mix_pre, g_mix_post=m_g_mix_post, g_ff_pre=m_g_ff_pre, g_ff_post=m_g_ff_post,
                w_in=m_w_in, b_f=m_b_f, w_pool=m_w_pool, pool_scale=m_pool_scale, conv_w=m_conv_w, w_branch=m_w_branch, w_out=m_w_out,
                w_ff1=m_w_ff1, w_ff2=m_w_ff2)
    v_in = dict(w_ada=v_w_ada, b_ada=v_b_ada, g_mix_pre=v_g_mix_pre, g_mix_post=v_g_mix_post, g_ff_pre=v_g_ff_pre, g_ff_post=v_g_ff_post,
                w_in=v_w_in, b_f=v_b_f, w_pool=v_w_pool, pool_scale=v_pool_scale, conv_w=v_conv_w, w_branch=v_w_branch, w_out=v_w_out,
                w_ff1=v_w_ff1, w_ff2=v_w_ff2)
    order = ("w_ada", "b_ada", "g_mix_pre", "g_mix_post", "g_ff_pre", "g_ff_post", "w_in", "b_f", "w_pool", "pool_scale", "conv_w",
             "w_branch", "w_out", "w_ff1", "w_ff2")
    delta, new_m, new_v = {}, {}, {}
    tiny = ("b_ada",) + _SMALL
    tiny_g = [_tie(grads[tiny[0]], tail_token)] + [grads[name] for name in tiny[1:]]
    res = _adamw_many([weights[name] for name in tiny], tiny_g, [m_in[name] for name in tiny], [v_in[name] for name in tiny], "adamw_small")
    for out, vals in zip((delta, new_m, new_v), res):
        out.update(zip(tiny, vals))
    delta["w_ada"], new_m["w_ada"], new_v["w_ada"] = _adamw(w_ada, grad_w_ada, m_w_ada, v_w_ada, "adamw_w_ada")
    comm.finish_ff(delta["w_ada"][0, :8, :128] + delta["b_ada"][0, :128])
    for name in ("w_ff1", "w_ff2", "w_in", "w_branch", "w_out"):
        if name == "w_in":
            comm.finish_mix(delta["w_ff2"][0, :8, :128])
        grads[name] = comm.grads[name]
        if name == "w_in":
            g_view = lax.optimization_barrier(_w_in_view(grads[name]))
            res = _adamw(_w_in_view(w_in), g_view, _w_in_view(m_w_in), _w_in_view(v_w_in), "adamw_w_in")
            grads[name], delta[name], new_m[name], new_v[name] = [_w_in_unview(t) for t in (g_view, *res)]
        else:
            delta[name], new_m[name], new_v[name] = _adamw(weights[name], grads[name], m_in[name], v_in[name], "adamw_" + name)

    return (loss, grad_x[None], *[grads[n] for n in order], *[delta[n] for n in order], *[new_m[n] for n in order],
            *[new_v[n] for n in order])
```

```python
from typing import NamedTuple

import jax
import jax.numpy as jnp
from jax import lax
from jax.experimental import pallas as pl
from jax.experimental.pallas import tpu as pltpu

F32 = jnp.float32
BF16 = jnp.bfloat16
MESH = pl.DeviceIdType.MESH

D = 1024
DEPTH = 2
HEADS = 8
HEAD_DIM = 64
A_WIDTH = 512
POOL_WIDTH = 256
CONV_WIDTH = 256
D_FF = 4096
IN_COLS = 5640
Z_GL, Z_QKV, Z_PC, Z_FL, Z_COLS = 0, 3072, 4608, 5632, 5760
RMS_EPS = 1e-6
NEG_INF = -1e30
ROW_TILE = 512
EW_ROWS = 256
N_CHIPS = 4
N_DEV = 8
V7X_VMEM_LIMIT = 48 * 1024 * 1024

ADAM_LR = 0.001
ADAM_B1 = 0.9
ADAM_B2 = 0.999
ADAM_EPS = 1e-08
ADAM_WD = 0.01
ADAM_STEP = 10

_HBM = pl.BlockSpec(memory_space=pltpu.HBM)


def _params(*sem):
    return pltpu.CompilerParams(dimension_semantics=sem, vmem_limit_bytes=V7X_VMEM_LIMIT)


def _pick(dim, cands):
    for cand in cands:
        if dim % cand == 0:
            return cand
    return dim


MM_TILE_BUDGET = 40 * 1024 * 1024


def _mm_tiles(m, n, k, k_unit, tn, a_size, b_size, out_size):
    for tk in (k_unit, 2048, 1152, 1024, 640, 512, 256, 128):
        if k_unit % tk:
            continue
        for tm in (2048, 1024, 512, 256, 128):
            if m % tm or ((m // tm) * (n // tn) < 2 and tm > 128):
                continue
            need = 2 * (tm * tk * a_size + tk * tn * b_size + tm * tn * out_size) + (0 if tk == k else 4 * tm * tn)
            if need <= MM_TILE_BUDGET and (tk == k_unit or tm >= 512):
                return tm, tk
    return 128, 128


def _mm(a, b, *, ta=False, tb=False, b_rows=None, b_split=1, out_split=1, out_dtype=F32, epilogue=None, extras=(), name):
    (k, m) = a.shape if ta else a.shape[::-1]
    b_row0, b_rows = (0, b.shape[-2]) if b_rows is None else b_rows
    b_cols = b.shape[-1] * b_split
    (n, k2) = (b_rows, b_cols) if tb else (b_cols, b_rows)
    assert k == k2, (a.shape, b.shape, ta, tb)
    n_unit = n // (out_split * (1 if tb else b_split))
    k_unit = k // (b_split if tb else 1)
    tn = _pick(n_unit, (1024, 1152, 768, 640, 512, 256, 128))
    tm, tk = _mm_tiles(m, n, k, k_unit, tn, a.dtype.itemsize, b.dtype.itemsize,
                       sum(jnp.dtype(dt).itemsize for dt in out_dtype) + 4 * len(extras) if epilogue else jnp.dtype(out_dtype).itemsize)
    nk = k // tk
    dims = (((0 if ta else 1,), (1 if tb else 0,)), ((), ()))

    def dot(a_ref, b_ref):
        b_val = b_ref[0] if b_split > 1 else b_ref[...]
        return lax.dot_general(a_ref[...].astype(BF16), b_val.astype(BF16), dims, preferred_element_type=F32)

    n_extra = len(extras)
    assert epilogue is None or out_split == 1

    def put(refs, val):
        if epilogue is not None:
            for o_ref, res in zip(refs[n_extra:], epilogue(val, *[r[...] for r in refs[:n_extra]])):
                o_ref[...] = res.astype(o_ref.dtype)
        elif out_split > 1:
            refs[0][0] = val.astype(refs[0].dtype)
        else:
            refs[0][...] = val.astype(refs[0].dtype)

    def body_single(a_ref, b_ref, *refs):
        put(refs, dot(a_ref, b_ref))

    def body_acc(a_ref, b_ref, *refs):
        kk = pl.program_id(2)
        acc_ref = refs[-1]

        @pl.when(kk == 0)
        def _():
            acc_ref[...] = jnp.zeros_like(acc_ref)

        acc_ref[...] += dot(a_ref, b_ref)

        @pl.when(kk == nk - 1)
        def _():
            put(refs[:-1], acc_ref[...])

    a_spec = pl.BlockSpec((tk, tm), lambda i, j, kk: (kk, i)) if ta else pl.BlockSpec((tm, tk), lambda i, j, kk: (i, kk))
    if b_split == 1:
        off = b_row0 // (tn if tb else tk)
        assert off * (tn if tb else tk) == b_row0
        b_spec = pl.BlockSpec((tn, tk), lambda i, j, kk: (j + off, kk)) if tb else pl.BlockSpec((tk, tn), lambda i, j, kk: (kk + off, j))
    elif tb:
        per = k_unit // tk
        b_spec = pl.BlockSpec((1, tn, tk), lambda i, j, kk: (kk // per, j, kk % per))
    else:
        per = n // b_split // tn
        b_spec = pl.BlockSpec((1, tk, tn), lambda i, j, kk: (j // per, kk, j % per))
    if out_split == 1:
        o_spec = pl.BlockSpec((tm, tn), lambda i, j, kk: (i, j))
        o_shape = None if epilogue is not None else jax.ShapeDtypeStruct((m, n), out_dtype)
    else:
        per_o = n // out_split // tn
        o_spec = pl.BlockSpec((1, tm, tn), lambda i, j, kk: (j // per_o, i, j % per_o))
        o_shape = jax.ShapeDtypeStruct((out_split, m, n // out_split), out_dtype)
    if epilogue is not None:
        o_shape = [jax.ShapeDtypeStruct((m, n), dt) for dt in out_dtype]
        o_spec = [o_spec] * len(out_dtype)
    return pl.pallas_call(
        body_single if nk == 1 else body_acc, name=name, grid=(m // tm, n // tn, nk),
        in_specs=[a_spec, b_spec] + [pl.BlockSpec((tm, tn), lambda i, j, kk: (i, j))] * n_extra, out_specs=o_spec, out_shape=o_shape,
        scratch_shapes=[] if nk == 1 else [pltpu.VMEM((tm, tn), F32)],
        compiler_params=_params("parallel", "parallel", "arbitrary"),
    )(a, b, *extras)


def _ew(fn, ins, out_dtypes, name, tc=None):
    shape = ins[0].shape
    lead, (rows, cols) = shape[:-2], shape[-2:]
    tc = cols if tc is None else tc
    if tc > 1024:
        tr = _pick(rows, (EW_ROWS, 128, 8))
    elif tc > 128:
        tr = _pick(rows, (2 * EW_ROWS, EW_ROWS, 128, 8))
    else:
        tr = _pick(rows, (4096, 2256, 2048, 1024, EW_ROWS, 8))
    n_in = len(ins)

    def body(*refs):
        res = fn(*[r[...] for r in refs[:n_in]])
        for o_ref, val in zip(refs[n_in:], res):
            o_ref[...] = val.astype(o_ref.dtype)

    if lead:
        spec = pl.BlockSpec((None, tr, tc), lambda l, i, j: (l, i, j))
    else:
        spec = pl.BlockSpec((tr, tc), lambda i, j: (i, j))
    return pl.pallas_call(
        body, name=name, grid=lead + (rows // tr, cols // tc),
        in_specs=[spec] * n_in, out_specs=[spec] * len(out_dtypes),
        out_shape=[jax.ShapeDtypeStruct(shape, dt) for dt in out_dtypes],
        compiler_params=_params(*(["parallel"] * (len(lead) + 2))),
    )(*ins)


def _relu2_fwd(a):
    r = jnp.maximum(a, 0.0)
    return a, r * r


def _relu2_bwd(dr, a):
    return (dr * (2.0 * jnp.maximum(a, 0.0)),)


def _adamw_math(w, g, m, v):
    m = ADAM_B1 * m + (1.0 - ADAM_B1) * g
    v = ADAM_B2 * v + (1.0 - ADAM_B2) * (g * g)
    m_hat = m / (1.0 - ADAM_B1 ** ADAM_STEP)
    v_hat = v / (1.0 - ADAM_B2 ** ADAM_STEP)
    delta = -ADAM_LR * (m_hat / (jnp.sqrt(v_hat) + ADAM_EPS) + ADAM_WD * w)
    return delta, m, v


def _adamw(w, g, m, v, name):
    return _ew(_adamw_math, [w, g, m, v], [F32, F32, F32], name)


def _adamw_many(ws, gs, ms, vs, name):
    n = len(ws)

    def body(*refs):
        for i in range(n):
            res = _adamw_math(*[refs[k * n + i][...] for k in range(4)])
            for k in range(3):
                refs[(4 + k) * n + i][...] = res[k]

    outs = pl.pallas_call(
        body, name=name, out_shape=[jax.ShapeDtypeStruct(w.shape, F32) for w in ws] * 3,
        compiler_params=pltpu.CompilerParams(vmem_limit_bytes=V7X_VMEM_LIMIT),
    )(*ws, *gs, *ms, *vs)
    return outs[:n], outs[n:2 * n], outs[2 * n:]


def _row_spec(cols, block=0):
    return pl.BlockSpec((ROW_TILE, cols), lambda i, block=block: (i, block))


def _vec_spec(cols):
    return pl.BlockSpec((1, cols), lambda i: (0, 0))


def _vec_args(*vecs):
    arrays = [v[0] if isinstance(v, tuple) else v for v in vecs]
    specs = [pl.BlockSpec((None, 1, D), lambda i, row=v[1]: (row, 0, 0)) if isinstance(v, tuple) else _vec_spec(D) for v in vecs]
    return arrays, specs


def _sum_spec(cols):
    return pl.BlockSpec((8, cols), lambda i: (0, 0))


def _rstd(x):
    return lax.rsqrt(jnp.mean(x * x, axis=-1, keepdims=True) + RMS_EPS)


def _modnorm_fwd(x, g, shift, scale, name):
    s = x.shape[0]

    def body(x_ref, g_ref, sh_ref, sc_ref, h_ref):
        xv = x_ref[...]
        n = xv * _rstd(xv)
        h_ref[...] = ((n * g_ref[...]) * (1.0 + sc_ref[...]) + sh_ref[...]).astype(BF16)

    vecs, vec_specs = _vec_args(g, shift, scale)
    return pl.pallas_call(
        body, name=name, grid=(s // ROW_TILE,),
        in_specs=[_row_spec(D)] + vec_specs, out_specs=_row_spec(D),
        out_shape=jax.ShapeDtypeStruct((s, D), BF16), compiler_params=_params("parallel"),
    )(x, *vecs)


def _post_fwd(x, y, g, gate, name):
    s = x.shape[0]

    def body(x_ref, y_ref, g_ref, gate_ref, o_ref):
        yv = y_ref[...]
        o_ref[...] = x_ref[...] + gate_ref[...] * ((yv * _rstd(yv)) * g_ref[...])

    vecs, vec_specs = _vec_args(g, gate)
    return pl.pallas_call(
        body, name=name, grid=(s // ROW_TILE,),
        in_specs=[_row_spec(D), _row_spec(D)] + vec_specs, out_specs=_row_spec(D),
        out_shape=jax.ShapeDtypeStruct((s, D), F32), compiler_params=_params("parallel"),
    )(x, y, *vecs)


def _post_bwd(dxo, y, g, gate, name):
    s = dxo.shape[0]

    def body(d_ref, y_ref, g_ref, gate_ref, dy_ref, sum_ref):
        @pl.when(pl.program_id(0) == 0)
        def _():
            sum_ref[...] = jnp.zeros_like(sum_ref)

        dv, yv = d_ref[...], y_ref[...]
        r = _rstd(yv)
        n = yv * r
        sum_ref[0:1, :] += jnp.sum(dv * (n * g_ref[...]), axis=0, keepdims=True)
        sum_ref[1:2, :] += jnp.sum((dv * gate_ref[...]) * n, axis=0, keepdims=True)
        dn = (dv * gate_ref[...]) * g_ref[...]
        dy_ref[...] = (r * (dn - n * jnp.mean(dn * n, axis=-1, keepdims=True))).astype(BF16)

    vecs, vec_specs = _vec_args(g, gate)
    return pl.pallas_call(
        body, name=name, grid=(s // ROW_TILE,),
        in_specs=[_row_spec(D), _row_spec(D)] + vec_specs,
        out_specs=[_row_spec(D), _sum_spec(D)],
        out_shape=[jax.ShapeDtypeStruct((s, D), BF16), jax.ShapeDtypeStruct((8, D), F32)],
        compiler_params=_params("arbitrary"),
    )(dxo, y, *vecs)


def _modnorm_bwd(dh, x, dxo, g, scale, name):
    s = dh.shape[0]

    def body(dh_ref, x_ref, d_ref, g_ref, sc_ref, dx_ref, sum_ref):
        @pl.when(pl.program_id(0) == 0)
        def _():
            sum_ref[...] = jnp.zeros_like(sum_ref)

        dhv, xv = dh_ref[...], x_ref[...]
        r = _rstd(xv)
        n = xv * r
        one_sc = 1.0 + sc_ref[...]
        sum_ref[0:1, :] += jnp.sum(dhv, axis=0, keepdims=True)
        sum_ref[1:2, :] += jnp.sum(dhv * (n * g_ref[...]), axis=0, keepdims=True)
        sum_ref[2:3, :] += jnp.sum((dhv * one_sc) * n, axis=0, keepdims=True)
        dn = (dhv * one_sc) * g_ref[...]
        dx_ref[...] = d_ref[...] + r * (dn - n * jnp.mean(dn * n, axis=-1, keepdims=True))

    vecs, vec_specs = _vec_args(g, scale)
    return pl.pallas_call(
        body, name=name, grid=(s // ROW_TILE,),
        in_specs=[_row_spec(D), _row_spec(D), _row_spec(D)] + vec_specs,
        out_specs=[_row_spec(D), _sum_spec(D)],
        out_shape=[jax.ShapeDtypeStruct((s, D), F32), jax.ShapeDtypeStruct((8, D), F32)],
        compiler_params=_params("arbitrary"),
    )(dh, x, dxo, *vecs)


def _post_pre_fwd(x, y, g_post, gate, g_pre, shift, scale, name):
    s = x.shape[0]

    def body(x_ref, y_ref, gp_ref, gate_ref, g_ref, sh_ref, sc_ref, o_ref, h_ref):
        yv = y_ref[...]
        xo = x_ref[...] + gate_ref[...] * ((yv * _rstd(yv)) * gp_ref[...])
        o_ref[...] = xo
        h_ref[...] = (((xo * _rstd(xo)) * g_ref[...]) * (1.0 + sc_ref[...]) + sh_ref[...]).astype(BF16)

    vecs, vec_specs = _vec_args(g_post, gate, g_pre, shift, scale)
    return pl.pallas_call(
        body, name=name, grid=(s // ROW_TILE,),
        in_specs=[_row_spec(D), _row_spec(D)] + vec_specs, out_specs=[_row_spec(D), _row_spec(D)],
        out_shape=[jax.ShapeDtypeStruct((s, D), F32), jax.ShapeDtypeStruct((s, D), BF16)], compiler_params=_params("parallel"),
    )(x, y, *vecs)


def _pre_post_bwd(dh, x, dxo, g_pre, scale, y, g_post, gate, name):
    s = dh.shape[0]

    def body(dh_ref, x_ref, d_ref, y_ref, g_ref, sc_ref, gp_ref, gate_ref, dx_ref, dy_ref, sum_ref):
        @pl.when(pl.program_id(0) == 0)
        def _():
            sum_ref[...] = jnp.zeros_like(sum_ref)

        dhv, xv = dh_ref[...], x_ref[...]
        r = _rstd(xv)
        n = xv * r
        one_sc = 1.0 + sc_ref[...]
        sum_ref[0:1, :] += jnp.sum(dhv, axis=0, keepdims=True)
        sum_ref[1:2, :] += jnp.sum(dhv * (n * g_ref[...]), axis=0, keepdims=True)
        sum_ref[2:3, :] += jnp.sum((dhv * one_sc) * n, axis=0, keepdims=True)
        dn = (dhv * one_sc) * g_ref[...]
        dv = d_ref[...] + r * (dn - n * jnp.mean(dn * n, axis=-1, keepdims=True))
        dx_ref[...] = dv

        yv = y_ref[...]
        ry = _rstd(yv)
        ny = yv * ry
        sum_ref[3:4, :] += jnp.sum(dv * (ny * gp_ref[...]), axis=0, keepdims=True)
        sum_ref[4:5, :] += jnp.sum((dv * gate_ref[...]) * ny, axis=0, keepdims=True)
        dny = (dv * gate_ref[...]) * gp_ref[...]
        dy_ref[...] = (ry * (dny - ny * jnp.mean(dny * ny, axis=-1, keepdims=True))).astype(BF16)

    vecs, vec_specs = _vec_args(g_pre, scale, g_post, gate)
    return pl.pallas_call(
        body, name=name, grid=(s // ROW_TILE,),
        in_specs=[_row_spec(D)] * 4 + vec_specs,
        out_specs=[_row_spec(D), _row_spec(D), _sum_spec(D)],
        out_shape=[jax.ShapeDtypeStruct((s, D), F32), jax.ShapeDtypeStruct((s, D), BF16), jax.ShapeDtypeStruct((8, D), F32)],
        compiler_params=_params("arbitrary"),
    )(dh, x, dxo, y, *vecs)


def _loss_head(y, target):
    s = y.shape[0]

    def body(y_ref, t_ref, dy_ref, sum_ref):
        @pl.when(pl.program_id(0) == 0)
        def _():
            sum_ref[...] = jnp.zeros_like(sum_ref)

        err = y_ref[...] - t_ref[...]
        dy_ref[...] = err * (1.0 / D)
        sum_ref[...] += jnp.sum(err * err)

    return pl.pallas_call(
        body, name="loss_head", grid=(s // ROW_TILE,),
        in_specs=[_row_spec(D), _row_spec(D)],
        out_specs=[_row_spec(D), pl.BlockSpec((8, 128), lambda i: (0, 0))],
        out_shape=[jax.ShapeDtypeStruct((s, D), F32), jax.ShapeDtypeStruct((8, 128), F32)],
        compiler_params=_params("arbitrary"),
    )(y, target)


def _merge_fwd(z, pa, pb, pc):
    s = z.shape[0]

    def body(g0_ref, g1_ref, g2_ref, pa_ref, pb_ref, pc_ref, o_ref):
        o_ref[...] = (jax.nn.sigmoid(g0_ref[...]) * pa_ref[...] + jax.nn.sigmoid(g1_ref[...]) * pb_ref[...]
                      + jax.nn.sigmoid(g2_ref[...]) * pc_ref[...]).astype(BF16)

    return pl.pallas_call(
        body, name="merge_fwd", grid=(s // ROW_TILE,),
        in_specs=[_row_spec(D, 0), _row_spec(D, 1), _row_spec(D, 2), _row_spec(D), _row_spec(D), _row_spec(D)],
        out_specs=_row_spec(D), out_shape=jax.ShapeDtypeStruct((s, D), BF16),
        compiler_params=_params("parallel"),
    )(z, z, z, pa, pb, pc)


def _merge_bwd(dm, z, pa, pb, pc):
    s = z.shape[0]

    def body(dm_ref, g0_ref, g1_ref, g2_ref, pa_ref, pb_ref, pc_ref, dgl_ref, da_ref, db_ref, dc_ref):
        dmv = dm_ref[...]
        for i, (g_ref, p_ref, d_ref) in enumerate(((g0_ref, pa_ref, da_ref), (g1_ref, pb_ref, db_ref), (g2_ref, pc_ref, dc_ref))):
            gate = jax.nn.sigmoid(g_ref[...])
            dgl_ref[:, i * D:(i + 1) * D] = ((dmv * p_ref[...]) * (gate * (1.0 - gate))).astype(BF16)
            d_ref[...] = (dmv * gate).astype(BF16)

    return pl.pallas_call(
        body, name="merge_bwd", grid=(s // ROW_TILE,),
        in_specs=[_row_spec(D), _row_spec(D, 0), _row_spec(D, 1), _row_spec(D, 2), _row_spec(D), _row_spec(D), _row_spec(D)],
        out_specs=[_row_spec(3 * D), _row_spec(D), _row_spec(D), _row_spec(D)],
        out_shape=[jax.ShapeDtypeStruct((s, Z_COLS), BF16)] + [jax.ShapeDtypeStruct((s, D), BF16)] * 3,
        compiler_params=_params("parallel"),
    )(dm, z, z, z, pa, pb, pc)


def _shift_down(v, n):
    row = lax.broadcasted_iota(jnp.int32, v.shape, 0)
    return jnp.where(row >= n, pltpu.roll(v, n, axis=0), 0.0)


def _shift_up(v, n):
    s = v.shape[0]
    row = lax.broadcasted_iota(jnp.int32, v.shape, 0)
    return jnp.where(row < s - n, pltpu.roll(v, s - n, axis=0), 0.0)


def _log_sigmoid(v):
    return jnp.minimum(v, 0.0) - jnp.log1p(jnp.exp(-jnp.abs(v)))


def _cumf_fwd(fl, bias):
    s = fl.shape[0]

    def body(fl_ref, b_ref, o_ref):
        acc = _log_sigmoid(fl_ref[...] + b_ref[...])
        step = 1
        while step < s:
            acc = acc + _shift_down(acc, step)
            step *= 2
        o_ref[...] = acc

    return pl.pallas_call(body, name="cumf_fwd", out_shape=jax.ShapeDtypeStruct((s, 128), F32),
                          compiler_params=pltpu.CompilerParams(vmem_limit_bytes=V7X_VMEM_LIMIT))(fl, bias)


def _cumf_bwd(dcum, fl, bias):
    s = fl.shape[0]

    def body(d_ref, fl_ref, b_ref, dfl_ref, db_ref):
        acc = d_ref[...]
        step = 1
        while step < s:
            acc = acc + _shift_up(acc, step)
            step *= 2
        dfl = acc * jax.nn.sigmoid(-(fl_ref[...] + b_ref[...]))
        dfl_ref[...] = dfl.astype(BF16)
        db_ref[...] = jnp.broadcast_to(jnp.sum(dfl, axis=0, keepdims=True), (8, 128))

    return pl.pallas_call(
        body, name="cumf_bwd",
        out_shape=[jax.ShapeDtypeStruct((s, 128), BF16), jax.ShapeDtypeStruct((8, 128), F32)],
        compiler_params=pltpu.CompilerParams(vmem_limit_bytes=V7X_VMEM_LIMIT))(dcum, fl, bias)


def _pool_windows(v, shift):
    s2 = v + shift(v, 1)
    s4 = s2 + shift(s2, 2)
    s8 = s4 + shift(s4, 4)
    s16 = s8 + shift(s8, 8)
    group = lax.broadcasted_iota(jnp.int32, v.shape, 1) // 64
    return jnp.where(group == 0, s2, jnp.where(group == 1, s4, jnp.where(group == 2, s8, s16)))


def _pool_count(shape):
    group = lax.broadcasted_iota(jnp.int32, shape, 1) // 64
    window = jnp.where(group == 0, 2.0, jnp.where(group == 1, 4.0, jnp.where(group == 2, 8.0, 16.0)))
    t1 = (lax.broadcasted_iota(jnp.int32, shape, 0) + 1).astype(F32)
    return jnp.minimum(t1, window)


def _pc_specs(s):
    zcol = lambda blk: pl.BlockSpec((s, 256), lambda i, blk=blk: (0, blk))
    first = Z_PC // 256
    return [zcol(first), zcol(first + 1), zcol(first + 2), zcol(first + 3),
            pl.BlockSpec((256, 256), lambda i: (0, 0)), pl.BlockSpec((1, 256), lambda i: (0, 0)),
            pl.BlockSpec((3, 256), lambda i: (0, 0))]


def _poolconv_fwd(z, wbd, pscale, convw):
    s = z.shape[0]

    def body(pu_ref, ch_ref, cb_ref, cc_ref, w_ref, ps_ref, cw_ref, yb_ref, yc_ref):
        u = pu_ref[...]
        p = _pool_windows(u, _shift_down) / _pool_count(u.shape) - u
        yb = jnp.dot(p.astype(BF16), w_ref[...].astype(BF16), preferred_element_type=F32) * ps_ref[...]
        yb_ref[...] = yb.astype(BF16)
        uc = cc_ref[...] * ch_ref[...]
        cw = cw_ref[...]
        conv = cw[0:1, :] * _shift_down(uc, 2) + cw[1:2, :] * _shift_down(uc, 1) + cw[2:3, :] * uc
        yc_ref[...] = (cb_ref[...] * conv).astype(BF16)

    out = pl.BlockSpec((s, 256), lambda i: (0, 0))
    return pl.pallas_call(
        body, name="poolconv_fwd", grid=(1,), in_specs=_pc_specs(s), out_specs=[out, out],
        out_shape=[jax.ShapeDtypeStruct((s, 256), BF16)] * 2, compiler_params=_params("arbitrary"),
    )(z, z, z, z, wbd, pscale, convw)


def _poolconv_bwd(dyb, dyc, z, wbd, pscale, convw):
    s = z.shape[0]

    def body(dyb_ref, dyc_ref, pu_ref, ch_ref, cb_ref, cc_ref, w_ref, ps_ref, cw_ref, dz_ref, dw_ref, dps_ref, dcw_ref):
        u = pu_ref[...]
        count = _pool_count(u.shape)
        p = (_pool_windows(u, _shift_down) / count - u).astype(BF16)
        wb = w_ref[...].astype(BF16)
        dyb_v = dyb_ref[...]
        pw = jnp.dot(p, wb, preferred_element_type=F32)
        dps_ref[...] = jnp.broadcast_to(jnp.sum(dyb_v * pw, axis=0, keepdims=True), (8, 256))
        dys = (dyb_v * ps_ref[...]).astype(BF16)
        dp = lax.dot_general(dys, wb, (((1,), (1,)), ((), ())), preferred_element_type=F32)
        dw_ref[...] = lax.dot_general(p, dys, (((0,), (0,)), ((), ())), preferred_element_type=F32)
        dz_ref[:, 0:256] = (_pool_windows(dp / count, _shift_up) - dp).astype(BF16)

        ch, cb, cc = ch_ref[...], cb_ref[...], cc_ref[...]
        uc = cc * ch
        cw = cw_ref[...]
        u2, u1 = _shift_down(uc, 2), _shift_down(uc, 1)
        conv = cw[0:1, :] * u2 + cw[1:2, :] * u1 + cw[2:3, :] * uc
        dyc_v = dyc_ref[...]
        dconv = dyc_v * cb
        du = cw[0:1, :] * _shift_up(dconv, 2) + cw[1:2, :] * _shift_up(dconv, 1) + cw[2:3, :] * dconv
        dz_ref[:, 256:512] = (du * cc).astype(BF16)
        dz_ref[:, 512:768] = (dyc_v * conv).astype(BF16)
        dz_ref[:, 768:1024] = (du * ch).astype(BF16)
        dcw_ref[...] = jnp.zeros_like(dcw_ref)
        dcw_ref[0:1, :] = jnp.sum(dconv * u2, axis=0, keepdims=True)
        dcw_ref[1:2, :] = jnp.sum(dconv * u1, axis=0, keepdims=True)
        dcw_ref[2:3, :] = jnp.sum(dconv * uc, axis=0, keepdims=True)

    blk = lambda r, c: pl.BlockSpec((r, c), lambda i: (0, 0))
    return pl.pallas_call(
        body, name="poolconv_bwd", grid=(1,),
        in_specs=[blk(s, 256), blk(s, 256)] + _pc_specs(s),
        out_specs=[blk(s, 1024), blk(256, 256), blk(8, 256), blk(8, 256)],
        out_shape=[jax.ShapeDtypeStruct((s, 1024), BF16), jax.ShapeDtypeStruct((256, 256), F32),
                   jax.ShapeDtypeStruct((8, 256), F32), jax.ShapeDtypeStruct((8, 256), F32)],
        compiler_params=_params("arbitrary"),
    )(dyb, dyc, z, z, z, z, wbd, pscale, convw)


_NT = (((1,), (1,)), ((), ()))
_TN = (((0,), (0,)), ((), ()))


ATT_Q, ATT_K = 256, 256
ATT_HEADS_BWD = 8
ATT_HEADS = 8


def _att_logits(q, k, fr, q0, k0, masked):
    logits = lax.dot_general(q, k, _NT, preferred_element_type=F32) - fr
    if not masked:
        return logits
    row = q0 + lax.broadcasted_iota(jnp.int32, logits.shape, 0)
    col = k0 + lax.broadcasted_iota(jnp.int32, logits.shape, 1)
    return jnp.where(row >= col, logits, NEG_INF)


def _causal_sweep(step, qi, init):
    n_full = (qi * ATT_Q) // ATT_K
    carry = lax.fori_loop(0, n_full, lambda j, carry: step(j, carry, False), init)
    return step(n_full, carry, True)


HEAD_PAIRS = HEADS // 2


def _lane_pick(v, lane, idx):
    return jnp.sum(jnp.where(lane == idx, v, 0.0), axis=-1, keepdims=True)


def _lane_put(lane, idx, col):
    return jnp.where(lane == idx, col, 0.0)


def _split_heads(v, low):
    zero = jnp.zeros_like(v)
    return jnp.where(low, v, zero), jnp.where(low, zero, v)


def _attn_fwd(qkv, fr):
    s = qkv.shape[0]
    nk = s // ATT_K
    width = ATT_HEADS * HEAD_DIM
    groups = HEADS // ATT_HEADS

    def body(q_ref, k_ref, v_ref, fr_ref, o_ref, lse_ref):
        qi, grp = pl.program_id(0), pl.program_id(1)
        lane = lax.broadcasted_iota(jnp.int32, (ATT_Q, 128), 1)
        low = lane < HEAD_DIM
        qs = []
        for pr in range(ATT_HEADS // 2):
            qs += _split_heads(q_ref[:, 128 * pr:128 * (pr + 1)] * (HEAD_DIM ** -0.5), low)

        def step(j, carry, masked):
            k0 = pl.multiple_of(j * ATT_K, ATT_K)
            out = []
            for h in range(ATT_HEADS):
                cols = slice(128 * (h // 2), 128 * (h // 2 + 1))
                m, l, acc = carry[h]
                logits = _att_logits(qs[h], k_ref[pl.ds(k0, ATT_K), cols], fr_ref[h, pl.ds(j, 1), :], qi * ATT_Q, k0, masked)
                m_new = jnp.maximum(m, jnp.max(logits, axis=-1, keepdims=True))
                p = jnp.exp(logits - m_new)
                alpha = jnp.exp(m - m_new)
                l = alpha * l + jnp.sum(p, axis=-1, keepdims=True)
                acc = alpha * acc + jnp.dot(p.astype(BF16), v_ref[pl.ds(k0, ATT_K), cols], preferred_element_type=F32)
                out.append((m_new, l, acc))
            return tuple(out)

        one = (jnp.full((ATT_Q, 1), NEG_INF, F32), jnp.zeros((ATT_Q, 1), F32), jnp.zeros((ATT_Q, 128), F32))
        done = _causal_sweep(step, qi, (one,) * ATT_HEADS)

        @pl.when(grp == 0)
        def _():
            lse_ref[...] = jnp.zeros_like(lse_ref)

        lse = jnp.zeros((ATT_Q, 128), F32)
        for pr in range(ATT_HEADS // 2):
            (m0, l0, acc0), (m1, l1, acc1) = done[2 * pr], done[2 * pr + 1]
            o_ref[:, 128 * pr:128 * (pr + 1)] = jnp.where(low, acc0 / l0, acc1 / l1)
            head = ATT_HEADS * grp + 2 * pr
            lse = lse + _lane_put(lane, head, m0 + jnp.log(l0)) + _lane_put(lane, head + 1, m1 + jnp.log(l1))
        lse_ref[...] += lse

    return pl.pallas_call(
        body, name="attn_fwd", grid=(s // ATT_Q, groups),
        in_specs=[pl.BlockSpec((ATT_Q, width), lambda i, g: (i, g)),
                  pl.BlockSpec((s, width), lambda i, g: (0, groups + g)),
                  pl.BlockSpec((s, width), lambda i, g: (0, 2 * groups + g)),
                  pl.BlockSpec((ATT_HEADS, nk, ATT_K), lambda i, g: (g, 0, 0))],
        out_specs=[pl.BlockSpec((ATT_Q, width), lambda i, g: (i, g)), pl.BlockSpec((ATT_Q, 128), lambda i, g: (i, 0))],
        out_shape=[jax.ShapeDtypeStruct((s, A_WIDTH), F32), jax.ShapeDtypeStruct((s, 128), F32)],
        compiler_params=_params("parallel", "arbitrary"),
    )(qkv, qkv, qkv, fr)


def _attn_bwd(qkv, do, o, lse, fr):
    s = qkv.shape[0]
    nk = s // ATT_K
    scale = HEAD_DIM ** -0.5
    heads = ATT_HEADS_BWD
    width = heads * HEAD_DIM
    groups = HEADS // heads

    def body(q_ref, k_ref, v_ref, do_ref, o_ref, lse_ref, fr_ref, dq_ref, dk_ref, dv_ref, dfc_ref, dfr_ref, dk_acc, dv_acc):
        grp = pl.program_id(0)
        lane = lax.broadcasted_iota(jnp.int32, (ATT_Q, 128), 1)
        low = lane < HEAD_DIM
        low_t = lax.broadcasted_iota(jnp.int32, (128, ATT_Q), 0) < HEAD_DIM
        dk_acc[...] = jnp.zeros_like(dk_acc)
        dv_acc[...] = jnp.zeros_like(dv_acc)
        dfr_ref[...] = jnp.zeros_like(dfr_ref)

        @pl.when(grp == 0)
        def _():
            dfc_ref[...] = jnp.zeros_like(dfc_ref)

        def outer(i, carry):
            q0 = pl.multiple_of(i * ATT_Q, ATT_Q)
            rows = pl.ds(q0, ATT_Q)
            lsev = lse_ref[rows, :]
            qts, dots, qs, dos, deltas, lses = [], [], [], [], [], []
            for pr in range(heads // 2):
                pcols = slice(128 * pr, 128 * (pr + 1))
                q2, do2 = q_ref[rows, pcols] * scale, do_ref[rows, pcols]
                prod = do2 * o_ref[rows, pcols]
                deltas += [jnp.sum(jnp.where(low, prod, 0.0), axis=-1, keepdims=True),
                           jnp.sum(jnp.where(low, 0.0, prod), axis=-1, keepdims=True)]
                dob2 = do2.astype(BF16)
                qts += _split_heads(q2.astype(F32).T.astype(BF16), low_t)
                dots += _split_heads(do2.T.astype(BF16), low_t)
                qs += _split_heads(q2, low)
                dos += _split_heads(dob2, low)
                lses += [_lane_pick(lsev, lane, heads * grp + 2 * pr), _lane_pick(lsev, lane, heads * grp + 2 * pr + 1)]

            def inner(j, carry, masked):
                k0 = pl.multiple_of(j * ATT_K, ATT_K)
                krows = pl.ds(k0, ATT_K)
                out, dkt, dvt = [], [], []
                for h in range(heads):
                    pcols = slice(128 * (h // 2), 128 * (h // 2 + 1))
                    dq, dfc = carry[h]
                    k2 = k_ref[krows, pcols]
                    p = jnp.exp(_att_logits(qs[h], k2, fr_ref[h, pl.ds(j, 1), :], q0, k0, masked) - lses[h])
                    dp = lax.dot_general(dos[h], v_ref[krows, pcols], _NT, preferred_element_type=F32)
                    ds = p * (dp - deltas[h])
                    dsb = ds.astype(BF16)
                    dkt.append(jnp.dot(qts[h], dsb, preferred_element_type=F32))
                    dvt.append(jnp.dot(dots[h], p.astype(BF16), preferred_element_type=F32))
                    dfr_ref[h, pl.ds(j, 1), :] -= jnp.sum(ds, axis=0, keepdims=True)
                    out.append((dq + jnp.dot(dsb, k2, preferred_element_type=F32), dfc + (ds[:, :128] + ds[:, 128:])))
                for pr in range(heads // 2):
                    prows = slice(128 * pr, 128 * (pr + 1))
                    dk_acc[j, prows, :] += dkt[2 * pr] + dkt[2 * pr + 1]
                    dv_acc[j, prows, :] += dvt[2 * pr] + dvt[2 * pr + 1]
                return tuple(out)

            one = (jnp.zeros((ATT_Q, 128), F32), jnp.zeros((ATT_Q, 128), F32))
            done = _causal_sweep(inner, i, (one,) * heads)
            dfc = jnp.zeros((ATT_Q, 128), F32)
            for pr in range(heads // 2):
                (dq0, dfc0), (dq1, dfc1) = done[2 * pr], done[2 * pr + 1]
                dq_ref[rows, 128 * pr:128 * (pr + 1)] = (jnp.where(low, dq0, dq1) * scale).astype(BF16)
                head = heads * grp + 2 * pr
                dfc = (dfc + _lane_put(lane, head, jnp.sum(dfc0, axis=-1, keepdims=True))
                       + _lane_put(lane, head + 1, jnp.sum(dfc1, axis=-1, keepdims=True)))
            dfc_ref[rows, :] += dfc
            return carry

        lax.fori_loop(0, s // ATT_Q, outer, 0)
        for j in range(nk):
            for pr in range(heads // 2):
                prows, pcols = slice(128 * pr, 128 * (pr + 1)), slice(128 * pr, 128 * (pr + 1))
                dk_ref[ATT_K * j:ATT_K * (j + 1), pcols] = dk_acc[j, prows, :].T.astype(BF16)
                dv_ref[ATT_K * j:ATT_K * (j + 1), pcols] = dv_acc[j, prows, :].T.astype(BF16)

    part = lambda first: pl.BlockSpec((s, width), lambda g, first=first: (0, first + g))
    whole = pl.BlockSpec((s, 128), lambda g: (0, 0))
    rowv = pl.BlockSpec((heads, nk, ATT_K), lambda g: (g, 0, 0))
    return pl.pallas_call(
        body, name="attn_bwd", grid=(groups,),
        in_specs=[part(0), part(groups), part(2 * groups), part(0), part(0), whole, rowv],
        out_specs=[part(0), part(0), part(0), whole, rowv],
        out_shape=[jax.ShapeDtypeStruct((s, A_WIDTH), BF16)] * 3 + [jax.ShapeDtypeStruct((s, 128), F32), jax.ShapeDtypeStruct((HEADS, nk, ATT_K), F32)],
        scratch_shapes=[pltpu.VMEM((nk, width, ATT_K), F32), pltpu.VMEM((nk, width, ATT_K), F32)],
        compiler_params=_params("arbitrary"),
    )(qkv, qkv, qkv, do, o, lse, fr)


def _ada_fwd(c_all, w_ada, b_loc):
    depth, _, n = w_ada.shape
    tn = 512

    def body(c_ref, w_ref, b_ref, o_ref, sc_ref):
        cv = c_ref[...]
        sc = cv * jax.nn.sigmoid(cv)
        sc_ref[...] = sc
        o_ref[0] = jnp.dot(sc.astype(BF16), w_ref[0].astype(BF16), preferred_element_type=F32) + b_ref[0]

    return pl.pallas_call(
        body, name="ada_fwd", grid=(depth, n // tn),
        in_specs=[pl.BlockSpec((N_DEV, D), lambda l, j: (0, 0)), pl.BlockSpec((1, D, tn), lambda l, j: (l, 0, j)),
                  pl.BlockSpec((1, 1, tn), lambda l, j: (l, 0, j))],
        out_specs=[pl.BlockSpec((1, N_DEV, tn), lambda l, j: (l, 0, j)), pl.BlockSpec((N_DEV, D), lambda l, j: (0, 0))],
        out_shape=[jax.ShapeDtypeStruct((depth, N_DEV, n), F32), jax.ShapeDtypeStruct((N_DEV, D), F32)],
        compiler_params=_params("arbitrary", "arbitrary"),
    )(c_all, w_ada, b_loc)


def _sum_devices(gathered):
    n = gathered.shape[1]
    tn = _pick(n, (1408, 1024, 640, 512, 128))

    def body(g_ref, o_ref):
        acc = g_ref[0:8, :]
        for dev in range(1, N_DEV):
            acc = acc + g_ref[8 * dev:8 * dev + 8, :]
        o_ref[...] = acc

    return pl.pallas_call(
        body, name="sum_devices", grid=(n // tn,),
        in_specs=[pl.BlockSpec((8 * N_DEV, tn), lambda j: (0, j))], out_specs=pl.BlockSpec((8, tn), lambda j: (0, j)),
        out_shape=jax.ShapeDtypeStruct((8, n), F32), compiler_params=_params("parallel"),
    )(gathered)


def _place():
    x, y, c = lax.axis_index("x"), lax.axis_index("y"), lax.axis_index("c")
    chips = [(1 - x, y), (x, 1 - y), (1 - x, 1 - y)]
    return x, y, c, chips


def _allgather8(block, name):
    m_per, n = block.shape

    def body(x_ref, out_ref, send_sems, recv_sems, local_sem):
        x, y, c, chips = _place()
        me, sibling = (x, y, c), (x, y, 1 - c)

        def rows(px, py, pc):
            return out_ref.at[pl.ds((4 * px + 2 * py + pc) * m_per, m_per), :]

        def copy(k, blk, to, src=None):
            return pltpu.make_async_remote_copy(
                src_ref=rows(*blk) if src is None else src, dst_ref=rows(*blk),
                send_sem=send_sems.at[k], recv_sem=recv_sems.at[k], device_id=to, device_id_type=MESH)

        mine = pltpu.make_async_copy(x_ref, rows(*me), local_sem)
        mine.start()
        first = [copy(0, me, sibling, src=x_ref)]
        first += [copy(1 + j, me, (*chip, c), src=x_ref) for j, chip in enumerate(chips)]
        for cp in first:
            cp.start()
        passed = [copy(4 + j, (*chip, c), sibling) for j, chip in enumerate(chips)]
        for j, chip in enumerate(chips):
            copy(1 + j, (*chip, c), me).wait_recv()
            passed[j].start()
        copy(0, sibling, me).wait_recv()
        for j, chip in enumerate(chips):
            copy(4 + j, (*chip, 1 - c), me).wait_recv()
        for cp in first + passed:
            cp.wait_send()
        mine.wait()

    return pl.pallas_call(
        body, name=name, out_shape=jax.ShapeDtypeStruct((N_DEV * m_per, n), block.dtype),
        in_specs=[pl.BlockSpec(memory_space=pltpu.VMEM)], out_specs=pl.BlockSpec(memory_space=pltpu.VMEM),
        scratch_shapes=[pltpu.SemaphoreType.DMA((7,)), pltpu.SemaphoreType.DMA((7,)), pltpu.SemaphoreType.DMA],
        compiler_params=pltpu.CompilerParams(vmem_limit_bytes=V7X_VMEM_LIMIT),
    )(block)


_SEM = pl.BlockSpec(memory_space=pltpu.SEMAPHORE)
_DATAFLOW = pltpu.SideEffectType.DATAFLOW_SIDE_EFFECTING


def _plan_copies(plan, refs, send_sems, recv_sems):
    return [pltpu.make_async_remote_copy(src_ref=src, dst_ref=dst, send_sem=send_sems.at[i], recv_sem=recv_sems.at[i],
                                         device_id=to, device_id_type=MESH) for i, (src, dst, to) in enumerate(plan(refs))]


class _Token(NamedTuple):
    after: jax.Array
    tie: jax.Array


def _after_operand(after):
    return after.after if isinstance(after, _Token) else after


def _copies_start(bufs, plan, n_copies, after, name):
    nb = len(bufs)

    def body(*refs):
        for cp in _plan_copies(plan, refs[:nb], refs[nb + 1], refs[nb + 2]):
            cp.start()
        for token in refs[-2:]:
            token[...] = jnp.zeros_like(token)

    sem = pltpu.SemaphoreType.DMA((n_copies,))
    vmem = pl.BlockSpec(memory_space=pltpu.VMEM)
    outs = pl.pallas_call(
        body, name=name,
        out_shape=(sem, sem, *[pltpu.HBM(b.shape, b.dtype) for b in bufs], jax.ShapeDtypeStruct((8, 128), F32),
                   jax.ShapeDtypeStruct((1, 1), F32)),
        in_specs=[_HBM] * nb + [pl.BlockSpec(memory_space=pl.ANY)],
        out_specs=(_SEM, _SEM, *[_HBM] * nb, vmem, vmem),
        input_output_aliases={i: 2 + i for i in range(nb)},
        compiler_params=pltpu.CompilerParams(has_side_effects=_DATAFLOW),
    )(*[pltpu.with_memory_space_constraint(b, pltpu.HBM) for b in bufs], _after_operand(after))
    return outs[0], outs[1], list(outs[2:2 + nb]), _Token(outs[-2], outs[-1])


def _copies_wait(started, plan, after, name):
    send_sems, recv_sems, bufs, _ = started
    nb = len(bufs)

    def body(*refs):
        for cp in _plan_copies(plan, refs[:nb], refs[nb], refs[nb + 1]):
            cp.wait_send()
            cp.wait_recv()

    return list(pl.pallas_call(
        body, name=name, out_shape=tuple(pltpu.HBM(b.shape, b.dtype) for b in bufs),
        in_specs=[_HBM] * nb + [_SEM, _SEM, pl.BlockSpec(memory_space=pl.ANY)], out_specs=tuple([_HBM] * nb),
        input_output_aliases={i: i for i in range(nb)},
        compiler_params=pltpu.CompilerParams(has_side_effects=_DATAFLOW),
    )(*bufs, send_sems, recv_sems, _after_operand(after)))


def _half_rows(ref, axis, c):
    half = ref.shape[axis] // 2
    return pl.ds(c * half, half)


def _plan_gather_ici(refs):
    n = len(refs) // 2
    x, y, c, chips = _place()
    out = []
    for a in range(n):
        rows = _half_rows(refs[a], 0, c)
        out += [(refs[a].at[rows], refs[n + a].at[2 * x + y, rows], (*chip, c)) for chip in chips]
        out.append((refs[a], refs[n + a].at[2 * x + y], (x, y, 1 - c)))
    return out


def _plan_gather_d2d(refs):
    x, y, c, chips = _place()
    out = []
    for ref in refs:
        rows = _half_rows(ref, 1, c)
        for px, py in chips:
            landed = ref.at[2 * px + py, rows]
            out.append((landed, landed, (x, y, 1 - c)))
    return out


def _plan_rs_sibling(refs):
    n = len(refs) // 2
    x, y, c, _ = _place()
    return [(refs[a].at[pl.ds(0, N_CHIPS), _half_rows(refs[a], 1, 1 - c)], refs[n + a], (x, y, 1 - c)) for a in range(n)]


def _plan_rs_chips(refs):
    n = len(refs) // 2
    x, y, c, chips = _place()
    return [(refs[a].at[2 * px + py], refs[n + a].at[k], (px, py, c)) for a in range(n) for k, (px, py) in enumerate(chips)]


def _plan_rs_share(layer):
    def plan(refs):
        x, y, c, _ = _place()
        return [(ref.at[layer, _half_rows(ref, 1, c)], ref.at[layer, _half_rows(ref, 1, c)], (x, y, 1 - c)) for ref in refs]
    return plan


def _chip_sum(g, other, sel, name):
    _, half, cdim = other.shape
    tr = _pick(half, (512, 256, 128, 64))
    per = half // tr

    def body(sel_ref, g_ref, t_ref, wire_ref, own_ref):
        total = g_ref[0] + t_ref[0]
        wire_ref[0] = total.astype(BF16)

        @pl.when(pl.program_id(1) == sel_ref[1])
        def _():
            own_ref[...] = total

    blk = pl.BlockSpec((1, tr, cdim), lambda i, p, sel_ref: (p, i, 0))
    return pl.pallas_call(
        body, name=name,
        grid_spec=pltpu.PrefetchScalarGridSpec(
            num_scalar_prefetch=1, grid=(per, N_CHIPS),
            in_specs=[pl.BlockSpec((1, tr, cdim), lambda i, p, sel_ref: (p, sel_ref[0] * per + i, 0)), blk],
            out_specs=[blk, pl.BlockSpec((tr, cdim), lambda i, p, sel_ref: (i, 0))]),
        out_shape=[jax.ShapeDtypeStruct(other.shape, BF16), jax.ShapeDtypeStruct((half, cdim), F32)],
        compiler_params=_params("parallel", "arbitrary"),
    )(sel, g, other)


def _final_sum(own, recv, sel, layer, into, name):
    half, cdim = own.shape
    tr = _pick(half, (512, 256, 128, 64))
    per = half // tr

    def body(sel_ref, own_ref, r0_ref, r1_ref, r2_ref, *rest):
        rest[-1][...] = ((own_ref[...] + r0_ref[0].astype(F32)) + r1_ref[0].astype(F32)) + r2_ref[0].astype(F32)

    part = lambda k: pl.BlockSpec((1, tr, cdim), lambda i, sel_ref, k=k: (k, i, 0))
    prior = [] if into is None else [into]
    return pl.pallas_call(
        body, name=name,
        grid_spec=pltpu.PrefetchScalarGridSpec(
            num_scalar_prefetch=1, grid=(per,),
            in_specs=[pl.BlockSpec((tr, cdim), lambda i, sel_ref: (i, 0)), part(0), part(1), part(2)]
            + [pl.BlockSpec(memory_space=pl.ANY)] * len(prior),
            out_specs=pl.BlockSpec((None, tr, cdim), lambda i, sel_ref: (layer, sel_ref[0] * per + i, 0))),
        out_shape=jax.ShapeDtypeStruct((DEPTH, 2 * half, cdim), F32),
        input_output_aliases={5: 0} if prior else {}, compiler_params=_params("parallel"),
    )(sel, own, recv, recv, recv, *prior)


def _row(v):
    return v.reshape(1, -1)


_BR_A, _BR_B, _BR_C = (0, A_WIDTH), (A_WIDTH, POOL_WIDTH), (A_WIDTH + POOL_WIDTH, CONV_WIDTH)


def _tie(v, token):
    return v if token is None else v + token.tie


def _no_hook(point, after, ready=None):
    return None


def _layer_fwd(x, w, mod, hook=_no_hook):
    s = x.shape[0]
    mod3 = mod.reshape(6, 1, D)
    h = _modnorm_fwd(x, _row(w["g_mix_pre"]), (mod3, 0), (mod3, 1), "mix_pre_fwd")
    z = _mm(h, w["w_all"], name="mm_in")
    qkv = z[:, Z_QKV:Z_PC].astype(BF16)
    fl = z[:, Z_FL:Z_COLS]
    cum = _cumf_fwd(fl, w["b_f_pad"])
    fr = cum[:, :HEADS].T.reshape(HEADS, s // ATT_K, ATT_K)
    br_a, lse = _attn_fwd(qkv, fr)
    br_b, br_c = _poolconv_fwd(z, w["w_pool_bd"], _tie(_row(w["pool_scale"]), hook("attn", lse)), w["conv_w"])
    hook("pool", br_b)
    wbr = w["w_branch"]
    pa = _mm(br_a, wbr, b_rows=_BR_A, name="mm_br_a")
    pb = _mm(br_b, wbr, b_rows=_BR_B, name="mm_br_b")
    pc = _mm(br_c, wbr, b_rows=_BR_C, name="mm_br_c")
    merged = _merge_fwd(z, pa, pb, pc)
    y = _mm(merged, w["w_out"], name="mm_out")
    x1, h2 = _post_pre_fwd(x, y, _row(w["g_mix_post"]), (mod3, 2), _row(w["g_ff_pre"]), (mod3, 3), (mod3, 4), "mix_post_ff_pre_fwd")
    a, r = _mm(h2, w["w_ff1"], b_split=N_CHIPS, epilogue=_relu2_fwd, out_dtype=(F32, BF16), name="mm_ff1")
    y2 = _mm(r, w["w_ff2"], name="mm_ff2")
    x2 = _post_fwd(x1, y2, _tie(_row(w["g_ff_post"]), hook("ff_post", y2)), (mod3, 5), "ff_post_fwd")
    hook("end", x2)
    saved = dict(x=x, h=h, z=z, qkv=qkv, fl=fl, fr=fr, lse=lse, br_a=br_a, br_b=br_b, br_c=br_c, pa=pa, pb=pb, pc=pc,
                 merged=merged, y=y, x1=x1, h2=h2, a=a, r=r, y2=y2)
    return x2, saved


def _layer_bwd(dx2, sv, w, mod, hook=_no_hook):
    s = dx2.shape[0]
    mod3 = mod.reshape(6, 1, D)
    dy2, sum_ff_post = _post_bwd(dx2, sv["y2"], _row(w["g_ff_post"]), (mod3, 5), "ff_post_bwd")
    (da,) = _mm(dy2, w["w_ff2"], tb=True, epilogue=_relu2_bwd, extras=(sv["a"],), out_dtype=(BF16,), name="mm_ff2_dx")
    d_w_ff2 = _mm(sv["r"], dy2, ta=True, name="mm_ff2_dw")
    dh2 = _mm(da, w["w_ff1"], tb=True, b_split=N_CHIPS, name="mm_ff1_dx")
    d_w_ff1 = _mm(sv["h2"], da, ta=True, out_split=N_CHIPS, name="mm_ff1_dw")
    g_ff_pre = _tie(_row(w["g_ff_pre"]), hook("ff_pre", dh2, dict(w_ff1=d_w_ff1, w_ff2=d_w_ff2)))
    dx1, dy, sum_mid = _pre_post_bwd(dh2, sv["x1"], dx2, g_ff_pre, (mod3, 4), sv["y"], _row(w["g_mix_post"]), (mod3, 2), "ff_pre_mix_post_bwd")
    sum_ff_pre, sum_mix_post = sum_mid, sum_mid[3:]
    dmerged = _mm(dy, w["w_out"], tb=True, name="mm_out_dx")
    d_w_out = _mm(sv["merged"], dy, ta=True, name="mm_out_dw")
    dz, dpa, dpb, dpc = _merge_bwd(dmerged, sv["z"], sv["pa"], sv["pb"], sv["pc"])
    wbr = w["w_branch"]
    dbr_a = _mm(dpa, wbr, tb=True, b_rows=_BR_A, name="mm_br_a_dx")
    dbr_b = _mm(dpb, wbr, tb=True, b_rows=_BR_B, name="mm_br_b_dx")
    dbr_c = _mm(dpc, wbr, tb=True, b_rows=_BR_C, name="mm_br_c_dx")
    d_w_branch = jnp.concatenate([_mm(sv["br_a"], dpa, ta=True, name="mm_br_a_dw"), _mm(sv["br_b"], dpb, ta=True, name="mm_br_b_dw"),
                                  _mm(sv["br_c"], dpc, ta=True, name="mm_br_c_dw")], axis=0)

    dq, dk, dv, dfc, dfr = _attn_bwd(sv["qkv"], dbr_a, sv["br_a"], sv["lse"], sv["fr"])
    dcum = dfc + jnp.pad(dfr.reshape(HEADS, s).T, ((0, 0), (0, 128 - HEADS)))
    dfl, sum_bf = _cumf_bwd(dcum, sv["fl"], _tie(w["b_f_pad"], hook("cumf", dfc)))
    dpc_z, d_wbd, sum_ps, sum_cw = _poolconv_bwd(dbr_b, dbr_c, sv["z"], w["w_pool_bd"], _row(w["pool_scale"]), w["conv_w"])
    for at, part in ((Z_QKV, dq), (Z_QKV + A_WIDTH, dk), (Z_QKV + 2 * A_WIDTH, dv), (Z_PC, dpc_z), (Z_FL, dfl)):
        dz = lax.dynamic_update_slice(dz, part, (0, at))
    dh = _mm(dz, w["w_all"], tb=True, name="mm_in_dx")
    d_w_all = _mm(sv["h"], dz, ta=True, name="mm_in_dw")
    hook("mix_pre", dh)
    dx, sum_mix_pre = _modnorm_bwd(dh, sv["x"], dx1, _row(w["g_mix_pre"]), (mod3, 1), "mix_pre_bwd")

    dmod = jnp.stack([sum_mix_pre[0], sum_mix_pre[1], sum_mix_post[0], sum_ff_pre[0], sum_ff_pre[1], sum_ff_post[0]])
    d_w_in = _w_in_shards(d_w_all)
    d_w_pool = jnp.stack([d_wbd[64 * g:64 * g + 64, 64 * g:64 * g + 64] for g in range(4)])
    big = dict(w_in=d_w_in, w_branch=d_w_branch, w_out=d_w_out, w_ff1=d_w_ff1, w_ff2=d_w_ff2)
    small = dict(g_mix_pre=sum_mix_pre[2], g_mix_post=sum_mix_post[1], g_ff_pre=sum_ff_pre[2], g_ff_post=sum_ff_post[1],
                 b_f=sum_bf[0, :HEADS], w_pool=d_w_pool, pool_scale=sum_ps[0], conv_w=sum_cw[0:3])
    return dx, dmod, big, small


_QKV_END, _FL_END, _PC_END = 3 * A_WIDTH, 3 * A_WIDTH + HEADS, 3 * A_WIDTH + HEADS + POOL_WIDTH + 3 * CONV_WIDTH
_W_IN_GROUPS = ((_PC_END, IN_COLS, Z_GL), (0, _QKV_END, Z_QKV), (_FL_END, _PC_END, Z_PC), (_QKV_END, _FL_END, Z_FL))
_SHARD_COLS = IN_COLS // N_CHIPS


def _w_all_from_shards(blocks):
    pieces = []
    for lo, hi, _ in _W_IN_GROUPS:
        for p in range(N_CHIPS):
            a, b = max(lo, p * _SHARD_COLS), min(hi, (p + 1) * _SHARD_COLS)
            if a < b:
                pieces.append(blocks[p][:, a - p * _SHARD_COLS:b - p * _SHARD_COLS])
    pieces.append(jnp.zeros((D, Z_COLS - IN_COLS), blocks.dtype))
    return jnp.concatenate(pieces, axis=1)


def _w_in_shards(d_w_all):
    blocks = []
    for p in range(N_CHIPS):
        pieces = []
        for lo, hi, at in sorted(_W_IN_GROUPS):
            a, b = max(lo, p * _SHARD_COLS), min(hi, (p + 1) * _SHARD_COLS)
            if a < b:
                pieces.append(d_w_all[:, at + a - lo:at + b - lo])
        blocks.append(jnp.concatenate(pieces, axis=1))
    return jnp.stack(blocks)


def _full_layer_weights(w_in_blocks, w_branch, w_out, w_ff1, w_ff2, g_mix_pre, g_mix_post, g_ff_pre, g_ff_post, b_f, w_pool, pool_scale, conv_w):
    w_all = _w_all_from_shards(w_in_blocks)
    wbd = (w_pool[:, :, None, :] * jnp.eye(4, dtype=F32)[:, None, :, None]).reshape(POOL_WIDTH, POOL_WIDTH)
    return dict(w_all=w_all, w_branch=w_branch, w_out=w_out, w_ff1=w_ff1, w_ff2=w_ff2, g_mix_pre=g_mix_pre, g_mix_post=g_mix_post,
                g_ff_pre=g_ff_pre, g_ff_post=g_ff_post, b_f_pad=jnp.pad(b_f, (0, 128 - HEADS)).reshape(1, 128), w_pool_bd=wbd,
                pool_scale=pool_scale, conv_w=conv_w)


class _NoComm:
    def layer_weights(self, l):
        raise NotImplementedError

    def fwd_hook(self, l):
        return _no_hook

    def bwd_hook(self, l):
        return _no_hook

    def grads_ready(self, l, big):
        return None


class _Layers(_NoComm):
    def __init__(self, layers):
        self.layers = layers

    def layer_weights(self, l):
        return self.layers[l]


def _local_step(x, target, mods, comm):
    saved, weights = [], []
    act = x
    for l in range(DEPTH):
        weights.append(comm.layer_weights(l))
        act, sv = _layer_fwd(act, weights[l], mods[l], comm.fwd_hook(l))
        saved.append(sv)
    dact, sq = _loss_head(act, target)
    loss = sq[0, 0] * (0.5 / D)
    dmods, bigs, smalls = [None] * DEPTH, [None] * DEPTH, [None] * DEPTH
    token = None
    for l in reversed(range(DEPTH)):
        dact, dmods[l], bigs[l], smalls[l] = _layer_bwd(dact, saved[l], weights[l], _tie(mods[l], token), comm.bwd_hook(l))
        token = comm.grads_ready(l, bigs[l])
    return loss, dact, jnp.stack(dmods), bigs, smalls


_BIG = ("w_in", "w_branch", "w_out", "w_ff1", "w_ff2")


class _GatherJob:
    def __init__(self, tag, shards, after):
        self.tag, self.n = tag, len(shards)
        lands = [lax.empty((N_CHIPS,) + s.shape, s.dtype) for s in shards]
        self.state = _copies_start(list(shards) + lands, _plan_gather_ici, 4 * self.n, after, "gather_ici_start_" + tag)
        self.token = self.state[3]

    def pass_on(self, after):
        bufs = _copies_wait(self.state, _plan_gather_ici, after, "gather_ici_wait_" + self.tag)
        self.state = _copies_start(bufs[self.n:], _plan_gather_d2d, 3 * self.n, bufs[0], "gather_d2d_start_" + self.tag)
        self.token = self.state[3]
        return self.token

    def done(self, after):
        return _copies_wait(self.state, _plan_gather_d2d, after, "gather_d2d_wait_" + self.tag)


class _ReduceJob:
    def __init__(self, tag, names, grads, sel, after, layer, into=None):
        self.tag, self.names, self.n, self.sel, self.layer, self.into = tag, names, len(names), sel, layer, into or {}
        lands = [lax.empty((N_CHIPS, g.shape[1] // 2, g.shape[2]), F32) for g in grads]
        self.state = _copies_start(list(grads) + lands, _plan_rs_sibling, self.n, after, "rs_sibling_start_" + tag)
        self.token = self.state[3]

    def chip_sums(self, after):
        bufs = _copies_wait(self.state, _plan_rs_sibling, after, "rs_sibling_wait_" + self.tag)
        wires, self.owns = zip(*[_chip_sum(bufs[i], bufs[self.n + i], self.sel, "rs_chip_sum_" + name) for i, name in enumerate(self.names)])
        lands = [lax.empty((3,) + w.shape[1:], BF16) for w in wires]
        self.state = _copies_start(list(wires) + lands, _plan_rs_chips, 3 * self.n, self.owns[0], "rs_chips_start_" + self.tag)
        self.token = self.state[3]
        return self.token

    def final_sums(self, after):
        bufs = _copies_wait(self.state, _plan_rs_chips, after, "rs_chips_wait_" + self.tag)
        sums = [_final_sum(self.owns[i], bufs[self.n + i], self.sel, self.layer, self.into.get(name), "rs_final_" + name)
                for i, name in enumerate(self.names)]
        self.state = _copies_start(sums, _plan_rs_share(self.layer), self.n, sums[0], "rs_share_start_" + self.tag)
        self.token = self.state[3]
        return self.token

    def done(self, after):
        return dict(zip(self.names, _copies_wait(self.state, _plan_rs_share(self.layer), after, "rs_share_wait_" + self.tag)))


def _chip_blocks(g):
    return g if g.ndim == 3 else g.reshape(N_CHIPS, -1, g.shape[1])


class _StepComm(_NoComm):
    def __init__(self, shards, sel, after):
        self.sel = sel
        self.small, self.grads, self.jobs = None, {}, {}
        self.jobs["in0"] = _GatherJob("in0", shards[0][:1], after)
        self.jobs["rest0"] = _GatherJob("rest0", shards[0][1:], self.jobs["in0"].token)
        self.jobs["all1"] = _GatherJob("all1", shards[1], self.jobs["rest0"].token)

    def layer_weights(self, l):
        if l == 0:
            job = self.jobs["in0"]
            (g_in,) = job.done(job.pass_on(self.jobs["all1"].token))
            self.weights0 = _full_layer_weights(g_in, None, None, None, None, *self.small[0])
            return self.weights0
        g_in, g_br, g_out, g_f1, g_f2 = self.landed1
        return _full_layer_weights(g_in, g_br.reshape(D, D), g_out.reshape(D, D), g_f1, g_f2.reshape(D_FF, D), *self.small[1])

    def fwd_hook(self, l):
        if l != 0:
            return _no_hook

        def hook(point, after, ready=None):
            if point == "attn":
                return self.jobs["rest0"].pass_on(after)
            if point == "ff_post":
                return self.jobs["all1"].pass_on(after)
            if point == "pool":
                g_br, g_out, g_f1, g_f2 = self.jobs["rest0"].done(after)
                self.weights0.update(w_branch=g_br.reshape(D, D), w_out=g_out.reshape(D, D), w_ff1=g_f1, w_ff2=g_f2.reshape(D_FF, D))
            if point == "end":
                self.landed1 = self.jobs["all1"].done(after)
            return None
        return hook

    def bwd_hook(self, l):
        if l != 0:
            return _no_hook

        def hook(point, after, ready=None):
            jobs = self.jobs
            if point == "ff_pre":
                token = jobs["rs1"].chip_sums(after)
                jobs["rs0_ff"] = _ReduceJob("0_ff", ("w_ff1", "w_ff2"), [_chip_blocks(ready[n]) for n in ("w_ff1", "w_ff2")], self.sel, token, 0)
                return jobs["rs0_ff"].token
            if point == "cumf":
                return jobs["rs0_ff"].chip_sums(jobs["rs1"].final_sums(after))
            self.layer1 = jobs["rs1"].done(after)
            jobs["rs0_ff"].into = self.layer1
            return None
        return hook

    def grads_ready(self, l, big):
        if l == 1:
            self.jobs["rs1"] = _ReduceJob("1", _BIG, [_chip_blocks(big[n]) for n in _BIG], self.sel, self.sel, 1)
            return self.jobs["rs1"].token
        names = ("w_in", "w_branch", "w_out")
        self.jobs["rs0_mix"] = _ReduceJob("0_mix", names, [_chip_blocks(big[n]) for n in names], self.sel, self.sel, 0, self.layer1)
        return self.jobs["rs0_mix"].token

    def finish_sums(self, after):
        jobs = self.jobs
        token = jobs["rs0_mix"].chip_sums(after)
        return jobs["rs0_ff"].final_sums(token)

    def finish_ff(self, after):
        self.grads.update(self.jobs["rs0_ff"].done(after))

    def finish_mix(self, after):
        job = self.jobs["rs0_mix"]
        self.grads.update(job.done(job.final_sums(after)))


_SMALL = ("g_mix_pre", "g_mix_post", "g_ff_pre", "g_ff_post", "b_f", "w_pool", "pool_scale", "conv_w")


def _w_in_view(t):
    return t.reshape(DEPTH, D // 128, 128, _SHARD_COLS).transpose(3, 1, 0, 2).reshape(_SHARD_COLS * (D // 128) * DEPTH, 128)


def _w_in_unview(t):
    return t.reshape(_SHARD_COLS, D // 128, DEPTH, 128).transpose(2, 1, 3, 0).reshape(DEPTH, D, _SHARD_COLS)


def _pack(parts, rows=8):
    flat = jnp.concatenate([p.reshape(-1) for p in parts])
    width = -(-flat.shape[0] // (rows * 128)) * 128
    return jnp.pad(flat, (0, rows * width - flat.shape[0])).reshape(rows, width)


def _unpack(packed, like):
    flat = packed.reshape(-1)
    out, at = [], 0
    for ref in like:
        out.append(flat[at:at + ref.size].reshape(ref.shape))
        at += ref.size
    return out


def kernel(x, c, w_ada, b_ada, g_mix_pre, g_mix_post, g_ff_pre, g_ff_post, w_in, b_f, w_pool, pool_scale, conv_w, w_branch, w_out, w_ff1, w_ff2, loss_target, m_w_ada, m_b_ada, m_g_mix_pre, m_g_mix_post, m_g_ff_pre, m_g_ff_post, m_w_in, m_b_f, m_w_pool, m_pool_scale, m_conv_w, m_w_branch, m_w_out, m_w_ff1, m_w_ff2, v_w_ada, v_b_ada, v_g_mix_pre, v_g_mix_post, v_g_ff_pre, v_g_ff_post, v_w_in, v_b_f, v_w_pool, v_pool_scale, v_conv_w, v_w_branch, v_w_out, v_w_ff1, v_w_ff2):
    xi, yi, ci = lax.axis_index("x"), lax.axis_index("y"), lax.axis_index("c")
    chip = 2 * xi + yi
    dev = 2 * chip + ci
    n_ada = w_ada.shape[2]

    first = jnp.zeros((8, D + 384), F32).at[0, :D].set(c[0]).at[0, D:].set(conv_w.reshape(-1))
    got = _allgather8(first, "gather_cond").reshape(N_DEV, 8, D + 384)[:, 0]
    c_all = got[:, :D]
    conv_full = got[0::2, D:].reshape(N_CHIPS, DEPTH, 3, CONV_WIDTH // N_CHIPS).transpose(1, 2, 0, 3).reshape(DEPTH, 3, CONV_WIDTH)

    b_loc = lax.dynamic_slice_in_dim(b_ada, chip * n_ada, n_ada, axis=1).reshape(DEPTH, 1, n_ada)
    mod_cols, silu_c = _ada_fwd(c_all, w_ada, b_loc)
    got = _allgather8(mod_cols.reshape(DEPTH * N_DEV, n_ada), "gather_mod").reshape(N_DEV, DEPTH, N_DEV, n_ada)[0::2]
    mod_all = got.transpose(1, 2, 0, 3).reshape(DEPTH, N_DEV, 6, D)
    mods = lax.dynamic_index_in_dim(mod_all, dev, axis=1, keepdims=False)

    comm = _StepComm([[w[l].astype(BF16) for w in (w_in, w_branch, w_out, w_ff1, w_ff2)] for l in range(DEPTH)],
                     jnp.stack([ci, chip]).astype(jnp.int32), mods)
    comm.small = [(g_mix_pre[l], g_mix_post[l], g_ff_pre[l], g_ff_post[l], b_f[l], w_pool[l], pool_scale[l], conv_full[l]) for l in range(DEPTH)]
    loss_part, grad_x, dmods, bigs, smalls = _local_step(x[0], loss_target[0], mods, comm)

    small_parts = [smalls[l][name] for name in _SMALL for l in range(DEPTH)] + [loss_part.reshape(1)]
    packed = _tie(_pack([dmods] + small_parts), comm.jobs["rs0_mix"].token)
    gathered = _allgather8(packed, "gather_small")
    dmod_all = gathered.reshape(N_DEV, -1)[:, :dmods.size].reshape(N_DEV, DEPTH, 6 * D)
    summed = _unpack(_sum_devices(gathered), [dmods] + small_parts)
    grad_b_ada = summed[0].reshape(DEPTH, 6 * D)
    loss = summed[-1][0]
    small_grads = {name: jnp.stack(summed[1 + 2 * i:3 + 2 * i]) for i, name in enumerate(_SMALL)}
    small_grads["conv_w"] = lax.dynamic_slice_in_dim(small_grads["conv_w"], chip * (CONV_WIDTH // N_CHIPS), CONV_WIDTH // N_CHIPS, axis=2)

    dmod_loc = lax.dynamic_slice_in_dim(dmod_all.transpose(1, 0, 2), chip * n_ada, n_ada, axis=2)
    tail_token = comm.finish_sums(grad_b_ada)
    silu_pad = _tie(jnp.pad(silu_c, ((0, 128 - N_DEV), (0, 0))), tail_token)
    dmod_pad = jnp.pad(dmod_loc.transpose(1, 0, 2).reshape(N_DEV, DEPTH * n_ada), ((0, 128 - N_DEV), (0, 0)))
    grad_w_ada = _mm(silu_pad, dmod_pad, ta=True, out_split=DEPTH, name="mm_ada_dw")

    grads = dict(w_ada=grad_w_ada, b_ada=grad_b_ada, **small_grads)
    weights = dict(w_ada=w_ada, b_ada=b_ada, g_mix_pre=g_mix_pre, g_mix_post=g_mix_post, g_ff_pre=g_ff_pre, g_ff_post=g_ff_post, w_in=w_in,
                   b_f=b_f, w_pool=w_pool, pool_scale=pool_scale, conv_w=conv_w, w_branch=w_branch, w_out=w_out, w_ff1=w_ff1, w_ff2=w_ff2)
    m_in = dict(w_ada=m_w_ada, b_ada=m_b_ada, g_mix_pre=m_g_mix_pre, g_mix_post=m_g_mix_post, g_ff_pre=m_g_ff_pre, g_ff_post=m_g_ff_post,
                w_in=m_w_in, b_f=m_b_f, w_pool=m_w_pool, pool_scale=m_pool_scale, conv_w=m_conv_w, w_branch=m_w_branch, w_out=m_w_out,
                w_ff1=m_w_ff1, w_ff2=m_w_ff2)
    v_in = dict(w_ada=v_w_ada, b_ada=v_b_ada, g_mix_pre=v_g_mix_pre, g_mix_post=v_g_mix_post, g_ff_pre=v_g_ff_pre, g_ff_post=v_g_ff_post,
                w_in=v_w_in, b_f=v_b_f, w_pool=v_w_pool, pool_scale=v_pool_scale, conv_w=v_conv_w, w_branch=v_w_branch, w_out=v_w_out,
                w_ff1=v_w_ff1, w_ff2=v_w_ff2)
    order = ("w_ada", "b_ada", "g_mix_pre", "g_mix_post", "g_ff_pre", "g_ff_post", "w_in", "b_f", "w_pool", "pool_scale", "conv_w",
             "w_branch", "w_out", "w_ff1", "w_ff2")
    delta, new_m, new_v = {}, {}, {}
    tiny = ("b_ada",) + _SMALL
    tiny_g = [_tie(grads[tiny[0]], tail_token)] + [grads[name] for name in tiny[1:]]
    res = _adamw_many([weights[name] for name in tiny], tiny_g, [m_in[name] for name in tiny], [v_in[name] for name in tiny], "adamw_small")
    for out, vals in zip((delta, new_m, new_v), res):
        out.update(zip(tiny, vals))
    delta["w_ada"], new_m["w_ada"], new_v["w_ada"] = _adamw(w_ada, grad_w_ada, m_w_ada, v_w_ada, "adamw_w_ada")
    comm.finish_ff(delta["w_ada"][0, :8, :128] + delta["b_ada"][0, :128])
    for name in ("w_ff1", "w_ff2", "w_in", "w_branch", "w_out"):
        if name == "w_in":
            comm.finish_mix(delta["w_ff2"][0, :8, :128])
        grads[name] = comm.grads[name]
        if name == "w_in":
            g_view = lax.optimization_barrier(_w_in_view(grads[name]))
            res = _adamw(_w_in_view(w_in), g_view, _w_in_view(m_w_in), _w_in_view(v_w_in), "adamw_w_in")
            grads[name], delta[name], new_m[name], new_v[name] = [_w_in_unview(t) for t in (g_view, *res)]
        else:
            delta[name], new_m[name], new_v[name] = _adamw(weights[name], grads[name], m_in[name], v_in[name], "adamw_" + name)

    return (loss, grad_x[None], *[grads[n] for n in order], *[delta[n] for n in order], *[new_m[n] for n in order],
            *[new_v[n] for n in order])
```

```python
from typing import NamedTuple

import jax
import jax.numpy as jnp
from jax import lax
from jax.experimental import pallas as pl
from jax.experimental.pallas import tpu as pltpu

F32 = jnp.float32
BF16 = jnp.bfloat16
MESH = pl.DeviceIdType.MESH

D = 1024
DEPTH = 2
HEADS = 8
HEAD_DIM = 64
A_WIDTH = 512
POOL_WIDTH = 256
CONV_WIDTH = 256
D_FF = 4096
IN_COLS = 5640
Z_GL, Z_QKV, Z_PC, Z_FL, Z_COLS = 0, 3072, 4608, 5632, 5760
RMS_EPS = 1e-6
NEG_INF = -1e30
ROW_TILE = 512
EW_ROWS = 256
N_CHIPS = 4
N_DEV = 8
V7X_VMEM_LIMIT = 48 * 1024 * 1024

ADAM_LR = 0.001
ADAM_B1 = 0.9
ADAM_B2 = 0.999
ADAM_EPS = 1e-08
ADAM_WD = 0.01
ADAM_STEP = 10

_HBM = pl.BlockSpec(memory_space=pltpu.HBM)


def _params(*sem):
    return pltpu.CompilerParams(dimension_semantics=sem, vmem_limit_bytes=V7X_VMEM_LIMIT)


def _pick(dim, cands):
    for cand in cands:
        if dim % cand == 0:
            return cand
    return dim


MM_TILE_BUDGET = 39 * 1024 * 1024


def _mm_tiles(m, n, k, k_unit, tn, a_size, b_size, out_size):
    for tk in (k_unit, 2048, 1152, 1024, 640, 512, 256, 128):
        if k_unit % tk:
            continue
        for tm in (2048, 1024, 512, 256, 128):
            if m % tm or ((m // tm) * (n // tn) < 2 and tm > 512):
                continue
            need = 2 * (tm * tk * a_size + tk * tn * b_size + tm * tn * out_size) + (0 if tk == k else 4 * tm * tn)
            if need <= MM_TILE_BUDGET and (tk == k_unit or tm >= 512):
                return tm, tk
    return 128, 128


def _mm(a, b, *, ta=False, tb=False, b_rows=None, b_split=1, out_split=1, out_dtype=F32, epilogue=None, extras=(), name):
    (k, m) = a.shape if ta else a.shape[::-1]
    b_row0, b_rows = (0, b.shape[-2]) if b_rows is None else b_rows
    b_cols = b.shape[-1] * b_split
    (n, k2) = (b_rows, b_cols) if tb else (b_cols, b_rows)
    assert k == k2, (a.shape, b.shape, ta, tb)
    n_unit = n // (out_split * (1 if tb else b_split))
    k_unit = k // (b_split if tb else 1)
    tn = _pick(n_unit, (1024, 1152, 768, 640, 512, 256, 128))
    tm, tk = _mm_tiles(m, n, k, k_unit, tn, a.dtype.itemsize, b.dtype.itemsize,
                       sum(jnp.dtype(dt).itemsize for dt in out_dtype) + 4 * len(extras) if epilogue else jnp.dtype(out_dtype).itemsize)
    nk = k // tk
    dims = (((0 if ta else 1,), (1 if tb else 0,)), ((), ()))

    def dot(a_ref, b_ref):
        b_val = b_ref[0] if b_split > 1 else b_ref[...]
        return lax.dot_general(a_ref[...].astype(BF16), b_val.astype(BF16), dims, preferred_element_type=F32)

    n_extra = len(extras)
    assert epilogue is None or out_split == 1

    def put(refs, val):
        if epilogue is not None:
            for o_ref, res in zip(refs[n_extra:], epilogue(val, *[r[...] for r in refs[:n_extra]])):
                o_ref[...] = res.astype(o_ref.dtype)
        elif out_split > 1:
            refs[0][0] = val.astype(refs[0].dtype)
        else:
            refs[0][...] = val.astype(refs[0].dtype)

    def body_single(a_ref, b_ref, *refs):
        put(refs, dot(a_ref, b_ref))

    def body_acc(a_ref, b_ref, *refs):
        kk = pl.program_id(2)
        acc_ref = refs[-1]

        @pl.when(kk == 0)
        def _():
            acc_ref[...] = jnp.zeros_like(acc_ref)

        acc_ref[...] += dot(a_ref, b_ref)

        @pl.when(kk == nk - 1)
        def _():
            put(refs[:-1], acc_ref[...])

    a_spec = pl.BlockSpec((tk, tm), lambda i, j, kk: (kk, i)) if ta else pl.BlockSpec((tm, tk), lambda i, j, kk: (i, kk))
    if b_split == 1:
        off = b_row0 // (tn if tb else tk)
        assert off * (tn if tb else tk) == b_row0
        b_spec = pl.BlockSpec((tn, tk), lambda i, j, kk: (j + off, kk)) if tb else pl.BlockSpec((tk, tn), lambda i, j, kk: (kk + off, j))
    elif tb:
        per = k_unit // tk
        b_spec = pl.BlockSpec((1, tn, tk), lambda i, j, kk: (kk // per, j, kk % per))
    else:
        per = n // b_split // tn
        b_spec = pl.BlockSpec((1, tk, tn), lambda i, j, kk: (j // per, kk, j % per))
    if out_split == 1:
        o_spec = pl.BlockSpec((tm, tn), lambda i, j, kk: (i, j))
        o_shape = None if epilogue is not None else jax.ShapeDtypeStruct((m, n), out_dtype)
    else:
        per_o = n // out_split // tn
        o_spec = pl.BlockSpec((1, tm, tn), lambda i, j, kk: (j // per_o, i, j % per_o))
        o_shape = jax.ShapeDtypeStruct((out_split, m, n // out_split), out_dtype)
    if epilogue is not None:
        o_shape = [jax.ShapeDtypeStruct((m, n), dt) for dt in out_dtype]
        o_spec = [o_spec] * len(out_dtype)
    return pl.pallas_call(
        body_single if nk == 1 else body_acc, name=name, grid=(m // tm, n // tn, nk),
        in_specs=[a_spec, b_spec] + [pl.BlockSpec((tm, tn), lambda i, j, kk: (i, j))] * n_extra, out_specs=o_spec, out_shape=o_shape,
        scratch_shapes=[] if nk == 1 else [pltpu.VMEM((tm, tn), F32)],
        compiler_params=_params("parallel", "parallel", "arbitrary"),
    )(a, b, *extras)


def _ew(fn, ins, out_dtypes, name, tc=None):
    shape = ins[0].shape
    lead, (rows, cols) = shape[:-2], shape[-2:]
    tc = cols if tc is None else tc
    if tc > 1024:
        tr = _pick(rows, (EW_ROWS, 128, 8))
    elif tc > 128:
        tr = _pick(rows, (2 * EW_ROWS, EW_ROWS, 128, 8))
    else:
        tr = _pick(rows, (4096, 2256, 2048, 1024, EW_ROWS, 8))
    n_in = len(ins)

    def body(*refs):
        res = fn(*[r[...] for r in refs[:n_in]])
        for o_ref, val in zip(refs[n_in:], res):
            o_ref[...] = val.astype(o_ref.dtype)

    if lead:
        spec = pl.BlockSpec((None, tr, tc), lambda l, i, j: (l, i, j))
    else:
        spec = pl.BlockSpec((tr, tc), lambda i, j: (i, j))
    return pl.pallas_call(
        body, name=name, grid=lead + (rows // tr, cols // tc),
        in_specs=[spec] * n_in, out_specs=[spec] * len(out_dtypes),
        out_shape=[jax.ShapeDtypeStruct(shape, dt) for dt in out_dtypes],
        compiler_params=_params(*(["parallel"] * (len(lead) + 2))),
    )(*ins)


def _relu2_fwd(a):
    r = jnp.maximum(a, 0.0)
    return a, r * r


def _relu2_bwd(dr, a):
    return (dr * (2.0 * jnp.maximum(a, 0.0)),)


def _adamw_math(w, g, m, v):
    m = ADAM_B1 * m + (1.0 - ADAM_B1) * g
    v = ADAM_B2 * v + (1.0 - ADAM_B2) * (g * g)
    m_hat = m / (1.0 - ADAM_B1 ** ADAM_STEP)
    v_hat = v / (1.0 - ADAM_B2 ** ADAM_STEP)
    delta = -ADAM_LR * (m_hat / (jnp.sqrt(v_hat) + ADAM_EPS) + ADAM_WD * w)
    return delta, m, v


def _adamw(w, g, m, v, name):
    return _ew(_adamw_math, [w, g, m, v], [F32, F32, F32], name)


def _adamw_many(ws, gs, ms, vs, name):
    n = len(ws)

    def body(*refs):
        for i in range(n):
            res = _adamw_math(*[refs[k * n + i][...] for k in range(4)])
            for k in range(3):
                refs[(4 + k) * n + i][...] = res[k]

    outs = pl.pallas_call(
        body, name=name, out_shape=[jax.ShapeDtypeStruct(w.shape, F32) for w in ws] * 3,
        compiler_params=pltpu.CompilerParams(vmem_limit_bytes=V7X_VMEM_LIMIT),
    )(*ws, *gs, *ms, *vs)
    return outs[:n], outs[n:2 * n], outs[2 * n:]


def _row_spec(cols, block=0):
    return pl.BlockSpec((ROW_TILE, cols), lambda i, block=block: (i, block))


def _vec_spec(cols):
    return pl.BlockSpec((1, cols), lambda i: (0, 0))


def _vec_args(*vecs):
    arrays = [v[0] if isinstance(v, tuple) else v for v in vecs]
    specs = [pl.BlockSpec((None, 1, D), lambda i, row=v[1]: (row, 0, 0)) if isinstance(v, tuple) else _vec_spec(D) for v in vecs]
    return arrays, specs


def _sum_spec(cols):
    return pl.BlockSpec((8, cols), lambda i: (0, 0))


def _rstd(x):
    return lax.rsqrt(jnp.mean(x * x, axis=-1, keepdims=True) + RMS_EPS)


def _modnorm_fwd(x, g, shift, scale, name):
    s = x.shape[0]

    def body(x_ref, g_ref, sh_ref, sc_ref, h_ref):
        xv = x_ref[...]
        n = xv * _rstd(xv)
        h_ref[...] = ((n * g_ref[...]) * (1.0 + sc_ref[...]) + sh_ref[...]).astype(BF16)

    vecs, vec_specs = _vec_args(g, shift, scale)
    return pl.pallas_call(
        body, name=name, grid=(s // ROW_TILE,),
        in_specs=[_row_spec(D)] + vec_specs, out_specs=_row_spec(D),
        out_shape=jax.ShapeDtypeStruct((s, D), BF16), compiler_params=_params("parallel"),
    )(x, *vecs)


def _post_fwd(x, y, g, gate, name):
    s = x.shape[0]

    def body(x_ref, y_ref, g_ref, gate_ref, o_ref):
        yv = y_ref[...]
        o_ref[...] = x_ref[...] + gate_ref[...] * ((yv * _rstd(yv)) * g_ref[...])

    vecs, vec_specs = _vec_args(g, gate)
    return pl.pallas_call(
        body, name=name, grid=(s // ROW_TILE,),
        in_specs=[_row_spec(D), _row_spec(D)] + vec_specs, out_specs=_row_spec(D),
        out_shape=jax.ShapeDtypeStruct((s, D), F32), compiler_params=_params("parallel"),
    )(x, y, *vecs)


def _post_bwd(dxo, y, g, gate, name):
    s = dxo.shape[0]

    def body(d_ref, y_ref, g_ref, gate_ref, dy_ref, sum_ref):
        @pl.when(pl.program_id(0) == 0)
        def _():
            sum_ref[...] = jnp.zeros_like(sum_ref)

        dv, yv = d_ref[...], y_ref[...]
        r = _rstd(yv)
        n = yv * r
        sum_ref[0:1, :] += jnp.sum(dv * (n * g_ref[...]), axis=0, keepdims=True)
        sum_ref[1:2, :] += jnp.sum((dv * gate_ref[...]) * n, axis=0, keepdims=True)
        dn = (dv * gate_ref[...]) * g_ref[...]
        dy_ref[...] = (r * (dn - n * jnp.mean(dn * n, axis=-1, keepdims=True))).astype(BF16)

    vecs, vec_specs = _vec_args(g, gate)
    return pl.pallas_call(
        body, name=name, grid=(s // ROW_TILE,),
        in_specs=[_row_spec(D), _row_spec(D)] + vec_specs,
        out_specs=[_row_spec(D), _sum_spec(D)],
        out_shape=[jax.ShapeDtypeStruct((s, D), BF16), jax.ShapeDtypeStruct((8, D), F32)],
        compiler_params=_params("arbitrary"),
    )(dxo, y, *vecs)


def _modnorm_bwd(dh, x, dxo, g, scale, name):
    s = dh.shape[0]

    def body(dh_ref, x_ref, d_ref, g_ref, sc_ref, dx_ref, sum_ref):
        @pl.when(pl.program_id(0) == 0)
        def _():
            sum_ref[...] = jnp.zeros_like(sum_ref)

        dhv, xv = dh_ref[...], x_ref[...]
        r = _rstd(xv)
        n = xv * r
        one_sc = 1.0 + sc_ref[...]
        sum_ref[0:1, :] += jnp.sum(dhv, axis=0, keepdims=True)
        sum_ref[1:2, :] += jnp.sum(dhv * (n * g_ref[...]), axis=0, keepdims=True)
        sum_ref[2:3, :] += jnp.sum((dhv * one_sc) * n, axis=0, keepdims=True)
        dn = (dhv * one_sc) * g_ref[...]
        dx_ref[...] = d_ref[...] + r * (dn - n * jnp.mean(dn * n, axis=-1, keepdims=True))

    vecs, vec_specs = _vec_args(g, scale)
    return pl.pallas_call(
        body, name=name, grid=(s // ROW_TILE,),
        in_specs=[_row_spec(D), _row_spec(D), _row_spec(D)] + vec_specs,
        out_specs=[_row_spec(D), _sum_spec(D)],
        out_shape=[jax.ShapeDtypeStruct((s, D), F32), jax.ShapeDtypeStruct((8, D), F32)],
        compiler_params=_params("arbitrary"),
    )(dh, x, dxo, *vecs)


def _post_pre_fwd(x, y, g_post, gate, g_pre, shift, scale, name):
    s = x.shape[0]

    def body(x_ref, y_ref, gp_ref, gate_ref, g_ref, sh_ref, sc_ref, o_ref, h_ref):
        yv = y_ref[...]
        xo = x_ref[...] + gate_ref[...] * ((yv * _rstd(yv)) * gp_ref[...])
        o_ref[...] = xo
        h_ref[...] = (((xo * _rstd(xo)) * g_ref[...]) * (1.0 + sc_ref[...]) + sh_ref[...]).astype(BF16)

    vecs, vec_specs = _vec_args(g_post, gate, g_pre, shift, scale)
    return pl.pallas_call(
        body, name=name, grid=(s // ROW_TILE,),
        in_specs=[_row_spec(D), _row_spec(D)] + vec_specs, out_specs=[_row_spec(D), _row_spec(D)],
        out_shape=[jax.ShapeDtypeStruct((s, D), F32), jax.ShapeDtypeStruct((s, D), BF16)], compiler_params=_params("parallel"),
    )(x, y, *vecs)


def _pre_post_bwd(dh, x, dxo, g_pre, scale, y, g_post, gate, name):
    s = dh.shape[0]

    def body(dh_ref, x_ref, d_ref, y_ref, g_ref, sc_ref, gp_ref, gate_ref, dx_ref, dy_ref, sum_ref):
        @pl.when(pl.program_id(0) == 0)
        def _():
            sum_ref[...] = jnp.zeros_like(sum_ref)

        dhv, xv = dh_ref[...], x_ref[...]
        r = _rstd(xv)
        n = xv * r
        one_sc = 1.0 + sc_ref[...]
        sum_ref[0:1, :] += jnp.sum(dhv, axis=0, keepdims=True)
        sum_ref[1:2, :] += jnp.sum(dhv * (n * g_ref[...]), axis=0, keepdims=True)
        sum_ref[2:3, :] += jnp.sum((dhv * one_sc) * n, axis=0, keepdims=True)
        dn = (dhv * one_sc) * g_ref[...]
        dv = d_ref[...] + r * (dn - n * jnp.mean(dn * n, axis=-1, keepdims=True))
        dx_ref[...] = dv

        yv = y_ref[...]
        ry = _rstd(yv)
        ny = yv * ry
        sum_ref[3:4, :] += jnp.sum(dv * (ny * gp_ref[...]), axis=0, keepdims=True)
        sum_ref[4:5, :] += jnp.sum((dv * gate_ref[...]) * ny, axis=0, keepdims=True)
        dny = (dv * gate_ref[...]) * gp_ref[...]
        dy_ref[...] = (ry * (dny - ny * jnp.mean(dny * ny, axis=-1, keepdims=True))).astype(BF16)

    vecs, vec_specs = _vec_args(g_pre, scale, g_post, gate)
    return pl.pallas_call(
        body, name=name, grid=(s // ROW_TILE,),
        in_specs=[_row_spec(D)] * 4 + vec_specs,
        out_specs=[_row_spec(D), _row_spec(D), _sum_spec(D)],
        out_shape=[jax.ShapeDtypeStruct((s, D), F32), jax.ShapeDtypeStruct((s, D), BF16), jax.ShapeDtypeStruct((8, D), F32)],
        compiler_params=_params("arbitrary"),
    )(dh, x, dxo, y, *vecs)


def _loss_head(y, target):
    s = y.shape[0]

    def body(y_ref, t_ref, dy_ref, sum_ref):
        @pl.when(pl.program_id(0) == 0)
        def _():
            sum_ref[...] = jnp.zeros_like(sum_ref)

        err = y_ref[...] - t_ref[...]
        dy_ref[...] = err * (1.0 / D)
        sum_ref[...] += jnp.sum(err * err)

    return pl.pallas_call(
        body, name="loss_head", grid=(s // ROW_TILE,),
        in_specs=[_row_spec(D), _row_spec(D)],
        out_specs=[_row_spec(D), pl.BlockSpec((8, 128), lambda i: (0, 0))],
        out_shape=[jax.ShapeDtypeStruct((s, D), F32), jax.ShapeDtypeStruct((8, 128), F32)],
        compiler_params=_params("arbitrary"),
    )(y, target)


def _merge_fwd(z, pa, pb, pc):
    s = z.shape[0]

    def body(g0_ref, g1_ref, g2_ref, pa_ref, pb_ref, pc_ref, o_ref):
        o_ref[...] = (jax.nn.sigmoid(g0_ref[...]) * pa_ref[...] + jax.nn.sigmoid(g1_ref[...]) * pb_ref[...]
                      + jax.nn.sigmoid(g2_ref[...]) * pc_ref[...]).astype(BF16)

    return pl.pallas_call(
        body, name="merge_fwd", grid=(s // ROW_TILE,),
        in_specs=[_row_spec(D, 0), _row_spec(D, 1), _row_spec(D, 2), _row_spec(D), _row_spec(D), _row_spec(D)],
        out_specs=_row_spec(D), out_shape=jax.ShapeDtypeStruct((s, D), BF16),
        compiler_params=_params("parallel"),
    )(z, z, z, pa, pb, pc)


def _merge_bwd(dm, z, pa, pb, pc):
    s = z.shape[0]

    def body(dm_ref, g0_ref, g1_ref, g2_ref, pa_ref, pb_ref, pc_ref, dgl_ref, da_ref, db_ref, dc_ref):
        dmv = dm_ref[...]
        for i, (g_ref, p_ref, d_ref) in enumerate(((g0_ref, pa_ref, da_ref), (g1_ref, pb_ref, db_ref), (g2_ref, pc_ref, dc_ref))):
            gate = jax.nn.sigmoid(g_ref[...])
            dgl_ref[:, i * D:(i + 1) * D] = ((dmv * p_ref[...]) * (gate * (1.0 - gate))).astype(BF16)
            d_ref[...] = (dmv * gate).astype(BF16)

    return pl.pallas_call(
        body, name="merge_bwd", grid=(s // ROW_TILE,),
        in_specs=[_row_spec(D), _row_spec(D, 0), _row_spec(D, 1), _row_spec(D, 2), _row_spec(D), _row_spec(D), _row_spec(D)],
        out_specs=[_row_spec(3 * D), _row_spec(D), _row_spec(D), _row_spec(D)],
        out_shape=[jax.ShapeDtypeStruct((s, Z_COLS), BF16)] + [jax.ShapeDtypeStruct((s, D), BF16)] * 3,
        compiler_params=_params("parallel"),
    )(dm, z, z, z, pa, pb, pc)


def _shift_down(v, n):
    row = lax.broadcasted_iota(jnp.int32, v.shape, 0)
    return jnp.where(row >= n, pltpu.roll(v, n, axis=0), 0.0)


def _shift_up(v, n):
    s = v.shape[0]
    row = lax.broadcasted_iota(jnp.int32, v.shape, 0)
    return jnp.where(row < s - n, pltpu.roll(v, s - n, axis=0), 0.0)


def _log_sigmoid(v):
    return jnp.minimum(v, 0.0) - jnp.log1p(jnp.exp(-jnp.abs(v)))


def _cumf_fwd(fl, bias):
    s = fl.shape[0]

    def body(fl_ref, b_ref, o_ref):
        acc = _log_sigmoid(fl_ref[...] + b_ref[...])
        step = 1
        while step < s:
            acc = acc + _shift_down(acc, step)
            step *= 2
        o_ref[...] = acc

    return pl.pallas_call(body, name="cumf_fwd", out_shape=jax.ShapeDtypeStruct((s, 128), F32),
                          compiler_params=pltpu.CompilerParams(vmem_limit_bytes=V7X_VMEM_LIMIT))(fl, bias)


def _cumf_bwd(dcum, fl, bias):
    s = fl.shape[0]

    def body(d_ref, fl_ref, b_ref, dfl_ref, db_ref):
        acc = d_ref[...]
        step = 1
        while step < s:
            acc = acc + _shift_up(acc, step)
            step *= 2
        dfl = acc * jax.nn.sigmoid(-(fl_ref[...] + b_ref[...]))
        dfl_ref[...] = dfl.astype(BF16)
        db_ref[...] = jnp.broadcast_to(jnp.sum(dfl, axis=0, keepdims=True), (8, 128))

    return pl.pallas_call(
        body, name="cumf_bwd",
        out_shape=[jax.ShapeDtypeStruct((s, 128), BF16), jax.ShapeDtypeStruct((8, 128), F32)],
        compiler_params=pltpu.CompilerParams(vmem_limit_bytes=V7X_VMEM_LIMIT))(dcum, fl, bias)


def _pool_windows(v, shift):
    s2 = v + shift(v, 1)
    s4 = s2 + shift(s2, 2)
    s8 = s4 + shift(s4, 4)
    s16 = s8 + shift(s8, 8)
    group = lax.broadcasted_iota(jnp.int32, v.shape, 1) // 64
    return jnp.where(group == 0, s2, jnp.where(group == 1, s4, jnp.where(group == 2, s8, s16)))


def _pool_count(shape):
    group = lax.broadcasted_iota(jnp.int32, shape, 1) // 64
    window = jnp.where(group == 0, 2.0, jnp.where(group == 1, 4.0, jnp.where(group == 2, 8.0, 16.0)))
    t1 = (lax.broadcasted_iota(jnp.int32, shape, 0) + 1).astype(F32)
    return jnp.minimum(t1, window)


def _pc_specs(s):
    zcol = lambda blk: pl.BlockSpec((s, 256), lambda i, blk=blk: (0, blk))
    first = Z_PC // 256
    return [zcol(first), zcol(first + 1), zcol(first + 2), zcol(first + 3),
            pl.BlockSpec((256, 256), lambda i: (0, 0)), pl.BlockSpec((1, 256), lambda i: (0, 0)),
            pl.BlockSpec((3, 256), lambda i: (0, 0))]


def _poolconv_fwd(z, wbd, pscale, convw):
    s = z.shape[0]

    def body(pu_ref, ch_ref, cb_ref, cc_ref, w_ref, ps_ref, cw_ref, yb_ref, yc_ref):
        u = pu_ref[...]
        p = _pool_windows(u, _shift_down) / _pool_count(u.shape) - u
        yb = jnp.dot(p.astype(BF16), w_ref[...].astype(BF16), preferred_element_type=F32) * ps_ref[...]
        yb_ref[...] = yb.astype(BF16)
        uc = cc_ref[...] * ch_ref[...]
        cw = cw_ref[...]
        conv = cw[0:1, :] * _shift_down(uc, 2) + cw[1:2, :] * _shift_down(uc, 1) + cw[2:3, :] * uc
        yc_ref[...] = (cb_ref[...] * conv).astype(BF16)

    out = pl.BlockSpec((s, 256), lambda i: (0, 0))
    return pl.pallas_call(
        body, name="poolconv_fwd", grid=(1,), in_specs=_pc_specs(s), out_specs=[out, out],
        out_shape=[jax.ShapeDtypeStruct((s, 256), BF16)] * 2, compiler_params=_params("arbitrary"),
    )(z, z, z, z, wbd, pscale, convw)


def _poolconv_bwd(dyb, dyc, z, wbd, pscale, convw):
    s = z.shape[0]

    def body(dyb_ref, dyc_ref, pu_ref, ch_ref, cb_ref, cc_ref, w_ref, ps_ref, cw_ref, dz_ref, dw_ref, dps_ref, dcw_ref):
        u = pu_ref[...]
        count = _pool_count(u.shape)
        p = (_pool_windows(u, _shift_down) / count - u).astype(BF16)
        wb = w_ref[...].astype(BF16)
        dyb_v = dyb_ref[...]
        pw = jnp.dot(p, wb, preferred_element_type=F32)
        dps_ref[...] = jnp.broadcast_to(jnp.sum(dyb_v * pw, axis=0, keepdims=True), (8, 256))
        dys = (dyb_v * ps_ref[...]).astype(BF16)
        dp = lax.dot_general(dys, wb, (((1,), (1,)), ((), ())), preferred_element_type=F32)
        dw_ref[...] = lax.dot_general(p, dys, (((0,), (0,)), ((), ())), preferred_element_type=F32)
        dz_ref[:, 0:256] = (_pool_windows(dp / count, _shift_up) - dp).astype(BF16)

        ch, cb, cc = ch_ref[...], cb_ref[...], cc_ref[...]
        uc = cc * ch
        cw = cw_ref[...]
        u2, u1 = _shift_down(uc, 2), _shift_down(uc, 1)
        conv = cw[0:1, :] * u2 + cw[1:2, :] * u1 + cw[2:3, :] * uc
        dyc_v = dyc_ref[...]
        dconv = dyc_v * cb
        du = cw[0:1, :] * _shift_up(dconv, 2) + cw[1:2, :] * _shift_up(dconv, 1) + cw[2:3, :] * dconv
        dz_ref[:, 256:512] = (du * cc).astype(BF16)
        dz_ref[:, 512:768] = (dyc_v * conv).astype(BF16)
        dz_ref[:, 768:1024] = (du * ch).astype(BF16)
        dcw_ref[...] = jnp.zeros_like(dcw_ref)
        dcw_ref[0:1, :] = jnp.sum(dconv * u2, axis=0, keepdims=True)
        dcw_ref[1:2, :] = jnp.sum(dconv * u1, axis=0, keepdims=True)
        dcw_ref[2:3, :] = jnp.sum(dconv * uc, axis=0, keepdims=True)

    blk = lambda r, c: pl.BlockSpec((r, c), lambda i: (0, 0))
    return pl.pallas_call(
        body, name="poolconv_bwd", grid=(1,),
        in_specs=[blk(s, 256), blk(s, 256)] + _pc_specs(s),
        out_specs=[blk(s, 1024), blk(256, 256), blk(8, 256), blk(8, 256)],
        out_shape=[jax.ShapeDtypeStruct((s, 1024), BF16), jax.ShapeDtypeStruct((256, 256), F32),
                   jax.ShapeDtypeStruct((8, 256), F32), jax.ShapeDtypeStruct((8, 256), F32)],
        compiler_params=_params("arbitrary"),
    )(dyb, dyc, z, z, z, z, wbd, pscale, convw)


_NT = (((1,), (1,)), ((), ()))
_TN = (((0,), (0,)), ((), ()))


ATT_Q, ATT_K = 256, 256
ATT_HEADS_BWD = 8
ATT_HEADS = 8


def _att_logits(q, k, fr, q0, k0, masked):
    logits = lax.dot_general(q, k, _NT, preferred_element_type=F32) - fr
    if not masked:
        return logits
    row = q0 + lax.broadcasted_iota(jnp.int32, logits.shape, 0)
    col = k0 + lax.broadcasted_iota(jnp.int32, logits.shape, 1)
    return jnp.where(row >= col, logits, NEG_INF)


def _causal_sweep(step, qi, init):
    n_full = (qi * ATT_Q) // ATT_K
    carry = lax.fori_loop(0, n_full, lambda j, carry: step(j, carry, False), init)
    return step(n_full, carry, True)


HEAD_PAIRS = HEADS // 2


def _lane_pick(v, lane, idx):
    return jnp.sum(jnp.where(lane == idx, v, 0.0), axis=-1, keepdims=True)


def _lane_put(lane, idx, col):
    return jnp.where(lane == idx, col, 0.0)


def _split_heads(v, low):
    zero = jnp.zeros_like(v)
    return jnp.where(low, v, zero), jnp.where(low, zero, v)


def _attn_fwd(qkv, fr):
    s = qkv.shape[0]
    nk = s // ATT_K
    width = ATT_HEADS * HEAD_DIM
    groups = HEADS // ATT_HEADS

    def body(q_ref, k_ref, v_ref, fr_ref, o_ref, lse_ref):
        qi, grp = pl.program_id(0), pl.program_id(1)
        lane = lax.broadcasted_iota(jnp.int32, (ATT_Q, 128), 1)
        low = lane < HEAD_DIM
        qs = []
        for pr in range(ATT_HEADS // 2):
            qs += _split_heads(q_ref[:, 128 * pr:128 * (pr + 1)] * (HEAD_DIM ** -0.5), low)

        def step(j, carry, masked):
            k0 = pl.multiple_of(j * ATT_K, ATT_K)
            out = []
            for h in range(ATT_HEADS):
                cols = slice(128 * (h // 2), 128 * (h // 2 + 1))
                m, l, acc = carry[h]
                logits = _att_logits(qs[h], k_ref[pl.ds(k0, ATT_K), cols], fr_ref[h, pl.ds(j, 1), :], qi * ATT_Q, k0, masked)
                m_new = jnp.maximum(m, jnp.max(logits, axis=-1, keepdims=True))
                p = jnp.exp(logits - m_new)
                alpha = jnp.exp(m - m_new)
                l = alpha * l + jnp.sum(p, axis=-1, keepdims=True)
                acc = alpha * acc + jnp.dot(p.astype(BF16), v_ref[pl.ds(k0, ATT_K), cols], preferred_element_type=F32)
                out.append((m_new, l, acc))
            return tuple(out)

        one = (jnp.full((ATT_Q, 1), NEG_INF, F32), jnp.zeros((ATT_Q, 1), F32), jnp.zeros((ATT_Q, 128), F32))
        done = _causal_sweep(step, qi, (one,) * ATT_HEADS)

        @pl.when(grp == 0)
        def _():
            lse_ref[...] = jnp.zeros_like(lse_ref)

        lse = jnp.zeros((ATT_Q, 128), F32)
        for pr in range(ATT_HEADS // 2):
            (m0, l0, acc0), (m1, l1, acc1) = done[2 * pr], done[2 * pr + 1]
            o_ref[:, 128 * pr:128 * (pr + 1)] = jnp.where(low, acc0 / l0, acc1 / l1)
            head = ATT_HEADS * grp + 2 * pr
            lse = lse + _lane_put(lane, head, m0 + jnp.log(l0)) + _lane_put(lane, head + 1, m1 + jnp.log(l1))
        lse_ref[...] += lse

    return pl.pallas_call(
        body, name="attn_fwd", grid=(s // ATT_Q, groups),
        in_specs=[pl.BlockSpec((ATT_Q, width), lambda i, g: (i, g)),
                  pl.BlockSpec((s, width), lambda i, g: (0, groups + g)),
                  pl.BlockSpec((s, width), lambda i, g: (0, 2 * groups + g)),
                  pl.BlockSpec((ATT_HEADS, nk, ATT_K), lambda i, g: (g, 0, 0))],
        out_specs=[pl.BlockSpec((ATT_Q, width), lambda i, g: (i, g)), pl.BlockSpec((ATT_Q, 128), lambda i, g: (i, 0))],
        out_shape=[jax.ShapeDtypeStruct((s, A_WIDTH), F32), jax.ShapeDtypeStruct((s, 128), F32)],
        compiler_params=_params("parallel", "arbitrary"),
    )(qkv, qkv, qkv, fr)


def _attn_bwd(qkv, do, o, lse, fr):
    s = qkv.shape[0]
    nk = s // ATT_K
    scale = HEAD_DIM ** -0.5
    heads = ATT_HEADS_BWD
    width = heads * HEAD_DIM
    groups = HEADS // heads

    def body(q_ref, k_ref, v_ref, do_ref, o_ref, lse_ref, fr_ref, dq_ref, dk_ref, dv_ref, dfc_ref, dfr_ref, dk_acc, dv_acc):
        grp = pl.program_id(0)
        lane = lax.broadcasted_iota(jnp.int32, (ATT_Q, 128), 1)
        low = lane < HEAD_DIM
        low_t = lax.broadcasted_iota(jnp.int32, (128, ATT_Q), 0) < HEAD_DIM
        dk_acc[...] = jnp.zeros_like(dk_acc)
        dv_acc[...] = jnp.zeros_like(dv_acc)
        dfr_ref[...] = jnp.zeros_like(dfr_ref)

        @pl.when(grp == 0)
        def _():
            dfc_ref[...] = jnp.zeros_like(dfc_ref)

        def outer(i, carry):
            q0 = pl.multiple_of(i * ATT_Q, ATT_Q)
            rows = pl.ds(q0, ATT_Q)
            lsev = lse_ref[rows, :]
            qts, dots, qs, dos, deltas, lses = [], [], [], [], [], []
            for pr in range(heads // 2):
                pcols = slice(128 * pr, 128 * (pr + 1))
                q2, do2 = q_ref[rows, pcols] * scale, do_ref[rows, pcols]
                prod = do2 * o_ref[rows, pcols]
                deltas += [jnp.sum(jnp.where(low, prod, 0.0), axis=-1, keepdims=True),
                           jnp.sum(jnp.where(low, 0.0, prod), axis=-1, keepdims=True)]
                dob2 = do2.astype(BF16)
                qts += _split_heads(q2.astype(F32).T.astype(BF16), low_t)
                dots += _split_heads(do2.T.astype(BF16), low_t)
                qs += _split_heads(q2, low)
                dos += _split_heads(dob2, low)
                lses += [_lane_pick(lsev, lane, heads * grp + 2 * pr), _lane_pick(lsev, lane, heads * grp + 2 * pr + 1)]

            def inner(j, carry, masked):
                k0 = pl.multiple_of(j * ATT_K, ATT_K)
                krows = pl.ds(k0, ATT_K)
                out, dkt, dvt = [], [], []
                for h in range(heads):
                    pcols = slice(128 * (h // 2), 128 * (h // 2 + 1))
                    dq, dfc = carry[h]
                    k2 = k_ref[krows, pcols]
                    p = jnp.exp(_att_logits(qs[h], k2, fr_ref[h, pl.ds(j, 1), :], q0, k0, masked) - lses[h])
                    dp = lax.dot_general(dos[h], v_ref[krows, pcols], _NT, preferred_element_type=F32)
                    ds = p * (dp - deltas[h])
                    dsb = ds.astype(BF16)
                    dkt.append(jnp.dot(qts[h], dsb, preferred_element_type=F32))
                    dvt.append(jnp.dot(dots[h], p.astype(BF16), preferred_element_type=F32))
                    dfr_ref[h, pl.ds(j, 1), :] -= jnp.sum(ds, axis=0, keepdims=True)
                    out.append((dq + jnp.dot(dsb, k2, preferred_element_type=F32), dfc + (ds[:, :128] + ds[:, 128:])))
                for pr in range(heads // 2):
                    prows = slice(128 * pr, 128 * (pr + 1))
                    dk_acc[j, prows, :] += dkt[2 * pr] + dkt[2 * pr + 1]
                    dv_acc[j, prows, :] += dvt[2 * pr] + dvt[2 * pr + 1]
                return tuple(out)

            one = (jnp.zeros((ATT_Q, 128), F32), jnp.zeros((ATT_Q, 128), F32))
            done = _causal_sweep(inner, i, (one,) * heads)
            dfc = jnp.zeros((ATT_Q, 128), F32)
            for pr in range(heads // 2):
                (dq0, dfc0), (dq1, dfc1) = done[2 * pr], done[2 * pr + 1]
                dq_ref[rows, 128 * pr:128 * (pr + 1)] = (jnp.where(low, dq0, dq1) * scale).astype(BF16)
                head = heads * grp + 2 * pr
                dfc = (dfc + _lane_put(lane, head, jnp.sum(dfc0, axis=-1, keepdims=True))
                       + _lane_put(lane, head + 1, jnp.sum(dfc1, axis=-1, keepdims=True)))
            dfc_ref[rows, :] += dfc
            return carry

        lax.fori_loop(0, s // ATT_Q, outer, 0)
        for j in range(nk):
            for pr in range(heads // 2):
                prows, pcols = slice(128 * pr, 128 * (pr + 1)), slice(128 * pr, 128 * (pr + 1))
                dk_ref[ATT_K * j:ATT_K * (j + 1), pcols] = dk_acc[j, prows, :].T.astype(BF16)
                dv_ref[ATT_K * j:ATT_K * (j + 1), pcols] = dv_acc[j, prows, :].T.astype(BF16)

    part = lambda first: pl.BlockSpec((s, width), lambda g, first=first: (0, first + g))
    whole = pl.BlockSpec((s, 128), lambda g: (0, 0))
    rowv = pl.BlockSpec((heads, nk, ATT_K), lambda g: (g, 0, 0))
    return pl.pallas_call(
        body, name="attn_bwd", grid=(groups,),
        in_specs=[part(0), part(groups), part(2 * groups), part(0), part(0), whole, rowv],
        out_specs=[part(0), part(0), part(0), whole, rowv],
        out_shape=[jax.ShapeDtypeStruct((s, A_WIDTH), BF16)] * 3 + [jax.ShapeDtypeStruct((s, 128), F32), jax.ShapeDtypeStruct((HEADS, nk, ATT_K), F32)],
        scratch_shapes=[pltpu.VMEM((nk, width, ATT_K), F32), pltpu.VMEM((nk, width, ATT_K), F32)],
        compiler_params=_params("arbitrary"),
    )(qkv, qkv, qkv, do, o, lse, fr)


def _ada_fwd(c_all, w_ada, b_loc):
    depth, _, n = w_ada.shape
    tn = 512

    def body(c_ref, w_ref, b_ref, o_ref, sc_ref):
        cv = c_ref[...]
        sc = cv * jax.nn.sigmoid(cv)
        sc_ref[...] = sc
        o_ref[0] = jnp.dot(sc.astype(BF16), w_ref[0].astype(BF16), preferred_element_type=F32) + b_ref[0]

    return pl.pallas_call(
        body, name="ada_fwd", grid=(depth, n // tn),
        in_specs=[pl.BlockSpec((N_DEV, D), lambda l, j: (0, 0)), pl.BlockSpec((1, D, tn), lambda l, j: (l, 0, j)),
                  pl.BlockSpec((1, 1, tn), lambda l, j: (l, 0, j))],
        out_specs=[pl.BlockSpec((1, N_DEV, tn), lambda l, j: (l, 0, j)), pl.BlockSpec((N_DEV, D), lambda l, j: (0, 0))],
        out_shape=[jax.ShapeDtypeStruct((depth, N_DEV, n), F32), jax.ShapeDtypeStruct((N_DEV, D), F32)],
        compiler_params=_params("arbitrary", "arbitrary"),
    )(c_all, w_ada, b_loc)


def _sum_devices(gathered):
    n = gathered.shape[1]
    tn = _pick(n, (1408, 1024, 640, 512, 128))

    def body(g_ref, o_ref):
        acc = g_ref[0:8, :]
        for dev in range(1, N_DEV):
            acc = acc + g_ref[8 * dev:8 * dev + 8, :]
        o_ref[...] = acc

    return pl.pallas_call(
        body, name="sum_devices", grid=(n // tn,),
        in_specs=[pl.BlockSpec((8 * N_DEV, tn), lambda j: (0, j))], out_specs=pl.BlockSpec((8, tn), lambda j: (0, j)),
        out_shape=jax.ShapeDtypeStruct((8, n), F32), compiler_params=_params("parallel"),
    )(gathered)


def _place():
    x, y, c = lax.axis_index("x"), lax.axis_index("y"), lax.axis_index("c")
    chips = [(1 - x, y), (x, 1 - y), (1 - x, 1 - y)]
    return x, y, c, chips


def _allgather8(block, name):
    m_per, n = block.shape

    def body(x_ref, out_ref, send_sems, recv_sems, local_sem):
        x, y, c, chips = _place()
        me, sibling = (x, y, c), (x, y, 1 - c)

        def rows(px, py, pc):
            return out_ref.at[pl.ds((4 * px + 2 * py + pc) * m_per, m_per), :]

        def copy(k, blk, to, src=None):
            return pltpu.make_async_remote_copy(
                src_ref=rows(*blk) if src is None else src, dst_ref=rows(*blk),
                send_sem=send_sems.at[k], recv_sem=recv_sems.at[k], device_id=to, device_id_type=MESH)

        mine = pltpu.make_async_copy(x_ref, rows(*me), local_sem)
        mine.start()
        first = [copy(0, me, sibling, src=x_ref)]
        first += [copy(1 + j, me, (*chip, c), src=x_ref) for j, chip in enumerate(chips)]
        for cp in first:
            cp.start()
        passed = [copy(4 + j, (*chip, c), sibling) for j, chip in enumerate(chips)]
        for j, chip in enumerate(chips):
            copy(1 + j, (*chip, c), me).wait_recv()
            passed[j].start()
        copy(0, sibling, me).wait_recv()
        for j, chip in enumerate(chips):
            copy(4 + j, (*chip, 1 - c), me).wait_recv()
        for cp in first + passed:
            cp.wait_send()
        mine.wait()

    return pl.pallas_call(
        body, name=name, out_shape=jax.ShapeDtypeStruct((N_DEV * m_per, n), block.dtype),
        in_specs=[pl.BlockSpec(memory_space=pltpu.VMEM)], out_specs=pl.BlockSpec(memory_space=pltpu.VMEM),
        scratch_shapes=[pltpu.SemaphoreType.DMA((7,)), pltpu.SemaphoreType.DMA((7,)), pltpu.SemaphoreType.DMA],
        compiler_params=pltpu.CompilerParams(vmem_limit_bytes=V7X_VMEM_LIMIT),
    )(block)


_SEM = pl.BlockSpec(memory_space=pltpu.SEMAPHORE)
_DATAFLOW = pltpu.SideEffectType.DATAFLOW_SIDE_EFFECTING


def _plan_copies(plan, refs, send_sems, recv_sems):
    return [pltpu.make_async_remote_copy(src_ref=src, dst_ref=dst, send_sem=send_sems.at[i], recv_sem=recv_sems.at[i],
                                         device_id=to, device_id_type=MESH) for i, (src, dst, to) in enumerate(plan(refs))]


class _Token(NamedTuple):
    after: jax.Array
    tie: jax.Array


def _after_operand(after):
    return after.after if isinstance(after, _Token) else after


def _copies_start(bufs, plan, n_copies, after, name):
    nb = len(bufs)

    def body(*refs):
        for cp in _plan_copies(plan, refs[:nb], refs[nb + 1], refs[nb + 2]):
            cp.start()
        for token in refs[-2:]:
            token[...] = jnp.zeros_like(token)

    sem = pltpu.SemaphoreType.DMA((n_copies,))
    vmem = pl.BlockSpec(memory_space=pltpu.VMEM)
    outs = pl.pallas_call(
        body, name=name,
        out_shape=(sem, sem, *[pltpu.HBM(b.shape, b.dtype) for b in bufs], jax.ShapeDtypeStruct((8, 128), F32),
                   jax.ShapeDtypeStruct((1, 1), F32)),
        in_specs=[_HBM] * nb + [pl.BlockSpec(memory_space=pl.ANY)],
        out_specs=(_SEM, _SEM, *[_HBM] * nb, vmem, vmem),
        input_output_aliases={i: 2 + i for i in range(nb)},
        compiler_params=pltpu.CompilerParams(has_side_effects=_DATAFLOW),
    )(*[pltpu.with_memory_space_constraint(b, pltpu.HBM) for b in bufs], _after_operand(after))
    return outs[0], outs[1], list(outs[2:2 + nb]), _Token(outs[-2], outs[-1])


def _copies_wait(started, plan, after, name):
    send_sems, recv_sems, bufs, _ = started
    nb = len(bufs)

    def body(*refs):
        for cp in _plan_copies(plan, refs[:nb], refs[nb], refs[nb + 1]):
            cp.wait_send()
            cp.wait_recv()

    return list(pl.pallas_call(
        body, name=name, out_shape=tuple(pltpu.HBM(b.shape, b.dtype) for b in bufs),
        in_specs=[_HBM] * nb + [_SEM, _SEM, pl.BlockSpec(memory_space=pl.ANY)], out_specs=tuple([_HBM] * nb),
        input_output_aliases={i: i for i in range(nb)},
        compiler_params=pltpu.CompilerParams(has_side_effects=_DATAFLOW),
    )(*bufs, send_sems, recv_sems, _after_operand(after)))


def _half_rows(ref, axis, c):
    half = ref.shape[axis] // 2
    return pl.ds(c * half, half)


def _plan_gather_ici(refs):
    n = len(refs) // 2
    x, y, c, chips = _place()
    out = []
    for a in range(n):
        rows = _half_rows(refs[a], 0, c)
        out += [(refs[a].at[rows], refs[n + a].at[2 * x + y, rows], (*chip, c)) for chip in chips]
        out.append((refs[a], refs[n + a].at[2 * x + y], (x, y, 1 - c)))
    return out


def _plan_gather_d2d(refs):
    x, y, c, chips = _place()
    out = []
    for ref in refs:
        rows = _half_rows(ref, 1, c)
        for px, py in chips:
            landed = ref.at[2 * px + py, rows]
            out.append((landed, landed, (x, y, 1 - c)))
    return out


def _plan_rs_sibling(refs):
    n = len(refs) // 2
    x, y, c, _ = _place()
    return [(refs[a].at[pl.ds(0, N_CHIPS), _half_rows(refs[a], 1, 1 - c)], refs[n + a], (x, y, 1 - c)) for a in range(n)]


def _plan_rs_chips(refs):
    n = len(refs) // 2
    x, y, c, chips = _place()
    return [(refs[a].at[2 * px + py], refs[n + a].at[k], (px, py, c)) for a in range(n) for k, (px, py) in enumerate(chips)]


def _plan_rs_share(layer):
    def plan(refs):
        x, y, c, _ = _place()
        return [(ref.at[layer, _half_rows(ref, 1, c)], ref.at[layer, _half_rows(ref, 1, c)], (x, y, 1 - c)) for ref in refs]
    return plan


def _chip_sum(g, other, sel, name):
    _, half, cdim = other.shape
    tr = _pick(half, (512, 256, 128, 64))
    per = half // tr

    def body(sel_ref, g_ref, t_ref, wire_ref, own_ref):
        total = g_ref[0] + t_ref[0]
        wire_ref[0] = total.astype(BF16)

        @pl.when(pl.program_id(1) == sel_ref[1])
        def _():
            own_ref[...] = total

    blk = pl.BlockSpec((1, tr, cdim), lambda i, p, sel_ref: (p, i, 0))
    return pl.pallas_call(
        body, name=name,
        grid_spec=pltpu.PrefetchScalarGridSpec(
            num_scalar_prefetch=1, grid=(per, N_CHIPS),
            in_specs=[pl.BlockSpec((1, tr, cdim), lambda i, p, sel_ref: (p, sel_ref[0] * per + i, 0)), blk],
            out_specs=[blk, pl.BlockSpec((tr, cdim), lambda i, p, sel_ref: (i, 0))]),
        out_shape=[jax.ShapeDtypeStruct(other.shape, BF16), jax.ShapeDtypeStruct((half, cdim), F32)],
        compiler_params=_params("parallel", "arbitrary"),
    )(sel, g, other)


def _final_sum(own, recv, sel, layer, into, name):
    half, cdim = own.shape
    tr = _pick(half, (512, 256, 128, 64))
    per = half // tr

    def body(sel_ref, own_ref, r0_ref, r1_ref, r2_ref, *rest):
        rest[-1][...] = ((own_ref[...] + r0_ref[0].astype(F32)) + r1_ref[0].astype(F32)) + r2_ref[0].astype(F32)

    part = lambda k: pl.BlockSpec((1, tr, cdim), lambda i, sel_ref, k=k: (k, i, 0))
    prior = [] if into is None else [into]
    return pl.pallas_call(
        body, name=name,
        grid_spec=pltpu.PrefetchScalarGridSpec(
            num_scalar_prefetch=1, grid=(per,),
            in_specs=[pl.BlockSpec((tr, cdim), lambda i, sel_ref: (i, 0)), part(0), part(1), part(2)]
            + [pl.BlockSpec(memory_space=pl.ANY)] * len(prior),
            out_specs=pl.BlockSpec((None, tr, cdim), lambda i, sel_ref: (layer, sel_ref[0] * per + i, 0))),
        out_shape=jax.ShapeDtypeStruct((DEPTH, 2 * half, cdim), F32),
        input_output_aliases={5: 0} if prior else {}, compiler_params=_params("parallel"),
    )(sel, own, recv, recv, recv, *prior)


def _row(v):
    return v.reshape(1, -1)


_BR_A, _BR_B, _BR_C = (0, A_WIDTH), (A_WIDTH, POOL_WIDTH), (A_WIDTH + POOL_WIDTH, CONV_WIDTH)


def _tie(v, token):
    return v if token is None else v + token.tie


def _no_hook(point, after, ready=None):
    return None


def _layer_fwd(x, w, mod, hook=_no_hook):
    s = x.shape[0]
    mod3 = mod.reshape(6, 1, D)
    h = _modnorm_fwd(x, _row(w["g_mix_pre"]), (mod3, 0), (mod3, 1), "mix_pre_fwd")
    z = _mm(h, w["w_all"], name="mm_in")
    qkv = z[:, Z_QKV:Z_PC].astype(BF16)
    fl = z[:, Z_FL:Z_COLS]
    cum = _cumf_fwd(fl, w["b_f_pad"])
    fr = cum[:, :HEADS].T.reshape(HEADS, s // ATT_K, ATT_K)
    br_a, lse = _attn_fwd(qkv, fr)
    br_b, br_c = _poolconv_fwd(z, w["w_pool_bd"], _tie(_row(w["pool_scale"]), hook("attn", lse)), w["conv_w"])
    hook("pool", br_b)
    wbr = w["w_branch"]
    pa = _mm(br_a, wbr, b_rows=_BR_A, name="mm_br_a")
    pb = _mm(br_b, wbr, b_rows=_BR_B, name="mm_br_b")
    pc = _mm(br_c, wbr, b_rows=_BR_C, name="mm_br_c")
    merged = _merge_fwd(z, pa, pb, pc)
    y = _mm(merged, w["w_out"], name="mm_out")
    x1, h2 = _post_pre_fwd(x, y, _row(w["g_mix_post"]), (mod3, 2), _row(w["g_ff_pre"]), (mod3, 3), (mod3, 4), "mix_post_ff_pre_fwd")
    a, r = _mm(h2, w["w_ff1"], b_split=N_CHIPS, epilogue=_relu2_fwd, out_dtype=(F32, BF16), name="mm_ff1")
    y2 = _mm(r, w["w_ff2"], name="mm_ff2")
    x2 = _post_fwd(x1, y2, _tie(_row(w["g_ff_post"]), hook("ff_post", y2)), (mod3, 5), "ff_post_fwd")
    hook("end", x2)
    saved = dict(x=x, h=h, z=z, qkv=qkv, fl=fl, fr=fr, lse=lse, br_a=br_a, br_b=br_b, br_c=br_c, pa=pa, pb=pb, pc=pc,
                 merged=merged, y=y, x1=x1, h2=h2, a=a, r=r, y2=y2)
    return x2, saved


def _layer_bwd(dx2, sv, w, mod, hook=_no_hook):
    s = dx2.shape[0]
    mod3 = mod.reshape(6, 1, D)
    dy2, sum_ff_post = _post_bwd(dx2, sv["y2"], _row(w["g_ff_post"]), (mod3, 5), "ff_post_bwd")
    (da,) = _mm(dy2, w["w_ff2"], tb=True, epilogue=_relu2_bwd, extras=(sv["a"],), out_dtype=(BF16,), name="mm_ff2_dx")
    d_w_ff2 = _mm(sv["r"], dy2, ta=True, name="mm_ff2_dw")
    dh2 = _mm(da, w["w_ff1"], tb=True, b_split=N_CHIPS, name="mm_ff1_dx")
    d_w_ff1 = _mm(sv["h2"], da, ta=True, out_split=N_CHIPS, name="mm_ff1_dw")
    g_ff_pre = _tie(_row(w["g_ff_pre"]), hook("ff_pre", dh2, dict(w_ff1=d_w_ff1, w_ff2=d_w_ff2)))
    dx1, dy, sum_mid = _pre_post_bwd(dh2, sv["x1"], dx2, g_ff_pre, (mod3, 4), sv["y"], _row(w["g_mix_post"]), (mod3, 2), "ff_pre_mix_post_bwd")
    sum_ff_pre, sum_mix_post = sum_mid, sum_mid[3:]
    dmerged = _mm(dy, w["w_out"], tb=True, name="mm_out_dx")
    d_w_out = _mm(sv["merged"], dy, ta=True, name="mm_out_dw")
    dz, dpa, dpb, dpc = _merge_bwd(dmerged, sv["z"], sv["pa"], sv["pb"], sv["pc"])
    wbr = w["w_branch"]
    dbr_a = _mm(dpa, wbr, tb=True, b_rows=_BR_A, name="mm_br_a_dx")
    dbr_b = _mm(dpb, wbr, tb=True, b_rows=_BR_B, name="mm_br_b_dx")
    dbr_c = _mm(dpc, wbr, tb=True, b_rows=_BR_C, name="mm_br_c_dx")
    d_w_branch = jnp.concatenate([_mm(sv["br_a"], dpa, ta=True, name="mm_br_a_dw"), _mm(sv["br_b"], dpb, ta=True, name="mm_br_b_dw"),
                                  _mm(sv["br_c"], dpc, ta=True, name="mm_br_c_dw")], axis=0)

    dq, dk, dv, dfc, dfr = _attn_bwd(sv["qkv"], dbr_a, sv["br_a"], sv["lse"], sv["fr"])
    dcum = dfc + jnp.pad(dfr.reshape(HEADS, s).T, ((0, 0), (0, 128 - HEADS)))
    dfl, sum_bf = _cumf_bwd(dcum, sv["fl"], _tie(w["b_f_pad"], hook("cumf", dfc)))
    dpc_z, d_wbd, sum_ps, sum_cw = _poolconv_bwd(dbr_b, dbr_c, sv["z"], w["w_pool_bd"], _row(w["pool_scale"]), w["conv_w"])
    for at, part in ((Z_QKV, dq), (Z_QKV + A_WIDTH, dk), (Z_QKV + 2 * A_WIDTH, dv), (Z_PC, dpc_z), (Z_FL, dfl)):
        dz = lax.dynamic_update_slice(dz, part, (0, at))
    dh = _mm(dz, w["w_all"], tb=True, name="mm_in_dx")
    d_w_all = _mm(sv["h"], dz, ta=True, name="mm_in_dw")
    hook("mix_pre", dh)
    dx, sum_mix_pre = _modnorm_bwd(dh, sv["x"], dx1, _row(w["g_mix_pre"]), (mod3, 1), "mix_pre_bwd")

    dmod = jnp.stack([sum_mix_pre[0], sum_mix_pre[1], sum_mix_post[0], sum_ff_pre[0], sum_ff_pre[1], sum_ff_post[0]])
    d_w_in = _w_in_shards(d_w_all)
    d_w_pool = jnp.stack([d_wbd[64 * g:64 * g + 64, 64 * g:64 * g + 64] for g in range(4)])
    big = dict(w_in=d_w_in, w_branch=d_w_branch, w_out=d_w_out, w_ff1=d_w_ff1, w_ff2=d_w_ff2)
    small = dict(g_mix_pre=sum_mix_pre[2], g_mix_post=sum_mix_post[1], g_ff_pre=sum_ff_pre[2], g_ff_post=sum_ff_post[1],
                 b_f=sum_bf[0, :HEADS], w_pool=d_w_pool, pool_scale=sum_ps[0], conv_w=sum_cw[0:3])
    return dx, dmod, big, small


_QKV_END, _FL_END, _PC_END = 3 * A_WIDTH, 3 * A_WIDTH + HEADS, 3 * A_WIDTH + HEADS + POOL_WIDTH + 3 * CONV_WIDTH
_W_IN_GROUPS = ((_PC_END, IN_COLS, Z_GL), (0, _QKV_END, Z_QKV), (_FL_END, _PC_END, Z_PC), (_QKV_END, _FL_END, Z_FL))
_SHARD_COLS = IN_COLS // N_CHIPS


def _w_all_from_shards(blocks):
    pieces = []
    for lo, hi, _ in _W_IN_GROUPS:
        for p in range(N_CHIPS):
            a, b = max(lo, p * _SHARD_COLS), min(hi, (p + 1) * _SHARD_COLS)
            if a < b:
                pieces.append(blocks[p][:, a - p * _SHARD_COLS:b - p * _SHARD_COLS])
    pieces.append(jnp.zeros((D, Z_COLS - IN_COLS), blocks.dtype))
    return jnp.concatenate(pieces, axis=1)


def _w_in_shards(d_w_all):
    blocks = []
    for p in range(N_CHIPS):
        pieces = []
        for lo, hi, at in sorted(_W_IN_GROUPS):
            a, b = max(lo, p * _SHARD_COLS), min(hi, (p + 1) * _SHARD_COLS)
            if a < b:
                pieces.append(d_w_all[:, at + a - lo:at + b - lo])
        blocks.append(jnp.concatenate(pieces, axis=1))
    return jnp.stack(blocks)


def _full_layer_weights(w_in_blocks, w_branch, w_out, w_ff1, w_ff2, g_mix_pre, g_mix_post, g_ff_pre, g_ff_post, b_f, w_pool, pool_scale, conv_w):
    w_all = _w_all_from_shards(w_in_blocks)
    wbd = (w_pool[:, :, None, :] * jnp.eye(4, dtype=F32)[:, None, :, None]).reshape(POOL_WIDTH, POOL_WIDTH)
    return dict(w_all=w_all, w_branch=w_branch, w_out=w_out, w_ff1=w_ff1, w_ff2=w_ff2, g_mix_pre=g_mix_pre, g_mix_post=g_mix_post,
                g_ff_pre=g_ff_pre, g_ff_post=g_ff_post, b_f_pad=jnp.pad(b_f, (0, 128 - HEADS)).reshape(1, 128), w_pool_bd=wbd,
                pool_scale=pool_scale, conv_w=conv_w)


class _NoComm:
    def layer_weights(self, l):
        raise NotImplementedError

    def fwd_hook(self, l):
        return _no_hook

    def bwd_hook(self, l):
        return _no_hook

    def grads_ready(self, l, big):
        return None


class _Layers(_NoComm):
    def __init__(self, layers):
        self.layers = layers

    def layer_weights(self, l):
        return self.layers[l]


def _local_step(x, target, mods, comm):
    saved, weights = [], []
    act = x
    for l in range(DEPTH):
        weights.append(comm.layer_weights(l))
        act, sv = _layer_fwd(act, weights[l], mods[l], comm.fwd_hook(l))
        saved.append(sv)
    dact, sq = _loss_head(act, target)
    loss = sq[0, 0] * (0.5 / D)
    dmods, bigs, smalls = [None] * DEPTH, [None] * DEPTH, [None] * DEPTH
    token = None
    for l in reversed(range(DEPTH)):
        dact, dmods[l], bigs[l], smalls[l] = _layer_bwd(dact, saved[l], weights[l], _tie(mods[l], token), comm.bwd_hook(l))
        token = comm.grads_ready(l, bigs[l])
    return loss, dact, jnp.stack(dmods), bigs, smalls


_BIG = ("w_in", "w_branch", "w_out", "w_ff1", "w_ff2")


class _GatherJob:
    def __init__(self, tag, shards, after):
        self.tag, self.n = tag, len(shards)
        lands = [lax.empty((N_CHIPS,) + s.shape, s.dtype) for s in shards]
        self.state = _copies_start(list(shards) + lands, _plan_gather_ici, 4 * self.n, after, "gather_ici_start_" + tag)
        self.token = self.state[3]

    def pass_on(self, after):
        bufs = _copies_wait(self.state, _plan_gather_ici, after, "gather_ici_wait_" + self.tag)
        self.state = _copies_start(bufs[self.n:], _plan_gather_d2d, 3 * self.n, bufs[0], "gather_d2d_start_" + self.tag)
        self.token = self.state[3]
        return self.token

    def done(self, after):
        return _copies_wait(self.state, _plan_gather_d2d, after, "gather_d2d_wait_" + self.tag)


class _ReduceJob:
    def __init__(self, tag, names, grads, sel, after, layer, into=None):
        self.tag, self.names, self.n, self.sel, self.layer, self.into = tag, names, len(names), sel, layer, into or {}
        lands = [lax.empty((N_CHIPS, g.shape[1] // 2, g.shape[2]), F32) for g in grads]
        self.state = _copies_start(list(grads) + lands, _plan_rs_sibling, self.n, after, "rs_sibling_start_" + tag)
        self.token = self.state[3]

    def chip_sums(self, after):
        bufs = _copies_wait(self.state, _plan_rs_sibling, after, "rs_sibling_wait_" + self.tag)
        wires, self.owns = zip(*[_chip_sum(bufs[i], bufs[self.n + i], self.sel, "rs_chip_sum_" + name) for i, name in enumerate(self.names)])
        lands = [lax.empty((3,) + w.shape[1:], BF16) for w in wires]
        self.state = _copies_start(list(wires) + lands, _plan_rs_chips, 3 * self.n, self.owns[0], "rs_chips_start_" + self.tag)
        self.token = self.state[3]
        return self.token

    def final_sums(self, after):
        bufs = _copies_wait(self.state, _plan_rs_chips, after, "rs_chips_wait_" + self.tag)
        sums = [_final_sum(self.owns[i], bufs[self.n + i], self.sel, self.layer, self.into.get(name), "rs_final_" + name)
                for i, name in enumerate(self.names)]
        self.state = _copies_start(sums, _plan_rs_share(self.layer), self.n, sums[0], "rs_share_start_" + self.tag)
        self.token = self.state[3]
        return self.token

    def done(self, after):
        return dict(zip(self.names, _copies_wait(self.state, _plan_rs_share(self.layer), after, "rs_share_wait_" + self.tag)))


def _chip_blocks(g):
    return g if g.ndim == 3 else g.reshape(N_CHIPS, -1, g.shape[1])


class _StepComm(_NoComm):
    def __init__(self, shards, sel, after):
        self.sel = sel
        self.small, self.grads, self.jobs = None, {}, {}
        self.jobs["in0"] = _GatherJob("in0", shards[0][:1], after)
        self.jobs["rest0"] = _GatherJob("rest0", shards[0][1:], self.jobs["in0"].token)
        self.jobs["all1"] = _GatherJob("all1", shards[1], self.jobs["rest0"].token)

    def layer_weights(self, l):
        if l == 0:
            job = self.jobs["in0"]
            (g_in,) = job.done(job.pass_on(self.jobs["all1"].token))
            self.weights0 = _full_layer_weights(g_in, None, None, None, None, *self.small[0])
            return self.weights0
        g_in, g_br, g_out, g_f1, g_f2 = self.landed1
        return _full_layer_weights(g_in, g_br.reshape(D, D), g_out.reshape(D, D), g_f1, g_f2.reshape(D_FF, D), *self.small[1])

    def fwd_hook(self, l):
        if l != 0:
            return _no_hook

        def hook(point, after, ready=None):
            if point == "attn":
                return self.jobs["rest0"].pass_on(after)
            if point == "ff_post":
                return self.jobs["all1"].pass_on(after)
            if point == "pool":
                g_br, g_out, g_f1, g_f2 = self.jobs["rest0"].done(after)
                self.weights0.update(w_branch=g_br.reshape(D, D), w_out=g_out.reshape(D, D), w_ff1=g_f1, w_ff2=g_f2.reshape(D_FF, D))
            if point == "end":
                self.landed1 = self.jobs["all1"].done(after)
            return None
        return hook

    def bwd_hook(self, l):
        if l != 0:
            return _no_hook

        def hook(point, after, ready=None):
            jobs = self.jobs
            if point == "ff_pre":
                token = jobs["rs1"].chip_sums(after)
                jobs["rs0_ff"] = _ReduceJob("0_ff", ("w_ff1", "w_ff2"), [_chip_blocks(ready[n]) for n in ("w_ff1", "w_ff2")], self.sel, token, 0)
                return jobs["rs0_ff"].token
            if point == "cumf":
                return jobs["rs0_ff"].chip_sums(jobs["rs1"].final_sums(after))
            self.layer1 = jobs["rs1"].done(after)
            jobs["rs0_ff"].into = self.layer1
            return None
        return hook

    def grads_ready(self, l, big):
        if l == 1:
            self.jobs["rs1"] = _ReduceJob("1", _BIG, [_chip_blocks(big[n]) for n in _BIG], self.sel, self.sel, 1)
            return self.jobs["rs1"].token
        names = ("w_in", "w_branch", "w_out")
        self.jobs["rs0_mix"] = _ReduceJob("0_mix", names, [_chip_blocks(big[n]) for n in names], self.sel, self.sel, 0, self.layer1)
        return self.jobs["rs0_mix"].token

    def finish_sums(self, after):
        jobs = self.jobs
        token = jobs["rs0_mix"].chip_sums(after)
        return jobs["rs0_ff"].final_sums(token)

    def finish_ff(self, after):
        self.grads.update(self.jobs["rs0_ff"].done(after))

    def finish_mix(self, after):
        job = self.jobs["rs0_mix"]
        self.grads.update(job.done(job.final_sums(after)))


_SMALL = ("g_mix_pre", "g_mix_post", "g_ff_pre", "g_ff_post", "b_f", "w_pool", "pool_scale", "conv_w")


def _w_in_view(t):
    return t.reshape(DEPTH, D // 128, 128, _SHARD_COLS).transpose(3, 1, 0, 2).reshape(_SHARD_COLS * (D // 128) * DEPTH, 128)


def _w_in_unview(t):
    return t.reshape(_SHARD_COLS, D // 128, DEPTH, 128).transpose(2, 1, 3, 0).reshape(DEPTH, D, _SHARD_COLS)


def _pack(parts, rows=8):
    flat = jnp.concatenate([p.reshape(-1) for p in parts])
    width = -(-flat.shape[0] // (rows * 128)) * 128
    return jnp.pad(flat, (0, rows * width - flat.shape[0])).reshape(rows, width)


def _unpack(packed, like):
    flat = packed.reshape(-1)
    out, at = [], 0
    for ref in like:
        out.append(flat[at:at + ref.size].reshape(ref.shape))
        at += ref.size
    return out


def kernel(x, c, w_ada, b_ada, g_mix_pre, g_mix_post, g_ff_pre, g_ff_post, w_in, b_f, w_pool, pool_scale, conv_w, w_branch, w_out, w_ff1, w_ff2, loss_target, m_w_ada, m_b_ada, m_g_mix_pre, m_g_mix_post, m_g_ff_pre, m_g_ff_post, m_w_in, m_b_f, m_w_pool, m_pool_scale, m_conv_w, m_w_branch, m_w_out, m_w_ff1, m_w_ff2, v_w_ada, v_b_ada, v_g_mix_pre, v_g_mix_post, v_g_ff_pre, v_g_ff_post, v_w_in, v_b_f, v_w_pool, v_pool_scale, v_conv_w, v_w_branch, v_w_out, v_w_ff1, v_w_ff2):
    xi, yi, ci = lax.axis_index("x"), lax.axis_index("y"), lax.axis_index("c")
    chip = 2 * xi + yi
    dev = 2 * chip + ci
    n_ada = w_ada.shape[2]

    first = jnp.zeros((8, D + 384), F32).at[0, :D].set(c[0]).at[0, D:].set(conv_w.reshape(-1))
    got = _allgather8(first, "gather_cond").reshape(N_DEV, 8, D + 384)[:, 0]
    c_all = got[:, :D]
    conv_full = got[0::2, D:].reshape(N_CHIPS, DEPTH, 3, CONV_WIDTH // N_CHIPS).transpose(1, 2, 0, 3).reshape(DEPTH, 3, CONV_WIDTH)

    b_loc = lax.dynamic_slice_in_dim(b_ada, chip * n_ada, n_ada, axis=1).reshape(DEPTH, 1, n_ada)
    mod_cols, silu_c = _ada_fwd(c_all, w_ada, b_loc)
    got = _allgather8(mod_cols.reshape(DEPTH * N_DEV, n_ada), "gather_mod").reshape(N_DEV, DEPTH, N_DEV, n_ada)[0::2]
    mod_all = got.transpose(1, 2, 0, 3).reshape(DEPTH, N_DEV, 6, D)
    mods = lax.dynamic_index_in_dim(mod_all, dev, axis=1, keepdims=False)

    comm = _StepComm([[w[l].astype(BF16) for w in (w_in, w_branch, w_out, w_ff1, w_ff2)] for l in range(DEPTH)],
                     jnp.stack([ci, chip]).astype(jnp.int32), mods)
    comm.small = [(g_mix_pre[l], g_mix_post[l], g_ff_pre[l], g_ff_post[l], b_f[l], w_pool[l], pool_scale[l], conv_full[l]) for l in range(DEPTH)]
    loss_part, grad_x, dmods, bigs, smalls = _local_step(x[0], loss_target[0], mods, comm)

    small_parts = [smalls[l][name] for name in _SMALL for l in range(DEPTH)] + [loss_part.reshape(1)]
    packed = _tie(_pack([dmods] + small_parts), comm.jobs["rs0_mix"].token)
    gathered = _allgather8(packed, "gather_small")
    dmod_all = gathered.reshape(N_DEV, -1)[:, :dmods.size].reshape(N_DEV, DEPTH, 6 * D)
    summed = _unpack(_sum_devices(gathered), [dmods] + small_parts)
    grad_b_ada = summed[0].reshape(DEPTH, 6 * D)
    loss = summed[-1][0]
    small_grads = {name: jnp.stack(summed[1 + 2 * i:3 + 2 * i]) for i, name in enumerate(_SMALL)}
    small_grads["conv_w"] = lax.dynamic_slice_in_dim(small_grads["conv_w"], chip * (CONV_WIDTH // N_CHIPS), CONV_WIDTH // N_CHIPS, axis=2)

    dmod_loc = lax.dynamic_slice_in_dim(dmod_all.transpose(1, 0, 2), chip * n_ada, n_ada, axis=2)
    tail_token = comm.finish_sums(grad_b_ada)
    silu_pad = _tie(jnp.pad(silu_c, ((0, 128 - N_DEV), (0, 0))), tail_token)
    dmod_pad = jnp.pad(dmod_loc.transpose(1, 0, 2).reshape(N_DEV, DEPTH * n_ada), ((0, 128 - N_DEV), (0, 0)))
    grad_w_ada = _mm(silu_pad, dmod_pad, ta=True, out_split=DEPTH, name="mm_ada_dw")

    grads = dict(w_ada=grad_w_ada, b_ada=grad_b_ada, **small_grads)
    weights = dict(w_ada=w_ada, b_ada=b_ada, g_mix_pre=g_mix_pre, g_mix_post=g_mix_post, g_ff_pre=g_ff_pre, g_ff_post=g_ff_post, w_in=w_in,
                   b_f=b_f, w_pool=w_pool, pool_scale=pool_scale, conv_w=conv_w, w_branch=w_branch, w_out=w_out, w_ff1=w_ff1, w_ff2=w_ff2)
    m_in = dict(w_ada=m_w_ada, b_ada=m_b_ada, g_mix_pre=m_g_mix_pre, g_mix_post=m_g_mix_post, g_ff_pre=m_g_ff_pre, g_ff_post=m_g_ff_post,
                w_in=m_w_in, b_f=m_b_f, w_pool=m_w_pool, pool_scale=m_pool_scale, conv_w=m_conv_w, w_branch=m_w_branch, w_out=m_w_out,
                w_ff1=m_w_ff1, w_ff2=m_w_ff2)
    v_in = dict(w_ada=v_w_ada, b_ada=v_b_ada, g_mix_pre=v_g_mix_pre, g_mix_post=v_g_mix_post, g_ff_pre=v_g_ff_pre, g_ff_post=v_g_ff_post,
                w_in=v_w_in, b_f=v_b_f, w_pool=v_w_pool, pool_scale=v_pool_scale, conv_w=v_conv_w, w_branch=v_w_branch, w_out=v_w_out,
                w_ff1=v_w_ff1, w_ff2=v_w_ff2)
    order = ("w_ada", "b_ada", "g_mix_pre", "g_mix_post", "g_ff_pre", "g_ff_post", "w_in", "b_f", "w_pool", "pool_scale", "conv_w",
             "w_branch", "w_out", "w_ff1", "w_ff2")
    delta, new_m, new_v = {}, {}, {}
    tiny = ("b_ada",) + _SMALL
    tiny_g = [_tie(grads[tiny[0]], tail_token)] + [grads[name] for name in tiny[1:]]
    res = _adamw_many([weights[name] for name in tiny], tiny_g, [m_in[name] for name in tiny], [v_in[name] for name in tiny], "adamw_small")
    for out, vals in zip((delta, new_m, new_v), res):
        out.update(zip(tiny, vals))
    delta["w_ada"], new_m["w_ada"], new_v["w_ada"] = _adamw(w_ada, grad_w_ada, m_w_ada, v_w_ada, "adamw_w_ada")
    comm.finish_ff(delta["w_ada"][0, :8, :128] + delta["b_ada"][0, :128])
    for name in ("w_ff1", "w_ff2", "w_in", "w_branch", "w_out"):
        if name == "w_in":
            comm.finish_mix(delta["w_ff2"][0, :8, :128])
        grads[name] = comm.grads[name]
        if name == "w_in":
            g_view = lax.optimization_barrier(_w_in_view(grads[name]))
            res = _adamw(_w_in_view(w_in), g_view, _w_in_view(m_w_in), _w_in_view(v_w_in), "adamw_w_in")
            grads[name], delta[name], new_m[name], new_v[name] = [_w_in_unview(t) for t in (g_view, *res)]
        else:
            delta[name], new_m[name], new_v[name] = _adamw(weights[name], grads[name], m_in[name], v_in[name], "adamw_" + name)

    return (loss, grad_x[None], *[grads[n] for n in order], *[delta[n] for n in order], *[new_m[n] for n in order],
            *[new_v[n] for n in order])
```

```python
from typing import NamedTuple

import jax
import jax.numpy as jnp
from jax import lax
from jax.experimental import pallas as pl
from jax.experimental.pallas import tpu as pltpu

F32 = jnp.float32
BF16 = jnp.bfloat16
MESH = pl.DeviceIdType.MESH

D = 1024
DEPTH = 2
HEADS = 8
HEAD_DIM = 64
A_WIDTH = 512
POOL_WIDTH = 256
CONV_WIDTH = 256
D_FF = 4096
IN_COLS = 5640
Z_GL, Z_QKV, Z_PC, Z_FL, Z_COLS = 0, 3072, 4608, 5632, 5760
RMS_EPS = 1e-6
NEG_INF = -1e30
ROW_TILE = 512
EW_ROWS = 256
N_CHIPS = 4
N_DEV = 8
V7X_VMEM_LIMIT = 48 * 1024 * 1024

ADAM_LR = 0.001
ADAM_B1 = 0.9
ADAM_B2 = 0.999
ADAM_EPS = 1e-08
ADAM_WD = 0.01
ADAM_STEP = 10

_HBM = pl.BlockSpec(memory_space=pltpu.HBM)


def _params(*sem):
    return pltpu.CompilerParams(dimension_semantics=sem, vmem_limit_bytes=V7X_VMEM_LIMIT)


def _pick(dim, cands):
    for cand in cands:
        if dim % cand == 0:
            return cand
    return dim


MM_TILE_BUDGET = 39 * 1024 * 1024


def _mm_tiles(m, n, k, k_unit, tn, a_size, b_size, out_size):
    for tk in (k_unit, 2048, 1152, 1024, 640, 512, 256, 128):
        if k_unit % tk:
            continue
        for tm in (2048, 1024, 512, 256, 128):
            if m % tm or ((m // tm) * (n // tn) < 2 and tm > 512):
                continue
            need = 2 * (tm * tk * a_size + tk * tn * b_size + tm * tn * out_size) + (0 if tk == k else 4 * tm * tn)
            if need <= MM_TILE_BUDGET and (tk == k_unit or tm >= 512):
                return tm, tk
    return 128, 128


def _mm(a, b, *, ta=False, tb=False, b_rows=None, b_split=1, out_split=1, out_dtype=F32, epilogue=None, extras=(), name):
    (k, m) = a.shape if ta else a.shape[::-1]
    b_row0, b_rows = (0, b.shape[-2]) if b_rows is None else b_rows
    b_cols = b.shape[-1] * b_split
    (n, k2) = (b_rows, b_cols) if tb else (b_cols, b_rows)
    assert k == k2, (a.shape, b.shape, ta, tb)
    n_unit = n // (out_split * (1 if tb else b_split))
    k_unit = k // (b_split if tb else 1)
    tn = _pick(n_unit, (1024, 1152, 768, 640, 512, 256, 128))
    tm, tk = _mm_tiles(m, n, k, k_unit, tn, a.dtype.itemsize, b.dtype.itemsize,
                       sum(jnp.dtype(dt).itemsize for dt in out_dtype) + 4 * len(extras) if epilogue else jnp.dtype(out_dtype).itemsize)
    nk = k // tk
    dims = (((0 if ta else 1,), (1 if tb else 0,)), ((), ()))

    def dot(a_ref, b_ref):
        b_val = b_ref[0] if b_split > 1 else b_ref[...]
        return lax.dot_general(a_ref[...].astype(BF16), b_val.astype(BF16), dims, preferred_element_type=F32)

    n_extra = len(extras)
    assert epilogue is None or out_split == 1

    def put(refs, val):
        if epilogue is not None:
            for o_ref, res in zip(refs[n_extra:], epilogue(val, *[r[...] for r in refs[:n_extra]])):
                o_ref[...] = res.astype(o_ref.dtype)
        elif out_split > 1:
            refs[0][0] = val.astype(refs[0].dtype)
        else:
            refs[0][...] = val.astype(refs[0].dtype)

    def body_single(a_ref, b_ref, *refs):
        put(refs, dot(a_ref, b_ref))

    def body_acc(a_ref, b_ref, *refs):
        kk = pl.program_id(2)
        acc_ref = refs[-1]

        @pl.when(kk == 0)
        def _():
            acc_ref[...] = jnp.zeros_like(acc_ref)

        acc_ref[...] += dot(a_ref, b_ref)

        @pl.when(kk == nk - 1)
        def _():
            put(refs[:-1], acc_ref[...])

    a_spec = pl.BlockSpec((tk, tm), lambda i, j, kk: (kk, i)) if ta else pl.BlockSpec((tm, tk), lambda i, j, kk: (i, kk))
    if b_split == 1:
        off = b_row0 // (tn if tb else tk)
        assert off * (tn if tb else tk) == b_row0
        b_spec = pl.BlockSpec((tn, tk), lambda i, j, kk: (j + off, kk)) if tb else pl.BlockSpec((tk, tn), lambda i, j, kk: (kk + off, j))
    elif tb:
        per = k_unit // tk
        b_spec = pl.BlockSpec((1, tn, tk), lambda i, j, kk: (kk // per, j, kk % per))
    else:
        per = n // b_split // tn
        b_spec = pl.BlockSpec((1, tk, tn), lambda i, j, kk: (j // per, kk, j % per))
    if out_split == 1:
        o_spec = pl.BlockSpec((tm, tn), lambda i, j, kk: (i, j))
        o_shape = None if epilogue is not None else jax.ShapeDtypeStruct((m, n), out_dtype)
    else:
        per_o = n // out_split // tn
        o_spec = pl.BlockSpec((1, tm, tn), lambda i, j, kk: (j // per_o, i, j % per_o))
        o_shape = jax.ShapeDtypeStruct((out_split, m, n // out_split), out_dtype)
    if epilogue is not None:
        o_shape = [jax.ShapeDtypeStruct((m, n), dt) for dt in out_dtype]
        o_spec = [o_spec] * len(out_dtype)
    return pl.pallas_call(
        body_single if nk == 1 else body_acc, name=name, grid=(m // tm, n // tn, nk),
        in_specs=[a_spec, b_spec] + [pl.BlockSpec((tm, tn), lambda i, j, kk: (i, j))] * n_extra, out_specs=o_spec, out_shape=o_shape,
        scratch_shapes=[] if nk == 1 else [pltpu.VMEM((tm, tn), F32)],
        compiler_params=_params("parallel", "parallel", "arbitrary"),
    )(a, b, *extras)


def _ew(fn, ins, out_dtypes, name, tc=None):
    shape = ins[0].shape
    lead, (rows, cols) = shape[:-2], shape[-2:]
    tc = cols if tc is None else tc
    if tc > 1024:
        tr = _pick(rows, (EW_ROWS, 128, 8))
    elif tc > 128:
        tr = _pick(rows, (2 * EW_ROWS, EW_ROWS, 128, 8))
    else:
        tr = _pick(rows, (4096, 2256, 2048, 1024, EW_ROWS, 8))
    n_in = len(ins)

    def body(*refs):
        res = fn(*[r[...] for r in refs[:n_in]])
        for o_ref, val in zip(refs[n_in:], res):
            o_ref[...] = val.astype(o_ref.dtype)

    if lead:
        spec = pl.BlockSpec((None, tr, tc), lambda l, i, j: (l, i, j))
    else:
        spec = pl.BlockSpec((tr, tc), lambda i, j: (i, j))
    return pl.pallas_call(
        body, name=name, grid=lead + (rows // tr, cols // tc),
        in_specs=[spec] * n_in, out_specs=[spec] * len(out_dtypes),
        out_shape=[jax.ShapeDtypeStruct(shape, dt) for dt in out_dtypes],
        compiler_params=_params(*(["parallel"] * (len(lead) + 2))),
    )(*ins)


def _relu2_fwd(a):
    r = jnp.maximum(a, 0.0)
    return a, r * r


def _relu2_bwd(dr, a):
    return (dr * (2.0 * jnp.maximum(a, 0.0)),)


def _adamw_math(w, g, m, v):
    m = ADAM_B1 * m + (1.0 - ADAM_B1) * g
    v = ADAM_B2 * v + (1.0 - ADAM_B2) * (g * g)
    m_hat = m / (1.0 - ADAM_B1 ** ADAM_STEP)
    v_hat = v / (1.0 - ADAM_B2 ** ADAM_STEP)
    delta = -ADAM_LR * (m_hat / (jnp.sqrt(v_hat) + ADAM_EPS) + ADAM_WD * w)
    return delta, m, v


def _adamw(w, g, m, v, name):
    return _ew(_adamw_math, [w, g, m, v], [F32, F32, F32], name)


def _adamw_many(ws, gs, ms, vs, name):
    n = len(ws)

    def body(*refs):
        for i in range(n):
            res = _adamw_math(*[refs[k * n + i][...] for k in range(4)])
            for k in range(3):
                refs[(4 + k) * n + i][...] = res[k]

    outs = pl.pallas_call(
        body, name=name, out_shape=[jax.ShapeDtypeStruct(w.shape, F32) for w in ws] * 3,
        compiler_params=pltpu.CompilerParams(vmem_limit_bytes=V7X_VMEM_LIMIT),
    )(*ws, *gs, *ms, *vs)
    return outs[:n], outs[n:2 * n], outs[2 * n:]


def _row_spec(cols, block=0):
    return pl.BlockSpec((ROW_TILE, cols), lambda i, block=block: (i, block))


def _vec_spec(cols):
    return pl.BlockSpec((1, cols), lambda i: (0, 0))


def _vec_args(*vecs):
    arrays = [v[0] if isinstance(v, tuple) else v for v in vecs]
    specs = [pl.BlockSpec((None, 1, D), lambda i, row=v[1]: (row, 0, 0)) if isinstance(v, tuple) else _vec_spec(D) for v in vecs]
    return arrays, specs


def _sum_spec(cols):
    return pl.BlockSpec((8, cols), lambda i: (0, 0))


def _rstd(x):
    return lax.rsqrt(jnp.mean(x * x, axis=-1, keepdims=True) + RMS_EPS)


def _modnorm_fwd(x, g, shift, scale, name):
    s = x.shape[0]

    def body(x_ref, g_ref, sh_ref, sc_ref, h_ref):
        xv = x_ref[...]
        n = xv * _rstd(xv)
        h_ref[...] = ((n * g_ref[...]) * (1.0 + sc_ref[...]) + sh_ref[...]).astype(BF16)

    vecs, vec_specs = _vec_args(g, shift, scale)
    return pl.pallas_call(
        body, name=name, grid=(s // ROW_TILE,),
        in_specs=[_row_spec(D)] + vec_specs, out_specs=_row_spec(D),
        out_shape=jax.ShapeDtypeStruct((s, D), BF16), compiler_params=_params("parallel"),
    )(x, *vecs)


def _post_fwd(x, y, g, gate, name):
    s = x.shape[0]

    def body(x_ref, y_ref, g_ref, gate_ref, o_ref):
        yv = y_ref[...]
        o_ref[...] = x_ref[...] + gate_ref[...] * ((yv * _rstd(yv)) * g_ref[...])

    vecs, vec_specs = _vec_args(g, gate)
    return pl.pallas_call(
        body, name=name, grid=(s // ROW_TILE,),
        in_specs=[_row_spec(D), _row_spec(D)] + vec_specs, out_specs=_row_spec(D),
        out_shape=jax.ShapeDtypeStruct((s, D), F32), compiler_params=_params("parallel"),
    )(x, y, *vecs)


def _post_bwd(dxo, y, g, gate, name):
    s = dxo.shape[0]

    def body(d_ref, y_ref, g_ref, gate_ref, dy_ref, sum_ref):
        @pl.when(pl.program_id(0) == 0)
        def _():
            sum_ref[...] = jnp.zeros_like(sum_ref)

        dv, yv = d_ref[...], y_ref[...]
        r = _rstd(yv)
        n = yv * r
        sum_ref[0:1, :] += jnp.sum(dv * (n * g_ref[...]), axis=0, keepdims=True)
        sum_ref[1:2, :] += jnp.sum((dv * gate_ref[...]) * n, axis=0, keepdims=True)
        dn = (dv * gate_ref[...]) * g_ref[...]
        dy_ref[...] = (r * (dn - n * jnp.mean(dn * n, axis=-1, keepdims=True))).astype(BF16)

    vecs, vec_specs = _vec_args(g, gate)
    return pl.pallas_call(
        body, name=name, grid=(s // ROW_TILE,),
        in_specs=[_row_spec(D), _row_spec(D)] + vec_specs,
        out_specs=[_row_spec(D), _sum_spec(D)],
        out_shape=[jax.ShapeDtypeStruct((s, D), BF16), jax.ShapeDtypeStruct((8, D), F32)],
        compiler_params=_params("arbitrary"),
    )(dxo, y, *vecs)


def _modnorm_bwd(dh, x, dxo, g, scale, name):
    s = dh.shape[0]

    def body(dh_ref, x_ref, d_ref, g_ref, sc_ref, dx_ref, sum_ref):
        @pl.when(pl.program_id(0) == 0)
        def _():
            sum_ref[...] = jnp.zeros_like(sum_ref)

        dhv, xv = dh_ref[...], x_ref[...]
        r = _rstd(xv)
        n = xv * r
        one_sc = 1.0 + sc_ref[...]
        sum_ref[0:1, :] += jnp.sum(dhv, axis=0, keepdims=True)
        sum_ref[1:2, :] += jnp.sum(dhv * (n * g_ref[...]), axis=0, keepdims=True)
        sum_ref[2:3, :] += jnp.sum((dhv * one_sc) * n, axis=0, keepdims=True)
        dn = (dhv * one_sc) * g_ref[...]
        dx_ref[...] = d_ref[...] + r * (dn - n * jnp.mean(dn * n, axis=-1, keepdims=True))

    vecs, vec_specs = _vec_args(g, scale)
    return pl.pallas_call(
        body, name=name, grid=(s // ROW_TILE,),
        in_specs=[_row_spec(D), _row_spec(D), _row_spec(D)] + vec_specs,
        out_specs=[_row_spec(D), _sum_spec(D)],
        out_shape=[jax.ShapeDtypeStruct((s, D), F32), jax.ShapeDtypeStruct((8, D), F32)],
        compiler_params=_params("arbitrary"),
    )(dh, x, dxo, *vecs)


def _post_pre_fwd(x, y, g_post, gate, g_pre, shift, scale, name):
    s = x.shape[0]

    def body(x_ref, y_ref, gp_ref, gate_ref, g_ref, sh_ref, sc_ref, o_ref, h_ref):
        yv = y_ref[...]
        xo = x_ref[...] + gate_ref[...] * ((yv * _rstd(yv)) * gp_ref[...])
        o_ref[...] = xo
        h_ref[...] = (((xo * _rstd(xo)) * g_ref[...]) * (1.0 + sc_ref[...]) + sh_ref[...]).astype(BF16)

    vecs, vec_specs = _vec_args(g_post, gate, g_pre, shift, scale)
    return pl.pallas_call(
        body, name=name, grid=(s // ROW_TILE,),
        in_specs=[_row_spec(D), _row_spec(D)] + vec_specs, out_specs=[_row_spec(D), _row_spec(D)],
        out_shape=[jax.ShapeDtypeStruct((s, D), F32), jax.ShapeDtypeStruct((s, D), BF16)], compiler_params=_params("parallel"),
    )(x, y, *vecs)


def _pre_post_bwd(dh, x, dxo, g_pre, scale, y, g_post, gate, name):
    s = dh.shape[0]

    def body(dh_ref, x_ref, d_ref, y_ref, g_ref, sc_ref, gp_ref, gate_ref, dx_ref, dy_ref, sum_ref):
        @pl.when(pl.program_id(0) == 0)
        def _():
            sum_ref[...] = jnp.zeros_like(sum_ref)

        dhv, xv = dh_ref[...], x_ref[...]
        r = _rstd(xv)
        n = xv * r
        one_sc = 1.0 + sc_ref[...]
        sum_ref[0:1, :] += jnp.sum(dhv, axis=0, keepdims=True)
        sum_ref[1:2, :] += jnp.sum(dhv * (n * g_ref[...]), axis=0, keepdims=True)
        sum_ref[2:3, :] += jnp.sum((dhv * one_sc) * n, axis=0, keepdims=True)
        dn = (dhv * one_sc) * g_ref[...]
        dv = d_ref[...] + r * (dn - n * jnp.mean(dn * n, axis=-1, keepdims=True))
        dx_ref[...] = dv

        yv = y_ref[...]
        ry = _rstd(yv)
        ny = yv * ry
        sum_ref[3:4, :] += jnp.sum(dv * (ny * gp_ref[...]), axis=0, keepdims=True)
        sum_ref[4:5, :] += jnp.sum((dv * gate_ref[...]) * ny, axis=0, keepdims=True)
        dny = (dv * gate_ref[...]) * gp_ref[...]
        dy_ref[...] = (ry * (dny - ny * jnp.mean(dny * ny, axis=-1, keepdims=True))).astype(BF16)

    vecs, vec_specs = _vec_args(g_pre, scale, g_post, gate)
    return pl.pallas_call(
        body, name=name, grid=(s // ROW_TILE,),
        in_specs=[_row_spec(D)] * 4 + vec_specs,
        out_specs=[_row_spec(D), _row_spec(D), _sum_spec(D)],
        out_shape=[jax.ShapeDtypeStruct((s, D), F32), jax.ShapeDtypeStruct((s, D), BF16), jax.ShapeDtypeStruct((8, D), F32)],
        compiler_params=_params("arbitrary"),
    )(dh, x, dxo, y, *vecs)


def _loss_head(y, target):
    s = y.shape[0]

    def body(y_ref, t_ref, dy_ref, sum_ref):
        @pl.when(pl.program_id(0) == 0)
        def _():
            sum_ref[...] = jnp.zeros_like(sum_ref)

        err = y_ref[...] - t_ref[...]
        dy_ref[...] = err * (1.0 / D)
        sum_ref[...] += jnp.sum(err * err)

    return pl.pallas_call(
        body, name="loss_head", grid=(s // ROW_TILE,),
        in_specs=[_row_spec(D), _row_spec(D)],
        out_specs=[_row_spec(D), pl.BlockSpec((8, 128), lambda i: (0, 0))],
        out_shape=[jax.ShapeDtypeStruct((s, D), F32), jax.ShapeDtypeStruct((8, 128), F32)],
        compiler_params=_params("arbitrary"),
    )(y, target)


def _merge_fwd(z, pa, pb, pc):
    s = z.shape[0]

    def body(g0_ref, g1_ref, g2_ref, pa_ref, pb_ref, pc_ref, o_ref):
        o_ref[...] = (jax.nn.sigmoid(g0_ref[...]) * pa_ref[...] + jax.nn.sigmoid(g1_ref[...]) * pb_ref[...]
                      + jax.nn.sigmoid(g2_ref[...]) * pc_ref[...]).astype(BF16)

    return pl.pallas_call(
        body, name="merge_fwd", grid=(s // ROW_TILE,),
        in_specs=[_row_spec(D, 0), _row_spec(D, 1), _row_spec(D, 2), _row_spec(D), _row_spec(D), _row_spec(D)],
        out_specs=_row_spec(D), out_shape=jax.ShapeDtypeStruct((s, D), BF16),
        compiler_params=_params("parallel"),
    )(z, z, z, pa, pb, pc)


def _merge_bwd(dm, z, pa, pb, pc):
    s = z.shape[0]

    def body(dm_ref, g0_ref, g1_ref, g2_ref, pa_ref, pb_ref, pc_ref, dgl_ref, da_ref, db_ref, dc_ref):
        dmv = dm_ref[...]
        for i, (g_ref, p_ref, d_ref) in enumerate(((g0_ref, pa_ref, da_ref), (g1_ref, pb_ref, db_ref), (g2_ref, pc_ref, dc_ref))):
            gate = jax.nn.sigmoid(g_ref[...])
            dgl_ref[:, i * D:(i + 1) * D] = ((dmv * p_ref[...]) * (gate * (1.0 - gate))).astype(BF16)
            d_ref[...] = (dmv * gate).astype(BF16)

    return pl.pallas_call(
        body, name="merge_bwd", grid=(s // ROW_TILE,),
        in_specs=[_row_spec(D), _row_spec(D, 0), _row_spec(D, 1), _row_spec(D, 2), _row_spec(D), _row_spec(D), _row_spec(D)],
        out_specs=[_row_spec(3 * D), _row_spec(D), _row_spec(D), _row_spec(D)],
        out_shape=[jax.ShapeDtypeStruct((s, Z_COLS), BF16)] + [jax.ShapeDtypeStruct((s, D), BF16)] * 3,
        compiler_params=_params("parallel"),
    )(dm, z, z, z, pa, pb, pc)


def _shift_down(v, n):
    row = lax.broadcasted_iota(jnp.int32, v.shape, 0)
    return jnp.where(row >= n, pltpu.roll(v, n, axis=0), 0.0)


def _shift_up(v, n):
    s = v.shape[0]
    row = lax.broadcasted_iota(jnp.int32, v.shape, 0)
    return jnp.where(row < s - n, pltpu.roll(v, s - n, axis=0), 0.0)


def _log_sigmoid(v):
    return jnp.minimum(v, 0.0) - jnp.log1p(jnp.exp(-jnp.abs(v)))


def _cumf_fwd(fl, bias):
    s = fl.shape[0]

    def body(fl_ref, b_ref, o_ref):
        acc = _log_sigmoid(fl_ref[...] + b_ref[...])
        step = 1
        while step < s:
            acc = acc + _shift_down(acc, step)
            step *= 2
        o_ref[...] = acc

    return pl.pallas_call(body, name="cumf_fwd", out_shape=jax.ShapeDtypeStruct((s, 128), F32),
                          compiler_params=pltpu.CompilerParams(vmem_limit_bytes=V7X_VMEM_LIMIT))(fl, bias)


def _cumf_bwd(dcum, fl, bias):
    s = fl.shape[0]

    def body(d_ref, fl_ref, b_ref, dfl_ref, db_ref):
        acc = d_ref[...]
        step = 1
        while step < s:
            acc = acc + _shift_up(acc, step)
            step *= 2
        dfl = acc * jax.nn.sigmoid(-(fl_ref[...] + b_ref[...]))
        dfl_ref[...] = dfl.astype(BF16)
        db_ref[...] = jnp.broadcast_to(jnp.sum(dfl, axis=0, keepdims=True), (8, 128))

    return pl.pallas_call(
        body, name="cumf_bwd",
        out_shape=[jax.ShapeDtypeStruct((s, 128), BF16), jax.ShapeDtypeStruct((8, 128), F32)],
        compiler_params=pltpu.CompilerParams(vmem_limit_bytes=V7X_VMEM_LIMIT))(dcum, fl, bias)


def _pool_windows(v, shift):
    s2 = v + shift(v, 1)
    s4 = s2 + shift(s2, 2)
    s8 = s4 + shift(s4, 4)
    s16 = s8 + shift(s8, 8)
    group = lax.broadcasted_iota(jnp.int32, v.shape, 1) // 64
    return jnp.where(group == 0, s2, jnp.where(group == 1, s4, jnp.where(group == 2, s8, s16)))


def _pool_count(shape):
    group = lax.broadcasted_iota(jnp.int32, shape, 1) // 64
    window = jnp.where(group == 0, 2.0, jnp.where(group == 1, 4.0, jnp.where(group == 2, 8.0, 16.0)))
    t1 = (lax.broadcasted_iota(jnp.int32, shape, 0) + 1).astype(F32)
    return jnp.minimum(t1, window)


def _pc_specs(s):
    zcol = lambda blk: pl.BlockSpec((s, 256), lambda i, blk=blk: (0, blk))
    first = Z_PC // 256
    return [zcol(first), zcol(first + 1), zcol(first + 2), zcol(first + 3),
            pl.BlockSpec((256, 256), lambda i: (0, 0)), pl.BlockSpec((1, 256), lambda i: (0, 0)),
            pl.BlockSpec((3, 256), lambda i: (0, 0))]


def _poolconv_fwd(z, wbd, pscale, convw):
    s = z.shape[0]

    def body(pu_ref, ch_ref, cb_ref, cc_ref, w_ref, ps_ref, cw_ref, yb_ref, yc_ref):
        u = pu_ref[...]
        p = _pool_windows(u, _shift_down) / _pool_count(u.shape) - u
        yb = jnp.dot(p.astype(BF16), w_ref[...].astype(BF16), preferred_element_type=F32) * ps_ref[...]
        yb_ref[...] = yb.astype(BF16)
        uc = cc_ref[...] * ch_ref[...]
        cw = cw_ref[...]
        conv = cw[0:1, :] * _shift_down(uc, 2) + cw[1:2, :] * _shift_down(uc, 1) + cw[2:3, :] * uc
        yc_ref[...] = (cb_ref[...] * conv).astype(BF16)

    out = pl.BlockSpec((s, 256), lambda i: (0, 0))
    return pl.pallas_call(
        body, name="poolconv_fwd", grid=(1,), in_specs=_pc_specs(s), out_specs=[out, out],
        out_shape=[jax.ShapeDtypeStruct((s, 256), BF16)] * 2, compiler_params=_params("arbitrary"),
    )(z, z, z, z, wbd, pscale, convw)


def _poolconv_bwd(dyb, dyc, z, wbd, pscale, convw):
    s = z.shape[0]

    def body(dyb_ref, dyc_ref, pu_ref, ch_ref, cb_ref, cc_ref, w_ref, ps_ref, cw_ref, dz_ref, dw_ref, dps_ref, dcw_ref):
        u = pu_ref[...]
        count = _pool_count(u.shape)
        p = (_pool_windows(u, _shift_down) / count - u).astype(BF16)
        wb = w_ref[...].astype(BF16)
        dyb_v = dyb_ref[...]
        pw = jnp.dot(p, wb, preferred_element_type=F32)
        dps_ref[...] = jnp.broadcast_to(jnp.sum(dyb_v * pw, axis=0, keepdims=True), (8, 256))
        dys = (dyb_v * ps_ref[...]).astype(BF16)
        dp = lax.dot_general(dys, wb, (((1,), (1,)), ((), ())), preferred_element_type=F32)
        dw_ref[...] = lax.dot_general(p, dys, (((0,), (0,)), ((), ())), preferred_element_type=F32)
        dz_ref[:, 0:256] = (_pool_windows(dp / count, _shift_up) - dp).astype(BF16)

        ch, cb, cc = ch_ref[...], cb_ref[...], cc_ref[...]
        uc = cc * ch
        cw = cw_ref[...]
        u2, u1 = _shift_down(uc, 2), _shift_down(uc, 1)
        conv = cw[0:1, :] * u2 + cw[1:2, :] * u1 + cw[2:3, :] * uc
        dyc_v = dyc_ref[...]
        dconv = dyc_v * cb
        du = cw[0:1, :] * _shift_up(dconv, 2) + cw[1:2, :] * _shift_up(dconv, 1) + cw[2:3, :] * dconv
        dz_ref[:, 256:512] = (du * cc).astype(BF16)
        dz_ref[:, 512:768] = (dyc_v * conv).astype(BF16)
        dz_ref[:, 768:1024] = (du * ch).astype(BF16)
        dcw_ref[...] = jnp.zeros_like(dcw_ref)
        dcw_ref[0:1, :] = jnp.sum(dconv * u2, axis=0, keepdims=True)
        dcw_ref[1:2, :] = jnp.sum(dconv * u1, axis=0, keepdims=True)
        dcw_ref[2:3, :] = jnp.sum(dconv * uc, axis=0, keepdims=True)

    blk = lambda r, c: pl.BlockSpec((r, c), lambda i: (0, 0))
    return pl.pallas_call(
        body, name="poolconv_bwd", grid=(1,),
        in_specs=[blk(s, 256), blk(s, 256)] + _pc_specs(s),
        out_specs=[blk(s, 1024), blk(256, 256), blk(8, 256), blk(8, 256)],
        out_shape=[jax.ShapeDtypeStruct((s, 1024), BF16), jax.ShapeDtypeStruct((256, 256), F32),
                   jax.ShapeDtypeStruct((8, 256), F32), jax.ShapeDtypeStruct((8, 256), F32)],
        compiler_params=_params("arbitrary"),
    )(dyb, dyc, z, z, z, z, wbd, pscale, convw)


_NT = (((1,), (1,)), ((), ()))
_TN = (((0,), (0,)), ((), ()))


ATT_Q, ATT_K = 256, 256
ATT_HEADS_BWD = 8
ATT_HEADS = 8


def _att_logits(q, k, fr, q0, k0, masked):
    logits = lax.dot_general(q, k, _NT, preferred_element_type=F32) - fr
    if not masked:
        return logits
    row = q0 + lax.broadcasted_iota(jnp.int32, logits.shape, 0)
    col = k0 + lax.broadcasted_iota(jnp.int32, logits.shape, 1)
    return jnp.where(row >= col, logits, NEG_INF)


def _causal_sweep(step, qi, init):
    n_full = (qi * ATT_Q) // ATT_K
    carry = lax.fori_loop(0, n_full, lambda j, carry: step(j, carry, False), init)
    return step(n_full, carry, True)


HEAD_PAIRS = HEADS // 2


def _lane_pick(v, lane, idx):
    return jnp.sum(jnp.where(lane == idx, v, 0.0), axis=-1, keepdims=True)


def _lane_put(lane, idx, col):
    return jnp.where(lane == idx, col, 0.0)


def _split_heads(v, low):
    zero = jnp.zeros_like(v)
    return jnp.where(low, v, zero), jnp.where(low, zero, v)


def _attn_fwd(qkv, fr):
    s = qkv.shape[0]
    nk = s // ATT_K
    width = ATT_HEADS * HEAD_DIM
    groups = HEADS // ATT_HEADS

    def body(q_ref, k_ref, v_ref, fr_ref, o_ref, lse_ref):
        qi, grp = pl.program_id(0), pl.program_id(1)
        lane = lax.broadcasted_iota(jnp.int32, (ATT_Q, 128), 1)
        low = lane < HEAD_DIM
        qs = []
        for pr in range(ATT_HEADS // 2):
            qs += _split_heads(q_ref[:, 128 * pr:128 * (pr + 1)] * (HEAD_DIM ** -0.5), low)

        def step(j, carry, masked):
            k0 = pl.multiple_of(j * ATT_K, ATT_K)
            out = []
            for h in range(ATT_HEADS):
                cols = slice(128 * (h // 2), 128 * (h // 2 + 1))
                m, l, acc = carry[h]
                logits = _att_logits(qs[h], k_ref[pl.ds(k0, ATT_K), cols], fr_ref[h, pl.ds(j, 1), :], qi * ATT_Q, k0, masked)
                m_new = jnp.maximum(m, jnp.max(logits, axis=-1, keepdims=True))
                p = jnp.exp(logits - m_new)
                alpha = jnp.exp(m - m_new)
                l = alpha * l + jnp.sum(p, axis=-1, keepdims=True)
                acc = alpha * acc + jnp.dot(p.astype(BF16), v_ref[pl.ds(k0, ATT_K), cols], preferred_element_type=F32)
                out.append((m_new, l, acc))
            return tuple(out)

        one = (jnp.full((ATT_Q, 1), NEG_INF, F32), jnp.zeros((ATT_Q, 1), F32), jnp.zeros((ATT_Q, 128), F32))
        done = _causal_sweep(step, qi, (one,) * ATT_HEADS)

        @pl.when(grp == 0)
        def _():
            lse_ref[...] = jnp.zeros_like(lse_ref)

        lse = jnp.zeros((ATT_Q, 128), F32)
        for pr in range(ATT_HEADS // 2):
            (m0, l0, acc0), (m1, l1, acc1) = done[2 * pr], done[2 * pr + 1]
            o_ref[:, 128 * pr:128 * (pr + 1)] = jnp.where(low, acc0 / l0, acc1 / l1)
            head = ATT_HEADS * grp + 2 * pr
            lse = lse + _lane_put(lane, head, m0 + jnp.log(l0)) + _lane_put(lane, head + 1, m1 + jnp.log(l1))
        lse_ref[...] += lse

    return pl.pallas_call(
        body, name="attn_fwd", grid=(s // ATT_Q, groups),
        in_specs=[pl.BlockSpec((ATT_Q, width), lambda i, g: (i, g)),
                  pl.BlockSpec((s, width), lambda i, g: (0, groups + g)),
                  pl.BlockSpec((s, width), lambda i, g: (0, 2 * groups + g)),
                  pl.BlockSpec((ATT_HEADS, nk, ATT_K), lambda i, g: (g, 0, 0))],
        out_specs=[pl.BlockSpec((ATT_Q, width), lambda i, g: (i, g)), pl.BlockSpec((ATT_Q, 128), lambda i, g: (i, 0))],
        out_shape=[jax.ShapeDtypeStruct((s, A_WIDTH), F32), jax.ShapeDtypeStruct((s, 128), F32)],
        compiler_params=_params("parallel", "arbitrary"),
    )(qkv, qkv, qkv, fr)


def _attn_bwd(qkv, do, o, lse, fr):
    s = qkv.shape[0]
    nk = s // ATT_K
    scale = HEAD_DIM ** -0.5
    heads = ATT_HEADS_BWD
    width = heads * HEAD_DIM
    groups = HEADS // heads

    def body(q_ref, k_ref, v_ref, do_ref, o_ref, lse_ref, fr_ref, dq_ref, dk_ref, dv_ref, dfc_ref, dfr_ref, dk_acc, dv_acc):
        grp = pl.program_id(0)
        lane = lax.broadcasted_iota(jnp.int32, (ATT_Q, 128), 1)
        low = lane < HEAD_DIM
        low_t = lax.broadcasted_iota(jnp.int32, (128, ATT_Q), 0) < HEAD_DIM
        dk_acc[...] = jnp.zeros_like(dk_acc)
        dv_acc[...] = jnp.zeros_like(dv_acc)
        dfr_ref[...] = jnp.zeros_like(dfr_ref)

        @pl.when(grp == 0)
        def _():
            dfc_ref[...] = jnp.zeros_like(dfc_ref)

        def outer(i, carry):
            q0 = pl.multiple_of(i * ATT_Q, ATT_Q)
            rows = pl.ds(q0, ATT_Q)
            lsev = lse_ref[rows, :]
            qts, dots, qs, dos, deltas, lses = [], [], [], [], [], []
            for pr in range(heads // 2):
                pcols = slice(128 * pr, 128 * (pr + 1))
                q2, do2 = q_ref[rows, pcols] * scale, do_ref[rows, pcols]
                prod = do2 * o_ref[rows, pcols]
                deltas += [jnp.sum(jnp.where(low, prod, 0.0), axis=-1, keepdims=True),
                           jnp.sum(jnp.where(low, 0.0, prod), axis=-1, keepdims=True)]
                dob2 = do2.astype(BF16)
                qts += _split_heads(q2.astype(F32).T.astype(BF16), low_t)
                dots += _split_heads(do2.T.astype(BF16), low_t)
                qs += _split_heads(q2, low)
                dos += _split_heads(dob2, low)
                lses += [_lane_pick(lsev, lane, heads * grp + 2 * pr), _lane_pick(lsev, lane, heads * grp + 2 * pr + 1)]

            def inner(j, carry, masked):
                k0 = pl.multiple_of(j * ATT_K, ATT_K)
                krows = pl.ds(k0, ATT_K)
                out, dkt, dvt = [], [], []
                for h in range(heads):
                    pcols = slice(128 * (h // 2), 128 * (h // 2 + 1))
                    dq, dfc = carry[h]
                    k2 = k_ref[krows, pcols]
                    p = jnp.exp(_att_logits(qs[h], k2, fr_ref[h, pl.ds(j, 1), :], q0, k0, masked) - lses[h])
                    dp = lax.dot_general(dos[h], v_ref[krows, pcols], _NT, preferred_element_type=F32)
                    ds = p * (dp - deltas[h])
                    dsb = ds.astype(BF16)
                    dkt.append(jnp.dot(qts[h], dsb, preferred_element_type=F32))
                    dvt.append(jnp.dot(dots[h], p.astype(BF16), preferred_element_type=F32))
                    dfr_ref[h, pl.ds(j, 1), :] -= jnp.sum(ds, axis=0, keepdims=True)
                    out.append((dq + jnp.dot(dsb, k2, preferred_element_type=F32), dfc + (ds[:, :128] + ds[:, 128:])))
                for pr in range(heads // 2):
                    prows = slice(128 * pr, 128 * (pr + 1))
                    dk_acc[j, prows, :] += dkt[2 * pr] + dkt[2 * pr + 1]
                    dv_acc[j, prows, :] += dvt[2 * pr] + dvt[2 * pr + 1]
                return tuple(out)

            one = (jnp.zeros((ATT_Q, 128), F32), jnp.zeros((ATT_Q, 128), F32))
            done = _causal_sweep(inner, i, (one,) * heads)
            dfc = jnp.zeros((ATT_Q, 128), F32)
            for pr in range(heads // 2):
                (dq0, dfc0), (dq1, dfc1) = done[2 * pr], done[2 * pr + 1]
                dq_ref[rows, 128 * pr:128 * (pr + 1)] = (jnp.where(low, dq0, dq1) * scale).astype(BF16)
                head = heads * grp + 2 * pr
                dfc = (dfc + _lane_put(lane, head, jnp.sum(dfc0, axis=-1, keepdims=True))
                       + _lane_put(lane, head + 1, jnp.sum(dfc1, axis=-1, keepdims=True)))
            dfc_ref[rows, :] += dfc
            return carry

        lax.fori_loop(0, s // ATT_Q, outer, 0)
        for j in range(nk):
            for pr in range(heads // 2):
                prows, pcols = slice(128 * pr, 128 * (pr + 1)), slice(128 * pr, 128 * (pr + 1))
                dk_ref[ATT_K * j:ATT_K * (j + 1), pcols] = dk_acc[j, prows, :].T.astype(BF16)
                dv_ref[ATT_K * j:ATT_K * (j + 1), pcols] = dv_acc[j, prows, :].T.astype(BF16)

    part = lambda first: pl.BlockSpec((s, width), lambda g, first=first: (0, first + g))
    whole = pl.BlockSpec((s, 128), lambda g: (0, 0))
    rowv = pl.BlockSpec((heads, nk, ATT_K), lambda g: (g, 0, 0))
    return pl.pallas_call(
        body, name="attn_bwd", grid=(groups,),
        in_specs=[part(0), part(groups), part(2 * groups), part(0), part(0), whole, rowv],
        out_specs=[part(0), part(0), part(0), whole, rowv],
        out_shape=[jax.ShapeDtypeStruct((s, A_WIDTH), BF16)] * 3 + [jax.ShapeDtypeStruct((s, 128), F32), jax.ShapeDtypeStruct((HEADS, nk, ATT_K), F32)],
        scratch_shapes=[pltpu.VMEM((nk, width, ATT_K), F32), pltpu.VMEM((nk, width, ATT_K), F32)],
        compiler_params=_params("arbitrary"),
    )(qkv, qkv, qkv, do, o, lse, fr)


def _ada_fwd(c_all, w_ada, b_loc):
    depth, _, n = w_ada.shape
    tn = 512

    def body(c_ref, w_ref, b_ref, o_ref, sc_ref):
        cv = c_ref[...]
        sc = cv * jax.nn.sigmoid(cv)
        sc_ref[...] = sc
        o_ref[0] = jnp.dot(sc.astype(BF16), w_ref[0].astype(BF16), preferred_element_type=F32) + b_ref[0]

    return pl.pallas_call(
        body, name="ada_fwd", grid=(depth, n // tn),
        in_specs=[pl.BlockSpec((N_DEV, D), lambda l, j: (0, 0)), pl.BlockSpec((1, D, tn), lambda l, j: (l, 0, j)),
                  pl.BlockSpec((1, 1, tn), lambda l, j: (l, 0, j))],
        out_specs=[pl.BlockSpec((1, N_DEV, tn), lambda l, j: (l, 0, j)), pl.BlockSpec((N_DEV, D), lambda l, j: (0, 0))],
        out_shape=[jax.ShapeDtypeStruct((depth, N_DEV, n), F32), jax.ShapeDtypeStruct((N_DEV, D), F32)],
        compiler_params=_params("arbitrary", "arbitrary"),
    )(c_all, w_ada, b_loc)


def _sum_devices(gathered):
    n = gathered.shape[1]
    tn = _pick(n, (1408, 1024, 640, 512, 128))

    def body(g_ref, o_ref):
        acc = g_ref[0:8, :]
        for dev in range(1, N_DEV):
            acc = acc + g_ref[8 * dev:8 * dev + 8, :]
        o_ref[...] = acc

    return pl.pallas_call(
        body, name="sum_devices", grid=(n // tn,),
        in_specs=[pl.BlockSpec((8 * N_DEV, tn), lambda j: (0, j))], out_specs=pl.BlockSpec((8, tn), lambda j: (0, j)),
        out_shape=jax.ShapeDtypeStruct((8, n), F32), compiler_params=_params("parallel"),
    )(gathered)


def _place():
    x, y, c = lax.axis_index("x"), lax.axis_index("y"), lax.axis_index("c")
    chips = [(1 - x, y), (x, 1 - y), (1 - x, 1 - y)]
    return x, y, c, chips


def _allgather8(block, name):
    m_per, n = block.shape

    def body(x_ref, out_ref, send_sems, recv_sems, local_sem):
        x, y, c, chips = _place()
        me, sibling = (x, y, c), (x, y, 1 - c)

        def rows(px, py, pc):
            return out_ref.at[pl.ds((4 * px + 2 * py + pc) * m_per, m_per), :]

        def copy(k, blk, to, src=None):
            return pltpu.make_async_remote_copy(
                src_ref=rows(*blk) if src is None else src, dst_ref=rows(*blk),
                send_sem=send_sems.at[k], recv_sem=recv_sems.at[k], device_id=to, device_id_type=MESH)

        mine = pltpu.make_async_copy(x_ref, rows(*me), local_sem)
        mine.start()
        first = [copy(0, me, sibling, src=x_ref)]
        first += [copy(1 + j, me, (*chip, c), src=x_ref) for j, chip in enumerate(chips)]
        for cp in first:
            cp.start()
        passed = [copy(4 + j, (*chip, c), sibling) for j, chip in enumerate(chips)]
        for j, chip in enumerate(chips):
            copy(1 + j, (*chip, c), me).wait_recv()
            passed[j].start()
        copy(0, sibling, me).wait_recv()
        for j, chip in enumerate(chips):
            copy(4 + j, (*chip, 1 - c), me).wait_recv()
        for cp in first + passed:
            cp.wait_send()
        mine.wait()

    return pl.pallas_call(
        body, name=name, out_shape=jax.ShapeDtypeStruct((N_DEV * m_per, n), block.dtype),
        in_specs=[pl.BlockSpec(memory_space=pltpu.VMEM)], out_specs=pl.BlockSpec(memory_space=pltpu.VMEM),
        scratch_shapes=[pltpu.SemaphoreType.DMA((7,)), pltpu.SemaphoreType.DMA((7,)), pltpu.SemaphoreType.DMA],
        compiler_params=pltpu.CompilerParams(vmem_limit_bytes=V7X_VMEM_LIMIT),
    )(block)


_SEM = pl.BlockSpec(memory_space=pltpu.SEMAPHORE)
_DATAFLOW = pltpu.SideEffectType.DATAFLOW_SIDE_EFFECTING


def _plan_copies(plan, refs, send_sems, recv_sems):
    return [pltpu.make_async_remote_copy(src_ref=src, dst_ref=dst, send_sem=send_sems.at[i], recv_sem=recv_sems.at[i],
                                         device_id=to, device_id_type=MESH) for i, (src, dst, to) in enumerate(plan(refs))]


class _Token(NamedTuple):
    after: jax.Array
    tie: jax.Array


def _after_operand(after):
    return after.after if isinstance(after, _Token) else after


def _copies_start(bufs, plan, n_copies, after, name):
    nb = len(bufs)

    def body(*refs):
        for cp in _plan_copies(plan, refs[:nb], refs[nb + 1], refs[nb + 2]):
            cp.start()
        for token in refs[-2:]:
            token[...] = jnp.zeros_like(token)

    sem = pltpu.SemaphoreType.DMA((n_copies,))
    vmem = pl.BlockSpec(memory_space=pltpu.VMEM)
    outs = pl.pallas_call(
        body, name=name,
        out_shape=(sem, sem, *[pltpu.HBM(b.shape, b.dtype) for b in bufs], jax.ShapeDtypeStruct((8, 128), F32),
                   jax.ShapeDtypeStruct((1, 1), F32)),
        in_specs=[_HBM] * nb + [pl.BlockSpec(memory_space=pl.ANY)],
        out_specs=(_SEM, _SEM, *[_HBM] * nb, vmem, vmem),
        input_output_aliases={i: 2 + i for i in range(nb)},
        compiler_params=pltpu.CompilerParams(has_side_effects=_DATAFLOW),
    )(*[pltpu.with_memory_space_constraint(b, pltpu.HBM) for b in bufs], _after_operand(after))
    return outs[0], outs[1], list(outs[2:2 + nb]), _Token(outs[-2], outs[-1])


def _copies_wait(started, plan, after, name):
    send_sems, recv_sems, bufs, _ = started
    nb = len(bufs)

    def body(*refs):
        for cp in _plan_copies(plan, refs[:nb], refs[nb], refs[nb + 1]):
            cp.wait_send()
            cp.wait_recv()

    return list(pl.pallas_call(
        body, name=name, out_shape=tuple(pltpu.HBM(b.shape, b.dtype) for b in bufs),
        in_specs=[_HBM] * nb + [_SEM, _SEM, pl.BlockSpec(memory_space=pl.ANY)], out_specs=tuple([_HBM] * nb),
        input_output_aliases={i: i for i in range(nb)},
        compiler_params=pltpu.CompilerParams(has_side_effects=_DATAFLOW),
    )(*bufs, send_sems, recv_sems, _after_operand(after)))


def _half_rows(ref, axis, c):
    half = ref.shape[axis] // 2
    return pl.ds(c * half, half)


def _plan_gather_ici(refs):
    n = len(refs) // 2
    x, y, c, chips = _place()
    out = []
    for a in range(n):
        rows = _half_rows(refs[a], 0, c)
        out += [(refs[a].at[rows], refs[n + a].at[2 * x + y, rows], (*chip, c)) for chip in chips]
        out.append((refs[a], refs[n + a].at[2 * x + y], (x, y, 1 - c)))
    return out


def _plan_gather_d2d(refs):
    x, y, c, chips = _place()
    out = []
    for ref in refs:
        rows = _half_rows(ref, 1, c)
        for px, py in chips:
            landed = ref.at[2 * px + py, rows]
            out.append((landed, landed, (x, y, 1 - c)))
    return out


def _plan_rs_sibling(refs):
    n = len(refs) // 2
    x, y, c, _ = _place()
    return [(refs[a].at[pl.ds(0, N_CHIPS), _half_rows(refs[a], 1, 1 - c)], refs[n + a], (x, y, 1 - c)) for a in range(n)]


def _plan_rs_chips(refs):
    n = len(refs) // 2
    x, y, c, chips = _place()
    return [(refs[a].at[2 * px + py], refs[n + a].at[k], (px, py, c)) for a in range(n) for k, (px, py) in enumerate(chips)]


def _plan_rs_share(layer):
    def plan(refs):
        x, y, c, _ = _place()
        return [(ref.at[layer, _half_rows(ref, 1, c)], ref.at[layer, _half_rows(ref, 1, c)], (x, y, 1 - c)) for ref in refs]
    return plan


def _chip_sum(g, other, sel, name):
    _, half, cdim = other.shape
    tr = _pick(half, (512, 256, 128, 64))
    per = half // tr

    def body(sel_ref, g_ref, t_ref, wire_ref, own_ref):
        total = g_ref[0] + t_ref[0]
        wire_ref[0] = total.astype(BF16)

        @pl.when(pl.program_id(1) == sel_ref[1])
        def _():
            own_ref[...] = total

    blk = pl.BlockSpec((1, tr, cdim), lambda i, p, sel_ref: (p, i, 0))
    return pl.pallas_call(
        body, name=name,
        grid_spec=pltpu.PrefetchScalarGridSpec(
            num_scalar_prefetch=1, grid=(per, N_CHIPS),
            in_specs=[pl.BlockSpec((1, tr, cdim), lambda i, p, sel_ref: (p, sel_ref[0] * per + i, 0)), blk],
            out_specs=[blk, pl.BlockSpec((tr, cdim), lambda i, p, sel_ref: (i, 0))]),
        out_shape=[jax.ShapeDtypeStruct(other.shape, BF16), jax.ShapeDtypeStruct((half, cdim), F32)],
        compiler_params=_params("parallel", "arbitrary"),
    )(sel, g, other)


def _final_sum(own, recv, sel, layer, into, name):
    half, cdim = own.shape
    tr = _pick(half, (512, 256, 128, 64))
    per = half // tr

    def body(sel_ref, own_ref, r0_ref, r1_ref, r2_ref, *rest):
        rest[-1][...] = ((own_ref[...] + r0_ref[0].astype(F32)) + r1_ref[0].astype(F32)) + r2_ref[0].astype(F32)

    part = lambda k: pl.BlockSpec((1, tr, cdim), lambda i, sel_ref, k=k: (k, i, 0))
    prior = [] if into is None else [into]
    return pl.pallas_call(
        body, name=name,
        grid_spec=pltpu.PrefetchScalarGridSpec(
            num_scalar_prefetch=1, grid=(per,),
            in_specs=[pl.BlockSpec((tr, cdim), lambda i, sel_ref: (i, 0)), part(0), part(1), part(2)]
            + [pl.BlockSpec(memory_space=pl.ANY)] * len(prior),
            out_specs=pl.BlockSpec((None, tr, cdim), lambda i, sel_ref: (layer, sel_ref[0] * per + i, 0))),
        out_shape=jax.ShapeDtypeStruct((DEPTH, 2 * half, cdim), F32),
        input_output_aliases={5: 0} if prior else {}, compiler_params=_params("parallel"),
    )(sel, own, recv, recv, recv, *prior)


def _row(v):
    return v.reshape(1, -1)


_BR_A, _BR_B, _BR_C = (0, A_WIDTH), (A_WIDTH, POOL_WIDTH), (A_WIDTH + POOL_WIDTH, CONV_WIDTH)


def _tie(v, token):
    return v if token is None else v + token.tie


def _no_hook(point, after, ready=None):
    return None


def _layer_fwd(x, w, mod, hook=_no_hook):
    s = x.shape[0]
    mod3 = mod.reshape(6, 1, D)
    h = _modnorm_fwd(x, _row(w["g_mix_pre"]), (mod3, 0), (mod3, 1), "mix_pre_fwd")
    hook("pre", h)
    z = _mm(h, w["w_all"], name="mm_in")
    qkv = z[:, Z_QKV:Z_PC].astype(BF16)
    fl = z[:, Z_FL:Z_COLS]
    cum = _cumf_fwd(fl, w["b_f_pad"])
    fr = cum[:, :HEADS].T.reshape(HEADS, s // ATT_K, ATT_K)
    br_a, lse = _attn_fwd(qkv, fr)
    br_b, br_c = _poolconv_fwd(z, w["w_pool_bd"], _tie(_row(w["pool_scale"]), hook("attn", lse)), w["conv_w"])
    hook("pool", br_b)
    wbr = w["w_branch"]
    pa = _mm(br_a, wbr, b_rows=_BR_A, name="mm_br_a")
    pb = _mm(br_b, wbr, b_rows=_BR_B, name="mm_br_b")
    pc = _mm(br_c, wbr, b_rows=_BR_C, name="mm_br_c")
    merged = _merge_fwd(z, pa, pb, pc)
    y = _mm(merged, w["w_out"], name="mm_out")
    x1, h2 = _post_pre_fwd(x, y, _row(w["g_mix_post"]), (mod3, 2), _row(w["g_ff_pre"]), (mod3, 3), (mod3, 4), "mix_post_ff_pre_fwd")
    a, r = _mm(h2, w["w_ff1"], b_split=N_CHIPS, epilogue=_relu2_fwd, out_dtype=(F32, BF16), name="mm_ff1")
    y2 = _mm(r, w["w_ff2"], name="mm_ff2")
    x2 = _post_fwd(x1, y2, _tie(_row(w["g_ff_post"]), hook("ff_post", y2)), (mod3, 5), "ff_post_fwd")
    hook("end", x2)
    saved = dict(x=x, h=h, z=z, qkv=qkv, fl=fl, fr=fr, lse=lse, br_a=br_a, br_b=br_b, br_c=br_c, pa=pa, pb=pb, pc=pc,
                 merged=merged, y=y, x1=x1, h2=h2, a=a, r=r, y2=y2)
    return x2, saved


def _layer_bwd(dx2, sv, w, mod, hook=_no_hook):
    s = dx2.shape[0]
    mod3 = mod.reshape(6, 1, D)
    dy2, sum_ff_post = _post_bwd(dx2, sv["y2"], _row(w["g_ff_post"]), (mod3, 5), "ff_post_bwd")
    (da,) = _mm(dy2, w["w_ff2"], tb=True, epilogue=_relu2_bwd, extras=(sv["a"],), out_dtype=(BF16,), name="mm_ff2_dx")
    d_w_ff2 = _mm(sv["r"], dy2, ta=True, name="mm_ff2_dw")
    dh2 = _mm(da, w["w_ff1"], tb=True, b_split=N_CHIPS, name="mm_ff1_dx")
    d_w_ff1 = _mm(sv["h2"], da, ta=True, out_split=N_CHIPS, name="mm_ff1_dw")
    g_ff_pre = _tie(_row(w["g_ff_pre"]), hook("ff_pre", dh2, dict(w_ff1=d_w_ff1, w_ff2=d_w_ff2)))
    dx1, dy, sum_mid = _pre_post_bwd(dh2, sv["x1"], dx2, g_ff_pre, (mod3, 4), sv["y"], _row(w["g_mix_post"]), (mod3, 2), "ff_pre_mix_post_bwd")
    sum_ff_pre, sum_mix_post = sum_mid, sum_mid[3:]
    dmerged = _mm(dy, w["w_out"], tb=True, name="mm_out_dx")
    d_w_out = _mm(sv["merged"], dy, ta=True, name="mm_out_dw")
    dz, dpa, dpb, dpc = _merge_bwd(dmerged, sv["z"], sv["pa"], sv["pb"], sv["pc"])
    wbr = w["w_branch"]
    dbr_a = _mm(dpa, wbr, tb=True, b_rows=_BR_A, name="mm_br_a_dx")
    dbr_b = _mm(dpb, wbr, tb=True, b_rows=_BR_B, name="mm_br_b_dx")
    dbr_c = _mm(dpc, wbr, tb=True, b_rows=_BR_C, name="mm_br_c_dx")
    d_w_branch = jnp.concatenate([_mm(sv["br_a"], dpa, ta=True, name="mm_br_a_dw"), _mm(sv["br_b"], dpb, ta=True, name="mm_br_b_dw"),
                                  _mm(sv["br_c"], dpc, ta=True, name="mm_br_c_dw")], axis=0)

    dq, dk, dv, dfc, dfr = _attn_bwd(sv["qkv"], dbr_a, sv["br_a"], sv["lse"], sv["fr"])
    dcum = dfc + jnp.pad(dfr.reshape(HEADS, s).T, ((0, 0), (0, 128 - HEADS)))
    dfl, sum_bf = _cumf_bwd(dcum, sv["fl"], _tie(w["b_f_pad"], hook("cumf", dfc)))
    dpc_z, d_wbd, sum_ps, sum_cw = _poolconv_bwd(dbr_b, dbr_c, sv["z"], w["w_pool_bd"], _row(w["pool_scale"]), w["conv_w"])
    for at, part in ((Z_QKV, dq), (Z_QKV + A_WIDTH, dk), (Z_QKV + 2 * A_WIDTH, dv), (Z_PC, dpc_z), (Z_FL, dfl)):
        dz = lax.dynamic_update_slice(dz, part, (0, at))
    dh = _mm(dz, w["w_all"], tb=True, name="mm_in_dx")
    d_w_all = _mm(sv["h"], dz, ta=True, name="mm_in_dw")
    hook("mix_pre", dh)
    dx, sum_mix_pre = _modnorm_bwd(dh, sv["x"], dx1, _row(w["g_mix_pre"]), (mod3, 1), "mix_pre_bwd")

    dmod = jnp.stack([sum_mix_pre[0], sum_mix_pre[1], sum_mix_post[0], sum_ff_pre[0], sum_ff_pre[1], sum_ff_post[0]])
    d_w_in = _w_in_shards(d_w_all)
    d_w_pool = jnp.stack([d_wbd[64 * g:64 * g + 64, 64 * g:64 * g + 64] for g in range(4)])
    big = dict(w_in=d_w_in, w_branch=d_w_branch, w_out=d_w_out, w_ff1=d_w_ff1, w_ff2=d_w_ff2)
    small = dict(g_mix_pre=sum_mix_pre[2], g_mix_post=sum_mix_post[1], g_ff_pre=sum_ff_pre[2], g_ff_post=sum_ff_post[1],
                 b_f=sum_bf[0, :HEADS], w_pool=d_w_pool, pool_scale=sum_ps[0], conv_w=sum_cw[0:3])
    return dx, dmod, big, small


_QKV_END, _FL_END, _PC_END = 3 * A_WIDTH, 3 * A_WIDTH + HEADS, 3 * A_WIDTH + HEADS + POOL_WIDTH + 3 * CONV_WIDTH
_W_IN_GROUPS = ((_PC_END, IN_COLS, Z_GL), (0, _QKV_END, Z_QKV), (_FL_END, _PC_END, Z_PC), (_QKV_END, _FL_END, Z_FL))
_SHARD_COLS = IN_COLS // N_CHIPS


def _w_all_from_shards(blocks):
    pieces = []
    for lo, hi, _ in _W_IN_GROUPS:
        for p in range(N_CHIPS):
            a, b = max(lo, p * _SHARD_COLS), min(hi, (p + 1) * _SHARD_COLS)
            if a < b:
                pieces.append(blocks[p][:, a - p * _SHARD_COLS:b - p * _SHARD_COLS])
    pieces.append(jnp.zeros((D, Z_COLS - IN_COLS), blocks.dtype))
    return jnp.concatenate(pieces, axis=1)


def _w_in_shards(d_w_all):
    blocks = []
    for p in range(N_CHIPS):
        pieces = []
        for lo, hi, at in sorted(_W_IN_GROUPS):
            a, b = max(lo, p * _SHARD_COLS), min(hi, (p + 1) * _SHARD_COLS)
            if a < b:
                pieces.append(d_w_all[:, at + a - lo:at + b - lo])
        blocks.append(jnp.concatenate(pieces, axis=1))
    return jnp.stack(blocks)


def _full_layer_weights(w_in_blocks, w_branch, w_out, w_ff1, w_ff2, g_mix_pre, g_mix_post, g_ff_pre, g_ff_post, b_f, w_pool, pool_scale, conv_w):
    w_all = None if w_in_blocks is None else _w_all_from_shards(w_in_blocks)
    wbd = (w_pool[:, :, None, :] * jnp.eye(4, dtype=F32)[:, None, :, None]).reshape(POOL_WIDTH, POOL_WIDTH)
    return dict(w_all=w_all, w_branch=w_branch, w_out=w_out, w_ff1=w_ff1, w_ff2=w_ff2, g_mix_pre=g_mix_pre, g_mix_post=g_mix_post,
                g_ff_pre=g_ff_pre, g_ff_post=g_ff_post, b_f_pad=jnp.pad(b_f, (0, 128 - HEADS)).reshape(1, 128), w_pool_bd=wbd,
                pool_scale=pool_scale, conv_w=conv_w)


class _NoComm:
    def layer_weights(self, l):
        raise NotImplementedError

    def fwd_hook(self, l):
        return _no_hook

    def bwd_hook(self, l):
        return _no_hook

    def grads_ready(self, l, big):
        return None


class _Layers(_NoComm):
    def __init__(self, layers):
        self.layers = layers

    def layer_weights(self, l):
        return self.layers[l]


def _local_step(x, target, mods, comm):
    saved, weights = [], []
    act = x
    for l in range(DEPTH):
        weights.append(comm.layer_weights(l))
        act, sv = _layer_fwd(act, weights[l], mods[l], comm.fwd_hook(l))
        saved.append(sv)
    dact, sq = _loss_head(act, target)
    loss = sq[0, 0] * (0.5 / D)
    dmods, bigs, smalls = [None] * DEPTH, [None] * DEPTH, [None] * DEPTH
    token = None
    for l in reversed(range(DEPTH)):
        dact, dmods[l], bigs[l], smalls[l] = _layer_bwd(dact, saved[l], weights[l], _tie(mods[l], token), comm.bwd_hook(l))
        token = comm.grads_ready(l, bigs[l])
    return loss, dact, jnp.stack(dmods), bigs, smalls


_BIG = ("w_in", "w_branch", "w_out", "w_ff1", "w_ff2")


class _GatherJob:
    def __init__(self, tag, shards, after):
        self.tag, self.n = tag, len(shards)
        lands = [lax.empty((N_CHIPS,) + s.shape, s.dtype) for s in shards]
        self.state = _copies_start(list(shards) + lands, _plan_gather_ici, 4 * self.n, after, "gather_ici_start_" + tag)
        self.token = self.state[3]

    def pass_on(self, after):
        bufs = _copies_wait(self.state, _plan_gather_ici, after, "gather_ici_wait_" + self.tag)
        self.state = _copies_start(bufs[self.n:], _plan_gather_d2d, 3 * self.n, bufs[0], "gather_d2d_start_" + self.tag)
        self.token = self.state[3]
        return self.token

    def done(self, after):
        return _copies_wait(self.state, _plan_gather_d2d, after, "gather_d2d_wait_" + self.tag)


class _ReduceJob:
    def __init__(self, tag, names, grads, sel, after, layer, into=None):
        self.tag, self.names, self.n, self.sel, self.layer, self.into = tag, names, len(names), sel, layer, into or {}
        lands = [lax.empty((N_CHIPS, g.shape[1] // 2, g.shape[2]), F32) for g in grads]
        self.state = _copies_start(list(grads) + lands, _plan_rs_sibling, self.n, after, "rs_sibling_start_" + tag)
        self.token = self.state[3]

    def chip_sums(self, after):
        bufs = _copies_wait(self.state, _plan_rs_sibling, after, "rs_sibling_wait_" + self.tag)
        wires, self.owns = zip(*[_chip_sum(bufs[i], bufs[self.n + i], self.sel, "rs_chip_sum_" + name) for i, name in enumerate(self.names)])
        lands = [lax.empty((3,) + w.shape[1:], BF16) for w in wires]
        self.state = _copies_start(list(wires) + lands, _plan_rs_chips, 3 * self.n, self.owns[0], "rs_chips_start_" + self.tag)
        self.token = self.state[3]
        return self.token

    def final_sums(self, after):
        bufs = _copies_wait(self.state, _plan_rs_chips, after, "rs_chips_wait_" + self.tag)
        sums = [_final_sum(self.owns[i], bufs[self.n + i], self.sel, self.layer, self.into.get(name), "rs_final_" + name)
                for i, name in enumerate(self.names)]
        self.state = _copies_start(sums, _plan_rs_share(self.layer), self.n, sums[0], "rs_share_start_" + self.tag)
        self.token = self.state[3]
        return self.token

    def done(self, after):
        return dict(zip(self.names, _copies_wait(self.state, _plan_rs_share(self.layer), after, "rs_share_wait_" + self.tag)))


def _chip_blocks(g):
    return g if g.ndim == 3 else g.reshape(N_CHIPS, -1, g.shape[1])


class _StepComm(_NoComm):
    def __init__(self, big_weights, sel, after):
        self.sel = sel
        self.small, self.grads, self.jobs = None, {}, {}
        self.jobs["in0"] = _GatherJob("in0", [big_weights[0][0].astype(BF16)], after)
        later = lax.optimization_barrier((tuple(big_weights), self.jobs["in0"].token.after))[0]
        self.jobs["rest0"] = _GatherJob("rest0", [w[0].astype(BF16) for w in later[1:]], self.jobs["in0"].token)
        self.jobs["all1"] = _GatherJob("all1", [w[1].astype(BF16) for w in later], self.jobs["rest0"].token)

    def layer_weights(self, l):
        if l == 0:
            self.weights0 = _full_layer_weights(None, None, None, None, None, *self.small[0])
            return self.weights0
        g_in, g_br, g_out, g_f1, g_f2 = self.landed1
        return _full_layer_weights(g_in, g_br.reshape(D, D), g_out.reshape(D, D), g_f1, g_f2.reshape(D_FF, D), *self.small[1])

    def fwd_hook(self, l):
        if l != 0:
            return _no_hook

        def hook(point, after, ready=None):
            if point == "pre":
                job = self.jobs["in0"]
                self.weights0["w_all"] = _w_all_from_shards(job.done(job.pass_on(after))[0])
            if point == "attn":
                return self.jobs["rest0"].pass_on(after)
            if point == "ff_post":
                return self.jobs["all1"].pass_on(after)
            if point == "pool":
                g_br, g_out, g_f1, g_f2 = self.jobs["rest0"].done(after)
                self.weights0.update(w_branch=g_br.reshape(D, D), w_out=g_out.reshape(D, D), w_ff1=g_f1, w_ff2=g_f2.reshape(D_FF, D))
            if point == "end":
                self.landed1 = self.jobs["all1"].done(after)
            return None
        return hook

    def bwd_hook(self, l):
        if l != 0:
            return _no_hook

        def hook(point, after, ready=None):
            jobs = self.jobs
            if point == "ff_pre":
                token = jobs["rs1"].chip_sums(after)
                jobs["rs0_ff"] = _ReduceJob("0_ff", ("w_ff1", "w_ff2"), [_chip_blocks(ready[n]) for n in ("w_ff1", "w_ff2")], self.sel, token, 0)
                return jobs["rs0_ff"].token
            if point == "cumf":
                return jobs["rs0_ff"].chip_sums(jobs["rs1"].final_sums(after))
            self.layer1 = jobs["rs1"].done(after)
            jobs["rs0_ff"].into = self.layer1
            return None
        return hook

    def grads_ready(self, l, big):
        if l == 1:
            self.jobs["rs1"] = _ReduceJob("1", _BIG, [_chip_blocks(big[n]) for n in _BIG], self.sel, self.sel, 1)
            return self.jobs["rs1"].token
        names = ("w_in", "w_branch", "w_out")
        self.jobs["rs0_mix"] = _ReduceJob("0_mix", names, [_chip_blocks(big[n]) for n in names], self.sel, self.sel, 0, self.layer1)
        return self.jobs["rs0_mix"].token

    def finish_sums(self, after):
        jobs = self.jobs
        token = jobs["rs0_mix"].chip_sums(after)
        return jobs["rs0_ff"].final_sums(token)

    def finish_ff(self, after):
        self.grads.update(self.jobs["rs0_ff"].done(after))

    def finish_mix(self, after):
        job = self.jobs["rs0_mix"]
        self.grads.update(job.done(job.final_sums(after)))


_SMALL = ("g_mix_pre", "g_mix_post", "g_ff_pre", "g_ff_post", "b_f", "w_pool", "pool_scale", "conv_w")


def _w_in_view(t):
    return t.reshape(DEPTH, D // 128, 128, _SHARD_COLS).transpose(3, 1, 0, 2).reshape(_SHARD_COLS * (D // 128) * DEPTH, 128)


def _w_in_unview(t):
    return t.reshape(_SHARD_COLS, D // 128, DEPTH, 128).transpose(2, 1, 3, 0).reshape(DEPTH, D, _SHARD_COLS)


def _pack(parts, rows=8):
    flat = jnp.concatenate([p.reshape(-1) for p in parts])
    width = -(-flat.shape[0] // (rows * 128)) * 128
    return jnp.pad(flat, (0, rows * width - flat.shape[0])).reshape(rows, width)


def _unpack(packed, like):
    flat = packed.reshape(-1)
    out, at = [], 0
    for ref in like:
        out.append(flat[at:at + ref.size].reshape(ref.shape))
        at += ref.size
    return out


def kernel(x, c, w_ada, b_ada, g_mix_pre, g_mix_post, g_ff_pre, g_ff_post, w_in, b_f, w_pool, pool_scale, conv_w, w_branch, w_out, w_ff1, w_ff2, loss_target, m_w_ada, m_b_ada, m_g_mix_pre, m_g_mix_post, m_g_ff_pre, m_g_ff_post, m_w_in, m_b_f, m_w_pool, m_pool_scale, m_conv_w, m_w_branch, m_w_out, m_w_ff1, m_w_ff2, v_w_ada, v_b_ada, v_g_mix_pre, v_g_mix_post, v_g_ff_pre, v_g_ff_post, v_w_in, v_b_f, v_w_pool, v_pool_scale, v_conv_w, v_w_branch, v_w_out, v_w_ff1, v_w_ff2):
    xi, yi, ci = lax.axis_index("x"), lax.axis_index("y"), lax.axis_index("c")
    chip = 2 * xi + yi
    dev = 2 * chip + ci
    n_ada = w_ada.shape[2]

    first = jnp.zeros((8, D + 384), F32).at[0, :D].set(c[0]).at[0, D:].set(conv_w.reshape(-1))
    got = _allgather8(first, "gather_cond").reshape(N_DEV, 8, D + 384)[:, 0]
    c_all = got[:, :D]
    conv_full = got[0::2, D:].reshape(N_CHIPS, DEPTH, 3, CONV_WIDTH // N_CHIPS).transpose(1, 2, 0, 3).reshape(DEPTH, 3, CONV_WIDTH)

    b_loc = lax.dynamic_slice_in_dim(b_ada, chip * n_ada, n_ada, axis=1).reshape(DEPTH, 1, n_ada)
    mod_cols, silu_c = _ada_fwd(c_all, w_ada, b_loc)
    got = _allgather8(mod_cols.reshape(DEPTH * N_DEV, n_ada), "gather_mod").reshape(N_DEV, DEPTH, N_DEV, n_ada)[0::2]
    mod_all = got.transpose(1, 2, 0, 3).reshape(DEPTH, N_DEV, 6, D)
    mods = lax.dynamic_index_in_dim(mod_all, dev, axis=1, keepdims=False)

    comm = _StepComm((w_in, w_branch, w_out, w_ff1, w_ff2), jnp.stack([ci, chip]).astype(jnp.int32), mods)
    comm.small = [(g_mix_pre[l], g_mix_post[l], g_ff_pre[l], g_ff_post[l], b_f[l], w_pool[l], pool_scale[l], conv_full[l]) for l in range(DEPTH)]
    loss_part, grad_x, dmods, bigs, smalls = _local_step(x[0], loss_target[0], mods, comm)

    small_parts = [smalls[l][name] for name in _SMALL for l in range(DEPTH)] + [loss_part.reshape(1)]
    packed = _tie(_pack([dmods] + small_parts), comm.jobs["rs0_mix"].token)
    gathered = _allgather8(packed, "gather_small")
    dmod_all = gathered.reshape(N_DEV, -1)[:, :dmods.size].reshape(N_DEV, DEPTH, 6 * D)
    summed = _unpack(_sum_devices(gathered), [dmods] + small_parts)
    grad_b_ada = summed[0].reshape(DEPTH, 6 * D)
    loss = summed[-1][0]
    small_grads = {name: jnp.stack(summed[1 + 2 * i:3 + 2 * i]) for i, name in enumerate(_SMALL)}
    small_grads["conv_w"] = lax.dynamic_slice_in_dim(small_grads["conv_w"], chip * (CONV_WIDTH // N_CHIPS), CONV_WIDTH // N_CHIPS, axis=2)

    dmod_loc = lax.dynamic_slice_in_dim(dmod_all.transpose(1, 0, 2), chip * n_ada, n_ada, axis=2)
    tail_token = comm.finish_sums(grad_b_ada)
    silu_pad = _tie(jnp.pad(silu_c, ((0, 128 - N_DEV), (0, 0))), tail_token)
    dmod_pad = jnp.pad(dmod_loc.transpose(1, 0, 2).reshape(N_DEV, DEPTH * n_ada), ((0, 128 - N_DEV), (0, 0)))
    grad_w_ada = _mm(silu_pad, dmod_pad, ta=True, out_split=DEPTH, name="mm_ada_dw")

    grads = dict(w_ada=grad_w_ada, b_ada=grad_b_ada, **small_grads)
    weights = dict(w_ada=w_ada, b_ada=b_ada, g_mix_pre=g_mix_pre, g_mix_post=g_mix_post, g_ff_pre=g_ff_pre, g_ff_post=g_ff_post, w_in=w_in,
                   b_f=b_f, w_pool=w_pool, pool_scale=pool_scale, conv_w=conv_w, w_branch=w_branch, w_out=w_out, w_ff1=w_ff1, w_ff2=w_ff2)
    m_in = dict(w_ada=m_w_ada, b_ada=m_b_ada, g_mix_pre=m_g_mix_pre, g_mix_post=m_g_mix_post, g_ff_pre=m_g_ff_pre, g_ff_post=m_g_ff_post,
                w_in=m_w_in, b_f=m_b_f, w_pool=m_w_pool, pool_scale=m_pool_scale, conv_w=m_conv_w, w_branch=m_w_branch, w_out=m_w_out,
                w_ff1=m_w_ff1, w_ff2=m_w_ff2)
    v_in = dict(w_ada=v_w_ada, b_ada=v_b_ada, g_mix_pre=v_g_mix_pre, g_mix_post=v_g_mix_post, g_ff_pre=v_g_ff_pre, g_ff_post=v_g_ff_post,
                w_in=v_w_in, b_f=v_b_f, w_pool=v_w_pool, pool_scale=v_pool_scale, conv_w=v_conv_w, w_branch=v_w_branch, w_out=v_w_out,
                w_ff1=v_w_ff1, w_ff2=v_w_ff2)
    order = ("w_ada", "b_ada", "g_mix_pre", "g_mix_post", "g_ff_pre", "g_ff_post", "w_in", "b_f", "w_pool", "pool_scale", "conv_w",
             "w_branch", "w_out", "w_ff1", "w_ff2")
    delta, new_m, new_v = {}, {}, {}
    tiny = ("b_ada",) + _SMALL
    tiny_g = [_tie(grads[tiny[0]], tail_token)] + [grads[name] for name in tiny[1:]]
    res = _adamw_many([weights[name] for name in tiny], tiny_g, [m_in[name] for name in tiny], [v_in[name] for name in tiny], "adamw_small")
    for out, vals in zip((delta, new_m, new_v), res):
        out.update(zip(tiny, vals))
    delta["w_ada"], new_m["w_ada"], new_v["w_ada"] = _adamw(w_ada, grad_w_ada, m_w_ada, v_w_ada, "adamw_w_ada")
    comm.finish_ff(delta["w_ada"][0, :8, :128] + delta["b_ada"][0, :128])
    for name in ("w_ff1", "w_ff2", "w_in", "w_branch", "w_out"):
        if name == "w_in":
            comm.finish_mix(delta["w_ff2"][0, :8, :128])
        grads[name] = comm.grads[name]
        if name == "w_in":
            g_view = lax.optimization_barrier(_w_in_view(grads[name]))
            res = _adamw(_w_in_view(w_in), g_view, _w_in_view(m_w_in), _w_in_view(v_w_in), "adamw_w_in")
            grads[name], delta[name], new_m[name], new_v[name] = [_w_in_unview(t) for t in (g_view, *res)]
        else:
            delta[name], new_m[name], new_v[name] = _adamw(weights[name], grads[name], m_in[name], v_in[name], "adamw_" + name)

    return (loss, grad_x[None], *[grads[n] for n in order], *[delta[n] for n in order], *[new_m[n] for n in order],
            *[new_v[n] for n in order])
```

```python
from typing import NamedTuple

import jax
import jax.numpy as jnp
from jax import lax
from jax.experimental import pallas as pl
from jax.experimental.pallas import tpu as pltpu

F32 = jnp.float32
BF16 = jnp.bfloat16
MESH = pl.DeviceIdType.MESH

D = 1024
DEPTH = 2
HEADS = 8
HEAD_DIM = 64
A_WIDTH = 512
POOL_WIDTH = 256
CONV_WIDTH = 256
D_FF = 4096
IN_COLS = 5640
Z_GL, Z_QKV, Z_PC, Z_FL, Z_COLS = 0, 3072, 4608, 5632, 5760
RMS_EPS = 1e-6
NEG_INF = -1e30
ROW_TILE = 512
EW_ROWS = 256
N_CHIPS = 4
N_DEV = 8
V7X_VMEM_LIMIT = 48 * 1024 * 1024

ADAM_LR = 0.001
ADAM_B1 = 0.9
ADAM_B2 = 0.999
ADAM_EPS = 1e-08
ADAM_WD = 0.01
ADAM_STEP = 10

_HBM = pl.BlockSpec(memory_space=pltpu.HBM)


def _params(*sem):
    return pltpu.CompilerParams(dimension_semantics=sem, vmem_limit_bytes=V7X_VMEM_LIMIT)


def _pick(dim, cands):
    for cand in cands:
        if dim % cand == 0:
            return cand
    return dim


MM_TILE_BUDGET = 39 * 1024 * 1024


def _mm_tiles(m, n, k, k_unit, tn, a_size, b_size, out_size):
    for tk in (k_unit, 2048, 1152, 1024, 640, 512, 256, 128):
        if k_unit % tk:
            continue
        for tm in (2048, 1024, 512, 256, 128):
            if m % tm or ((m // tm) * (n // tn) < 2 and tm > 512):
                continue
            need = 2 * (tm * tk * a_size + tk * tn * b_size + tm * tn * out_size) + (0 if tk == k else 4 * tm * tn)
            if need <= MM_TILE_BUDGET and (tk == k_unit or tm >= 512):
                return tm, tk
    return 128, 128


def _mm(a, b, *, ta=False, tb=False, b_rows=None, b_split=1, out_split=1, out_dtype=F32, epilogue=None, extras=(), name):
    (k, m) = a.shape if ta else a.shape[::-1]
    b_row0, b_rows = (0, b.shape[-2]) if b_rows is None else b_rows
    b_cols = b.shape[-1] * b_split
    (n, k2) = (b_rows, b_cols) if tb else (b_cols, b_rows)
    assert k == k2, (a.shape, b.shape, ta, tb)
    n_unit = n // (out_split * (1 if tb else b_split))
    k_unit = k // (b_split if tb else 1)
    tn = _pick(n_unit, (1024, 1152, 768, 640, 512, 256, 128))
    tm, tk = _mm_tiles(m, n, k, k_unit, tn, a.dtype.itemsize, b.dtype.itemsize,
                       sum(jnp.dtype(dt).itemsize for dt in out_dtype) + 4 * len(extras) if epilogue else jnp.dtype(out_dtype).itemsize)
    nk = k // tk
    dims = (((0 if ta else 1,), (1 if tb else 0,)), ((), ()))

    def dot(a_ref, b_ref):
        b_val = b_ref[0] if b_split > 1 else b_ref[...]
        return lax.dot_general(a_ref[...].astype(BF16), b_val.astype(BF16), dims, preferred_element_type=F32)

    n_extra = len(extras)
    assert epilogue is None or out_split == 1

    def put(refs, val):
        if epilogue is not None:
            for o_ref, res in zip(refs[n_extra:], epilogue(val, *[r[...] for r in refs[:n_extra]])):
                o_ref[...] = res.astype(o_ref.dtype)
        elif out_split > 1:
            refs[0][0] = val.astype(refs[0].dtype)
        else:
            refs[0][...] = val.astype(refs[0].dtype)

    def body_single(a_ref, b_ref, *refs):
        put(refs, dot(a_ref, b_ref))

    def body_acc(a_ref, b_ref, *refs):
        kk = pl.program_id(2)
        acc_ref = refs[-1]

        @pl.when(kk == 0)
        def _():
            acc_ref[...] = jnp.zeros_like(acc_ref)

        acc_ref[...] += dot(a_ref, b_ref)

        @pl.when(kk == nk - 1)
        def _():
            put(refs[:-1], acc_ref[...])

    a_spec = pl.BlockSpec((tk, tm), lambda i, j, kk: (kk, i)) if ta else pl.BlockSpec((tm, tk), lambda i, j, kk: (i, kk))
    if b_split == 1:
        off = b_row0 // (tn if tb else tk)
        assert off * (tn if tb else tk) == b_row0
        b_spec = pl.BlockSpec((tn, tk), lambda i, j, kk: (j + off, kk)) if tb else pl.BlockSpec((tk, tn), lambda i, j, kk: (kk + off, j))
    elif tb:
        per = k_unit // tk
        b_spec = pl.BlockSpec((1, tn, tk), lambda i, j, kk: (kk // per, j, kk % per))
    else:
        per = n // b_split // tn
        b_spec = pl.BlockSpec((1, tk, tn), lambda i, j, kk: (j // per, kk, j % per))
    if out_split == 1:
        o_spec = pl.BlockSpec((tm, tn), lambda i, j, kk: (i, j))
        o_shape = None if epilogue is not None else jax.ShapeDtypeStruct((m, n), out_dtype)
    else:
        per_o = n // out_split // tn
        o_spec = pl.BlockSpec((1, tm, tn), lambda i, j, kk: (j // per_o, i, j % per_o))
        o_shape = jax.ShapeDtypeStruct((out_split, m, n // out_split), out_dtype)
    if epilogue is not None:
        o_shape = [jax.ShapeDtypeStruct((m, n), dt) for dt in out_dtype]
        o_spec = [o_spec] * len(out_dtype)
    return pl.pallas_call(
        body_single if nk == 1 else body_acc, name=name, grid=(m // tm, n // tn, nk),
        in_specs=[a_spec, b_spec] + [pl.BlockSpec((tm, tn), lambda i, j, kk: (i, j))] * n_extra, out_specs=o_spec, out_shape=o_shape,
        scratch_shapes=[] if nk == 1 else [pltpu.VMEM((tm, tn), F32)],
        compiler_params=_params("parallel", "parallel", "arbitrary"),
    )(a, b, *extras)


def _ew(fn, ins, out_dtypes, name, tc=None):
    shape = ins[0].shape
    lead, (rows, cols) = shape[:-2], shape[-2:]
    tc = cols if tc is None else tc
    if tc > 1024:
        tr = _pick(rows, (EW_ROWS, 128, 8))
    elif tc > 128:
        tr = _pick(rows, (2 * EW_ROWS, EW_ROWS, 128, 8))
    else:
        tr = _pick(rows, (4096, 2256, 2048, 1024, EW_ROWS, 8))
    n_in = len(ins)

    def body(*refs):
        res = fn(*[r[...] for r in refs[:n_in]])
        for o_ref, val in zip(refs[n_in:], res):
            o_ref[...] = val.astype(o_ref.dtype)

    if lead:
        spec = pl.BlockSpec((None, tr, tc), lambda l, i, j: (l, i, j))
    else:
        spec = pl.BlockSpec((tr, tc), lambda i, j: (i, j))
    return pl.pallas_call(
        body, name=name, grid=lead + (rows // tr, cols // tc),
        in_specs=[spec] * n_in, out_specs=[spec] * len(out_dtypes),
        out_shape=[jax.ShapeDtypeStruct(shape, dt) for dt in out_dtypes],
        compiler_params=_params(*(["parallel"] * (len(lead) + 2))),
    )(*ins)


def _relu2_fwd(a):
    r = jnp.maximum(a, 0.0)
    return a, r * r


def _relu2_bwd(dr, a):
    return (dr * (2.0 * jnp.maximum(a, 0.0)),)


def _adamw_math(w, g, m, v):
    m = ADAM_B1 * m + (1.0 - ADAM_B1) * g
    v = ADAM_B2 * v + (1.0 - ADAM_B2) * (g * g)
    m_hat = m / (1.0 - ADAM_B1 ** ADAM_STEP)
    v_hat = v / (1.0 - ADAM_B2 ** ADAM_STEP)
    delta = -ADAM_LR * (m_hat / (jnp.sqrt(v_hat) + ADAM_EPS) + ADAM_WD * w)
    return delta, m, v


def _adamw(w, g, m, v, name):
    return _ew(_adamw_math, [w, g, m, v], [F32, F32, F32], name)


def _adamw_many(ws, gs, ms, vs, name):
    n = len(ws)

    def body(*refs):
        for i in range(n):
            res = _adamw_math(*[refs[k * n + i][...] for k in range(4)])
            for k in range(3):
                refs[(4 + k) * n + i][...] = res[k]

    outs = pl.pallas_call(
        body, name=name, out_shape=[jax.ShapeDtypeStruct(w.shape, F32) for w in ws] * 3,
        compiler_params=pltpu.CompilerParams(vmem_limit_bytes=V7X_VMEM_LIMIT),
    )(*ws, *gs, *ms, *vs)
    return outs[:n], outs[n:2 * n], outs[2 * n:]


def _row_spec(cols, block=0):
    return pl.BlockSpec((ROW_TILE, cols), lambda i, block=block: (i, block))


def _vec_spec(cols):
    return pl.BlockSpec((1, cols), lambda i: (0, 0))


def _vec_args(*vecs):
    arrays = [v[0] if isinstance(v, tuple) else v for v in vecs]
    specs = [pl.BlockSpec((None, 1, D), lambda i, row=v[1]: (row, 0, 0)) if isinstance(v, tuple) else _vec_spec(D) for v in vecs]
    return arrays, specs


def _sum_spec(cols):
    return pl.BlockSpec((8, cols), lambda i: (0, 0))


def _rstd(x):
    return lax.rsqrt(jnp.mean(x * x, axis=-1, keepdims=True) + RMS_EPS)


def _modnorm_fwd(x, g, shift, scale, name):
    s = x.shape[0]

    def body(x_ref, g_ref, sh_ref, sc_ref, h_ref):
        xv = x_ref[...]
        n = xv * _rstd(xv)
        h_ref[...] = ((n * g_ref[...]) * (1.0 + sc_ref[...]) + sh_ref[...]).astype(BF16)

    vecs, vec_specs = _vec_args(g, shift, scale)
    return pl.pallas_call(
        body, name=name, grid=(s // ROW_TILE,),
        in_specs=[_row_spec(D)] + vec_specs, out_specs=_row_spec(D),
        out_shape=jax.ShapeDtypeStruct((s, D), BF16), compiler_params=_params("parallel"),
    )(x, *vecs)


def _post_fwd(x, y, g, gate, name):
    s = x.shape[0]

    def body(x_ref, y_ref, g_ref, gate_ref, o_ref):
        yv = y_ref[...]
        o_ref[...] = x_ref[...] + gate_ref[...] * ((yv * _rstd(yv)) * g_ref[...])

    vecs, vec_specs = _vec_args(g, gate)
    return pl.pallas_call(
        body, name=name, grid=(s // ROW_TILE,),
        in_specs=[_row_spec(D), _row_spec(D)] + vec_specs, out_specs=_row_spec(D),
        out_shape=jax.ShapeDtypeStruct((s, D), F32), compiler_params=_params("parallel"),
    )(x, y, *vecs)


def _post_bwd(dxo, y, g, gate, name):
    s = dxo.shape[0]

    def body(d_ref, y_ref, g_ref, gate_ref, dy_ref, sum_ref):
        @pl.when(pl.program_id(0) == 0)
        def _():
            sum_ref[...] = jnp.zeros_like(sum_ref)

        dv, yv = d_ref[...], y_ref[...]
        r = _rstd(yv)
        n = yv * r
        sum_ref[0:1, :] += jnp.sum(dv * (n * g_ref[...]), axis=0, keepdims=True)
        sum_ref[1:2, :] += jnp.sum((dv * gate_ref[...]) * n, axis=0, keepdims=True)
        dn = (dv * gate_ref[...]) * g_ref[...]
        dy_ref[...] = (r * (dn - n * jnp.mean(dn * n, axis=-1, keepdims=True))).astype(BF16)

    vecs, vec_specs = _vec_args(g, gate)
    return pl.pallas_call(
        body, name=name, grid=(s // ROW_TILE,),
        in_specs=[_row_spec(D), _row_spec(D)] + vec_specs,
        out_specs=[_row_spec(D), _sum_spec(D)],
        out_shape=[jax.ShapeDtypeStruct((s, D), BF16), jax.ShapeDtypeStruct((8, D), F32)],
        compiler_params=_params("arbitrary"),
    )(dxo, y, *vecs)


def _modnorm_bwd(dh, x, dxo, g, scale, name):
    s = dh.shape[0]

    def body(dh_ref, x_ref, d_ref, g_ref, sc_ref, dx_ref, sum_ref):
        @pl.when(pl.program_id(0) == 0)
        def _():
            sum_ref[...] = jnp.zeros_like(sum_ref)

        dhv, xv = dh_ref[...], x_ref[...]
        r = _rstd(xv)
        n = xv * r
        one_sc = 1.0 + sc_ref[...]
        sum_ref[0:1, :] += jnp.sum(dhv, axis=0, keepdims=True)
        sum_ref[1:2, :] += jnp.sum(dhv * (n * g_ref[...]), axis=0, keepdims=True)
        sum_ref[2:3, :] += jnp.sum((dhv * one_sc) * n, axis=0, keepdims=True)
        dn = (dhv * one_sc) * g_ref[...]
        dx_ref[...] = d_ref[...] + r * (dn - n * jnp.mean(dn * n, axis=-1, keepdims=True))

    vecs, vec_specs = _vec_args(g, scale)
    return pl.pallas_call(
        body, name=name, grid=(s // ROW_TILE,),
        in_specs=[_row_spec(D), _row_spec(D), _row_spec(D)] + vec_specs,
        out_specs=[_row_spec(D), _sum_spec(D)],
        out_shape=[jax.ShapeDtypeStruct((s, D), F32), jax.ShapeDtypeStruct((8, D), F32)],
        compiler_params=_params("arbitrary"),
    )(dh, x, dxo, *vecs)


def _post_pre_fwd(x, y, g_post, gate, g_pre, shift, scale, name):
    s = x.shape[0]

    def body(x_ref, y_ref, gp_ref, gate_ref, g_ref, sh_ref, sc_ref, o_ref, h_ref):
        yv = y_ref[...]
        xo = x_ref[...] + gate_ref[...] * ((yv * _rstd(yv)) * gp_ref[...])
        o_ref[...] = xo
        h_ref[...] = (((xo * _rstd(xo)) * g_ref[...]) * (1.0 + sc_ref[...]) + sh_ref[...]).astype(BF16)

    vecs, vec_specs = _vec_args(g_post, gate, g_pre, shift, scale)
    return pl.pallas_call(
        body, name=name, grid=(s // ROW_TILE,),
        in_specs=[_row_spec(D), _row_spec(D)] + vec_specs, out_specs=[_row_spec(D), _row_spec(D)],
        out_shape=[jax.ShapeDtypeStruct((s, D), F32), jax.ShapeDtypeStruct((s, D), BF16)], compiler_params=_params("parallel"),
    )(x, y, *vecs)


def _pre_post_bwd(dh, x, dxo, g_pre, scale, y, g_post, gate, name):
    s = dh.shape[0]

    def body(dh_ref, x_ref, d_ref, y_ref, g_ref, sc_ref, gp_ref, gate_ref, dx_ref, dy_ref, sum_ref):
        @pl.when(pl.program_id(0) == 0)
        def _():
            sum_ref[...] = jnp.zeros_like(sum_ref)

        dhv, xv = dh_ref[...], x_ref[...]
        r = _rstd(xv)
        n = xv * r
        one_sc = 1.0 + sc_ref[...]
        sum_ref[0:1, :] += jnp.sum(dhv, axis=0, keepdims=True)
        sum_ref[1:2, :] += jnp.sum(dhv * (n * g_ref[...]), axis=0, keepdims=True)
        sum_ref[2:3, :] += jnp.sum((dhv * one_sc) * n, axis=0, keepdims=True)
        dn = (dhv * one_sc) * g_ref[...]
        dv = d_ref[...] + r * (dn - n * jnp.mean(dn * n, axis=-1, keepdims=True))
        dx_ref[...] = dv

        yv = y_ref[...]
        ry = _rstd(yv)
        ny = yv * ry
        sum_ref[3:4, :] += jnp.sum(dv * (ny * gp_ref[...]), axis=0, keepdims=True)
        sum_ref[4:5, :] += jnp.sum((dv * gate_ref[...]) * ny, axis=0, keepdims=True)
        dny = (dv * gate_ref[...]) * gp_ref[...]
        dy_ref[...] = (ry * (dny - ny * jnp.mean(dny * ny, axis=-1, keepdims=True))).astype(BF16)

    vecs, vec_specs = _vec_args(g_pre, scale, g_post, gate)
    return pl.pallas_call(
        body, name=name, grid=(s // ROW_TILE,),
        in_specs=[_row_spec(D)] * 4 + vec_specs,
        out_specs=[_row_spec(D), _row_spec(D), _sum_spec(D)],
        out_shape=[jax.ShapeDtypeStruct((s, D), F32), jax.ShapeDtypeStruct((s, D), BF16), jax.ShapeDtypeStruct((8, D), F32)],
        compiler_params=_params("arbitrary"),
    )(dh, x, dxo, y, *vecs)


def _loss_head(y, target):
    s = y.shape[0]

    def body(y_ref, t_ref, dy_ref, sum_ref):
        @pl.when(pl.program_id(0) == 0)
        def _():
            sum_ref[...] = jnp.zeros_like(sum_ref)

        err = y_ref[...] - t_ref[...]
        dy_ref[...] = err * (1.0 / D)
        sum_ref[...] += jnp.sum(err * err)

    return pl.pallas_call(
        body, name="loss_head", grid=(s // ROW_TILE,),
        in_specs=[_row_spec(D), _row_spec(D)],
        out_specs=[_row_spec(D), pl.BlockSpec((8, 128), lambda i: (0, 0))],
        out_shape=[jax.ShapeDtypeStruct((s, D), F32), jax.ShapeDtypeStruct((8, 128), F32)],
        compiler_params=_params("arbitrary"),
    )(y, target)


def _merge_fwd(z, pa, pb, pc):
    s = z.shape[0]

    def body(g0_ref, g1_ref, g2_ref, pa_ref, pb_ref, pc_ref, o_ref):
        o_ref[...] = (jax.nn.sigmoid(g0_ref[...]) * pa_ref[...] + jax.nn.sigmoid(g1_ref[...]) * pb_ref[...]
                      + jax.nn.sigmoid(g2_ref[...]) * pc_ref[...]).astype(BF16)

    return pl.pallas_call(
        body, name="merge_fwd", grid=(s // ROW_TILE,),
        in_specs=[_row_spec(D, 0), _row_spec(D, 1), _row_spec(D, 2), _row_spec(D), _row_spec(D), _row_spec(D)],
        out_specs=_row_spec(D), out_shape=jax.ShapeDtypeStruct((s, D), BF16),
        compiler_params=_params("parallel"),
    )(z, z, z, pa, pb, pc)


def _merge_bwd(dm, z, pa, pb, pc):
    s = z.shape[0]

    def body(dm_ref, g0_ref, g1_ref, g2_ref, pa_ref, pb_ref, pc_ref, dgl_ref, da_ref, db_ref, dc_ref):
        dmv = dm_ref[...]
        for i, (g_ref, p_ref, d_ref) in enumerate(((g0_ref, pa_ref, da_ref), (g1_ref, pb_ref, db_ref), (g2_ref, pc_ref, dc_ref))):
            gate = jax.nn.sigmoid(g_ref[...])
            dgl_ref[:, i * D:(i + 1) * D] = ((dmv * p_ref[...]) * (gate * (1.0 - gate))).astype(BF16)
            d_ref[...] = (dmv * gate).astype(BF16)

    return pl.pallas_call(
        body, name="merge_bwd", grid=(s // ROW_TILE,),
        in_specs=[_row_spec(D), _row_spec(D, 0), _row_spec(D, 1), _row_spec(D, 2), _row_spec(D), _row_spec(D), _row_spec(D)],
        out_specs=[_row_spec(3 * D), _row_spec(D), _row_spec(D), _row_spec(D)],
        out_shape=[jax.ShapeDtypeStruct((s, Z_COLS), BF16)] + [jax.ShapeDtypeStruct((s, D), BF16)] * 3,
        compiler_params=_params("parallel"),
    )(dm, z, z, z, pa, pb, pc)


def _shift_down(v, n):
    row = lax.broadcasted_iota(jnp.int32, v.shape, 0)
    return jnp.where(row >= n, pltpu.roll(v, n, axis=0), 0.0)


def _shift_up(v, n):
    s = v.shape[0]
    row = lax.broadcasted_iota(jnp.int32, v.shape, 0)
    return jnp.where(row < s - n, pltpu.roll(v, s - n, axis=0), 0.0)


def _log_sigmoid(v):
    return jnp.minimum(v, 0.0) - jnp.log1p(jnp.exp(-jnp.abs(v)))


def _cumf_fwd(fl, bias):
    s = fl.shape[0]

    def body(fl_ref, b_ref, o_ref):
        acc = _log_sigmoid(fl_ref[...] + b_ref[...])
        step = 1
        while step < s:
            acc = acc + _shift_down(acc, step)
            step *= 2
        o_ref[...] = acc

    return pl.pallas_call(body, name="cumf_fwd", out_shape=jax.ShapeDtypeStruct((s, 128), F32),
                          compiler_params=pltpu.CompilerParams(vmem_limit_bytes=V7X_VMEM_LIMIT))(fl, bias)


def _cumf_bwd(dcum, fl, bias):
    s = fl.shape[0]

    def body(d_ref, fl_ref, b_ref, dfl_ref, db_ref):
        acc = d_ref[...]
        step = 1
        while step < s:
            acc = acc + _shift_up(acc, step)
            step *= 2
        dfl = acc * jax.nn.sigmoid(-(fl_ref[...] + b_ref[...]))
        dfl_ref[...] = dfl.astype(BF16)
        db_ref[...] = jnp.broadcast_to(jnp.sum(dfl, axis=0, keepdims=True), (8, 128))

    return pl.pallas_call(
        body, name="cumf_bwd",
        out_shape=[jax.ShapeDtypeStruct((s, 128), BF16), jax.ShapeDtypeStruct((8, 128), F32)],
        compiler_params=pltpu.CompilerParams(vmem_limit_bytes=V7X_VMEM_LIMIT))(dcum, fl, bias)


def _pool_windows(v, shift):
    s2 = v + shift(v, 1)
    s4 = s2 + shift(s2, 2)
    s8 = s4 + shift(s4, 4)
    s16 = s8 + shift(s8, 8)
    group = lax.broadcasted_iota(jnp.int32, v.shape, 1) // 64
    return jnp.where(group == 0, s2, jnp.where(group == 1, s4, jnp.where(group == 2, s8, s16)))


def _pool_count(shape):
    group = lax.broadcasted_iota(jnp.int32, shape, 1) // 64
    window = jnp.where(group == 0, 2.0, jnp.where(group == 1, 4.0, jnp.where(group == 2, 8.0, 16.0)))
    t1 = (lax.broadcasted_iota(jnp.int32, shape, 0) + 1).astype(F32)
    return jnp.minimum(t1, window)


def _pc_specs(s):
    zcol = lambda blk: pl.BlockSpec((s, 256), lambda i, blk=blk: (0, blk))
    first = Z_PC // 256
    return [zcol(first), zcol(first + 1), zcol(first + 2), zcol(first + 3),
            pl.BlockSpec((256, 256), lambda i: (0, 0)), pl.BlockSpec((1, 256), lambda i: (0, 0)),
            pl.BlockSpec((3, 256), lambda i: (0, 0))]


def _poolconv_fwd(z, wbd, pscale, convw):
    s = z.shape[0]

    def body(pu_ref, ch_ref, cb_ref, cc_ref, w_ref, ps_ref, cw_ref, yb_ref, yc_ref):
        u = pu_ref[...]
        p = _pool_windows(u, _shift_down) / _pool_count(u.shape) - u
        yb = jnp.dot(p.astype(BF16), w_ref[...].astype(BF16), preferred_element_type=F32) * ps_ref[...]
        yb_ref[...] = yb.astype(BF16)
        uc = cc_ref[...] * ch_ref[...]
        cw = cw_ref[...]
        conv = cw[0:1, :] * _shift_down(uc, 2) + cw[1:2, :] * _shift_down(uc, 1) + cw[2:3, :] * uc
        yc_ref[...] = (cb_ref[...] * conv).astype(BF16)

    out = pl.BlockSpec((s, 256), lambda i: (0, 0))
    return pl.pallas_call(
        body, name="poolconv_fwd", grid=(1,), in_specs=_pc_specs(s), out_specs=[out, out],
        out_shape=[jax.ShapeDtypeStruct((s, 256), BF16)] * 2, compiler_params=_params("arbitrary"),
    )(z, z, z, z, wbd, pscale, convw)


def _poolconv_bwd(dyb, dyc, z, wbd, pscale, convw):
    s = z.shape[0]

    def body(dyb_ref, dyc_ref, pu_ref, ch_ref, cb_ref, cc_ref, w_ref, ps_ref, cw_ref, dz_ref, dw_ref, dps_ref, dcw_ref):
        u = pu_ref[...]
        count = _pool_count(u.shape)
        p = (_pool_windows(u, _shift_down) / count - u).astype(BF16)
        wb = w_ref[...].astype(BF16)
        dyb_v = dyb_ref[...]
        pw = jnp.dot(p, wb, preferred_element_type=F32)
        dps_ref[...] = jnp.broadcast_to(jnp.sum(dyb_v * pw, axis=0, keepdims=True), (8, 256))
        dys = (dyb_v * ps_ref[...]).astype(BF16)
        dp = lax.dot_general(dys, wb, (((1,), (1,)), ((), ())), preferred_element_type=F32)
        dw_ref[...] = lax.dot_general(p, dys, (((0,), (0,)), ((), ())), preferred_element_type=F32)
        dz_ref[:, 0:256] = (_pool_windows(dp / count, _shift_up) - dp).astype(BF16)

        ch, cb, cc = ch_ref[...], cb_ref[...], cc_ref[...]
        uc = cc * ch
        cw = cw_ref[...]
        u2, u1 = _shift_down(uc, 2), _shift_down(uc, 1)
        conv = cw[0:1, :] * u2 + cw[1:2, :] * u1 + cw[2:3, :] * uc
        dyc_v = dyc_ref[...]
        dconv = dyc_v * cb
        du = cw[0:1, :] * _shift_up(dconv, 2) + cw[1:2, :] * _shift_up(dconv, 1) + cw[2:3, :] * dconv
        dz_ref[:, 256:512] = (du * cc).astype(BF16)
        dz_ref[:, 512:768] = (dyc_v * conv).astype(BF16)
        dz_ref[:, 768:1024] = (du * ch).astype(BF16)
        dcw_ref[...] = jnp.zeros_like(dcw_ref)
        dcw_ref[0:1, :] = jnp.sum(dconv * u2, axis=0, keepdims=True)
        dcw_ref[1:2, :] = jnp.sum(dconv * u1, axis=0, keepdims=True)
        dcw_ref[2:3, :] = jnp.sum(dconv * uc, axis=0, keepdims=True)

    blk = lambda r, c: pl.BlockSpec((r, c), lambda i: (0, 0))
    return pl.pallas_call(
        body, name="poolconv_bwd", grid=(1,),
        in_specs=[blk(s, 256), blk(s, 256)] + _pc_specs(s),
        out_specs=[blk(s, 1024), blk(256, 256), blk(8, 256), blk(8, 256)],
        out_shape=[jax.ShapeDtypeStruct((s, 1024), BF16), jax.ShapeDtypeStruct((256, 256), F32),
                   jax.ShapeDtypeStruct((8, 256), F32), jax.ShapeDtypeStruct((8, 256), F32)],
        compiler_params=_params("arbitrary"),
    )(dyb, dyc, z, z, z, z, wbd, pscale, convw)


_NT = (((1,), (1,)), ((), ()))
_TN = (((0,), (0,)), ((), ()))


ATT_Q, ATT_K = 256, 256
ATT_HEADS_BWD = 8
ATT_HEADS = 8


def _att_logits(q, k, fr, q0, k0, masked):
    logits = lax.dot_general(q, k, _NT, preferred_element_type=F32) - fr
    if not masked:
        return logits
    row = q0 + lax.broadcasted_iota(jnp.int32, logits.shape, 0)
    col = k0 + lax.broadcasted_iota(jnp.int32, logits.shape, 1)
    return jnp.where(row >= col, logits, NEG_INF)


def _causal_sweep(step, qi, init):
    n_full = (qi * ATT_Q) // ATT_K
    carry = lax.fori_loop(0, n_full, lambda j, carry: step(j, carry, False), init)
    return step(n_full, carry, True)


HEAD_PAIRS = HEADS // 2


def _lane_pick(v, lane, idx):
    return jnp.sum(jnp.where(lane == idx, v, 0.0), axis=-1, keepdims=True)


def _lane_put(lane, idx, col):
    return jnp.where(lane == idx, col, 0.0)


def _split_heads(v, low):
    zero = jnp.zeros_like(v)
    return jnp.where(low, v, zero), jnp.where(low, zero, v)


def _attn_fwd(qkv, fr):
    s = qkv.shape[0]
    nk = s // ATT_K
    width = ATT_HEADS * HEAD_DIM
    groups = HEADS // ATT_HEADS

    def body(q_ref, k_ref, v_ref, fr_ref, o_ref, lse_ref):
        qi, grp = pl.program_id(0), pl.program_id(1)
        lane = lax.broadcasted_iota(jnp.int32, (ATT_Q, 128), 1)
        low = lane < HEAD_DIM
        qs = []
        for pr in range(ATT_HEADS // 2):
            qs += _split_heads(q_ref[:, 128 * pr:128 * (pr + 1)] * (HEAD_DIM ** -0.5), low)

        def step(j, carry, masked):
            k0 = pl.multiple_of(j * ATT_K, ATT_K)
            out = []
            for h in range(ATT_HEADS):
                cols = slice(128 * (h // 2), 128 * (h // 2 + 1))
                m, l, acc = carry[h]
                logits = _att_logits(qs[h], k_ref[pl.ds(k0, ATT_K), cols], fr_ref[h, pl.ds(j, 1), :], qi * ATT_Q, k0, masked)
                m_new = jnp.maximum(m, jnp.max(logits, axis=-1, keepdims=True))
                p = jnp.exp(logits - m_new)
                alpha = jnp.exp(m - m_new)
                l = alpha * l + jnp.sum(p, axis=-1, keepdims=True)
                acc = alpha * acc + jnp.dot(p.astype(BF16), v_ref[pl.ds(k0, ATT_K), cols], preferred_element_type=F32)
                out.append((m_new, l, acc))
            return tuple(out)

        one = (jnp.full((ATT_Q, 1), NEG_INF, F32), jnp.zeros((ATT_Q, 1), F32), jnp.zeros((ATT_Q, 128), F32))
        done = _causal_sweep(step, qi, (one,) * ATT_HEADS)

        @pl.when(grp == 0)
        def _():
            lse_ref[...] = jnp.zeros_like(lse_ref)

        lse = jnp.zeros((ATT_Q, 128), F32)
        for pr in range(ATT_HEADS // 2):
            (m0, l0, acc0), (m1, l1, acc1) = done[2 * pr], done[2 * pr + 1]
            o_ref[:, 128 * pr:128 * (pr + 1)] = jnp.where(low, acc0 / l0, acc1 / l1)
            head = ATT_HEADS * grp + 2 * pr
            lse = lse + _lane_put(lane, head, m0 + jnp.log(l0)) + _lane_put(lane, head + 1, m1 + jnp.log(l1))
        lse_ref[...] += lse

    return pl.pallas_call(
        body, name="attn_fwd", grid=(s // ATT_Q, groups),
        in_specs=[pl.BlockSpec((ATT_Q, width), lambda i, g: (i, g)),
                  pl.BlockSpec((s, width), lambda i, g: (0, groups + g)),
                  pl.BlockSpec((s, width), lambda i, g: (0, 2 * groups + g)),
                  pl.BlockSpec((ATT_HEADS, nk, ATT_K), lambda i, g: (g, 0, 0))],
        out_specs=[pl.BlockSpec((ATT_Q, width), lambda i, g: (i, g)), pl.BlockSpec((ATT_Q, 128), lambda i, g: (i, 0))],
        out_shape=[jax.ShapeDtypeStruct((s, A_WIDTH), F32), jax.ShapeDtypeStruct((s, 128), F32)],
        compiler_params=_params("parallel", "arbitrary"),
    )(qkv, qkv, qkv, fr)


def _attn_bwd(qkv, do, o, lse, fr):
    s = qkv.shape[0]
    nk = s // ATT_K
    scale = HEAD_DIM ** -0.5
    heads = ATT_HEADS_BWD
    width = heads * HEAD_DIM
    groups = HEADS // heads

    def body(q_ref, k_ref, v_ref, do_ref, o_ref, lse_ref, fr_ref, dq_ref, dk_ref, dv_ref, dfc_ref, dfr_ref, dk_acc, dv_acc):
        grp = pl.program_id(0)
        lane = lax.broadcasted_iota(jnp.int32, (ATT_Q, 128), 1)
        low = lane < HEAD_DIM
        low_t = lax.broadcasted_iota(jnp.int32, (128, ATT_Q), 0) < HEAD_DIM
        dk_acc[...] = jnp.zeros_like(dk_acc)
        dv_acc[...] = jnp.zeros_like(dv_acc)
        dfr_ref[...] = jnp.zeros_like(dfr_ref)

        @pl.when(grp == 0)
        def _():
            dfc_ref[...] = jnp.zeros_like(dfc_ref)

        def outer(i, carry):
            q0 = pl.multiple_of(i * ATT_Q, ATT_Q)
            rows = pl.ds(q0, ATT_Q)
            lsev = lse_ref[rows, :]
            qts, dots, qs, dos, deltas, lses = [], [], [], [], [], []
            for pr in range(heads // 2):
                pcols = slice(128 * pr, 128 * (pr + 1))
                q2, do2 = q_ref[rows, pcols] * scale, do_ref[rows, pcols]
                prod = do2 * o_ref[rows, pcols]
                deltas += [jnp.sum(jnp.where(low, prod, 0.0), axis=-1, keepdims=True),
                           jnp.sum(jnp.where(low, 0.0, prod), axis=-1, keepdims=True)]
                dob2 = do2.astype(BF16)
                qts += _split_heads(q2.astype(F32).T.astype(BF16), low_t)
                dots += _split_heads(do2.T.astype(BF16), low_t)
                qs += _split_heads(q2, low)
                dos += _split_heads(dob2, low)
                lses += [_lane_pick(lsev, lane, heads * grp + 2 * pr), _lane_pick(lsev, lane, heads * grp + 2 * pr + 1)]

            def inner(j, carry, masked):
                k0 = pl.multiple_of(j * ATT_K, ATT_K)
                krows = pl.ds(k0, ATT_K)
                out, dkt, dvt = [], [], []
                for h in range(heads):
                    pcols = slice(128 * (h // 2), 128 * (h // 2 + 1))
                    dq, dfc = carry[h]
                    k2 = k_ref[krows, pcols]
                    p = jnp.exp(_att_logits(qs[h], k2, fr_ref[h, pl.ds(j, 1), :], q0, k0, masked) - lses[h])
                    dp = lax.dot_general(dos[h], v_ref[krows, pcols], _NT, preferred_element_type=F32)
                    ds = p * (dp - deltas[h])
                    dsb = ds.astype(BF16)
                    dkt.append(jnp.dot(qts[h], dsb, preferred_element_type=F32))
                    dvt.append(jnp.dot(dots[h], p.astype(BF16), preferred_element_type=F32))
                    dfr_ref[h, pl.ds(j, 1), :] -= jnp.sum(ds, axis=0, keepdims=True)
                    out.append((dq + jnp.dot(dsb, k2, preferred_element_type=F32), dfc + (ds[:, :128] + ds[:, 128:])))
                for pr in range(heads // 2):
                    prows = slice(128 * pr, 128 * (pr + 1))
                    dk_acc[j, prows, :] += dkt[2 * pr] + dkt[2 * pr + 1]
                    dv_acc[j, prows, :] += dvt[2 * pr] + dvt[2 * pr + 1]
                return tuple(out)

            one = (jnp.zeros((ATT_Q, 128), F32), jnp.zeros((ATT_Q, 128), F32))
            done = _causal_sweep(inner, i, (one,) * heads)
            dfc = jnp.zeros((ATT_Q, 128), F32)
            for pr in range(heads // 2):
                (dq0, dfc0), (dq1, dfc1) = done[2 * pr], done[2 * pr + 1]
                dq_ref[rows, 128 * pr:128 * (pr + 1)] = (jnp.where(low, dq0, dq1) * scale).astype(BF16)
                head = heads * grp + 2 * pr
                dfc = (dfc + _lane_put(lane, head, jnp.sum(dfc0, axis=-1, keepdims=True))
                       + _lane_put(lane, head + 1, jnp.sum(dfc1, axis=-1, keepdims=True)))
            dfc_ref[rows, :] += dfc
            return carry

        lax.fori_loop(0, s // ATT_Q, outer, 0)
        for j in range(nk):
            for pr in range(heads // 2):
                prows, pcols = slice(128 * pr, 128 * (pr + 1)), slice(128 * pr, 128 * (pr + 1))
                dk_ref[ATT_K * j:ATT_K * (j + 1), pcols] = dk_acc[j, prows, :].T.astype(BF16)
                dv_ref[ATT_K * j:ATT_K * (j + 1), pcols] = dv_acc[j, prows, :].T.astype(BF16)

    part = lambda first: pl.BlockSpec((s, width), lambda g, first=first: (0, first + g))
    whole = pl.BlockSpec((s, 128), lambda g: (0, 0))
    rowv = pl.BlockSpec((heads, nk, ATT_K), lambda g: (g, 0, 0))
    return pl.pallas_call(
        body, name="attn_bwd", grid=(groups,),
        in_specs=[part(0), part(groups), part(2 * groups), part(0), part(0), whole, rowv],
        out_specs=[part(0), part(0), part(0), whole, rowv],
        out_shape=[jax.ShapeDtypeStruct((s, A_WIDTH), BF16)] * 3 + [jax.ShapeDtypeStruct((s, 128), F32), jax.ShapeDtypeStruct((HEADS, nk, ATT_K), F32)],
        scratch_shapes=[pltpu.VMEM((nk, width, ATT_K), F32), pltpu.VMEM((nk, width, ATT_K), F32)],
        compiler_params=_params("arbitrary"),
    )(qkv, qkv, qkv, do, o, lse, fr)


def _ada_fwd(c_all, w_ada, b_loc):
    depth, _, n = w_ada.shape
    tn = 512

    def body(c_ref, w_ref, b_ref, o_ref, sc_ref):
        cv = c_ref[...]
        sc = cv * jax.nn.sigmoid(cv)
        sc_ref[...] = sc
        o_ref[0] = jnp.dot(sc.astype(BF16), w_ref[0].astype(BF16), preferred_element_type=F32) + b_ref[0]

    return pl.pallas_call(
        body, name="ada_fwd", grid=(depth, n // tn),
        in_specs=[pl.BlockSpec((N_DEV, D), lambda l, j: (0, 0)), pl.BlockSpec((1, D, tn), lambda l, j: (l, 0, j)),
                  pl.BlockSpec((1, 1, tn), lambda l, j: (l, 0, j))],
        out_specs=[pl.BlockSpec((1, N_DEV, tn), lambda l, j: (l, 0, j)), pl.BlockSpec((N_DEV, D), lambda l, j: (0, 0))],
        out_shape=[jax.ShapeDtypeStruct((depth, N_DEV, n), F32), jax.ShapeDtypeStruct((N_DEV, D), F32)],
        compiler_params=_params("arbitrary", "arbitrary"),
    )(c_all, w_ada, b_loc)


def _sum_devices(gathered):
    n = gathered.shape[1]
    tn = _pick(n, (1408, 1024, 640, 512, 128))

    def body(g_ref, o_ref):
        acc = g_ref[0:8, :]
        for dev in range(1, N_DEV):
            acc = acc + g_ref[8 * dev:8 * dev + 8, :]
        o_ref[...] = acc

    return pl.pallas_call(
        body, name="sum_devices", grid=(n // tn,),
        in_specs=[pl.BlockSpec((8 * N_DEV, tn), lambda j: (0, j))], out_specs=pl.BlockSpec((8, tn), lambda j: (0, j)),
        out_shape=jax.ShapeDtypeStruct((8, n), F32), compiler_params=_params("parallel"),
    )(gathered)


def _place():
    x, y, c = lax.axis_index("x"), lax.axis_index("y"), lax.axis_index("c")
    chips = [(1 - x, y), (x, 1 - y), (1 - x, 1 - y)]
    return x, y, c, chips


def _allgather8(block, name):
    m_per, n = block.shape

    def body(x_ref, out_ref, send_sems, recv_sems, local_sem):
        x, y, c, chips = _place()
        me, sibling = (x, y, c), (x, y, 1 - c)

        def rows(px, py, pc):
            return out_ref.at[pl.ds((4 * px + 2 * py + pc) * m_per, m_per), :]

        def copy(k, blk, to, src=None):
            return pltpu.make_async_remote_copy(
                src_ref=rows(*blk) if src is None else src, dst_ref=rows(*blk),
                send_sem=send_sems.at[k], recv_sem=recv_sems.at[k], device_id=to, device_id_type=MESH)

        mine = pltpu.make_async_copy(x_ref, rows(*me), local_sem)
        mine.start()
        first = [copy(0, me, sibling, src=x_ref)]
        first += [copy(1 + j, me, (*chip, c), src=x_ref) for j, chip in enumerate(chips)]
        for cp in first:
            cp.start()
        passed = [copy(4 + j, (*chip, c), sibling) for j, chip in enumerate(chips)]
        for j, chip in enumerate(chips):
            copy(1 + j, (*chip, c), me).wait_recv()
            passed[j].start()
        copy(0, sibling, me).wait_recv()
        for j, chip in enumerate(chips):
            copy(4 + j, (*chip, 1 - c), me).wait_recv()
        for cp in first + passed:
            cp.wait_send()
        mine.wait()

    return pl.pallas_call(
        body, name=name, out_shape=jax.ShapeDtypeStruct((N_DEV * m_per, n), block.dtype),
        in_specs=[pl.BlockSpec(memory_space=pltpu.VMEM)], out_specs=pl.BlockSpec(memory_space=pltpu.VMEM),
        scratch_shapes=[pltpu.SemaphoreType.DMA((7,)), pltpu.SemaphoreType.DMA((7,)), pltpu.SemaphoreType.DMA],
        compiler_params=pltpu.CompilerParams(vmem_limit_bytes=V7X_VMEM_LIMIT),
    )(block)


_SEM = pl.BlockSpec(memory_space=pltpu.SEMAPHORE)
_DATAFLOW = pltpu.SideEffectType.DATAFLOW_SIDE_EFFECTING


def _plan_copies(plan, refs, send_sems, recv_sems):
    return [pltpu.make_async_remote_copy(src_ref=src, dst_ref=dst, send_sem=send_sems.at[i], recv_sem=recv_sems.at[i],
                                         device_id=to, device_id_type=MESH) for i, (src, dst, to) in enumerate(plan(refs))]


class _Token(NamedTuple):
    after: jax.Array
    tie: jax.Array


def _after_operand(after):
    return after.after if isinstance(after, _Token) else after


def _copies_start(bufs, plan, n_copies, after, name):
    nb = len(bufs)

    def body(*refs):
        for cp in _plan_copies(plan, refs[:nb], refs[nb + 1], refs[nb + 2]):
            cp.start()
        for token in refs[-2:]:
            token[...] = jnp.zeros_like(token)

    sem = pltpu.SemaphoreType.DMA((n_copies,))
    vmem = pl.BlockSpec(memory_space=pltpu.VMEM)
    outs = pl.pallas_call(
        body, name=name,
        out_shape=(sem, sem, *[pltpu.HBM(b.shape, b.dtype) for b in bufs], jax.ShapeDtypeStruct((8, 128), F32),
                   jax.ShapeDtypeStruct((1, 1), F32)),
        in_specs=[_HBM] * nb + [pl.BlockSpec(memory_space=pl.ANY)],
        out_specs=(_SEM, _SEM, *[_HBM] * nb, vmem, vmem),
        input_output_aliases={i: 2 + i for i in range(nb)},
        compiler_params=pltpu.CompilerParams(has_side_effects=_DATAFLOW),
    )(*[pltpu.with_memory_space_constraint(b, pltpu.HBM) for b in bufs], _after_operand(after))
    return outs[0], outs[1], list(outs[2:2 + nb]), _Token(outs[-2], outs[-1])


def _copies_wait(started, plan, after, name):
    send_sems, recv_sems, bufs, _ = started
    nb = len(bufs)

    def body(*refs):
        for cp in _plan_copies(plan, refs[:nb], refs[nb], refs[nb + 1]):
            cp.wait_send()
            cp.wait_recv()

    return list(pl.pallas_call(
        body, name=name, out_shape=tuple(pltpu.HBM(b.shape, b.dtype) for b in bufs),
        in_specs=[_HBM] * nb + [_SEM, _SEM, pl.BlockSpec(memory_space=pl.ANY)], out_specs=tuple([_HBM] * nb),
        input_output_aliases={i: i for i in range(nb)},
        compiler_params=pltpu.CompilerParams(has_side_effects=_DATAFLOW),
    )(*bufs, send_sems, recv_sems, _after_operand(after)))


def _half_rows(ref, axis, c):
    half = ref.shape[axis] // 2
    return pl.ds(c * half, half)


def _plan_gather_ici(refs):
    n = len(refs) // 2
    x, y, c, chips = _place()
    out = []
    for a in range(n):
        rows = _half_rows(refs[a], 0, c)
        out += [(refs[a].at[rows], refs[n + a].at[2 * x + y, rows], (*chip, c)) for chip in chips]
        out.append((refs[a], refs[n + a].at[2 * x + y], (x, y, 1 - c)))
    return out


def _plan_gather_d2d(refs):
    x, y, c, chips = _place()
    out = []
    for ref in refs:
        rows = _half_rows(ref, 1, c)
        for px, py in chips:
            landed = ref.at[2 * px + py, rows]
            out.append((landed, landed, (x, y, 1 - c)))
    return out


def _plan_rs_sibling(refs):
    n = len(refs) // 2
    x, y, c, _ = _place()
    return [(refs[a].at[pl.ds(0, N_CHIPS), _half_rows(refs[a], 1, 1 - c)], refs[n + a], (x, y, 1 - c)) for a in range(n)]


def _plan_rs_chips(refs):
    n = len(refs) // 2
    x, y, c, chips = _place()
    return [(refs[a].at[2 * px + py], refs[n + a].at[k], (px, py, c)) for a in range(n) for k, (px, py) in enumerate(chips)]


def _plan_rs_share(layer):
    def plan(refs):
        x, y, c, _ = _place()
        return [(ref.at[layer, _half_rows(ref, 1, c)], ref.at[layer, _half_rows(ref, 1, c)], (x, y, 1 - c)) for ref in refs]
    return plan


def _chip_sum(g, other, sel, name):
    _, half, cdim = other.shape
    tr = _pick(half, (512, 256, 128, 64))
    per = half // tr

    def body(sel_ref, g_ref, t_ref, wire_ref, own_ref):
        total = g_ref[0] + t_ref[0]
        wire_ref[0] = total.astype(BF16)

        @pl.when(pl.program_id(1) == sel_ref[1])
        def _():
            own_ref[...] = total

    blk = pl.BlockSpec((1, tr, cdim), lambda i, p, sel_ref: (p, i, 0))
    return pl.pallas_call(
        body, name=name,
        grid_spec=pltpu.PrefetchScalarGridSpec(
            num_scalar_prefetch=1, grid=(per, N_CHIPS),
            in_specs=[pl.BlockSpec((1, tr, cdim), lambda i, p, sel_ref: (p, sel_ref[0] * per + i, 0)), blk],
            out_specs=[blk, pl.BlockSpec((tr, cdim), lambda i, p, sel_ref: (i, 0))]),
        out_shape=[jax.ShapeDtypeStruct(other.shape, BF16), jax.ShapeDtypeStruct((half, cdim), F32)],
        compiler_params=_params("parallel", "arbitrary"),
    )(sel, g, other)


def _final_sum(own, recv, sel, layer, into, name):
    half, cdim = own.shape
    tr = _pick(half, (512, 256, 128, 64))
    per = half // tr

    def body(sel_ref, own_ref, r0_ref, r1_ref, r2_ref, *rest):
        rest[-1][...] = ((own_ref[...] + r0_ref[0].astype(F32)) + r1_ref[0].astype(F32)) + r2_ref[0].astype(F32)

    part = lambda k: pl.BlockSpec((1, tr, cdim), lambda i, sel_ref, k=k: (k, i, 0))
    prior = [] if into is None else [into]
    return pl.pallas_call(
        body, name=name,
        grid_spec=pltpu.PrefetchScalarGridSpec(
            num_scalar_prefetch=1, grid=(per,),
            in_specs=[pl.BlockSpec((tr, cdim), lambda i, sel_ref: (i, 0)), part(0), part(1), part(2)]
            + [pl.BlockSpec(memory_space=pl.ANY)] * len(prior),
            out_specs=pl.BlockSpec((None, tr, cdim), lambda i, sel_ref: (layer, sel_ref[0] * per + i, 0))),
        out_shape=jax.ShapeDtypeStruct((DEPTH, 2 * half, cdim), F32),
        input_output_aliases={5: 0} if prior else {}, compiler_params=_params("parallel"),
    )(sel, own, recv, recv, recv, *prior)


def _row(v):
    return v.reshape(1, -1)


_BR_A, _BR_B, _BR_C = (0, A_WIDTH), (A_WIDTH, POOL_WIDTH), (A_WIDTH + POOL_WIDTH, CONV_WIDTH)


def _tie(v, token):
    return v if token is None else v + token.tie


def _no_hook(point, after, ready=None):
    return None


def _layer_fwd(x, w, mod, hook=_no_hook):
    s = x.shape[0]
    mod3 = mod.reshape(6, 1, D)
    h = _modnorm_fwd(x, _row(w["g_mix_pre"]), (mod3, 0), (mod3, 1), "mix_pre_fwd")
    hook("pre", h)
    z = _mm(h, w["w_all"], name="mm_in")
    qkv = z[:, Z_QKV:Z_PC].astype(BF16)
    fl = z[:, Z_FL:Z_COLS]
    cum = _cumf_fwd(fl, w["b_f_pad"])
    fr = cum[:, :HEADS].T.reshape(HEADS, s // ATT_K, ATT_K)
    br_a, lse = _attn_fwd(qkv, fr)
    br_b, br_c = _poolconv_fwd(z, w["w_pool_bd"], _tie(_row(w["pool_scale"]), hook("attn", lse)), w["conv_w"])
    hook("pool", br_b)
    wbr = w["w_branch"]
    pa = _mm(br_a, wbr, b_rows=_BR_A, name="mm_br_a")
    pb = _mm(br_b, wbr, b_rows=_BR_B, name="mm_br_b")
    pc = _mm(br_c, wbr, b_rows=_BR_C, name="mm_br_c")
    merged = _merge_fwd(z, pa, pb, pc)
    y = _mm(merged, w["w_out"], name="mm_out")
    x1, h2 = _post_pre_fwd(x, y, _row(w["g_mix_post"]), (mod3, 2), _row(w["g_ff_pre"]), (mod3, 3), (mod3, 4), "mix_post_ff_pre_fwd")
    a, r = _mm(h2, w["w_ff1"], b_split=N_CHIPS, epilogue=_relu2_fwd, out_dtype=(F32, BF16), name="mm_ff1")
    y2 = _mm(r, w["w_ff2"], name="mm_ff2")
    x2 = _post_fwd(x1, y2, _tie(_row(w["g_ff_post"]), hook("ff_post", y2)), (mod3, 5), "ff_post_fwd")
    hook("end", x2)
    saved = dict(x=x, h=h, z=z, qkv=qkv, fl=fl, fr=fr, lse=lse, br_a=br_a, br_b=br_b, br_c=br_c, pa=pa, pb=pb, pc=pc,
                 merged=merged, y=y, x1=x1, h2=h2, a=a, r=r, y2=y2)
    return x2, saved


def _layer_bwd(dx2, sv, w, mod, hook=_no_hook):
    s = dx2.shape[0]
    mod3 = mod.reshape(6, 1, D)
    dy2, sum_ff_post = _post_bwd(dx2, sv["y2"], _row(w["g_ff_post"]), (mod3, 5), "ff_post_bwd")
    (da,) = _mm(dy2, w["w_ff2"], tb=True, epilogue=_relu2_bwd, extras=(sv["a"],), out_dtype=(BF16,), name="mm_ff2_dx")
    d_w_ff2 = _mm(sv["r"], dy2, ta=True, name="mm_ff2_dw")
    dh2 = _mm(da, w["w_ff1"], tb=True, b_split=N_CHIPS, name="mm_ff1_dx")
    d_w_ff1 = _mm(sv["h2"], da, ta=True, out_split=N_CHIPS, name="mm_ff1_dw")
    g_ff_pre = _tie(_row(w["g_ff_pre"]), hook("ff_pre", dh2, dict(w_ff1=d_w_ff1, w_ff2=d_w_ff2)))
    dx1, dy, sum_mid = _pre_post_bwd(dh2, sv["x1"], dx2, g_ff_pre, (mod3, 4), sv["y"], _row(w["g_mix_post"]), (mod3, 2), "ff_pre_mix_post_bwd")
    sum_ff_pre, sum_mix_post = sum_mid, sum_mid[3:]
    dmerged = _mm(dy, w["w_out"], tb=True, name="mm_out_dx")
    d_w_out = _mm(sv["merged"], dy, ta=True, name="mm_out_dw")
    dz, dpa, dpb, dpc = _merge_bwd(dmerged, sv["z"], sv["pa"], sv["pb"], sv["pc"])
    wbr = w["w_branch"]
    dbr_a = _mm(dpa, wbr, tb=True, b_rows=_BR_A, name="mm_br_a_dx")
    dbr_b = _mm(dpb, wbr, tb=True, b_rows=_BR_B, name="mm_br_b_dx")
    dbr_c = _mm(dpc, wbr, tb=True, b_rows=_BR_C, name="mm_br_c_dx")
    d_w_branch = jnp.concatenate([_mm(sv["br_a"], dpa, ta=True, name="mm_br_a_dw"), _mm(sv["br_b"], dpb, ta=True, name="mm_br_b_dw"),
                                  _mm(sv["br_c"], dpc, ta=True, name="mm_br_c_dw")], axis=0)

    dq, dk, dv, dfc, dfr = _attn_bwd(sv["qkv"], dbr_a, sv["br_a"], sv["lse"], sv["fr"])
    dcum = dfc + jnp.pad(dfr.reshape(HEADS, s).T, ((0, 0), (0, 128 - HEADS)))
    dfl, sum_bf = _cumf_bwd(dcum, sv["fl"], _tie(w["b_f_pad"], hook("cumf", dfc)))
    dpc_z, d_wbd, sum_ps, sum_cw = _poolconv_bwd(dbr_b, dbr_c, sv["z"], w["w_pool_bd"], _row(w["pool_scale"]), w["conv_w"])
    for at, part in ((Z_QKV, dq), (Z_QKV + A_WIDTH, dk), (Z_QKV + 2 * A_WIDTH, dv), (Z_PC, dpc_z), (Z_FL, dfl)):
        dz = lax.dynamic_update_slice(dz, part, (0, at))
    dh = _mm(dz, w["w_all"], tb=True, name="mm_in_dx")
    d_w_all = _mm(sv["h"], dz, ta=True, name="mm_in_dw")
    hook("mix_pre", dh)
    dx, sum_mix_pre = _modnorm_bwd(dh, sv["x"], dx1, _row(w["g_mix_pre"]), (mod3, 1), "mix_pre_bwd")

    dmod = jnp.stack([sum_mix_pre[0], sum_mix_pre[1], sum_mix_post[0], sum_ff_pre[0], sum_ff_pre[1], sum_ff_post[0]])
    d_w_in = _w_in_shards(d_w_all)
    d_w_pool = jnp.stack([d_wbd[64 * g:64 * g + 64, 64 * g:64 * g + 64] for g in range(4)])
    big = dict(w_in=d_w_in, w_branch=d_w_branch, w_out=d_w_out, w_ff1=d_w_ff1, w_ff2=d_w_ff2)
    small = dict(g_mix_pre=sum_mix_pre[2], g_mix_post=sum_mix_post[1], g_ff_pre=sum_ff_pre[2], g_ff_post=sum_ff_post[1],
                 b_f=sum_bf[0, :HEADS], w_pool=d_w_pool, pool_scale=sum_ps[0], conv_w=sum_cw[0:3])
    return dx, dmod, big, small


_QKV_END, _FL_END, _PC_END = 3 * A_WIDTH, 3 * A_WIDTH + HEADS, 3 * A_WIDTH + HEADS + POOL_WIDTH + 3 * CONV_WIDTH
_W_IN_GROUPS = ((_PC_END, IN_COLS, Z_GL), (0, _QKV_END, Z_QKV), (_FL_END, _PC_END, Z_PC), (_QKV_END, _FL_END, Z_FL))
_SHARD_COLS = IN_COLS // N_CHIPS


def _w_all_from_shards(blocks):
    pieces = []
    for lo, hi, _ in _W_IN_GROUPS:
        for p in range(N_CHIPS):
            a, b = max(lo, p * _SHARD_COLS), min(hi, (p + 1) * _SHARD_COLS)
            if a < b:
                pieces.append(blocks[p][:, a - p * _SHARD_COLS:b - p * _SHARD_COLS])
    pieces.append(jnp.zeros((D, Z_COLS - IN_COLS), blocks.dtype))
    return jnp.concatenate(pieces, axis=1)


def _w_in_shards(d_w_all):
    blocks = []
    for p in range(N_CHIPS):
        pieces = []
        for lo, hi, at in sorted(_W_IN_GROUPS):
            a, b = max(lo, p * _SHARD_COLS), min(hi, (p + 1) * _SHARD_COLS)
            if a < b:
                pieces.append(d_w_all[:, at + a - lo:at + b - lo])
        blocks.append(jnp.concatenate(pieces, axis=1))
    return jnp.stack(blocks)


def _full_layer_weights(w_in_blocks, w_branch, w_out, w_ff1, w_ff2, g_mix_pre, g_mix_post, g_ff_pre, g_ff_post, b_f, w_pool, pool_scale, conv_w):
    w_all = None if w_in_blocks is None else _w_all_from_shards(w_in_blocks)
    wbd = (w_pool[:, :, None, :] * jnp.eye(4, dtype=F32)[:, None, :, None]).reshape(POOL_WIDTH, POOL_WIDTH)
    return dict(w_all=w_all, w_branch=w_branch, w_out=w_out, w_ff1=w_ff1, w_ff2=w_ff2, g_mix_pre=g_mix_pre, g_mix_post=g_mix_post,
                g_ff_pre=g_ff_pre, g_ff_post=g_ff_post, b_f_pad=jnp.pad(b_f, (0, 128 - HEADS)).reshape(1, 128), w_pool_bd=wbd,
                pool_scale=pool_scale, conv_w=conv_w)


class _NoComm:
    def layer_weights(self, l):
        raise NotImplementedError

    def fwd_hook(self, l):
        return _no_hook

    def bwd_hook(self, l):
        return _no_hook

    def grads_ready(self, l, big):
        return None


class _Layers(_NoComm):
    def __init__(self, layers):
        self.layers = layers

    def layer_weights(self, l):
        return self.layers[l]


def _local_step(x, target, mods, comm):
    saved, weights = [], []
    act = x
    for l in range(DEPTH):
        weights.append(comm.layer_weights(l))
        act, sv = _layer_fwd(act, weights[l], mods[l], comm.fwd_hook(l))
        saved.append(sv)
    dact, sq = _loss_head(act, target)
    loss = sq[0, 0] * (0.5 / D)
    dmods, bigs, smalls = [None] * DEPTH, [None] * DEPTH, [None] * DEPTH
    token = None
    for l in reversed(range(DEPTH)):
        dact, dmods[l], bigs[l], smalls[l] = _layer_bwd(dact, saved[l], weights[l], _tie(mods[l], token), comm.bwd_hook(l))
        token = comm.grads_ready(l, bigs[l])
    return loss, dact, jnp.stack(dmods), bigs, smalls


_BIG = ("w_in", "w_branch", "w_out", "w_ff1", "w_ff2")


class _GatherJob:
    def __init__(self, tag, shards, after):
        self.tag, self.n = tag, len(shards)
        lands = [lax.empty((N_CHIPS,) + s.shape, s.dtype) for s in shards]
        self.state = _copies_start(list(shards) + lands, _plan_gather_ici, 4 * self.n, after, "gather_ici_start_" + tag)
        self.token = self.state[3]

    def pass_on(self, after):
        bufs = _copies_wait(self.state, _plan_gather_ici, after, "gather_ici_wait_" + self.tag)
        self.state = _copies_start(bufs[self.n:], _plan_gather_d2d, 3 * self.n, bufs[0], "gather_d2d_start_" + self.tag)
        self.token = self.state[3]
        return self.token

    def done(self, after):
        return _copies_wait(self.state, _plan_gather_d2d, after, "gather_d2d_wait_" + self.tag)


class _ReduceJob:
    def __init__(self, tag, names, grads, sel, after, layer, into=None):
        self.tag, self.names, self.n, self.sel, self.layer, self.into = tag, names, len(names), sel, layer, into or {}
        lands = [lax.empty((N_CHIPS, g.shape[1] // 2, g.shape[2]), F32) for g in grads]
        self.state = _copies_start(list(grads) + lands, _plan_rs_sibling, self.n, after, "rs_sibling_start_" + tag)
        self.token = self.state[3]

    def chip_sums(self, after):
        bufs = _copies_wait(self.state, _plan_rs_sibling, after, "rs_sibling_wait_" + self.tag)
        wires, self.owns = zip(*[_chip_sum(bufs[i], bufs[self.n + i], self.sel, "rs_chip_sum_" + name) for i, name in enumerate(self.names)])
        lands = [lax.empty((3,) + w.shape[1:], BF16) for w in wires]
        self.state = _copies_start(list(wires) + lands, _plan_rs_chips, 3 * self.n, self.owns[0], "rs_chips_start_" + self.tag)
        self.token = self.state[3]
        return self.token

    def final_sums(self, after):
        bufs = _copies_wait(self.state, _plan_rs_chips, after, "rs_chips_wait_" + self.tag)
        sums = [_final_sum(self.owns[i], bufs[self.n + i], self.sel, self.layer, self.into.get(name), "rs_final_" + name)
                for i, name in enumerate(self.names)]
        self.state = _copies_start(sums, _plan_rs_share(self.layer), self.n, sums[0], "rs_share_start_" + self.tag)
        self.token = self.state[3]
        return self.token

    def done(self, after):
        return dict(zip(self.names, _copies_wait(self.state, _plan_rs_share(self.layer), after, "rs_share_wait_" + self.tag)))


def _chip_blocks(g):
    return g if g.ndim == 3 else g.reshape(N_CHIPS, -1, g.shape[1])


class _StepComm(_NoComm):
    def __init__(self, shards, sel, after):
        self.sel = sel
        self.small, self.grads, self.jobs = None, {}, {}
        self.jobs["in0"] = _GatherJob("in0", shards[0][:1], after)
        self.jobs["rest0"] = _GatherJob("rest0", shards[0][1:], self.jobs["in0"].token)
        self.jobs["all1"] = _GatherJob("all1", shards[1], self.jobs["rest0"].token)

    def layer_weights(self, l):
        if l == 0:
            self.weights0 = _full_layer_weights(None, None, None, None, None, *self.small[0])
            return self.weights0
        g_in, g_br, g_out, g_f1, g_f2 = self.landed1
        return _full_layer_weights(g_in, g_br.reshape(D, D), g_out.reshape(D, D), g_f1, g_f2.reshape(D_FF, D), *self.small[1])

    def fwd_hook(self, l):
        if l != 0:
            return _no_hook

        def hook(point, after, ready=None):
            if point == "pre":
                job = self.jobs["in0"]
                started = after[:8, :128].astype(F32) + self.jobs["all1"].token.after
                self.weights0["w_all"] = _w_all_from_shards(job.done(job.pass_on(started))[0])
            if point == "attn":
                return self.jobs["rest0"].pass_on(after)
            if point == "ff_post":
                return self.jobs["all1"].pass_on(after)
            if point == "pool":
                g_br, g_out, g_f1, g_f2 = self.jobs["rest0"].done(after)
                self.weights0.update(w_branch=g_br.reshape(D, D), w_out=g_out.reshape(D, D), w_ff1=g_f1, w_ff2=g_f2.reshape(D_FF, D))
            if point == "end":
                self.landed1 = self.jobs["all1"].done(after)
            return None
        return hook

    def bwd_hook(self, l):
        if l != 0:
            return _no_hook

        def hook(point, after, ready=None):
            jobs = self.jobs
            if point == "ff_pre":
                token = jobs["rs1"].chip_sums(after)
                jobs["rs0_ff"] = _ReduceJob("0_ff", ("w_ff1", "w_ff2"), [_chip_blocks(ready[n]) for n in ("w_ff1", "w_ff2")], self.sel, token, 0)
                return jobs["rs0_ff"].token
            if point == "cumf":
                return jobs["rs0_ff"].chip_sums(jobs["rs1"].final_sums(after))
            self.layer1 = jobs["rs1"].done(after)
            jobs["rs0_ff"].into = self.layer1
            return None
        return hook

    def grads_ready(self, l, big):
        if l == 1:
            self.jobs["rs1"] = _ReduceJob("1", _BIG, [_chip_blocks(big[n]) for n in _BIG], self.sel, self.sel, 1)
            return self.jobs["rs1"].token
        names = ("w_in", "w_branch", "w_out")
        self.jobs["rs0_mix"] = _ReduceJob("0_mix", names, [_chip_blocks(big[n]) for n in names], self.sel, self.sel, 0, self.layer1)
        return self.jobs["rs0_mix"].token

    def finish_sums(self, after):
        jobs = self.jobs
        token = jobs["rs0_mix"].chip_sums(after)
        return jobs["rs0_ff"].final_sums(token)

    def finish_ff(self, after):
        self.grads.update(self.jobs["rs0_ff"].done(after))

    def finish_mix(self, after):
        job = self.jobs["rs0_mix"]
        self.grads.update(job.done(job.final_sums(after)))


_SMALL = ("g_mix_pre", "g_mix_post", "g_ff_pre", "g_ff_post", "b_f", "w_pool", "pool_scale", "conv_w")


def _w_in_view(t):
    return t.reshape(DEPTH, D // 128, 128, _SHARD_COLS).transpose(3, 1, 0, 2).reshape(_SHARD_COLS * (D // 128) * DEPTH, 128)


def _w_in_unview(t):
    return t.reshape(_SHARD_COLS, D // 128, DEPTH, 128).transpose(2, 1, 3, 0).reshape(DEPTH, D, _SHARD_COLS)


def _pack(parts, rows=8):
    flat = jnp.concatenate([p.reshape(-1) for p in parts])
    width = -(-flat.shape[0] // (rows * 128)) * 128
    return jnp.pad(flat, (0, rows * width - flat.shape[0])).reshape(rows, width)


def _unpack(packed, like):
    flat = packed.reshape(-1)
    out, at = [], 0
    for ref in like:
        out.append(flat[at:at + ref.size].reshape(ref.shape))
        at += ref.size
    return out


def kernel(x, c, w_ada, b_ada, g_mix_pre, g_mix_post, g_ff_pre, g_ff_post, w_in, b_f, w_pool, pool_scale, conv_w, w_branch, w_out, w_ff1, w_ff2, loss_target, m_w_ada, m_b_ada, m_g_mix_pre, m_g_mix_post, m_g_ff_pre, m_g_ff_post, m_w_in, m_b_f, m_w_pool, m_pool_scale, m_conv_w, m_w_branch, m_w_out, m_w_ff1, m_w_ff2, v_w_ada, v_b_ada, v_g_mix_pre, v_g_mix_post, v_g_ff_pre, v_g_ff_post, v_w_in, v_b_f, v_w_pool, v_pool_scale, v_conv_w, v_w_branch, v_w_out, v_w_ff1, v_w_ff2):
    xi, yi, ci = lax.axis_index("x"), lax.axis_index("y"), lax.axis_index("c")
    chip = 2 * xi + yi
    dev = 2 * chip + ci
    n_ada = w_ada.shape[2]

    first = jnp.zeros((8, D + 384), F32).at[0, :D].set(c[0]).at[0, D:].set(conv_w.reshape(-1))
    got = _allgather8(first, "gather_cond").reshape(N_DEV, 8, D + 384)[:, 0]
    c_all = got[:, :D]
    conv_full = got[0::2, D:].reshape(N_CHIPS, DEPTH, 3, CONV_WIDTH // N_CHIPS).transpose(1, 2, 0, 3).reshape(DEPTH, 3, CONV_WIDTH)

    b_loc = lax.dynamic_slice_in_dim(b_ada, chip * n_ada, n_ada, axis=1).reshape(DEPTH, 1, n_ada)
    mod_cols, silu_c = _ada_fwd(c_all, w_ada, b_loc)
    got = _allgather8(mod_cols.reshape(DEPTH * N_DEV, n_ada), "gather_mod").reshape(N_DEV, DEPTH, N_DEV, n_ada)[0::2]
    mod_all = got.transpose(1, 2, 0, 3).reshape(DEPTH, N_DEV, 6, D)
    mods = lax.dynamic_index_in_dim(mod_all, dev, axis=1, keepdims=False)

    comm = _StepComm([[w[l].astype(BF16) for w in (w_in, w_branch, w_out, w_ff1, w_ff2)] for l in range(DEPTH)],
                     jnp.stack([ci, chip]).astype(jnp.int32), mods)
    comm.small = [(g_mix_pre[l], g_mix_post[l], g_ff_pre[l], g_ff_post[l], b_f[l], w_pool[l], pool_scale[l], conv_full[l]) for l in range(DEPTH)]
    loss_part, grad_x, dmods, bigs, smalls = _local_step(x[0], loss_target[0], mods, comm)

    small_parts = [smalls[l][name] for name in _SMALL for l in range(DEPTH)] + [loss_part.reshape(1)]
    packed = _tie(_pack([dmods] + small_parts), comm.jobs["rs0_mix"].token)
    gathered = _allgather8(packed, "gather_small")
    dmod_all = gathered.reshape(N_DEV, -1)[:, :dmods.size].reshape(N_DEV, DEPTH, 6 * D)
    summed = _unpack(_sum_devices(gathered), [dmods] + small_parts)
    grad_b_ada = summed[0].reshape(DEPTH, 6 * D)
    loss = summed[-1][0]
    small_grads = {name: jnp.stack(summed[1 + 2 * i:3 + 2 * i]) for i, name in enumerate(_SMALL)}
    small_grads["conv_w"] = lax.dynamic_slice_in_dim(small_grads["conv_w"], chip * (CONV_WIDTH // N_CHIPS), CONV_WIDTH // N_CHIPS, axis=2)

    dmod_loc = lax.dynamic_slice_in_dim(dmod_all.transpose(1, 0, 2), chip * n_ada, n_ada, axis=2)
    tail_token = comm.finish_sums(grad_b_ada)
    silu_pad = _tie(jnp.pad(silu_c, ((0, 128 - N_DEV), (0, 0))), tail_token)
    dmod_pad = jnp.pad(dmod_loc.transpose(1, 0, 2).reshape(N_DEV, DEPTH * n_ada), ((0, 128 - N_DEV), (0, 0)))
    grad_w_ada = _mm(silu_pad, dmod_pad, ta=True, out_split=DEPTH, name="mm_ada_dw")

    grads = dict(w_ada=grad_w_ada, b_ada=grad_b_ada, **small_grads)
    weights = dict(w_ada=w_ada, b_ada=b_ada, g_mix_pre=g_mix_pre, g_mix_post=g_mix_post, g_ff_pre=g_ff_pre, g_ff_post=g_ff_post, w_in=w_in,
                   b_f=b_f, w_pool=w_pool, pool_scale=pool_scale, conv_w=conv_w, w_branch=w_branch, w_out=w_out, w_ff1=w_ff1, w_ff2=w_ff2)
    m_in = dict(w_ada=m_w_ada, b_ada=m_b_ada, g_mix_pre=m_g_mix_pre, g_mix_post=m_g_mix_post, g_ff_pre=m_g_ff_pre, g_ff_post=m_g_ff_post,
                w_in=m_w_in, b_f=m_b_f, w_pool=m_w_pool, pool_scale=m_pool_scale, conv_w=m_conv_w, w_branch=m_w_branch, w_out=m_w_out,
                w_ff1=m_w_ff1, w_ff2=m_w_ff2)
    v_in = dict(w_ada=v_w_ada, b_ada=v_b_ada, g_mix_pre=v_g_mix_pre, g_mix_post=v_g_mix_post, g_ff_pre=v_g_ff_pre, g_ff_post=v_g_ff_post,
                w_in=v_w_in, b_f=v_b_f, w_pool=v_w_pool, pool_scale=v_pool_scale, conv_w=v_conv_w, w_branch=v_w_branch, w_out=v_w_out,
                w_ff1=v_w_ff1, w_ff2=v_w_ff2)
    order = ("w_ada", "b_ada", "g_mix_pre", "g_mix_post", "g_ff_pre", "g_ff_post", "w_in", "b_f", "w_pool", "pool_scale", "conv_w",
             "w_branch", "w_out", "w_ff1", "w_ff2")
    delta, new_m, new_v = {}, {}, {}
    tiny = ("b_ada",) + _SMALL
    tiny_g = [_tie(grads[tiny[0]], tail_token)] + [grads[name] for name in tiny[1:]]
    res = _adamw_many([weights[name] for name in tiny], tiny_g, [m_in[name] for name in tiny], [v_in[name] for name in tiny], "adamw_small")
    for out, vals in zip((delta, new_m, new_v), res):
        out.update(zip(tiny, vals))
    delta["w_ada"], new_m["w_ada"], new_v["w_ada"] = _adamw(w_ada, grad_w_ada, m_w_ada, v_w_ada, "adamw_w_ada")
    comm.finish_ff(delta["w_ada"][0, :8, :128] + delta["b_ada"][0, :128])
    for name in ("w_ff1", "w_ff2", "w_in", "w_branch", "w_out"):
        if name == "w_in":
            comm.finish_mix(delta["w_ff2"][0, :8, :128])
        grads[name] = comm.grads[name]
        if name == "w_in":
            g_view = lax.optimization_barrier(_w_in_view(grads[name]))
            res = _adamw(_w_in_view(w_in), g_view, _w_in_view(m_w_in), _w_in_view(v_w_in), "adamw_w_in")
            grads[name], delta[name], new_m[name], new_v[name] = [_w_in_unview(t) for t in (g_view, *res)]
        else:
            delta[name], new_m[name], new_v[name] = _adamw(weights[name], grads[name], m_in[name], v_in[name], "adamw_" + name)

    return (loss, grad_x[None], *[grads[n] for n in order], *[delta[n] for n in order], *[new_m[n] for n in order],
            *[new_v[n] for n in order])
```

```python
from typing import NamedTuple

import jax
import jax.numpy as jnp
from jax import lax
from jax.experimental import pallas as pl
from jax.experimental.pallas import tpu as pltpu

F32 = jnp.float32
BF16 = jnp.bfloat16
MESH = pl.DeviceIdType.MESH

D = 1024
DEPTH = 2
HEADS = 8
HEAD_DIM = 64
A_WIDTH = 512
POOL_WIDTH = 256
CONV_WIDTH = 256
D_FF = 4096
IN_COLS = 5640
Z_GL, Z_QKV, Z_PC, Z_FL, Z_COLS = 0, 3072, 4608, 5632, 5760
RMS_EPS = 1e-6
NEG_INF = -1e30
ROW_TILE = 512
EW_ROWS = 256
N_CHIPS = 4
N_DEV = 8
V7X_VMEM_LIMIT = 48 * 1024 * 1024

ADAM_LR = 0.001
ADAM_B1 = 0.9
ADAM_B2 = 0.999
ADAM_EPS = 1e-08
ADAM_WD = 0.01
ADAM_STEP = 10

_HBM = pl.BlockSpec(memory_space=pltpu.HBM)


def _params(*sem):
    return pltpu.CompilerParams(dimension_semantics=sem, vmem_limit_bytes=V7X_VMEM_LIMIT)


def _pick(dim, cands):
    for cand in cands:
        if dim % cand == 0:
            return cand
    return dim


MM_TILE_BUDGET = 39 * 1024 * 1024


def _mm_tiles(m, n, k, k_unit, tn, a_size, b_size, out_size):
    for tk in (k_unit, 2048, 1152, 1024, 640, 512, 256, 128):
        if k_unit % tk:
            continue
        for tm in (2048, 1024, 512, 256, 128):
            if m % tm or ((m // tm) * (n // tn) < 2 and tm > 512):
                continue
            need = 2 * (tm * tk * a_size + tk * tn * b_size + tm * tn * out_size) + (0 if tk == k else 4 * tm * tn)
            if need <= MM_TILE_BUDGET and (tk == k_unit or tm >= 512):
                return tm, tk
    return 128, 128


def _mm(a, b, *, ta=False, tb=False, b_rows=None, b_split=1, out_split=1, out_dtype=F32, epilogue=None, extras=(), name):
    (k, m) = a.shape if ta else a.shape[::-1]
    b_row0, b_rows = (0, b.shape[-2]) if b_rows is None else b_rows
    b_cols = b.shape[-1] * b_split
    (n, k2) = (b_rows, b_cols) if tb else (b_cols, b_rows)
    assert k == k2, (a.shape, b.shape, ta, tb)
    n_unit = n // (out_split * (1 if tb else b_split))
    k_unit = k // (b_split if tb else 1)
    tn = _pick(n_unit, (1024, 1152, 768, 640, 512, 256, 128))
    tm, tk = _mm_tiles(m, n, k, k_unit, tn, a.dtype.itemsize, b.dtype.itemsize,
                       sum(jnp.dtype(dt).itemsize for dt in out_dtype) + 4 * len(extras) if epilogue else jnp.dtype(out_dtype).itemsize)
    nk = k // tk
    dims = (((0 if ta else 1,), (1 if tb else 0,)), ((), ()))

    def dot(a_ref, b_ref):
        b_val = b_ref[0] if b_split > 1 else b_ref[...]
        return lax.dot_general(a_ref[...].astype(BF16), b_val.astype(BF16), dims, preferred_element_type=F32)

    n_extra = len(extras)
    assert epilogue is None or out_split == 1

    def put(refs, val):
        if epilogue is not None:
            for o_ref, res in zip(refs[n_extra:], epilogue(val, *[r[...] for r in refs[:n_extra]])):
                o_ref[...] = res.astype(o_ref.dtype)
        elif out_split > 1:
            refs[0][0] = val.astype(refs[0].dtype)
        else:
            refs[0][...] = val.astype(refs[0].dtype)

    def body_single(a_ref, b_ref, *refs):
        put(refs, dot(a_ref, b_ref))

    def body_acc(a_ref, b_ref, *refs):
        kk = pl.program_id(2)
        acc_ref = refs[-1]

        @pl.when(kk == 0)
        def _():
            acc_ref[...] = jnp.zeros_like(acc_ref)

        acc_ref[...] += dot(a_ref, b_ref)

        @pl.when(kk == nk - 1)
        def _():
            put(refs[:-1], acc_ref[...])

    a_spec = pl.BlockSpec((tk, tm), lambda i, j, kk: (kk, i)) if ta else pl.BlockSpec((tm, tk), lambda i, j, kk: (i, kk))
    if b_split == 1:
        off = b_row0 // (tn if tb else tk)
        assert off * (tn if tb else tk) == b_row0
        b_spec = pl.BlockSpec((tn, tk), lambda i, j, kk: (j + off, kk)) if tb else pl.BlockSpec((tk, tn), lambda i, j, kk: (kk + off, j))
    elif tb:
        per = k_unit // tk
        b_spec = pl.BlockSpec((1, tn, tk), lambda i, j, kk: (kk // per, j, kk % per))
    else:
        per = n // b_split // tn
        b_spec = pl.BlockSpec((1, tk, tn), lambda i, j, kk: (j // per, kk, j % per))
    if out_split == 1:
        o_spec = pl.BlockSpec((tm, tn), lambda i, j, kk: (i, j))
        o_shape = None if epilogue is not None else jax.ShapeDtypeStruct((m, n), out_dtype)
    else:
        per_o = n // out_split // tn
        o_spec = pl.BlockSpec((1, tm, tn), lambda i, j, kk: (j // per_o, i, j % per_o))
        o_shape = jax.ShapeDtypeStruct((out_split, m, n // out_split), out_dtype)
    if epilogue is not None:
        o_shape = [jax.ShapeDtypeStruct((m, n), dt) for dt in out_dtype]
        o_spec = [o_spec] * len(out_dtype)
    return pl.pallas_call(
        body_single if nk == 1 else body_acc, name=name, grid=(m // tm, n // tn, nk),
        in_specs=[a_spec, b_spec] + [pl.BlockSpec((tm, tn), lambda i, j, kk: (i, j))] * n_extra, out_specs=o_spec, out_shape=o_shape,
        scratch_shapes=[] if nk == 1 else [pltpu.VMEM((tm, tn), F32)],
        compiler_params=_params("parallel", "parallel", "arbitrary"),
    )(a, b, *extras)


def _ew(fn, ins, out_dtypes, name, tc=None):
    shape = ins[0].shape
    lead, (rows, cols) = shape[:-2], shape[-2:]
    tc = cols if tc is None else tc
    if tc > 1024:
        tr = _pick(rows, (EW_ROWS, 128, 8))
    elif tc > 128:
        tr = _pick(rows, (2 * EW_ROWS, EW_ROWS, 128, 8))
    else:
        tr = _pick(rows, (4096, 2256, 2048, 1024, EW_ROWS, 8))
    n_in = len(ins)

    def body(*refs):
        res = fn(*[r[...] for r in refs[:n_in]])
        for o_ref, val in zip(refs[n_in:], res):
            o_ref[...] = val.astype(o_ref.dtype)

    if lead:
        spec = pl.BlockSpec((None, tr, tc), lambda l, i, j: (l, i, j))
    else:
        spec = pl.BlockSpec((tr, tc), lambda i, j: (i, j))
    return pl.pallas_call(
        body, name=name, grid=lead + (rows // tr, cols // tc),
        in_specs=[spec] * n_in, out_specs=[spec] * len(out_dtypes),
        out_shape=[jax.ShapeDtypeStruct(shape, dt) for dt in out_dtypes],
        compiler_params=_params(*(["parallel"] * (len(lead) + 2))),
    )(*ins)


def _relu2_fwd(a):
    r = jnp.maximum(a, 0.0)
    return a, r * r


def _relu2_bwd(dr, a):
    return (dr * (2.0 * jnp.maximum(a, 0.0)),)


def _adamw_math(w, g, m, v):
    m = ADAM_B1 * m + (1.0 - ADAM_B1) * g
    v = ADAM_B2 * v + (1.0 - ADAM_B2) * (g * g)
    m_hat = m / (1.0 - ADAM_B1 ** ADAM_STEP)
    v_hat = v / (1.0 - ADAM_B2 ** ADAM_STEP)
    delta = -ADAM_LR * (m_hat / (jnp.sqrt(v_hat) + ADAM_EPS) + ADAM_WD * w)
    return delta, m, v


def _adamw(w, g, m, v, name):
    return _ew(_adamw_math, [w, g, m, v], [F32, F32, F32], name)


def _adamw_many(ws, gs, ms, vs, name):
    n = len(ws)

    def body(*refs):
        for i in range(n):
            res = _adamw_math(*[refs[k * n + i][...] for k in range(4)])
            for k in range(3):
                refs[(4 + k) * n + i][...] = res[k]

    outs = pl.pallas_call(
        body, name=name, out_shape=[jax.ShapeDtypeStruct(w.shape, F32) for w in ws] * 3,
        compiler_params=pltpu.CompilerParams(vmem_limit_bytes=V7X_VMEM_LIMIT),
    )(*ws, *gs, *ms, *vs)
    return outs[:n], outs[n:2 * n], outs[2 * n:]


def _row_spec(cols, block=0):
    return pl.BlockSpec((ROW_TILE, cols), lambda i, block=block: (i, block))


def _vec_spec(cols):
    return pl.BlockSpec((1, cols), lambda i: (0, 0))


def _vec_args(*vecs):
    arrays = [v[0] if isinstance(v, tuple) else v for v in vecs]
    specs = [pl.BlockSpec((None, 1, D), lambda i, row=v[1]: (row, 0, 0)) if isinstance(v, tuple) else _vec_spec(D) for v in vecs]
    return arrays, specs


def _sum_spec(cols):
    return pl.BlockSpec((8, cols), lambda i: (0, 0))


def _rstd(x):
    return lax.rsqrt(jnp.mean(x * x, axis=-1, keepdims=True) + RMS_EPS)


def _modnorm_fwd(x, g, shift, scale, name):
    s = x.shape[0]

    def body(x_ref, g_ref, sh_ref, sc_ref, h_ref):
        xv = x_ref[...]
        n = xv * _rstd(xv)
        h_ref[...] = ((n * g_ref[...]) * (1.0 + sc_ref[...]) + sh_ref[...]).astype(BF16)

    vecs, vec_specs = _vec_args(g, shift, scale)
    return pl.pallas_call(
        body, name=name, grid=(s // ROW_TILE,),
        in_specs=[_row_spec(D)] + vec_specs, out_specs=_row_spec(D),
        out_shape=jax.ShapeDtypeStruct((s, D), BF16), compiler_params=_params("parallel"),
    )(x, *vecs)


def _post_fwd(x, y, g, gate, name):
    s = x.shape[0]

    def body(x_ref, y_ref, g_ref, gate_ref, o_ref):
        yv = y_ref[...]
        o_ref[...] = x_ref[...] + gate_ref[...] * ((yv * _rstd(yv)) * g_ref[...])

    vecs, vec_specs = _vec_args(g, gate)
    return pl.pallas_call(
        body, name=name, grid=(s // ROW_TILE,),
        in_specs=[_row_spec(D), _row_spec(D)] + vec_specs, out_specs=_row_spec(D),
        out_shape=jax.ShapeDtypeStruct((s, D), F32), compiler_params=_params("parallel"),
    )(x, y, *vecs)


def _post_bwd(dxo, y, g, gate, name):
    s = dxo.shape[0]

    def body(d_ref, y_ref, g_ref, gate_ref, dy_ref, sum_ref):
        @pl.when(pl.program_id(0) == 0)
        def _():
            sum_ref[...] = jnp.zeros_like(sum_ref)

        dv, yv = d_ref[...], y_ref[...]
        r = _rstd(yv)
        n = yv * r
        sum_ref[0:1, :] += jnp.sum(dv * (n * g_ref[...]), axis=0, keepdims=True)
        sum_ref[1:2, :] += jnp.sum((dv * gate_ref[...]) * n, axis=0, keepdims=True)
        dn = (dv * gate_ref[...]) * g_ref[...]
        dy_ref[...] = (r * (dn - n * jnp.mean(dn * n, axis=-1, keepdims=True))).astype(BF16)

    vecs, vec_specs = _vec_args(g, gate)
    return pl.pallas_call(
        body, name=name, grid=(s // ROW_TILE,),
        in_specs=[_row_spec(D), _row_spec(D)] + vec_specs,
        out_specs=[_row_spec(D), _sum_spec(D)],
        out_shape=[jax.ShapeDtypeStruct((s, D), BF16), jax.ShapeDtypeStruct((8, D), F32)],
        compiler_params=_params("arbitrary"),
    )(dxo, y, *vecs)


def _modnorm_bwd(dh, x, dxo, g, scale, name):
    s = dh.shape[0]

    def body(dh_ref, x_ref, d_ref, g_ref, sc_ref, dx_ref, sum_ref):
        @pl.when(pl.program_id(0) == 0)
        def _():
            sum_ref[...] = jnp.zeros_like(sum_ref)

        dhv, xv = dh_ref[...], x_ref[...]
        r = _rstd(xv)
        n = xv * r
        one_sc = 1.0 + sc_ref[...]
        sum_ref[0:1, :] += jnp.sum(dhv, axis=0, keepdims=True)
        sum_ref[1:2, :] += jnp.sum(dhv * (n * g_ref[...]), axis=0, keepdims=True)
        sum_ref[2:3, :] += jnp.sum((dhv * one_sc) * n, axis=0, keepdims=True)
        dn = (dhv * one_sc) * g_ref[...]
        dx_ref[...] = d_ref[...] + r * (dn - n * jnp.mean(dn * n, axis=-1, keepdims=True))

    vecs, vec_specs = _vec_args(g, scale)
    return pl.pallas_call(
        body, name=name, grid=(s // ROW_TILE,),
        in_specs=[_row_spec(D), _row_spec(D), _row_spec(D)] + vec_specs,
        out_specs=[_row_spec(D), _sum_spec(D)],
        out_shape=[jax.ShapeDtypeStruct((s, D), F32), jax.ShapeDtypeStruct((8, D), F32)],
        compiler_params=_params("arbitrary"),
    )(dh, x, dxo, *vecs)


def _post_pre_fwd(x, y, g_post, gate, g_pre, shift, scale, name):
    s = x.shape[0]

    def body(x_ref, y_ref, gp_ref, gate_ref, g_ref, sh_ref, sc_ref, o_ref, h_ref):
        yv = y_ref[...]
        xo = x_ref[...] + gate_ref[...] * ((yv * _rstd(yv)) * gp_ref[...])
        o_ref[...] = xo
        h_ref[...] = (((xo * _rstd(xo)) * g_ref[...]) * (1.0 + sc_ref[...]) + sh_ref[...]).astype(BF16)

    vecs, vec_specs = _vec_args(g_post, gate, g_pre, shift, scale)
    return pl.pallas_call(
        body, name=name, grid=(s // ROW_TILE,),
        in_specs=[_row_spec(D), _row_spec(D)] + vec_specs, out_specs=[_row_spec(D), _row_spec(D)],
        out_shape=[jax.ShapeDtypeStruct((s, D), F32), jax.ShapeDtypeStruct((s, D), BF16)], compiler_params=_params("parallel"),
    )(x, y, *vecs)


def _pre_post_bwd(dh, x, dxo, g_pre, scale, y, g_post, gate, name):
    s = dh.shape[0]

    def body(dh_ref, x_ref, d_ref, y_ref, g_ref, sc_ref, gp_ref, gate_ref, dx_ref, dy_ref, sum_ref):
        @pl.when(pl.program_id(0) == 0)
        def _():
            sum_ref[...] = jnp.zeros_like(sum_ref)

        dhv, xv = dh_ref[...], x_ref[...]
        r = _rstd(xv)
        n = xv * r
        one_sc = 1.0 + sc_ref[...]
        sum_ref[0:1, :] += jnp.sum(dhv, axis=0, keepdims=True)
        sum_ref[1:2, :] += jnp.sum(dhv * (n * g_ref[...]), axis=0, keepdims=True)
        sum_ref[2:3, :] += jnp.sum((dhv * one_sc) * n, axis=0, keepdims=True)
        dn = (dhv * one_sc) * g_ref[...]
        dv = d_ref[...] + r * (dn - n * jnp.mean(dn * n, axis=-1, keepdims=True))
        dx_ref[...] = dv

        yv = y_ref[...]
        ry = _rstd(yv)
        ny = yv * ry
        sum_ref[3:4, :] += jnp.sum(dv * (ny * gp_ref[...]), axis=0, keepdims=True)
        sum_ref[4:5, :] += jnp.sum((dv * gate_ref[...]) * ny, axis=0, keepdims=True)
        dny = (dv * gate_ref[...]) * gp_ref[...]
        dy_ref[...] = (ry * (dny - ny * jnp.mean(dny * ny, axis=-1, keepdims=True))).astype(BF16)

    vecs, vec_specs = _vec_args(g_pre, scale, g_post, gate)
    return pl.pallas_call(
        body, name=name, grid=(s // ROW_TILE,),
        in_specs=[_row_spec(D)] * 4 + vec_specs,
        out_specs=[_row_spec(D), _row_spec(D), _sum_spec(D)],
        out_shape=[jax.ShapeDtypeStruct((s, D), F32), jax.ShapeDtypeStruct((s, D), BF16), jax.ShapeDtypeStruct((8, D), F32)],
        compiler_params=_params("arbitrary"),
    )(dh, x, dxo, y, *vecs)


def _loss_head(y, target):
    s = y.shape[0]

    def body(y_ref, t_ref, dy_ref, sum_ref):
        @pl.when(pl.program_id(0) == 0)
        def _():
            sum_ref[...] = jnp.zeros_like(sum_ref)

        err = y_ref[...] - t_ref[...]
        dy_ref[...] = err * (1.0 / D)
        sum_ref[...] += jnp.sum(err * err)

    return pl.pallas_call(
        body, name="loss_head", grid=(s // ROW_TILE,),
        in_specs=[_row_spec(D), _row_spec(D)],
        out_specs=[_row_spec(D), pl.BlockSpec((8, 128), lambda i: (0, 0))],
        out_shape=[jax.ShapeDtypeStruct((s, D), F32), jax.ShapeDtypeStruct((8, 128), F32)],
        compiler_params=_params("arbitrary"),
    )(y, target)


def _merge_fwd(z, pa, pb, pc):
    s = z.shape[0]

    def body(g0_ref, g1_ref, g2_ref, pa_ref, pb_ref, pc_ref, o_ref):
        o_ref[...] = (jax.nn.sigmoid(g0_ref[...]) * pa_ref[...] + jax.nn.sigmoid(g1_ref[...]) * pb_ref[...]
                      + jax.nn.sigmoid(g2_ref[...]) * pc_ref[...]).astype(BF16)

    return pl.pallas_call(
        body, name="merge_fwd", grid=(s // ROW_TILE,),
        in_specs=[_row_spec(D, 0), _row_spec(D, 1), _row_spec(D, 2), _row_spec(D), _row_spec(D), _row_spec(D)],
        out_specs=_row_spec(D), out_shape=jax.ShapeDtypeStruct((s, D), BF16),
        compiler_params=_params("parallel"),
    )(z, z, z, pa, pb, pc)


def _merge_bwd(dm, z, pa, pb, pc):
    s = z.shape[0]

    def body(dm_ref, g0_ref, g1_ref, g2_ref, pa_ref, pb_ref, pc_ref, dgl_ref, da_ref, db_ref, dc_ref):
        dmv = dm_ref[...]
        for i, (g_ref, p_ref, d_ref) in enumerate(((g0_ref, pa_ref, da_ref), (g1_ref, pb_ref, db_ref), (g2_ref, pc_ref, dc_ref))):
            gate = jax.nn.sigmoid(g_ref[...])
            dgl_ref[:, i * D:(i + 1) * D] = ((dmv * p_ref[...]) * (gate * (1.0 - gate))).astype(BF16)
            d_ref[...] = (dmv * gate).astype(BF16)

    return pl.pallas_call(
        body, name="merge_bwd", grid=(s // ROW_TILE,),
        in_specs=[_row_spec(D), _row_spec(D, 0), _row_spec(D, 1), _row_spec(D, 2), _row_spec(D), _row_spec(D), _row_spec(D)],
        out_specs=[_row_spec(3 * D), _row_spec(D), _row_spec(D), _row_spec(D)],
        out_shape=[jax.ShapeDtypeStruct((s, Z_COLS), BF16)] + [jax.ShapeDtypeStruct((s, D), BF16)] * 3,
        compiler_params=_params("parallel"),
    )(dm, z, z, z, pa, pb, pc)


def _shift_down(v, n):
    row = lax.broadcasted_iota(jnp.int32, v.shape, 0)
    return jnp.where(row >= n, pltpu.roll(v, n, axis=0), 0.0)


def _shift_up(v, n):
    s = v.shape[0]
    row = lax.broadcasted_iota(jnp.int32, v.shape, 0)
    return jnp.where(row < s - n, pltpu.roll(v, s - n, axis=0), 0.0)


def _log_sigmoid(v):
    return jnp.minimum(v, 0.0) - jnp.log1p(jnp.exp(-jnp.abs(v)))


def _cumf_fwd(fl, bias):
    s = fl.shape[0]

    def body(fl_ref, b_ref, o_ref):
        acc = _log_sigmoid(fl_ref[...] + b_ref[...])
        step = 1
        while step < s:
            acc = acc + _shift_down(acc, step)
            step *= 2
        o_ref[...] = acc

    return pl.pallas_call(body, name="cumf_fwd", out_shape=jax.ShapeDtypeStruct((s, 128), F32),
                          compiler_params=pltpu.CompilerParams(vmem_limit_bytes=V7X_VMEM_LIMIT))(fl, bias)


def _cumf_bwd(dcum, fl, bias):
    s = fl.shape[0]

    def body(d_ref, fl_ref, b_ref, dfl_ref, db_ref):
        acc = d_ref[...]
        step = 1
        while step < s:
            acc = acc + _shift_up(acc, step)
            step *= 2
        dfl = acc * jax.nn.sigmoid(-(fl_ref[...] + b_ref[...]))
        dfl_ref[...] = dfl.astype(BF16)
        db_ref[...] = jnp.broadcast_to(jnp.sum(dfl, axis=0, keepdims=True), (8, 128))

    return pl.pallas_call(
        body, name="cumf_bwd",
        out_shape=[jax.ShapeDtypeStruct((s, 128), BF16), jax.ShapeDtypeStruct((8, 128), F32)],
        compiler_params=pltpu.CompilerParams(vmem_limit_bytes=V7X_VMEM_LIMIT))(dcum, fl, bias)


def _pool_windows(v, shift):
    s2 = v + shift(v, 1)
    s4 = s2 + shift(s2, 2)
    s8 = s4 + shift(s4, 4)
    s16 = s8 + shift(s8, 8)
    group = lax.broadcasted_iota(jnp.int32, v.shape, 1) // 64
    return jnp.where(group == 0, s2, jnp.where(group == 1, s4, jnp.where(group == 2, s8, s16)))


def _pool_count(shape):
    group = lax.broadcasted_iota(jnp.int32, shape, 1) // 64
    window = jnp.where(group == 0, 2.0, jnp.where(group == 1, 4.0, jnp.where(group == 2, 8.0, 16.0)))
    t1 = (lax.broadcasted_iota(jnp.int32, shape, 0) + 1).astype(F32)
    return jnp.minimum(t1, window)


def _pc_specs(s):
    zcol = lambda blk: pl.BlockSpec((s, 256), lambda i, blk=blk: (0, blk))
    first = Z_PC // 256
    return [zcol(first), zcol(first + 1), zcol(first + 2), zcol(first + 3),
            pl.BlockSpec((256, 256), lambda i: (0, 0)), pl.BlockSpec((1, 256), lambda i: (0, 0)),
            pl.BlockSpec((3, 256), lambda i: (0, 0))]


def _poolconv_fwd(z, wbd, pscale, convw):
    s = z.shape[0]

    def body(pu_ref, ch_ref, cb_ref, cc_ref, w_ref, ps_ref, cw_ref, yb_ref, yc_ref):
        u = pu_ref[...]
        p = _pool_windows(u, _shift_down) / _pool_count(u.shape) - u
        yb = jnp.dot(p.astype(BF16), w_ref[...].astype(BF16), preferred_element_type=F32) * ps_ref[...]
        yb_ref[...] = yb.astype(BF16)
        uc = cc_ref[...] * ch_ref[...]
        cw = cw_ref[...]
        conv = cw[0:1, :] * _shift_down(uc, 2) + cw[1:2, :] * _shift_down(uc, 1) + cw[2:3, :] * uc
        yc_ref[...] = (cb_ref[...] * conv).astype(BF16)

    out = pl.BlockSpec((s, 256), lambda i: (0, 0))
    return pl.pallas_call(
        body, name="poolconv_fwd", grid=(1,), in_specs=_pc_specs(s), out_specs=[out, out],
        out_shape=[jax.ShapeDtypeStruct((s, 256), BF16)] * 2, compiler_params=_params("arbitrary"),
    )(z, z, z, z, wbd, pscale, convw)


def _poolconv_bwd(dyb, dyc, z, wbd, pscale, convw):
    s = z.shape[0]

    def body(dyb_ref, dyc_ref, pu_ref, ch_ref, cb_ref, cc_ref, w_ref, ps_ref, cw_ref, dz_ref, dw_ref, dps_ref, dcw_ref):
        u = pu_ref[...]
        count = _pool_count(u.shape)
        p = (_pool_windows(u, _shift_down) / count - u).astype(BF16)
        wb = w_ref[...].astype(BF16)
        dyb_v = dyb_ref[...]
        pw = jnp.dot(p, wb, preferred_element_type=F32)
        dps_ref[...] = jnp.broadcast_to(jnp.sum(dyb_v * pw, axis=0, keepdims=True), (8, 256))
        dys = (dyb_v * ps_ref[...]).astype(BF16)
        dp = lax.dot_general(dys, wb, (((1,), (1,)), ((), ())), preferred_element_type=F32)
        dw_ref[...] = lax.dot_general(p, dys, (((0,), (0,)), ((), ())), preferred_element_type=F32)
        dz_ref[:, 0:256] = (_pool_windows(dp / count, _shift_up) - dp).astype(BF16)

        ch, cb, cc = ch_ref[...], cb_ref[...], cc_ref[...]
        uc = cc * ch
        cw = cw_ref[...]
        u2, u1 = _shift_down(uc, 2), _shift_down(uc, 1)
        conv = cw[0:1, :] * u2 + cw[1:2, :] * u1 + cw[2:3, :] * uc
        dyc_v = dyc_ref[...]
        dconv = dyc_v * cb
        du = cw[0:1, :] * _shift_up(dconv, 2) + cw[1:2, :] * _shift_up(dconv, 1) + cw[2:3, :] * dconv
        dz_ref[:, 256:512] = (du * cc).astype(BF16)
        dz_ref[:, 512:768] = (dyc_v * conv).astype(BF16)
        dz_ref[:, 768:1024] = (du * ch).astype(BF16)
        dcw_ref[...] = jnp.zeros_like(dcw_ref)
        dcw_ref[0:1, :] = jnp.sum(dconv * u2, axis=0, keepdims=True)
        dcw_ref[1:2, :] = jnp.sum(dconv * u1, axis=0, keepdims=True)
        dcw_ref[2:3, :] = jnp.sum(dconv * uc, axis=0, keepdims=True)

    blk = lambda r, c: pl.BlockSpec((r, c), lambda i: (0, 0))
    return pl.pallas_call(
        body, name="poolconv_bwd", grid=(1,),
        in_specs=[blk(s, 256), blk(s, 256)] + _pc_specs(s),
        out_specs=[blk(s, 1024), blk(256, 256), blk(8, 256), blk(8, 256)],
        out_shape=[jax.ShapeDtypeStruct((s, 1024), BF16), jax.ShapeDtypeStruct((256, 256), F32),
                   jax.ShapeDtypeStruct((8, 256), F32), jax.ShapeDtypeStruct((8, 256), F32)],
        compiler_params=_params("arbitrary"),
    )(dyb, dyc, z, z, z, z, wbd, pscale, convw)


_NT = (((1,), (1,)), ((), ()))
_TN = (((0,), (0,)), ((), ()))


ATT_Q, ATT_K = 256, 256
ATT_HEADS_BWD = 8
ATT_HEADS = 8


def _att_logits(q, k, fr, q0, k0, masked):
    logits = lax.dot_general(q, k, _NT, preferred_element_type=F32) - fr
    if not masked:
        return logits
    row = q0 + lax.broadcasted_iota(jnp.int32, logits.shape, 0)
    col = k0 + lax.broadcasted_iota(jnp.int32, logits.shape, 1)
    return jnp.where(row >= col, logits, NEG_INF)


def _causal_sweep(step, qi, init):
    n_full = (qi * ATT_Q) // ATT_K
    carry = lax.fori_loop(0, n_full, lambda j, carry: step(j, carry, False), init)
    return step(n_full, carry, True)


HEAD_PAIRS = HEADS // 2


def _lane_pick(v, lane, idx):
    return jnp.sum(jnp.where(lane == idx, v, 0.0), axis=-1, keepdims=True)


def _lane_put(lane, idx, col):
    return jnp.where(lane == idx, col, 0.0)


def _split_heads(v, low):
    zero = jnp.zeros_like(v)
    return jnp.where(low, v, zero), jnp.where(low, zero, v)


def _attn_fwd(qkv, fr):
    s = qkv.shape[0]
    nk = s // ATT_K
    width = ATT_HEADS * HEAD_DIM
    groups = HEADS // ATT_HEADS

    def body(q_ref, k_ref, v_ref, fr_ref, o_ref, lse_ref):
        qi, grp = pl.program_id(0), pl.program_id(1)
        lane = lax.broadcasted_iota(jnp.int32, (ATT_Q, 128), 1)
        low = lane < HEAD_DIM
        qs = []
        for pr in range(ATT_HEADS // 2):
            qs += _split_heads(q_ref[:, 128 * pr:128 * (pr + 1)] * (HEAD_DIM ** -0.5), low)

        def step(j, carry, masked):
            k0 = pl.multiple_of(j * ATT_K, ATT_K)
            out = []
            for h in range(ATT_HEADS):
                cols = slice(128 * (h // 2), 128 * (h // 2 + 1))
                m, l, acc = carry[h]
                logits = _att_logits(qs[h], k_ref[pl.ds(k0, ATT_K), cols], fr_ref[h, pl.ds(j, 1), :], qi * ATT_Q, k0, masked)
                m_new = jnp.maximum(m, jnp.max(logits, axis=-1, keepdims=True))
                p = jnp.exp(logits - m_new)
                alpha = jnp.exp(m - m_new)
                l = alpha * l + jnp.sum(p, axis=-1, keepdims=True)
                acc = alpha * acc + jnp.dot(p.astype(BF16), v_ref[pl.ds(k0, ATT_K), cols], preferred_element_type=F32)
                out.append((m_new, l, acc))
            return tuple(out)

        one = (jnp.full((ATT_Q, 1), NEG_INF, F32), jnp.zeros((ATT_Q, 1), F32), jnp.zeros((ATT_Q, 128), F32))
        done = _causal_sweep(step, qi, (one,) * ATT_HEADS)

        @pl.when(grp == 0)
        def _():
            lse_ref[...] = jnp.zeros_like(lse_ref)

        lse = jnp.zeros((ATT_Q, 128), F32)
        for pr in range(ATT_HEADS // 2):
            (m0, l0, acc0), (m1, l1, acc1) = done[2 * pr], done[2 * pr + 1]
            o_ref[:, 128 * pr:128 * (pr + 1)] = jnp.where(low, acc0 / l0, acc1 / l1)
            head = ATT_HEADS * grp + 2 * pr
            lse = lse + _lane_put(lane, head, m0 + jnp.log(l0)) + _lane_put(lane, head + 1, m1 + jnp.log(l1))
        lse_ref[...] += lse

    return pl.pallas_call(
        body, name="attn_fwd", grid=(s // ATT_Q, groups),
        in_specs=[pl.BlockSpec((ATT_Q, width), lambda i, g: (i, g)),
                  pl.BlockSpec((s, width), lambda i, g: (0, groups + g)),
                  pl.BlockSpec((s, width), lambda i, g: (0, 2 * groups + g)),
                  pl.BlockSpec((ATT_HEADS, nk, ATT_K), lambda i, g: (g, 0, 0))],
        out_specs=[pl.BlockSpec((ATT_Q, width), lambda i, g: (i, g)), pl.BlockSpec((ATT_Q, 128), lambda i, g: (i, 0))],
        out_shape=[jax.ShapeDtypeStruct((s, A_WIDTH), F32), jax.ShapeDtypeStruct((s, 128), F32)],
        compiler_params=_params("parallel", "arbitrary"),
    )(qkv, qkv, qkv, fr)


def _attn_bwd(qkv, do, o, lse, fr):
    s = qkv.shape[0]
    nk = s // ATT_K
    scale = HEAD_DIM ** -0.5
    heads = ATT_HEADS_BWD
    width = heads * HEAD_DIM
    groups = HEADS // heads

    def body(q_ref, k_ref, v_ref, do_ref, o_ref, lse_ref, fr_ref, dq_ref, dk_ref, dv_ref, dfc_ref, dfr_ref, dk_acc, dv_acc):
        grp = pl.program_id(0)
        lane = lax.broadcasted_iota(jnp.int32, (ATT_Q, 128), 1)
        low = lane < HEAD_DIM
        low_t = lax.broadcasted_iota(jnp.int32, (128, ATT_Q), 0) < HEAD_DIM
        dk_acc[...] = jnp.zeros_like(dk_acc)
        dv_acc[...] = jnp.zeros_like(dv_acc)
        dfr_ref[...] = jnp.zeros_like(dfr_ref)

        @pl.when(grp == 0)
        def _():
            dfc_ref[...] = jnp.zeros_like(dfc_ref)

        def outer(i, carry):
            q0 = pl.multiple_of(i * ATT_Q, ATT_Q)
            rows = pl.ds(q0, ATT_Q)
            lsev = lse_ref[rows, :]
            qts, dots, qs, dos, deltas, lses = [], [], [], [], [], []
            for pr in range(heads // 2):
                pcols = slice(128 * pr, 128 * (pr + 1))
                q2, do2 = q_ref[rows, pcols] * scale, do_ref[rows, pcols]
                prod = do2 * o_ref[rows, pcols]
                deltas += [jnp.sum(jnp.where(low, prod, 0.0), axis=-1, keepdims=True),
                           jnp.sum(jnp.where(low, 0.0, prod), axis=-1, keepdims=True)]
                dob2 = do2.astype(BF16)
                qts += _split_heads(q2.astype(F32).T.astype(BF16), low_t)
                dots += _split_heads(do2.T.astype(BF16), low_t)
                qs += _split_heads(q2, low)
                dos += _split_heads(dob2, low)
                lses += [_lane_pick(lsev, lane, heads * grp + 2 * pr), _lane_pick(lsev, lane, heads * grp + 2 * pr + 1)]

            def inner(j, carry, masked):
                k0 = pl.multiple_of(j * ATT_K, ATT_K)
                krows = pl.ds(k0, ATT_K)
                out, dkt, dvt = [], [], []
                for h in range(heads):
                    pcols = slice(128 * (h // 2), 128 * (h // 2 + 1))
                    dq, dfc = carry[h]
                    k2 = k_ref[krows, pcols]
                    p = jnp.exp(_att_logits(qs[h], k2, fr_ref[h, pl.ds(j, 1), :], q0, k0, masked) - lses[h])
                    dp = lax.dot_general(dos[h], v_ref[krows, pcols], _NT, preferred_element_type=F32)
                    ds = p * (dp - deltas[h])
                    dsb = ds.astype(BF16)
                    dkt.append(jnp.dot(qts[h], dsb, preferred_element_type=F32))
                    dvt.append(jnp.dot(dots[h], p.astype(BF16), preferred_element_type=F32))
                    dfr_ref[h, pl.ds(j, 1), :] -= jnp.sum(ds, axis=0, keepdims=True)
                    out.append((dq + jnp.dot(dsb, k2, preferred_element_type=F32), dfc + (ds[:, :128] + ds[:, 128:])))
                for pr in range(heads // 2):
                    prows = slice(128 * pr, 128 * (pr + 1))
                    dk_acc[j, prows, :] += dkt[2 * pr] + dkt[2 * pr + 1]
                    dv_acc[j, prows, :] += dvt[2 * pr] + dvt[2 * pr + 1]
                return tuple(out)

            one = (jnp.zeros((ATT_Q, 128), F32), jnp.zeros((ATT_Q, 128), F32))
            done = _causal_sweep(inner, i, (one,) * heads)
            dfc = jnp.zeros((ATT_Q, 128), F32)
            for pr in range(heads // 2):
                (dq0, dfc0), (dq1, dfc1) = done[2 * pr], done[2 * pr + 1]
                dq_ref[rows, 128 * pr:128 * (pr + 1)] = (jnp.where(low, dq0, dq1) * scale).astype(BF16)
                head = heads * grp + 2 * pr
                dfc = (dfc + _lane_put(lane, head, jnp.sum(dfc0, axis=-1, keepdims=True))
                       + _lane_put(lane, head + 1, jnp.sum(dfc1, axis=-1, keepdims=True)))
            dfc_ref[rows, :] += dfc
            return carry

        lax.fori_loop(0, s // ATT_Q, outer, 0)
        for j in range(nk):
            for pr in range(heads // 2):
                prows, pcols = slice(128 * pr, 128 * (pr + 1)), slice(128 * pr, 128 * (pr + 1))
                dk_ref[ATT_K * j:ATT_K * (j + 1), pcols] = dk_acc[j, prows, :].T.astype(BF16)
                dv_ref[ATT_K * j:ATT_K * (j + 1), pcols] = dv_acc[j, prows, :].T.astype(BF16)

    part = lambda first: pl.BlockSpec((s, width), lambda g, first=first: (0, first + g))
    whole = pl.BlockSpec((s, 128), lambda g: (0, 0))
    rowv = pl.BlockSpec((heads, nk, ATT_K), lambda g: (g, 0, 0))
    return pl.pallas_call(
        body, name="attn_bwd", grid=(groups,),
        in_specs=[part(0), part(groups), part(2 * groups), part(0), part(0), whole, rowv],
        out_specs=[part(0), part(0), part(0), whole, rowv],
        out_shape=[jax.ShapeDtypeStruct((s, A_WIDTH), BF16)] * 3 + [jax.ShapeDtypeStruct((s, 128), F32), jax.ShapeDtypeStruct((HEADS, nk, ATT_K), F32)],
        scratch_shapes=[pltpu.VMEM((nk, width, ATT_K), F32), pltpu.VMEM((nk, width, ATT_K), F32)],
        compiler_params=_params("arbitrary"),
    )(qkv, qkv, qkv, do, o, lse, fr)


def _ada_fwd(c_all, w_ada, b_loc):
    depth, _, n = w_ada.shape
    tn = 512

    def body(c_ref, w_ref, b_ref, o_ref, sc_ref):
        cv = c_ref[...]
        sc = cv * jax.nn.sigmoid(cv)
        sc_ref[...] = sc
        o_ref[0] = jnp.dot(sc.astype(BF16), w_ref[0].astype(BF16), preferred_element_type=F32) + b_ref[0]

    return pl.pallas_call(
        body, name="ada_fwd", grid=(depth, n // tn),
        in_specs=[pl.BlockSpec((N_DEV, D), lambda l, j: (0, 0)), pl.BlockSpec((1, D, tn), lambda l, j: (l, 0, j)),
                  pl.BlockSpec((1, 1, tn), lambda l, j: (l, 0, j))],
        out_specs=[pl.BlockSpec((1, N_DEV, tn), lambda l, j: (l, 0, j)), pl.BlockSpec((N_DEV, D), lambda l, j: (0, 0))],
        out_shape=[jax.ShapeDtypeStruct((depth, N_DEV, n), F32), jax.ShapeDtypeStruct((N_DEV, D), F32)],
        compiler_params=_params("arbitrary", "arbitrary"),
    )(c_all, w_ada, b_loc)


def _sum_devices(gathered):
    n = gathered.shape[1]
    tn = _pick(n, (1408, 1024, 640, 512, 128))

    def body(g_ref, o_ref):
        acc = g_ref[0:8, :]
        for dev in range(1, N_DEV):
            acc = acc + g_ref[8 * dev:8 * dev + 8, :]
        o_ref[...] = acc

    return pl.pallas_call(
        body, name="sum_devices", grid=(n // tn,),
        in_specs=[pl.BlockSpec((8 * N_DEV, tn), lambda j: (0, j))], out_specs=pl.BlockSpec((8, tn), lambda j: (0, j)),
        out_shape=jax.ShapeDtypeStruct((8, n), F32), compiler_params=_params("parallel"),
    )(gathered)


def _place():
    x, y, c = lax.axis_index("x"), lax.axis_index("y"), lax.axis_index("c")
    chips = [(1 - x, y), (x, 1 - y), (1 - x, 1 - y)]
    return x, y, c, chips


def _allgather8(block, name):
    m_per, n = block.shape

    def body(x_ref, out_ref, send_sems, recv_sems, local_sem):
        x, y, c, chips = _place()
        me, sibling = (x, y, c), (x, y, 1 - c)

        def rows(px, py, pc):
            return out_ref.at[pl.ds((4 * px + 2 * py + pc) * m_per, m_per), :]

        def copy(k, blk, to, src=None):
            return pltpu.make_async_remote_copy(
                src_ref=rows(*blk) if src is None else src, dst_ref=rows(*blk),
                send_sem=send_sems.at[k], recv_sem=recv_sems.at[k], device_id=to, device_id_type=MESH)

        mine = pltpu.make_async_copy(x_ref, rows(*me), local_sem)
        mine.start()
        first = [copy(0, me, sibling, src=x_ref)]
        first += [copy(1 + j, me, (*chip, c), src=x_ref) for j, chip in enumerate(chips)]
        for cp in first:
            cp.start()
        passed = [copy(4 + j, (*chip, c), sibling) for j, chip in enumerate(chips)]
        for j, chip in enumerate(chips):
            copy(1 + j, (*chip, c), me).wait_recv()
            passed[j].start()
        copy(0, sibling, me).wait_recv()
        for j, chip in enumerate(chips):
            copy(4 + j, (*chip, 1 - c), me).wait_recv()
        for cp in first + passed:
            cp.wait_send()
        mine.wait()

    return pl.pallas_call(
        body, name=name, out_shape=jax.ShapeDtypeStruct((N_DEV * m_per, n), block.dtype),
        in_specs=[pl.BlockSpec(memory_space=pltpu.VMEM)], out_specs=pl.BlockSpec(memory_space=pltpu.VMEM),
        scratch_shapes=[pltpu.SemaphoreType.DMA((7,)), pltpu.SemaphoreType.DMA((7,)), pltpu.SemaphoreType.DMA],
        compiler_params=pltpu.CompilerParams(vmem_limit_bytes=V7X_VMEM_LIMIT),
    )(block)


_SEM = pl.BlockSpec(memory_space=pltpu.SEMAPHORE)
_DATAFLOW = pltpu.SideEffectType.DATAFLOW_SIDE_EFFECTING


def _plan_copies(plan, refs, send_sems, recv_sems):
    return [pltpu.make_async_remote_copy(src_ref=src, dst_ref=dst, send_sem=send_sems.at[i], recv_sem=recv_sems.at[i],
                                         device_id=to, device_id_type=MESH) for i, (src, dst, to) in enumerate(plan(refs))]


class _Token(NamedTuple):
    after: jax.Array
    tie: jax.Array


def _after_operand(after):
    return after.after if isinstance(after, _Token) else after


def _copies_start(bufs, plan, n_copies, after, name):
    nb = len(bufs)

    def body(*refs):
        for cp in _plan_copies(plan, refs[:nb], refs[nb + 1], refs[nb + 2]):
            cp.start()
        for token in refs[-2:]:
            token[...] = jnp.zeros_like(token)

    sem = pltpu.SemaphoreType.DMA((n_copies,))
    vmem = pl.BlockSpec(memory_space=pltpu.VMEM)
    outs = pl.pallas_call(
        body, name=name,
        out_shape=(sem, sem, *[pltpu.HBM(b.shape, b.dtype) for b in bufs], jax.ShapeDtypeStruct((8, 128), F32),
                   jax.ShapeDtypeStruct((1, 1), F32)),
        in_specs=[_HBM] * nb + [pl.BlockSpec(memory_space=pl.ANY)],
        out_specs=(_SEM, _SEM, *[_HBM] * nb, vmem, vmem),
        input_output_aliases={i: 2 + i for i in range(nb)},
        compiler_params=pltpu.CompilerParams(has_side_effects=_DATAFLOW),
    )(*[pltpu.with_memory_space_constraint(b, pltpu.HBM) for b in bufs], _after_operand(after))
    return outs[0], outs[1], list(outs[2:2 + nb]), _Token(outs[-2], outs[-1])


def _copies_wait(started, plan, after, name):
    send_sems, recv_sems, bufs, _ = started
    nb = len(bufs)

    def body(*refs):
        for cp in _plan_copies(plan, refs[:nb], refs[nb], refs[nb + 1]):
            cp.wait_send()
            cp.wait_recv()

    return list(pl.pallas_call(
        body, name=name, out_shape=tuple(pltpu.HBM(b.shape, b.dtype) for b in bufs),
        in_specs=[_HBM] * nb + [_SEM, _SEM, pl.BlockSpec(memory_space=pl.ANY)], out_specs=tuple([_HBM] * nb),
        input_output_aliases={i: i for i in range(nb)},
        compiler_params=pltpu.CompilerParams(has_side_effects=_DATAFLOW),
    )(*bufs, send_sems, recv_sems, _after_operand(after)))


def _half_rows(ref, axis, c):
    half = ref.shape[axis] // 2
    return pl.ds(c * half, half)


def _plan_gather_ici(refs):
    n = len(refs) // 2
    x, y, c, chips = _place()
    out = []
    for a in range(n):
        rows = _half_rows(refs[a], 0, c)
        out += [(refs[a].at[rows], refs[n + a].at[2 * x + y, rows], (*chip, c)) for chip in chips]
        out.append((refs[a], refs[n + a].at[2 * x + y], (x, y, 1 - c)))
    return out


def _plan_gather_d2d(refs):
    x, y, c, chips = _place()
    out = []
    for ref in refs:
        rows = _half_rows(ref, 1, c)
        for px, py in chips:
            landed = ref.at[2 * px + py, rows]
            out.append((landed, landed, (x, y, 1 - c)))
    return out


def _plan_rs_sibling(refs):
    n = len(refs) // 2
    x, y, c, _ = _place()
    return [(refs[a].at[pl.ds(0, N_CHIPS), _half_rows(refs[a], 1, 1 - c)], refs[n + a], (x, y, 1 - c)) for a in range(n)]


def _plan_rs_chips(refs):
    n = len(refs) // 2
    x, y, c, chips = _place()
    return [(refs[a].at[2 * px + py], refs[n + a].at[k], (px, py, c)) for a in range(n) for k, (px, py) in enumerate(chips)]


def _plan_rs_share(layer):
    def plan(refs):
        x, y, c, _ = _place()
        return [(ref.at[layer, _half_rows(ref, 1, c)], ref.at[layer, _half_rows(ref, 1, c)], (x, y, 1 - c)) for ref in refs]
    return plan


def _chip_sum(g, other, sel, name):
    _, half, cdim = other.shape
    tr = _pick(half, (512, 256, 128, 64))
    per = half // tr

    def body(sel_ref, g_ref, t_ref, wire_ref, own_ref):
        total = g_ref[0] + t_ref[0]
        wire_ref[0] = total.astype(BF16)

        @pl.when(pl.program_id(1) == sel_ref[1])
        def _():
            own_ref[...] = total

    blk = pl.BlockSpec((1, tr, cdim), lambda i, p, sel_ref: (p, i, 0))
    return pl.pallas_call(
        body, name=name,
        grid_spec=pltpu.PrefetchScalarGridSpec(
            num_scalar_prefetch=1, grid=(per, N_CHIPS),
            in_specs=[pl.BlockSpec((1, tr, cdim), lambda i, p, sel_ref: (p, sel_ref[0] * per + i, 0)), blk],
            out_specs=[blk, pl.BlockSpec((tr, cdim), lambda i, p, sel_ref: (i, 0))]),
        out_shape=[jax.ShapeDtypeStruct(other.shape, BF16), jax.ShapeDtypeStruct((half, cdim), F32)],
        compiler_params=_params("parallel", "arbitrary"),
    )(sel, g, other)


def _final_sum(own, recv, sel, layer, into, name):
    half, cdim = own.shape
    tr = _pick(half, (512, 256, 128, 64))
    per = half // tr

    def body(sel_ref, own_ref, r0_ref, r1_ref, r2_ref, *rest):
        rest[-1][...] = ((own_ref[...] + r0_ref[0].astype(F32)) + r1_ref[0].astype(F32)) + r2_ref[0].astype(F32)

    part = lambda k: pl.BlockSpec((1, tr, cdim), lambda i, sel_ref, k=k: (k, i, 0))
    prior = [] if into is None else [into]
    return pl.pallas_call(
        body, name=name,
        grid_spec=pltpu.PrefetchScalarGridSpec(
            num_scalar_prefetch=1, grid=(per,),
            in_specs=[pl.BlockSpec((tr, cdim), lambda i, sel_ref: (i, 0)), part(0), part(1), part(2)]
            + [pl.BlockSpec(memory_space=pl.ANY)] * len(prior),
            out_specs=pl.BlockSpec((None, tr, cdim), lambda i, sel_ref: (layer, sel_ref[0] * per + i, 0))),
        out_shape=jax.ShapeDtypeStruct((DEPTH, 2 * half, cdim), F32),
        input_output_aliases={5: 0} if prior else {}, compiler_params=_params("parallel"),
    )(sel, own, recv, recv, recv, *prior)


def _row(v):
    return v.reshape(1, -1)


_BR_A, _BR_B, _BR_C = (0, A_WIDTH), (A_WIDTH, POOL_WIDTH), (A_WIDTH + POOL_WIDTH, CONV_WIDTH)


def _tie(v, token):
    return v if token is None else v + token.tie


def _no_hook(point, after, ready=None):
    return None


def _layer_fwd(x, w, mod, hook=_no_hook):
    s = x.shape[0]
    mod3 = mod.reshape(6, 1, D)
    h = _modnorm_fwd(x, _row(w["g_mix_pre"]), (mod3, 0), (mod3, 1), "mix_pre_fwd")
    hook("pre", h)
    z = _mm(h, w["w_all"], name="mm_in")
    qkv = z[:, Z_QKV:Z_PC].astype(BF16)
    fl = z[:, Z_FL:Z_COLS]
    cum = _cumf_fwd(fl, w["b_f_pad"])
    fr = cum[:, :HEADS].T.reshape(HEADS, s // ATT_K, ATT_K)
    br_a, lse = _attn_fwd(qkv, fr)
    br_b, br_c = _poolconv_fwd(z, w["w_pool_bd"], _tie(_row(w["pool_scale"]), hook("attn", lse)), w["conv_w"])
    hook("pool", br_b)
    wbr = w["w_branch"]
    pa = _mm(br_a, wbr, b_rows=_BR_A, name="mm_br_a")
    pb = _mm(br_b, wbr, b_rows=_BR_B, name="mm_br_b")
    pc = _mm(br_c, wbr, b_rows=_BR_C, name="mm_br_c")
    merged = _merge_fwd(z, pa, pb, pc)
    y = _mm(merged, w["w_out"], name="mm_out")
    x1, h2 = _post_pre_fwd(x, y, _row(w["g_mix_post"]), (mod3, 2), _row(w["g_ff_pre"]), (mod3, 3), (mod3, 4), "mix_post_ff_pre_fwd")
    a, r = _mm(h2, w["w_ff1"], b_split=N_CHIPS, epilogue=_relu2_fwd, out_dtype=(F32, BF16), name="mm_ff1")
    y2 = _mm(r, w["w_ff2"], name="mm_ff2")
    x2 = _post_fwd(x1, y2, _tie(_row(w["g_ff_post"]), hook("ff_post", y2)), (mod3, 5), "ff_post_fwd")
    hook("end", x2)
    saved = dict(x=x, h=h, z=z, qkv=qkv, fl=fl, fr=fr, lse=lse, br_a=br_a, br_b=br_b, br_c=br_c, pa=pa, pb=pb, pc=pc,
                 merged=merged, y=y, x1=x1, h2=h2, a=a, r=r, y2=y2)
    return x2, saved


def _layer_bwd(dx2, sv, w, mod, hook=_no_hook):
    s = dx2.shape[0]
    mod3 = mod.reshape(6, 1, D)
    dy2, sum_ff_post = _post_bwd(dx2, sv["y2"], _row(w["g_ff_post"]), (mod3, 5), "ff_post_bwd")
    (da,) = _mm(dy2, w["w_ff2"], tb=True, epilogue=_relu2_bwd, extras=(sv["a"],), out_dtype=(BF16,), name="mm_ff2_dx")
    d_w_ff2 = _mm(sv["r"], dy2, ta=True, name="mm_ff2_dw")
    dh2 = _mm(da, w["w_ff1"], tb=True, b_split=N_CHIPS, name="mm_ff1_dx")
    d_w_ff1 = _mm(sv["h2"], da, ta=True, out_split=N_CHIPS, name="mm_ff1_dw")
    g_ff_pre = _tie(_row(w["g_ff_pre"]), hook("ff_pre", dh2, dict(w_ff1=d_w_ff1, w_ff2=d_w_ff2)))
    dx1, dy, sum_mid = _pre_post_bwd(dh2, sv["x1"], dx2, g_ff_pre, (mod3, 4), sv["y"], _row(w["g_mix_post"]), (mod3, 2), "ff_pre_mix_post_bwd")
    sum_ff_pre, sum_mix_post = sum_mid, sum_mid[3:]
    dmerged = _mm(dy, w["w_out"], tb=True, name="mm_out_dx")
    d_w_out = _mm(sv["merged"], dy, ta=True, name="mm_out_dw")
    dz, dpa, dpb, dpc = _merge_bwd(dmerged, sv["z"], sv["pa"], sv["pb"], sv["pc"])
    wbr = w["w_branch"]
    dbr_a = _mm(dpa, wbr, tb=True, b_rows=_BR_A, name="mm_br_a_dx")
    dbr_b = _mm(dpb, wbr, tb=True, b_rows=_BR_B, name="mm_br_b_dx")
    dbr_c = _mm(dpc, wbr, tb=True, b_rows=_BR_C, name="mm_br_c_dx")
    d_w_branch = jnp.concatenate([_mm(sv["br_a"], dpa, ta=True, name="mm_br_a_dw"), _mm(sv["br_b"], dpb, ta=True, name="mm_br_b_dw"),
                                  _mm(sv["br_c"], dpc, ta=True, name="mm_br_c_dw")], axis=0)

    dq, dk, dv, dfc, dfr = _attn_bwd(sv["qkv"], dbr_a, sv["br_a"], sv["lse"], sv["fr"])
    dcum = dfc + jnp.pad(dfr.reshape(HEADS, s).T, ((0, 0), (0, 128 - HEADS)))
    dfl, sum_bf = _cumf_bwd(dcum, sv["fl"], _tie(w["b_f_pad"], hook("cumf", dfc)))
    dpc_z, d_wbd, sum_ps, sum_cw = _poolconv_bwd(dbr_b, dbr_c, sv["z"], w["w_pool_bd"], _row(w["pool_scale"]), w["conv_w"])
    for at, part in ((Z_QKV, dq), (Z_QKV + A_WIDTH, dk), (Z_QKV + 2 * A_WIDTH, dv), (Z_PC, dpc_z), (Z_FL, dfl)):
        dz = lax.dynamic_update_slice(dz, part, (0, at))
    dh = _mm(dz, w["w_all"], tb=True, name="mm_in_dx")
    d_w_all = _mm(sv["h"], dz, ta=True, name="mm_in_dw")
    hook("mix_pre", dh)
    dx, sum_mix_pre = _modnorm_bwd(dh, sv["x"], dx1, _row(w["g_mix_pre"]), (mod3, 1), "mix_pre_bwd")

    dmod = jnp.stack([sum_mix_pre[0], sum_mix_pre[1], sum_mix_post[0], sum_ff_pre[0], sum_ff_pre[1], sum_ff_post[0]])
    d_w_in = _w_in_shards(d_w_all)
    d_w_pool = jnp.stack([d_wbd[64 * g:64 * g + 64, 64 * g:64 * g + 64] for g in range(4)])
    big = dict(w_in=d_w_in, w_branch=d_w_branch, w_out=d_w_out, w_ff1=d_w_ff1, w_ff2=d_w_ff2)
    small = dict(g_mix_pre=sum_mix_pre[2], g_mix_post=sum_mix_post[1], g_ff_pre=sum_ff_pre[2], g_ff_post=sum_ff_post[1],
                 b_f=sum_bf[0, :HEADS], w_pool=d_w_pool, pool_scale=sum_ps[0], conv_w=sum_cw[0:3])
    return dx, dmod, big, small


_QKV_END, _FL_END, _PC_END = 3 * A_WIDTH, 3 * A_WIDTH + HEADS, 3 * A_WIDTH + HEADS + POOL_WIDTH + 3 * CONV_WIDTH
_W_IN_GROUPS = ((_PC_END, IN_COLS, Z_GL), (0, _QKV_END, Z_QKV), (_FL_END, _PC_END, Z_PC), (_QKV_END, _FL_END, Z_FL))
_SHARD_COLS = IN_COLS // N_CHIPS


def _w_all_from_shards(blocks):
    pieces = []
    for lo, hi, _ in _W_IN_GROUPS:
        for p in range(N_CHIPS):
            a, b = max(lo, p * _SHARD_COLS), min(hi, (p + 1) * _SHARD_COLS)
            if a < b:
                pieces.append(blocks[p][:, a - p * _SHARD_COLS:b - p * _SHARD_COLS])
    pieces.append(jnp.zeros((D, Z_COLS - IN_COLS), blocks.dtype))
    return jnp.concatenate(pieces, axis=1)


def _w_in_shards(d_w_all):
    blocks = []
    for p in range(N_CHIPS):
        pieces = []
        for lo, hi, at in sorted(_W_IN_GROUPS):
            a, b = max(lo, p * _SHARD_COLS), min(hi, (p + 1) * _SHARD_COLS)
            if a < b:
                pieces.append(d_w_all[:, at + a - lo:at + b - lo])
        blocks.append(jnp.concatenate(pieces, axis=1))
    return jnp.stack(blocks)


def _full_layer_weights(w_in_blocks, w_branch, w_out, w_ff1, w_ff2, g_mix_pre, g_mix_post, g_ff_pre, g_ff_post, b_f, w_pool, pool_scale, conv_w):
    w_all = None if w_in_blocks is None else _w_all_from_shards(w_in_blocks)
    wbd = (w_pool[:, :, None, :] * jnp.eye(4, dtype=F32)[:, None, :, None]).reshape(POOL_WIDTH, POOL_WIDTH)
    return dict(w_all=w_all, w_branch=w_branch, w_out=w_out, w_ff1=w_ff1, w_ff2=w_ff2, g_mix_pre=g_mix_pre, g_mix_post=g_mix_post,
                g_ff_pre=g_ff_pre, g_ff_post=g_ff_post, b_f_pad=jnp.pad(b_f, (0, 128 - HEADS)).reshape(1, 128), w_pool_bd=wbd,
                pool_scale=pool_scale, conv_w=conv_w)


class _NoComm:
    def layer_weights(self, l):
        raise NotImplementedError

    def fwd_hook(self, l):
        return _no_hook

    def bwd_hook(self, l):
        return _no_hook

    def grads_ready(self, l, big):
        return None


class _Layers(_NoComm):
    def __init__(self, layers):
        self.layers = layers

    def layer_weights(self, l):
        return self.layers[l]


def _local_step(x, target, mods, comm):
    saved, weights = [], []
    act = x
    for l in range(DEPTH):
        weights.append(comm.layer_weights(l))
        act, sv = _layer_fwd(act, weights[l], mods[l], comm.fwd_hook(l))
        saved.append(sv)
    dact, sq = _loss_head(act, target)
    loss = sq[0, 0] * (0.5 / D)
    dmods, bigs, smalls = [None] * DEPTH, [None] * DEPTH, [None] * DEPTH
    token = None
    for l in reversed(range(DEPTH)):
        dact, dmods[l], bigs[l], smalls[l] = _layer_bwd(dact, saved[l], weights[l], _tie(mods[l], token), comm.bwd_hook(l))
        token = comm.grads_ready(l, bigs[l])
    return loss, dact, jnp.stack(dmods), bigs, smalls


_BIG = ("w_in", "w_branch", "w_out", "w_ff1", "w_ff2")


class _GatherJob:
    def __init__(self, tag, shards, after):
        self.tag, self.n = tag, len(shards)
        lands = [lax.empty((N_CHIPS,) + s.shape, s.dtype) for s in shards]
        self.state = _copies_start(list(shards) + lands, _plan_gather_ici, 4 * self.n, after, "gather_ici_start_" + tag)
        self.token = self.state[3]

    def pass_on(self, after):
        bufs = _copies_wait(self.state, _plan_gather_ici, after, "gather_ici_wait_" + self.tag)
        self.state = _copies_start(bufs[self.n:], _plan_gather_d2d, 3 * self.n, bufs[0], "gather_d2d_start_" + self.tag)
        self.token = self.state[3]
        return self.token

    def done(self, after):
        return _copies_wait(self.state, _plan_gather_d2d, after, "gather_d2d_wait_" + self.tag)


class _ReduceJob:
    def __init__(self, tag, names, grads, sel, after, layer, into=None):
        self.tag, self.names, self.n, self.sel, self.layer, self.into = tag, names, len(names), sel, layer, into or {}
        lands = [lax.empty((N_CHIPS, g.shape[1] // 2, g.shape[2]), F32) for g in grads]
        self.state = _copies_start(list(grads) + lands, _plan_rs_sibling, self.n, after, "rs_sibling_start_" + tag)
        self.token = self.state[3]

    def chip_sums(self, after):
        bufs = _copies_wait(self.state, _plan_rs_sibling, after, "rs_sibling_wait_" + self.tag)
        wires, self.owns = zip(*[_chip_sum(bufs[i], bufs[self.n + i], self.sel, "rs_chip_sum_" + name) for i, name in enumerate(self.names)])
        lands = [lax.empty((3,) + w.shape[1:], BF16) for w in wires]
        self.state = _copies_start(list(wires) + lands, _plan_rs_chips, 3 * self.n, self.owns[0], "rs_chips_start_" + self.tag)
        self.token = self.state[3]
        return self.token

    def final_sums(self, after):
        bufs = _copies_wait(self.state, _plan_rs_chips, after, "rs_chips_wait_" + self.tag)
        sums = [_final_sum(self.owns[i], bufs[self.n + i], self.sel, self.layer, self.into.get(name), "rs_final_" + name)
                for i, name in enumerate(self.names)]
        self.state = _copies_start(sums, _plan_rs_share(self.layer), self.n, sums[0], "rs_share_start_" + self.tag)
        self.token = self.state[3]
        return self.token

    def done(self, after):
        return dict(zip(self.names, _copies_wait(self.state, _plan_rs_share(self.layer), after, "rs_share_wait_" + self.tag)))


def _chip_blocks(g):
    return g if g.ndim == 3 else g.reshape(N_CHIPS, -1, g.shape[1])


class _StepComm(_NoComm):
    def __init__(self, big_weights, sel, after):
        self.sel = sel
        self.small, self.grads, self.jobs = None, {}, {}
        self.jobs["in0"] = _GatherJob("in0", [big_weights[0][0].astype(BF16)], after)
        later = lax.optimization_barrier((tuple(big_weights), self.jobs["in0"].token.after))[0]
        self.jobs["rest0"] = _GatherJob("rest0", [w[0].astype(BF16) for w in later[1:]], self.jobs["in0"].token)
        self.jobs["all1"] = _GatherJob("all1", [w[1].astype(BF16) for w in later], self.jobs["rest0"].token)

    def layer_weights(self, l):
        if l == 0:
            self.weights0 = _full_layer_weights(None, None, None, None, None, *self.small[0])
            return self.weights0
        g_in, g_br, g_out, g_f1, g_f2 = self.landed1
        return _full_layer_weights(g_in, g_br.reshape(D, D), g_out.reshape(D, D), g_f1, g_f2.reshape(D_FF, D), *self.small[1])

    def fwd_hook(self, l):
        if l != 0:
            return _no_hook

        def hook(point, after, ready=None):
            if point == "pre":
                job = self.jobs["in0"]
                started = after[:8, :128].astype(F32) + self.jobs["all1"].token.after
                self.weights0["w_all"] = _w_all_from_shards(job.done(job.pass_on(started))[0])
            if point == "attn":
                return self.jobs["rest0"].pass_on(after)
            if point == "ff_post":
                return self.jobs["all1"].pass_on(after)
            if point == "pool":
                g_br, g_out, g_f1, g_f2 = self.jobs["rest0"].done(after)
                self.weights0.update(w_branch=g_br.reshape(D, D), w_out=g_out.reshape(D, D), w_ff1=g_f1, w_ff2=g_f2.reshape(D_FF, D))
            if point == "end":
                self.landed1 = self.jobs["all1"].done(after)
            return None
        return hook

    def bwd_hook(self, l):
        if l != 0:
            return _no_hook

        def hook(point, after, ready=None):
            jobs = self.jobs
            if point == "ff_pre":
                token = jobs["rs1"].chip_sums(after)
                jobs["rs0_ff"] = _ReduceJob("0_ff", ("w_ff1", "w_ff2"), [_chip_blocks(ready[n]) for n in ("w_ff1", "w_ff2")], self.sel, token, 0)
                return jobs["rs0_ff"].token
            if point == "cumf":
                return jobs["rs0_ff"].chip_sums(jobs["rs1"].final_sums(after))
            self.layer1 = jobs["rs1"].done(after)
            jobs["rs0_ff"].into = self.layer1
            return None
        return hook

    def grads_ready(self, l, big):
        if l == 1:
            self.jobs["rs1"] = _ReduceJob("1", _BIG, [_chip_blocks(big[n]) for n in _BIG], self.sel, self.sel, 1)
            return self.jobs["rs1"].token
        names = ("w_in", "w_branch", "w_out")
        self.jobs["rs0_mix"] = _ReduceJob("0_mix", names, [_chip_blocks(big[n]) for n in names], self.sel, self.sel, 0, self.layer1)
        return self.jobs["rs0_mix"].token

    def finish_sums(self, after):
        jobs = self.jobs
        token = jobs["rs0_mix"].chip_sums(after)
        return jobs["rs0_ff"].final_sums(token)

    def finish_ff(self, after):
        self.grads.update(self.jobs["rs0_ff"].done(after))

    def finish_mix(self, after):
        job = self.jobs["rs0_mix"]
        self.grads.update(job.done(job.final_sums(after)))


_SMALL = ("g_mix_pre", "g_mix_post", "g_ff_pre", "g_ff_post", "b_f", "w_pool", "pool_scale", "conv_w")


def _w_in_view(t):
    return t.reshape(DEPTH, D // 128, 128, _SHARD_COLS).transpose(3, 1, 0, 2).reshape(_SHARD_COLS * (D // 128) * DEPTH, 128)


def _w_in_unview(t):
    return t.reshape(_SHARD_COLS, D // 128, DEPTH, 128).transpose(2, 1, 3, 0).reshape(DEPTH, D, _SHARD_COLS)


def _pack(parts, rows=8):
    flat = jnp.concatenate([p.reshape(-1) for p in parts])
    width = -(-flat.shape[0] // (rows * 128)) * 128
    return jnp.pad(flat, (0, rows * width - flat.shape[0])).reshape(rows, width)


def _unpack(packed, like):
    flat = packed.reshape(-1)
    out, at = [], 0
    for ref in like:
        out.append(flat[at:at + ref.size].reshape(ref.shape))
        at += ref.size
    return out


def kernel(x, c, w_ada, b_ada, g_mix_pre, g_mix_post, g_ff_pre, g_ff_post, w_in, b_f, w_pool, pool_scale, conv_w, w_branch, w_out, w_ff1, w_ff2, loss_target, m_w_ada, m_b_ada, m_g_mix_pre, m_g_mix_post, m_g_ff_pre, m_g_ff_post, m_w_in, m_b_f, m_w_pool, m_pool_scale, m_conv_w, m_w_branch, m_w_out, m_w_ff1, m_w_ff2, v_w_ada, v_b_ada, v_g_mix_pre, v_g_mix_post, v_g_ff_pre, v_g_ff_post, v_w_in, v_b_f, v_w_pool, v_pool_scale, v_conv_w, v_w_branch, v_w_out, v_w_ff1, v_w_ff2):
    xi, yi, ci = lax.axis_index("x"), lax.axis_index("y"), lax.axis_index("c")
    chip = 2 * xi + yi
    dev = 2 * chip + ci
    n_ada = w_ada.shape[2]

    first = jnp.zeros((8, D + 384), F32).at[0, :D].set(c[0]).at[0, D:].set(conv_w.reshape(-1))
    got = _allgather8(first, "gather_cond").reshape(N_DEV, 8, D + 384)[:, 0]
    c_all = got[:, :D]
    conv_full = got[0::2, D:].reshape(N_CHIPS, DEPTH, 3, CONV_WIDTH // N_CHIPS).transpose(1, 2, 0, 3).reshape(DEPTH, 3, CONV_WIDTH)

    b_loc = lax.dynamic_slice_in_dim(b_ada, chip * n_ada, n_ada, axis=1).reshape(DEPTH, 1, n_ada)
    mod_cols, silu_c = _ada_fwd(c_all, w_ada, b_loc)
    got = _allgather8(mod_cols.reshape(DEPTH * N_DEV, n_ada), "gather_mod").reshape(N_DEV, DEPTH, N_DEV, n_ada)[0::2]
    mod_all = got.transpose(1, 2, 0, 3).reshape(DEPTH, N_DEV, 6, D)
    mods = lax.dynamic_index_in_dim(mod_all, dev, axis=1, keepdims=False)

    comm = _StepComm((w_in, w_branch, w_out, w_ff1, w_ff2), jnp.stack([ci, chip]).astype(jnp.int32), mods)
    comm.small = [(g_mix_pre[l], g_mix_post[l], g_ff_pre[l], g_ff_post[l], b_f[l], w_pool[l], pool_scale[l], conv_full[l]) for l in range(DEPTH)]
    loss_part, grad_x, dmods, bigs, smalls = _local_step(x[0], loss_target[0], mods, comm)

    small_parts = [smalls[l][name] for name in _SMALL for l in range(DEPTH)] + [loss_part.reshape(1)]
    packed = _tie(_pack([dmods] + small_parts), comm.jobs["rs0_mix"].token)
    gathered = _allgather8(packed, "gather_small")
    dmod_all = gathered.reshape(N_DEV, -1)[:, :dmods.size].reshape(N_DEV, DEPTH, 6 * D)
    summed = _unpack(_sum_devices(gathered), [dmods] + small_parts)
    grad_b_ada = summed[0].reshape(DEPTH, 6 * D)
    loss = summed[-1][0]
    small_grads = {name: jnp.stack(summed[1 + 2 * i:3 + 2 * i]) for i, name in enumerate(_SMALL)}
    small_grads["conv_w"] = lax.dynamic_slice_in_dim(small_grads["conv_w"], chip * (CONV_WIDTH // N_CHIPS), CONV_WIDTH // N_CHIPS, axis=2)

    dmod_loc = lax.dynamic_slice_in_dim(dmod_all.transpose(1, 0, 2), chip * n_ada, n_ada, axis=2)
    tail_token = comm.finish_sums(grad_b_ada)
    silu_pad = _tie(jnp.pad(silu_c, ((0, 128 - N_DEV), (0, 0))), tail_token)
    dmod_pad = jnp.pad(dmod_loc.transpose(1, 0, 2).reshape(N_DEV, DEPTH * n_ada), ((0, 128 - N_DEV), (0, 0)))
    grad_w_ada = _mm(silu_pad, dmod_pad, ta=True, out_split=DEPTH, name="mm_ada_dw")

    grads = dict(w_ada=grad_w_ada, b_ada=grad_b_ada, **small_grads)
    weights = dict(w_ada=w_ada, b_ada=b_ada, g_mix_pre=g_mix_pre, g_mix_post=g_mix_post, g_ff_pre=g_ff_pre, g_ff_post=g_ff_post, w_in=w_in,
                   b_f=b_f, w_pool=w_pool, pool_scale=pool_scale, conv_w=conv_w, w_branch=w_branch, w_out=w_out, w_ff1=w_ff1, w_ff2=w_ff2)
    m_in = dict(w_ada=m_w_ada, b_ada=m_b_ada, g_mix_pre=m_g_mix_pre, g_mix_post=m_g_mix_post, g_ff_pre=m_g_ff_pre, g_ff_post=m_g_ff_post,
                w_in=m_w_in, b_f=m_b_f, w_pool=m_w_pool, pool_scale=m_pool_scale, conv_w=m_conv_w, w_branch=m_w_branch, w_out=m_w_out,
                w_ff1=m_w_ff1, w_ff2=m_w_ff2)
    v_in = dict(w_ada=v_w_ada, b_ada=v_b_ada, g_mix_pre=v_g_mix_pre, g_mix_post=v_g_mix_post, g_ff_pre=v_g_ff_pre, g_ff_post=v_g_ff_post,
                w_in=v_w_in, b_f=v_b_f, w_pool=v_w_pool, pool_scale=v_pool_scale, conv_w=v_conv_w, w_branch=v_w_branch, w_out=v_w_out,
                w_ff1=v_w_ff1, w_ff2=v_w_ff2)
    order = ("w_ada", "b_ada", "g_mix_pre", "g_mix_post", "g_ff_pre", "g_ff_post", "w_in", "b_f", "w_pool", "pool_scale", "conv_w",
             "w_branch", "w_out", "w_ff1", "w_ff2")
    delta, new_m, new_v = {}, {}, {}
    tiny = ("b_ada",) + _SMALL
    tiny_g = [_tie(grads[tiny[0]], tail_token)] + [grads[name] for name in tiny[1:]]
    res = _adamw_many([weights[name] for name in tiny], tiny_g, [m_in[name] for name in tiny], [v_in[name] for name in tiny], "adamw_small")
    for out, vals in zip((delta, new_m, new_v), res):
        out.update(zip(tiny, vals))
    delta["w_ada"], new_m["w_ada"], new_v["w_ada"] = _adamw(w_ada, grad_w_ada, m_w_ada, v_w_ada, "adamw_w_ada")
    comm.finish_ff(delta["w_ada"][0, :8, :128] + delta["b_ada"][0, :128])
    for name in ("w_ff1", "w_ff2", "w_in", "w_branch", "w_out"):
        if name == "w_in":
            comm.finish_mix(delta["w_ff2"][0, :8, :128])
        grads[name] = comm.grads[name]
        if name == "w_in":
            g_view = lax.optimization_barrier(_w_in_view(grads[name]))
            res = _adamw(_w_in_view(w_in), g_view, _w_in_view(m_w_in), _w_in_view(v_w_in), "adamw_w_in")
            grads[name], delta[name], new_m[name], new_v[name] = [_w_in_unview(t) for t in (g_view, *res)]
        else:
            delta[name], new_m[name], new_v[name] = _adamw(weights[name], grads[name], m_in[name], v_in[name], "adamw_" + name)

    return (loss, grad_x[None], *[grads[n] for n in order], *[delta[n] for n in order], *[new_m[n] for n in order],
            *[new_v[n] for n in order])
```

```python
from typing import NamedTuple

import jax
import jax.numpy as jnp
from jax import lax
from jax.experimental import pallas as pl
from jax.experimental.pallas import tpu as pltpu

F32 = jnp.float32
BF16 = jnp.bfloat16
MESH = pl.DeviceIdType.MESH

D = 1024
DEPTH = 2
HEADS = 8
HEAD_DIM = 64
A_WIDTH = 512
POOL_WIDTH = 256
CONV_WIDTH = 256
D_FF = 4096
IN_COLS = 5640
Z_GL, Z_QKV, Z_PC, Z_FL, Z_COLS = 0, 3072, 4608, 5632, 5760
RMS_EPS = 1e-6
NEG_INF = -1e30
ROW_TILE = 512
EW_ROWS = 256
N_CHIPS = 4
N_DEV = 8
V7X_VMEM_LIMIT = 48 * 1024 * 1024

ADAM_LR = 0.001
ADAM_B1 = 0.9
ADAM_B2 = 0.999
ADAM_EPS = 1e-08
ADAM_WD = 0.01
ADAM_STEP = 10

_HBM = pl.BlockSpec(memory_space=pltpu.HBM)


def _params(*sem):
    return pltpu.CompilerParams(dimension_semantics=sem, vmem_limit_bytes=V7X_VMEM_LIMIT)


def _pick(dim, cands):
    for cand in cands:
        if dim % cand == 0:
            return cand
    return dim


MM_TILE_BUDGET = 39 * 1024 * 1024


def _mm_tiles(m, n, k, k_unit, tn, a_size, b_size, out_size):
    for tk in (k_unit, 2048, 1152, 1024, 640, 512, 256, 128):
        if k_unit % tk:
            continue
        for tm in (2048, 1024, 512, 256, 128):
            if m % tm or ((m // tm) * (n // tn) < 2 and tm > 512):
                continue
            need = 2 * (tm * tk * a_size + tk * tn * b_size + tm * tn * out_size) + (0 if tk == k else 4 * tm * tn)
            if need <= MM_TILE_BUDGET and (tk == k_unit or tm >= 512):
                return tm, tk
    return 128, 128


def _mm(a, b, *, ta=False, tb=False, b_rows=None, b_split=1, out_split=1, out_dtype=F32, epilogue=None, extras=(), name):
    (k, m) = a.shape if ta else a.shape[::-1]
    b_row0, b_rows = (0, b.shape[-2]) if b_rows is None else b_rows
    b_cols = b.shape[-1] * b_split
    (n, k2) = (b_rows, b_cols) if tb else (b_cols, b_rows)
    assert k == k2, (a.shape, b.shape, ta, tb)
    n_unit = n // (out_split * (1 if tb else b_split))
    k_unit = k // (b_split if tb else 1)
    tn = _pick(n_unit, (1024, 1152, 768, 640, 512, 256, 128))
    tm, tk = _mm_tiles(m, n, k, k_unit, tn, a.dtype.itemsize, b.dtype.itemsize,
                       sum(jnp.dtype(dt).itemsize for dt in out_dtype) + 4 * len(extras) if epilogue else jnp.dtype(out_dtype).itemsize)
    nk = k // tk
    dims = (((0 if ta else 1,), (1 if tb else 0,)), ((), ()))

    def dot(a_ref, b_ref):
        b_val = b_ref[0] if b_split > 1 else b_ref[...]
        return lax.dot_general(a_ref[...].astype(BF16), b_val.astype(BF16), dims, preferred_element_type=F32)

    n_extra = len(extras)
    assert epilogue is None or out_split == 1

    def put(refs, val):
        if epilogue is not None:
            for o_ref, res in zip(refs[n_extra:], epilogue(val, *[r[...] for r in refs[:n_extra]])):
                o_ref[...] = res.astype(o_ref.dtype)
        elif out_split > 1:
            refs[0][0] = val.astype(refs[0].dtype)
        else:
            refs[0][...] = val.astype(refs[0].dtype)

    def body_single(a_ref, b_ref, *refs):
        put(refs, dot(a_ref, b_ref))

    def body_acc(a_ref, b_ref, *refs):
        kk = pl.program_id(2)
        acc_ref = refs[-1]

        @pl.when(kk == 0)
        def _():
            acc_ref[...] = jnp.zeros_like(acc_ref)

        acc_ref[...] += dot(a_ref, b_ref)

        @pl.when(kk == nk - 1)
        def _():
            put(refs[:-1], acc_ref[...])

    a_spec = pl.BlockSpec((tk, tm), lambda i, j, kk: (kk, i)) if ta else pl.BlockSpec((tm, tk), lambda i, j, kk: (i, kk))
    if b_split == 1:
        off = b_row0 // (tn if tb else tk)
        assert off * (tn if tb else tk) == b_row0
        b_spec = pl.BlockSpec((tn, tk), lambda i, j, kk: (j + off, kk)) if tb else pl.BlockSpec((tk, tn), lambda i, j, kk: (kk + off, j))
    elif tb:
        per = k_unit // tk
        b_spec = pl.BlockSpec((1, tn, tk), lambda i, j, kk: (kk // per, j, kk % per))
    else:
        per = n // b_split // tn
        b_spec = pl.BlockSpec((1, tk, tn), lambda i, j, kk: (j // per, kk, j % per))
    if out_split == 1:
        o_spec = pl.BlockSpec((tm, tn), lambda i, j, kk: (i, j))
        o_shape = None if epilogue is not None else jax.ShapeDtypeStruct((m, n), out_dtype)
    else:
        per_o = n // out_split // tn
        o_spec = pl.BlockSpec((1, tm, tn), lambda i, j, kk: (j // per_o, i, j % per_o))
        o_shape = jax.ShapeDtypeStruct((out_split, m, n // out_split), out_dtype)
    if epilogue is not None:
        o_shape = [jax.ShapeDtypeStruct((m, n), dt) for dt in out_dtype]
        o_spec = [o_spec] * len(out_dtype)
    return pl.pallas_call(
        body_single if nk == 1 else body_acc, name=name, grid=(m // tm, n // tn, nk),
        in_specs=[a_spec, b_spec] + [pl.BlockSpec((tm, tn), lambda i, j, kk: (i, j))] * n_extra, out_specs=o_spec, out_shape=o_shape,
        scratch_shapes=[] if nk == 1 else [pltpu.VMEM((tm, tn), F32)],
        compiler_params=_params("parallel", "parallel", "arbitrary"),
    )(a, b, *extras)


def _ew(fn, ins, out_dtypes, name, tc=None):
    shape = ins[0].shape
    lead, (rows, cols) = shape[:-2], shape[-2:]
    tc = cols if tc is None else tc
    if tc > 1024:
        tr = _pick(rows, (EW_ROWS, 128, 8))
    elif tc > 128:
        tr = _pick(rows, (2 * EW_ROWS, EW_ROWS, 128, 8))
    else:
        tr = _pick(rows, (4096, 2256, 2048, 1024, EW_ROWS, 8))
    n_in = len(ins)

    def body(*refs):
        res = fn(*[r[...] for r in refs[:n_in]])
        for o_ref, val in zip(refs[n_in:], res):
            o_ref[...] = val.astype(o_ref.dtype)

    if lead:
        spec = pl.BlockSpec((None, tr, tc), lambda l, i, j: (l, i, j))
    else:
        spec = pl.BlockSpec((tr, tc), lambda i, j: (i, j))
    return pl.pallas_call(
        body, name=name, grid=lead + (rows // tr, cols // tc),
        in_specs=[spec] * n_in, out_specs=[spec] * len(out_dtypes),
        out_shape=[jax.ShapeDtypeStruct(shape, dt) for dt in out_dtypes],
        compiler_params=_params(*(["parallel"] * (len(lead) + 2))),
    )(*ins)


def _relu2_fwd(a):
    r = jnp.maximum(a, 0.0)
    return a, r * r


def _relu2_bwd(dr, a):
    return (dr * (2.0 * jnp.maximum(a, 0.0)),)


def _adamw_math(w, g, m, v):
    m = ADAM_B1 * m + (1.0 - ADAM_B1) * g
    v = ADAM_B2 * v + (1.0 - ADAM_B2) * (g * g)
    m_hat = m / (1.0 - ADAM_B1 ** ADAM_STEP)
    v_hat = v / (1.0 - ADAM_B2 ** ADAM_STEP)
    delta = -ADAM_LR * (m_hat / (jnp.sqrt(v_hat) + ADAM_EPS) + ADAM_WD * w)
    return delta, m, v


def _adamw(w, g, m, v, name):
    return _ew(_adamw_math, [w, g, m, v], [F32, F32, F32], name)


def _adamw_many(ws, gs, ms, vs, name):
    n = len(ws)

    def body(*refs):
        for i in range(n):
            res = _adamw_math(*[refs[k * n + i][...] for k in range(4)])
            for k in range(3):
                refs[(4 + k) * n + i][...] = res[k]

    outs = pl.pallas_call(
        body, name=name, out_shape=[jax.ShapeDtypeStruct(w.shape, F32) for w in ws] * 3,
        compiler_params=pltpu.CompilerParams(vmem_limit_bytes=V7X_VMEM_LIMIT),
    )(*ws, *gs, *ms, *vs)
    return outs[:n], outs[n:2 * n], outs[2 * n:]


def _row_spec(cols, block=0):
    return pl.BlockSpec((ROW_TILE, cols), lambda i, block=block: (i, block))


def _vec_spec(cols):
    return pl.BlockSpec((1, cols), lambda i: (0, 0))


def _vec_args(*vecs):
    arrays = [v[0] if isinstance(v, tuple) else v for v in vecs]
    specs = [pl.BlockSpec((None, 1, D), lambda i, row=v[1]: (row, 0, 0)) if isinstance(v, tuple) else _vec_spec(D) for v in vecs]
    return arrays, specs


def _sum_spec(cols):
    return pl.BlockSpec((8, cols), lambda i: (0, 0))


def _rstd(x):
    return lax.rsqrt(jnp.mean(x * x, axis=-1, keepdims=True) + RMS_EPS)


def _modnorm_fwd(x, g, shift, scale, name):
    s = x.shape[0]

    def body(x_ref, g_ref, sh_ref, sc_ref, h_ref):
        xv = x_ref[...]
        n = xv * _rstd(xv)
        h_ref[...] = ((n * g_ref[...]) * (1.0 + sc_ref[...]) + sh_ref[...]).astype(BF16)

    vecs, vec_specs = _vec_args(g, shift, scale)
    return pl.pallas_call(
        body, name=name, grid=(s // ROW_TILE,),
        in_specs=[_row_spec(D)] + vec_specs, out_specs=_row_spec(D),
        out_shape=jax.ShapeDtypeStruct((s, D), BF16), compiler_params=_params("parallel"),
    )(x, *vecs)


def _post_fwd(x, y, g, gate, name):
    s = x.shape[0]

    def body(x_ref, y_ref, g_ref, gate_ref, o_ref):
        yv = y_ref[...]
        o_ref[...] = x_ref[...] + gate_ref[...] * ((yv * _rstd(yv)) * g_ref[...])

    vecs, vec_specs = _vec_args(g, gate)
    return pl.pallas_call(
        body, name=name, grid=(s // ROW_TILE,),
        in_specs=[_row_spec(D), _row_spec(D)] + vec_specs, out_specs=_row_spec(D),
        out_shape=jax.ShapeDtypeStruct((s, D), F32), compiler_params=_params("parallel"),
    )(x, y, *vecs)


def _post_bwd(dxo, y, g, gate, name):
    s = dxo.shape[0]

    def body(d_ref, y_ref, g_ref, gate_ref, dy_ref, sum_ref):
        @pl.when(pl.program_id(0) == 0)
        def _():
            sum_ref[...] = jnp.zeros_like(sum_ref)

        dv, yv = d_ref[...], y_ref[...]
        r = _rstd(yv)
        n = yv * r
        sum_ref[0:1, :] += jnp.sum(dv * (n * g_ref[...]), axis=0, keepdims=True)
        sum_ref[1:2, :] += jnp.sum((dv * gate_ref[...]) * n, axis=0, keepdims=True)
        dn = (dv * gate_ref[...]) * g_ref[...]
        dy_ref[...] = (r * (dn - n * jnp.mean(dn * n, axis=-1, keepdims=True))).astype(BF16)

    vecs, vec_specs = _vec_args(g, gate)
    return pl.pallas_call(
        body, name=name, grid=(s // ROW_TILE,),
        in_specs=[_row_spec(D), _row_spec(D)] + vec_specs,
        out_specs=[_row_spec(D), _sum_spec(D)],
        out_shape=[jax.ShapeDtypeStruct((s, D), BF16), jax.ShapeDtypeStruct((8, D), F32)],
        compiler_params=_params("arbitrary"),
    )(dxo, y, *vecs)


def _modnorm_bwd(dh, x, dxo, g, scale, name):
    s = dh.shape[0]

    def body(dh_ref, x_ref, d_ref, g_ref, sc_ref, dx_ref, sum_ref):
        @pl.when(pl.program_id(0) == 0)
        def _():
            sum_ref[...] = jnp.zeros_like(sum_ref)

        dhv, xv = dh_ref[...], x_ref[...]
        r = _rstd(xv)
        n = xv * r
        one_sc = 1.0 + sc_ref[...]
        sum_ref[0:1, :] += jnp.sum(dhv, axis=0, keepdims=True)
        sum_ref[1:2, :] += jnp.sum(dhv * (n * g_ref[...]), axis=0, keepdims=True)
        sum_ref[2:3, :] += jnp.sum((dhv * one_sc) * n, axis=0, keepdims=True)
        dn = (dhv * one_sc) * g_ref[...]
        dx_ref[...] = d_ref[...] + r * (dn - n * jnp.mean(dn * n, axis=-1, keepdims=True))

    vecs, vec_specs = _vec_args(g, scale)
    return pl.pallas_call(
        body, name=name, grid=(s // ROW_TILE,),
        in_specs=[_row_spec(D), _row_spec(D), _row_spec(D)] + vec_specs,
        out_specs=[_row_spec(D), _sum_spec(D)],
        out_shape=[jax.ShapeDtypeStruct((s, D), F32), jax.ShapeDtypeStruct((8, D), F32)],
        compiler_params=_params("arbitrary"),
    )(dh, x, dxo, *vecs)


def _post_pre_fwd(x, y, g_post, gate, g_pre, shift, scale, name):
    s = x.shape[0]

    def body(x_ref, y_ref, gp_ref, gate_ref, g_ref, sh_ref, sc_ref, o_ref, h_ref):
        yv = y_ref[...]
        xo = x_ref[...] + gate_ref[...] * ((yv * _rstd(yv)) * gp_ref[...])
        o_ref[...] = xo
        h_ref[...] = (((xo * _rstd(xo)) * g_ref[...]) * (1.0 + sc_ref[...]) + sh_ref[...]).astype(BF16)

    vecs, vec_specs = _vec_args(g_post, gate, g_pre, shift, scale)
    return pl.pallas_call(
        body, name=name, grid=(s // ROW_TILE,),
        in_specs=[_row_spec(D), _row_spec(D)] + vec_specs, out_specs=[_row_spec(D), _row_spec(D)],
        out_shape=[jax.ShapeDtypeStruct((s, D), F32), jax.ShapeDtypeStruct((s, D), BF16)], compiler_params=_params("parallel"),
    )(x, y, *vecs)


def _pre_post_bwd(dh, x, dxo, g_pre, scale, y, g_post, gate, name):
    s = dh.shape[0]

    def body(dh_ref, x_ref, d_ref, y_ref, g_ref, sc_ref, gp_ref, gate_ref, dx_ref, dy_ref, sum_ref):
        @pl.when(pl.program_id(0) == 0)
        def _():
            sum_ref[...] = jnp.zeros_like(sum_ref)

        dhv, xv = dh_ref[...], x_ref[...]
        r = _rstd(xv)
        n = xv * r
        one_sc = 1.0 + sc_ref[...]
        sum_ref[0:1, :] += jnp.sum(dhv, axis=0, keepdims=True)
        sum_ref[1:2, :] += jnp.sum(dhv * (n * g_ref[...]), axis=0, keepdims=True)
        sum_ref[2:3, :] += jnp.sum((dhv * one_sc) * n, axis=0, keepdims=True)
        dn = (dhv * one_sc) * g_ref[...]
        dv = d_ref[...] + r * (dn - n * jnp.mean(dn * n, axis=-1, keepdims=True))
        dx_ref[...] = dv

        yv = y_ref[...]
        ry = _rstd(yv)
        ny = yv * ry
        sum_ref[3:4, :] += jnp.sum(dv * (ny * gp_ref[...]), axis=0, keepdims=True)
        sum_ref[4:5, :] += jnp.sum((dv * gate_ref[...]) * ny, axis=0, keepdims=True)
        dny = (dv * gate_ref[...]) * gp_ref[...]
        dy_ref[...] = (ry * (dny - ny * jnp.mean(dny * ny, axis=-1, keepdims=True))).astype(BF16)

    vecs, vec_specs = _vec_args(g_pre, scale, g_post, gate)
    return pl.pallas_call(
        body, name=name, grid=(s // ROW_TILE,),
        in_specs=[_row_spec(D)] * 4 + vec_specs,
        out_specs=[_row_spec(D), _row_spec(D), _sum_spec(D)],
        out_shape=[jax.ShapeDtypeStruct((s, D), F32), jax.ShapeDtypeStruct((s, D), BF16), jax.ShapeDtypeStruct((8, D), F32)],
        compiler_params=_params("arbitrary"),
    )(dh, x, dxo, y, *vecs)


def _loss_head(y, target):
    s = y.shape[0]

    def body(y_ref, t_ref, dy_ref, sum_ref):
        @pl.when(pl.program_id(0) == 0)
        def _():
            sum_ref[...] = jnp.zeros_like(sum_ref)

        err = y_ref[...] - t_ref[...]
        dy_ref[...] = err * (1.0 / D)
        sum_ref[...] += jnp.sum(err * err)

    return pl.pallas_call(
        body, name="loss_head", grid=(s // ROW_TILE,),
        in_specs=[_row_spec(D), _row_spec(D)],
        out_specs=[_row_spec(D), pl.BlockSpec((8, 128), lambda i: (0, 0))],
        out_shape=[jax.ShapeDtypeStruct((s, D), F32), jax.ShapeDtypeStruct((8, 128), F32)],
        compiler_params=_params("arbitrary"),
    )(y, target)


def _merge_fwd(z, pa, pb, pc):
    s = z.shape[0]

    def body(g0_ref, g1_ref, g2_ref, pa_ref, pb_ref, pc_ref, o_ref):
        o_ref[...] = (jax.nn.sigmoid(g0_ref[...]) * pa_ref[...] + jax.nn.sigmoid(g1_ref[...]) * pb_ref[...]
                      + jax.nn.sigmoid(g2_ref[...]) * pc_ref[...]).astype(BF16)

    return pl.pallas_call(
        body, name="merge_fwd", grid=(s // ROW_TILE,),
        in_specs=[_row_spec(D, 0), _row_spec(D, 1), _row_spec(D, 2), _row_spec(D), _row_spec(D), _row_spec(D)],
        out_specs=_row_spec(D), out_shape=jax.ShapeDtypeStruct((s, D), BF16),
        compiler_params=_params("parallel"),
    )(z, z, z, pa, pb, pc)


def _merge_bwd(dm, z, pa, pb, pc):
    s = z.shape[0]

    def body(dm_ref, g0_ref, g1_ref, g2_ref, pa_ref, pb_ref, pc_ref, dgl_ref, da_ref, db_ref, dc_ref):
        dmv = dm_ref[...]
        for i, (g_ref, p_ref, d_ref) in enumerate(((g0_ref, pa_ref, da_ref), (g1_ref, pb_ref, db_ref), (g2_ref, pc_ref, dc_ref))):
            gate = jax.nn.sigmoid(g_ref[...])
            dgl_ref[:, i * D:(i + 1) * D] = ((dmv * p_ref[...]) * (gate * (1.0 - gate))).astype(BF16)
            d_ref[...] = (dmv * gate).astype(BF16)

    return pl.pallas_call(
        body, name="merge_bwd", grid=(s // ROW_TILE,),
        in_specs=[_row_spec(D), _row_spec(D, 0), _row_spec(D, 1), _row_spec(D, 2), _row_spec(D), _row_spec(D), _row_spec(D)],
        out_specs=[_row_spec(3 * D), _row_spec(D), _row_spec(D), _row_spec(D)],
        out_shape=[jax.ShapeDtypeStruct((s, Z_COLS), BF16)] + [jax.ShapeDtypeStruct((s, D), BF16)] * 3,
        compiler_params=_params("parallel"),
    )(dm, z, z, z, pa, pb, pc)


def _shift_down(v, n):
    row = lax.broadcasted_iota(jnp.int32, v.shape, 0)
    return jnp.where(row >= n, pltpu.roll(v, n, axis=0), 0.0)


def _shift_up(v, n):
    s = v.shape[0]
    row = lax.broadcasted_iota(jnp.int32, v.shape, 0)
    return jnp.where(row < s - n, pltpu.roll(v, s - n, axis=0), 0.0)


def _log_sigmoid(v):
    return jnp.minimum(v, 0.0) - jnp.log1p(jnp.exp(-jnp.abs(v)))


def _cumf_fwd(fl, bias):
    s = fl.shape[0]

    def body(fl_ref, b_ref, o_ref):
        acc = _log_sigmoid(fl_ref[...] + b_ref[...])
        step = 1
        while step < s:
            acc = acc + _shift_down(acc, step)
            step *= 2
        o_ref[...] = acc

    return pl.pallas_call(body, name="cumf_fwd", out_shape=jax.ShapeDtypeStruct((s, 128), F32),
                          compiler_params=pltpu.CompilerParams(vmem_limit_bytes=V7X_VMEM_LIMIT))(fl, bias)


def _cumf_bwd(dcum, fl, bias):
    s = fl.shape[0]

    def body(d_ref, fl_ref, b_ref, dfl_ref, db_ref):
        acc = d_ref[...]
        step = 1
        while step < s:
            acc = acc + _shift_up(acc, step)
            step *= 2
        dfl = acc * jax.nn.sigmoid(-(fl_ref[...] + b_ref[...]))
        dfl_ref[...] = dfl.astype(BF16)
        db_ref[...] = jnp.broadcast_to(jnp.sum(dfl, axis=0, keepdims=True), (8, 128))

    return pl.pallas_call(
        body, name="cumf_bwd",
        out_shape=[jax.ShapeDtypeStruct((s, 128), BF16), jax.ShapeDtypeStruct((8, 128), F32)],
        compiler_params=pltpu.CompilerParams(vmem_limit_bytes=V7X_VMEM_LIMIT))(dcum, fl, bias)


def _pool_windows(v, shift):
    s2 = v + shift(v, 1)
    s4 = s2 + shift(s2, 2)
    s8 = s4 + shift(s4, 4)
    s16 = s8 + shift(s8, 8)
    group = lax.broadcasted_iota(jnp.int32, v.shape, 1) // 64
    return jnp.where(group == 0, s2, jnp.where(group == 1, s4, jnp.where(group == 2, s8, s16)))


def _pool_count(shape):
    group = lax.broadcasted_iota(jnp.int32, shape, 1) // 64
    window = jnp.where(group == 0, 2.0, jnp.where(group == 1, 4.0, jnp.where(group == 2, 8.0, 16.0)))
    t1 = (lax.broadcasted_iota(jnp.int32, shape, 0) + 1).astype(F32)
    return jnp.minimum(t1, window)


def _pc_specs(s):
    zcol = lambda blk: pl.BlockSpec((s, 256), lambda i, blk=blk: (0, blk))
    first = Z_PC // 256
    return [zcol(first), zcol(first + 1), zcol(first + 2), zcol(first + 3),
            pl.BlockSpec((256, 256), lambda i: (0, 0)), pl.BlockSpec((1, 256), lambda i: (0, 0)),
            pl.BlockSpec((3, 256), lambda i: (0, 0))]


def _poolconv_fwd(z, wbd, pscale, convw):
    s = z.shape[0]

    def body(pu_ref, ch_ref, cb_ref, cc_ref, w_ref, ps_ref, cw_ref, yb_ref, yc_ref):
        u = pu_ref[...]
        p = _pool_windows(u, _shift_down) / _pool_count(u.shape) - u
        yb = jnp.dot(p.astype(BF16), w_ref[...].astype(BF16), preferred_element_type=F32) * ps_ref[...]
        yb_ref[...] = yb.astype(BF16)
        uc = cc_ref[...] * ch_ref[...]
        cw = cw_ref[...]
        conv = cw[0:1, :] * _shift_down(uc, 2) + cw[1:2, :] * _shift_down(uc, 1) + cw[2:3, :] * uc
        yc_ref[...] = (cb_ref[...] * conv).astype(BF16)

    out = pl.BlockSpec((s, 256), lambda i: (0, 0))
    return pl.pallas_call(
        body, name="poolconv_fwd", grid=(1,), in_specs=_pc_specs(s), out_specs=[out, out],
        out_shape=[jax.ShapeDtypeStruct((s, 256), BF16)] * 2, compiler_params=_params("arbitrary"),
    )(z, z, z, z, wbd, pscale, convw)


def _poolconv_bwd(dyb, dyc, z, wbd, pscale, convw):
    s = z.shape[0]

    def body(dyb_ref, dyc_ref, pu_ref, ch_ref, cb_ref, cc_ref, w_ref, ps_ref, cw_ref, dz_ref, dw_ref, dps_ref, dcw_ref):
        u = pu_ref[...]
        count = _pool_count(u.shape)
        p = (_pool_windows(u, _shift_down) / count - u).astype(BF16)
        wb = w_ref[...].astype(BF16)
        dyb_v = dyb_ref[...]
        pw = jnp.dot(p, wb, preferred_element_type=F32)
        dps_ref[...] = jnp.broadcast_to(jnp.sum(dyb_v * pw, axis=0, keepdims=True), (8, 256))
        dys = (dyb_v * ps_ref[...]).astype(BF16)
        dp = lax.dot_general(dys, wb, (((1,), (1,)), ((), ())), preferred_element_type=F32)
        dw_ref[...] = lax.dot_general(p, dys, (((0,), (0,)), ((), ())), preferred_element_type=F32)
        dz_ref[:, 0:256] = (_pool_windows(dp / count, _shift_up) - dp).astype(BF16)

        ch, cb, cc = ch_ref[...], cb_ref[...], cc_ref[...]
        uc = cc * ch
        cw = cw_ref[...]
        u2, u1 = _shift_down(uc, 2), _shift_down(uc, 1)
        conv = cw[0:1, :] * u2 + cw[1:2, :] * u1 + cw[2:3, :] * uc
        dyc_v = dyc_ref[...]
        dconv = dyc_v * cb
        du = cw[0:1, :] * _shift_up(dconv, 2) + cw[1:2, :] * _shift_up(dconv, 1) + cw[2:3, :] * dconv
        dz_ref[:, 256:512] = (du * cc).astype(BF16)
        dz_ref[:, 512:768] = (dyc_v * conv).astype(BF16)
        dz_ref[:, 768:1024] = (du * ch).astype(BF16)
        dcw_ref[...] = jnp.zeros_like(dcw_ref)
        dcw_ref[0:1, :] = jnp.sum(dconv * u2, axis=0, keepdims=True)
        dcw_ref[1:2, :] = jnp.sum(dconv * u1, axis=0, keepdims=True)
        dcw_ref[2:3, :] = jnp.sum(dconv * uc, axis=0, keepdims=True)

    blk = lambda r, c: pl.BlockSpec((r, c), lambda i: (0, 0))
    return pl.pallas_call(
        body, name="poolconv_bwd", grid=(1,),
        in_specs=[blk(s, 256), blk(s, 256)] + _pc_specs(s),
        out_specs=[blk(s, 1024), blk(256, 256), blk(8, 256), blk(8, 256)],
        out_shape=[jax.ShapeDtypeStruct((s, 1024), BF16), jax.ShapeDtypeStruct((256, 256), F32),
                   jax.ShapeDtypeStruct((8, 256), F32), jax.ShapeDtypeStruct((8, 256), F32)],
        compiler_params=_params("arbitrary"),
    )(dyb, dyc, z, z, z, z, wbd, pscale, convw)


_NT = (((1,), (1,)), ((), ()))
_TN = (((0,), (0,)), ((), ()))


ATT_Q, ATT_K = 256, 256
ATT_HEADS_BWD = 8
ATT_HEADS = 8


def _att_logits(q, k, fr, q0, k0, masked):
    logits = lax.dot_general(q, k, _NT, preferred_element_type=F32) - fr
    if not masked:
        return logits
    row = q0 + lax.broadcasted_iota(jnp.int32, logits.shape, 0)
    col = k0 + lax.broadcasted_iota(jnp.int32, logits.shape, 1)
    return jnp.where(row >= col, logits, NEG_INF)


def _causal_sweep(step, qi, init):
    n_full = (qi * ATT_Q) // ATT_K
    carry = lax.fori_loop(0, n_full, lambda j, carry: step(j, carry, False), init)
    return step(n_full, carry, True)


HEAD_PAIRS = HEADS // 2


def _lane_pick(v, lane, idx):
    return jnp.sum(jnp.where(lane == idx, v, 0.0), axis=-1, keepdims=True)


def _lane_put(lane, idx, col):
    return jnp.where(lane == idx, col, 0.0)


def _split_heads(v, low):
    zero = jnp.zeros_like(v)
    return jnp.where(low, v, zero), jnp.where(low, zero, v)


def _attn_fwd(qkv, fr):
    s = qkv.shape[0]
    nk = s // ATT_K
    width = ATT_HEADS * HEAD_DIM
    groups = HEADS // ATT_HEADS

    def body(q_ref, k_ref, v_ref, fr_ref, o_ref, lse_ref):
        qi, grp = pl.program_id(0), pl.program_id(1)
        lane = lax.broadcasted_iota(jnp.int32, (ATT_Q, 128), 1)
        low = lane < HEAD_DIM
        qs = []
        for pr in range(ATT_HEADS // 2):
            qs += _split_heads(q_ref[:, 128 * pr:128 * (pr + 1)] * (HEAD_DIM ** -0.5), low)

        def step(j, carry, masked):
            k0 = pl.multiple_of(j * ATT_K, ATT_K)
            out = []
            for h in range(ATT_HEADS):
                cols = slice(128 * (h // 2), 128 * (h // 2 + 1))
                m, l, acc = carry[h]
                logits = _att_logits(qs[h], k_ref[pl.ds(k0, ATT_K), cols], fr_ref[h, pl.ds(j, 1), :], qi * ATT_Q, k0, masked)
                m_new = jnp.maximum(m, jnp.max(logits, axis=-1, keepdims=True))
                p = jnp.exp(logits - m_new)
                alpha = jnp.exp(m - m_new)
                l = alpha * l + jnp.sum(p, axis=-1, keepdims=True)
                acc = alpha * acc + jnp.dot(p.astype(BF16), v_ref[pl.ds(k0, ATT_K), cols], preferred_element_type=F32)
                out.append((m_new, l, acc))
            return tuple(out)

        one = (jnp.full((ATT_Q, 1), NEG_INF, F32), jnp.zeros((ATT_Q, 1), F32), jnp.zeros((ATT_Q, 128), F32))
        done = _causal_sweep(step, qi, (one,) * ATT_HEADS)

        @pl.when(grp == 0)
        def _():
            lse_ref[...] = jnp.zeros_like(lse_ref)

        lse = jnp.zeros((ATT_Q, 128), F32)
        for pr in range(ATT_HEADS // 2):
            (m0, l0, acc0), (m1, l1, acc1) = done[2 * pr], done[2 * pr + 1]
            o_ref[:, 128 * pr:128 * (pr + 1)] = jnp.where(low, acc0 / l0, acc1 / l1)
            head = ATT_HEADS * grp + 2 * pr
            lse = lse + _lane_put(lane, head, m0 + jnp.log(l0)) + _lane_put(lane, head + 1, m1 + jnp.log(l1))
        lse_ref[...] += lse

    return pl.pallas_call(
        body, name="attn_fwd", grid=(s // ATT_Q, groups),
        in_specs=[pl.BlockSpec((ATT_Q, width), lambda i, g: (i, g)),
                  pl.BlockSpec((s, width), lambda i, g: (0, groups + g)),
                  pl.BlockSpec((s, width), lambda i, g: (0, 2 * groups + g)),
                  pl.BlockSpec((ATT_HEADS, nk, ATT_K), lambda i, g: (g, 0, 0))],
        out_specs=[pl.BlockSpec((ATT_Q, width), lambda i, g: (i, g)), pl.BlockSpec((ATT_Q, 128), lambda i, g: (i, 0))],
        out_shape=[jax.ShapeDtypeStruct((s, A_WIDTH), F32), jax.ShapeDtypeStruct((s, 128), F32)],
        compiler_params=_params("parallel", "arbitrary"),
    )(qkv, qkv, qkv, fr)


def _attn_bwd(qkv, do, o, lse, fr):
    s = qkv.shape[0]
    nk = s // ATT_K
    scale = HEAD_DIM ** -0.5
    heads = ATT_HEADS_BWD
    width = heads * HEAD_DIM
    groups = HEADS // heads

    def body(q_ref, k_ref, v_ref, do_ref, o_ref, lse_ref, fr_ref, dq_ref, dk_ref, dv_ref, dfc_ref, dfr_ref, dk_acc, dv_acc):
        grp = pl.program_id(0)
        lane = lax.broadcasted_iota(jnp.int32, (ATT_Q, 128), 1)
        low = lane < HEAD_DIM
        low_t = lax.broadcasted_iota(jnp.int32, (128, ATT_Q), 0) < HEAD_DIM
        dk_acc[...] = jnp.zeros_like(dk_acc)
        dv_acc[...] = jnp.zeros_like(dv_acc)
        dfr_ref[...] = jnp.zeros_like(dfr_ref)

        @pl.when(grp == 0)
        def _():
            dfc_ref[...] = jnp.zeros_like(dfc_ref)

        def outer(i, carry):
            q0 = pl.multiple_of(i * ATT_Q, ATT_Q)
            rows = pl.ds(q0, ATT_Q)
            lsev = lse_ref[rows, :]
            qts, dots, qs, dos, deltas, lses = [], [], [], [], [], []
            for pr in range(heads // 2):
                pcols = slice(128 * pr, 128 * (pr + 1))
                q2, do2 = q_ref[rows, pcols] * scale, do_ref[rows, pcols]
                prod = do2 * o_ref[rows, pcols]
                deltas += [jnp.sum(jnp.where(low, prod, 0.0), axis=-1, keepdims=True),
                           jnp.sum(jnp.where(low, 0.0, prod), axis=-1, keepdims=True)]
                dob2 = do2.astype(BF16)
                qts += _split_heads(q2.astype(F32).T.astype(BF16), low_t)
                dots += _split_heads(do2.T.astype(BF16), low_t)
                qs += _split_heads(q2, low)
                dos += _split_heads(dob2, low)
                lses += [_lane_pick(lsev, lane, heads * grp + 2 * pr), _lane_pick(lsev, lane, heads * grp + 2 * pr + 1)]

            def inner(j, carry, masked):
                k0 = pl.multiple_of(j * ATT_K, ATT_K)
                krows = pl.ds(k0, ATT_K)
                out, dkt, dvt = [], [], []
                for h in range(heads):
                    pcols = slice(128 * (h // 2), 128 * (h // 2 + 1))
                    dq, dfc = carry[h]
                    k2 = k_ref[krows, pcols]
                    p = jnp.exp(_att_logits(qs[h], k2, fr_ref[h, pl.ds(j, 1), :], q0, k0, masked) - lses[h])
                    dp = lax.dot_general(dos[h], v_ref[krows, pcols], _NT, preferred_element_type=F32)
                    ds = p * (dp - deltas[h])
                    dsb = ds.astype(BF16)
                    dkt.append(jnp.dot(qts[h], dsb, preferred_element_type=F32))
                    dvt.append(jnp.dot(dots[h], p.astype(BF16), preferred_element_type=F32))
                    dfr_ref[h, pl.ds(j, 1), :] -= jnp.sum(ds, axis=0, keepdims=True)
                    out.append((dq + jnp.dot(dsb, k2, preferred_element_type=F32), dfc + (ds[:, :128] + ds[:, 128:])))
                for pr in range(heads // 2):
                    prows = slice(128 * pr, 128 * (pr + 1))
                    dk_acc[j, prows, :] += dkt[2 * pr] + dkt[2 * pr + 1]
                    dv_acc[j, prows, :] += dvt[2 * pr] + dvt[2 * pr + 1]
                return tuple(out)

            one = (jnp.zeros((ATT_Q, 128), F32), jnp.zeros((ATT_Q, 128), F32))
            done = _causal_sweep(inner, i, (one,) * heads)
            dfc = jnp.zeros((ATT_Q, 128), F32)
            for pr in range(heads // 2):
                (dq0, dfc0), (dq1, dfc1) = done[2 * pr], done[2 * pr + 1]
                dq_ref[rows, 128 * pr:128 * (pr + 1)] = (jnp.where(low, dq0, dq1) * scale).astype(BF16)
                head = heads * grp + 2 * pr
                dfc = (dfc + _lane_put(lane, head, jnp.sum(dfc0, axis=-1, keepdims=True))
                       + _lane_put(lane, head + 1, jnp.sum(dfc1, axis=-1, keepdims=True)))
            dfc_ref[rows, :] += dfc
            return carry

        lax.fori_loop(0, s // ATT_Q, outer, 0)
        for j in range(nk):
            for pr in range(heads // 2):
                prows, pcols = slice(128 * pr, 128 * (pr + 1)), slice(128 * pr, 128 * (pr + 1))
                dk_ref[ATT_K * j:ATT_K * (j + 1), pcols] = dk_acc[j, prows, :].T.astype(BF16)
                dv_ref[ATT_K * j:ATT_K * (j + 1), pcols] = dv_acc[j, prows, :].T.astype(BF16)

    part = lambda first: pl.BlockSpec((s, width), lambda g, first=first: (0, first + g))
    whole = pl.BlockSpec((s, 128), lambda g: (0, 0))
    rowv = pl.BlockSpec((heads, nk, ATT_K), lambda g: (g, 0, 0))
    return pl.pallas_call(
        body, name="attn_bwd", grid=(groups,),
        in_specs=[part(0), part(groups), part(2 * groups), part(0), part(0), whole, rowv],
        out_specs=[part(0), part(0), part(0), whole, rowv],
        out_shape=[jax.ShapeDtypeStruct((s, A_WIDTH), BF16)] * 3 + [jax.ShapeDtypeStruct((s, 128), F32), jax.ShapeDtypeStruct((HEADS, nk, ATT_K), F32)],
        scratch_shapes=[pltpu.VMEM((nk, width, ATT_K), F32), pltpu.VMEM((nk, width, ATT_K), F32)],
        compiler_params=_params("arbitrary"),
    )(qkv, qkv, qkv, do, o, lse, fr)


def _ada_fwd(c_all, w_ada, b_loc):
    depth, _, n = w_ada.shape
    tn = 512

    def body(c_ref, w_ref, b_ref, o_ref, sc_ref):
        cv = c_ref[...]
        sc = cv * jax.nn.sigmoid(cv)
        sc_ref[...] = sc
        o_ref[0] = jnp.dot(sc.astype(BF16), w_ref[0].astype(BF16), preferred_element_type=F32) + b_ref[0]

    return pl.pallas_call(
        body, name="ada_fwd", grid=(depth, n // tn),
        in_specs=[pl.BlockSpec((N_DEV, D), lambda l, j: (0, 0)), pl.BlockSpec((1, D, tn), lambda l, j: (l, 0, j)),
                  pl.BlockSpec((1, 1, tn), lambda l, j: (l, 0, j))],
        out_specs=[pl.BlockSpec((1, N_DEV, tn), lambda l, j: (l, 0, j)), pl.BlockSpec((N_DEV, D), lambda l, j: (0, 0))],
        out_shape=[jax.ShapeDtypeStruct((depth, N_DEV, n), F32), jax.ShapeDtypeStruct((N_DEV, D), F32)],
        compiler_params=_params("arbitrary", "arbitrary"),
    )(c_all, w_ada, b_loc)


def _sum_devices(gathered):
    n = gathered.shape[1]
    tn = _pick(n, (1408, 1024, 640, 512, 128))

    def body(g_ref, o_ref):
        acc = g_ref[0:8, :]
        for dev in range(1, N_DEV):
            acc = acc + g_ref[8 * dev:8 * dev + 8, :]
        o_ref[...] = acc

    return pl.pallas_call(
        body, name="sum_devices", grid=(n // tn,),
        in_specs=[pl.BlockSpec((8 * N_DEV, tn), lambda j: (0, j))], out_specs=pl.BlockSpec((8, tn), lambda j: (0, j)),
        out_shape=jax.ShapeDtypeStruct((8, n), F32), compiler_params=_params("parallel"),
    )(gathered)


def _place():
    x, y, c = lax.axis_index("x"), lax.axis_index("y"), lax.axis_index("c")
    chips = [(1 - x, y), (x, 1 - y), (1 - x, 1 - y)]
    return x, y, c, chips


def _allgather8(block, name, after=()):
    m_per, n = block.shape

    def body(x_ref, *rest):
        out_ref, send_sems, recv_sems, local_sem = rest[len(after):]
        x, y, c, chips = _place()
        me, sibling = (x, y, c), (x, y, 1 - c)

        def rows(px, py, pc):
            return out_ref.at[pl.ds((4 * px + 2 * py + pc) * m_per, m_per), :]

        def copy(k, blk, to, src=None):
            return pltpu.make_async_remote_copy(
                src_ref=rows(*blk) if src is None else src, dst_ref=rows(*blk),
                send_sem=send_sems.at[k], recv_sem=recv_sems.at[k], device_id=to, device_id_type=MESH)

        mine = pltpu.make_async_copy(x_ref, rows(*me), local_sem)
        mine.start()
        first = [copy(0, me, sibling, src=x_ref)]
        first += [copy(1 + j, me, (*chip, c), src=x_ref) for j, chip in enumerate(chips)]
        for cp in first:
            cp.start()
        passed = [copy(4 + j, (*chip, c), sibling) for j, chip in enumerate(chips)]
        for j, chip in enumerate(chips):
            copy(1 + j, (*chip, c), me).wait_recv()
            passed[j].start()
        copy(0, sibling, me).wait_recv()
        for j, chip in enumerate(chips):
            copy(4 + j, (*chip, 1 - c), me).wait_recv()
        for cp in first + passed:
            cp.wait_send()
        mine.wait()

    return pl.pallas_call(
        body, name=name, out_shape=jax.ShapeDtypeStruct((N_DEV * m_per, n), block.dtype),
        in_specs=[pl.BlockSpec(memory_space=pltpu.VMEM)] + [pl.BlockSpec(memory_space=pl.ANY)] * len(after),
        out_specs=pl.BlockSpec(memory_space=pltpu.VMEM),
        scratch_shapes=[pltpu.SemaphoreType.DMA((7,)), pltpu.SemaphoreType.DMA((7,)), pltpu.SemaphoreType.DMA],
        compiler_params=pltpu.CompilerParams(vmem_limit_bytes=V7X_VMEM_LIMIT),
    )(block, *after)


_SEM = pl.BlockSpec(memory_space=pltpu.SEMAPHORE)
_DATAFLOW = pltpu.SideEffectType.DATAFLOW_SIDE_EFFECTING


def _plan_copies(plan, refs, send_sems, recv_sems):
    return [pltpu.make_async_remote_copy(src_ref=src, dst_ref=dst, send_sem=send_sems.at[i], recv_sem=recv_sems.at[i],
                                         device_id=to, device_id_type=MESH) for i, (src, dst, to) in enumerate(plan(refs))]


class _Token(NamedTuple):
    after: jax.Array
    tie: jax.Array


def _after_operand(after):
    return after.after if isinstance(after, _Token) else after


def _copies_start(bufs, plan, n_copies, after, name):
    nb = len(bufs)

    def body(*refs):
        for cp in _plan_copies(plan, refs[:nb], refs[nb + 1], refs[nb + 2]):
            cp.start()
        for token in refs[-2:]:
            token[...] = jnp.zeros_like(token)

    sem = pltpu.SemaphoreType.DMA((n_copies,))
    vmem = pl.BlockSpec(memory_space=pltpu.VMEM)
    outs = pl.pallas_call(
        body, name=name,
        out_shape=(sem, sem, *[pltpu.HBM(b.shape, b.dtype) for b in bufs], jax.ShapeDtypeStruct((8, 128), F32),
                   jax.ShapeDtypeStruct((1, 1), F32)),
        in_specs=[_HBM] * nb + [pl.BlockSpec(memory_space=pl.ANY)],
        out_specs=(_SEM, _SEM, *[_HBM] * nb, vmem, vmem),
        input_output_aliases={i: 2 + i for i in range(nb)},
        compiler_params=pltpu.CompilerParams(has_side_effects=_DATAFLOW),
    )(*[pltpu.with_memory_space_constraint(b, pltpu.HBM) for b in bufs], _after_operand(after))
    return outs[0], outs[1], list(outs[2:2 + nb]), _Token(outs[-2], outs[-1])


def _copies_wait(started, plan, after, name):
    send_sems, recv_sems, bufs, _ = started
    nb = len(bufs)

    def body(*refs):
        for cp in _plan_copies(plan, refs[:nb], refs[nb], refs[nb + 1]):
            cp.wait_send()
            cp.wait_recv()

    return list(pl.pallas_call(
        body, name=name, out_shape=tuple(pltpu.HBM(b.shape, b.dtype) for b in bufs),
        in_specs=[_HBM] * nb + [_SEM, _SEM, pl.BlockSpec(memory_space=pl.ANY)], out_specs=tuple([_HBM] * nb),
        input_output_aliases={i: i for i in range(nb)},
        compiler_params=pltpu.CompilerParams(has_side_effects=_DATAFLOW),
    )(*bufs, send_sems, recv_sems, _after_operand(after)))


def _half_rows(ref, axis, c):
    half = ref.shape[axis] // 2
    return pl.ds(c * half, half)


def _plan_gather_ici(refs):
    n = len(refs) // 2
    x, y, c, chips = _place()
    out = []
    for a in range(n):
        rows = _half_rows(refs[a], 0, c)
        out += [(refs[a].at[rows], refs[n + a].at[2 * x + y, rows], (*chip, c)) for chip in chips]
        out.append((refs[a], refs[n + a].at[2 * x + y], (x, y, 1 - c)))
    return out


def _plan_gather_d2d(refs):
    x, y, c, chips = _place()
    out = []
    for ref in refs:
        rows = _half_rows(ref, 1, c)
        for px, py in chips:
            landed = ref.at[2 * px + py, rows]
            out.append((landed, landed, (x, y, 1 - c)))
    return out


def _plan_rs_sibling(refs):
    n = len(refs) // 2
    x, y, c, _ = _place()
    return [(refs[a].at[pl.ds(0, N_CHIPS), _half_rows(refs[a], 1, 1 - c)], refs[n + a], (x, y, 1 - c)) for a in range(n)]


def _plan_rs_chips(refs):
    n = len(refs) // 2
    x, y, c, chips = _place()
    return [(refs[a].at[2 * px + py], refs[n + a].at[k], (px, py, c)) for a in range(n) for k, (px, py) in enumerate(chips)]


def _plan_rs_share(layer):
    def plan(refs):
        x, y, c, _ = _place()
        return [(ref.at[layer, _half_rows(ref, 1, c)], ref.at[layer, _half_rows(ref, 1, c)], (x, y, 1 - c)) for ref in refs]
    return plan


def _chip_sum(g, other, sel, name):
    _, half, cdim = other.shape
    tr = _pick(half, (512, 256, 128, 64))
    per = half // tr

    def body(sel_ref, g_ref, t_ref, wire_ref, own_ref):
        total = g_ref[0] + t_ref[0]
        wire_ref[0] = total.astype(BF16)

        @pl.when(pl.program_id(1) == sel_ref[1])
        def _():
            own_ref[...] = total

    blk = pl.BlockSpec((1, tr, cdim), lambda i, p, sel_ref: (p, i, 0))
    return pl.pallas_call(
        body, name=name,
        grid_spec=pltpu.PrefetchScalarGridSpec(
            num_scalar_prefetch=1, grid=(per, N_CHIPS),
            in_specs=[pl.BlockSpec((1, tr, cdim), lambda i, p, sel_ref: (p, sel_ref[0] * per + i, 0)), blk],
            out_specs=[blk, pl.BlockSpec((tr, cdim), lambda i, p, sel_ref: (i, 0))]),
        out_shape=[jax.ShapeDtypeStruct(other.shape, BF16), jax.ShapeDtypeStruct((half, cdim), F32)],
        compiler_params=_params("parallel", "arbitrary"),
    )(sel, g, other)


def _final_sum(own, recv, sel, layer, into, name):
    half, cdim = own.shape
    tr = _pick(half, (512, 256, 128, 64))
    per = half // tr

    def body(sel_ref, own_ref, r0_ref, r1_ref, r2_ref, *rest):
        rest[-1][...] = ((own_ref[...] + r0_ref[0].astype(F32)) + r1_ref[0].astype(F32)) + r2_ref[0].astype(F32)

    part = lambda k: pl.BlockSpec((1, tr, cdim), lambda i, sel_ref, k=k: (k, i, 0))
    prior = [] if into is None else [into]
    return pl.pallas_call(
        body, name=name,
        grid_spec=pltpu.PrefetchScalarGridSpec(
            num_scalar_prefetch=1, grid=(per,),
            in_specs=[pl.BlockSpec((tr, cdim), lambda i, sel_ref: (i, 0)), part(0), part(1), part(2)]
            + [pl.BlockSpec(memory_space=pl.ANY)] * len(prior),
            out_specs=pl.BlockSpec((None, tr, cdim), lambda i, sel_ref: (layer, sel_ref[0] * per + i, 0))),
        out_shape=jax.ShapeDtypeStruct((DEPTH, 2 * half, cdim), F32),
        input_output_aliases={5: 0} if prior else {}, compiler_params=_params("parallel"),
    )(sel, own, recv, recv, recv, *prior)


def _row(v):
    return v.reshape(1, -1)


_BR_A, _BR_B, _BR_C = (0, A_WIDTH), (A_WIDTH, POOL_WIDTH), (A_WIDTH + POOL_WIDTH, CONV_WIDTH)


def _tie(v, token):
    return v if token is None else v + token.tie


def _no_hook(point, after, ready=None):
    return None


def _layer_fwd(x, w, mod, hook=_no_hook):
    s = x.shape[0]
    mod3 = mod.reshape(6, 1, D)
    h = _modnorm_fwd(x, _row(w["g_mix_pre"]), (mod3, 0), (mod3, 1), "mix_pre_fwd")
    hook("pre", h)
    z = _mm(h, w["w_all"], name="mm_in")
    qkv = z[:, Z_QKV:Z_PC].astype(BF16)
    fl = z[:, Z_FL:Z_COLS]
    cum = _cumf_fwd(fl, w["b_f_pad"])
    fr = cum[:, :HEADS].T.reshape(HEADS, s // ATT_K, ATT_K)
    br_a, lse = _attn_fwd(qkv, fr)
    br_b, br_c = _poolconv_fwd(z, w["w_pool_bd"], _tie(_row(w["pool_scale"]), hook("attn", lse)), w["conv_w"])
    hook("pool", br_b)
    wbr = w["w_branch"]
    pa = _mm(br_a, wbr, b_rows=_BR_A, name="mm_br_a")
    pb = _mm(br_b, wbr, b_rows=_BR_B, name="mm_br_b")
    pc = _mm(br_c, wbr, b_rows=_BR_C, name="mm_br_c")
    merged = _merge_fwd(z, pa, pb, pc)
    y = _mm(merged, w["w_out"], name="mm_out")
    x1, h2 = _post_pre_fwd(x, y, _row(w["g_mix_post"]), (mod3, 2), _row(w["g_ff_pre"]), (mod3, 3), (mod3, 4), "mix_post_ff_pre_fwd")
    a, r = _mm(h2, w["w_ff1"], b_split=N_CHIPS, epilogue=_relu2_fwd, out_dtype=(F32, BF16), name="mm_ff1")
    y2 = _mm(r, w["w_ff2"], name="mm_ff2")
    x2 = _post_fwd(x1, y2, _tie(_row(w["g_ff_post"]), hook("ff_post", y2)), (mod3, 5), "ff_post_fwd")
    hook("end", x2)
    saved = dict(x=x, h=h, z=z, qkv=qkv, fl=fl, fr=fr, lse=lse, br_a=br_a, br_b=br_b, br_c=br_c, pa=pa, pb=pb, pc=pc,
                 merged=merged, y=y, x1=x1, h2=h2, a=a, r=r, y2=y2)
    return x2, saved


def _layer_bwd(dx2, sv, w, mod, hook=_no_hook):
    s = dx2.shape[0]
    mod3 = mod.reshape(6, 1, D)
    dy2, sum_ff_post = _post_bwd(dx2, sv["y2"], _row(w["g_ff_post"]), (mod3, 5), "ff_post_bwd")
    (da,) = _mm(dy2, w["w_ff2"], tb=True, epilogue=_relu2_bwd, extras=(sv["a"],), out_dtype=(BF16,), name="mm_ff2_dx")
    d_w_ff2 = _mm(sv["r"], dy2, ta=True, name="mm_ff2_dw")
    dh2 = _mm(da, w["w_ff1"], tb=True, b_split=N_CHIPS, name="mm_ff1_dx")
    d_w_ff1 = _mm(sv["h2"], da, ta=True, out_split=N_CHIPS, name="mm_ff1_dw")
    g_ff_pre = _tie(_row(w["g_ff_pre"]), hook("ff_pre", dh2, dict(w_ff1=d_w_ff1, w_ff2=d_w_ff2)))
    dx1, dy, sum_mid = _pre_post_bwd(dh2, sv["x1"], dx2, g_ff_pre, (mod3, 4), sv["y"], _row(w["g_mix_post"]), (mod3, 2), "ff_pre_mix_post_bwd")
    sum_ff_pre, sum_mix_post = sum_mid, sum_mid[3:]
    dmerged = _mm(dy, w["w_out"], tb=True, name="mm_out_dx")
    d_w_out = _mm(sv["merged"], dy, ta=True, name="mm_out_dw")
    dz, dpa, dpb, dpc = _merge_bwd(dmerged, sv["z"], sv["pa"], sv["pb"], sv["pc"])
    wbr = w["w_branch"]
    dbr_a = _mm(dpa, wbr, tb=True, b_rows=_BR_A, name="mm_br_a_dx")
    dbr_b = _mm(dpb, wbr, tb=True, b_rows=_BR_B, name="mm_br_b_dx")
    dbr_c = _mm(dpc, wbr, tb=True, b_rows=_BR_C, name="mm_br_c_dx")
    d_w_branch = jnp.concatenate([_mm(sv["br_a"], dpa, ta=True, name="mm_br_a_dw"), _mm(sv["br_b"], dpb, ta=True, name="mm_br_b_dw"),
                                  _mm(sv["br_c"], dpc, ta=True, name="mm_br_c_dw")], axis=0)

    dq, dk, dv, dfc, dfr = _attn_bwd(sv["qkv"], dbr_a, sv["br_a"], sv["lse"], sv["fr"])
    dcum = dfc + jnp.pad(dfr.reshape(HEADS, s).T, ((0, 0), (0, 128 - HEADS)))
    dfl, sum_bf = _cumf_bwd(dcum, sv["fl"], _tie(w["b_f_pad"], hook("cumf", dfc)))
    dpc_z, d_wbd, sum_ps, sum_cw = _poolconv_bwd(dbr_b, dbr_c, sv["z"], w["w_pool_bd"], _row(w["pool_scale"]), w["conv_w"])
    for at, part in ((Z_QKV, dq), (Z_QKV + A_WIDTH, dk), (Z_QKV + 2 * A_WIDTH, dv), (Z_PC, dpc_z), (Z_FL, dfl)):
        dz = lax.dynamic_update_slice(dz, part, (0, at))
    dh = _mm(dz, w["w_all"], tb=True, name="mm_in_dx")
    d_w_all = _mm(sv["h"], dz, ta=True, name="mm_in_dw")
    hook("mix_pre", dh)
    dx, sum_mix_pre = _modnorm_bwd(dh, sv["x"], dx1, _row(w["g_mix_pre"]), (mod3, 1), "mix_pre_bwd")

    dmod = jnp.stack([sum_mix_pre[0], sum_mix_pre[1], sum_mix_post[0], sum_ff_pre[0], sum_ff_pre[1], sum_ff_post[0]])
    d_w_in = _w_in_shards(d_w_all)
    d_w_pool = jnp.stack([d_wbd[64 * g:64 * g + 64, 64 * g:64 * g + 64] for g in range(4)])
    big = dict(w_in=d_w_in, w_branch=d_w_branch, w_out=d_w_out, w_ff1=d_w_ff1, w_ff2=d_w_ff2)
    small = dict(g_mix_pre=sum_mix_pre[2], g_mix_post=sum_mix_post[1], g_ff_pre=sum_ff_pre[2], g_ff_post=sum_ff_post[1],
                 b_f=sum_bf[0, :HEADS], w_pool=d_w_pool, pool_scale=sum_ps[0], conv_w=sum_cw[0:3])
    return dx, dmod, big, small


_QKV_END, _FL_END, _PC_END = 3 * A_WIDTH, 3 * A_WIDTH + HEADS, 3 * A_WIDTH + HEADS + POOL_WIDTH + 3 * CONV_WIDTH
_W_IN_GROUPS = ((_PC_END, IN_COLS, Z_GL), (0, _QKV_END, Z_QKV), (_FL_END, _PC_END, Z_PC), (_QKV_END, _FL_END, Z_FL))
_SHARD_COLS = IN_COLS // N_CHIPS


def _w_all_from_shards(blocks):
    pieces = []
    for lo, hi, _ in _W_IN_GROUPS:
        for p in range(N_CHIPS):
            a, b = max(lo, p * _SHARD_COLS), min(hi, (p + 1) * _SHARD_COLS)
            if a < b:
                pieces.append(blocks[p][:, a - p * _SHARD_COLS:b - p * _SHARD_COLS])
    pieces.append(jnp.zeros((D, Z_COLS - IN_COLS), blocks.dtype))
    return jnp.concatenate(pieces, axis=1)


def _w_in_shards(d_w_all):
    blocks = []
    for p in range(N_CHIPS):
        pieces = []
        for lo, hi, at in sorted(_W_IN_GROUPS):
            a, b = max(lo, p * _SHARD_COLS), min(hi, (p + 1) * _SHARD_COLS)
            if a < b:
                pieces.append(d_w_all[:, at + a - lo:at + b - lo])
        blocks.append(jnp.concatenate(pieces, axis=1))
    return jnp.stack(blocks)


def _full_layer_weights(w_in_blocks, w_branch, w_out, w_ff1, w_ff2, g_mix_pre, g_mix_post, g_ff_pre, g_ff_post, b_f, w_pool, pool_scale, conv_w):
    w_all = None if w_in_blocks is None else _w_all_from_shards(w_in_blocks)
    wbd = (w_pool[:, :, None, :] * jnp.eye(4, dtype=F32)[:, None, :, None]).reshape(POOL_WIDTH, POOL_WIDTH)
    return dict(w_all=w_all, w_branch=w_branch, w_out=w_out, w_ff1=w_ff1, w_ff2=w_ff2, g_mix_pre=g_mix_pre, g_mix_post=g_mix_post,
                g_ff_pre=g_ff_pre, g_ff_post=g_ff_post, b_f_pad=jnp.pad(b_f, (0, 128 - HEADS)).reshape(1, 128), w_pool_bd=wbd,
                pool_scale=pool_scale, conv_w=conv_w)


class _NoComm:
    def layer_weights(self, l):
        raise NotImplementedError

    def fwd_hook(self, l):
        return _no_hook

    def bwd_hook(self, l):
        return _no_hook

    def grads_ready(self, l, big):
        return None


class _Layers(_NoComm):
    def __init__(self, layers):
        self.layers = layers

    def layer_weights(self, l):
        return self.layers[l]


def _local_step(x, target, mods, comm):
    saved, weights = [], []
    act = x
    for l in range(DEPTH):
        weights.append(comm.layer_weights(l))
        act, sv = _layer_fwd(act, weights[l], mods[l], comm.fwd_hook(l))
        saved.append(sv)
    dact, sq = _loss_head(act, target)
    loss = sq[0, 0] * (0.5 / D)
    dmods, bigs, smalls = [None] * DEPTH, [None] * DEPTH, [None] * DEPTH
    token = None
    for l in reversed(range(DEPTH)):
        dact, dmods[l], bigs[l], smalls[l] = _layer_bwd(dact, saved[l], weights[l], _tie(mods[l], token), comm.bwd_hook(l))
        token = comm.grads_ready(l, bigs[l])
    return loss, dact, jnp.stack(dmods), bigs, smalls


_BIG = ("w_in", "w_branch", "w_out", "w_ff1", "w_ff2")


class _GatherJob:
    def __init__(self, tag, shards, after):
        self.tag, self.n = tag, len(shards)
        lands = [lax.empty((N_CHIPS,) + s.shape, s.dtype) for s in shards]
        self.state = _copies_start(list(shards) + lands, _plan_gather_ici, 4 * self.n, after, "gather_ici_start_" + tag)
        self.token = self.state[3]

    def pass_on(self, after):
        bufs = _copies_wait(self.state, _plan_gather_ici, after, "gather_ici_wait_" + self.tag)
        self.state = _copies_start(bufs[self.n:], _plan_gather_d2d, 3 * self.n, bufs[0], "gather_d2d_start_" + self.tag)
        self.token = self.state[3]
        return self.token

    def done(self, after):
        return _copies_wait(self.state, _plan_gather_d2d, after, "gather_d2d_wait_" + self.tag)


class _ReduceJob:
    def __init__(self, tag, names, grads, sel, after, layer, into=None):
        self.tag, self.names, self.n, self.sel, self.layer, self.into = tag, names, len(names), sel, layer, into or {}
        lands = [lax.empty((N_CHIPS, g.shape[1] // 2, g.shape[2]), F32) for g in grads]
        self.state = _copies_start(list(grads) + lands, _plan_rs_sibling, self.n, after, "rs_sibling_start_" + tag)
        self.token = self.state[3]

    def chip_sums(self, after):
        bufs = _copies_wait(self.state, _plan_rs_sibling, after, "rs_sibling_wait_" + self.tag)
        wires, self.owns = zip(*[_chip_sum(bufs[i], bufs[self.n + i], self.sel, "rs_chip_sum_" + name) for i, name in enumerate(self.names)])
        lands = [lax.empty((3,) + w.shape[1:], BF16) for w in wires]
        self.state = _copies_start(list(wires) + lands, _plan_rs_chips, 3 * self.n, self.owns[0], "rs_chips_start_" + self.tag)
        self.token = self.state[3]
        return self.token

    def final_sums(self, after):
        bufs = _copies_wait(self.state, _plan_rs_chips, after, "rs_chips_wait_" + self.tag)
        sums = [_final_sum(self.owns[i], bufs[self.n + i], self.sel, self.layer, self.into.get(name), "rs_final_" + name)
                for i, name in enumerate(self.names)]
        self.state = _copies_start(sums, _plan_rs_share(self.layer), self.n, sums[0], "rs_share_start_" + self.tag)
        self.token = self.state[3]
        return self.token

    def done(self, after):
        return dict(zip(self.names, _copies_wait(self.state, _plan_rs_share(self.layer), after, "rs_share_wait_" + self.tag)))


def _chip_blocks(g):
    return g if g.ndim == 3 else g.reshape(N_CHIPS, -1, g.shape[1])


class _StepComm(_NoComm):
    def __init__(self, big_weights, w_in0, sel, after):
        self.sel = sel
        self.small, self.grads, self.jobs = None, {}, {}
        self.jobs["in0"] = _GatherJob("in0", [w_in0], after)
        later = lax.optimization_barrier((tuple(big_weights), self.jobs["in0"].token.after))[0]
        self.jobs["rest0"] = _GatherJob("rest0", [w[0].astype(BF16) for w in later[1:]], self.jobs["in0"].token)
        self.jobs["all1"] = _GatherJob("all1", [w[1].astype(BF16) for w in later], self.jobs["rest0"].token)

    def layer_weights(self, l):
        if l == 0:
            self.weights0 = _full_layer_weights(None, None, None, None, None, *self.small[0])
            return self.weights0
        g_in, g_br, g_out, g_f1, g_f2 = self.landed1
        return _full_layer_weights(g_in, g_br.reshape(D, D), g_out.reshape(D, D), g_f1, g_f2.reshape(D_FF, D), *self.small[1])

    def fwd_hook(self, l):
        if l != 0:
            return _no_hook

        def hook(point, after, ready=None):
            if point == "pre":
                job = self.jobs["in0"]
                started = after[:8, :128].astype(F32) + self.jobs["all1"].token.after
                self.weights0["w_all"] = _w_all_from_shards(job.done(job.pass_on(started))[0])
            if point == "attn":
                return self.jobs["rest0"].pass_on(after)
            if point == "ff_post":
                return self.jobs["all1"].pass_on(after)
            if point == "pool":
                g_br, g_out, g_f1, g_f2 = self.jobs["rest0"].done(after)
                self.weights0.update(w_branch=g_br.reshape(D, D), w_out=g_out.reshape(D, D), w_ff1=g_f1, w_ff2=g_f2.reshape(D_FF, D))
            if point == "end":
                self.landed1 = self.jobs["all1"].done(after)
            return None
        return hook

    def bwd_hook(self, l):
        if l != 0:
            return _no_hook

        def hook(point, after, ready=None):
            jobs = self.jobs
            if point == "ff_pre":
                token = jobs["rs1"].chip_sums(after)
                jobs["rs0_ff"] = _ReduceJob("0_ff", ("w_ff1", "w_ff2"), [_chip_blocks(ready[n]) for n in ("w_ff1", "w_ff2")], self.sel, token, 0)
                return jobs["rs0_ff"].token
            if point == "cumf":
                return jobs["rs0_ff"].chip_sums(jobs["rs1"].final_sums(after))
            self.layer1 = jobs["rs1"].done(after)
            jobs["rs0_ff"].into = self.layer1
            return None
        return hook

    def grads_ready(self, l, big):
        if l == 1:
            self.jobs["rs1"] = _ReduceJob("1", _BIG, [_chip_blocks(big[n]) for n in _BIG], self.sel, self.sel, 1)
            return self.jobs["rs1"].token
        names = ("w_in", "w_branch", "w_out")
        self.jobs["rs0_mix"] = _ReduceJob("0_mix", names, [_chip_blocks(big[n]) for n in names], self.sel, self.sel, 0, self.layer1)
        return self.jobs["rs0_mix"].token

    def finish_sums(self, after):
        jobs = self.jobs
        token = jobs["rs0_mix"].chip_sums(after)
        return jobs["rs0_ff"].final_sums(token)

    def finish_ff(self, after):
        self.grads.update(self.jobs["rs0_ff"].done(after))

    def finish_mix(self, after):
        job = self.jobs["rs0_mix"]
        self.grads.update(job.done(job.final_sums(after)))


_SMALL = ("g_mix_pre", "g_mix_post", "g_ff_pre", "g_ff_post", "b_f", "w_pool", "pool_scale", "conv_w")


def _w_in_view(t):
    return t.reshape(DEPTH, D // 128, 128, _SHARD_COLS).transpose(3, 1, 0, 2).reshape(_SHARD_COLS * (D // 128) * DEPTH, 128)


def _w_in_unview(t):
    return t.reshape(_SHARD_COLS, D // 128, DEPTH, 128).transpose(2, 1, 3, 0).reshape(DEPTH, D, _SHARD_COLS)


def _pack(parts, rows=8):
    flat = jnp.concatenate([p.reshape(-1) for p in parts])
    width = -(-flat.shape[0] // (rows * 128)) * 128
    return jnp.pad(flat, (0, rows * width - flat.shape[0])).reshape(rows, width)


def _unpack(packed, like):
    flat = packed.reshape(-1)
    out, at = [], 0
    for ref in like:
        out.append(flat[at:at + ref.size].reshape(ref.shape))
        at += ref.size
    return out


def kernel(x, c, w_ada, b_ada, g_mix_pre, g_mix_post, g_ff_pre, g_ff_post, w_in, b_f, w_pool, pool_scale, conv_w, w_branch, w_out, w_ff1, w_ff2, loss_target, m_w_ada, m_b_ada, m_g_mix_pre, m_g_mix_post, m_g_ff_pre, m_g_ff_post, m_w_in, m_b_f, m_w_pool, m_pool_scale, m_conv_w, m_w_branch, m_w_out, m_w_ff1, m_w_ff2, v_w_ada, v_b_ada, v_g_mix_pre, v_g_mix_post, v_g_ff_pre, v_g_ff_post, v_w_in, v_b_f, v_w_pool, v_pool_scale, v_conv_w, v_w_branch, v_w_out, v_w_ff1, v_w_ff2):
    xi, yi, ci = lax.axis_index("x"), lax.axis_index("y"), lax.axis_index("c")
    chip = 2 * xi + yi
    dev = 2 * chip + ci
    n_ada = w_ada.shape[2]

    first = jnp.zeros((8, D + 384), F32).at[0, :D].set(c[0]).at[0, D:].set(conv_w.reshape(-1))
    w_in0 = w_in[0].astype(BF16)
    got = _allgather8(first, "gather_cond", after=(w_in0,)).reshape(N_DEV, 8, D + 384)[:, 0]
    c_all = got[:, :D]
    conv_full = got[0::2, D:].reshape(N_CHIPS, DEPTH, 3, CONV_WIDTH // N_CHIPS).transpose(1, 2, 0, 3).reshape(DEPTH, 3, CONV_WIDTH)

    b_loc = lax.dynamic_slice_in_dim(b_ada, chip * n_ada, n_ada, axis=1).reshape(DEPTH, 1, n_ada)
    mod_cols, silu_c = _ada_fwd(c_all, w_ada, b_loc)
    got = _allgather8(mod_cols.reshape(DEPTH * N_DEV, n_ada), "gather_mod").reshape(N_DEV, DEPTH, N_DEV, n_ada)[0::2]
    mod_all = got.transpose(1, 2, 0, 3).reshape(DEPTH, N_DEV, 6, D)
    mods = lax.dynamic_index_in_dim(mod_all, dev, axis=1, keepdims=False)

    comm = _StepComm((w_in, w_branch, w_out, w_ff1, w_ff2), w_in0, jnp.stack([ci, chip]).astype(jnp.int32), mods)
    comm.small = [(g_mix_pre[l], g_mix_post[l], g_ff_pre[l], g_ff_post[l], b_f[l], w_pool[l], pool_scale[l], conv_full[l]) for l in range(DEPTH)]
    loss_part, grad_x, dmods, bigs, smalls = _local_step(x[0], loss_target[0], mods, comm)

    small_parts = [smalls[l][name] for name in _SMALL for l in range(DEPTH)] + [loss_part.reshape(1)]
    packed = _tie(_pack([dmods] + small_parts), comm.jobs["rs0_mix"].token)
    gathered = _allgather8(packed, "gather_small")
    dmod_all = gathered.reshape(N_DEV, -1)[:, :dmods.size].reshape(N_DEV, DEPTH, 6 * D)
    summed = _unpack(_sum_devices(gathered), [dmods] + small_parts)
    grad_b_ada = summed[0].reshape(DEPTH, 6 * D)
    loss = summed[-1][0]
    small_grads = {name: jnp.stack(summed[1 + 2 * i:3 + 2 * i]) for i, name in enumerate(_SMALL)}
    small_grads["conv_w"] = lax.dynamic_slice_in_dim(small_grads["conv_w"], chip * (CONV_WIDTH // N_CHIPS), CONV_WIDTH // N_CHIPS, axis=2)

    dmod_loc = lax.dynamic_slice_in_dim(dmod_all.transpose(1, 0, 2), chip * n_ada, n_ada, axis=2)
    tail_token = comm.finish_sums(grad_b_ada)
    silu_pad = _tie(jnp.pad(silu_c, ((0, 128 - N_DEV), (0, 0))), tail_token)
    dmod_pad = jnp.pad(dmod_loc.transpose(1, 0, 2).reshape(N_DEV, DEPTH * n_ada), ((0, 128 - N_DEV), (0, 0)))
    grad_w_ada = _mm(silu_pad, dmod_pad, ta=True, out_split=DEPTH, name="mm_ada_dw")

    grads = dict(w_ada=grad_w_ada, b_ada=grad_b_ada, **small_grads)
    weights = dict(w_ada=w_ada, b_ada=b_ada, g_mix_pre=g_mix_pre, g_mix_post=g_mix_post, g_ff_pre=g_ff_pre, g_ff_post=g_ff_post, w_in=w_in,
                   b_f=b_f, w_pool=w_pool, pool_scale=pool_scale, conv_w=conv_w, w_branch=w_branch, w_out=w_out, w_ff1=w_ff1, w_ff2=w_ff2)
    m_in = dict(w_ada=m_w_ada, b_ada=m_b_ada, g_mix_pre=m_g_mix_pre, g_mix_post=m_g_mix_post, g_ff_pre=m_g_ff_pre, g_ff_post=m_g_ff_post,
                w_in=m_w_in, b_f=m_b_f, w_pool=m_w_pool, pool_scale=m_pool_scale, conv_w=m_conv_w, w_branch=m_w_branch, w_out=m_w_out,
                w_ff1=m_w_ff1, w_ff2=m_w_ff2)
    v_in = dict(w_ada=v_w_ada, b_ada=v_b_ada, g_mix_pre=v_g_mix_pre, g_mix_post=v_g_mix_post, g_ff_pre=v_g_ff_pre, g_ff_post=v_g_ff_post,
                w_in=v_w_in, b_f=v_b_f, w_pool=v_w_pool, pool_scale=v_pool_scale, conv_w=v_conv_w, w_branch=v_w_branch, w_out=v_w_out,
                w_ff1=v_w_ff1, w_ff2=v_w_ff2)
    order = ("w_ada", "b_ada", "g_mix_pre", "g_mix_post", "g_ff_pre", "g_ff_post", "w_in", "b_f", "w_pool", "pool_scale", "conv_w",
             "w_branch", "w_out", "w_ff1", "w_ff2")
    delta, new_m, new_v = {}, {}, {}
    tiny = ("b_ada",) + _SMALL
    tiny_g = [_tie(grads[tiny[0]], tail_token)] + [grads[name] for name in tiny[1:]]
    res = _adamw_many([weights[name] for name in tiny], tiny_g, [m_in[name] for name in tiny], [v_in[name] for name in tiny], "adamw_small")
    for out, vals in zip((delta, new_m, new_v), res):
        out.update(zip(tiny, vals))
    delta["w_ada"], new_m["w_ada"], new_v["w_ada"] = _adamw(w_ada, grad_w_ada, m_w_ada, v_w_ada, "adamw_w_ada")
    comm.finish_ff(delta["w_ada"][0, :8, :128] + delta["b_ada"][0, :128])
    for name in ("w_ff1", "w_ff2", "w_in", "w_branch", "w_out"):
        if name == "w_in":
            comm.finish_mix(delta["w_ff2"][0, :8, :128])
        grads[name] = comm.grads[name]
        if name == "w_in":
            g_view = lax.optimization_barrier(_w_in_view(grads[name]))
            res = _adamw(_w_in_view(w_in), g_view, _w_in_view(m_w_in), _w_in_view(v_w_in), "adamw_w_in")
            grads[name], delta[name], new_m[name], new_v[name] = [_w_in_unview(t) for t in (g_view, *res)]
        else:
            delta[name], new_m[name], new_v[name] = _adamw(weights[name], grads[name], m_in[name], v_in[name], "adamw_" + name)

    return (loss, grad_x[None], *[grads[n] for n in order], *[delta[n] for n in order], *[new_m[n] for n in order],
            *[new_v[n] for n in order])
```

```python
from typing import NamedTuple

import jax
import jax.numpy as jnp
from jax import lax
from jax.experimental import pallas as pl
from jax.experimental.pallas import tpu as pltpu

F32 = jnp.float32
BF16 = jnp.bfloat16
MESH = pl.DeviceIdType.MESH

D = 1024
DEPTH = 2
HEADS = 8
HEAD_DIM = 64
A_WIDTH = 512
POOL_WIDTH = 256
CONV_WIDTH = 256
D_FF = 4096
IN_COLS = 5640
Z_GL, Z_QKV, Z_PC, Z_FL, Z_COLS = 0, 3072, 4608, 5632, 5760
RMS_EPS = 1e-6
NEG_INF = -1e30
ROW_TILE = 512
EW_ROWS = 256
N_CHIPS = 4
N_DEV = 8
V7X_VMEM_LIMIT = 48 * 1024 * 1024

ADAM_LR = 0.001
ADAM_B1 = 0.9
ADAM_B2 = 0.999
ADAM_EPS = 1e-08
ADAM_WD = 0.01
ADAM_STEP = 10

_HBM = pl.BlockSpec(memory_space=pltpu.HBM)


def _params(*sem):
    return pltpu.CompilerParams(dimension_semantics=sem, vmem_limit_bytes=V7X_VMEM_LIMIT)


def _pick(dim, cands):
    for cand in cands:
        if dim % cand == 0:
            return cand
    return dim


MM_TILE_BUDGET = 39 * 1024 * 1024


def _mm_tiles(m, n, k, k_unit, tn, a_size, b_size, out_size):
    for tk in (k_unit, 2048, 1152, 1024, 640, 512, 256, 128):
        if k_unit % tk:
            continue
        for tm in (2048, 1024, 512, 256, 128):
            if m % tm or ((m // tm) * (n // tn) < 2 and tm > 512):
                continue
            need = 2 * (tm * tk * a_size + tk * tn * b_size + tm * tn * out_size) + (0 if tk == k else 4 * tm * tn)
            if need <= MM_TILE_BUDGET and (tk == k_unit or tm >= 512):
                return tm, tk
    return 128, 128


def _mm(a, b, *, ta=False, tb=False, b_rows=None, b_split=1, out_split=1, out_dtype=F32, epilogue=None, extras=(), name):
    (k, m) = a.shape if ta else a.shape[::-1]
    b_row0, b_rows = (0, b.shape[-2]) if b_rows is None else b_rows
    b_cols = b.shape[-1] * b_split
    (n, k2) = (b_rows, b_cols) if tb else (b_cols, b_rows)
    assert k == k2, (a.shape, b.shape, ta, tb)
    n_unit = n // (out_split * (1 if tb else b_split))
    k_unit = k // (b_split if tb else 1)
    tn = _pick(n_unit, (1024, 1152, 768, 640, 512, 256, 128))
    tm, tk = _mm_tiles(m, n, k, k_unit, tn, a.dtype.itemsize, b.dtype.itemsize,
                       sum(jnp.dtype(dt).itemsize for dt in out_dtype) + 4 * len(extras) if epilogue else jnp.dtype(out_dtype).itemsize)
    nk = k // tk
    dims = (((0 if ta else 1,), (1 if tb else 0,)), ((), ()))

    def dot(a_ref, b_ref):
        b_val = b_ref[0] if b_split > 1 else b_ref[...]
        return lax.dot_general(a_ref[...].astype(BF16), b_val.astype(BF16), dims, preferred_element_type=F32)

    n_extra = len(extras)
    assert epilogue is None or out_split == 1

    def put(refs, val):
        if epilogue is not None:
            for o_ref, res in zip(refs[n_extra:], epilogue(val, *[r[...] for r in refs[:n_extra]])):
                o_ref[...] = res.astype(o_ref.dtype)
        elif out_split > 1:
            refs[0][0] = val.astype(refs[0].dtype)
        else:
            refs[0][...] = val.astype(refs[0].dtype)

    def body_single(a_ref, b_ref, *refs):
        put(refs, dot(a_ref, b_ref))

    def body_acc(a_ref, b_ref, *refs):
        kk = pl.program_id(2)
        acc_ref = refs[-1]

        @pl.when(kk == 0)
        def _():
            acc_ref[...] = jnp.zeros_like(acc_ref)

        acc_ref[...] += dot(a_ref, b_ref)

        @pl.when(kk == nk - 1)
        def _():
            put(refs[:-1], acc_ref[...])

    a_spec = pl.BlockSpec((tk, tm), lambda i, j, kk: (kk, i)) if ta else pl.BlockSpec((tm, tk), lambda i, j, kk: (i, kk))
    if b_split == 1:
        off = b_row0 // (tn if tb else tk)
        assert off * (tn if tb else tk) == b_row0
        b_spec = pl.BlockSpec((tn, tk), lambda i, j, kk: (j + off, kk)) if tb else pl.BlockSpec((tk, tn), lambda i, j, kk: (kk + off, j))
    elif tb:
        per = k_unit // tk
        b_spec = pl.BlockSpec((1, tn, tk), lambda i, j, kk: (kk // per, j, kk % per))
    else:
        per = n // b_split // tn
        b_spec = pl.BlockSpec((1, tk, tn), lambda i, j, kk: (j // per, kk, j % per))
    if out_split == 1:
        o_spec = pl.BlockSpec((tm, tn), lambda i, j, kk: (i, j))
        o_shape = None if epilogue is not None else jax.ShapeDtypeStruct((m, n), out_dtype)
    else:
        per_o = n // out_split // tn
        o_spec = pl.BlockSpec((1, tm, tn), lambda i, j, kk: (j // per_o, i, j % per_o))
        o_shape = jax.ShapeDtypeStruct((out_split, m, n // out_split), out_dtype)
    if epilogue is not None:
        o_shape = [jax.ShapeDtypeStruct((m, n), dt) for dt in out_dtype]
        o_spec = [o_spec] * len(out_dtype)
    return pl.pallas_call(
        body_single if nk == 1 else body_acc, name=name, grid=(m // tm, n // tn, nk),
        in_specs=[a_spec, b_spec] + [pl.BlockSpec((tm, tn), lambda i, j, kk: (i, j))] * n_extra, out_specs=o_spec, out_shape=o_shape,
        scratch_shapes=[] if nk == 1 else [pltpu.VMEM((tm, tn), F32)],
        compiler_params=_params("parallel", "parallel", "arbitrary"),
    )(a, b, *extras)


def _ew(fn, ins, out_dtypes, name, tc=None):
    shape = ins[0].shape
    lead, (rows, cols) = shape[:-2], shape[-2:]
    tc = cols if tc is None else tc
    if tc > 1024:
        tr = _pick(rows, (EW_ROWS, 128, 8))
    elif tc > 128:
        tr = _pick(rows, (2 * EW_ROWS, EW_ROWS, 128, 8))
    else:
        tr = _pick(rows, (4096, 2256, 2048, 1024, EW_ROWS, 8))
    n_in = len(ins)

    def body(*refs):
        res = fn(*[r[...] for r in refs[:n_in]])
        for o_ref, val in zip(refs[n_in:], res):
            o_ref[...] = val.astype(o_ref.dtype)

    if lead:
        spec = pl.BlockSpec((None, tr, tc), lambda l, i, j: (l, i, j))
    else:
        spec = pl.BlockSpec((tr, tc), lambda i, j: (i, j))
    return pl.pallas_call(
        body, name=name, grid=lead + (rows // tr, cols // tc),
        in_specs=[spec] * n_in, out_specs=[spec] * len(out_dtypes),
        out_shape=[jax.ShapeDtypeStruct(shape, dt) for dt in out_dtypes],
        compiler_params=_params(*(["parallel"] * (len(lead) + 2))),
    )(*ins)


def _relu2_fwd(a):
    r = jnp.maximum(a, 0.0)
    return a, r * r


def _relu2_bwd(dr, a):
    return (dr * (2.0 * jnp.maximum(a, 0.0)),)


def _adamw_math(w, g, m, v):
    m = ADAM_B1 * m + (1.0 - ADAM_B1) * g
    v = ADAM_B2 * v + (1.0 - ADAM_B2) * (g * g)
    m_hat = m / (1.0 - ADAM_B1 ** ADAM_STEP)
    v_hat = v / (1.0 - ADAM_B2 ** ADAM_STEP)
    delta = -ADAM_LR * (m_hat / (jnp.sqrt(v_hat) + ADAM_EPS) + ADAM_WD * w)
    return delta, m, v


def _adamw(w, g, m, v, name):
    return _ew(_adamw_math, [w, g, m, v], [F32, F32, F32], name)


def _adamw_many(ws, gs, ms, vs, name):
    n = len(ws)

    def body(*refs):
        for i in range(n):
            res = _adamw_math(*[refs[k * n + i][...] for k in range(4)])
            for k in range(3):
                refs[(4 + k) * n + i][...] = res[k]

    outs = pl.pallas_call(
        body, name=name, out_shape=[jax.ShapeDtypeStruct(w.shape, F32) for w in ws] * 3,
        compiler_params=pltpu.CompilerParams(vmem_limit_bytes=V7X_VMEM_LIMIT),
    )(*ws, *gs, *ms, *vs)
    return outs[:n], outs[n:2 * n], outs[2 * n:]


def _row_spec(cols, block=0):
    return pl.BlockSpec((ROW_TILE, cols), lambda i, block=block: (i, block))


def _vec_spec(cols):
    return pl.BlockSpec((1, cols), lambda i: (0, 0))


def _vec_args(*vecs):
    arrays = [v[0] if isinstance(v, tuple) else v for v in vecs]
    specs = [pl.BlockSpec((None, 1, D), lambda i, row=v[1]: (row, 0, 0)) if isinstance(v, tuple) else _vec_spec(D) for v in vecs]
    return arrays, specs


def _sum_spec(cols):
    return pl.BlockSpec((8, cols), lambda i: (0, 0))


def _rstd(x):
    return lax.rsqrt(jnp.mean(x * x, axis=-1, keepdims=True) + RMS_EPS)


def _modnorm_fwd(x, g, shift, scale, name):
    s = x.shape[0]

    def body(x_ref, g_ref, sh_ref, sc_ref, h_ref):
        xv = x_ref[...]
        n = xv * _rstd(xv)
        h_ref[...] = ((n * g_ref[...]) * (1.0 + sc_ref[...]) + sh_ref[...]).astype(BF16)

    vecs, vec_specs = _vec_args(g, shift, scale)
    return pl.pallas_call(
        body, name=name, grid=(s // ROW_TILE,),
        in_specs=[_row_spec(D)] + vec_specs, out_specs=_row_spec(D),
        out_shape=jax.ShapeDtypeStruct((s, D), BF16), compiler_params=_params("parallel"),
    )(x, *vecs)


def _post_fwd(x, y, g, gate, name):
    s = x.shape[0]

    def body(x_ref, y_ref, g_ref, gate_ref, o_ref):
        yv = y_ref[...]
        o_ref[...] = x_ref[...] + gate_ref[...] * ((yv * _rstd(yv)) * g_ref[...])

    vecs, vec_specs = _vec_args(g, gate)
    return pl.pallas_call(
        body, name=name, grid=(s // ROW_TILE,),
        in_specs=[_row_spec(D), _row_spec(D)] + vec_specs, out_specs=_row_spec(D),
        out_shape=jax.ShapeDtypeStruct((s, D), F32), compiler_params=_params("parallel"),
    )(x, y, *vecs)


def _post_bwd(dxo, y, g, gate, name):
    s = dxo.shape[0]

    def body(d_ref, y_ref, g_ref, gate_ref, dy_ref, sum_ref):
        @pl.when(pl.program_id(0) == 0)
        def _():
            sum_ref[...] = jnp.zeros_like(sum_ref)

        dv, yv = d_ref[...], y_ref[...]
        r = _rstd(yv)
        n = yv * r
        sum_ref[0:1, :] += jnp.sum(dv * (n * g_ref[...]), axis=0, keepdims=True)
        sum_ref[1:2, :] += jnp.sum((dv * gate_ref[...]) * n, axis=0, keepdims=True)
        dn = (dv * gate_ref[...]) * g_ref[...]
        dy_ref[...] = (r * (dn - n * jnp.mean(dn * n, axis=-1, keepdims=True))).astype(BF16)

    vecs, vec_specs = _vec_args(g, gate)
    return pl.pallas_call(
        body, name=name, grid=(s // ROW_TILE,),
        in_specs=[_row_spec(D), _row_spec(D)] + vec_specs,
        out_specs=[_row_spec(D), _sum_spec(D)],
        out_shape=[jax.ShapeDtypeStruct((s, D), BF16), jax.ShapeDtypeStruct((8, D), F32)],
        compiler_params=_params("arbitrary"),
    )(dxo, y, *vecs)


def _modnorm_bwd(dh, x, dxo, g, scale, name):
    s = dh.shape[0]

    def body(dh_ref, x_ref, d_ref, g_ref, sc_ref, dx_ref, sum_ref):
        @pl.when(pl.program_id(0) == 0)
        def _():
            sum_ref[...] = jnp.zeros_like(sum_ref)

        dhv, xv = dh_ref[...], x_ref[...]
        r = _rstd(xv)
        n = xv * r
        one_sc = 1.0 + sc_ref[...]
        sum_ref[0:1, :] += jnp.sum(dhv, axis=0, keepdims=True)
        sum_ref[1:2, :] += jnp.sum(dhv * (n * g_ref[...]), axis=0, keepdims=True)
        sum_ref[2:3, :] += jnp.sum((dhv * one_sc) * n, axis=0, keepdims=True)
        dn = (dhv * one_sc) * g_ref[...]
        dx_ref[...] = d_ref[...] + r * (dn - n * jnp.mean(dn * n, axis=-1, keepdims=True))

    vecs, vec_specs = _vec_args(g, scale)
    return pl.pallas_call(
        body, name=name, grid=(s // ROW_TILE,),
        in_specs=[_row_spec(D), _row_spec(D), _row_spec(D)] + vec_specs,
        out_specs=[_row_spec(D), _sum_spec(D)],
        out_shape=[jax.ShapeDtypeStruct((s, D), F32), jax.ShapeDtypeStruct((8, D), F32)],
        compiler_params=_params("arbitrary"),
    )(dh, x, dxo, *vecs)


def _post_pre_fwd(x, y, g_post, gate, g_pre, shift, scale, name):
    s = x.shape[0]

    def body(x_ref, y_ref, gp_ref, gate_ref, g_ref, sh_ref, sc_ref, o_ref, h_ref):
        yv = y_ref[...]
        xo = x_ref[...] + gate_ref[...] * ((yv * _rstd(yv)) * gp_ref[...])
        o_ref[...] = xo
        h_ref[...] = (((xo * _rstd(xo)) * g_ref[...]) * (1.0 + sc_ref[...]) + sh_ref[...]).astype(BF16)

    vecs, vec_specs = _vec_args(g_post, gate, g_pre, shift, scale)
    return pl.pallas_call(
        body, name=name, grid=(s // ROW_TILE,),
        in_specs=[_row_spec(D), _row_spec(D)] + vec_specs, out_specs=[_row_spec(D), _row_spec(D)],
        out_shape=[jax.ShapeDtypeStruct((s, D), F32), jax.ShapeDtypeStruct((s, D), BF16)], compiler_params=_params("parallel"),
    )(x, y, *vecs)


def _pre_post_bwd(dh, x, dxo, g_pre, scale, y, g_post, gate, name):
    s = dh.shape[0]

    def body(dh_ref, x_ref, d_ref, y_ref, g_ref, sc_ref, gp_ref, gate_ref, dx_ref, dy_ref, sum_ref):
        @pl.when(pl.program_id(0) == 0)
        def _():
            sum_ref[...] = jnp.zeros_like(sum_ref)

        dhv, xv = dh_ref[...], x_ref[...]
        r = _rstd(xv)
        n = xv * r
        one_sc = 1.0 + sc_ref[...]
        sum_ref[0:1, :] += jnp.sum(dhv, axis=0, keepdims=True)
        sum_ref[1:2, :] += jnp.sum(dhv * (n * g_ref[...]), axis=0, keepdims=True)
        sum_ref[2:3, :] += jnp.sum((dhv * one_sc) * n, axis=0, keepdims=True)
        dn = (dhv * one_sc) * g_ref[...]
        dv = d_ref[...] + r * (dn - n * jnp.mean(dn * n, axis=-1, keepdims=True))
        dx_ref[...] = dv

        yv = y_ref[...]
        ry = _rstd(yv)
        ny = yv * ry
        sum_ref[3:4, :] += jnp.sum(dv * (ny * gp_ref[...]), axis=0, keepdims=True)
        sum_ref[4:5, :] += jnp.sum((dv * gate_ref[...]) * ny, axis=0, keepdims=True)
        dny = (dv * gate_ref[...]) * gp_ref[...]
        dy_ref[...] = (ry * (dny - ny * jnp.mean(dny * ny, axis=-1, keepdims=True))).astype(BF16)

    vecs, vec_specs = _vec_args(g_pre, scale, g_post, gate)
    return pl.pallas_call(
        body, name=name, grid=(s // ROW_TILE,),
        in_specs=[_row_spec(D)] * 4 + vec_specs,
        out_specs=[_row_spec(D), _row_spec(D), _sum_spec(D)],
        out_shape=[jax.ShapeDtypeStruct((s, D), F32), jax.ShapeDtypeStruct((s, D), BF16), jax.ShapeDtypeStruct((8, D), F32)],
        compiler_params=_params("arbitrary"),
    )(dh, x, dxo, y, *vecs)


def _loss_head(y, target):
    s = y.shape[0]

    def body(y_ref, t_ref, dy_ref, sum_ref):
        @pl.when(pl.program_id(0) == 0)
        def _():
            sum_ref[...] = jnp.zeros_like(sum_ref)

        err = y_ref[...] - t_ref[...]
        dy_ref[...] = err * (1.0 / D)
        sum_ref[...] += jnp.sum(err * err)

    return pl.pallas_call(
        body, name="loss_head", grid=(s // ROW_TILE,),
        in_specs=[_row_spec(D), _row_spec(D)],
        out_specs=[_row_spec(D), pl.BlockSpec((8, 128), lambda i: (0, 0))],
        out_shape=[jax.ShapeDtypeStruct((s, D), F32), jax.ShapeDtypeStruct((8, 128), F32)],
        compiler_params=_params("arbitrary"),
    )(y, target)


def _merge_fwd(z, pa, pb, pc):
    s = z.shape[0]

    def body(g0_ref, g1_ref, g2_ref, pa_ref, pb_ref, pc_ref, o_ref):
        o_ref[...] = (jax.nn.sigmoid(g0_ref[...]) * pa_ref[...] + jax.nn.sigmoid(g1_ref[...]) * pb_ref[...]
                      + jax.nn.sigmoid(g2_ref[...]) * pc_ref[...]).astype(BF16)

    return pl.pallas_call(
        body, name="merge_fwd", grid=(s // ROW_TILE,),
        in_specs=[_row_spec(D, 0), _row_spec(D, 1), _row_spec(D, 2), _row_spec(D), _row_spec(D), _row_spec(D)],
        out_specs=_row_spec(D), out_shape=jax.ShapeDtypeStruct((s, D), BF16),
        compiler_params=_params("parallel"),
    )(z, z, z, pa, pb, pc)


def _merge_bwd(dm, z, pa, pb, pc):
    s = z.shape[0]

    def body(dm_ref, g0_ref, g1_ref, g2_ref, pa_ref, pb_ref, pc_ref, dgl_ref, da_ref, db_ref, dc_ref):
        dmv = dm_ref[...]
        for i, (g_ref, p_ref, d_ref) in enumerate(((g0_ref, pa_ref, da_ref), (g1_ref, pb_ref, db_ref), (g2_ref, pc_ref, dc_ref))):
            gate = jax.nn.sigmoid(g_ref[...])
            dgl_ref[:, i * D:(i + 1) * D] = ((dmv * p_ref[...]) * (gate * (1.0 - gate))).astype(BF16)
            d_ref[...] = (dmv * gate).astype(BF16)

    return pl.pallas_call(
        body, name="merge_bwd", grid=(s // ROW_TILE,),
        in_specs=[_row_spec(D), _row_spec(D, 0), _row_spec(D, 1), _row_spec(D, 2), _row_spec(D), _row_spec(D), _row_spec(D)],
        out_specs=[_row_spec(3 * D), _row_spec(D), _row_spec(D), _row_spec(D)],
        out_shape=[jax.ShapeDtypeStruct((s, Z_COLS), BF16)] + [jax.ShapeDtypeStruct((s, D), BF16)] * 3,
        compiler_params=_params("parallel"),
    )(dm, z, z, z, pa, pb, pc)


def _shift_down(v, n):
    row = lax.broadcasted_iota(jnp.int32, v.shape, 0)
    return jnp.where(row >= n, pltpu.roll(v, n, axis=0), 0.0)


def _shift_up(v, n):
    s = v.shape[0]
    row = lax.broadcasted_iota(jnp.int32, v.shape, 0)
    return jnp.where(row < s - n, pltpu.roll(v, s - n, axis=0), 0.0)


def _log_sigmoid(v):
    return jnp.minimum(v, 0.0) - jnp.log1p(jnp.exp(-jnp.abs(v)))


def _cumf_fwd(fl, bias):
    s = fl.shape[0]

    def body(fl_ref, b_ref, o_ref):
        acc = _log_sigmoid(fl_ref[...] + b_ref[...])
        step = 1
        while step < s:
            acc = acc + _shift_down(acc, step)
            step *= 2
        o_ref[...] = acc

    return pl.pallas_call(body, name="cumf_fwd", out_shape=jax.ShapeDtypeStruct((s, 128), F32),
                          compiler_params=pltpu.CompilerParams(vmem_limit_bytes=V7X_VMEM_LIMIT))(fl, bias)


def _cumf_bwd(dcum, fl, bias):
    s = fl.shape[0]

    def body(d_ref, fl_ref, b_ref, dfl_ref, db_ref):
        acc = d_ref[...]
        step = 1
        while step < s:
            acc = acc + _shift_up(acc, step)
            step *= 2
        dfl = acc * jax.nn.sigmoid(-(fl_ref[...] + b_ref[...]))
        dfl_ref[...] = dfl.astype(BF16)
        db_ref[...] = jnp.broadcast_to(jnp.sum(dfl, axis=0, keepdims=True), (8, 128))

    return pl.pallas_call(
        body, name="cumf_bwd",
        out_shape=[jax.ShapeDtypeStruct((s, 128), BF16), jax.ShapeDtypeStruct((8, 128), F32)],
        compiler_params=pltpu.CompilerParams(vmem_limit_bytes=V7X_VMEM_LIMIT))(dcum, fl, bias)


def _pool_windows(v, shift):
    s2 = v + shift(v, 1)
    s4 = s2 + shift(s2, 2)
    s8 = s4 + shift(s4, 4)
    s16 = s8 + shift(s8, 8)
    group = lax.broadcasted_iota(jnp.int32, v.shape, 1) // 64
    return jnp.where(group == 0, s2, jnp.where(group == 1, s4, jnp.where(group == 2, s8, s16)))


def _pool_count(shape):
    group = lax.broadcasted_iota(jnp.int32, shape, 1) // 64
    window = jnp.where(group == 0, 2.0, jnp.where(group == 1, 4.0, jnp.where(group == 2, 8.0, 16.0)))
    t1 = (lax.broadcasted_iota(jnp.int32, shape, 0) + 1).astype(F32)
    return jnp.minimum(t1, window)


def _pc_specs(s):
    zcol = lambda blk: pl.BlockSpec((s, 256), lambda i, blk=blk: (0, blk))
    first = Z_PC // 256
    return [zcol(first), zcol(first + 1), zcol(first + 2), zcol(first + 3),
            pl.BlockSpec((256, 256), lambda i: (0, 0)), pl.BlockSpec((1, 256), lambda i: (0, 0)),
            pl.BlockSpec((3, 256), lambda i: (0, 0))]


def _poolconv_fwd(z, wbd, pscale, convw):
    s = z.shape[0]

    def body(pu_ref, ch_ref, cb_ref, cc_ref, w_ref, ps_ref, cw_ref, yb_ref, yc_ref):
        u = pu_ref[...]
        p = _pool_windows(u, _shift_down) / _pool_count(u.shape) - u
        yb = jnp.dot(p.astype(BF16), w_ref[...].astype(BF16), preferred_element_type=F32) * ps_ref[...]
        yb_ref[...] = yb.astype(BF16)
        uc = cc_ref[...] * ch_ref[...]
        cw = cw_ref[...]
        conv = cw[0:1, :] * _shift_down(uc, 2) + cw[1:2, :] * _shift_down(uc, 1) + cw[2:3, :] * uc
        yc_ref[...] = (cb_ref[...] * conv).astype(BF16)

    out = pl.BlockSpec((s, 256), lambda i: (0, 0))
    return pl.pallas_call(
        body, name="poolconv_fwd", grid=(1,), in_specs=_pc_specs(s), out_specs=[out, out],
        out_shape=[jax.ShapeDtypeStruct((s, 256), BF16)] * 2, compiler_params=_params("arbitrary"),
    )(z, z, z, z, wbd, pscale, convw)


def _poolconv_bwd(dyb, dyc, z, wbd, pscale, convw):
    s = z.shape[0]

    def body(dyb_ref, dyc_ref, pu_ref, ch_ref, cb_ref, cc_ref, w_ref, ps_ref, cw_ref, dz_ref, dw_ref, dps_ref, dcw_ref):
        u = pu_ref[...]
        count = _pool_count(u.shape)
        p = (_pool_windows(u, _shift_down) / count - u).astype(BF16)
        wb = w_ref[...].astype(BF16)
        dyb_v = dyb_ref[...]
        pw = jnp.dot(p, wb, preferred_element_type=F32)
        dps_ref[...] = jnp.broadcast_to(jnp.sum(dyb_v * pw, axis=0, keepdims=True), (8, 256))
        dys = (dyb_v * ps_ref[...]).astype(BF16)
        dp = lax.dot_general(dys, wb, (((1,), (1,)), ((), ())), preferred_element_type=F32)
        dw_ref[...] = lax.dot_general(p, dys, (((0,), (0,)), ((), ())), preferred_element_type=F32)
        dz_ref[:, 0:256] = (_pool_windows(dp / count, _shift_up) - dp).astype(BF16)

        ch, cb, cc = ch_ref[...], cb_ref[...], cc_ref[...]
        uc = cc * ch
        cw = cw_ref[...]
        u2, u1 = _shift_down(uc, 2), _shift_down(uc, 1)
        conv = cw[0:1, :] * u2 + cw[1:2, :] * u1 + cw[2:3, :] * uc
        dyc_v = dyc_ref[...]
        dconv = dyc_v * cb
        du = cw[0:1, :] * _shift_up(dconv, 2) + cw[1:2, :] * _shift_up(dconv, 1) + cw[2:3, :] * dconv
        dz_ref[:, 256:512] = (du * cc).astype(BF16)
        dz_ref[:, 512:768] = (dyc_v * conv).astype(BF16)
        dz_ref[:, 768:1024] = (du * ch).astype(BF16)
        dcw_ref[...] = jnp.zeros_like(dcw_ref)
        dcw_ref[0:1, :] = jnp.sum(dconv * u2, axis=0, keepdims=True)
        dcw_ref[1:2, :] = jnp.sum(dconv * u1, axis=0, keepdims=True)
        dcw_ref[2:3, :] = jnp.sum(dconv * uc, axis=0, keepdims=True)

    blk = lambda r, c: pl.BlockSpec((r, c), lambda i: (0, 0))
    return pl.pallas_call(
        body, name="poolconv_bwd", grid=(1,),
        in_specs=[blk(s, 256), blk(s, 256)] + _pc_specs(s),
        out_specs=[blk(s, 1024), blk(256, 256), blk(8, 256), blk(8, 256)],
        out_shape=[jax.ShapeDtypeStruct((s, 1024), BF16), jax.ShapeDtypeStruct((256, 256), F32),
                   jax.ShapeDtypeStruct((8, 256), F32), jax.ShapeDtypeStruct((8, 256), F32)],
        compiler_params=_params("arbitrary"),
    )(dyb, dyc, z, z, z, z, wbd, pscale, convw)


_NT = (((1,), (1,)), ((), ()))
_TN = (((0,), (0,)), ((), ()))


ATT_Q, ATT_K = 256, 256
ATT_HEADS_BWD = 8
ATT_HEADS = 8


def _att_logits(q, k, fr, q0, k0, masked):
    logits = lax.dot_general(q, k, _NT, preferred_element_type=F32) - fr
    if not masked:
        return logits
    row = q0 + lax.broadcasted_iota(jnp.int32, logits.shape, 0)
    col = k0 + lax.broadcasted_iota(jnp.int32, logits.shape, 1)
    return jnp.where(row >= col, logits, NEG_INF)


def _causal_sweep(step, qi, init):
    n_full = (qi * ATT_Q) // ATT_K
    carry = lax.fori_loop(0, n_full, lambda j, carry: step(j, carry, False), init)
    return step(n_full, carry, True)


HEAD_PAIRS = HEADS // 2


def _lane_pick(v, lane, idx):
    return jnp.sum(jnp.where(lane == idx, v, 0.0), axis=-1, keepdims=True)


def _lane_put(lane, idx, col):
    return jnp.where(lane == idx, col, 0.0)


def _split_heads(v, low):
    zero = jnp.zeros_like(v)
    return jnp.where(low, v, zero), jnp.where(low, zero, v)


def _attn_fwd(qkv, fr):
    s = qkv.shape[0]
    nk = s // ATT_K
    width = ATT_HEADS * HEAD_DIM
    groups = HEADS // ATT_HEADS

    def body(q_ref, k_ref, v_ref, fr_ref, o_ref, lse_ref):
        qi, grp = pl.program_id(0), pl.program_id(1)
        lane = lax.broadcasted_iota(jnp.int32, (ATT_Q, 128), 1)
        low = lane < HEAD_DIM
        qs = []
        for pr in range(ATT_HEADS // 2):
            qs += _split_heads(q_ref[:, 128 * pr:128 * (pr + 1)] * (HEAD_DIM ** -0.5), low)

        def step(j, carry, masked):
            k0 = pl.multiple_of(j * ATT_K, ATT_K)
            out = []
            for h in range(ATT_HEADS):
                cols = slice(128 * (h // 2), 128 * (h // 2 + 1))
                m, l, acc = carry[h]
                logits = _att_logits(qs[h], k_ref[pl.ds(k0, ATT_K), cols], fr_ref[h, pl.ds(j, 1), :], qi * ATT_Q, k0, masked)
                m_new = jnp.maximum(m, jnp.max(logits, axis=-1, keepdims=True))
                p = jnp.exp(logits - m_new)
                alpha = jnp.exp(m - m_new)
                l = alpha * l + jnp.sum(p, axis=-1, keepdims=True)
                acc = alpha * acc + jnp.dot(p.astype(BF16), v_ref[pl.ds(k0, ATT_K), cols], preferred_element_type=F32)
                out.append((m_new, l, acc))
            return tuple(out)

        one = (jnp.full((ATT_Q, 1), NEG_INF, F32), jnp.zeros((ATT_Q, 1), F32), jnp.zeros((ATT_Q, 128), F32))
        done = _causal_sweep(step, qi, (one,) * ATT_HEADS)

        @pl.when(grp == 0)
        def _():
            lse_ref[...] = jnp.zeros_like(lse_ref)

        lse = jnp.zeros((ATT_Q, 128), F32)
        for pr in range(ATT_HEADS // 2):
            (m0, l0, acc0), (m1, l1, acc1) = done[2 * pr], done[2 * pr + 1]
            o_ref[:, 128 * pr:128 * (pr + 1)] = jnp.where(low, acc0 / l0, acc1 / l1)
            head = ATT_HEADS * grp + 2 * pr
            lse = lse + _lane_put(lane, head, m0 + jnp.log(l0)) + _lane_put(lane, head + 1, m1 + jnp.log(l1))
        lse_ref[...] += lse

    return pl.pallas_call(
        body, name="attn_fwd", grid=(s // ATT_Q, groups),
        in_specs=[pl.BlockSpec((ATT_Q, width), lambda i, g: (i, g)),
                  pl.BlockSpec((s, width), lambda i, g: (0, groups + g)),
                  pl.BlockSpec((s, width), lambda i, g: (0, 2 * groups + g)),
                  pl.BlockSpec((ATT_HEADS, nk, ATT_K), lambda i, g: (g, 0, 0))],
        out_specs=[pl.BlockSpec((ATT_Q, width), lambda i, g: (i, g)), pl.BlockSpec((ATT_Q, 128), lambda i, g: (i, 0))],
        out_shape=[jax.ShapeDtypeStruct((s, A_WIDTH), F32), jax.ShapeDtypeStruct((s, 128), F32)],
        compiler_params=_params("parallel", "arbitrary"),
    )(qkv, qkv, qkv, fr)


def _attn_bwd(qkv, do, o, lse, fr):
    s = qkv.shape[0]
    nk = s // ATT_K
    scale = HEAD_DIM ** -0.5
    heads = ATT_HEADS_BWD
    width = heads * HEAD_DIM
    groups = HEADS // heads

    def body(q_ref, k_ref, v_ref, do_ref, o_ref, lse_ref, fr_ref, dq_ref, dk_ref, dv_ref, dfc_ref, dfr_ref, dk_acc, dv_acc):
        grp = pl.program_id(0)
        lane = lax.broadcasted_iota(jnp.int32, (ATT_Q, 128), 1)
        low = lane < HEAD_DIM
        low_t = lax.broadcasted_iota(jnp.int32, (128, ATT_Q), 0) < HEAD_DIM
        dk_acc[...] = jnp.zeros_like(dk_acc)
        dv_acc[...] = jnp.zeros_like(dv_acc)
        dfr_ref[...] = jnp.zeros_like(dfr_ref)

        @pl.when(grp == 0)
        def _():
            dfc_ref[...] = jnp.zeros_like(dfc_ref)

        def outer(i, carry):
            q0 = pl.multiple_of(i * ATT_Q, ATT_Q)
            rows = pl.ds(q0, ATT_Q)
            lsev = lse_ref[rows, :]
            qts, dots, qs, dos, deltas, lses = [], [], [], [], [], []
            for pr in range(heads // 2):
                pcols = slice(128 * pr, 128 * (pr + 1))
                q2, do2 = q_ref[rows, pcols] * scale, do_ref[rows, pcols]
                prod = do2 * o_ref[rows, pcols]
                deltas += [jnp.sum(jnp.where(low, prod, 0.0), axis=-1, keepdims=True),
                           jnp.sum(jnp.where(low, 0.0, prod), axis=-1, keepdims=True)]
                dob2 = do2.astype(BF16)
                qts += _split_heads(q2.astype(F32).T.astype(BF16), low_t)
                dots += _split_heads(do2.T.astype(BF16), low_t)
                qs += _split_heads(q2, low)
                dos += _split_heads(dob2, low)
                lses += [_lane_pick(lsev, lane, heads * grp + 2 * pr), _lane_pick(lsev, lane, heads * grp + 2 * pr + 1)]

            def inner(j, carry, masked):
                k0 = pl.multiple_of(j * ATT_K, ATT_K)
                krows = pl.ds(k0, ATT_K)
                out, dkt, dvt = [], [], []
                for h in range(heads):
                    pcols = slice(128 * (h // 2), 128 * (h // 2 + 1))
                    dq, dfc = carry[h]
                    k2 = k_ref[krows, pcols]
                    p = jnp.exp(_att_logits(qs[h], k2, fr_ref[h, pl.ds(j, 1), :], q0, k0, masked) - lses[h])
                    dp = lax.dot_general(dos[h], v_ref[krows, pcols], _NT, preferred_element_type=F32)
                    ds = p * (dp - deltas[h])
                    dsb = ds.astype(BF16)
                    dkt.append(jnp.dot(qts[h], dsb, preferred_element_type=F32))
                    dvt.append(jnp.dot(dots[h], p.astype(BF16), preferred_element_type=F32))
                    dfr_ref[h, pl.ds(j, 1), :] -= jnp.sum(ds, axis=0, keepdims=True)
                    out.append((dq + jnp.dot(dsb, k2, preferred_element_type=F32), dfc + (ds[:, :128] + ds[:, 128:])))
                for pr in range(heads // 2):
                    prows = slice(128 * pr, 128 * (pr + 1))
                    dk_acc[j, prows, :] += dkt[2 * pr] + dkt[2 * pr + 1]
                    dv_acc[j, prows, :] += dvt[2 * pr] + dvt[2 * pr + 1]
                return tuple(out)

            one = (jnp.zeros((ATT_Q, 128), F32), jnp.zeros((ATT_Q, 128), F32))
            done = _causal_sweep(inner, i, (one,) * heads)
            dfc = jnp.zeros((ATT_Q, 128), F32)
            for pr in range(heads // 2):
                (dq0, dfc0), (dq1, dfc1) = done[2 * pr], done[2 * pr + 1]
                dq_ref[rows, 128 * pr:128 * (pr + 1)] = (jnp.where(low, dq0, dq1) * scale).astype(BF16)
                head = heads * grp + 2 * pr
                dfc = (dfc + _lane_put(lane, head, jnp.sum(dfc0, axis=-1, keepdims=True))
                       + _lane_put(lane, head + 1, jnp.sum(dfc1, axis=-1, keepdims=True)))
            dfc_ref[rows, :] += dfc
            return carry

        lax.fori_loop(0, s // ATT_Q, outer, 0)
        for j in range(nk):
            for pr in range(heads // 2):
                prows, pcols = slice(128 * pr, 128 * (pr + 1)), slice(128 * pr, 128 * (pr + 1))
                dk_ref[ATT_K * j:ATT_K * (j + 1), pcols] = dk_acc[j, prows, :].T.astype(BF16)
                dv_ref[ATT_K * j:ATT_K * (j + 1), pcols] = dv_acc[j, prows, :].T.astype(BF16)

    part = lambda first: pl.BlockSpec((s, width), lambda g, first=first: (0, first + g))
    whole = pl.BlockSpec((s, 128), lambda g: (0, 0))
    rowv = pl.BlockSpec((heads, nk, ATT_K), lambda g: (g, 0, 0))
    return pl.pallas_call(
        body, name="attn_bwd", grid=(groups,),
        in_specs=[part(0), part(groups), part(2 * groups), part(0), part(0), whole, rowv],
        out_specs=[part(0), part(0), part(0), whole, rowv],
        out_shape=[jax.ShapeDtypeStruct((s, A_WIDTH), BF16)] * 3 + [jax.ShapeDtypeStruct((s, 128), F32), jax.ShapeDtypeStruct((HEADS, nk, ATT_K), F32)],
        scratch_shapes=[pltpu.VMEM((nk, width, ATT_K), F32), pltpu.VMEM((nk, width, ATT_K), F32)],
        compiler_params=_params("arbitrary"),
    )(qkv, qkv, qkv, do, o, lse, fr)


def _ada_fwd(c_all, w_ada, b_loc):
    depth, _, n = w_ada.shape
    tn = 512

    def body(c_ref, w_ref, b_ref, o_ref, sc_ref):
        cv = c_ref[...]
        sc = cv * jax.nn.sigmoid(cv)
        sc_ref[...] = sc
        o_ref[0] = jnp.dot(sc.astype(BF16), w_ref[0].astype(BF16), preferred_element_type=F32) + b_ref[0]

    return pl.pallas_call(
        body, name="ada_fwd", grid=(depth, n // tn),
        in_specs=[pl.BlockSpec((N_DEV, D), lambda l, j: (0, 0)), pl.BlockSpec((1, D, tn), lambda l, j: (l, 0, j)),
                  pl.BlockSpec((1, 1, tn), lambda l, j: (l, 0, j))],
        out_specs=[pl.BlockSpec((1, N_DEV, tn), lambda l, j: (l, 0, j)), pl.BlockSpec((N_DEV, D), lambda l, j: (0, 0))],
        out_shape=[jax.ShapeDtypeStruct((depth, N_DEV, n), F32), jax.ShapeDtypeStruct((N_DEV, D), F32)],
        compiler_params=_params("arbitrary", "arbitrary"),
    )(c_all, w_ada, b_loc)


def _sum_devices(gathered):
    n = gathered.shape[1]
    tn = _pick(n, (1408, 1024, 640, 512, 128))

    def body(g_ref, o_ref):
        acc = g_ref[0:8, :]
        for dev in range(1, N_DEV):
            acc = acc + g_ref[8 * dev:8 * dev + 8, :]
        o_ref[...] = acc

    return pl.pallas_call(
        body, name="sum_devices", grid=(n // tn,),
        in_specs=[pl.BlockSpec((8 * N_DEV, tn), lambda j: (0, j))], out_specs=pl.BlockSpec((8, tn), lambda j: (0, j)),
        out_shape=jax.ShapeDtypeStruct((8, n), F32), compiler_params=_params("parallel"),
    )(gathered)


def _place():
    x, y, c = lax.axis_index("x"), lax.axis_index("y"), lax.axis_index("c")
    chips = [(1 - x, y), (x, 1 - y), (1 - x, 1 - y)]
    return x, y, c, chips


def _allgather8(block, name, after=()):
    m_per, n = block.shape

    def body(x_ref, *rest):
        out_ref, send_sems, recv_sems, local_sem = rest[len(after):]
        x, y, c, chips = _place()
        me, sibling = (x, y, c), (x, y, 1 - c)

        def rows(px, py, pc):
            return out_ref.at[pl.ds((4 * px + 2 * py + pc) * m_per, m_per), :]

        def copy(k, blk, to, src=None):
            return pltpu.make_async_remote_copy(
                src_ref=rows(*blk) if src is None else src, dst_ref=rows(*blk),
                send_sem=send_sems.at[k], recv_sem=recv_sems.at[k], device_id=to, device_id_type=MESH)

        mine = pltpu.make_async_copy(x_ref, rows(*me), local_sem)
        mine.start()
        first = [copy(0, me, sibling, src=x_ref)]
        first += [copy(1 + j, me, (*chip, c), src=x_ref) for j, chip in enumerate(chips)]
        for cp in first:
            cp.start()
        passed = [copy(4 + j, (*chip, c), sibling) for j, chip in enumerate(chips)]
        for j, chip in enumerate(chips):
            copy(1 + j, (*chip, c), me).wait_recv()
            passed[j].start()
        copy(0, sibling, me).wait_recv()
        for j, chip in enumerate(chips):
            copy(4 + j, (*chip, 1 - c), me).wait_recv()
        for cp in first + passed:
            cp.wait_send()
        mine.wait()

    return pl.pallas_call(
        body, name=name, out_shape=jax.ShapeDtypeStruct((N_DEV * m_per, n), block.dtype),
        in_specs=[pl.BlockSpec(memory_space=pltpu.VMEM)] + [pl.BlockSpec(memory_space=pl.ANY)] * len(after),
        out_specs=pl.BlockSpec(memory_space=pltpu.VMEM),
        scratch_shapes=[pltpu.SemaphoreType.DMA((7,)), pltpu.SemaphoreType.DMA((7,)), pltpu.SemaphoreType.DMA],
        compiler_params=pltpu.CompilerParams(vmem_limit_bytes=V7X_VMEM_LIMIT),
    )(block, *after)


_SEM = pl.BlockSpec(memory_space=pltpu.SEMAPHORE)
_DATAFLOW = pltpu.SideEffectType.DATAFLOW_SIDE_EFFECTING


def _plan_copies(plan, refs, send_sems, recv_sems):
    return [pltpu.make_async_remote_copy(src_ref=src, dst_ref=dst, send_sem=send_sems.at[i], recv_sem=recv_sems.at[i],
                                         device_id=to, device_id_type=MESH) for i, (src, dst, to) in enumerate(plan(refs))]


class _Token(NamedTuple):
    after: jax.Array
    tie: jax.Array


def _after_operand(after):
    return after.after if isinstance(after, _Token) else after


def _copies_start(bufs, plan, n_copies, after, name):
    nb = len(bufs)

    def body(*refs):
        for cp in _plan_copies(plan, refs[:nb], refs[nb + 1], refs[nb + 2]):
            cp.start()
        for token in refs[-2:]:
            token[...] = jnp.zeros_like(token)

    sem = pltpu.SemaphoreType.DMA((n_copies,))
    vmem = pl.BlockSpec(memory_space=pltpu.VMEM)
    outs = pl.pallas_call(
        body, name=name,
        out_shape=(sem, sem, *[pltpu.HBM(b.shape, b.dtype) for b in bufs], jax.ShapeDtypeStruct((8, 128), F32),
                   jax.ShapeDtypeStruct((1, 1), F32)),
        in_specs=[_HBM] * nb + [pl.BlockSpec(memory_space=pl.ANY)],
        out_specs=(_SEM, _SEM, *[_HBM] * nb, vmem, vmem),
        input_output_aliases={i: 2 + i for i in range(nb)},
        compiler_params=pltpu.CompilerParams(has_side_effects=_DATAFLOW),
    )(*[pltpu.with_memory_space_constraint(b, pltpu.HBM) for b in bufs], _after_operand(after))
    return outs[0], outs[1], list(outs[2:2 + nb]), _Token(outs[-2], outs[-1])


def _copies_wait(started, plan, after, name):
    send_sems, recv_sems, bufs, _ = started
    nb = len(bufs)

    def body(*refs):
        for cp in _plan_copies(plan, refs[:nb], refs[nb], refs[nb + 1]):
            cp.wait_send()
            cp.wait_recv()

    return list(pl.pallas_call(
        body, name=name, out_shape=tuple(pltpu.HBM(b.shape, b.dtype) for b in bufs),
        in_specs=[_HBM] * nb + [_SEM, _SEM, pl.BlockSpec(memory_space=pl.ANY)], out_specs=tuple([_HBM] * nb),
        input_output_aliases={i: i for i in range(nb)},
        compiler_params=pltpu.CompilerParams(has_side_effects=_DATAFLOW),
    )(*bufs, send_sems, recv_sems, _after_operand(after)))


def _half_rows(ref, axis, c):
    half = ref.shape[axis] // 2
    return pl.ds(c * half, half)


def _plan_gather_ici(refs):
    n = len(refs) // 2
    x, y, c, chips = _place()
    out = []
    for a in range(n):
        rows = _half_rows(refs[a], 0, c)
        out += [(refs[a].at[rows], refs[n + a].at[2 * x + y, rows], (*chip, c)) for chip in chips]
        out.append((refs[a], refs[n + a].at[2 * x + y], (x, y, 1 - c)))
    return out


def _plan_gather_d2d(refs):
    x, y, c, chips = _place()
    out = []
    for ref in refs:
        rows = _half_rows(ref, 1, c)
        for px, py in chips:
            landed = ref.at[2 * px + py, rows]
            out.append((landed, landed, (x, y, 1 - c)))
    return out


def _plan_rs_sibling(refs):
    n = len(refs) // 2
    x, y, c, _ = _place()
    return [(refs[a].at[pl.ds(0, refs[a].shape[0]), _half_rows(refs[a], 1, 1 - c)], refs[n + a], (x, y, 1 - c)) for a in range(n)]


def _plan_rs_chips(refs):
    n = len(refs) // 2
    x, y, c, chips = _place()
    return [(refs[a].at[2 * px + py], refs[n + a].at[k], (px, py, c)) for a in range(n) for k, (px, py) in enumerate(chips)]


def _plan_rs_share(layer):
    def plan(refs):
        x, y, c, _ = _place()
        return [(ref.at[layer, _half_rows(ref, 1, c)], ref.at[layer, _half_rows(ref, 1, c)], (x, y, 1 - c)) for ref in refs]
    return plan


def _chip_sum(g, other, sel, name, blocked=True):
    nblk, half, cdim = other.shape
    tr = _pick(half, (512, 256, 128, 64))
    per = half // tr

    def body(sel_ref, g_ref, t_ref, wire_ref, own_ref):
        total = g_ref[0] + t_ref[0]
        wire_ref[0] = total.astype(BF16)
        if blocked:
            @pl.when(pl.program_id(1) == sel_ref[1])
            def _():
                own_ref[...] = total
        else:
            own_ref[0] = total

    blk = pl.BlockSpec((1, tr, cdim), lambda i, p, sel_ref: (p, i, 0))
    own_spec = pl.BlockSpec((tr, cdim), lambda i, p, sel_ref: (i, 0)) if blocked else blk
    own_shape = jax.ShapeDtypeStruct((half, cdim) if blocked else other.shape, F32)
    return pl.pallas_call(
        body, name=name,
        grid_spec=pltpu.PrefetchScalarGridSpec(
            num_scalar_prefetch=1, grid=(per, nblk),
            in_specs=[pl.BlockSpec((1, tr, cdim), lambda i, p, sel_ref: (p, sel_ref[0] * per + i, 0)), blk],
            out_specs=[blk, own_spec]),
        out_shape=[jax.ShapeDtypeStruct(other.shape, BF16), own_shape],
        compiler_params=_params("parallel", "arbitrary"),
    )(sel, g, other)


def _final_sum(own, recv, sel, layer, into, name):
    half, cdim = own.shape
    tr = _pick(half, (512, 256, 128, 64))
    per = half // tr

    def body(sel_ref, own_ref, r0_ref, r1_ref, r2_ref, *rest):
        rest[-1][...] = ((own_ref[...] + r0_ref[0].astype(F32)) + r1_ref[0].astype(F32)) + r2_ref[0].astype(F32)

    part = lambda k: pl.BlockSpec((1, tr, cdim), lambda i, sel_ref, k=k: (k, i, 0))
    prior = [] if into is None else [into]
    return pl.pallas_call(
        body, name=name,
        grid_spec=pltpu.PrefetchScalarGridSpec(
            num_scalar_prefetch=1, grid=(per,),
            in_specs=[pl.BlockSpec((tr, cdim), lambda i, sel_ref: (i, 0)), part(0), part(1), part(2)]
            + [pl.BlockSpec(memory_space=pl.ANY)] * len(prior),
            out_specs=pl.BlockSpec((None, tr, cdim), lambda i, sel_ref: (layer, sel_ref[0] * per + i, 0))),
        out_shape=jax.ShapeDtypeStruct((DEPTH, 2 * half, cdim), F32),
        input_output_aliases={5: 0} if prior else {}, compiler_params=_params("parallel"),
    )(sel, own, recv, recv, recv, *prior)


def _row(v):
    return v.reshape(1, -1)


_BR_A, _BR_B, _BR_C = (0, A_WIDTH), (A_WIDTH, POOL_WIDTH), (A_WIDTH + POOL_WIDTH, CONV_WIDTH)


def _tie(v, token):
    return v if token is None else v + token.tie


def _no_hook(point, after, ready=None):
    return None


def _layer_fwd(x, w, mod, hook=_no_hook):
    s = x.shape[0]
    mod3 = mod.reshape(6, 1, D)
    h = _modnorm_fwd(x, _row(w["g_mix_pre"]), (mod3, 0), (mod3, 1), "mix_pre_fwd")
    hook("pre", h)
    z = _mm(h, w["w_all"], name="mm_in")
    qkv = z[:, Z_QKV:Z_PC].astype(BF16)
    fl = z[:, Z_FL:Z_COLS]
    cum = _cumf_fwd(fl, w["b_f_pad"])
    fr = cum[:, :HEADS].T.reshape(HEADS, s // ATT_K, ATT_K)
    br_a, lse = _attn_fwd(qkv, fr)
    br_b, br_c = _poolconv_fwd(z, w["w_pool_bd"], _tie(_row(w["pool_scale"]), hook("attn", lse)), w["conv_w"])
    hook("pool", br_b)
    wbr = w["w_branch"]
    pa = _mm(br_a, wbr, b_rows=_BR_A, name="mm_br_a")
    pb = _mm(br_b, wbr, b_rows=_BR_B, name="mm_br_b")
    pc = _mm(br_c, wbr, b_rows=_BR_C, name="mm_br_c")
    merged = _merge_fwd(z, pa, pb, pc)
    y = _mm(merged, w["w_out"], name="mm_out")
    x1, h2 = _post_pre_fwd(x, y, _row(w["g_mix_post"]), (mod3, 2), _row(w["g_ff_pre"]), (mod3, 3), (mod3, 4), "mix_post_ff_pre_fwd")
    a, r = _mm(h2, w["w_ff1"], b_split=N_CHIPS, epilogue=_relu2_fwd, out_dtype=(F32, BF16), name="mm_ff1")
    y2 = _mm(r, w["w_ff2"], name="mm_ff2")
    x2 = _post_fwd(x1, y2, _tie(_row(w["g_ff_post"]), hook("ff_post", y2)), (mod3, 5), "ff_post_fwd")
    hook("end", x2)
    saved = dict(x=x, h=h, z=z, qkv=qkv, fl=fl, fr=fr, lse=lse, br_a=br_a, br_b=br_b, br_c=br_c, pa=pa, pb=pb, pc=pc,
                 merged=merged, y=y, x1=x1, h2=h2, a=a, r=r, y2=y2)
    return x2, saved


def _layer_bwd(dx2, sv, w, mod, hook=_no_hook):
    s = dx2.shape[0]
    mod3 = mod.reshape(6, 1, D)
    dy2, sum_ff_post = _post_bwd(dx2, sv["y2"], _row(w["g_ff_post"]), (mod3, 5), "ff_post_bwd")
    (da,) = _mm(dy2, w["w_ff2"], tb=True, epilogue=_relu2_bwd, extras=(sv["a"],), out_dtype=(BF16,), name="mm_ff2_dx")
    d_w_ff2 = _mm(sv["r"], dy2, ta=True, name="mm_ff2_dw")
    dh2 = _mm(da, w["w_ff1"], tb=True, b_split=N_CHIPS, name="mm_ff1_dx")
    d_w_ff1 = _mm(sv["h2"], da, ta=True, out_split=N_CHIPS, name="mm_ff1_dw")
    g_ff_pre = _tie(_row(w["g_ff_pre"]), hook("ff_pre", dh2, dict(w_ff1=d_w_ff1, w_ff2=d_w_ff2)))
    dx1, dy, sum_mid = _pre_post_bwd(dh2, sv["x1"], dx2, g_ff_pre, (mod3, 4), sv["y"], _row(w["g_mix_post"]), (mod3, 2), "ff_pre_mix_post_bwd")
    sum_ff_pre, sum_mix_post = sum_mid, sum_mid[3:]
    dmerged = _mm(dy, w["w_out"], tb=True, name="mm_out_dx")
    d_w_out = _mm(sv["merged"], dy, ta=True, name="mm_out_dw")
    dz, dpa, dpb, dpc = _merge_bwd(dmerged, sv["z"], sv["pa"], sv["pb"], sv["pc"])
    wbr = w["w_branch"]
    dbr_a = _mm(dpa, wbr, tb=True, b_rows=_BR_A, name="mm_br_a_dx")
    dbr_b = _mm(dpb, wbr, tb=True, b_rows=_BR_B, name="mm_br_b_dx")
    dbr_c = _mm(dpc, wbr, tb=True, b_rows=_BR_C, name="mm_br_c_dx")
    d_w_branch = jnp.concatenate([_mm(sv["br_a"], dpa, ta=True, name="mm_br_a_dw"), _mm(sv["br_b"], dpb, ta=True, name="mm_br_b_dw"),
                                  _mm(sv["br_c"], dpc, ta=True, name="mm_br_c_dw")], axis=0)

    dq, dk, dv, dfc, dfr = _attn_bwd(sv["qkv"], dbr_a, sv["br_a"], sv["lse"], sv["fr"])
    dcum = dfc + jnp.pad(dfr.reshape(HEADS, s).T, ((0, 0), (0, 128 - HEADS)))
    dfl, sum_bf = _cumf_bwd(dcum, sv["fl"], _tie(w["b_f_pad"], hook("cumf", dfc)))
    dpc_z, d_wbd, sum_ps, sum_cw = _poolconv_bwd(dbr_b, dbr_c, sv["z"], w["w_pool_bd"], _row(w["pool_scale"]), w["conv_w"])
    for at, part in ((Z_QKV, dq), (Z_QKV + A_WIDTH, dk), (Z_QKV + 2 * A_WIDTH, dv), (Z_PC, dpc_z), (Z_FL, dfl)):
        dz = lax.dynamic_update_slice(dz, part, (0, at))
    dh = _mm(dz, w["w_all"], tb=True, name="mm_in_dx")
    d_w_all = _mm(sv["h"], dz, ta=True, name="mm_in_dw")
    hook("mix_pre", dh)
    dx, sum_mix_pre = _modnorm_bwd(dh, sv["x"], dx1, _row(w["g_mix_pre"]), (mod3, 1), "mix_pre_bwd")

    dmod = jnp.stack([sum_mix_pre[0], sum_mix_pre[1], sum_mix_post[0], sum_ff_pre[0], sum_ff_pre[1], sum_ff_post[0]])
    d_w_in = d_w_all[None]
    d_w_pool = jnp.stack([d_wbd[64 * g:64 * g + 64, 64 * g:64 * g + 64] for g in range(4)])
    big = dict(w_in=d_w_in, w_branch=d_w_branch, w_out=d_w_out, w_ff1=d_w_ff1, w_ff2=d_w_ff2)
    small = dict(g_mix_pre=sum_mix_pre[2], g_mix_post=sum_mix_post[1], g_ff_pre=sum_ff_pre[2], g_ff_post=sum_ff_post[1],
                 b_f=sum_bf[0, :HEADS], w_pool=d_w_pool, pool_scale=sum_ps[0], conv_w=sum_cw[0:3])
    return dx, dmod, big, small


_QKV_END, _FL_END, _PC_END = 3 * A_WIDTH, 3 * A_WIDTH + HEADS, 3 * A_WIDTH + HEADS + POOL_WIDTH + 3 * CONV_WIDTH
_W_IN_GROUPS = ((_PC_END, IN_COLS, Z_GL), (0, _QKV_END, Z_QKV), (_FL_END, _PC_END, Z_PC), (_QKV_END, _FL_END, Z_FL))
_SHARD_COLS = IN_COLS // N_CHIPS


def _w_all_from_shards(blocks):
    pieces = []
    for lo, hi, _ in _W_IN_GROUPS:
        for p in range(N_CHIPS):
            a, b = max(lo, p * _SHARD_COLS), min(hi, (p + 1) * _SHARD_COLS)
            if a < b:
                pieces.append(blocks[p][:, a - p * _SHARD_COLS:b - p * _SHARD_COLS])
    pieces.append(jnp.zeros((D, Z_COLS - IN_COLS), blocks.dtype))
    return jnp.concatenate(pieces, axis=1)


def _w_in_shard(d_w_all, p):
    pieces = []
    for lo, hi, at in sorted(_W_IN_GROUPS):
        a, b = max(lo, p * _SHARD_COLS), min(hi, (p + 1) * _SHARD_COLS)
        if a < b:
            pieces.append(d_w_all[:, at + a - lo:at + b - lo])
    return jnp.concatenate(pieces, axis=1)


def _w_in_shards(d_w_all):
    return jnp.stack([_w_in_shard(d_w_all, p) for p in range(N_CHIPS)])


def _full_layer_weights(w_in_blocks, w_branch, w_out, w_ff1, w_ff2, g_mix_pre, g_mix_post, g_ff_pre, g_ff_post, b_f, w_pool, pool_scale, conv_w):
    w_all = None if w_in_blocks is None else _w_all_from_shards(w_in_blocks)
    wbd = (w_pool[:, :, None, :] * jnp.eye(4, dtype=F32)[:, None, :, None]).reshape(POOL_WIDTH, POOL_WIDTH)
    return dict(w_all=w_all, w_branch=w_branch, w_out=w_out, w_ff1=w_ff1, w_ff2=w_ff2, g_mix_pre=g_mix_pre, g_mix_post=g_mix_post,
                g_ff_pre=g_ff_pre, g_ff_post=g_ff_post, b_f_pad=jnp.pad(b_f, (0, 128 - HEADS)).reshape(1, 128), w_pool_bd=wbd,
                pool_scale=pool_scale, conv_w=conv_w)


class _NoComm:
    def layer_weights(self, l):
        raise NotImplementedError

    def fwd_hook(self, l):
        return _no_hook

    def bwd_hook(self, l):
        return _no_hook

    def grads_ready(self, l, big):
        return None


class _Layers(_NoComm):
    def __init__(self, layers):
        self.layers = layers

    def layer_weights(self, l):
        return self.layers[l]


def _local_step(x, target, mods, comm):
    saved, weights = [], []
    act = x
    for l in range(DEPTH):
        weights.append(comm.layer_weights(l))
        act, sv = _layer_fwd(act, weights[l], mods[l], comm.fwd_hook(l))
        saved.append(sv)
    dact, sq = _loss_head(act, target)
    loss = sq[0, 0] * (0.5 / D)
    dmods, bigs, smalls = [None] * DEPTH, [None] * DEPTH, [None] * DEPTH
    token = None
    for l in reversed(range(DEPTH)):
        dact, dmods[l], bigs[l], smalls[l] = _layer_bwd(dact, saved[l], weights[l], _tie(mods[l], token), comm.bwd_hook(l))
        token = comm.grads_ready(l, bigs[l])
    return loss, dact, jnp.stack(dmods), bigs, smalls


_BIG = ("w_in", "w_branch", "w_out", "w_ff1", "w_ff2")


class _GatherJob:
    def __init__(self, tag, shards, after):
        self.tag, self.n = tag, len(shards)
        lands = [lax.empty((N_CHIPS,) + s.shape, s.dtype) for s in shards]
        self.state = _copies_start(list(shards) + lands, _plan_gather_ici, 4 * self.n, after, "gather_ici_start_" + tag)
        self.token = self.state[3]

    def pass_on(self, after):
        bufs = _copies_wait(self.state, _plan_gather_ici, after, "gather_ici_wait_" + self.tag)
        self.state = _copies_start(bufs[self.n:], _plan_gather_d2d, 3 * self.n, bufs[0], "gather_d2d_start_" + self.tag)
        self.token = self.state[3]
        return self.token

    def done(self, after):
        return _copies_wait(self.state, _plan_gather_d2d, after, "gather_d2d_wait_" + self.tag)


class _ReduceJob:
    def __init__(self, tag, names, grads, sel, after, layer, into=None):
        self.tag, self.names, self.n, self.sel, self.layer, self.into = tag, names, len(names), sel, layer, into or {}
        lands = [lax.empty((g.shape[0], g.shape[1] // 2, g.shape[2]), F32) for g in grads]
        self.state = _copies_start(list(grads) + lands, _plan_rs_sibling, self.n, after, "rs_sibling_start_" + tag)
        self.token = self.state[3]

    def _chip_sum(self, name, g, other):
        if g.shape[0] == N_CHIPS:
            return _chip_sum(g, other, self.sel, "rs_chip_sum_" + name)
        wire, total = _chip_sum(g, other, self.sel, "rs_chip_sum_" + name, blocked=False)
        own = lax.switch(self.sel[1], [lambda t, p=p: _w_in_shard(t, p) for p in range(N_CHIPS)], total[0])
        return _w_in_shards(wire[0]), own

    def chip_sums(self, after):
        bufs = _copies_wait(self.state, _plan_rs_sibling, after, "rs_sibling_wait_" + self.tag)
        wires, self.owns = zip(*[self._chip_sum(name, bufs[i], bufs[self.n + i]) for i, name in enumerate(self.names)])
        lands = [lax.empty((3,) + w.shape[1:], BF16) for w in wires]
        self.state = _copies_start(list(wires) + lands, _plan_rs_chips, 3 * self.n, self.owns[0], "rs_chips_start_" + self.tag)
        self.token = self.state[3]
        return self.token

    def final_sums(self, after):
        bufs = _copies_wait(self.state, _plan_rs_chips, after, "rs_chips_wait_" + self.tag)
        sums = [_final_sum(self.owns[i], bufs[self.n + i], self.sel, self.layer, self.into.get(name), "rs_final_" + name)
                for i, name in enumerate(self.names)]
        self.state = _copies_start(sums, _plan_rs_share(self.layer), self.n, sums[0], "rs_share_start_" + self.tag)
        self.token = self.state[3]
        return self.token

    def done(self, after):
        return dict(zip(self.names, _copies_wait(self.state, _plan_rs_share(self.layer), after, "rs_share_wait_" + self.tag)))


def _chip_blocks(g):
    return g if g.ndim == 3 else g.reshape(N_CHIPS, -1, g.shape[1])


class _StepComm(_NoComm):
    def __init__(self, big_weights, w_in0, sel, after):
        self.sel = sel
        self.small, self.grads, self.jobs = None, {}, {}
        self.jobs["in0"] = _GatherJob("in0", [w_in0], after)
        later = lax.optimization_barrier((tuple(big_weights), self.jobs["in0"].token.after))[0]
        self.jobs["rest0"] = _GatherJob("rest0", [w[0].astype(BF16) for w in later[1:]], self.jobs["in0"].token)
        self.jobs["all1"] = _GatherJob("all1", [w[1].astype(BF16) for w in later], self.jobs["rest0"].token)

    def layer_weights(self, l):
        if l == 0:
            self.weights0 = _full_layer_weights(None, None, None, None, None, *self.small[0])
            return self.weights0
        g_in, g_br, g_out, g_f1, g_f2 = self.landed1
        return _full_layer_weights(g_in, g_br.reshape(D, D), g_out.reshape(D, D), g_f1, g_f2.reshape(D_FF, D), *self.small[1])

    def fwd_hook(self, l):
        if l != 0:
            return _no_hook

        def hook(point, after, ready=None):
            if point == "pre":
                job = self.jobs["in0"]
                started = after[:8, :128].astype(F32) + self.jobs["all1"].token.after
                self.weights0["w_all"] = _w_all_from_shards(job.done(job.pass_on(started))[0])
            if point == "attn":
                return self.jobs["rest0"].pass_on(after)
            if point == "ff_post":
                return self.jobs["all1"].pass_on(after)
            if point == "pool":
                g_br, g_out, g_f1, g_f2 = self.jobs["rest0"].done(after)
                self.weights0.update(w_branch=g_br.reshape(D, D), w_out=g_out.reshape(D, D), w_ff1=g_f1, w_ff2=g_f2.reshape(D_FF, D))
            if point == "end":
                self.landed1 = self.jobs["all1"].done(after)
            return None
        return hook

    def bwd_hook(self, l):
        if l != 0:
            return _no_hook

        def hook(point, after, ready=None):
            jobs = self.jobs
            if point == "ff_pre":
                token = jobs["rs1"].chip_sums(after)
                jobs["rs0_ff"] = _ReduceJob("0_ff", ("w_ff1", "w_ff2"), [_chip_blocks(ready[n]) for n in ("w_ff1", "w_ff2")], self.sel, token, 0)
                return jobs["rs0_ff"].token
            if point == "cumf":
                return jobs["rs0_ff"].chip_sums(jobs["rs1"].final_sums(after))
            self.layer1 = jobs["rs1"].done(after)
            jobs["rs0_ff"].into = self.layer1
            return None
        return hook

    def grads_ready(self, l, big):
        if l == 1:
            self.jobs["rs1"] = _ReduceJob("1", _BIG, [_chip_blocks(big[n]) for n in _BIG], self.sel, self.sel, 1)
            return self.jobs["rs1"].token
        names = ("w_in", "w_branch", "w_out")
        self.jobs["rs0_mix"] = _ReduceJob("0_mix", names, [_chip_blocks(big[n]) for n in names], self.sel, self.sel, 0, self.layer1)
        return self.jobs["rs0_mix"].token

    def finish_sums(self, after):
        jobs = self.jobs
        token = jobs["rs0_mix"].chip_sums(after)
        return jobs["rs0_ff"].final_sums(token)

    def finish_ff(self, after):
        self.grads.update(self.jobs["rs0_ff"].done(after))

    def finish_mix(self, after):
        job = self.jobs["rs0_mix"]
        self.grads.update(job.done(job.final_sums(after)))


_SMALL = ("g_mix_pre", "g_mix_post", "g_ff_pre", "g_ff_post", "b_f", "w_pool", "pool_scale", "conv_w")


def _w_in_view(t):
    return t.reshape(DEPTH, D // 128, 128, _SHARD_COLS).transpose(3, 1, 0, 2).reshape(_SHARD_COLS * (D // 128) * DEPTH, 128)


def _w_in_unview(t):
    return t.reshape(_SHARD_COLS, D // 128, DEPTH, 128).transpose(2, 1, 3, 0).reshape(DEPTH, D, _SHARD_COLS)


def _pack(parts, rows=8):
    flat = jnp.concatenate([p.reshape(-1) for p in parts])
    width = -(-flat.shape[0] // (rows * 128)) * 128
    return jnp.pad(flat, (0, rows * width - flat.shape[0])).reshape(rows, width)


def _unpack(packed, like):
    flat = packed.reshape(-1)
    out, at = [], 0
    for ref in like:
        out.append(flat[at:at + ref.size].reshape(ref.shape))
        at += ref.size
    return out


def kernel(x, c, w_ada, b_ada, g_mix_pre, g_mix_post, g_ff_pre, g_ff_post, w_in, b_f, w_pool, pool_scale, conv_w, w_branch, w_out, w_ff1, w_ff2, loss_target, m_w_ada, m_b_ada, m_g_mix_pre, m_g_mix_post, m_g_ff_pre, m_g_ff_post, m_w_in, m_b_f, m_w_pool, m_pool_scale, m_conv_w, m_w_branch, m_w_out, m_w_ff1, m_w_ff2, v_w_ada, v_b_ada, v_g_mix_pre, v_g_mix_post, v_g_ff_pre, v_g_ff_post, v_w_in, v_b_f, v_w_pool, v_pool_scale, v_conv_w, v_w_branch, v_w_out, v_w_ff1, v_w_ff2):
    xi, yi, ci = lax.axis_index("x"), lax.axis_index("y"), lax.axis_index("c")
    chip = 2 * xi + yi
    dev = 2 * chip + ci
    n_ada = w_ada.shape[2]

    first = jnp.zeros((8, D + 384), F32).at[0, :D].set(c[0]).at[0, D:].set(conv_w.reshape(-1))
    w_in0 = w_in[0].astype(BF16)
    got = _allgather8(first, "gather_cond", after=(w_in0,)).reshape(N_DEV, 8, D + 384)[:, 0]
    c_all = got[:, :D]
    conv_full = got[0::2, D:].reshape(N_CHIPS, DEPTH, 3, CONV_WIDTH // N_CHIPS).transpose(1, 2, 0, 3).reshape(DEPTH, 3, CONV_WIDTH)

    b_loc = lax.dynamic_slice_in_dim(b_ada, chip * n_ada, n_ada, axis=1).reshape(DEPTH, 1, n_ada)
    mod_cols, silu_c = _ada_fwd(c_all, w_ada, b_loc)
    got = _allgather8(mod_cols.reshape(DEPTH * N_DEV, n_ada), "gather_mod").reshape(N_DEV, DEPTH, N_DEV, n_ada)[0::2]
    mod_all = got.transpose(1, 2, 0, 3).reshape(DEPTH, N_DEV, 6, D)
    mods = lax.dynamic_index_in_dim(mod_all, dev, axis=1, keepdims=False)

    comm = _StepComm((w_in, w_branch, w_out, w_ff1, w_ff2), w_in0, jnp.stack([ci, chip]).astype(jnp.int32), mods)
    comm.small = [(g_mix_pre[l], g_mix_post[l], g_ff_pre[l], g_ff_post[l], b_f[l], w_pool[l], pool_scale[l], conv_full[l]) for l in range(DEPTH)]
    loss_part, grad_x, dmods, bigs, smalls = _local_step(x[0], loss_target[0], mods, comm)

    small_parts = [smalls[l][name] for name in _SMALL for l in range(DEPTH)] + [loss_part.reshape(1)]
    packed = _tie(_pack([dmods] + small_parts), comm.jobs["rs0_mix"].token)
    gathered = _allgather8(packed, "gather_small")
    dmod_all = gathered.reshape(N_DEV, -1)[:, :dmods.size].reshape(N_DEV, DEPTH, 6 * D)
    summed = _unpack(_sum_devices(gathered), [dmods] + small_parts)
    grad_b_ada = summed[0].reshape(DEPTH, 6 * D)
    loss = summed[-1][0]
    small_grads = {name: jnp.stack(summed[1 + 2 * i:3 + 2 * i]) for i, name in enumerate(_SMALL)}
    small_grads["conv_w"] = lax.dynamic_slice_in_dim(small_grads["conv_w"], chip * (CONV_WIDTH // N_CHIPS), CONV_WIDTH // N_CHIPS, axis=2)

    dmod_loc = lax.dynamic_slice_in_dim(dmod_all.transpose(1, 0, 2), chip * n_ada, n_ada, axis=2)
    tail_token = comm.finish_sums(grad_b_ada)
    silu_pad = _tie(jnp.pad(silu_c, ((0, 128 - N_DEV), (0, 0))), tail_token)
    dmod_pad = jnp.pad(dmod_loc.transpose(1, 0, 2).reshape(N_DEV, DEPTH * n_ada), ((0, 128 - N_DEV), (0, 0)))
    grad_w_ada = _mm(silu_pad, dmod_pad, ta=True, out_split=DEPTH, name="mm_ada_dw")

    grads = dict(w_ada=grad_w_ada, b_ada=grad_b_ada, **small_grads)
    weights = dict(w_ada=w_ada, b_ada=b_ada, g_mix_pre=g_mix_pre, g_mix_post=g_mix_post, g_ff_pre=g_ff_pre, g_ff_post=g_ff_post, w_in=w_in,
                   b_f=b_f, w_pool=w_pool, pool_scale=pool_scale, conv_w=conv_w, w_branch=w_branch, w_out=w_out, w_ff1=w_ff1, w_ff2=w_ff2)
    m_in = dict(w_ada=m_w_ada, b_ada=m_b_ada, g_mix_pre=m_g_mix_pre, g_mix_post=m_g_mix_post, g_ff_pre=m_g_ff_pre, g_ff_post=m_g_ff_post,
                w_in=m_w_in, b_f=m_b_f, w_pool=m_w_pool, pool_scale=m_pool_scale, conv_w=m_conv_w, w_branch=m_w_branch, w_out=m_w_out,
                w_ff1=m_w_ff1, w_ff2=m_w_ff2)
    v_in = dict(w_ada=v_w_ada, b_ada=v_b_ada, g_mix_pre=v_g_mix_pre, g_mix_post=v_g_mix_post, g_ff_pre=v_g_ff_pre, g_ff_post=v_g_ff_post,
                w_in=v_w_in, b_f=v_b_f, w_pool=v_w_pool, pool_scale=v_pool_scale, conv_w=v_conv_w, w_branch=v_w_branch, w_out=v_w_out,
                w_ff1=v_w_ff1, w_ff2=v_w_ff2)
    order = ("w_ada", "b_ada", "g_mix_pre", "g_mix_post", "g_ff_pre", "g_ff_post", "w_in", "b_f", "w_pool", "pool_scale", "conv_w",
             "w_branch", "w_out", "w_ff1", "w_ff2")
    delta, new_m, new_v = {}, {}, {}
    tiny = ("b_ada",) + _SMALL
    tiny_g = [_tie(grads[tiny[0]], tail_token)] + [grads[name] for name in tiny[1:]]
    res = _adamw_many([weights[name] for name in tiny], tiny_g, [m_in[name] for name in tiny], [v_in[name] for name in tiny], "adamw_small")
    for out, vals in zip((delta, new_m, new_v), res):
        out.update(zip(tiny, vals))
    delta["w_ada"], new_m["w_ada"], new_v["w_ada"] = _adamw(w_ada, grad_w_ada, m_w_ada, v_w_ada, "adamw_w_ada")
    comm.finish_ff(delta["w_ada"][0, :8, :128] + delta["b_ada"][0, :128])
    for name in ("w_ff1", "w_ff2", "w_in", "w_branch", "w_out"):
        if name == "w_in":
            comm.finish_mix(delta["w_ff2"][0, :8, :128])
        grads[name] = comm.grads[name]
        if name == "w_in":
            g_view = lax.optimization_barrier(_w_in_view(grads[name]))
            res = _adamw(_w_in_view(w_in), g_view, _w_in_view(m_w_in), _w_in_view(v_w_in), "adamw_w_in")
            grads[name], delta[name], new_m[name], new_v[name] = [_w_in_unview(t) for t in (g_view, *res)]
        else:
            delta[name], new_m[name], new_v[name] = _adamw(weights[name], grads[name], m_in[name], v_in[name], "adamw_" + name)

    return (loss, grad_x[None], *[grads[n] for n in order], *[delta[n] for n in order], *[new_m[n] for n in order],
            *[new_v[n] for n in order])
```

```python
from typing import NamedTuple

import jax
import jax.numpy as jnp
from jax import lax
from jax.experimental import pallas as pl
from jax.experimental.pallas import tpu as pltpu

F32 = jnp.float32
BF16 = jnp.bfloat16
MESH = pl.DeviceIdType.MESH

D = 1024
DEPTH = 2
HEADS = 8
HEAD_DIM = 64
A_WIDTH = 512
POOL_WIDTH = 256
CONV_WIDTH = 256
D_FF = 4096
IN_COLS = 5640
Z_GL, Z_QKV, Z_PC, Z_FL, Z_COLS = 0, 3072, 4608, 5632, 5760
RMS_EPS = 1e-6
NEG_INF = -1e30
ROW_TILE = 512
EW_ROWS = 256
N_CHIPS = 4
N_DEV = 8
V7X_VMEM_LIMIT = 48 * 1024 * 1024

ADAM_LR = 0.001
ADAM_B1 = 0.9
ADAM_B2 = 0.999
ADAM_EPS = 1e-08
ADAM_WD = 0.01
ADAM_STEP = 10

_HBM = pl.BlockSpec(memory_space=pltpu.HBM)


def _params(*sem):
    return pltpu.CompilerParams(dimension_semantics=sem, vmem_limit_bytes=V7X_VMEM_LIMIT)


def _pick(dim, cands):
    for cand in cands:
        if dim % cand == 0:
            return cand
    return dim


MM_TILE_BUDGET = 39 * 1024 * 1024


def _mm_tiles(m, n, k, k_unit, tn, a_size, b_size, out_size):
    for tk in (k_unit, 2048, 1152, 1024, 640, 512, 256, 128):
        if k_unit % tk:
            continue
        for tm in (2048, 1024, 512, 256, 128):
            if m % tm or ((m // tm) * (n // tn) < 2 and tm > 512):
                continue
            need = 2 * (tm * tk * a_size + tk * tn * b_size + tm * tn * out_size) + (0 if tk == k else 4 * tm * tn)
            if need <= MM_TILE_BUDGET and (tk == k_unit or tm >= 512):
                return tm, tk
    return 128, 128


def _mm(a, b, *, ta=False, tb=False, b_rows=None, b_split=1, out_split=1, out_dtype=F32, epilogue=None, extras=(), name):
    (k, m) = a.shape if ta else a.shape[::-1]
    b_row0, b_rows = (0, b.shape[-2]) if b_rows is None else b_rows
    b_cols = b.shape[-1] * b_split
    (n, k2) = (b_rows, b_cols) if tb else (b_cols, b_rows)
    assert k == k2, (a.shape, b.shape, ta, tb)
    n_unit = n // (out_split * (1 if tb else b_split))
    k_unit = k // (b_split if tb else 1)
    tn = _pick(n_unit, (1024, 1152, 768, 640, 512, 256, 128))
    tm, tk = _mm_tiles(m, n, k, k_unit, tn, a.dtype.itemsize, b.dtype.itemsize,
                       sum(jnp.dtype(dt).itemsize for dt in out_dtype) + 4 * len(extras) if epilogue else jnp.dtype(out_dtype).itemsize)
    nk = k // tk
    dims = (((0 if ta else 1,), (1 if tb else 0,)), ((), ()))

    def dot(a_ref, b_ref):
        b_val = b_ref[0] if b_split > 1 else b_ref[...]
        return lax.dot_general(a_ref[...].astype(BF16), b_val.astype(BF16), dims, preferred_element_type=F32)

    n_extra = len(extras)
    assert epilogue is None or out_split == 1

    def put(refs, val):
        if epilogue is not None:
            for o_ref, res in zip(refs[n_extra:], epilogue(val, *[r[...] for r in refs[:n_extra]])):
                o_ref[...] = res.astype(o_ref.dtype)
        elif out_split > 1:
            refs[0][0] = val.astype(refs[0].dtype)
        else:
            refs[0][...] = val.astype(refs[0].dtype)

    def body_single(a_ref, b_ref, *refs):
        put(refs, dot(a_ref, b_ref))

    def body_acc(a_ref, b_ref, *refs):
        kk = pl.program_id(2)
        acc_ref = refs[-1]

        @pl.when(kk == 0)
        def _():
            acc_ref[...] = jnp.zeros_like(acc_ref)

        acc_ref[...] += dot(a_ref, b_ref)

        @pl.when(kk == nk - 1)
        def _():
            put(refs[:-1], acc_ref[...])

    a_spec = pl.BlockSpec((tk, tm), lambda i, j, kk: (kk, i)) if ta else pl.BlockSpec((tm, tk), lambda i, j, kk: (i, kk))
    if b_split == 1:
        off = b_row0 // (tn if tb else tk)
        assert off * (tn if tb else tk) == b_row0
        b_spec = pl.BlockSpec((tn, tk), lambda i, j, kk: (j + off, kk)) if tb else pl.BlockSpec((tk, tn), lambda i, j, kk: (kk + off, j))
    elif tb:
        per = k_unit // tk
        b_spec = pl.BlockSpec((1, tn, tk), lambda i, j, kk: (kk // per, j, kk % per))
    else:
        per = n // b_split // tn
        b_spec = pl.BlockSpec((1, tk, tn), lambda i, j, kk: (j // per, kk, j % per))
    if out_split == 1:
        o_spec = pl.BlockSpec((tm, tn), lambda i, j, kk: (i, j))
        o_shape = None if epilogue is not None else jax.ShapeDtypeStruct((m, n), out_dtype)
    else:
        per_o = n // out_split // tn
        o_spec = pl.BlockSpec((1, tm, tn), lambda i, j, kk: (j // per_o, i, j % per_o))
        o_shape = jax.ShapeDtypeStruct((out_split, m, n // out_split), out_dtype)
    if epilogue is not None:
        o_shape = [jax.ShapeDtypeStruct((m, n), dt) for dt in out_dtype]
        o_spec = [o_spec] * len(out_dtype)
    return pl.pallas_call(
        body_single if nk == 1 else body_acc, name=name, grid=(m // tm, n // tn, nk),
        in_specs=[a_spec, b_spec] + [pl.BlockSpec((tm, tn), lambda i, j, kk: (i, j))] * n_extra, out_specs=o_spec, out_shape=o_shape,
        scratch_shapes=[] if nk == 1 else [pltpu.VMEM((tm, tn), F32)],
        compiler_params=_params("parallel", "parallel", "arbitrary"),
    )(a, b, *extras)


def _ew(fn, ins, out_dtypes, name, tc=None):
    shape = ins[0].shape
    lead, (rows, cols) = shape[:-2], shape[-2:]
    tc = cols if tc is None else tc
    if tc > 1024:
        tr = _pick(rows, (EW_ROWS, 128, 8))
    elif tc > 128:
        tr = _pick(rows, (2 * EW_ROWS, EW_ROWS, 128, 8))
    else:
        tr = _pick(rows, (4096, 2256, 2048, 1024, EW_ROWS, 8))
    n_in = len(ins)

    def body(*refs):
        res = fn(*[r[...] for r in refs[:n_in]])
        for o_ref, val in zip(refs[n_in:], res):
            o_ref[...] = val.astype(o_ref.dtype)

    if lead:
        spec = pl.BlockSpec((None, tr, tc), lambda l, i, j: (l, i, j))
    else:
        spec = pl.BlockSpec((tr, tc), lambda i, j: (i, j))
    return pl.pallas_call(
        body, name=name, grid=lead + (rows // tr, cols // tc),
        in_specs=[spec] * n_in, out_specs=[spec] * len(out_dtypes),
        out_shape=[jax.ShapeDtypeStruct(shape, dt) for dt in out_dtypes],
        compiler_params=_params(*(["parallel"] * (len(lead) + 2))),
    )(*ins)


def _relu2_fwd(a):
    r = jnp.maximum(a, 0.0)
    return a, r * r


def _relu2_bwd(dr, a):
    return (dr * (2.0 * jnp.maximum(a, 0.0)),)


def _adamw_math(w, g, m, v):
    m = ADAM_B1 * m + (1.0 - ADAM_B1) * g
    v = ADAM_B2 * v + (1.0 - ADAM_B2) * (g * g)
    m_hat = m / (1.0 - ADAM_B1 ** ADAM_STEP)
    v_hat = v / (1.0 - ADAM_B2 ** ADAM_STEP)
    delta = -ADAM_LR * (m_hat / (jnp.sqrt(v_hat) + ADAM_EPS) + ADAM_WD * w)
    return delta, m, v


def _adamw(w, g, m, v, name):
    return _ew(_adamw_math, [w, g, m, v], [F32, F32, F32], name)


def _adamw_layers(w, g_layers, m, v, name):
    depth, rows, cols = w.shape
    tr = _pick(rows, (2 * EW_ROWS, EW_ROWS, 128, 8)) if cols <= 1024 else _pick(rows, (EW_ROWS, 128, 8))

    def body(w_ref, *refs):
        g_refs, (m_ref, v_ref, d_ref, mo_ref, vo_ref, go_ref) = refs[:depth], refs[depth:]
        layer = pl.program_id(0)
        g = g_refs[0][...]
        for l in range(1, depth):
            g = jnp.where(layer == l, g_refs[l][...], g)
        d_ref[...], mo_ref[...], vo_ref[...] = _adamw_math(w_ref[...], g, m_ref[...], v_ref[...])
        go_ref[...] = g

    spec = pl.BlockSpec((None, tr, cols), lambda l, i: (l, i, 0))
    g_specs = [pl.BlockSpec((tr, cols), lambda l, i, k=k: (jnp.where(l == k, i, 0), 0)) for k in range(depth)]
    return pl.pallas_call(
        body, name=name, grid=(depth, rows // tr),
        in_specs=[spec] + g_specs + [spec, spec], out_specs=[spec] * 4,
        out_shape=[jax.ShapeDtypeStruct(w.shape, F32)] * 4, compiler_params=_params("arbitrary", "arbitrary"),
    )(w, *g_layers, m, v)


def _adamw_many(ws, gs, ms, vs, name):
    n = len(ws)

    def body(*refs):
        for i in range(n):
            res = _adamw_math(*[refs[k * n + i][...] for k in range(4)])
            for k in range(3):
                refs[(4 + k) * n + i][...] = res[k]

    outs = pl.pallas_call(
        body, name=name, out_shape=[jax.ShapeDtypeStruct(w.shape, F32) for w in ws] * 3,
        compiler_params=pltpu.CompilerParams(vmem_limit_bytes=V7X_VMEM_LIMIT),
    )(*ws, *gs, *ms, *vs)
    return outs[:n], outs[n:2 * n], outs[2 * n:]


def _row_spec(cols, block=0):
    return pl.BlockSpec((ROW_TILE, cols), lambda i, block=block: (i, block))


def _vec_spec(cols):
    return pl.BlockSpec((1, cols), lambda i: (0, 0))


def _vec_args(*vecs):
    arrays = [v[0] if isinstance(v, tuple) else v for v in vecs]
    specs = [pl.BlockSpec((None, 1, D), lambda i, row=v[1]: (row, 0, 0)) if isinstance(v, tuple) else _vec_spec(D) for v in vecs]
    return arrays, specs


def _sum_spec(cols):
    return pl.BlockSpec((8, cols), lambda i: (0, 0))


def _rstd(x):
    return lax.rsqrt(jnp.mean(x * x, axis=-1, keepdims=True) + RMS_EPS)


def _modnorm_fwd(x, g, shift, scale, name):
    s = x.shape[0]

    def body(x_ref, g_ref, sh_ref, sc_ref, h_ref):
        xv = x_ref[...]
        n = xv * _rstd(xv)
        h_ref[...] = ((n * g_ref[...]) * (1.0 + sc_ref[...]) + sh_ref[...]).astype(BF16)

    vecs, vec_specs = _vec_args(g, shift, scale)
    return pl.pallas_call(
        body, name=name, grid=(s // ROW_TILE,),
        in_specs=[_row_spec(D)] + vec_specs, out_specs=_row_spec(D),
        out_shape=jax.ShapeDtypeStruct((s, D), BF16), compiler_params=_params("parallel"),
    )(x, *vecs)


def _post_fwd(x, y, g, gate, name):
    s = x.shape[0]

    def body(x_ref, y_ref, g_ref, gate_ref, o_ref):
        yv = y_ref[...]
        o_ref[...] = x_ref[...] + gate_ref[...] * ((yv * _rstd(yv)) * g_ref[...])

    vecs, vec_specs = _vec_args(g, gate)
    return pl.pallas_call(
        body, name=name, grid=(s // ROW_TILE,),
        in_specs=[_row_spec(D), _row_spec(D)] + vec_specs, out_specs=_row_spec(D),
        out_shape=jax.ShapeDtypeStruct((s, D), F32), compiler_params=_params("parallel"),
    )(x, y, *vecs)


def _post_bwd(dxo, y, g, gate, name):
    s = dxo.shape[0]

    def body(d_ref, y_ref, g_ref, gate_ref, dy_ref, sum_ref):
        @pl.when(pl.program_id(0) == 0)
        def _():
            sum_ref[...] = jnp.zeros_like(sum_ref)

        dv, yv = d_ref[...], y_ref[...]
        r = _rstd(yv)
        n = yv * r
        sum_ref[0:1, :] += jnp.sum(dv * (n * g_ref[...]), axis=0, keepdims=True)
        sum_ref[1:2, :] += jnp.sum((dv * gate_ref[...]) * n, axis=0, keepdims=True)
        dn = (dv * gate_ref[...]) * g_ref[...]
        dy_ref[...] = (r * (dn - n * jnp.mean(dn * n, axis=-1, keepdims=True))).astype(BF16)

    vecs, vec_specs = _vec_args(g, gate)
    return pl.pallas_call(
        body, name=name, grid=(s // ROW_TILE,),
        in_specs=[_row_spec(D), _row_spec(D)] + vec_specs,
        out_specs=[_row_spec(D), _sum_spec(D)],
        out_shape=[jax.ShapeDtypeStruct((s, D), BF16), jax.ShapeDtypeStruct((8, D), F32)],
        compiler_params=_params("arbitrary"),
    )(dxo, y, *vecs)


def _modnorm_bwd(dh, x, dxo, g, scale, name):
    s = dh.shape[0]

    def body(dh_ref, x_ref, d_ref, g_ref, sc_ref, dx_ref, sum_ref):
        @pl.when(pl.program_id(0) == 0)
        def _():
            sum_ref[...] = jnp.zeros_like(sum_ref)

        dhv, xv = dh_ref[...], x_ref[...]
        r = _rstd(xv)
        n = xv * r
        one_sc = 1.0 + sc_ref[...]
        sum_ref[0:1, :] += jnp.sum(dhv, axis=0, keepdims=True)
        sum_ref[1:2, :] += jnp.sum(dhv * (n * g_ref[...]), axis=0, keepdims=True)
        sum_ref[2:3, :] += jnp.sum((dhv * one_sc) * n, axis=0, keepdims=True)
        dn = (dhv * one_sc) * g_ref[...]
        dx_ref[...] = d_ref[...] + r * (dn - n * jnp.mean(dn * n, axis=-1, keepdims=True))

    vecs, vec_specs = _vec_args(g, scale)
    return pl.pallas_call(
        body, name=name, grid=(s // ROW_TILE,),
        in_specs=[_row_spec(D), _row_spec(D), _row_spec(D)] + vec_specs,
        out_specs=[_row_spec(D), _sum_spec(D)],
        out_shape=[jax.ShapeDtypeStruct((s, D), F32), jax.ShapeDtypeStruct((8, D), F32)],
        compiler_params=_params("arbitrary"),
    )(dh, x, dxo, *vecs)


def _post_pre_fwd(x, y, g_post, gate, g_pre, shift, scale, name):
    s = x.shape[0]

    def body(x_ref, y_ref, gp_ref, gate_ref, g_ref, sh_ref, sc_ref, o_ref, h_ref):
        yv = y_ref[...]
        xo = x_ref[...] + gate_ref[...] * ((yv * _rstd(yv)) * gp_ref[...])
        o_ref[...] = xo
        h_ref[...] = (((xo * _rstd(xo)) * g_ref[...]) * (1.0 + sc_ref[...]) + sh_ref[...]).astype(BF16)

    vecs, vec_specs = _vec_args(g_post, gate, g_pre, shift, scale)
    return pl.pallas_call(
        body, name=name, grid=(s // ROW_TILE,),
        in_specs=[_row_spec(D), _row_spec(D)] + vec_specs, out_specs=[_row_spec(D), _row_spec(D)],
        out_shape=[jax.ShapeDtypeStruct((s, D), F32), jax.ShapeDtypeStruct((s, D), BF16)], compiler_params=_params("parallel"),
    )(x, y, *vecs)


def _pre_post_bwd(dh, x, dxo, g_pre, scale, y, g_post, gate, name):
    s = dh.shape[0]

    def body(dh_ref, x_ref, d_ref, y_ref, g_ref, sc_ref, gp_ref, gate_ref, dx_ref, dy_ref, sum_ref):
        @pl.when(pl.program_id(0) == 0)
        def _():
            sum_ref[...] = jnp.zeros_like(sum_ref)

        dhv, xv = dh_ref[...], x_ref[...]
        r = _rstd(xv)
        n = xv * r
        one_sc = 1.0 + sc_ref[...]
        sum_ref[0:1, :] += jnp.sum(dhv, axis=0, keepdims=True)
        sum_ref[1:2, :] += jnp.sum(dhv * (n * g_ref[...]), axis=0, keepdims=True)
        sum_ref[2:3, :] += jnp.sum((dhv * one_sc) * n, axis=0, keepdims=True)
        dn = (dhv * one_sc) * g_ref[...]
        dv = d_ref[...] + r * (dn - n * jnp.mean(dn * n, axis=-1, keepdims=True))
        dx_ref[...] = dv

        yv = y_ref[...]
        ry = _rstd(yv)
        ny = yv * ry
        sum_ref[3:4, :] += jnp.sum(dv * (ny * gp_ref[...]), axis=0, keepdims=True)
        sum_ref[4:5, :] += jnp.sum((dv * gate_ref[...]) * ny, axis=0, keepdims=True)
        dny = (dv * gate_ref[...]) * gp_ref[...]
        dy_ref[...] = (ry * (dny - ny * jnp.mean(dny * ny, axis=-1, keepdims=True))).astype(BF16)

    vecs, vec_specs = _vec_args(g_pre, scale, g_post, gate)
    return pl.pallas_call(
        body, name=name, grid=(s // ROW_TILE,),
        in_specs=[_row_spec(D)] * 4 + vec_specs,
        out_specs=[_row_spec(D), _row_spec(D), _sum_spec(D)],
        out_shape=[jax.ShapeDtypeStruct((s, D), F32), jax.ShapeDtypeStruct((s, D), BF16), jax.ShapeDtypeStruct((8, D), F32)],
        compiler_params=_params("arbitrary"),
    )(dh, x, dxo, y, *vecs)


def _loss_head(y, target):
    s = y.shape[0]

    def body(y_ref, t_ref, dy_ref, sum_ref):
        @pl.when(pl.program_id(0) == 0)
        def _():
            sum_ref[...] = jnp.zeros_like(sum_ref)

        err = y_ref[...] - t_ref[...]
        dy_ref[...] = err * (1.0 / D)
        sum_ref[...] += jnp.sum(err * err)

    return pl.pallas_call(
        body, name="loss_head", grid=(s // ROW_TILE,),
        in_specs=[_row_spec(D), _row_spec(D)],
        out_specs=[_row_spec(D), pl.BlockSpec((8, 128), lambda i: (0, 0))],
        out_shape=[jax.ShapeDtypeStruct((s, D), F32), jax.ShapeDtypeStruct((8, 128), F32)],
        compiler_params=_params("arbitrary"),
    )(y, target)


def _merge_fwd(z, pa, pb, pc):
    s = z.shape[0]

    def body(g0_ref, g1_ref, g2_ref, pa_ref, pb_ref, pc_ref, o_ref):
        o_ref[...] = (jax.nn.sigmoid(g0_ref[...]) * pa_ref[...] + jax.nn.sigmoid(g1_ref[...]) * pb_ref[...]
                      + jax.nn.sigmoid(g2_ref[...]) * pc_ref[...]).astype(BF16)

    return pl.pallas_call(
        body, name="merge_fwd", grid=(s // ROW_TILE,),
        in_specs=[_row_spec(D, 0), _row_spec(D, 1), _row_spec(D, 2), _row_spec(D), _row_spec(D), _row_spec(D)],
        out_specs=_row_spec(D), out_shape=jax.ShapeDtypeStruct((s, D), BF16),
        compiler_params=_params("parallel"),
    )(z, z, z, pa, pb, pc)


def _merge_bwd(dm, z, pa, pb, pc):
    s = z.shape[0]

    def body(dm_ref, g0_ref, g1_ref, g2_ref, pa_ref, pb_ref, pc_ref, dgl_ref, da_ref, db_ref, dc_ref):
        dmv = dm_ref[...]
        for i, (g_ref, p_ref, d_ref) in enumerate(((g0_ref, pa_ref, da_ref), (g1_ref, pb_ref, db_ref), (g2_ref, pc_ref, dc_ref))):
            gate = jax.nn.sigmoid(g_ref[...])
            dgl_ref[:, i * D:(i + 1) * D] = ((dmv * p_ref[...]) * (gate * (1.0 - gate))).astype(BF16)
            d_ref[...] = (dmv * gate).astype(BF16)

    return pl.pallas_call(
        body, name="merge_bwd", grid=(s // ROW_TILE,),
        in_specs=[_row_spec(D), _row_spec(D, 0), _row_spec(D, 1), _row_spec(D, 2), _row_spec(D), _row_spec(D), _row_spec(D)],
        out_specs=[_row_spec(3 * D), _row_spec(D), _row_spec(D), _row_spec(D)],
        out_shape=[jax.ShapeDtypeStruct((s, Z_COLS), BF16)] + [jax.ShapeDtypeStruct((s, D), BF16)] * 3,
        compiler_params=_params("parallel"),
    )(dm, z, z, z, pa, pb, pc)


def _shift_down(v, n):
    row = lax.broadcasted_iota(jnp.int32, v.shape, 0)
    return jnp.where(row >= n, pltpu.roll(v, n, axis=0), 0.0)


def _shift_up(v, n):
    s = v.shape[0]
    row = lax.broadcasted_iota(jnp.int32, v.shape, 0)
    return jnp.where(row < s - n, pltpu.roll(v, s - n, axis=0), 0.0)


def _log_sigmoid(v):
    return jnp.minimum(v, 0.0) - jnp.log1p(jnp.exp(-jnp.abs(v)))


def _cumf_fwd(fl, bias):
    s = fl.shape[0]

    def body(fl_ref, b_ref, o_ref):
        acc = _log_sigmoid(fl_ref[...] + b_ref[...])
        step = 1
        while step < s:
            acc = acc + _shift_down(acc, step)
            step *= 2
        o_ref[...] = acc

    return pl.pallas_call(body, name="cumf_fwd", out_shape=jax.ShapeDtypeStruct((s, 128), F32),
                          compiler_params=pltpu.CompilerParams(vmem_limit_bytes=V7X_VMEM_LIMIT))(fl, bias)


def _cumf_bwd(dcum, fl, bias):
    s = fl.shape[0]

    def body(d_ref, fl_ref, b_ref, dfl_ref, db_ref):
        acc = d_ref[...]
        step = 1
        while step < s:
            acc = acc + _shift_up(acc, step)
            step *= 2
        dfl = acc * jax.nn.sigmoid(-(fl_ref[...] + b_ref[...]))
        dfl_ref[...] = dfl.astype(BF16)
        db_ref[...] = jnp.broadcast_to(jnp.sum(dfl, axis=0, keepdims=True), (8, 128))

    return pl.pallas_call(
        body, name="cumf_bwd",
        out_shape=[jax.ShapeDtypeStruct((s, 128), BF16), jax.ShapeDtypeStruct((8, 128), F32)],
        compiler_params=pltpu.CompilerParams(vmem_limit_bytes=V7X_VMEM_LIMIT))(dcum, fl, bias)


def _pool_windows(v, shift):
    s2 = v + shift(v, 1)
    s4 = s2 + shift(s2, 2)
    s8 = s4 + shift(s4, 4)
    s16 = s8 + shift(s8, 8)
    group = lax.broadcasted_iota(jnp.int32, v.shape, 1) // 64
    return jnp.where(group == 0, s2, jnp.where(group == 1, s4, jnp.where(group == 2, s8, s16)))


def _pool_count(shape):
    group = lax.broadcasted_iota(jnp.int32, shape, 1) // 64
    window = jnp.where(group == 0, 2.0, jnp.where(group == 1, 4.0, jnp.where(group == 2, 8.0, 16.0)))
    t1 = (lax.broadcasted_iota(jnp.int32, shape, 0) + 1).astype(F32)
    return jnp.minimum(t1, window)


def _pc_specs(s):
    zcol = lambda blk: pl.BlockSpec((s, 256), lambda i, blk=blk: (0, blk))
    first = Z_PC // 256
    return [zcol(first), zcol(first + 1), zcol(first + 2), zcol(first + 3),
            pl.BlockSpec((256, 256), lambda i: (0, 0)), pl.BlockSpec((1, 256), lambda i: (0, 0)),
            pl.BlockSpec((3, 256), lambda i: (0, 0))]


def _poolconv_fwd(z, wbd, pscale, convw):
    s = z.shape[0]

    def body(pu_ref, ch_ref, cb_ref, cc_ref, w_ref, ps_ref, cw_ref, yb_ref, yc_ref):
        u = pu_ref[...]
        p = _pool_windows(u, _shift_down) / _pool_count(u.shape) - u
        yb = jnp.dot(p.astype(BF16), w_ref[...].astype(BF16), preferred_element_type=F32) * ps_ref[...]
        yb_ref[...] = yb.astype(BF16)
        uc = cc_ref[...] * ch_ref[...]
        cw = cw_ref[...]
        conv = cw[0:1, :] * _shift_down(uc, 2) + cw[1:2, :] * _shift_down(uc, 1) + cw[2:3, :] * uc
        yc_ref[...] = (cb_ref[...] * conv).astype(BF16)

    out = pl.BlockSpec((s, 256), lambda i: (0, 0))
    return pl.pallas_call(
        body, name="poolconv_fwd", grid=(1,), in_specs=_pc_specs(s), out_specs=[out, out],
        out_shape=[jax.ShapeDtypeStruct((s, 256), BF16)] * 2, compiler_params=_params("arbitrary"),
    )(z, z, z, z, wbd, pscale, convw)


def _poolconv_bwd(dyb, dyc, z, wbd, pscale, convw):
    s = z.shape[0]

    def body(dyb_ref, dyc_ref, pu_ref, ch_ref, cb_ref, cc_ref, w_ref, ps_ref, cw_ref, dz_ref, dw_ref, dps_ref, dcw_ref):
        u = pu_ref[...]
        count = _pool_count(u.shape)
        p = (_pool_windows(u, _shift_down) / count - u).astype(BF16)
        wb = w_ref[...].astype(BF16)
        dyb_v = dyb_ref[...]
        pw = jnp.dot(p, wb, preferred_element_type=F32)
        dps_ref[...] = jnp.broadcast_to(jnp.sum(dyb_v * pw, axis=0, keepdims=True), (8, 256))
        dys = (dyb_v * ps_ref[...]).astype(BF16)
        dp = lax.dot_general(dys, wb, (((1,), (1,)), ((), ())), preferred_element_type=F32)
        dw_ref[...] = lax.dot_general(p, dys, (((0,), (0,)), ((), ())), preferred_element_type=F32)
        dz_ref[:, 0:256] = (_pool_windows(dp / count, _shift_up) - dp).astype(BF16)

        ch, cb, cc = ch_ref[...], cb_ref[...], cc_ref[...]
        uc = cc * ch
        cw = cw_ref[...]
        u2, u1 = _shift_down(uc, 2), _shift_down(uc, 1)
        conv = cw[0:1, :] * u2 + cw[1:2, :] * u1 + cw[2:3, :] * uc
        dyc_v = dyc_ref[...]
        dconv = dyc_v * cb
        du = cw[0:1, :] * _shift_up(dconv, 2) + cw[1:2, :] * _shift_up(dconv, 1) + cw[2:3, :] * dconv
        dz_ref[:, 256:512] = (du * cc).astype(BF16)
        dz_ref[:, 512:768] = (dyc_v * conv).astype(BF16)
        dz_ref[:, 768:1024] = (du * ch).astype(BF16)
        dcw_ref[...] = jnp.zeros_like(dcw_ref)
        dcw_ref[0:1, :] = jnp.sum(dconv * u2, axis=0, keepdims=True)
        dcw_ref[1:2, :] = jnp.sum(dconv * u1, axis=0, keepdims=True)
        dcw_ref[2:3, :] = jnp.sum(dconv * uc, axis=0, keepdims=True)

    blk = lambda r, c: pl.BlockSpec((r, c), lambda i: (0, 0))
    return pl.pallas_call(
        body, name="poolconv_bwd", grid=(1,),
        in_specs=[blk(s, 256), blk(s, 256)] + _pc_specs(s),
        out_specs=[blk(s, 1024), blk(256, 256), blk(8, 256), blk(8, 256)],
        out_shape=[jax.ShapeDtypeStruct((s, 1024), BF16), jax.ShapeDtypeStruct((256, 256), F32),
                   jax.ShapeDtypeStruct((8, 256), F32), jax.ShapeDtypeStruct((8, 256), F32)],
        compiler_params=_params("arbitrary"),
    )(dyb, dyc, z, z, z, z, wbd, pscale, convw)


_NT = (((1,), (1,)), ((), ()))
_TN = (((0,), (0,)), ((), ()))


ATT_Q, ATT_K = 256, 256
ATT_HEADS_BWD = 8
ATT_HEADS = 8


def _att_logits(q, k, fr, q0, k0, masked):
    logits = lax.dot_general(q, k, _NT, preferred_element_type=F32) - fr
    if not masked:
        return logits
    row = q0 + lax.broadcasted_iota(jnp.int32, logits.shape, 0)
    col = k0 + lax.broadcasted_iota(jnp.int32, logits.shape, 1)
    return jnp.where(row >= col, logits, NEG_INF)


def _causal_sweep(step, qi, init):
    n_full = (qi * ATT_Q) // ATT_K
    carry = lax.fori_loop(0, n_full, lambda j, carry: step(j, carry, False), init)
    return step(n_full, carry, True)


HEAD_PAIRS = HEADS // 2


def _lane_pick(v, lane, idx):
    return jnp.sum(jnp.where(lane == idx, v, 0.0), axis=-1, keepdims=True)


def _lane_put(lane, idx, col):
    return jnp.where(lane == idx, col, 0.0)


def _split_heads(v, low):
    zero = jnp.zeros_like(v)
    return jnp.where(low, v, zero), jnp.where(low, zero, v)


def _attn_fwd(qkv, fr):
    s = qkv.shape[0]
    nk = s // ATT_K
    width = ATT_HEADS * HEAD_DIM
    groups = HEADS // ATT_HEADS

    def body(q_ref, k_ref, v_ref, fr_ref, o_ref, lse_ref):
        qi, grp = pl.program_id(0), pl.program_id(1)
        lane = lax.broadcasted_iota(jnp.int32, (ATT_Q, 128), 1)
        low = lane < HEAD_DIM
        qs = []
        for pr in range(ATT_HEADS // 2):
            qs += _split_heads(q_ref[:, 128 * pr:128 * (pr + 1)] * (HEAD_DIM ** -0.5), low)

        def step(j, carry, masked):
            k0 = pl.multiple_of(j * ATT_K, ATT_K)
            out = []
            for h in range(ATT_HEADS):
                cols = slice(128 * (h // 2), 128 * (h // 2 + 1))
                m, l, acc = carry[h]
                logits = _att_logits(qs[h], k_ref[pl.ds(k0, ATT_K), cols], fr_ref[h, pl.ds(j, 1), :], qi * ATT_Q, k0, masked)
                m_new = jnp.maximum(m, jnp.max(logits, axis=-1, keepdims=True))
                p = jnp.exp(logits - m_new)
                alpha = jnp.exp(m - m_new)
                l = alpha * l + jnp.sum(p, axis=-1, keepdims=True)
                acc = alpha * acc + jnp.dot(p.astype(BF16), v_ref[pl.ds(k0, ATT_K), cols], preferred_element_type=F32)
                out.append((m_new, l, acc))
            return tuple(out)

        one = (jnp.full((ATT_Q, 1), NEG_INF, F32), jnp.zeros((ATT_Q, 1), F32), jnp.zeros((ATT_Q, 128), F32))
        done = _causal_sweep(step, qi, (one,) * ATT_HEADS)

        @pl.when(grp == 0)
        def _():
            lse_ref[...] = jnp.zeros_like(lse_ref)

        lse = jnp.zeros((ATT_Q, 128), F32)
        for pr in range(ATT_HEADS // 2):
            (m0, l0, acc0), (m1, l1, acc1) = done[2 * pr], done[2 * pr + 1]
            o_ref[:, 128 * pr:128 * (pr + 1)] = jnp.where(low, acc0 / l0, acc1 / l1)
            head = ATT_HEADS * grp + 2 * pr
            lse = lse + _lane_put(lane, head, m0 + jnp.log(l0)) + _lane_put(lane, head + 1, m1 + jnp.log(l1))
        lse_ref[...] += lse

    return pl.pallas_call(
        body, name="attn_fwd", grid=(s // ATT_Q, groups),
        in_specs=[pl.BlockSpec((ATT_Q, width), lambda i, g: (i, g)),
                  pl.BlockSpec((s, width), lambda i, g: (0, groups + g)),
                  pl.BlockSpec((s, width), lambda i, g: (0, 2 * groups + g)),
                  pl.BlockSpec((ATT_HEADS, nk, ATT_K), lambda i, g: (g, 0, 0))],
        out_specs=[pl.BlockSpec((ATT_Q, width), lambda i, g: (i, g)), pl.BlockSpec((ATT_Q, 128), lambda i, g: (i, 0))],
        out_shape=[jax.ShapeDtypeStruct((s, A_WIDTH), F32), jax.ShapeDtypeStruct((s, 128), F32)],
        compiler_params=_params("parallel", "arbitrary"),
    )(qkv, qkv, qkv, fr)


def _attn_bwd(qkv, do, o, lse, fr):
    s = qkv.shape[0]
    nk = s // ATT_K
    scale = HEAD_DIM ** -0.5
    heads = ATT_HEADS_BWD
    width = heads * HEAD_DIM
    groups = HEADS // heads

    def body(q_ref, k_ref, v_ref, do_ref, o_ref, lse_ref, fr_ref, dq_ref, dk_ref, dv_ref, dfc_ref, dfr_ref, dk_acc, dv_acc):
        grp = pl.program_id(0)
        lane = lax.broadcasted_iota(jnp.int32, (ATT_Q, 128), 1)
        low = lane < HEAD_DIM
        low_t = lax.broadcasted_iota(jnp.int32, (128, ATT_Q), 0) < HEAD_DIM
        dk_acc[...] = jnp.zeros_like(dk_acc)
        dv_acc[...] = jnp.zeros_like(dv_acc)
        dfr_ref[...] = jnp.zeros_like(dfr_ref)

        @pl.when(grp == 0)
        def _():
            dfc_ref[...] = jnp.zeros_like(dfc_ref)

        def outer(i, carry):
            q0 = pl.multiple_of(i * ATT_Q, ATT_Q)
            rows = pl.ds(q0, ATT_Q)
            lsev = lse_ref[rows, :]
            qts, dots, qs, dos, deltas, lses = [], [], [], [], [], []
            for pr in range(heads // 2):
                pcols = slice(128 * pr, 128 * (pr + 1))
                q2, do2 = q_ref[rows, pcols] * scale, do_ref[rows, pcols]
                prod = do2 * o_ref[rows, pcols]
                deltas += [jnp.sum(jnp.where(low, prod, 0.0), axis=-1, keepdims=True),
                           jnp.sum(jnp.where(low, 0.0, prod), axis=-1, keepdims=True)]
                dob2 = do2.astype(BF16)
                qts += _split_heads(q2.astype(F32).T.astype(BF16), low_t)
                dots += _split_heads(do2.T.astype(BF16), low_t)
                qs += _split_heads(q2, low)
                dos += _split_heads(dob2, low)
                lses += [_lane_pick(lsev, lane, heads * grp + 2 * pr), _lane_pick(lsev, lane, heads * grp + 2 * pr + 1)]

            def inner(j, carry, masked):
                k0 = pl.multiple_of(j * ATT_K, ATT_K)
                krows = pl.ds(k0, ATT_K)
                out, dkt, dvt = [], [], []
                for h in range(heads):
                    pcols = slice(128 * (h // 2), 128 * (h // 2 + 1))
                    dq, dfc = carry[h]
                    k2 = k_ref[krows, pcols]
                    p = jnp.exp(_att_logits(qs[h], k2, fr_ref[h, pl.ds(j, 1), :], q0, k0, masked) - lses[h])
                    dp = lax.dot_general(dos[h], v_ref[krows, pcols], _NT, preferred_element_type=F32)
                    ds = p * (dp - deltas[h])
                    dsb = ds.astype(BF16)
                    dkt.append(jnp.dot(qts[h], dsb, preferred_element_type=F32))
                    dvt.append(jnp.dot(dots[h], p.astype(BF16), preferred_element_type=F32))
                    dfr_ref[h, pl.ds(j, 1), :] -= jnp.sum(ds, axis=0, keepdims=True)
                    out.append((dq + jnp.dot(dsb, k2, preferred_element_type=F32), dfc + (ds[:, :128] + ds[:, 128:])))
                for pr in range(heads // 2):
                    prows = slice(128 * pr, 128 * (pr + 1))
                    dk_acc[j, prows, :] += dkt[2 * pr] + dkt[2 * pr + 1]
                    dv_acc[j, prows, :] += dvt[2 * pr] + dvt[2 * pr + 1]
                return tuple(out)

            one = (jnp.zeros((ATT_Q, 128), F32), jnp.zeros((ATT_Q, 128), F32))
            done = _causal_sweep(inner, i, (one,) * heads)
            dfc = jnp.zeros((ATT_Q, 128), F32)
            for pr in range(heads // 2):
                (dq0, dfc0), (dq1, dfc1) = done[2 * pr], done[2 * pr + 1]
                dq_ref[rows, 128 * pr:128 * (pr + 1)] = (jnp.where(low, dq0, dq1) * scale).astype(BF16)
                head = heads * grp + 2 * pr
                dfc = (dfc + _lane_put(lane, head, jnp.sum(dfc0, axis=-1, keepdims=True))
                       + _lane_put(lane, head + 1, jnp.sum(dfc1, axis=-1, keepdims=True)))
            dfc_ref[rows, :] += dfc
            return carry

        lax.fori_loop(0, s // ATT_Q, outer, 0)
        for j in range(nk):
            for pr in range(heads // 2):
                prows, pcols = slice(128 * pr, 128 * (pr + 1)), slice(128 * pr, 128 * (pr + 1))
                dk_ref[ATT_K * j:ATT_K * (j + 1), pcols] = dk_acc[j, prows, :].T.astype(BF16)
                dv_ref[ATT_K * j:ATT_K * (j + 1), pcols] = dv_acc[j, prows, :].T.astype(BF16)

    part = lambda first: pl.BlockSpec((s, width), lambda g, first=first: (0, first + g))
    whole = pl.BlockSpec((s, 128), lambda g: (0, 0))
    rowv = pl.BlockSpec((heads, nk, ATT_K), lambda g: (g, 0, 0))
    return pl.pallas_call(
        body, name="attn_bwd", grid=(groups,),
        in_specs=[part(0), part(groups), part(2 * groups), part(0), part(0), whole, rowv],
        out_specs=[part(0), part(0), part(0), whole, rowv],
        out_shape=[jax.ShapeDtypeStruct((s, A_WIDTH), BF16)] * 3 + [jax.ShapeDtypeStruct((s, 128), F32), jax.ShapeDtypeStruct((HEADS, nk, ATT_K), F32)],
        scratch_shapes=[pltpu.VMEM((nk, width, ATT_K), F32), pltpu.VMEM((nk, width, ATT_K), F32)],
        compiler_params=_params("arbitrary"),
    )(qkv, qkv, qkv, do, o, lse, fr)


def _ada_fwd(c_all, w_ada, b_loc):
    depth, _, n = w_ada.shape
    tn = 512

    def body(c_ref, w_ref, b_ref, o_ref, sc_ref):
        cv = c_ref[...]
        sc = cv * jax.nn.sigmoid(cv)
        sc_ref[...] = sc
        o_ref[0] = jnp.dot(sc.astype(BF16), w_ref[0].astype(BF16), preferred_element_type=F32) + b_ref[0]

    return pl.pallas_call(
        body, name="ada_fwd", grid=(depth, n // tn),
        in_specs=[pl.BlockSpec((N_DEV, D), lambda l, j: (0, 0)), pl.BlockSpec((1, D, tn), lambda l, j: (l, 0, j)),
                  pl.BlockSpec((1, 1, tn), lambda l, j: (l, 0, j))],
        out_specs=[pl.BlockSpec((1, N_DEV, tn), lambda l, j: (l, 0, j)), pl.BlockSpec((N_DEV, D), lambda l, j: (0, 0))],
        out_shape=[jax.ShapeDtypeStruct((depth, N_DEV, n), F32), jax.ShapeDtypeStruct((N_DEV, D), F32)],
        compiler_params=_params("arbitrary", "arbitrary"),
    )(c_all, w_ada, b_loc)


def _sum_devices(gathered):
    n = gathered.shape[1]
    tn = _pick(n, (1408, 1024, 640, 512, 128))

    def body(g_ref, o_ref):
        acc = g_ref[0:8, :]
        for dev in range(1, N_DEV):
            acc = acc + g_ref[8 * dev:8 * dev + 8, :]
        o_ref[...] = acc

    return pl.pallas_call(
        body, name="sum_devices", grid=(n // tn,),
        in_specs=[pl.BlockSpec((8 * N_DEV, tn), lambda j: (0, j))], out_specs=pl.BlockSpec((8, tn), lambda j: (0, j)),
        out_shape=jax.ShapeDtypeStruct((8, n), F32), compiler_params=_params("parallel"),
    )(gathered)


def _place():
    x, y, c = lax.axis_index("x"), lax.axis_index("y"), lax.axis_index("c")
    chips = [(1 - x, y), (x, 1 - y), (1 - x, 1 - y)]
    return x, y, c, chips


def _allgather8(block, name, after=()):
    m_per, n = block.shape

    def body(x_ref, *rest):
        out_ref, send_sems, recv_sems, local_sem = rest[len(after):]
        x, y, c, chips = _place()
        me, sibling = (x, y, c), (x, y, 1 - c)

        def rows(px, py, pc):
            return out_ref.at[pl.ds((4 * px + 2 * py + pc) * m_per, m_per), :]

        def copy(k, blk, to, src=None):
            return pltpu.make_async_remote_copy(
                src_ref=rows(*blk) if src is None else src, dst_ref=rows(*blk),
                send_sem=send_sems.at[k], recv_sem=recv_sems.at[k], device_id=to, device_id_type=MESH)

        mine = pltpu.make_async_copy(x_ref, rows(*me), local_sem)
        mine.start()
        first = [copy(0, me, sibling, src=x_ref)]
        first += [copy(1 + j, me, (*chip, c), src=x_ref) for j, chip in enumerate(chips)]
        for cp in first:
            cp.start()
        passed = [copy(4 + j, (*chip, c), sibling) for j, chip in enumerate(chips)]
        for j, chip in enumerate(chips):
            copy(1 + j, (*chip, c), me).wait_recv()
            passed[j].start()
        copy(0, sibling, me).wait_recv()
        for j, chip in enumerate(chips):
            copy(4 + j, (*chip, 1 - c), me).wait_recv()
        for cp in first + passed:
            cp.wait_send()
        mine.wait()

    return pl.pallas_call(
        body, name=name, out_shape=jax.ShapeDtypeStruct((N_DEV * m_per, n), block.dtype),
        in_specs=[pl.BlockSpec(memory_space=pltpu.VMEM)] + [pl.BlockSpec(memory_space=pl.ANY)] * len(after),
        out_specs=pl.BlockSpec(memory_space=pltpu.VMEM),
        scratch_shapes=[pltpu.SemaphoreType.DMA((7,)), pltpu.SemaphoreType.DMA((7,)), pltpu.SemaphoreType.DMA],
        compiler_params=pltpu.CompilerParams(vmem_limit_bytes=V7X_VMEM_LIMIT),
    )(block, *after)


_SEM = pl.BlockSpec(memory_space=pltpu.SEMAPHORE)
_DATAFLOW = pltpu.SideEffectType.DATAFLOW_SIDE_EFFECTING


def _plan_copies(plan, refs, send_sems, recv_sems):
    return [pltpu.make_async_remote_copy(src_ref=src, dst_ref=dst, send_sem=send_sems.at[i], recv_sem=recv_sems.at[i],
                                         device_id=to, device_id_type=MESH) for i, (src, dst, to) in enumerate(plan(refs))]


class _Token(NamedTuple):
    after: jax.Array
    tie: jax.Array


def _after_operand(after):
    return after.after if isinstance(after, _Token) else after


def _copies_start(bufs, plan, n_copies, after, name):
    nb = len(bufs)

    def body(*refs):
        for cp in _plan_copies(plan, refs[:nb], refs[nb + 1], refs[nb + 2]):
            cp.start()
        for token in refs[-2:]:
            token[...] = jnp.zeros_like(token)

    sem = pltpu.SemaphoreType.DMA((n_copies,))
    vmem = pl.BlockSpec(memory_space=pltpu.VMEM)
    outs = pl.pallas_call(
        body, name=name,
        out_shape=(sem, sem, *[pltpu.HBM(b.shape, b.dtype) for b in bufs], jax.ShapeDtypeStruct((8, 128), F32),
                   jax.ShapeDtypeStruct((1, 1), F32)),
        in_specs=[_HBM] * nb + [pl.BlockSpec(memory_space=pl.ANY)],
        out_specs=(_SEM, _SEM, *[_HBM] * nb, vmem, vmem),
        input_output_aliases={i: 2 + i for i in range(nb)},
        compiler_params=pltpu.CompilerParams(has_side_effects=_DATAFLOW),
    )(*[pltpu.with_memory_space_constraint(b, pltpu.HBM) for b in bufs], _after_operand(after))
    return outs[0], outs[1], list(outs[2:2 + nb]), _Token(outs[-2], outs[-1])


def _copies_wait(started, plan, after, name):
    send_sems, recv_sems, bufs, _ = started
    nb = len(bufs)

    def body(*refs):
        for cp in _plan_copies(plan, refs[:nb], refs[nb], refs[nb + 1]):
            cp.wait_send()
            cp.wait_recv()

    return list(pl.pallas_call(
        body, name=name, out_shape=tuple(pltpu.HBM(b.shape, b.dtype) for b in bufs),
        in_specs=[_HBM] * nb + [_SEM, _SEM, pl.BlockSpec(memory_space=pl.ANY)], out_specs=tuple([_HBM] * nb),
        input_output_aliases={i: i for i in range(nb)},
        compiler_params=pltpu.CompilerParams(has_side_effects=_DATAFLOW),
    )(*bufs, send_sems, recv_sems, _after_operand(after)))


def _half_rows(ref, axis, c):
    half = ref.shape[axis] // 2
    return pl.ds(c * half, half)


def _plan_gather_ici(refs):
    n = len(refs) // 2
    x, y, c, chips = _place()
    out = []
    for a in range(n):
        rows = _half_rows(refs[a], 0, c)
        out += [(refs[a].at[rows], refs[n + a].at[2 * x + y, rows], (*chip, c)) for chip in chips]
        out.append((refs[a], refs[n + a].at[2 * x + y], (x, y, 1 - c)))
    return out


def _plan_gather_d2d(refs):
    x, y, c, chips = _place()
    out = []
    for ref in refs:
        rows = _half_rows(ref, 1, c)
        for px, py in chips:
            landed = ref.at[2 * px + py, rows]
            out.append((landed, landed, (x, y, 1 - c)))
    return out


def _plan_rs_sibling(refs):
    n = len(refs) // 2
    x, y, c, _ = _place()
    return [(refs[a].at[pl.ds(0, refs[a].shape[0]), _half_rows(refs[a], 1, 1 - c)], refs[n + a], (x, y, 1 - c)) for a in range(n)]


def _plan_rs_chips(refs):
    n = len(refs) // 2
    x, y, c, chips = _place()
    return [(refs[a].at[2 * px + py], refs[n + a].at[k], (px, py, c)) for a in range(n) for k, (px, py) in enumerate(chips)]


def _plan_rs_share(refs):
    x, y, c, _ = _place()
    return [(ref.at[_half_rows(ref, 0, c)], ref.at[_half_rows(ref, 0, c)], (x, y, 1 - c)) for ref in refs]


def _chip_sum(g, other, sel, name, blocked=True):
    nblk, half, cdim = other.shape
    tr = _pick(half, (512, 256, 128, 64))
    per = half // tr

    def body(sel_ref, g_ref, t_ref, wire_ref, own_ref):
        total = g_ref[0] + t_ref[0]
        wire_ref[0] = total.astype(BF16)
        if blocked:
            @pl.when(pl.program_id(1) == sel_ref[1])
            def _():
                own_ref[...] = total
        else:
            own_ref[0] = total

    blk = pl.BlockSpec((1, tr, cdim), lambda i, p, sel_ref: (p, i, 0))
    own_spec = pl.BlockSpec((tr, cdim), lambda i, p, sel_ref: (i, 0)) if blocked else blk
    own_shape = jax.ShapeDtypeStruct((half, cdim) if blocked else other.shape, F32)
    return pl.pallas_call(
        body, name=name,
        grid_spec=pltpu.PrefetchScalarGridSpec(
            num_scalar_prefetch=1, grid=(per, nblk),
            in_specs=[pl.BlockSpec((1, tr, cdim), lambda i, p, sel_ref: (p, sel_ref[0] * per + i, 0)), blk],
            out_specs=[blk, own_spec]),
        out_shape=[jax.ShapeDtypeStruct(other.shape, BF16), own_shape],
        compiler_params=_params("parallel", "arbitrary"),
    )(sel, g, other)


def _final_sum(own, recv, sel, name):
    half, cdim = own.shape
    tr = _pick(half, (512, 256, 128, 64))
    per = half // tr

    def body(sel_ref, own_ref, r0_ref, r1_ref, r2_ref, o_ref):
        o_ref[...] = ((own_ref[...] + r0_ref[0].astype(F32)) + r1_ref[0].astype(F32)) + r2_ref[0].astype(F32)

    part = lambda k: pl.BlockSpec((1, tr, cdim), lambda i, sel_ref, k=k: (k, i, 0))
    return pl.pallas_call(
        body, name=name,
        grid_spec=pltpu.PrefetchScalarGridSpec(
            num_scalar_prefetch=1, grid=(per,),
            in_specs=[pl.BlockSpec((tr, cdim), lambda i, sel_ref: (i, 0)), part(0), part(1), part(2)],
            out_specs=pl.BlockSpec((tr, cdim), lambda i, sel_ref: (sel_ref[0] * per + i, 0))),
        out_shape=jax.ShapeDtypeStruct((2 * half, cdim), F32), compiler_params=_params("parallel"),
    )(sel, own, recv, recv, recv)


def _row(v):
    return v.reshape(1, -1)


_BR_A, _BR_B, _BR_C = (0, A_WIDTH), (A_WIDTH, POOL_WIDTH), (A_WIDTH + POOL_WIDTH, CONV_WIDTH)


def _tie(v, token):
    return v if token is None else v + token.tie


def _no_hook(point, after, ready=None):
    return None


def _layer_fwd(x, w, mod, hook=_no_hook):
    s = x.shape[0]
    mod3 = mod.reshape(6, 1, D)
    h = _modnorm_fwd(x, _row(w["g_mix_pre"]), (mod3, 0), (mod3, 1), "mix_pre_fwd")
    hook("pre", h)
    z = _mm(h, w["w_all"], name="mm_in")
    qkv = z[:, Z_QKV:Z_PC].astype(BF16)
    fl = z[:, Z_FL:Z_COLS]
    cum = _cumf_fwd(fl, w["b_f_pad"])
    fr = cum[:, :HEADS].T.reshape(HEADS, s // ATT_K, ATT_K)
    br_a, lse = _attn_fwd(qkv, fr)
    br_b, br_c = _poolconv_fwd(z, w["w_pool_bd"], _tie(_row(w["pool_scale"]), hook("attn", lse)), w["conv_w"])
    hook("pool", br_b)
    wbr = w["w_branch"]
    pa = _mm(br_a, wbr, b_rows=_BR_A, name="mm_br_a")
    pb = _mm(br_b, wbr, b_rows=_BR_B, name="mm_br_b")
    pc = _mm(br_c, wbr, b_rows=_BR_C, name="mm_br_c")
    merged = _merge_fwd(z, pa, pb, pc)
    y = _mm(merged, w["w_out"], name="mm_out")
    x1, h2 = _post_pre_fwd(x, y, _row(w["g_mix_post"]), (mod3, 2), _row(w["g_ff_pre"]), (mod3, 3), (mod3, 4), "mix_post_ff_pre_fwd")
    a, r = _mm(h2, w["w_ff1"], b_split=N_CHIPS, epilogue=_relu2_fwd, out_dtype=(F32, BF16), name="mm_ff1")
    y2 = _mm(r, w["w_ff2"], name="mm_ff2")
    x2 = _post_fwd(x1, y2, _tie(_row(w["g_ff_post"]), hook("ff_post", y2)), (mod3, 5), "ff_post_fwd")
    hook("end", x2)
    saved = dict(x=x, h=h, z=z, qkv=qkv, fl=fl, fr=fr, lse=lse, br_a=br_a, br_b=br_b, br_c=br_c, pa=pa, pb=pb, pc=pc,
                 merged=merged, y=y, x1=x1, h2=h2, a=a, r=r, y2=y2)
    return x2, saved


def _layer_bwd(dx2, sv, w, mod, hook=_no_hook):
    s = dx2.shape[0]
    mod3 = mod.reshape(6, 1, D)
    dy2, sum_ff_post = _post_bwd(dx2, sv["y2"], _row(w["g_ff_post"]), (mod3, 5), "ff_post_bwd")
    (da,) = _mm(dy2, w["w_ff2"], tb=True, epilogue=_relu2_bwd, extras=(sv["a"],), out_dtype=(BF16,), name="mm_ff2_dx")
    d_w_ff2 = _mm(sv["r"], dy2, ta=True, name="mm_ff2_dw")
    dh2 = _mm(da, w["w_ff1"], tb=True, b_split=N_CHIPS, name="mm_ff1_dx")
    d_w_ff1 = _mm(sv["h2"], da, ta=True, out_split=N_CHIPS, name="mm_ff1_dw")
    g_ff_pre = _tie(_row(w["g_ff_pre"]), hook("ff_pre", dh2, dict(w_ff1=d_w_ff1, w_ff2=d_w_ff2)))
    dx1, dy, sum_mid = _pre_post_bwd(dh2, sv["x1"], dx2, g_ff_pre, (mod3, 4), sv["y"], _row(w["g_mix_post"]), (mod3, 2), "ff_pre_mix_post_bwd")
    sum_ff_pre, sum_mix_post = sum_mid, sum_mid[3:]
    dmerged = _mm(dy, w["w_out"], tb=True, name="mm_out_dx")
    d_w_out = _mm(sv["merged"], dy, ta=True, name="mm_out_dw")
    dz, dpa, dpb, dpc = _merge_bwd(dmerged, sv["z"], sv["pa"], sv["pb"], sv["pc"])
    wbr = w["w_branch"]
    dbr_a = _mm(dpa, wbr, tb=True, b_rows=_BR_A, name="mm_br_a_dx")
    dbr_b = _mm(dpb, wbr, tb=True, b_rows=_BR_B, name="mm_br_b_dx")
    dbr_c = _mm(dpc, wbr, tb=True, b_rows=_BR_C, name="mm_br_c_dx")
    d_w_branch = jnp.concatenate([_mm(sv["br_a"], dpa, ta=True, name="mm_br_a_dw"), _mm(sv["br_b"], dpb, ta=True, name="mm_br_b_dw"),
                                  _mm(sv["br_c"], dpc, ta=True, name="mm_br_c_dw")], axis=0)

    dq, dk, dv, dfc, dfr = _attn_bwd(sv["qkv"], dbr_a, sv["br_a"], sv["lse"], sv["fr"])
    dcum = dfc + jnp.pad(dfr.reshape(HEADS, s).T, ((0, 0), (0, 128 - HEADS)))
    dfl, sum_bf = _cumf_bwd(dcum, sv["fl"], _tie(w["b_f_pad"], hook("cumf", dfc)))
    dpc_z, d_wbd, sum_ps, sum_cw = _poolconv_bwd(dbr_b, dbr_c, sv["z"], w["w_pool_bd"], _row(w["pool_scale"]), w["conv_w"])
    for at, part in ((Z_QKV, dq), (Z_QKV + A_WIDTH, dk), (Z_QKV + 2 * A_WIDTH, dv), (Z_PC, dpc_z), (Z_FL, dfl)):
        dz = lax.dynamic_update_slice(dz, part, (0, at))
    dh = _mm(dz, w["w_all"], tb=True, name="mm_in_dx")
    d_w_all = _mm(sv["h"], dz, ta=True, name="mm_in_dw")
    hook("mix_pre", dh)
    dx, sum_mix_pre = _modnorm_bwd(dh, sv["x"], dx1, _row(w["g_mix_pre"]), (mod3, 1), "mix_pre_bwd")

    dmod = jnp.stack([sum_mix_pre[0], sum_mix_pre[1], sum_mix_post[0], sum_ff_pre[0], sum_ff_pre[1], sum_ff_post[0]])
    d_w_in = d_w_all[None]
    d_w_pool = jnp.stack([d_wbd[64 * g:64 * g + 64, 64 * g:64 * g + 64] for g in range(4)])
    big = dict(w_in=d_w_in, w_branch=d_w_branch, w_out=d_w_out, w_ff1=d_w_ff1, w_ff2=d_w_ff2)
    small = dict(g_mix_pre=sum_mix_pre[2], g_mix_post=sum_mix_post[1], g_ff_pre=sum_ff_pre[2], g_ff_post=sum_ff_post[1],
                 b_f=sum_bf[0, :HEADS], w_pool=d_w_pool, pool_scale=sum_ps[0], conv_w=sum_cw[0:3])
    return dx, dmod, big, small


_QKV_END, _FL_END, _PC_END = 3 * A_WIDTH, 3 * A_WIDTH + HEADS, 3 * A_WIDTH + HEADS + POOL_WIDTH + 3 * CONV_WIDTH
_W_IN_GROUPS = ((_PC_END, IN_COLS, Z_GL), (0, _QKV_END, Z_QKV), (_FL_END, _PC_END, Z_PC), (_QKV_END, _FL_END, Z_FL))
_SHARD_COLS = IN_COLS // N_CHIPS


def _w_all_from_shards(blocks):
    pieces = []
    for lo, hi, _ in _W_IN_GROUPS:
        for p in range(N_CHIPS):
            a, b = max(lo, p * _SHARD_COLS), min(hi, (p + 1) * _SHARD_COLS)
            if a < b:
                pieces.append(blocks[p][:, a - p * _SHARD_COLS:b - p * _SHARD_COLS])
    pieces.append(jnp.zeros((D, Z_COLS - IN_COLS), blocks.dtype))
    return jnp.concatenate(pieces, axis=1)


def _w_in_shard(d_w_all, p):
    pieces = []
    for lo, hi, at in sorted(_W_IN_GROUPS):
        a, b = max(lo, p * _SHARD_COLS), min(hi, (p + 1) * _SHARD_COLS)
        if a < b:
            pieces.append(d_w_all[:, at + a - lo:at + b - lo])
    return jnp.concatenate(pieces, axis=1)


def _w_in_shards(d_w_all):
    return jnp.stack([_w_in_shard(d_w_all, p) for p in range(N_CHIPS)])


def _full_layer_weights(w_in_blocks, w_branch, w_out, w_ff1, w_ff2, g_mix_pre, g_mix_post, g_ff_pre, g_ff_post, b_f, w_pool, pool_scale, conv_w):
    w_all = None if w_in_blocks is None else _w_all_from_shards(w_in_blocks)
    wbd = (w_pool[:, :, None, :] * jnp.eye(4, dtype=F32)[:, None, :, None]).reshape(POOL_WIDTH, POOL_WIDTH)
    return dict(w_all=w_all, w_branch=w_branch, w_out=w_out, w_ff1=w_ff1, w_ff2=w_ff2, g_mix_pre=g_mix_pre, g_mix_post=g_mix_post,
                g_ff_pre=g_ff_pre, g_ff_post=g_ff_post, b_f_pad=jnp.pad(b_f, (0, 128 - HEADS)).reshape(1, 128), w_pool_bd=wbd,
                pool_scale=pool_scale, conv_w=conv_w)


class _NoComm:
    def layer_weights(self, l):
        raise NotImplementedError

    def fwd_hook(self, l):
        return _no_hook

    def bwd_hook(self, l):
        return _no_hook

    def grads_ready(self, l, big):
        return None


class _Layers(_NoComm):
    def __init__(self, layers):
        self.layers = layers

    def layer_weights(self, l):
        return self.layers[l]


def _local_step(x, target, mods, comm):
    saved, weights = [], []
    act = x
    for l in range(DEPTH):
        weights.append(comm.layer_weights(l))
        act, sv = _layer_fwd(act, weights[l], mods[l], comm.fwd_hook(l))
        saved.append(sv)
    dact, sq = _loss_head(act, target)
    loss = sq[0, 0] * (0.5 / D)
    dmods, bigs, smalls = [None] * DEPTH, [None] * DEPTH, [None] * DEPTH
    token = None
    for l in reversed(range(DEPTH)):
        dact, dmods[l], bigs[l], smalls[l] = _layer_bwd(dact, saved[l], weights[l], _tie(mods[l], token), comm.bwd_hook(l))
        token = comm.grads_ready(l, bigs[l])
    return loss, dact, jnp.stack(dmods), bigs, smalls


_BIG = ("w_in", "w_branch", "w_out", "w_ff1", "w_ff2")


class _GatherJob:
    def __init__(self, tag, shards, after):
        self.tag, self.n = tag, len(shards)
        lands = [lax.empty((N_CHIPS,) + s.shape, s.dtype) for s in shards]
        self.state = _copies_start(list(shards) + lands, _plan_gather_ici, 4 * self.n, after, "gather_ici_start_" + tag)
        self.token = self.state[3]

    def pass_on(self, after):
        bufs = _copies_wait(self.state, _plan_gather_ici, after, "gather_ici_wait_" + self.tag)
        self.state = _copies_start(bufs[self.n:], _plan_gather_d2d, 3 * self.n, bufs[0], "gather_d2d_start_" + self.tag)
        self.token = self.state[3]
        return self.token

    def done(self, after):
        return _copies_wait(self.state, _plan_gather_d2d, after, "gather_d2d_wait_" + self.tag)


class _ReduceJob:
    def __init__(self, tag, names, grads, sel, after):
        self.tag, self.names, self.n, self.sel = tag, names, len(names), sel
        lands = [lax.empty((g.shape[0], g.shape[1] // 2, g.shape[2]), F32) for g in grads]
        self.state = _copies_start(list(grads) + lands, _plan_rs_sibling, self.n, after, "rs_sibling_start_" + tag)
        self.token = self.state[3]

    def _chip_sum(self, name, g, other):
        if g.shape[0] == N_CHIPS:
            return _chip_sum(g, other, self.sel, "rs_chip_sum_" + name)
        wire, total = _chip_sum(g, other, self.sel, "rs_chip_sum_" + name, blocked=False)
        own = lax.switch(self.sel[1], [lambda t, p=p: _w_in_shard(t, p) for p in range(N_CHIPS)], total[0])
        return _w_in_shards(wire[0]), own

    def chip_sums(self, after):
        bufs = _copies_wait(self.state, _plan_rs_sibling, after, "rs_sibling_wait_" + self.tag)
        wires, self.owns = zip(*[self._chip_sum(name, bufs[i], bufs[self.n + i]) for i, name in enumerate(self.names)])
        lands = [lax.empty((3,) + w.shape[1:], BF16) for w in wires]
        self.state = _copies_start(list(wires) + lands, _plan_rs_chips, 3 * self.n, self.owns[0], "rs_chips_start_" + self.tag)
        self.token = self.state[3]
        return self.token

    def final_sums(self, after):
        bufs = _copies_wait(self.state, _plan_rs_chips, after, "rs_chips_wait_" + self.tag)
        sums = [_final_sum(self.owns[i], bufs[self.n + i], self.sel, "rs_final_" + name) for i, name in enumerate(self.names)]
        self.state = _copies_start(sums, _plan_rs_share, self.n, sums[0], "rs_share_start_" + self.tag)
        self.token = self.state[3]
        return self.token

    def done(self, after):
        return dict(zip(self.names, _copies_wait(self.state, _plan_rs_share, after, "rs_share_wait_" + self.tag)))


def _chip_blocks(g):
    return g if g.ndim == 3 else g.reshape(N_CHIPS, -1, g.shape[1])


class _StepComm(_NoComm):
    def __init__(self, big_weights, w_in0, sel, after):
        self.sel = sel
        self.small, self.grads, self.jobs = None, [dict() for _ in range(DEPTH)], {}
        self.jobs["in0"] = _GatherJob("in0", [w_in0], after)
        later = lax.optimization_barrier((tuple(big_weights), self.jobs["in0"].token.after))[0]
        self.jobs["rest0"] = _GatherJob("rest0", [w[0].astype(BF16) for w in later[1:]], self.jobs["in0"].token)
        self.jobs["all1"] = _GatherJob("all1", [w[1].astype(BF16) for w in later], self.jobs["rest0"].token)

    def layer_weights(self, l):
        if l == 0:
            self.weights0 = _full_layer_weights(None, None, None, None, None, *self.small[0])
            return self.weights0
        g_in, g_br, g_out, g_f1, g_f2 = self.landed1
        return _full_layer_weights(g_in, g_br.reshape(D, D), g_out.reshape(D, D), g_f1, g_f2.reshape(D_FF, D), *self.small[1])

    def fwd_hook(self, l):
        if l != 0:
            return _no_hook

        def hook(point, after, ready=None):
            if point == "pre":
                job = self.jobs["in0"]
                started = after[:8, :128].astype(F32) + self.jobs["all1"].token.after
                self.weights0["w_all"] = _w_all_from_shards(job.done(job.pass_on(started))[0])
            if point == "attn":
                return self.jobs["rest0"].pass_on(after)
            if point == "ff_post":
                return self.jobs["all1"].pass_on(after)
            if point == "pool":
                g_br, g_out, g_f1, g_f2 = self.jobs["rest0"].done(after)
                self.weights0.update(w_branch=g_br.reshape(D, D), w_out=g_out.reshape(D, D), w_ff1=g_f1, w_ff2=g_f2.reshape(D_FF, D))
            if point == "end":
                self.landed1 = self.jobs["all1"].done(after)
            return None
        return hook

    def bwd_hook(self, l):
        if l != 0:
            return _no_hook

        def hook(point, after, ready=None):
            jobs = self.jobs
            if point == "ff_pre":
                token = jobs["rs1"].chip_sums(after)
                jobs["rs0_ff"] = _ReduceJob("0_ff", ("w_ff1", "w_ff2"), [_chip_blocks(ready[n]) for n in ("w_ff1", "w_ff2")], self.sel, token)
                return jobs["rs0_ff"].token
            if point == "cumf":
                return jobs["rs0_ff"].chip_sums(jobs["rs1"].final_sums(after))
            self.grads[1] = jobs["rs1"].done(after)
            return None
        return hook

    def grads_ready(self, l, big):
        if l == 1:
            self.jobs["rs1"] = _ReduceJob("1", _BIG, [_chip_blocks(big[n]) for n in _BIG], self.sel, self.sel)
            return self.jobs["rs1"].token
        names = ("w_in", "w_branch", "w_out")
        self.jobs["rs0_mix"] = _ReduceJob("0_mix", names, [_chip_blocks(big[n]) for n in names], self.sel, self.sel)
        return self.jobs["rs0_mix"].token

    def finish_sums(self, after):
        jobs = self.jobs
        token = jobs["rs0_mix"].chip_sums(after)
        return jobs["rs0_ff"].final_sums(token)

    def finish_ff(self, after):
        self.grads[0].update(self.jobs["rs0_ff"].done(after))

    def finish_mix(self, after):
        job = self.jobs["rs0_mix"]
        self.grads[0].update(job.done(job.final_sums(after)))


_SMALL = ("g_mix_pre", "g_mix_post", "g_ff_pre", "g_ff_post", "b_f", "w_pool", "pool_scale", "conv_w")


def _w_in_view(t):
    return t.reshape(DEPTH, D // 128, 128, _SHARD_COLS).transpose(3, 1, 0, 2).reshape(_SHARD_COLS * (D // 128) * DEPTH, 128)


def _w_in_unview(t):
    return t.reshape(_SHARD_COLS, D // 128, DEPTH, 128).transpose(2, 1, 3, 0).reshape(DEPTH, D, _SHARD_COLS)


def _pack(parts, rows=8):
    flat = jnp.concatenate([p.reshape(-1) for p in parts])
    width = -(-flat.shape[0] // (rows * 128)) * 128
    return jnp.pad(flat, (0, rows * width - flat.shape[0])).reshape(rows, width)


def _unpack(packed, like):
    flat = packed.reshape(-1)
    out, at = [], 0
    for ref in like:
        out.append(flat[at:at + ref.size].reshape(ref.shape))
        at += ref.size
    return out


def kernel(x, c, w_ada, b_ada, g_mix_pre, g_mix_post, g_ff_pre, g_ff_post, w_in, b_f, w_pool, pool_scale, conv_w, w_branch, w_out, w_ff1, w_ff2, loss_target, m_w_ada, m_b_ada, m_g_mix_pre, m_g_mix_post, m_g_ff_pre, m_g_ff_post, m_w_in, m_b_f, m_w_pool, m_pool_scale, m_conv_w, m_w_branch, m_w_out, m_w_ff1, m_w_ff2, v_w_ada, v_b_ada, v_g_mix_pre, v_g_mix_post, v_g_ff_pre, v_g_ff_post, v_w_in, v_b_f, v_w_pool, v_pool_scale, v_conv_w, v_w_branch, v_w_out, v_w_ff1, v_w_ff2):
    xi, yi, ci = lax.axis_index("x"), lax.axis_index("y"), lax.axis_index("c")
    chip = 2 * xi + yi
    dev = 2 * chip + ci
    n_ada = w_ada.shape[2]

    first = jnp.zeros((8, D + 384), F32).at[0, :D].set(c[0]).at[0, D:].set(conv_w.reshape(-1))
    w_in0 = w_in[0].astype(BF16)
    got = _allgather8(first, "gather_cond", after=(w_in0,)).reshape(N_DEV, 8, D + 384)[:, 0]
    c_all = got[:, :D]
    conv_full = got[0::2, D:].reshape(N_CHIPS, DEPTH, 3, CONV_WIDTH // N_CHIPS).transpose(1, 2, 0, 3).reshape(DEPTH, 3, CONV_WIDTH)

    b_loc = lax.dynamic_slice_in_dim(b_ada, chip * n_ada, n_ada, axis=1).reshape(DEPTH, 1, n_ada)
    mod_cols, silu_c = _ada_fwd(c_all, w_ada, b_loc)
    got = _allgather8(mod_cols.reshape(DEPTH * N_DEV, n_ada), "gather_mod").reshape(N_DEV, DEPTH, N_DEV, n_ada)[0::2]
    mod_all = got.transpose(1, 2, 0, 3).reshape(DEPTH, N_DEV, 6, D)
    mods = lax.dynamic_index_in_dim(mod_all, dev, axis=1, keepdims=False)

    comm = _StepComm((w_in, w_branch, w_out, w_ff1, w_ff2), w_in0, jnp.stack([ci, chip]).astype(jnp.int32), mods)
    comm.small = [(g_mix_pre[l], g_mix_post[l], g_ff_pre[l], g_ff_post[l], b_f[l], w_pool[l], pool_scale[l], conv_full[l]) for l in range(DEPTH)]
    loss_part, grad_x, dmods, bigs, smalls = _local_step(x[0], loss_target[0], mods, comm)

    small_parts = [smalls[l][name] for name in _SMALL for l in range(DEPTH)] + [loss_part.reshape(1)]
    packed = _tie(_pack([dmods] + small_parts), comm.jobs["rs0_mix"].token)
    gathered = _allgather8(packed, "gather_small")
    dmod_all = gathered.reshape(N_DEV, -1)[:, :dmods.size].reshape(N_DEV, DEPTH, 6 * D)
    summed = _unpack(_sum_devices(gathered), [dmods] + small_parts)
    grad_b_ada = summed[0].reshape(DEPTH, 6 * D)
    loss = summed[-1][0]
    small_grads = {name: jnp.stack(summed[1 + 2 * i:3 + 2 * i]) for i, name in enumerate(_SMALL)}
    small_grads["conv_w"] = lax.dynamic_slice_in_dim(small_grads["conv_w"], chip * (CONV_WIDTH // N_CHIPS), CONV_WIDTH // N_CHIPS, axis=2)

    dmod_loc = lax.dynamic_slice_in_dim(dmod_all.transpose(1, 0, 2), chip * n_ada, n_ada, axis=2)
    tail_token = comm.finish_sums(grad_b_ada)
    silu_pad = _tie(jnp.pad(silu_c, ((0, 128 - N_DEV), (0, 0))), tail_token)
    dmod_pad = jnp.pad(dmod_loc.transpose(1, 0, 2).reshape(N_DEV, DEPTH * n_ada), ((0, 128 - N_DEV), (0, 0)))
    grad_w_ada = _mm(silu_pad, dmod_pad, ta=True, out_split=DEPTH, name="mm_ada_dw")

    grads = dict(w_ada=grad_w_ada, b_ada=grad_b_ada, **small_grads)
    weights = dict(w_ada=w_ada, b_ada=b_ada, g_mix_pre=g_mix_pre, g_mix_post=g_mix_post, g_ff_pre=g_ff_pre, g_ff_post=g_ff_post, w_in=w_in,
                   b_f=b_f, w_pool=w_pool, pool_scale=pool_scale, conv_w=conv_w, w_branch=w_branch, w_out=w_out, w_ff1=w_ff1, w_ff2=w_ff2)
    m_in = dict(w_ada=m_w_ada, b_ada=m_b_ada, g_mix_pre=m_g_mix_pre, g_mix_post=m_g_mix_post, g_ff_pre=m_g_ff_pre, g_ff_post=m_g_ff_post,
                w_in=m_w_in, b_f=m_b_f, w_pool=m_w_pool, pool_scale=m_pool_scale, conv_w=m_conv_w, w_branch=m_w_branch, w_out=m_w_out,
                w_ff1=m_w_ff1, w_ff2=m_w_ff2)
    v_in = dict(w_ada=v_w_ada, b_ada=v_b_ada, g_mix_pre=v_g_mix_pre, g_mix_post=v_g_mix_post, g_ff_pre=v_g_ff_pre, g_ff_post=v_g_ff_post,
                w_in=v_w_in, b_f=v_b_f, w_pool=v_w_pool, pool_scale=v_pool_scale, conv_w=v_conv_w, w_branch=v_w_branch, w_out=v_w_out,
                w_ff1=v_w_ff1, w_ff2=v_w_ff2)
    order = ("w_ada", "b_ada", "g_mix_pre", "g_mix_post", "g_ff_pre", "g_ff_post", "w_in", "b_f", "w_pool", "pool_scale", "conv_w",
             "w_branch", "w_out", "w_ff1", "w_ff2")
    delta, new_m, new_v = {}, {}, {}
    tiny = ("b_ada",) + _SMALL
    tiny_g = [_tie(grads[tiny[0]], tail_token)] + [grads[name] for name in tiny[1:]]
    res = _adamw_many([weights[name] for name in tiny], tiny_g, [m_in[name] for name in tiny], [v_in[name] for name in tiny], "adamw_small")
    for out, vals in zip((delta, new_m, new_v), res):
        out.update(zip(tiny, vals))
    delta["w_ada"], new_m["w_ada"], new_v["w_ada"] = _adamw(w_ada, grad_w_ada, m_w_ada, v_w_ada, "adamw_w_ada")
    comm.finish_ff(delta["w_ada"][0, :8, :128] + delta["b_ada"][0, :128])
    for name in ("w_ff1", "w_ff2", "w_in", "w_branch", "w_out"):
        if name == "w_in":
            comm.finish_mix(delta["w_ff2"][0, :8, :128])
        g_layers = [comm.grads[l][name] for l in range(DEPTH)]
        if name == "w_in":
            g_view = lax.optimization_barrier(_w_in_view(jnp.stack(g_layers)))
            res = _adamw(_w_in_view(w_in), g_view, _w_in_view(m_w_in), _w_in_view(v_w_in), "adamw_w_in")
            grads[name], delta[name], new_m[name], new_v[name] = [_w_in_unview(t) for t in (g_view, *res)]
        else:
            delta[name], new_m[name], new_v[name], grads[name] = _adamw_layers(weights[name], g_layers, m_in[name], v_in[name], "adamw_" + name)

    return (loss, grad_x[None], *[grads[n] for n in order], *[delta[n] for n in order], *[new_m[n] for n in order],
            *[new_v[n] for n in order])
```

```python
from typing import NamedTuple

import jax
import jax.numpy as jnp
from jax import lax
from jax.experimental import pallas as pl
from jax.experimental.pallas import tpu as pltpu

F32 = jnp.float32
BF16 = jnp.bfloat16
MESH = pl.DeviceIdType.MESH

D = 1024
DEPTH = 2
HEADS = 8
HEAD_DIM = 64
A_WIDTH = 512
POOL_WIDTH = 256
CONV_WIDTH = 256
D_FF = 4096
IN_COLS = 5640
Z_GL, Z_QKV, Z_PC, Z_FL, Z_COLS = 0, 3072, 4608, 5632, 5760
RMS_EPS = 1e-6
NEG_INF = -1e30
ROW_TILE = 512
EW_ROWS = 256
N_CHIPS = 4
N_DEV = 8
V7X_VMEM_LIMIT = 48 * 1024 * 1024

ADAM_LR = 0.001
ADAM_B1 = 0.9
ADAM_B2 = 0.999
ADAM_EPS = 1e-08
ADAM_WD = 0.01
ADAM_STEP = 10

_HBM = pl.BlockSpec(memory_space=pltpu.HBM)


def _params(*sem):
    return pltpu.CompilerParams(dimension_semantics=sem, vmem_limit_bytes=V7X_VMEM_LIMIT)


def _pick(dim, cands):
    for cand in cands:
        if dim % cand == 0:
            return cand
    return dim


MM_TILE_BUDGET = 39 * 1024 * 1024


def _mm_tiles(m, n, k, k_unit, tn, a_size, b_size, out_size):
    for tk in (k_unit, 2048, 1152, 1024, 640, 512, 256, 128):
        if k_unit % tk:
            continue
        for tm in (2048, 1024, 512, 256, 128):
            if m % tm or ((m // tm) * (n // tn) < 2 and tm > 512):
                continue
            need = 2 * (tm * tk * a_size + tk * tn * b_size + tm * tn * out_size) + (0 if tk == k else 4 * tm * tn)
            if need <= MM_TILE_BUDGET and (tk == k_unit or tm >= 512):
                return tm, tk
    return 128, 128


def _mm(a, b, *, ta=False, tb=False, b_rows=None, b_split=1, out_split=1, out_dtype=F32, epilogue=None, extras=(), name):
    (k, m) = a.shape if ta else a.shape[::-1]
    b_row0, b_rows = (0, b.shape[-2]) if b_rows is None else b_rows
    b_cols = b.shape[-1] * b_split
    (n, k2) = (b_rows, b_cols) if tb else (b_cols, b_rows)
    assert k == k2, (a.shape, b.shape, ta, tb)
    n_unit = n // (out_split * (1 if tb else b_split))
    k_unit = k // (b_split if tb else 1)
    tn = _pick(n_unit, (1024, 1152, 768, 640, 512, 256, 128))
    tm, tk = _mm_tiles(m, n, k, k_unit, tn, a.dtype.itemsize, b.dtype.itemsize,
                       sum(jnp.dtype(dt).itemsize for dt in out_dtype) + 4 * len(extras) if epilogue else jnp.dtype(out_dtype).itemsize)
    nk = k // tk
    dims = (((0 if ta else 1,), (1 if tb else 0,)), ((), ()))

    def dot(a_ref, b_ref):
        b_val = b_ref[0] if b_split > 1 else b_ref[...]
        return lax.dot_general(a_ref[...].astype(BF16), b_val.astype(BF16), dims, preferred_element_type=F32)

    n_extra = len(extras)
    assert epilogue is None or out_split == 1

    def put(refs, val):
        if epilogue is not None:
            for o_ref, res in zip(refs[n_extra:], epilogue(val, *[r[...] for r in refs[:n_extra]])):
                o_ref[...] = res.astype(o_ref.dtype)
        elif out_split > 1:
            refs[0][0] = val.astype(refs[0].dtype)
        else:
            refs[0][...] = val.astype(refs[0].dtype)

    def body_single(a_ref, b_ref, *refs):
        put(refs, dot(a_ref, b_ref))

    def body_acc(a_ref, b_ref, *refs):
        kk = pl.program_id(2)
        acc_ref = refs[-1]

        @pl.when(kk == 0)
        def _():
            acc_ref[...] = jnp.zeros_like(acc_ref)

        acc_ref[...] += dot(a_ref, b_ref)

        @pl.when(kk == nk - 1)
        def _():
            put(refs[:-1], acc_ref[...])

    a_spec = pl.BlockSpec((tk, tm), lambda i, j, kk: (kk, i)) if ta else pl.BlockSpec((tm, tk), lambda i, j, kk: (i, kk))
    if b_split == 1:
        off = b_row0 // (tn if tb else tk)
        assert off * (tn if tb else tk) == b_row0
        b_spec = pl.BlockSpec((tn, tk), lambda i, j, kk: (j + off, kk)) if tb else pl.BlockSpec((tk, tn), lambda i, j, kk: (kk + off, j))
    elif tb:
        per = k_unit // tk
        b_spec = pl.BlockSpec((1, tn, tk), lambda i, j, kk: (kk // per, j, kk % per))
    else:
        per = n // b_split // tn
        b_spec = pl.BlockSpec((1, tk, tn), lambda i, j, kk: (j // per, kk, j % per))
    if out_split == 1:
        o_spec = pl.BlockSpec((tm, tn), lambda i, j, kk: (i, j))
        o_shape = None if epilogue is not None else jax.ShapeDtypeStruct((m, n), out_dtype)
    else:
        per_o = n // out_split // tn
        o_spec = pl.BlockSpec((1, tm, tn), lambda i, j, kk: (j // per_o, i, j % per_o))
        o_shape = jax.ShapeDtypeStruct((out_split, m, n // out_split), out_dtype)
    if epilogue is not None:
        o_shape = [jax.ShapeDtypeStruct((m, n), dt) for dt in out_dtype]
        o_spec = [o_spec] * len(out_dtype)
    return pl.pallas_call(
        body_single if nk == 1 else body_acc, name=name, grid=(m // tm, n // tn, nk),
        in_specs=[a_spec, b_spec] + [pl.BlockSpec((tm, tn), lambda i, j, kk: (i, j))] * n_extra, out_specs=o_spec, out_shape=o_shape,
        scratch_shapes=[] if nk == 1 else [pltpu.VMEM((tm, tn), F32)],
        compiler_params=_params("parallel", "parallel", "arbitrary"),
    )(a, b, *extras)


def _ew(fn, ins, out_dtypes, name, tc=None):
    shape = ins[0].shape
    lead, (rows, cols) = shape[:-2], shape[-2:]
    tc = cols if tc is None else tc
    if tc > 1024:
        tr = _pick(rows, (EW_ROWS, 128, 8))
    elif tc > 128:
        tr = _pick(rows, (2 * EW_ROWS, EW_ROWS, 128, 8))
    else:
        tr = _pick(rows, (4096, 2256, 2048, 1024, EW_ROWS, 8))
    n_in = len(ins)

    def body(*refs):
        res = fn(*[r[...] for r in refs[:n_in]])
        for o_ref, val in zip(refs[n_in:], res):
            o_ref[...] = val.astype(o_ref.dtype)

    if lead:
        spec = pl.BlockSpec((None, tr, tc), lambda l, i, j: (l, i, j))
    else:
        spec = pl.BlockSpec((tr, tc), lambda i, j: (i, j))
    return pl.pallas_call(
        body, name=name, grid=lead + (rows // tr, cols // tc),
        in_specs=[spec] * n_in, out_specs=[spec] * len(out_dtypes),
        out_shape=[jax.ShapeDtypeStruct(shape, dt) for dt in out_dtypes],
        compiler_params=_params(*(["parallel"] * (len(lead) + 2))),
    )(*ins)


def _relu2_fwd(a):
    r = jnp.maximum(a, 0.0)
    return a, r * r


def _relu2_bwd(dr, a):
    return (dr * (2.0 * jnp.maximum(a, 0.0)),)


def _adamw_math(w, g, m, v):
    m = ADAM_B1 * m + (1.0 - ADAM_B1) * g
    v = ADAM_B2 * v + (1.0 - ADAM_B2) * (g * g)
    m_hat = m / (1.0 - ADAM_B1 ** ADAM_STEP)
    v_hat = v / (1.0 - ADAM_B2 ** ADAM_STEP)
    delta = -ADAM_LR * (m_hat / (jnp.sqrt(v_hat) + ADAM_EPS) + ADAM_WD * w)
    return delta, m, v


def _adamw(w, g, m, v, name):
    return _ew(_adamw_math, [w, g, m, v], [F32, F32, F32], name)


def _adamw_layers(w, g_layers, m, v, name):
    depth, rows, cols = w.shape
    tr = _pick(rows, (2 * EW_ROWS, EW_ROWS, 128, 8)) if cols <= 1024 else _pick(rows, (EW_ROWS, 128, 8))

    def body(w_ref, *refs):
        g_refs, (m_ref, v_ref, d_ref, mo_ref, vo_ref, go_ref) = refs[:depth], refs[depth:]
        layer = pl.program_id(0)
        g = g_refs[0][...]
        for l in range(1, depth):
            g = jnp.where(layer == l, g_refs[l][...], g)
        d_ref[...], mo_ref[...], vo_ref[...] = _adamw_math(w_ref[...], g, m_ref[...], v_ref[...])
        go_ref[...] = g

    spec = pl.BlockSpec((None, tr, cols), lambda l, i: (l, i, 0))
    g_specs = [pl.BlockSpec((tr, cols), lambda l, i, k=k: (jnp.where(l == k, i, 0), 0)) for k in range(depth)]
    return pl.pallas_call(
        body, name=name, grid=(depth, rows // tr),
        in_specs=[spec] + g_specs + [spec, spec], out_specs=[spec] * 4,
        out_shape=[jax.ShapeDtypeStruct(w.shape, F32)] * 4, compiler_params=_params("arbitrary", "arbitrary"),
    )(w, *g_layers, m, v)


def _adamw_many(ws, gs, ms, vs, name):
    n = len(ws)

    def body(*refs):
        for i in range(n):
            res = _adamw_math(*[refs[k * n + i][...] for k in range(4)])
            for k in range(3):
                refs[(4 + k) * n + i][...] = res[k]

    outs = pl.pallas_call(
        body, name=name, out_shape=[jax.ShapeDtypeStruct(w.shape, F32) for w in ws] * 3,
        compiler_params=pltpu.CompilerParams(vmem_limit_bytes=V7X_VMEM_LIMIT),
    )(*ws, *gs, *ms, *vs)
    return outs[:n], outs[n:2 * n], outs[2 * n:]


def _row_spec(cols, block=0):
    return pl.BlockSpec((ROW_TILE, cols), lambda i, block=block: (i, block))


def _vec_spec(cols):
    return pl.BlockSpec((1, cols), lambda i: (0, 0))


def _vec_args(*vecs):
    arrays = [v[0] if isinstance(v, tuple) else v for v in vecs]
    specs = [pl.BlockSpec((None, 1, D), lambda i, row=v[1]: (row, 0, 0)) if isinstance(v, tuple) else _vec_spec(D) for v in vecs]
    return arrays, specs


def _sum_spec(cols):
    return pl.BlockSpec((8, cols), lambda i: (0, 0))


def _rstd(x):
    return lax.rsqrt(jnp.mean(x * x, axis=-1, keepdims=True) + RMS_EPS)


def _modnorm_fwd(x, g, shift, scale, name):
    s = x.shape[0]

    def body(x_ref, g_ref, sh_ref, sc_ref, h_ref):
        xv = x_ref[...]
        n = xv * _rstd(xv)
        h_ref[...] = ((n * g_ref[...]) * (1.0 + sc_ref[...]) + sh_ref[...]).astype(BF16)

    vecs, vec_specs = _vec_args(g, shift, scale)
    return pl.pallas_call(
        body, name=name, grid=(s // ROW_TILE,),
        in_specs=[_row_spec(D)] + vec_specs, out_specs=_row_spec(D),
        out_shape=jax.ShapeDtypeStruct((s, D), BF16), compiler_params=_params("parallel"),
    )(x, *vecs)


def _post_fwd(x, y, g, gate, name):
    s = x.shape[0]

    def body(x_ref, y_ref, g_ref, gate_ref, o_ref):
        yv = y_ref[...]
        o_ref[...] = x_ref[...] + gate_ref[...] * ((yv * _rstd(yv)) * g_ref[...])

    vecs, vec_specs = _vec_args(g, gate)
    return pl.pallas_call(
        body, name=name, grid=(s // ROW_TILE,),
        in_specs=[_row_spec(D), _row_spec(D)] + vec_specs, out_specs=_row_spec(D),
        out_shape=jax.ShapeDtypeStruct((s, D), F32), compiler_params=_params("parallel"),
    )(x, y, *vecs)


def _post_bwd(dxo, y, g, gate, name):
    s = dxo.shape[0]

    def body(d_ref, y_ref, g_ref, gate_ref, dy_ref, sum_ref):
        @pl.when(pl.program_id(0) == 0)
        def _():
            sum_ref[...] = jnp.zeros_like(sum_ref)

        dv, yv = d_ref[...], y_ref[...]
        r = _rstd(yv)
        n = yv * r
        sum_ref[0:1, :] += jnp.sum(dv * (n * g_ref[...]), axis=0, keepdims=True)
        sum_ref[1:2, :] += jnp.sum((dv * gate_ref[...]) * n, axis=0, keepdims=True)
        dn = (dv * gate_ref[...]) * g_ref[...]
        dy_ref[...] = (r * (dn - n * jnp.mean(dn * n, axis=-1, keepdims=True))).astype(BF16)

    vecs, vec_specs = _vec_args(g, gate)
    return pl.pallas_call(
        body, name=name, grid=(s // ROW_TILE,),
        in_specs=[_row_spec(D), _row_spec(D)] + vec_specs,
        out_specs=[_row_spec(D), _sum_spec(D)],
        out_shape=[jax.ShapeDtypeStruct((s, D), BF16), jax.ShapeDtypeStruct((8, D), F32)],
        compiler_params=_params("arbitrary"),
    )(dxo, y, *vecs)


def _modnorm_bwd(dh, x, dxo, g, scale, name):
    s = dh.shape[0]

    def body(dh_ref, x_ref, d_ref, g_ref, sc_ref, dx_ref, sum_ref):
        @pl.when(pl.program_id(0) == 0)
        def _():
            sum_ref[...] = jnp.zeros_like(sum_ref)

        dhv, xv = dh_ref[...], x_ref[...]
        r = _rstd(xv)
        n = xv * r
        one_sc = 1.0 + sc_ref[...]
        sum_ref[0:1, :] += jnp.sum(dhv, axis=0, keepdims=True)
        sum_ref[1:2, :] += jnp.sum(dhv * (n * g_ref[...]), axis=0, keepdims=True)
        sum_ref[2:3, :] += jnp.sum((dhv * one_sc) * n, axis=0, keepdims=True)
        dn = (dhv * one_sc) * g_ref[...]
        dx_ref[...] = d_ref[...] + r * (dn - n * jnp.mean(dn * n, axis=-1, keepdims=True))

    vecs, vec_specs = _vec_args(g, scale)
    return pl.pallas_call(
        body, name=name, grid=(s // ROW_TILE,),
        in_specs=[_row_spec(D), _row_spec(D), _row_spec(D)] + vec_specs,
        out_specs=[_row_spec(D), _sum_spec(D)],
        out_shape=[jax.ShapeDtypeStruct((s, D), F32), jax.ShapeDtypeStruct((8, D), F32)],
        compiler_params=_params("arbitrary"),
    )(dh, x, dxo, *vecs)


def _post_pre_fwd(x, y, g_post, gate, g_pre, shift, scale, name):
    s = x.shape[0]

    def body(x_ref, y_ref, gp_ref, gate_ref, g_ref, sh_ref, sc_ref, o_ref, h_ref):
        yv = y_ref[...]
        xo = x_ref[...] + gate_ref[...] * ((yv * _rstd(yv)) * gp_ref[...])
        o_ref[...] = xo
        h_ref[...] = (((xo * _rstd(xo)) * g_ref[...]) * (1.0 + sc_ref[...]) + sh_ref[...]).astype(BF16)

    vecs, vec_specs = _vec_args(g_post, gate, g_pre, shift, scale)
    return pl.pallas_call(
        body, name=name, grid=(s // ROW_TILE,),
        in_specs=[_row_spec(D), _row_spec(D)] + vec_specs, out_specs=[_row_spec(D), _row_spec(D)],
        out_shape=[jax.ShapeDtypeStruct((s, D), F32), jax.ShapeDtypeStruct((s, D), BF16)], compiler_params=_params("parallel"),
    )(x, y, *vecs)


def _pre_post_bwd(dh, x, dxo, g_pre, scale, y, g_post, gate, name):
    s = dh.shape[0]

    def body(dh_ref, x_ref, d_ref, y_ref, g_ref, sc_ref, gp_ref, gate_ref, dx_ref, dy_ref, sum_ref):
        @pl.when(pl.program_id(0) == 0)
        def _():
            sum_ref[...] = jnp.zeros_like(sum_ref)

        dhv, xv = dh_ref[...], x_ref[...]
        r = _rstd(xv)
        n = xv * r
        one_sc = 1.0 + sc_ref[...]
        sum_ref[0:1, :] += jnp.sum(dhv, axis=0, keepdims=True)
        sum_ref[1:2, :] += jnp.sum(dhv * (n * g_ref[...]), axis=0, keepdims=True)
        sum_ref[2:3, :] += jnp.sum((dhv * one_sc) * n, axis=0, keepdims=True)
        dn = (dhv * one_sc) * g_ref[...]
        dv = d_ref[...] + r * (dn - n * jnp.mean(dn * n, axis=-1, keepdims=True))
        dx_ref[...] = dv

        yv = y_ref[...]
        ry = _rstd(yv)
        ny = yv * ry
        sum_ref[3:4, :] += jnp.sum(dv * (ny * gp_ref[...]), axis=0, keepdims=True)
        sum_ref[4:5, :] += jnp.sum((dv * gate_ref[...]) * ny, axis=0, keepdims=True)
        dny = (dv * gate_ref[...]) * gp_ref[...]
        dy_ref[...] = (ry * (dny - ny * jnp.mean(dny * ny, axis=-1, keepdims=True))).astype(BF16)

    vecs, vec_specs = _vec_args(g_pre, scale, g_post, gate)
    return pl.pallas_call(
        body, name=name, grid=(s // ROW_TILE,),
        in_specs=[_row_spec(D)] * 4 + vec_specs,
        out_specs=[_row_spec(D), _row_spec(D), _sum_spec(D)],
        out_shape=[jax.ShapeDtypeStruct((s, D), F32), jax.ShapeDtypeStruct((s, D), BF16), jax.ShapeDtypeStruct((8, D), F32)],
        compiler_params=_params("arbitrary"),
    )(dh, x, dxo, y, *vecs)


def _loss_head(y, target):
    s = y.shape[0]

    def body(y_ref, t_ref, dy_ref, sum_ref):
        @pl.when(pl.program_id(0) == 0)
        def _():
            sum_ref[...] = jnp.zeros_like(sum_ref)

        err = y_ref[...] - t_ref[...]
        dy_ref[...] = err * (1.0 / D)
        sum_ref[...] += jnp.sum(err * err)

    return pl.pallas_call(
        body, name="loss_head", grid=(s // ROW_TILE,),
        in_specs=[_row_spec(D), _row_spec(D)],
        out_specs=[_row_spec(D), pl.BlockSpec((8, 128), lambda i: (0, 0))],
        out_shape=[jax.ShapeDtypeStruct((s, D), F32), jax.ShapeDtypeStruct((8, 128), F32)],
        compiler_params=_params("arbitrary"),
    )(y, target)


def _merge_fwd(z, pa, pb, pc):
    s = z.shape[0]

    def body(g0_ref, g1_ref, g2_ref, pa_ref, pb_ref, pc_ref, o_ref):
        o_ref[...] = (jax.nn.sigmoid(g0_ref[...]) * pa_ref[...] + jax.nn.sigmoid(g1_ref[...]) * pb_ref[...]
                      + jax.nn.sigmoid(g2_ref[...]) * pc_ref[...]).astype(BF16)

    return pl.pallas_call(
        body, name="merge_fwd", grid=(s // ROW_TILE,),
        in_specs=[_row_spec(D, 0), _row_spec(D, 1), _row_spec(D, 2), _row_spec(D), _row_spec(D), _row_spec(D)],
        out_specs=_row_spec(D), out_shape=jax.ShapeDtypeStruct((s, D), BF16),
        compiler_params=_params("parallel"),
    )(z, z, z, pa, pb, pc)


def _merge_bwd(dm, z, pa, pb, pc):
    s = z.shape[0]

    def body(dm_ref, g0_ref, g1_ref, g2_ref, pa_ref, pb_ref, pc_ref, dgl_ref, da_ref, db_ref, dc_ref):
        dmv = dm_ref[...]
        for i, (g_ref, p_ref, d_ref) in enumerate(((g0_ref, pa_ref, da_ref), (g1_ref, pb_ref, db_ref), (g2_ref, pc_ref, dc_ref))):
            gate = jax.nn.sigmoid(g_ref[...])
            dgl_ref[:, i * D:(i + 1) * D] = ((dmv * p_ref[...]) * (gate * (1.0 - gate))).astype(BF16)
            d_ref[...] = (dmv * gate).astype(BF16)

    return pl.pallas_call(
        body, name="merge_bwd", grid=(s // ROW_TILE,),
        in_specs=[_row_spec(D), _row_spec(D, 0), _row_spec(D, 1), _row_spec(D, 2), _row_spec(D), _row_spec(D), _row_spec(D)],
        out_specs=[_row_spec(3 * D), _row_spec(D), _row_spec(D), _row_spec(D)],
        out_shape=[jax.ShapeDtypeStruct((s, Z_COLS), BF16)] + [jax.ShapeDtypeStruct((s, D), BF16)] * 3,
        compiler_params=_params("parallel"),
    )(dm, z, z, z, pa, pb, pc)


def _shift_down(v, n):
    row = lax.broadcasted_iota(jnp.int32, v.shape, 0)
    return jnp.where(row >= n, pltpu.roll(v, n, axis=0), 0.0)


def _shift_up(v, n):
    s = v.shape[0]
    row = lax.broadcasted_iota(jnp.int32, v.shape, 0)
    return jnp.where(row < s - n, pltpu.roll(v, s - n, axis=0), 0.0)


def _log_sigmoid(v):
    return jnp.minimum(v, 0.0) - jnp.log1p(jnp.exp(-jnp.abs(v)))


def _cumf_fwd(fl, bias):
    s = fl.shape[0]

    def body(fl_ref, b_ref, o_ref):
        acc = _log_sigmoid(fl_ref[...] + b_ref[...])
        step = 1
        while step < s:
            acc = acc + _shift_down(acc, step)
            step *= 2
        o_ref[...] = acc

    return pl.pallas_call(body, name="cumf_fwd", out_shape=jax.ShapeDtypeStruct((s, 128), F32),
                          compiler_params=pltpu.CompilerParams(vmem_limit_bytes=V7X_VMEM_LIMIT))(fl, bias)


def _cumf_bwd(dcum, fl, bias):
    s = fl.shape[0]

    def body(d_ref, fl_ref, b_ref, dfl_ref, db_ref):
        acc = d_ref[...]
        step = 1
        while step < s:
            acc = acc + _shift_up(acc, step)
            step *= 2
        dfl = acc * jax.nn.sigmoid(-(fl_ref[...] + b_ref[...]))
        dfl_ref[...] = dfl.astype(BF16)
        db_ref[...] = jnp.broadcast_to(jnp.sum(dfl, axis=0, keepdims=True), (8, 128))

    return pl.pallas_call(
        body, name="cumf_bwd",
        out_shape=[jax.ShapeDtypeStruct((s, 128), BF16), jax.ShapeDtypeStruct((8, 128), F32)],
        compiler_params=pltpu.CompilerParams(vmem_limit_bytes=V7X_VMEM_LIMIT))(dcum, fl, bias)


def _pool_windows(v, shift):
    s2 = v + shift(v, 1)
    s4 = s2 + shift(s2, 2)
    s8 = s4 + shift(s4, 4)
    s16 = s8 + shift(s8, 8)
    group = lax.broadcasted_iota(jnp.int32, v.shape, 1) // 64
    return jnp.where(group == 0, s2, jnp.where(group == 1, s4, jnp.where(group == 2, s8, s16)))


def _pool_count(shape):
    group = lax.broadcasted_iota(jnp.int32, shape, 1) // 64
    window = jnp.where(group == 0, 2.0, jnp.where(group == 1, 4.0, jnp.where(group == 2, 8.0, 16.0)))
    t1 = (lax.broadcasted_iota(jnp.int32, shape, 0) + 1).astype(F32)
    return jnp.minimum(t1, window)


def _pc_specs(s):
    zcol = lambda blk: pl.BlockSpec((s, 256), lambda i, blk=blk: (0, blk))
    first = Z_PC // 256
    return [zcol(first), zcol(first + 1), zcol(first + 2), zcol(first + 3),
            pl.BlockSpec((256, 256), lambda i: (0, 0)), pl.BlockSpec((1, 256), lambda i: (0, 0)),
            pl.BlockSpec((3, 256), lambda i: (0, 0))]


def _poolconv_fwd(z, wbd, pscale, convw):
    s = z.shape[0]

    def body(pu_ref, ch_ref, cb_ref, cc_ref, w_ref, ps_ref, cw_ref, yb_ref, yc_ref):
        u = pu_ref[...]
        p = _pool_windows(u, _shift_down) / _pool_count(u.shape) - u
        yb = jnp.dot(p.astype(BF16), w_ref[...].astype(BF16), preferred_element_type=F32) * ps_ref[...]
        yb_ref[...] = yb.astype(BF16)
        uc = cc_ref[...] * ch_ref[...]
        cw = cw_ref[...]
        conv = cw[0:1, :] * _shift_down(uc, 2) + cw[1:2, :] * _shift_down(uc, 1) + cw[2:3, :] * uc
        yc_ref[...] = (cb_ref[...] * conv).astype(BF16)

    out = pl.BlockSpec((s, 256), lambda i: (0, 0))
    return pl.pallas_call(
        body, name="poolconv_fwd", grid=(1,), in_specs=_pc_specs(s), out_specs=[out, out],
        out_shape=[jax.ShapeDtypeStruct((s, 256), BF16)] * 2, compiler_params=_params("arbitrary"),
    )(z, z, z, z, wbd, pscale, convw)


def _poolconv_bwd(dyb, dyc, z, wbd, pscale, convw):
    s = z.shape[0]

    def body(dyb_ref, dyc_ref, pu_ref, ch_ref, cb_ref, cc_ref, w_ref, ps_ref, cw_ref, dz_ref, dw_ref, dps_ref, dcw_ref):
        u = pu_ref[...]
        count = _pool_count(u.shape)
        p = (_pool_windows(u, _shift_down) / count - u).astype(BF16)
        wb = w_ref[...].astype(BF16)
        dyb_v = dyb_ref[...]
        pw = jnp.dot(p, wb, preferred_element_type=F32)
        dps_ref[...] = jnp.broadcast_to(jnp.sum(dyb_v * pw, axis=0, keepdims=True), (8, 256))
        dys = (dyb_v * ps_ref[...]).astype(BF16)
        dp = lax.dot_general(dys, wb, (((1,), (1,)), ((), ())), preferred_element_type=F32)
        dw_ref[...] = lax.dot_general(p, dys, (((0,), (0,)), ((), ())), preferred_element_type=F32)
        dz_ref[:, 0:256] = (_pool_windows(dp / count, _shift_up) - dp).astype(BF16)

        ch, cb, cc = ch_ref[...], cb_ref[...], cc_ref[...]
        uc = cc * ch
        cw = cw_ref[...]
        u2, u1 = _shift_down(uc, 2), _shift_down(uc, 1)
        conv = cw[0:1, :] * u2 + cw[1:2, :] * u1 + cw[2:3, :] * uc
        dyc_v = dyc_ref[...]
        dconv = dyc_v * cb
        du = cw[0:1, :] * _shift_up(dconv, 2) + cw[1:2, :] * _shift_up(dconv, 1) + cw[2:3, :] * dconv
        dz_ref[:, 256:512] = (du * cc).astype(BF16)
        dz_ref[:, 512:768] = (dyc_v * conv).astype(BF16)
        dz_ref[:, 768:1024] = (du * ch).astype(BF16)
        dcw_ref[...] = jnp.zeros_like(dcw_ref)
        dcw_ref[0:1, :] = jnp.sum(dconv * u2, axis=0, keepdims=True)
        dcw_ref[1:2, :] = jnp.sum(dconv * u1, axis=0, keepdims=True)
        dcw_ref[2:3, :] = jnp.sum(dconv * uc, axis=0, keepdims=True)

    blk = lambda r, c: pl.BlockSpec((r, c), lambda i: (0, 0))
    return pl.pallas_call(
        body, name="poolconv_bwd", grid=(1,),
        in_specs=[blk(s, 256), blk(s, 256)] + _pc_specs(s),
        out_specs=[blk(s, 1024), blk(256, 256), blk(8, 256), blk(8, 256)],
        out_shape=[jax.ShapeDtypeStruct((s, 1024), BF16), jax.ShapeDtypeStruct((256, 256), F32),
                   jax.ShapeDtypeStruct((8, 256), F32), jax.ShapeDtypeStruct((8, 256), F32)],
        compiler_params=_params("arbitrary"),
    )(dyb, dyc, z, z, z, z, wbd, pscale, convw)


_NT = (((1,), (1,)), ((), ()))
_TN = (((0,), (0,)), ((), ()))


ATT_Q, ATT_K = 256, 256
ATT_HEADS_BWD = 8
ATT_HEADS = 8


def _att_logits(q, k, fr, q0, k0, masked):
    logits = lax.dot_general(q, k, _NT, preferred_element_type=F32) - fr
    if not masked:
        return logits
    row = q0 + lax.broadcasted_iota(jnp.int32, logits.shape, 0)
    col = k0 + lax.broadcasted_iota(jnp.int32, logits.shape, 1)
    return jnp.where(row >= col, logits, NEG_INF)


def _causal_sweep(step, qi, init):
    n_full = (qi * ATT_Q) // ATT_K
    carry = lax.fori_loop(0, n_full, lambda j, carry: step(j, carry, False), init)
    return step(n_full, carry, True)


HEAD_PAIRS = HEADS // 2


def _lane_pick(v, lane, idx):
    return jnp.sum(jnp.where(lane == idx, v, 0.0), axis=-1, keepdims=True)


def _lane_put(lane, idx, col):
    return jnp.where(lane == idx, col, 0.0)


def _split_heads(v, low):
    zero = jnp.zeros_like(v)
    return jnp.where(low, v, zero), jnp.where(low, zero, v)


def _attn_fwd(qkv, fr):
    s = qkv.shape[0]
    nk = s // ATT_K
    width = ATT_HEADS * HEAD_DIM
    groups = HEADS // ATT_HEADS

    def body(q_ref, k_ref, v_ref, fr_ref, o_ref, lse_ref):
        qi, grp = pl.program_id(0), pl.program_id(1)
        lane = lax.broadcasted_iota(jnp.int32, (ATT_Q, 128), 1)
        low = lane < HEAD_DIM
        qs = []
        for pr in range(ATT_HEADS // 2):
            qs += _split_heads(q_ref[:, 128 * pr:128 * (pr + 1)] * (HEAD_DIM ** -0.5), low)

        def step(j, carry, masked):
            k0 = pl.multiple_of(j * ATT_K, ATT_K)
            out = []
            for h in range(ATT_HEADS):
                cols = slice(128 * (h // 2), 128 * (h // 2 + 1))
                m, l, acc = carry[h]
                logits = _att_logits(qs[h], k_ref[pl.ds(k0, ATT_K), cols], fr_ref[h, pl.ds(j, 1), :], qi * ATT_Q, k0, masked)
                m_new = jnp.maximum(m, jnp.max(logits, axis=-1, keepdims=True))
                p = jnp.exp(logits - m_new)
                alpha = jnp.exp(m - m_new)
                l = alpha * l + jnp.sum(p, axis=-1, keepdims=True)
                acc = alpha * acc + jnp.dot(p.astype(BF16), v_ref[pl.ds(k0, ATT_K), cols], preferred_element_type=F32)
                out.append((m_new, l, acc))
            return tuple(out)

        one = (jnp.full((ATT_Q, 1), NEG_INF, F32), jnp.zeros((ATT_Q, 1), F32), jnp.zeros((ATT_Q, 128), F32))
        done = _causal_sweep(step, qi, (one,) * ATT_HEADS)

        @pl.when(grp == 0)
        def _():
            lse_ref[...] = jnp.zeros_like(lse_ref)

        lse = jnp.zeros((ATT_Q, 128), F32)
        for pr in range(ATT_HEADS // 2):
            (m0, l0, acc0), (m1, l1, acc1) = done[2 * pr], done[2 * pr + 1]
            o_ref[:, 128 * pr:128 * (pr + 1)] = jnp.where(low, acc0 / l0, acc1 / l1)
            head = ATT_HEADS * grp + 2 * pr
            lse = lse + _lane_put(lane, head, m0 + jnp.log(l0)) + _lane_put(lane, head + 1, m1 + jnp.log(l1))
        lse_ref[...] += lse

    return pl.pallas_call(
        body, name="attn_fwd", grid=(s // ATT_Q, groups),
        in_specs=[pl.BlockSpec((ATT_Q, width), lambda i, g: (i, g)),
                  pl.BlockSpec((s, width), lambda i, g: (0, groups + g)),
                  pl.BlockSpec((s, width), lambda i, g: (0, 2 * groups + g)),
                  pl.BlockSpec((ATT_HEADS, nk, ATT_K), lambda i, g: (g, 0, 0))],
        out_specs=[pl.BlockSpec((ATT_Q, width), lambda i, g: (i, g)), pl.BlockSpec((ATT_Q, 128), lambda i, g: (i, 0))],
        out_shape=[jax.ShapeDtypeStruct((s, A_WIDTH), F32), jax.ShapeDtypeStruct((s, 128), F32)],
        compiler_params=_params("parallel", "arbitrary"),
    )(qkv, qkv, qkv, fr)


def _attn_bwd(qkv, do, o, lse, fr):
    s = qkv.shape[0]
    nk = s // ATT_K
    scale = HEAD_DIM ** -0.5
    heads = ATT_HEADS_BWD
    width = heads * HEAD_DIM
    groups = HEADS // heads

    def body(q_ref, k_ref, v_ref, do_ref, o_ref, lse_ref, fr_ref, dq_ref, dk_ref, dv_ref, dfc_ref, dfr_ref, dk_acc, dv_acc):
        grp = pl.program_id(0)
        lane = lax.broadcasted_iota(jnp.int32, (ATT_Q, 128), 1)
        low = lane < HEAD_DIM
        low_t = lax.broadcasted_iota(jnp.int32, (128, ATT_Q), 0) < HEAD_DIM
        dk_acc[...] = jnp.zeros_like(dk_acc)
        dv_acc[...] = jnp.zeros_like(dv_acc)
        dfr_ref[...] = jnp.zeros_like(dfr_ref)

        @pl.when(grp == 0)
        def _():
            dfc_ref[...] = jnp.zeros_like(dfc_ref)

        def outer(i, carry):
            q0 = pl.multiple_of(i * ATT_Q, ATT_Q)
            rows = pl.ds(q0, ATT_Q)
            lsev = lse_ref[rows, :]
            qts, dots, qs, dos, deltas, lses = [], [], [], [], [], []
            for pr in range(heads // 2):
                pcols = slice(128 * pr, 128 * (pr + 1))
                q2, do2 = q_ref[rows, pcols] * scale, do_ref[rows, pcols]
                prod = do2 * o_ref[rows, pcols]
                deltas += [jnp.sum(jnp.where(low, prod, 0.0), axis=-1, keepdims=True),
                           jnp.sum(jnp.where(low, 0.0, prod), axis=-1, keepdims=True)]
                dob2 = do2.astype(BF16)
                qts += _split_heads(q2.astype(F32).T.astype(BF16), low_t)
                dots += _split_heads(do2.T.astype(BF16), low_t)
                qs += _split_heads(q2, low)
                dos += _split_heads(dob2, low)
                lses += [_lane_pick(lsev, lane, heads * grp + 2 * pr), _lane_pick(lsev, lane, heads * grp + 2 * pr + 1)]

            def inner(j, carry, masked):
                k0 = pl.multiple_of(j * ATT_K, ATT_K)
                krows = pl.ds(k0, ATT_K)
                out, dkt, dvt = [], [], []
                for h in range(heads):
                    pcols = slice(128 * (h // 2), 128 * (h // 2 + 1))
                    dq, dfc = carry[h]
                    k2 = k_ref[krows, pcols]
                    p = jnp.exp(_att_logits(qs[h], k2, fr_ref[h, pl.ds(j, 1), :], q0, k0, masked) - lses[h])
                    dp = lax.dot_general(dos[h], v_ref[krows, pcols], _NT, preferred_element_type=F32)
                    ds = p * (dp - deltas[h])
                    dsb = ds.astype(BF16)
                    dkt.append(jnp.dot(qts[h], dsb, preferred_element_type=F32))
                    dvt.append(jnp.dot(dots[h], p.astype(BF16), preferred_element_type=F32))
                    dfr_ref[h, pl.ds(j, 1), :] -= jnp.sum(ds, axis=0, keepdims=True)
                    out.append((dq + jnp.dot(dsb, k2, preferred_element_type=F32), dfc + (ds[:, :128] + ds[:, 128:])))
                for pr in range(heads // 2):
                    prows = slice(128 * pr, 128 * (pr + 1))
                    dk_acc[j, prows, :] += dkt[2 * pr] + dkt[2 * pr + 1]
                    dv_acc[j, prows, :] += dvt[2 * pr] + dvt[2 * pr + 1]
                return tuple(out)

            one = (jnp.zeros((ATT_Q, 128), F32), jnp.zeros((ATT_Q, 128), F32))
            done = _causal_sweep(inner, i, (one,) * heads)
            dfc = jnp.zeros((ATT_Q, 128), F32)
            for pr in range(heads // 2):
                (dq0, dfc0), (dq1, dfc1) = done[2 * pr], done[2 * pr + 1]
                dq_ref[rows, 128 * pr:128 * (pr + 1)] = (jnp.where(low, dq0, dq1) * scale).astype(BF16)
                head = heads * grp + 2 * pr
                dfc = (dfc + _lane_put(lane, head, jnp.sum(dfc0, axis=-1, keepdims=True))
                       + _lane_put(lane, head + 1, jnp.sum(dfc1, axis=-1, keepdims=True)))
            dfc_ref[rows, :] += dfc
            return carry

        lax.fori_loop(0, s // ATT_Q, outer, 0)
        for j in range(nk):
            for pr in range(heads // 2):
                prows, pcols = slice(128 * pr, 128 * (pr + 1)), slice(128 * pr, 128 * (pr + 1))
                dk_ref[ATT_K * j:ATT_K * (j + 1), pcols] = dk_acc[j, prows, :].T.astype(BF16)
                dv_ref[ATT_K * j:ATT_K * (j + 1), pcols] = dv_acc[j, prows, :].T.astype(BF16)

    part = lambda first: pl.BlockSpec((s, width), lambda g, first=first: (0, first + g))
    whole = pl.BlockSpec((s, 128), lambda g: (0, 0))
    rowv = pl.BlockSpec((heads, nk, ATT_K), lambda g: (g, 0, 0))
    return pl.pallas_call(
        body, name="attn_bwd", grid=(groups,),
        in_specs=[part(0), part(groups), part(2 * groups), part(0), part(0), whole, rowv],
        out_specs=[part(0), part(0), part(0), whole, rowv],
        out_shape=[jax.ShapeDtypeStruct((s, A_WIDTH), BF16)] * 3 + [jax.ShapeDtypeStruct((s, 128), F32), jax.ShapeDtypeStruct((HEADS, nk, ATT_K), F32)],
        scratch_shapes=[pltpu.VMEM((nk, width, ATT_K), F32), pltpu.VMEM((nk, width, ATT_K), F32)],
        compiler_params=_params("arbitrary"),
    )(qkv, qkv, qkv, do, o, lse, fr)


def _ada_fwd(c_all, w_ada, b_loc):
    depth, _, n = w_ada.shape
    tn = 512

    def body(c_ref, w_ref, b_ref, o_ref, sc_ref):
        cv = c_ref[...]
        sc = cv * jax.nn.sigmoid(cv)
        sc_ref[...] = sc
        o_ref[0] = jnp.dot(sc.astype(BF16), w_ref[0].astype(BF16), preferred_element_type=F32) + b_ref[0]

    return pl.pallas_call(
        body, name="ada_fwd", grid=(depth, n // tn),
        in_specs=[pl.BlockSpec((N_DEV, D), lambda l, j: (0, 0)), pl.BlockSpec((1, D, tn), lambda l, j: (l, 0, j)),
                  pl.BlockSpec((1, 1, tn), lambda l, j: (l, 0, j))],
        out_specs=[pl.BlockSpec((1, N_DEV, tn), lambda l, j: (l, 0, j)), pl.BlockSpec((N_DEV, D), lambda l, j: (0, 0))],
        out_shape=[jax.ShapeDtypeStruct((depth, N_DEV, n), F32), jax.ShapeDtypeStruct((N_DEV, D), F32)],
        compiler_params=_params("arbitrary", "arbitrary"),
    )(c_all, w_ada, b_loc)


def _sum_devices(gathered):
    n = gathered.shape[1]
    tn = _pick(n, (1408, 1024, 640, 512, 128))

    def body(g_ref, o_ref):
        acc = g_ref[0:8, :]
        for dev in range(1, N_DEV):
            acc = acc + g_ref[8 * dev:8 * dev + 8, :]
        o_ref[...] = acc

    return pl.pallas_call(
        body, name="sum_devices", grid=(n // tn,),
        in_specs=[pl.BlockSpec((8 * N_DEV, tn), lambda j: (0, j))], out_specs=pl.BlockSpec((8, tn), lambda j: (0, j)),
        out_shape=jax.ShapeDtypeStruct((8, n), F32), compiler_params=_params("parallel"),
    )(gathered)


def _place():
    x, y, c = lax.axis_index("x"), lax.axis_index("y"), lax.axis_index("c")
    chips = [(1 - x, y), (x, 1 - y), (1 - x, 1 - y)]
    return x, y, c, chips


def _allgather8(block, name, after=()):
    m_per, n = block.shape

    def body(x_ref, *rest):
        out_ref, send_sems, recv_sems, local_sem = rest[len(after):]
        x, y, c, chips = _place()
        me, sibling = (x, y, c), (x, y, 1 - c)

        def rows(px, py, pc):
            return out_ref.at[pl.ds((4 * px + 2 * py + pc) * m_per, m_per), :]

        def copy(k, blk, to, src=None):
            return pltpu.make_async_remote_copy(
                src_ref=rows(*blk) if src is None else src, dst_ref=rows(*blk),
                send_sem=send_sems.at[k], recv_sem=recv_sems.at[k], device_id=to, device_id_type=MESH)

        mine = pltpu.make_async_copy(x_ref, rows(*me), local_sem)
        mine.start()
        first = [copy(0, me, sibling, src=x_ref)]
        first += [copy(1 + j, me, (*chip, c), src=x_ref) for j, chip in enumerate(chips)]
        for cp in first:
            cp.start()
        passed = [copy(4 + j, (*chip, c), sibling) for j, chip in enumerate(chips)]
        for j, chip in enumerate(chips):
            copy(1 + j, (*chip, c), me).wait_recv()
            passed[j].start()
        copy(0, sibling, me).wait_recv()
        for j, chip in enumerate(chips):
            copy(4 + j, (*chip, 1 - c), me).wait_recv()
        for cp in first + passed:
            cp.wait_send()
        mine.wait()

    return pl.pallas_call(
        body, name=name, out_shape=jax.ShapeDtypeStruct((N_DEV * m_per, n), block.dtype),
        in_specs=[pl.BlockSpec(memory_space=pltpu.VMEM)] + [pl.BlockSpec(memory_space=pl.ANY)] * len(after),
        out_specs=pl.BlockSpec(memory_space=pltpu.VMEM),
        scratch_shapes=[pltpu.SemaphoreType.DMA((7,)), pltpu.SemaphoreType.DMA((7,)), pltpu.SemaphoreType.DMA],
        compiler_params=pltpu.CompilerParams(vmem_limit_bytes=V7X_VMEM_LIMIT),
    )(block, *after)


_SEM = pl.BlockSpec(memory_space=pltpu.SEMAPHORE)
_DATAFLOW = pltpu.SideEffectType.DATAFLOW_SIDE_EFFECTING


def _plan_copies(plan, refs, send_sems, recv_sems):
    return [pltpu.make_async_remote_copy(src_ref=src, dst_ref=dst, send_sem=send_sems.at[i], recv_sem=recv_sems.at[i],
                                         device_id=to, device_id_type=MESH) for i, (src, dst, to) in enumerate(plan(refs))]


class _Token(NamedTuple):
    after: jax.Array
    tie: jax.Array


def _after_operand(after):
    return after.after if isinstance(after, _Token) else after


def _copies_start(bufs, plan, n_copies, after, name):
    nb = len(bufs)

    def body(*refs):
        for cp in _plan_copies(plan, refs[:nb], refs[nb + 1], refs[nb + 2]):
            cp.start()
        for token in refs[-2:]:
            token[...] = jnp.zeros_like(token)

    sem = pltpu.SemaphoreType.DMA((n_copies,))
    vmem = pl.BlockSpec(memory_space=pltpu.VMEM)
    outs = pl.pallas_call(
        body, name=name,
        out_shape=(sem, sem, *[pltpu.HBM(b.shape, b.dtype) for b in bufs], jax.ShapeDtypeStruct((8, 128), F32),
                   jax.ShapeDtypeStruct((1, 1), F32)),
        in_specs=[_HBM] * nb + [pl.BlockSpec(memory_space=pl.ANY)],
        out_specs=(_SEM, _SEM, *[_HBM] * nb, vmem, vmem),
        input_output_aliases={i: 2 + i for i in range(nb)},
        compiler_params=pltpu.CompilerParams(has_side_effects=_DATAFLOW),
    )(*[pltpu.with_memory_space_constraint(b, pltpu.HBM) for b in bufs], _after_operand(after))
    return outs[0], outs[1], list(outs[2:2 + nb]), _Token(outs[-2], outs[-1])


def _copies_wait(started, plan, after, name):
    send_sems, recv_sems, bufs, _ = started
    nb = len(bufs)

    def body(*refs):
        for cp in _plan_copies(plan, refs[:nb], refs[nb], refs[nb + 1]):
            cp.wait_send()
            cp.wait_recv()

    return list(pl.pallas_call(
        body, name=name, out_shape=tuple(pltpu.HBM(b.shape, b.dtype) for b in bufs),
        in_specs=[_HBM] * nb + [_SEM, _SEM, pl.BlockSpec(memory_space=pl.ANY)], out_specs=tuple([_HBM] * nb),
        input_output_aliases={i: i for i in range(nb)},
        compiler_params=pltpu.CompilerParams(has_side_effects=_DATAFLOW),
    )(*bufs, send_sems, recv_sems, _after_operand(after)))


def _half_rows(ref, axis, c):
    half = ref.shape[axis] // 2
    return pl.ds(c * half, half)


def _plan_gather_ici(refs):
    n = len(refs) // 2
    x, y, c, chips = _place()
    out = []
    for a in range(n):
        rows = _half_rows(refs[a], 0, c)
        out += [(refs[a].at[rows], refs[n + a].at[2 * x + y, rows], (*chip, c)) for chip in chips]
        out.append((refs[a], refs[n + a].at[2 * x + y], (x, y, 1 - c)))
    return out


def _plan_gather_d2d(refs):
    x, y, c, chips = _place()
    out = []
    for ref in refs:
        rows = _half_rows(ref, 1, c)
        for px, py in chips:
            landed = ref.at[2 * px + py, rows]
            out.append((landed, landed, (x, y, 1 - c)))
    return out


def _plan_rs_sibling(refs):
    n = len(refs) // 2
    x, y, c, _ = _place()
    return [(refs[a].at[pl.ds(0, refs[a].shape[0]), _half_rows(refs[a], 1, 1 - c)], refs[n + a], (x, y, 1 - c)) for a in range(n)]


def _plan_rs_chips(refs):
    n = len(refs) // 2
    x, y, c, chips = _place()
    return [(refs[a].at[2 * px + py], refs[n + a].at[k], (px, py, c)) for a in range(n) for k, (px, py) in enumerate(chips)]


def _plan_rs_share(refs):
    x, y, c, _ = _place()
    return [(ref.at[_half_rows(ref, 0, c)], ref.at[_half_rows(ref, 0, c)], (x, y, 1 - c)) for ref in refs]


def _chip_sum(g, other, sel, name, blocked=True):
    nblk, half, cdim = other.shape
    tr = _pick(half, (512, 256, 128, 64))
    per = half // tr

    def body(sel_ref, g_ref, t_ref, wire_ref, own_ref):
        total = g_ref[0] + t_ref[0]
        wire_ref[0] = total.astype(BF16)
        if blocked:
            @pl.when(pl.program_id(1) == sel_ref[1])
            def _():
                own_ref[...] = total
        else:
            own_ref[0] = total

    blk = pl.BlockSpec((1, tr, cdim), lambda i, p, sel_ref: (p, i, 0))
    own_spec = pl.BlockSpec((tr, cdim), lambda i, p, sel_ref: (i, 0)) if blocked else blk
    own_shape = jax.ShapeDtypeStruct((half, cdim) if blocked else other.shape, F32)
    return pl.pallas_call(
        body, name=name,
        grid_spec=pltpu.PrefetchScalarGridSpec(
            num_scalar_prefetch=1, grid=(per, nblk),
            in_specs=[pl.BlockSpec((1, tr, cdim), lambda i, p, sel_ref: (p, sel_ref[0] * per + i, 0)), blk],
            out_specs=[blk, own_spec]),
        out_shape=[jax.ShapeDtypeStruct(other.shape, BF16), own_shape],
        compiler_params=_params("parallel", "arbitrary"),
    )(sel, g, other)


def _final_sum(own, recv, sel, name):
    half, cdim = own.shape
    tr = _pick(half, (512, 256, 128, 64))
    per = half // tr

    def body(sel_ref, own_ref, r0_ref, r1_ref, r2_ref, o_ref):
        o_ref[...] = ((own_ref[...] + r0_ref[0].astype(F32)) + r1_ref[0].astype(F32)) + r2_ref[0].astype(F32)

    part = lambda k: pl.BlockSpec((1, tr, cdim), lambda i, sel_ref, k=k: (k, i, 0))
    return pl.pallas_call(
        body, name=name,
        grid_spec=pltpu.PrefetchScalarGridSpec(
            num_scalar_prefetch=1, grid=(per,),
            in_specs=[pl.BlockSpec((tr, cdim), lambda i, sel_ref: (i, 0)), part(0), part(1), part(2)],
            out_specs=pl.BlockSpec((tr, cdim), lambda i, sel_ref: (sel_ref[0] * per + i, 0))),
        out_shape=jax.ShapeDtypeStruct((2 * half, cdim), F32), compiler_params=_params("parallel"),
    )(sel, own, recv, recv, recv)


def _row(v):
    return v.reshape(1, -1)


_BR_A, _BR_B, _BR_C = (0, A_WIDTH), (A_WIDTH, POOL_WIDTH), (A_WIDTH + POOL_WIDTH, CONV_WIDTH)


def _tie(v, token):
    return v if token is None else v + token.tie


def _no_hook(point, after, ready=None):
    return None


def _layer_fwd(x, w, mod, hook=_no_hook):
    s = x.shape[0]
    mod3 = mod.reshape(6, 1, D)
    h = _modnorm_fwd(x, _row(w["g_mix_pre"]), (mod3, 0), (mod3, 1), "mix_pre_fwd")
    hook("pre", h)
    z = _mm(h, w["w_all"], name="mm_in")
    qkv = z[:, Z_QKV:Z_PC].astype(BF16)
    fl = z[:, Z_FL:Z_COLS]
    cum = _cumf_fwd(fl, w["b_f_pad"])
    fr = cum[:, :HEADS].T.reshape(HEADS, s // ATT_K, ATT_K)
    br_a, lse = _attn_fwd(qkv, fr)
    br_b, br_c = _poolconv_fwd(z, w["w_pool_bd"], _tie(_row(w["pool_scale"]), hook("attn", lse)), w["conv_w"])
    hook("pool", br_b)
    wbr = w["w_branch"]
    pa = _mm(br_a, wbr, b_rows=_BR_A, name="mm_br_a")
    pb = _mm(br_b, wbr, b_rows=_BR_B, name="mm_br_b")
    pc = _mm(br_c, wbr, b_rows=_BR_C, name="mm_br_c")
    merged = _merge_fwd(z, pa, pb, pc)
    y = _mm(merged, w["w_out"], name="mm_out")
    x1, h2 = _post_pre_fwd(x, y, _row(w["g_mix_post"]), (mod3, 2), _row(w["g_ff_pre"]), (mod3, 3), (mod3, 4), "mix_post_ff_pre_fwd")
    a, r = _mm(h2, w["w_ff1"], b_split=N_CHIPS, epilogue=_relu2_fwd, out_dtype=(F32, BF16), name="mm_ff1")
    y2 = _mm(r, w["w_ff2"], name="mm_ff2")
    x2 = _post_fwd(x1, y2, _tie(_row(w["g_ff_post"]), hook("ff_post", y2)), (mod3, 5), "ff_post_fwd")
    hook("end", x2)
    saved = dict(x=x, h=h, z=z, qkv=qkv, fl=fl, fr=fr, lse=lse, br_a=br_a, br_b=br_b, br_c=br_c, pa=pa, pb=pb, pc=pc,
                 merged=merged, y=y, x1=x1, h2=h2, a=a, r=r, y2=y2)
    return x2, saved


def _layer_bwd(dx2, sv, w, mod, hook=_no_hook):
    s = dx2.shape[0]
    mod3 = mod.reshape(6, 1, D)
    dy2, sum_ff_post = _post_bwd(dx2, sv["y2"], _row(w["g_ff_post"]), (mod3, 5), "ff_post_bwd")
    (da,) = _mm(dy2, w["w_ff2"], tb=True, epilogue=_relu2_bwd, extras=(sv["a"],), out_dtype=(BF16,), name="mm_ff2_dx")
    d_w_ff2 = _mm(sv["r"], dy2, ta=True, name="mm_ff2_dw")
    dh2 = _mm(da, w["w_ff1"], tb=True, b_split=N_CHIPS, name="mm_ff1_dx")
    d_w_ff1 = _mm(sv["h2"], da, ta=True, out_split=N_CHIPS, name="mm_ff1_dw")
    g_ff_pre = _tie(_row(w["g_ff_pre"]), hook("ff_pre", dh2, dict(w_ff1=d_w_ff1, w_ff2=d_w_ff2)))
    dx1, dy, sum_mid = _pre_post_bwd(dh2, sv["x1"], dx2, g_ff_pre, (mod3, 4), sv["y"], _row(w["g_mix_post"]), (mod3, 2), "ff_pre_mix_post_bwd")
    sum_ff_pre, sum_mix_post = sum_mid, sum_mid[3:]
    dmerged = _mm(dy, w["w_out"], tb=True, name="mm_out_dx")
    d_w_out = _mm(sv["merged"], dy, ta=True, name="mm_out_dw")
    dz, dpa, dpb, dpc = _merge_bwd(dmerged, sv["z"], sv["pa"], sv["pb"], sv["pc"])
    wbr = w["w_branch"]
    dbr_a = _mm(dpa, wbr, tb=True, b_rows=_BR_A, name="mm_br_a_dx")
    dbr_b = _mm(dpb, wbr, tb=True, b_rows=_BR_B, name="mm_br_b_dx")
    dbr_c = _mm(dpc, wbr, tb=True, b_rows=_BR_C, name="mm_br_c_dx")
    d_w_branch = jnp.concatenate([_mm(sv["br_a"], dpa, ta=True, name="mm_br_a_dw"), _mm(sv["br_b"], dpb, ta=True, name="mm_br_b_dw"),
                                  _mm(sv["br_c"], dpc, ta=True, name="mm_br_c_dw")], axis=0)

    dq, dk, dv, dfc, dfr = _attn_bwd(sv["qkv"], dbr_a, sv["br_a"], sv["lse"], sv["fr"])
    dcum = dfc + jnp.pad(dfr.reshape(HEADS, s).T, ((0, 0), (0, 128 - HEADS)))
    dfl, sum_bf = _cumf_bwd(dcum, sv["fl"], _tie(w["b_f_pad"], hook("cumf", dfc)))
    dpc_z, d_wbd, sum_ps, sum_cw = _poolconv_bwd(dbr_b, dbr_c, sv["z"], w["w_pool_bd"], _row(w["pool_scale"]), w["conv_w"])
    for at, part in ((Z_QKV, dq), (Z_QKV + A_WIDTH, dk), (Z_QKV + 2 * A_WIDTH, dv), (Z_PC, dpc_z), (Z_FL, dfl)):
        dz = lax.dynamic_update_slice(dz, part, (0, at))
    dh = _mm(dz, w["w_all"], tb=True, name="mm_in_dx")
    d_w_all = _mm(sv["h"], dz, ta=True, name="mm_in_dw")
    hook("mix_pre", dh)
    dx, sum_mix_pre = _modnorm_bwd(dh, sv["x"], dx1, _row(w["g_mix_pre"]), (mod3, 1), "mix_pre_bwd")

    dmod = jnp.stack([sum_mix_pre[0], sum_mix_pre[1], sum_mix_post[0], sum_ff_pre[0], sum_ff_pre[1], sum_ff_post[0]])
    d_w_in = d_w_all[None]
    d_w_pool = jnp.stack([d_wbd[64 * g:64 * g + 64, 64 * g:64 * g + 64] for g in range(4)])
    big = dict(w_in=d_w_in, w_branch=d_w_branch, w_out=d_w_out, w_ff1=d_w_ff1, w_ff2=d_w_ff2)
    small = dict(g_mix_pre=sum_mix_pre[2], g_mix_post=sum_mix_post[1], g_ff_pre=sum_ff_pre[2], g_ff_post=sum_ff_post[1],
                 b_f=sum_bf[0, :HEADS], w_pool=d_w_pool, pool_scale=sum_ps[0], conv_w=sum_cw[0:3])
    return dx, dmod, big, small


_QKV_END, _FL_END, _PC_END = 3 * A_WIDTH, 3 * A_WIDTH + HEADS, 3 * A_WIDTH + HEADS + POOL_WIDTH + 3 * CONV_WIDTH
_W_IN_GROUPS = ((_PC_END, IN_COLS, Z_GL), (0, _QKV_END, Z_QKV), (_FL_END, _PC_END, Z_PC), (_QKV_END, _FL_END, Z_FL))
_SHARD_COLS = IN_COLS // N_CHIPS


def _w_in_layout():
    out = []
    for p in range(N_CHIPS):
        pieces = []
        for lo, hi, at in _W_IN_GROUPS:
            a, b = max(lo, p * _SHARD_COLS), min(hi, (p + 1) * _SHARD_COLS)
            if a < b:
                pieces.append((at + a - lo, at + b - lo, a - p * _SHARD_COLS))
        pieces.sort()
        segs = []
        for z0, z1, _ in pieces:
            s, e = z0 // 128 * 128, -(-z1 // 128) * 128
            if segs and s <= segs[-1][1]:
                segs[-1] = (segs[-1][0], max(e, segs[-1][1]))
            else:
                segs.append((s, e))
        assert sum(e - s for s, e in segs) == Z_WINDOW
        out.append((pieces, segs))
    return out


Z_WINDOW = 1536


def _w_in_window(shard, p):
    pieces, segs = _w_in_layout()[p]
    cols = []
    for s, e in segs:
        at = s
        for z0, z1, src in pieces:
            if s <= z0 < e:
                if z0 > at:
                    cols.append(jnp.zeros((shard.shape[0], z0 - at), shard.dtype))
                cols.append(shard[:, src:src + z1 - z0])
                at = z1
        if e > at:
            cols.append(jnp.zeros((shard.shape[0], e - at), shard.dtype))
    return jnp.concatenate(cols, axis=1)


def _own_window(shard, chip):
    return lax.switch(chip, [lambda t, p=p: _w_in_window(t, p) for p in range(N_CHIPS)], shard)


def _w_all_from_windows(blocks):
    layout = _w_in_layout()
    bounds = sorted({edge for _, segs in layout for seg in segs for edge in seg})
    parts = []
    for lo, hi in zip(bounds[:-1], bounds[1:]):
        covering = []
        for p, (_, segs) in enumerate(layout):
            at = 0
            for s, e in segs:
                if s <= lo and hi <= e:
                    covering.append(blocks[p][:, at + lo - s:at + hi - s])
                at += e - s
        assert covering
        parts.append(covering[0] if len(covering) == 1 else covering[0] + covering[1])
    return jnp.concatenate(parts, axis=1)


def _w_in_shard(d_w_all, p):
    pieces = []
    for lo, hi, at in sorted(_W_IN_GROUPS):
        a, b = max(lo, p * _SHARD_COLS), min(hi, (p + 1) * _SHARD_COLS)
        if a < b:
            pieces.append(d_w_all[:, at + a - lo:at + b - lo])
    return jnp.concatenate(pieces, axis=1)


def _w_in_shards(d_w_all):
    return jnp.stack([_w_in_shard(d_w_all, p) for p in range(N_CHIPS)])


def _full_layer_weights(w_in_blocks, w_branch, w_out, w_ff1, w_ff2, g_mix_pre, g_mix_post, g_ff_pre, g_ff_post, b_f, w_pool, pool_scale, conv_w):
    w_all = None if w_in_blocks is None else _w_all_from_windows(w_in_blocks)
    wbd = (w_pool[:, :, None, :] * jnp.eye(4, dtype=F32)[:, None, :, None]).reshape(POOL_WIDTH, POOL_WIDTH)
    return dict(w_all=w_all, w_branch=w_branch, w_out=w_out, w_ff1=w_ff1, w_ff2=w_ff2, g_mix_pre=g_mix_pre, g_mix_post=g_mix_post,
                g_ff_pre=g_ff_pre, g_ff_post=g_ff_post, b_f_pad=jnp.pad(b_f, (0, 128 - HEADS)).reshape(1, 128), w_pool_bd=wbd,
                pool_scale=pool_scale, conv_w=conv_w)


class _NoComm:
    def layer_weights(self, l):
        raise NotImplementedError

    def fwd_hook(self, l):
        return _no_hook

    def bwd_hook(self, l):
        return _no_hook

    def grads_ready(self, l, big):
        return None


class _Layers(_NoComm):
    def __init__(self, layers):
        self.layers = layers

    def layer_weights(self, l):
        return self.layers[l]


def _local_step(x, target, mods, comm):
    saved, weights = [], []
    act = x
    for l in range(DEPTH):
        weights.append(comm.layer_weights(l))
        act, sv = _layer_fwd(act, weights[l], mods[l], comm.fwd_hook(l))
        saved.append(sv)
    dact, sq = _loss_head(act, target)
    loss = sq[0, 0] * (0.5 / D)
    dmods, bigs, smalls = [None] * DEPTH, [None] * DEPTH, [None] * DEPTH
    token = None
    for l in reversed(range(DEPTH)):
        dact, dmods[l], bigs[l], smalls[l] = _layer_bwd(dact, saved[l], weights[l], _tie(mods[l], token), comm.bwd_hook(l))
        token = comm.grads_ready(l, bigs[l])
    return loss, dact, jnp.stack(dmods), bigs, smalls


_BIG = ("w_in", "w_branch", "w_out", "w_ff1", "w_ff2")


class _GatherJob:
    def __init__(self, tag, shards, after):
        self.tag, self.n = tag, len(shards)
        lands = [lax.empty((N_CHIPS,) + s.shape, s.dtype) for s in shards]
        self.state = _copies_start(list(shards) + lands, _plan_gather_ici, 4 * self.n, after, "gather_ici_start_" + tag)
        self.token = self.state[3]

    def pass_on(self, after):
        bufs = _copies_wait(self.state, _plan_gather_ici, after, "gather_ici_wait_" + self.tag)
        self.state = _copies_start(bufs[self.n:], _plan_gather_d2d, 3 * self.n, bufs[0], "gather_d2d_start_" + self.tag)
        self.token = self.state[3]
        return self.token

    def done(self, after):
        return _copies_wait(self.state, _plan_gather_d2d, after, "gather_d2d_wait_" + self.tag)


class _ReduceJob:
    def __init__(self, tag, names, grads, sel, after):
        self.tag, self.names, self.n, self.sel = tag, names, len(names), sel
        lands = [lax.empty((g.shape[0], g.shape[1] // 2, g.shape[2]), F32) for g in grads]
        self.state = _copies_start(list(grads) + lands, _plan_rs_sibling, self.n, after, "rs_sibling_start_" + tag)
        self.token = self.state[3]

    def _chip_sum(self, name, g, other):
        if g.shape[0] == N_CHIPS:
            return _chip_sum(g, other, self.sel, "rs_chip_sum_" + name)
        wire, total = _chip_sum(g, other, self.sel, "rs_chip_sum_" + name, blocked=False)
        own = lax.switch(self.sel[1], [lambda t, p=p: _w_in_shard(t, p) for p in range(N_CHIPS)], total[0])
        return _w_in_shards(wire[0]), own

    def chip_sums(self, after):
        bufs = _copies_wait(self.state, _plan_rs_sibling, after, "rs_sibling_wait_" + self.tag)
        wires, self.owns = zip(*[self._chip_sum(name, bufs[i], bufs[self.n + i]) for i, name in enumerate(self.names)])
        lands = [lax.empty((3,) + w.shape[1:], BF16) for w in wires]
        self.state = _copies_start(list(wires) + lands, _plan_rs_chips, 3 * self.n, self.owns[0], "rs_chips_start_" + self.tag)
        self.token = self.state[3]
        return self.token

    def final_sums(self, after):
        bufs = _copies_wait(self.state, _plan_rs_chips, after, "rs_chips_wait_" + self.tag)
        sums = [_final_sum(self.owns[i], bufs[self.n + i], self.sel, "rs_final_" + name) for i, name in enumerate(self.names)]
        self.state = _copies_start(sums, _plan_rs_share, self.n, sums[0], "rs_share_start_" + self.tag)
        self.token = self.state[3]
        return self.token

    def done(self, after):
        return dict(zip(self.names, _copies_wait(self.state, _plan_rs_share, after, "rs_share_wait_" + self.tag)))


def _chip_blocks(g):
    return g if g.ndim == 3 else g.reshape(N_CHIPS, -1, g.shape[1])


class _StepComm(_NoComm):
    def __init__(self, big_weights, w_in0, sel, after):
        self.sel = sel
        self.small, self.grads, self.jobs = None, [dict() for _ in range(DEPTH)], {}
        self.jobs["in0"] = _GatherJob("in0", [w_in0], after)
        later = lax.optimization_barrier((tuple(big_weights), self.jobs["in0"].token.after))[0]
        self.jobs["rest0"] = _GatherJob("rest0", [w[0].astype(BF16) for w in later[1:]], self.jobs["in0"].token)
        layer1 = [w[1].astype(BF16) for w in later]
        self.jobs["all1"] = _GatherJob("all1", [_own_window(layer1[0], sel[1])] + layer1[1:], self.jobs["rest0"].token)

    def layer_weights(self, l):
        if l == 0:
            self.weights0 = _full_layer_weights(None, None, None, None, None, *self.small[0])
            return self.weights0
        g_in, g_br, g_out, g_f1, g_f2 = self.landed1
        return _full_layer_weights(g_in, g_br.reshape(D, D), g_out.reshape(D, D), g_f1, g_f2.reshape(D_FF, D), *self.small[1])

    def fwd_hook(self, l):
        if l != 0:
            return _no_hook

        def hook(point, after, ready=None):
            if point == "pre":
                job = self.jobs["in0"]
                started = after[:8, :128].astype(F32) + self.jobs["all1"].token.after
                self.weights0["w_all"] = _w_all_from_windows(job.done(job.pass_on(started))[0])
            if point == "attn":
                return self.jobs["rest0"].pass_on(after)
            if point == "ff_post":
                return self.jobs["all1"].pass_on(after)
            if point == "pool":
                g_br, g_out, g_f1, g_f2 = self.jobs["rest0"].done(after)
                self.weights0.update(w_branch=g_br.reshape(D, D), w_out=g_out.reshape(D, D), w_ff1=g_f1, w_ff2=g_f2.reshape(D_FF, D))
            if point == "end":
                self.landed1 = self.jobs["all1"].done(after)
            return None
        return hook

    def bwd_hook(self, l):
        if l != 0:
            return _no_hook

        def hook(point, after, ready=None):
            jobs = self.jobs
            if point == "ff_pre":
                token = jobs["rs1"].chip_sums(after)
                jobs["rs0_ff"] = _ReduceJob("0_ff", ("w_ff1", "w_ff2"), [_chip_blocks(ready[n]) for n in ("w_ff1", "w_ff2")], self.sel, token)
                return jobs["rs0_ff"].token
            if point == "cumf":
                return jobs["rs0_ff"].chip_sums(jobs["rs1"].final_sums(after))
            self.grads[1] = jobs["rs1"].done(after)
            return None
        return hook

    def grads_ready(self, l, big):
        if l == 1:
            self.jobs["rs1"] = _ReduceJob("1", _BIG, [_chip_blocks(big[n]) for n in _BIG], self.sel, self.sel)
            return self.jobs["rs1"].token
        names = ("w_in", "w_branch", "w_out")
        self.jobs["rs0_mix"] = _ReduceJob("0_mix", names, [_chip_blocks(big[n]) for n in names], self.sel, self.sel)
        return self.jobs["rs0_mix"].token

    def finish_sums(self, after):
        jobs = self.jobs
        token = jobs["rs0_mix"].chip_sums(after)
        return jobs["rs0_ff"].final_sums(token)

    def finish_ff(self, after):
        self.grads[0].update(self.jobs["rs0_ff"].done(after))

    def finish_mix(self, after):
        job = self.jobs["rs0_mix"]
        self.grads[0].update(job.done(job.final_sums(after)))


_SMALL = ("g_mix_pre", "g_mix_post", "g_ff_pre", "g_ff_post", "b_f", "w_pool", "pool_scale", "conv_w")


def _w_in_view(t):
    return t.reshape(DEPTH, D // 128, 128, _SHARD_COLS).transpose(3, 1, 0, 2).reshape(_SHARD_COLS * (D // 128) * DEPTH, 128)


def _w_in_unview(t):
    return t.reshape(_SHARD_COLS, D // 128, DEPTH, 128).transpose(2, 1, 3, 0).reshape(DEPTH, D, _SHARD_COLS)


def _pack(parts, rows=8):
    flat = jnp.concatenate([p.reshape(-1) for p in parts])
    width = -(-flat.shape[0] // (rows * 128)) * 128
    return jnp.pad(flat, (0, rows * width - flat.shape[0])).reshape(rows, width)


def _unpack(packed, like):
    flat = packed.reshape(-1)
    out, at = [], 0
    for ref in like:
        out.append(flat[at:at + ref.size].reshape(ref.shape))
        at += ref.size
    return out


def kernel(x, c, w_ada, b_ada, g_mix_pre, g_mix_post, g_ff_pre, g_ff_post, w_in, b_f, w_pool, pool_scale, conv_w, w_branch, w_out, w_ff1, w_ff2, loss_target, m_w_ada, m_b_ada, m_g_mix_pre, m_g_mix_post, m_g_ff_pre, m_g_ff_post, m_w_in, m_b_f, m_w_pool, m_pool_scale, m_conv_w, m_w_branch, m_w_out, m_w_ff1, m_w_ff2, v_w_ada, v_b_ada, v_g_mix_pre, v_g_mix_post, v_g_ff_pre, v_g_ff_post, v_w_in, v_b_f, v_w_pool, v_pool_scale, v_conv_w, v_w_branch, v_w_out, v_w_ff1, v_w_ff2):
    xi, yi, ci = lax.axis_index("x"), lax.axis_index("y"), lax.axis_index("c")
    chip = 2 * xi + yi
    dev = 2 * chip + ci
    n_ada = w_ada.shape[2]

    first = jnp.zeros((8, D + 384), F32).at[0, :D].set(c[0]).at[0, D:].set(conv_w.reshape(-1))
    w_in0 = _own_window(w_in[0].astype(BF16), chip)
    got = _allgather8(first, "gather_cond", after=(w_in0,)).reshape(N_DEV, 8, D + 384)[:, 0]
    c_all = got[:, :D]
    conv_full = got[0::2, D:].reshape(N_CHIPS, DEPTH, 3, CONV_WIDTH // N_CHIPS).transpose(1, 2, 0, 3).reshape(DEPTH, 3, CONV_WIDTH)

    b_loc = lax.dynamic_slice_in_dim(b_ada, chip * n_ada, n_ada, axis=1).reshape(DEPTH, 1, n_ada)
    mod_cols, silu_c = _ada_fwd(c_all, w_ada, b_loc)
    got = _allgather8(mod_cols.reshape(DEPTH * N_DEV, n_ada), "gather_mod").reshape(N_DEV, DEPTH, N_DEV, n_ada)[0::2]
    mod_all = got.transpose(1, 2, 0, 3).reshape(DEPTH, N_DEV, 6, D)
    mods = lax.dynamic_index_in_dim(mod_all, dev, axis=1, keepdims=False)

    comm = _StepComm((w_in, w_branch, w_out, w_ff1, w_ff2), w_in0, jnp.stack([ci, chip]).astype(jnp.int32), mods)
    comm.small = [(g_mix_pre[l], g_mix_post[l], g_ff_pre[l], g_ff_post[l], b_f[l], w_pool[l], pool_scale[l], conv_full[l]) for l in range(DEPTH)]
    loss_part, grad_x, dmods, bigs, smalls = _local_step(x[0], loss_target[0], mods, comm)

    small_parts = [smalls[l][name] for name in _SMALL for l in range(DEPTH)] + [loss_part.reshape(1)]
    packed = _tie(_pack([dmods] + small_parts), comm.jobs["rs0_mix"].token)
    gathered = _allgather8(packed, "gather_small")
    dmod_all = gathered.reshape(N_DEV, -1)[:, :dmods.size].reshape(N_DEV, DEPTH, 6 * D)
    summed = _unpack(_sum_devices(gathered), [dmods] + small_parts)
    grad_b_ada = summed[0].reshape(DEPTH, 6 * D)
    loss = summed[-1][0]
    small_grads = {name: jnp.stack(summed[1 + 2 * i:3 + 2 * i]) for i, name in enumerate(_SMALL)}
    small_grads["conv_w"] = lax.dynamic_slice_in_dim(small_grads["conv_w"], chip * (CONV_WIDTH // N_CHIPS), CONV_WIDTH // N_CHIPS, axis=2)

    dmod_loc = lax.dynamic_slice_in_dim(dmod_all.transpose(1, 0, 2), chip * n_ada, n_ada, axis=2)
    tail_token = comm.finish_sums(grad_b_ada)
    silu_pad = _tie(jnp.pad(silu_c, ((0, 128 - N_DEV), (0, 0))), tail_token)
    dmod_pad = jnp.pad(dmod_loc.transpose(1, 0, 2).reshape(N_DEV, DEPTH * n_ada), ((0, 128 - N_DEV), (0, 0)))
    grad_w_ada = _mm(silu_pad, dmod_pad, ta=True, out_split=DEPTH, name="mm_ada_dw")

    grads = dict(w_ada=grad_w_ada, b_ada=grad_b_ada, **small_grads)
    weights = dict(w_ada=w_ada, b_ada=b_ada, g_mix_pre=g_mix_pre, g_mix_post=g_mix_post, g_ff_pre=g_ff_pre, g_ff_post=g_ff_post, w_in=w_in,
                   b_f=b_f, w_pool=w_pool, pool_scale=pool_scale, conv_w=conv_w, w_branch=w_branch, w_out=w_out, w_ff1=w_ff1, w_ff2=w_ff2)
    m_in = dict(w_ada=m_w_ada, b_ada=m_b_ada, g_mix_pre=m_g_mix_pre, g_mix_post=m_g_mix_post, g_ff_pre=m_g_ff_pre, g_ff_post=m_g_ff_post,
                w_in=m_w_in, b_f=m_b_f, w_pool=m_w_pool, pool_scale=m_pool_scale, conv_w=m_conv_w, w_branch=m_w_branch, w_out=m_w_out,
                w_ff1=m_w_ff1, w_ff2=m_w_ff2)
    v_in = dict(w_ada=v_w_ada, b_ada=v_b_ada, g_mix_pre=v_g_mix_pre, g_mix_post=v_g_mix_post, g_ff_pre=v_g_ff_pre, g_ff_post=v_g_ff_post,
                w_in=v_w_in, b_f=v_b_f, w_pool=v_w_pool, pool_scale=v_pool_scale, conv_w=v_conv_w, w_branch=v_w_branch, w_out=v_w_out,
                w_ff1=v_w_ff1, w_ff2=v_w_ff2)
    order = ("w_ada", "b_ada", "g_mix_pre", "g_mix_post", "g_ff_pre", "g_ff_post", "w_in", "b_f", "w_pool", "pool_scale", "conv_w",
             "w_branch", "w_out", "w_ff1", "w_ff2")
    delta, new_m, new_v = {}, {}, {}
    tiny = ("b_ada",) + _SMALL
    tiny_g = [_tie(grads[tiny[0]], tail_token)] + [grads[name] for name in tiny[1:]]
    res = _adamw_many([weights[name] for name in tiny], tiny_g, [m_in[name] for name in tiny], [v_in[name] for name in tiny], "adamw_small")
    for out, vals in zip((delta, new_m, new_v), res):
        out.update(zip(tiny, vals))
    delta["w_ada"], new_m["w_ada"], new_v["w_ada"] = _adamw(w_ada, grad_w_ada, m_w_ada, v_w_ada, "adamw_w_ada")
    comm.finish_ff(delta["w_ada"][0, :8, :128] + delta["b_ada"][0, :128])
    for name in ("w_ff1", "w_ff2", "w_in", "w_branch", "w_out"):
        if name == "w_in":
            comm.finish_mix(delta["w_ff2"][0, :8, :128])
        g_layers = [comm.grads[l][name] for l in range(DEPTH)]
        if name == "w_in":
            g_view = lax.optimization_barrier(_w_in_view(jnp.stack(g_layers)))
            res = _adamw(_w_in_view(w_in), g_view, _w_in_view(m_w_in), _w_in_view(v_w_in), "adamw_w_in")
            grads[name], delta[name], new_m[name], new_v[name] = [_w_in_unview(t) for t in (g_view, *res)]
        else:
            delta[name], new_m[name], new_v[name], grads[name] = _adamw_layers(weights[name], g_layers, m_in[name], v_in[name], "adamw_" + name)

    return (loss, grad_x[None], *[grads[n] for n in order], *[delta[n] for n in order], *[new_m[n] for n in order],
            *[new_v[n] for n in order])
```

```python
from typing import NamedTuple

import jax
import jax.numpy as jnp
from jax import lax
from jax.experimental import pallas as pl
from jax.experimental.pallas import tpu as pltpu

F32 = jnp.float32
BF16 = jnp.bfloat16
MESH = pl.DeviceIdType.MESH

D = 1024
DEPTH = 2
HEADS = 8
HEAD_DIM = 64
A_WIDTH = 512
POOL_WIDTH = 256
CONV_WIDTH = 256
D_FF = 4096
IN_COLS = 5640
Z_GL, Z_QKV, Z_PC, Z_FL, Z_COLS = 0, 3072, 4608, 5632, 5760
RMS_EPS = 1e-6
NEG_INF = -1e30
ROW_TILE = 512
EW_ROWS = 256
N_CHIPS = 4
N_DEV = 8
V7X_VMEM_LIMIT = 48 * 1024 * 1024

ADAM_LR = 0.001
ADAM_B1 = 0.9
ADAM_B2 = 0.999
ADAM_EPS = 1e-08
ADAM_WD = 0.01
ADAM_STEP = 10

_HBM = pl.BlockSpec(memory_space=pltpu.HBM)


def _params(*sem):
    return pltpu.CompilerParams(dimension_semantics=sem, vmem_limit_bytes=V7X_VMEM_LIMIT)


def _pick(dim, cands):
    for cand in cands:
        if dim % cand == 0:
            return cand
    return dim


MM_TILE_BUDGET = 39 * 1024 * 1024


def _mm_tiles(m, n, k, k_unit, tn, a_size, b_size, out_size):
    for tk in (k_unit, 2048, 1152, 1024, 640, 512, 256, 128):
        if k_unit % tk:
            continue
        for tm in (2048, 1024, 512, 256, 128):
            if m % tm or ((m // tm) * (n // tn) < 2 and tm > 512):
                continue
            need = 2 * (tm * tk * a_size + tk * tn * b_size + tm * tn * out_size) + (0 if tk == k else 4 * tm * tn)
            if need <= MM_TILE_BUDGET and (tk == k_unit or tm >= 512):
                return tm, tk
    return 128, 128


def _mm(a, b, *, ta=False, tb=False, b_rows=None, b_split=1, out_split=1, out_dtype=F32, epilogue=None, extras=(), name):
    (k, m) = a.shape if ta else a.shape[::-1]
    b_row0, b_rows = (0, b.shape[-2]) if b_rows is None else b_rows
    b_cols = b.shape[-1] * b_split
    (n, k2) = (b_rows, b_cols) if tb else (b_cols, b_rows)
    assert k == k2, (a.shape, b.shape, ta, tb)
    n_unit = n // (out_split * (1 if tb else b_split))
    k_unit = k // (b_split if tb else 1)
    tn = _pick(n_unit, (1024, 1152, 768, 640, 512, 256, 128))
    tm, tk = _mm_tiles(m, n, k, k_unit, tn, a.dtype.itemsize, b.dtype.itemsize,
                       sum(jnp.dtype(dt).itemsize for dt in out_dtype) + 4 * len(extras) if epilogue else jnp.dtype(out_dtype).itemsize)
    nk = k // tk
    dims = (((0 if ta else 1,), (1 if tb else 0,)), ((), ()))

    def dot(a_ref, b_ref):
        b_val = b_ref[0] if b_split > 1 else b_ref[...]
        return lax.dot_general(a_ref[...].astype(BF16), b_val.astype(BF16), dims, preferred_element_type=F32)

    n_extra = len(extras)
    assert epilogue is None or out_split == 1

    def put(refs, val):
        if epilogue is not None:
            for o_ref, res in zip(refs[n_extra:], epilogue(val, *[r[...] for r in refs[:n_extra]])):
                o_ref[...] = res.astype(o_ref.dtype)
        elif out_split > 1:
            refs[0][0] = val.astype(refs[0].dtype)
        else:
            refs[0][...] = val.astype(refs[0].dtype)

    def body_single(a_ref, b_ref, *refs):
        put(refs, dot(a_ref, b_ref))

    def body_acc(a_ref, b_ref, *refs):
        kk = pl.program_id(2)
        acc_ref = refs[-1]

        @pl.when(kk == 0)
        def _():
            acc_ref[...] = jnp.zeros_like(acc_ref)

        acc_ref[...] += dot(a_ref, b_ref)

        @pl.when(kk == nk - 1)
        def _():
            put(refs[:-1], acc_ref[...])

    a_spec = pl.BlockSpec((tk, tm), lambda i, j, kk: (kk, i)) if ta else pl.BlockSpec((tm, tk), lambda i, j, kk: (i, kk))
    if b_split == 1:
        off = b_row0 // (tn if tb else tk)
        assert off * (tn if tb else tk) == b_row0
        b_spec = pl.BlockSpec((tn, tk), lambda i, j, kk: (j + off, kk)) if tb else pl.BlockSpec((tk, tn), lambda i, j, kk: (kk + off, j))
    elif tb:
        per = k_unit // tk
        b_spec = pl.BlockSpec((1, tn, tk), lambda i, j, kk: (kk // per, j, kk % per))
    else:
        per = n // b_split // tn
        b_spec = pl.BlockSpec((1, tk, tn), lambda i, j, kk: (j // per, kk, j % per))
    if out_split == 1:
        o_spec = pl.BlockSpec((tm, tn), lambda i, j, kk: (i, j))
        o_shape = None if epilogue is not None else jax.ShapeDtypeStruct((m, n), out_dtype)
    else:
        per_o = n // out_split // tn
        o_spec = pl.BlockSpec((1, tm, tn), lambda i, j, kk: (j // per_o, i, j % per_o))
        o_shape = jax.ShapeDtypeStruct((out_split, m, n // out_split), out_dtype)
    if epilogue is not None:
        o_shape = [jax.ShapeDtypeStruct((m, n), dt) for dt in out_dtype]
        o_spec = [o_spec] * len(out_dtype)
    return pl.pallas_call(
        body_single if nk == 1 else body_acc, name=name, grid=(m // tm, n // tn, nk),
        in_specs=[a_spec, b_spec] + [pl.BlockSpec((tm, tn), lambda i, j, kk: (i, j))] * n_extra, out_specs=o_spec, out_shape=o_shape,
        scratch_shapes=[] if nk == 1 else [pltpu.VMEM((tm, tn), F32)],
        compiler_params=_params("parallel", "parallel", "arbitrary"),
    )(a, b, *extras)


def _ew(fn, ins, out_dtypes, name, tc=None):
    shape = ins[0].shape
    lead, (rows, cols) = shape[:-2], shape[-2:]
    tc = cols if tc is None else tc
    if tc > 1024:
        tr = _pick(rows, (EW_ROWS, 128, 8))
    elif tc > 128:
        tr = _pick(rows, (2 * EW_ROWS, EW_ROWS, 128, 8))
    else:
        tr = _pick(rows, (4096, 2256, 2048, 1024, EW_ROWS, 8))
    n_in = len(ins)

    def body(*refs):
        res = fn(*[r[...] for r in refs[:n_in]])
        for o_ref, val in zip(refs[n_in:], res):
            o_ref[...] = val.astype(o_ref.dtype)

    if lead:
        spec = pl.BlockSpec((None, tr, tc), lambda l, i, j: (l, i, j))
    else:
        spec = pl.BlockSpec((tr, tc), lambda i, j: (i, j))
    return pl.pallas_call(
        body, name=name, grid=lead + (rows // tr, cols // tc),
        in_specs=[spec] * n_in, out_specs=[spec] * len(out_dtypes),
        out_shape=[jax.ShapeDtypeStruct(shape, dt) for dt in out_dtypes],
        compiler_params=_params(*(["parallel"] * (len(lead) + 2))),
    )(*ins)


def _relu2_fwd(a):
    r = jnp.maximum(a, 0.0)
    return a, r * r


def _relu2_bwd(dr, a):
    return (dr * (2.0 * jnp.maximum(a, 0.0)),)


def _adamw_math(w, g, m, v):
    m = ADAM_B1 * m + (1.0 - ADAM_B1) * g
    v = ADAM_B2 * v + (1.0 - ADAM_B2) * (g * g)
    m_hat = m / (1.0 - ADAM_B1 ** ADAM_STEP)
    v_hat = v / (1.0 - ADAM_B2 ** ADAM_STEP)
    delta = -ADAM_LR * (m_hat / (jnp.sqrt(v_hat) + ADAM_EPS) + ADAM_WD * w)
    return delta, m, v


def _adamw(w, g, m, v, name):
    return _ew(_adamw_math, [w, g, m, v], [F32, F32, F32], name)


def _adamw_layers(w, g_layers, m, v, name):
    depth, rows, cols = w.shape
    tr = _pick(rows, (2 * EW_ROWS, EW_ROWS, 128, 8)) if cols <= 1024 else _pick(rows, (EW_ROWS, 128, 8))

    def body(w_ref, *refs):
        g_refs, (m_ref, v_ref, d_ref, mo_ref, vo_ref, go_ref) = refs[:depth], refs[depth:]
        layer = pl.program_id(0)
        g = g_refs[0][...]
        for l in range(1, depth):
            g = jnp.where(layer == l, g_refs[l][...], g)
        d_ref[...], mo_ref[...], vo_ref[...] = _adamw_math(w_ref[...], g, m_ref[...], v_ref[...])
        go_ref[...] = g

    spec = pl.BlockSpec((None, tr, cols), lambda l, i: (l, i, 0))
    g_specs = [pl.BlockSpec((tr, cols), lambda l, i, k=k: (jnp.where(l == k, i, 0), 0)) for k in range(depth)]
    return pl.pallas_call(
        body, name=name, grid=(depth, rows // tr),
        in_specs=[spec] + g_specs + [spec, spec], out_specs=[spec] * 4,
        out_shape=[jax.ShapeDtypeStruct(w.shape, F32)] * 4, compiler_params=_params("arbitrary", "arbitrary"),
    )(w, *g_layers, m, v)


def _adamw_many(ws, gs, ms, vs, name):
    n = len(ws)

    def body(*refs):
        for i in range(n):
            res = _adamw_math(*[refs[k * n + i][...] for k in range(4)])
            for k in range(3):
                refs[(4 + k) * n + i][...] = res[k]

    outs = pl.pallas_call(
        body, name=name, out_shape=[jax.ShapeDtypeStruct(w.shape, F32) for w in ws] * 3,
        compiler_params=pltpu.CompilerParams(vmem_limit_bytes=V7X_VMEM_LIMIT),
    )(*ws, *gs, *ms, *vs)
    return outs[:n], outs[n:2 * n], outs[2 * n:]


def _row_spec(cols, block=0):
    return pl.BlockSpec((ROW_TILE, cols), lambda i, block=block: (i, block))


def _vec_spec(cols):
    return pl.BlockSpec((1, cols), lambda i: (0, 0))


def _vec_args(*vecs):
    arrays = [v[0] if isinstance(v, tuple) else v for v in vecs]
    specs = [pl.BlockSpec((None, 1, D), lambda i, row=v[1]: (row, 0, 0)) if isinstance(v, tuple) else _vec_spec(D) for v in vecs]
    return arrays, specs


def _sum_spec(cols):
    return pl.BlockSpec((8, cols), lambda i: (0, 0))


def _rstd(x):
    return lax.rsqrt(jnp.mean(x * x, axis=-1, keepdims=True) + RMS_EPS)


def _modnorm_fwd(x, g, shift, scale, name):
    s = x.shape[0]

    def body(x_ref, g_ref, sh_ref, sc_ref, h_ref):
        xv = x_ref[...]
        n = xv * _rstd(xv)
        h_ref[...] = ((n * g_ref[...]) * (1.0 + sc_ref[...]) + sh_ref[...]).astype(BF16)

    vecs, vec_specs = _vec_args(g, shift, scale)
    return pl.pallas_call(
        body, name=name, grid=(s // ROW_TILE,),
        in_specs=[_row_spec(D)] + vec_specs, out_specs=_row_spec(D),
        out_shape=jax.ShapeDtypeStruct((s, D), BF16), compiler_params=_params("parallel"),
    )(x, *vecs)


def _post_fwd(x, y, g, gate, name):
    s = x.shape[0]

    def body(x_ref, y_ref, g_ref, gate_ref, o_ref):
        yv = y_ref[...]
        o_ref[...] = x_ref[...] + gate_ref[...] * ((yv * _rstd(yv)) * g_ref[...])

    vecs, vec_specs = _vec_args(g, gate)
    return pl.pallas_call(
        body, name=name, grid=(s // ROW_TILE,),
        in_specs=[_row_spec(D), _row_spec(D)] + vec_specs, out_specs=_row_spec(D),
        out_shape=jax.ShapeDtypeStruct((s, D), F32), compiler_params=_params("parallel"),
    )(x, y, *vecs)


def _post_bwd(dxo, y, g, gate, name):
    s = dxo.shape[0]

    def body(d_ref, y_ref, g_ref, gate_ref, dy_ref, sum_ref):
        @pl.when(pl.program_id(0) == 0)
        def _():
            sum_ref[...] = jnp.zeros_like(sum_ref)

        dv, yv = d_ref[...], y_ref[...]
        r = _rstd(yv)
        n = yv * r
        sum_ref[0:1, :] += jnp.sum(dv * (n * g_ref[...]), axis=0, keepdims=True)
        sum_ref[1:2, :] += jnp.sum((dv * gate_ref[...]) * n, axis=0, keepdims=True)
        dn = (dv * gate_ref[...]) * g_ref[...]
        dy_ref[...] = (r * (dn - n * jnp.mean(dn * n, axis=-1, keepdims=True))).astype(BF16)

    vecs, vec_specs = _vec_args(g, gate)
    return pl.pallas_call(
        body, name=name, grid=(s // ROW_TILE,),
        in_specs=[_row_spec(D), _row_spec(D)] + vec_specs,
        out_specs=[_row_spec(D), _sum_spec(D)],
        out_shape=[jax.ShapeDtypeStruct((s, D), BF16), jax.ShapeDtypeStruct((8, D), F32)],
        compiler_params=_params("arbitrary"),
    )(dxo, y, *vecs)


def _modnorm_bwd(dh, x, dxo, g, scale, name):
    s = dh.shape[0]

    def body(dh_ref, x_ref, d_ref, g_ref, sc_ref, dx_ref, sum_ref):
        @pl.when(pl.program_id(0) == 0)
        def _():
            sum_ref[...] = jnp.zeros_like(sum_ref)

        dhv, xv = dh_ref[...], x_ref[...]
        r = _rstd(xv)
        n = xv * r
        one_sc = 1.0 + sc_ref[...]
        sum_ref[0:1, :] += jnp.sum(dhv, axis=0, keepdims=True)
        sum_ref[1:2, :] += jnp.sum(dhv * (n * g_ref[...]), axis=0, keepdims=True)
        sum_ref[2:3, :] += jnp.sum((dhv * one_sc) * n, axis=0, keepdims=True)
        dn = (dhv * one_sc) * g_ref[...]
        dx_ref[...] = d_ref[...] + r * (dn - n * jnp.mean(dn * n, axis=-1, keepdims=True))

    vecs, vec_specs = _vec_args(g, scale)
    return pl.pallas_call(
        body, name=name, grid=(s // ROW_TILE,),
        in_specs=[_row_spec(D), _row_spec(D), _row_spec(D)] + vec_specs,
        out_specs=[_row_spec(D), _sum_spec(D)],
        out_shape=[jax.ShapeDtypeStruct((s, D), F32), jax.ShapeDtypeStruct((8, D), F32)],
        compiler_params=_params("arbitrary"),
    )(dh, x, dxo, *vecs)


def _post_pre_fwd(x, y, g_post, gate, g_pre, shift, scale, name):
    s = x.shape[0]

    def body(x_ref, y_ref, gp_ref, gate_ref, g_ref, sh_ref, sc_ref, o_ref, h_ref):
        yv = y_ref[...]
        xo = x_ref[...] + gate_ref[...] * ((yv * _rstd(yv)) * gp_ref[...])
        o_ref[...] = xo
        h_ref[...] = (((xo * _rstd(xo)) * g_ref[...]) * (1.0 + sc_ref[...]) + sh_ref[...]).astype(BF16)

    vecs, vec_specs = _vec_args(g_post, gate, g_pre, shift, scale)
    return pl.pallas_call(
        body, name=name, grid=(s // ROW_TILE,),
        in_specs=[_row_spec(D), _row_spec(D)] + vec_specs, out_specs=[_row_spec(D), _row_spec(D)],
        out_shape=[jax.ShapeDtypeStruct((s, D), F32), jax.ShapeDtypeStruct((s, D), BF16)], compiler_params=_params("parallel"),
    )(x, y, *vecs)


def _pre_post_bwd(dh, x, dxo, g_pre, scale, y, g_post, gate, name):
    s = dh.shape[0]

    def body(dh_ref, x_ref, d_ref, y_ref, g_ref, sc_ref, gp_ref, gate_ref, dx_ref, dy_ref, sum_ref):
        @pl.when(pl.program_id(0) == 0)
        def _():
            sum_ref[...] = jnp.zeros_like(sum_ref)

        dhv, xv = dh_ref[...], x_ref[...]
        r = _rstd(xv)
        n = xv * r
        one_sc = 1.0 + sc_ref[...]
        sum_ref[0:1, :] += jnp.sum(dhv, axis=0, keepdims=True)
        sum_ref[1:2, :] += jnp.sum(dhv * (n * g_ref[...]), axis=0, keepdims=True)
        sum_ref[2:3, :] += jnp.sum((dhv * one_sc) * n, axis=0, keepdims=True)
        dn = (dhv * one_sc) * g_ref[...]
        dv = d_ref[...] + r * (dn - n * jnp.mean(dn * n, axis=-1, keepdims=True))
        dx_ref[...] = dv

        yv = y_ref[...]
        ry = _rstd(yv)
        ny = yv * ry
        sum_ref[3:4, :] += jnp.sum(dv * (ny * gp_ref[...]), axis=0, keepdims=True)
        sum_ref[4:5, :] += jnp.sum((dv * gate_ref[...]) * ny, axis=0, keepdims=True)
        dny = (dv * gate_ref[...]) * gp_ref[...]
        dy_ref[...] = (ry * (dny - ny * jnp.mean(dny * ny, axis=-1, keepdims=True))).astype(BF16)

    vecs, vec_specs = _vec_args(g_pre, scale, g_post, gate)
    return pl.pallas_call(
        body, name=name, grid=(s // ROW_TILE,),
        in_specs=[_row_spec(D)] * 4 + vec_specs,
        out_specs=[_row_spec(D), _row_spec(D), _sum_spec(D)],
        out_shape=[jax.ShapeDtypeStruct((s, D), F32), jax.ShapeDtypeStruct((s, D), BF16), jax.ShapeDtypeStruct((8, D), F32)],
        compiler_params=_params("arbitrary"),
    )(dh, x, dxo, y, *vecs)


def _loss_head(y, target):
    s = y.shape[0]

    def body(y_ref, t_ref, dy_ref, sum_ref):
        @pl.when(pl.program_id(0) == 0)
        def _():
            sum_ref[...] = jnp.zeros_like(sum_ref)

        err = y_ref[...] - t_ref[...]
        dy_ref[...] = err * (1.0 / D)
        sum_ref[...] += jnp.sum(err * err)

    return pl.pallas_call(
        body, name="loss_head", grid=(s // ROW_TILE,),
        in_specs=[_row_spec(D), _row_spec(D)],
        out_specs=[_row_spec(D), pl.BlockSpec((8, 128), lambda i: (0, 0))],
        out_shape=[jax.ShapeDtypeStruct((s, D), F32), jax.ShapeDtypeStruct((8, 128), F32)],
        compiler_params=_params("arbitrary"),
    )(y, target)


def _merge_fwd(z, pa, pb, pc):
    s = z.shape[0]

    def body(g0_ref, g1_ref, g2_ref, pa_ref, pb_ref, pc_ref, o_ref):
        o_ref[...] = (jax.nn.sigmoid(g0_ref[...]) * pa_ref[...] + jax.nn.sigmoid(g1_ref[...]) * pb_ref[...]
                      + jax.nn.sigmoid(g2_ref[...]) * pc_ref[...]).astype(BF16)

    return pl.pallas_call(
        body, name="merge_fwd", grid=(s // ROW_TILE,),
        in_specs=[_row_spec(D, 0), _row_spec(D, 1), _row_spec(D, 2), _row_spec(D), _row_spec(D), _row_spec(D)],
        out_specs=_row_spec(D), out_shape=jax.ShapeDtypeStruct((s, D), BF16),
        compiler_params=_params("parallel"),
    )(z, z, z, pa, pb, pc)


def _merge_bwd(dm, z, pa, pb, pc):
    s = z.shape[0]

    def body(dm_ref, g0_ref, g1_ref, g2_ref, pa_ref, pb_ref, pc_ref, dgl_ref, da_ref, db_ref, dc_ref):
        dmv = dm_ref[...]
        for i, (g_ref, p_ref, d_ref) in enumerate(((g0_ref, pa_ref, da_ref), (g1_ref, pb_ref, db_ref), (g2_ref, pc_ref, dc_ref))):
            gate = jax.nn.sigmoid(g_ref[...])
            dgl_ref[:, i * D:(i + 1) * D] = ((dmv * p_ref[...]) * (gate * (1.0 - gate))).astype(BF16)
            d_ref[...] = (dmv * gate).astype(BF16)

    return pl.pallas_call(
        body, name="merge_bwd", grid=(s // ROW_TILE,),
        in_specs=[_row_spec(D), _row_spec(D, 0), _row_spec(D, 1), _row_spec(D, 2), _row_spec(D), _row_spec(D), _row_spec(D)],
        out_specs=[_row_spec(3 * D), _row_spec(D), _row_spec(D), _row_spec(D)],
        out_shape=[jax.ShapeDtypeStruct((s, Z_COLS), BF16)] + [jax.ShapeDtypeStruct((s, D), BF16)] * 3,
        compiler_params=_params("parallel"),
    )(dm, z, z, z, pa, pb, pc)


def _shift_down(v, n):
    row = lax.broadcasted_iota(jnp.int32, v.shape, 0)
    return jnp.where(row >= n, pltpu.roll(v, n, axis=0), 0.0)


def _shift_up(v, n):
    s = v.shape[0]
    row = lax.broadcasted_iota(jnp.int32, v.shape, 0)
    return jnp.where(row < s - n, pltpu.roll(v, s - n, axis=0), 0.0)


def _log_sigmoid(v):
    return jnp.minimum(v, 0.0) - jnp.log1p(jnp.exp(-jnp.abs(v)))


def _cumf_fwd(fl, bias):
    s = fl.shape[0]

    def body(fl_ref, b_ref, o_ref):
        acc = _log_sigmoid(fl_ref[...] + b_ref[...])
        step = 1
        while step < s:
            acc = acc + _shift_down(acc, step)
            step *= 2
        o_ref[...] = acc

    return pl.pallas_call(body, name="cumf_fwd", out_shape=jax.ShapeDtypeStruct((s, 128), F32),
                          compiler_params=pltpu.CompilerParams(vmem_limit_bytes=V7X_VMEM_LIMIT))(fl, bias)


def _cumf_bwd(dcum, fl, bias):
    s = fl.shape[0]

    def body(d_ref, fl_ref, b_ref, dfl_ref, db_ref):
        acc = d_ref[...]
        step = 1
        while step < s:
            acc = acc + _shift_up(acc, step)
            step *= 2
        dfl = acc * jax.nn.sigmoid(-(fl_ref[...] + b_ref[...]))
        dfl_ref[...] = dfl.astype(BF16)
        db_ref[...] = jnp.broadcast_to(jnp.sum(dfl, axis=0, keepdims=True), (8, 128))

    return pl.pallas_call(
        body, name="cumf_bwd",
        out_shape=[jax.ShapeDtypeStruct((s, 128), BF16), jax.ShapeDtypeStruct((8, 128), F32)],
        compiler_params=pltpu.CompilerParams(vmem_limit_bytes=V7X_VMEM_LIMIT))(dcum, fl, bias)


def _pool_windows(v, shift):
    s2 = v + shift(v, 1)
    s4 = s2 + shift(s2, 2)
    s8 = s4 + shift(s4, 4)
    s16 = s8 + shift(s8, 8)
    group = lax.broadcasted_iota(jnp.int32, v.shape, 1) // 64
    return jnp.where(group == 0, s2, jnp.where(group == 1, s4, jnp.where(group == 2, s8, s16)))


def _pool_count(shape):
    group = lax.broadcasted_iota(jnp.int32, shape, 1) // 64
    window = jnp.where(group == 0, 2.0, jnp.where(group == 1, 4.0, jnp.where(group == 2, 8.0, 16.0)))
    t1 = (lax.broadcasted_iota(jnp.int32, shape, 0) + 1).astype(F32)
    return jnp.minimum(t1, window)


def _pc_specs(s):
    zcol = lambda blk: pl.BlockSpec((s, 256), lambda i, blk=blk: (0, blk))
    first = Z_PC // 256
    return [zcol(first), zcol(first + 1), zcol(first + 2), zcol(first + 3),
            pl.BlockSpec((256, 256), lambda i: (0, 0)), pl.BlockSpec((1, 256), lambda i: (0, 0)),
            pl.BlockSpec((3, 256), lambda i: (0, 0))]


def _poolconv_fwd(z, wbd, pscale, convw):
    s = z.shape[0]

    def body(pu_ref, ch_ref, cb_ref, cc_ref, w_ref, ps_ref, cw_ref, yb_ref, yc_ref):
        u = pu_ref[...]
        p = _pool_windows(u, _shift_down) / _pool_count(u.shape) - u
        yb = jnp.dot(p.astype(BF16), w_ref[...].astype(BF16), preferred_element_type=F32) * ps_ref[...]
        yb_ref[...] = yb.astype(BF16)
        uc = cc_ref[...] * ch_ref[...]
        cw = cw_ref[...]
        conv = cw[0:1, :] * _shift_down(uc, 2) + cw[1:2, :] * _shift_down(uc, 1) + cw[2:3, :] * uc
        yc_ref[...] = (cb_ref[...] * conv).astype(BF16)

    out = pl.BlockSpec((s, 256), lambda i: (0, 0))
    return pl.pallas_call(
        body, name="poolconv_fwd", grid=(1,), in_specs=_pc_specs(s), out_specs=[out, out],
        out_shape=[jax.ShapeDtypeStruct((s, 256), BF16)] * 2, compiler_params=_params("arbitrary"),
    )(z, z, z, z, wbd, pscale, convw)


def _poolconv_bwd(dyb, dyc, z, wbd, pscale, convw):
    s = z.shape[0]

    def body(dyb_ref, dyc_ref, pu_ref, ch_ref, cb_ref, cc_ref, w_ref, ps_ref, cw_ref, dz_ref, dw_ref, dps_ref, dcw_ref):
        u = pu_ref[...]
        count = _pool_count(u.shape)
        p = (_pool_windows(u, _shift_down) / count - u).astype(BF16)
        wb = w_ref[...].astype(BF16)
        dyb_v = dyb_ref[...]
        pw = jnp.dot(p, wb, preferred_element_type=F32)
        dps_ref[...] = jnp.broadcast_to(jnp.sum(dyb_v * pw, axis=0, keepdims=True), (8, 256))
        dys = (dyb_v * ps_ref[...]).astype(BF16)
        dp = lax.dot_general(dys, wb, (((1,), (1,)), ((), ())), preferred_element_type=F32)
        dw_ref[...] = lax.dot_general(p, dys, (((0,), (0,)), ((), ())), preferred_element_type=F32)
        dz_ref[:, 0:256] = (_pool_windows(dp / count, _shift_up) - dp).astype(BF16)

        ch, cb, cc = ch_ref[...], cb_ref[...], cc_ref[...]
        uc = cc * ch
        cw = cw_ref[...]
        u2, u1 = _shift_down(uc, 2), _shift_down(uc, 1)
        conv = cw[0:1, :] * u2 + cw[1:2, :] * u1 + cw[2:3, :] * uc
        dyc_v = dyc_ref[...]
        dconv = dyc_v * cb
        du = cw[0:1, :] * _shift_up(dconv, 2) + cw[1:2, :] * _shift_up(dconv, 1) + cw[2:3, :] * dconv
        dz_ref[:, 256:512] = (du * cc).astype(BF16)
        dz_ref[:, 512:768] = (dyc_v * conv).astype(BF16)
        dz_ref[:, 768:1024] = (du * ch).astype(BF16)
        dcw_ref[...] = jnp.zeros_like(dcw_ref)
        dcw_ref[0:1, :] = jnp.sum(dconv * u2, axis=0, keepdims=True)
        dcw_ref[1:2, :] = jnp.sum(dconv * u1, axis=0, keepdims=True)
        dcw_ref[2:3, :] = jnp.sum(dconv * uc, axis=0, keepdims=True)

    blk = lambda r, c: pl.BlockSpec((r, c), lambda i: (0, 0))
    return pl.pallas_call(
        body, name="poolconv_bwd", grid=(1,),
        in_specs=[blk(s, 256), blk(s, 256)] + _pc_specs(s),
        out_specs=[blk(s, 1024), blk(256, 256), blk(8, 256), blk(8, 256)],
        out_shape=[jax.ShapeDtypeStruct((s, 1024), BF16), jax.ShapeDtypeStruct((256, 256), F32),
                   jax.ShapeDtypeStruct((8, 256), F32), jax.ShapeDtypeStruct((8, 256), F32)],
        compiler_params=_params("arbitrary"),
    )(dyb, dyc, z, z, z, z, wbd, pscale, convw)


_NT = (((1,), (1,)), ((), ()))
_TN = (((0,), (0,)), ((), ()))


ATT_Q, ATT_K = 256, 256
ATT_HEADS_BWD = 8
ATT_HEADS = 8


def _att_logits(q, k, fr, q0, k0, masked):
    logits = lax.dot_general(q, k, _NT, preferred_element_type=F32) - fr
    if not masked:
        return logits
    row = q0 + lax.broadcasted_iota(jnp.int32, logits.shape, 0)
    col = k0 + lax.broadcasted_iota(jnp.int32, logits.shape, 1)
    return jnp.where(row >= col, logits, NEG_INF)


def _causal_sweep(step, qi, init):
    n_full = (qi * ATT_Q) // ATT_K
    carry = lax.fori_loop(0, n_full, lambda j, carry: step(j, carry, False), init)
    return step(n_full, carry, True)


HEAD_PAIRS = HEADS // 2


def _lane_pick(v, lane, idx):
    return jnp.sum(jnp.where(lane == idx, v, 0.0), axis=-1, keepdims=True)


def _lane_put(lane, idx, col):
    return jnp.where(lane == idx, col, 0.0)


def _split_heads(v, low):
    zero = jnp.zeros_like(v)
    return jnp.where(low, v, zero), jnp.where(low, zero, v)


def _attn_fwd(qkv, fr):
    s = qkv.shape[0]
    nk = s // ATT_K
    width = ATT_HEADS * HEAD_DIM
    groups = HEADS // ATT_HEADS

    def body(q_ref, k_ref, v_ref, fr_ref, o_ref, lse_ref):
        qi, grp = pl.program_id(0), pl.program_id(1)
        lane = lax.broadcasted_iota(jnp.int32, (ATT_Q, 128), 1)
        low = lane < HEAD_DIM
        qs = []
        for pr in range(ATT_HEADS // 2):
            qs += _split_heads(q_ref[:, 128 * pr:128 * (pr + 1)] * (HEAD_DIM ** -0.5), low)

        def step(j, carry, masked):
            k0 = pl.multiple_of(j * ATT_K, ATT_K)
            out = []
            for h in range(ATT_HEADS):
                cols = slice(128 * (h // 2), 128 * (h // 2 + 1))
                m, l, acc = carry[h]
                logits = _att_logits(qs[h], k_ref[pl.ds(k0, ATT_K), cols], fr_ref[h, pl.ds(j, 1), :], qi * ATT_Q, k0, masked)
                m_new = jnp.maximum(m, jnp.max(logits, axis=-1, keepdims=True))
                p = jnp.exp(logits - m_new)
                alpha = jnp.exp(m - m_new)
                l = alpha * l + jnp.sum(p, axis=-1, keepdims=True)
                acc = alpha * acc + jnp.dot(p.astype(BF16), v_ref[pl.ds(k0, ATT_K), cols], preferred_element_type=F32)
                out.append((m_new, l, acc))
            return tuple(out)

        one = (jnp.full((ATT_Q, 1), NEG_INF, F32), jnp.zeros((ATT_Q, 1), F32), jnp.zeros((ATT_Q, 128), F32))
        done = _causal_sweep(step, qi, (one,) * ATT_HEADS)

        @pl.when(grp == 0)
        def _():
            lse_ref[...] = jnp.zeros_like(lse_ref)

        lse = jnp.zeros((ATT_Q, 128), F32)
        for pr in range(ATT_HEADS // 2):
            (m0, l0, acc0), (m1, l1, acc1) = done[2 * pr], done[2 * pr + 1]
            o_ref[:, 128 * pr:128 * (pr + 1)] = jnp.where(low, acc0 / l0, acc1 / l1)
            head = ATT_HEADS * grp + 2 * pr
            lse = lse + _lane_put(lane, head, m0 + jnp.log(l0)) + _lane_put(lane, head + 1, m1 + jnp.log(l1))
        lse_ref[...] += lse

    return pl.pallas_call(
        body, name="attn_fwd", grid=(s // ATT_Q, groups),
        in_specs=[pl.BlockSpec((ATT_Q, width), lambda i, g: (i, g)),
                  pl.BlockSpec((s, width), lambda i, g: (0, groups + g)),
                  pl.BlockSpec((s, width), lambda i, g: (0, 2 * groups + g)),
                  pl.BlockSpec((ATT_HEADS, nk, ATT_K), lambda i, g: (g, 0, 0))],
        out_specs=[pl.BlockSpec((ATT_Q, width), lambda i, g: (i, g)), pl.BlockSpec((ATT_Q, 128), lambda i, g: (i, 0))],
        out_shape=[jax.ShapeDtypeStruct((s, A_WIDTH), F32), jax.ShapeDtypeStruct((s, 128), F32)],
        compiler_params=_params("parallel", "arbitrary"),
    )(qkv, qkv, qkv, fr)


def _attn_bwd(qkv, do, o, lse, fr):
    s = qkv.shape[0]
    nk = s // ATT_K
    scale = HEAD_DIM ** -0.5
    heads = ATT_HEADS_BWD
    width = heads * HEAD_DIM
    groups = HEADS // heads

    def body(q_ref, k_ref, v_ref, do_ref, o_ref, lse_ref, fr_ref, dq_ref, dk_ref, dv_ref, dfc_ref, dfr_ref, dk_acc, dv_acc):
        grp = pl.program_id(0)
        lane = lax.broadcasted_iota(jnp.int32, (ATT_Q, 128), 1)
        low = lane < HEAD_DIM
        low_t = lax.broadcasted_iota(jnp.int32, (128, ATT_Q), 0) < HEAD_DIM
        dk_acc[...] = jnp.zeros_like(dk_acc)
        dv_acc[...] = jnp.zeros_like(dv_acc)
        dfr_ref[...] = jnp.zeros_like(dfr_ref)

        @pl.when(grp == 0)
        def _():
            dfc_ref[...] = jnp.zeros_like(dfc_ref)

        def outer(i, carry):
            q0 = pl.multiple_of(i * ATT_Q, ATT_Q)
            rows = pl.ds(q0, ATT_Q)
            lsev = lse_ref[rows, :]
            qts, dots, qs, dos, deltas, lses = [], [], [], [], [], []
            for pr in range(heads // 2):
                pcols = slice(128 * pr, 128 * (pr + 1))
                q2, do2 = q_ref[rows, pcols] * scale, do_ref[rows, pcols]
                prod = do2 * o_ref[rows, pcols]
                deltas += [jnp.sum(jnp.where(low, prod, 0.0), axis=-1, keepdims=True),
                           jnp.sum(jnp.where(low, 0.0, prod), axis=-1, keepdims=True)]
                dob2 = do2.astype(BF16)
                qts += _split_heads(q2.astype(F32).T.astype(BF16), low_t)
                dots += _split_heads(do2.T.astype(BF16), low_t)
                qs += _split_heads(q2, low)
                dos += _split_heads(dob2, low)
                lses += [_lane_pick(lsev, lane, heads * grp + 2 * pr), _lane_pick(lsev, lane, heads * grp + 2 * pr + 1)]

            def inner(j, carry, masked):
                k0 = pl.multiple_of(j * ATT_K, ATT_K)
                krows = pl.ds(k0, ATT_K)
                out, dkt, dvt = [], [], []
                for h in range(heads):
                    pcols = slice(128 * (h // 2), 128 * (h // 2 + 1))
                    dq, dfc = carry[h]
                    k2 = k_ref[krows, pcols]
                    p = jnp.exp(_att_logits(qs[h], k2, fr_ref[h, pl.ds(j, 1), :], q0, k0, masked) - lses[h])
                    dp = lax.dot_general(dos[h], v_ref[krows, pcols], _NT, preferred_element_type=F32)
                    ds = p * (dp - deltas[h])
                    dsb = ds.astype(BF16)
                    dkt.append(jnp.dot(qts[h], dsb, preferred_element_type=F32))
                    dvt.append(jnp.dot(dots[h], p.astype(BF16), preferred_element_type=F32))
                    dfr_ref[h, pl.ds(j, 1), :] -= jnp.sum(ds, axis=0, keepdims=True)
                    out.append((dq + jnp.dot(dsb, k2, preferred_element_type=F32), dfc + (ds[:, :128] + ds[:, 128:])))
                for pr in range(heads // 2):
                    prows = slice(128 * pr, 128 * (pr + 1))
                    dk_acc[j, prows, :] += dkt[2 * pr] + dkt[2 * pr + 1]
                    dv_acc[j, prows, :] += dvt[2 * pr] + dvt[2 * pr + 1]
                return tuple(out)

            one = (jnp.zeros((ATT_Q, 128), F32), jnp.zeros((ATT_Q, 128), F32))
            done = _causal_sweep(inner, i, (one,) * heads)
            dfc = jnp.zeros((ATT_Q, 128), F32)
            for pr in range(heads // 2):
                (dq0, dfc0), (dq1, dfc1) = done[2 * pr], done[2 * pr + 1]
                dq_ref[rows, 128 * pr:128 * (pr + 1)] = (jnp.where(low, dq0, dq1) * scale).astype(BF16)
                head = heads * grp + 2 * pr
                dfc = (dfc + _lane_put(lane, head, jnp.sum(dfc0, axis=-1, keepdims=True))
                       + _lane_put(lane, head + 1, jnp.sum(dfc1, axis=-1, keepdims=True)))
            dfc_ref[rows, :] += dfc
            return carry

        lax.fori_loop(0, s // ATT_Q, outer, 0)
        for j in range(nk):
            for pr in range(heads // 2):
                prows, pcols = slice(128 * pr, 128 * (pr + 1)), slice(128 * pr, 128 * (pr + 1))
                dk_ref[ATT_K * j:ATT_K * (j + 1), pcols] = dk_acc[j, prows, :].T.astype(BF16)
                dv_ref[ATT_K * j:ATT_K * (j + 1), pcols] = dv_acc[j, prows, :].T.astype(BF16)

    part = lambda first: pl.BlockSpec((s, width), lambda g, first=first: (0, first + g))
    whole = pl.BlockSpec((s, 128), lambda g: (0, 0))
    rowv = pl.BlockSpec((heads, nk, ATT_K), lambda g: (g, 0, 0))
    return pl.pallas_call(
        body, name="attn_bwd", grid=(groups,),
        in_specs=[part(0), part(groups), part(2 * groups), part(0), part(0), whole, rowv],
        out_specs=[part(0), part(0), part(0), whole, rowv],
        out_shape=[jax.ShapeDtypeStruct((s, A_WIDTH), BF16)] * 3 + [jax.ShapeDtypeStruct((s, 128), F32), jax.ShapeDtypeStruct((HEADS, nk, ATT_K), F32)],
        scratch_shapes=[pltpu.VMEM((nk, width, ATT_K), F32), pltpu.VMEM((nk, width, ATT_K), F32)],
        compiler_params=_params("arbitrary"),
    )(qkv, qkv, qkv, do, o, lse, fr)


def _ada_fwd(c_all, w_ada, b_loc):
    depth, _, n = w_ada.shape
    tn = 512

    def body(c_ref, w_ref, b_ref, o_ref, sc_ref):
        cv = c_ref[...]
        sc = cv * jax.nn.sigmoid(cv)
        sc_ref[...] = sc
        o_ref[0] = jnp.dot(sc.astype(BF16), w_ref[0].astype(BF16), preferred_element_type=F32) + b_ref[0]

    return pl.pallas_call(
        body, name="ada_fwd", grid=(depth, n // tn),
        in_specs=[pl.BlockSpec((N_DEV, D), lambda l, j: (0, 0)), pl.BlockSpec((1, D, tn), lambda l, j: (l, 0, j)),
                  pl.BlockSpec((1, 1, tn), lambda l, j: (l, 0, j))],
        out_specs=[pl.BlockSpec((1, N_DEV, tn), lambda l, j: (l, 0, j)), pl.BlockSpec((N_DEV, D), lambda l, j: (0, 0))],
        out_shape=[jax.ShapeDtypeStruct((depth, N_DEV, n), F32), jax.ShapeDtypeStruct((N_DEV, D), F32)],
        compiler_params=_params("arbitrary", "arbitrary"),
    )(c_all, w_ada, b_loc)


def _sum_devices(gathered):
    n = gathered.shape[1]
    tn = _pick(n, (1408, 1024, 640, 512, 128))

    def body(g_ref, o_ref):
        acc = g_ref[0:8, :]
        for dev in range(1, N_DEV):
            acc = acc + g_ref[8 * dev:8 * dev + 8, :]
        o_ref[...] = acc

    return pl.pallas_call(
        body, name="sum_devices", grid=(n // tn,),
        in_specs=[pl.BlockSpec((8 * N_DEV, tn), lambda j: (0, j))], out_specs=pl.BlockSpec((8, tn), lambda j: (0, j)),
        out_shape=jax.ShapeDtypeStruct((8, n), F32), compiler_params=_params("parallel"),
    )(gathered)


def _place():
    x, y, c = lax.axis_index("x"), lax.axis_index("y"), lax.axis_index("c")
    chips = [(1 - x, y), (x, 1 - y), (1 - x, 1 - y)]
    return x, y, c, chips


def _allgather8(block, name, after=()):
    m_per, n = block.shape

    def body(x_ref, *rest):
        out_ref, send_sems, recv_sems, local_sem = rest[len(after):]
        x, y, c, chips = _place()
        me, sibling = (x, y, c), (x, y, 1 - c)

        def rows(px, py, pc):
            return out_ref.at[pl.ds((4 * px + 2 * py + pc) * m_per, m_per), :]

        def copy(k, blk, to, src=None):
            return pltpu.make_async_remote_copy(
                src_ref=rows(*blk) if src is None else src, dst_ref=rows(*blk),
                send_sem=send_sems.at[k], recv_sem=recv_sems.at[k], device_id=to, device_id_type=MESH)

        mine = pltpu.make_async_copy(x_ref, rows(*me), local_sem)
        mine.start()
        first = [copy(0, me, sibling, src=x_ref)]
        first += [copy(1 + j, me, (*chip, c), src=x_ref) for j, chip in enumerate(chips)]
        for cp in first:
            cp.start()
        passed = [copy(4 + j, (*chip, c), sibling) for j, chip in enumerate(chips)]
        for j, chip in enumerate(chips):
            copy(1 + j, (*chip, c), me).wait_recv()
            passed[j].start()
        copy(0, sibling, me).wait_recv()
        for j, chip in enumerate(chips):
            copy(4 + j, (*chip, 1 - c), me).wait_recv()
        for cp in first + passed:
            cp.wait_send()
        mine.wait()

    return pl.pallas_call(
        body, name=name, out_shape=jax.ShapeDtypeStruct((N_DEV * m_per, n), block.dtype),
        in_specs=[pl.BlockSpec(memory_space=pltpu.VMEM)] + [pl.BlockSpec(memory_space=pl.ANY)] * len(after),
        out_specs=pl.BlockSpec(memory_space=pltpu.VMEM),
        scratch_shapes=[pltpu.SemaphoreType.DMA((7,)), pltpu.SemaphoreType.DMA((7,)), pltpu.SemaphoreType.DMA],
        compiler_params=pltpu.CompilerParams(vmem_limit_bytes=V7X_VMEM_LIMIT),
    )(block, *after)


_SEM = pl.BlockSpec(memory_space=pltpu.SEMAPHORE)
_DATAFLOW = pltpu.SideEffectType.DATAFLOW_SIDE_EFFECTING


def _plan_copies(plan, refs, send_sems, recv_sems):
    return [pltpu.make_async_remote_copy(src_ref=src, dst_ref=dst, send_sem=send_sems.at[i], recv_sem=recv_sems.at[i],
                                         device_id=to, device_id_type=MESH) for i, (src, dst, to) in enumerate(plan(refs))]


class _Token(NamedTuple):
    after: jax.Array
    tie: jax.Array


def _after_operand(after):
    return after.after if isinstance(after, _Token) else after


def _copies_start(bufs, plan, n_copies, after, name):
    nb = len(bufs)

    def body(*refs):
        for cp in _plan_copies(plan, refs[:nb], refs[nb + 1], refs[nb + 2]):
            cp.start()
        for token in refs[-2:]:
            token[...] = jnp.zeros_like(token)

    sem = pltpu.SemaphoreType.DMA((n_copies,))
    vmem = pl.BlockSpec(memory_space=pltpu.VMEM)
    outs = pl.pallas_call(
        body, name=name,
        out_shape=(sem, sem, *[pltpu.HBM(b.shape, b.dtype) for b in bufs], jax.ShapeDtypeStruct((8, 128), F32),
                   jax.ShapeDtypeStruct((1, 1), F32)),
        in_specs=[_HBM] * nb + [pl.BlockSpec(memory_space=pl.ANY)],
        out_specs=(_SEM, _SEM, *[_HBM] * nb, vmem, vmem),
        input_output_aliases={i: 2 + i for i in range(nb)},
        compiler_params=pltpu.CompilerParams(has_side_effects=_DATAFLOW),
    )(*[pltpu.with_memory_space_constraint(b, pltpu.HBM) for b in bufs], _after_operand(after))
    return outs[0], outs[1], list(outs[2:2 + nb]), _Token(outs[-2], outs[-1])


def _copies_wait(started, plan, after, name):
    send_sems, recv_sems, bufs, _ = started
    nb = len(bufs)

    def body(*refs):
        for cp in _plan_copies(plan, refs[:nb], refs[nb], refs[nb + 1]):
            cp.wait_send()
            cp.wait_recv()

    return list(pl.pallas_call(
        body, name=name, out_shape=tuple(pltpu.HBM(b.shape, b.dtype) for b in bufs),
        in_specs=[_HBM] * nb + [_SEM, _SEM, pl.BlockSpec(memory_space=pl.ANY)], out_specs=tuple([_HBM] * nb),
        input_output_aliases={i: i for i in range(nb)},
        compiler_params=pltpu.CompilerParams(has_side_effects=_DATAFLOW),
    )(*bufs, send_sems, recv_sems, _after_operand(after)))


def _half_rows(ref, axis, c):
    half = ref.shape[axis] // 2
    return pl.ds(c * half, half)


def _plan_gather_ici(refs):
    n = len(refs) // 2
    x, y, c, chips = _place()
    out = []
    for a in range(n):
        rows = _half_rows(refs[a], 0, c)
        out += [(refs[a].at[rows], refs[n + a].at[2 * x + y, rows], (*chip, c)) for chip in chips]
        out.append((refs[a], refs[n + a].at[2 * x + y], (x, y, 1 - c)))
    return out


def _plan_gather_d2d(refs):
    x, y, c, chips = _place()
    out = []
    for ref in refs:
        rows = _half_rows(ref, 1, c)
        for px, py in chips:
            landed = ref.at[2 * px + py, rows]
            out.append((landed, landed, (x, y, 1 - c)))
    return out


def _plan_rs_sibling(refs):
    n = len(refs) // 2
    x, y, c, _ = _place()
    return [(refs[a].at[pl.ds(0, refs[a].shape[0]), _half_rows(refs[a], 1, 1 - c)], refs[n + a], (x, y, 1 - c)) for a in range(n)]


def _plan_rs_chips(refs):
    n = len(refs) // 2
    x, y, c, chips = _place()
    return [(refs[a].at[2 * px + py], refs[n + a].at[k], (px, py, c)) for a in range(n) for k, (px, py) in enumerate(chips)]


def _plan_rs_share(refs):
    x, y, c, _ = _place()
    return [(ref.at[_half_rows(ref, 0, c)], ref.at[_half_rows(ref, 0, c)], (x, y, 1 - c)) for ref in refs]


def _chip_sum(g, other, sel, name, blocked=True):
    nblk, half, cdim = other.shape
    tr = _pick(half, (512, 256, 128, 64))
    per = half // tr

    def body(sel_ref, g_ref, t_ref, wire_ref, own_ref):
        total = g_ref[0] + t_ref[0]
        wire_ref[0] = total.astype(BF16)
        if blocked:
            @pl.when(pl.program_id(1) == sel_ref[1])
            def _():
                own_ref[...] = total
        else:
            own_ref[0] = total

    blk = pl.BlockSpec((1, tr, cdim), lambda i, p, sel_ref: (p, i, 0))
    own_spec = pl.BlockSpec((tr, cdim), lambda i, p, sel_ref: (i, 0)) if blocked else blk
    own_shape = jax.ShapeDtypeStruct((half, cdim) if blocked else other.shape, F32)
    return pl.pallas_call(
        body, name=name,
        grid_spec=pltpu.PrefetchScalarGridSpec(
            num_scalar_prefetch=1, grid=(per, nblk),
            in_specs=[pl.BlockSpec((1, tr, cdim), lambda i, p, sel_ref: (p, sel_ref[0] * per + i, 0)), blk],
            out_specs=[blk, own_spec]),
        out_shape=[jax.ShapeDtypeStruct(other.shape, BF16), own_shape],
        compiler_params=_params("parallel", "arbitrary"),
    )(sel, g, other)


def _final_sum(own, recv, sel, name):
    half, cdim = own.shape
    tr = _pick(half, (512, 256, 128, 64))
    per = half // tr

    def body(sel_ref, own_ref, r0_ref, r1_ref, r2_ref, o_ref):
        o_ref[...] = ((own_ref[...] + r0_ref[0].astype(F32)) + r1_ref[0].astype(F32)) + r2_ref[0].astype(F32)

    part = lambda k: pl.BlockSpec((1, tr, cdim), lambda i, sel_ref, k=k: (k, i, 0))
    return pl.pallas_call(
        body, name=name,
        grid_spec=pltpu.PrefetchScalarGridSpec(
            num_scalar_prefetch=1, grid=(per,),
            in_specs=[pl.BlockSpec((tr, cdim), lambda i, sel_ref: (i, 0)), part(0), part(1), part(2)],
            out_specs=pl.BlockSpec((tr, cdim), lambda i, sel_ref: (sel_ref[0] * per + i, 0))),
        out_shape=jax.ShapeDtypeStruct((2 * half, cdim), F32), compiler_params=_params("parallel"),
    )(sel, own, recv, recv, recv)


def _row(v):
    return v.reshape(1, -1)


_BR_A, _BR_B, _BR_C = (0, A_WIDTH), (A_WIDTH, POOL_WIDTH), (A_WIDTH + POOL_WIDTH, CONV_WIDTH)


def _tie(v, token):
    return v if token is None else v + token.tie


def _no_hook(point, after, ready=None):
    return None


def _layer_fwd(x, w, mod, hook=_no_hook):
    s = x.shape[0]
    mod3 = mod.reshape(6, 1, D)
    h = _modnorm_fwd(x, _row(w["g_mix_pre"]), (mod3, 0), (mod3, 1), "mix_pre_fwd")
    hook("pre", h)
    z = _mm(h, w["w_all"], name="mm_in")
    qkv = z[:, Z_QKV:Z_PC].astype(BF16)
    fl = z[:, Z_FL:Z_COLS]
    cum = _cumf_fwd(fl, w["b_f_pad"])
    fr = cum[:, :HEADS].T.reshape(HEADS, s // ATT_K, ATT_K)
    br_a, lse = _attn_fwd(qkv, fr)
    br_b, br_c = _poolconv_fwd(z, w["w_pool_bd"], _tie(_row(w["pool_scale"]), hook("attn", lse)), w["conv_w"])
    hook("pool", br_b)
    wbr = w["w_branch"]
    pa = _mm(br_a, wbr, b_rows=_BR_A, name="mm_br_a")
    pb = _mm(br_b, wbr, b_rows=_BR_B, name="mm_br_b")
    pc = _mm(br_c, wbr, b_rows=_BR_C, name="mm_br_c")
    merged = _merge_fwd(z, pa, pb, pc)
    y = _mm(merged, w["w_out"], name="mm_out")
    x1, h2 = _post_pre_fwd(x, y, _row(w["g_mix_post"]), (mod3, 2), _row(w["g_ff_pre"]), (mod3, 3), (mod3, 4), "mix_post_ff_pre_fwd")
    a, r = _mm(h2, w["w_ff1"], b_split=N_CHIPS, epilogue=_relu2_fwd, out_dtype=(F32, BF16), name="mm_ff1")
    y2 = _mm(r, w["w_ff2"], name="mm_ff2")
    x2 = _post_fwd(x1, y2, _tie(_row(w["g_ff_post"]), hook("ff_post", y2)), (mod3, 5), "ff_post_fwd")
    hook("end", x2)
    saved = dict(x=x, h=h, z=z, qkv=qkv, fl=fl, fr=fr, lse=lse, br_a=br_a, br_b=br_b, br_c=br_c, pa=pa, pb=pb, pc=pc,
                 merged=merged, y=y, x1=x1, h2=h2, a=a, r=r, y2=y2)
    return x2, saved


def _layer_bwd(dx2, sv, w, mod, hook=_no_hook):
    s = dx2.shape[0]
    mod3 = mod.reshape(6, 1, D)
    dy2, sum_ff_post = _post_bwd(dx2, sv["y2"], _row(w["g_ff_post"]), (mod3, 5), "ff_post_bwd")
    (da,) = _mm(dy2, w["w_ff2"], tb=True, epilogue=_relu2_bwd, extras=(sv["a"],), out_dtype=(BF16,), name="mm_ff2_dx")
    d_w_ff2 = _mm(sv["r"], dy2, ta=True, name="mm_ff2_dw")
    dh2 = _mm(da, w["w_ff1"], tb=True, b_split=N_CHIPS, name="mm_ff1_dx")
    d_w_ff1 = _mm(sv["h2"], da, ta=True, out_split=N_CHIPS, name="mm_ff1_dw")
    g_ff_pre = _tie(_row(w["g_ff_pre"]), hook("ff_pre", dh2, dict(w_ff1=d_w_ff1, w_ff2=d_w_ff2)))
    dx1, dy, sum_mid = _pre_post_bwd(dh2, sv["x1"], dx2, g_ff_pre, (mod3, 4), sv["y"], _row(w["g_mix_post"]), (mod3, 2), "ff_pre_mix_post_bwd")
    sum_ff_pre, sum_mix_post = sum_mid, sum_mid[3:]
    dmerged = _mm(dy, w["w_out"], tb=True, name="mm_out_dx")
    d_w_out = _mm(sv["merged"], dy, ta=True, name="mm_out_dw")
    dz, dpa, dpb, dpc = _merge_bwd(dmerged, sv["z"], sv["pa"], sv["pb"], sv["pc"])
    wbr = w["w_branch"]
    dbr_a = _mm(dpa, wbr, tb=True, b_rows=_BR_A, name="mm_br_a_dx")
    dbr_b = _mm(dpb, wbr, tb=True, b_rows=_BR_B, name="mm_br_b_dx")
    dbr_c = _mm(dpc, wbr, tb=True, b_rows=_BR_C, name="mm_br_c_dx")
    d_w_branch = jnp.concatenate([_mm(sv["br_a"], dpa, ta=True, name="mm_br_a_dw"), _mm(sv["br_b"], dpb, ta=True, name="mm_br_b_dw"),
                                  _mm(sv["br_c"], dpc, ta=True, name="mm_br_c_dw")], axis=0)

    dq, dk, dv, dfc, dfr = _attn_bwd(sv["qkv"], dbr_a, sv["br_a"], sv["lse"], sv["fr"])
    dcum = dfc + jnp.pad(dfr.reshape(HEADS, s).T, ((0, 0), (0, 128 - HEADS)))
    dfl, sum_bf = _cumf_bwd(dcum, sv["fl"], _tie(w["b_f_pad"], hook("cumf", dfc)))
    dpc_z, d_wbd, sum_ps, sum_cw = _poolconv_bwd(dbr_b, dbr_c, sv["z"], w["w_pool_bd"], _row(w["pool_scale"]), w["conv_w"])
    for at, part in ((Z_QKV, dq), (Z_QKV + A_WIDTH, dk), (Z_QKV + 2 * A_WIDTH, dv), (Z_PC, dpc_z), (Z_FL, dfl)):
        dz = lax.dynamic_update_slice(dz, part, (0, at))
    dh = _mm(dz, w["w_all"], tb=True, name="mm_in_dx")
    d_w_all = _mm(sv["h"], dz, ta=True, name="mm_in_dw")
    hook("mix_pre", dh)
    dx, sum_mix_pre = _modnorm_bwd(dh, sv["x"], dx1, _row(w["g_mix_pre"]), (mod3, 1), "mix_pre_bwd")

    dmod = jnp.stack([sum_mix_pre[0], sum_mix_pre[1], sum_mix_post[0], sum_ff_pre[0], sum_ff_pre[1], sum_ff_post[0]])
    d_w_in = d_w_all[None]
    d_w_pool = jnp.stack([d_wbd[64 * g:64 * g + 64, 64 * g:64 * g + 64] for g in range(4)])
    big = dict(w_in=d_w_in, w_branch=d_w_branch, w_out=d_w_out, w_ff1=d_w_ff1, w_ff2=d_w_ff2)
    small = dict(g_mix_pre=sum_mix_pre[2], g_mix_post=sum_mix_post[1], g_ff_pre=sum_ff_pre[2], g_ff_post=sum_ff_post[1],
                 b_f=sum_bf[0, :HEADS], w_pool=d_w_pool, pool_scale=sum_ps[0], conv_w=sum_cw[0:3])
    return dx, dmod, big, small


_QKV_END, _FL_END, _PC_END = 3 * A_WIDTH, 3 * A_WIDTH + HEADS, 3 * A_WIDTH + HEADS + POOL_WIDTH + 3 * CONV_WIDTH
_W_IN_GROUPS = ((_PC_END, IN_COLS, Z_GL), (0, _QKV_END, Z_QKV), (_FL_END, _PC_END, Z_PC), (_QKV_END, _FL_END, Z_FL))
_SHARD_COLS = IN_COLS // N_CHIPS


def _w_in_layout():
    out = []
    for p in range(N_CHIPS):
        pieces = []
        for lo, hi, at in _W_IN_GROUPS:
            a, b = max(lo, p * _SHARD_COLS), min(hi, (p + 1) * _SHARD_COLS)
            if a < b:
                pieces.append((at + a - lo, at + b - lo, a - p * _SHARD_COLS))
        pieces.sort()
        segs = []
        for z0, z1, _ in pieces:
            s, e = z0 // 128 * 128, -(-z1 // 128) * 128
            if segs and s <= segs[-1][1]:
                segs[-1] = (segs[-1][0], max(e, segs[-1][1]))
            else:
                segs.append((s, e))
        assert sum(e - s for s, e in segs) == Z_WINDOW
        out.append((pieces, segs))
    return out


Z_WINDOW = 1536


def _w_in_window(shard, p):
    pieces, segs = _w_in_layout()[p]
    cols = []
    for s, e in segs:
        at = s
        for z0, z1, src in pieces:
            if s <= z0 < e:
                if z0 > at:
                    cols.append(jnp.zeros((shard.shape[0], z0 - at), shard.dtype))
                cols.append(shard[:, src:src + z1 - z0])
                at = z1
        if e > at:
            cols.append(jnp.zeros((shard.shape[0], e - at), shard.dtype))
    return jnp.concatenate(cols, axis=1)


def _window_of(z_cols, p):
    return jnp.concatenate([z_cols[..., s:e] for s, e in _w_in_layout()[p][1]], axis=-1)


def _shard_of_window(window, p):
    pieces, segs = _w_in_layout()[p]
    starts, at = {}, 0
    for s, e in segs:
        starts[(s, e)] = at
        at += e - s
    cols = []
    for z0, z1, src in sorted(pieces, key=lambda piece: piece[2]):
        (s, e), = [seg for seg in segs if seg[0] <= z0 < seg[1]]
        cols.append(window[..., starts[(s, e)] + z0 - s:starts[(s, e)] + z1 - s])
    return jnp.concatenate(cols, axis=-1)


def _own_window(shard, chip):
    return lax.switch(chip, [lambda t, p=p: _w_in_window(t, p) for p in range(N_CHIPS)], shard)


def _w_all_from_windows(blocks):
    layout = _w_in_layout()
    bounds = sorted({edge for _, segs in layout for seg in segs for edge in seg})
    parts = []
    for lo, hi in zip(bounds[:-1], bounds[1:]):
        covering = []
        for p, (_, segs) in enumerate(layout):
            at = 0
            for s, e in segs:
                if s <= lo and hi <= e:
                    covering.append(blocks[p][:, at + lo - s:at + hi - s])
                at += e - s
        assert covering
        parts.append(covering[0] if len(covering) == 1 else covering[0] + covering[1])
    return jnp.concatenate(parts, axis=1)


def _full_layer_weights(w_in_blocks, w_branch, w_out, w_ff1, w_ff2, g_mix_pre, g_mix_post, g_ff_pre, g_ff_post, b_f, w_pool, pool_scale, conv_w):
    w_all = None if w_in_blocks is None else _w_all_from_windows(w_in_blocks)
    wbd = (w_pool[:, :, None, :] * jnp.eye(4, dtype=F32)[:, None, :, None]).reshape(POOL_WIDTH, POOL_WIDTH)
    return dict(w_all=w_all, w_branch=w_branch, w_out=w_out, w_ff1=w_ff1, w_ff2=w_ff2, g_mix_pre=g_mix_pre, g_mix_post=g_mix_post,
                g_ff_pre=g_ff_pre, g_ff_post=g_ff_post, b_f_pad=jnp.pad(b_f, (0, 128 - HEADS)).reshape(1, 128), w_pool_bd=wbd,
                pool_scale=pool_scale, conv_w=conv_w)


class _NoComm:
    def layer_weights(self, l):
        raise NotImplementedError

    def fwd_hook(self, l):
        return _no_hook

    def bwd_hook(self, l):
        return _no_hook

    def grads_ready(self, l, big):
        return None


class _Layers(_NoComm):
    def __init__(self, layers):
        self.layers = layers

    def layer_weights(self, l):
        return self.layers[l]


def _local_step(x, target, mods, comm):
    saved, weights = [], []
    act = x
    for l in range(DEPTH):
        weights.append(comm.layer_weights(l))
        act, sv = _layer_fwd(act, weights[l], mods[l], comm.fwd_hook(l))
        saved.append(sv)
    dact, sq = _loss_head(act, target)
    loss = sq[0, 0] * (0.5 / D)
    dmods, bigs, smalls = [None] * DEPTH, [None] * DEPTH, [None] * DEPTH
    token = None
    for l in reversed(range(DEPTH)):
        dact, dmods[l], bigs[l], smalls[l] = _layer_bwd(dact, saved[l], weights[l], _tie(mods[l], token), comm.bwd_hook(l))
        token = comm.grads_ready(l, bigs[l])
    return loss, dact, jnp.stack(dmods), bigs, smalls


_BIG = ("w_in", "w_branch", "w_out", "w_ff1", "w_ff2")


class _GatherJob:
    def __init__(self, tag, shards, after):
        self.tag, self.n = tag, len(shards)
        lands = [lax.empty((N_CHIPS,) + s.shape, s.dtype) for s in shards]
        self.state = _copies_start(list(shards) + lands, _plan_gather_ici, 4 * self.n, after, "gather_ici_start_" + tag)
        self.token = self.state[3]

    def pass_on(self, after):
        bufs = _copies_wait(self.state, _plan_gather_ici, after, "gather_ici_wait_" + self.tag)
        self.state = _copies_start(bufs[self.n:], _plan_gather_d2d, 3 * self.n, bufs[0], "gather_d2d_start_" + self.tag)
        self.token = self.state[3]
        return self.token

    def done(self, after):
        return _copies_wait(self.state, _plan_gather_d2d, after, "gather_d2d_wait_" + self.tag)


class _ReduceJob:
    def __init__(self, tag, names, grads, sel, after):
        self.tag, self.names, self.n, self.sel = tag, names, len(names), sel
        lands = [lax.empty((g.shape[0], g.shape[1] // 2, g.shape[2]), F32) for g in grads]
        self.state = _copies_start(list(grads) + lands, _plan_rs_sibling, self.n, after, "rs_sibling_start_" + tag)
        self.token = self.state[3]

    def _chip_sum(self, name, g, other):
        if g.shape[0] == N_CHIPS:
            return _chip_sum(g, other, self.sel, "rs_chip_sum_" + name)
        wire, total = _chip_sum(g, other, self.sel, "rs_chip_sum_" + name, blocked=False)
        own = lax.switch(self.sel[1], [lambda t, p=p: _window_of(t, p) for p in range(N_CHIPS)], total[0])
        return jnp.stack([_window_of(wire[0], p) for p in range(N_CHIPS)]), own

    def chip_sums(self, after):
        bufs = _copies_wait(self.state, _plan_rs_sibling, after, "rs_sibling_wait_" + self.tag)
        wires, self.owns = zip(*[self._chip_sum(name, bufs[i], bufs[self.n + i]) for i, name in enumerate(self.names)])
        lands = [lax.empty((3,) + w.shape[1:], BF16) for w in wires]
        self.state = _copies_start(list(wires) + lands, _plan_rs_chips, 3 * self.n, self.owns[0], "rs_chips_start_" + self.tag)
        self.token = self.state[3]
        return self.token

    def final_sums(self, after):
        bufs = _copies_wait(self.state, _plan_rs_chips, after, "rs_chips_wait_" + self.tag)
        sums = [_final_sum(self.owns[i], bufs[self.n + i], self.sel, "rs_final_" + name) for i, name in enumerate(self.names)]
        self.state = _copies_start(sums, _plan_rs_share, self.n, sums[0], "rs_share_start_" + self.tag)
        self.token = self.state[3]
        return self.token

    def done(self, after):
        return dict(zip(self.names, _copies_wait(self.state, _plan_rs_share, after, "rs_share_wait_" + self.tag)))


def _chip_blocks(g):
    return g if g.ndim == 3 else g.reshape(N_CHIPS, -1, g.shape[1])


class _StepComm(_NoComm):
    def __init__(self, big_weights, w_in0, sel, after):
        self.sel = sel
        self.small, self.grads, self.jobs = None, [dict() for _ in range(DEPTH)], {}
        self.jobs["in0"] = _GatherJob("in0", [w_in0], after)
        later = lax.optimization_barrier((tuple(big_weights), self.jobs["in0"].token.after))[0]
        self.jobs["rest0"] = _GatherJob("rest0", [w[0].astype(BF16) for w in later[1:]], self.jobs["in0"].token)
        layer1 = [w[1].astype(BF16) for w in later]
        self.jobs["all1"] = _GatherJob("all1", [_own_window(layer1[0], sel[1])] + layer1[1:], self.jobs["rest0"].token)

    def layer_weights(self, l):
        if l == 0:
            self.weights0 = _full_layer_weights(None, None, None, None, None, *self.small[0])
            return self.weights0
        g_in, g_br, g_out, g_f1, g_f2 = self.landed1
        return _full_layer_weights(g_in, g_br.reshape(D, D), g_out.reshape(D, D), g_f1, g_f2.reshape(D_FF, D), *self.small[1])

    def fwd_hook(self, l):
        if l != 0:
            return _no_hook

        def hook(point, after, ready=None):
            if point == "pre":
                job = self.jobs["in0"]
                started = after[:8, :128].astype(F32) + self.jobs["all1"].token.after
                self.weights0["w_all"] = _w_all_from_windows(job.done(job.pass_on(started))[0])
            if point == "attn":
                return self.jobs["rest0"].pass_on(after)
            if point == "ff_post":
                return self.jobs["all1"].pass_on(after)
            if point == "pool":
                g_br, g_out, g_f1, g_f2 = self.jobs["rest0"].done(after)
                self.weights0.update(w_branch=g_br.reshape(D, D), w_out=g_out.reshape(D, D), w_ff1=g_f1, w_ff2=g_f2.reshape(D_FF, D))
            if point == "end":
                self.landed1 = self.jobs["all1"].done(after)
            return None
        return hook

    def bwd_hook(self, l):
        if l != 0:
            return _no_hook

        def hook(point, after, ready=None):
            jobs = self.jobs
            if point == "ff_pre":
                token = jobs["rs1"].chip_sums(after)
                jobs["rs0_ff"] = _ReduceJob("0_ff", ("w_ff1", "w_ff2"), [_chip_blocks(ready[n]) for n in ("w_ff1", "w_ff2")], self.sel, token)
                return jobs["rs0_ff"].token
            if point == "cumf":
                return jobs["rs0_ff"].chip_sums(jobs["rs1"].final_sums(after))
            self.grads[1] = jobs["rs1"].done(after)
            return None
        return hook

    def grads_ready(self, l, big):
        if l == 1:
            self.jobs["rs1"] = _ReduceJob("1", _BIG, [_chip_blocks(big[n]) for n in _BIG], self.sel, self.sel)
            return self.jobs["rs1"].token
        names = ("w_in", "w_branch", "w_out")
        self.jobs["rs0_mix"] = _ReduceJob("0_mix", names, [_chip_blocks(big[n]) for n in names], self.sel, self.sel)
        return self.jobs["rs0_mix"].token

    def finish_sums(self, after):
        jobs = self.jobs
        token = jobs["rs0_mix"].chip_sums(after)
        return jobs["rs0_ff"].final_sums(token)

    def finish_ff(self, after):
        self.grads[0].update(self.jobs["rs0_ff"].done(after))

    def finish_mix(self, after):
        job = self.jobs["rs0_mix"]
        self.grads[0].update(job.done(job.final_sums(after)))


_SMALL = ("g_mix_pre", "g_mix_post", "g_ff_pre", "g_ff_post", "b_f", "w_pool", "pool_scale", "conv_w")


def _w_in_view(t):
    return t.reshape(DEPTH, D // 128, 128, _SHARD_COLS).transpose(3, 1, 0, 2).reshape(_SHARD_COLS * (D // 128) * DEPTH, 128)


def _w_in_unview(t):
    return t.reshape(_SHARD_COLS, D // 128, DEPTH, 128).transpose(2, 1, 3, 0).reshape(DEPTH, D, _SHARD_COLS)


def _pack(parts, rows=8):
    flat = jnp.concatenate([p.reshape(-1) for p in parts])
    width = -(-flat.shape[0] // (rows * 128)) * 128
    return jnp.pad(flat, (0, rows * width - flat.shape[0])).reshape(rows, width)


def _unpack(packed, like):
    flat = packed.reshape(-1)
    out, at = [], 0
    for ref in like:
        out.append(flat[at:at + ref.size].reshape(ref.shape))
        at += ref.size
    return out


def kernel(x, c, w_ada, b_ada, g_mix_pre, g_mix_post, g_ff_pre, g_ff_post, w_in, b_f, w_pool, pool_scale, conv_w, w_branch, w_out, w_ff1, w_ff2, loss_target, m_w_ada, m_b_ada, m_g_mix_pre, m_g_mix_post, m_g_ff_pre, m_g_ff_post, m_w_in, m_b_f, m_w_pool, m_pool_scale, m_conv_w, m_w_branch, m_w_out, m_w_ff1, m_w_ff2, v_w_ada, v_b_ada, v_g_mix_pre, v_g_mix_post, v_g_ff_pre, v_g_ff_post, v_w_in, v_b_f, v_w_pool, v_pool_scale, v_conv_w, v_w_branch, v_w_out, v_w_ff1, v_w_ff2):
    xi, yi, ci = lax.axis_index("x"), lax.axis_index("y"), lax.axis_index("c")
    chip = 2 * xi + yi
    dev = 2 * chip + ci
    n_ada = w_ada.shape[2]

    first = jnp.zeros((8, D + 384), F32).at[0, :D].set(c[0]).at[0, D:].set(conv_w.reshape(-1))
    w_in0 = _own_window(w_in[0].astype(BF16), chip)
    got = _allgather8(first, "gather_cond", after=(w_in0,)).reshape(N_DEV, 8, D + 384)[:, 0]
    c_all = got[:, :D]
    conv_full = got[0::2, D:].reshape(N_CHIPS, DEPTH, 3, CONV_WIDTH // N_CHIPS).transpose(1, 2, 0, 3).reshape(DEPTH, 3, CONV_WIDTH)

    b_loc = lax.dynamic_slice_in_dim(b_ada, chip * n_ada, n_ada, axis=1).reshape(DEPTH, 1, n_ada)
    mod_cols, silu_c = _ada_fwd(c_all, w_ada, b_loc)
    got = _allgather8(mod_cols.reshape(DEPTH * N_DEV, n_ada), "gather_mod").reshape(N_DEV, DEPTH, N_DEV, n_ada)[0::2]
    mod_all = got.transpose(1, 2, 0, 3).reshape(DEPTH, N_DEV, 6, D)
    mods = lax.dynamic_index_in_dim(mod_all, dev, axis=1, keepdims=False)

    comm = _StepComm((w_in, w_branch, w_out, w_ff1, w_ff2), w_in0, jnp.stack([ci, chip]).astype(jnp.int32), mods)
    comm.small = [(g_mix_pre[l], g_mix_post[l], g_ff_pre[l], g_ff_post[l], b_f[l], w_pool[l], pool_scale[l], conv_full[l]) for l in range(DEPTH)]
    loss_part, grad_x, dmods, bigs, smalls = _local_step(x[0], loss_target[0], mods, comm)

    small_parts = [smalls[l][name] for name in _SMALL for l in range(DEPTH)] + [loss_part.reshape(1)]
    packed = _tie(_pack([dmods] + small_parts), comm.jobs["rs0_mix"].token)
    gathered = _allgather8(packed, "gather_small")
    dmod_all = gathered.reshape(N_DEV, -1)[:, :dmods.size].reshape(N_DEV, DEPTH, 6 * D)
    summed = _unpack(_sum_devices(gathered), [dmods] + small_parts)
    grad_b_ada = summed[0].reshape(DEPTH, 6 * D)
    loss = summed[-1][0]
    small_grads = {name: jnp.stack(summed[1 + 2 * i:3 + 2 * i]) for i, name in enumerate(_SMALL)}
    small_grads["conv_w"] = lax.dynamic_slice_in_dim(small_grads["conv_w"], chip * (CONV_WIDTH // N_CHIPS), CONV_WIDTH // N_CHIPS, axis=2)

    dmod_loc = lax.dynamic_slice_in_dim(dmod_all.transpose(1, 0, 2), chip * n_ada, n_ada, axis=2)
    tail_token = comm.finish_sums(grad_b_ada)
    silu_pad = _tie(jnp.pad(silu_c, ((0, 128 - N_DEV), (0, 0))), tail_token)
    dmod_pad = jnp.pad(dmod_loc.transpose(1, 0, 2).reshape(N_DEV, DEPTH * n_ada), ((0, 128 - N_DEV), (0, 0)))
    grad_w_ada = _mm(silu_pad, dmod_pad, ta=True, out_split=DEPTH, name="mm_ada_dw")

    grads = dict(w_ada=grad_w_ada, b_ada=grad_b_ada, **small_grads)
    weights = dict(w_ada=w_ada, b_ada=b_ada, g_mix_pre=g_mix_pre, g_mix_post=g_mix_post, g_ff_pre=g_ff_pre, g_ff_post=g_ff_post, w_in=w_in,
                   b_f=b_f, w_pool=w_pool, pool_scale=pool_scale, conv_w=conv_w, w_branch=w_branch, w_out=w_out, w_ff1=w_ff1, w_ff2=w_ff2)
    m_in = dict(w_ada=m_w_ada, b_ada=m_b_ada, g_mix_pre=m_g_mix_pre, g_mix_post=m_g_mix_post, g_ff_pre=m_g_ff_pre, g_ff_post=m_g_ff_post,
                w_in=m_w_in, b_f=m_b_f, w_pool=m_w_pool, pool_scale=m_pool_scale, conv_w=m_conv_w, w_branch=m_w_branch, w_out=m_w_out,
                w_ff1=m_w_ff1, w_ff2=m_w_ff2)
    v_in = dict(w_ada=v_w_ada, b_ada=v_b_ada, g_mix_pre=v_g_mix_pre, g_mix_post=v_g_mix_post, g_ff_pre=v_g_ff_pre, g_ff_post=v_g_ff_post,
                w_in=v_w_in, b_f=v_b_f, w_pool=v_w_pool, pool_scale=v_pool_scale, conv_w=v_conv_w, w_branch=v_w_branch, w_out=v_w_out,
                w_ff1=v_w_ff1, w_ff2=v_w_ff2)
    order = ("w_ada", "b_ada", "g_mix_pre", "g_mix_post", "g_ff_pre", "g_ff_post", "w_in", "b_f", "w_pool", "pool_scale", "conv_w",
             "w_branch", "w_out", "w_ff1", "w_ff2")
    delta, new_m, new_v = {}, {}, {}
    tiny = ("b_ada",) + _SMALL
    tiny_g = [_tie(grads[tiny[0]], tail_token)] + [grads[name] for name in tiny[1:]]
    res = _adamw_many([weights[name] for name in tiny], tiny_g, [m_in[name] for name in tiny], [v_in[name] for name in tiny], "adamw_small")
    for out, vals in zip((delta, new_m, new_v), res):
        out.update(zip(tiny, vals))
    delta["w_ada"], new_m["w_ada"], new_v["w_ada"] = _adamw(w_ada, grad_w_ada, m_w_ada, v_w_ada, "adamw_w_ada")
    comm.finish_ff(delta["w_ada"][0, :8, :128] + delta["b_ada"][0, :128])
    for name in ("w_ff1", "w_ff2", "w_in", "w_branch", "w_out"):
        if name == "w_in":
            comm.finish_mix(delta["w_ff2"][0, :8, :128])
        g_layers = [comm.grads[l][name] for l in range(DEPTH)]
        if name == "w_in":
            g_own = lax.switch(chip, [lambda t, p=p: _shard_of_window(t, p) for p in range(N_CHIPS)], jnp.stack(g_layers))
            g_view = lax.optimization_barrier(_w_in_view(g_own))
            res = _adamw(_w_in_view(w_in), g_view, _w_in_view(m_w_in), _w_in_view(v_w_in), "adamw_w_in")
            grads[name], delta[name], new_m[name], new_v[name] = [_w_in_unview(t) for t in (g_view, *res)]
        else:
            delta[name], new_m[name], new_v[name], grads[name] = _adamw_layers(weights[name], g_layers, m_in[name], v_in[name], "adamw_" + name)

    return (loss, grad_x[None], *[grads[n] for n in order], *[delta[n] for n in order], *[new_m[n] for n in order],
            *[new_v[n] for n in order])
```

```python
from typing import NamedTuple

import jax
import jax.numpy as jnp
from jax import lax
from jax.experimental import pallas as pl
from jax.experimental.pallas import tpu as pltpu

F32 = jnp.float32
BF16 = jnp.bfloat16
MESH = pl.DeviceIdType.MESH

D = 1024
DEPTH = 2
HEADS = 8
HEAD_DIM = 64
A_WIDTH = 512
POOL_WIDTH = 256
CONV_WIDTH = 256
D_FF = 4096
IN_COLS = 5640
Z_GL, Z_QKV, Z_PC, Z_FL, Z_COLS = 0, 3072, 4608, 5632, 5760
RMS_EPS = 1e-6
NEG_INF = -1e30
ROW_TILE = 512
EW_ROWS = 256
N_CHIPS = 4
N_DEV = 8
V7X_VMEM_LIMIT = 48 * 1024 * 1024

ADAM_LR = 0.001
ADAM_B1 = 0.9
ADAM_B2 = 0.999
ADAM_EPS = 1e-08
ADAM_WD = 0.01
ADAM_STEP = 10

_HBM = pl.BlockSpec(memory_space=pltpu.HBM)


def _params(*sem):
    return pltpu.CompilerParams(dimension_semantics=sem, vmem_limit_bytes=V7X_VMEM_LIMIT)


def _pick(dim, cands):
    for cand in cands:
        if dim % cand == 0:
            return cand
    return dim


MM_TILE_BUDGET = 39 * 1024 * 1024


def _mm_tiles(m, n, k, k_unit, tn, a_size, b_size, out_size):
    for tk in (k_unit, 2048, 1152, 1024, 640, 512, 256, 128):
        if k_unit % tk:
            continue
        for tm in (2048, 1024, 512, 256, 128):
            if m % tm or ((m // tm) * (n // tn) < 2 and tm > 512):
                continue
            need = 2 * (tm * tk * a_size + tk * tn * b_size + tm * tn * out_size) + (0 if tk == k else 4 * tm * tn)
            if need <= MM_TILE_BUDGET and (tk == k_unit or tm >= 512):
                return tm, tk
    return 128, 128


def _mm(a, b, *, ta=False, tb=False, b_rows=None, b_split=1, out_split=1, out_dtype=F32, epilogue=None, extras=(), name):
    (k, m) = a.shape if ta else a.shape[::-1]
    b_row0, b_rows = (0, b.shape[-2]) if b_rows is None else b_rows
    b_cols = b.shape[-1] * b_split
    (n, k2) = (b_rows, b_cols) if tb else (b_cols, b_rows)
    assert k == k2, (a.shape, b.shape, ta, tb)
    n_unit = n // (out_split * (1 if tb else b_split))
    k_unit = k // (b_split if tb else 1)
    tn = _pick(n_unit, (1024, 1152, 768, 640, 512, 256, 128))
    tm, tk = _mm_tiles(m, n, k, k_unit, tn, a.dtype.itemsize, b.dtype.itemsize,
                       sum(jnp.dtype(dt).itemsize for dt in out_dtype) + 4 * len(extras) if epilogue else jnp.dtype(out_dtype).itemsize)
    nk = k // tk
    dims = (((0 if ta else 1,), (1 if tb else 0,)), ((), ()))

    def dot(a_ref, b_ref):
        b_val = b_ref[0] if b_split > 1 else b_ref[...]
        return lax.dot_general(a_ref[...].astype(BF16), b_val.astype(BF16), dims, preferred_element_type=F32)

    n_extra = len(extras)
    assert epilogue is None or out_split == 1

    def put(refs, val):
        if epilogue is not None:
            for o_ref, res in zip(refs[n_extra:], epilogue(val, *[r[...] for r in refs[:n_extra]])):
                o_ref[...] = res.astype(o_ref.dtype)
        elif out_split > 1:
            refs[0][0] = val.astype(refs[0].dtype)
        else:
            refs[0][...] = val.astype(refs[0].dtype)

    def body_single(a_ref, b_ref, *refs):
        put(refs, dot(a_ref, b_ref))

    def body_acc(a_ref, b_ref, *refs):
        kk = pl.program_id(2)
        acc_ref = refs[-1]

        @pl.when(kk == 0)
        def _():
            acc_ref[...] = jnp.zeros_like(acc_ref)

        acc_ref[...] += dot(a_ref, b_ref)

        @pl.when(kk == nk - 1)
        def _():
            put(refs[:-1], acc_ref[...])

    a_spec = pl.BlockSpec((tk, tm), lambda i, j, kk: (kk, i)) if ta else pl.BlockSpec((tm, tk), lambda i, j, kk: (i, kk))
    if b_split == 1:
        off = b_row0 // (tn if tb else tk)
        assert off * (tn if tb else tk) == b_row0
        b_spec = pl.BlockSpec((tn, tk), lambda i, j, kk: (j + off, kk)) if tb else pl.BlockSpec((tk, tn), lambda i, j, kk: (kk + off, j))
    elif tb:
        per = k_unit // tk
        b_spec = pl.BlockSpec((1, tn, tk), lambda i, j, kk: (kk // per, j, kk % per))
    else:
        per = n // b_split // tn
        b_spec = pl.BlockSpec((1, tk, tn), lambda i, j, kk: (j // per, kk, j % per))
    if out_split == 1:
        o_spec = pl.BlockSpec((tm, tn), lambda i, j, kk: (i, j))
        o_shape = None if epilogue is not None else jax.ShapeDtypeStruct((m, n), out_dtype)
    else:
        per_o = n // out_split // tn
        o_spec = pl.BlockSpec((1, tm, tn), lambda i, j, kk: (j // per_o, i, j % per_o))
        o_shape = jax.ShapeDtypeStruct((out_split, m, n // out_split), out_dtype)
    if epilogue is not None:
        o_shape = [jax.ShapeDtypeStruct((m, n), dt) for dt in out_dtype]
        o_spec = [o_spec] * len(out_dtype)
    return pl.pallas_call(
        body_single if nk == 1 else body_acc, name=name, grid=(m // tm, n // tn, nk),
        in_specs=[a_spec, b_spec] + [pl.BlockSpec((tm, tn), lambda i, j, kk: (i, j))] * n_extra, out_specs=o_spec, out_shape=o_shape,
        scratch_shapes=[] if nk == 1 else [pltpu.VMEM((tm, tn), F32)],
        compiler_params=_params("parallel", "parallel", "arbitrary"),
    )(a, b, *extras)


def _ew(fn, ins, out_dtypes, name, tc=None):
    shape = ins[0].shape
    lead, (rows, cols) = shape[:-2], shape[-2:]
    tc = cols if tc is None else tc
    if tc > 1024:
        tr = _pick(rows, (EW_ROWS, 128, 8))
    elif tc > 128:
        tr = _pick(rows, (2 * EW_ROWS, EW_ROWS, 128, 8))
    else:
        tr = _pick(rows, (4096, 2256, 2048, 1024, EW_ROWS, 8))
    n_in = len(ins)

    def body(*refs):
        res = fn(*[r[...] for r in refs[:n_in]])
        for o_ref, val in zip(refs[n_in:], res):
            o_ref[...] = val.astype(o_ref.dtype)

    if lead:
        spec = pl.BlockSpec((None, tr, tc), lambda l, i, j: (l, i, j))
    else:
        spec = pl.BlockSpec((tr, tc), lambda i, j: (i, j))
    return pl.pallas_call(
        body, name=name, grid=lead + (rows // tr, cols // tc),
        in_specs=[spec] * n_in, out_specs=[spec] * len(out_dtypes),
        out_shape=[jax.ShapeDtypeStruct(shape, dt) for dt in out_dtypes],
        compiler_params=_params(*(["parallel"] * (len(lead) + 2))),
    )(*ins)


def _relu2_fwd(a):
    r = jnp.maximum(a, 0.0)
    return a, r * r


def _relu2_bwd(dr, a):
    return (dr * (2.0 * jnp.maximum(a, 0.0)),)


def _adamw_math(w, g, m, v):
    m = ADAM_B1 * m + (1.0 - ADAM_B1) * g
    v = ADAM_B2 * v + (1.0 - ADAM_B2) * (g * g)
    m_hat = m / (1.0 - ADAM_B1 ** ADAM_STEP)
    v_hat = v / (1.0 - ADAM_B2 ** ADAM_STEP)
    delta = -ADAM_LR * (m_hat / (jnp.sqrt(v_hat) + ADAM_EPS) + ADAM_WD * w)
    return delta, m, v


def _adamw(w, g, m, v, name):
    return _ew(_adamw_math, [w, g, m, v], [F32, F32, F32], name)


def _adamw_layers(w, g_layers, m, v, name):
    depth, rows, cols = w.shape
    tr = _pick(rows, (2 * EW_ROWS, EW_ROWS, 128, 8)) if cols <= 1024 else _pick(rows, (EW_ROWS, 128, 8))

    def body(w_ref, *refs):
        g_refs, (m_ref, v_ref, d_ref, mo_ref, vo_ref, go_ref) = refs[:depth], refs[depth:]
        layer = pl.program_id(0)
        g = g_refs[0][...]
        for l in range(1, depth):
            g = jnp.where(layer == l, g_refs[l][...], g)
        d_ref[...], mo_ref[...], vo_ref[...] = _adamw_math(w_ref[...], g, m_ref[...], v_ref[...])
        go_ref[...] = g

    spec = pl.BlockSpec((None, tr, cols), lambda l, i: (l, i, 0))
    g_specs = [pl.BlockSpec((tr, cols), lambda l, i, k=k: (jnp.where(l == k, i, 0), 0)) for k in range(depth)]
    return pl.pallas_call(
        body, name=name, grid=(depth, rows // tr),
        in_specs=[spec] + g_specs + [spec, spec], out_specs=[spec] * 4,
        out_shape=[jax.ShapeDtypeStruct(w.shape, F32)] * 4, compiler_params=_params("arbitrary", "arbitrary"),
    )(w, *g_layers, m, v)


def _adamw_many(ws, gs, ms, vs, name):
    n = len(ws)

    def body(*refs):
        for i in range(n):
            res = _adamw_math(*[refs[k * n + i][...] for k in range(4)])
            for k in range(3):
                refs[(4 + k) * n + i][...] = res[k]

    outs = pl.pallas_call(
        body, name=name, out_shape=[jax.ShapeDtypeStruct(w.shape, F32) for w in ws] * 3,
        compiler_params=pltpu.CompilerParams(vmem_limit_bytes=V7X_VMEM_LIMIT),
    )(*ws, *gs, *ms, *vs)
    return outs[:n], outs[n:2 * n], outs[2 * n:]


def _row_spec(cols, block=0):
    return pl.BlockSpec((ROW_TILE, cols), lambda i, block=block: (i, block))


def _vec_spec(cols):
    return pl.BlockSpec((1, cols), lambda i: (0, 0))


def _vec_args(*vecs):
    arrays = [v[0] if isinstance(v, tuple) else v for v in vecs]
    specs = [pl.BlockSpec((None, 1, D), lambda i, row=v[1]: (row, 0, 0)) if isinstance(v, tuple) else _vec_spec(D) for v in vecs]
    return arrays, specs


def _sum_spec(cols):
    return pl.BlockSpec((8, cols), lambda i: (0, 0))


def _rstd(x):
    return lax.rsqrt(jnp.mean(x * x, axis=-1, keepdims=True) + RMS_EPS)


def _modnorm_fwd(x, g, shift, scale, name):
    s = x.shape[0]

    def body(x_ref, g_ref, sh_ref, sc_ref, h_ref):
        xv = x_ref[...]
        n = xv * _rstd(xv)
        h_ref[...] = ((n * g_ref[...]) * (1.0 + sc_ref[...]) + sh_ref[...]).astype(BF16)

    vecs, vec_specs = _vec_args(g, shift, scale)
    return pl.pallas_call(
        body, name=name, grid=(s // ROW_TILE,),
        in_specs=[_row_spec(D)] + vec_specs, out_specs=_row_spec(D),
        out_shape=jax.ShapeDtypeStruct((s, D), BF16), compiler_params=_params("parallel"),
    )(x, *vecs)


def _post_fwd(x, y, g, gate, name):
    s = x.shape[0]

    def body(x_ref, y_ref, g_ref, gate_ref, o_ref):
        yv = y_ref[...]
        o_ref[...] = x_ref[...] + gate_ref[...] * ((yv * _rstd(yv)) * g_ref[...])

    vecs, vec_specs = _vec_args(g, gate)
    return pl.pallas_call(
        body, name=name, grid=(s // ROW_TILE,),
        in_specs=[_row_spec(D), _row_spec(D)] + vec_specs, out_specs=_row_spec(D),
        out_shape=jax.ShapeDtypeStruct((s, D), F32), compiler_params=_params("parallel"),
    )(x, y, *vecs)


def _post_bwd(dxo, y, g, gate, name):
    s = dxo.shape[0]

    def body(d_ref, y_ref, g_ref, gate_ref, dy_ref, sum_ref):
        @pl.when(pl.program_id(0) == 0)
        def _():
            sum_ref[...] = jnp.zeros_like(sum_ref)

        dv, yv = d_ref[...], y_ref[...]
        r = _rstd(yv)
        n = yv * r
        sum_ref[0:1, :] += jnp.sum(dv * (n * g_ref[...]), axis=0, keepdims=True)
        sum_ref[1:2, :] += jnp.sum((dv * gate_ref[...]) * n, axis=0, keepdims=True)
        dn = (dv * gate_ref[...]) * g_ref[...]
        dy_ref[...] = (r * (dn - n * jnp.mean(dn * n, axis=-1, keepdims=True))).astype(BF16)

    vecs, vec_specs = _vec_args(g, gate)
    return pl.pallas_call(
        body, name=name, grid=(s // ROW_TILE,),
        in_specs=[_row_spec(D), _row_spec(D)] + vec_specs,
        out_specs=[_row_spec(D), _sum_spec(D)],
        out_shape=[jax.ShapeDtypeStruct((s, D), BF16), jax.ShapeDtypeStruct((8, D), F32)],
        compiler_params=_params("arbitrary"),
    )(dxo, y, *vecs)


def _modnorm_bwd(dh, x, dxo, g, scale, name):
    s = dh.shape[0]

    def body(dh_ref, x_ref, d_ref, g_ref, sc_ref, dx_ref, sum_ref):
        @pl.when(pl.program_id(0) == 0)
        def _():
            sum_ref[...] = jnp.zeros_like(sum_ref)

        dhv, xv = dh_ref[...], x_ref[...]
        r = _rstd(xv)
        n = xv * r
        one_sc = 1.0 + sc_ref[...]
        sum_ref[0:1, :] += jnp.sum(dhv, axis=0, keepdims=True)
        sum_ref[1:2, :] += jnp.sum(dhv * (n * g_ref[...]), axis=0, keepdims=True)
        sum_ref[2:3, :] += jnp.sum((dhv * one_sc) * n, axis=0, keepdims=True)
        dn = (dhv * one_sc) * g_ref[...]
        dx_ref[...] = d_ref[...] + r * (dn - n * jnp.mean(dn * n, axis=-1, keepdims=True))

    vecs, vec_specs = _vec_args(g, scale)
    return pl.pallas_call(
        body, name=name, grid=(s // ROW_TILE,),
        in_specs=[_row_spec(D), _row_spec(D), _row_spec(D)] + vec_specs,
        out_specs=[_row_spec(D), _sum_spec(D)],
        out_shape=[jax.ShapeDtypeStruct((s, D), F32), jax.ShapeDtypeStruct((8, D), F32)],
        compiler_params=_params("arbitrary"),
    )(dh, x, dxo, *vecs)


def _post_pre_fwd(x, y, g_post, gate, g_pre, shift, scale, name):
    s = x.shape[0]

    def body(x_ref, y_ref, gp_ref, gate_ref, g_ref, sh_ref, sc_ref, o_ref, h_ref):
        yv = y_ref[...]
        xo = x_ref[...] + gate_ref[...] * ((yv * _rstd(yv)) * gp_ref[...])
        o_ref[...] = xo
        h_ref[...] = (((xo * _rstd(xo)) * g_ref[...]) * (1.0 + sc_ref[...]) + sh_ref[...]).astype(BF16)

    vecs, vec_specs = _vec_args(g_post, gate, g_pre, shift, scale)
    return pl.pallas_call(
        body, name=name, grid=(s // ROW_TILE,),
        in_specs=[_row_spec(D), _row_spec(D)] + vec_specs, out_specs=[_row_spec(D), _row_spec(D)],
        out_shape=[jax.ShapeDtypeStruct((s, D), F32), jax.ShapeDtypeStruct((s, D), BF16)], compiler_params=_params("parallel"),
    )(x, y, *vecs)


def _pre_post_bwd(dh, x, dxo, g_pre, scale, y, g_post, gate, name):
    s = dh.shape[0]

    def body(dh_ref, x_ref, d_ref, y_ref, g_ref, sc_ref, gp_ref, gate_ref, dx_ref, dy_ref, sum_ref):
        @pl.when(pl.program_id(0) == 0)
        def _():
            sum_ref[...] = jnp.zeros_like(sum_ref)

        dhv, xv = dh_ref[...], x_ref[...]
        r = _rstd(xv)
        n = xv * r
        one_sc = 1.0 + sc_ref[...]
        sum_ref[0:1, :] += jnp.sum(dhv, axis=0, keepdims=True)
        sum_ref[1:2, :] += jnp.sum(dhv * (n * g_ref[...]), axis=0, keepdims=True)
        sum_ref[2:3, :] += jnp.sum((dhv * one_sc) * n, axis=0, keepdims=True)
        dn = (dhv * one_sc) * g_ref[...]
        dv = d_ref[...] + r * (dn - n * jnp.mean(dn * n, axis=-1, keepdims=True))
        dx_ref[...] = dv

        yv = y_ref[...]
        ry = _rstd(yv)
        ny = yv * ry
        sum_ref[3:4, :] += jnp.sum(dv * (ny * gp_ref[...]), axis=0, keepdims=True)
        sum_ref[4:5, :] += jnp.sum((dv * gate_ref[...]) * ny, axis=0, keepdims=True)
        dny = (dv * gate_ref[...]) * gp_ref[...]
        dy_ref[...] = (ry * (dny - ny * jnp.mean(dny * ny, axis=-1, keepdims=True))).astype(BF16)

    vecs, vec_specs = _vec_args(g_pre, scale, g_post, gate)
    return pl.pallas_call(
        body, name=name, grid=(s // ROW_TILE,),
        in_specs=[_row_spec(D)] * 4 + vec_specs,
        out_specs=[_row_spec(D), _row_spec(D), _sum_spec(D)],
        out_shape=[jax.ShapeDtypeStruct((s, D), F32), jax.ShapeDtypeStruct((s, D), BF16), jax.ShapeDtypeStruct((8, D), F32)],
        compiler_params=_params("arbitrary"),
    )(dh, x, dxo, y, *vecs)


def _loss_head(y, target):
    s = y.shape[0]

    def body(y_ref, t_ref, dy_ref, sum_ref):
        @pl.when(pl.program_id(0) == 0)
        def _():
            sum_ref[...] = jnp.zeros_like(sum_ref)

        err = y_ref[...] - t_ref[...]
        dy_ref[...] = err * (1.0 / D)
        sum_ref[...] += jnp.sum(err * err)

    return pl.pallas_call(
        body, name="loss_head", grid=(s // ROW_TILE,),
        in_specs=[_row_spec(D), _row_spec(D)],
        out_specs=[_row_spec(D), pl.BlockSpec((8, 128), lambda i: (0, 0))],
        out_shape=[jax.ShapeDtypeStruct((s, D), F32), jax.ShapeDtypeStruct((8, 128), F32)],
        compiler_params=_params("arbitrary"),
    )(y, target)


def _merge_fwd(z, pa, pb, pc):
    s = z.shape[0]

    def body(g0_ref, g1_ref, g2_ref, pa_ref, pb_ref, pc_ref, o_ref):
        o_ref[...] = (jax.nn.sigmoid(g0_ref[...]) * pa_ref[...] + jax.nn.sigmoid(g1_ref[...]) * pb_ref[...]
                      + jax.nn.sigmoid(g2_ref[...]) * pc_ref[...]).astype(BF16)

    return pl.pallas_call(
        body, name="merge_fwd", grid=(s // ROW_TILE,),
        in_specs=[_row_spec(D, 0), _row_spec(D, 1), _row_spec(D, 2), _row_spec(D), _row_spec(D), _row_spec(D)],
        out_specs=_row_spec(D), out_shape=jax.ShapeDtypeStruct((s, D), BF16),
        compiler_params=_params("parallel"),
    )(z, z, z, pa, pb, pc)


def _merge_bwd(dm, z, pa, pb, pc):
    s = z.shape[0]

    def body(dm_ref, g0_ref, g1_ref, g2_ref, pa_ref, pb_ref, pc_ref, dgl_ref, da_ref, db_ref, dc_ref):
        dmv = dm_ref[...]
        for i, (g_ref, p_ref, d_ref) in enumerate(((g0_ref, pa_ref, da_ref), (g1_ref, pb_ref, db_ref), (g2_ref, pc_ref, dc_ref))):
            gate = jax.nn.sigmoid(g_ref[...])
            dgl_ref[:, i * D:(i + 1) * D] = ((dmv * p_ref[...]) * (gate * (1.0 - gate))).astype(BF16)
            d_ref[...] = (dmv * gate).astype(BF16)

    return pl.pallas_call(
        body, name="merge_bwd", grid=(s // ROW_TILE,),
        in_specs=[_row_spec(D), _row_spec(D, 0), _row_spec(D, 1), _row_spec(D, 2), _row_spec(D), _row_spec(D), _row_spec(D)],
        out_specs=[_row_spec(3 * D), _row_spec(D), _row_spec(D), _row_spec(D)],
        out_shape=[jax.ShapeDtypeStruct((s, Z_COLS), BF16)] + [jax.ShapeDtypeStruct((s, D), BF16)] * 3,
        compiler_params=_params("parallel"),
    )(dm, z, z, z, pa, pb, pc)


def _shift_down(v, n):
    row = lax.broadcasted_iota(jnp.int32, v.shape, 0)
    return jnp.where(row >= n, pltpu.roll(v, n, axis=0), 0.0)


def _shift_up(v, n):
    s = v.shape[0]
    row = lax.broadcasted_iota(jnp.int32, v.shape, 0)
    return jnp.where(row < s - n, pltpu.roll(v, s - n, axis=0), 0.0)


def _log_sigmoid(v):
    return jnp.minimum(v, 0.0) - jnp.log1p(jnp.exp(-jnp.abs(v)))


def _cumf_fwd(fl, bias):
    s = fl.shape[0]

    def body(fl_ref, b_ref, o_ref):
        acc = _log_sigmoid(fl_ref[...] + b_ref[...])
        step = 1
        while step < s:
            acc = acc + _shift_down(acc, step)
            step *= 2
        o_ref[...] = acc

    return pl.pallas_call(body, name="cumf_fwd", out_shape=jax.ShapeDtypeStruct((s, 128), F32),
                          compiler_params=pltpu.CompilerParams(vmem_limit_bytes=V7X_VMEM_LIMIT))(fl, bias)


def _cumf_bwd(dcum, fl, bias):
    s = fl.shape[0]

    def body(d_ref, fl_ref, b_ref, dfl_ref, db_ref):
        acc = d_ref[...]
        step = 1
        while step < s:
            acc = acc + _shift_up(acc, step)
            step *= 2
        dfl = acc * jax.nn.sigmoid(-(fl_ref[...] + b_ref[...]))
        dfl_ref[...] = dfl.astype(BF16)
        db_ref[...] = jnp.broadcast_to(jnp.sum(dfl, axis=0, keepdims=True), (8, 128))

    return pl.pallas_call(
        body, name="cumf_bwd",
        out_shape=[jax.ShapeDtypeStruct((s, 128), BF16), jax.ShapeDtypeStruct((8, 128), F32)],
        compiler_params=pltpu.CompilerParams(vmem_limit_bytes=V7X_VMEM_LIMIT))(dcum, fl, bias)


def _pool_windows(v, shift):
    s2 = v + shift(v, 1)
    s4 = s2 + shift(s2, 2)
    s8 = s4 + shift(s4, 4)
    s16 = s8 + shift(s8, 8)
    group = lax.broadcasted_iota(jnp.int32, v.shape, 1) // 64
    return jnp.where(group == 0, s2, jnp.where(group == 1, s4, jnp.where(group == 2, s8, s16)))


def _pool_count(shape):
    group = lax.broadcasted_iota(jnp.int32, shape, 1) // 64
    window = jnp.where(group == 0, 2.0, jnp.where(group == 1, 4.0, jnp.where(group == 2, 8.0, 16.0)))
    t1 = (lax.broadcasted_iota(jnp.int32, shape, 0) + 1).astype(F32)
    return jnp.minimum(t1, window)


def _pc_specs(s):
    zcol = lambda blk: pl.BlockSpec((s, 256), lambda i, blk=blk: (0, blk))
    first = Z_PC // 256
    return [zcol(first), zcol(first + 1), zcol(first + 2), zcol(first + 3),
            pl.BlockSpec((256, 256), lambda i: (0, 0)), pl.BlockSpec((1, 256), lambda i: (0, 0)),
            pl.BlockSpec((3, 256), lambda i: (0, 0))]


def _poolconv_fwd(z, wbd, pscale, convw):
    s = z.shape[0]

    def body(pu_ref, ch_ref, cb_ref, cc_ref, w_ref, ps_ref, cw_ref, yb_ref, yc_ref):
        u = pu_ref[...]
        p = _pool_windows(u, _shift_down) / _pool_count(u.shape) - u
        yb = jnp.dot(p.astype(BF16), w_ref[...].astype(BF16), preferred_element_type=F32) * ps_ref[...]
        yb_ref[...] = yb.astype(BF16)
        uc = cc_ref[...] * ch_ref[...]
        cw = cw_ref[...]
        conv = cw[0:1, :] * _shift_down(uc, 2) + cw[1:2, :] * _shift_down(uc, 1) + cw[2:3, :] * uc
        yc_ref[...] = (cb_ref[...] * conv).astype(BF16)

    out = pl.BlockSpec((s, 256), lambda i: (0, 0))
    return pl.pallas_call(
        body, name="poolconv_fwd", grid=(1,), in_specs=_pc_specs(s), out_specs=[out, out],
        out_shape=[jax.ShapeDtypeStruct((s, 256), BF16)] * 2, compiler_params=_params("arbitrary"),
    )(z, z, z, z, wbd, pscale, convw)


def _poolconv_bwd(dyb, dyc, z, wbd, pscale, convw):
    s = z.shape[0]

    def body(dyb_ref, dyc_ref, pu_ref, ch_ref, cb_ref, cc_ref, w_ref, ps_ref, cw_ref, dz_ref, dw_ref, dps_ref, dcw_ref):
        u = pu_ref[...]
        count = _pool_count(u.shape)
        p = (_pool_windows(u, _shift_down) / count - u).astype(BF16)
        wb = w_ref[...].astype(BF16)
        dyb_v = dyb_ref[...]
        pw = jnp.dot(p, wb, preferred_element_type=F32)
        dps_ref[...] = jnp.broadcast_to(jnp.sum(dyb_v * pw, axis=0, keepdims=True), (8, 256))
        dys = (dyb_v * ps_ref[...]).astype(BF16)
        dp = lax.dot_general(dys, wb, (((1,), (1,)), ((), ())), preferred_element_type=F32)
        dw_ref[...] = lax.dot_general(p, dys, (((0,), (0,)), ((), ())), preferred_element_type=F32)
        dz_ref[:, 0:256] = (_pool_windows(dp / count, _shift_up) - dp).astype(BF16)

        ch, cb, cc = ch_ref[...], cb_ref[...], cc_ref[...]
        uc = cc * ch
        cw = cw_ref[...]
        u2, u1 = _shift_down(uc, 2), _shift_down(uc, 1)
        conv = cw[0:1, :] * u2 + cw[1:2, :] * u1 + cw[2:3, :] * uc
        dyc_v = dyc_ref[...]
        dconv = dyc_v * cb
        du = cw[0:1, :] * _shift_up(dconv, 2) + cw[1:2, :] * _shift_up(dconv, 1) + cw[2:3, :] * dconv
        dz_ref[:, 256:512] = (du * cc).astype(BF16)
        dz_ref[:, 512:768] = (dyc_v * conv).astype(BF16)
        dz_ref[:, 768:1024] = (du * ch).astype(BF16)
        dcw_ref[...] = jnp.zeros_like(dcw_ref)
        dcw_ref[0:1, :] = jnp.sum(dconv * u2, axis=0, keepdims=True)
        dcw_ref[1:2, :] = jnp.sum(dconv * u1, axis=0, keepdims=True)
        dcw_ref[2:3, :] = jnp.sum(dconv * uc, axis=0, keepdims=True)

    blk = lambda r, c: pl.BlockSpec((r, c), lambda i: (0, 0))
    return pl.pallas_call(
        body, name="poolconv_bwd", grid=(1,),
        in_specs=[blk(s, 256), blk(s, 256)] + _pc_specs(s),
        out_specs=[blk(s, 1024), blk(256, 256), blk(8, 256), blk(8, 256)],
        out_shape=[jax.ShapeDtypeStruct((s, 1024), BF16), jax.ShapeDtypeStruct((256, 256), F32),
                   jax.ShapeDtypeStruct((8, 256), F32), jax.ShapeDtypeStruct((8, 256), F32)],
        compiler_params=_params("arbitrary"),
    )(dyb, dyc, z, z, z, z, wbd, pscale, convw)


_NT = (((1,), (1,)), ((), ()))
_TN = (((0,), (0,)), ((), ()))


ATT_Q, ATT_K = 256, 256
ATT_HEADS_BWD = 8
ATT_HEADS = 8


def _att_logits(q, k, fr, q0, k0, masked):
    logits = lax.dot_general(q, k, _NT, preferred_element_type=F32) - fr
    if not masked:
        return logits
    row = q0 + lax.broadcasted_iota(jnp.int32, logits.shape, 0)
    col = k0 + lax.broadcasted_iota(jnp.int32, logits.shape, 1)
    return jnp.where(row >= col, logits, NEG_INF)


def _causal_sweep(step, qi, init):
    n_full = (qi * ATT_Q) // ATT_K
    carry = lax.fori_loop(0, n_full, lambda j, carry: step(j, carry, False), init)
    return step(n_full, carry, True)


HEAD_PAIRS = HEADS // 2


def _lane_pick(v, lane, idx):
    return jnp.sum(jnp.where(lane == idx, v, 0.0), axis=-1, keepdims=True)


def _lane_put(lane, idx, col):
    return jnp.where(lane == idx, col, 0.0)


def _split_heads(v, low):
    zero = jnp.zeros_like(v)
    return jnp.where(low, v, zero), jnp.where(low, zero, v)


def _attn_fwd(qkv, fr):
    s = qkv.shape[0]
    nk = s // ATT_K
    width = ATT_HEADS * HEAD_DIM
    groups = HEADS // ATT_HEADS

    def body(q_ref, k_ref, v_ref, fr_ref, o_ref, lse_ref):
        qi, grp = pl.program_id(0), pl.program_id(1)
        lane = lax.broadcasted_iota(jnp.int32, (ATT_Q, 128), 1)
        low = lane < HEAD_DIM
        qs = []
        for pr in range(ATT_HEADS // 2):
            qs += _split_heads(q_ref[:, 128 * pr:128 * (pr + 1)] * (HEAD_DIM ** -0.5), low)

        def step(j, carry, masked):
            k0 = pl.multiple_of(j * ATT_K, ATT_K)
            out = []
            for h in range(ATT_HEADS):
                cols = slice(128 * (h // 2), 128 * (h // 2 + 1))
                m, l, acc = carry[h]
                logits = _att_logits(qs[h], k_ref[pl.ds(k0, ATT_K), cols], fr_ref[h, pl.ds(j, 1), :], qi * ATT_Q, k0, masked)
                m_new = jnp.maximum(m, jnp.max(logits, axis=-1, keepdims=True))
                p = jnp.exp(logits - m_new)
                alpha = jnp.exp(m - m_new)
                l = alpha * l + jnp.sum(p, axis=-1, keepdims=True)
                acc = alpha * acc + jnp.dot(p.astype(BF16), v_ref[pl.ds(k0, ATT_K), cols], preferred_element_type=F32)
                out.append((m_new, l, acc))
            return tuple(out)

        one = (jnp.full((ATT_Q, 1), NEG_INF, F32), jnp.zeros((ATT_Q, 1), F32), jnp.zeros((ATT_Q, 128), F32))
        done = _causal_sweep(step, qi, (one,) * ATT_HEADS)

        @pl.when(grp == 0)
        def _():
            lse_ref[...] = jnp.zeros_like(lse_ref)

        lse = jnp.zeros((ATT_Q, 128), F32)
        for pr in range(ATT_HEADS // 2):
            (m0, l0, acc0), (m1, l1, acc1) = done[2 * pr], done[2 * pr + 1]
            o_ref[:, 128 * pr:128 * (pr + 1)] = jnp.where(low, acc0 / l0, acc1 / l1)
            head = ATT_HEADS * grp + 2 * pr
            lse = lse + _lane_put(lane, head, m0 + jnp.log(l0)) + _lane_put(lane, head + 1, m1 + jnp.log(l1))
        lse_ref[...] += lse

    return pl.pallas_call(
        body, name="attn_fwd", grid=(s // ATT_Q, groups),
        in_specs=[pl.BlockSpec((ATT_Q, width), lambda i, g: (i, g)),
                  pl.BlockSpec((s, width), lambda i, g: (0, groups + g)),
                  pl.BlockSpec((s, width), lambda i, g: (0, 2 * groups + g)),
                  pl.BlockSpec((ATT_HEADS, nk, ATT_K), lambda i, g: (g, 0, 0))],
        out_specs=[pl.BlockSpec((ATT_Q, width), lambda i, g: (i, g)), pl.BlockSpec((ATT_Q, 128), lambda i, g: (i, 0))],
        out_shape=[jax.ShapeDtypeStruct((s, A_WIDTH), F32), jax.ShapeDtypeStruct((s, 128), F32)],
        compiler_params=_params("parallel", "arbitrary"),
    )(qkv, qkv, qkv, fr)


def _attn_bwd(qkv, do, o, lse, fr):
    s = qkv.shape[0]
    nk = s // ATT_K
    scale = HEAD_DIM ** -0.5
    heads = ATT_HEADS_BWD
    width = heads * HEAD_DIM
    groups = HEADS // heads

    def body(q_ref, k_ref, v_ref, do_ref, o_ref, lse_ref, fr_ref, dq_ref, dk_ref, dv_ref, dfc_ref, dfr_ref, dk_acc, dv_acc):
        grp = pl.program_id(0)
        lane = lax.broadcasted_iota(jnp.int32, (ATT_Q, 128), 1)
        low = lane < HEAD_DIM
        low_t = lax.broadcasted_iota(jnp.int32, (128, ATT_Q), 0) < HEAD_DIM
        dk_acc[...] = jnp.zeros_like(dk_acc)
        dv_acc[...] = jnp.zeros_like(dv_acc)
        dfr_ref[...] = jnp.zeros_like(dfr_ref)

        @pl.when(grp == 0)
        def _():
            dfc_ref[...] = jnp.zeros_like(dfc_ref)

        def outer(i, carry):
            q0 = pl.multiple_of(i * ATT_Q, ATT_Q)
            rows = pl.ds(q0, ATT_Q)
            lsev = lse_ref[rows, :]
            qts, dots, qs, dos, deltas, lses = [], [], [], [], [], []
            for pr in range(heads // 2):
                pcols = slice(128 * pr, 128 * (pr + 1))
                q2, do2 = q_ref[rows, pcols] * scale, do_ref[rows, pcols]
                prod = do2 * o_ref[rows, pcols]
                deltas += [jnp.sum(jnp.where(low, prod, 0.0), axis=-1, keepdims=True),
                           jnp.sum(jnp.where(low, 0.0, prod), axis=-1, keepdims=True)]
                dob2 = do2.astype(BF16)
                qts += _split_heads(q2.astype(F32).T.astype(BF16), low_t)
                dots += _split_heads(do2.T.astype(BF16), low_t)
                qs += _split_heads(q2, low)
                dos += _split_heads(dob2, low)
                lses += [_lane_pick(lsev, lane, heads * grp + 2 * pr), _lane_pick(lsev, lane, heads * grp + 2 * pr + 1)]

            def inner(j, carry, masked):
                k0 = pl.multiple_of(j * ATT_K, ATT_K)
                krows = pl.ds(k0, ATT_K)
                out, dkt, dvt = [], [], []
                for h in range(heads):
                    pcols = slice(128 * (h // 2), 128 * (h // 2 + 1))
                    dq, dfc = carry[h]
                    k2 = k_ref[krows, pcols]
                    p = jnp.exp(_att_logits(qs[h], k2, fr_ref[h, pl.ds(j, 1), :], q0, k0, masked) - lses[h])
                    dp = lax.dot_general(dos[h], v_ref[krows, pcols], _NT, preferred_element_type=F32)
                    ds = p * (dp - deltas[h])
                    dsb = ds.astype(BF16)
                    dkt.append(jnp.dot(qts[h], dsb, preferred_element_type=F32))
                    dvt.append(jnp.dot(dots[h], p.astype(BF16), preferred_element_type=F32))
                    dfr_ref[h, pl.ds(j, 1), :] -= jnp.sum(ds, axis=0, keepdims=True)
                    out.append((dq + jnp.dot(dsb, k2, preferred_element_type=F32), dfc + (ds[:, :128] + ds[:, 128:])))
                for pr in range(heads // 2):
                    prows = slice(128 * pr, 128 * (pr + 1))
                    dk_acc[j, prows, :] += dkt[2 * pr] + dkt[2 * pr + 1]
                    dv_acc[j, prows, :] += dvt[2 * pr] + dvt[2 * pr + 1]
                return tuple(out)

            one = (jnp.zeros((ATT_Q, 128), F32), jnp.zeros((ATT_Q, 128), F32))
            done = _causal_sweep(inner, i, (one,) * heads)
            dfc = jnp.zeros((ATT_Q, 128), F32)
            for pr in range(heads // 2):
                (dq0, dfc0), (dq1, dfc1) = done[2 * pr], done[2 * pr + 1]
                dq_ref[rows, 128 * pr:128 * (pr + 1)] = (jnp.where(low, dq0, dq1) * scale).astype(BF16)
                head = heads * grp + 2 * pr
                dfc = (dfc + _lane_put(lane, head, jnp.sum(dfc0, axis=-1, keepdims=True))
                       + _lane_put(lane, head + 1, jnp.sum(dfc1, axis=-1, keepdims=True)))
            dfc_ref[rows, :] += dfc
            return carry

        lax.fori_loop(0, s // ATT_Q, outer, 0)
        for j in range(nk):
            for pr in range(heads // 2):
                prows, pcols = slice(128 * pr, 128 * (pr + 1)), slice(128 * pr, 128 * (pr + 1))
                dk_ref[ATT_K * j:ATT_K * (j + 1), pcols] = dk_acc[j, prows, :].T.astype(BF16)
                dv_ref[ATT_K * j:ATT_K * (j + 1), pcols] = dv_acc[j, prows, :].T.astype(BF16)

    part = lambda first: pl.BlockSpec((s, width), lambda g, first=first: (0, first + g))
    whole = pl.BlockSpec((s, 128), lambda g: (0, 0))
    rowv = pl.BlockSpec((heads, nk, ATT_K), lambda g: (g, 0, 0))
    return pl.pallas_call(
        body, name="attn_bwd", grid=(groups,),
        in_specs=[part(0), part(groups), part(2 * groups), part(0), part(0), whole, rowv],
        out_specs=[part(0), part(0), part(0), whole, rowv],
        out_shape=[jax.ShapeDtypeStruct((s, A_WIDTH), BF16)] * 3 + [jax.ShapeDtypeStruct((s, 128), F32), jax.ShapeDtypeStruct((HEADS, nk, ATT_K), F32)],
        scratch_shapes=[pltpu.VMEM((nk, width, ATT_K), F32), pltpu.VMEM((nk, width, ATT_K), F32)],
        compiler_params=_params("arbitrary"),
    )(qkv, qkv, qkv, do, o, lse, fr)


def _ada_fwd(c_all, w_ada, b_loc):
    depth, _, n = w_ada.shape
    tn = 512

    def body(c_ref, w_ref, b_ref, o_ref, sc_ref):
        cv = c_ref[...]
        sc = cv * jax.nn.sigmoid(cv)
        sc_ref[...] = sc
        o_ref[0] = jnp.dot(sc.astype(BF16), w_ref[0].astype(BF16), preferred_element_type=F32) + b_ref[0]

    return pl.pallas_call(
        body, name="ada_fwd", grid=(depth, n // tn),
        in_specs=[pl.BlockSpec((N_DEV, D), lambda l, j: (0, 0)), pl.BlockSpec((1, D, tn), lambda l, j: (l, 0, j)),
                  pl.BlockSpec((1, 1, tn), lambda l, j: (l, 0, j))],
        out_specs=[pl.BlockSpec((1, N_DEV, tn), lambda l, j: (l, 0, j)), pl.BlockSpec((N_DEV, D), lambda l, j: (0, 0))],
        out_shape=[jax.ShapeDtypeStruct((depth, N_DEV, n), F32), jax.ShapeDtypeStruct((N_DEV, D), F32)],
        compiler_params=_params("arbitrary", "arbitrary"),
    )(c_all, w_ada, b_loc)


def _sum_devices(gathered):
    n = gathered.shape[1]
    tn = _pick(n, (1408, 1024, 640, 512, 128))

    def body(g_ref, o_ref):
        acc = g_ref[0:8, :]
        for dev in range(1, N_DEV):
            acc = acc + g_ref[8 * dev:8 * dev + 8, :]
        o_ref[...] = acc

    return pl.pallas_call(
        body, name="sum_devices", grid=(n // tn,),
        in_specs=[pl.BlockSpec((8 * N_DEV, tn), lambda j: (0, j))], out_specs=pl.BlockSpec((8, tn), lambda j: (0, j)),
        out_shape=jax.ShapeDtypeStruct((8, n), F32), compiler_params=_params("parallel"),
    )(gathered)


def _place():
    x, y, c = lax.axis_index("x"), lax.axis_index("y"), lax.axis_index("c")
    chips = [(1 - x, y), (x, 1 - y), (1 - x, 1 - y)]
    return x, y, c, chips


def _allgather8(block, name, after=()):
    m_per, n = block.shape

    def body(x_ref, *rest):
        out_ref, send_sems, recv_sems, local_sem = rest[len(after):]
        x, y, c, chips = _place()
        me, sibling = (x, y, c), (x, y, 1 - c)

        def rows(px, py, pc):
            return out_ref.at[pl.ds((4 * px + 2 * py + pc) * m_per, m_per), :]

        def copy(k, blk, to, src=None):
            return pltpu.make_async_remote_copy(
                src_ref=rows(*blk) if src is None else src, dst_ref=rows(*blk),
                send_sem=send_sems.at[k], recv_sem=recv_sems.at[k], device_id=to, device_id_type=MESH)

        mine = pltpu.make_async_copy(x_ref, rows(*me), local_sem)
        mine.start()
        first = [copy(0, me, sibling, src=x_ref)]
        first += [copy(1 + j, me, (*chip, c), src=x_ref) for j, chip in enumerate(chips)]
        for cp in first:
            cp.start()
        passed = [copy(4 + j, (*chip, c), sibling) for j, chip in enumerate(chips)]
        for j, chip in enumerate(chips):
            copy(1 + j, (*chip, c), me).wait_recv()
            passed[j].start()
        copy(0, sibling, me).wait_recv()
        for j, chip in enumerate(chips):
            copy(4 + j, (*chip, 1 - c), me).wait_recv()
        for cp in first + passed:
            cp.wait_send()
        mine.wait()

    return pl.pallas_call(
        body, name=name, out_shape=jax.ShapeDtypeStruct((N_DEV * m_per, n), block.dtype),
        in_specs=[pl.BlockSpec(memory_space=pltpu.VMEM)] + [pl.BlockSpec(memory_space=pl.ANY)] * len(after),
        out_specs=pl.BlockSpec(memory_space=pltpu.VMEM),
        scratch_shapes=[pltpu.SemaphoreType.DMA((7,)), pltpu.SemaphoreType.DMA((7,)), pltpu.SemaphoreType.DMA],
        compiler_params=pltpu.CompilerParams(vmem_limit_bytes=V7X_VMEM_LIMIT),
    )(block, *after)


_SEM = pl.BlockSpec(memory_space=pltpu.SEMAPHORE)
_DATAFLOW = pltpu.SideEffectType.DATAFLOW_SIDE_EFFECTING


def _plan_copies(plan, refs, send_sems, recv_sems):
    return [pltpu.make_async_remote_copy(src_ref=src, dst_ref=dst, send_sem=send_sems.at[i], recv_sem=recv_sems.at[i],
                                         device_id=to, device_id_type=MESH) for i, (src, dst, to) in enumerate(plan(refs))]


class _Token(NamedTuple):
    after: jax.Array
    tie: jax.Array


def _after_operand(after):
    return after.after if isinstance(after, _Token) else after


def _copies_start(bufs, plan, n_copies, after, name):
    nb = len(bufs)

    def body(*refs):
        for cp in _plan_copies(plan, refs[:nb], refs[nb + 1], refs[nb + 2]):
            cp.start()
        for token in refs[-2:]:
            token[...] = jnp.zeros_like(token)

    sem = pltpu.SemaphoreType.DMA((n_copies,))
    vmem = pl.BlockSpec(memory_space=pltpu.VMEM)
    outs = pl.pallas_call(
        body, name=name,
        out_shape=(sem, sem, *[pltpu.HBM(b.shape, b.dtype) for b in bufs], jax.ShapeDtypeStruct((8, 128), F32),
                   jax.ShapeDtypeStruct((1, 1), F32)),
        in_specs=[_HBM] * nb + [pl.BlockSpec(memory_space=pl.ANY)],
        out_specs=(_SEM, _SEM, *[_HBM] * nb, vmem, vmem),
        input_output_aliases={i: 2 + i for i in range(nb)},
        compiler_params=pltpu.CompilerParams(has_side_effects=_DATAFLOW),
    )(*[pltpu.with_memory_space_constraint(b, pltpu.HBM) for b in bufs], _after_operand(after))
    return outs[0], outs[1], list(outs[2:2 + nb]), _Token(outs[-2], outs[-1])


def _copies_wait(started, plan, after, name):
    send_sems, recv_sems, bufs, _ = started
    nb = len(bufs)

    def body(*refs):
        for cp in _plan_copies(plan, refs[:nb], refs[nb], refs[nb + 1]):
            cp.wait_send()
            cp.wait_recv()

    return list(pl.pallas_call(
        body, name=name, out_shape=tuple(pltpu.HBM(b.shape, b.dtype) for b in bufs),
        in_specs=[_HBM] * nb + [_SEM, _SEM, pl.BlockSpec(memory_space=pl.ANY)], out_specs=tuple([_HBM] * nb),
        input_output_aliases={i: i for i in range(nb)},
        compiler_params=pltpu.CompilerParams(has_side_effects=_DATAFLOW),
    )(*bufs, send_sems, recv_sems, _after_operand(after)))


def _half_rows(ref, axis, c):
    half = ref.shape[axis] // 2
    return pl.ds(c * half, half)


def _plan_gather_ici(refs):
    n = len(refs) // 2
    x, y, c, chips = _place()
    out = []
    for a in range(n):
        rows = _half_rows(refs[a], 0, c)
        out += [(refs[a].at[rows], refs[n + a].at[2 * x + y, rows], (*chip, c)) for chip in chips]
        out.append((refs[a], refs[n + a].at[2 * x + y], (x, y, 1 - c)))
    return out


def _plan_gather_d2d(refs):
    x, y, c, chips = _place()
    out = []
    for ref in refs:
        rows = _half_rows(ref, 1, c)
        for px, py in chips:
            landed = ref.at[2 * px + py, rows]
            out.append((landed, landed, (x, y, 1 - c)))
    return out


def _plan_rs_sibling(refs):
    n = len(refs) // 2
    x, y, c, _ = _place()
    return [(refs[a].at[pl.ds(0, refs[a].shape[0]), _half_rows(refs[a], 1, 1 - c)], refs[n + a], (x, y, 1 - c)) for a in range(n)]


def _plan_rs_chips(refs):
    n = len(refs) // 2
    x, y, c, chips = _place()
    return [(refs[a].at[2 * px + py], refs[n + a].at[k], (px, py, c)) for a in range(n) for k, (px, py) in enumerate(chips)]


def _plan_rs_share(refs):
    x, y, c, _ = _place()
    return [(ref.at[_half_rows(ref, 0, c)], ref.at[_half_rows(ref, 0, c)], (x, y, 1 - c)) for ref in refs]


def _chip_sum(g, other, sel, name, blocked=True):
    nblk, half, cdim = other.shape
    tr = _pick(half, (512, 256, 128, 64) if nblk > 1 else (128, 64))
    per = half // tr

    def body(sel_ref, g_ref, t_ref, wire_ref, own_ref):
        total = g_ref[0] + t_ref[0]
        wire_ref[0] = total.astype(BF16)
        if blocked:
            @pl.when(pl.program_id(1) == sel_ref[1])
            def _():
                own_ref[...] = total
        else:
            own_ref[0] = total

    blk = pl.BlockSpec((1, tr, cdim), lambda i, p, sel_ref: (p, i, 0))
    own_spec = pl.BlockSpec((tr, cdim), lambda i, p, sel_ref: (i, 0)) if blocked else blk
    own_shape = jax.ShapeDtypeStruct((half, cdim) if blocked else other.shape, F32)
    return pl.pallas_call(
        body, name=name,
        grid_spec=pltpu.PrefetchScalarGridSpec(
            num_scalar_prefetch=1, grid=(per, nblk),
            in_specs=[pl.BlockSpec((1, tr, cdim), lambda i, p, sel_ref: (p, sel_ref[0] * per + i, 0)), blk],
            out_specs=[blk, own_spec]),
        out_shape=[jax.ShapeDtypeStruct(other.shape, BF16), own_shape],
        compiler_params=_params("parallel", "arbitrary"),
    )(sel, g, other)


def _final_sum(own, recv, sel, name):
    half, cdim = own.shape
    tr = _pick(half, (512, 256, 128, 64))
    per = half // tr

    def body(sel_ref, own_ref, r0_ref, r1_ref, r2_ref, o_ref):
        o_ref[...] = ((own_ref[...] + r0_ref[0].astype(F32)) + r1_ref[0].astype(F32)) + r2_ref[0].astype(F32)

    part = lambda k: pl.BlockSpec((1, tr, cdim), lambda i, sel_ref, k=k: (k, i, 0))
    return pl.pallas_call(
        body, name=name,
        grid_spec=pltpu.PrefetchScalarGridSpec(
            num_scalar_prefetch=1, grid=(per,),
            in_specs=[pl.BlockSpec((tr, cdim), lambda i, sel_ref: (i, 0)), part(0), part(1), part(2)],
            out_specs=pl.BlockSpec((tr, cdim), lambda i, sel_ref: (sel_ref[0] * per + i, 0))),
        out_shape=jax.ShapeDtypeStruct((2 * half, cdim), F32), compiler_params=_params("parallel"),
    )(sel, own, recv, recv, recv)


def _row(v):
    return v.reshape(1, -1)


_BR_A, _BR_B, _BR_C = (0, A_WIDTH), (A_WIDTH, POOL_WIDTH), (A_WIDTH + POOL_WIDTH, CONV_WIDTH)


def _tie(v, token):
    return v if token is None else v + token.tie


def _no_hook(point, after, ready=None):
    return None


def _layer_fwd(x, w, mod, hook=_no_hook):
    s = x.shape[0]
    mod3 = mod.reshape(6, 1, D)
    h = _modnorm_fwd(x, _row(w["g_mix_pre"]), (mod3, 0), (mod3, 1), "mix_pre_fwd")
    hook("pre", h)
    z = _mm(h, w["w_all"], name="mm_in")
    qkv = z[:, Z_QKV:Z_PC].astype(BF16)
    fl = z[:, Z_FL:Z_COLS]
    cum = _cumf_fwd(fl, w["b_f_pad"])
    fr = cum[:, :HEADS].T.reshape(HEADS, s // ATT_K, ATT_K)
    br_a, lse = _attn_fwd(qkv, fr)
    br_b, br_c = _poolconv_fwd(z, w["w_pool_bd"], _tie(_row(w["pool_scale"]), hook("attn", lse)), w["conv_w"])
    hook("pool", br_b)
    wbr = w["w_branch"]
    pa = _mm(br_a, wbr, b_rows=_BR_A, name="mm_br_a")
    pb = _mm(br_b, wbr, b_rows=_BR_B, name="mm_br_b")
    pc = _mm(br_c, wbr, b_rows=_BR_C, name="mm_br_c")
    merged = _merge_fwd(z, pa, pb, pc)
    y = _mm(merged, w["w_out"], name="mm_out")
    x1, h2 = _post_pre_fwd(x, y, _row(w["g_mix_post"]), (mod3, 2), _row(w["g_ff_pre"]), (mod3, 3), (mod3, 4), "mix_post_ff_pre_fwd")
    a, r = _mm(h2, w["w_ff1"], b_split=N_CHIPS, epilogue=_relu2_fwd, out_dtype=(F32, BF16), name="mm_ff1")
    y2 = _mm(r, w["w_ff2"], name="mm_ff2")
    x2 = _post_fwd(x1, y2, _tie(_row(w["g_ff_post"]), hook("ff_post", y2)), (mod3, 5), "ff_post_fwd")
    hook("end", x2)
    saved = dict(x=x, h=h, z=z, qkv=qkv, fl=fl, fr=fr, lse=lse, br_a=br_a, br_b=br_b, br_c=br_c, pa=pa, pb=pb, pc=pc,
                 merged=merged, y=y, x1=x1, h2=h2, a=a, r=r, y2=y2)
    return x2, saved


def _layer_bwd(dx2, sv, w, mod, hook=_no_hook):
    s = dx2.shape[0]
    mod3 = mod.reshape(6, 1, D)
    dy2, sum_ff_post = _post_bwd(dx2, sv["y2"], _row(w["g_ff_post"]), (mod3, 5), "ff_post_bwd")
    (da,) = _mm(dy2, w["w_ff2"], tb=True, epilogue=_relu2_bwd, extras=(sv["a"],), out_dtype=(BF16,), name="mm_ff2_dx")
    d_w_ff2 = _mm(sv["r"], dy2, ta=True, name="mm_ff2_dw")
    dh2 = _mm(da, w["w_ff1"], tb=True, b_split=N_CHIPS, name="mm_ff1_dx")
    d_w_ff1 = _mm(sv["h2"], da, ta=True, out_split=N_CHIPS, name="mm_ff1_dw")
    g_ff_pre = _tie(_row(w["g_ff_pre"]), hook("ff_pre", dh2, dict(w_ff1=d_w_ff1, w_ff2=d_w_ff2)))
    dx1, dy, sum_mid = _pre_post_bwd(dh2, sv["x1"], dx2, g_ff_pre, (mod3, 4), sv["y"], _row(w["g_mix_post"]), (mod3, 2), "ff_pre_mix_post_bwd")
    sum_ff_pre, sum_mix_post = sum_mid, sum_mid[3:]
    dmerged = _mm(dy, w["w_out"], tb=True, name="mm_out_dx")
    d_w_out = _mm(sv["merged"], dy, ta=True, name="mm_out_dw")
    dz, dpa, dpb, dpc = _merge_bwd(dmerged, sv["z"], sv["pa"], sv["pb"], sv["pc"])
    wbr = w["w_branch"]
    dbr_a = _mm(dpa, wbr, tb=True, b_rows=_BR_A, name="mm_br_a_dx")
    dbr_b = _mm(dpb, wbr, tb=True, b_rows=_BR_B, name="mm_br_b_dx")
    dbr_c = _mm(dpc, wbr, tb=True, b_rows=_BR_C, name="mm_br_c_dx")
    d_w_branch = jnp.concatenate([_mm(sv["br_a"], dpa, ta=True, name="mm_br_a_dw"), _mm(sv["br_b"], dpb, ta=True, name="mm_br_b_dw"),
                                  _mm(sv["br_c"], dpc, ta=True, name="mm_br_c_dw")], axis=0)

    dq, dk, dv, dfc, dfr = _attn_bwd(sv["qkv"], dbr_a, sv["br_a"], sv["lse"], sv["fr"])
    dcum = dfc + jnp.pad(dfr.reshape(HEADS, s).T, ((0, 0), (0, 128 - HEADS)))
    dfl, sum_bf = _cumf_bwd(dcum, sv["fl"], _tie(w["b_f_pad"], hook("cumf", dfc)))
    dpc_z, d_wbd, sum_ps, sum_cw = _poolconv_bwd(dbr_b, dbr_c, sv["z"], w["w_pool_bd"], _row(w["pool_scale"]), w["conv_w"])
    for at, part in ((Z_QKV, dq), (Z_QKV + A_WIDTH, dk), (Z_QKV + 2 * A_WIDTH, dv), (Z_PC, dpc_z), (Z_FL, dfl)):
        dz = lax.dynamic_update_slice(dz, part, (0, at))
    dh = _mm(dz, w["w_all"], tb=True, name="mm_in_dx")
    d_w_all = _mm(sv["h"], dz, ta=True, name="mm_in_dw")
    hook("mix_pre", dh)
    dx, sum_mix_pre = _modnorm_bwd(dh, sv["x"], dx1, _row(w["g_mix_pre"]), (mod3, 1), "mix_pre_bwd")

    dmod = jnp.stack([sum_mix_pre[0], sum_mix_pre[1], sum_mix_post[0], sum_ff_pre[0], sum_ff_pre[1], sum_ff_post[0]])
    d_w_in = d_w_all[None]
    d_w_pool = jnp.stack([d_wbd[64 * g:64 * g + 64, 64 * g:64 * g + 64] for g in range(4)])
    big = dict(w_in=d_w_in, w_branch=d_w_branch, w_out=d_w_out, w_ff1=d_w_ff1, w_ff2=d_w_ff2)
    small = dict(g_mix_pre=sum_mix_pre[2], g_mix_post=sum_mix_post[1], g_ff_pre=sum_ff_pre[2], g_ff_post=sum_ff_post[1],
                 b_f=sum_bf[0, :HEADS], w_pool=d_w_pool, pool_scale=sum_ps[0], conv_w=sum_cw[0:3])
    return dx, dmod, big, small


_QKV_END, _FL_END, _PC_END = 3 * A_WIDTH, 3 * A_WIDTH + HEADS, 3 * A_WIDTH + HEADS + POOL_WIDTH + 3 * CONV_WIDTH
_W_IN_GROUPS = ((_PC_END, IN_COLS, Z_GL), (0, _QKV_END, Z_QKV), (_FL_END, _PC_END, Z_PC), (_QKV_END, _FL_END, Z_FL))
_SHARD_COLS = IN_COLS // N_CHIPS


def _w_in_layout():
    out = []
    for p in range(N_CHIPS):
        pieces = []
        for lo, hi, at in _W_IN_GROUPS:
            a, b = max(lo, p * _SHARD_COLS), min(hi, (p + 1) * _SHARD_COLS)
            if a < b:
                pieces.append((at + a - lo, at + b - lo, a - p * _SHARD_COLS))
        pieces.sort()
        segs = []
        for z0, z1, _ in pieces:
            s, e = z0 // 128 * 128, -(-z1 // 128) * 128
            if segs and s <= segs[-1][1]:
                segs[-1] = (segs[-1][0], max(e, segs[-1][1]))
            else:
                segs.append((s, e))
        assert sum(e - s for s, e in segs) == Z_WINDOW
        out.append((pieces, segs))
    return out


Z_WINDOW = 1536


def _w_in_window(shard, p):
    pieces, segs = _w_in_layout()[p]
    cols = []
    for s, e in segs:
        at = s
        for z0, z1, src in pieces:
            if s <= z0 < e:
                if z0 > at:
                    cols.append(jnp.zeros((shard.shape[0], z0 - at), shard.dtype))
                cols.append(shard[:, src:src + z1 - z0])
                at = z1
        if e > at:
            cols.append(jnp.zeros((shard.shape[0], e - at), shard.dtype))
    return jnp.concatenate(cols, axis=1)


def _own_window(shard, chip):
    return lax.switch(chip, [lambda t, p=p: _w_in_window(t, p) for p in range(N_CHIPS)], shard)


def _w_all_from_windows(blocks):
    layout = _w_in_layout()
    bounds = sorted({edge for _, segs in layout for seg in segs for edge in seg})
    parts = []
    for lo, hi in zip(bounds[:-1], bounds[1:]):
        covering = []
        for p, (_, segs) in enumerate(layout):
            at = 0
            for s, e in segs:
                if s <= lo and hi <= e:
                    covering.append(blocks[p][:, at + lo - s:at + hi - s])
                at += e - s
        assert covering
        parts.append(covering[0] if len(covering) == 1 else covering[0] + covering[1])
    return jnp.concatenate(parts, axis=1)


def _w_in_shard(d_w_all, p):
    pieces = []
    for lo, hi, at in sorted(_W_IN_GROUPS):
        a, b = max(lo, p * _SHARD_COLS), min(hi, (p + 1) * _SHARD_COLS)
        if a < b:
            pieces.append(d_w_all[:, at + a - lo:at + b - lo])
    return jnp.concatenate(pieces, axis=1)


def _w_in_shards(d_w_all):
    return jnp.stack([_w_in_shard(d_w_all, p) for p in range(N_CHIPS)])


def _full_layer_weights(w_in_blocks, w_branch, w_out, w_ff1, w_ff2, g_mix_pre, g_mix_post, g_ff_pre, g_ff_post, b_f, w_pool, pool_scale, conv_w):
    w_all = None if w_in_blocks is None else _w_all_from_windows(w_in_blocks)
    wbd = (w_pool[:, :, None, :] * jnp.eye(4, dtype=F32)[:, None, :, None]).reshape(POOL_WIDTH, POOL_WIDTH)
    return dict(w_all=w_all, w_branch=w_branch, w_out=w_out, w_ff1=w_ff1, w_ff2=w_ff2, g_mix_pre=g_mix_pre, g_mix_post=g_mix_post,
                g_ff_pre=g_ff_pre, g_ff_post=g_ff_post, b_f_pad=jnp.pad(b_f, (0, 128 - HEADS)).reshape(1, 128), w_pool_bd=wbd,
                pool_scale=pool_scale, conv_w=conv_w)


class _NoComm:
    def layer_weights(self, l):
        raise NotImplementedError

    def fwd_hook(self, l):
        return _no_hook

    def bwd_hook(self, l):
        return _no_hook

    def grads_ready(self, l, big):
        return None


class _Layers(_NoComm):
    def __init__(self, layers):
        self.layers = layers

    def layer_weights(self, l):
        return self.layers[l]


def _local_step(x, target, mods, comm):
    saved, weights = [], []
    act = x
    for l in range(DEPTH):
        weights.append(comm.layer_weights(l))
        act, sv = _layer_fwd(act, weights[l], mods[l], comm.fwd_hook(l))
        saved.append(sv)
    dact, sq = _loss_head(act, target)
    loss = sq[0, 0] * (0.5 / D)
    dmods, bigs, smalls = [None] * DEPTH, [None] * DEPTH, [None] * DEPTH
    token = None
    for l in reversed(range(DEPTH)):
        dact, dmods[l], bigs[l], smalls[l] = _layer_bwd(dact, saved[l], weights[l], _tie(mods[l], token), comm.bwd_hook(l))
        token = comm.grads_ready(l, bigs[l])
    return loss, dact, jnp.stack(dmods), bigs, smalls


_BIG = ("w_in", "w_branch", "w_out", "w_ff1", "w_ff2")


class _GatherJob:
    def __init__(self, tag, shards, after):
        self.tag, self.n = tag, len(shards)
        lands = [lax.empty((N_CHIPS,) + s.shape, s.dtype) for s in shards]
        self.state = _copies_start(list(shards) + lands, _plan_gather_ici, 4 * self.n, after, "gather_ici_start_" + tag)
        self.token = self.state[3]

    def pass_on(self, after):
        bufs = _copies_wait(self.state, _plan_gather_ici, after, "gather_ici_wait_" + self.tag)
        self.state = _copies_start(bufs[self.n:], _plan_gather_d2d, 3 * self.n, bufs[0], "gather_d2d_start_" + self.tag)
        self.token = self.state[3]
        return self.token

    def done(self, after):
        return _copies_wait(self.state, _plan_gather_d2d, after, "gather_d2d_wait_" + self.tag)


class _ReduceJob:
    def __init__(self, tag, names, grads, sel, after):
        self.tag, self.names, self.n, self.sel = tag, names, len(names), sel
        lands = [lax.empty((g.shape[0], g.shape[1] // 2, g.shape[2]), F32) for g in grads]
        self.state = _copies_start(list(grads) + lands, _plan_rs_sibling, self.n, after, "rs_sibling_start_" + tag)
        self.token = self.state[3]

    def _chip_sum(self, name, g, other):
        if g.shape[0] == N_CHIPS:
            return _chip_sum(g, other, self.sel, "rs_chip_sum_" + name)
        wire, total = _chip_sum(g, other, self.sel, "rs_chip_sum_" + name, blocked=False)
        own = lax.switch(self.sel[1], [lambda t, p=p: _w_in_shard(t, p) for p in range(N_CHIPS)], total[0])
        return _w_in_shards(wire[0]), own

    def chip_sums(self, after):
        bufs = _copies_wait(self.state, _plan_rs_sibling, after, "rs_sibling_wait_" + self.tag)
        wires, self.owns = zip(*[self._chip_sum(name, bufs[i], bufs[self.n + i]) for i, name in enumerate(self.names)])
        lands = [lax.empty((3,) + w.shape[1:], BF16) for w in wires]
        self.state = _copies_start(list(wires) + lands, _plan_rs_chips, 3 * self.n, self.owns[0], "rs_chips_start_" + self.tag)
        self.token = self.state[3]
        return self.token

    def final_sums(self, after):
        bufs = _copies_wait(self.state, _plan_rs_chips, after, "rs_chips_wait_" + self.tag)
        sums = [_final_sum(self.owns[i], bufs[self.n + i], self.sel, "rs_final_" + name) for i, name in enumerate(self.names)]
        self.state = _copies_start(sums, _plan_rs_share, self.n, sums[0], "rs_share_start_" + self.tag)
        self.token = self.state[3]
        return self.token

    def done(self, after):
        return dict(zip(self.names, _copies_wait(self.state, _plan_rs_share, after, "rs_share_wait_" + self.tag)))


def _chip_blocks(g):
    return g if g.ndim == 3 else g.reshape(N_CHIPS, -1, g.shape[1])


class _StepComm(_NoComm):
    def __init__(self, big_weights, w_in0, sel, after):
        self.sel = sel
        self.small, self.grads, self.jobs = None, [dict() for _ in range(DEPTH)], {}
        self.jobs["in0"] = _GatherJob("in0", [w_in0], after)
        later = lax.optimization_barrier((tuple(big_weights), self.jobs["in0"].token.after))[0]
        self.jobs["rest0"] = _GatherJob("rest0", [w[0].astype(BF16) for w in later[1:]], self.jobs["in0"].token)
        layer1 = [w[1].astype(BF16) for w in later]
        self.jobs["all1"] = _GatherJob("all1", [_own_window(layer1[0], sel[1])] + layer1[1:], self.jobs["rest0"].token)

    def layer_weights(self, l):
        if l == 0:
            self.weights0 = _full_layer_weights(None, None, None, None, None, *self.small[0])
            return self.weights0
        g_in, g_br, g_out, g_f1, g_f2 = self.landed1
        return _full_layer_weights(g_in, g_br.reshape(D, D), g_out.reshape(D, D), g_f1, g_f2.reshape(D_FF, D), *self.small[1])

    def fwd_hook(self, l):
        if l != 0:
            return _no_hook

        def hook(point, after, ready=None):
            if point == "pre":
                job = self.jobs["in0"]
                started = after[:8, :128].astype(F32) + self.jobs["all1"].token.after
                self.weights0["w_all"] = _w_all_from_windows(job.done(job.pass_on(started))[0])
            if point == "attn":
                return self.jobs["rest0"].pass_on(after)
            if point == "ff_post":
                return self.jobs["all1"].pass_on(after)
            if point == "pool":
                g_br, g_out, g_f1, g_f2 = self.jobs["rest0"].done(after)
                self.weights0.update(w_branch=g_br.reshape(D, D), w_out=g_out.reshape(D, D), w_ff1=g_f1, w_ff2=g_f2.reshape(D_FF, D))
            if point == "end":
                self.landed1 = self.jobs["all1"].done(after)
            return None
        return hook

    def bwd_hook(self, l):
        if l != 0:
            return _no_hook

        def hook(point, after, ready=None):
            jobs = self.jobs
            if point == "ff_pre":
                token = jobs["rs1"].chip_sums(after)
                jobs["rs0_ff"] = _ReduceJob("0_ff", ("w_ff1", "w_ff2"), [_chip_blocks(ready[n]) for n in ("w_ff1", "w_ff2")], self.sel, token)
                return jobs["rs0_ff"].token
            if point == "cumf":
                return jobs["rs0_ff"].chip_sums(jobs["rs1"].final_sums(after))
            self.grads[1] = jobs["rs1"].done(after)
            return None
        return hook

    def grads_ready(self, l, big):
        if l == 1:
            self.jobs["rs1"] = _ReduceJob("1", _BIG, [_chip_blocks(big[n]) for n in _BIG], self.sel, self.sel)
            return self.jobs["rs1"].token
        names = ("w_in", "w_branch", "w_out")
        self.jobs["rs0_mix"] = _ReduceJob("0_mix", names, [_chip_blocks(big[n]) for n in names], self.sel, self.sel)
        return self.jobs["rs0_mix"].token

    def finish_sums(self, after):
        jobs = self.jobs
        token = jobs["rs0_mix"].chip_sums(after)
        return jobs["rs0_ff"].final_sums(token)

    def finish_ff(self, after):
        self.grads[0].update(self.jobs["rs0_ff"].done(after))

    def finish_mix(self, after):
        job = self.jobs["rs0_mix"]
        self.grads[0].update(job.done(job.final_sums(after)))


_SMALL = ("g_mix_pre", "g_mix_post", "g_ff_pre", "g_ff_post", "b_f", "w_pool", "pool_scale", "conv_w")


def _w_in_view(t):
    return t.reshape(DEPTH, D // 128, 128, _SHARD_COLS).transpose(3, 1, 0, 2).reshape(_SHARD_COLS * (D // 128) * DEPTH, 128)


def _w_in_unview(t):
    return t.reshape(_SHARD_COLS, D // 128, DEPTH, 128).transpose(2, 1, 3, 0).reshape(DEPTH, D, _SHARD_COLS)


def _pack(parts, rows=8):
    flat = jnp.concatenate([p.reshape(-1) for p in parts])
    width = -(-flat.shape[0] // (rows * 128)) * 128
    return jnp.pad(flat, (0, rows * width - flat.shape[0])).reshape(rows, width)


def _unpack(packed, like):
    flat = packed.reshape(-1)
    out, at = [], 0
    for ref in like:
        out.append(flat[at:at + ref.size].reshape(ref.shape))
        at += ref.size
    return out


def kernel(x, c, w_ada, b_ada, g_mix_pre, g_mix_post, g_ff_pre, g_ff_post, w_in, b_f, w_pool, pool_scale, conv_w, w_branch, w_out, w_ff1, w_ff2, loss_target, m_w_ada, m_b_ada, m_g_mix_pre, m_g_mix_post, m_g_ff_pre, m_g_ff_post, m_w_in, m_b_f, m_w_pool, m_pool_scale, m_conv_w, m_w_branch, m_w_out, m_w_ff1, m_w_ff2, v_w_ada, v_b_ada, v_g_mix_pre, v_g_mix_post, v_g_ff_pre, v_g_ff_post, v_w_in, v_b_f, v_w_pool, v_pool_scale, v_conv_w, v_w_branch, v_w_out, v_w_ff1, v_w_ff2):
    xi, yi, ci = lax.axis_index("x"), lax.axis_index("y"), lax.axis_index("c")
    chip = 2 * xi + yi
    dev = 2 * chip + ci
    n_ada = w_ada.shape[2]

    first = jnp.zeros((8, D + 384), F32).at[0, :D].set(c[0]).at[0, D:].set(conv_w.reshape(-1))
    w_in0 = _own_window(w_in[0].astype(BF16), chip)
    got = _allgather8(first, "gather_cond", after=(w_in0,)).reshape(N_DEV, 8, D + 384)[:, 0]
    c_all = got[:, :D]
    conv_full = got[0::2, D:].reshape(N_CHIPS, DEPTH, 3, CONV_WIDTH // N_CHIPS).transpose(1, 2, 0, 3).reshape(DEPTH, 3, CONV_WIDTH)

    b_loc = lax.dynamic_slice_in_dim(b_ada, chip * n_ada, n_ada, axis=1).reshape(DEPTH, 1, n_ada)
    mod_cols, silu_c = _ada_fwd(c_all, w_ada, b_loc)
    got = _allgather8(mod_cols.reshape(DEPTH * N_DEV, n_ada), "gather_mod").reshape(N_DEV, DEPTH, N_DEV, n_ada)[0::2]
    mod_all = got.transpose(1, 2, 0, 3).reshape(DEPTH, N_DEV, 6, D)
    mods = lax.dynamic_index_in_dim(mod_all, dev, axis=1, keepdims=False)

    comm = _StepComm((w_in, w_branch, w_out, w_ff1, w_ff2), w_in0, jnp.stack([ci, chip]).astype(jnp.int32), mods)
    comm.small = [(g_mix_pre[l], g_mix_post[l], g_ff_pre[l], g_ff_post[l], b_f[l], w_pool[l], pool_scale[l], conv_full[l]) for l in range(DEPTH)]
    loss_part, grad_x, dmods, bigs, smalls = _local_step(x[0], loss_target[0], mods, comm)

    small_parts = [smalls[l][name] for name in _SMALL for l in range(DEPTH)] + [loss_part.reshape(1)]
    packed = _tie(_pack([dmods] + small_parts), comm.jobs["rs0_mix"].token)
    gathered = _allgather8(packed, "gather_small")
    dmod_all = gathered.reshape(N_DEV, -1)[:, :dmods.size].reshape(N_DEV, DEPTH, 6 * D)
    summed = _unpack(_sum_devices(gathered), [dmods] + small_parts)
    grad_b_ada = summed[0].reshape(DEPTH, 6 * D)
    loss = summed[-1][0]
    small_grads = {name: jnp.stack(summed[1 + 2 * i:3 + 2 * i]) for i, name in enumerate(_SMALL)}
    small_grads["conv_w"] = lax.dynamic_slice_in_dim(small_grads["conv_w"], chip * (CONV_WIDTH // N_CHIPS), CONV_WIDTH // N_CHIPS, axis=2)

    dmod_loc = lax.dynamic_slice_in_dim(dmod_all.transpose(1, 0, 2), chip * n_ada, n_ada, axis=2)
    tail_token = comm.finish_sums(grad_b_ada)
    silu_pad = _tie(jnp.pad(silu_c, ((0, 128 - N_DEV), (0, 0))), tail_token)
    dmod_pad = jnp.pad(dmod_loc.transpose(1, 0, 2).reshape(N_DEV, DEPTH * n_ada), ((0, 128 - N_DEV), (0, 0)))
    grad_w_ada = _mm(silu_pad, dmod_pad, ta=True, out_split=DEPTH, name="mm_ada_dw")

    grads = dict(w_ada=grad_w_ada, b_ada=grad_b_ada, **small_grads)
    weights = dict(w_ada=w_ada, b_ada=b_ada, g_mix_pre=g_mix_pre, g_mix_post=g_mix_post, g_ff_pre=g_ff_pre, g_ff_post=g_ff_post, w_in=w_in,
                   b_f=b_f, w_pool=w_pool, pool_scale=pool_scale, conv_w=conv_w, w_branch=w_branch, w_out=w_out, w_ff1=w_ff1, w_ff2=w_ff2)
    m_in = dict(w_ada=m_w_ada, b_ada=m_b_ada, g_mix_pre=m_g_mix_pre, g_mix_post=m_g_mix_post, g_ff_pre=m_g_ff_pre, g_ff_post=m_g_ff_post,
                w_in=m_w_in, b_f=m_b_f, w_pool=m_w_pool, pool_scale=m_pool_scale, conv_w=m_conv_w, w_branch=m_w_branch, w_out=m_w_out,
                w_ff1=m_w_ff1, w_ff2=m_w_ff2)
    v_in = dict(w_ada=v_w_ada, b_ada=v_b_ada, g_mix_pre=v_g_mix_pre, g_mix_post=v_g_mix_post, g_ff_pre=v_g_ff_pre, g_ff_post=v_g_ff_post,
                w_in=v_w_in, b_f=v_b_f, w_pool=v_w_pool, pool_scale=v_pool_scale, conv_w=v_conv_w, w_branch=v_w_branch, w_out=v_w_out,
                w_ff1=v_w_ff1, w_ff2=v_w_ff2)
    order = ("w_ada", "b_ada", "g_mix_pre", "g_mix_post", "g_ff_pre", "g_ff_post", "w_in", "b_f", "w_pool", "pool_scale", "conv_w",
             "w_branch", "w_out", "w_ff1", "w_ff2")
    delta, new_m, new_v = {}, {}, {}
    tiny = ("b_ada",) + _SMALL
    tiny_g = [_tie(grads[tiny[0]], tail_token)] + [grads[name] for name in tiny[1:]]
    res = _adamw_many([weights[name] for name in tiny], tiny_g, [m_in[name] for name in tiny], [v_in[name] for name in tiny], "adamw_small")
    for out, vals in zip((delta, new_m, new_v), res):
        out.update(zip(tiny, vals))
    delta["w_ada"], new_m["w_ada"], new_v["w_ada"] = _adamw(w_ada, grad_w_ada, m_w_ada, v_w_ada, "adamw_w_ada")
    comm.finish_ff(delta["w_ada"][0, :8, :128] + delta["b_ada"][0, :128])
    for name in ("w_ff1", "w_ff2", "w_in", "w_branch", "w_out"):
        if name == "w_in":
            comm.finish_mix(delta["w_ff2"][0, :8, :128])
        g_layers = [comm.grads[l][name] for l in range(DEPTH)]
        if name == "w_in":
            g_view = lax.optimization_barrier(_w_in_view(jnp.stack(g_layers)))
            res = _adamw(_w_in_view(w_in), g_view, _w_in_view(m_w_in), _w_in_view(v_w_in), "adamw_w_in")
            grads[name], delta[name], new_m[name], new_v[name] = [_w_in_unview(t) for t in (g_view, *res)]
        else:
            delta[name], new_m[name], new_v[name], grads[name] = _adamw_layers(weights[name], g_layers, m_in[name], v_in[name], "adamw_" + name)

    return (loss, grad_x[None], *[grads[n] for n in order], *[delta[n] for n in order], *[new_m[n] for n in order],
            *[new_v[n] for n in order])
```

```python
from typing import NamedTuple

import jax
import jax.numpy as jnp
from jax import lax
from jax.experimental import pallas as pl
from jax.experimental.pallas import tpu as pltpu

F32 = jnp.float32
BF16 = jnp.bfloat16
MESH = pl.DeviceIdType.MESH

D = 1024
DEPTH = 2
HEADS = 8
HEAD_DIM = 64
A_WIDTH = 512
POOL_WIDTH = 256
CONV_WIDTH = 256
D_FF = 4096
IN_COLS = 5640
Z_GL, Z_QKV, Z_PC, Z_FL, Z_COLS = 0, 3072, 4608, 5632, 5760
RMS_EPS = 1e-6
NEG_INF = -1e30
ROW_TILE = 512
EW_ROWS = 256
N_CHIPS = 4
N_DEV = 8
V7X_VMEM_LIMIT = 48 * 1024 * 1024

ADAM_LR = 0.001
ADAM_B1 = 0.9
ADAM_B2 = 0.999
ADAM_EPS = 1e-08
ADAM_WD = 0.01
ADAM_STEP = 10

_HBM = pl.BlockSpec(memory_space=pltpu.HBM)


def _params(*sem):
    return pltpu.CompilerParams(dimension_semantics=sem, vmem_limit_bytes=V7X_VMEM_LIMIT)


def _pick(dim, cands):
    for cand in cands:
        if dim % cand == 0:
            return cand
    return dim


MM_TILE_BUDGET = 39 * 1024 * 1024


def _mm_tiles(m, n, k, k_unit, tn, a_size, b_size, out_size):
    for tk in (k_unit, 2048, 1152, 1024, 640, 512, 256, 128):
        if k_unit % tk:
            continue
        for tm in (2048, 1024, 512, 256, 128):
            if m % tm or ((m // tm) * (n // tn) < 2 and tm > 512):
                continue
            need = 2 * (tm * tk * a_size + tk * tn * b_size + tm * tn * out_size) + (0 if tk == k else 4 * tm * tn)
            if need <= MM_TILE_BUDGET and (tk == k_unit or tm >= 512):
                return tm, tk
    return 128, 128


def _mm(a, b, *, ta=False, tb=False, b_rows=None, b_split=1, out_split=1, out_dtype=F32, epilogue=None, extras=(), name):
    (k, m) = a.shape if ta else a.shape[::-1]
    b_row0, b_rows = (0, b.shape[-2]) if b_rows is None else b_rows
    b_cols = b.shape[-1] * b_split
    (n, k2) = (b_rows, b_cols) if tb else (b_cols, b_rows)
    assert k == k2, (a.shape, b.shape, ta, tb)
    n_unit = n // (out_split * (1 if tb else b_split))
    k_unit = k // (b_split if tb else 1)
    tn = _pick(n_unit, (1024, 1152, 768, 640, 512, 256, 128))
    tm, tk = _mm_tiles(m, n, k, k_unit, tn, a.dtype.itemsize, b.dtype.itemsize,
                       sum(jnp.dtype(dt).itemsize for dt in out_dtype) + 4 * len(extras) if epilogue else jnp.dtype(out_dtype).itemsize)
    nk = k // tk
    dims = (((0 if ta else 1,), (1 if tb else 0,)), ((), ()))

    def dot(a_ref, b_ref):
        b_val = b_ref[0] if b_split > 1 else b_ref[...]
        return lax.dot_general(a_ref[...].astype(BF16), b_val.astype(BF16), dims, preferred_element_type=F32)

    n_extra = len(extras)
    assert epilogue is None or out_split == 1

    def put(refs, val):
        if epilogue is not None:
            for o_ref, res in zip(refs[n_extra:], epilogue(val, *[r[...] for r in refs[:n_extra]])):
                o_ref[...] = res.astype(o_ref.dtype)
        elif out_split > 1:
            refs[0][0] = val.astype(refs[0].dtype)
        else:
            refs[0][...] = val.astype(refs[0].dtype)

    def body_single(a_ref, b_ref, *refs):
        put(refs, dot(a_ref, b_ref))

    def body_acc(a_ref, b_ref, *refs):
        kk = pl.program_id(2)
        acc_ref = refs[-1]

        @pl.when(kk == 0)
        def _():
            acc_ref[...] = jnp.zeros_like(acc_ref)

        acc_ref[...] += dot(a_ref, b_ref)

        @pl.when(kk == nk - 1)
        def _():
            put(refs[:-1], acc_ref[...])

    a_spec = pl.BlockSpec((tk, tm), lambda i, j, kk: (kk, i)) if ta else pl.BlockSpec((tm, tk), lambda i, j, kk: (i, kk))
    if b_split == 1:
        off = b_row0 // (tn if tb else tk)
        assert off * (tn if tb else tk) == b_row0
        b_spec = pl.BlockSpec((tn, tk), lambda i, j, kk: (j + off, kk)) if tb else pl.BlockSpec((tk, tn), lambda i, j, kk: (kk + off, j))
    elif tb:
        per = k_unit // tk
        b_spec = pl.BlockSpec((1, tn, tk), lambda i, j, kk: (kk // per, j, kk % per))
    else:
        per = n // b_split // tn
        b_spec = pl.BlockSpec((1, tk, tn), lambda i, j, kk: (j // per, kk, j % per))
    if out_split == 1:
        o_spec = pl.BlockSpec((tm, tn), lambda i, j, kk: (i, j))
        o_shape = None if epilogue is not None else jax.ShapeDtypeStruct((m, n), out_dtype)
    else:
        per_o = n // out_split // tn
        o_spec = pl.BlockSpec((1, tm, tn), lambda i, j, kk: (j // per_o, i, j % per_o))
        o_shape = jax.ShapeDtypeStruct((out_split, m, n // out_split), out_dtype)
    if epilogue is not None:
        o_shape = [jax.ShapeDtypeStruct((m, n), dt) for dt in out_dtype]
        o_spec = [o_spec] * len(out_dtype)
    return pl.pallas_call(
        body_single if nk == 1 else body_acc, name=name, grid=(m // tm, n // tn, nk),
        in_specs=[a_spec, b_spec] + [pl.BlockSpec((tm, tn), lambda i, j, kk: (i, j))] * n_extra, out_specs=o_spec, out_shape=o_shape,
        scratch_shapes=[] if nk == 1 else [pltpu.VMEM((tm, tn), F32)],
        compiler_params=_params("parallel", "parallel", "arbitrary"),
    )(a, b, *extras)


def _stacked_dw(lhs, rhs, name):
    n = rhs[0].shape[1]
    tr = min(a.shape[1] for a in lhs)
    tn = _pick(n, (512, 256, 128))
    counts = [a.shape[1] // tr for a in lhs]
    starts = [sum(counts[:i]) for i in range(len(lhs))]
    assert all(a.shape[1] == c * tr for a, c in zip(lhs, counts))
    n_lhs = len(lhs)

    def body(*refs):
        o_ref = refs[-1]
        j = pl.program_id(1)
        for a_ref, b_ref, start, count in zip(refs[:n_lhs], refs[n_lhs:-1], starts, counts):
            @pl.when((j >= start) & (j < start + count))
            def _():
                o_ref[...] = lax.dot_general(a_ref[...].astype(BF16), b_ref[...].astype(BF16), (((0,), (0,)), ((), ())),
                                             preferred_element_type=F32)

    def lhs_spec(a, start, count):
        return pl.BlockSpec((a.shape[0], tr), lambda c, j: (0, jnp.clip(j - start, 0, count - 1)))

    return pl.pallas_call(
        body, name=name, grid=(n // tn, sum(counts)),
        in_specs=[lhs_spec(a, st, ct) for a, st, ct in zip(lhs, starts, counts)]
        + [pl.BlockSpec((b.shape[0], tn), lambda c, j: (0, c)) for b in rhs],
        out_specs=pl.BlockSpec((tr, tn), lambda c, j: (j, c)),
        out_shape=jax.ShapeDtypeStruct((sum(counts) * tr, n), F32),
        compiler_params=_params("parallel", "parallel"),
    )(*lhs, *rhs)


def _ew(fn, ins, out_dtypes, name, tc=None):
    shape = ins[0].shape
    lead, (rows, cols) = shape[:-2], shape[-2:]
    tc = cols if tc is None else tc
    if tc > 1024:
        tr = _pick(rows, (EW_ROWS, 128, 8))
    elif tc > 128:
        tr = _pick(rows, (2 * EW_ROWS, EW_ROWS, 128, 8))
    else:
        tr = _pick(rows, (4096, 2256, 2048, 1024, EW_ROWS, 8))
    n_in = len(ins)

    def body(*refs):
        res = fn(*[r[...] for r in refs[:n_in]])
        for o_ref, val in zip(refs[n_in:], res):
            o_ref[...] = val.astype(o_ref.dtype)

    if lead:
        spec = pl.BlockSpec((None, tr, tc), lambda l, i, j: (l, i, j))
    else:
        spec = pl.BlockSpec((tr, tc), lambda i, j: (i, j))
    return pl.pallas_call(
        body, name=name, grid=lead + (rows // tr, cols // tc),
        in_specs=[spec] * n_in, out_specs=[spec] * len(out_dtypes),
        out_shape=[jax.ShapeDtypeStruct(shape, dt) for dt in out_dtypes],
        compiler_params=_params(*(["parallel"] * (len(lead) + 2))),
    )(*ins)


def _relu2_fwd(a):
    r = jnp.maximum(a, 0.0)
    return a, r * r


def _relu2_bwd(dr, a):
    return (dr * (2.0 * jnp.maximum(a, 0.0)),)


def _adamw_math(w, g, m, v):
    m = ADAM_B1 * m + (1.0 - ADAM_B1) * g
    v = ADAM_B2 * v + (1.0 - ADAM_B2) * (g * g)
    m_hat = m / (1.0 - ADAM_B1 ** ADAM_STEP)
    v_hat = v / (1.0 - ADAM_B2 ** ADAM_STEP)
    delta = -ADAM_LR * (m_hat / (jnp.sqrt(v_hat) + ADAM_EPS) + ADAM_WD * w)
    return delta, m, v


def _adamw(w, g, m, v, name):
    return _ew(_adamw_math, [w, g, m, v], [F32, F32, F32], name)


def _adamw_layers(w, g_layers, m, v, name):
    depth, rows, cols = w.shape
    tr = _pick(rows, (2 * EW_ROWS, EW_ROWS, 128, 8)) if cols <= 1024 else _pick(rows, (EW_ROWS, 128, 8))

    def body(w_ref, *refs):
        g_refs, (m_ref, v_ref, d_ref, mo_ref, vo_ref, go_ref) = refs[:depth], refs[depth:]
        layer = pl.program_id(0)
        g = g_refs[0][...]
        for l in range(1, depth):
            g = jnp.where(layer == l, g_refs[l][...], g)
        d_ref[...], mo_ref[...], vo_ref[...] = _adamw_math(w_ref[...], g, m_ref[...], v_ref[...])
        go_ref[...] = g

    spec = pl.BlockSpec((None, tr, cols), lambda l, i: (l, i, 0))
    g_specs = [pl.BlockSpec((tr, cols), lambda l, i, k=k: (jnp.where(l == k, i, 0), 0)) for k in range(depth)]
    return pl.pallas_call(
        body, name=name, grid=(depth, rows // tr),
        in_specs=[spec] + g_specs + [spec, spec], out_specs=[spec] * 4,
        out_shape=[jax.ShapeDtypeStruct(w.shape, F32)] * 4, compiler_params=_params("arbitrary", "arbitrary"),
    )(w, *g_layers, m, v)


def _adamw_many(ws, gs, ms, vs, name):
    n = len(ws)

    def body(*refs):
        for i in range(n):
            res = _adamw_math(*[refs[k * n + i][...] for k in range(4)])
            for k in range(3):
                refs[(4 + k) * n + i][...] = res[k]

    outs = pl.pallas_call(
        body, name=name, out_shape=[jax.ShapeDtypeStruct(w.shape, F32) for w in ws] * 3,
        compiler_params=pltpu.CompilerParams(vmem_limit_bytes=V7X_VMEM_LIMIT),
    )(*ws, *gs, *ms, *vs)
    return outs[:n], outs[n:2 * n], outs[2 * n:]


def _row_spec(cols, block=0):
    return pl.BlockSpec((ROW_TILE, cols), lambda i, block=block: (i, block))


def _vec_spec(cols):
    return pl.BlockSpec((1, cols), lambda i: (0, 0))


def _vec_args(*vecs):
    arrays = [v[0] if isinstance(v, tuple) else v for v in vecs]
    specs = [pl.BlockSpec((None, 1, D), lambda i, row=v[1]: (row, 0, 0)) if isinstance(v, tuple) else _vec_spec(D) for v in vecs]
    return arrays, specs


def _sum_spec(cols):
    return pl.BlockSpec((8, cols), lambda i: (0, 0))


def _rstd(x):
    return lax.rsqrt(jnp.mean(x * x, axis=-1, keepdims=True) + RMS_EPS)


def _modnorm_fwd(x, g, shift, scale, name):
    s = x.shape[0]

    def body(x_ref, g_ref, sh_ref, sc_ref, h_ref):
        xv = x_ref[...]
        n = xv * _rstd(xv)
        h_ref[...] = ((n * g_ref[...]) * (1.0 + sc_ref[...]) + sh_ref[...]).astype(BF16)

    vecs, vec_specs = _vec_args(g, shift, scale)
    return pl.pallas_call(
        body, name=name, grid=(s // ROW_TILE,),
        in_specs=[_row_spec(D)] + vec_specs, out_specs=_row_spec(D),
        out_shape=jax.ShapeDtypeStruct((s, D), BF16), compiler_params=_params("parallel"),
    )(x, *vecs)


def _post_fwd(x, y, g, gate, name):
    s = x.shape[0]

    def body(x_ref, y_ref, g_ref, gate_ref, o_ref):
        yv = y_ref[...]
        o_ref[...] = x_ref[...] + gate_ref[...] * ((yv * _rstd(yv)) * g_ref[...])

    vecs, vec_specs = _vec_args(g, gate)
    return pl.pallas_call(
        body, name=name, grid=(s // ROW_TILE,),
        in_specs=[_row_spec(D), _row_spec(D)] + vec_specs, out_specs=_row_spec(D),
        out_shape=jax.ShapeDtypeStruct((s, D), F32), compiler_params=_params("parallel"),
    )(x, y, *vecs)


def _post_bwd(dxo, y, g, gate, name):
    s = dxo.shape[0]

    def body(d_ref, y_ref, g_ref, gate_ref, dy_ref, sum_ref):
        @pl.when(pl.program_id(0) == 0)
        def _():
            sum_ref[...] = jnp.zeros_like(sum_ref)

        dv, yv = d_ref[...], y_ref[...]
        r = _rstd(yv)
        n = yv * r
        sum_ref[0:1, :] += jnp.sum(dv * (n * g_ref[...]), axis=0, keepdims=True)
        sum_ref[1:2, :] += jnp.sum((dv * gate_ref[...]) * n, axis=0, keepdims=True)
        dn = (dv * gate_ref[...]) * g_ref[...]
        dy_ref[...] = (r * (dn - n * jnp.mean(dn * n, axis=-1, keepdims=True))).astype(BF16)

    vecs, vec_specs = _vec_args(g, gate)
    return pl.pallas_call(
        body, name=name, grid=(s // ROW_TILE,),
        in_specs=[_row_spec(D), _row_spec(D)] + vec_specs,
        out_specs=[_row_spec(D), _sum_spec(D)],
        out_shape=[jax.ShapeDtypeStruct((s, D), BF16), jax.ShapeDtypeStruct((8, D), F32)],
        compiler_params=_params("arbitrary"),
    )(dxo, y, *vecs)


def _modnorm_bwd(dh, x, dxo, g, scale, name):
    s = dh.shape[0]

    def body(dh_ref, x_ref, d_ref, g_ref, sc_ref, dx_ref, sum_ref):
        @pl.when(pl.program_id(0) == 0)
        def _():
            sum_ref[...] = jnp.zeros_like(sum_ref)

        dhv, xv = dh_ref[...], x_ref[...]
        r = _rstd(xv)
        n = xv * r
        one_sc = 1.0 + sc_ref[...]
        sum_ref[0:1, :] += jnp.sum(dhv, axis=0, keepdims=True)
        sum_ref[1:2, :] += jnp.sum(dhv * (n * g_ref[...]), axis=0, keepdims=True)
        sum_ref[2:3, :] += jnp.sum((dhv * one_sc) * n, axis=0, keepdims=True)
        dn = (dhv * one_sc) * g_ref[...]
        dx_ref[...] = d_ref[...] + r * (dn - n * jnp.mean(dn * n, axis=-1, keepdims=True))

    vecs, vec_specs = _vec_args(g, scale)
    return pl.pallas_call(
        body, name=name, grid=(s // ROW_TILE,),
        in_specs=[_row_spec(D), _row_spec(D), _row_spec(D)] + vec_specs,
        out_specs=[_row_spec(D), _sum_spec(D)],
        out_shape=[jax.ShapeDtypeStruct((s, D), F32), jax.ShapeDtypeStruct((8, D), F32)],
        compiler_params=_params("arbitrary"),
    )(dh, x, dxo, *vecs)


def _post_pre_fwd(x, y, g_post, gate, g_pre, shift, scale, name):
    s = x.shape[0]

    def body(x_ref, y_ref, gp_ref, gate_ref, g_ref, sh_ref, sc_ref, o_ref, h_ref):
        yv = y_ref[...]
        xo = x_ref[...] + gate_ref[...] * ((yv * _rstd(yv)) * gp_ref[...])
        o_ref[...] = xo
        h_ref[...] = (((xo * _rstd(xo)) * g_ref[...]) * (1.0 + sc_ref[...]) + sh_ref[...]).astype(BF16)

    vecs, vec_specs = _vec_args(g_post, gate, g_pre, shift, scale)
    return pl.pallas_call(
        body, name=name, grid=(s // ROW_TILE,),
        in_specs=[_row_spec(D), _row_spec(D)] + vec_specs, out_specs=[_row_spec(D), _row_spec(D)],
        out_shape=[jax.ShapeDtypeStruct((s, D), F32), jax.ShapeDtypeStruct((s, D), BF16)], compiler_params=_params("parallel"),
    )(x, y, *vecs)


def _pre_post_bwd(dh, x, dxo, g_pre, scale, y, g_post, gate, name):
    s = dh.shape[0]

    def body(dh_ref, x_ref, d_ref, y_ref, g_ref, sc_ref, gp_ref, gate_ref, dx_ref, dy_ref, sum_ref):
        @pl.when(pl.program_id(0) == 0)
        def _():
            sum_ref[...] = jnp.zeros_like(sum_ref)

        dhv, xv = dh_ref[...], x_ref[...]
        r = _rstd(xv)
        n = xv * r
        one_sc = 1.0 + sc_ref[...]
        sum_ref[0:1, :] += jnp.sum(dhv, axis=0, keepdims=True)
        sum_ref[1:2, :] += jnp.sum(dhv * (n * g_ref[...]), axis=0, keepdims=True)
        sum_ref[2:3, :] += jnp.sum((dhv * one_sc) * n, axis=0, keepdims=True)
        dn = (dhv * one_sc) * g_ref[...]
        dv = d_ref[...] + r * (dn - n * jnp.mean(dn * n, axis=-1, keepdims=True))
        dx_ref[...] = dv

        yv = y_ref[...]
        ry = _rstd(yv)
        ny = yv * ry
        sum_ref[3:4, :] += jnp.sum(dv * (ny * gp_ref[...]), axis=0, keepdims=True)
        sum_ref[4:5, :] += jnp.sum((dv * gate_ref[...]) * ny, axis=0, keepdims=True)
        dny = (dv * gate_ref[...]) * gp_ref[...]
        dy_ref[...] = (ry * (dny - ny * jnp.mean(dny * ny, axis=-1, keepdims=True))).astype(BF16)

    vecs, vec_specs = _vec_args(g_pre, scale, g_post, gate)
    return pl.pallas_call(
        body, name=name, grid=(s // ROW_TILE,),
        in_specs=[_row_spec(D)] * 4 + vec_specs,
        out_specs=[_row_spec(D), _row_spec(D), _sum_spec(D)],
        out_shape=[jax.ShapeDtypeStruct((s, D), F32), jax.ShapeDtypeStruct((s, D), BF16), jax.ShapeDtypeStruct((8, D), F32)],
        compiler_params=_params("arbitrary"),
    )(dh, x, dxo, y, *vecs)


def _loss_head(y, target):
    s = y.shape[0]

    def body(y_ref, t_ref, dy_ref, sum_ref):
        @pl.when(pl.program_id(0) == 0)
        def _():
            sum_ref[...] = jnp.zeros_like(sum_ref)

        err = y_ref[...] - t_ref[...]
        dy_ref[...] = err * (1.0 / D)
        sum_ref[...] += jnp.sum(err * err)

    return pl.pallas_call(
        body, name="loss_head", grid=(s // ROW_TILE,),
        in_specs=[_row_spec(D), _row_spec(D)],
        out_specs=[_row_spec(D), pl.BlockSpec((8, 128), lambda i: (0, 0))],
        out_shape=[jax.ShapeDtypeStruct((s, D), F32), jax.ShapeDtypeStruct((8, 128), F32)],
        compiler_params=_params("arbitrary"),
    )(y, target)


def _merge_fwd(z, pa, pb, pc):
    s = z.shape[0]

    def body(g0_ref, g1_ref, g2_ref, pa_ref, pb_ref, pc_ref, o_ref):
        o_ref[...] = (jax.nn.sigmoid(g0_ref[...]) * pa_ref[...] + jax.nn.sigmoid(g1_ref[...]) * pb_ref[...]
                      + jax.nn.sigmoid(g2_ref[...]) * pc_ref[...]).astype(BF16)

    return pl.pallas_call(
        body, name="merge_fwd", grid=(s // ROW_TILE,),
        in_specs=[_row_spec(D, 0), _row_spec(D, 1), _row_spec(D, 2), _row_spec(D), _row_spec(D), _row_spec(D)],
        out_specs=_row_spec(D), out_shape=jax.ShapeDtypeStruct((s, D), BF16),
        compiler_params=_params("parallel"),
    )(z, z, z, pa, pb, pc)


def _merge_bwd(dm, z, pa, pb, pc):
    s = z.shape[0]

    def body(dm_ref, g0_ref, g1_ref, g2_ref, pa_ref, pb_ref, pc_ref, dgl_ref, da_ref, db_ref, dc_ref):
        dmv = dm_ref[...]
        for i, (g_ref, p_ref, d_ref) in enumerate(((g0_ref, pa_ref, da_ref), (g1_ref, pb_ref, db_ref), (g2_ref, pc_ref, dc_ref))):
            gate = jax.nn.sigmoid(g_ref[...])
            dgl_ref[:, i * D:(i + 1) * D] = ((dmv * p_ref[...]) * (gate * (1.0 - gate))).astype(BF16)
            d_ref[...] = (dmv * gate).astype(BF16)

    return pl.pallas_call(
        body, name="merge_bwd", grid=(s // ROW_TILE,),
        in_specs=[_row_spec(D), _row_spec(D, 0), _row_spec(D, 1), _row_spec(D, 2), _row_spec(D), _row_spec(D), _row_spec(D)],
        out_specs=[_row_spec(3 * D), _row_spec(D), _row_spec(D), _row_spec(D)],
        out_shape=[jax.ShapeDtypeStruct((s, Z_COLS), BF16)] + [jax.ShapeDtypeStruct((s, D), BF16)] * 3,
        compiler_params=_params("parallel"),
    )(dm, z, z, z, pa, pb, pc)


def _shift_down(v, n):
    row = lax.broadcasted_iota(jnp.int32, v.shape, 0)
    return jnp.where(row >= n, pltpu.roll(v, n, axis=0), 0.0)


def _shift_up(v, n):
    s = v.shape[0]
    row = lax.broadcasted_iota(jnp.int32, v.shape, 0)
    return jnp.where(row < s - n, pltpu.roll(v, s - n, axis=0), 0.0)


def _log_sigmoid(v):
    return jnp.minimum(v, 0.0) - jnp.log1p(jnp.exp(-jnp.abs(v)))


def _cumf_fwd(fl, bias):
    s = fl.shape[0]

    def body(fl_ref, b_ref, o_ref):
        acc = _log_sigmoid(fl_ref[...] + b_ref[...])
        step = 1
        while step < s:
            acc = acc + _shift_down(acc, step)
            step *= 2
        o_ref[...] = acc

    return pl.pallas_call(body, name="cumf_fwd", out_shape=jax.ShapeDtypeStruct((s, 128), F32),
                          compiler_params=pltpu.CompilerParams(vmem_limit_bytes=V7X_VMEM_LIMIT))(fl, bias)


def _cumf_bwd(dcum, fl, bias):
    s = fl.shape[0]

    def body(d_ref, fl_ref, b_ref, dfl_ref, db_ref):
        acc = d_ref[...]
        step = 1
        while step < s:
            acc = acc + _shift_up(acc, step)
            step *= 2
        dfl = acc * jax.nn.sigmoid(-(fl_ref[...] + b_ref[...]))
        dfl_ref[...] = dfl.astype(BF16)
        db_ref[...] = jnp.broadcast_to(jnp.sum(dfl, axis=0, keepdims=True), (8, 128))

    return pl.pallas_call(
        body, name="cumf_bwd",
        out_shape=[jax.ShapeDtypeStruct((s, 128), BF16), jax.ShapeDtypeStruct((8, 128), F32)],
        compiler_params=pltpu.CompilerParams(vmem_limit_bytes=V7X_VMEM_LIMIT))(dcum, fl, bias)


def _pool_windows(v, shift):
    s2 = v + shift(v, 1)
    s4 = s2 + shift(s2, 2)
    s8 = s4 + shift(s4, 4)
    s16 = s8 + shift(s8, 8)
    group = lax.broadcasted_iota(jnp.int32, v.shape, 1) // 64
    return jnp.where(group == 0, s2, jnp.where(group == 1, s4, jnp.where(group == 2, s8, s16)))


def _pool_count(shape):
    group = lax.broadcasted_iota(jnp.int32, shape, 1) // 64
    window = jnp.where(group == 0, 2.0, jnp.where(group == 1, 4.0, jnp.where(group == 2, 8.0, 16.0)))
    t1 = (lax.broadcasted_iota(jnp.int32, shape, 0) + 1).astype(F32)
    return jnp.minimum(t1, window)


def _pc_specs(s):
    zcol = lambda blk: pl.BlockSpec((s, 256), lambda i, blk=blk: (0, blk))
    first = Z_PC // 256
    return [zcol(first), zcol(first + 1), zcol(first + 2), zcol(first + 3),
            pl.BlockSpec((256, 256), lambda i: (0, 0)), pl.BlockSpec((1, 256), lambda i: (0, 0)),
            pl.BlockSpec((3, 256), lambda i: (0, 0))]


def _poolconv_fwd(z, wbd, pscale, convw):
    s = z.shape[0]

    def body(pu_ref, ch_ref, cb_ref, cc_ref, w_ref, ps_ref, cw_ref, yb_ref, yc_ref):
        u = pu_ref[...]
        p = _pool_windows(u, _shift_down) / _pool_count(u.shape) - u
        yb = jnp.dot(p.astype(BF16), w_ref[...].astype(BF16), preferred_element_type=F32) * ps_ref[...]
        yb_ref[...] = yb.astype(BF16)
        uc = cc_ref[...] * ch_ref[...]
        cw = cw_ref[...]
        conv = cw[0:1, :] * _shift_down(uc, 2) + cw[1:2, :] * _shift_down(uc, 1) + cw[2:3, :] * uc
        yc_ref[...] = (cb_ref[...] * conv).astype(BF16)

    out = pl.BlockSpec((s, 256), lambda i: (0, 0))
    return pl.pallas_call(
        body, name="poolconv_fwd", grid=(1,), in_specs=_pc_specs(s), out_specs=[out, out],
        out_shape=[jax.ShapeDtypeStruct((s, 256), BF16)] * 2, compiler_params=_params("arbitrary"),
    )(z, z, z, z, wbd, pscale, convw)


def _poolconv_bwd(dyb, dyc, z, wbd, pscale, convw):
    s = z.shape[0]

    def body(dyb_ref, dyc_ref, pu_ref, ch_ref, cb_ref, cc_ref, w_ref, ps_ref, cw_ref, dz_ref, dw_ref, dps_ref, dcw_ref):
        u = pu_ref[...]
        count = _pool_count(u.shape)
        p = (_pool_windows(u, _shift_down) / count - u).astype(BF16)
        wb = w_ref[...].astype(BF16)
        dyb_v = dyb_ref[...]
        pw = jnp.dot(p, wb, preferred_element_type=F32)
        dps_ref[...] = jnp.broadcast_to(jnp.sum(dyb_v * pw, axis=0, keepdims=True), (8, 256))
        dys = (dyb_v * ps_ref[...]).astype(BF16)
        dp = lax.dot_general(dys, wb, (((1,), (1,)), ((), ())), preferred_element_type=F32)
        dw_ref[...] = lax.dot_general(p, dys, (((0,), (0,)), ((), ())), preferred_element_type=F32)
        dz_ref[:, 0:256] = (_pool_windows(dp / count, _shift_up) - dp).astype(BF16)

        ch, cb, cc = ch_ref[...], cb_ref[...], cc_ref[...]
        uc = cc * ch
        cw = cw_ref[...]
        u2, u1 = _shift_down(uc, 2), _shift_down(uc, 1)
        conv = cw[0:1, :] * u2 + cw[1:2, :] * u1 + cw[2:3, :] * uc
        dyc_v = dyc_ref[...]
        dconv = dyc_v * cb
        du = cw[0:1, :] * _shift_up(dconv, 2) + cw[1:2, :] * _shift_up(dconv, 1) + cw[2:3, :] * dconv
        dz_ref[:, 256:512] = (du * cc).astype(BF16)
        dz_ref[:, 512:768] = (dyc_v * conv).astype(BF16)
        dz_ref[:, 768:1024] = (du * ch).astype(BF16)
        dcw_ref[...] = jnp.zeros_like(dcw_ref)
        dcw_ref[0:1, :] = jnp.sum(dconv * u2, axis=0, keepdims=True)
        dcw_ref[1:2, :] = jnp.sum(dconv * u1, axis=0, keepdims=True)
        dcw_ref[2:3, :] = jnp.sum(dconv * uc, axis=0, keepdims=True)

    blk = lambda r, c: pl.BlockSpec((r, c), lambda i: (0, 0))
    return pl.pallas_call(
        body, name="poolconv_bwd", grid=(1,),
        in_specs=[blk(s, 256), blk(s, 256)] + _pc_specs(s),
        out_specs=[blk(s, 1024), blk(256, 256), blk(8, 256), blk(8, 256)],
        out_shape=[jax.ShapeDtypeStruct((s, 1024), BF16), jax.ShapeDtypeStruct((256, 256), F32),
                   jax.ShapeDtypeStruct((8, 256), F32), jax.ShapeDtypeStruct((8, 256), F32)],
        compiler_params=_params("arbitrary"),
    )(dyb, dyc, z, z, z, z, wbd, pscale, convw)


_NT = (((1,), (1,)), ((), ()))
_TN = (((0,), (0,)), ((), ()))


ATT_Q, ATT_K = 256, 256
ATT_HEADS_BWD = 8
ATT_HEADS = 8


def _att_logits(q, k, fr, q0, k0, masked):
    logits = lax.dot_general(q, k, _NT, preferred_element_type=F32) - fr
    if not masked:
        return logits
    row = q0 + lax.broadcasted_iota(jnp.int32, logits.shape, 0)
    col = k0 + lax.broadcasted_iota(jnp.int32, logits.shape, 1)
    return jnp.where(row >= col, logits, NEG_INF)


def _causal_sweep(step, qi, init):
    n_full = (qi * ATT_Q) // ATT_K
    carry = lax.fori_loop(0, n_full, lambda j, carry: step(j, carry, False), init)
    return step(n_full, carry, True)


HEAD_PAIRS = HEADS // 2


def _lane_pick(v, lane, idx):
    return jnp.sum(jnp.where(lane == idx, v, 0.0), axis=-1, keepdims=True)


def _lane_put(lane, idx, col):
    return jnp.where(lane == idx, col, 0.0)


def _split_heads(v, low):
    zero = jnp.zeros_like(v)
    return jnp.where(low, v, zero), jnp.where(low, zero, v)


def _attn_fwd(qkv, fr):
    s = qkv.shape[0]
    nk = s // ATT_K
    width = ATT_HEADS * HEAD_DIM
    groups = HEADS // ATT_HEADS

    def body(q_ref, k_ref, v_ref, fr_ref, o_ref, lse_ref):
        qi, grp = pl.program_id(0), pl.program_id(1)
        lane = lax.broadcasted_iota(jnp.int32, (ATT_Q, 128), 1)
        low = lane < HEAD_DIM
        qs = []
        for pr in range(ATT_HEADS // 2):
            qs += _split_heads(q_ref[:, 128 * pr:128 * (pr + 1)] * (HEAD_DIM ** -0.5), low)

        def step(j, carry, masked):
            k0 = pl.multiple_of(j * ATT_K, ATT_K)
            out = []
            for h in range(ATT_HEADS):
                cols = slice(128 * (h // 2), 128 * (h // 2 + 1))
                m, l, acc = carry[h]
                logits = _att_logits(qs[h], k_ref[pl.ds(k0, ATT_K), cols], fr_ref[h, pl.ds(j, 1), :], qi * ATT_Q, k0, masked)
                m_new = jnp.maximum(m, jnp.max(logits, axis=-1, keepdims=True))
                p = jnp.exp(logits - m_new)
                alpha = jnp.exp(m - m_new)
                l = alpha * l + jnp.sum(p, axis=-1, keepdims=True)
                acc = alpha * acc + jnp.dot(p.astype(BF16), v_ref[pl.ds(k0, ATT_K), cols], preferred_element_type=F32)
                out.append((m_new, l, acc))
            return tuple(out)

        one = (jnp.full((ATT_Q, 1), NEG_INF, F32), jnp.zeros((ATT_Q, 1), F32), jnp.zeros((ATT_Q, 128), F32))
        done = _causal_sweep(step, qi, (one,) * ATT_HEADS)

        @pl.when(grp == 0)
        def _():
            lse_ref[...] = jnp.zeros_like(lse_ref)

        lse = jnp.zeros((ATT_Q, 128), F32)
        for pr in range(ATT_HEADS // 2):
            (m0, l0, acc0), (m1, l1, acc1) = done[2 * pr], done[2 * pr + 1]
            o_ref[:, 128 * pr:128 * (pr + 1)] = jnp.where(low, acc0 / l0, acc1 / l1)
            head = ATT_HEADS * grp + 2 * pr
            lse = lse + _lane_put(lane, head, m0 + jnp.log(l0)) + _lane_put(lane, head + 1, m1 + jnp.log(l1))
        lse_ref[...] += lse

    return pl.pallas_call(
        body, name="attn_fwd", grid=(s // ATT_Q, groups),
        in_specs=[pl.BlockSpec((ATT_Q, width), lambda i, g: (i, g)),
                  pl.BlockSpec((s, width), lambda i, g: (0, groups + g)),
                  pl.BlockSpec((s, width), lambda i, g: (0, 2 * groups + g)),
                  pl.BlockSpec((ATT_HEADS, nk, ATT_K), lambda i, g: (g, 0, 0))],
        out_specs=[pl.BlockSpec((ATT_Q, width), lambda i, g: (i, g)), pl.BlockSpec((ATT_Q, 128), lambda i, g: (i, 0))],
        out_shape=[jax.ShapeDtypeStruct((s, A_WIDTH), F32), jax.ShapeDtypeStruct((s, 128), F32)],
        compiler_params=_params("parallel", "arbitrary"),
    )(qkv, qkv, qkv, fr)


def _attn_bwd(qkv, do, o, lse, fr):
    s = qkv.shape[0]
    nk = s // ATT_K
    scale = HEAD_DIM ** -0.5
    heads = ATT_HEADS_BWD
    width = heads * HEAD_DIM
    groups = HEADS // heads

    def body(q_ref, k_ref, v_ref, do_ref, o_ref, lse_ref, fr_ref, dq_ref, dk_ref, dv_ref, dfc_ref, dfr_ref, dk_acc, dv_acc):
        grp = pl.program_id(0)
        lane = lax.broadcasted_iota(jnp.int32, (ATT_Q, 128), 1)
        low = lane < HEAD_DIM
        low_t = lax.broadcasted_iota(jnp.int32, (128, ATT_Q), 0) < HEAD_DIM
        dk_acc[...] = jnp.zeros_like(dk_acc)
        dv_acc[...] = jnp.zeros_like(dv_acc)
        dfr_ref[...] = jnp.zeros_like(dfr_ref)

        @pl.when(grp == 0)
        def _():
            dfc_ref[...] = jnp.zeros_like(dfc_ref)

        def outer(i, carry):
            q0 = pl.multiple_of(i * ATT_Q, ATT_Q)
            rows = pl.ds(q0, ATT_Q)
            lsev = lse_ref[rows, :]
            qts, dots, qs, dos, deltas, lses = [], [], [], [], [], []
            for pr in range(heads // 2):
                pcols = slice(128 * pr, 128 * (pr + 1))
                q2, do2 = q_ref[rows, pcols] * scale, do_ref[rows, pcols]
                prod = do2 * o_ref[rows, pcols]
                deltas += [jnp.sum(jnp.where(low, prod, 0.0), axis=-1, keepdims=True),
                           jnp.sum(jnp.where(low, 0.0, prod), axis=-1, keepdims=True)]
                dob2 = do2.astype(BF16)
                qts += _split_heads(q2.astype(F32).T.astype(BF16), low_t)
                dots += _split_heads(do2.T.astype(BF16), low_t)
                qs += _split_heads(q2, low)
                dos += _split_heads(dob2, low)
                lses += [_lane_pick(lsev, lane, heads * grp + 2 * pr), _lane_pick(lsev, lane, heads * grp + 2 * pr + 1)]

            def inner(j, carry, masked):
                k0 = pl.multiple_of(j * ATT_K, ATT_K)
                krows = pl.ds(k0, ATT_K)
                out, dkt, dvt = [], [], []
                for h in range(heads):
                    pcols = slice(128 * (h // 2), 128 * (h // 2 + 1))
                    dq, dfc = carry[h]
                    k2 = k_ref[krows, pcols]
                    p = jnp.exp(_att_logits(qs[h], k2, fr_ref[h, pl.ds(j, 1), :], q0, k0, masked) - lses[h])
                    dp = lax.dot_general(dos[h], v_ref[krows, pcols], _NT, preferred_element_type=F32)
                    ds = p * (dp - deltas[h])
                    dsb = ds.astype(BF16)
                    dkt.append(jnp.dot(qts[h], dsb, preferred_element_type=F32))
                    dvt.append(jnp.dot(dots[h], p.astype(BF16), preferred_element_type=F32))
                    dfr_ref[h, pl.ds(j, 1), :] -= jnp.sum(ds, axis=0, keepdims=True)
                    out.append((dq + jnp.dot(dsb, k2, preferred_element_type=F32), dfc + (ds[:, :128] + ds[:, 128:])))
                for pr in range(heads // 2):
                    prows = slice(128 * pr, 128 * (pr + 1))
                    dk_acc[j, prows, :] += dkt[2 * pr] + dkt[2 * pr + 1]
                    dv_acc[j, prows, :] += dvt[2 * pr] + dvt[2 * pr + 1]
                return tuple(out)

            one = (jnp.zeros((ATT_Q, 128), F32), jnp.zeros((ATT_Q, 128), F32))
            done = _causal_sweep(inner, i, (one,) * heads)
            dfc = jnp.zeros((ATT_Q, 128), F32)
            for pr in range(heads // 2):
                (dq0, dfc0), (dq1, dfc1) = done[2 * pr], done[2 * pr + 1]
                dq_ref[rows, 128 * pr:128 * (pr + 1)] = (jnp.where(low, dq0, dq1) * scale).astype(BF16)
                head = heads * grp + 2 * pr
                dfc = (dfc + _lane_put(lane, head, jnp.sum(dfc0, axis=-1, keepdims=True))
                       + _lane_put(lane, head + 1, jnp.sum(dfc1, axis=-1, keepdims=True)))
            dfc_ref[rows, :] += dfc
            return carry

        lax.fori_loop(0, s // ATT_Q, outer, 0)
        for j in range(nk):
            for pr in range(heads // 2):
                prows, pcols = slice(128 * pr, 128 * (pr + 1)), slice(128 * pr, 128 * (pr + 1))
                dk_ref[ATT_K * j:ATT_K * (j + 1), pcols] = dk_acc[j, prows, :].T.astype(BF16)
                dv_ref[ATT_K * j:ATT_K * (j + 1), pcols] = dv_acc[j, prows, :].T.astype(BF16)

    part = lambda first: pl.BlockSpec((s, width), lambda g, first=first: (0, first + g))
    whole = pl.BlockSpec((s, 128), lambda g: (0, 0))
    rowv = pl.BlockSpec((heads, nk, ATT_K), lambda g: (g, 0, 0))
    return pl.pallas_call(
        body, name="attn_bwd", grid=(groups,),
        in_specs=[part(0), part(groups), part(2 * groups), part(0), part(0), whole, rowv],
        out_specs=[part(0), part(0), part(0), whole, rowv],
        out_shape=[jax.ShapeDtypeStruct((s, A_WIDTH), BF16)] * 3 + [jax.ShapeDtypeStruct((s, 128), F32), jax.ShapeDtypeStruct((HEADS, nk, ATT_K), F32)],
        scratch_shapes=[pltpu.VMEM((nk, width, ATT_K), F32), pltpu.VMEM((nk, width, ATT_K), F32)],
        compiler_params=_params("arbitrary"),
    )(qkv, qkv, qkv, do, o, lse, fr)


def _ada_fwd(c_all, w_ada, b_loc):
    depth, _, n = w_ada.shape
    tn = 512

    def body(c_ref, w_ref, b_ref, o_ref, sc_ref):
        cv = c_ref[...]
        sc = cv * jax.nn.sigmoid(cv)
        sc_ref[...] = sc
        o_ref[0] = jnp.dot(sc.astype(BF16), w_ref[0].astype(BF16), preferred_element_type=F32) + b_ref[0]

    return pl.pallas_call(
        body, name="ada_fwd", grid=(depth, n // tn),
        in_specs=[pl.BlockSpec((N_DEV, D), lambda l, j: (0, 0)), pl.BlockSpec((1, D, tn), lambda l, j: (l, 0, j)),
                  pl.BlockSpec((1, 1, tn), lambda l, j: (l, 0, j))],
        out_specs=[pl.BlockSpec((1, N_DEV, tn), lambda l, j: (l, 0, j)), pl.BlockSpec((N_DEV, D), lambda l, j: (0, 0))],
        out_shape=[jax.ShapeDtypeStruct((depth, N_DEV, n), F32), jax.ShapeDtypeStruct((N_DEV, D), F32)],
        compiler_params=_params("arbitrary", "arbitrary"),
    )(c_all, w_ada, b_loc)


def _sum_devices(gathered):
    n = gathered.shape[1]
    tn = _pick(n, (1408, 1024, 640, 512, 128))

    def body(g_ref, o_ref):
        acc = g_ref[0:8, :]
        for dev in range(1, N_DEV):
            acc = acc + g_ref[8 * dev:8 * dev + 8, :]
        o_ref[...] = acc

    return pl.pallas_call(
        body, name="sum_devices", grid=(n // tn,),
        in_specs=[pl.BlockSpec((8 * N_DEV, tn), lambda j: (0, j))], out_specs=pl.BlockSpec((8, tn), lambda j: (0, j)),
        out_shape=jax.ShapeDtypeStruct((8, n), F32), compiler_params=_params("parallel"),
    )(gathered)


def _place():
    x, y, c = lax.axis_index("x"), lax.axis_index("y"), lax.axis_index("c")
    chips = [(1 - x, y), (x, 1 - y), (1 - x, 1 - y)]
    return x, y, c, chips


def _allgather8(block, name, after=()):
    m_per, n = block.shape

    def body(x_ref, *rest):
        out_ref, send_sems, recv_sems, local_sem = rest[len(after):]
        x, y, c, chips = _place()
        me, sibling = (x, y, c), (x, y, 1 - c)

        def rows(px, py, pc):
            return out_ref.at[pl.ds((4 * px + 2 * py + pc) * m_per, m_per), :]

        def copy(k, blk, to, src=None):
            return pltpu.make_async_remote_copy(
                src_ref=rows(*blk) if src is None else src, dst_ref=rows(*blk),
                send_sem=send_sems.at[k], recv_sem=recv_sems.at[k], device_id=to, device_id_type=MESH)

        mine = pltpu.make_async_copy(x_ref, rows(*me), local_sem)
        mine.start()
        first = [copy(0, me, sibling, src=x_ref)]
        first += [copy(1 + j, me, (*chip, c), src=x_ref) for j, chip in enumerate(chips)]
        for cp in first:
            cp.start()
        passed = [copy(4 + j, (*chip, c), sibling) for j, chip in enumerate(chips)]
        for j, chip in enumerate(chips):
            copy(1 + j, (*chip, c), me).wait_recv()
            passed[j].start()
        copy(0, sibling, me).wait_recv()
        for j, chip in enumerate(chips):
            copy(4 + j, (*chip, 1 - c), me).wait_recv()
        for cp in first + passed:
            cp.wait_send()
        mine.wait()

    return pl.pallas_call(
        body, name=name, out_shape=jax.ShapeDtypeStruct((N_DEV * m_per, n), block.dtype),
        in_specs=[pl.BlockSpec(memory_space=pltpu.VMEM)] + [pl.BlockSpec(memory_space=pl.ANY)] * len(after),
        out_specs=pl.BlockSpec(memory_space=pltpu.VMEM),
        scratch_shapes=[pltpu.SemaphoreType.DMA((7,)), pltpu.SemaphoreType.DMA((7,)), pltpu.SemaphoreType.DMA],
        compiler_params=pltpu.CompilerParams(vmem_limit_bytes=V7X_VMEM_LIMIT),
    )(block, *after)


_SEM = pl.BlockSpec(memory_space=pltpu.SEMAPHORE)
_DATAFLOW = pltpu.SideEffectType.DATAFLOW_SIDE_EFFECTING


def _plan_copies(plan, refs, send_sems, recv_sems):
    return [pltpu.make_async_remote_copy(src_ref=src, dst_ref=dst, send_sem=send_sems.at[i], recv_sem=recv_sems.at[i],
                                         device_id=to, device_id_type=MESH) for i, (src, dst, to) in enumerate(plan(refs))]


class _Token(NamedTuple):
    after: jax.Array
    tie: jax.Array


def _after_operand(after):
    return after.after if isinstance(after, _Token) else after


def _copies_start(bufs, plan, n_copies, after, name):
    nb = len(bufs)

    def body(*refs):
        for cp in _plan_copies(plan, refs[:nb], refs[nb + 1], refs[nb + 2]):
            cp.start()
        for token in refs[-2:]:
            token[...] = jnp.zeros_like(token)

    sem = pltpu.SemaphoreType.DMA((n_copies,))
    vmem = pl.BlockSpec(memory_space=pltpu.VMEM)
    outs = pl.pallas_call(
        body, name=name,
        out_shape=(sem, sem, *[pltpu.HBM(b.shape, b.dtype) for b in bufs], jax.ShapeDtypeStruct((8, 128), F32),
                   jax.ShapeDtypeStruct((1, 1), F32)),
        in_specs=[_HBM] * nb + [pl.BlockSpec(memory_space=pl.ANY)],
        out_specs=(_SEM, _SEM, *[_HBM] * nb, vmem, vmem),
        input_output_aliases={i: 2 + i for i in range(nb)},
        compiler_params=pltpu.CompilerParams(has_side_effects=_DATAFLOW),
    )(*[pltpu.with_memory_space_constraint(b, pltpu.HBM) for b in bufs], _after_operand(after))
    return outs[0], outs[1], list(outs[2:2 + nb]), _Token(outs[-2], outs[-1])


def _copies_wait(started, plan, after, name):
    send_sems, recv_sems, bufs, _ = started
    nb = len(bufs)

    def body(*refs):
        for cp in _plan_copies(plan, refs[:nb], refs[nb], refs[nb + 1]):
            cp.wait_send()
            cp.wait_recv()

    return list(pl.pallas_call(
        body, name=name, out_shape=tuple(pltpu.HBM(b.shape, b.dtype) for b in bufs),
        in_specs=[_HBM] * nb + [_SEM, _SEM, pl.BlockSpec(memory_space=pl.ANY)], out_specs=tuple([_HBM] * nb),
        input_output_aliases={i: i for i in range(nb)},
        compiler_params=pltpu.CompilerParams(has_side_effects=_DATAFLOW),
    )(*bufs, send_sems, recv_sems, _after_operand(after)))


def _half_rows(ref, axis, c):
    half = ref.shape[axis] // 2
    return pl.ds(c * half, half)


def _plan_gather_ici(refs):
    n = len(refs) // 2
    x, y, c, chips = _place()
    out = []
    for a in range(n):
        rows = _half_rows(refs[a], 0, c)
        out += [(refs[a].at[rows], refs[n + a].at[2 * x + y, rows], (*chip, c)) for chip in chips]
        out.append((refs[a], refs[n + a].at[2 * x + y], (x, y, 1 - c)))
    return out


def _plan_gather_d2d(refs):
    x, y, c, chips = _place()
    out = []
    for ref in refs:
        rows = _half_rows(ref, 1, c)
        for px, py in chips:
            landed = ref.at[2 * px + py, rows]
            out.append((landed, landed, (x, y, 1 - c)))
    return out


def _plan_rs_sibling(refs):
    n = len(refs) // 2
    x, y, c, _ = _place()
    return [(refs[a].at[pl.ds(0, refs[a].shape[0]), _half_rows(refs[a], 1, 1 - c)], refs[n + a], (x, y, 1 - c)) for a in range(n)]


def _plan_rs_chips(refs):
    n = len(refs) // 2
    x, y, c, chips = _place()
    return [(refs[a].at[2 * px + py], refs[n + a].at[k], (px, py, c)) for a in range(n) for k, (px, py) in enumerate(chips)]


def _plan_rs_share(refs):
    x, y, c, _ = _place()
    return [(ref.at[_half_rows(ref, 0, c)], ref.at[_half_rows(ref, 0, c)], (x, y, 1 - c)) for ref in refs]


def _chip_sum(g, other, sel, name, blocked=True):
    nblk, half, cdim = other.shape
    tr = _pick(half, (512, 256, 128, 64) if nblk > 1 else (128, 64))
    per = half // tr

    def body(sel_ref, g_ref, t_ref, wire_ref, own_ref):
        total = g_ref[0] + t_ref[0]
        wire_ref[0] = total.astype(BF16)
        if blocked:
            @pl.when(pl.program_id(1) == sel_ref[1])
            def _():
                own_ref[...] = total
        else:
            own_ref[0] = total

    blk = pl.BlockSpec((1, tr, cdim), lambda i, p, sel_ref: (p, i, 0))
    own_spec = pl.BlockSpec((tr, cdim), lambda i, p, sel_ref: (i, 0)) if blocked else blk
    own_shape = jax.ShapeDtypeStruct((half, cdim) if blocked else other.shape, F32)
    return pl.pallas_call(
        body, name=name,
        grid_spec=pltpu.PrefetchScalarGridSpec(
            num_scalar_prefetch=1, grid=(per, nblk),
            in_specs=[pl.BlockSpec((1, tr, cdim), lambda i, p, sel_ref: (p, sel_ref[0] * per + i, 0)), blk],
            out_specs=[blk, own_spec]),
        out_shape=[jax.ShapeDtypeStruct(other.shape, BF16), own_shape],
        compiler_params=_params("parallel", "arbitrary"),
    )(sel, g, other)


def _final_sum(own, recv, sel, name):
    half, cdim = own.shape
    tr = _pick(half, (512, 256, 128, 64))
    per = half // tr

    def body(sel_ref, own_ref, r0_ref, r1_ref, r2_ref, o_ref):
        o_ref[...] = ((own_ref[...] + r0_ref[0].astype(F32)) + r1_ref[0].astype(F32)) + r2_ref[0].astype(F32)

    part = lambda k: pl.BlockSpec((1, tr, cdim), lambda i, sel_ref, k=k: (k, i, 0))
    return pl.pallas_call(
        body, name=name,
        grid_spec=pltpu.PrefetchScalarGridSpec(
            num_scalar_prefetch=1, grid=(per,),
            in_specs=[pl.BlockSpec((tr, cdim), lambda i, sel_ref: (i, 0)), part(0), part(1), part(2)],
            out_specs=pl.BlockSpec((tr, cdim), lambda i, sel_ref: (sel_ref[0] * per + i, 0))),
        out_shape=jax.ShapeDtypeStruct((2 * half, cdim), F32), compiler_params=_params("parallel"),
    )(sel, own, recv, recv, recv)


def _row(v):
    return v.reshape(1, -1)


_BR_A, _BR_B, _BR_C = (0, A_WIDTH), (A_WIDTH, POOL_WIDTH), (A_WIDTH + POOL_WIDTH, CONV_WIDTH)


def _tie(v, token):
    return v if token is None else v + token.tie


def _no_hook(point, after, ready=None):
    return None


def _layer_fwd(x, w, mod, hook=_no_hook):
    s = x.shape[0]
    mod3 = mod.reshape(6, 1, D)
    h = _modnorm_fwd(x, _row(w["g_mix_pre"]), (mod3, 0), (mod3, 1), "mix_pre_fwd")
    hook("pre", h)
    z = _mm(h, w["w_all"], name="mm_in")
    qkv = z[:, Z_QKV:Z_PC].astype(BF16)
    fl = z[:, Z_FL:Z_COLS]
    cum = _cumf_fwd(fl, w["b_f_pad"])
    fr = cum[:, :HEADS].T.reshape(HEADS, s // ATT_K, ATT_K)
    br_a, lse = _attn_fwd(qkv, fr)
    br_b, br_c = _poolconv_fwd(z, w["w_pool_bd"], _tie(_row(w["pool_scale"]), hook("attn", lse)), w["conv_w"])
    hook("pool", br_b)
    wbr = w["w_branch"]
    pa = _mm(br_a, wbr, b_rows=_BR_A, name="mm_br_a")
    pb = _mm(br_b, wbr, b_rows=_BR_B, name="mm_br_b")
    pc = _mm(br_c, wbr, b_rows=_BR_C, name="mm_br_c")
    merged = _merge_fwd(z, pa, pb, pc)
    y = _mm(merged, w["w_out"], name="mm_out")
    x1, h2 = _post_pre_fwd(x, y, _row(w["g_mix_post"]), (mod3, 2), _row(w["g_ff_pre"]), (mod3, 3), (mod3, 4), "mix_post_ff_pre_fwd")
    a, r = _mm(h2, w["w_ff1"], b_split=N_CHIPS, epilogue=_relu2_fwd, out_dtype=(F32, BF16), name="mm_ff1")
    y2 = _mm(r, w["w_ff2"], name="mm_ff2")
    x2 = _post_fwd(x1, y2, _tie(_row(w["g_ff_post"]), hook("ff_post", y2)), (mod3, 5), "ff_post_fwd")
    hook("end", x2)
    saved = dict(x=x, h=h, z=z, qkv=qkv, fl=fl, fr=fr, lse=lse, br_a=br_a, br_b=br_b, br_c=br_c, pa=pa, pb=pb, pc=pc,
                 merged=merged, y=y, x1=x1, h2=h2, a=a, r=r, y2=y2)
    return x2, saved


def _layer_bwd(dx2, sv, w, mod, hook=_no_hook):
    s = dx2.shape[0]
    mod3 = mod.reshape(6, 1, D)
    dy2, sum_ff_post = _post_bwd(dx2, sv["y2"], _row(w["g_ff_post"]), (mod3, 5), "ff_post_bwd")
    (da,) = _mm(dy2, w["w_ff2"], tb=True, epilogue=_relu2_bwd, extras=(sv["a"],), out_dtype=(BF16,), name="mm_ff2_dx")
    d_w_ff2 = _mm(sv["r"], dy2, ta=True, name="mm_ff2_dw")
    dh2 = _mm(da, w["w_ff1"], tb=True, b_split=N_CHIPS, name="mm_ff1_dx")
    d_w_ff1 = _mm(sv["h2"], da, ta=True, out_split=N_CHIPS, name="mm_ff1_dw")
    g_ff_pre = _tie(_row(w["g_ff_pre"]), hook("ff_pre", dh2, dict(w_ff1=d_w_ff1, w_ff2=d_w_ff2)))
    dx1, dy, sum_mid = _pre_post_bwd(dh2, sv["x1"], dx2, g_ff_pre, (mod3, 4), sv["y"], _row(w["g_mix_post"]), (mod3, 2), "ff_pre_mix_post_bwd")
    sum_ff_pre, sum_mix_post = sum_mid, sum_mid[3:]
    dmerged = _mm(dy, w["w_out"], tb=True, name="mm_out_dx")
    d_w_out = _mm(sv["merged"], dy, ta=True, name="mm_out_dw")
    dz, dpa, dpb, dpc = _merge_bwd(dmerged, sv["z"], sv["pa"], sv["pb"], sv["pc"])
    wbr = w["w_branch"]
    dbr_a = _mm(dpa, wbr, tb=True, b_rows=_BR_A, name="mm_br_a_dx")
    dbr_b = _mm(dpb, wbr, tb=True, b_rows=_BR_B, name="mm_br_b_dx")
    dbr_c = _mm(dpc, wbr, tb=True, b_rows=_BR_C, name="mm_br_c_dx")
    d_w_branch = _stacked_dw((sv["br_a"], sv["br_b"], sv["br_c"]), (dpa, dpb, dpc), "mm_br_dw")

    dq, dk, dv, dfc, dfr = _attn_bwd(sv["qkv"], dbr_a, sv["br_a"], sv["lse"], sv["fr"])
    dcum = dfc + jnp.pad(dfr.reshape(HEADS, s).T, ((0, 0), (0, 128 - HEADS)))
    dfl, sum_bf = _cumf_bwd(dcum, sv["fl"], _tie(w["b_f_pad"], hook("cumf", dfc)))
    dpc_z, d_wbd, sum_ps, sum_cw = _poolconv_bwd(dbr_b, dbr_c, sv["z"], w["w_pool_bd"], _row(w["pool_scale"]), w["conv_w"])
    for at, part in ((Z_QKV, dq), (Z_QKV + A_WIDTH, dk), (Z_QKV + 2 * A_WIDTH, dv), (Z_PC, dpc_z), (Z_FL, dfl)):
        dz = lax.dynamic_update_slice(dz, part, (0, at))
    dh = _mm(dz, w["w_all"], tb=True, name="mm_in_dx")
    d_w_all = _mm(sv["h"], dz, ta=True, name="mm_in_dw")
    hook("mix_pre", dh)
    dx, sum_mix_pre = _modnorm_bwd(dh, sv["x"], dx1, _row(w["g_mix_pre"]), (mod3, 1), "mix_pre_bwd")

    dmod = jnp.stack([sum_mix_pre[0], sum_mix_pre[1], sum_mix_post[0], sum_ff_pre[0], sum_ff_pre[1], sum_ff_post[0]])
    d_w_in = d_w_all[None]
    d_w_pool = jnp.stack([d_wbd[64 * g:64 * g + 64, 64 * g:64 * g + 64] for g in range(4)])
    big = dict(w_in=d_w_in, w_branch=d_w_branch, w_out=d_w_out, w_ff1=d_w_ff1, w_ff2=d_w_ff2)
    small = dict(g_mix_pre=sum_mix_pre[2], g_mix_post=sum_mix_post[1], g_ff_pre=sum_ff_pre[2], g_ff_post=sum_ff_post[1],
                 b_f=sum_bf[0, :HEADS], w_pool=d_w_pool, pool_scale=sum_ps[0], conv_w=sum_cw[0:3])
    return dx, dmod, big, small


_QKV_END, _FL_END, _PC_END = 3 * A_WIDTH, 3 * A_WIDTH + HEADS, 3 * A_WIDTH + HEADS + POOL_WIDTH + 3 * CONV_WIDTH
_W_IN_GROUPS = ((_PC_END, IN_COLS, Z_GL), (0, _QKV_END, Z_QKV), (_FL_END, _PC_END, Z_PC), (_QKV_END, _FL_END, Z_FL))
_SHARD_COLS = IN_COLS // N_CHIPS


def _w_in_layout():
    out = []
    for p in range(N_CHIPS):
        pieces = []
        for lo, hi, at in _W_IN_GROUPS:
            a, b = max(lo, p * _SHARD_COLS), min(hi, (p + 1) * _SHARD_COLS)
            if a < b:
                pieces.append((at + a - lo, at + b - lo, a - p * _SHARD_COLS))
        pieces.sort()
        segs = []
        for z0, z1, _ in pieces:
            s, e = z0 // 128 * 128, -(-z1 // 128) * 128
            if segs and s <= segs[-1][1]:
                segs[-1] = (segs[-1][0], max(e, segs[-1][1]))
            else:
                segs.append((s, e))
        assert sum(e - s for s, e in segs) == Z_WINDOW
        out.append((pieces, segs))
    return out


Z_WINDOW = 1536


def _w_in_window(shard, p):
    pieces, segs = _w_in_layout()[p]
    cols = []
    for s, e in segs:
        at = s
        for z0, z1, src in pieces:
            if s <= z0 < e:
                if z0 > at:
                    cols.append(jnp.zeros((shard.shape[0], z0 - at), shard.dtype))
                cols.append(shard[:, src:src + z1 - z0])
                at = z1
        if e > at:
            cols.append(jnp.zeros((shard.shape[0], e - at), shard.dtype))
    return jnp.concatenate(cols, axis=1)


def _own_window(shard, chip):
    return lax.switch(chip, [lambda t, p=p: _w_in_window(t, p) for p in range(N_CHIPS)], shard)


def _w_all_from_windows(blocks):
    layout = _w_in_layout()
    bounds = sorted({edge for _, segs in layout for seg in segs for edge in seg})
    parts = []
    for lo, hi in zip(bounds[:-1], bounds[1:]):
        covering = []
        for p, (_, segs) in enumerate(layout):
            at = 0
            for s, e in segs:
                if s <= lo and hi <= e:
                    covering.append(blocks[p][:, at + lo - s:at + hi - s])
                at += e - s
        assert covering
        parts.append(covering[0] if len(covering) == 1 else covering[0] + covering[1])
    return jnp.concatenate(parts, axis=1)


def _w_in_shard(d_w_all, p):
    pieces = []
    for lo, hi, at in sorted(_W_IN_GROUPS):
        a, b = max(lo, p * _SHARD_COLS), min(hi, (p + 1) * _SHARD_COLS)
        if a < b:
            pieces.append(d_w_all[:, at + a - lo:at + b - lo])
    return jnp.concatenate(pieces, axis=1)


def _w_in_shards(d_w_all):
    return jnp.stack([_w_in_shard(d_w_all, p) for p in range(N_CHIPS)])


def _full_layer_weights(w_in_blocks, w_branch, w_out, w_ff1, w_ff2, g_mix_pre, g_mix_post, g_ff_pre, g_ff_post, b_f, w_pool, pool_scale, conv_w):
    w_all = None if w_in_blocks is None else _w_all_from_windows(w_in_blocks)
    wbd = (w_pool[:, :, None, :] * jnp.eye(4, dtype=F32)[:, None, :, None]).reshape(POOL_WIDTH, POOL_WIDTH)
    return dict(w_all=w_all, w_branch=w_branch, w_out=w_out, w_ff1=w_ff1, w_ff2=w_ff2, g_mix_pre=g_mix_pre, g_mix_post=g_mix_post,
                g_ff_pre=g_ff_pre, g_ff_post=g_ff_post, b_f_pad=jnp.pad(b_f, (0, 128 - HEADS)).reshape(1, 128), w_pool_bd=wbd,
                pool_scale=pool_scale, conv_w=conv_w)


class _NoComm:
    def layer_weights(self, l):
        raise NotImplementedError

    def fwd_hook(self, l):
        return _no_hook

    def bwd_hook(self, l):
        return _no_hook

    def grads_ready(self, l, big):
        return None


class _Layers(_NoComm):
    def __init__(self, layers):
        self.layers = layers

    def layer_weights(self, l):
        return self.layers[l]


def _local_step(x, target, mods, comm):
    saved, weights = [], []
    act = x
    for l in range(DEPTH):
        weights.append(comm.layer_weights(l))
        act, sv = _layer_fwd(act, weights[l], mods[l], comm.fwd_hook(l))
        saved.append(sv)
    dact, sq = _loss_head(act, target)
    loss = sq[0, 0] * (0.5 / D)
    dmods, bigs, smalls = [None] * DEPTH, [None] * DEPTH, [None] * DEPTH
    token = None
    for l in reversed(range(DEPTH)):
        dact, dmods[l], bigs[l], smalls[l] = _layer_bwd(dact, saved[l], weights[l], _tie(mods[l], token), comm.bwd_hook(l))
        token = comm.grads_ready(l, bigs[l])
    return loss, dact, jnp.stack(dmods), bigs, smalls


_BIG = ("w_in", "w_branch", "w_out", "w_ff1", "w_ff2")


class _GatherJob:
    def __init__(self, tag, shards, after):
        self.tag, self.n = tag, len(shards)
        lands = [lax.empty((N_CHIPS,) + s.shape, s.dtype) for s in shards]
        self.state = _copies_start(list(shards) + lands, _plan_gather_ici, 4 * self.n, after, "gather_ici_start_" + tag)
        self.token = self.state[3]

    def pass_on(self, after):
        bufs = _copies_wait(self.state, _plan_gather_ici, after, "gather_ici_wait_" + self.tag)
        self.state = _copies_start(bufs[self.n:], _plan_gather_d2d, 3 * self.n, bufs[0], "gather_d2d_start_" + self.tag)
        self.token = self.state[3]
        return self.token

    def done(self, after):
        return _copies_wait(self.state, _plan_gather_d2d, after, "gather_d2d_wait_" + self.tag)


class _ReduceJob:
    def __init__(self, tag, names, grads, sel, after):
        self.tag, self.names, self.n, self.sel = tag, names, len(names), sel
        lands = [lax.empty((g.shape[0], g.shape[1] // 2, g.shape[2]), F32) for g in grads]
        self.state = _copies_start(list(grads) + lands, _plan_rs_sibling, self.n, after, "rs_sibling_start_" + tag)
        self.token = self.state[3]

    def _chip_sum(self, name, g, other):
        if g.shape[0] == N_CHIPS:
            return _chip_sum(g, other, self.sel, "rs_chip_sum_" + name)
        wire, total = _chip_sum(g, other, self.sel, "rs_chip_sum_" + name, blocked=False)
        own = lax.switch(self.sel[1], [lambda t, p=p: _w_in_shard(t, p) for p in range(N_CHIPS)], total[0])
        return _w_in_shards(wire[0]), own

    def chip_sums(self, after):
        bufs = _copies_wait(self.state, _plan_rs_sibling, after, "rs_sibling_wait_" + self.tag)
        wires, self.owns = zip(*[self._chip_sum(name, bufs[i], bufs[self.n + i]) for i, name in enumerate(self.names)])
        lands = [lax.empty((3,) + w.shape[1:], BF16) for w in wires]
        self.state = _copies_start(list(wires) + lands, _plan_rs_chips, 3 * self.n, self.owns[0], "rs_chips_start_" + self.tag)
        self.token = self.state[3]
        return self.token

    def final_sums(self, after):
        bufs = _copies_wait(self.state, _plan_rs_chips, after, "rs_chips_wait_" + self.tag)
        sums = [_final_sum(self.owns[i], bufs[self.n + i], self.sel, "rs_final_" + name) for i, name in enumerate(self.names)]
        self.state = _copies_start(sums, _plan_rs_share, self.n, sums[0], "rs_share_start_" + self.tag)
        self.token = self.state[3]
        return self.token

    def done(self, after):
        return dict(zip(self.names, _copies_wait(self.state, _plan_rs_share, after, "rs_share_wait_" + self.tag)))


def _chip_blocks(g):
    return g if g.ndim == 3 else g.reshape(N_CHIPS, -1, g.shape[1])


class _StepComm(_NoComm):
    def __init__(self, big_weights, w_in0, sel, after):
        self.sel = sel
        self.small, self.grads, self.jobs = None, [dict() for _ in range(DEPTH)], {}
        self.jobs["in0"] = _GatherJob("in0", [w_in0], after)
        later = lax.optimization_barrier((tuple(big_weights), self.jobs["in0"].token.after))[0]
        self.jobs["rest0"] = _GatherJob("rest0", [w[0].astype(BF16) for w in later[1:]], self.jobs["in0"].token)
        layer1 = [w[1].astype(BF16) for w in later]
        self.jobs["all1"] = _GatherJob("all1", [_own_window(layer1[0], sel[1])] + layer1[1:], self.jobs["rest0"].token)

    def layer_weights(self, l):
        if l == 0:
            self.weights0 = _full_layer_weights(None, None, None, None, None, *self.small[0])
            return self.weights0
        g_in, g_br, g_out, g_f1, g_f2 = self.landed1
        return _full_layer_weights(g_in, g_br.reshape(D, D), g_out.reshape(D, D), g_f1, g_f2.reshape(D_FF, D), *self.small[1])

    def fwd_hook(self, l):
        if l != 0:
            return _no_hook

        def hook(point, after, ready=None):
            if point == "pre":
                job = self.jobs["in0"]
                started = after[:8, :128].astype(F32) + self.jobs["all1"].token.after
                self.weights0["w_all"] = _w_all_from_windows(job.done(job.pass_on(started))[0])
            if point == "attn":
                return self.jobs["rest0"].pass_on(after)
            if point == "ff_post":
                return self.jobs["all1"].pass_on(after)
            if point == "pool":
                g_br, g_out, g_f1, g_f2 = self.jobs["rest0"].done(after)
                self.weights0.update(w_branch=g_br.reshape(D, D), w_out=g_out.reshape(D, D), w_ff1=g_f1, w_ff2=g_f2.reshape(D_FF, D))
            if point == "end":
                self.landed1 = self.jobs["all1"].done(after)
            return None
        return hook

    def bwd_hook(self, l):
        if l != 0:
            return _no_hook

        def hook(point, after, ready=None):
            jobs = self.jobs
            if point == "ff_pre":
                token = jobs["rs1"].chip_sums(after)
                jobs["rs0_ff"] = _ReduceJob("0_ff", ("w_ff1", "w_ff2"), [_chip_blocks(ready[n]) for n in ("w_ff1", "w_ff2")], self.sel, token)
                return jobs["rs0_ff"].token
            if point == "cumf":
                return jobs["rs0_ff"].chip_sums(jobs["rs1"].final_sums(after))
            self.grads[1] = jobs["rs1"].done(after)
            return None
        return hook

    def grads_ready(self, l, big):
        if l == 1:
            self.jobs["rs1"] = _ReduceJob("1", _BIG, [_chip_blocks(big[n]) for n in _BIG], self.sel, self.sel)
            return self.jobs["rs1"].token
        names = ("w_in", "w_branch", "w_out")
        self.jobs["rs0_mix"] = _ReduceJob("0_mix", names, [_chip_blocks(big[n]) for n in names], self.sel, self.sel)
        return self.jobs["rs0_mix"].token

    def finish_sums(self, after):
        jobs = self.jobs
        token = jobs["rs0_mix"].chip_sums(after)
        return jobs["rs0_ff"].final_sums(token)

    def finish_ff(self, after):
        self.grads[0].update(self.jobs["rs0_ff"].done(after))

    def finish_mix(self, after):
        job = self.jobs["rs0_mix"]
        self.grads[0].update(job.done(job.final_sums(after)))


_SMALL = ("g_mix_pre", "g_mix_post", "g_ff_pre", "g_ff_post", "b_f", "w_pool", "pool_scale", "conv_w")


def _w_in_view(t):
    return t.reshape(DEPTH, D // 128, 128, _SHARD_COLS).transpose(3, 1, 0, 2).reshape(_SHARD_COLS * (D // 128) * DEPTH, 128)


def _w_in_unview(t):
    return t.reshape(_SHARD_COLS, D // 128, DEPTH, 128).transpose(2, 1, 3, 0).reshape(DEPTH, D, _SHARD_COLS)


def _pack(parts, rows=8):
    flat = jnp.concatenate([p.reshape(-1) for p in parts])
    width = -(-flat.shape[0] // (rows * 128)) * 128
    return jnp.pad(flat, (0, rows * width - flat.shape[0])).reshape(rows, width)


def _unpack(packed, like):
    flat = packed.reshape(-1)
    out, at = [], 0
    for ref in like:
        out.append(flat[at:at + ref.size].reshape(ref.shape))
        at += ref.size
    return out


def kernel(x, c, w_ada, b_ada, g_mix_pre, g_mix_post, g_ff_pre, g_ff_post, w_in, b_f, w_pool, pool_scale, conv_w, w_branch, w_out, w_ff1, w_ff2, loss_target, m_w_ada, m_b_ada, m_g_mix_pre, m_g_mix_post, m_g_ff_pre, m_g_ff_post, m_w_in, m_b_f, m_w_pool, m_pool_scale, m_conv_w, m_w_branch, m_w_out, m_w_ff1, m_w_ff2, v_w_ada, v_b_ada, v_g_mix_pre, v_g_mix_post, v_g_ff_pre, v_g_ff_post, v_w_in, v_b_f, v_w_pool, v_pool_scale, v_conv_w, v_w_branch, v_w_out, v_w_ff1, v_w_ff2):
    xi, yi, ci = lax.axis_index("x"), lax.axis_index("y"), lax.axis_index("c")
    chip = 2 * xi + yi
    dev = 2 * chip + ci
    n_ada = w_ada.shape[2]

    first = jnp.zeros((8, D + 384), F32).at[0, :D].set(c[0]).at[0, D:].set(conv_w.reshape(-1))
    w_in0 = _own_window(w_in[0].astype(BF16), chip)
    got = _allgather8(first, "gather_cond", after=(w_in0,)).reshape(N_DEV, 8, D + 384)[:, 0]
    c_all = got[:, :D]
    conv_full = got[0::2, D:].reshape(N_CHIPS, DEPTH, 3, CONV_WIDTH // N_CHIPS).transpose(1, 2, 0, 3).reshape(DEPTH, 3, CONV_WIDTH)

    b_loc = lax.dynamic_slice_in_dim(b_ada, chip * n_ada, n_ada, axis=1).reshape(DEPTH, 1, n_ada)
    mod_cols, silu_c = _ada_fwd(c_all, w_ada, b_loc)
    got = _allgather8(mod_cols.reshape(DEPTH * N_DEV, n_ada), "gather_mod").reshape(N_DEV, DEPTH, N_DEV, n_ada)[0::2]
    mod_all = got.transpose(1, 2, 0, 3).reshape(DEPTH, N_DEV, 6, D)
    mods = lax.dynamic_index_in_dim(mod_all, dev, axis=1, keepdims=False)

    comm = _StepComm((w_in, w_branch, w_out, w_ff1, w_ff2), w_in0, jnp.stack([ci, chip]).astype(jnp.int32), mods)
    comm.small = [(g_mix_pre[l], g_mix_post[l], g_ff_pre[l], g_ff_post[l], b_f[l], w_pool[l], pool_scale[l], conv_full[l]) for l in range(DEPTH)]
    loss_part, grad_x, dmods, bigs, smalls = _local_step(x[0], loss_target[0], mods, comm)

    small_parts = [smalls[l][name] for name in _SMALL for l in range(DEPTH)] + [loss_part.reshape(1)]
    packed = _tie(_pack([dmods] + small_parts), comm.jobs["rs0_mix"].token)
    gathered = _allgather8(packed, "gather_small")
    dmod_all = gathered.reshape(N_DEV, -1)[:, :dmods.size].reshape(N_DEV, DEPTH, 6 * D)
    summed = _unpack(_sum_devices(gathered), [dmods] + small_parts)
    grad_b_ada = summed[0].reshape(DEPTH, 6 * D)
    loss = summed[-1][0]
    small_grads = {name: jnp.stack(summed[1 + 2 * i:3 + 2 * i]) for i, name in enumerate(_SMALL)}
    small_grads["conv_w"] = lax.dynamic_slice_in_dim(small_grads["conv_w"], chip * (CONV_WIDTH // N_CHIPS), CONV_WIDTH // N_CHIPS, axis=2)

    dmod_loc = lax.dynamic_slice_in_dim(dmod_all.transpose(1, 0, 2), chip * n_ada, n_ada, axis=2)
    tail_token = comm.finish_sums(grad_b_ada)
    silu_pad = _tie(jnp.pad(silu_c, ((0, 128 - N_DEV), (0, 0))), tail_token)
    dmod_pad = jnp.pad(dmod_loc.transpose(1, 0, 2).reshape(N_DEV, DEPTH * n_ada), ((0, 128 - N_DEV), (0, 0)))
    grad_w_ada = _mm(silu_pad, dmod_pad, ta=True, out_split=DEPTH, name="mm_ada_dw")

    grads = dict(w_ada=grad_w_ada, b_ada=grad_b_ada, **small_grads)
    weights = dict(w_ada=w_ada, b_ada=b_ada, g_mix_pre=g_mix_pre, g_mix_post=g_mix_post, g_ff_pre=g_ff_pre, g_ff_post=g_ff_post, w_in=w_in,
                   b_f=b_f, w_pool=w_pool, pool_scale=pool_scale, conv_w=conv_w, w_branch=w_branch, w_out=w_out, w_ff1=w_ff1, w_ff2=w_ff2)
    m_in = dict(w_ada=m_w_ada, b_ada=m_b_ada, g_mix_pre=m_g_mix_pre, g_mix_post=m_g_mix_post, g_ff_pre=m_g_ff_pre, g_ff_post=m_g_ff_post,
                w_in=m_w_in, b_f=m_b_f, w_pool=m_w_pool, pool_scale=m_pool_scale, conv_w=m_conv_w, w_branch=m_w_branch, w_out=m_w_out,
                w_ff1=m_w_ff1, w_ff2=m_w_ff2)
    v_in = dict(w_ada=v_w_ada, b_ada=v_b_ada, g_mix_pre=v_g_mix_pre, g_mix_post=v_g_mix_post, g_ff_pre=v_g_ff_pre, g_ff_post=v_g_ff_post,
                w_in=v_w_in, b_f=v_b_f, w_pool=v_w_pool, pool_scale=v_pool_scale, conv_w=v_conv_w, w_branch=v_w_branch, w_out=v_w_out,
                w_ff1=v_w_ff1, w_ff2=v_w_ff2)
    order = ("w_ada", "b_ada", "g_mix_pre", "g_mix_post", "g_ff_pre", "g_ff_post", "w_in", "b_f", "w_pool", "pool_scale", "conv_w",
             "w_branch", "w_out", "w_ff1", "w_ff2")
    delta, new_m, new_v = {}, {}, {}
    tiny = ("b_ada",) + _SMALL
    tiny_g = [_tie(grads[tiny[0]], tail_token)] + [grads[name] for name in tiny[1:]]
    res = _adamw_many([weights[name] for name in tiny], tiny_g, [m_in[name] for name in tiny], [v_in[name] for name in tiny], "adamw_small")
    for out, vals in zip((delta, new_m, new_v), res):
        out.update(zip(tiny, vals))
    delta["w_ada"], new_m["w_ada"], new_v["w_ada"] = _adamw(w_ada, grad_w_ada, m_w_ada, v_w_ada, "adamw_w_ada")
    comm.finish_ff(delta["w_ada"][0, :8, :128] + delta["b_ada"][0, :128])
    for name in ("w_ff1", "w_ff2", "w_in", "w_branch", "w_out"):
        if name == "w_in":
            comm.finish_mix(delta["w_ff2"][0, :8, :128])
        g_layers = [comm.grads[l][name] for l in range(DEPTH)]
        if name == "w_in":
            g_view = lax.optimization_barrier(_w_in_view(jnp.stack(g_layers)))
            res = _adamw(_w_in_view(w_in), g_view, _w_in_view(m_w_in), _w_in_view(v_w_in), "adamw_w_in")
            grads[name], delta[name], new_m[name], new_v[name] = [_w_in_unview(t) for t in (g_view, *res)]
        else:
            delta[name], new_m[name], new_v[name], grads[name] = _adamw_layers(weights[name], g_layers, m_in[name], v_in[name], "adamw_" + name)

    return (loss, grad_x[None], *[grads[n] for n in order], *[delta[n] for n in order], *[new_m[n] for n in order],
            *[new_v[n] for n in order])
```

```python
from typing import NamedTuple

import jax
import jax.numpy as jnp
from jax import lax
from jax.experimental import pallas as pl
from jax.experimental.pallas import tpu as pltpu

F32 = jnp.float32
BF16 = jnp.bfloat16
MESH = pl.DeviceIdType.MESH

D = 1024
DEPTH = 2
HEADS = 8
HEAD_DIM = 64
A_WIDTH = 512
POOL_WIDTH = 256
CONV_WIDTH = 256
D_FF = 4096
IN_COLS = 5640
Z_GL, Z_QKV, Z_PC, Z_FL, Z_COLS = 0, 3072, 4608, 5632, 5760
RMS_EPS = 1e-6
NEG_INF = -1e30
ROW_TILE = 512
EW_ROWS = 256
N_CHIPS = 4
N_DEV = 8
V7X_VMEM_LIMIT = 48 * 1024 * 1024

ADAM_LR = 0.001
ADAM_B1 = 0.9
ADAM_B2 = 0.999
ADAM_EPS = 1e-08
ADAM_WD = 0.01
ADAM_STEP = 10

_HBM = pl.BlockSpec(memory_space=pltpu.HBM)


def _params(*sem):
    return pltpu.CompilerParams(dimension_semantics=sem, vmem_limit_bytes=V7X_VMEM_LIMIT)


def _pick(dim, cands):
    for cand in cands:
        if dim % cand == 0:
            return cand
    return dim


MM_TILE_BUDGET = 39 * 1024 * 1024


def _mm_tiles(m, n, k, k_unit, tn, a_size, b_size, out_size):
    for tk in (k_unit, 2048, 1152, 1024, 640, 512, 256, 128):
        if k_unit % tk:
            continue
        for tm in (2048, 1024, 512, 256, 128):
            if m % tm or ((m // tm) * (n // tn) < 2 and tm > 512):
                continue
            need = 2 * (tm * tk * a_size + tk * tn * b_size + tm * tn * out_size) + (0 if tk == k else 4 * tm * tn)
            if need <= MM_TILE_BUDGET and (tk == k_unit or tm >= 512):
                return tm, tk
    return 128, 128


def _mm(a, b, *, ta=False, tb=False, b_rows=None, b_split=1, out_split=1, out_dtype=F32, epilogue=None, extras=(), name):
    (k, m) = a.shape if ta else a.shape[::-1]
    b_row0, b_rows = (0, b.shape[-2]) if b_rows is None else b_rows
    b_cols = b.shape[-1] * b_split
    (n, k2) = (b_rows, b_cols) if tb else (b_cols, b_rows)
    assert k == k2, (a.shape, b.shape, ta, tb)
    n_unit = n // (out_split * (1 if tb else b_split))
    k_unit = k // (b_split if tb else 1)
    tn = _pick(n_unit, (1024, 1152, 768, 640, 512, 256, 128))
    tm, tk = _mm_tiles(m, n, k, k_unit, tn, a.dtype.itemsize, b.dtype.itemsize,
                       sum(jnp.dtype(dt).itemsize for dt in out_dtype) + 4 * len(extras) if epilogue else jnp.dtype(out_dtype).itemsize)
    nk = k // tk
    dims = (((0 if ta else 1,), (1 if tb else 0,)), ((), ()))

    def dot(a_ref, b_ref):
        b_val = b_ref[0] if b_split > 1 else b_ref[...]
        return lax.dot_general(a_ref[...].astype(BF16), b_val.astype(BF16), dims, preferred_element_type=F32)

    n_extra = len(extras)
    assert epilogue is None or out_split == 1

    def put(refs, val):
        if epilogue is not None:
            for o_ref, res in zip(refs[n_extra:], epilogue(val, *[r[...] for r in refs[:n_extra]])):
                o_ref[...] = res.astype(o_ref.dtype)
        elif out_split > 1:
            refs[0][0] = val.astype(refs[0].dtype)
        else:
            refs[0][...] = val.astype(refs[0].dtype)

    def body_single(a_ref, b_ref, *refs):
        put(refs, dot(a_ref, b_ref))

    def body_acc(a_ref, b_ref, *refs):
        kk = pl.program_id(2)
        acc_ref = refs[-1]

        @pl.when(kk == 0)
        def _():
            acc_ref[...] = jnp.zeros_like(acc_ref)

        acc_ref[...] += dot(a_ref, b_ref)

        @pl.when(kk == nk - 1)
        def _():
            put(refs[:-1], acc_ref[...])

    a_spec = pl.BlockSpec((tk, tm), lambda i, j, kk: (kk, i)) if ta else pl.BlockSpec((tm, tk), lambda i, j, kk: (i, kk))
    if b_split == 1:
        off = b_row0 // (tn if tb else tk)
        assert off * (tn if tb else tk) == b_row0
        b_spec = pl.BlockSpec((tn, tk), lambda i, j, kk: (j + off, kk)) if tb else pl.BlockSpec((tk, tn), lambda i, j, kk: (kk + off, j))
    elif tb:
        per = k_unit // tk
        b_spec = pl.BlockSpec((1, tn, tk), lambda i, j, kk: (kk // per, j, kk % per))
    else:
        per = n // b_split // tn
        b_spec = pl.BlockSpec((1, tk, tn), lambda i, j, kk: (j // per, kk, j % per))
    if out_split == 1:
        o_spec = pl.BlockSpec((tm, tn), lambda i, j, kk: (i, j))
        o_shape = None if epilogue is not None else jax.ShapeDtypeStruct((m, n), out_dtype)
    else:
        per_o = n // out_split // tn
        o_spec = pl.BlockSpec((1, tm, tn), lambda i, j, kk: (j // per_o, i, j % per_o))
        o_shape = jax.ShapeDtypeStruct((out_split, m, n // out_split), out_dtype)
    if epilogue is not None:
        o_shape = [jax.ShapeDtypeStruct((m, n), dt) for dt in out_dtype]
        o_spec = [o_spec] * len(out_dtype)
    return pl.pallas_call(
        body_single if nk == 1 else body_acc, name=name, grid=(m // tm, n // tn, nk),
        in_specs=[a_spec, b_spec] + [pl.BlockSpec((tm, tn), lambda i, j, kk: (i, j))] * n_extra, out_specs=o_spec, out_shape=o_shape,
        scratch_shapes=[] if nk == 1 else [pltpu.VMEM((tm, tn), F32)],
        compiler_params=_params("parallel", "parallel", "arbitrary"),
    )(a, b, *extras)


def _stacked_dw(lhs, rhs, name):
    n = rhs[0].shape[1]
    tr = min(a.shape[1] for a in lhs)
    tn = _pick(n, (1024, 512, 256, 128))
    counts = [a.shape[1] // tr for a in lhs]
    starts = [sum(counts[:i]) for i in range(len(lhs))]
    assert all(a.shape[1] == c * tr for a, c in zip(lhs, counts))
    n_lhs = len(lhs)

    def body(*refs):
        o_ref = refs[-1]
        j = pl.program_id(1)
        for a_ref, b_ref, start, count in zip(refs[:n_lhs], refs[n_lhs:-1], starts, counts):
            @pl.when((j >= start) & (j < start + count))
            def _():
                o_ref[...] = lax.dot_general(a_ref[...].astype(BF16), b_ref[...].astype(BF16), (((0,), (0,)), ((), ())),
                                             preferred_element_type=F32)

    def lhs_spec(a, start, count):
        return pl.BlockSpec((a.shape[0], tr), lambda c, j: (0, jnp.clip(j - start, 0, count - 1)))

    return pl.pallas_call(
        body, name=name, grid=(n // tn, sum(counts)),
        in_specs=[lhs_spec(a, st, ct) for a, st, ct in zip(lhs, starts, counts)]
        + [pl.BlockSpec((b.shape[0], tn), lambda c, j: (0, c)) for b in rhs],
        out_specs=pl.BlockSpec((tr, tn), lambda c, j: (j, c)),
        out_shape=jax.ShapeDtypeStruct((sum(counts) * tr, n), F32),
        compiler_params=_params("parallel", "parallel"),
    )(*lhs, *rhs)


def _ew(fn, ins, out_dtypes, name, tc=None):
    shape = ins[0].shape
    lead, (rows, cols) = shape[:-2], shape[-2:]
    tc = cols if tc is None else tc
    if tc > 1024:
        tr = _pick(rows, (EW_ROWS, 128, 8))
    elif tc > 128:
        tr = _pick(rows, (2 * EW_ROWS, EW_ROWS, 128, 8))
    else:
        tr = _pick(rows, (4096, 2256, 2048, 1024, EW_ROWS, 8))
    n_in = len(ins)

    def body(*refs):
        res = fn(*[r[...] for r in refs[:n_in]])
        for o_ref, val in zip(refs[n_in:], res):
            o_ref[...] = val.astype(o_ref.dtype)

    if lead:
        spec = pl.BlockSpec((None, tr, tc), lambda l, i, j: (l, i, j))
    else:
        spec = pl.BlockSpec((tr, tc), lambda i, j: (i, j))
    return pl.pallas_call(
        body, name=name, grid=lead + (rows // tr, cols // tc),
        in_specs=[spec] * n_in, out_specs=[spec] * len(out_dtypes),
        out_shape=[jax.ShapeDtypeStruct(shape, dt) for dt in out_dtypes],
        compiler_params=_params(*(["parallel"] * (len(lead) + 2))),
    )(*ins)


def _relu2_fwd(a):
    r = jnp.maximum(a, 0.0)
    return a, r * r


def _relu2_bwd(dr, a):
    return (dr * (2.0 * jnp.maximum(a, 0.0)),)


def _adamw_math(w, g, m, v):
    m = ADAM_B1 * m + (1.0 - ADAM_B1) * g
    v = ADAM_B2 * v + (1.0 - ADAM_B2) * (g * g)
    m_hat = m / (1.0 - ADAM_B1 ** ADAM_STEP)
    v_hat = v / (1.0 - ADAM_B2 ** ADAM_STEP)
    delta = -ADAM_LR * (m_hat / (jnp.sqrt(v_hat) + ADAM_EPS) + ADAM_WD * w)
    return delta, m, v


def _adamw(w, g, m, v, name):
    return _ew(_adamw_math, [w, g, m, v], [F32, F32, F32], name)


def _adamw_layers(w, g_layers, m, v, name):
    depth, rows, cols = w.shape
    tr = _pick(rows, (2 * EW_ROWS, EW_ROWS, 128, 8)) if cols <= 1024 else _pick(rows, (EW_ROWS, 128, 8))

    def body(w_ref, *refs):
        g_refs, (m_ref, v_ref, d_ref, mo_ref, vo_ref, go_ref) = refs[:depth], refs[depth:]
        layer = pl.program_id(0)
        g = g_refs[0][...]
        for l in range(1, depth):
            g = jnp.where(layer == l, g_refs[l][...], g)
        d_ref[...], mo_ref[...], vo_ref[...] = _adamw_math(w_ref[...], g, m_ref[...], v_ref[...])
        go_ref[...] = g

    spec = pl.BlockSpec((None, tr, cols), lambda l, i: (l, i, 0))
    g_specs = [pl.BlockSpec((tr, cols), lambda l, i, k=k: (jnp.where(l == k, i, 0), 0)) for k in range(depth)]
    return pl.pallas_call(
        body, name=name, grid=(depth, rows // tr),
        in_specs=[spec] + g_specs + [spec, spec], out_specs=[spec] * 4,
        out_shape=[jax.ShapeDtypeStruct(w.shape, F32)] * 4, compiler_params=_params("arbitrary", "arbitrary"),
    )(w, *g_layers, m, v)


def _adamw_many(ws, gs, ms, vs, name):
    n = len(ws)

    def body(*refs):
        for i in range(n):
            res = _adamw_math(*[refs[k * n + i][...] for k in range(4)])
            for k in range(3):
                refs[(4 + k) * n + i][...] = res[k]

    outs = pl.pallas_call(
        body, name=name, out_shape=[jax.ShapeDtypeStruct(w.shape, F32) for w in ws] * 3,
        compiler_params=pltpu.CompilerParams(vmem_limit_bytes=V7X_VMEM_LIMIT),
    )(*ws, *gs, *ms, *vs)
    return outs[:n], outs[n:2 * n], outs[2 * n:]


def _row_spec(cols, block=0):
    return pl.BlockSpec((ROW_TILE, cols), lambda i, block=block: (i, block))


def _vec_spec(cols):
    return pl.BlockSpec((1, cols), lambda i: (0, 0))


def _vec_args(*vecs):
    arrays = [v[0] if isinstance(v, tuple) else v for v in vecs]
    specs = [pl.BlockSpec((None, 1, D), lambda i, row=v[1]: (row, 0, 0)) if isinstance(v, tuple) else _vec_spec(D) for v in vecs]
    return arrays, specs


def _sum_spec(cols):
    return pl.BlockSpec((8, cols), lambda i: (0, 0))


def _rstd(x):
    return lax.rsqrt(jnp.mean(x * x, axis=-1, keepdims=True) + RMS_EPS)


def _modnorm_fwd(x, g, shift, scale, name):
    s = x.shape[0]

    def body(x_ref, g_ref, sh_ref, sc_ref, h_ref):
        xv = x_ref[...]
        n = xv * _rstd(xv)
        h_ref[...] = ((n * g_ref[...]) * (1.0 + sc_ref[...]) + sh_ref[...]).astype(BF16)

    vecs, vec_specs = _vec_args(g, shift, scale)
    return pl.pallas_call(
        body, name=name, grid=(s // ROW_TILE,),
        in_specs=[_row_spec(D)] + vec_specs, out_specs=_row_spec(D),
        out_shape=jax.ShapeDtypeStruct((s, D), BF16), compiler_params=_params("parallel"),
    )(x, *vecs)


def _post_fwd(x, y, g, gate, name):
    s = x.shape[0]

    def body(x_ref, y_ref, g_ref, gate_ref, o_ref):
        yv = y_ref[...]
        o_ref[...] = x_ref[...] + gate_ref[...] * ((yv * _rstd(yv)) * g_ref[...])

    vecs, vec_specs = _vec_args(g, gate)
    return pl.pallas_call(
        body, name=name, grid=(s // ROW_TILE,),
        in_specs=[_row_spec(D), _row_spec(D)] + vec_specs, out_specs=_row_spec(D),
        out_shape=jax.ShapeDtypeStruct((s, D), F32), compiler_params=_params("parallel"),
    )(x, y, *vecs)


def _post_bwd(dxo, y, g, gate, name):
    s = dxo.shape[0]

    def body(d_ref, y_ref, g_ref, gate_ref, dy_ref, sum_ref):
        @pl.when(pl.program_id(0) == 0)
        def _():
            sum_ref[...] = jnp.zeros_like(sum_ref)

        dv, yv = d_ref[...], y_ref[...]
        r = _rstd(yv)
        n = yv * r
        sum_ref[0:1, :] += jnp.sum(dv * (n * g_ref[...]), axis=0, keepdims=True)
        sum_ref[1:2, :] += jnp.sum((dv * gate_ref[...]) * n, axis=0, keepdims=True)
        dn = (dv * gate_ref[...]) * g_ref[...]
        dy_ref[...] = (r * (dn - n * jnp.mean(dn * n, axis=-1, keepdims=True))).astype(BF16)

    vecs, vec_specs = _vec_args(g, gate)
    return pl.pallas_call(
        body, name=name, grid=(s // ROW_TILE,),
        in_specs=[_row_spec(D), _row_spec(D)] + vec_specs,
        out_specs=[_row_spec(D), _sum_spec(D)],
        out_shape=[jax.ShapeDtypeStruct((s, D), BF16), jax.ShapeDtypeStruct((8, D), F32)],
        compiler_params=_params("arbitrary"),
    )(dxo, y, *vecs)


def _modnorm_bwd(dh, x, dxo, g, scale, name):
    s = dh.shape[0]

    def body(dh_ref, x_ref, d_ref, g_ref, sc_ref, dx_ref, sum_ref):
        @pl.when(pl.program_id(0) == 0)
        def _():
            sum_ref[...] = jnp.zeros_like(sum_ref)

        dhv, xv = dh_ref[...], x_ref[...]
        r = _rstd(xv)
        n = xv * r
        one_sc = 1.0 + sc_ref[...]
        sum_ref[0:1, :] += jnp.sum(dhv, axis=0, keepdims=True)
        sum_ref[1:2, :] += jnp.sum(dhv * (n * g_ref[...]), axis=0, keepdims=True)
        sum_ref[2:3, :] += jnp.sum((dhv * one_sc) * n, axis=0, keepdims=True)
        dn = (dhv * one_sc) * g_ref[...]
        dx_ref[...] = d_ref[...] + r * (dn - n * jnp.mean(dn * n, axis=-1, keepdims=True))

    vecs, vec_specs = _vec_args(g, scale)
    return pl.pallas_call(
        body, name=name, grid=(s // ROW_TILE,),
        in_specs=[_row_spec(D), _row_spec(D), _row_spec(D)] + vec_specs,
        out_specs=[_row_spec(D), _sum_spec(D)],
        out_shape=[jax.ShapeDtypeStruct((s, D), F32), jax.ShapeDtypeStruct((8, D), F32)],
        compiler_params=_params("arbitrary"),
    )(dh, x, dxo, *vecs)


def _post_pre_fwd(x, y, g_post, gate, g_pre, shift, scale, name):
    s = x.shape[0]

    def body(x_ref, y_ref, gp_ref, gate_ref, g_ref, sh_ref, sc_ref, o_ref, h_ref):
        yv = y_ref[...]
        xo = x_ref[...] + gate_ref[...] * ((yv * _rstd(yv)) * gp_ref[...])
        o_ref[...] = xo
        h_ref[...] = (((xo * _rstd(xo)) * g_ref[...]) * (1.0 + sc_ref[...]) + sh_ref[...]).astype(BF16)

    vecs, vec_specs = _vec_args(g_post, gate, g_pre, shift, scale)
    return pl.pallas_call(
        body, name=name, grid=(s // ROW_TILE,),
        in_specs=[_row_spec(D), _row_spec(D)] + vec_specs, out_specs=[_row_spec(D), _row_spec(D)],
        out_shape=[jax.ShapeDtypeStruct((s, D), F32), jax.ShapeDtypeStruct((s, D), BF16)], compiler_params=_params("parallel"),
    )(x, y, *vecs)


def _pre_post_bwd(dh, x, dxo, g_pre, scale, y, g_post, gate, name):
    s = dh.shape[0]

    def body(dh_ref, x_ref, d_ref, y_ref, g_ref, sc_ref, gp_ref, gate_ref, dx_ref, dy_ref, sum_ref):
        @pl.when(pl.program_id(0) == 0)
        def _():
            sum_ref[...] = jnp.zeros_like(sum_ref)

        dhv, xv = dh_ref[...], x_ref[...]
        r = _rstd(xv)
        n = xv * r
        one_sc = 1.0 + sc_ref[...]
        sum_ref[0:1, :] += jnp.sum(dhv, axis=0, keepdims=True)
        sum_ref[1:2, :] += jnp.sum(dhv * (n * g_ref[...]), axis=0, keepdims=True)
        sum_ref[2:3, :] += jnp.sum((dhv * one_sc) * n, axis=0, keepdims=True)
        dn = (dhv * one_sc) * g_ref[...]
        dv = d_ref[...] + r * (dn - n * jnp.mean(dn * n, axis=-1, keepdims=True))
        dx_ref[...] = dv

        yv = y_ref[...]
        ry = _rstd(yv)
        ny = yv * ry
        sum_ref[3:4, :] += jnp.sum(dv * (ny * gp_ref[...]), axis=0, keepdims=True)
        sum_ref[4:5, :] += jnp.sum((dv * gate_ref[...]) * ny, axis=0, keepdims=True)
        dny = (dv * gate_ref[...]) * gp_ref[...]
        dy_ref[...] = (ry * (dny - ny * jnp.mean(dny * ny, axis=-1, keepdims=True))).astype(BF16)

    vecs, vec_specs = _vec_args(g_pre, scale, g_post, gate)
    return pl.pallas_call(
        body, name=name, grid=(s // ROW_TILE,),
        in_specs=[_row_spec(D)] * 4 + vec_specs,
        out_specs=[_row_spec(D), _row_spec(D), _sum_spec(D)],
        out_shape=[jax.ShapeDtypeStruct((s, D), F32), jax.ShapeDtypeStruct((s, D), BF16), jax.ShapeDtypeStruct((8, D), F32)],
        compiler_params=_params("arbitrary"),
    )(dh, x, dxo, y, *vecs)


def _loss_head(y, target):
    s = y.shape[0]

    def body(y_ref, t_ref, dy_ref, sum_ref):
        @pl.when(pl.program_id(0) == 0)
        def _():
            sum_ref[...] = jnp.zeros_like(sum_ref)

        err = y_ref[...] - t_ref[...]
        dy_ref[...] = err * (1.0 / D)
        sum_ref[...] += jnp.sum(err * err)

    return pl.pallas_call(
        body, name="loss_head", grid=(s // ROW_TILE,),
        in_specs=[_row_spec(D), _row_spec(D)],
        out_specs=[_row_spec(D), pl.BlockSpec((8, 128), lambda i: (0, 0))],
        out_shape=[jax.ShapeDtypeStruct((s, D), F32), jax.ShapeDtypeStruct((8, 128), F32)],
        compiler_params=_params("arbitrary"),
    )(y, target)


def _merge_fwd(z, pa, pb, pc):
    s = z.shape[0]

    def body(g0_ref, g1_ref, g2_ref, pa_ref, pb_ref, pc_ref, o_ref):
        o_ref[...] = (jax.nn.sigmoid(g0_ref[...]) * pa_ref[...] + jax.nn.sigmoid(g1_ref[...]) * pb_ref[...]
                      + jax.nn.sigmoid(g2_ref[...]) * pc_ref[...]).astype(BF16)

    return pl.pallas_call(
        body, name="merge_fwd", grid=(s // ROW_TILE,),
        in_specs=[_row_spec(D, 0), _row_spec(D, 1), _row_spec(D, 2), _row_spec(D), _row_spec(D), _row_spec(D)],
        out_specs=_row_spec(D), out_shape=jax.ShapeDtypeStruct((s, D), BF16),
        compiler_params=_params("parallel"),
    )(z, z, z, pa, pb, pc)


def _merge_bwd(dm, z, pa, pb, pc):
    s = z.shape[0]

    def body(dm_ref, g0_ref, g1_ref, g2_ref, pa_ref, pb_ref, pc_ref, dgl_ref, da_ref, db_ref, dc_ref):
        dmv = dm_ref[...]
        for i, (g_ref, p_ref, d_ref) in enumerate(((g0_ref, pa_ref, da_ref), (g1_ref, pb_ref, db_ref), (g2_ref, pc_ref, dc_ref))):
            gate = jax.nn.sigmoid(g_ref[...])
            dgl_ref[:, i * D:(i + 1) * D] = ((dmv * p_ref[...]) * (gate * (1.0 - gate))).astype(BF16)
            d_ref[...] = (dmv * gate).astype(BF16)

    return pl.pallas_call(
        body, name="merge_bwd", grid=(s // ROW_TILE,),
        in_specs=[_row_spec(D), _row_spec(D, 0), _row_spec(D, 1), _row_spec(D, 2), _row_spec(D), _row_spec(D), _row_spec(D)],
        out_specs=[_row_spec(3 * D), _row_spec(D), _row_spec(D), _row_spec(D)],
        out_shape=[jax.ShapeDtypeStruct((s, Z_COLS), BF16)] + [jax.ShapeDtypeStruct((s, D), BF16)] * 3,
        compiler_params=_params("parallel"),
    )(dm, z, z, z, pa, pb, pc)


def _shift_down(v, n):
    row = lax.broadcasted_iota(jnp.int32, v.shape, 0)
    return jnp.where(row >= n, pltpu.roll(v, n, axis=0), 0.0)


def _shift_up(v, n):
    s = v.shape[0]
    row = lax.broadcasted_iota(jnp.int32, v.shape, 0)
    return jnp.where(row < s - n, pltpu.roll(v, s - n, axis=0), 0.0)


def _log_sigmoid(v):
    return jnp.minimum(v, 0.0) - jnp.log1p(jnp.exp(-jnp.abs(v)))


def _cumf_fwd(fl, bias):
    s = fl.shape[0]

    def body(fl_ref, b_ref, o_ref):
        acc = _log_sigmoid(fl_ref[...] + b_ref[...])
        step = 1
        while step < s:
            acc = acc + _shift_down(acc, step)
            step *= 2
        o_ref[...] = acc

    return pl.pallas_call(body, name="cumf_fwd", out_shape=jax.ShapeDtypeStruct((s, 128), F32),
                          compiler_params=pltpu.CompilerParams(vmem_limit_bytes=V7X_VMEM_LIMIT))(fl, bias)


def _cumf_bwd(dcum, fl, bias):
    s = fl.shape[0]

    def body(d_ref, fl_ref, b_ref, dfl_ref, db_ref):
        acc = d_ref[...]
        step = 1
        while step < s:
            acc = acc + _shift_up(acc, step)
            step *= 2
        dfl = acc * jax.nn.sigmoid(-(fl_ref[...] + b_ref[...]))
        dfl_ref[...] = dfl.astype(BF16)
        db_ref[...] = jnp.broadcast_to(jnp.sum(dfl, axis=0, keepdims=True), (8, 128))

    return pl.pallas_call(
        body, name="cumf_bwd",
        out_shape=[jax.ShapeDtypeStruct((s, 128), BF16), jax.ShapeDtypeStruct((8, 128), F32)],
        compiler_params=pltpu.CompilerParams(vmem_limit_bytes=V7X_VMEM_LIMIT))(dcum, fl, bias)


def _pool_windows(v, shift):
    s2 = v + shift(v, 1)
    s4 = s2 + shift(s2, 2)
    s8 = s4 + shift(s4, 4)
    s16 = s8 + shift(s8, 8)
    group = lax.broadcasted_iota(jnp.int32, v.shape, 1) // 64
    return jnp.where(group == 0, s2, jnp.where(group == 1, s4, jnp.where(group == 2, s8, s16)))


def _pool_count(shape):
    group = lax.broadcasted_iota(jnp.int32, shape, 1) // 64
    window = jnp.where(group == 0, 2.0, jnp.where(group == 1, 4.0, jnp.where(group == 2, 8.0, 16.0)))
    t1 = (lax.broadcasted_iota(jnp.int32, shape, 0) + 1).astype(F32)
    return jnp.minimum(t1, window)


def _pc_specs(s):
    zcol = lambda blk: pl.BlockSpec((s, 256), lambda i, blk=blk: (0, blk))
    first = Z_PC // 256
    return [zcol(first), zcol(first + 1), zcol(first + 2), zcol(first + 3),
            pl.BlockSpec((256, 256), lambda i: (0, 0)), pl.BlockSpec((1, 256), lambda i: (0, 0)),
            pl.BlockSpec((3, 256), lambda i: (0, 0))]


def _poolconv_fwd(z, wbd, pscale, convw):
    s = z.shape[0]

    def body(pu_ref, ch_ref, cb_ref, cc_ref, w_ref, ps_ref, cw_ref, yb_ref, yc_ref):
        u = pu_ref[...]
        p = _pool_windows(u, _shift_down) / _pool_count(u.shape) - u
        yb = jnp.dot(p.astype(BF16), w_ref[...].astype(BF16), preferred_element_type=F32) * ps_ref[...]
        yb_ref[...] = yb.astype(BF16)
        uc = cc_ref[...] * ch_ref[...]
        cw = cw_ref[...]
        conv = cw[0:1, :] * _shift_down(uc, 2) + cw[1:2, :] * _shift_down(uc, 1) + cw[2:3, :] * uc
        yc_ref[...] = (cb_ref[...] * conv).astype(BF16)

    out = pl.BlockSpec((s, 256), lambda i: (0, 0))
    return pl.pallas_call(
        body, name="poolconv_fwd", grid=(1,), in_specs=_pc_specs(s), out_specs=[out, out],
        out_shape=[jax.ShapeDtypeStruct((s, 256), BF16)] * 2, compiler_params=_params("arbitrary"),
    )(z, z, z, z, wbd, pscale, convw)


def _poolconv_bwd(dyb, dyc, z, wbd, pscale, convw):
    s = z.shape[0]

    def body(dyb_ref, dyc_ref, pu_ref, ch_ref, cb_ref, cc_ref, w_ref, ps_ref, cw_ref, dz_ref, dw_ref, dps_ref, dcw_ref):
        u = pu_ref[...]
        count = _pool_count(u.shape)
        p = (_pool_windows(u, _shift_down) / count - u).astype(BF16)
        wb = w_ref[...].astype(BF16)
        dyb_v = dyb_ref[...]
        pw = jnp.dot(p, wb, preferred_element_type=F32)
        dps_ref[...] = jnp.broadcast_to(jnp.sum(dyb_v * pw, axis=0, keepdims=True), (8, 256))
        dys = (dyb_v * ps_ref[...]).astype(BF16)
        dp = lax.dot_general(dys, wb, (((1,), (1,)), ((), ())), preferred_element_type=F32)
        dw_ref[...] = lax.dot_general(p, dys, (((0,), (0,)), ((), ())), preferred_element_type=F32)
        dz_ref[:, 0:256] = (_pool_windows(dp / count, _shift_up) - dp).astype(BF16)

        ch, cb, cc = ch_ref[...], cb_ref[...], cc_ref[...]
        uc = cc * ch
        cw = cw_ref[...]
        u2, u1 = _shift_down(uc, 2), _shift_down(uc, 1)
        conv = cw[0:1, :] * u2 + cw[1:2, :] * u1 + cw[2:3, :] * uc
        dyc_v = dyc_ref[...]
        dconv = dyc_v * cb
        du = cw[0:1, :] * _shift_up(dconv, 2) + cw[1:2, :] * _shift_up(dconv, 1) + cw[2:3, :] * dconv
        dz_ref[:, 256:512] = (du * cc).astype(BF16)
        dz_ref[:, 512:768] = (dyc_v * conv).astype(BF16)
        dz_ref[:, 768:1024] = (du * ch).astype(BF16)
        dcw_ref[...] = jnp.zeros_like(dcw_ref)
        dcw_ref[0:1, :] = jnp.sum(dconv * u2, axis=0, keepdims=True)
        dcw_ref[1:2, :] = jnp.sum(dconv * u1, axis=0, keepdims=True)
        dcw_ref[2:3, :] = jnp.sum(dconv * uc, axis=0, keepdims=True)

    blk = lambda r, c: pl.BlockSpec((r, c), lambda i: (0, 0))
    return pl.pallas_call(
        body, name="poolconv_bwd", grid=(1,),
        in_specs=[blk(s, 256), blk(s, 256)] + _pc_specs(s),
        out_specs=[blk(s, 1024), blk(256, 256), blk(8, 256), blk(8, 256)],
        out_shape=[jax.ShapeDtypeStruct((s, 1024), BF16), jax.ShapeDtypeStruct((256, 256), F32),
                   jax.ShapeDtypeStruct((8, 256), F32), jax.ShapeDtypeStruct((8, 256), F32)],
        compiler_params=_params("arbitrary"),
    )(dyb, dyc, z, z, z, z, wbd, pscale, convw)


_NT = (((1,), (1,)), ((), ()))
_TN = (((0,), (0,)), ((), ()))


ATT_Q, ATT_K = 256, 256
ATT_HEADS_BWD = 8
ATT_HEADS = 8


def _att_logits(q, k, fr, q0, k0, masked):
    logits = lax.dot_general(q, k, _NT, preferred_element_type=F32) - fr
    if not masked:
        return logits
    row = q0 + lax.broadcasted_iota(jnp.int32, logits.shape, 0)
    col = k0 + lax.broadcasted_iota(jnp.int32, logits.shape, 1)
    return jnp.where(row >= col, logits, NEG_INF)


def _causal_sweep(step, qi, init):
    n_full = (qi * ATT_Q) // ATT_K
    carry = lax.fori_loop(0, n_full, lambda j, carry: step(j, carry, False), init)
    return step(n_full, carry, True)


HEAD_PAIRS = HEADS // 2


def _lane_pick(v, lane, idx):
    return jnp.sum(jnp.where(lane == idx, v, 0.0), axis=-1, keepdims=True)


def _lane_put(lane, idx, col):
    return jnp.where(lane == idx, col, 0.0)


def _split_heads(v, low):
    zero = jnp.zeros_like(v)
    return jnp.where(low, v, zero), jnp.where(low, zero, v)


def _attn_fwd(qkv, fr):
    s = qkv.shape[0]
    nk = s // ATT_K
    width = ATT_HEADS * HEAD_DIM
    groups = HEADS // ATT_HEADS

    def body(q_ref, k_ref, v_ref, fr_ref, o_ref, lse_ref):
        qi, grp = pl.program_id(0), pl.program_id(1)
        lane = lax.broadcasted_iota(jnp.int32, (ATT_Q, 128), 1)
        low = lane < HEAD_DIM
        qs = []
        for pr in range(ATT_HEADS // 2):
            qs += _split_heads(q_ref[:, 128 * pr:128 * (pr + 1)] * (HEAD_DIM ** -0.5), low)

        def step(j, carry, masked):
            k0 = pl.multiple_of(j * ATT_K, ATT_K)
            out = []
            for h in range(ATT_HEADS):
                cols = slice(128 * (h // 2), 128 * (h // 2 + 1))
                m, l, acc = carry[h]
                logits = _att_logits(qs[h], k_ref[pl.ds(k0, ATT_K), cols], fr_ref[h, pl.ds(j, 1), :], qi * ATT_Q, k0, masked)
                m_new = jnp.maximum(m, jnp.max(logits, axis=-1, keepdims=True))
                p = jnp.exp(logits - m_new)
                alpha = jnp.exp(m - m_new)
                l = alpha * l + jnp.sum(p, axis=-1, keepdims=True)
                acc = alpha * acc + jnp.dot(p.astype(BF16), v_ref[pl.ds(k0, ATT_K), cols], preferred_element_type=F32)
                out.append((m_new, l, acc))
            return tuple(out)

        one = (jnp.full((ATT_Q, 1), NEG_INF, F32), jnp.zeros((ATT_Q, 1), F32), jnp.zeros((ATT_Q, 128), F32))
        done = _causal_sweep(step, qi, (one,) * ATT_HEADS)

        @pl.when(grp == 0)
        def _():
            lse_ref[...] = jnp.zeros_like(lse_ref)

        lse = jnp.zeros((ATT_Q, 128), F32)
        for pr in range(ATT_HEADS // 2):
            (m0, l0, acc0), (m1, l1, acc1) = done[2 * pr], done[2 * pr + 1]
            o_ref[:, 128 * pr:128 * (pr + 1)] = jnp.where(low, acc0 / l0, acc1 / l1)
            head = ATT_HEADS * grp + 2 * pr
            lse = lse + _lane_put(lane, head, m0 + jnp.log(l0)) + _lane_put(lane, head + 1, m1 + jnp.log(l1))
        lse_ref[...] += lse

    return pl.pallas_call(
        body, name="attn_fwd", grid=(s // ATT_Q, groups),
        in_specs=[pl.BlockSpec((ATT_Q, width), lambda i, g: (i, g)),
                  pl.BlockSpec((s, width), lambda i, g: (0, groups + g)),
                  pl.BlockSpec((s, width), lambda i, g: (0, 2 * groups + g)),
                  pl.BlockSpec((ATT_HEADS, nk, ATT_K), lambda i, g: (g, 0, 0))],
        out_specs=[pl.BlockSpec((ATT_Q, width), lambda i, g: (i, g)), pl.BlockSpec((ATT_Q, 128), lambda i, g: (i, 0))],
        out_shape=[jax.ShapeDtypeStruct((s, A_WIDTH), F32), jax.ShapeDtypeStruct((s, 128), F32)],
        compiler_params=_params("parallel", "arbitrary"),
    )(qkv, qkv, qkv, fr)


def _attn_bwd(qkv, do, o, lse, fr):
    s = qkv.shape[0]
    nk = s // ATT_K
    scale = HEAD_DIM ** -0.5
    heads = ATT_HEADS_BWD
    width = heads * HEAD_DIM
    groups = HEADS // heads

    def body(q_ref, k_ref, v_ref, do_ref, o_ref, lse_ref, fr_ref, dq_ref, dk_ref, dv_ref, dfc_ref, dfr_ref, dk_acc, dv_acc):
        grp = pl.program_id(0)
        lane = lax.broadcasted_iota(jnp.int32, (ATT_Q, 128), 1)
        low = lane < HEAD_DIM
        low_t = lax.broadcasted_iota(jnp.int32, (128, ATT_Q), 0) < HEAD_DIM
        dk_acc[...] = jnp.zeros_like(dk_acc)
        dv_acc[...] = jnp.zeros_like(dv_acc)
        dfr_ref[...] = jnp.zeros_like(dfr_ref)

        @pl.when(grp == 0)
        def _():
            dfc_ref[...] = jnp.zeros_like(dfc_ref)

        def outer(i, carry):
            q0 = pl.multiple_of(i * ATT_Q, ATT_Q)
            rows = pl.ds(q0, ATT_Q)
            lsev = lse_ref[rows, :]
            qts, dots, qs, dos, deltas, lses = [], [], [], [], [], []
            for pr in range(heads // 2):
                pcols = slice(128 * pr, 128 * (pr + 1))
                q2, do2 = q_ref[rows, pcols] * scale, do_ref[rows, pcols]
                prod = do2 * o_ref[rows, pcols]
                deltas += [jnp.sum(jnp.where(low, prod, 0.0), axis=-1, keepdims=True),
                           jnp.sum(jnp.where(low, 0.0, prod), axis=-1, keepdims=True)]
                dob2 = do2.astype(BF16)
                qts += _split_heads(q2.astype(F32).T.astype(BF16), low_t)
                dots += _split_heads(do2.T.astype(BF16), low_t)
                qs += _split_heads(q2, low)
                dos += _split_heads(dob2, low)
                lses += [_lane_pick(lsev, lane, heads * grp + 2 * pr), _lane_pick(lsev, lane, heads * grp + 2 * pr + 1)]

            def inner(j, carry, masked):
                k0 = pl.multiple_of(j * ATT_K, ATT_K)
                krows = pl.ds(k0, ATT_K)
                out, dkt, dvt = [], [], []
                for h in range(heads):
                    pcols = slice(128 * (h // 2), 128 * (h // 2 + 1))
                    dq, dfc = carry[h]
                    k2 = k_ref[krows, pcols]
                    p = jnp.exp(_att_logits(qs[h], k2, fr_ref[h, pl.ds(j, 1), :], q0, k0, masked) - lses[h])
                    dp = lax.dot_general(dos[h], v_ref[krows, pcols], _NT, preferred_element_type=F32)
                    ds = p * (dp - deltas[h])
                    dsb = ds.astype(BF16)
                    dkt.append(jnp.dot(qts[h], dsb, preferred_element_type=F32))
                    dvt.append(jnp.dot(dots[h], p.astype(BF16), preferred_element_type=F32))
                    dfr_ref[h, pl.ds(j, 1), :] -= jnp.sum(ds, axis=0, keepdims=True)
                    out.append((dq + jnp.dot(dsb, k2, preferred_element_type=F32), dfc + (ds[:, :128] + ds[:, 128:])))
                for pr in range(heads // 2):
                    prows = slice(128 * pr, 128 * (pr + 1))
                    dk_acc[j, prows, :] += dkt[2 * pr] + dkt[2 * pr + 1]
                    dv_acc[j, prows, :] += dvt[2 * pr] + dvt[2 * pr + 1]
                return tuple(out)

            one = (jnp.zeros((ATT_Q, 128), F32), jnp.zeros((ATT_Q, 128), F32))
            done = _causal_sweep(inner, i, (one,) * heads)
            dfc = jnp.zeros((ATT_Q, 128), F32)
            for pr in range(heads // 2):
                (dq0, dfc0), (dq1, dfc1) = done[2 * pr], done[2 * pr + 1]
                dq_ref[rows, 128 * pr:128 * (pr + 1)] = (jnp.where(low, dq0, dq1) * scale).astype(BF16)
                head = heads * grp + 2 * pr
                dfc = (dfc + _lane_put(lane, head, jnp.sum(dfc0, axis=-1, keepdims=True))
                       + _lane_put(lane, head + 1, jnp.sum(dfc1, axis=-1, keepdims=True)))
            dfc_ref[rows, :] += dfc
            return carry

        lax.fori_loop(0, s // ATT_Q, outer, 0)
        for j in range(nk):
            for pr in range(heads // 2):
                prows, pcols = slice(128 * pr, 128 * (pr + 1)), slice(128 * pr, 128 * (pr + 1))
                dk_ref[ATT_K * j:ATT_K * (j + 1), pcols] = dk_acc[j, prows, :].T.astype(BF16)
                dv_ref[ATT_K * j:ATT_K * (j + 1), pcols] = dv_acc[j, prows, :].T.astype(BF16)

    part = lambda first: pl.BlockSpec((s, width), lambda g, first=first: (0, first + g))
    whole = pl.BlockSpec((s, 128), lambda g: (0, 0))
    rowv = pl.BlockSpec((heads, nk, ATT_K), lambda g: (g, 0, 0))
    return pl.pallas_call(
        body, name="attn_bwd", grid=(groups,),
        in_specs=[part(0), part(groups), part(2 * groups), part(0), part(0), whole, rowv],
        out_specs=[part(0), part(0), part(0), whole, rowv],
        out_shape=[jax.ShapeDtypeStruct((s, A_WIDTH), BF16)] * 3 + [jax.ShapeDtypeStruct((s, 128), F32), jax.ShapeDtypeStruct((HEADS, nk, ATT_K), F32)],
        scratch_shapes=[pltpu.VMEM((nk, width, ATT_K), F32), pltpu.VMEM((nk, width, ATT_K), F32)],
        compiler_params=_params("arbitrary"),
    )(qkv, qkv, qkv, do, o, lse, fr)


def _ada_fwd(c_all, w_ada, b_loc):
    depth, _, n = w_ada.shape
    tn = 512

    def body(c_ref, w_ref, b_ref, o_ref, sc_ref):
        cv = c_ref[...]
        sc = cv * jax.nn.sigmoid(cv)
        sc_ref[...] = sc
        o_ref[0] = jnp.dot(sc.astype(BF16), w_ref[0].astype(BF16), preferred_element_type=F32) + b_ref[0]

    return pl.pallas_call(
        body, name="ada_fwd", grid=(depth, n // tn),
        in_specs=[pl.BlockSpec((N_DEV, D), lambda l, j: (0, 0)), pl.BlockSpec((1, D, tn), lambda l, j: (l, 0, j)),
                  pl.BlockSpec((1, 1, tn), lambda l, j: (l, 0, j))],
        out_specs=[pl.BlockSpec((1, N_DEV, tn), lambda l, j: (l, 0, j)), pl.BlockSpec((N_DEV, D), lambda l, j: (0, 0))],
        out_shape=[jax.ShapeDtypeStruct((depth, N_DEV, n), F32), jax.ShapeDtypeStruct((N_DEV, D), F32)],
        compiler_params=_params("arbitrary", "arbitrary"),
    )(c_all, w_ada, b_loc)


def _sum_devices(gathered):
    n = gathered.shape[1]
    tn = _pick(n, (1408, 1024, 640, 512, 128))

    def body(g_ref, o_ref):
        acc = g_ref[0:8, :]
        for dev in range(1, N_DEV):
            acc = acc + g_ref[8 * dev:8 * dev + 8, :]
        o_ref[...] = acc

    return pl.pallas_call(
        body, name="sum_devices", grid=(n // tn,),
        in_specs=[pl.BlockSpec((8 * N_DEV, tn), lambda j: (0, j))], out_specs=pl.BlockSpec((8, tn), lambda j: (0, j)),
        out_shape=jax.ShapeDtypeStruct((8, n), F32), compiler_params=_params("parallel"),
    )(gathered)


def _place():
    x, y, c = lax.axis_index("x"), lax.axis_index("y"), lax.axis_index("c")
    chips = [(1 - x, y), (x, 1 - y), (1 - x, 1 - y)]
    return x, y, c, chips


def _allgather8(block, name, after=()):
    m_per, n = block.shape

    def body(x_ref, *rest):
        out_ref, send_sems, recv_sems, local_sem = rest[len(after):]
        x, y, c, chips = _place()
        me, sibling = (x, y, c), (x, y, 1 - c)

        def rows(px, py, pc):
            return out_ref.at[pl.ds((4 * px + 2 * py + pc) * m_per, m_per), :]

        def copy(k, blk, to, src=None):
            return pltpu.make_async_remote_copy(
                src_ref=rows(*blk) if src is None else src, dst_ref=rows(*blk),
                send_sem=send_sems.at[k], recv_sem=recv_sems.at[k], device_id=to, device_id_type=MESH)

        mine = pltpu.make_async_copy(x_ref, rows(*me), local_sem)
        mine.start()
        first = [copy(0, me, sibling, src=x_ref)]
        first += [copy(1 + j, me, (*chip, c), src=x_ref) for j, chip in enumerate(chips)]
        for cp in first:
            cp.start()
        passed = [copy(4 + j, (*chip, c), sibling) for j, chip in enumerate(chips)]
        for j, chip in enumerate(chips):
            copy(1 + j, (*chip, c), me).wait_recv()
            passed[j].start()
        copy(0, sibling, me).wait_recv()
        for j, chip in enumerate(chips):
            copy(4 + j, (*chip, 1 - c), me).wait_recv()
        for cp in first + passed:
            cp.wait_send()
        mine.wait()

    return pl.pallas_call(
        body, name=name, out_shape=jax.ShapeDtypeStruct((N_DEV * m_per, n), block.dtype),
        in_specs=[pl.BlockSpec(memory_space=pltpu.VMEM)] + [pl.BlockSpec(memory_space=pl.ANY)] * len(after),
        out_specs=pl.BlockSpec(memory_space=pltpu.VMEM),
        scratch_shapes=[pltpu.SemaphoreType.DMA((7,)), pltpu.SemaphoreType.DMA((7,)), pltpu.SemaphoreType.DMA],
        compiler_params=pltpu.CompilerParams(vmem_limit_bytes=V7X_VMEM_LIMIT),
    )(block, *after)


_SEM = pl.BlockSpec(memory_space=pltpu.SEMAPHORE)
_DATAFLOW = pltpu.SideEffectType.DATAFLOW_SIDE_EFFECTING


def _plan_copies(plan, refs, send_sems, recv_sems):
    return [pltpu.make_async_remote_copy(src_ref=src, dst_ref=dst, send_sem=send_sems.at[i], recv_sem=recv_sems.at[i],
                                         device_id=to, device_id_type=MESH) for i, (src, dst, to) in enumerate(plan(refs))]


class _Token(NamedTuple):
    after: jax.Array
    tie: jax.Array


def _after_operand(after):
    return after.after if isinstance(after, _Token) else after


def _copies_start(bufs, plan, n_copies, after, name):
    nb = len(bufs)

    def body(*refs):
        for cp in _plan_copies(plan, refs[:nb], refs[nb + 1], refs[nb + 2]):
            cp.start()
        for token in refs[-2:]:
            token[...] = jnp.zeros_like(token)

    sem = pltpu.SemaphoreType.DMA((n_copies,))
    vmem = pl.BlockSpec(memory_space=pltpu.VMEM)
    outs = pl.pallas_call(
        body, name=name,
        out_shape=(sem, sem, *[pltpu.HBM(b.shape, b.dtype) for b in bufs], jax.ShapeDtypeStruct((8, 128), F32),
                   jax.ShapeDtypeStruct((1, 1), F32)),
        in_specs=[_HBM] * nb + [pl.BlockSpec(memory_space=pl.ANY)],
        out_specs=(_SEM, _SEM, *[_HBM] * nb, vmem, vmem),
        input_output_aliases={i: 2 + i for i in range(nb)},
        compiler_params=pltpu.CompilerParams(has_side_effects=_DATAFLOW),
    )(*[pltpu.with_memory_space_constraint(b, pltpu.HBM) for b in bufs], _after_operand(after))
    return outs[0], outs[1], list(outs[2:2 + nb]), _Token(outs[-2], outs[-1])


def _copies_wait(started, plan, after, name):
    send_sems, recv_sems, bufs, _ = started
    nb = len(bufs)

    def body(*refs):
        for cp in _plan_copies(plan, refs[:nb], refs[nb], refs[nb + 1]):
            cp.wait_send()
            cp.wait_recv()

    return list(pl.pallas_call(
        body, name=name, out_shape=tuple(pltpu.HBM(b.shape, b.dtype) for b in bufs),
        in_specs=[_HBM] * nb + [_SEM, _SEM, pl.BlockSpec(memory_space=pl.ANY)], out_specs=tuple([_HBM] * nb),
        input_output_aliases={i: i for i in range(nb)},
        compiler_params=pltpu.CompilerParams(has_side_effects=_DATAFLOW),
    )(*bufs, send_sems, recv_sems, _after_operand(after)))


def _half_rows(ref, axis, c):
    half = ref.shape[axis] // 2
    return pl.ds(c * half, half)


def _plan_gather_ici(refs):
    n = len(refs) // 2
    x, y, c, chips = _place()
    out = []
    for a in range(n):
        rows = _half_rows(refs[a], 0, c)
        out += [(refs[a].at[rows], refs[n + a].at[2 * x + y, rows], (*chip, c)) for chip in chips]
        out.append((refs[a], refs[n + a].at[2 * x + y], (x, y, 1 - c)))
    return out


def _plan_gather_d2d(refs):
    x, y, c, chips = _place()
    out = []
    for ref in refs:
        rows = _half_rows(ref, 1, c)
        for px, py in chips:
            landed = ref.at[2 * px + py, rows]
            out.append((landed, landed, (x, y, 1 - c)))
    return out


def _plan_rs_sibling(refs):
    n = len(refs) // 2
    x, y, c, _ = _place()
    return [(refs[a].at[pl.ds(0, refs[a].shape[0]), _half_rows(refs[a], 1, 1 - c)], refs[n + a], (x, y, 1 - c)) for a in range(n)]


def _plan_rs_chips(refs):
    n = len(refs) // 2
    x, y, c, chips = _place()
    return [(refs[a].at[2 * px + py], refs[n + a].at[k], (px, py, c)) for a in range(n) for k, (px, py) in enumerate(chips)]


def _plan_rs_share(refs):
    x, y, c, _ = _place()
    return [(ref.at[_half_rows(ref, 0, c)], ref.at[_half_rows(ref, 0, c)], (x, y, 1 - c)) for ref in refs]


def _chip_sum(g, other, sel, name, blocked=True):
    nblk, half, cdim = other.shape
    tr = _pick(half, (512, 256, 128, 64) if nblk > 1 else (128, 64))
    per = half // tr

    def body(sel_ref, g_ref, t_ref, wire_ref, own_ref):
        total = g_ref[0] + t_ref[0]
        wire_ref[0] = total.astype(BF16)
        if blocked:
            @pl.when(pl.program_id(1) == sel_ref[1])
            def _():
                own_ref[...] = total
        else:
            own_ref[0] = total

    blk = pl.BlockSpec((1, tr, cdim), lambda i, p, sel_ref: (p, i, 0))
    own_spec = pl.BlockSpec((tr, cdim), lambda i, p, sel_ref: (i, 0)) if blocked else blk
    own_shape = jax.ShapeDtypeStruct((half, cdim) if blocked else other.shape, F32)
    return pl.pallas_call(
        body, name=name,
        grid_spec=pltpu.PrefetchScalarGridSpec(
            num_scalar_prefetch=1, grid=(per, nblk),
            in_specs=[pl.BlockSpec((1, tr, cdim), lambda i, p, sel_ref: (p, sel_ref[0] * per + i, 0)), blk],
            out_specs=[blk, own_spec]),
        out_shape=[jax.ShapeDtypeStruct(other.shape, BF16), own_shape],
        compiler_params=_params("parallel", "arbitrary"),
    )(sel, g, other)


def _final_sum(own, recv, sel, name):
    half, cdim = own.shape
    tr = _pick(half, (512, 256, 128, 64))
    per = half // tr

    def body(sel_ref, own_ref, r0_ref, r1_ref, r2_ref, o_ref):
        o_ref[...] = ((own_ref[...] + r0_ref[0].astype(F32)) + r1_ref[0].astype(F32)) + r2_ref[0].astype(F32)

    part = lambda k: pl.BlockSpec((1, tr, cdim), lambda i, sel_ref, k=k: (k, i, 0))
    return pl.pallas_call(
        body, name=name,
        grid_spec=pltpu.PrefetchScalarGridSpec(
            num_scalar_prefetch=1, grid=(per,),
            in_specs=[pl.BlockSpec((tr, cdim), lambda i, sel_ref: (i, 0)), part(0), part(1), part(2)],
            out_specs=pl.BlockSpec((tr, cdim), lambda i, sel_ref: (sel_ref[0] * per + i, 0))),
        out_shape=jax.ShapeDtypeStruct((2 * half, cdim), F32), compiler_params=_params("parallel"),
    )(sel, own, recv, recv, recv)


def _row(v):
    return v.reshape(1, -1)


_BR_A, _BR_B, _BR_C = (0, A_WIDTH), (A_WIDTH, POOL_WIDTH), (A_WIDTH + POOL_WIDTH, CONV_WIDTH)


def _tie(v, token):
    return v if token is None else v + token.tie


def _no_hook(point, after, ready=None):
    return None


def _layer_fwd(x, w, mod, hook=_no_hook):
    s = x.shape[0]
    mod3 = mod.reshape(6, 1, D)
    h = _modnorm_fwd(x, _row(w["g_mix_pre"]), (mod3, 0), (mod3, 1), "mix_pre_fwd")
    hook("pre", h)
    z = _mm(h, w["w_all"], name="mm_in")
    qkv = z[:, Z_QKV:Z_PC].astype(BF16)
    fl = z[:, Z_FL:Z_COLS]
    cum = _cumf_fwd(fl, w["b_f_pad"])
    fr = cum[:, :HEADS].T.reshape(HEADS, s // ATT_K, ATT_K)
    br_a, lse = _attn_fwd(qkv, fr)
    br_b, br_c = _poolconv_fwd(z, w["w_pool_bd"], _tie(_row(w["pool_scale"]), hook("attn", lse)), w["conv_w"])
    hook("pool", br_b)
    wbr = w["w_branch"]
    pa = _mm(br_a, wbr, b_rows=_BR_A, name="mm_br_a")
    pb = _mm(br_b, wbr, b_rows=_BR_B, name="mm_br_b")
    pc = _mm(br_c, wbr, b_rows=_BR_C, name="mm_br_c")
    merged = _merge_fwd(z, pa, pb, pc)
    y = _mm(merged, w["w_out"], name="mm_out")
    x1, h2 = _post_pre_fwd(x, y, _row(w["g_mix_post"]), (mod3, 2), _row(w["g_ff_pre"]), (mod3, 3), (mod3, 4), "mix_post_ff_pre_fwd")
    a, r = _mm(h2, w["w_ff1"], b_split=N_CHIPS, epilogue=_relu2_fwd, out_dtype=(F32, BF16), name="mm_ff1")
    y2 = _mm(r, w["w_ff2"], name="mm_ff2")
    x2 = _post_fwd(x1, y2, _tie(_row(w["g_ff_post"]), hook("ff_post", y2)), (mod3, 5), "ff_post_fwd")
    hook("end", x2)
    saved = dict(x=x, h=h, z=z, qkv=qkv, fl=fl, fr=fr, lse=lse, br_a=br_a, br_b=br_b, br_c=br_c, pa=pa, pb=pb, pc=pc,
                 merged=merged, y=y, x1=x1, h2=h2, a=a, r=r, y2=y2)
    return x2, saved


def _layer_bwd(dx2, sv, w, mod, hook=_no_hook):
    s = dx2.shape[0]
    mod3 = mod.reshape(6, 1, D)
    dy2, sum_ff_post = _post_bwd(dx2, sv["y2"], _row(w["g_ff_post"]), (mod3, 5), "ff_post_bwd")
    (da,) = _mm(dy2, w["w_ff2"], tb=True, epilogue=_relu2_bwd, extras=(sv["a"],), out_dtype=(BF16,), name="mm_ff2_dx")
    d_w_ff2 = _mm(sv["r"], dy2, ta=True, name="mm_ff2_dw")
    dh2 = _mm(da, w["w_ff1"], tb=True, b_split=N_CHIPS, name="mm_ff1_dx")
    d_w_ff1 = _mm(sv["h2"], da, ta=True, out_split=N_CHIPS, name="mm_ff1_dw")
    g_ff_pre = _tie(_row(w["g_ff_pre"]), hook("ff_pre", dh2, dict(w_ff1=d_w_ff1, w_ff2=d_w_ff2)))
    dx1, dy, sum_mid = _pre_post_bwd(dh2, sv["x1"], dx2, g_ff_pre, (mod3, 4), sv["y"], _row(w["g_mix_post"]), (mod3, 2), "ff_pre_mix_post_bwd")
    sum_ff_pre, sum_mix_post = sum_mid, sum_mid[3:]
    dmerged = _mm(dy, w["w_out"], tb=True, name="mm_out_dx")
    d_w_out = _mm(sv["merged"], dy, ta=True, name="mm_out_dw")
    dz, dpa, dpb, dpc = _merge_bwd(dmerged, sv["z"], sv["pa"], sv["pb"], sv["pc"])
    wbr = w["w_branch"]
    dbr_a = _mm(dpa, wbr, tb=True, b_rows=_BR_A, name="mm_br_a_dx")
    dbr_b = _mm(dpb, wbr, tb=True, b_rows=_BR_B, name="mm_br_b_dx")
    dbr_c = _mm(dpc, wbr, tb=True, b_rows=_BR_C, name="mm_br_c_dx")
    d_w_branch = _stacked_dw((sv["br_a"], sv["br_b"], sv["br_c"]), (dpa, dpb, dpc), "mm_br_dw")

    dq, dk, dv, dfc, dfr = _attn_bwd(sv["qkv"], dbr_a, sv["br_a"], sv["lse"], sv["fr"])
    dcum = dfc + jnp.pad(dfr.reshape(HEADS, s).T, ((0, 0), (0, 128 - HEADS)))
    dfl, sum_bf = _cumf_bwd(dcum, sv["fl"], _tie(w["b_f_pad"], hook("cumf", dfc)))
    dpc_z, d_wbd, sum_ps, sum_cw = _poolconv_bwd(dbr_b, dbr_c, sv["z"], w["w_pool_bd"], _row(w["pool_scale"]), w["conv_w"])
    for at, part in ((Z_QKV, dq), (Z_QKV + A_WIDTH, dk), (Z_QKV + 2 * A_WIDTH, dv), (Z_PC, dpc_z), (Z_FL, dfl)):
        dz = lax.dynamic_update_slice(dz, part, (0, at))
    dh = _mm(dz, w["w_all"], tb=True, name="mm_in_dx")
    d_w_all = _mm(sv["h"], dz, ta=True, name="mm_in_dw")
    hook("mix_pre", dh)
    dx, sum_mix_pre = _modnorm_bwd(dh, sv["x"], dx1, _row(w["g_mix_pre"]), (mod3, 1), "mix_pre_bwd")

    dmod = jnp.stack([sum_mix_pre[0], sum_mix_pre[1], sum_mix_post[0], sum_ff_pre[0], sum_ff_pre[1], sum_ff_post[0]])
    d_w_in = d_w_all[None]
    d_w_pool = jnp.stack([d_wbd[64 * g:64 * g + 64, 64 * g:64 * g + 64] for g in range(4)])
    big = dict(w_in=d_w_in, w_branch=d_w_branch, w_out=d_w_out, w_ff1=d_w_ff1, w_ff2=d_w_ff2)
    small = dict(g_mix_pre=sum_mix_pre[2], g_mix_post=sum_mix_post[1], g_ff_pre=sum_ff_pre[2], g_ff_post=sum_ff_post[1],
                 b_f=sum_bf[0, :HEADS], w_pool=d_w_pool, pool_scale=sum_ps[0], conv_w=sum_cw[0:3])
    return dx, dmod, big, small


_QKV_END, _FL_END, _PC_END = 3 * A_WIDTH, 3 * A_WIDTH + HEADS, 3 * A_WIDTH + HEADS + POOL_WIDTH + 3 * CONV_WIDTH
_W_IN_GROUPS = ((_PC_END, IN_COLS, Z_GL), (0, _QKV_END, Z_QKV), (_FL_END, _PC_END, Z_PC), (_QKV_END, _FL_END, Z_FL))
_SHARD_COLS = IN_COLS // N_CHIPS


def _w_in_layout():
    out = []
    for p in range(N_CHIPS):
        pieces = []
        for lo, hi, at in _W_IN_GROUPS:
            a, b = max(lo, p * _SHARD_COLS), min(hi, (p + 1) * _SHARD_COLS)
            if a < b:
                pieces.append((at + a - lo, at + b - lo, a - p * _SHARD_COLS))
        pieces.sort()
        segs = []
        for z0, z1, _ in pieces:
            s, e = z0 // 128 * 128, -(-z1 // 128) * 128
            if segs and s <= segs[-1][1]:
                segs[-1] = (segs[-1][0], max(e, segs[-1][1]))
            else:
                segs.append((s, e))
        assert sum(e - s for s, e in segs) == Z_WINDOW
        out.append((pieces, segs))
    return out


Z_WINDOW = 1536


def _w_in_window(shard, p):
    pieces, segs = _w_in_layout()[p]
    cols = []
    for s, e in segs:
        at = s
        for z0, z1, src in pieces:
            if s <= z0 < e:
                if z0 > at:
                    cols.append(jnp.zeros((shard.shape[0], z0 - at), shard.dtype))
                cols.append(shard[:, src:src + z1 - z0])
                at = z1
        if e > at:
            cols.append(jnp.zeros((shard.shape[0], e - at), shard.dtype))
    return jnp.concatenate(cols, axis=1)


def _own_window(shard, chip):
    return lax.switch(chip, [lambda t, p=p: _w_in_window(t, p) for p in range(N_CHIPS)], shard)


def _w_all_from_windows(blocks):
    layout = _w_in_layout()
    bounds = sorted({edge for _, segs in layout for seg in segs for edge in seg})
    parts = []
    for lo, hi in zip(bounds[:-1], bounds[1:]):
        covering = []
        for p, (_, segs) in enumerate(layout):
            at = 0
            for s, e in segs:
                if s <= lo and hi <= e:
                    covering.append(blocks[p][:, at + lo - s:at + hi - s])
                at += e - s
        assert covering
        parts.append(covering[0] if len(covering) == 1 else covering[0] + covering[1])
    return jnp.concatenate(parts, axis=1)


def _w_in_shard(d_w_all, p):
    pieces = []
    for lo, hi, at in sorted(_W_IN_GROUPS):
        a, b = max(lo, p * _SHARD_COLS), min(hi, (p + 1) * _SHARD_COLS)
        if a < b:
            pieces.append(d_w_all[:, at + a - lo:at + b - lo])
    return jnp.concatenate(pieces, axis=1)


def _w_in_shards(d_w_all):
    return jnp.stack([_w_in_shard(d_w_all, p) for p in range(N_CHIPS)])


def _full_layer_weights(w_in_blocks, w_branch, w_out, w_ff1, w_ff2, g_mix_pre, g_mix_post, g_ff_pre, g_ff_post, b_f, w_pool, pool_scale, conv_w):
    w_all = None if w_in_blocks is None else _w_all_from_windows(w_in_blocks)
    wbd = (w_pool[:, :, None, :] * jnp.eye(4, dtype=F32)[:, None, :, None]).reshape(POOL_WIDTH, POOL_WIDTH)
    return dict(w_all=w_all, w_branch=w_branch, w_out=w_out, w_ff1=w_ff1, w_ff2=w_ff2, g_mix_pre=g_mix_pre, g_mix_post=g_mix_post,
                g_ff_pre=g_ff_pre, g_ff_post=g_ff_post, b_f_pad=jnp.pad(b_f, (0, 128 - HEADS)).reshape(1, 128), w_pool_bd=wbd,
                pool_scale=pool_scale, conv_w=conv_w)


class _NoComm:
    def layer_weights(self, l):
        raise NotImplementedError

    def fwd_hook(self, l):
        return _no_hook

    def bwd_hook(self, l):
        return _no_hook

    def grads_ready(self, l, big):
        return None


class _Layers(_NoComm):
    def __init__(self, layers):
        self.layers = layers

    def layer_weights(self, l):
        return self.layers[l]


def _local_step(x, target, mods, comm):
    saved, weights = [], []
    act = x
    for l in range(DEPTH):
        weights.append(comm.layer_weights(l))
        act, sv = _layer_fwd(act, weights[l], mods[l], comm.fwd_hook(l))
        saved.append(sv)
    dact, sq = _loss_head(act, target)
    loss = sq[0, 0] * (0.5 / D)
    dmods, bigs, smalls = [None] * DEPTH, [None] * DEPTH, [None] * DEPTH
    token = None
    for l in reversed(range(DEPTH)):
        dact, dmods[l], bigs[l], smalls[l] = _layer_bwd(dact, saved[l], weights[l], _tie(mods[l], token), comm.bwd_hook(l))
        token = comm.grads_ready(l, bigs[l])
    return loss, dact, jnp.stack(dmods), bigs, smalls


_BIG = ("w_in", "w_branch", "w_out", "w_ff1", "w_ff2")


class _GatherJob:
    def __init__(self, tag, shards, after):
        self.tag, self.n = tag, len(shards)
        lands = [lax.empty((N_CHIPS,) + s.shape, s.dtype) for s in shards]
        self.state = _copies_start(list(shards) + lands, _plan_gather_ici, 4 * self.n, after, "gather_ici_start_" + tag)
        self.token = self.state[3]

    def pass_on(self, after):
        bufs = _copies_wait(self.state, _plan_gather_ici, after, "gather_ici_wait_" + self.tag)
        self.state = _copies_start(bufs[self.n:], _plan_gather_d2d, 3 * self.n, bufs[0], "gather_d2d_start_" + self.tag)
        self.token = self.state[3]
        return self.token

    def done(self, after):
        return _copies_wait(self.state, _plan_gather_d2d, after, "gather_d2d_wait_" + self.tag)


class _ReduceJob:
    def __init__(self, tag, names, grads, sel, after):
        self.tag, self.names, self.n, self.sel = tag, names, len(names), sel
        lands = [lax.empty((g.shape[0], g.shape[1] // 2, g.shape[2]), F32) for g in grads]
        self.state = _copies_start(list(grads) + lands, _plan_rs_sibling, self.n, after, "rs_sibling_start_" + tag)
        self.token = self.state[3]

    def _chip_sum(self, name, g, other):
        if g.shape[0] == N_CHIPS:
            return _chip_sum(g, other, self.sel, "rs_chip_sum_" + name)
        wire, total = _chip_sum(g, other, self.sel, "rs_chip_sum_" + name, blocked=False)
        own = lax.switch(self.sel[1], [lambda t, p=p: _w_in_shard(t, p) for p in range(N_CHIPS)], total[0])
        return _w_in_shards(wire[0]), own

    def chip_sums(self, after):
        bufs = _copies_wait(self.state, _plan_rs_sibling, after, "rs_sibling_wait_" + self.tag)
        wires, self.owns = zip(*[self._chip_sum(name, bufs[i], bufs[self.n + i]) for i, name in enumerate(self.names)])
        lands = [lax.empty((3,) + w.shape[1:], BF16) for w in wires]
        self.state = _copies_start(list(wires) + lands, _plan_rs_chips, 3 * self.n, self.owns[0], "rs_chips_start_" + self.tag)
        self.token = self.state[3]
        return self.token

    def final_sums(self, after):
        bufs = _copies_wait(self.state, _plan_rs_chips, after, "rs_chips_wait_" + self.tag)
        sums = [_final_sum(self.owns[i], bufs[self.n + i], self.sel, "rs_final_" + name) for i, name in enumerate(self.names)]
        self.state = _copies_start(sums, _plan_rs_share, self.n, sums[0], "rs_share_start_" + self.tag)
        self.token = self.state[3]
        return self.token

    def done(self, after):
        return dict(zip(self.names, _copies_wait(self.state, _plan_rs_share, after, "rs_share_wait_" + self.tag)))


def _chip_blocks(g):
    return g if g.ndim == 3 else g.reshape(N_CHIPS, -1, g.shape[1])


class _StepComm(_NoComm):
    def __init__(self, big_weights, w_in0, sel, after):
        self.sel = sel
        self.small, self.grads, self.jobs = None, [dict() for _ in range(DEPTH)], {}
        self.jobs["in0"] = _GatherJob("in0", [w_in0], after)
        later = lax.optimization_barrier((tuple(big_weights), self.jobs["in0"].token.after))[0]
        self.jobs["rest0"] = _GatherJob("rest0", [w[0].astype(BF16) for w in later[1:]], self.jobs["in0"].token)
        layer1 = [w[1].astype(BF16) for w in later]
        self.jobs["all1"] = _GatherJob("all1", [_own_window(layer1[0], sel[1])] + layer1[1:], self.jobs["rest0"].token)

    def layer_weights(self, l):
        if l == 0:
            self.weights0 = _full_layer_weights(None, None, None, None, None, *self.small[0])
            return self.weights0
        g_in, g_br, g_out, g_f1, g_f2 = self.landed1
        return _full_layer_weights(g_in, g_br.reshape(D, D), g_out.reshape(D, D), g_f1, g_f2.reshape(D_FF, D), *self.small[1])

    def fwd_hook(self, l):
        if l != 0:
            return _no_hook

        def hook(point, after, ready=None):
            if point == "pre":
                job = self.jobs["in0"]
                started = after[:8, :128].astype(F32) + self.jobs["all1"].token.after
                self.weights0["w_all"] = _w_all_from_windows(job.done(job.pass_on(started))[0])
            if point == "attn":
                return self.jobs["rest0"].pass_on(after)
            if point == "ff_post":
                return self.jobs["all1"].pass_on(after)
            if point == "pool":
                g_br, g_out, g_f1, g_f2 = self.jobs["rest0"].done(after)
                self.weights0.update(w_branch=g_br.reshape(D, D), w_out=g_out.reshape(D, D), w_ff1=g_f1, w_ff2=g_f2.reshape(D_FF, D))
            if point == "end":
                self.landed1 = self.jobs["all1"].done(after)
            return None
        return hook

    def bwd_hook(self, l):
        if l != 0:
            return _no_hook

        def hook(point, after, ready=None):
            jobs = self.jobs
            if point == "ff_pre":
                token = jobs["rs1"].chip_sums(after)
                jobs["rs0_ff"] = _ReduceJob("0_ff", ("w_ff1", "w_ff2"), [_chip_blocks(ready[n]) for n in ("w_ff1", "w_ff2")], self.sel, token)
                return jobs["rs0_ff"].token
            if point == "cumf":
                return jobs["rs0_ff"].chip_sums(jobs["rs1"].final_sums(after))
            self.grads[1] = jobs["rs1"].done(after)
            return None
        return hook

    def grads_ready(self, l, big):
        if l == 1:
            self.jobs["rs1"] = _ReduceJob("1", _BIG, [_chip_blocks(big[n]) for n in _BIG], self.sel, self.sel)
            return self.jobs["rs1"].token
        names = ("w_in", "w_branch", "w_out")
        self.jobs["rs0_mix"] = _ReduceJob("0_mix", names, [_chip_blocks(big[n]) for n in names], self.sel, self.sel)
        return self.jobs["rs0_mix"].token

    def finish_sums(self, after):
        jobs = self.jobs
        token = jobs["rs0_mix"].chip_sums(after)
        return jobs["rs0_ff"].final_sums(token)

    def finish_ff(self, after):
        self.grads[0].update(self.jobs["rs0_ff"].done(after))

    def finish_mix(self, after):
        job = self.jobs["rs0_mix"]
        self.grads[0].update(job.done(job.final_sums(after)))


_SMALL = ("g_mix_pre", "g_mix_post", "g_ff_pre", "g_ff_post", "b_f", "w_pool", "pool_scale", "conv_w")


def _w_in_view(t):
    return t.reshape(DEPTH, D // 128, 128, _SHARD_COLS).transpose(3, 1, 0, 2).reshape(_SHARD_COLS * (D // 128) * DEPTH, 128)


def _w_in_unview(t):
    return t.reshape(_SHARD_COLS, D // 128, DEPTH, 128).transpose(2, 1, 3, 0).reshape(DEPTH, D, _SHARD_COLS)


def _pack(parts, rows=8):
    flat = jnp.concatenate([p.reshape(-1) for p in parts])
    width = -(-flat.shape[0] // (rows * 128)) * 128
    return jnp.pad(flat, (0, rows * width - flat.shape[0])).reshape(rows, width)


def _unpack(packed, like):
    flat = packed.reshape(-1)
    out, at = [], 0
    for ref in like:
        out.append(flat[at:at + ref.size].reshape(ref.shape))
        at += ref.size
    return out


def kernel(x, c, w_ada, b_ada, g_mix_pre, g_mix_post, g_ff_pre, g_ff_post, w_in, b_f, w_pool, pool_scale, conv_w, w_branch, w_out, w_ff1, w_ff2, loss_target, m_w_ada, m_b_ada, m_g_mix_pre, m_g_mix_post, m_g_ff_pre, m_g_ff_post, m_w_in, m_b_f, m_w_pool, m_pool_scale, m_conv_w, m_w_branch, m_w_out, m_w_ff1, m_w_ff2, v_w_ada, v_b_ada, v_g_mix_pre, v_g_mix_post, v_g_ff_pre, v_g_ff_post, v_w_in, v_b_f, v_w_pool, v_pool_scale, v_conv_w, v_w_branch, v_w_out, v_w_ff1, v_w_ff2):
    xi, yi, ci = lax.axis_index("x"), lax.axis_index("y"), lax.axis_index("c")
    chip = 2 * xi + yi
    dev = 2 * chip + ci
    n_ada = w_ada.shape[2]

    first = jnp.zeros((8, D + 384), F32).at[0, :D].set(c[0]).at[0, D:].set(conv_w.reshape(-1))
    w_in0 = _own_window(w_in[0].astype(BF16), chip)
    got = _allgather8(first, "gather_cond", after=(w_in0,)).reshape(N_DEV, 8, D + 384)[:, 0]
    c_all = got[:, :D]
    conv_full = got[0::2, D:].reshape(N_CHIPS, DEPTH, 3, CONV_WIDTH // N_CHIPS).transpose(1, 2, 0, 3).reshape(DEPTH, 3, CONV_WIDTH)

    b_loc = lax.dynamic_slice_in_dim(b_ada, chip * n_ada, n_ada, axis=1).reshape(DEPTH, 1, n_ada)
    mod_cols, silu_c = _ada_fwd(c_all, w_ada, b_loc)
    got = _allgather8(mod_cols.reshape(DEPTH * N_DEV, n_ada), "gather_mod").reshape(N_DEV, DEPTH, N_DEV, n_ada)[0::2]
    mod_all = got.transpose(1, 2, 0, 3).reshape(DEPTH, N_DEV, 6, D)
    mods = lax.dynamic_index_in_dim(mod_all, dev, axis=1, keepdims=False)

    comm = _StepComm((w_in, w_branch, w_out, w_ff1, w_ff2), w_in0, jnp.stack([ci, chip]).astype(jnp.int32), mods)
    comm.small = [(g_mix_pre[l], g_mix_post[l], g_ff_pre[l], g_ff_post[l], b_f[l], w_pool[l], pool_scale[l], conv_full[l]) for l in range(DEPTH)]
    loss_part, grad_x, dmods, bigs, smalls = _local_step(x[0], loss_target[0], mods, comm)

    small_parts = [smalls[l][name] for name in _SMALL for l in range(DEPTH)] + [loss_part.reshape(1)]
    packed = _tie(_pack([dmods] + small_parts), comm.jobs["rs0_mix"].token)
    gathered = _allgather8(packed, "gather_small")
    dmod_all = gathered.reshape(N_DEV, -1)[:, :dmods.size].reshape(N_DEV, DEPTH, 6 * D)
    summed = _unpack(_sum_devices(gathered), [dmods] + small_parts)
    grad_b_ada = summed[0].reshape(DEPTH, 6 * D)
    loss = summed[-1][0]
    small_grads = {name: jnp.stack(summed[1 + 2 * i:3 + 2 * i]) for i, name in enumerate(_SMALL)}
    small_grads["conv_w"] = lax.dynamic_slice_in_dim(small_grads["conv_w"], chip * (CONV_WIDTH // N_CHIPS), CONV_WIDTH // N_CHIPS, axis=2)

    dmod_loc = lax.dynamic_slice_in_dim(dmod_all.transpose(1, 0, 2), chip * n_ada, n_ada, axis=2)
    tail_token = comm.finish_sums(grad_b_ada)
    silu_pad = _tie(jnp.pad(silu_c, ((0, 128 - N_DEV), (0, 0))), tail_token)
    dmod_pad = jnp.pad(dmod_loc.transpose(1, 0, 2).reshape(N_DEV, DEPTH * n_ada), ((0, 128 - N_DEV), (0, 0)))
    grad_w_ada = _mm(silu_pad, dmod_pad, ta=True, out_split=DEPTH, name="mm_ada_dw")

    grads = dict(w_ada=grad_w_ada, b_ada=grad_b_ada, **small_grads)
    weights = dict(w_ada=w_ada, b_ada=b_ada, g_mix_pre=g_mix_pre, g_mix_post=g_mix_post, g_ff_pre=g_ff_pre, g_ff_post=g_ff_post, w_in=w_in,
                   b_f=b_f, w_pool=w_pool, pool_scale=pool_scale, conv_w=conv_w, w_branch=w_branch, w_out=w_out, w_ff1=w_ff1, w_ff2=w_ff2)
    m_in = dict(w_ada=m_w_ada, b_ada=m_b_ada, g_mix_pre=m_g_mix_pre, g_mix_post=m_g_mix_post, g_ff_pre=m_g_ff_pre, g_ff_post=m_g_ff_post,
                w_in=m_w_in, b_f=m_b_f, w_pool=m_w_pool, pool_scale=m_pool_scale, conv_w=m_conv_w, w_branch=m_w_branch, w_out=m_w_out,
                w_ff1=m_w_ff1, w_ff2=m_w_ff2)
    v_in = dict(w_ada=v_w_ada, b_ada=v_b_ada, g_mix_pre=v_g_mix_pre, g_mix_post=v_g_mix_post, g_ff_pre=v_g_ff_pre, g_ff_post=v_g_ff_post,
                w_in=v_w_in, b_f=v_b_f, w_pool=v_w_pool, pool_scale=v_pool_scale, conv_w=v_conv_w, w_branch=v_w_branch, w_out=v_w_out,
                w_ff1=v_w_ff1, w_ff2=v_w_ff2)
    order = ("w_ada", "b_ada", "g_mix_pre", "g_mix_post", "g_ff_pre", "g_ff_post", "w_in", "b_f", "w_pool", "pool_scale", "conv_w",
             "w_branch", "w_out", "w_ff1", "w_ff2")
    delta, new_m, new_v = {}, {}, {}
    tiny = ("b_ada",) + _SMALL
    tiny_g = [_tie(grads[tiny[0]], tail_token)] + [grads[name] for name in tiny[1:]]
    res = _adamw_many([weights[name] for name in tiny], tiny_g, [m_in[name] for name in tiny], [v_in[name] for name in tiny], "adamw_small")
    for out, vals in zip((delta, new_m, new_v), res):
        out.update(zip(tiny, vals))
    delta["w_ada"], new_m["w_ada"], new_v["w_ada"] = _adamw(w_ada, grad_w_ada, m_w_ada, v_w_ada, "adamw_w_ada")
    comm.finish_ff(delta["w_ada"][0, :8, :128] + delta["b_ada"][0, :128])
    for name in ("w_ff1", "w_ff2", "w_in", "w_branch", "w_out"):
        if name == "w_in":
            comm.finish_mix(delta["w_ff2"][0, :8, :128])
        g_layers = [comm.grads[l][name] for l in range(DEPTH)]
        if name == "w_in":
            g_view = lax.optimization_barrier(_w_in_view(jnp.stack(g_layers)))
            res = _adamw(_w_in_view(w_in), g_view, _w_in_view(m_w_in), _w_in_view(v_w_in), "adamw_w_in")
            grads[name], delta[name], new_m[name], new_v[name] = [_w_in_unview(t) for t in (g_view, *res)]
        else:
            delta[name], new_m[name], new_v[name], grads[name] = _adamw_layers(weights[name], g_layers, m_in[name], v_in[name], "adamw_" + name)

    return (loss, grad_x[None], *[grads[n] for n in order], *[delta[n] for n in order], *[new_m[n] for n in order],
            *[new_v[n] for n in order])
```

```python
from typing import NamedTuple

import jax
import jax.numpy as jnp
from jax import lax
from jax.experimental import pallas as pl
from jax.experimental.pallas import tpu as pltpu

F32 = jnp.float32
BF16 = jnp.bfloat16
MESH = pl.DeviceIdType.MESH

D = 1024
DEPTH = 2
HEADS = 8
HEAD_DIM = 64
A_WIDTH = 512
POOL_WIDTH = 256
CONV_WIDTH = 256
D_FF = 4096
IN_COLS = 5640
Z_GL, Z_QKV, Z_PC, Z_FL, Z_COLS = 0, 3072, 4608, 5632, 5760
RMS_EPS = 1e-6
NEG_INF = -1e30
ROW_TILE = 512
EW_ROWS = 256
N_CHIPS = 4
N_DEV = 8
V7X_VMEM_LIMIT = 48 * 1024 * 1024

ADAM_LR = 0.001
ADAM_B1 = 0.9
ADAM_B2 = 0.999
ADAM_EPS = 1e-08
ADAM_WD = 0.01
ADAM_STEP = 10

_HBM = pl.BlockSpec(memory_space=pltpu.HBM)


def _params(*sem):
    return pltpu.CompilerParams(dimension_semantics=sem, vmem_limit_bytes=V7X_VMEM_LIMIT)


def _pick(dim, cands):
    for cand in cands:
        if dim % cand == 0:
            return cand
    return dim


MM_TILE_BUDGET = 39 * 1024 * 1024


def _mm_tiles(m, n, k, k_unit, tn, a_size, b_size, out_size):
    for tk in (k_unit, 2048, 1152, 1024, 640, 512, 256, 128):
        if k_unit % tk:
            continue
        for tm in (2048, 1024, 512, 256, 128):
            if m % tm or ((m // tm) * (n // tn) < 2 and tm > 512):
                continue
            need = 2 * (tm * tk * a_size + tk * tn * b_size + tm * tn * out_size) + (0 if tk == k else 4 * tm * tn)
            if need <= MM_TILE_BUDGET and (tk == k_unit or tm >= 512):
                return tm, tk
    return 128, 128


def _mm(a, b, *, ta=False, tb=False, b_rows=None, b_split=1, out_split=1, out_dtype=F32, epilogue=None, extras=(), name):
    (k, m) = a.shape if ta else a.shape[::-1]
    b_row0, b_rows = (0, b.shape[-2]) if b_rows is None else b_rows
    b_cols = b.shape[-1] * b_split
    (n, k2) = (b_rows, b_cols) if tb else (b_cols, b_rows)
    assert k == k2, (a.shape, b.shape, ta, tb)
    n_unit = n // (out_split * (1 if tb else b_split))
    k_unit = k // (b_split if tb else 1)
    tn = _pick(n_unit, (1024, 1152, 768, 640, 512, 256, 128))
    tm, tk = _mm_tiles(m, n, k, k_unit, tn, a.dtype.itemsize, b.dtype.itemsize,
                       sum(jnp.dtype(dt).itemsize for dt in out_dtype) + 4 * len(extras) if epilogue else jnp.dtype(out_dtype).itemsize)
    nk = k // tk
    dims = (((0 if ta else 1,), (1 if tb else 0,)), ((), ()))

    def dot(a_ref, b_ref):
        b_val = b_ref[0] if b_split > 1 else b_ref[...]
        return lax.dot_general(a_ref[...].astype(BF16), b_val.astype(BF16), dims, preferred_element_type=F32)

    n_extra = len(extras)
    assert epilogue is None or out_split == 1

    def put(refs, val):
        if epilogue is not None:
            for o_ref, res in zip(refs[n_extra:], epilogue(val, *[r[...] for r in refs[:n_extra]])):
                o_ref[...] = res.astype(o_ref.dtype)
        elif out_split > 1:
            refs[0][0] = val.astype(refs[0].dtype)
        else:
            refs[0][...] = val.astype(refs[0].dtype)

    def body_single(a_ref, b_ref, *refs):
        put(refs, dot(a_ref, b_ref))

    def body_acc(a_ref, b_ref, *refs):
        kk = pl.program_id(2)
        acc_ref = refs[-1]

        @pl.when(kk == 0)
        def _():
            acc_ref[...] = jnp.zeros_like(acc_ref)

        acc_ref[...] += dot(a_ref, b_ref)

        @pl.when(kk == nk - 1)
        def _():
            put(refs[:-1], acc_ref[...])

    a_spec = pl.BlockSpec((tk, tm), lambda i, j, kk: (kk, i)) if ta else pl.BlockSpec((tm, tk), lambda i, j, kk: (i, kk))
    if b_split == 1:
        off = b_row0 // (tn if tb else tk)
        assert off * (tn if tb else tk) == b_row0
        b_spec = pl.BlockSpec((tn, tk), lambda i, j, kk: (j + off, kk)) if tb else pl.BlockSpec((tk, tn), lambda i, j, kk: (kk + off, j))
    elif tb:
        per = k_unit // tk
        b_spec = pl.BlockSpec((1, tn, tk), lambda i, j, kk: (kk // per, j, kk % per))
    else:
        per = n // b_split // tn
        b_spec = pl.BlockSpec((1, tk, tn), lambda i, j, kk: (j // per, kk, j % per))
    if out_split == 1:
        o_spec = pl.BlockSpec((tm, tn), lambda i, j, kk: (i, j))
        o_shape = None if epilogue is not None else jax.ShapeDtypeStruct((m, n), out_dtype)
    else:
        per_o = n // out_split // tn
        o_spec = pl.BlockSpec((1, tm, tn), lambda i, j, kk: (j // per_o, i, j % per_o))
        o_shape = jax.ShapeDtypeStruct((out_split, m, n // out_split), out_dtype)
    if epilogue is not None:
        o_shape = [jax.ShapeDtypeStruct((m, n), dt) for dt in out_dtype]
        o_spec = [o_spec] * len(out_dtype)
    return pl.pallas_call(
        body_single if nk == 1 else body_acc, name=name, grid=(m // tm, n // tn, nk),
        in_specs=[a_spec, b_spec] + [pl.BlockSpec((tm, tn), lambda i, j, kk: (i, j))] * n_extra, out_specs=o_spec, out_shape=o_shape,
        scratch_shapes=[] if nk == 1 else [pltpu.VMEM((tm, tn), F32)],
        compiler_params=_params("parallel", "parallel", "arbitrary"),
    )(a, b, *extras)


def _stacked_proj(lhs, w, widths, tb, name):
    s = lhs[0].shape[0]
    n = w.shape[1]
    starts = [sum(widths[:i]) for i in range(len(widths))]
    n_lhs = len(lhs)
    dims = (((1,), (1 if tb else 0,)), ((), ()))

    def body(*refs):
        w_ref = refs[n_lhs]
        for a_ref, o_ref, start, width in zip(refs[:n_lhs], refs[n_lhs + 1:], starts, widths):
            o_ref[...] = lax.dot_general(a_ref[...].astype(BF16), w_ref[start:start + width, :].astype(BF16), dims,
                                         preferred_element_type=F32)

    in_cols = [n if tb else width for width in widths]
    out_cols = [width if tb else n for width in widths]
    assert [a.shape for a in lhs] == [(s, cols) for cols in in_cols], ([a.shape for a in lhs], widths, tb)
    return pl.pallas_call(
        body, name=name, grid=(s // ROW_TILE,),
        in_specs=[_row_spec(cols) for cols in in_cols] + [pl.BlockSpec(w.shape, lambda i: (0, 0))],
        out_specs=[_row_spec(cols) for cols in out_cols],
        out_shape=[jax.ShapeDtypeStruct((s, cols), F32) for cols in out_cols],
        compiler_params=_params("parallel"),
    )(*lhs, w)


def _stacked_dw(lhs, rhs, name):
    n = rhs[0].shape[1]
    tr = min(a.shape[1] for a in lhs)
    tn = _pick(n, (1024, 512, 256, 128))
    counts = [a.shape[1] // tr for a in lhs]
    starts = [sum(counts[:i]) for i in range(len(lhs))]
    assert all(a.shape[1] == c * tr for a, c in zip(lhs, counts))
    n_lhs = len(lhs)

    def body(*refs):
        o_ref = refs[-1]
        j = pl.program_id(1)
        for a_ref, b_ref, start, count in zip(refs[:n_lhs], refs[n_lhs:-1], starts, counts):
            @pl.when((j >= start) & (j < start + count))
            def _():
                o_ref[...] = lax.dot_general(a_ref[...].astype(BF16), b_ref[...].astype(BF16), (((0,), (0,)), ((), ())),
                                             preferred_element_type=F32)

    def lhs_spec(a, start, count):
        return pl.BlockSpec((a.shape[0], tr), lambda c, j: (0, jnp.clip(j - start, 0, count - 1)))

    return pl.pallas_call(
        body, name=name, grid=(n // tn, sum(counts)),
        in_specs=[lhs_spec(a, st, ct) for a, st, ct in zip(lhs, starts, counts)]
        + [pl.BlockSpec((b.shape[0], tn), lambda c, j: (0, c)) for b in rhs],
        out_specs=pl.BlockSpec((tr, tn), lambda c, j: (j, c)),
        out_shape=jax.ShapeDtypeStruct((sum(counts) * tr, n), F32),
        compiler_params=_params("parallel", "parallel"),
    )(*lhs, *rhs)


def _ew(fn, ins, out_dtypes, name, tc=None):
    shape = ins[0].shape
    lead, (rows, cols) = shape[:-2], shape[-2:]
    tc = cols if tc is None else tc
    if tc > 1024:
        tr = _pick(rows, (EW_ROWS, 128, 8))
    elif tc > 128:
        tr = _pick(rows, (2 * EW_ROWS, EW_ROWS, 128, 8))
    else:
        tr = _pick(rows, (4096, 2256, 2048, 1024, EW_ROWS, 8))
    n_in = len(ins)

    def body(*refs):
        res = fn(*[r[...] for r in refs[:n_in]])
        for o_ref, val in zip(refs[n_in:], res):
            o_ref[...] = val.astype(o_ref.dtype)

    if lead:
        spec = pl.BlockSpec((None, tr, tc), lambda l, i, j: (l, i, j))
    else:
        spec = pl.BlockSpec((tr, tc), lambda i, j: (i, j))
    return pl.pallas_call(
        body, name=name, grid=lead + (rows // tr, cols // tc),
        in_specs=[spec] * n_in, out_specs=[spec] * len(out_dtypes),
        out_shape=[jax.ShapeDtypeStruct(shape, dt) for dt in out_dtypes],
        compiler_params=_params(*(["parallel"] * (len(lead) + 2))),
    )(*ins)


def _relu2_fwd(a):
    r = jnp.maximum(a, 0.0)
    return a, r * r


def _relu2_bwd(dr, a):
    return (dr * (2.0 * jnp.maximum(a, 0.0)),)


def _adamw_math(w, g, m, v):
    m = ADAM_B1 * m + (1.0 - ADAM_B1) * g
    v = ADAM_B2 * v + (1.0 - ADAM_B2) * (g * g)
    m_hat = m / (1.0 - ADAM_B1 ** ADAM_STEP)
    v_hat = v / (1.0 - ADAM_B2 ** ADAM_STEP)
    delta = -ADAM_LR * (m_hat / (jnp.sqrt(v_hat) + ADAM_EPS) + ADAM_WD * w)
    return delta, m, v


def _adamw(w, g, m, v, name):
    return _ew(_adamw_math, [w, g, m, v], [F32, F32, F32], name)


def _adamw_layers(w, g_layers, m, v, name):
    depth, rows, cols = w.shape
    tr = _pick(rows, (2 * EW_ROWS, EW_ROWS, 128, 8)) if cols <= 1024 else _pick(rows, (EW_ROWS, 128, 8))

    def body(w_ref, *refs):
        g_refs, (m_ref, v_ref, d_ref, mo_ref, vo_ref, go_ref) = refs[:depth], refs[depth:]
        layer = pl.program_id(0)
        g = g_refs[0][...]
        for l in range(1, depth):
            g = jnp.where(layer == l, g_refs[l][...], g)
        d_ref[...], mo_ref[...], vo_ref[...] = _adamw_math(w_ref[...], g, m_ref[...], v_ref[...])
        go_ref[...] = g

    spec = pl.BlockSpec((None, tr, cols), lambda l, i: (l, i, 0))
    g_specs = [pl.BlockSpec((tr, cols), lambda l, i, k=k: (jnp.where(l == k, i, 0), 0)) for k in range(depth)]
    return pl.pallas_call(
        body, name=name, grid=(depth, rows // tr),
        in_specs=[spec] + g_specs + [spec, spec], out_specs=[spec] * 4,
        out_shape=[jax.ShapeDtypeStruct(w.shape, F32)] * 4, compiler_params=_params("arbitrary", "arbitrary"),
    )(w, *g_layers, m, v)


def _adamw_many(ws, gs, ms, vs, name):
    n = len(ws)

    def body(*refs):
        for i in range(n):
            res = _adamw_math(*[refs[k * n + i][...] for k in range(4)])
            for k in range(3):
                refs[(4 + k) * n + i][...] = res[k]

    outs = pl.pallas_call(
        body, name=name, out_shape=[jax.ShapeDtypeStruct(w.shape, F32) for w in ws] * 3,
        compiler_params=pltpu.CompilerParams(vmem_limit_bytes=V7X_VMEM_LIMIT),
    )(*ws, *gs, *ms, *vs)
    return outs[:n], outs[n:2 * n], outs[2 * n:]


def _row_spec(cols, block=0):
    return pl.BlockSpec((ROW_TILE, cols), lambda i, block=block: (i, block))


def _vec_spec(cols):
    return pl.BlockSpec((1, cols), lambda i: (0, 0))


def _vec_args(*vecs):
    arrays = [v[0] if isinstance(v, tuple) else v for v in vecs]
    specs = [pl.BlockSpec((None, 1, D), lambda i, row=v[1]: (row, 0, 0)) if isinstance(v, tuple) else _vec_spec(D) for v in vecs]
    return arrays, specs


def _sum_spec(cols):
    return pl.BlockSpec((8, cols), lambda i: (0, 0))


def _rstd(x):
    return lax.rsqrt(jnp.mean(x * x, axis=-1, keepdims=True) + RMS_EPS)


def _modnorm_fwd(x, g, shift, scale, name):
    s = x.shape[0]

    def body(x_ref, g_ref, sh_ref, sc_ref, h_ref):
        xv = x_ref[...]
        n = xv * _rstd(xv)
        h_ref[...] = ((n * g_ref[...]) * (1.0 + sc_ref[...]) + sh_ref[...]).astype(BF16)

    vecs, vec_specs = _vec_args(g, shift, scale)
    return pl.pallas_call(
        body, name=name, grid=(s // ROW_TILE,),
        in_specs=[_row_spec(D)] + vec_specs, out_specs=_row_spec(D),
        out_shape=jax.ShapeDtypeStruct((s, D), BF16), compiler_params=_params("parallel"),
    )(x, *vecs)


def _post_fwd(x, y, g, gate, name):
    s = x.shape[0]

    def body(x_ref, y_ref, g_ref, gate_ref, o_ref):
        yv = y_ref[...]
        o_ref[...] = x_ref[...] + gate_ref[...] * ((yv * _rstd(yv)) * g_ref[...])

    vecs, vec_specs = _vec_args(g, gate)
    return pl.pallas_call(
        body, name=name, grid=(s // ROW_TILE,),
        in_specs=[_row_spec(D), _row_spec(D)] + vec_specs, out_specs=_row_spec(D),
        out_shape=jax.ShapeDtypeStruct((s, D), F32), compiler_params=_params("parallel"),
    )(x, y, *vecs)


def _post_bwd(dxo, y, g, gate, name):
    s = dxo.shape[0]

    def body(d_ref, y_ref, g_ref, gate_ref, dy_ref, sum_ref):
        @pl.when(pl.program_id(0) == 0)
        def _():
            sum_ref[...] = jnp.zeros_like(sum_ref)

        dv, yv = d_ref[...], y_ref[...]
        r = _rstd(yv)
        n = yv * r
        sum_ref[0:1, :] += jnp.sum(dv * (n * g_ref[...]), axis=0, keepdims=True)
        sum_ref[1:2, :] += jnp.sum((dv * gate_ref[...]) * n, axis=0, keepdims=True)
        dn = (dv * gate_ref[...]) * g_ref[...]
        dy_ref[...] = (r * (dn - n * jnp.mean(dn * n, axis=-1, keepdims=True))).astype(BF16)

    vecs, vec_specs = _vec_args(g, gate)
    return pl.pallas_call(
        body, name=name, grid=(s // ROW_TILE,),
        in_specs=[_row_spec(D), _row_spec(D)] + vec_specs,
        out_specs=[_row_spec(D), _sum_spec(D)],
        out_shape=[jax.ShapeDtypeStruct((s, D), BF16), jax.ShapeDtypeStruct((8, D), F32)],
        compiler_params=_params("arbitrary"),
    )(dxo, y, *vecs)


def _modnorm_bwd(dh, x, dxo, g, scale, name):
    s = dh.shape[0]

    def body(dh_ref, x_ref, d_ref, g_ref, sc_ref, dx_ref, sum_ref):
        @pl.when(pl.program_id(0) == 0)
        def _():
            sum_ref[...] = jnp.zeros_like(sum_ref)

        dhv, xv = dh_ref[...], x_ref[...]
        r = _rstd(xv)
        n = xv * r
        one_sc = 1.0 + sc_ref[...]
        sum_ref[0:1, :] += jnp.sum(dhv, axis=0, keepdims=True)
        sum_ref[1:2, :] += jnp.sum(dhv * (n * g_ref[...]), axis=0, keepdims=True)
        sum_ref[2:3, :] += jnp.sum((dhv * one_sc) * n, axis=0, keepdims=True)
        dn = (dhv * one_sc) * g_ref[...]
        dx_ref[...] = d_ref[...] + r * (dn - n * jnp.mean(dn * n, axis=-1, keepdims=True))

    vecs, vec_specs = _vec_args(g, scale)
    return pl.pallas_call(
        body, name=name, grid=(s // ROW_TILE,),
        in_specs=[_row_spec(D), _row_spec(D), _row_spec(D)] + vec_specs,
        out_specs=[_row_spec(D), _sum_spec(D)],
        out_shape=[jax.ShapeDtypeStruct((s, D), F32), jax.ShapeDtypeStruct((8, D), F32)],
        compiler_params=_params("arbitrary"),
    )(dh, x, dxo, *vecs)


def _post_pre_fwd(x, y, g_post, gate, g_pre, shift, scale, name):
    s = x.shape[0]

    def body(x_ref, y_ref, gp_ref, gate_ref, g_ref, sh_ref, sc_ref, o_ref, h_ref):
        yv = y_ref[...]
        xo = x_ref[...] + gate_ref[...] * ((yv * _rstd(yv)) * gp_ref[...])
        o_ref[...] = xo
        h_ref[...] = (((xo * _rstd(xo)) * g_ref[...]) * (1.0 + sc_ref[...]) + sh_ref[...]).astype(BF16)

    vecs, vec_specs = _vec_args(g_post, gate, g_pre, shift, scale)
    return pl.pallas_call(
        body, name=name, grid=(s // ROW_TILE,),
        in_specs=[_row_spec(D), _row_spec(D)] + vec_specs, out_specs=[_row_spec(D), _row_spec(D)],
        out_shape=[jax.ShapeDtypeStruct((s, D), F32), jax.ShapeDtypeStruct((s, D), BF16)], compiler_params=_params("parallel"),
    )(x, y, *vecs)


def _pre_post_bwd(dh, x, dxo, g_pre, scale, y, g_post, gate, name):
    s = dh.shape[0]

    def body(dh_ref, x_ref, d_ref, y_ref, g_ref, sc_ref, gp_ref, gate_ref, dx_ref, dy_ref, sum_ref):
        @pl.when(pl.program_id(0) == 0)
        def _():
            sum_ref[...] = jnp.zeros_like(sum_ref)

        dhv, xv = dh_ref[...], x_ref[...]
        r = _rstd(xv)
        n = xv * r
        one_sc = 1.0 + sc_ref[...]
        sum_ref[0:1, :] += jnp.sum(dhv, axis=0, keepdims=True)
        sum_ref[1:2, :] += jnp.sum(dhv * (n * g_ref[...]), axis=0, keepdims=True)
        sum_ref[2:3, :] += jnp.sum((dhv * one_sc) * n, axis=0, keepdims=True)
        dn = (dhv * one_sc) * g_ref[...]
        dv = d_ref[...] + r * (dn - n * jnp.mean(dn * n, axis=-1, keepdims=True))
        dx_ref[...] = dv

        yv = y_ref[...]
        ry = _rstd(yv)
        ny = yv * ry
        sum_ref[3:4, :] += jnp.sum(dv * (ny * gp_ref[...]), axis=0, keepdims=True)
        sum_ref[4:5, :] += jnp.sum((dv * gate_ref[...]) * ny, axis=0, keepdims=True)
        dny = (dv * gate_ref[...]) * gp_ref[...]
        dy_ref[...] = (ry * (dny - ny * jnp.mean(dny * ny, axis=-1, keepdims=True))).astype(BF16)

    vecs, vec_specs = _vec_args(g_pre, scale, g_post, gate)
    return pl.pallas_call(
        body, name=name, grid=(s // ROW_TILE,),
        in_specs=[_row_spec(D)] * 4 + vec_specs,
        out_specs=[_row_spec(D), _row_spec(D), _sum_spec(D)],
        out_shape=[jax.ShapeDtypeStruct((s, D), F32), jax.ShapeDtypeStruct((s, D), BF16), jax.ShapeDtypeStruct((8, D), F32)],
        compiler_params=_params("arbitrary"),
    )(dh, x, dxo, y, *vecs)


def _loss_head(y, target):
    s = y.shape[0]

    def body(y_ref, t_ref, dy_ref, sum_ref):
        @pl.when(pl.program_id(0) == 0)
        def _():
            sum_ref[...] = jnp.zeros_like(sum_ref)

        err = y_ref[...] - t_ref[...]
        dy_ref[...] = err * (1.0 / D)
        sum_ref[...] += jnp.sum(err * err)

    return pl.pallas_call(
        body, name="loss_head", grid=(s // ROW_TILE,),
        in_specs=[_row_spec(D), _row_spec(D)],
        out_specs=[_row_spec(D), pl.BlockSpec((8, 128), lambda i: (0, 0))],
        out_shape=[jax.ShapeDtypeStruct((s, D), F32), jax.ShapeDtypeStruct((8, 128), F32)],
        compiler_params=_params("arbitrary"),
    )(y, target)


def _merge_fwd(z, pa, pb, pc):
    s = z.shape[0]

    def body(g0_ref, g1_ref, g2_ref, pa_ref, pb_ref, pc_ref, o_ref):
        o_ref[...] = (jax.nn.sigmoid(g0_ref[...]) * pa_ref[...] + jax.nn.sigmoid(g1_ref[...]) * pb_ref[...]
                      + jax.nn.sigmoid(g2_ref[...]) * pc_ref[...]).astype(BF16)

    return pl.pallas_call(
        body, name="merge_fwd", grid=(s // ROW_TILE,),
        in_specs=[_row_spec(D, 0), _row_spec(D, 1), _row_spec(D, 2), _row_spec(D), _row_spec(D), _row_spec(D)],
        out_specs=_row_spec(D), out_shape=jax.ShapeDtypeStruct((s, D), BF16),
        compiler_params=_params("parallel"),
    )(z, z, z, pa, pb, pc)


def _merge_bwd(dm, z, pa, pb, pc):
    s = z.shape[0]

    def body(dm_ref, g0_ref, g1_ref, g2_ref, pa_ref, pb_ref, pc_ref, dgl_ref, da_ref, db_ref, dc_ref):
        dmv = dm_ref[...]
        for i, (g_ref, p_ref, d_ref) in enumerate(((g0_ref, pa_ref, da_ref), (g1_ref, pb_ref, db_ref), (g2_ref, pc_ref, dc_ref))):
            gate = jax.nn.sigmoid(g_ref[...])
            dgl_ref[:, i * D:(i + 1) * D] = ((dmv * p_ref[...]) * (gate * (1.0 - gate))).astype(BF16)
            d_ref[...] = (dmv * gate).astype(BF16)

    return pl.pallas_call(
        body, name="merge_bwd", grid=(s // ROW_TILE,),
        in_specs=[_row_spec(D), _row_spec(D, 0), _row_spec(D, 1), _row_spec(D, 2), _row_spec(D), _row_spec(D), _row_spec(D)],
        out_specs=[_row_spec(3 * D), _row_spec(D), _row_spec(D), _row_spec(D)],
        out_shape=[jax.ShapeDtypeStruct((s, Z_COLS), BF16)] + [jax.ShapeDtypeStruct((s, D), BF16)] * 3,
        compiler_params=_params("parallel"),
    )(dm, z, z, z, pa, pb, pc)


def _shift_down(v, n):
    row = lax.broadcasted_iota(jnp.int32, v.shape, 0)
    return jnp.where(row >= n, pltpu.roll(v, n, axis=0), 0.0)


def _shift_up(v, n):
    s = v.shape[0]
    row = lax.broadcasted_iota(jnp.int32, v.shape, 0)
    return jnp.where(row < s - n, pltpu.roll(v, s - n, axis=0), 0.0)


def _log_sigmoid(v):
    return jnp.minimum(v, 0.0) - jnp.log1p(jnp.exp(-jnp.abs(v)))


def _cumf_fwd(fl, bias):
    s = fl.shape[0]

    def body(fl_ref, b_ref, o_ref):
        acc = _log_sigmoid(fl_ref[...] + b_ref[...])
        step = 1
        while step < s:
            acc = acc + _shift_down(acc, step)
            step *= 2
        o_ref[...] = acc

    return pl.pallas_call(body, name="cumf_fwd", out_shape=jax.ShapeDtypeStruct((s, 128), F32),
                          compiler_params=pltpu.CompilerParams(vmem_limit_bytes=V7X_VMEM_LIMIT))(fl, bias)


def _cumf_bwd(dcum, fl, bias):
    s = fl.shape[0]

    def body(d_ref, fl_ref, b_ref, dfl_ref, db_ref):
        acc = d_ref[...]
        step = 1
        while step < s:
            acc = acc + _shift_up(acc, step)
            step *= 2
        dfl = acc * jax.nn.sigmoid(-(fl_ref[...] + b_ref[...]))
        dfl_ref[...] = dfl.astype(BF16)
        db_ref[...] = jnp.broadcast_to(jnp.sum(dfl, axis=0, keepdims=True), (8, 128))

    return pl.pallas_call(
        body, name="cumf_bwd",
        out_shape=[jax.ShapeDtypeStruct((s, 128), BF16), jax.ShapeDtypeStruct((8, 128), F32)],
        compiler_params=pltpu.CompilerParams(vmem_limit_bytes=V7X_VMEM_LIMIT))(dcum, fl, bias)


def _pool_windows(v, shift):
    s2 = v + shift(v, 1)
    s4 = s2 + shift(s2, 2)
    s8 = s4 + shift(s4, 4)
    s16 = s8 + shift(s8, 8)
    group = lax.broadcasted_iota(jnp.int32, v.shape, 1) // 64
    return jnp.where(group == 0, s2, jnp.where(group == 1, s4, jnp.where(group == 2, s8, s16)))


def _pool_count(shape):
    group = lax.broadcasted_iota(jnp.int32, shape, 1) // 64
    window = jnp.where(group == 0, 2.0, jnp.where(group == 1, 4.0, jnp.where(group == 2, 8.0, 16.0)))
    t1 = (lax.broadcasted_iota(jnp.int32, shape, 0) + 1).astype(F32)
    return jnp.minimum(t1, window)


def _pc_specs(s):
    zcol = lambda blk: pl.BlockSpec((s, 256), lambda i, blk=blk: (0, blk))
    first = Z_PC // 256
    return [zcol(first), zcol(first + 1), zcol(first + 2), zcol(first + 3),
            pl.BlockSpec((256, 256), lambda i: (0, 0)), pl.BlockSpec((1, 256), lambda i: (0, 0)),
            pl.BlockSpec((3, 256), lambda i: (0, 0))]


def _poolconv_fwd(z, wbd, pscale, convw):
    s = z.shape[0]

    def body(pu_ref, ch_ref, cb_ref, cc_ref, w_ref, ps_ref, cw_ref, yb_ref, yc_ref):
        u = pu_ref[...]
        p = _pool_windows(u, _shift_down) / _pool_count(u.shape) - u
        yb = jnp.dot(p.astype(BF16), w_ref[...].astype(BF16), preferred_element_type=F32) * ps_ref[...]
        yb_ref[...] = yb.astype(BF16)
        uc = cc_ref[...] * ch_ref[...]
        cw = cw_ref[...]
        conv = cw[0:1, :] * _shift_down(uc, 2) + cw[1:2, :] * _shift_down(uc, 1) + cw[2:3, :] * uc
        yc_ref[...] = (cb_ref[...] * conv).astype(BF16)

    out = pl.BlockSpec((s, 256), lambda i: (0, 0))
    return pl.pallas_call(
        body, name="poolconv_fwd", grid=(1,), in_specs=_pc_specs(s), out_specs=[out, out],
        out_shape=[jax.ShapeDtypeStruct((s, 256), BF16)] * 2, compiler_params=_params("arbitrary"),
    )(z, z, z, z, wbd, pscale, convw)


def _poolconv_bwd(dyb, dyc, z, wbd, pscale, convw):
    s = z.shape[0]

    def body(dyb_ref, dyc_ref, pu_ref, ch_ref, cb_ref, cc_ref, w_ref, ps_ref, cw_ref, dz_ref, dw_ref, dps_ref, dcw_ref):
        u = pu_ref[...]
        count = _pool_count(u.shape)
        p = (_pool_windows(u, _shift_down) / count - u).astype(BF16)
        wb = w_ref[...].astype(BF16)
        dyb_v = dyb_ref[...]
        pw = jnp.dot(p, wb, preferred_element_type=F32)
        dps_ref[...] = jnp.broadcast_to(jnp.sum(dyb_v * pw, axis=0, keepdims=True), (8, 256))
        dys = (dyb_v * ps_ref[...]).astype(BF16)
        dp = lax.dot_general(dys, wb, (((1,), (1,)), ((), ())), preferred_element_type=F32)
        dw_ref[...] = lax.dot_general(p, dys, (((0,), (0,)), ((), ())), preferred_element_type=F32)
        dz_ref[:, 0:256] = (_pool_windows(dp / count, _shift_up) - dp).astype(BF16)

        ch, cb, cc = ch_ref[...], cb_ref[...], cc_ref[...]
        uc = cc * ch
        cw = cw_ref[...]
        u2, u1 = _shift_down(uc, 2), _shift_down(uc, 1)
        conv = cw[0:1, :] * u2 + cw[1:2, :] * u1 + cw[2:3, :] * uc
        dyc_v = dyc_ref[...]
        dconv = dyc_v * cb
        du = cw[0:1, :] * _shift_up(dconv, 2) + cw[1:2, :] * _shift_up(dconv, 1) + cw[2:3, :] * dconv
        dz_ref[:, 256:512] = (du * cc).astype(BF16)
        dz_ref[:, 512:768] = (dyc_v * conv).astype(BF16)
        dz_ref[:, 768:1024] = (du * ch).astype(BF16)
        dcw_ref[...] = jnp.zeros_like(dcw_ref)
        dcw_ref[0:1, :] = jnp.sum(dconv * u2, axis=0, keepdims=True)
        dcw_ref[1:2, :] = jnp.sum(dconv * u1, axis=0, keepdims=True)
        dcw_ref[2:3, :] = jnp.sum(dconv * uc, axis=0, keepdims=True)

    blk = lambda r, c: pl.BlockSpec((r, c), lambda i: (0, 0))
    return pl.pallas_call(
        body, name="poolconv_bwd", grid=(1,),
        in_specs=[blk(s, 256), blk(s, 256)] + _pc_specs(s),
        out_specs=[blk(s, 1024), blk(256, 256), blk(8, 256), blk(8, 256)],
        out_shape=[jax.ShapeDtypeStruct((s, 1024), BF16), jax.ShapeDtypeStruct((256, 256), F32),
                   jax.ShapeDtypeStruct((8, 256), F32), jax.ShapeDtypeStruct((8, 256), F32)],
        compiler_params=_params("arbitrary"),
    )(dyb, dyc, z, z, z, z, wbd, pscale, convw)


_NT = (((1,), (1,)), ((), ()))
_TN = (((0,), (0,)), ((), ()))


ATT_Q, ATT_K = 256, 256
ATT_HEADS_BWD = 8
ATT_HEADS = 8


def _att_logits(q, k, fr, q0, k0, masked):
    logits = lax.dot_general(q, k, _NT, preferred_element_type=F32) - fr
    if not masked:
        return logits
    row = q0 + lax.broadcasted_iota(jnp.int32, logits.shape, 0)
    col = k0 + lax.broadcasted_iota(jnp.int32, logits.shape, 1)
    return jnp.where(row >= col, logits, NEG_INF)


def _causal_sweep(step, qi, init):
    n_full = (qi * ATT_Q) // ATT_K
    carry = lax.fori_loop(0, n_full, lambda j, carry: step(j, carry, False), init)
    return step(n_full, carry, True)


HEAD_PAIRS = HEADS // 2


def _lane_pick(v, lane, idx):
    return jnp.sum(jnp.where(lane == idx, v, 0.0), axis=-1, keepdims=True)


def _lane_put(lane, idx, col):
    return jnp.where(lane == idx, col, 0.0)


def _split_heads(v, low):
    zero = jnp.zeros_like(v)
    return jnp.where(low, v, zero), jnp.where(low, zero, v)


def _attn_fwd(qkv, fr):
    s = qkv.shape[0]
    nk = s // ATT_K
    width = ATT_HEADS * HEAD_DIM
    groups = HEADS // ATT_HEADS

    def body(q_ref, k_ref, v_ref, fr_ref, o_ref, lse_ref):
        qi, grp = pl.program_id(0), pl.program_id(1)
        lane = lax.broadcasted_iota(jnp.int32, (ATT_Q, 128), 1)
        low = lane < HEAD_DIM
        qs = []
        for pr in range(ATT_HEADS // 2):
            qs += _split_heads(q_ref[:, 128 * pr:128 * (pr + 1)] * (HEAD_DIM ** -0.5), low)

        def step(j, carry, masked):
            k0 = pl.multiple_of(j * ATT_K, ATT_K)
            out = []
            for h in range(ATT_HEADS):
                cols = slice(128 * (h // 2), 128 * (h // 2 + 1))
                m, l, acc = carry[h]
                logits = _att_logits(qs[h], k_ref[pl.ds(k0, ATT_K), cols], fr_ref[h, pl.ds(j, 1), :], qi * ATT_Q, k0, masked)
                m_new = jnp.maximum(m, jnp.max(logits, axis=-1, keepdims=True))
                p = jnp.exp(logits - m_new)
                alpha = jnp.exp(m - m_new)
                l = alpha * l + jnp.sum(p, axis=-1, keepdims=True)
                acc = alpha * acc + jnp.dot(p.astype(BF16), v_ref[pl.ds(k0, ATT_K), cols], preferred_element_type=F32)
                out.append((m_new, l, acc))
            return tuple(out)

        one = (jnp.full((ATT_Q, 1), NEG_INF, F32), jnp.zeros((ATT_Q, 1), F32), jnp.zeros((ATT_Q, 128), F32))
        done = _causal_sweep(step, qi, (one,) * ATT_HEADS)

        @pl.when(grp == 0)
        def _():
            lse_ref[...] = jnp.zeros_like(lse_ref)

        lse = jnp.zeros((ATT_Q, 128), F32)
        for pr in range(ATT_HEADS // 2):
            (m0, l0, acc0), (m1, l1, acc1) = done[2 * pr], done[2 * pr + 1]
            o_ref[:, 128 * pr:128 * (pr + 1)] = jnp.where(low, acc0 / l0, acc1 / l1)
            head = ATT_HEADS * grp + 2 * pr
            lse = lse + _lane_put(lane, head, m0 + jnp.log(l0)) + _lane_put(lane, head + 1, m1 + jnp.log(l1))
        lse_ref[...] += lse

    return pl.pallas_call(
        body, name="attn_fwd", grid=(s // ATT_Q, groups),
        in_specs=[pl.BlockSpec((ATT_Q, width), lambda i, g: (i, g)),
                  pl.BlockSpec((s, width), lambda i, g: (0, groups + g)),
                  pl.BlockSpec((s, width), lambda i, g: (0, 2 * groups + g)),
                  pl.BlockSpec((ATT_HEADS, nk, ATT_K), lambda i, g: (g, 0, 0))],
        out_specs=[pl.BlockSpec((ATT_Q, width), lambda i, g: (i, g)), pl.BlockSpec((ATT_Q, 128), lambda i, g: (i, 0))],
        out_shape=[jax.ShapeDtypeStruct((s, A_WIDTH), F32), jax.ShapeDtypeStruct((s, 128), F32)],
        compiler_params=_params("parallel", "arbitrary"),
    )(qkv, qkv, qkv, fr)


def _attn_bwd(qkv, do, o, lse, fr):
    s = qkv.shape[0]
    nk = s // ATT_K
    scale = HEAD_DIM ** -0.5
    heads = ATT_HEADS_BWD
    width = heads * HEAD_DIM
    groups = HEADS // heads

    def body(q_ref, k_ref, v_ref, do_ref, o_ref, lse_ref, fr_ref, dq_ref, dk_ref, dv_ref, dfc_ref, dfr_ref, dk_acc, dv_acc):
        grp = pl.program_id(0)
        lane = lax.broadcasted_iota(jnp.int32, (ATT_Q, 128), 1)
        low = lane < HEAD_DIM
        low_t = lax.broadcasted_iota(jnp.int32, (128, ATT_Q), 0) < HEAD_DIM
        dk_acc[...] = jnp.zeros_like(dk_acc)
        dv_acc[...] = jnp.zeros_like(dv_acc)
        dfr_ref[...] = jnp.zeros_like(dfr_ref)

        @pl.when(grp == 0)
        def _():
            dfc_ref[...] = jnp.zeros_like(dfc_ref)

        def outer(i, carry):
            q0 = pl.multiple_of(i * ATT_Q, ATT_Q)
            rows = pl.ds(q0, ATT_Q)
            lsev = lse_ref[rows, :]
            qts, dots, qs, dos, deltas, lses = [], [], [], [], [], []
            for pr in range(heads // 2):
                pcols = slice(128 * pr, 128 * (pr + 1))
                q2, do2 = q_ref[rows, pcols] * scale, do_ref[rows, pcols]
                prod = do2 * o_ref[rows, pcols]
                deltas += [jnp.sum(jnp.where(low, prod, 0.0), axis=-1, keepdims=True),
                           jnp.sum(jnp.where(low, 0.0, prod), axis=-1, keepdims=True)]
                dob2 = do2.astype(BF16)
                qts += _split_heads(q2.astype(F32).T.astype(BF16), low_t)
                dots += _split_heads(do2.T.astype(BF16), low_t)
                qs += _split_heads(q2, low)
                dos += _split_heads(dob2, low)
                lses += [_lane_pick(lsev, lane, heads * grp + 2 * pr), _lane_pick(lsev, lane, heads * grp + 2 * pr + 1)]

            def inner(j, carry, masked):
                k0 = pl.multiple_of(j * ATT_K, ATT_K)
                krows = pl.ds(k0, ATT_K)
                out, dkt, dvt = [], [], []
                for h in range(heads):
                    pcols = slice(128 * (h // 2), 128 * (h // 2 + 1))
                    dq, dfc = carry[h]
                    k2 = k_ref[krows, pcols]
                    p = jnp.exp(_att_logits(qs[h], k2, fr_ref[h, pl.ds(j, 1), :], q0, k0, masked) - lses[h])
                    dp = lax.dot_general(dos[h], v_ref[krows, pcols], _NT, preferred_element_type=F32)
                    ds = p * (dp - deltas[h])
                    dsb = ds.astype(BF16)
                    dkt.append(jnp.dot(qts[h], dsb, preferred_element_type=F32))
                    dvt.append(jnp.dot(dots[h], p.astype(BF16), preferred_element_type=F32))
                    dfr_ref[h, pl.ds(j, 1), :] -= jnp.sum(ds, axis=0, keepdims=True)
                    out.append((dq + jnp.dot(dsb, k2, preferred_element_type=F32), dfc + (ds[:, :128] + ds[:, 128:])))
                for pr in range(heads // 2):
                    prows = slice(128 * pr, 128 * (pr + 1))
                    dk_acc[j, prows, :] += dkt[2 * pr] + dkt[2 * pr + 1]
                    dv_acc[j, prows, :] += dvt[2 * pr] + dvt[2 * pr + 1]
                return tuple(out)

            one = (jnp.zeros((ATT_Q, 128), F32), jnp.zeros((ATT_Q, 128), F32))
            done = _causal_sweep(inner, i, (one,) * heads)
            dfc = jnp.zeros((ATT_Q, 128), F32)
            for pr in range(heads // 2):
                (dq0, dfc0), (dq1, dfc1) = done[2 * pr], done[2 * pr + 1]
                dq_ref[rows, 128 * pr:128 * (pr + 1)] = (jnp.where(low, dq0, dq1) * scale).astype(BF16)
                head = heads * grp + 2 * pr
                dfc = (dfc + _lane_put(lane, head, jnp.sum(dfc0, axis=-1, keepdims=True))
                       + _lane_put(lane, head + 1, jnp.sum(dfc1, axis=-1, keepdims=True)))
            dfc_ref[rows, :] += dfc
            return carry

        lax.fori_loop(0, s // ATT_Q, outer, 0)
        for j in range(nk):
            for pr in range(heads // 2):
                prows, pcols = slice(128 * pr, 128 * (pr + 1)), slice(128 * pr, 128 * (pr + 1))
                dk_ref[ATT_K * j:ATT_K * (j + 1), pcols] = dk_acc[j, prows, :].T.astype(BF16)
                dv_ref[ATT_K * j:ATT_K * (j + 1), pcols] = dv_acc[j, prows, :].T.astype(BF16)

    part = lambda first: pl.BlockSpec((s, width), lambda g, first=first: (0, first + g))
    whole = pl.BlockSpec((s, 128), lambda g: (0, 0))
    rowv = pl.BlockSpec((heads, nk, ATT_K), lambda g: (g, 0, 0))
    return pl.pallas_call(
        body, name="attn_bwd", grid=(groups,),
        in_specs=[part(0), part(groups), part(2 * groups), part(0), part(0), whole, rowv],
        out_specs=[part(0), part(0), part(0), whole, rowv],
        out_shape=[jax.ShapeDtypeStruct((s, A_WIDTH), BF16)] * 3 + [jax.ShapeDtypeStruct((s, 128), F32), jax.ShapeDtypeStruct((HEADS, nk, ATT_K), F32)],
        scratch_shapes=[pltpu.VMEM((nk, width, ATT_K), F32), pltpu.VMEM((nk, width, ATT_K), F32)],
        compiler_params=_params("arbitrary"),
    )(qkv, qkv, qkv, do, o, lse, fr)


def _ada_fwd(c_all, w_ada, b_loc):
    depth, _, n = w_ada.shape
    tn = 512

    def body(c_ref, w_ref, b_ref, o_ref, sc_ref):
        cv = c_ref[...]
        sc = cv * jax.nn.sigmoid(cv)
        sc_ref[...] = sc
        o_ref[0] = jnp.dot(sc.astype(BF16), w_ref[0].astype(BF16), preferred_element_type=F32) + b_ref[0]

    return pl.pallas_call(
        body, name="ada_fwd", grid=(depth, n // tn),
        in_specs=[pl.BlockSpec((N_DEV, D), lambda l, j: (0, 0)), pl.BlockSpec((1, D, tn), lambda l, j: (l, 0, j)),
                  pl.BlockSpec((1, 1, tn), lambda l, j: (l, 0, j))],
        out_specs=[pl.BlockSpec((1, N_DEV, tn), lambda l, j: (l, 0, j)), pl.BlockSpec((N_DEV, D), lambda l, j: (0, 0))],
        out_shape=[jax.ShapeDtypeStruct((depth, N_DEV, n), F32), jax.ShapeDtypeStruct((N_DEV, D), F32)],
        compiler_params=_params("arbitrary", "arbitrary"),
    )(c_all, w_ada, b_loc)


def _sum_devices(gathered):
    n = gathered.shape[1]
    tn = _pick(n, (1408, 1024, 640, 512, 128))

    def body(g_ref, o_ref):
        acc = g_ref[0:8, :]
        for dev in range(1, N_DEV):
            acc = acc + g_ref[8 * dev:8 * dev + 8, :]
        o_ref[...] = acc

    return pl.pallas_call(
        body, name="sum_devices", grid=(n // tn,),
        in_specs=[pl.BlockSpec((8 * N_DEV, tn), lambda j: (0, j))], out_specs=pl.BlockSpec((8, tn), lambda j: (0, j)),
        out_shape=jax.ShapeDtypeStruct((8, n), F32), compiler_params=_params("parallel"),
    )(gathered)


def _place():
    x, y, c = lax.axis_index("x"), lax.axis_index("y"), lax.axis_index("c")
    chips = [(1 - x, y), (x, 1 - y), (1 - x, 1 - y)]
    return x, y, c, chips


def _allgather8(block, name, after=()):
    m_per, n = block.shape

    def body(x_ref, *rest):
        out_ref, send_sems, recv_sems, local_sem = rest[len(after):]
        x, y, c, chips = _place()
        me, sibling = (x, y, c), (x, y, 1 - c)

        def rows(px, py, pc):
            return out_ref.at[pl.ds((4 * px + 2 * py + pc) * m_per, m_per), :]

        def copy(k, blk, to, src=None):
            return pltpu.make_async_remote_copy(
                src_ref=rows(*blk) if src is None else src, dst_ref=rows(*blk),
                send_sem=send_sems.at[k], recv_sem=recv_sems.at[k], device_id=to, device_id_type=MESH)

        mine = pltpu.make_async_copy(x_ref, rows(*me), local_sem)
        mine.start()
        first = [copy(0, me, sibling, src=x_ref)]
        first += [copy(1 + j, me, (*chip, c), src=x_ref) for j, chip in enumerate(chips)]
        for cp in first:
            cp.start()
        passed = [copy(4 + j, (*chip, c), sibling) for j, chip in enumerate(chips)]
        for j, chip in enumerate(chips):
            copy(1 + j, (*chip, c), me).wait_recv()
            passed[j].start()
        copy(0, sibling, me).wait_recv()
        for j, chip in enumerate(chips):
            copy(4 + j, (*chip, 1 - c), me).wait_recv()
        for cp in first + passed:
            cp.wait_send()
        mine.wait()

    return pl.pallas_call(
        body, name=name, out_shape=jax.ShapeDtypeStruct((N_DEV * m_per, n), block.dtype),
        in_specs=[pl.BlockSpec(memory_space=pltpu.VMEM)] + [pl.BlockSpec(memory_space=pl.ANY)] * len(after),
        out_specs=pl.BlockSpec(memory_space=pltpu.VMEM),
        scratch_shapes=[pltpu.SemaphoreType.DMA((7,)), pltpu.SemaphoreType.DMA((7,)), pltpu.SemaphoreType.DMA],
        compiler_params=pltpu.CompilerParams(vmem_limit_bytes=V7X_VMEM_LIMIT),
    )(block, *after)


_SEM = pl.BlockSpec(memory_space=pltpu.SEMAPHORE)
_DATAFLOW = pltpu.SideEffectType.DATAFLOW_SIDE_EFFECTING


def _plan_copies(plan, refs, send_sems, recv_sems):
    return [pltpu.make_async_remote_copy(src_ref=src, dst_ref=dst, send_sem=send_sems.at[i], recv_sem=recv_sems.at[i],
                                         device_id=to, device_id_type=MESH) for i, (src, dst, to) in enumerate(plan(refs))]


class _Token(NamedTuple):
    after: jax.Array
    tie: jax.Array


def _after_operand(after):
    return after.after if isinstance(after, _Token) else after


def _copies_start(bufs, plan, n_copies, after, name):
    nb = len(bufs)

    def body(*refs):
        for cp in _plan_copies(plan, refs[:nb], refs[nb + 1], refs[nb + 2]):
            cp.start()
        for token in refs[-2:]:
            token[...] = jnp.zeros_like(token)

    sem = pltpu.SemaphoreType.DMA((n_copies,))
    vmem = pl.BlockSpec(memory_space=pltpu.VMEM)
    outs = pl.pallas_call(
        body, name=name,
        out_shape=(sem, sem, *[pltpu.HBM(b.shape, b.dtype) for b in bufs], jax.ShapeDtypeStruct((8, 128), F32),
                   jax.ShapeDtypeStruct((1, 1), F32)),
        in_specs=[_HBM] * nb + [pl.BlockSpec(memory_space=pl.ANY)],
        out_specs=(_SEM, _SEM, *[_HBM] * nb, vmem, vmem),
        input_output_aliases={i: 2 + i for i in range(nb)},
        compiler_params=pltpu.CompilerParams(has_side_effects=_DATAFLOW),
    )(*[pltpu.with_memory_space_constraint(b, pltpu.HBM) for b in bufs], _after_operand(after))
    return outs[0], outs[1], list(outs[2:2 + nb]), _Token(outs[-2], outs[-1])


def _copies_wait(started, plan, after, name):
    send_sems, recv_sems, bufs, _ = started
    nb = len(bufs)

    def body(*refs):
        for cp in _plan_copies(plan, refs[:nb], refs[nb], refs[nb + 1]):
            cp.wait_send()
            cp.wait_recv()

    return list(pl.pallas_call(
        body, name=name, out_shape=tuple(pltpu.HBM(b.shape, b.dtype) for b in bufs),
        in_specs=[_HBM] * nb + [_SEM, _SEM, pl.BlockSpec(memory_space=pl.ANY)], out_specs=tuple([_HBM] * nb),
        input_output_aliases={i: i for i in range(nb)},
        compiler_params=pltpu.CompilerParams(has_side_effects=_DATAFLOW),
    )(*bufs, send_sems, recv_sems, _after_operand(after)))


def _half_rows(ref, axis, c):
    half = ref.shape[axis] // 2
    return pl.ds(c * half, half)


def _plan_gather_ici(refs):
    n = len(refs) // 2
    x, y, c, chips = _place()
    out = []
    for a in range(n):
        rows = _half_rows(refs[a], 0, c)
        out += [(refs[a].at[rows], refs[n + a].at[2 * x + y, rows], (*chip, c)) for chip in chips]
        out.append((refs[a], refs[n + a].at[2 * x + y], (x, y, 1 - c)))
    return out


def _plan_gather_d2d(refs):
    x, y, c, chips = _place()
    out = []
    for ref in refs:
        rows = _half_rows(ref, 1, c)
        for px, py in chips:
            landed = ref.at[2 * px + py, rows]
            out.append((landed, landed, (x, y, 1 - c)))
    return out


def _plan_rs_sibling(refs):
    n = len(refs) // 2
    x, y, c, _ = _place()
    return [(refs[a].at[pl.ds(0, refs[a].shape[0]), _half_rows(refs[a], 1, 1 - c)], refs[n + a], (x, y, 1 - c)) for a in range(n)]


def _plan_rs_chips(refs):
    n = len(refs) // 2
    x, y, c, chips = _place()
    return [(refs[a].at[2 * px + py], refs[n + a].at[k], (px, py, c)) for a in range(n) for k, (px, py) in enumerate(chips)]


def _plan_rs_share(refs):
    x, y, c, _ = _place()
    return [(ref.at[_half_rows(ref, 0, c)], ref.at[_half_rows(ref, 0, c)], (x, y, 1 - c)) for ref in refs]


def _chip_sum(g, other, sel, name, blocked=True):
    nblk, half, cdim = other.shape
    tr = _pick(half, (512, 256, 128, 64) if nblk > 1 else (128, 64))
    per = half // tr

    def body(sel_ref, g_ref, t_ref, wire_ref, own_ref):
        total = g_ref[0] + t_ref[0]
        wire_ref[0] = total.astype(BF16)
        if blocked:
            @pl.when(pl.program_id(1) == sel_ref[1])
            def _():
                own_ref[...] = total
        else:
            own_ref[0] = total

    blk = pl.BlockSpec((1, tr, cdim), lambda i, p, sel_ref: (p, i, 0))
    own_spec = pl.BlockSpec((tr, cdim), lambda i, p, sel_ref: (i, 0)) if blocked else blk
    own_shape = jax.ShapeDtypeStruct((half, cdim) if blocked else other.shape, F32)
    return pl.pallas_call(
        body, name=name,
        grid_spec=pltpu.PrefetchScalarGridSpec(
            num_scalar_prefetch=1, grid=(per, nblk),
            in_specs=[pl.BlockSpec((1, tr, cdim), lambda i, p, sel_ref: (p, sel_ref[0] * per + i, 0)), blk],
            out_specs=[blk, own_spec]),
        out_shape=[jax.ShapeDtypeStruct(other.shape, BF16), own_shape],
        compiler_params=_params("parallel", "arbitrary"),
    )(sel, g, other)


def _final_sum(own, recv, sel, name):
    half, cdim = own.shape
    tr = _pick(half, (512, 256, 128, 64))
    per = half // tr

    def body(sel_ref, own_ref, r0_ref, r1_ref, r2_ref, o_ref):
        o_ref[...] = ((own_ref[...] + r0_ref[0].astype(F32)) + r1_ref[0].astype(F32)) + r2_ref[0].astype(F32)

    part = lambda k: pl.BlockSpec((1, tr, cdim), lambda i, sel_ref, k=k: (k, i, 0))
    return pl.pallas_call(
        body, name=name,
        grid_spec=pltpu.PrefetchScalarGridSpec(
            num_scalar_prefetch=1, grid=(per,),
            in_specs=[pl.BlockSpec((tr, cdim), lambda i, sel_ref: (i, 0)), part(0), part(1), part(2)],
            out_specs=pl.BlockSpec((tr, cdim), lambda i, sel_ref: (sel_ref[0] * per + i, 0))),
        out_shape=jax.ShapeDtypeStruct((2 * half, cdim), F32), compiler_params=_params("parallel"),
    )(sel, own, recv, recv, recv)


def _row(v):
    return v.reshape(1, -1)


_BR_WIDTHS = (A_WIDTH, POOL_WIDTH, CONV_WIDTH)


def _tie(v, token):
    return v if token is None else v + token.tie


def _no_hook(point, after, ready=None):
    return None


def _layer_fwd(x, w, mod, hook=_no_hook):
    s = x.shape[0]
    mod3 = mod.reshape(6, 1, D)
    h = _modnorm_fwd(x, _row(w["g_mix_pre"]), (mod3, 0), (mod3, 1), "mix_pre_fwd")
    hook("pre", h)
    z = _mm(h, w["w_all"], name="mm_in")
    qkv = z[:, Z_QKV:Z_PC].astype(BF16)
    fl = z[:, Z_FL:Z_COLS]
    cum = _cumf_fwd(fl, w["b_f_pad"])
    fr = cum[:, :HEADS].T.reshape(HEADS, s // ATT_K, ATT_K)
    br_a, lse = _attn_fwd(qkv, fr)
    br_b, br_c = _poolconv_fwd(z, w["w_pool_bd"], _tie(_row(w["pool_scale"]), hook("attn", lse)), w["conv_w"])
    hook("pool", br_b)
    pa, pb, pc = _stacked_proj((br_a, br_b, br_c), w["w_branch"], _BR_WIDTHS, False, "mm_br")
    merged = _merge_fwd(z, pa, pb, pc)
    y = _mm(merged, w["w_out"], name="mm_out")
    x1, h2 = _post_pre_fwd(x, y, _row(w["g_mix_post"]), (mod3, 2), _row(w["g_ff_pre"]), (mod3, 3), (mod3, 4), "mix_post_ff_pre_fwd")
    a, r = _mm(h2, w["w_ff1"], b_split=N_CHIPS, epilogue=_relu2_fwd, out_dtype=(F32, BF16), name="mm_ff1")
    y2 = _mm(r, w["w_ff2"], name="mm_ff2")
    x2 = _post_fwd(x1, y2, _tie(_row(w["g_ff_post"]), hook("ff_post", y2)), (mod3, 5), "ff_post_fwd")
    hook("end", x2)
    saved = dict(x=x, h=h, z=z, qkv=qkv, fl=fl, fr=fr, lse=lse, br_a=br_a, br_b=br_b, br_c=br_c, pa=pa, pb=pb, pc=pc,
                 merged=merged, y=y, x1=x1, h2=h2, a=a, r=r, y2=y2)
    return x2, saved


def _layer_bwd(dx2, sv, w, mod, hook=_no_hook):
    s = dx2.shape[0]
    mod3 = mod.reshape(6, 1, D)
    dy2, sum_ff_post = _post_bwd(dx2, sv["y2"], _row(w["g_ff_post"]), (mod3, 5), "ff_post_bwd")
    (da,) = _mm(dy2, w["w_ff2"], tb=True, epilogue=_relu2_bwd, extras=(sv["a"],), out_dtype=(BF16,), name="mm_ff2_dx")
    d_w_ff2 = _mm(sv["r"], dy2, ta=True, name="mm_ff2_dw")
    dh2 = _mm(da, w["w_ff1"], tb=True, b_split=N_CHIPS, name="mm_ff1_dx")
    d_w_ff1 = _mm(sv["h2"], da, ta=True, out_split=N_CHIPS, name="mm_ff1_dw")
    g_ff_pre = _tie(_row(w["g_ff_pre"]), hook("ff_pre", dh2, dict(w_ff1=d_w_ff1, w_ff2=d_w_ff2)))
    dx1, dy, sum_mid = _pre_post_bwd(dh2, sv["x1"], dx2, g_ff_pre, (mod3, 4), sv["y"], _row(w["g_mix_post"]), (mod3, 2), "ff_pre_mix_post_bwd")
    sum_ff_pre, sum_mix_post = sum_mid, sum_mid[3:]
    dmerged = _mm(dy, w["w_out"], tb=True, name="mm_out_dx")
    d_w_out = _mm(sv["merged"], dy, ta=True, name="mm_out_dw")
    dz, dpa, dpb, dpc = _merge_bwd(dmerged, sv["z"], sv["pa"], sv["pb"], sv["pc"])
    dbr_a, dbr_b, dbr_c = _stacked_proj((dpa, dpb, dpc), w["w_branch"], _BR_WIDTHS, True, "mm_br_dx")
    d_w_branch = _stacked_dw((sv["br_a"], sv["br_b"], sv["br_c"]), (dpa, dpb, dpc), "mm_br_dw")

    dq, dk, dv, dfc, dfr = _attn_bwd(sv["qkv"], dbr_a, sv["br_a"], sv["lse"], sv["fr"])
    dcum = dfc + jnp.pad(dfr.reshape(HEADS, s).T, ((0, 0), (0, 128 - HEADS)))
    dfl, sum_bf = _cumf_bwd(dcum, sv["fl"], _tie(w["b_f_pad"], hook("cumf", dfc)))
    dpc_z, d_wbd, sum_ps, sum_cw = _poolconv_bwd(dbr_b, dbr_c, sv["z"], w["w_pool_bd"], _row(w["pool_scale"]), w["conv_w"])
    for at, part in ((Z_QKV, dq), (Z_QKV + A_WIDTH, dk), (Z_QKV + 2 * A_WIDTH, dv), (Z_PC, dpc_z), (Z_FL, dfl)):
        dz = lax.dynamic_update_slice(dz, part, (0, at))
    dh = _mm(dz, w["w_all"], tb=True, name="mm_in_dx")
    d_w_all = _mm(sv["h"], dz, ta=True, name="mm_in_dw")
    hook("mix_pre", dh)
    dx, sum_mix_pre = _modnorm_bwd(dh, sv["x"], dx1, _row(w["g_mix_pre"]), (mod3, 1), "mix_pre_bwd")

    dmod = jnp.stack([sum_mix_pre[0], sum_mix_pre[1], sum_mix_post[0], sum_ff_pre[0], sum_ff_pre[1], sum_ff_post[0]])
    d_w_in = d_w_all[None]
    d_w_pool = jnp.stack([d_wbd[64 * g:64 * g + 64, 64 * g:64 * g + 64] for g in range(4)])
    big = dict(w_in=d_w_in, w_branch=d_w_branch, w_out=d_w_out, w_ff1=d_w_ff1, w_ff2=d_w_ff2)
    small = dict(g_mix_pre=sum_mix_pre[2], g_mix_post=sum_mix_post[1], g_ff_pre=sum_ff_pre[2], g_ff_post=sum_ff_post[1],
                 b_f=sum_bf[0, :HEADS], w_pool=d_w_pool, pool_scale=sum_ps[0], conv_w=sum_cw[0:3])
    return dx, dmod, big, small


_QKV_END, _FL_END, _PC_END = 3 * A_WIDTH, 3 * A_WIDTH + HEADS, 3 * A_WIDTH + HEADS + POOL_WIDTH + 3 * CONV_WIDTH
_W_IN_GROUPS = ((_PC_END, IN_COLS, Z_GL), (0, _QKV_END, Z_QKV), (_FL_END, _PC_END, Z_PC), (_QKV_END, _FL_END, Z_FL))
_SHARD_COLS = IN_COLS // N_CHIPS


def _w_in_layout():
    out = []
    for p in range(N_CHIPS):
        pieces = []
        for lo, hi, at in _W_IN_GROUPS:
            a, b = max(lo, p * _SHARD_COLS), min(hi, (p + 1) * _SHARD_COLS)
            if a < b:
                pieces.append((at + a - lo, at + b - lo, a - p * _SHARD_COLS))
        pieces.sort()
        segs = []
        for z0, z1, _ in pieces:
            s, e = z0 // 128 * 128, -(-z1 // 128) * 128
            if segs and s <= segs[-1][1]:
                segs[-1] = (segs[-1][0], max(e, segs[-1][1]))
            else:
                segs.append((s, e))
        assert sum(e - s for s, e in segs) == Z_WINDOW
        out.append((pieces, segs))
    return out


Z_WINDOW = 1536


def _w_in_window(shard, p):
    pieces, segs = _w_in_layout()[p]
    cols = []
    for s, e in segs:
        at = s
        for z0, z1, src in pieces:
            if s <= z0 < e:
                if z0 > at:
                    cols.append(jnp.zeros((shard.shape[0], z0 - at), shard.dtype))
                cols.append(shard[:, src:src + z1 - z0])
                at = z1
        if e > at:
            cols.append(jnp.zeros((shard.shape[0], e - at), shard.dtype))
    return jnp.concatenate(cols, axis=1)


def _own_window(shard, chip):
    return lax.switch(chip, [lambda t, p=p: _w_in_window(t, p) for p in range(N_CHIPS)], shard)


def _w_all_from_windows(blocks):
    layout = _w_in_layout()
    bounds = sorted({edge for _, segs in layout for seg in segs for edge in seg})
    parts = []
    for lo, hi in zip(bounds[:-1], bounds[1:]):
        covering = []
        for p, (_, segs) in enumerate(layout):
            at = 0
            for s, e in segs:
                if s <= lo and hi <= e:
                    covering.append(blocks[p][:, at + lo - s:at + hi - s])
                at += e - s
        assert covering
        parts.append(covering[0] if len(covering) == 1 else covering[0] + covering[1])
    return jnp.concatenate(parts, axis=1)


def _w_in_shard(d_w_all, p):
    pieces = []
    for lo, hi, at in sorted(_W_IN_GROUPS):
        a, b = max(lo, p * _SHARD_COLS), min(hi, (p + 1) * _SHARD_COLS)
        if a < b:
            pieces.append(d_w_all[:, at + a - lo:at + b - lo])
    return jnp.concatenate(pieces, axis=1)


def _w_in_shards(d_w_all):
    return jnp.stack([_w_in_shard(d_w_all, p) for p in range(N_CHIPS)])


def _full_layer_weights(w_in_blocks, w_branch, w_out, w_ff1, w_ff2, g_mix_pre, g_mix_post, g_ff_pre, g_ff_post, b_f, w_pool, pool_scale, conv_w):
    w_all = None if w_in_blocks is None else _w_all_from_windows(w_in_blocks)
    wbd = (w_pool[:, :, None, :] * jnp.eye(4, dtype=F32)[:, None, :, None]).reshape(POOL_WIDTH, POOL_WIDTH)
    return dict(w_all=w_all, w_branch=w_branch, w_out=w_out, w_ff1=w_ff1, w_ff2=w_ff2, g_mix_pre=g_mix_pre, g_mix_post=g_mix_post,
                g_ff_pre=g_ff_pre, g_ff_post=g_ff_post, b_f_pad=jnp.pad(b_f, (0, 128 - HEADS)).reshape(1, 128), w_pool_bd=wbd,
                pool_scale=pool_scale, conv_w=conv_w)


class _NoComm:
    def layer_weights(self, l):
        raise NotImplementedError

    def fwd_hook(self, l):
        return _no_hook

    def bwd_hook(self, l):
        return _no_hook

    def grads_ready(self, l, big):
        return None


class _Layers(_NoComm):
    def __init__(self, layers):
        self.layers = layers

    def layer_weights(self, l):
        return self.layers[l]


def _local_step(x, target, mods, comm):
    saved, weights = [], []
    act = x
    for l in range(DEPTH):
        weights.append(comm.layer_weights(l))
        act, sv = _layer_fwd(act, weights[l], mods[l], comm.fwd_hook(l))
        saved.append(sv)
    dact, sq = _loss_head(act, target)
    loss = sq[0, 0] * (0.5 / D)
    dmods, bigs, smalls = [None] * DEPTH, [None] * DEPTH, [None] * DEPTH
    token = None
    for l in reversed(range(DEPTH)):
        dact, dmods[l], bigs[l], smalls[l] = _layer_bwd(dact, saved[l], weights[l], _tie(mods[l], token), comm.bwd_hook(l))
        token = comm.grads_ready(l, bigs[l])
    return loss, dact, jnp.stack(dmods), bigs, smalls


_BIG = ("w_in", "w_branch", "w_out", "w_ff1", "w_ff2")


class _GatherJob:
    def __init__(self, tag, shards, after):
        self.tag, self.n = tag, len(shards)
        lands = [lax.empty((N_CHIPS,) + s.shape, s.dtype) for s in shards]
        self.state = _copies_start(list(shards) + lands, _plan_gather_ici, 4 * self.n, after, "gather_ici_start_" + tag)
        self.token = self.state[3]

    def pass_on(self, after):
        bufs = _copies_wait(self.state, _plan_gather_ici, after, "gather_ici_wait_" + self.tag)
        self.state = _copies_start(bufs[self.n:], _plan_gather_d2d, 3 * self.n, bufs[0], "gather_d2d_start_" + self.tag)
        self.token = self.state[3]
        return self.token

    def done(self, after):
        return _copies_wait(self.state, _plan_gather_d2d, after, "gather_d2d_wait_" + self.tag)


class _ReduceJob:
    def __init__(self, tag, names, grads, sel, after):
        self.tag, self.names, self.n, self.sel = tag, names, len(names), sel
        lands = [lax.empty((g.shape[0], g.shape[1] // 2, g.shape[2]), F32) for g in grads]
        self.state = _copies_start(list(grads) + lands, _plan_rs_sibling, self.n, after, "rs_sibling_start_" + tag)
        self.token = self.state[3]

    def _chip_sum(self, name, g, other):
        if g.shape[0] == N_CHIPS:
            return _chip_sum(g, other, self.sel, "rs_chip_sum_" + name)
        wire, total = _chip_sum(g, other, self.sel, "rs_chip_sum_" + name, blocked=False)
        own = lax.switch(self.sel[1], [lambda t, p=p: _w_in_shard(t, p) for p in range(N_CHIPS)], total[0])
        return _w_in_shards(wire[0]), own

    def chip_sums(self, after):
        bufs = _copies_wait(self.state, _plan_rs_sibling, after, "rs_sibling_wait_" + self.tag)
        wires, self.owns = zip(*[self._chip_sum(name, bufs[i], bufs[self.n + i]) for i, name in enumerate(self.names)])
        lands = [lax.empty((3,) + w.shape[1:], BF16) for w in wires]
        self.state = _copies_start(list(wires) + lands, _plan_rs_chips, 3 * self.n, self.owns[0], "rs_chips_start_" + self.tag)
        self.token = self.state[3]
        return self.token

    def final_sums(self, after):
        bufs = _copies_wait(self.state, _plan_rs_chips, after, "rs_chips_wait_" + self.tag)
        sums = [_final_sum(self.owns[i], bufs[self.n + i], self.sel, "rs_final_" + name) for i, name in enumerate(self.names)]
        self.state = _copies_start(sums, _plan_rs_share, self.n, sums[0], "rs_share_start_" + self.tag)
        self.token = self.state[3]
        return self.token

    def done(self, after):
        return dict(zip(self.names, _copies_wait(self.state, _plan_rs_share, after, "rs_share_wait_" + self.tag)))


def _chip_blocks(g):
    return g if g.ndim == 3 else g.reshape(N_CHIPS, -1, g.shape[1])


class _StepComm(_NoComm):
    def __init__(self, big_weights, w_in0, sel, after):
        self.sel = sel
        self.small, self.grads, self.jobs = None, [dict() for _ in range(DEPTH)], {}
        self.jobs["in0"] = _GatherJob("in0", [w_in0], after)
        later = lax.optimization_barrier((tuple(big_weights), self.jobs["in0"].token.after))[0]
        self.jobs["rest0"] = _GatherJob("rest0", [w[0].astype(BF16) for w in later[1:]], self.jobs["in0"].token)
        layer1 = [w[1].astype(BF16) for w in later]
        self.jobs["all1"] = _GatherJob("all1", [_own_window(layer1[0], sel[1])] + layer1[1:], self.jobs["rest0"].token)

    def layer_weights(self, l):
        if l == 0:
            self.weights0 = _full_layer_weights(None, None, None, None, None, *self.small[0])
            return self.weights0
        g_in, g_br, g_out, g_f1, g_f2 = self.landed1
        return _full_layer_weights(g_in, g_br.reshape(D, D), g_out.reshape(D, D), g_f1, g_f2.reshape(D_FF, D), *self.small[1])

    def fwd_hook(self, l):
        if l != 0:
            return _no_hook

        def hook(point, after, ready=None):
            if point == "pre":
                job = self.jobs["in0"]
                started = after[:8, :128].astype(F32) + self.jobs["all1"].token.after
                self.weights0["w_all"] = _w_all_from_windows(job.done(job.pass_on(started))[0])
            if point == "attn":
                return self.jobs["rest0"].pass_on(after)
            if point == "ff_post":
                return self.jobs["all1"].pass_on(after)
            if point == "pool":
                g_br, g_out, g_f1, g_f2 = self.jobs["rest0"].done(after)
                self.weights0.update(w_branch=g_br.reshape(D, D), w_out=g_out.reshape(D, D), w_ff1=g_f1, w_ff2=g_f2.reshape(D_FF, D))
            if point == "end":
                self.landed1 = self.jobs["all1"].done(after)
            return None
        return hook

    def bwd_hook(self, l):
        if l != 0:
            return _no_hook

        def hook(point, after, ready=None):
            jobs = self.jobs
            if point == "ff_pre":
                token = jobs["rs1"].chip_sums(after)
                jobs["rs0_ff"] = _ReduceJob("0_ff", ("w_ff1", "w_ff2"), [_chip_blocks(ready[n]) for n in ("w_ff1", "w_ff2")], self.sel, token)
                return jobs["rs0_ff"].token
            if point == "cumf":
                return jobs["rs0_ff"].chip_sums(jobs["rs1"].final_sums(after))
            self.grads[1] = jobs["rs1"].done(after)
            return None
        return hook

    def grads_ready(self, l, big):
        if l == 1:
            self.jobs["rs1"] = _ReduceJob("1", _BIG, [_chip_blocks(big[n]) for n in _BIG], self.sel, self.sel)
            return self.jobs["rs1"].token
        names = ("w_in", "w_branch", "w_out")
        self.jobs["rs0_mix"] = _ReduceJob("0_mix", names, [_chip_blocks(big[n]) for n in names], self.sel, self.sel)
        return self.jobs["rs0_mix"].token

    def finish_sums(self, after):
        jobs = self.jobs
        token = jobs["rs0_mix"].chip_sums(after)
        return jobs["rs0_ff"].final_sums(token)

    def finish_ff(self, after):
        self.grads[0].update(self.jobs["rs0_ff"].done(after))

    def finish_mix(self, after):
        job = self.jobs["rs0_mix"]
        self.grads[0].update(job.done(job.final_sums(after)))


_SMALL = ("g_mix_pre", "g_mix_post", "g_ff_pre", "g_ff_post", "b_f", "w_pool", "pool_scale", "conv_w")


def _w_in_view(t):
    return t.reshape(DEPTH, D // 128, 128, _SHARD_COLS).transpose(3, 1, 0, 2).reshape(_SHARD_COLS * (D // 128) * DEPTH, 128)


def _w_in_unview(t):
    return t.reshape(_SHARD_COLS, D // 128, DEPTH, 128).transpose(2, 1, 3, 0).reshape(DEPTH, D, _SHARD_COLS)


def _pack(parts, rows=8):
    flat = jnp.concatenate([p.reshape(-1) for p in parts])
    width = -(-flat.shape[0] // (rows * 128)) * 128
    return jnp.pad(flat, (0, rows * width - flat.shape[0])).reshape(rows, width)


def _unpack(packed, like):
    flat = packed.reshape(-1)
    out, at = [], 0
    for ref in like:
        out.append(flat[at:at + ref.size].reshape(ref.shape))
        at += ref.size
    return out


def kernel(x, c, w_ada, b_ada, g_mix_pre, g_mix_post, g_ff_pre, g_ff_post, w_in, b_f, w_pool, pool_scale, conv_w, w_branch, w_out, w_ff1, w_ff2, loss_target, m_w_ada, m_b_ada, m_g_mix_pre, m_g_mix_post, m_g_ff_pre, m_g_ff_post, m_w_in, m_b_f, m_w_pool, m_pool_scale, m_conv_w, m_w_branch, m_w_out, m_w_ff1, m_w_ff2, v_w_ada, v_b_ada, v_g_mix_pre, v_g_mix_post, v_g_ff_pre, v_g_ff_post, v_w_in, v_b_f, v_w_pool, v_pool_scale, v_conv_w, v_w_branch, v_w_out, v_w_ff1, v_w_ff2):
    xi, yi, ci = lax.axis_index("x"), lax.axis_index("y"), lax.axis_index("c")
    chip = 2 * xi + yi
    dev = 2 * chip + ci
    n_ada = w_ada.shape[2]

    first = jnp.zeros((8, D + 384), F32).at[0, :D].set(c[0]).at[0, D:].set(conv_w.reshape(-1))
    w_in0 = _own_window(w_in[0].astype(BF16), chip)
    got = _allgather8(first, "gather_cond", after=(w_in0,)).reshape(N_DEV, 8, D + 384)[:, 0]
    c_all = got[:, :D]
    conv_full = got[0::2, D:].reshape(N_CHIPS, DEPTH, 3, CONV_WIDTH // N_CHIPS).transpose(1, 2, 0, 3).reshape(DEPTH, 3, CONV_WIDTH)

    b_loc = lax.dynamic_slice_in_dim(b_ada, chip * n_ada, n_ada, axis=1).reshape(DEPTH, 1, n_ada)
    mod_cols, silu_c = _ada_fwd(c_all, w_ada, b_loc)
    got = _allgather8(mod_cols.reshape(DEPTH * N_DEV, n_ada), "gather_mod").reshape(N_DEV, DEPTH, N_DEV, n_ada)[0::2]
    mod_all = got.transpose(1, 2, 0, 3).reshape(DEPTH, N_DEV, 6, D)
    mods = lax.dynamic_index_in_dim(mod_all, dev, axis=1, keepdims=False)

    comm = _StepComm((w_in, w_branch, w_out, w_ff1, w_ff2), w_in0, jnp.stack([ci, chip]).astype(jnp.int32), mods)
    comm.small = [(g_mix_pre[l], g_mix_post[l], g_ff_pre[l], g_ff_post[l], b_f[l], w_pool[l], pool_scale[l], conv_full[l]) for l in range(DEPTH)]
    loss_part, grad_x, dmods, bigs, smalls = _local_step(x[0], loss_target[0], mods, comm)

    small_parts = [smalls[l][name] for name in _SMALL for l in range(DEPTH)] + [loss_part.reshape(1)]
    packed = _tie(_pack([dmods] + small_parts), comm.jobs["rs0_mix"].token)
    gathered = _allgather8(packed, "gather_small")
    dmod_all = gathered.reshape(N_DEV, -1)[:, :dmods.size].reshape(N_DEV, DEPTH, 6 * D)
    summed = _unpack(_sum_devices(gathered), [dmods] + small_parts)
    grad_b_ada = summed[0].reshape(DEPTH, 6 * D)
    loss = summed[-1][0]
    small_grads = {name: jnp.stack(summed[1 + 2 * i:3 + 2 * i]) for i, name in enumerate(_SMALL)}
    small_grads["conv_w"] = lax.dynamic_slice_in_dim(small_grads["conv_w"], chip * (CONV_WIDTH // N_CHIPS), CONV_WIDTH // N_CHIPS, axis=2)

    dmod_loc = lax.dynamic_slice_in_dim(dmod_all.transpose(1, 0, 2), chip * n_ada, n_ada, axis=2)
    tail_token = comm.finish_sums(grad_b_ada)
    silu_pad = _tie(jnp.pad(silu_c, ((0, 128 - N_DEV), (0, 0))), tail_token)
    dmod_pad = jnp.pad(dmod_loc.transpose(1, 0, 2).reshape(N_DEV, DEPTH * n_ada), ((0, 128 - N_DEV), (0, 0)))
    grad_w_ada = _mm(silu_pad, dmod_pad, ta=True, out_split=DEPTH, name="mm_ada_dw")

    grads = dict(w_ada=grad_w_ada, b_ada=grad_b_ada, **small_grads)
    weights = dict(w_ada=w_ada, b_ada=b_ada, g_mix_pre=g_mix_pre, g_mix_post=g_mix_post, g_ff_pre=g_ff_pre, g_ff_post=g_ff_post, w_in=w_in,
                   b_f=b_f, w_pool=w_pool, pool_scale=pool_scale, conv_w=conv_w, w_branch=w_branch, w_out=w_out, w_ff1=w_ff1, w_ff2=w_ff2)
    m_in = dict(w_ada=m_w_ada, b_ada=m_b_ada, g_mix_pre=m_g_mix_pre, g_mix_post=m_g_mix_post, g_ff_pre=m_g_ff_pre, g_ff_post=m_g_ff_post,
                w_in=m_w_in, b_f=m_b_f, w_pool=m_w_pool, pool_scale=m_pool_scale, conv_w=m_conv_w, w_branch=m_w_branch, w_out=m_w_out,
                w_ff1=m_w_ff1, w_ff2=m_w_ff2)
    v_in = dict(w_ada=v_w_ada, b_ada=v_b_ada, g_mix_pre=v_g_mix_pre, g_mix_post=v_g_mix_post, g_ff_pre=v_g_ff_pre, g_ff_post=v_g_ff_post,
                w_in=v_w_in, b_f=v_b_f, w_pool=v_w_pool, pool_scale=v_pool_scale, conv_w=v_conv_w, w_branch=v_w_branch, w_out=v_w_out,
                w_ff1=v_w_ff1, w_ff2=v_w_ff2)
    order = ("w_ada", "b_ada", "g_mix_pre", "g_mix_post", "g_ff_pre", "g_ff_post", "w_in", "b_f", "w_pool", "pool_scale", "conv_w",
             "w_branch", "w_out", "w_ff1", "w_ff2")
    delta, new_m, new_v = {}, {}, {}
    tiny = ("b_ada",) + _SMALL
    tiny_g = [_tie(grads[tiny[0]], tail_token)] + [grads[name] for name in tiny[1:]]
    res = _adamw_many([weights[name] for name in tiny], tiny_g, [m_in[name] for name in tiny], [v_in[name] for name in tiny], "adamw_small")
    for out, vals in zip((delta, new_m, new_v), res):
        out.update(zip(tiny, vals))
    delta["w_ada"], new_m["w_ada"], new_v["w_ada"] = _adamw(w_ada, grad_w_ada, m_w_ada, v_w_ada, "adamw_w_ada")
    comm.finish_ff(delta["w_ada"][0, :8, :128] + delta["b_ada"][0, :128])
    for name in ("w_ff1", "w_ff2", "w_in", "w_branch", "w_out"):
        if name == "w_in":
            comm.finish_mix(delta["w_ff2"][0, :8, :128])
        g_layers = [comm.grads[l][name] for l in range(DEPTH)]
        if name == "w_in":
            g_view = lax.optimization_barrier(_w_in_view(jnp.stack(g_layers)))
            res = _adamw(_w_in_view(w_in), g_view, _w_in_view(m_w_in), _w_in_view(v_w_in), "adamw_w_in")
            grads[name], delta[name], new_m[name], new_v[name] = [_w_in_unview(t) for t in (g_view, *res)]
        else:
            delta[name], new_m[name], new_v[name], grads[name] = _adamw_layers(weights[name], g_layers, m_in[name], v_in[name], "adamw_" + name)

    return (loss, grad_x[None], *[grads[n] for n in order], *[delta[n] for n in order], *[new_m[n] for n in order],
            *[new_v[n] for n in order])
```

```python
from typing import NamedTuple

import jax
import jax.numpy as jnp
from jax import lax
from jax.experimental import pallas as pl
from jax.experimental.pallas import tpu as pltpu

F32 = jnp.float32
BF16 = jnp.bfloat16
MESH = pl.DeviceIdType.MESH

D = 1024
DEPTH = 2
HEADS = 8
HEAD_DIM = 64
A_WIDTH = 512
POOL_WIDTH = 256
CONV_WIDTH = 256
D_FF = 4096
IN_COLS = 5640
Z_GL, Z_QKV, Z_PC, Z_FL, Z_COLS = 0, 3072, 4608, 5632, 5760
RMS_EPS = 1e-6
NEG_INF = -1e30
ROW_TILE = 512
EW_ROWS = 256
N_CHIPS = 4
N_DEV = 8
V7X_VMEM_LIMIT = 48 * 1024 * 1024

ADAM_LR = 0.001
ADAM_B1 = 0.9
ADAM_B2 = 0.999
ADAM_EPS = 1e-08
ADAM_WD = 0.01
ADAM_STEP = 10

_HBM = pl.BlockSpec(memory_space=pltpu.HBM)


def _params(*sem):
    return pltpu.CompilerParams(dimension_semantics=sem, vmem_limit_bytes=V7X_VMEM_LIMIT)


def _pick(dim, cands):
    for cand in cands:
        if dim % cand == 0:
            return cand
    return dim


MM_TILE_BUDGET = 39 * 1024 * 1024


def _mm_tiles(m, n, k, k_unit, tn, a_size, b_size, out_size):
    for tk in (k_unit, 2048, 1152, 1024, 640, 512, 256, 128):
        if k_unit % tk:
            continue
        for tm in (2048, 1024, 512, 256, 128):
            if m % tm or ((m // tm) * (n // tn) < 2 and tm > 512):
                continue
            need = 2 * (tm * tk * a_size + tk * tn * b_size + tm * tn * out_size) + (0 if tk == k else 4 * tm * tn)
            if need <= MM_TILE_BUDGET and (tk == k_unit or tm >= 512):
                return tm, tk
    return 128, 128


def _mm(a, b, *, ta=False, tb=False, b_split=1, out_split=1, out_dtype=F32, epilogue=None, extras=(), name):
    (k, m) = a.shape if ta else a.shape[::-1]
    b_rows = b.shape[-2]
    b_cols = b.shape[-1] * b_split
    (n, k2) = (b_rows, b_cols) if tb else (b_cols, b_rows)
    assert k == k2, (a.shape, b.shape, ta, tb)
    n_unit = n // (out_split * (1 if tb else b_split))
    k_unit = k // (b_split if tb else 1)
    tn = _pick(n_unit, (1024, 1152, 768, 640, 512, 256, 128))
    tm, tk = _mm_tiles(m, n, k, k_unit, tn, a.dtype.itemsize, b.dtype.itemsize,
                       sum(jnp.dtype(dt).itemsize for dt in out_dtype) + 4 * len(extras) if epilogue else jnp.dtype(out_dtype).itemsize)
    nk = k // tk
    dims = (((0 if ta else 1,), (1 if tb else 0,)), ((), ()))

    def dot(a_ref, b_ref):
        b_val = b_ref[0] if b_split > 1 else b_ref[...]
        return lax.dot_general(a_ref[...].astype(BF16), b_val.astype(BF16), dims, preferred_element_type=F32)

    n_extra = len(extras)
    assert epilogue is None or out_split == 1

    def put(refs, val):
        if epilogue is not None:
            for o_ref, res in zip(refs[n_extra:], epilogue(val, *[r[...] for r in refs[:n_extra]])):
                o_ref[...] = res.astype(o_ref.dtype)
        elif out_split > 1:
            refs[0][0] = val.astype(refs[0].dtype)
        else:
            refs[0][...] = val.astype(refs[0].dtype)

    def body_single(a_ref, b_ref, *refs):
        put(refs, dot(a_ref, b_ref))

    def body_acc(a_ref, b_ref, *refs):
        kk = pl.program_id(2)
        acc_ref = refs[-1]

        @pl.when(kk == 0)
        def _():
            acc_ref[...] = jnp.zeros_like(acc_ref)

        acc_ref[...] += dot(a_ref, b_ref)

        @pl.when(kk == nk - 1)
        def _():
            put(refs[:-1], acc_ref[...])

    a_spec = pl.BlockSpec((tk, tm), lambda i, j, kk: (kk, i)) if ta else pl.BlockSpec((tm, tk), lambda i, j, kk: (i, kk))
    if b_split == 1:
        b_spec = pl.BlockSpec((tn, tk), lambda i, j, kk: (j, kk)) if tb else pl.BlockSpec((tk, tn), lambda i, j, kk: (kk, j))
    elif tb:
        per = k_unit // tk
        b_spec = pl.BlockSpec((1, tn, tk), lambda i, j, kk: (kk // per, j, kk % per))
    else:
        per = n // b_split // tn
        b_spec = pl.BlockSpec((1, tk, tn), lambda i, j, kk: (j // per, kk, j % per))
    if out_split == 1:
        o_spec = pl.BlockSpec((tm, tn), lambda i, j, kk: (i, j))
        o_shape = None if epilogue is not None else jax.ShapeDtypeStruct((m, n), out_dtype)
    else:
        per_o = n // out_split // tn
        o_spec = pl.BlockSpec((1, tm, tn), lambda i, j, kk: (j // per_o, i, j % per_o))
        o_shape = jax.ShapeDtypeStruct((out_split, m, n // out_split), out_dtype)
    if epilogue is not None:
        o_shape = [jax.ShapeDtypeStruct((m, n), dt) for dt in out_dtype]
        o_spec = [o_spec] * len(out_dtype)
    return pl.pallas_call(
        body_single if nk == 1 else body_acc, name=name, grid=(m // tm, n // tn, nk),
        in_specs=[a_spec, b_spec] + [pl.BlockSpec((tm, tn), lambda i, j, kk: (i, j))] * n_extra, out_specs=o_spec, out_shape=o_shape,
        scratch_shapes=[] if nk == 1 else [pltpu.VMEM((tm, tn), F32)],
        compiler_params=_params("parallel", "parallel", "arbitrary"),
    )(a, b, *extras)


def _stacked_proj(lhs, w, widths, tb, name, gates=None):
    s = lhs[0].shape[0]
    n = w.shape[1]
    starts = [sum(widths[:i]) for i in range(len(widths))]
    n_lhs = len(lhs)
    n_gate = 0 if gates is None else n_lhs
    assert not (tb and n_gate)
    dims = (((1,), (1 if tb else 0,)), ((), ()))

    def body(*refs):
        w_ref = refs[n_lhs]
        g_refs, o_refs = refs[n_lhs + 1:n_lhs + 1 + n_gate], refs[n_lhs + 1 + n_gate:]
        for a_ref, o_ref, start, width in zip(refs[:n_lhs], o_refs, starts, widths):
            o_ref[...] = lax.dot_general(a_ref[...].astype(BF16), w_ref[start:start + width, :].astype(BF16), dims,
                                         preferred_element_type=F32)
        if n_gate:
            merged = jax.nn.sigmoid(g_refs[0][...]) * o_refs[0][...]
            for g_ref, o_ref in zip(g_refs[1:], o_refs[1:n_lhs]):
                merged = merged + jax.nn.sigmoid(g_ref[...]) * o_ref[...]
            o_refs[n_lhs][...] = merged.astype(BF16)

    in_cols = [n if tb else width for width in widths]
    out_cols = [width if tb else n for width in widths]
    assert [a.shape for a in lhs] == [(s, cols) for cols in in_cols], ([a.shape for a in lhs], widths, tb)
    return pl.pallas_call(
        body, name=name, grid=(s // ROW_TILE,),
        in_specs=[_row_spec(cols) for cols in in_cols] + [pl.BlockSpec(w.shape, lambda i: (0, 0))]
        + [_row_spec(n, i) for i in range(n_gate)],
        out_specs=[_row_spec(cols) for cols in out_cols] + [_row_spec(n)] * (n_gate > 0),
        out_shape=[jax.ShapeDtypeStruct((s, cols), F32) for cols in out_cols] + [jax.ShapeDtypeStruct((s, n), BF16)] * (n_gate > 0),
        compiler_params=_params("parallel"),
    )(*lhs, w, *([gates] * n_gate))


def _stacked_dw(lhs, rhs, name):
    n = rhs[0].shape[1]
    tr = min(a.shape[1] for a in lhs)
    tn = _pick(n, (1024, 512, 256, 128))
    counts = [a.shape[1] // tr for a in lhs]
    starts = [sum(counts[:i]) for i in range(len(lhs))]
    assert all(a.shape[1] == c * tr for a, c in zip(lhs, counts))
    n_lhs = len(lhs)

    def body(*refs):
        o_ref = refs[-1]
        j = pl.program_id(1)
        for a_ref, b_ref, start, count in zip(refs[:n_lhs], refs[n_lhs:-1], starts, counts):
            @pl.when((j >= start) & (j < start + count))
            def _():
                o_ref[...] = lax.dot_general(a_ref[...].astype(BF16), b_ref[...].astype(BF16), (((0,), (0,)), ((), ())),
                                             preferred_element_type=F32)

    def lhs_spec(a, start, count):
        return pl.BlockSpec((a.shape[0], tr), lambda c, j: (0, jnp.clip(j - start, 0, count - 1)))

    return pl.pallas_call(
        body, name=name, grid=(n // tn, sum(counts)),
        in_specs=[lhs_spec(a, st, ct) for a, st, ct in zip(lhs, starts, counts)]
        + [pl.BlockSpec((b.shape[0], tn), lambda c, j: (0, c)) for b in rhs],
        out_specs=pl.BlockSpec((tr, tn), lambda c, j: (j, c)),
        out_shape=jax.ShapeDtypeStruct((sum(counts) * tr, n), F32),
        compiler_params=_params("parallel", "parallel"),
    )(*lhs, *rhs)


def _ew(fn, ins, out_dtypes, name, tc=None):
    shape = ins[0].shape
    lead, (rows, cols) = shape[:-2], shape[-2:]
    tc = cols if tc is None else tc
    if tc > 1024:
        tr = _pick(rows, (EW_ROWS, 128, 8))
    elif tc > 128:
        tr = _pick(rows, (2 * EW_ROWS, EW_ROWS, 128, 8))
    else:
        tr = _pick(rows, (4096, 2256, 2048, 1024, EW_ROWS, 8))
    n_in = len(ins)

    def body(*refs):
        res = fn(*[r[...] for r in refs[:n_in]])
        for o_ref, val in zip(refs[n_in:], res):
            o_ref[...] = val.astype(o_ref.dtype)

    if lead:
        spec = pl.BlockSpec((None, tr, tc), lambda l, i, j: (l, i, j))
    else:
        spec = pl.BlockSpec((tr, tc), lambda i, j: (i, j))
    return pl.pallas_call(
        body, name=name, grid=lead + (rows // tr, cols // tc),
        in_specs=[spec] * n_in, out_specs=[spec] * len(out_dtypes),
        out_shape=[jax.ShapeDtypeStruct(shape, dt) for dt in out_dtypes],
        compiler_params=_params(*(["parallel"] * (len(lead) + 2))),
    )(*ins)


def _relu2_fwd(a):
    r = jnp.maximum(a, 0.0)
    return a, r * r


def _relu2_bwd(dr, a):
    return (dr * (2.0 * jnp.maximum(a, 0.0)),)


def _adamw_math(w, g, m, v):
    m = ADAM_B1 * m + (1.0 - ADAM_B1) * g
    v = ADAM_B2 * v + (1.0 - ADAM_B2) * (g * g)
    m_hat = m / (1.0 - ADAM_B1 ** ADAM_STEP)
    v_hat = v / (1.0 - ADAM_B2 ** ADAM_STEP)
    delta = -ADAM_LR * (m_hat / (jnp.sqrt(v_hat) + ADAM_EPS) + ADAM_WD * w)
    return delta, m, v


def _adamw(w, g, m, v, name):
    return _ew(_adamw_math, [w, g, m, v], [F32, F32, F32], name)


def _adamw_layers(w, g_layers, m, v, name):
    depth, rows, cols = w.shape
    tr = _pick(rows, (2 * EW_ROWS, EW_ROWS, 128, 8)) if cols <= 1024 else _pick(rows, (EW_ROWS, 128, 8))

    def body(w_ref, *refs):
        g_refs, (m_ref, v_ref, d_ref, mo_ref, vo_ref, go_ref) = refs[:depth], refs[depth:]
        layer = pl.program_id(0)
        g = g_refs[0][...]
        for l in range(1, depth):
            g = jnp.where(layer == l, g_refs[l][...], g)
        d_ref[...], mo_ref[...], vo_ref[...] = _adamw_math(w_ref[...], g, m_ref[...], v_ref[...])
        go_ref[...] = g

    spec = pl.BlockSpec((None, tr, cols), lambda l, i: (l, i, 0))
    g_specs = [pl.BlockSpec((tr, cols), lambda l, i, k=k: (jnp.where(l == k, i, 0), 0)) for k in range(depth)]
    return pl.pallas_call(
        body, name=name, grid=(depth, rows // tr),
        in_specs=[spec] + g_specs + [spec, spec], out_specs=[spec] * 4,
        out_shape=[jax.ShapeDtypeStruct(w.shape, F32)] * 4, compiler_params=_params("arbitrary", "arbitrary"),
    )(w, *g_layers, m, v)


def _adamw_many(ws, gs, ms, vs, name):
    n = len(ws)

    def body(*refs):
        for i in range(n):
            res = _adamw_math(*[refs[k * n + i][...] for k in range(4)])
            for k in range(3):
                refs[(4 + k) * n + i][...] = res[k]

    outs = pl.pallas_call(
        body, name=name, out_shape=[jax.ShapeDtypeStruct(w.shape, F32) for w in ws] * 3,
        compiler_params=pltpu.CompilerParams(vmem_limit_bytes=V7X_VMEM_LIMIT),
    )(*ws, *gs, *ms, *vs)
    return outs[:n], outs[n:2 * n], outs[2 * n:]


def _row_spec(cols, block=0):
    return pl.BlockSpec((ROW_TILE, cols), lambda i, block=block: (i, block))


def _vec_spec(cols):
    return pl.BlockSpec((1, cols), lambda i: (0, 0))


def _vec_args(*vecs):
    arrays = [v[0] if isinstance(v, tuple) else v for v in vecs]
    specs = [pl.BlockSpec((None, 1, D), lambda i, row=v[1]: (row, 0, 0)) if isinstance(v, tuple) else _vec_spec(D) for v in vecs]
    return arrays, specs


def _sum_spec(cols):
    return pl.BlockSpec((8, cols), lambda i: (0, 0))


def _rstd(x):
    return lax.rsqrt(jnp.mean(x * x, axis=-1, keepdims=True) + RMS_EPS)


def _modnorm_fwd(x, g, shift, scale, name):
    s = x.shape[0]

    def body(x_ref, g_ref, sh_ref, sc_ref, h_ref):
        xv = x_ref[...]
        n = xv * _rstd(xv)
        h_ref[...] = ((n * g_ref[...]) * (1.0 + sc_ref[...]) + sh_ref[...]).astype(BF16)

    vecs, vec_specs = _vec_args(g, shift, scale)
    return pl.pallas_call(
        body, name=name, grid=(s // ROW_TILE,),
        in_specs=[_row_spec(D)] + vec_specs, out_specs=_row_spec(D),
        out_shape=jax.ShapeDtypeStruct((s, D), BF16), compiler_params=_params("parallel"),
    )(x, *vecs)


def _post_fwd(x, y, g, gate, name):
    s = x.shape[0]

    def body(x_ref, y_ref, g_ref, gate_ref, o_ref):
        yv = y_ref[...]
        o_ref[...] = x_ref[...] + gate_ref[...] * ((yv * _rstd(yv)) * g_ref[...])

    vecs, vec_specs = _vec_args(g, gate)
    return pl.pallas_call(
        body, name=name, grid=(s // ROW_TILE,),
        in_specs=[_row_spec(D), _row_spec(D)] + vec_specs, out_specs=_row_spec(D),
        out_shape=jax.ShapeDtypeStruct((s, D), F32), compiler_params=_params("parallel"),
    )(x, y, *vecs)


def _post_bwd(dxo, y, g, gate, name):
    s = dxo.shape[0]

    def body(d_ref, y_ref, g_ref, gate_ref, dy_ref, sum_ref):
        @pl.when(pl.program_id(0) == 0)
        def _():
            sum_ref[...] = jnp.zeros_like(sum_ref)

        dv, yv = d_ref[...], y_ref[...]
        r = _rstd(yv)
        n = yv * r
        sum_ref[0:1, :] += jnp.sum(dv * (n * g_ref[...]), axis=0, keepdims=True)
        sum_ref[1:2, :] += jnp.sum((dv * gate_ref[...]) * n, axis=0, keepdims=True)
        dn = (dv * gate_ref[...]) * g_ref[...]
        dy_ref[...] = (r * (dn - n * jnp.mean(dn * n, axis=-1, keepdims=True))).astype(BF16)

    vecs, vec_specs = _vec_args(g, gate)
    return pl.pallas_call(
        body, name=name, grid=(s // ROW_TILE,),
        in_specs=[_row_spec(D), _row_spec(D)] + vec_specs,
        out_specs=[_row_spec(D), _sum_spec(D)],
        out_shape=[jax.ShapeDtypeStruct((s, D), BF16), jax.ShapeDtypeStruct((8, D), F32)],
        compiler_params=_params("arbitrary"),
    )(dxo, y, *vecs)


def _modnorm_bwd(dh, x, dxo, g, scale, name):
    s = dh.shape[0]

    def body(dh_ref, x_ref, d_ref, g_ref, sc_ref, dx_ref, sum_ref):
        @pl.when(pl.program_id(0) == 0)
        def _():
            sum_ref[...] = jnp.zeros_like(sum_ref)

        dhv, xv = dh_ref[...], x_ref[...]
        r = _rstd(xv)
        n = xv * r
        one_sc = 1.0 + sc_ref[...]
        sum_ref[0:1, :] += jnp.sum(dhv, axis=0, keepdims=True)
        sum_ref[1:2, :] += jnp.sum(dhv * (n * g_ref[...]), axis=0, keepdims=True)
        sum_ref[2:3, :] += jnp.sum((dhv * one_sc) * n, axis=0, keepdims=True)
        dn = (dhv * one_sc) * g_ref[...]
        dx_ref[...] = d_ref[...] + r * (dn - n * jnp.mean(dn * n, axis=-1, keepdims=True))

    vecs, vec_specs = _vec_args(g, scale)
    return pl.pallas_call(
        body, name=name, grid=(s // ROW_TILE,),
        in_specs=[_row_spec(D), _row_spec(D), _row_spec(D)] + vec_specs,
        out_specs=[_row_spec(D), _sum_spec(D)],
        out_shape=[jax.ShapeDtypeStruct((s, D), F32), jax.ShapeDtypeStruct((8, D), F32)],
        compiler_params=_params("arbitrary"),
    )(dh, x, dxo, *vecs)


def _post_pre_fwd(x, y, g_post, gate, g_pre, shift, scale, name):
    s = x.shape[0]

    def body(x_ref, y_ref, gp_ref, gate_ref, g_ref, sh_ref, sc_ref, o_ref, h_ref):
        yv = y_ref[...]
        xo = x_ref[...] + gate_ref[...] * ((yv * _rstd(yv)) * gp_ref[...])
        o_ref[...] = xo
        h_ref[...] = (((xo * _rstd(xo)) * g_ref[...]) * (1.0 + sc_ref[...]) + sh_ref[...]).astype(BF16)

    vecs, vec_specs = _vec_args(g_post, gate, g_pre, shift, scale)
    return pl.pallas_call(
        body, name=name, grid=(s // ROW_TILE,),
        in_specs=[_row_spec(D), _row_spec(D)] + vec_specs, out_specs=[_row_spec(D), _row_spec(D)],
        out_shape=[jax.ShapeDtypeStruct((s, D), F32), jax.ShapeDtypeStruct((s, D), BF16)], compiler_params=_params("parallel"),
    )(x, y, *vecs)


def _pre_post_bwd(dh, x, dxo, g_pre, scale, y, g_post, gate, name):
    s = dh.shape[0]

    def body(dh_ref, x_ref, d_ref, y_ref, g_ref, sc_ref, gp_ref, gate_ref, dx_ref, dy_ref, sum_ref):
        @pl.when(pl.program_id(0) == 0)
        def _():
            sum_ref[...] = jnp.zeros_like(sum_ref)

        dhv, xv = dh_ref[...], x_ref[...]
        r = _rstd(xv)
        n = xv * r
        one_sc = 1.0 + sc_ref[...]
        sum_ref[0:1, :] += jnp.sum(dhv, axis=0, keepdims=True)
        sum_ref[1:2, :] += jnp.sum(dhv * (n * g_ref[...]), axis=0, keepdims=True)
        sum_ref[2:3, :] += jnp.sum((dhv * one_sc) * n, axis=0, keepdims=True)
        dn = (dhv * one_sc) * g_ref[...]
        dv = d_ref[...] + r * (dn - n * jnp.mean(dn * n, axis=-1, keepdims=True))
        dx_ref[...] = dv

        yv = y_ref[...]
        ry = _rstd(yv)
        ny = yv * ry
        sum_ref[3:4, :] += jnp.sum(dv * (ny * gp_ref[...]), axis=0, keepdims=True)
        sum_ref[4:5, :] += jnp.sum((dv * gate_ref[...]) * ny, axis=0, keepdims=True)
        dny = (dv * gate_ref[...]) * gp_ref[...]
        dy_ref[...] = (ry * (dny - ny * jnp.mean(dny * ny, axis=-1, keepdims=True))).astype(BF16)

    vecs, vec_specs = _vec_args(g_pre, scale, g_post, gate)
    return pl.pallas_call(
        body, name=name, grid=(s // ROW_TILE,),
        in_specs=[_row_spec(D)] * 4 + vec_specs,
        out_specs=[_row_spec(D), _row_spec(D), _sum_spec(D)],
        out_shape=[jax.ShapeDtypeStruct((s, D), F32), jax.ShapeDtypeStruct((s, D), BF16), jax.ShapeDtypeStruct((8, D), F32)],
        compiler_params=_params("arbitrary"),
    )(dh, x, dxo, y, *vecs)


def _loss_head(y, target):
    s = y.shape[0]

    def body(y_ref, t_ref, dy_ref, sum_ref):
        @pl.when(pl.program_id(0) == 0)
        def _():
            sum_ref[...] = jnp.zeros_like(sum_ref)

        err = y_ref[...] - t_ref[...]
        dy_ref[...] = err * (1.0 / D)
        sum_ref[...] += jnp.sum(err * err)

    return pl.pallas_call(
        body, name="loss_head", grid=(s // ROW_TILE,),
        in_specs=[_row_spec(D), _row_spec(D)],
        out_specs=[_row_spec(D), pl.BlockSpec((8, 128), lambda i: (0, 0))],
        out_shape=[jax.ShapeDtypeStruct((s, D), F32), jax.ShapeDtypeStruct((8, 128), F32)],
        compiler_params=_params("arbitrary"),
    )(y, target)


def _merge_bwd(dm, z, pa, pb, pc):
    s = z.shape[0]

    def body(dm_ref, g0_ref, g1_ref, g2_ref, pa_ref, pb_ref, pc_ref, dgl_ref, da_ref, db_ref, dc_ref):
        dmv = dm_ref[...]
        for i, (g_ref, p_ref, d_ref) in enumerate(((g0_ref, pa_ref, da_ref), (g1_ref, pb_ref, db_ref), (g2_ref, pc_ref, dc_ref))):
            gate = jax.nn.sigmoid(g_ref[...])
            dgl_ref[:, i * D:(i + 1) * D] = ((dmv * p_ref[...]) * (gate * (1.0 - gate))).astype(BF16)
            d_ref[...] = (dmv * gate).astype(BF16)

    return pl.pallas_call(
        body, name="merge_bwd", grid=(s // ROW_TILE,),
        in_specs=[_row_spec(D), _row_spec(D, 0), _row_spec(D, 1), _row_spec(D, 2), _row_spec(D), _row_spec(D), _row_spec(D)],
        out_specs=[_row_spec(3 * D), _row_spec(D), _row_spec(D), _row_spec(D)],
        out_shape=[jax.ShapeDtypeStruct((s, Z_COLS), BF16)] + [jax.ShapeDtypeStruct((s, D), BF16)] * 3,
        compiler_params=_params("parallel"),
    )(dm, z, z, z, pa, pb, pc)


def _shift_down(v, n):
    row = lax.broadcasted_iota(jnp.int32, v.shape, 0)
    return jnp.where(row >= n, pltpu.roll(v, n, axis=0), 0.0)


def _shift_up(v, n):
    s = v.shape[0]
    row = lax.broadcasted_iota(jnp.int32, v.shape, 0)
    return jnp.where(row < s - n, pltpu.roll(v, s - n, axis=0), 0.0)


def _log_sigmoid(v):
    return jnp.minimum(v, 0.0) - jnp.log1p(jnp.exp(-jnp.abs(v)))


def _cumf_fwd(fl, bias):
    s = fl.shape[0]

    def body(fl_ref, b_ref, o_ref):
        acc = _log_sigmoid(fl_ref[...] + b_ref[...])
        step = 1
        while step < s:
            acc = acc + _shift_down(acc, step)
            step *= 2
        o_ref[...] = acc

    return pl.pallas_call(body, name="cumf_fwd", out_shape=jax.ShapeDtypeStruct((s, 128), F32),
                          compiler_params=pltpu.CompilerParams(vmem_limit_bytes=V7X_VMEM_LIMIT))(fl, bias)


def _cumf_bwd(dcum, fl, bias):
    s = fl.shape[0]

    def body(d_ref, fl_ref, b_ref, dfl_ref, db_ref):
        acc = d_ref[...]
        step = 1
        while step < s:
            acc = acc + _shift_up(acc, step)
            step *= 2
        dfl = acc * jax.nn.sigmoid(-(fl_ref[...] + b_ref[...]))
        dfl_ref[...] = dfl.astype(BF16)
        db_ref[...] = jnp.broadcast_to(jnp.sum(dfl, axis=0, keepdims=True), (8, 128))

    return pl.pallas_call(
        body, name="cumf_bwd",
        out_shape=[jax.ShapeDtypeStruct((s, 128), BF16), jax.ShapeDtypeStruct((8, 128), F32)],
        compiler_params=pltpu.CompilerParams(vmem_limit_bytes=V7X_VMEM_LIMIT))(dcum, fl, bias)


def _pool_windows(v, shift):
    s2 = v + shift(v, 1)
    s4 = s2 + shift(s2, 2)
    s8 = s4 + shift(s4, 4)
    s16 = s8 + shift(s8, 8)
    group = lax.broadcasted_iota(jnp.int32, v.shape, 1) // 64
    return jnp.where(group == 0, s2, jnp.where(group == 1, s4, jnp.where(group == 2, s8, s16)))


def _pool_count(shape):
    group = lax.broadcasted_iota(jnp.int32, shape, 1) // 64
    window = jnp.where(group == 0, 2.0, jnp.where(group == 1, 4.0, jnp.where(group == 2, 8.0, 16.0)))
    t1 = (lax.broadcasted_iota(jnp.int32, shape, 0) + 1).astype(F32)
    return jnp.minimum(t1, window)


def _pc_specs(s):
    zcol = lambda blk: pl.BlockSpec((s, 256), lambda i, blk=blk: (0, blk))
    first = Z_PC // 256
    return [zcol(first), zcol(first + 1), zcol(first + 2), zcol(first + 3),
            pl.BlockSpec((256, 256), lambda i: (0, 0)), pl.BlockSpec((1, 256), lambda i: (0, 0)),
            pl.BlockSpec((3, 256), lambda i: (0, 0))]


def _poolconv_fwd(z, wbd, pscale, convw):
    s = z.shape[0]

    def body(pu_ref, ch_ref, cb_ref, cc_ref, w_ref, ps_ref, cw_ref, yb_ref, yc_ref):
        u = pu_ref[...]
        p = _pool_windows(u, _shift_down) / _pool_count(u.shape) - u
        yb = jnp.dot(p.astype(BF16), w_ref[...].astype(BF16), preferred_element_type=F32) * ps_ref[...]
        yb_ref[...] = yb.astype(BF16)
        uc = cc_ref[...] * ch_ref[...]
        cw = cw_ref[...]
        conv = cw[0:1, :] * _shift_down(uc, 2) + cw[1:2, :] * _shift_down(uc, 1) + cw[2:3, :] * uc
        yc_ref[...] = (cb_ref[...] * conv).astype(BF16)

    out = pl.BlockSpec((s, 256), lambda i: (0, 0))
    return pl.pallas_call(
        body, name="poolconv_fwd", grid=(1,), in_specs=_pc_specs(s), out_specs=[out, out],
        out_shape=[jax.ShapeDtypeStruct((s, 256), BF16)] * 2, compiler_params=_params("arbitrary"),
    )(z, z, z, z, wbd, pscale, convw)


def _poolconv_bwd(dyb, dyc, z, wbd, pscale, convw):
    s = z.shape[0]

    def body(dyb_ref, dyc_ref, pu_ref, ch_ref, cb_ref, cc_ref, w_ref, ps_ref, cw_ref, dz_ref, dw_ref, dps_ref, dcw_ref):
        u = pu_ref[...]
        count = _pool_count(u.shape)
        p = (_pool_windows(u, _shift_down) / count - u).astype(BF16)
        wb = w_ref[...].astype(BF16)
        dyb_v = dyb_ref[...]
        pw = jnp.dot(p, wb, preferred_element_type=F32)
        dps_ref[...] = jnp.broadcast_to(jnp.sum(dyb_v * pw, axis=0, keepdims=True), (8, 256))
        dys = (dyb_v * ps_ref[...]).astype(BF16)
        dp = lax.dot_general(dys, wb, (((1,), (1,)), ((), ())), preferred_element_type=F32)
        dw_ref[...] = lax.dot_general(p, dys, (((0,), (0,)), ((), ())), preferred_element_type=F32)
        dz_ref[:, 0:256] = (_pool_windows(dp / count, _shift_up) - dp).astype(BF16)

        ch, cb, cc = ch_ref[...], cb_ref[...], cc_ref[...]
        uc = cc * ch
        cw = cw_ref[...]
        u2, u1 = _shift_down(uc, 2), _shift_down(uc, 1)
        conv = cw[0:1, :] * u2 + cw[1:2, :] * u1 + cw[2:3, :] * uc
        dyc_v = dyc_ref[...]
        dconv = dyc_v * cb
        du = cw[0:1, :] * _shift_up(dconv, 2) + cw[1:2, :] * _shift_up(dconv, 1) + cw[2:3, :] * dconv
        dz_ref[:, 256:512] = (du * cc).astype(BF16)
        dz_ref[:, 512:768] = (dyc_v * conv).astype(BF16)
        dz_ref[:, 768:1024] = (du * ch).astype(BF16)
        dcw_ref[...] = jnp.zeros_like(dcw_ref)
        dcw_ref[0:1, :] = jnp.sum(dconv * u2, axis=0, keepdims=True)
        dcw_ref[1:2, :] = jnp.sum(dconv * u1, axis=0, keepdims=True)
        dcw_ref[2:3, :] = jnp.sum(dconv * uc, axis=0, keepdims=True)

    blk = lambda r, c: pl.BlockSpec((r, c), lambda i: (0, 0))
    return pl.pallas_call(
        body, name="poolconv_bwd", grid=(1,),
        in_specs=[blk(s, 256), blk(s, 256)] + _pc_specs(s),
        out_specs=[blk(s, 1024), blk(256, 256), blk(8, 256), blk(8, 256)],
        out_shape=[jax.ShapeDtypeStruct((s, 1024), BF16), jax.ShapeDtypeStruct((256, 256), F32),
                   jax.ShapeDtypeStruct((8, 256), F32), jax.ShapeDtypeStruct((8, 256), F32)],
        compiler_params=_params("arbitrary"),
    )(dyb, dyc, z, z, z, z, wbd, pscale, convw)


_NT = (((1,), (1,)), ((), ()))
_TN = (((0,), (0,)), ((), ()))


ATT_Q, ATT_K = 256, 256
ATT_HEADS_BWD = 8
ATT_HEADS = 8


def _att_logits(q, k, fr, q0, k0, masked):
    logits = lax.dot_general(q, k, _NT, preferred_element_type=F32) - fr
    if not masked:
        return logits
    row = q0 + lax.broadcasted_iota(jnp.int32, logits.shape, 0)
    col = k0 + lax.broadcasted_iota(jnp.int32, logits.shape, 1)
    return jnp.where(row >= col, logits, NEG_INF)


def _causal_sweep(step, qi, init):
    n_full = (qi * ATT_Q) // ATT_K
    carry = lax.fori_loop(0, n_full, lambda j, carry: step(j, carry, False), init)
    return step(n_full, carry, True)


HEAD_PAIRS = HEADS // 2


def _lane_pick(v, lane, idx):
    return jnp.sum(jnp.where(lane == idx, v, 0.0), axis=-1, keepdims=True)


def _lane_put(lane, idx, col):
    return jnp.where(lane == idx, col, 0.0)


def _split_heads(v, low):
    zero = jnp.zeros_like(v)
    return jnp.where(low, v, zero), jnp.where(low, zero, v)


def _attn_fwd(qkv, fr):
    s = qkv.shape[0]
    nk = s // ATT_K
    width = ATT_HEADS * HEAD_DIM
    groups = HEADS // ATT_HEADS

    def body(q_ref, k_ref, v_ref, fr_ref, o_ref, lse_ref):
        qi, grp = pl.program_id(0), pl.program_id(1)
        lane = lax.broadcasted_iota(jnp.int32, (ATT_Q, 128), 1)
        low = lane < HEAD_DIM
        qs = []
        for pr in range(ATT_HEADS // 2):
            qs += _split_heads(q_ref[:, 128 * pr:128 * (pr + 1)] * (HEAD_DIM ** -0.5), low)

        def step(j, carry, masked):
            k0 = pl.multiple_of(j * ATT_K, ATT_K)
            out = []
            for h in range(ATT_HEADS):
                cols = slice(128 * (h // 2), 128 * (h // 2 + 1))
                m, l, acc = carry[h]
                logits = _att_logits(qs[h], k_ref[pl.ds(k0, ATT_K), cols], fr_ref[h, pl.ds(j, 1), :], qi * ATT_Q, k0, masked)
                m_new = jnp.maximum(m, jnp.max(logits, axis=-1, keepdims=True))
                p = jnp.exp(logits - m_new)
                alpha = jnp.exp(m - m_new)
                l = alpha * l + jnp.sum(p, axis=-1, keepdims=True)
                acc = alpha * acc + jnp.dot(p.astype(BF16), v_ref[pl.ds(k0, ATT_K), cols], preferred_element_type=F32)
                out.append((m_new, l, acc))
            return tuple(out)

        one = (jnp.full((ATT_Q, 1), NEG_INF, F32), jnp.zeros((ATT_Q, 1), F32), jnp.zeros((ATT_Q, 128), F32))
        done = _causal_sweep(step, qi, (one,) * ATT_HEADS)

        @pl.when(grp == 0)
        def _():
            lse_ref[...] = jnp.zeros_like(lse_ref)

        lse = jnp.zeros((ATT_Q, 128), F32)
        for pr in range(ATT_HEADS // 2):
            (m0, l0, acc0), (m1, l1, acc1) = done[2 * pr], done[2 * pr + 1]
            o_ref[:, 128 * pr:128 * (pr + 1)] = jnp.where(low, acc0 / l0, acc1 / l1)
            head = ATT_HEADS * grp + 2 * pr
            lse = lse + _lane_put(lane, head, m0 + jnp.log(l0)) + _lane_put(lane, head + 1, m1 + jnp.log(l1))
        lse_ref[...] += lse

    return pl.pallas_call(
        body, name="attn_fwd", grid=(s // ATT_Q, groups),
        in_specs=[pl.BlockSpec((ATT_Q, width), lambda i, g: (i, g)),
                  pl.BlockSpec((s, width), lambda i, g: (0, groups + g)),
                  pl.BlockSpec((s, width), lambda i, g: (0, 2 * groups + g)),
                  pl.BlockSpec((ATT_HEADS, nk, ATT_K), lambda i, g: (g, 0, 0))],
        out_specs=[pl.BlockSpec((ATT_Q, width), lambda i, g: (i, g)), pl.BlockSpec((ATT_Q, 128), lambda i, g: (i, 0))],
        out_shape=[jax.ShapeDtypeStruct((s, A_WIDTH), F32), jax.ShapeDtypeStruct((s, 128), F32)],
        compiler_params=_params("parallel", "arbitrary"),
    )(qkv, qkv, qkv, fr)


def _attn_bwd(qkv, do, o, lse, fr):
    s = qkv.shape[0]
    nk = s // ATT_K
    scale = HEAD_DIM ** -0.5
    heads = ATT_HEADS_BWD
    width = heads * HEAD_DIM
    groups = HEADS // heads

    def body(q_ref, k_ref, v_ref, do_ref, o_ref, lse_ref, fr_ref, dq_ref, dk_ref, dv_ref, dfc_ref, dfr_ref, dk_acc, dv_acc):
        grp = pl.program_id(0)
        lane = lax.broadcasted_iota(jnp.int32, (ATT_Q, 128), 1)
        low = lane < HEAD_DIM
        low_t = lax.broadcasted_iota(jnp.int32, (128, ATT_Q), 0) < HEAD_DIM
        dk_acc[...] = jnp.zeros_like(dk_acc)
        dv_acc[...] = jnp.zeros_like(dv_acc)
        dfr_ref[...] = jnp.zeros_like(dfr_ref)

        @pl.when(grp == 0)
        def _():
            dfc_ref[...] = jnp.zeros_like(dfc_ref)

        def outer(i, carry):
            q0 = pl.multiple_of(i * ATT_Q, ATT_Q)
            rows = pl.ds(q0, ATT_Q)
            lsev = lse_ref[rows, :]
            qts, dots, qs, dos, deltas, lses = [], [], [], [], [], []
            for pr in range(heads // 2):
                pcols = slice(128 * pr, 128 * (pr + 1))
                q2, do2 = q_ref[rows, pcols] * scale, do_ref[rows, pcols]
                prod = do2 * o_ref[rows, pcols]
                deltas += [jnp.sum(jnp.where(low, prod, 0.0), axis=-1, keepdims=True),
                           jnp.sum(jnp.where(low, 0.0, prod), axis=-1, keepdims=True)]
                dob2 = do2.astype(BF16)
                qts += _split_heads(q2.astype(F32).T.astype(BF16), low_t)
                dots += _split_heads(do2.T.astype(BF16), low_t)
                qs += _split_heads(q2, low)
                dos += _split_heads(dob2, low)
                lses += [_lane_pick(lsev, lane, heads * grp + 2 * pr), _lane_pick(lsev, lane, heads * grp + 2 * pr + 1)]

            def inner(j, carry, masked):
                k0 = pl.multiple_of(j * ATT_K, ATT_K)
                krows = pl.ds(k0, ATT_K)
                out, dkt, dvt = [], [], []
                for h in range(heads):
                    pcols = slice(128 * (h // 2), 128 * (h // 2 + 1))
                    dq, dfc = carry[h]
                    k2 = k_ref[krows, pcols]
                    p = jnp.exp(_att_logits(qs[h], k2, fr_ref[h, pl.ds(j, 1), :], q0, k0, masked) - lses[h])
                    dp = lax.dot_general(dos[h], v_ref[krows, pcols], _NT, preferred_element_type=F32)
                    ds = p * (dp - deltas[h])
                    dsb = ds.astype(BF16)
                    dkt.append(jnp.dot(qts[h], dsb, preferred_element_type=F32))
                    dvt.append(jnp.dot(dots[h], p.astype(BF16), preferred_element_type=F32))
                    dfr_ref[h, pl.ds(j, 1), :] -= jnp.sum(ds, axis=0, keepdims=True)
                    out.append((dq + jnp.dot(dsb, k2, preferred_element_type=F32), dfc + (ds[:, :128] + ds[:, 128:])))
                for pr in range(heads // 2):
                    prows = slice(128 * pr, 128 * (pr + 1))
                    dk_acc[j, prows, :] += dkt[2 * pr] + dkt[2 * pr + 1]
                    dv_acc[j, prows, :] += dvt[2 * pr] + dvt[2 * pr + 1]
                return tuple(out)

            one = (jnp.zeros((ATT_Q, 128), F32), jnp.zeros((ATT_Q, 128), F32))
            done = _causal_sweep(inner, i, (one,) * heads)
            dfc = jnp.zeros((ATT_Q, 128), F32)
            for pr in range(heads // 2):
                (dq0, dfc0), (dq1, dfc1) = done[2 * pr], done[2 * pr + 1]
                dq_ref[rows, 128 * pr:128 * (pr + 1)] = (jnp.where(low, dq0, dq1) * scale).astype(BF16)
                head = heads * grp + 2 * pr
                dfc = (dfc + _lane_put(lane, head, jnp.sum(dfc0, axis=-1, keepdims=True))
                       + _lane_put(lane, head + 1, jnp.sum(dfc1, axis=-1, keepdims=True)))
            dfc_ref[rows, :] += dfc
            return carry

        lax.fori_loop(0, s // ATT_Q, outer, 0)
        for j in range(nk):
            for pr in range(heads // 2):
                prows, pcols = slice(128 * pr, 128 * (pr + 1)), slice(128 * pr, 128 * (pr + 1))
                dk_ref[ATT_K * j:ATT_K * (j + 1), pcols] = dk_acc[j, prows, :].T.astype(BF16)
                dv_ref[ATT_K * j:ATT_K * (j + 1), pcols] = dv_acc[j, prows, :].T.astype(BF16)

    part = lambda first: pl.BlockSpec((s, width), lambda g, first=first: (0, first + g))
    whole = pl.BlockSpec((s, 128), lambda g: (0, 0))
    rowv = pl.BlockSpec((heads, nk, ATT_K), lambda g: (g, 0, 0))
    return pl.pallas_call(
        body, name="attn_bwd", grid=(groups,),
        in_specs=[part(0), part(groups), part(2 * groups), part(0), part(0), whole, rowv],
        out_specs=[part(0), part(0), part(0), whole, rowv],
        out_shape=[jax.ShapeDtypeStruct((s, A_WIDTH), BF16)] * 3 + [jax.ShapeDtypeStruct((s, 128), F32), jax.ShapeDtypeStruct((HEADS, nk, ATT_K), F32)],
        scratch_shapes=[pltpu.VMEM((nk, width, ATT_K), F32), pltpu.VMEM((nk, width, ATT_K), F32)],
        compiler_params=_params("arbitrary"),
    )(qkv, qkv, qkv, do, o, lse, fr)


def _ada_fwd(c_all, w_ada, b_loc):
    depth, _, n = w_ada.shape
    tn = 512

    def body(c_ref, w_ref, b_ref, o_ref, sc_ref):
        cv = c_ref[...]
        sc = cv * jax.nn.sigmoid(cv)
        sc_ref[...] = sc
        o_ref[0] = jnp.dot(sc.astype(BF16), w_ref[0].astype(BF16), preferred_element_type=F32) + b_ref[0]

    return pl.pallas_call(
        body, name="ada_fwd", grid=(depth, n // tn),
        in_specs=[pl.BlockSpec((N_DEV, D), lambda l, j: (0, 0)), pl.BlockSpec((1, D, tn), lambda l, j: (l, 0, j)),
                  pl.BlockSpec((1, 1, tn), lambda l, j: (l, 0, j))],
        out_specs=[pl.BlockSpec((1, N_DEV, tn), lambda l, j: (l, 0, j)), pl.BlockSpec((N_DEV, D), lambda l, j: (0, 0))],
        out_shape=[jax.ShapeDtypeStruct((depth, N_DEV, n), F32), jax.ShapeDtypeStruct((N_DEV, D), F32)],
        compiler_params=_params("arbitrary", "arbitrary"),
    )(c_all, w_ada, b_loc)


def _sum_devices(gathered):
    n = gathered.shape[1]
    tn = _pick(n, (1408, 1024, 640, 512, 128))

    def body(g_ref, o_ref):
        acc = g_ref[0:8, :]
        for dev in range(1, N_DEV):
            acc = acc + g_ref[8 * dev:8 * dev + 8, :]
        o_ref[...] = acc

    return pl.pallas_call(
        body, name="sum_devices", grid=(n // tn,),
        in_specs=[pl.BlockSpec((8 * N_DEV, tn), lambda j: (0, j))], out_specs=pl.BlockSpec((8, tn), lambda j: (0, j)),
        out_shape=jax.ShapeDtypeStruct((8, n), F32), compiler_params=_params("parallel"),
    )(gathered)


def _place():
    x, y, c = lax.axis_index("x"), lax.axis_index("y"), lax.axis_index("c")
    chips = [(1 - x, y), (x, 1 - y), (1 - x, 1 - y)]
    return x, y, c, chips


def _allgather8(block, name, after=()):
    m_per, n = block.shape

    def body(x_ref, *rest):
        out_ref, send_sems, recv_sems, local_sem = rest[len(after):]
        x, y, c, chips = _place()
        me, sibling = (x, y, c), (x, y, 1 - c)

        def rows(px, py, pc):
            return out_ref.at[pl.ds((4 * px + 2 * py + pc) * m_per, m_per), :]

        def copy(k, blk, to, src=None):
            return pltpu.make_async_remote_copy(
                src_ref=rows(*blk) if src is None else src, dst_ref=rows(*blk),
                send_sem=send_sems.at[k], recv_sem=recv_sems.at[k], device_id=to, device_id_type=MESH)

        mine = pltpu.make_async_copy(x_ref, rows(*me), local_sem)
        mine.start()
        first = [copy(0, me, sibling, src=x_ref)]
        first += [copy(1 + j, me, (*chip, c), src=x_ref) for j, chip in enumerate(chips)]
        for cp in first:
            cp.start()
        passed = [copy(4 + j, (*chip, c), sibling) for j, chip in enumerate(chips)]
        for j, chip in enumerate(chips):
            copy(1 + j, (*chip, c), me).wait_recv()
            passed[j].start()
        copy(0, sibling, me).wait_recv()
        for j, chip in enumerate(chips):
            copy(4 + j, (*chip, 1 - c), me).wait_recv()
        for cp in first + passed:
            cp.wait_send()
        mine.wait()

    return pl.pallas_call(
        body, name=name, out_shape=jax.ShapeDtypeStruct((N_DEV * m_per, n), block.dtype),
        in_specs=[pl.BlockSpec(memory_space=pltpu.VMEM)] + [pl.BlockSpec(memory_space=pl.ANY)] * len(after),
        out_specs=pl.BlockSpec(memory_space=pltpu.VMEM),
        scratch_shapes=[pltpu.SemaphoreType.DMA((7,)), pltpu.SemaphoreType.DMA((7,)), pltpu.SemaphoreType.DMA],
        compiler_params=pltpu.CompilerParams(vmem_limit_bytes=V7X_VMEM_LIMIT),
    )(block, *after)


_SEM = pl.BlockSpec(memory_space=pltpu.SEMAPHORE)
_DATAFLOW = pltpu.SideEffectType.DATAFLOW_SIDE_EFFECTING


def _plan_copies(plan, refs, send_sems, recv_sems):
    return [pltpu.make_async_remote_copy(src_ref=src, dst_ref=dst, send_sem=send_sems.at[i], recv_sem=recv_sems.at[i],
                                         device_id=to, device_id_type=MESH) for i, (src, dst, to) in enumerate(plan(refs))]


class _Token(NamedTuple):
    after: jax.Array
    tie: jax.Array


def _after_operand(after):
    return after.after if isinstance(after, _Token) else after


def _copies_start(bufs, plan, n_copies, after, name):
    nb = len(bufs)

    def body(*refs):
        for cp in _plan_copies(plan, refs[:nb], refs[nb + 1], refs[nb + 2]):
            cp.start()
        for token in refs[-2:]:
            token[...] = jnp.zeros_like(token)

    sem = pltpu.SemaphoreType.DMA((n_copies,))
    vmem = pl.BlockSpec(memory_space=pltpu.VMEM)
    outs = pl.pallas_call(
        body, name=name,
        out_shape=(sem, sem, *[pltpu.HBM(b.shape, b.dtype) for b in bufs], jax.ShapeDtypeStruct((8, 128), F32),
                   jax.ShapeDtypeStruct((1, 1), F32)),
        in_specs=[_HBM] * nb + [pl.BlockSpec(memory_space=pl.ANY)],
        out_specs=(_SEM, _SEM, *[_HBM] * nb, vmem, vmem),
        input_output_aliases={i: 2 + i for i in range(nb)},
        compiler_params=pltpu.CompilerParams(has_side_effects=_DATAFLOW),
    )(*[pltpu.with_memory_space_constraint(b, pltpu.HBM) for b in bufs], _after_operand(after))
    return outs[0], outs[1], list(outs[2:2 + nb]), _Token(outs[-2], outs[-1])


def _copies_wait(started, plan, after, name):
    send_sems, recv_sems, bufs, _ = started
    nb = len(bufs)

    def body(*refs):
        for cp in _plan_copies(plan, refs[:nb], refs[nb], refs[nb + 1]):
            cp.wait_send()
            cp.wait_recv()

    return list(pl.pallas_call(
        body, name=name, out_shape=tuple(pltpu.HBM(b.shape, b.dtype) for b in bufs),
        in_specs=[_HBM] * nb + [_SEM, _SEM, pl.BlockSpec(memory_space=pl.ANY)], out_specs=tuple([_HBM] * nb),
        input_output_aliases={i: i for i in range(nb)},
        compiler_params=pltpu.CompilerParams(has_side_effects=_DATAFLOW),
    )(*bufs, send_sems, recv_sems, _after_operand(after)))


def _half_rows(ref, axis, c):
    half = ref.shape[axis] // 2
    return pl.ds(c * half, half)


def _plan_gather_ici(refs):
    n = len(refs) // 2
    x, y, c, chips = _place()
    out = []
    for a in range(n):
        rows = _half_rows(refs[a], 0, c)
        out += [(refs[a].at[rows], refs[n + a].at[2 * x + y, rows], (*chip, c)) for chip in chips]
        out.append((refs[a], refs[n + a].at[2 * x + y], (x, y, 1 - c)))
    return out


def _plan_gather_d2d(refs):
    x, y, c, chips = _place()
    out = []
    for ref in refs:
        rows = _half_rows(ref, 1, c)
        for px, py in chips:
            landed = ref.at[2 * px + py, rows]
            out.append((landed, landed, (x, y, 1 - c)))
    return out


def _plan_rs_sibling(refs):
    n = len(refs) // 2
    x, y, c, _ = _place()
    return [(refs[a].at[pl.ds(0, refs[a].shape[0]), _half_rows(refs[a], 1, 1 - c)], refs[n + a], (x, y, 1 - c)) for a in range(n)]


def _plan_rs_chips(refs):
    n = len(refs) // 2
    x, y, c, chips = _place()
    return [(refs[a].at[2 * px + py], refs[n + a].at[k], (px, py, c)) for a in range(n) for k, (px, py) in enumerate(chips)]


def _plan_rs_share(refs):
    x, y, c, _ = _place()
    return [(ref.at[_half_rows(ref, 0, c)], ref.at[_half_rows(ref, 0, c)], (x, y, 1 - c)) for ref in refs]


def _chip_sum(g, other, sel, name, blocked=True):
    nblk, half, cdim = other.shape
    tr = _pick(half, (512, 256, 128, 64) if nblk > 1 else (128, 64))
    per = half // tr

    def body(sel_ref, g_ref, t_ref, wire_ref, own_ref):
        total = g_ref[0] + t_ref[0]
        wire_ref[0] = total.astype(BF16)
        if blocked:
            @pl.when(pl.program_id(1) == sel_ref[1])
            def _():
                own_ref[...] = total
        else:
            own_ref[0] = total

    blk = pl.BlockSpec((1, tr, cdim), lambda i, p, sel_ref: (p, i, 0))
    own_spec = pl.BlockSpec((tr, cdim), lambda i, p, sel_ref: (i, 0)) if blocked else blk
    own_shape = jax.ShapeDtypeStruct((half, cdim) if blocked else other.shape, F32)
    return pl.pallas_call(
        body, name=name,
        grid_spec=pltpu.PrefetchScalarGridSpec(
            num_scalar_prefetch=1, grid=(per, nblk),
            in_specs=[pl.BlockSpec((1, tr, cdim), lambda i, p, sel_ref: (p, sel_ref[0] * per + i, 0)), blk],
            out_specs=[blk, own_spec]),
        out_shape=[jax.ShapeDtypeStruct(other.shape, BF16), own_shape],
        compiler_params=_params("parallel", "arbitrary"),
    )(sel, g, other)


def _final_sum(own, recv, sel, name):
    half, cdim = own.shape
    tr = _pick(half, (512, 256, 128, 64))
    per = half // tr

    def body(sel_ref, own_ref, r0_ref, r1_ref, r2_ref, o_ref):
        o_ref[...] = ((own_ref[...] + r0_ref[0].astype(F32)) + r1_ref[0].astype(F32)) + r2_ref[0].astype(F32)

    part = lambda k: pl.BlockSpec((1, tr, cdim), lambda i, sel_ref, k=k: (k, i, 0))
    return pl.pallas_call(
        body, name=name,
        grid_spec=pltpu.PrefetchScalarGridSpec(
            num_scalar_prefetch=1, grid=(per,),
            in_specs=[pl.BlockSpec((tr, cdim), lambda i, sel_ref: (i, 0)), part(0), part(1), part(2)],
            out_specs=pl.BlockSpec((tr, cdim), lambda i, sel_ref: (sel_ref[0] * per + i, 0))),
        out_shape=jax.ShapeDtypeStruct((2 * half, cdim), F32), compiler_params=_params("parallel"),
    )(sel, own, recv, recv, recv)


def _row(v):
    return v.reshape(1, -1)


_BR_WIDTHS = (A_WIDTH, POOL_WIDTH, CONV_WIDTH)


def _tie(v, token):
    return v if token is None else v + token.tie


def _no_hook(point, after, ready=None):
    return None


def _layer_fwd(x, w, mod, hook=_no_hook):
    s = x.shape[0]
    mod3 = mod.reshape(6, 1, D)
    h = _modnorm_fwd(x, _row(w["g_mix_pre"]), (mod3, 0), (mod3, 1), "mix_pre_fwd")
    hook("pre", h)
    z = _mm(h, w["w_all"], name="mm_in")
    qkv = z[:, Z_QKV:Z_PC].astype(BF16)
    fl = z[:, Z_FL:Z_COLS]
    cum = _cumf_fwd(fl, w["b_f_pad"])
    fr = cum[:, :HEADS].T.reshape(HEADS, s // ATT_K, ATT_K)
    br_a, lse = _attn_fwd(qkv, fr)
    br_b, br_c = _poolconv_fwd(z, w["w_pool_bd"], _tie(_row(w["pool_scale"]), hook("attn", lse)), w["conv_w"])
    hook("pool", br_b)
    pa, pb, pc, merged = _stacked_proj((br_a, br_b, br_c), w["w_branch"], _BR_WIDTHS, False, "mm_br_merge", gates=z)
    y = _mm(merged, w["w_out"], name="mm_out")
    x1, h2 = _post_pre_fwd(x, y, _row(w["g_mix_post"]), (mod3, 2), _row(w["g_ff_pre"]), (mod3, 3), (mod3, 4), "mix_post_ff_pre_fwd")
    a, r = _mm(h2, w["w_ff1"], b_split=N_CHIPS, epilogue=_relu2_fwd, out_dtype=(F32, BF16), name="mm_ff1")
    y2 = _mm(r, w["w_ff2"], name="mm_ff2")
    x2 = _post_fwd(x1, y2, _tie(_row(w["g_ff_post"]), hook("ff_post", y2)), (mod3, 5), "ff_post_fwd")
    hook("end", x2)
    saved = dict(x=x, h=h, z=z, qkv=qkv, fl=fl, fr=fr, lse=lse, br_a=br_a, br_b=br_b, br_c=br_c, pa=pa, pb=pb, pc=pc,
                 merged=merged, y=y, x1=x1, h2=h2, a=a, r=r, y2=y2)
    return x2, saved


def _layer_bwd(dx2, sv, w, mod, hook=_no_hook):
    s = dx2.shape[0]
    mod3 = mod.reshape(6, 1, D)
    dy2, sum_ff_post = _post_bwd(dx2, sv["y2"], _row(w["g_ff_post"]), (mod3, 5), "ff_post_bwd")
    (da,) = _mm(dy2, w["w_ff2"], tb=True, epilogue=_relu2_bwd, extras=(sv["a"],), out_dtype=(BF16,), name="mm_ff2_dx")
    d_w_ff2 = _mm(sv["r"], dy2, ta=True, name="mm_ff2_dw")
    dh2 = _mm(da, w["w_ff1"], tb=True, b_split=N_CHIPS, name="mm_ff1_dx")
    d_w_ff1 = _mm(sv["h2"], da, ta=True, out_split=N_CHIPS, name="mm_ff1_dw")
    g_ff_pre = _tie(_row(w["g_ff_pre"]), hook("ff_pre", dh2, dict(w_ff1=d_w_ff1, w_ff2=d_w_ff2)))
    dx1, dy, sum_mid = _pre_post_bwd(dh2, sv["x1"], dx2, g_ff_pre, (mod3, 4), sv["y"], _row(w["g_mix_post"]), (mod3, 2), "ff_pre_mix_post_bwd")
    sum_ff_pre, sum_mix_post = sum_mid, sum_mid[3:]
    dmerged = _mm(dy, w["w_out"], tb=True, name="mm_out_dx")
    d_w_out = _mm(sv["merged"], dy, ta=True, name="mm_out_dw")
    dz, dpa, dpb, dpc = _merge_bwd(dmerged, sv["z"], sv["pa"], sv["pb"], sv["pc"])
    dbr_a, dbr_b, dbr_c = _stacked_proj((dpa, dpb, dpc), w["w_branch"], _BR_WIDTHS, True, "mm_br_dx")
    d_w_branch = _stacked_dw((sv["br_a"], sv["br_b"], sv["br_c"]), (dpa, dpb, dpc), "mm_br_dw")

    dq, dk, dv, dfc, dfr = _attn_bwd(sv["qkv"], dbr_a, sv["br_a"], sv["lse"], sv["fr"])
    dcum = dfc + jnp.pad(dfr.reshape(HEADS, s).T, ((0, 0), (0, 128 - HEADS)))
    dfl, sum_bf = _cumf_bwd(dcum, sv["fl"], _tie(w["b_f_pad"], hook("cumf", dfc)))
    dpc_z, d_wbd, sum_ps, sum_cw = _poolconv_bwd(dbr_b, dbr_c, sv["z"], w["w_pool_bd"], _row(w["pool_scale"]), w["conv_w"])
    for at, part in ((Z_QKV, dq), (Z_QKV + A_WIDTH, dk), (Z_QKV + 2 * A_WIDTH, dv), (Z_PC, dpc_z), (Z_FL, dfl)):
        dz = lax.dynamic_update_slice(dz, part, (0, at))
    dh = _mm(dz, w["w_all"], tb=True, name="mm_in_dx")
    d_w_all = _mm(sv["h"], dz, ta=True, name="mm_in_dw")
    hook("mix_pre", dh)
    dx, sum_mix_pre = _modnorm_bwd(dh, sv["x"], dx1, _row(w["g_mix_pre"]), (mod3, 1), "mix_pre_bwd")

    dmod = jnp.stack([sum_mix_pre[0], sum_mix_pre[1], sum_mix_post[0], sum_ff_pre[0], sum_ff_pre[1], sum_ff_post[0]])
    d_w_in = d_w_all[None]
    d_w_pool = jnp.stack([d_wbd[64 * g:64 * g + 64, 64 * g:64 * g + 64] for g in range(4)])
    big = dict(w_in=d_w_in, w_branch=d_w_branch, w_out=d_w_out, w_ff1=d_w_ff1, w_ff2=d_w_ff2)
    small = dict(g_mix_pre=sum_mix_pre[2], g_mix_post=sum_mix_post[1], g_ff_pre=sum_ff_pre[2], g_ff_post=sum_ff_post[1],
                 b_f=sum_bf[0, :HEADS], w_pool=d_w_pool, pool_scale=sum_ps[0], conv_w=sum_cw[0:3])
    return dx, dmod, big, small


_QKV_END, _FL_END, _PC_END = 3 * A_WIDTH, 3 * A_WIDTH + HEADS, 3 * A_WIDTH + HEADS + POOL_WIDTH + 3 * CONV_WIDTH
_W_IN_GROUPS = ((_PC_END, IN_COLS, Z_GL), (0, _QKV_END, Z_QKV), (_FL_END, _PC_END, Z_PC), (_QKV_END, _FL_END, Z_FL))
_SHARD_COLS = IN_COLS // N_CHIPS


def _w_in_layout():
    out = []
    for p in range(N_CHIPS):
        pieces = []
        for lo, hi, at in _W_IN_GROUPS:
            a, b = max(lo, p * _SHARD_COLS), min(hi, (p + 1) * _SHARD_COLS)
            if a < b:
                pieces.append((at + a - lo, at + b - lo, a - p * _SHARD_COLS))
        pieces.sort()
        segs = []
        for z0, z1, _ in pieces:
            s, e = z0 // 128 * 128, -(-z1 // 128) * 128
            if segs and s <= segs[-1][1]:
                segs[-1] = (segs[-1][0], max(e, segs[-1][1]))
            else:
                segs.append((s, e))
        assert sum(e - s for s, e in segs) == Z_WINDOW
        out.append((pieces, segs))
    return out


Z_WINDOW = 1536


def _w_in_window(shard, p):
    pieces, segs = _w_in_layout()[p]
    cols = []
    for s, e in segs:
        at = s
        for z0, z1, src in pieces:
            if s <= z0 < e:
                if z0 > at:
                    cols.append(jnp.zeros((shard.shape[0], z0 - at), shard.dtype))
                cols.append(shard[:, src:src + z1 - z0])
                at = z1
        if e > at:
            cols.append(jnp.zeros((shard.shape[0], e - at), shard.dtype))
    return jnp.concatenate(cols, axis=1)


def _own_window(shard, chip):
    return lax.switch(chip, [lambda t, p=p: _w_in_window(t, p) for p in range(N_CHIPS)], shard)


def _w_all_from_windows(blocks):
    layout = _w_in_layout()
    bounds = sorted({edge for _, segs in layout for seg in segs for edge in seg})
    parts = []
    for lo, hi in zip(bounds[:-1], bounds[1:]):
        covering = []
        for p, (_, segs) in enumerate(layout):
            at = 0
            for s, e in segs:
                if s <= lo and hi <= e:
                    covering.append(blocks[p][:, at + lo - s:at + hi - s])
                at += e - s
        assert covering
        parts.append(covering[0] if len(covering) == 1 else covering[0] + covering[1])
    return jnp.concatenate(parts, axis=1)


def _w_in_shard(d_w_all, p):
    pieces = []
    for lo, hi, at in sorted(_W_IN_GROUPS):
        a, b = max(lo, p * _SHARD_COLS), min(hi, (p + 1) * _SHARD_COLS)
        if a < b:
            pieces.append(d_w_all[:, at + a - lo:at + b - lo])
    return jnp.concatenate(pieces, axis=1)


def _w_in_shards(d_w_all):
    return jnp.stack([_w_in_shard(d_w_all, p) for p in range(N_CHIPS)])


def _full_layer_weights(w_in_blocks, w_branch, w_out, w_ff1, w_ff2, g_mix_pre, g_mix_post, g_ff_pre, g_ff_post, b_f, w_pool, pool_scale, conv_w):
    w_all = None if w_in_blocks is None else _w_all_from_windows(w_in_blocks)
    wbd = (w_pool[:, :, None, :] * jnp.eye(4, dtype=F32)[:, None, :, None]).reshape(POOL_WIDTH, POOL_WIDTH)
    return dict(w_all=w_all, w_branch=w_branch, w_out=w_out, w_ff1=w_ff1, w_ff2=w_ff2, g_mix_pre=g_mix_pre, g_mix_post=g_mix_post,
                g_ff_pre=g_ff_pre, g_ff_post=g_ff_post, b_f_pad=jnp.pad(b_f, (0, 128 - HEADS)).reshape(1, 128), w_pool_bd=wbd,
                pool_scale=pool_scale, conv_w=conv_w)


class _NoComm:
    def layer_weights(self, l):
        raise NotImplementedError

    def fwd_hook(self, l):
        return _no_hook

    def bwd_hook(self, l):
        return _no_hook

    def grads_ready(self, l, big):
        return None


class _Layers(_NoComm):
    def __init__(self, layers):
        self.layers = layers

    def layer_weights(self, l):
        return self.layers[l]


def _local_step(x, target, mods, comm):
    saved, weights = [], []
    act = x
    for l in range(DEPTH):
        weights.append(comm.layer_weights(l))
        act, sv = _layer_fwd(act, weights[l], mods[l], comm.fwd_hook(l))
        saved.append(sv)
    dact, sq = _loss_head(act, target)
    loss = sq[0, 0] * (0.5 / D)
    dmods, bigs, smalls = [None] * DEPTH, [None] * DEPTH, [None] * DEPTH
    token = None
    for l in reversed(range(DEPTH)):
        dact, dmods[l], bigs[l], smalls[l] = _layer_bwd(dact, saved[l], weights[l], _tie(mods[l], token), comm.bwd_hook(l))
        token = comm.grads_ready(l, bigs[l])
    return loss, dact, jnp.stack(dmods), bigs, smalls


_BIG = ("w_in", "w_branch", "w_out", "w_ff1", "w_ff2")


class _GatherJob:
    def __init__(self, tag, shards, after):
        self.tag, self.n = tag, len(shards)
        lands = [lax.empty((N_CHIPS,) + s.shape, s.dtype) for s in shards]
        self.state = _copies_start(list(shards) + lands, _plan_gather_ici, 4 * self.n, after, "gather_ici_start_" + tag)
        self.token = self.state[3]

    def pass_on(self, after):
        bufs = _copies_wait(self.state, _plan_gather_ici, after, "gather_ici_wait_" + self.tag)
        self.state = _copies_start(bufs[self.n:], _plan_gather_d2d, 3 * self.n, bufs[0], "gather_d2d_start_" + self.tag)
        self.token = self.state[3]
        return self.token

    def done(self, after):
        return _copies_wait(self.state, _plan_gather_d2d, after, "gather_d2d_wait_" + self.tag)


class _ReduceJob:
    def __init__(self, tag, names, grads, sel, after):
        self.tag, self.names, self.n, self.sel = tag, names, len(names), sel
        lands = [lax.empty((g.shape[0], g.shape[1] // 2, g.shape[2]), F32) for g in grads]
        self.state = _copies_start(list(grads) + lands, _plan_rs_sibling, self.n, after, "rs_sibling_start_" + tag)
        self.token = self.state[3]

    def _chip_sum(self, name, g, other):
        if g.shape[0] == N_CHIPS:
            return _chip_sum(g, other, self.sel, "rs_chip_sum_" + name)
        wire, total = _chip_sum(g, other, self.sel, "rs_chip_sum_" + name, blocked=False)
        own = lax.switch(self.sel[1], [lambda t, p=p: _w_in_shard(t, p) for p in range(N_CHIPS)], total[0])
        return _w_in_shards(wire[0]), own

    def chip_sums(self, after):
        bufs = _copies_wait(self.state, _plan_rs_sibling, after, "rs_sibling_wait_" + self.tag)
        wires, self.owns = zip(*[self._chip_sum(name, bufs[i], bufs[self.n + i]) for i, name in enumerate(self.names)])
        lands = [lax.empty((3,) + w.shape[1:], BF16) for w in wires]
        self.state = _copies_start(list(wires) + lands, _plan_rs_chips, 3 * self.n, self.owns[0], "rs_chips_start_" + self.tag)
        self.token = self.state[3]
        return self.token

    def final_sums(self, after):
        bufs = _copies_wait(self.state, _plan_rs_chips, after, "rs_chips_wait_" + self.tag)
        sums = [_final_sum(self.owns[i], bufs[self.n + i], self.sel, "rs_final_" + name) for i, name in enumerate(self.names)]
        self.state = _copies_start(sums, _plan_rs_share, self.n, sums[0], "rs_share_start_" + self.tag)
        self.token = self.state[3]
        return self.token

    def done(self, after):
        return dict(zip(self.names, _copies_wait(self.state, _plan_rs_share, after, "rs_share_wait_" + self.tag)))


def _chip_blocks(g):
    return g if g.ndim == 3 else g.reshape(N_CHIPS, -1, g.shape[1])


class _StepComm(_NoComm):
    def __init__(self, big_weights, w_in0, sel, after):
        self.sel = sel
        self.small, self.grads, self.jobs = None, [dict() for _ in range(DEPTH)], {}
        self.jobs["in0"] = _GatherJob("in0", [w_in0], after)
        later = lax.optimization_barrier((tuple(big_weights), self.jobs["in0"].token.after))[0]
        self.jobs["rest0"] = _GatherJob("rest0", [w[0].astype(BF16) for w in later[1:]], self.jobs["in0"].token)
        layer1 = [w[1].astype(BF16) for w in later]
        self.jobs["all1"] = _GatherJob("all1", [_own_window(layer1[0], sel[1])] + layer1[1:], self.jobs["rest0"].token)

    def layer_weights(self, l):
        if l == 0:
            self.weights0 = _full_layer_weights(None, None, None, None, None, *self.small[0])
            return self.weights0
        g_in, g_br, g_out, g_f1, g_f2 = self.landed1
        return _full_layer_weights(g_in, g_br.reshape(D, D), g_out.reshape(D, D), g_f1, g_f2.reshape(D_FF, D), *self.small[1])

    def fwd_hook(self, l):
        if l != 0:
            return _no_hook

        def hook(point, after, ready=None):
            if point == "pre":
                job = self.jobs["in0"]
                started = after[:8, :128].astype(F32) + self.jobs["all1"].token.after
                self.weights0["w_all"] = _w_all_from_windows(job.done(job.pass_on(started))[0])
            if point == "attn":
                return self.jobs["rest0"].pass_on(after)
            if point == "ff_post":
                return self.jobs["all1"].pass_on(after)
            if point == "pool":
                g_br, g_out, g_f1, g_f2 = self.jobs["rest0"].done(after)
                self.weights0.update(w_branch=g_br.reshape(D, D), w_out=g_out.reshape(D, D), w_ff1=g_f1, w_ff2=g_f2.reshape(D_FF, D))
            if point == "end":
                self.landed1 = self.jobs["all1"].done(after)
            return None
        return hook

    def bwd_hook(self, l):
        if l != 0:
            return _no_hook

        def hook(point, after, ready=None):
            jobs = self.jobs
            if point == "ff_pre":
                token = jobs["rs1"].chip_sums(after)
                jobs["rs0_ff"] = _ReduceJob("0_ff", ("w_ff1", "w_ff2"), [_chip_blocks(ready[n]) for n in ("w_ff1", "w_ff2")], self.sel, token)
                return jobs["rs0_ff"].token
            if point == "cumf":
                return jobs["rs0_ff"].chip_sums(jobs["rs1"].final_sums(after))
            self.grads[1] = jobs["rs1"].done(after)
            return None
        return hook

    def grads_ready(self, l, big):
        if l == 1:
            self.jobs["rs1"] = _ReduceJob("1", _BIG, [_chip_blocks(big[n]) for n in _BIG], self.sel, self.sel)
            return self.jobs["rs1"].token
        names = ("w_in", "w_branch", "w_out")
        self.jobs["rs0_mix"] = _ReduceJob("0_mix", names, [_chip_blocks(big[n]) for n in names], self.sel, self.sel)
        return self.jobs["rs0_mix"].token

    def finish_sums(self, after):
        jobs = self.jobs
        token = jobs["rs0_mix"].chip_sums(after)
        return jobs["rs0_ff"].final_sums(token)

    def finish_ff(self, after):
        self.grads[0].update(self.jobs["rs0_ff"].done(after))

    def finish_mix(self, after):
        job = self.jobs["rs0_mix"]
        self.grads[0].update(job.done(job.final_sums(after)))


_SMALL = ("g_mix_pre", "g_mix_post", "g_ff_pre", "g_ff_post", "b_f", "w_pool", "pool_scale", "conv_w")


def _w_in_view(t):
    return t.reshape(DEPTH, D // 128, 128, _SHARD_COLS).transpose(3, 1, 0, 2).reshape(_SHARD_COLS * (D // 128) * DEPTH, 128)


def _w_in_unview(t):
    return t.reshape(_SHARD_COLS, D // 128, DEPTH, 128).transpose(2, 1, 3, 0).reshape(DEPTH, D, _SHARD_COLS)


def _pack(parts, rows=8):
    flat = jnp.concatenate([p.reshape(-1) for p in parts])
    width = -(-flat.shape[0] // (rows * 128)) * 128
    return jnp.pad(flat, (0, rows * width - flat.shape[0])).reshape(rows, width)


def _unpack(packed, like):
    flat = packed.reshape(-1)
    out, at = [], 0
    for ref in like:
        out.append(flat[at:at + ref.size].reshape(ref.shape))
        at += ref.size
    return out


def kernel(x, c, w_ada, b_ada, g_mix_pre, g_mix_post, g_ff_pre, g_ff_post, w_in, b_f, w_pool, pool_scale, conv_w, w_branch, w_out, w_ff1, w_ff2, loss_target, m_w_ada, m_b_ada, m_g_mix_pre, m_g_mix_post, m_g_ff_pre, m_g_ff_post, m_w_in, m_b_f, m_w_pool, m_pool_scale, m_conv_w, m_w_branch, m_w_out, m_w_ff1, m_w_ff2, v_w_ada, v_b_ada, v_g_mix_pre, v_g_mix_post, v_g_ff_pre, v_g_ff_post, v_w_in, v_b_f, v_w_pool, v_pool_scale, v_conv_w, v_w_branch, v_w_out, v_w_ff1, v_w_ff2):
    xi, yi, ci = lax.axis_index("x"), lax.axis_index("y"), lax.axis_index("c")
    chip = 2 * xi + yi
    dev = 2 * chip + ci
    n_ada = w_ada.shape[2]

    first = jnp.zeros((8, D + 384), F32).at[0, :D].set(c[0]).at[0, D:].set(conv_w.reshape(-1))
    w_in0 = _own_window(w_in[0].astype(BF16), chip)
    got = _allgather8(first, "gather_cond", after=(w_in0,)).reshape(N_DEV, 8, D + 384)[:, 0]
    c_all = got[:, :D]
    conv_full = got[0::2, D:].reshape(N_CHIPS, DEPTH, 3, CONV_WIDTH // N_CHIPS).transpose(1, 2, 0, 3).reshape(DEPTH, 3, CONV_WIDTH)

    b_loc = lax.dynamic_slice_in_dim(b_ada, chip * n_ada, n_ada, axis=1).reshape(DEPTH, 1, n_ada)
    mod_cols, silu_c = _ada_fwd(c_all, w_ada, b_loc)
    got = _allgather8(mod_cols.reshape(DEPTH * N_DEV, n_ada), "gather_mod").reshape(N_DEV, DEPTH, N_DEV, n_ada)[0::2]
    mod_all = got.transpose(1, 2, 0, 3).reshape(DEPTH, N_DEV, 6, D)
    mods = lax.dynamic_index_in_dim(mod_all, dev, axis=1, keepdims=False)

    comm = _StepComm((w_in, w_branch, w_out, w_ff1, w_ff2), w_in0, jnp.stack([ci, chip]).astype(jnp.int32), mods)
    comm.small = [(g_mix_pre[l], g_mix_post[l], g_ff_pre[l], g_ff_post[l], b_f[l], w_pool[l], pool_scale[l], conv_full[l]) for l in range(DEPTH)]
    loss_part, grad_x, dmods, bigs, smalls = _local_step(x[0], loss_target[0], mods, comm)

    small_parts = [smalls[l][name] for name in _SMALL for l in range(DEPTH)] + [loss_part.reshape(1)]
    packed = _tie(_pack([dmods] + small_parts), comm.jobs["rs0_mix"].token)
    gathered = _allgather8(packed, "gather_small")
    dmod_all = gathered.reshape(N_DEV, -1)[:, :dmods.size].reshape(N_DEV, DEPTH, 6 * D)
    summed = _unpack(_sum_devices(gathered), [dmods] + small_parts)
    grad_b_ada = summed[0].reshape(DEPTH, 6 * D)
    loss = summed[-1][0]
    small_grads = {name: jnp.stack(summed[1 + 2 * i:3 + 2 * i]) for i, name in enumerate(_SMALL)}
    small_grads["conv_w"] = lax.dynamic_slice_in_dim(small_grads["conv_w"], chip * (CONV_WIDTH // N_CHIPS), CONV_WIDTH // N_CHIPS, axis=2)

    dmod_loc = lax.dynamic_slice_in_dim(dmod_all.transpose(1, 0, 2), chip * n_ada, n_ada, axis=2)
    tail_token = comm.finish_sums(grad_b_ada)
    silu_pad = _tie(jnp.pad(silu_c, ((0, 128 - N_DEV), (0, 0))), tail_token)
    dmod_pad = jnp.pad(dmod_loc.transpose(1, 0, 2).reshape(N_DEV, DEPTH * n_ada), ((0, 128 - N_DEV), (0, 0)))
    grad_w_ada = _mm(silu_pad, dmod_pad, ta=True, out_split=DEPTH, name="mm_ada_dw")

    grads = dict(w_ada=grad_w_ada, b_ada=grad_b_ada, **small_grads)
    weights = dict(w_ada=w_ada, b_ada=b_ada, g_mix_pre=g_mix_pre, g_mix_post=g_mix_post, g_ff_pre=g_ff_pre, g_ff_post=g_ff_post, w_in=w_in,
                   b_f=b_f, w_pool=w_pool, pool_scale=pool_scale, conv_w=conv_w, w_branch=w_branch, w_out=w_out, w_ff1=w_ff1, w_ff2=w_ff2)
    m_in = dict(w_ada=m_w_ada, b_ada=m_b_ada, g_mix_pre=m_g_mix_pre, g_mix_post=m_g_mix_post, g_ff_pre=m_g_ff_pre, g_ff_post=m_g_ff_post,
                w_in=m_w_in, b_f=m_b_f, w_pool=m_w_pool, pool_scale=m_pool_scale, conv_w=m_conv_w, w_branch=m_w_branch, w_out=m_w_out,
                w_ff1=m_w_ff1, w_ff2=m_w_ff2)
    v_in = dict(w_ada=v_w_ada, b_ada=v_b_ada, g_mix_pre=v_g_mix_pre, g_mix_post=v_g_mix_post, g_ff_pre=v_g_ff_pre, g_ff_post=v_g_ff_post,
                w_in=v_w_in, b_f=v_b_f, w_pool=v_w_pool, pool_scale=v_pool_scale, conv_w=v_conv_w, w_branch=v_w_branch, w_out=v_w_out,
                w_ff1=v_w_ff1, w_ff2=v_w_ff2)
    order = ("w_ada", "b_ada", "g_mix_pre", "g_mix_post", "g_ff_pre", "g_ff_post", "w_in", "b_f", "w_pool", "pool_scale", "conv_w",
             "w_branch", "w_out", "w_ff1", "w_ff2")
    delta, new_m, new_v = {}, {}, {}
    tiny = ("b_ada",) + _SMALL
    tiny_g = [_tie(grads[tiny[0]], tail_token)] + [grads[name] for name in tiny[1:]]
    res = _adamw_many([weights[name] for name in tiny], tiny_g, [m_in[name] for name in tiny], [v_in[name] for name in tiny], "adamw_small")
    for out, vals in zip((delta, new_m, new_v), res):
        out.update(zip(tiny, vals))
    delta["w_ada"], new_m["w_ada"], new_v["w_ada"] = _adamw(w_ada, grad_w_ada, m_w_ada, v_w_ada, "adamw_w_ada")
    comm.finish_ff(delta["w_ada"][0, :8, :128] + delta["b_ada"][0, :128])
    for name in ("w_ff1", "w_ff2", "w_in", "w_branch", "w_out"):
        if name == "w_in":
            comm.finish_mix(delta["w_ff2"][0, :8, :128])
        g_layers = [comm.grads[l][name] for l in range(DEPTH)]
        if name == "w_in":
            g_view = lax.optimization_barrier(_w_in_view(jnp.stack(g_layers)))
            res = _adamw(_w_in_view(w_in), g_view, _w_in_view(m_w_in), _w_in_view(v_w_in), "adamw_w_in")
            grads[name], delta[name], new_m[name], new_v[name] = [_w_in_unview(t) for t in (g_view, *res)]
        else:
            delta[name], new_m[name], new_v[name], grads[name] = _adamw_layers(weights[name], g_layers, m_in[name], v_in[name], "adamw_" + name)

    return (loss, grad_x[None], *[grads[n] for n in order], *[delta[n] for n in order], *[new_m[n] for n in order],
            *[new_v[n] for n in order])
```

```python
from typing import NamedTuple

import jax
import jax.numpy as jnp
from jax import lax
from jax.experimental import pallas as pl
from jax.experimental.pallas import tpu as pltpu

F32 = jnp.float32
BF16 = jnp.bfloat16
MESH = pl.DeviceIdType.MESH

D = 1024
DEPTH = 2
HEADS = 8
HEAD_DIM = 64
A_WIDTH = 512
POOL_WIDTH = 256
CONV_WIDTH = 256
D_FF = 4096
IN_COLS = 5640
Z_GL, Z_QKV, Z_PC, Z_FL, Z_COLS = 0, 3072, 4608, 5632, 5760
RMS_EPS = 1e-6
NEG_INF = -1e30
ROW_TILE = 512
EW_ROWS = 256
N_CHIPS = 4
N_DEV = 8
V7X_VMEM_LIMIT = 48 * 1024 * 1024

ADAM_LR = 0.001
ADAM_B1 = 0.9
ADAM_B2 = 0.999
ADAM_EPS = 1e-08
ADAM_WD = 0.01
ADAM_STEP = 10

_HBM = pl.BlockSpec(memory_space=pltpu.HBM)


def _params(*sem):
    return pltpu.CompilerParams(dimension_semantics=sem, vmem_limit_bytes=V7X_VMEM_LIMIT)


def _pick(dim, cands):
    for cand in cands:
        if dim % cand == 0:
            return cand
    return dim


MM_TILE_BUDGET = 39 * 1024 * 1024


def _mm_tiles(m, n, k, k_unit, tn, a_size, b_size, out_size):
    for tk in (k_unit, 2048, 1152, 1024, 640, 512, 256, 128):
        if k_unit % tk:
            continue
        for tm in (2048, 1024, 512, 256, 128):
            if m % tm or ((m // tm) * (n // tn) < 2 and tm > 512):
                continue
            need = 2 * (tm * tk * a_size + tk * tn * b_size + tm * tn * out_size) + (0 if tk == k else 4 * tm * tn)
            if need <= MM_TILE_BUDGET and (tk == k_unit or tm >= 512):
                return tm, tk
    return 128, 128


def _mm(a, b, *, ta=False, tb=False, b_split=1, out_split=1, out_dtype=F32, epilogue=None, extras=(), name):
    (k, m) = a.shape if ta else a.shape[::-1]
    b_rows = b.shape[-2]
    b_cols = b.shape[-1] * b_split
    (n, k2) = (b_rows, b_cols) if tb else (b_cols, b_rows)
    assert k == k2, (a.shape, b.shape, ta, tb)
    n_unit = n // (out_split * (1 if tb else b_split))
    k_unit = k // (b_split if tb else 1)
    tn = _pick(n_unit, (1024, 1152, 768, 640, 512, 256, 128))
    tm, tk = _mm_tiles(m, n, k, k_unit, tn, a.dtype.itemsize, b.dtype.itemsize,
                       sum(jnp.dtype(dt).itemsize for dt in out_dtype) + 4 * len(extras) if epilogue else jnp.dtype(out_dtype).itemsize)
    nk = k // tk
    dims = (((0 if ta else 1,), (1 if tb else 0,)), ((), ()))

    def dot(a_ref, b_ref):
        b_val = b_ref[0] if b_split > 1 else b_ref[...]
        return lax.dot_general(a_ref[...].astype(BF16), b_val.astype(BF16), dims, preferred_element_type=F32)

    n_extra = len(extras)
    assert epilogue is None or out_split == 1

    def put(refs, val):
        if epilogue is not None:
            for o_ref, res in zip(refs[n_extra:], epilogue(val, *[r[...] for r in refs[:n_extra]])):
                o_ref[...] = res.astype(o_ref.dtype)
        elif out_split > 1:
            refs[0][0] = val.astype(refs[0].dtype)
        else:
            refs[0][...] = val.astype(refs[0].dtype)

    def body_single(a_ref, b_ref, *refs):
        put(refs, dot(a_ref, b_ref))

    def body_acc(a_ref, b_ref, *refs):
        kk = pl.program_id(2)
        acc_ref = refs[-1]

        @pl.when(kk == 0)
        def _():
            acc_ref[...] = jnp.zeros_like(acc_ref)

        acc_ref[...] += dot(a_ref, b_ref)

        @pl.when(kk == nk - 1)
        def _():
            put(refs[:-1], acc_ref[...])

    a_spec = pl.BlockSpec((tk, tm), lambda i, j, kk: (kk, i)) if ta else pl.BlockSpec((tm, tk), lambda i, j, kk: (i, kk))
    if b_split == 1:
        b_spec = pl.BlockSpec((tn, tk), lambda i, j, kk: (j, kk)) if tb else pl.BlockSpec((tk, tn), lambda i, j, kk: (kk, j))
    elif tb:
        per = k_unit // tk
        b_spec = pl.BlockSpec((1, tn, tk), lambda i, j, kk: (kk // per, j, kk % per))
    else:
        per = n // b_split // tn
        b_spec = pl.BlockSpec((1, tk, tn), lambda i, j, kk: (j // per, kk, j % per))
    if out_split == 1:
        o_spec = pl.BlockSpec((tm, tn), lambda i, j, kk: (i, j))
        o_shape = None if epilogue is not None else jax.ShapeDtypeStruct((m, n), out_dtype)
    else:
        per_o = n // out_split // tn
        o_spec = pl.BlockSpec((1, tm, tn), lambda i, j, kk: (j // per_o, i, j % per_o))
        o_shape = jax.ShapeDtypeStruct((out_split, m, n // out_split), out_dtype)
    if epilogue is not None:
        o_shape = [jax.ShapeDtypeStruct((m, n), dt) for dt in out_dtype]
        o_spec = [o_spec] * len(out_dtype)
    return pl.pallas_call(
        body_single if nk == 1 else body_acc, name=name, grid=(m // tm, n // tn, nk),
        in_specs=[a_spec, b_spec] + [pl.BlockSpec((tm, tn), lambda i, j, kk: (i, j))] * n_extra, out_specs=o_spec, out_shape=o_shape,
        scratch_shapes=[] if nk == 1 else [pltpu.VMEM((tm, tn), F32)],
        compiler_params=_params("parallel", "parallel", "arbitrary"),
    )(a, b, *extras)


def _stacked_proj(lhs, w, widths, tb, name, gates=None, out_dtype=F32):
    s = lhs[0].shape[0]
    n = w.shape[1]
    starts = [sum(widths[:i]) for i in range(len(widths))]
    n_lhs = len(lhs)
    n_gate = 0 if gates is None else n_lhs
    assert not (tb and n_gate)
    dims = (((1,), (1 if tb else 0,)), ((), ()))

    def body(*refs):
        w_ref = refs[n_lhs]
        g_refs, o_refs = refs[n_lhs + 1:n_lhs + 1 + n_gate], refs[n_lhs + 1 + n_gate:]
        merged = None
        for i, (a_ref, o_ref, start, width) in enumerate(zip(refs[:n_lhs], o_refs, starts, widths)):
            prod = lax.dot_general(a_ref[...].astype(BF16), w_ref[start:start + width, :].astype(BF16), dims,
                                   preferred_element_type=F32)
            o_ref[...] = prod.astype(o_ref.dtype)
            if n_gate:
                term = jax.nn.sigmoid(g_refs[i][...]) * prod
                merged = term if merged is None else merged + term
        if n_gate:
            o_refs[n_lhs][...] = merged.astype(BF16)

    in_cols = [n if tb else width for width in widths]
    out_cols = [width if tb else n for width in widths]
    assert [a.shape for a in lhs] == [(s, cols) for cols in in_cols], ([a.shape for a in lhs], widths, tb)
    return pl.pallas_call(
        body, name=name, grid=(s // ROW_TILE,),
        in_specs=[_row_spec(cols) for cols in in_cols] + [pl.BlockSpec(w.shape, lambda i: (0, 0))]
        + [_row_spec(n, i) for i in range(n_gate)],
        out_specs=[_row_spec(cols) for cols in out_cols] + [_row_spec(n)] * (n_gate > 0),
        out_shape=[jax.ShapeDtypeStruct((s, cols), out_dtype) for cols in out_cols] + [jax.ShapeDtypeStruct((s, n), BF16)] * (n_gate > 0),
        compiler_params=_params("parallel"),
    )(*lhs, w, *([gates] * n_gate))


def _stacked_dw(lhs, rhs, name):
    n = rhs[0].shape[1]
    tr = min(a.shape[1] for a in lhs)
    tn = _pick(n, (1024, 512, 256, 128))
    counts = [a.shape[1] // tr for a in lhs]
    starts = [sum(counts[:i]) for i in range(len(lhs))]
    assert all(a.shape[1] == c * tr for a, c in zip(lhs, counts))
    n_lhs = len(lhs)

    def body(*refs):
        o_ref = refs[-1]
        j = pl.program_id(1)
        for a_ref, b_ref, start, count in zip(refs[:n_lhs], refs[n_lhs:-1], starts, counts):
            @pl.when((j >= start) & (j < start + count))
            def _():
                o_ref[...] = lax.dot_general(a_ref[...].astype(BF16), b_ref[...].astype(BF16), (((0,), (0,)), ((), ())),
                                             preferred_element_type=F32)

    def lhs_spec(a, start, count):
        return pl.BlockSpec((a.shape[0], tr), lambda c, j: (0, jnp.clip(j - start, 0, count - 1)))

    return pl.pallas_call(
        body, name=name, grid=(n // tn, sum(counts)),
        in_specs=[lhs_spec(a, st, ct) for a, st, ct in zip(lhs, starts, counts)]
        + [pl.BlockSpec((b.shape[0], tn), lambda c, j: (0, c)) for b in rhs],
        out_specs=pl.BlockSpec((tr, tn), lambda c, j: (j, c)),
        out_shape=jax.ShapeDtypeStruct((sum(counts) * tr, n), F32),
        compiler_params=_params("parallel", "parallel"),
    )(*lhs, *rhs)


def _ew(fn, ins, out_dtypes, name, tc=None):
    shape = ins[0].shape
    lead, (rows, cols) = shape[:-2], shape[-2:]
    tc = cols if tc is None else tc
    if tc > 1024:
        tr = _pick(rows, (EW_ROWS, 128, 8))
    elif tc > 128:
        tr = _pick(rows, (2 * EW_ROWS, EW_ROWS, 128, 8))
    else:
        tr = _pick(rows, (4096, 2256, 2048, 1024, EW_ROWS, 8))
    n_in = len(ins)

    def body(*refs):
        res = fn(*[r[...] for r in refs[:n_in]])
        for o_ref, val in zip(refs[n_in:], res):
            o_ref[...] = val.astype(o_ref.dtype)

    if lead:
        spec = pl.BlockSpec((None, tr, tc), lambda l, i, j: (l, i, j))
    else:
        spec = pl.BlockSpec((tr, tc), lambda i, j: (i, j))
    return pl.pallas_call(
        body, name=name, grid=lead + (rows // tr, cols // tc),
        in_specs=[spec] * n_in, out_specs=[spec] * len(out_dtypes),
        out_shape=[jax.ShapeDtypeStruct(shape, dt) for dt in out_dtypes],
        compiler_params=_params(*(["parallel"] * (len(lead) + 2))),
    )(*ins)


def _relu2_fwd(a):
    r = jnp.maximum(a, 0.0)
    return a, r * r


def _relu2_bwd(dr, a):
    return (dr * (2.0 * jnp.maximum(a, 0.0)),)


def _adamw_math(w, g, m, v):
    m = ADAM_B1 * m + (1.0 - ADAM_B1) * g
    v = ADAM_B2 * v + (1.0 - ADAM_B2) * (g * g)
    m_hat = m / (1.0 - ADAM_B1 ** ADAM_STEP)
    v_hat = v / (1.0 - ADAM_B2 ** ADAM_STEP)
    delta = -ADAM_LR * (m_hat / (jnp.sqrt(v_hat) + ADAM_EPS) + ADAM_WD * w)
    return delta, m, v


def _adamw(w, g, m, v, name):
    return _ew(_adamw_math, [w, g, m, v], [F32, F32, F32], name)


def _adamw_layers(w, g_layers, m, v, name):
    depth, rows, cols = w.shape
    tr = _pick(rows, (2 * EW_ROWS, EW_ROWS, 128, 8)) if cols <= 1024 else _pick(rows, (EW_ROWS, 128, 8))

    def body(w_ref, *refs):
        g_refs, (m_ref, v_ref, d_ref, mo_ref, vo_ref, go_ref) = refs[:depth], refs[depth:]
        layer = pl.program_id(0)
        g = g_refs[0][...]
        for l in range(1, depth):
            g = jnp.where(layer == l, g_refs[l][...], g)
        d_ref[...], mo_ref[...], vo_ref[...] = _adamw_math(w_ref[...], g, m_ref[...], v_ref[...])
        go_ref[...] = g

    spec = pl.BlockSpec((None, tr, cols), lambda l, i: (l, i, 0))
    g_specs = [pl.BlockSpec((tr, cols), lambda l, i, k=k: (jnp.where(l == k, i, 0), 0)) for k in range(depth)]
    return pl.pallas_call(
        body, name=name, grid=(depth, rows // tr),
        in_specs=[spec] + g_specs + [spec, spec], out_specs=[spec] * 4,
        out_shape=[jax.ShapeDtypeStruct(w.shape, F32)] * 4, compiler_params=_params("arbitrary", "arbitrary"),
    )(w, *g_layers, m, v)


def _adamw_many(ws, gs, ms, vs, name):
    n = len(ws)

    def body(*refs):
        for i in range(n):
            res = _adamw_math(*[refs[k * n + i][...] for k in range(4)])
            for k in range(3):
                refs[(4 + k) * n + i][...] = res[k]

    outs = pl.pallas_call(
        body, name=name, out_shape=[jax.ShapeDtypeStruct(w.shape, F32) for w in ws] * 3,
        compiler_params=pltpu.CompilerParams(vmem_limit_bytes=V7X_VMEM_LIMIT),
    )(*ws, *gs, *ms, *vs)
    return outs[:n], outs[n:2 * n], outs[2 * n:]


def _row_spec(cols, block=0):
    return pl.BlockSpec((ROW_TILE, cols), lambda i, block=block: (i, block))


def _vec_spec(cols):
    return pl.BlockSpec((1, cols), lambda i: (0, 0))


def _vec_args(*vecs):
    arrays = [v[0] if isinstance(v, tuple) else v for v in vecs]
    specs = [pl.BlockSpec((None, 1, D), lambda i, row=v[1]: (row, 0, 0)) if isinstance(v, tuple) else _vec_spec(D) for v in vecs]
    return arrays, specs


def _sum_spec(cols):
    return pl.BlockSpec((8, cols), lambda i: (0, 0))


def _rstd(x):
    return lax.rsqrt(jnp.mean(x * x, axis=-1, keepdims=True) + RMS_EPS)


def _modnorm_fwd(x, g, shift, scale, name):
    s = x.shape[0]

    def body(x_ref, g_ref, sh_ref, sc_ref, h_ref):
        xv = x_ref[...]
        n = xv * _rstd(xv)
        h_ref[...] = ((n * g_ref[...]) * (1.0 + sc_ref[...]) + sh_ref[...]).astype(BF16)

    vecs, vec_specs = _vec_args(g, shift, scale)
    return pl.pallas_call(
        body, name=name, grid=(s // ROW_TILE,),
        in_specs=[_row_spec(D)] + vec_specs, out_specs=_row_spec(D),
        out_shape=jax.ShapeDtypeStruct((s, D), BF16), compiler_params=_params("parallel"),
    )(x, *vecs)


def _post_fwd(x, y, g, gate, name):
    s = x.shape[0]

    def body(x_ref, y_ref, g_ref, gate_ref, o_ref):
        yv = y_ref[...]
        o_ref[...] = x_ref[...] + gate_ref[...] * ((yv * _rstd(yv)) * g_ref[...])

    vecs, vec_specs = _vec_args(g, gate)
    return pl.pallas_call(
        body, name=name, grid=(s // ROW_TILE,),
        in_specs=[_row_spec(D), _row_spec(D)] + vec_specs, out_specs=_row_spec(D),
        out_shape=jax.ShapeDtypeStruct((s, D), F32), compiler_params=_params("parallel"),
    )(x, y, *vecs)


def _post_bwd(dxo, y, g, gate, name):
    s = dxo.shape[0]

    def body(d_ref, y_ref, g_ref, gate_ref, dy_ref, sum_ref):
        @pl.when(pl.program_id(0) == 0)
        def _():
            sum_ref[...] = jnp.zeros_like(sum_ref)

        dv, yv = d_ref[...], y_ref[...]
        r = _rstd(yv)
        n = yv * r
        sum_ref[0:1, :] += jnp.sum(dv * (n * g_ref[...]), axis=0, keepdims=True)
        sum_ref[1:2, :] += jnp.sum((dv * gate_ref[...]) * n, axis=0, keepdims=True)
        dn = (dv * gate_ref[...]) * g_ref[...]
        dy_ref[...] = (r * (dn - n * jnp.mean(dn * n, axis=-1, keepdims=True))).astype(BF16)

    vecs, vec_specs = _vec_args(g, gate)
    return pl.pallas_call(
        body, name=name, grid=(s // ROW_TILE,),
        in_specs=[_row_spec(D), _row_spec(D)] + vec_specs,
        out_specs=[_row_spec(D), _sum_spec(D)],
        out_shape=[jax.ShapeDtypeStruct((s, D), BF16), jax.ShapeDtypeStruct((8, D), F32)],
        compiler_params=_params("arbitrary"),
    )(dxo, y, *vecs)


def _modnorm_bwd(dh, x, dxo, g, scale, name):
    s = dh.shape[0]

    def body(dh_ref, x_ref, d_ref, g_ref, sc_ref, dx_ref, sum_ref):
        @pl.when(pl.program_id(0) == 0)
        def _():
            sum_ref[...] = jnp.zeros_like(sum_ref)

        dhv, xv = dh_ref[...], x_ref[...]
        r = _rstd(xv)
        n = xv * r
        one_sc = 1.0 + sc_ref[...]
        sum_ref[0:1, :] += jnp.sum(dhv, axis=0, keepdims=True)
        sum_ref[1:2, :] += jnp.sum(dhv * (n * g_ref[...]), axis=0, keepdims=True)
        sum_ref[2:3, :] += jnp.sum((dhv * one_sc) * n, axis=0, keepdims=True)
        dn = (dhv * one_sc) * g_ref[...]
        dx_ref[...] = d_ref[...] + r * (dn - n * jnp.mean(dn * n, axis=-1, keepdims=True))

    vecs, vec_specs = _vec_args(g, scale)
    return pl.pallas_call(
        body, name=name, grid=(s // ROW_TILE,),
        in_specs=[_row_spec(D), _row_spec(D), _row_spec(D)] + vec_specs,
        out_specs=[_row_spec(D), _sum_spec(D)],
        out_shape=[jax.ShapeDtypeStruct((s, D), F32), jax.ShapeDtypeStruct((8, D), F32)],
        compiler_params=_params("arbitrary"),
    )(dh, x, dxo, *vecs)


def _post_pre_fwd(x, y, g_post, gate, g_pre, shift, scale, name):
    s = x.shape[0]

    def body(x_ref, y_ref, gp_ref, gate_ref, g_ref, sh_ref, sc_ref, o_ref, h_ref):
        yv = y_ref[...]
        xo = x_ref[...] + gate_ref[...] * ((yv * _rstd(yv)) * gp_ref[...])
        o_ref[...] = xo
        h_ref[...] = (((xo * _rstd(xo)) * g_ref[...]) * (1.0 + sc_ref[...]) + sh_ref[...]).astype(BF16)

    vecs, vec_specs = _vec_args(g_post, gate, g_pre, shift, scale)
    return pl.pallas_call(
        body, name=name, grid=(s // ROW_TILE,),
        in_specs=[_row_spec(D), _row_spec(D)] + vec_specs, out_specs=[_row_spec(D), _row_spec(D)],
        out_shape=[jax.ShapeDtypeStruct((s, D), F32), jax.ShapeDtypeStruct((s, D), BF16)], compiler_params=_params("parallel"),
    )(x, y, *vecs)


def _pre_post_bwd(dh, x, dxo, g_pre, scale, y, g_post, gate, name):
    s = dh.shape[0]

    def body(dh_ref, x_ref, d_ref, y_ref, g_ref, sc_ref, gp_ref, gate_ref, dx_ref, dy_ref, sum_ref):
        @pl.when(pl.program_id(0) == 0)
        def _():
            sum_ref[...] = jnp.zeros_like(sum_ref)

        dhv, xv = dh_ref[...], x_ref[...]
        r = _rstd(xv)
        n = xv * r
        one_sc = 1.0 + sc_ref[...]
        sum_ref[0:1, :] += jnp.sum(dhv, axis=0, keepdims=True)
        sum_ref[1:2, :] += jnp.sum(dhv * (n * g_ref[...]), axis=0, keepdims=True)
        sum_ref[2:3, :] += jnp.sum((dhv * one_sc) * n, axis=0, keepdims=True)
        dn = (dhv * one_sc) * g_ref[...]
        dv = d_ref[...] + r * (dn - n * jnp.mean(dn * n, axis=-1, keepdims=True))
        dx_ref[...] = dv

        yv = y_ref[...]
        ry = _rstd(yv)
        ny = yv * ry
        sum_ref[3:4, :] += jnp.sum(dv * (ny * gp_ref[...]), axis=0, keepdims=True)
        sum_ref[4:5, :] += jnp.sum((dv * gate_ref[...]) * ny, axis=0, keepdims=True)
        dny = (dv * gate_ref[...]) * gp_ref[...]
        dy_ref[...] = (ry * (dny - ny * jnp.mean(dny * ny, axis=-1, keepdims=True))).astype(BF16)

    vecs, vec_specs = _vec_args(g_pre, scale, g_post, gate)
    return pl.pallas_call(
        body, name=name, grid=(s // ROW_TILE,),
        in_specs=[_row_spec(D)] * 4 + vec_specs,
        out_specs=[_row_spec(D), _row_spec(D), _sum_spec(D)],
        out_shape=[jax.ShapeDtypeStruct((s, D), F32), jax.ShapeDtypeStruct((s, D), BF16), jax.ShapeDtypeStruct((8, D), F32)],
        compiler_params=_params("arbitrary"),
    )(dh, x, dxo, y, *vecs)


def _loss_head(y, target):
    s = y.shape[0]

    def body(y_ref, t_ref, dy_ref, sum_ref):
        @pl.when(pl.program_id(0) == 0)
        def _():
            sum_ref[...] = jnp.zeros_like(sum_ref)

        err = y_ref[...] - t_ref[...]
        dy_ref[...] = err * (1.0 / D)
        sum_ref[...] += jnp.sum(err * err)

    return pl.pallas_call(
        body, name="loss_head", grid=(s // ROW_TILE,),
        in_specs=[_row_spec(D), _row_spec(D)],
        out_specs=[_row_spec(D), pl.BlockSpec((8, 128), lambda i: (0, 0))],
        out_shape=[jax.ShapeDtypeStruct((s, D), F32), jax.ShapeDtypeStruct((8, 128), F32)],
        compiler_params=_params("arbitrary"),
    )(y, target)


def _merge_bwd(dm, z, pa, pb, pc):
    s = z.shape[0]

    def body(dm_ref, g0_ref, g1_ref, g2_ref, pa_ref, pb_ref, pc_ref, dgl_ref, da_ref, db_ref, dc_ref):
        dmv = dm_ref[...]
        for i, (g_ref, p_ref, d_ref) in enumerate(((g0_ref, pa_ref, da_ref), (g1_ref, pb_ref, db_ref), (g2_ref, pc_ref, dc_ref))):
            gate = jax.nn.sigmoid(g_ref[...])
            dgl_ref[:, i * D:(i + 1) * D] = ((dmv * p_ref[...]) * (gate * (1.0 - gate))).astype(BF16)
            d_ref[...] = (dmv * gate).astype(BF16)

    return pl.pallas_call(
        body, name="merge_bwd", grid=(s // ROW_TILE,),
        in_specs=[_row_spec(D), _row_spec(D, 0), _row_spec(D, 1), _row_spec(D, 2), _row_spec(D), _row_spec(D), _row_spec(D)],
        out_specs=[_row_spec(3 * D), _row_spec(D), _row_spec(D), _row_spec(D)],
        out_shape=[jax.ShapeDtypeStruct((s, Z_COLS), BF16)] + [jax.ShapeDtypeStruct((s, D), BF16)] * 3,
        compiler_params=_params("parallel"),
    )(dm, z, z, z, pa, pb, pc)


def _shift_down(v, n):
    row = lax.broadcasted_iota(jnp.int32, v.shape, 0)
    return jnp.where(row >= n, pltpu.roll(v, n, axis=0), 0.0)


def _shift_up(v, n):
    s = v.shape[0]
    row = lax.broadcasted_iota(jnp.int32, v.shape, 0)
    return jnp.where(row < s - n, pltpu.roll(v, s - n, axis=0), 0.0)


def _log_sigmoid(v):
    return jnp.minimum(v, 0.0) - jnp.log1p(jnp.exp(-jnp.abs(v)))


def _cumf_fwd(fl, bias):
    s = fl.shape[0]

    def body(fl_ref, b_ref, o_ref):
        acc = _log_sigmoid(fl_ref[...] + b_ref[...])
        step = 1
        while step < s:
            acc = acc + _shift_down(acc, step)
            step *= 2
        o_ref[...] = acc

    return pl.pallas_call(body, name="cumf_fwd", out_shape=jax.ShapeDtypeStruct((s, 128), F32),
                          compiler_params=pltpu.CompilerParams(vmem_limit_bytes=V7X_VMEM_LIMIT))(fl, bias)


def _cumf_bwd(dcum, fl, bias):
    s = fl.shape[0]

    def body(d_ref, fl_ref, b_ref, dfl_ref, db_ref):
        acc = d_ref[...]
        step = 1
        while step < s:
            acc = acc + _shift_up(acc, step)
            step *= 2
        dfl = acc * jax.nn.sigmoid(-(fl_ref[...] + b_ref[...]))
        dfl_ref[...] = dfl.astype(BF16)
        db_ref[...] = jnp.broadcast_to(jnp.sum(dfl, axis=0, keepdims=True), (8, 128))

    return pl.pallas_call(
        body, name="cumf_bwd",
        out_shape=[jax.ShapeDtypeStruct((s, 128), BF16), jax.ShapeDtypeStruct((8, 128), F32)],
        compiler_params=pltpu.CompilerParams(vmem_limit_bytes=V7X_VMEM_LIMIT))(dcum, fl, bias)


def _pool_windows(v, shift):
    s2 = v + shift(v, 1)
    s4 = s2 + shift(s2, 2)
    s8 = s4 + shift(s4, 4)
    s16 = s8 + shift(s8, 8)
    group = lax.broadcasted_iota(jnp.int32, v.shape, 1) // 64
    return jnp.where(group == 0, s2, jnp.where(group == 1, s4, jnp.where(group == 2, s8, s16)))


def _pool_count(shape):
    group = lax.broadcasted_iota(jnp.int32, shape, 1) // 64
    window = jnp.where(group == 0, 2.0, jnp.where(group == 1, 4.0, jnp.where(group == 2, 8.0, 16.0)))
    t1 = (lax.broadcasted_iota(jnp.int32, shape, 0) + 1).astype(F32)
    return jnp.minimum(t1, window)


def _pc_specs(s):
    zcol = lambda blk: pl.BlockSpec((s, 256), lambda i, blk=blk: (0, blk))
    first = Z_PC // 256
    return [zcol(first), zcol(first + 1), zcol(first + 2), zcol(first + 3),
            pl.BlockSpec((256, 256), lambda i: (0, 0)), pl.BlockSpec((1, 256), lambda i: (0, 0)),
            pl.BlockSpec((3, 256), lambda i: (0, 0))]


def _poolconv_fwd(z, wbd, pscale, convw):
    s = z.shape[0]

    def body(pu_ref, ch_ref, cb_ref, cc_ref, w_ref, ps_ref, cw_ref, yb_ref, yc_ref):
        u = pu_ref[...]
        p = _pool_windows(u, _shift_down) / _pool_count(u.shape) - u
        yb = jnp.dot(p.astype(BF16), w_ref[...].astype(BF16), preferred_element_type=F32) * ps_ref[...]
        yb_ref[...] = yb.astype(BF16)
        uc = cc_ref[...] * ch_ref[...]
        cw = cw_ref[...]
        conv = cw[0:1, :] * _shift_down(uc, 2) + cw[1:2, :] * _shift_down(uc, 1) + cw[2:3, :] * uc
        yc_ref[...] = (cb_ref[...] * conv).astype(BF16)

    out = pl.BlockSpec((s, 256), lambda i: (0, 0))
    return pl.pallas_call(
        body, name="poolconv_fwd", grid=(1,), in_specs=_pc_specs(s), out_specs=[out, out],
        out_shape=[jax.ShapeDtypeStruct((s, 256), BF16)] * 2, compiler_params=_params("arbitrary"),
    )(z, z, z, z, wbd, pscale, convw)


def _poolconv_bwd(dyb, dyc, z, wbd, pscale, convw):
    s = z.shape[0]

    def body(dyb_ref, dyc_ref, pu_ref, ch_ref, cb_ref, cc_ref, w_ref, ps_ref, cw_ref, dz_ref, dw_ref, dps_ref, dcw_ref):
        u = pu_ref[...]
        count = _pool_count(u.shape)
        p = (_pool_windows(u, _shift_down) / count - u).astype(BF16)
        wb = w_ref[...].astype(BF16)
        dyb_v = dyb_ref[...]
        pw = jnp.dot(p, wb, preferred_element_type=F32)
        dps_ref[...] = jnp.broadcast_to(jnp.sum(dyb_v * pw, axis=0, keepdims=True), (8, 256))
        dys = (dyb_v * ps_ref[...]).astype(BF16)
        dp = lax.dot_general(dys, wb, (((1,), (1,)), ((), ())), preferred_element_type=F32)
        dw_ref[...] = lax.dot_general(p, dys, (((0,), (0,)), ((), ())), preferred_element_type=F32)
        dz_ref[:, 0:256] = (_pool_windows(dp / count, _shift_up) - dp).astype(BF16)

        ch, cb, cc = ch_ref[...], cb_ref[...], cc_ref[...]
        uc = cc * ch
        cw = cw_ref[...]
        u2, u1 = _shift_down(uc, 2), _shift_down(uc, 1)
        conv = cw[0:1, :] * u2 + cw[1:2, :] * u1 + cw[2:3, :] * uc
        dyc_v = dyc_ref[...]
        dconv = dyc_v * cb
        du = cw[0:1, :] * _shift_up(dconv, 2) + cw[1:2, :] * _shift_up(dconv, 1) + cw[2:3, :] * dconv
        dz_ref[:, 256:512] = (du * cc).astype(BF16)
        dz_ref[:, 512:768] = (dyc_v * conv).astype(BF16)
        dz_ref[:, 768:1024] = (du * ch).astype(BF16)
        dcw_ref[...] = jnp.zeros_like(dcw_ref)
        dcw_ref[0:1, :] = jnp.sum(dconv * u2, axis=0, keepdims=True)
        dcw_ref[1:2, :] = jnp.sum(dconv * u1, axis=0, keepdims=True)
        dcw_ref[2:3, :] = jnp.sum(dconv * uc, axis=0, keepdims=True)

    blk = lambda r, c: pl.BlockSpec((r, c), lambda i: (0, 0))
    return pl.pallas_call(
        body, name="poolconv_bwd", grid=(1,),
        in_specs=[blk(s, 256), blk(s, 256)] + _pc_specs(s),
        out_specs=[blk(s, 1024), blk(256, 256), blk(8, 256), blk(8, 256)],
        out_shape=[jax.ShapeDtypeStruct((s, 1024), BF16), jax.ShapeDtypeStruct((256, 256), F32),
                   jax.ShapeDtypeStruct((8, 256), F32), jax.ShapeDtypeStruct((8, 256), F32)],
        compiler_params=_params("arbitrary"),
    )(dyb, dyc, z, z, z, z, wbd, pscale, convw)


_NT = (((1,), (1,)), ((), ()))
_TN = (((0,), (0,)), ((), ()))


ATT_Q, ATT_K = 256, 256
ATT_HEADS_BWD = 8
ATT_HEADS = 8


def _att_logits(q, k, fr, q0, k0, masked):
    logits = lax.dot_general(q, k, _NT, preferred_element_type=F32) - fr
    if not masked:
        return logits
    row = q0 + lax.broadcasted_iota(jnp.int32, logits.shape, 0)
    col = k0 + lax.broadcasted_iota(jnp.int32, logits.shape, 1)
    return jnp.where(row >= col, logits, NEG_INF)


def _causal_sweep(step, qi, init):
    n_full = (qi * ATT_Q) // ATT_K
    carry = lax.fori_loop(0, n_full, lambda j, carry: step(j, carry, False), init)
    return step(n_full, carry, True)


HEAD_PAIRS = HEADS // 2


def _lane_pick(v, lane, idx):
    return jnp.sum(jnp.where(lane == idx, v, 0.0), axis=-1, keepdims=True)


def _lane_put(lane, idx, col):
    return jnp.where(lane == idx, col, 0.0)


def _split_heads(v, low):
    zero = jnp.zeros_like(v)
    return jnp.where(low, v, zero), jnp.where(low, zero, v)


def _attn_fwd(qkv, fr):
    s = qkv.shape[0]
    nk = s // ATT_K
    width = ATT_HEADS * HEAD_DIM
    groups = HEADS // ATT_HEADS

    def body(q_ref, k_ref, v_ref, fr_ref, o_ref, lse_ref):
        qi, grp = pl.program_id(0), pl.program_id(1)
        lane = lax.broadcasted_iota(jnp.int32, (ATT_Q, 128), 1)
        low = lane < HEAD_DIM
        qs = []
        for pr in range(ATT_HEADS // 2):
            qs += _split_heads(q_ref[:, 128 * pr:128 * (pr + 1)] * (HEAD_DIM ** -0.5), low)

        def step(j, carry, masked):
            k0 = pl.multiple_of(j * ATT_K, ATT_K)
            out = []
            for h in range(ATT_HEADS):
                cols = slice(128 * (h // 2), 128 * (h // 2 + 1))
                m, l, acc = carry[h]
                logits = _att_logits(qs[h], k_ref[pl.ds(k0, ATT_K), cols], fr_ref[h, pl.ds(j, 1), :], qi * ATT_Q, k0, masked)
                m_new = jnp.maximum(m, jnp.max(logits, axis=-1, keepdims=True))
                p = jnp.exp(logits - m_new)
                alpha = jnp.exp(m - m_new)
                l = alpha * l + jnp.sum(p, axis=-1, keepdims=True)
                acc = alpha * acc + jnp.dot(p.astype(BF16), v_ref[pl.ds(k0, ATT_K), cols], preferred_element_type=F32)
                out.append((m_new, l, acc))
            return tuple(out)

        one = (jnp.full((ATT_Q, 1), NEG_INF, F32), jnp.zeros((ATT_Q, 1), F32), jnp.zeros((ATT_Q, 128), F32))
        done = _causal_sweep(step, qi, (one,) * ATT_HEADS)

        @pl.when(grp == 0)
        def _():
            lse_ref[...] = jnp.zeros_like(lse_ref)

        lse = jnp.zeros((ATT_Q, 128), F32)
        for pr in range(ATT_HEADS // 2):
            (m0, l0, acc0), (m1, l1, acc1) = done[2 * pr], done[2 * pr + 1]
            o_ref[:, 128 * pr:128 * (pr + 1)] = jnp.where(low, acc0 / l0, acc1 / l1)
            head = ATT_HEADS * grp + 2 * pr
            lse = lse + _lane_put(lane, head, m0 + jnp.log(l0)) + _lane_put(lane, head + 1, m1 + jnp.log(l1))
        lse_ref[...] += lse

    return pl.pallas_call(
        body, name="attn_fwd", grid=(s // ATT_Q, groups),
        in_specs=[pl.BlockSpec((ATT_Q, width), lambda i, g: (i, g)),
                  pl.BlockSpec((s, width), lambda i, g: (0, groups + g)),
                  pl.BlockSpec((s, width), lambda i, g: (0, 2 * groups + g)),
                  pl.BlockSpec((ATT_HEADS, nk, ATT_K), lambda i, g: (g, 0, 0))],
        out_specs=[pl.BlockSpec((ATT_Q, width), lambda i, g: (i, g)), pl.BlockSpec((ATT_Q, 128), lambda i, g: (i, 0))],
        out_shape=[jax.ShapeDtypeStruct((s, A_WIDTH), F32), jax.ShapeDtypeStruct((s, 128), F32)],
        compiler_params=_params("parallel", "arbitrary"),
    )(qkv, qkv, qkv, fr)


def _attn_bwd(qkv, do, o, lse, fr):
    s = qkv.shape[0]
    nk = s // ATT_K
    scale = HEAD_DIM ** -0.5
    heads = ATT_HEADS_BWD
    width = heads * HEAD_DIM
    groups = HEADS // heads

    def body(q_ref, k_ref, v_ref, do_ref, o_ref, lse_ref, fr_ref, dq_ref, dk_ref, dv_ref, dfc_ref, dfr_ref, dk_acc, dv_acc):
        grp = pl.program_id(0)
        lane = lax.broadcasted_iota(jnp.int32, (ATT_Q, 128), 1)
        low = lane < HEAD_DIM
        low_t = lax.broadcasted_iota(jnp.int32, (128, ATT_Q), 0) < HEAD_DIM
        dk_acc[...] = jnp.zeros_like(dk_acc)
        dv_acc[...] = jnp.zeros_like(dv_acc)
        dfr_ref[...] = jnp.zeros_like(dfr_ref)

        @pl.when(grp == 0)
        def _():
            dfc_ref[...] = jnp.zeros_like(dfc_ref)

        def outer(i, carry):
            q0 = pl.multiple_of(i * ATT_Q, ATT_Q)
            rows = pl.ds(q0, ATT_Q)
            lsev = lse_ref[rows, :]
            qts, dots, qs, dos, deltas, lses = [], [], [], [], [], []
            for pr in range(heads // 2):
                pcols = slice(128 * pr, 128 * (pr + 1))
                q2, do2 = q_ref[rows, pcols] * scale, do_ref[rows, pcols]
                prod = do2 * o_ref[rows, pcols]
                deltas += [jnp.sum(jnp.where(low, prod, 0.0), axis=-1, keepdims=True),
                           jnp.sum(jnp.where(low, 0.0, prod), axis=-1, keepdims=True)]
                dob2 = do2.astype(BF16)
                qts += _split_heads(q2.astype(F32).T.astype(BF16), low_t)
                dots += _split_heads(do2.T.astype(BF16), low_t)
                qs += _split_heads(q2, low)
                dos += _split_heads(dob2, low)
                lses += [_lane_pick(lsev, lane, heads * grp + 2 * pr), _lane_pick(lsev, lane, heads * grp + 2 * pr + 1)]

            def inner(j, carry, masked):
                k0 = pl.multiple_of(j * ATT_K, ATT_K)
                krows = pl.ds(k0, ATT_K)
                out, dkt, dvt = [], [], []
                for h in range(heads):
                    pcols = slice(128 * (h // 2), 128 * (h // 2 + 1))
                    dq, dfc = carry[h]
                    k2 = k_ref[krows, pcols]
                    p = jnp.exp(_att_logits(qs[h], k2, fr_ref[h, pl.ds(j, 1), :], q0, k0, masked) - lses[h])
                    dp = lax.dot_general(dos[h], v_ref[krows, pcols], _NT, preferred_element_type=F32)
                    ds = p * (dp - deltas[h])
                    dsb = ds.astype(BF16)
                    dkt.append(jnp.dot(qts[h], dsb, preferred_element_type=F32))
                    dvt.append(jnp.dot(dots[h], p.astype(BF16), preferred_element_type=F32))
                    dfr_ref[h, pl.ds(j, 1), :] -= jnp.sum(ds, axis=0, keepdims=True)
                    out.append((dq + jnp.dot(dsb, k2, preferred_element_type=F32), dfc + (ds[:, :128] + ds[:, 128:])))
                for pr in range(heads // 2):
                    prows = slice(128 * pr, 128 * (pr + 1))
                    dk_acc[j, prows, :] += dkt[2 * pr] + dkt[2 * pr + 1]
                    dv_acc[j, prows, :] += dvt[2 * pr] + dvt[2 * pr + 1]
                return tuple(out)

            one = (jnp.zeros((ATT_Q, 128), F32), jnp.zeros((ATT_Q, 128), F32))
            done = _causal_sweep(inner, i, (one,) * heads)
            dfc = jnp.zeros((ATT_Q, 128), F32)
            for pr in range(heads // 2):
                (dq0, dfc0), (dq1, dfc1) = done[2 * pr], done[2 * pr + 1]
                dq_ref[rows, 128 * pr:128 * (pr + 1)] = (jnp.where(low, dq0, dq1) * scale).astype(BF16)
                head = heads * grp + 2 * pr
                dfc = (dfc + _lane_put(lane, head, jnp.sum(dfc0, axis=-1, keepdims=True))
                       + _lane_put(lane, head + 1, jnp.sum(dfc1, axis=-1, keepdims=True)))
            dfc_ref[rows, :] += dfc
            return carry

        lax.fori_loop(0, s // ATT_Q, outer, 0)
        for j in range(nk):
            for pr in range(heads // 2):
                prows, pcols = slice(128 * pr, 128 * (pr + 1)), slice(128 * pr, 128 * (pr + 1))
                dk_ref[ATT_K * j:ATT_K * (j + 1), pcols] = dk_acc[j, prows, :].T.astype(BF16)
                dv_ref[ATT_K * j:ATT_K * (j + 1), pcols] = dv_acc[j, prows, :].T.astype(BF16)

    part = lambda first: pl.BlockSpec((s, width), lambda g, first=first: (0, first + g))
    whole = pl.BlockSpec((s, 128), lambda g: (0, 0))
    rowv = pl.BlockSpec((heads, nk, ATT_K), lambda g: (g, 0, 0))
    return pl.pallas_call(
        body, name="attn_bwd", grid=(groups,),
        in_specs=[part(0), part(groups), part(2 * groups), part(0), part(0), whole, rowv],
        out_specs=[part(0), part(0), part(0), whole, rowv],
        out_shape=[jax.ShapeDtypeStruct((s, A_WIDTH), BF16)] * 3 + [jax.ShapeDtypeStruct((s, 128), F32), jax.ShapeDtypeStruct((HEADS, nk, ATT_K), F32)],
        scratch_shapes=[pltpu.VMEM((nk, width, ATT_K), F32), pltpu.VMEM((nk, width, ATT_K), F32)],
        compiler_params=_params("arbitrary"),
    )(qkv, qkv, qkv, do, o, lse, fr)


def _ada_fwd(c_all, w_ada, b_loc):
    depth, _, n = w_ada.shape
    tn = 512

    def body(c_ref, w_ref, b_ref, o_ref, sc_ref):
        cv = c_ref[...]
        sc = cv * jax.nn.sigmoid(cv)
        sc_ref[...] = sc
        o_ref[0] = jnp.dot(sc.astype(BF16), w_ref[0].astype(BF16), preferred_element_type=F32) + b_ref[0]

    return pl.pallas_call(
        body, name="ada_fwd", grid=(depth, n // tn),
        in_specs=[pl.BlockSpec((N_DEV, D), lambda l, j: (0, 0)), pl.BlockSpec((1, D, tn), lambda l, j: (l, 0, j)),
                  pl.BlockSpec((1, 1, tn), lambda l, j: (l, 0, j))],
        out_specs=[pl.BlockSpec((1, N_DEV, tn), lambda l, j: (l, 0, j)), pl.BlockSpec((N_DEV, D), lambda l, j: (0, 0))],
        out_shape=[jax.ShapeDtypeStruct((depth, N_DEV, n), F32), jax.ShapeDtypeStruct((N_DEV, D), F32)],
        compiler_params=_params("arbitrary", "arbitrary"),
    )(c_all, w_ada, b_loc)


def _sum_devices(gathered):
    n = gathered.shape[1]
    tn = _pick(n, (1408, 1024, 640, 512, 128))

    def body(g_ref, o_ref):
        acc = g_ref[0:8, :]
        for dev in range(1, N_DEV):
            acc = acc + g_ref[8 * dev:8 * dev + 8, :]
        o_ref[...] = acc

    return pl.pallas_call(
        body, name="sum_devices", grid=(n // tn,),
        in_specs=[pl.BlockSpec((8 * N_DEV, tn), lambda j: (0, j))], out_specs=pl.BlockSpec((8, tn), lambda j: (0, j)),
        out_shape=jax.ShapeDtypeStruct((8, n), F32), compiler_params=_params("parallel"),
    )(gathered)


def _place():
    x, y, c = lax.axis_index("x"), lax.axis_index("y"), lax.axis_index("c")
    chips = [(1 - x, y), (x, 1 - y), (1 - x, 1 - y)]
    return x, y, c, chips


def _allgather8(block, name, after=()):
    m_per, n = block.shape

    def body(x_ref, *rest):
        out_ref, send_sems, recv_sems, local_sem = rest[len(after):]
        x, y, c, chips = _place()
        me, sibling = (x, y, c), (x, y, 1 - c)

        def rows(px, py, pc):
            return out_ref.at[pl.ds((4 * px + 2 * py + pc) * m_per, m_per), :]

        def copy(k, blk, to, src=None):
            return pltpu.make_async_remote_copy(
                src_ref=rows(*blk) if src is None else src, dst_ref=rows(*blk),
                send_sem=send_sems.at[k], recv_sem=recv_sems.at[k], device_id=to, device_id_type=MESH)

        mine = pltpu.make_async_copy(x_ref, rows(*me), local_sem)
        mine.start()
        first = [copy(0, me, sibling, src=x_ref)]
        first += [copy(1 + j, me, (*chip, c), src=x_ref) for j, chip in enumerate(chips)]
        for cp in first:
            cp.start()
        passed = [copy(4 + j, (*chip, c), sibling) for j, chip in enumerate(chips)]
        for j, chip in enumerate(chips):
            copy(1 + j, (*chip, c), me).wait_recv()
            passed[j].start()
        copy(0, sibling, me).wait_recv()
        for j, chip in enumerate(chips):
            copy(4 + j, (*chip, 1 - c), me).wait_recv()
        for cp in first + passed:
            cp.wait_send()
        mine.wait()

    return pl.pallas_call(
        body, name=name, out_shape=jax.ShapeDtypeStruct((N_DEV * m_per, n), block.dtype),
        in_specs=[pl.BlockSpec(memory_space=pltpu.VMEM)] + [pl.BlockSpec(memory_space=pl.ANY)] * len(after),
        out_specs=pl.BlockSpec(memory_space=pltpu.VMEM),
        scratch_shapes=[pltpu.SemaphoreType.DMA((7,)), pltpu.SemaphoreType.DMA((7,)), pltpu.SemaphoreType.DMA],
        compiler_params=pltpu.CompilerParams(vmem_limit_bytes=V7X_VMEM_LIMIT),
    )(block, *after)


_SEM = pl.BlockSpec(memory_space=pltpu.SEMAPHORE)
_DATAFLOW = pltpu.SideEffectType.DATAFLOW_SIDE_EFFECTING


def _plan_copies(plan, refs, send_sems, recv_sems):
    return [pltpu.make_async_remote_copy(src_ref=src, dst_ref=dst, send_sem=send_sems.at[i], recv_sem=recv_sems.at[i],
                                         device_id=to, device_id_type=MESH) for i, (src, dst, to) in enumerate(plan(refs))]


class _Token(NamedTuple):
    after: jax.Array
    tie: jax.Array


def _after_operand(after):
    return after.after if isinstance(after, _Token) else after


def _copies_start(bufs, plan, n_copies, after, name):
    nb = len(bufs)

    def body(*refs):
        for cp in _plan_copies(plan, refs[:nb], refs[nb + 1], refs[nb + 2]):
            cp.start()
        for token in refs[-2:]:
            token[...] = jnp.zeros_like(token)

    sem = pltpu.SemaphoreType.DMA((n_copies,))
    vmem = pl.BlockSpec(memory_space=pltpu.VMEM)
    outs = pl.pallas_call(
        body, name=name,
        out_shape=(sem, sem, *[pltpu.HBM(b.shape, b.dtype) for b in bufs], jax.ShapeDtypeStruct((8, 128), F32),
                   jax.ShapeDtypeStruct((1, 1), F32)),
        in_specs=[_HBM] * nb + [pl.BlockSpec(memory_space=pl.ANY)],
        out_specs=(_SEM, _SEM, *[_HBM] * nb, vmem, vmem),
        input_output_aliases={i: 2 + i for i in range(nb)},
        compiler_params=pltpu.CompilerParams(has_side_effects=_DATAFLOW),
    )(*[pltpu.with_memory_space_constraint(b, pltpu.HBM) for b in bufs], _after_operand(after))
    return outs[0], outs[1], list(outs[2:2 + nb]), _Token(outs[-2], outs[-1])


def _copies_wait(started, plan, after, name):
    send_sems, recv_sems, bufs, _ = started
    nb = len(bufs)

    def body(*refs):
        for cp in _plan_copies(plan, refs[:nb], refs[nb], refs[nb + 1]):
            cp.wait_send()
            cp.wait_recv()

    return list(pl.pallas_call(
        body, name=name, out_shape=tuple(pltpu.HBM(b.shape, b.dtype) for b in bufs),
        in_specs=[_HBM] * nb + [_SEM, _SEM, pl.BlockSpec(memory_space=pl.ANY)], out_specs=tuple([_HBM] * nb),
        input_output_aliases={i: i for i in range(nb)},
        compiler_params=pltpu.CompilerParams(has_side_effects=_DATAFLOW),
    )(*bufs, send_sems, recv_sems, _after_operand(after)))


def _half_rows(ref, axis, c):
    half = ref.shape[axis] // 2
    return pl.ds(c * half, half)


def _plan_gather_ici(refs):
    n = len(refs) // 2
    x, y, c, chips = _place()
    out = []
    for a in range(n):
        rows = _half_rows(refs[a], 0, c)
        out += [(refs[a].at[rows], refs[n + a].at[2 * x + y, rows], (*chip, c)) for chip in chips]
        out.append((refs[a], refs[n + a].at[2 * x + y], (x, y, 1 - c)))
    return out


def _plan_gather_d2d(refs):
    x, y, c, chips = _place()
    out = []
    for ref in refs:
        rows = _half_rows(ref, 1, c)
        for px, py in chips:
            landed = ref.at[2 * px + py, rows]
            out.append((landed, landed, (x, y, 1 - c)))
    return out


def _plan_rs_sibling(refs):
    n = len(refs) // 2
    x, y, c, _ = _place()
    return [(refs[a].at[pl.ds(0, refs[a].shape[0]), _half_rows(refs[a], 1, 1 - c)], refs[n + a], (x, y, 1 - c)) for a in range(n)]


def _plan_rs_chips(refs):
    n = len(refs) // 2
    x, y, c, chips = _place()
    return [(refs[a].at[2 * px + py], refs[n + a].at[k], (px, py, c)) for a in range(n) for k, (px, py) in enumerate(chips)]


def _plan_rs_share(refs):
    x, y, c, _ = _place()
    return [(ref.at[_half_rows(ref, 0, c)], ref.at[_half_rows(ref, 0, c)], (x, y, 1 - c)) for ref in refs]


def _chip_sum(g, other, sel, name, blocked=True):
    nblk, half, cdim = other.shape
    tr = _pick(half, (512, 256, 128, 64) if nblk > 1 else (128, 64))
    per = half // tr

    def body(sel_ref, g_ref, t_ref, wire_ref, own_ref):
        total = g_ref[0] + t_ref[0]
        wire_ref[0] = total.astype(BF16)
        if blocked:
            @pl.when(pl.program_id(1) == sel_ref[1])
            def _():
                own_ref[...] = total
        else:
            own_ref[0] = total

    blk = pl.BlockSpec((1, tr, cdim), lambda i, p, sel_ref: (p, i, 0))
    own_spec = pl.BlockSpec((tr, cdim), lambda i, p, sel_ref: (i, 0)) if blocked else blk
    own_shape = jax.ShapeDtypeStruct((half, cdim) if blocked else other.shape, F32)
    return pl.pallas_call(
        body, name=name,
        grid_spec=pltpu.PrefetchScalarGridSpec(
            num_scalar_prefetch=1, grid=(per, nblk),
            in_specs=[pl.BlockSpec((1, tr, cdim), lambda i, p, sel_ref: (p, sel_ref[0] * per + i, 0)), blk],
            out_specs=[blk, own_spec]),
        out_shape=[jax.ShapeDtypeStruct(other.shape, BF16), own_shape],
        compiler_params=_params("parallel", "arbitrary"),
    )(sel, g, other)


def _final_sum(own, recv, sel, name):
    half, cdim = own.shape
    tr = _pick(half, (512, 256, 128, 64))
    per = half // tr

    def body(sel_ref, own_ref, r0_ref, r1_ref, r2_ref, o_ref):
        o_ref[...] = ((own_ref[...] + r0_ref[0].astype(F32)) + r1_ref[0].astype(F32)) + r2_ref[0].astype(F32)

    part = lambda k: pl.BlockSpec((1, tr, cdim), lambda i, sel_ref, k=k: (k, i, 0))
    return pl.pallas_call(
        body, name=name,
        grid_spec=pltpu.PrefetchScalarGridSpec(
            num_scalar_prefetch=1, grid=(per,),
            in_specs=[pl.BlockSpec((tr, cdim), lambda i, sel_ref: (i, 0)), part(0), part(1), part(2)],
            out_specs=pl.BlockSpec((tr, cdim), lambda i, sel_ref: (sel_ref[0] * per + i, 0))),
        out_shape=jax.ShapeDtypeStruct((2 * half, cdim), F32), compiler_params=_params("parallel"),
    )(sel, own, recv, recv, recv)


def _row(v):
    return v.reshape(1, -1)


_BR_WIDTHS = (A_WIDTH, POOL_WIDTH, CONV_WIDTH)


def _tie(v, token):
    return v if token is None else v + token.tie


def _no_hook(point, after, ready=None):
    return None


def _layer_fwd(x, w, mod, hook=_no_hook):
    s = x.shape[0]
    mod3 = mod.reshape(6, 1, D)
    h = _modnorm_fwd(x, _row(w["g_mix_pre"]), (mod3, 0), (mod3, 1), "mix_pre_fwd")
    hook("pre", h)
    z = _mm(h, w["w_all"], name="mm_in")
    qkv = z[:, Z_QKV:Z_PC].astype(BF16)
    fl = z[:, Z_FL:Z_COLS]
    cum = _cumf_fwd(fl, w["b_f_pad"])
    fr = cum[:, :HEADS].T.reshape(HEADS, s // ATT_K, ATT_K)
    br_a, lse = _attn_fwd(qkv, fr)
    br_b, br_c = _poolconv_fwd(z, w["w_pool_bd"], _tie(_row(w["pool_scale"]), hook("attn", lse)), w["conv_w"])
    hook("pool", br_b)
    pa, pb, pc, merged = _stacked_proj((br_a, br_b, br_c), w["w_branch"], _BR_WIDTHS, False, "mm_br_merge", gates=z, out_dtype=BF16)
    y = _mm(merged, w["w_out"], name="mm_out")
    x1, h2 = _post_pre_fwd(x, y, _row(w["g_mix_post"]), (mod3, 2), _row(w["g_ff_pre"]), (mod3, 3), (mod3, 4), "mix_post_ff_pre_fwd")
    a, r = _mm(h2, w["w_ff1"], b_split=N_CHIPS, epilogue=_relu2_fwd, out_dtype=(F32, BF16), name="mm_ff1")
    y2 = _mm(r, w["w_ff2"], name="mm_ff2")
    x2 = _post_fwd(x1, y2, _tie(_row(w["g_ff_post"]), hook("ff_post", y2)), (mod3, 5), "ff_post_fwd")
    hook("end", x2)
    saved = dict(x=x, h=h, z=z, qkv=qkv, fl=fl, fr=fr, lse=lse, br_a=br_a, br_b=br_b, br_c=br_c, pa=pa, pb=pb, pc=pc,
                 merged=merged, y=y, x1=x1, h2=h2, a=a, r=r, y2=y2)
    return x2, saved


def _layer_bwd(dx2, sv, w, mod, hook=_no_hook):
    s = dx2.shape[0]
    mod3 = mod.reshape(6, 1, D)
    dy2, sum_ff_post = _post_bwd(dx2, sv["y2"], _row(w["g_ff_post"]), (mod3, 5), "ff_post_bwd")
    (da,) = _mm(dy2, w["w_ff2"], tb=True, epilogue=_relu2_bwd, extras=(sv["a"],), out_dtype=(BF16,), name="mm_ff2_dx")
    d_w_ff2 = _mm(sv["r"], dy2, ta=True, name="mm_ff2_dw")
    dh2 = _mm(da, w["w_ff1"], tb=True, b_split=N_CHIPS, name="mm_ff1_dx")
    d_w_ff1 = _mm(sv["h2"], da, ta=True, out_split=N_CHIPS, name="mm_ff1_dw")
    g_ff_pre = _tie(_row(w["g_ff_pre"]), hook("ff_pre", dh2, dict(w_ff1=d_w_ff1, w_ff2=d_w_ff2)))
    dx1, dy, sum_mid = _pre_post_bwd(dh2, sv["x1"], dx2, g_ff_pre, (mod3, 4), sv["y"], _row(w["g_mix_post"]), (mod3, 2), "ff_pre_mix_post_bwd")
    sum_ff_pre, sum_mix_post = sum_mid, sum_mid[3:]
    dmerged = _mm(dy, w["w_out"], tb=True, name="mm_out_dx")
    d_w_out = _mm(sv["merged"], dy, ta=True, name="mm_out_dw")
    dz, dpa, dpb, dpc = _merge_bwd(dmerged, sv["z"], sv["pa"], sv["pb"], sv["pc"])
    dbr_a, dbr_b, dbr_c = _stacked_proj((dpa, dpb, dpc), w["w_branch"], _BR_WIDTHS, True, "mm_br_dx")
    d_w_branch = _stacked_dw((sv["br_a"], sv["br_b"], sv["br_c"]), (dpa, dpb, dpc), "mm_br_dw")

    dq, dk, dv, dfc, dfr = _attn_bwd(sv["qkv"], dbr_a, sv["br_a"], sv["lse"], sv["fr"])
    dcum = dfc + jnp.pad(dfr.reshape(HEADS, s).T, ((0, 0), (0, 128 - HEADS)))
    dfl, sum_bf = _cumf_bwd(dcum, sv["fl"], _tie(w["b_f_pad"], hook("cumf", dfc)))
    dpc_z, d_wbd, sum_ps, sum_cw = _poolconv_bwd(dbr_b, dbr_c, sv["z"], w["w_pool_bd"], _row(w["pool_scale"]), w["conv_w"])
    for at, part in ((Z_QKV, dq), (Z_QKV + A_WIDTH, dk), (Z_QKV + 2 * A_WIDTH, dv), (Z_PC, dpc_z), (Z_FL, dfl)):
        dz = lax.dynamic_update_slice(dz, part, (0, at))
    dh = _mm(dz, w["w_all"], tb=True, name="mm_in_dx")
    d_w_all = _mm(sv["h"], dz, ta=True, name="mm_in_dw")
    hook("mix_pre", dh)
    dx, sum_mix_pre = _modnorm_bwd(dh, sv["x"], dx1, _row(w["g_mix_pre"]), (mod3, 1), "mix_pre_bwd")

    dmod = jnp.stack([sum_mix_pre[0], sum_mix_pre[1], sum_mix_post[0], sum_ff_pre[0], sum_ff_pre[1], sum_ff_post[0]])
    d_w_in = d_w_all[None]
    d_w_pool = jnp.stack([d_wbd[64 * g:64 * g + 64, 64 * g:64 * g + 64] for g in range(4)])
    big = dict(w_in=d_w_in, w_branch=d_w_branch, w_out=d_w_out, w_ff1=d_w_ff1, w_ff2=d_w_ff2)
    small = dict(g_mix_pre=sum_mix_pre[2], g_mix_post=sum_mix_post[1], g_ff_pre=sum_ff_pre[2], g_ff_post=sum_ff_post[1],
                 b_f=sum_bf[0, :HEADS], w_pool=d_w_pool, pool_scale=sum_ps[0], conv_w=sum_cw[0:3])
    return dx, dmod, big, small


_QKV_END, _FL_END, _PC_END = 3 * A_WIDTH, 3 * A_WIDTH + HEADS, 3 * A_WIDTH + HEADS + POOL_WIDTH + 3 * CONV_WIDTH
_W_IN_GROUPS = ((_PC_END, IN_COLS, Z_GL), (0, _QKV_END, Z_QKV), (_FL_END, _PC_END, Z_PC), (_QKV_END, _FL_END, Z_FL))
_SHARD_COLS = IN_COLS // N_CHIPS


def _w_in_layout():
    out = []
    for p in range(N_CHIPS):
        pieces = []
        for lo, hi, at in _W_IN_GROUPS:
            a, b = max(lo, p * _SHARD_COLS), min(hi, (p + 1) * _SHARD_COLS)
            if a < b:
                pieces.append((at + a - lo, at + b - lo, a - p * _SHARD_COLS))
        pieces.sort()
        segs = []
        for z0, z1, _ in pieces:
            s, e = z0 // 128 * 128, -(-z1 // 128) * 128
            if segs and s <= segs[-1][1]:
                segs[-1] = (segs[-1][0], max(e, segs[-1][1]))
            else:
                segs.append((s, e))
        assert sum(e - s for s, e in segs) == Z_WINDOW
        out.append((pieces, segs))
    return out


Z_WINDOW = 1536


def _w_in_window(shard, p):
    pieces, segs = _w_in_layout()[p]
    cols = []
    for s, e in segs:
        at = s
        for z0, z1, src in pieces:
            if s <= z0 < e:
                if z0 > at:
                    cols.append(jnp.zeros((shard.shape[0], z0 - at), shard.dtype))
                cols.append(shard[:, src:src + z1 - z0])
                at = z1
        if e > at:
            cols.append(jnp.zeros((shard.shape[0], e - at), shard.dtype))
    return jnp.concatenate(cols, axis=1)


def _own_window(shard, chip):
    return lax.switch(chip, [lambda t, p=p: _w_in_window(t, p) for p in range(N_CHIPS)], shard)


def _w_all_from_windows(blocks):
    layout = _w_in_layout()
    bounds = sorted({edge for _, segs in layout for seg in segs for edge in seg})
    parts = []
    for lo, hi in zip(bounds[:-1], bounds[1:]):
        covering = []
        for p, (_, segs) in enumerate(layout):
            at = 0
            for s, e in segs:
                if s <= lo and hi <= e:
                    covering.append(blocks[p][:, at + lo - s:at + hi - s])
                at += e - s
        assert covering
        parts.append(covering[0] if len(covering) == 1 else covering[0] + covering[1])
    return jnp.concatenate(parts, axis=1)


def _w_in_shard(d_w_all, p):
    pieces = []
    for lo, hi, at in sorted(_W_IN_GROUPS):
        a, b = max(lo, p * _SHARD_COLS), min(hi, (p + 1) * _SHARD_COLS)
        if a < b:
            pieces.append(d_w_all[:, at + a - lo:at + b - lo])
    return jnp.concatenate(pieces, axis=1)


def _w_in_shards(d_w_all):
    return jnp.stack([_w_in_shard(d_w_all, p) for p in range(N_CHIPS)])


def _full_layer_weights(w_in_blocks, w_branch, w_out, w_ff1, w_ff2, g_mix_pre, g_mix_post, g_ff_pre, g_ff_post, b_f, w_pool, pool_scale, conv_w):
    w_all = None if w_in_blocks is None else _w_all_from_windows(w_in_blocks)
    wbd = (w_pool[:, :, None, :] * jnp.eye(4, dtype=F32)[:, None, :, None]).reshape(POOL_WIDTH, POOL_WIDTH)
    return dict(w_all=w_all, w_branch=w_branch, w_out=w_out, w_ff1=w_ff1, w_ff2=w_ff2, g_mix_pre=g_mix_pre, g_mix_post=g_mix_post,
                g_ff_pre=g_ff_pre, g_ff_post=g_ff_post, b_f_pad=jnp.pad(b_f, (0, 128 - HEADS)).reshape(1, 128), w_pool_bd=wbd,
                pool_scale=pool_scale, conv_w=conv_w)


class _NoComm:
    def layer_weights(self, l):
        raise NotImplementedError

    def fwd_hook(self, l):
        return _no_hook

    def bwd_hook(self, l):
        return _no_hook

    def grads_ready(self, l, big):
        return None


class _Layers(_NoComm):
    def __init__(self, layers):
        self.layers = layers

    def layer_weights(self, l):
        return self.layers[l]


def _local_step(x, target, mods, comm):
    saved, weights = [], []
    act = x
    for l in range(DEPTH):
        weights.append(comm.layer_weights(l))
        act, sv = _layer_fwd(act, weights[l], mods[l], comm.fwd_hook(l))
        saved.append(sv)
    dact, sq = _loss_head(act, target)
    loss = sq[0, 0] * (0.5 / D)
    dmods, bigs, smalls = [None] * DEPTH, [None] * DEPTH, [None] * DEPTH
    token = None
    for l in reversed(range(DEPTH)):
        dact, dmods[l], bigs[l], smalls[l] = _layer_bwd(dact, saved[l], weights[l], _tie(mods[l], token), comm.bwd_hook(l))
        token = comm.grads_ready(l, bigs[l])
    return loss, dact, jnp.stack(dmods), bigs, smalls


_BIG = ("w_in", "w_branch", "w_out", "w_ff1", "w_ff2")


class _GatherJob:
    def __init__(self, tag, shards, after):
        self.tag, self.n = tag, len(shards)
        lands = [lax.empty((N_CHIPS,) + s.shape, s.dtype) for s in shards]
        self.state = _copies_start(list(shards) + lands, _plan_gather_ici, 4 * self.n, after, "gather_ici_start_" + tag)
        self.token = self.state[3]

    def pass_on(self, after):
        bufs = _copies_wait(self.state, _plan_gather_ici, after, "gather_ici_wait_" + self.tag)
        self.state = _copies_start(bufs[self.n:], _plan_gather_d2d, 3 * self.n, bufs[0], "gather_d2d_start_" + self.tag)
        self.token = self.state[3]
        return self.token

    def done(self, after):
        return _copies_wait(self.state, _plan_gather_d2d, after, "gather_d2d_wait_" + self.tag)


class _ReduceJob:
    def __init__(self, tag, names, grads, sel, after):
        self.tag, self.names, self.n, self.sel = tag, names, len(names), sel
        lands = [lax.empty((g.shape[0], g.shape[1] // 2, g.shape[2]), F32) for g in grads]
        self.state = _copies_start(list(grads) + lands, _plan_rs_sibling, self.n, after, "rs_sibling_start_" + tag)
        self.token = self.state[3]

    def _chip_sum(self, name, g, other):
        if g.shape[0] == N_CHIPS:
            return _chip_sum(g, other, self.sel, "rs_chip_sum_" + name)
        wire, total = _chip_sum(g, other, self.sel, "rs_chip_sum_" + name, blocked=False)
        own = lax.switch(self.sel[1], [lambda t, p=p: _w_in_shard(t, p) for p in range(N_CHIPS)], total[0])
        return _w_in_shards(wire[0]), own

    def chip_sums(self, after):
        bufs = _copies_wait(self.state, _plan_rs_sibling, after, "rs_sibling_wait_" + self.tag)
        wires, self.owns = zip(*[self._chip_sum(name, bufs[i], bufs[self.n + i]) for i, name in enumerate(self.names)])
        lands = [lax.empty((3,) + w.shape[1:], BF16) for w in wires]
        self.state = _copies_start(list(wires) + lands, _plan_rs_chips, 3 * self.n, self.owns[0], "rs_chips_start_" + self.tag)
        self.token = self.state[3]
        return self.token

    def final_sums(self, after):
        bufs = _copies_wait(self.state, _plan_rs_chips, after, "rs_chips_wait_" + self.tag)
        sums = [_final_sum(self.owns[i], bufs[self.n + i], self.sel, "rs_final_" + name) for i, name in enumerate(self.names)]
        self.state = _copies_start(sums, _plan_rs_share, self.n, sums[0], "rs_share_start_" + self.tag)
        self.token = self.state[3]
        return self.token

    def done(self, after):
        return dict(zip(self.names, _copies_wait(self.state, _plan_rs_share, after, "rs_share_wait_" + self.tag)))


def _chip_blocks(g):
    return g if g.ndim == 3 else g.reshape(N_CHIPS, -1, g.shape[1])


class _StepComm(_NoComm):
    def __init__(self, big_weights, w_in0, sel, after):
        self.sel = sel
        self.small, self.grads, self.jobs = None, [dict() for _ in range(DEPTH)], {}
        self.jobs["in0"] = _GatherJob("in0", [w_in0], after)
        later = lax.optimization_barrier((tuple(big_weights), self.jobs["in0"].token.after))[0]
        self.jobs["rest0"] = _GatherJob("rest0", [w[0].astype(BF16) for w in later[1:]], self.jobs["in0"].token)
        layer1 = [w[1].astype(BF16) for w in later]
        self.jobs["all1"] = _GatherJob("all1", [_own_window(layer1[0], sel[1])] + layer1[1:], self.jobs["rest0"].token)

    def layer_weights(self, l):
        if l == 0:
            self.weights0 = _full_layer_weights(None, None, None, None, None, *self.small[0])
            return self.weights0
        g_in, g_br, g_out, g_f1, g_f2 = self.landed1
        return _full_layer_weights(g_in, g_br.reshape(D, D), g_out.reshape(D, D), g_f1, g_f2.reshape(D_FF, D), *self.small[1])

    def fwd_hook(self, l):
        if l != 0:
            return _no_hook

        def hook(point, after, ready=None):
            if point == "pre":
                job = self.jobs["in0"]
                started = after[:8, :128].astype(F32) + self.jobs["all1"].token.after
                self.weights0["w_all"] = _w_all_from_windows(job.done(job.pass_on(started))[0])
            if point == "attn":
                return self.jobs["rest0"].pass_on(after)
            if point == "ff_post":
                return self.jobs["all1"].pass_on(after)
            if point == "pool":
                g_br, g_out, g_f1, g_f2 = self.jobs["rest0"].done(after)
                self.weights0.update(w_branch=g_br.reshape(D, D), w_out=g_out.reshape(D, D), w_ff1=g_f1, w_ff2=g_f2.reshape(D_FF, D))
            if point == "end":
                self.landed1 = self.jobs["all1"].done(after)
            return None
        return hook

    def bwd_hook(self, l):
        if l != 0:
            return _no_hook

        def hook(point, after, ready=None):
            jobs = self.jobs
            if point == "ff_pre":
                token = jobs["rs1"].chip_sums(after)
                jobs["rs0_ff"] = _ReduceJob("0_ff", ("w_ff1", "w_ff2"), [_chip_blocks(ready[n]) for n in ("w_ff1", "w_ff2")], self.sel, token)
                return jobs["rs0_ff"].token
            if point == "cumf":
                return jobs["rs0_ff"].chip_sums(jobs["rs1"].final_sums(after))
            self.grads[1] = jobs["rs1"].done(after)
            return None
        return hook

    def grads_ready(self, l, big):
        if l == 1:
            self.jobs["rs1"] = _ReduceJob("1", _BIG, [_chip_blocks(big[n]) for n in _BIG], self.sel, self.sel)
            return self.jobs["rs1"].token
        names = ("w_in", "w_branch", "w_out")
        self.jobs["rs0_mix"] = _ReduceJob("0_mix", names, [_chip_blocks(big[n]) for n in names], self.sel, self.sel)
        return self.jobs["rs0_mix"].token

    def finish_sums(self, after):
        jobs = self.jobs
        token = jobs["rs0_mix"].chip_sums(after)
        return jobs["rs0_ff"].final_sums(token)

    def finish_ff(self, after):
        self.grads[0].update(self.jobs["rs0_ff"].done(after))

    def finish_mix(self, after):
        job = self.jobs["rs0_mix"]
        self.grads[0].update(job.done(job.final_sums(after)))


_SMALL = ("g_mix_pre", "g_mix_post", "g_ff_pre", "g_ff_post", "b_f", "w_pool", "pool_scale", "conv_w")


def _w_in_view(t):
    return t.reshape(DEPTH, D // 128, 128, _SHARD_COLS).transpose(3, 1, 0, 2).reshape(_SHARD_COLS * (D // 128) * DEPTH, 128)


def _w_in_unview(t):
    return t.reshape(_SHARD_COLS, D // 128, DEPTH, 128).transpose(2, 1, 3, 0).reshape(DEPTH, D, _SHARD_COLS)


def _pack(parts, rows=8):
    flat = jnp.concatenate([p.reshape(-1) for p in parts])
    width = -(-flat.shape[0] // (rows * 128)) * 128
    return jnp.pad(flat, (0, rows * width - flat.shape[0])).reshape(rows, width)


def _unpack(packed, like):
    flat = packed.reshape(-1)
    out, at = [], 0
    for ref in like:
        out.append(flat[at:at + ref.size].reshape(ref.shape))
        at += ref.size
    return out


def kernel(x, c, w_ada, b_ada, g_mix_pre, g_mix_post, g_ff_pre, g_ff_post, w_in, b_f, w_pool, pool_scale, conv_w, w_branch, w_out, w_ff1, w_ff2, loss_target, m_w_ada, m_b_ada, m_g_mix_pre, m_g_mix_post, m_g_ff_pre, m_g_ff_post, m_w_in, m_b_f, m_w_pool, m_pool_scale, m_conv_w, m_w_branch, m_w_out, m_w_ff1, m_w_ff2, v_w_ada, v_b_ada, v_g_mix_pre, v_g_mix_post, v_g_ff_pre, v_g_ff_post, v_w_in, v_b_f, v_w_pool, v_pool_scale, v_conv_w, v_w_branch, v_w_out, v_w_ff1, v_w_ff2):
    xi, yi, ci = lax.axis_index("x"), lax.axis_index("y"), lax.axis_index("c")
    chip = 2 * xi + yi
    dev = 2 * chip + ci
    n_ada = w_ada.shape[2]

    first = jnp.zeros((8, D + 384), F32).at[0, :D].set(c[0]).at[0, D:].set(conv_w.reshape(-1))
    w_in0 = _own_window(w_in[0].astype(BF16), chip)
    got = _allgather8(first, "gather_cond", after=(w_in0,)).reshape(N_DEV, 8, D + 384)[:, 0]
    c_all = got[:, :D]
    conv_full = got[0::2, D:].reshape(N_CHIPS, DEPTH, 3, CONV_WIDTH // N_CHIPS).transpose(1, 2, 0, 3).reshape(DEPTH, 3, CONV_WIDTH)

    b_loc = lax.dynamic_slice_in_dim(b_ada, chip * n_ada, n_ada, axis=1).reshape(DEPTH, 1, n_ada)
    mod_cols, silu_c = _ada_fwd(c_all, w_ada, b_loc)
    got = _allgather8(mod_cols.reshape(DEPTH * N_DEV, n_ada), "gather_mod").reshape(N_DEV, DEPTH, N_DEV, n_ada)[0::2]
    mod_all = got.transpose(1, 2, 0, 3).reshape(DEPTH, N_DEV, 6, D)
    mods = lax.dynamic_index_in_dim(mod_all, dev, axis=1, keepdims=False)

    comm = _StepComm((w_in, w_branch, w_out, w_ff1, w_ff2), w_in0, jnp.stack([ci, chip]).astype(jnp.int32), mods)
    comm.small = [(g_mix_pre[l], g_mix_post[l], g_ff_pre[l], g_ff_post[l], b_f[l], w_pool[l], pool_scale[l], conv_full[l]) for l in range(DEPTH)]
    loss_part, grad_x, dmods, bigs, smalls = _local_step(x[0], loss_target[0], mods, comm)

    small_parts = [smalls[l][name] for name in _SMALL for l in range(DEPTH)] + [loss_part.reshape(1)]
    packed = _tie(_pack([dmods] + small_parts), comm.jobs["rs0_mix"].token)
    gathered = _allgather8(packed, "gather_small")
    dmod_all = gathered.reshape(N_DEV, -1)[:, :dmods.size].reshape(N_DEV, DEPTH, 6 * D)
    summed = _unpack(_sum_devices(gathered), [dmods] + small_parts)
    grad_b_ada = summed[0].reshape(DEPTH, 6 * D)
    loss = summed[-1][0]
    small_grads = {name: jnp.stack(summed[1 + 2 * i:3 + 2 * i]) for i, name in enumerate(_SMALL)}
    small_grads["conv_w"] = lax.dynamic_slice_in_dim(small_grads["conv_w"], chip * (CONV_WIDTH // N_CHIPS), CONV_WIDTH // N_CHIPS, axis=2)

    dmod_loc = lax.dynamic_slice_in_dim(dmod_all.transpose(1, 0, 2), chip * n_ada, n_ada, axis=2)
    tail_token = comm.finish_sums(grad_b_ada)
    silu_pad = _tie(jnp.pad(silu_c, ((0, 128 - N_DEV), (0, 0))), tail_token)
    dmod_pad = jnp.pad(dmod_loc.transpose(1, 0, 2).reshape(N_DEV, DEPTH * n_ada), ((0, 128 - N_DEV), (0, 0)))
    grad_w_ada = _mm(silu_pad, dmod_pad, ta=True, out_split=DEPTH, name="mm_ada_dw")

    grads = dict(w_ada=grad_w_ada, b_ada=grad_b_ada, **small_grads)
    weights = dict(w_ada=w_ada, b_ada=b_ada, g_mix_pre=g_mix_pre, g_mix_post=g_mix_post, g_ff_pre=g_ff_pre, g_ff_post=g_ff_post, w_in=w_in,
                   b_f=b_f, w_pool=w_pool, pool_scale=pool_scale, conv_w=conv_w, w_branch=w_branch, w_out=w_out, w_ff1=w_ff1, w_ff2=w_ff2)
    m_in = dict(w_ada=m_w_ada, b_ada=m_b_ada, g_mix_pre=m_g_mix_pre, g_mix_post=m_g_mix_post, g_ff_pre=m_g_ff_pre, g_ff_post=m_g_ff_post,
                w_in=m_w_in, b_f=m_b_f, w_pool=m_w_pool, pool_scale=m_pool_scale, conv_w=m_conv_w, w_branch=m_w_branch, w_out=m_w_out,
                w_ff1=m_w_ff1, w_ff2=m_w_ff2)
    v_in = dict(w_ada=v_w_ada, b_ada=v_b_ada, g_mix_pre=v_g_mix_pre, g_mix_post=v_g_mix_post, g_ff_pre=v_g_ff_pre, g_ff_post=v_g_ff_post,
                w_in=v_w_in, b_f=v_b_f, w_pool=v_w_pool, pool_scale=v_pool_scale, conv_w=v_conv_w, w_branch=v_w_branch, w_out=v_w_out,
                w_ff1=v_w_ff1, w_ff2=v_w_ff2)
    order = ("w_ada", "b_ada", "g_mix_pre", "g_mix_post", "g_ff_pre", "g_ff_post", "w_in", "b_f", "w_pool", "pool_scale", "conv_w",
             "w_branch", "w_out", "w_ff1", "w_ff2")
    delta, new_m, new_v = {}, {}, {}
    tiny = ("b_ada",) + _SMALL
    tiny_g = [_tie(grads[tiny[0]], tail_token)] + [grads[name] for name in tiny[1:]]
    res = _adamw_many([weights[name] for name in tiny], tiny_g, [m_in[name] for name in tiny], [v_in[name] for name in tiny], "adamw_small")
    for out, vals in zip((delta, new_m, new_v), res):
        out.update(zip(tiny, vals))
    delta["w_ada"], new_m["w_ada"], new_v["w_ada"] = _adamw(w_ada, grad_w_ada, m_w_ada, v_w_ada, "adamw_w_ada")
    comm.finish_ff(delta["w_ada"][0, :8, :128] + delta["b_ada"][0, :128])
    for name in ("w_ff1", "w_ff2", "w_in", "w_branch", "w_out"):
        if name == "w_in":
            comm.finish_mix(delta["w_ff2"][0, :8, :128])
        g_layers = [comm.grads[l][name] for l in range(DEPTH)]
        if name == "w_in":
            g_view = lax.optimization_barrier(_w_in_view(jnp.stack(g_layers)))
            res = _adamw(_w_in_view(w_in), g_view, _w_in_view(m_w_in), _w_in_view(v_w_in), "adamw_w_in")
            grads[name], delta[name], new_m[name], new_v[name] = [_w_in_unview(t) for t in (g_view, *res)]
        else:
            delta[name], new_m[name], new_v[name], grads[name] = _adamw_layers(weights[name], g_layers, m_in[name], v_in[name], "adamw_" + name)

    return (loss, grad_x[None], *[grads[n] for n in order], *[delta[n] for n in order], *[new_m[n] for n in order],
            *[new_v[n] for n in order])
```

```python
from typing import NamedTuple

import jax
import jax.numpy as jnp
from jax import lax
from jax.experimental import pallas as pl
from jax.experimental.pallas import tpu as pltpu

F32 = jnp.float32
BF16 = jnp.bfloat16
MESH = pl.DeviceIdType.MESH

D = 1024
DEPTH = 2
HEADS = 8
HEAD_DIM = 64
A_WIDTH = 512
POOL_WIDTH = 256
CONV_WIDTH = 256
D_FF = 4096
IN_COLS = 5640
Z_GL, Z_QKV, Z_PC, Z_FL, Z_COLS = 0, 3072, 4608, 5632, 5760
RMS_EPS = 1e-6
NEG_INF = -1e30
ROW_TILE = 512
EW_ROWS = 256
N_CHIPS = 4
N_DEV = 8
V7X_VMEM_LIMIT = 48 * 1024 * 1024

ADAM_LR = 0.001
ADAM_B1 = 0.9
ADAM_B2 = 0.999
ADAM_EPS = 1e-08
ADAM_WD = 0.01
ADAM_STEP = 10

_HBM = pl.BlockSpec(memory_space=pltpu.HBM)


def _params(*sem):
    return pltpu.CompilerParams(dimension_semantics=sem, vmem_limit_bytes=V7X_VMEM_LIMIT)


def _pick(dim, cands):
    for cand in cands:
        if dim % cand == 0:
            return cand
    return dim


MM_TILE_BUDGET = 39 * 1024 * 1024


def _mm_tiles(m, n, k, k_unit, tn, a_size, b_size, out_size):
    for tk in (k_unit, 2048, 1152, 1024, 640, 512, 256, 128):
        if k_unit % tk:
            continue
        for tm in (2048, 1024, 512, 256, 128):
            if m % tm or ((m // tm) * (n // tn) < 2 and tm > 512):
                continue
            need = 2 * (tm * tk * a_size + tk * tn * b_size + tm * tn * out_size) + (0 if tk == k else 4 * tm * tn)
            if need <= MM_TILE_BUDGET and (tk == k_unit or tm >= 512):
                return tm, tk
    return 128, 128


def _mm(a, b, *, ta=False, tb=False, b_split=1, out_split=1, out_dtype=F32, epilogue=None, extras=(), name):
    (k, m) = a.shape if ta else a.shape[::-1]
    b_rows = b.shape[-2]
    b_cols = b.shape[-1] * b_split
    (n, k2) = (b_rows, b_cols) if tb else (b_cols, b_rows)
    assert k == k2, (a.shape, b.shape, ta, tb)
    n_unit = n // (out_split * (1 if tb else b_split))
    k_unit = k // (b_split if tb else 1)
    tn = _pick(n_unit, (1024, 1152, 768, 640, 512, 256, 128))
    tm, tk = _mm_tiles(m, n, k, k_unit, tn, a.dtype.itemsize, b.dtype.itemsize,
                       sum(jnp.dtype(dt).itemsize for dt in out_dtype) + 4 * len(extras) if epilogue else jnp.dtype(out_dtype).itemsize)
    nk = k // tk
    dims = (((0 if ta else 1,), (1 if tb else 0,)), ((), ()))

    def dot(a_ref, b_ref):
        b_val = b_ref[0] if b_split > 1 else b_ref[...]
        return lax.dot_general(a_ref[...].astype(BF16), b_val.astype(BF16), dims, preferred_element_type=F32)

    n_extra = len(extras)
    assert epilogue is None or out_split == 1

    def put(refs, val):
        if epilogue is not None:
            for o_ref, res in zip(refs[n_extra:], epilogue(val, *[r[...] for r in refs[:n_extra]])):
                o_ref[...] = res.astype(o_ref.dtype)
        elif out_split > 1:
            refs[0][0] = val.astype(refs[0].dtype)
        else:
            refs[0][...] = val.astype(refs[0].dtype)

    def body_single(a_ref, b_ref, *refs):
        put(refs, dot(a_ref, b_ref))

    def body_acc(a_ref, b_ref, *refs):
        kk = pl.program_id(2)
        acc_ref = refs[-1]

        @pl.when(kk == 0)
        def _():
            acc_ref[...] = jnp.zeros_like(acc_ref)

        acc_ref[...] += dot(a_ref, b_ref)

        @pl.when(kk == nk - 1)
        def _():
            put(refs[:-1], acc_ref[...])

    a_spec = pl.BlockSpec((tk, tm), lambda i, j, kk: (kk, i)) if ta else pl.BlockSpec((tm, tk), lambda i, j, kk: (i, kk))
    if b_split == 1:
        b_spec = pl.BlockSpec((tn, tk), lambda i, j, kk: (j, kk)) if tb else pl.BlockSpec((tk, tn), lambda i, j, kk: (kk, j))
    elif tb:
        per = k_unit // tk
        b_spec = pl.BlockSpec((1, tn, tk), lambda i, j, kk: (kk // per, j, kk % per))
    else:
        per = n // b_split // tn
        b_spec = pl.BlockSpec((1, tk, tn), lambda i, j, kk: (j // per, kk, j % per))
    if out_split == 1:
        o_spec = pl.BlockSpec((tm, tn), lambda i, j, kk: (i, j))
        o_shape = None if epilogue is not None else jax.ShapeDtypeStruct((m, n), out_dtype)
    else:
        per_o = n // out_split // tn
        o_spec = pl.BlockSpec((1, tm, tn), lambda i, j, kk: (j // per_o, i, j % per_o))
        o_shape = jax.ShapeDtypeStruct((out_split, m, n // out_split), out_dtype)
    if epilogue is not None:
        o_shape = [jax.ShapeDtypeStruct((m, n), dt) for dt in out_dtype]
        o_spec = [o_spec] * len(out_dtype)
    return pl.pallas_call(
        body_single if nk == 1 else body_acc, name=name, grid=(m // tm, n // tn, nk),
        in_specs=[a_spec, b_spec] + [pl.BlockSpec((tm, tn), lambda i, j, kk: (i, j))] * n_extra, out_specs=o_spec, out_shape=o_shape,
        scratch_shapes=[] if nk == 1 else [pltpu.VMEM((tm, tn), F32)],
        compiler_params=_params("parallel", "parallel", "arbitrary"),
    )(a, b, *extras)


def _stacked_proj(lhs, w, widths, tb, name, gates=None, out_dtype=F32):
    s = lhs[0].shape[0]
    n = w.shape[1]
    starts = [sum(widths[:i]) for i in range(len(widths))]
    n_lhs = len(lhs)
    n_gate = 0 if gates is None else n_lhs
    assert not (tb and n_gate)
    dims = (((1,), (1 if tb else 0,)), ((), ()))

    def body(*refs):
        w_ref = refs[n_lhs]
        g_refs, o_refs = refs[n_lhs + 1:n_lhs + 1 + n_gate], refs[n_lhs + 1 + n_gate:]
        merged = None
        for i, (a_ref, o_ref, start, width) in enumerate(zip(refs[:n_lhs], o_refs, starts, widths)):
            prod = lax.dot_general(a_ref[...].astype(BF16), w_ref[start:start + width, :].astype(BF16), dims,
                                   preferred_element_type=F32)
            o_ref[...] = prod.astype(o_ref.dtype)
            if n_gate:
                term = jax.nn.sigmoid(g_refs[i][...]) * prod
                merged = term if merged is None else merged + term
        if n_gate:
            o_refs[n_lhs][...] = merged.astype(BF16)

    in_cols = [n if tb else width for width in widths]
    out_cols = [width if tb else n for width in widths]
    assert [a.shape for a in lhs] == [(s, cols) for cols in in_cols], ([a.shape for a in lhs], widths, tb)
    return pl.pallas_call(
        body, name=name, grid=(s // ROW_TILE,),
        in_specs=[_row_spec(cols) for cols in in_cols] + [pl.BlockSpec(w.shape, lambda i: (0, 0))]
        + [_row_spec(n, i) for i in range(n_gate)],
        out_specs=[_row_spec(cols) for cols in out_cols] + [_row_spec(n)] * (n_gate > 0),
        out_shape=[jax.ShapeDtypeStruct((s, cols), out_dtype) for cols in out_cols] + [jax.ShapeDtypeStruct((s, n), BF16)] * (n_gate > 0),
        compiler_params=_params("parallel"),
    )(*lhs, w, *([gates] * n_gate))


def _stacked_dw(lhs, rhs, name):
    n = rhs[0].shape[1]
    tr = min(a.shape[1] for a in lhs)
    tn = _pick(n, (1024, 512, 256, 128))
    counts = [a.shape[1] // tr for a in lhs]
    starts = [sum(counts[:i]) for i in range(len(lhs))]
    assert all(a.shape[1] == c * tr for a, c in zip(lhs, counts))
    n_lhs = len(lhs)

    def body(*refs):
        o_ref = refs[-1]
        j = pl.program_id(1)
        for a_ref, b_ref, start, count in zip(refs[:n_lhs], refs[n_lhs:-1], starts, counts):
            @pl.when((j >= start) & (j < start + count))
            def _():
                o_ref[...] = lax.dot_general(a_ref[...].astype(BF16), b_ref[...].astype(BF16), (((0,), (0,)), ((), ())),
                                             preferred_element_type=F32)

    def lhs_spec(a, start, count):
        return pl.BlockSpec((a.shape[0], tr), lambda c, j: (0, jnp.clip(j - start, 0, count - 1)))

    return pl.pallas_call(
        body, name=name, grid=(n // tn, sum(counts)),
        in_specs=[lhs_spec(a, st, ct) for a, st, ct in zip(lhs, starts, counts)]
        + [pl.BlockSpec((b.shape[0], tn), lambda c, j: (0, c)) for b in rhs],
        out_specs=pl.BlockSpec((tr, tn), lambda c, j: (j, c)),
        out_shape=jax.ShapeDtypeStruct((sum(counts) * tr, n), F32),
        compiler_params=_params("parallel", "parallel"),
    )(*lhs, *rhs)


def _ew(fn, ins, out_dtypes, name, tc=None):
    shape = ins[0].shape
    lead, (rows, cols) = shape[:-2], shape[-2:]
    tc = cols if tc is None else tc
    if tc > 1024:
        tr = _pick(rows, (EW_ROWS, 128, 8))
    elif tc > 128:
        tr = _pick(rows, (2 * EW_ROWS, EW_ROWS, 128, 8))
    else:
        tr = _pick(rows, (4096, 2256, 2048, 1024, EW_ROWS, 8))
    n_in = len(ins)

    def body(*refs):
        res = fn(*[r[...] for r in refs[:n_in]])
        for o_ref, val in zip(refs[n_in:], res):
            o_ref[...] = val.astype(o_ref.dtype)

    if lead:
        spec = pl.BlockSpec((None, tr, tc), lambda l, i, j: (l, i, j))
    else:
        spec = pl.BlockSpec((tr, tc), lambda i, j: (i, j))
    return pl.pallas_call(
        body, name=name, grid=lead + (rows // tr, cols // tc),
        in_specs=[spec] * n_in, out_specs=[spec] * len(out_dtypes),
        out_shape=[jax.ShapeDtypeStruct(shape, dt) for dt in out_dtypes],
        compiler_params=_params(*(["parallel"] * (len(lead) + 2))),
    )(*ins)


def _relu2_fwd(a):
    r = jnp.maximum(a, 0.0)
    return a, r * r


def _relu2_bwd(dr, a):
    return (dr * (2.0 * jnp.maximum(a, 0.0)),)


def _adamw_math(w, g, m, v):
    m = ADAM_B1 * m + (1.0 - ADAM_B1) * g
    v = ADAM_B2 * v + (1.0 - ADAM_B2) * (g * g)
    m_hat = m / (1.0 - ADAM_B1 ** ADAM_STEP)
    v_hat = v / (1.0 - ADAM_B2 ** ADAM_STEP)
    delta = -ADAM_LR * (m_hat / (jnp.sqrt(v_hat) + ADAM_EPS) + ADAM_WD * w)
    return delta, m, v


def _adamw(w, g, m, v, name):
    return _ew(_adamw_math, [w, g, m, v], [F32, F32, F32], name)


def _adamw_layers(w, g_layers, m, v, name):
    depth, rows, cols = w.shape
    tr = _pick(rows, (2 * EW_ROWS, EW_ROWS, 128, 8)) if cols <= 1024 else _pick(rows, (EW_ROWS, 128, 8))

    def body(w_ref, *refs):
        g_refs, (m_ref, v_ref, d_ref, mo_ref, vo_ref, go_ref) = refs[:depth], refs[depth:]
        layer = pl.program_id(0)
        g = g_refs[0][...]
        for l in range(1, depth):
            g = jnp.where(layer == l, g_refs[l][...], g)
        d_ref[...], mo_ref[...], vo_ref[...] = _adamw_math(w_ref[...], g, m_ref[...], v_ref[...])
        go_ref[...] = g

    spec = pl.BlockSpec((None, tr, cols), lambda l, i: (l, i, 0))
    g_specs = [pl.BlockSpec((tr, cols), lambda l, i, k=k: (jnp.where(l == k, i, 0), 0)) for k in range(depth)]
    return pl.pallas_call(
        body, name=name, grid=(depth, rows // tr),
        in_specs=[spec] + g_specs + [spec, spec], out_specs=[spec] * 4,
        out_shape=[jax.ShapeDtypeStruct(w.shape, F32)] * 4, compiler_params=_params("arbitrary", "arbitrary"),
    )(w, *g_layers, m, v)


def _adamw_many(ws, gs, ms, vs, name):
    n = len(ws)

    def body(*refs):
        for i in range(n):
            res = _adamw_math(*[refs[k * n + i][...] for k in range(4)])
            for k in range(3):
                refs[(4 + k) * n + i][...] = res[k]

    outs = pl.pallas_call(
        body, name=name, out_shape=[jax.ShapeDtypeStruct(w.shape, F32) for w in ws] * 3,
        compiler_params=pltpu.CompilerParams(vmem_limit_bytes=V7X_VMEM_LIMIT),
    )(*ws, *gs, *ms, *vs)
    return outs[:n], outs[n:2 * n], outs[2 * n:]


def _row_spec(cols, block=0):
    return pl.BlockSpec((ROW_TILE, cols), lambda i, block=block: (i, block))


def _vec_spec(cols):
    return pl.BlockSpec((1, cols), lambda i: (0, 0))


def _vec_args(*vecs):
    arrays = [v[0] if isinstance(v, tuple) else v for v in vecs]
    specs = [pl.BlockSpec((None, 1, D), lambda i, row=v[1]: (row, 0, 0)) if isinstance(v, tuple) else _vec_spec(D) for v in vecs]
    return arrays, specs


def _sum_spec(cols):
    return pl.BlockSpec((8, cols), lambda i: (0, 0))


def _rstd(x):
    return lax.rsqrt(jnp.mean(x * x, axis=-1, keepdims=True) + RMS_EPS)


def _modnorm_fwd(x, g, shift, scale, name):
    s = x.shape[0]

    def body(x_ref, g_ref, sh_ref, sc_ref, h_ref):
        xv = x_ref[...]
        n = xv * _rstd(xv)
        h_ref[...] = ((n * g_ref[...]) * (1.0 + sc_ref[...]) + sh_ref[...]).astype(BF16)

    vecs, vec_specs = _vec_args(g, shift, scale)
    return pl.pallas_call(
        body, name=name, grid=(s // ROW_TILE,),
        in_specs=[_row_spec(D)] + vec_specs, out_specs=_row_spec(D),
        out_shape=jax.ShapeDtypeStruct((s, D), BF16), compiler_params=_params("parallel"),
    )(x, *vecs)


def _post_fwd(x, y, g, gate, name):
    s = x.shape[0]

    def body(x_ref, y_ref, g_ref, gate_ref, o_ref):
        yv = y_ref[...]
        o_ref[...] = x_ref[...] + gate_ref[...] * ((yv * _rstd(yv)) * g_ref[...])

    vecs, vec_specs = _vec_args(g, gate)
    return pl.pallas_call(
        body, name=name, grid=(s // ROW_TILE,),
        in_specs=[_row_spec(D), _row_spec(D)] + vec_specs, out_specs=_row_spec(D),
        out_shape=jax.ShapeDtypeStruct((s, D), F32), compiler_params=_params("parallel"),
    )(x, y, *vecs)


def _post_bwd(dxo, y, g, gate, name):
    s = dxo.shape[0]

    def body(d_ref, y_ref, g_ref, gate_ref, dy_ref, sum_ref):
        @pl.when(pl.program_id(0) == 0)
        def _():
            sum_ref[...] = jnp.zeros_like(sum_ref)

        dv, yv = d_ref[...], y_ref[...]
        r = _rstd(yv)
        n = yv * r
        sum_ref[0:1, :] += jnp.sum(dv * (n * g_ref[...]), axis=0, keepdims=True)
        sum_ref[1:2, :] += jnp.sum((dv * gate_ref[...]) * n, axis=0, keepdims=True)
        dn = (dv * gate_ref[...]) * g_ref[...]
        dy_ref[...] = (r * (dn - n * jnp.mean(dn * n, axis=-1, keepdims=True))).astype(BF16)

    vecs, vec_specs = _vec_args(g, gate)
    return pl.pallas_call(
        body, name=name, grid=(s // ROW_TILE,),
        in_specs=[_row_spec(D), _row_spec(D)] + vec_specs,
        out_specs=[_row_spec(D), _sum_spec(D)],
        out_shape=[jax.ShapeDtypeStruct((s, D), BF16), jax.ShapeDtypeStruct((8, D), F32)],
        compiler_params=_params("arbitrary"),
    )(dxo, y, *vecs)


def _modnorm_bwd(dh, x, dxo, g, scale, name):
    s = dh.shape[0]

    def body(dh_ref, x_ref, d_ref, g_ref, sc_ref, dx_ref, sum_ref):
        @pl.when(pl.program_id(0) == 0)
        def _():
            sum_ref[...] = jnp.zeros_like(sum_ref)

        dhv, xv = dh_ref[...], x_ref[...]
        r = _rstd(xv)
        n = xv * r
        one_sc = 1.0 + sc_ref[...]
        sum_ref[0:1, :] += jnp.sum(dhv, axis=0, keepdims=True)
        sum_ref[1:2, :] += jnp.sum(dhv * (n * g_ref[...]), axis=0, keepdims=True)
        sum_ref[2:3, :] += jnp.sum((dhv * one_sc) * n, axis=0, keepdims=True)
        dn = (dhv * one_sc) * g_ref[...]
        dx_ref[...] = d_ref[...] + r * (dn - n * jnp.mean(dn * n, axis=-1, keepdims=True))

    vecs, vec_specs = _vec_args(g, scale)
    return pl.pallas_call(
        body, name=name, grid=(s // ROW_TILE,),
        in_specs=[_row_spec(D), _row_spec(D), _row_spec(D)] + vec_specs,
        out_specs=[_row_spec(D), _sum_spec(D)],
        out_shape=[jax.ShapeDtypeStruct((s, D), F32), jax.ShapeDtypeStruct((8, D), F32)],
        compiler_params=_params("arbitrary"),
    )(dh, x, dxo, *vecs)


def _post_pre_fwd(x, y, g_post, gate, g_pre, shift, scale, name):
    s = x.shape[0]

    def body(x_ref, y_ref, gp_ref, gate_ref, g_ref, sh_ref, sc_ref, o_ref, h_ref):
        yv = y_ref[...]
        xo = x_ref[...] + gate_ref[...] * ((yv * _rstd(yv)) * gp_ref[...])
        o_ref[...] = xo
        h_ref[...] = (((xo * _rstd(xo)) * g_ref[...]) * (1.0 + sc_ref[...]) + sh_ref[...]).astype(BF16)

    vecs, vec_specs = _vec_args(g_post, gate, g_pre, shift, scale)
    return pl.pallas_call(
        body, name=name, grid=(s // ROW_TILE,),
        in_specs=[_row_spec(D), _row_spec(D)] + vec_specs, out_specs=[_row_spec(D), _row_spec(D)],
        out_shape=[jax.ShapeDtypeStruct((s, D), F32), jax.ShapeDtypeStruct((s, D), BF16)], compiler_params=_params("parallel"),
    )(x, y, *vecs)


def _pre_post_bwd(dh, x, dxo, g_pre, scale, y, g_post, gate, name):
    s = dh.shape[0]

    def body(dh_ref, x_ref, d_ref, y_ref, g_ref, sc_ref, gp_ref, gate_ref, dx_ref, dy_ref, sum_ref):
        @pl.when(pl.program_id(0) == 0)
        def _():
            sum_ref[...] = jnp.zeros_like(sum_ref)

        dhv, xv = dh_ref[...], x_ref[...]
        r = _rstd(xv)
        n = xv * r
        one_sc = 1.0 + sc_ref[...]
        sum_ref[0:1, :] += jnp.sum(dhv, axis=0, keepdims=True)
        sum_ref[1:2, :] += jnp.sum(dhv * (n * g_ref[...]), axis=0, keepdims=True)
        sum_ref[2:3, :] += jnp.sum((dhv * one_sc) * n, axis=0, keepdims=True)
        dn = (dhv * one_sc) * g_ref[...]
        dv = d_ref[...] + r * (dn - n * jnp.mean(dn * n, axis=-1, keepdims=True))
        dx_ref[...] = dv

        yv = y_ref[...]
        ry = _rstd(yv)
        ny = yv * ry
        sum_ref[3:4, :] += jnp.sum(dv * (ny * gp_ref[...]), axis=0, keepdims=True)
        sum_ref[4:5, :] += jnp.sum((dv * gate_ref[...]) * ny, axis=0, keepdims=True)
        dny = (dv * gate_ref[...]) * gp_ref[...]
        dy_ref[...] = (ry * (dny - ny * jnp.mean(dny * ny, axis=-1, keepdims=True))).astype(BF16)

    vecs, vec_specs = _vec_args(g_pre, scale, g_post, gate)
    return pl.pallas_call(
        body, name=name, grid=(s // ROW_TILE,),
        in_specs=[_row_spec(D)] * 4 + vec_specs,
        out_specs=[_row_spec(D), _row_spec(D), _sum_spec(D)],
        out_shape=[jax.ShapeDtypeStruct((s, D), F32), jax.ShapeDtypeStruct((s, D), BF16), jax.ShapeDtypeStruct((8, D), F32)],
        compiler_params=_params("arbitrary"),
    )(dh, x, dxo, y, *vecs)


def _loss_head(y, target):
    s = y.shape[0]

    def body(y_ref, t_ref, dy_ref, sum_ref):
        @pl.when(pl.program_id(0) == 0)
        def _():
            sum_ref[...] = jnp.zeros_like(sum_ref)

        err = y_ref[...] - t_ref[...]
        dy_ref[...] = err * (1.0 / D)
        sum_ref[...] += jnp.sum(err * err)

    return pl.pallas_call(
        body, name="loss_head", grid=(s // ROW_TILE,),
        in_specs=[_row_spec(D), _row_spec(D)],
        out_specs=[_row_spec(D), pl.BlockSpec((8, 128), lambda i: (0, 0))],
        out_shape=[jax.ShapeDtypeStruct((s, D), F32), jax.ShapeDtypeStruct((8, 128), F32)],
        compiler_params=_params("arbitrary"),
    )(y, target)


def _merge_bwd(dm, z, pa, pb, pc):
    s = z.shape[0]

    def body(dm_ref, g0_ref, g1_ref, g2_ref, pa_ref, pb_ref, pc_ref, dgl_ref, da_ref, db_ref, dc_ref):
        dmv = dm_ref[...]
        for i, (g_ref, p_ref, d_ref) in enumerate(((g0_ref, pa_ref, da_ref), (g1_ref, pb_ref, db_ref), (g2_ref, pc_ref, dc_ref))):
            gate = jax.nn.sigmoid(g_ref[...])
            dgl_ref[:, i * D:(i + 1) * D] = ((dmv * p_ref[...]) * (gate * (1.0 - gate))).astype(BF16)
            d_ref[...] = (dmv * gate).astype(BF16)

    return pl.pallas_call(
        body, name="merge_bwd", grid=(s // ROW_TILE,),
        in_specs=[_row_spec(D), _row_spec(D, 0), _row_spec(D, 1), _row_spec(D, 2), _row_spec(D), _row_spec(D), _row_spec(D)],
        out_specs=[_row_spec(3 * D), _row_spec(D), _row_spec(D), _row_spec(D)],
        out_shape=[jax.ShapeDtypeStruct((s, Z_COLS), BF16)] + [jax.ShapeDtypeStruct((s, D), BF16)] * 3,
        compiler_params=_params("parallel"),
    )(dm, z, z, z, pa, pb, pc)


def _shift_down(v, n):
    row = lax.broadcasted_iota(jnp.int32, v.shape, 0)
    return jnp.where(row >= n, pltpu.roll(v, n, axis=0), 0.0)


def _shift_up(v, n):
    s = v.shape[0]
    row = lax.broadcasted_iota(jnp.int32, v.shape, 0)
    return jnp.where(row < s - n, pltpu.roll(v, s - n, axis=0), 0.0)


def _log_sigmoid(v):
    return jnp.minimum(v, 0.0) - jnp.log1p(jnp.exp(-jnp.abs(v)))


def _cumf_fwd(fl, bias):
    s = fl.shape[0]

    def body(fl_ref, b_ref, o_ref):
        acc = _log_sigmoid(fl_ref[...] + b_ref[...])
        step = 1
        while step < s:
            acc = acc + _shift_down(acc, step)
            step *= 2
        o_ref[...] = acc

    return pl.pallas_call(body, name="cumf_fwd", out_shape=jax.ShapeDtypeStruct((s, 128), F32),
                          compiler_params=pltpu.CompilerParams(vmem_limit_bytes=V7X_VMEM_LIMIT))(fl, bias)


def _cumf_bwd(dcum, fl, bias):
    s = fl.shape[0]

    def body(d_ref, fl_ref, b_ref, dfl_ref, db_ref):
        acc = d_ref[...]
        step = 1
        while step < s:
            acc = acc + _shift_up(acc, step)
            step *= 2
        dfl = acc * jax.nn.sigmoid(-(fl_ref[...] + b_ref[...]))
        dfl_ref[...] = dfl.astype(BF16)
        db_ref[...] = jnp.broadcast_to(jnp.sum(dfl, axis=0, keepdims=True), (8, 128))

    return pl.pallas_call(
        body, name="cumf_bwd",
        out_shape=[jax.ShapeDtypeStruct((s, 128), BF16), jax.ShapeDtypeStruct((8, 128), F32)],
        compiler_params=pltpu.CompilerParams(vmem_limit_bytes=V7X_VMEM_LIMIT))(dcum, fl, bias)


def _pool_windows(v, shift):
    s2 = v + shift(v, 1)
    s4 = s2 + shift(s2, 2)
    s8 = s4 + shift(s4, 4)
    s16 = s8 + shift(s8, 8)
    group = lax.broadcasted_iota(jnp.int32, v.shape, 1) // 64
    return jnp.where(group == 0, s2, jnp.where(group == 1, s4, jnp.where(group == 2, s8, s16)))


def _pool_count(shape):
    group = lax.broadcasted_iota(jnp.int32, shape, 1) // 64
    window = jnp.where(group == 0, 2.0, jnp.where(group == 1, 4.0, jnp.where(group == 2, 8.0, 16.0)))
    t1 = (lax.broadcasted_iota(jnp.int32, shape, 0) + 1).astype(F32)
    return jnp.minimum(t1, window)


def _pc_specs(s):
    zcol = lambda blk: pl.BlockSpec((s, 256), lambda i, blk=blk: (0, blk))
    first = Z_PC // 256
    return [zcol(first), zcol(first + 1), zcol(first + 2), zcol(first + 3),
            pl.BlockSpec((256, 256), lambda i: (0, 0)), pl.BlockSpec((1, 256), lambda i: (0, 0)),
            pl.BlockSpec((3, 256), lambda i: (0, 0))]


def _poolconv_fwd(z, wbd, pscale, convw):
    s = z.shape[0]

    def body(pu_ref, ch_ref, cb_ref, cc_ref, w_ref, ps_ref, cw_ref, yb_ref, yc_ref):
        u = pu_ref[...]
        p = _pool_windows(u, _shift_down) / _pool_count(u.shape) - u
        yb = jnp.dot(p.astype(BF16), w_ref[...].astype(BF16), preferred_element_type=F32) * ps_ref[...]
        yb_ref[...] = yb.astype(BF16)
        uc = cc_ref[...] * ch_ref[...]
        cw = cw_ref[...]
        conv = cw[0:1, :] * _shift_down(uc, 2) + cw[1:2, :] * _shift_down(uc, 1) + cw[2:3, :] * uc
        yc_ref[...] = (cb_ref[...] * conv).astype(BF16)

    out = pl.BlockSpec((s, 256), lambda i: (0, 0))
    return pl.pallas_call(
        body, name="poolconv_fwd", grid=(1,), in_specs=_pc_specs(s), out_specs=[out, out],
        out_shape=[jax.ShapeDtypeStruct((s, 256), BF16)] * 2, compiler_params=_params("arbitrary"),
    )(z, z, z, z, wbd, pscale, convw)


def _poolconv_bwd(dyb, dyc, z, wbd, pscale, convw):
    s = z.shape[0]

    def body(dyb_ref, dyc_ref, pu_ref, ch_ref, cb_ref, cc_ref, w_ref, ps_ref, cw_ref, dz_ref, dw_ref, dps_ref, dcw_ref):
        u = pu_ref[...]
        count = _pool_count(u.shape)
        p = (_pool_windows(u, _shift_down) / count - u).astype(BF16)
        wb = w_ref[...].astype(BF16)
        dyb_v = dyb_ref[...]
        pw = jnp.dot(p, wb, preferred_element_type=F32)
        dps_ref[...] = jnp.broadcast_to(jnp.sum(dyb_v * pw, axis=0, keepdims=True), (8, 256))
        dys = (dyb_v * ps_ref[...]).astype(BF16)
        dp = lax.dot_general(dys, wb, (((1,), (1,)), ((), ())), preferred_element_type=F32)
        dw_ref[...] = lax.dot_general(p, dys, (((0,), (0,)), ((), ())), preferred_element_type=F32)
        dz_ref[:, 0:256] = (_pool_windows(dp / count, _shift_up) - dp).astype(BF16)

        ch, cb, cc = ch_ref[...], cb_ref[...], cc_ref[...]
        uc = cc * ch
        cw = cw_ref[...]
        u2, u1 = _shift_down(uc, 2), _shift_down(uc, 1)
        conv = cw[0:1, :] * u2 + cw[1:2, :] * u1 + cw[2:3, :] * uc
        dyc_v = dyc_ref[...]
        dconv = dyc_v * cb
        du = cw[0:1, :] * _shift_up(dconv, 2) + cw[1:2, :] * _shift_up(dconv, 1) + cw[2:3, :] * dconv
        dz_ref[:, 256:512] = (du * cc).astype(BF16)
        dz_ref[:, 512:768] = (dyc_v * conv).astype(BF16)
        dz_ref[:, 768:1024] = (du * ch).astype(BF16)
        dcw_ref[...] = jnp.zeros_like(dcw_ref)
        dcw_ref[0:1, :] = jnp.sum(dconv * u2, axis=0, keepdims=True)
        dcw_ref[1:2, :] = jnp.sum(dconv * u1, axis=0, keepdims=True)
        dcw_ref[2:3, :] = jnp.sum(dconv * uc, axis=0, keepdims=True)

    blk = lambda r, c: pl.BlockSpec((r, c), lambda i: (0, 0))
    return pl.pallas_call(
        body, name="poolconv_bwd", grid=(1,),
        in_specs=[blk(s, 256), blk(s, 256)] + _pc_specs(s),
        out_specs=[blk(s, 1024), blk(256, 256), blk(8, 256), blk(8, 256)],
        out_shape=[jax.ShapeDtypeStruct((s, 1024), BF16), jax.ShapeDtypeStruct((256, 256), F32),
                   jax.ShapeDtypeStruct((8, 256), F32), jax.ShapeDtypeStruct((8, 256), F32)],
        compiler_params=_params("arbitrary"),
    )(dyb, dyc, z, z, z, z, wbd, pscale, convw)


_NT = (((1,), (1,)), ((), ()))
_TN = (((0,), (0,)), ((), ()))


ATT_Q, ATT_K = 256, 256
ATT_HEADS_BWD = 8
ATT_HEADS = 8


def _att_logits(q, k, fr, q0, k0, masked):
    logits = lax.dot_general(q, k, _NT, preferred_element_type=F32) - fr
    if not masked:
        return logits
    row = q0 + lax.broadcasted_iota(jnp.int32, logits.shape, 0)
    col = k0 + lax.broadcasted_iota(jnp.int32, logits.shape, 1)
    return jnp.where(row >= col, logits, NEG_INF)


def _causal_sweep(step, qi, init):
    n_full = (qi * ATT_Q) // ATT_K
    carry = lax.fori_loop(0, n_full, lambda j, carry: step(j, carry, False), init)
    return step(n_full, carry, True)


HEAD_PAIRS = HEADS // 2


def _lane_pick(v, lane, idx):
    return jnp.sum(jnp.where(lane == idx, v, 0.0), axis=-1, keepdims=True)


def _lane_put(lane, idx, col):
    return jnp.where(lane == idx, col, 0.0)


def _split_heads(v, low):
    zero = jnp.zeros_like(v)
    return jnp.where(low, v, zero), jnp.where(low, zero, v)


def _attn_fwd(qkv, fr):
    s = qkv.shape[0]
    nk = s // ATT_K
    width = ATT_HEADS * HEAD_DIM
    groups = HEADS // ATT_HEADS

    def body(q_ref, k_ref, v_ref, fr_ref, o_ref, lse_ref):
        qi, grp = pl.program_id(0), pl.program_id(1)
        lane = lax.broadcasted_iota(jnp.int32, (ATT_Q, 128), 1)
        low = lane < HEAD_DIM
        qs = []
        for pr in range(ATT_HEADS // 2):
            qs += _split_heads(q_ref[:, 128 * pr:128 * (pr + 1)] * (HEAD_DIM ** -0.5), low)

        def step(j, carry, masked):
            k0 = pl.multiple_of(j * ATT_K, ATT_K)
            out = []
            for h in range(ATT_HEADS):
                cols = slice(128 * (h // 2), 128 * (h // 2 + 1))
                m, l, acc = carry[h]
                logits = _att_logits(qs[h], k_ref[pl.ds(k0, ATT_K), cols], fr_ref[h, pl.ds(j, 1), :], qi * ATT_Q, k0, masked)
                m_new = jnp.maximum(m, jnp.max(logits, axis=-1, keepdims=True))
                p = jnp.exp(logits - m_new)
                alpha = jnp.exp(m - m_new)
                l = alpha * l + jnp.sum(p, axis=-1, keepdims=True)
                acc = alpha * acc + jnp.dot(p.astype(BF16), v_ref[pl.ds(k0, ATT_K), cols], preferred_element_type=F32)
                out.append((m_new, l, acc))
            return tuple(out)

        one = (jnp.full((ATT_Q, 1), NEG_INF, F32), jnp.zeros((ATT_Q, 1), F32), jnp.zeros((ATT_Q, 128), F32))
        done = _causal_sweep(step, qi, (one,) * ATT_HEADS)

        @pl.when(grp == 0)
        def _():
            lse_ref[...] = jnp.zeros_like(lse_ref)

        lse = jnp.zeros((ATT_Q, 128), F32)
        for pr in range(ATT_HEADS // 2):
            (m0, l0, acc0), (m1, l1, acc1) = done[2 * pr], done[2 * pr + 1]
            o_ref[:, 128 * pr:128 * (pr + 1)] = jnp.where(low, acc0 / l0, acc1 / l1)
            head = ATT_HEADS * grp + 2 * pr
            lse = lse + _lane_put(lane, head, m0 + jnp.log(l0)) + _lane_put(lane, head + 1, m1 + jnp.log(l1))
        lse_ref[...] += lse

    return pl.pallas_call(
        body, name="attn_fwd", grid=(s // ATT_Q, groups),
        in_specs=[pl.BlockSpec((ATT_Q, width), lambda i, g: (i, g)),
                  pl.BlockSpec((s, width), lambda i, g: (0, groups + g)),
                  pl.BlockSpec((s, width), lambda i, g: (0, 2 * groups + g)),
                  pl.BlockSpec((ATT_HEADS, nk, ATT_K), lambda i, g: (g, 0, 0))],
        out_specs=[pl.BlockSpec((ATT_Q, width), lambda i, g: (i, g)), pl.BlockSpec((ATT_Q, 128), lambda i, g: (i, 0))],
        out_shape=[jax.ShapeDtypeStruct((s, A_WIDTH), F32), jax.ShapeDtypeStruct((s, 128), F32)],
        compiler_params=_params("parallel", "arbitrary"),
    )(qkv, qkv, qkv, fr)


def _attn_bwd(qkv, do, o, lse, fr):
    s = qkv.shape[0]
    nk = s // ATT_K
    scale = HEAD_DIM ** -0.5
    heads = ATT_HEADS_BWD
    width = heads * HEAD_DIM
    groups = HEADS // heads

    def body(q_ref, k_ref, v_ref, do_ref, o_ref, lse_ref, fr_ref, dq_ref, dk_ref, dv_ref, dfc_ref, dfr_ref, dk_acc, dv_acc):
        grp = pl.program_id(0)
        lane = lax.broadcasted_iota(jnp.int32, (ATT_Q, 128), 1)
        low = lane < HEAD_DIM
        low_t = lax.broadcasted_iota(jnp.int32, (128, ATT_Q), 0) < HEAD_DIM
        dk_acc[...] = jnp.zeros_like(dk_acc)
        dv_acc[...] = jnp.zeros_like(dv_acc)
        dfr_ref[...] = jnp.zeros_like(dfr_ref)

        @pl.when(grp == 0)
        def _():
            dfc_ref[...] = jnp.zeros_like(dfc_ref)

        def outer(i, carry):
            q0 = pl.multiple_of(i * ATT_Q, ATT_Q)
            rows = pl.ds(q0, ATT_Q)
            lsev = lse_ref[rows, :]
            qts, dots, qs, dos, deltas, lses = [], [], [], [], [], []
            for pr in range(heads // 2):
                pcols = slice(128 * pr, 128 * (pr + 1))
                q2, do2 = q_ref[rows, pcols] * scale, do_ref[rows, pcols]
                prod = do2 * o_ref[rows, pcols]
                deltas += [jnp.sum(jnp.where(low, prod, 0.0), axis=-1, keepdims=True),
                           jnp.sum(jnp.where(low, 0.0, prod), axis=-1, keepdims=True)]
                dob2 = do2.astype(BF16)
                qts += _split_heads(q2.astype(F32).T.astype(BF16), low_t)
                dots += _split_heads(do2.T.astype(BF16), low_t)
                qs += _split_heads(q2, low)
                dos += _split_heads(dob2, low)
                lses += [_lane_pick(lsev, lane, heads * grp + 2 * pr), _lane_pick(lsev, lane, heads * grp + 2 * pr + 1)]

            def inner(j, carry, masked):
                k0 = pl.multiple_of(j * ATT_K, ATT_K)
                krows = pl.ds(k0, ATT_K)
                out, dkt, dvt = [], [], []
                for h in range(heads):
                    pcols = slice(128 * (h // 2), 128 * (h // 2 + 1))
                    dq, dfc = carry[h]
                    k2 = k_ref[krows, pcols]
                    p = jnp.exp(_att_logits(qs[h], k2, fr_ref[h, pl.ds(j, 1), :], q0, k0, masked) - lses[h])
                    dp = lax.dot_general(dos[h], v_ref[krows, pcols], _NT, preferred_element_type=F32)
                    ds = p * (dp - deltas[h])
                    dsb = ds.astype(BF16)
                    dkt.append(jnp.dot(qts[h], dsb, preferred_element_type=F32))
                    dvt.append(jnp.dot(dots[h], p.astype(BF16), preferred_element_type=F32))
                    dfr_ref[h, pl.ds(j, 1), :] -= jnp.sum(ds, axis=0, keepdims=True)
                    out.append((dq + jnp.dot(dsb, k2, preferred_element_type=F32), dfc + (ds[:, :128] + ds[:, 128:])))
                for pr in range(heads // 2):
                    prows = slice(128 * pr, 128 * (pr + 1))
                    dk_acc[j, prows, :] += dkt[2 * pr] + dkt[2 * pr + 1]
                    dv_acc[j, prows, :] += dvt[2 * pr] + dvt[2 * pr + 1]
                return tuple(out)

            one = (jnp.zeros((ATT_Q, 128), F32), jnp.zeros((ATT_Q, 128), F32))
            done = _causal_sweep(inner, i, (one,) * heads)
            dfc = jnp.zeros((ATT_Q, 128), F32)
            for pr in range(heads // 2):
                (dq0, dfc0), (dq1, dfc1) = done[2 * pr], done[2 * pr + 1]
                dq_ref[rows, 128 * pr:128 * (pr + 1)] = (jnp.where(low, dq0, dq1) * scale).astype(BF16)
                head = heads * grp + 2 * pr
                dfc = (dfc + _lane_put(lane, head, jnp.sum(dfc0, axis=-1, keepdims=True))
                       + _lane_put(lane, head + 1, jnp.sum(dfc1, axis=-1, keepdims=True)))
            dfc_ref[rows, :] += dfc
            return carry

        lax.fori_loop(0, s // ATT_Q, outer, 0)
        for j in range(nk):
            for pr in range(heads // 2):
                prows, pcols = slice(128 * pr, 128 * (pr + 1)), slice(128 * pr, 128 * (pr + 1))
                dk_ref[ATT_K * j:ATT_K * (j + 1), pcols] = dk_acc[j, prows, :].T.astype(BF16)
                dv_ref[ATT_K * j:ATT_K * (j + 1), pcols] = dv_acc[j, prows, :].T.astype(BF16)

    part = lambda first: pl.BlockSpec((s, width), lambda g, first=first: (0, first + g))
    whole = pl.BlockSpec((s, 128), lambda g: (0, 0))
    rowv = pl.BlockSpec((heads, nk, ATT_K), lambda g: (g, 0, 0))
    return pl.pallas_call(
        body, name="attn_bwd", grid=(groups,),
        in_specs=[part(0), part(groups), part(2 * groups), part(0), part(0), whole, rowv],
        out_specs=[part(0), part(0), part(0), whole, rowv],
        out_shape=[jax.ShapeDtypeStruct((s, A_WIDTH), BF16)] * 3 + [jax.ShapeDtypeStruct((s, 128), F32), jax.ShapeDtypeStruct((HEADS, nk, ATT_K), F32)],
        scratch_shapes=[pltpu.VMEM((nk, width, ATT_K), F32), pltpu.VMEM((nk, width, ATT_K), F32)],
        compiler_params=_params("arbitrary"),
    )(qkv, qkv, qkv, do, o, lse, fr)


def _ada_fwd(c_all, w_ada, b_loc):
    depth, _, n = w_ada.shape
    tn = 512

    def body(c_ref, w_ref, b_ref, o_ref, sc_ref):
        cv = c_ref[...]
        sc = cv * jax.nn.sigmoid(cv)
        sc_ref[...] = sc
        o_ref[0] = jnp.dot(sc.astype(BF16), w_ref[0].astype(BF16), preferred_element_type=F32) + b_ref[0]

    return pl.pallas_call(
        body, name="ada_fwd", grid=(depth, n // tn),
        in_specs=[pl.BlockSpec((N_DEV, D), lambda l, j: (0, 0)), pl.BlockSpec((1, D, tn), lambda l, j: (l, 0, j)),
                  pl.BlockSpec((1, 1, tn), lambda l, j: (l, 0, j))],
        out_specs=[pl.BlockSpec((1, N_DEV, tn), lambda l, j: (l, 0, j)), pl.BlockSpec((N_DEV, D), lambda l, j: (0, 0))],
        out_shape=[jax.ShapeDtypeStruct((depth, N_DEV, n), F32), jax.ShapeDtypeStruct((N_DEV, D), F32)],
        compiler_params=_params("arbitrary", "arbitrary"),
    )(c_all, w_ada, b_loc)


def _sum_devices(gathered):
    n = gathered.shape[1]
    tn = _pick(n, (1408, 1024, 640, 512, 128))

    def body(g_ref, o_ref):
        acc = g_ref[0:8, :]
        for dev in range(1, N_DEV):
            acc = acc + g_ref[8 * dev:8 * dev + 8, :]
        o_ref[...] = acc

    return pl.pallas_call(
        body, name="sum_devices", grid=(n // tn,),
        in_specs=[pl.BlockSpec((8 * N_DEV, tn), lambda j: (0, j))], out_specs=pl.BlockSpec((8, tn), lambda j: (0, j)),
        out_shape=jax.ShapeDtypeStruct((8, n), F32), compiler_params=_params("parallel"),
    )(gathered)


def _place():
    x, y, c = lax.axis_index("x"), lax.axis_index("y"), lax.axis_index("c")
    chips = [(1 - x, y), (x, 1 - y), (1 - x, 1 - y)]
    return x, y, c, chips


def _allgather8(block, name, after=()):
    m_per, n = block.shape

    def body(x_ref, *rest):
        out_ref, send_sems, recv_sems, local_sem = rest[len(after):]
        x, y, c, chips = _place()
        me, sibling = (x, y, c), (x, y, 1 - c)

        def rows(px, py, pc):
            return out_ref.at[pl.ds((4 * px + 2 * py + pc) * m_per, m_per), :]

        def copy(k, blk, to, src=None):
            return pltpu.make_async_remote_copy(
                src_ref=rows(*blk) if src is None else src, dst_ref=rows(*blk),
                send_sem=send_sems.at[k], recv_sem=recv_sems.at[k], device_id=to, device_id_type=MESH)

        mine = pltpu.make_async_copy(x_ref, rows(*me), local_sem)
        mine.start()
        first = [copy(0, me, sibling, src=x_ref)]
        first += [copy(1 + j, me, (*chip, c), src=x_ref) for j, chip in enumerate(chips)]
        for cp in first:
            cp.start()
        passed = [copy(4 + j, (*chip, c), sibling) for j, chip in enumerate(chips)]
        for j, chip in enumerate(chips):
            copy(1 + j, (*chip, c), me).wait_recv()
            passed[j].start()
        copy(0, sibling, me).wait_recv()
        for j, chip in enumerate(chips):
            copy(4 + j, (*chip, 1 - c), me).wait_recv()
        for cp in first + passed:
            cp.wait_send()
        mine.wait()

    return pl.pallas_call(
        body, name=name, out_shape=jax.ShapeDtypeStruct((N_DEV * m_per, n), block.dtype),
        in_specs=[pl.BlockSpec(memory_space=pltpu.VMEM)] + [pl.BlockSpec(memory_space=pl.ANY)] * len(after),
        out_specs=pl.BlockSpec(memory_space=pltpu.VMEM),
        scratch_shapes=[pltpu.SemaphoreType.DMA((7,)), pltpu.SemaphoreType.DMA((7,)), pltpu.SemaphoreType.DMA],
        compiler_params=pltpu.CompilerParams(vmem_limit_bytes=V7X_VMEM_LIMIT),
    )(block, *after)


_SEM = pl.BlockSpec(memory_space=pltpu.SEMAPHORE)
_DATAFLOW = pltpu.SideEffectType.DATAFLOW_SIDE_EFFECTING


def _plan_copies(plan, refs, send_sems, recv_sems):
    return [pltpu.make_async_remote_copy(src_ref=src, dst_ref=dst, send_sem=send_sems.at[i], recv_sem=recv_sems.at[i],
                                         device_id=to, device_id_type=MESH) for i, (src, dst, to) in enumerate(plan(refs))]


class _Token(NamedTuple):
    after: jax.Array
    tie: jax.Array


def _after_operand(after):
    return after.after if isinstance(after, _Token) else after


def _copies_start(bufs, plan, n_copies, after, name):
    nb = len(bufs)

    def body(*refs):
        for cp in _plan_copies(plan, refs[:nb], refs[nb + 1], refs[nb + 2]):
            cp.start()
        for token in refs[-2:]:
            token[...] = jnp.zeros_like(token)

    sem = pltpu.SemaphoreType.DMA((n_copies,))
    vmem = pl.BlockSpec(memory_space=pltpu.VMEM)
    outs = pl.pallas_call(
        body, name=name,
        out_shape=(sem, sem, *[pltpu.HBM(b.shape, b.dtype) for b in bufs], jax.ShapeDtypeStruct((8, 128), F32),
                   jax.ShapeDtypeStruct((1, 1), F32)),
        in_specs=[_HBM] * nb + [pl.BlockSpec(memory_space=pl.ANY)],
        out_specs=(_SEM, _SEM, *[_HBM] * nb, vmem, vmem),
        input_output_aliases={i: 2 + i for i in range(nb)},
        compiler_params=pltpu.CompilerParams(has_side_effects=_DATAFLOW),
    )(*[pltpu.with_memory_space_constraint(b, pltpu.HBM) for b in bufs], _after_operand(after))
    return outs[0], outs[1], list(outs[2:2 + nb]), _Token(outs[-2], outs[-1])


def _copies_wait(started, plan, after, name):
    send_sems, recv_sems, bufs, _ = started
    nb = len(bufs)

    def body(*refs):
        for cp in _plan_copies(plan, refs[:nb], refs[nb], refs[nb + 1]):
            cp.wait_send()
            cp.wait_recv()

    return list(pl.pallas_call(
        body, name=name, out_shape=tuple(pltpu.HBM(b.shape, b.dtype) for b in bufs),
        in_specs=[_HBM] * nb + [_SEM, _SEM, pl.BlockSpec(memory_space=pl.ANY)], out_specs=tuple([_HBM] * nb),
        input_output_aliases={i: i for i in range(nb)},
        compiler_params=pltpu.CompilerParams(has_side_effects=_DATAFLOW),
    )(*bufs, send_sems, recv_sems, _after_operand(after)))


def _half_rows(ref, axis, c):
    half = ref.shape[axis] // 2
    return pl.ds(c * half, half)


def _plan_gather_ici(refs):
    n = len(refs) // 2
    x, y, c, chips = _place()
    out = []
    for a in range(n):
        rows = _half_rows(refs[a], 0, c)
        out += [(refs[a].at[rows], refs[n + a].at[2 * x + y, rows], (*chip, c)) for chip in chips]
        out.append((refs[a], refs[n + a].at[2 * x + y], (x, y, 1 - c)))
    return out


def _plan_gather_d2d(refs):
    x, y, c, chips = _place()
    out = []
    for ref in refs:
        rows = _half_rows(ref, 1, c)
        for px, py in chips:
            landed = ref.at[2 * px + py, rows]
            out.append((landed, landed, (x, y, 1 - c)))
    return out


def _plan_rs_sibling(refs):
    n = len(refs) // 2
    x, y, c, _ = _place()
    return [(refs[a].at[pl.ds(0, refs[a].shape[0]), _half_rows(refs[a], 1, 1 - c)], refs[n + a], (x, y, 1 - c)) for a in range(n)]


def _plan_rs_chips(refs):
    n = len(refs) // 2
    x, y, c, chips = _place()
    return [(refs[a].at[2 * px + py], refs[n + a].at[k], (px, py, c)) for a in range(n) for k, (px, py) in enumerate(chips)]


def _plan_rs_share(refs):
    x, y, c, _ = _place()
    return [(ref.at[_half_rows(ref, 0, c)], ref.at[_half_rows(ref, 0, c)], (x, y, 1 - c)) for ref in refs]


def _chip_sum(g, other, sel, name, blocked=True):
    nblk, half, cdim = other.shape
    tr = _pick(half, (512, 256, 128, 64) if nblk > 1 else (128, 64))
    per = half // tr

    def body(sel_ref, g_ref, t_ref, wire_ref, own_ref):
        total = g_ref[0] + t_ref[0]
        wire_ref[0] = total.astype(BF16)
        if blocked:
            @pl.when(pl.program_id(1) == sel_ref[1])
            def _():
                own_ref[...] = total
        else:
            own_ref[0] = total

    blk = pl.BlockSpec((1, tr, cdim), lambda i, p, sel_ref: (p, i, 0))
    own_spec = pl.BlockSpec((tr, cdim), lambda i, p, sel_ref: (i, 0)) if blocked else blk
    own_shape = jax.ShapeDtypeStruct((half, cdim) if blocked else other.shape, F32)
    return pl.pallas_call(
        body, name=name,
        grid_spec=pltpu.PrefetchScalarGridSpec(
            num_scalar_prefetch=1, grid=(per, nblk),
            in_specs=[pl.BlockSpec((1, tr, cdim), lambda i, p, sel_ref: (p, sel_ref[0] * per + i, 0)), blk],
            out_specs=[blk, own_spec]),
        out_shape=[jax.ShapeDtypeStruct(other.shape, BF16), own_shape],
        compiler_params=_params("parallel", "arbitrary"),
    )(sel, g, other)


def _final_sum(own, recv, sel, name):
    half, cdim = own.shape
    tr = _pick(half, (512, 256, 128, 64))
    per = half // tr

    def body(sel_ref, own_ref, r0_ref, r1_ref, r2_ref, o_ref):
        o_ref[...] = ((own_ref[...] + r0_ref[0].astype(F32)) + r1_ref[0].astype(F32)) + r2_ref[0].astype(F32)

    part = lambda k: pl.BlockSpec((1, tr, cdim), lambda i, sel_ref, k=k: (k, i, 0))
    return pl.pallas_call(
        body, name=name,
        grid_spec=pltpu.PrefetchScalarGridSpec(
            num_scalar_prefetch=1, grid=(per,),
            in_specs=[pl.BlockSpec((tr, cdim), lambda i, sel_ref: (i, 0)), part(0), part(1), part(2)],
            out_specs=pl.BlockSpec((tr, cdim), lambda i, sel_ref: (sel_ref[0] * per + i, 0))),
        out_shape=jax.ShapeDtypeStruct((2 * half, cdim), F32), compiler_params=_params("parallel"),
    )(sel, own, recv, recv, recv)


def _row(v):
    return v.reshape(1, -1)


_BR_WIDTHS = (A_WIDTH, POOL_WIDTH, CONV_WIDTH)


def _tie(v, token):
    return v if token is None else v + token.tie


def _no_hook(point, after, ready=None):
    return None


def _layer_fwd(x, w, mod, hook=_no_hook):
    s = x.shape[0]
    mod3 = mod.reshape(6, 1, D)
    h = _modnorm_fwd(x, _row(w["g_mix_pre"]), (mod3, 0), (mod3, 1), "mix_pre_fwd")
    hook("pre", h)
    z = _mm(h, w["w_all"], name="mm_in")
    qkv = z[:, Z_QKV:Z_PC].astype(BF16)
    fl = z[:, Z_FL:Z_COLS]
    cum = _cumf_fwd(fl, w["b_f_pad"])
    fr = cum[:, :HEADS].T.reshape(HEADS, s // ATT_K, ATT_K)
    br_a, lse = _attn_fwd(qkv, fr)
    br_b, br_c = _poolconv_fwd(z, w["w_pool_bd"], _tie(_row(w["pool_scale"]), hook("attn", lse)), w["conv_w"])
    hook("pool", br_b)
    pa, pb, pc, merged = _stacked_proj((br_a, br_b, br_c), w["w_branch"], _BR_WIDTHS, False, "mm_br_merge", gates=z, out_dtype=BF16)
    y = _mm(merged, w["w_out"], name="mm_out")
    x1, h2 = _post_pre_fwd(x, y, _row(w["g_mix_post"]), (mod3, 2), _row(w["g_ff_pre"]), (mod3, 3), (mod3, 4), "mix_post_ff_pre_fwd")
    a, r = _mm(h2, w["w_ff1"], b_split=N_CHIPS, epilogue=_relu2_fwd, out_dtype=(BF16, BF16), name="mm_ff1")
    y2 = _mm(r, w["w_ff2"], name="mm_ff2")
    x2 = _post_fwd(x1, y2, _tie(_row(w["g_ff_post"]), hook("ff_post", y2)), (mod3, 5), "ff_post_fwd")
    hook("end", x2)
    saved = dict(x=x, h=h, z=z, qkv=qkv, fl=fl, fr=fr, lse=lse, br_a=br_a, br_b=br_b, br_c=br_c, pa=pa, pb=pb, pc=pc,
                 merged=merged, y=y, x1=x1, h2=h2, a=a, r=r, y2=y2)
    return x2, saved


def _layer_bwd(dx2, sv, w, mod, hook=_no_hook):
    s = dx2.shape[0]
    mod3 = mod.reshape(6, 1, D)
    dy2, sum_ff_post = _post_bwd(dx2, sv["y2"], _row(w["g_ff_post"]), (mod3, 5), "ff_post_bwd")
    (da,) = _mm(dy2, w["w_ff2"], tb=True, epilogue=_relu2_bwd, extras=(sv["a"],), out_dtype=(BF16,), name="mm_ff2_dx")
    d_w_ff2 = _mm(sv["r"], dy2, ta=True, name="mm_ff2_dw")
    dh2 = _mm(da, w["w_ff1"], tb=True, b_split=N_CHIPS, name="mm_ff1_dx")
    d_w_ff1 = _mm(sv["h2"], da, ta=True, out_split=N_CHIPS, name="mm_ff1_dw")
    g_ff_pre = _tie(_row(w["g_ff_pre"]), hook("ff_pre", dh2, dict(w_ff1=d_w_ff1, w_ff2=d_w_ff2)))
    dx1, dy, sum_mid = _pre_post_bwd(dh2, sv["x1"], dx2, g_ff_pre, (mod3, 4), sv["y"], _row(w["g_mix_post"]), (mod3, 2), "ff_pre_mix_post_bwd")
    sum_ff_pre, sum_mix_post = sum_mid, sum_mid[3:]
    dmerged = _mm(dy, w["w_out"], tb=True, name="mm_out_dx")
    d_w_out = _mm(sv["merged"], dy, ta=True, name="mm_out_dw")
    dz, dpa, dpb, dpc = _merge_bwd(dmerged, sv["z"], sv["pa"], sv["pb"], sv["pc"])
    dbr_a, dbr_b, dbr_c = _stacked_proj((dpa, dpb, dpc), w["w_branch"], _BR_WIDTHS, True, "mm_br_dx")
    d_w_branch = _stacked_dw((sv["br_a"], sv["br_b"], sv["br_c"]), (dpa, dpb, dpc), "mm_br_dw")

    dq, dk, dv, dfc, dfr = _attn_bwd(sv["qkv"], dbr_a, sv["br_a"], sv["lse"], sv["fr"])
    dcum = dfc + jnp.pad(dfr.reshape(HEADS, s).T, ((0, 0), (0, 128 - HEADS)))
    dfl, sum_bf = _cumf_bwd(dcum, sv["fl"], _tie(w["b_f_pad"], hook("cumf", dfc)))
    dpc_z, d_wbd, sum_ps, sum_cw = _poolconv_bwd(dbr_b, dbr_c, sv["z"], w["w_pool_bd"], _row(w["pool_scale"]), w["conv_w"])
    for at, part in ((Z_QKV, dq), (Z_QKV + A_WIDTH, dk), (Z_QKV + 2 * A_WIDTH, dv), (Z_PC, dpc_z), (Z_FL, dfl)):
        dz = lax.dynamic_update_slice(dz, part, (0, at))
    dh = _mm(dz, w["w_all"], tb=True, name="mm_in_dx")
    d_w_all = _mm(sv["h"], dz, ta=True, name="mm_in_dw")
    hook("mix_pre", dh)
    dx, sum_mix_pre = _modnorm_bwd(dh, sv["x"], dx1, _row(w["g_mix_pre"]), (mod3, 1), "mix_pre_bwd")

    dmod = jnp.stack([sum_mix_pre[0], sum_mix_pre[1], sum_mix_post[0], sum_ff_pre[0], sum_ff_pre[1], sum_ff_post[0]])
    d_w_in = d_w_all[None]
    d_w_pool = jnp.stack([d_wbd[64 * g:64 * g + 64, 64 * g:64 * g + 64] for g in range(4)])
    big = dict(w_in=d_w_in, w_branch=d_w_branch, w_out=d_w_out, w_ff1=d_w_ff1, w_ff2=d_w_ff2)
    small = dict(g_mix_pre=sum_mix_pre[2], g_mix_post=sum_mix_post[1], g_ff_pre=sum_ff_pre[2], g_ff_post=sum_ff_post[1],
                 b_f=sum_bf[0, :HEADS], w_pool=d_w_pool, pool_scale=sum_ps[0], conv_w=sum_cw[0:3])
    return dx, dmod, big, small


_QKV_END, _FL_END, _PC_END = 3 * A_WIDTH, 3 * A_WIDTH + HEADS, 3 * A_WIDTH + HEADS + POOL_WIDTH + 3 * CONV_WIDTH
_W_IN_GROUPS = ((_PC_END, IN_COLS, Z_GL), (0, _QKV_END, Z_QKV), (_FL_END, _PC_END, Z_PC), (_QKV_END, _FL_END, Z_FL))
_SHARD_COLS = IN_COLS // N_CHIPS


def _w_in_layout():
    out = []
    for p in range(N_CHIPS):
        pieces = []
        for lo, hi, at in _W_IN_GROUPS:
            a, b = max(lo, p * _SHARD_COLS), min(hi, (p + 1) * _SHARD_COLS)
            if a < b:
                pieces.append((at + a - lo, at + b - lo, a - p * _SHARD_COLS))
        pieces.sort()
        segs = []
        for z0, z1, _ in pieces:
            s, e = z0 // 128 * 128, -(-z1 // 128) * 128
            if segs and s <= segs[-1][1]:
                segs[-1] = (segs[-1][0], max(e, segs[-1][1]))
            else:
                segs.append((s, e))
        assert sum(e - s for s, e in segs) == Z_WINDOW
        out.append((pieces, segs))
    return out


Z_WINDOW = 1536


def _w_in_window(shard, p):
    pieces, segs = _w_in_layout()[p]
    cols = []
    for s, e in segs:
        at = s
        for z0, z1, src in pieces:
            if s <= z0 < e:
                if z0 > at:
                    cols.append(jnp.zeros((shard.shape[0], z0 - at), shard.dtype))
                cols.append(shard[:, src:src + z1 - z0])
                at = z1
        if e > at:
            cols.append(jnp.zeros((shard.shape[0], e - at), shard.dtype))
    return jnp.concatenate(cols, axis=1)


def _own_window(shard, chip):
    return lax.switch(chip, [lambda t, p=p: _w_in_window(t, p) for p in range(N_CHIPS)], shard)


def _w_all_from_windows(blocks):
    layout = _w_in_layout()
    bounds = sorted({edge for _, segs in layout for seg in segs for edge in seg})
    parts = []
    for lo, hi in zip(bounds[:-1], bounds[1:]):
        covering = []
        for p, (_, segs) in enumerate(layout):
            at = 0
            for s, e in segs:
                if s <= lo and hi <= e:
                    covering.append(blocks[p][:, at + lo - s:at + hi - s])
                at += e - s
        assert covering
        parts.append(covering[0] if len(covering) == 1 else covering[0] + covering[1])
    return jnp.concatenate(parts, axis=1)


def _w_in_shard(d_w_all, p):
    pieces = []
    for lo, hi, at in sorted(_W_IN_GROUPS):
        a, b = max(lo, p * _SHARD_COLS), min(hi, (p + 1) * _SHARD_COLS)
        if a < b:
            pieces.append(d_w_all[:, at + a - lo:at + b - lo])
    return jnp.concatenate(pieces, axis=1)


def _w_in_shards(d_w_all):
    return jnp.stack([_w_in_shard(d_w_all, p) for p in range(N_CHIPS)])


def _full_layer_weights(w_in_blocks, w_branch, w_out, w_ff1, w_ff2, g_mix_pre, g_mix_post, g_ff_pre, g_ff_post, b_f, w_pool, pool_scale, conv_w):
    w_all = None if w_in_blocks is None else _w_all_from_windows(w_in_blocks)
    wbd = (w_pool[:, :, None, :] * jnp.eye(4, dtype=F32)[:, None, :, None]).reshape(POOL_WIDTH, POOL_WIDTH)
    return dict(w_all=w_all, w_branch=w_branch, w_out=w_out, w_ff1=w_ff1, w_ff2=w_ff2, g_mix_pre=g_mix_pre, g_mix_post=g_mix_post,
                g_ff_pre=g_ff_pre, g_ff_post=g_ff_post, b_f_pad=jnp.pad(b_f, (0, 128 - HEADS)).reshape(1, 128), w_pool_bd=wbd,
                pool_scale=pool_scale, conv_w=conv_w)


class _NoComm:
    def layer_weights(self, l):
        raise NotImplementedError

    def fwd_hook(self, l):
        return _no_hook

    def bwd_hook(self, l):
        return _no_hook

    def grads_ready(self, l, big):
        return None


class _Layers(_NoComm):
    def __init__(self, layers):
        self.layers = layers

    def layer_weights(self, l):
        return self.layers[l]


def _local_step(x, target, mods, comm):
    saved, weights = [], []
    act = x
    for l in range(DEPTH):
        weights.append(comm.layer_weights(l))
        act, sv = _layer_fwd(act, weights[l], mods[l], comm.fwd_hook(l))
        saved.append(sv)
    dact, sq = _loss_head(act, target)
    loss = sq[0, 0] * (0.5 / D)
    dmods, bigs, smalls = [None] * DEPTH, [None] * DEPTH, [None] * DEPTH
    token = None
    for l in reversed(range(DEPTH)):
        dact, dmods[l], bigs[l], smalls[l] = _layer_bwd(dact, saved[l], weights[l], _tie(mods[l], token), comm.bwd_hook(l))
        token = comm.grads_ready(l, bigs[l])
    return loss, dact, jnp.stack(dmods), bigs, smalls


_BIG = ("w_in", "w_branch", "w_out", "w_ff1", "w_ff2")


class _GatherJob:
    def __init__(self, tag, shards, after):
        self.tag, self.n = tag, len(shards)
        lands = [lax.empty((N_CHIPS,) + s.shape, s.dtype) for s in shards]
        self.state = _copies_start(list(shards) + lands, _plan_gather_ici, 4 * self.n, after, "gather_ici_start_" + tag)
        self.token = self.state[3]

    def pass_on(self, after):
        bufs = _copies_wait(self.state, _plan_gather_ici, after, "gather_ici_wait_" + self.tag)
        self.state = _copies_start(bufs[self.n:], _plan_gather_d2d, 3 * self.n, bufs[0], "gather_d2d_start_" + self.tag)
        self.token = self.state[3]
        return self.token

    def done(self, after):
        return _copies_wait(self.state, _plan_gather_d2d, after, "gather_d2d_wait_" + self.tag)


class _ReduceJob:
    def __init__(self, tag, names, grads, sel, after):
        self.tag, self.names, self.n, self.sel = tag, names, len(names), sel
        lands = [lax.empty((g.shape[0], g.shape[1] // 2, g.shape[2]), F32) for g in grads]
        self.state = _copies_start(list(grads) + lands, _plan_rs_sibling, self.n, after, "rs_sibling_start_" + tag)
        self.token = self.state[3]

    def _chip_sum(self, name, g, other):
        if g.shape[0] == N_CHIPS:
            return _chip_sum(g, other, self.sel, "rs_chip_sum_" + name)
        wire, total = _chip_sum(g, other, self.sel, "rs_chip_sum_" + name, blocked=False)
        own = lax.switch(self.sel[1], [lambda t, p=p: _w_in_shard(t, p) for p in range(N_CHIPS)], total[0])
        return _w_in_shards(wire[0]), own

    def chip_sums(self, after):
        bufs = _copies_wait(self.state, _plan_rs_sibling, after, "rs_sibling_wait_" + self.tag)
        wires, self.owns = zip(*[self._chip_sum(name, bufs[i], bufs[self.n + i]) for i, name in enumerate(self.names)])
        lands = [lax.empty((3,) + w.shape[1:], BF16) for w in wires]
        self.state = _copies_start(list(wires) + lands, _plan_rs_chips, 3 * self.n, self.owns[0], "rs_chips_start_" + self.tag)
        self.token = self.state[3]
        return self.token

    def final_sums(self, after):
        bufs = _copies_wait(self.state, _plan_rs_chips, after, "rs_chips_wait_" + self.tag)
        sums = [_final_sum(self.owns[i], bufs[self.n + i], self.sel, "rs_final_" + name) for i, name in enumerate(self.names)]
        self.state = _copies_start(sums, _plan_rs_share, self.n, sums[0], "rs_share_start_" + self.tag)
        self.token = self.state[3]
        return self.token

    def done(self, after):
        return dict(zip(self.names, _copies_wait(self.state, _plan_rs_share, after, "rs_share_wait_" + self.tag)))


def _chip_blocks(g):
    return g if g.ndim == 3 else g.reshape(N_CHIPS, -1, g.shape[1])


class _StepComm(_NoComm):
    def __init__(self, big_weights, w_in0, sel, after):
        self.sel = sel
        self.small, self.grads, self.jobs = None, [dict() for _ in range(DEPTH)], {}
        self.jobs["in0"] = _GatherJob("in0", [w_in0], after)
        later = lax.optimization_barrier((tuple(big_weights), self.jobs["in0"].token.after))[0]
        self.jobs["rest0"] = _GatherJob("rest0", [w[0].astype(BF16) for w in later[1:]], self.jobs["in0"].token)
        layer1 = [w[1].astype(BF16) for w in later]
        self.jobs["all1"] = _GatherJob("all1", [_own_window(layer1[0], sel[1])] + layer1[1:], self.jobs["rest0"].token)

    def layer_weights(self, l):
        if l == 0:
            self.weights0 = _full_layer_weights(None, None, None, None, None, *self.small[0])
            return self.weights0
        g_in, g_br, g_out, g_f1, g_f2 = self.landed1
        return _full_layer_weights(g_in, g_br.reshape(D, D), g_out.reshape(D, D), g_f1, g_f2.reshape(D_FF, D), *self.small[1])

    def fwd_hook(self, l):
        if l != 0:
            return _no_hook

        def hook(point, after, ready=None):
            if point == "pre":
                job = self.jobs["in0"]
                started = after[:8, :128].astype(F32) + self.jobs["all1"].token.after
                self.weights0["w_all"] = _w_all_from_windows(job.done(job.pass_on(started))[0])
            if point == "attn":
                return self.jobs["rest0"].pass_on(after)
            if point == "ff_post":
                return self.jobs["all1"].pass_on(after)
            if point == "pool":
                g_br, g_out, g_f1, g_f2 = self.jobs["rest0"].done(after)
                self.weights0.update(w_branch=g_br.reshape(D, D), w_out=g_out.reshape(D, D), w_ff1=g_f1, w_ff2=g_f2.reshape(D_FF, D))
            if point == "end":
                self.landed1 = self.jobs["all1"].done(after)
            return None
        return hook

    def bwd_hook(self, l):
        if l != 0:
            return _no_hook

        def hook(point, after, ready=None):
            jobs = self.jobs
            if point == "ff_pre":
                token = jobs["rs1"].chip_sums(after)
                jobs["rs0_ff"] = _ReduceJob("0_ff", ("w_ff1", "w_ff2"), [_chip_blocks(ready[n]) for n in ("w_ff1", "w_ff2")], self.sel, token)
                return jobs["rs0_ff"].token
            if point == "cumf":
                return jobs["rs0_ff"].chip_sums(jobs["rs1"].final_sums(after))
            self.grads[1] = jobs["rs1"].done(after)
            return None
        return hook

    def grads_ready(self, l, big):
        if l == 1:
            self.jobs["rs1"] = _ReduceJob("1", _BIG, [_chip_blocks(big[n]) for n in _BIG], self.sel, self.sel)
            return self.jobs["rs1"].token
        names = ("w_in", "w_branch", "w_out")
        self.jobs["rs0_mix"] = _ReduceJob("0_mix", names, [_chip_blocks(big[n]) for n in names], self.sel, self.sel)
        return self.jobs["rs0_mix"].token

    def finish_sums(self, after):
        jobs = self.jobs
        token = jobs["rs0_mix"].chip_sums(after)
        return jobs["rs0_ff"].final_sums(token)

    def finish_ff(self, after):
        self.grads[0].update(self.jobs["rs0_ff"].done(after))

    def finish_mix(self, after):
        job = self.jobs["rs0_mix"]
        self.grads[0].update(job.done(job.final_sums(after)))


_SMALL = ("g_mix_pre", "g_mix_post", "g_ff_pre", "g_ff_post", "b_f", "w_pool", "pool_scale", "conv_w")


def _w_in_view(t):
    return t.reshape(DEPTH, D // 128, 128, _SHARD_COLS).transpose(3, 1, 0, 2).reshape(_SHARD_COLS * (D // 128) * DEPTH, 128)


def _w_in_unview(t):
    return t.reshape(_SHARD_COLS, D // 128, DEPTH, 128).transpose(2, 1, 3, 0).reshape(DEPTH, D, _SHARD_COLS)


def _pack(parts, rows=8):
    flat = jnp.concatenate([p.reshape(-1) for p in parts])
    width = -(-flat.shape[0] // (rows * 128)) * 128
    return jnp.pad(flat, (0, rows * width - flat.shape[0])).reshape(rows, width)


def _unpack(packed, like):
    flat = packed.reshape(-1)
    out, at = [], 0
    for ref in like:
        out.append(flat[at:at + ref.size].reshape(ref.shape))
        at += ref.size
    return out


def kernel(x, c, w_ada, b_ada, g_mix_pre, g_mix_post, g_ff_pre, g_ff_post, w_in, b_f, w_pool, pool_scale, conv_w, w_branch, w_out, w_ff1, w_ff2, loss_target, m_w_ada, m_b_ada, m_g_mix_pre, m_g_mix_post, m_g_ff_pre, m_g_ff_post, m_w_in, m_b_f, m_w_pool, m_pool_scale, m_conv_w, m_w_branch, m_w_out, m_w_ff1, m_w_ff2, v_w_ada, v_b_ada, v_g_mix_pre, v_g_mix_post, v_g_ff_pre, v_g_ff_post, v_w_in, v_b_f, v_w_pool, v_pool_scale, v_conv_w, v_w_branch, v_w_out, v_w_ff1, v_w_ff2):
    xi, yi, ci = lax.axis_index("x"), lax.axis_index("y"), lax.axis_index("c")
    chip = 2 * xi + yi
    dev = 2 * chip + ci
    n_ada = w_ada.shape[2]

    first = jnp.zeros((8, D + 384), F32).at[0, :D].set(c[0]).at[0, D:].set(conv_w.reshape(-1))
    w_in0 = _own_window(w_in[0].astype(BF16), chip)
    got = _allgather8(first, "gather_cond", after=(w_in0,)).reshape(N_DEV, 8, D + 384)[:, 0]
    c_all = got[:, :D]
    conv_full = got[0::2, D:].reshape(N_CHIPS, DEPTH, 3, CONV_WIDTH // N_CHIPS).transpose(1, 2, 0, 3).reshape(DEPTH, 3, CONV_WIDTH)

    b_loc = lax.dynamic_slice_in_dim(b_ada, chip * n_ada, n_ada, axis=1).reshape(DEPTH, 1, n_ada)
    mod_cols, silu_c = _ada_fwd(c_all, w_ada, b_loc)
    got = _allgather8(mod_cols.reshape(DEPTH * N_DEV, n_ada), "gather_mod").reshape(N_DEV, DEPTH, N_DEV, n_ada)[0::2]
    mod_all = got.transpose(1, 2, 0, 3).reshape(DEPTH, N_DEV, 6, D)
    mods = lax.dynamic_index_in_dim(mod_all, dev, axis=1, keepdims=False)

    comm = _StepComm((w_in, w_branch, w_out, w_ff1, w_ff2), w_in0, jnp.stack([ci, chip]).astype(jnp.int32), mods)
    comm.small = [(g_mix_pre[l], g_mix_post[l], g_ff_pre[l], g_ff_post[l], b_f[l], w_pool[l], pool_scale[l], conv_full[l]) for l in range(DEPTH)]
    loss_part, grad_x, dmods, bigs, smalls = _local_step(x[0], loss_target[0], mods, comm)

    small_parts = [smalls[l][name] for name in _SMALL for l in range(DEPTH)] + [loss_part.reshape(1)]
    packed = _tie(_pack([dmods] + small_parts), comm.jobs["rs0_mix"].token)
    gathered = _allgather8(packed, "gather_small")
    dmod_all = gathered.reshape(N_DEV, -1)[:, :dmods.size].reshape(N_DEV, DEPTH, 6 * D)
    summed = _unpack(_sum_devices(gathered), [dmods] + small_parts)
    grad_b_ada = summed[0].reshape(DEPTH, 6 * D)
    loss = summed[-1][0]
    small_grads = {name: jnp.stack(summed[1 + 2 * i:3 + 2 * i]) for i, name in enumerate(_SMALL)}
    small_grads["conv_w"] = lax.dynamic_slice_in_dim(small_grads["conv_w"], chip * (CONV_WIDTH // N_CHIPS), CONV_WIDTH // N_CHIPS, axis=2)

    dmod_loc = lax.dynamic_slice_in_dim(dmod_all.transpose(1, 0, 2), chip * n_ada, n_ada, axis=2)
    tail_token = comm.finish_sums(grad_b_ada)
    silu_pad = _tie(jnp.pad(silu_c, ((0, 128 - N_DEV), (0, 0))), tail_token)
    dmod_pad = jnp.pad(dmod_loc.transpose(1, 0, 2).reshape(N_DEV, DEPTH * n_ada), ((0, 128 - N_DEV), (0, 0)))
    grad_w_ada = _mm(silu_pad, dmod_pad, ta=True, out_split=DEPTH, name="mm_ada_dw")

    grads = dict(w_ada=grad_w_ada, b_ada=grad_b_ada, **small_grads)
    weights = dict(w_ada=w_ada, b_ada=b_ada, g_mix_pre=g_mix_pre, g_mix_post=g_mix_post, g_ff_pre=g_ff_pre, g_ff_post=g_ff_post, w_in=w_in,
                   b_f=b_f, w_pool=w_pool, pool_scale=pool_scale, conv_w=conv_w, w_branch=w_branch, w_out=w_out, w_ff1=w_ff1, w_ff2=w_ff2)
    m_in = dict(w_ada=m_w_ada, b_ada=m_b_ada, g_mix_pre=m_g_mix_pre, g_mix_post=m_g_mix_post, g_ff_pre=m_g_ff_pre, g_ff_post=m_g_ff_post,
                w_in=m_w_in, b_f=m_b_f, w_pool=m_w_pool, pool_scale=m_pool_scale, conv_w=m_conv_w, w_branch=m_w_branch, w_out=m_w_out,
                w_ff1=m_w_ff1, w_ff2=m_w_ff2)
    v_in = dict(w_ada=v_w_ada, b_ada=v_b_ada, g_mix_pre=v_g_mix_pre, g_mix_post=v_g_mix_post, g_ff_pre=v_g_ff_pre, g_ff_post=v_g_ff_post,
                w_in=v_w_in, b_f=v_b_f, w_pool=v_w_pool, pool_scale=v_pool_scale, conv_w=v_conv_w, w_branch=v_w_branch, w_out=v_w_out,
                w_ff1=v_w_ff1, w_ff2=v_w_ff2)
    order = ("w_ada", "b_ada", "g_mix_pre", "g_mix_post", "g_ff_pre", "g_ff_post", "w_in", "b_f", "w_pool", "pool_scale", "conv_w",
             "w_branch", "w_out", "w_ff1", "w_ff2")
    delta, new_m, new_v = {}, {}, {}
    tiny = ("b_ada",) + _SMALL
    tiny_g = [_tie(grads[tiny[0]], tail_token)] + [grads[name] for name in tiny[1:]]
    res = _adamw_many([weights[name] for name in tiny], tiny_g, [m_in[name] for name in tiny], [v_in[name] for name in tiny], "adamw_small")
    for out, vals in zip((delta, new_m, new_v), res):
        out.update(zip(tiny, vals))
    delta["w_ada"], new_m["w_ada"], new_v["w_ada"] = _adamw(w_ada, grad_w_ada, m_w_ada, v_w_ada, "adamw_w_ada")
    comm.finish_ff(delta["w_ada"][0, :8, :128] + delta["b_ada"][0, :128])
    for name in ("w_ff1", "w_ff2", "w_in", "w_branch", "w_out"):
        if name == "w_in":
            comm.finish_mix(delta["w_ff2"][0, :8, :128])
        g_layers = [comm.grads[l][name] for l in range(DEPTH)]
        if name == "w_in":
            g_view = lax.optimization_barrier(_w_in_view(jnp.stack(g_layers)))
            res = _adamw(_w_in_view(w_in), g_view, _w_in_view(m_w_in), _w_in_view(v_w_in), "adamw_w_in")
            grads[name], delta[name], new_m[name], new_v[name] = [_w_in_unview(t) for t in (g_view, *res)]
        else:
            delta[name], new_m[name], new_v[name], grads[name] = _adamw_layers(weights[name], g_layers, m_in[name], v_in[name], "adamw_" + name)

    return (loss, grad_x[None], *[grads[n] for n in order], *[delta[n] for n in order], *[new_m[n] for n in order],
            *[new_v[n] for n in order])
```
